```python
import jax, jax.numpy as jnp
from jax import lax
import numpy as np

D_MODEL = 1024
BATCH = 8
SEQ = 2048
DEPTH = 1

MLA_HEADS = 8
MLA_NOPE = 64
MLA_ROPE = 32
MLA_V = 64
MLA_Q_LORA = 512
MLA_KV_LORA = 256
DIL_HEADS = 8
DIL_HEAD_DIM = 64
DIL_PATTERNS = ((128, 1), (512, 4), (2048, 16))
DIL_WIDTH = DIL_HEADS * DIL_HEAD_DIM
MIX_WIDTH = MLA_HEADS * MLA_V + DIL_WIDTH
IN_COLS = MLA_Q_LORA + MLA_KV_LORA + MLA_ROPE + 3 * DIL_WIDTH
SPLITS = (MLA_Q_LORA,
          MLA_Q_LORA + MLA_KV_LORA,
          MLA_Q_LORA + MLA_KV_LORA + MLA_ROPE,
          MLA_Q_LORA + MLA_KV_LORA + MLA_ROPE + DIL_WIDTH,
          MLA_Q_LORA + MLA_KV_LORA + MLA_ROPE + 2 * DIL_WIDTH)
D_FF = 2816
CONV_WIDTH = 3
ROPE_THETA = 10000.0
EPS = 1e-6
Q_BLOCK = 128
NEG_INF = -1e30

kernel_name = "hybrid_mla_dilated_convffn_adaln"


def rms_norm(x, g):
    xf = x.astype(jnp.float32)
    y = xf * lax.rsqrt(jnp.mean(xf * xf, axis=-1, keepdims=True) + EPS)
    return (y * g.astype(jnp.float32)).astype(x.dtype)


def rope(x, positions):
    d = x.shape[-1]
    half = d // 2
    inv_freq = jnp.power(ROPE_THETA, -2.0 * jnp.arange(half, dtype=jnp.float32) / d)
    ang = positions.astype(jnp.float32)[:, :, None, None] * inv_freq
    cos, sin = jnp.cos(ang), jnp.sin(ang)
    xf = x.astype(jnp.float32)
    x1, x2 = xf[..., :half], xf[..., half:]
    return jnp.concatenate([x1 * cos - x2 * sin, x1 * sin + x2 * cos], axis=-1).astype(x.dtype)


def causal_dense_attention(q, k, v, scale):
    B, S, H, Dk = q.shape
    Dv = v.shape[-1]
    nb = S // Q_BLOCK
    qb = q.reshape(B, nb, Q_BLOCK, H, Dk).transpose(1, 0, 2, 3, 4)
    kpos = jnp.arange(S)

    def one_block(args):
        qi, i = args
        s = jnp.einsum('bqhd,bkhd->bhqk', qi, k).astype(jnp.float32) * scale
        qpos = i * Q_BLOCK + jnp.arange(Q_BLOCK)
        mask = kpos[None, :] <= qpos[:, None]
        s = jnp.where(mask, s, NEG_INF)
        p = jax.nn.softmax(s, axis=-1).astype(v.dtype)
        return jnp.einsum('bhqk,bkhd->bqhd', p, v)

    out = lax.map(one_block, (qb, jnp.arange(nb)))
    return out.transpose(1, 0, 2, 3, 4).reshape(B, S, H, Dv)


def banded_causal_attention(q, k, v, span):
    N, L, H, D = q.shape
    blk = span
    nb = -(-L // blk)
    Lp = nb * blk
    pad = ((0, 0), (0, Lp - L), (0, 0), (0, 0))
    q, k, v = jnp.pad(q, pad), jnp.pad(k, pad), jnp.pad(v, pad)
    qb = q.reshape(N, nb, blk, H, D)

    def two_blocks(t):
        tb = jnp.pad(t, ((0, 0), (blk, 0), (0, 0), (0, 0))).reshape(N, nb + 1, blk, H, D)
        return jnp.concatenate([tb[:, :-1], tb[:, 1:]], axis=2)

    kb, vb = two_blocks(k), two_blocks(v)
    s = jnp.einsum('nbqhd,nbkhd->nbhqk', qb, kb).astype(jnp.float32) * (D ** -0.5)
    blk_idx = jnp.arange(nb)[:, None, None]
    qry_pos = blk_idx * blk + jnp.arange(blk)[None, :, None]
    key_pos = (blk_idx - 1) * blk + jnp.arange(2 * blk)[None, None, :]
    dist = qry_pos - key_pos
    mask = (dist >= 0) & (dist <= span) & (key_pos >= 0)
    s = jnp.where(mask[None, :, None], s, NEG_INF)
    m = jnp.max(s, axis=-1, keepdims=True)
    e = jnp.exp(s - m)
    denom = jnp.sum(e, axis=-1, keepdims=True)
    p = (e / denom).astype(v.dtype)
    o = jnp.einsum('nbhqk,nbkhd->nbqhd', p, vb).reshape(N, Lp, H, D)[:, :L]
    lse = (m + jnp.log(denom))[..., 0]
    lse = lse.transpose(0, 1, 3, 2).reshape(N, Lp, H)[:, :L]
    return o, lse


def to_strided(t, dil):
    B, S, H, D = t.shape
    return t.reshape(B, S // dil, dil, H, D).transpose(0, 2, 1, 3, 4).reshape(B * dil, S // dil, H, D)


def dilated_attention(q, k, v):
    B, S, H, D = q.shape
    outs, lses = [], []
    for window, dil in DIL_PATTERNS:
        L = S // dil
        o, lse = banded_causal_attention(to_strided(q, dil), to_strided(k, dil),
                                         to_strided(v, dil), window // dil)
        outs.append(o.reshape(B, dil, L, H, D).transpose(0, 2, 1, 3, 4).reshape(B, S, H, D))
        lses.append(lse.reshape(B, dil, L, H).transpose(0, 2, 1, 3).reshape(B, S, H))
    w = jax.nn.softmax(jnp.stack(lses, axis=0), axis=0)
    out = jnp.sum(w[..., None] * jnp.stack(outs, axis=0).astype(jnp.float32), axis=0)
    return out.astype(q.dtype)


def causal_depthwise_conv(u, w, b):
    K = w.shape[0]
    S = u.shape[1]
    up = jnp.pad(u, ((0, 0), (K - 1, 0), (0, 0)))
    y = b
    for kk in range(K):
        y = y + up[:, kk:kk + S] * w[kk]
    return y


def _fwd_setup_inputs(seed: int = 0) -> dict:
    key = jax.random.key(seed)
    ks = jax.random.split(key, 24)
    nrm = jax.random.normal
    L = DEPTH

    def gain(k, n):
        return 1.0 + 0.05 * nrm(k, (L, n), jnp.float32)

    x = nrm(ks[0], (BATCH, SEQ, D_MODEL), jnp.float32)
    c = nrm(ks[1], (BATCH, D_MODEL), jnp.float32)
    positions = (jnp.arange(SEQ, dtype=jnp.int32)[None, :]
                 + jax.random.randint(ks[2], (BATCH, 1), 0, 4096, dtype=jnp.int32))
    return {
        "x": x,
        "c": c,
        "positions": positions,
        "w_ada": nrm(ks[3], (L, D_MODEL, 6 * D_MODEL), jnp.float32) * (0.5 * D_MODEL ** -0.5),
        "b_ada": 0.02 * nrm(ks[4], (L, 6 * D_MODEL), jnp.float32),
        "g_mix_norm": gain(ks[5], D_MODEL),
        "w_in": nrm(ks[6], (L, D_MODEL, IN_COLS), jnp.float32) * D_MODEL ** -0.5,
        "g_q_lat": gain(ks[7], MLA_Q_LORA),
        "w_q_b": nrm(ks[8], (L, MLA_Q_LORA, MLA_HEADS * (MLA_NOPE + MLA_ROPE)), jnp.float32) * MLA_Q_LORA ** -0.5,
        "g_kv_lat": gain(ks[9], MLA_KV_LORA),
        "w_kv_b": nrm(ks[10], (L, MLA_KV_LORA, MLA_HEADS * (MLA_NOPE + MLA_V)), jnp.float32) * MLA_KV_LORA ** -0.5,
        "g_mla_q_nope": gain(ks[11], MLA_NOPE),
        "g_mla_q_pe": gain(ks[12], MLA_ROPE),
        "g_mla_k_nope": gain(ks[13], MLA_NOPE),
        "g_mla_k_pe": gain(ks[14], MLA_ROPE),
        "g_dil_q": gain(ks[15], DIL_HEAD_DIM),
        "g_dil_k": gain(ks[16], DIL_HEAD_DIM),
        "w_o": nrm(ks[17], (L, MIX_WIDTH, D_MODEL), jnp.float32) * MIX_WIDTH ** -0.5,
        "g_ffn_norm": gain(ks[18], D_MODEL),
        "w_up": nrm(ks[19], (L, D_MODEL, 2 * D_FF), jnp.float32) * D_MODEL ** -0.5,
        "w_conv": nrm(ks[20], (L, CONV_WIDTH, 2 * D_FF), jnp.float32) * CONV_WIDTH ** -0.5,
        "b_conv": 0.02 * nrm(ks[21], (L, 2 * D_FF), jnp.float32),
        "w_down": nrm(ks[22], (L, D_FF, D_MODEL), jnp.float32) * D_FF ** -0.5,
    }


def _fwd_reference(x, c, positions, w_ada, b_ada, g_mix_norm, w_in, g_q_lat, w_q_b, g_kv_lat, w_kv_b,
              g_mla_q_nope, g_mla_q_pe, g_mla_k_nope, g_mla_k_pe, g_dil_q, g_dil_k, w_o,
              g_ffn_norm, w_up, w_conv, b_conv, w_down):
    B, S, _ = x.shape
    for l in range(DEPTH):
        mod = jax.nn.silu(c) @ w_ada[l] + b_ada[l]
        sh1, sc1, g1, sh2, sc2, g2 = jnp.split(mod, 6, axis=-1)

        h = rms_norm(x, g_mix_norm[l]) * (1.0 + sc1[:, None]) + sh1[:, None]
        proj = h @ w_in[l]
        q_lat, kv_lat, k_pe, qd, kd, vd = jnp.split(proj, SPLITS, axis=-1)

        q = (rms_norm(q_lat, g_q_lat[l]) @ w_q_b[l]).reshape(B, S, MLA_HEADS, MLA_NOPE + MLA_ROPE)
        q_nope = rms_norm(q[..., :MLA_NOPE], g_mla_q_nope[l])
        q_pe = rope(rms_norm(q[..., MLA_NOPE:], g_mla_q_pe[l]), positions)
        kv = (rms_norm(kv_lat, g_kv_lat[l]) @ w_kv_b[l]).reshape(B, S, MLA_HEADS, MLA_NOPE + MLA_V)
        k_nope = rms_norm(kv[..., :MLA_NOPE], g_mla_k_nope[l])
        v_mla = kv[..., MLA_NOPE:]
        k_pe = rope(rms_norm(k_pe, g_mla_k_pe[l])[:, :, None, :], positions)
        k_mla = jnp.concatenate([k_nope, jnp.broadcast_to(k_pe, (B, S, MLA_HEADS, MLA_ROPE))], axis=-1)
        q_mla = jnp.concatenate([q_nope, q_pe], axis=-1)
        o_mla = causal_dense_attention(q_mla, k_mla, v_mla, (MLA_NOPE + MLA_ROPE) ** -0.5)

        qd = rope(rms_norm(qd.reshape(B, S, DIL_HEADS, DIL_HEAD_DIM), g_dil_q[l]), positions)
        kd = rope(rms_norm(kd.reshape(B, S, DIL_HEADS, DIL_HEAD_DIM), g_dil_k[l]), positions)
        vd = vd.reshape(B, S, DIL_HEADS, DIL_HEAD_DIM)
        o_dil = dilated_attention(qd, kd, vd)

        mix = jnp.concatenate([o_mla.reshape(B, S, MLA_HEADS * MLA_V),
                               o_dil.reshape(B, S, DIL_WIDTH)], axis=-1) @ w_o[l]
        x = x + g1[:, None] * mix

        h2 = rms_norm(x, g_ffn_norm[l]) * (1.0 + sc2[:, None]) + sh2[:, None]
        u = causal_depthwise_conv(h2 @ w_up[l], w_conv[l], b_conv[l])
        gate, val = jnp.split(u, 2, axis=-1)
        x = x + g2[:, None] * ((jax.nn.silu(gate) * val) @ w_down[l])
    return x


import jax as _jax
import jax.numpy as _jnp

TWIN_FORMAT = 'train_step'
FWD_PARAMS = ['x', 'c', 'positions', 'w_ada', 'b_ada', 'g_mix_norm', 'w_in', 'g_q_lat', 'w_q_b', 'g_kv_lat', 'w_kv_b', 'g_mla_q_nope', 'g_mla_q_pe', 'g_mla_k_nope', 'g_mla_k_pe', 'g_dil_q', 'g_dil_k', 'w_o', 'g_ffn_norm', 'w_up', 'w_conv', 'b_conv', 'w_down']
TWIN_WEIGHTS = ['w_ada', 'b_ada', 'g_mix_norm', 'w_in', 'g_q_lat', 'w_q_b', 'g_kv_lat', 'w_kv_b', 'g_mla_q_nope', 'g_mla_q_pe', 'g_mla_k_nope', 'g_mla_k_pe', 'g_dil_q', 'g_dil_k', 'w_o', 'g_ffn_norm', 'w_up', 'w_conv', 'b_conv', 'w_down']
TWIN_DIFF_INPUT = 'x'
TWIN_INPUTS = ['x', 'c', 'positions', 'w_ada', 'b_ada', 'g_mix_norm', 'w_in', 'g_q_lat', 'w_q_b', 'g_kv_lat', 'w_kv_b', 'g_mla_q_nope', 'g_mla_q_pe', 'g_mla_k_nope', 'g_mla_k_pe', 'g_dil_q', 'g_dil_k', 'w_o', 'g_ffn_norm', 'w_up', 'w_conv', 'b_conv', 'w_down', 'loss_target', 'm_w_ada', 'm_b_ada', 'm_g_mix_norm', 'm_w_in', 'm_g_q_lat', 'm_w_q_b', 'm_g_kv_lat', 'm_w_kv_b', 'm_g_mla_q_nope', 'm_g_mla_q_pe', 'm_g_mla_k_nope', 'm_g_mla_k_pe', 'm_g_dil_q', 'm_g_dil_k', 'm_w_o', 'm_g_ffn_norm', 'm_w_up', 'm_w_conv', 'm_b_conv', 'm_w_down', 'v_w_ada', 'v_b_ada', 'v_g_mix_norm', 'v_w_in', 'v_g_q_lat', 'v_w_q_b', 'v_g_kv_lat', 'v_w_kv_b', 'v_g_mla_q_nope', 'v_g_mla_q_pe', 'v_g_mla_k_nope', 'v_g_mla_k_pe', 'v_g_dil_q', 'v_g_dil_k', 'v_w_o', 'v_g_ffn_norm', 'v_w_up', 'v_w_conv', 'v_b_conv', 'v_w_down']
TWIN_OUTPUTS = ['loss', 'grad_x', 'grad_w_ada', 'grad_b_ada', 'grad_g_mix_norm', 'grad_w_in', 'grad_g_q_lat', 'grad_w_q_b', 'grad_g_kv_lat', 'grad_w_kv_b', 'grad_g_mla_q_nope', 'grad_g_mla_q_pe', 'grad_g_mla_k_nope', 'grad_g_mla_k_pe', 'grad_g_dil_q', 'grad_g_dil_k', 'grad_w_o', 'grad_g_ffn_norm', 'grad_w_up', 'grad_w_conv', 'grad_b_conv', 'grad_w_down', 'delta_w_ada', 'delta_b_ada', 'delta_g_mix_norm', 'delta_w_in', 'delta_g_q_lat', 'delta_w_q_b', 'delta_g_kv_lat', 'delta_w_kv_b', 'delta_g_mla_q_nope', 'delta_g_mla_q_pe', 'delta_g_mla_k_nope', 'delta_g_mla_k_pe', 'delta_g_dil_q', 'delta_g_dil_k', 'delta_w_o', 'delta_g_ffn_norm', 'delta_w_up', 'delta_w_conv', 'delta_b_conv', 'delta_w_down', 'new_m_w_ada', 'new_m_b_ada', 'new_m_g_mix_norm', 'new_m_w_in', 'new_m_g_q_lat', 'new_m_w_q_b', 'new_m_g_kv_lat', 'new_m_w_kv_b', 'new_m_g_mla_q_nope', 'new_m_g_mla_q_pe', 'new_m_g_mla_k_nope', 'new_m_g_mla_k_pe', 'new_m_g_dil_q', 'new_m_g_dil_k', 'new_m_w_o', 'new_m_g_ffn_norm', 'new_m_w_up', 'new_m_w_conv', 'new_m_b_conv', 'new_m_w_down', 'new_v_w_ada', 'new_v_b_ada', 'new_v_g_mix_norm', 'new_v_w_in', 'new_v_g_q_lat', 'new_v_w_q_b', 'new_v_g_kv_lat', 'new_v_w_kv_b', 'new_v_g_mla_q_nope', 'new_v_g_mla_q_pe', 'new_v_g_mla_k_nope', 'new_v_g_mla_k_pe', 'new_v_g_dil_q', 'new_v_g_dil_k', 'new_v_w_o', 'new_v_g_ffn_norm', 'new_v_w_up', 'new_v_w_conv', 'new_v_b_conv', 'new_v_w_down']
TWIN_LEAF_KINDS = {'loss': 'loss', 'grad_x': 'grad_x', 'grad_w_ada': 'grad_w', 'grad_b_ada': 'grad_w', 'grad_g_mix_norm': 'grad_w', 'grad_w_in': 'grad_w', 'grad_g_q_lat': 'grad_w', 'grad_w_q_b': 'grad_w', 'grad_g_kv_lat': 'grad_w', 'grad_w_kv_b': 'grad_w', 'grad_g_mla_q_nope': 'grad_w', 'grad_g_mla_q_pe': 'grad_w', 'grad_g_mla_k_nope': 'grad_w', 'grad_g_mla_k_pe': 'grad_w', 'grad_g_dil_q': 'grad_w', 'grad_g_dil_k': 'grad_w', 'grad_w_o': 'grad_w', 'grad_g_ffn_norm': 'grad_w', 'grad_w_up': 'grad_w', 'grad_w_conv': 'grad_w', 'grad_b_conv': 'grad_w', 'grad_w_down': 'grad_w', 'delta_w_ada': 'delta_w', 'delta_b_ada': 'delta_w', 'delta_g_mix_norm': 'delta_w', 'delta_w_in': 'delta_w', 'delta_g_q_lat': 'delta_w', 'delta_w_q_b': 'delta_w', 'delta_g_kv_lat': 'delta_w', 'delta_w_kv_b': 'delta_w', 'delta_g_mla_q_nope': 'delta_w', 'delta_g_mla_q_pe': 'delta_w', 'delta_g_mla_k_nope': 'delta_w', 'delta_g_mla_k_pe': 'delta_w', 'delta_g_dil_q': 'delta_w', 'delta_g_dil_k': 'delta_w', 'delta_w_o': 'delta_w', 'delta_g_ffn_norm': 'delta_w', 'delta_w_up': 'delta_w', 'delta_w_conv': 'delta_w', 'delta_b_conv': 'delta_w', 'delta_w_down': 'delta_w', 'new_m_w_ada': 'new_m', 'new_m_b_ada': 'new_m', 'new_m_g_mix_norm': 'new_m', 'new_m_w_in': 'new_m', 'new_m_g_q_lat': 'new_m', 'new_m_w_q_b': 'new_m', 'new_m_g_kv_lat': 'new_m', 'new_m_w_kv_b': 'new_m', 'new_m_g_mla_q_nope': 'new_m', 'new_m_g_mla_q_pe': 'new_m', 'new_m_g_mla_k_nope': 'new_m', 'new_m_g_mla_k_pe': 'new_m', 'new_m_g_dil_q': 'new_m', 'new_m_g_dil_k': 'new_m', 'new_m_w_o': 'new_m', 'new_m_g_ffn_norm': 'new_m', 'new_m_w_up': 'new_m', 'new_m_w_conv': 'new_m', 'new_m_b_conv': 'new_m', 'new_m_w_down': 'new_m', 'new_v_w_ada': 'new_v', 'new_v_b_ada': 'new_v', 'new_v_g_mix_norm': 'new_v', 'new_v_w_in': 'new_v', 'new_v_g_q_lat': 'new_v', 'new_v_w_q_b': 'new_v', 'new_v_g_kv_lat': 'new_v', 'new_v_w_kv_b': 'new_v', 'new_v_g_mla_q_nope': 'new_v', 'new_v_g_mla_q_pe': 'new_v', 'new_v_g_mla_k_nope': 'new_v', 'new_v_g_mla_k_pe': 'new_v', 'new_v_g_dil_q': 'new_v', 'new_v_g_dil_k': 'new_v', 'new_v_w_o': 'new_v', 'new_v_g_ffn_norm': 'new_v', 'new_v_w_up': 'new_v', 'new_v_w_conv': 'new_v', 'new_v_b_conv': 'new_v', 'new_v_w_down': 'new_v'}


def _forward(args):
    return _fwd_reference(*[args[k] for k in FWD_PARAMS])


def _output_shape():
    out = _jax.eval_shape(lambda: _forward(_fwd_setup_inputs(0)))
    return out.shape, out.dtype

N_MICROBATCH = 1
ADAM_LR = 0.001
ADAM_B1 = 0.9
ADAM_B2 = 0.999
ADAM_EPS = 1e-08
ADAM_WD = 0.01
ADAM_STEP = 10
PER_EXAMPLE_BATCH_AXIS = {'x': 0, 'c': 0, 'positions': 0, 'loss_target': 0}
SHARED_INPUTS = []
_WEIGHT_DTYPES = {'w_ada': _jnp.float32, 'b_ada': _jnp.float32, 'g_mix_norm': _jnp.float32, 'w_in': _jnp.float32, 'g_q_lat': _jnp.float32, 'w_q_b': _jnp.float32, 'g_kv_lat': _jnp.float32, 'w_kv_b': _jnp.float32, 'g_mla_q_nope': _jnp.float32, 'g_mla_q_pe': _jnp.float32, 'g_mla_k_nope': _jnp.float32, 'g_mla_k_pe': _jnp.float32, 'g_dil_q': _jnp.float32, 'g_dil_k': _jnp.float32, 'w_o': _jnp.float32, 'g_ffn_norm': _jnp.float32, 'w_up': _jnp.float32, 'w_conv': _jnp.float32, 'b_conv': _jnp.float32, 'w_down': _jnp.float32}
MOMENT_SCALE = {'w_ada': 4.830523e-01, 'b_ada': 1.037903e+00, 'g_mix_norm': 2.874699e-02, 'w_in': 8.977231e-02, 'g_q_lat': 1.050115e-02, 'w_q_b': 8.729583e-03, 'g_kv_lat': 3.638748e-01, 'w_kv_b': 9.285677e-02, 'g_mla_q_nope': 5.496283e-02, 'g_mla_q_pe': 4.802929e-02, 'g_mla_k_nope': 5.371666e-02, 'g_mla_k_pe': 4.704255e-02, 'g_dil_q': 1.088611e-01, 'g_dil_k': 1.091401e-01, 'w_o': 1.231715e-01, 'g_ffn_norm': 1.788759e+00, 'w_up': 7.202317e-02, 'w_conv': 2.797798e-01, 'b_conv': 2.275088e-01, 'w_down': 5.787006e-02}


def _to_microbatches(a, axis):
    t = _jnp.moveaxis(a, axis, 0)
    t = t.reshape((N_MICROBATCH, t.shape[0] // N_MICROBATCH) + t.shape[1:])
    return _jnp.moveaxis(t, 1, axis + 1)


def setup_inputs(seed: int = 0) -> dict:
    inp = _fwd_setup_inputs(seed)
    key = _jax.random.fold_in(_jax.random.key(seed), 7919)
    shape, _ = _output_shape()
    out = dict(inp)
    out["loss_target"] = _jax.random.normal(_jax.random.fold_in(key, 0), shape, _jnp.float32)
    for i, name in enumerate(TWIN_WEIGHTS):
        w = inp[name].astype(_jnp.float32)
        if MOMENT_SCALE is None:
            s = _jnp.sqrt(_jnp.mean(_jnp.square(w)) + 1e-30)
        else:
            s = MOMENT_SCALE[name]
        km, kv = _jax.random.split(_jax.random.fold_in(key, i + 1))
        out[name] = w
        out["m_" + name] = s * _jax.random.normal(km, w.shape, _jnp.float32)
        out["v_" + name] = (s * s) * _jax.random.uniform(kv, w.shape, _jnp.float32, 0.5, 1.5)
    if N_MICROBATCH > 1:
        for name, axis in PER_EXAMPLE_BATCH_AXIS.items():
            out[name] = _to_microbatches(out[name], axis)
    return {'x': out['x'], 'c': out['c'], 'positions': out['positions'], 'w_ada': out['w_ada'], 'b_ada': out['b_ada'], 'g_mix_norm': out['g_mix_norm'], 'w_in': out['w_in'], 'g_q_lat': out['g_q_lat'], 'w_q_b': out['w_q_b'], 'g_kv_lat': out['g_kv_lat'], 'w_kv_b': out['w_kv_b'], 'g_mla_q_nope': out['g_mla_q_nope'], 'g_mla_q_pe': out['g_mla_q_pe'], 'g_mla_k_nope': out['g_mla_k_nope'], 'g_mla_k_pe': out['g_mla_k_pe'], 'g_dil_q': out['g_dil_q'], 'g_dil_k': out['g_dil_k'], 'w_o': out['w_o'], 'g_ffn_norm': out['g_ffn_norm'], 'w_up': out['w_up'], 'w_conv': out['w_conv'], 'b_conv': out['b_conv'], 'w_down': out['w_down'], 'loss_target': out['loss_target'], 'm_w_ada': out['m_w_ada'], 'm_b_ada': out['m_b_ada'], 'm_g_mix_norm': out['m_g_mix_norm'], 'm_w_in': out['m_w_in'], 'm_g_q_lat': out['m_g_q_lat'], 'm_w_q_b': out['m_w_q_b'], 'm_g_kv_lat': out['m_g_kv_lat'], 'm_w_kv_b': out['m_w_kv_b'], 'm_g_mla_q_nope': out['m_g_mla_q_nope'], 'm_g_mla_q_pe': out['m_g_mla_q_pe'], 'm_g_mla_k_nope': out['m_g_mla_k_nope'], 'm_g_mla_k_pe': out['m_g_mla_k_pe'], 'm_g_dil_q': out['m_g_dil_q'], 'm_g_dil_k': out['m_g_dil_k'], 'm_w_o': out['m_w_o'], 'm_g_ffn_norm': out['m_g_ffn_norm'], 'm_w_up': out['m_w_up'], 'm_w_conv': out['m_w_conv'], 'm_b_conv': out['m_b_conv'], 'm_w_down': out['m_w_down'], 'v_w_ada': out['v_w_ada'], 'v_b_ada': out['v_b_ada'], 'v_g_mix_norm': out['v_g_mix_norm'], 'v_w_in': out['v_w_in'], 'v_g_q_lat': out['v_g_q_lat'], 'v_w_q_b': out['v_w_q_b'], 'v_g_kv_lat': out['v_g_kv_lat'], 'v_w_kv_b': out['v_w_kv_b'], 'v_g_mla_q_nope': out['v_g_mla_q_nope'], 'v_g_mla_q_pe': out['v_g_mla_q_pe'], 'v_g_mla_k_nope': out['v_g_mla_k_nope'], 'v_g_mla_k_pe': out['v_g_mla_k_pe'], 'v_g_dil_q': out['v_g_dil_q'], 'v_g_dil_k': out['v_g_dil_k'], 'v_w_o': out['v_w_o'], 'v_g_ffn_norm': out['v_g_ffn_norm'], 'v_w_up': out['v_w_up'], 'v_w_conv': out['v_w_conv'], 'v_b_conv': out['v_b_conv'], 'v_w_down': out['v_w_down']}


def _loss(weights, diff, rest, loss_target):
    with _jax.named_scope("forward"):
        args = {**rest, TWIN_DIFF_INPUT: diff, **{k: w.astype(_WEIGHT_DTYPES[k]) for k, w in weights.items()}}
        y = _forward(args)
    with _jax.named_scope("loss_head"):
        err = _jnp.square(y.astype(_jnp.float32) - loss_target)
        return 0.5 * _jnp.sum(_jnp.mean(err, axis=-1)) if err.ndim else 0.5 * err


def _adamw(w, g, m, v):
    m = ADAM_B1 * m + (1.0 - ADAM_B1) * g
    v = ADAM_B2 * v + (1.0 - ADAM_B2) * _jnp.square(g)
    m_hat = m / (1.0 - ADAM_B1 ** ADAM_STEP)
    v_hat = v / (1.0 - ADAM_B2 ** ADAM_STEP)
    delta = -ADAM_LR * (m_hat / (_jnp.sqrt(v_hat) + ADAM_EPS) + ADAM_WD * w)
    return delta, m, v


def reference(x, c, positions, w_ada, b_ada, g_mix_norm, w_in, g_q_lat, w_q_b, g_kv_lat, w_kv_b, g_mla_q_nope, g_mla_q_pe, g_mla_k_nope, g_mla_k_pe, g_dil_q, g_dil_k, w_o, g_ffn_norm, w_up, w_conv, b_conv, w_down, loss_target, m_w_ada, m_b_ada, m_g_mix_norm, m_w_in, m_g_q_lat, m_w_q_b, m_g_kv_lat, m_w_kv_b, m_g_mla_q_nope, m_g_mla_q_pe, m_g_mla_k_nope, m_g_mla_k_pe, m_g_dil_q, m_g_dil_k, m_w_o, m_g_ffn_norm, m_w_up, m_w_conv, m_b_conv, m_w_down, v_w_ada, v_b_ada, v_g_mix_norm, v_w_in, v_g_q_lat, v_w_q_b, v_g_kv_lat, v_w_kv_b, v_g_mla_q_nope, v_g_mla_q_pe, v_g_mla_k_nope, v_g_mla_k_pe, v_g_dil_q, v_g_dil_k, v_w_o, v_g_ffn_norm, v_w_up, v_w_conv, v_b_conv, v_w_down):
    given = dict(x=x, c=c, positions=positions, w_ada=w_ada, b_ada=b_ada, g_mix_norm=g_mix_norm, w_in=w_in, g_q_lat=g_q_lat, w_q_b=w_q_b, g_kv_lat=g_kv_lat, w_kv_b=w_kv_b, g_mla_q_nope=g_mla_q_nope, g_mla_q_pe=g_mla_q_pe, g_mla_k_nope=g_mla_k_nope, g_mla_k_pe=g_mla_k_pe, g_dil_q=g_dil_q, g_dil_k=g_dil_k, w_o=w_o, g_ffn_norm=g_ffn_norm, w_up=w_up, w_conv=w_conv, b_conv=b_conv, w_down=w_down, loss_target=loss_target, m_w_ada=m_w_ada, m_b_ada=m_b_ada, m_g_mix_norm=m_g_mix_norm, m_w_in=m_w_in, m_g_q_lat=m_g_q_lat, m_w_q_b=m_w_q_b, m_g_kv_lat=m_g_kv_lat, m_w_kv_b=m_w_kv_b, m_g_mla_q_nope=m_g_mla_q_nope, m_g_mla_q_pe=m_g_mla_q_pe, m_g_mla_k_nope=m_g_mla_k_nope, m_g_mla_k_pe=m_g_mla_k_pe, m_g_dil_q=m_g_dil_q, m_g_dil_k=m_g_dil_k, m_w_o=m_w_o, m_g_ffn_norm=m_g_ffn_norm, m_w_up=m_w_up, m_w_conv=m_w_conv, m_b_conv=m_b_conv, m_w_down=m_w_down, v_w_ada=v_w_ada, v_b_ada=v_b_ada, v_g_mix_norm=v_g_mix_norm, v_w_in=v_w_in, v_g_q_lat=v_g_q_lat, v_w_q_b=v_w_q_b, v_g_kv_lat=v_g_kv_lat, v_w_kv_b=v_w_kv_b, v_g_mla_q_nope=v_g_mla_q_nope, v_g_mla_q_pe=v_g_mla_q_pe, v_g_mla_k_nope=v_g_mla_k_nope, v_g_mla_k_pe=v_g_mla_k_pe, v_g_dil_q=v_g_dil_q, v_g_dil_k=v_g_dil_k, v_w_o=v_w_o, v_g_ffn_norm=v_g_ffn_norm, v_w_up=v_w_up, v_w_conv=v_w_conv, v_b_conv=v_b_conv, v_w_down=v_w_down)
    weights = {n: given[n] for n in TWIN_WEIGHTS}
    shared = {n: given[n] for n in SHARED_INPUTS}
    per_example = {n: given[n] for n in ['x', 'c', 'positions']}
    grad_fn = _jax.value_and_grad(_loss, argnums=(0, 1))

    def one_microbatch(ex, loss_target):
        ex = dict(ex)
        diff = ex.pop(TWIN_DIFF_INPUT)
        return grad_fn(weights, diff, {**shared, **ex}, loss_target)

    if N_MICROBATCH == 1:
        loss, (grad_w, grad_x) = one_microbatch(per_example, given["loss_target"])
    else:
        def body(carry, xs):
            loss_sum, grad_sum = carry
            l_k, (gw_k, gx_k) = one_microbatch(xs[0], xs[1])
            with _jax.named_scope("update"):
                return (loss_sum + l_k, _jax.tree.map(_jnp.add, grad_sum, gw_k)), gx_k

        init = (_jnp.zeros((), _jnp.float32), _jax.tree.map(_jnp.zeros_like, weights))
        (loss, grad_w), grad_x = _jax.lax.scan(body, init, (per_example, given["loss_target"]))
    with _jax.named_scope("update"):
        delta_w, new_m, new_v = {}, {}, {}
        for n in TWIN_WEIGHTS:
            delta_w[n], new_m[n], new_v[n] = _adamw(weights[n], grad_w[n], given["m_" + n], given["v_" + n])
    return (loss, grad_x, *[grad_w[n] for n in TWIN_WEIGHTS], *[delta_w[n] for n in TWIN_WEIGHTS],
            *[new_m[n] for n in TWIN_WEIGHTS], *[new_v[n] for n in TWIN_WEIGHTS])
```

```python
import functools
import math

import jax
import jax.numpy as jnp
from jax import lax
from jax.experimental import pallas as pl
from jax.experimental.pallas import tpu as pltpu

F32 = jnp.float32
MXU_DTYPE = jnp.bfloat16

N_DEV = 8
D_MODEL = 1024
SEQ = 2048
HEADS = 8
NOPE = 64
ROPE = 32
V_DIM = 64
Q_LORA = 512
KV_LORA = 256
DIL_DIM = 64
DIL_WIDTH = HEADS * DIL_DIM
DILATIONS = (1, 4, 16)
SPAN = 128
IN_COLS = Q_LORA + KV_LORA + ROPE + 3 * DIL_WIDTH
D_FF = 2816
ROPE_THETA = 10000.0
EPS = 1e-6
NEG_INF = -1e30
ADAM_LR, ADAM_B1, ADAM_B2, ADAM_EPS, ADAM_WD, ADAM_STEP = 0.001, 0.9, 0.999, 1e-08, 0.01, 10
VMEM_LIMIT = 56 * 1024 * 1024
MESH_ID = pl.DeviceIdType.MESH


def _params(**kw):
    return pltpu.CompilerParams(vmem_limit_bytes=VMEM_LIMIT, **kw)


def rowwise(name, fn, rows, params, out_rows, out_accs=(), tm=512):
    R = rows[0].shape[0]
    tm = min(tm, R)
    steps = R // tm
    assert steps * tm == R
    in_specs = []
    for a in rows:
        ri, di = a.shape
        if ri == R:
            in_specs.append(pl.BlockSpec((tm, di), lambda i: (i, 0)))
        else:
            per = ri // tm
            assert per * tm == ri
            in_specs.append(pl.BlockSpec((tm, di), lambda i, per=per: (i % per, 0)))
    for p in params:
        in_specs.append(pl.BlockSpec(p.shape, lambda i: (0,) * p.ndim))
    out_shape = [jax.ShapeDtypeStruct((R, d), dt) for d, dt in out_rows]
    out_specs = [pl.BlockSpec((tm, d), lambda i: (i, 0)) for d, _ in out_rows]
    out_shape += [jax.ShapeDtypeStruct((1, n), F32) for n in out_accs]
    out_specs += [pl.BlockSpec((1, n), lambda i: (0, 0)) for n in out_accs]
    nr, npar, no, na = len(rows), len(params), len(out_rows), len(out_accs)

    def body(*refs):
        rvals = [r[...] for r in refs[:nr]]
        pvals = [r[...] for r in refs[nr:nr + npar]]
        outs, accs = fn(rvals, pvals)
        for ref, v in zip(refs[nr + npar:nr + npar + no], outs, strict=True):
            ref[...] = v.astype(ref.dtype)
        if na:
            acc_refs = refs[nr + npar + no:]
            i = pl.program_id(0)

            @pl.when(i == 0)
            def _():
                for ref, v in zip(acc_refs, accs, strict=True):
                    ref[...] = v

            @pl.when(i > 0)
            def _():
                for ref, v in zip(acc_refs, accs, strict=True):
                    ref[...] += v

    res = pl.pallas_call(body, name=name, grid=(steps,), in_specs=in_specs, out_specs=out_specs,
                         out_shape=out_shape, compiler_params=_params())(*rows, *params)
    return list(res)


_DIMS = {"nn": ((1,), (0,)), "nt": ((1,), (1,)), "tn": ((0,), (0,))}


def _dot(a, b, mode="nn"):
    return lax.dot_general(a.astype(MXU_DTYPE), b.astype(MXU_DTYPE), (_DIMS[mode], ((), ())),
                           preferred_element_type=F32)


def matmul(name, a, b, mode, tm=None, tn=None, tk=None, out_dtype=F32):
    if mode == "tn":
        K, M = a.shape
    else:
        M, K = a.shape
    N = b.shape[0] if mode == "nt" else b.shape[1]
    tm, tn, tk = tm or M, tn or N, tk or K
    nm, nn, nk = M // tm, N // tn, K // tk
    assert nm * tm == M and nn * tn == N and nk * tk == K
    a_spec = pl.BlockSpec((tk, tm), lambda i, j, k: (k, i)) if mode == "tn" else pl.BlockSpec((tm, tk), lambda i, j, k: (i, k))
    b_spec = pl.BlockSpec((tn, tk), lambda i, j, k: (j, k)) if mode == "nt" else pl.BlockSpec((tk, tn), lambda i, j, k: (k, j))

    def body(a_ref, b_ref, o_ref, *scratch):
        p = _dot(a_ref[...], b_ref[...], mode)
        if nk == 1:
            o_ref[...] = p.astype(o_ref.dtype)
        else:
            acc = scratch[0]
            k = pl.program_id(2)

            @pl.when(k == 0)
            def _():
                acc[...] = p

            @pl.when(k > 0)
            def _():
                acc[...] += p

            @pl.when(k == nk - 1)
            def _():
                o_ref[...] = acc[...].astype(o_ref.dtype)

    return pl.pallas_call(
        body, name=name, grid=(nm, nn, nk), in_specs=[a_spec, b_spec],
        out_specs=pl.BlockSpec((tm, tn), lambda i, j, k: (i, j)),
        out_shape=jax.ShapeDtypeStruct((M, N), out_dtype),
        scratch_shapes=[pltpu.VMEM((tm, tn), F32)] if nk > 1 else [],
        compiler_params=_params())(a, b)


def _rms(x, g):
    rstd = lax.rsqrt(jnp.mean(x * x, axis=-1, keepdims=True) + EPS)
    n = x * rstd
    return n * g, n, rstd


def _rms_bwd(dy, n, rstd, g):
    dg = jnp.sum(dy * n, axis=0, keepdims=True)
    dn = dy * g
    dx = rstd * (dn - n * jnp.mean(dn * n, axis=-1, keepdims=True))
    return dx, dg


def _rot(x):
    h = x.shape[-1] // 2
    return jnp.concatenate([-x[:, h:], x[:, :h]], axis=-1)


def _rot_t(z):
    h = z.shape[-1] // 2
    return jnp.concatenate([z[:, h:], -z[:, :h]], axis=-1)


def _rope(x, cos, sin):
    return x * cos + _rot(x) * sin


def _rope_bwd(dy, cos, sin):
    return dy * cos + _rot_t(dy * sin)


def _norm_rope_bwd(dy, x, g, cos, sin):
    _, n, rstd = _rms(x, g)
    return _rms_bwd(_rope_bwd(dy, cos, sin), n, rstd, g)


def _norm_bwd(dy, x, g):
    _, n, rstd = _rms(x, g)
    return _rms_bwd(dy, n, rstd, g)


def _colsum(v):
    return jnp.sum(v, axis=0, keepdims=True)


def _silu(x):
    return x * (1.0 / (1.0 + jnp.exp(-x)))


def attn_fwd(name, q, k, v, scale, tq=512):
    H, S, Dk = q.shape
    Dv = v.shape[-1]

    def body(q_ref, k_ref, v_ref, o_ref, lse_ref):
        for i in range(S // tq):
            kext = (i + 1) * tq
            blk = slice(i * tq, kext)
            s = _dot(q_ref[0, blk, :], k_ref[0, :kext, :], "nt") * scale
            row = lax.broadcasted_iota(jnp.int32, s.shape, 0) + i * tq
            col = lax.broadcasted_iota(jnp.int32, s.shape, 1)
            s = jnp.where(col <= row, s, NEG_INF)
            m = jnp.max(s, axis=-1, keepdims=True)
            e = jnp.exp(s - m)
            l = jnp.sum(e, axis=-1, keepdims=True)
            o_ref[0, blk, :] = _dot(e / l, v_ref[0, :kext, :])
            lse_ref[0, blk, :] = m + jnp.log(l)

    spec = lambda d: pl.BlockSpec((1, S, d), lambda h: (h, 0, 0))
    return pl.pallas_call(
        body, name=name, grid=(H,), in_specs=[spec(Dk), spec(Dk), spec(Dv)], out_specs=[spec(Dv), spec(1)],
        out_shape=[jax.ShapeDtypeStruct((H, S, Dv), F32), jax.ShapeDtypeStruct((H, S, 1), F32)],
        compiler_params=_params())(q, k, v)


def attn_bwd(name, q, k, v, o, do, lse, scale, tq=512):
    H, S, Dk = q.shape
    Dv = v.shape[-1]

    def body(q_ref, k_ref, v_ref, o_ref, do_ref, lse_ref, dq_ref, dk_ref, dv_ref, dkpe_ref, dk_acc, dv_acc):
        dk_acc[...] = jnp.zeros_like(dk_acc)
        dv_acc[...] = jnp.zeros_like(dv_acc)
        for i in range(S // tq):
            kext = (i + 1) * tq
            blk = slice(i * tq, kext)
            qi, kk, vv = q_ref[0, blk, :], k_ref[0, :kext, :], v_ref[0, :kext, :]
            doi = do_ref[0, blk, :]
            s = _dot(qi, kk, "nt") * scale
            row = lax.broadcasted_iota(jnp.int32, s.shape, 0) + i * tq
            col = lax.broadcasted_iota(jnp.int32, s.shape, 1)
            p = jnp.where(col <= row, jnp.exp(s - lse_ref[0, blk, :]), 0.0)
            dp = _dot(doi, vv, "nt")
            delta = jnp.sum(doi * o_ref[0, blk, :], axis=-1, keepdims=True)
            ds = p * (dp - delta) * scale
            dq_ref[0, blk, :] = _dot(ds, kk)
            dk_acc[:kext, :] += _dot(ds, qi, "tn")
            dv_acc[:kext, :] += _dot(p, doi, "tn")
        dk_ref[0] = dk_acc[...]
        dv_ref[0] = dv_acc[...]
        h = pl.program_id(0)

        @pl.when(h == 0)
        def _():
            dkpe_ref[...] = dk_acc[:, NOPE:NOPE + ROPE]

        @pl.when(h > 0)
        def _():
            dkpe_ref[...] += dk_acc[:, NOPE:NOPE + ROPE]

    spec = lambda d: pl.BlockSpec((1, S, d), lambda h: (h, 0, 0))
    return pl.pallas_call(
        body, name=name, grid=(H,),
        in_specs=[spec(Dk), spec(Dk), spec(Dv), spec(Dv), spec(Dv), spec(1)],
        out_specs=[spec(Dk), spec(Dk), spec(Dv), pl.BlockSpec((S, ROPE), lambda h: (0, 0))],
        out_shape=[jax.ShapeDtypeStruct((H, S, Dk), F32), jax.ShapeDtypeStruct((H, S, Dk), F32),
                   jax.ShapeDtypeStruct((H, S, Dv), F32), jax.ShapeDtypeStruct((S, ROPE), F32)],
        scratch_shapes=[pltpu.VMEM((S, Dk), F32), pltpu.VMEM((S, Dv), F32)],
        compiler_params=_params())(q, k, v, o, do, lse)


def _band_blocks(L, tq):
    out = []
    for i in range(L // tq):
        out.append((i * tq, (i + 1) * tq, max(0, i * tq - SPAN)))
    return out


def _band_mask(q0, q1, k0):
    shape = (q1 - q0, q1 - k0)
    dist = (lax.broadcasted_iota(jnp.int32, shape, 0) + q0) - (lax.broadcasted_iota(jnp.int32, shape, 1) + k0)
    return (dist >= 0) & (dist <= SPAN)


def band_fwd(name, q, k, v, gb):
    G, L, D = q.shape
    tq = min(L, 512)
    scale = D ** -0.5

    def body(q_ref, k_ref, v_ref, o_ref, lse_ref):
        for g in range(gb):
            for q0, q1, k0 in _band_blocks(L, tq):
                s = _dot(q_ref[g, q0:q1, :], k_ref[g, k0:q1, :], "nt") * scale
                s = jnp.where(_band_mask(q0, q1, k0), s, NEG_INF)
                m = jnp.max(s, axis=-1, keepdims=True)
                e = jnp.exp(s - m)
                l = jnp.sum(e, axis=-1, keepdims=True)
                o_ref[g, q0:q1, :] = _dot(e / l, v_ref[g, k0:q1, :])
                lse_ref[g, q0:q1, :] = m + jnp.log(l)

    spec = lambda d: pl.BlockSpec((gb, L, d), lambda i: (i, 0, 0))
    return pl.pallas_call(
        body, name=name, grid=(G // gb,), in_specs=[spec(D)] * 3, out_specs=[spec(D), spec(1)],
        out_shape=[jax.ShapeDtypeStruct((G, L, D), F32), jax.ShapeDtypeStruct((G, L, 1), F32)],
        compiler_params=_params())(q, k, v)


def band_bwd(name, q, k, v, o, lse, do, dlse, gb):
    G, L, D = q.shape
    tq = min(L, 512)
    scale = D ** -0.5

    def body(q_ref, k_ref, v_ref, o_ref, lse_ref, do_ref, dlse_ref, dq_ref, dk_ref, dv_ref):
        dk_ref[...] = jnp.zeros_like(dk_ref)
        dv_ref[...] = jnp.zeros_like(dv_ref)
        for g in range(gb):
            for q0, q1, k0 in _band_blocks(L, tq):
                qi, kk, vv = q_ref[g, q0:q1, :], k_ref[g, k0:q1, :], v_ref[g, k0:q1, :]
                doi = do_ref[g, q0:q1, :]
                s = _dot(qi, kk, "nt") * scale
                p = jnp.where(_band_mask(q0, q1, k0), jnp.exp(s - lse_ref[g, q0:q1, :]), 0.0)
                dp = _dot(doi, vv, "nt")
                delta = jnp.sum(doi * o_ref[g, q0:q1, :], axis=-1, keepdims=True)
                ds = p * (dp - delta + dlse_ref[g, q0:q1, :]) * scale
                dq_ref[g, q0:q1, :] = _dot(ds, kk)
                dk_ref[g, k0:q1, :] += _dot(ds, qi, "tn")
                dv_ref[g, k0:q1, :] += _dot(p, doi, "tn")

    spec = lambda d: pl.BlockSpec((gb, L, d), lambda i: (i, 0, 0))
    return pl.pallas_call(
        body, name=name, grid=(G // gb,), in_specs=[spec(D)] * 4 + [spec(1), spec(D), spec(1)], out_specs=[spec(D)] * 3,
        out_shape=[jax.ShapeDtypeStruct((G, L, D), F32)] * 3, compiler_params=_params())(q, k, v, o, lse, do, dlse)


def _shift_down(u, n):
    t = lax.broadcasted_iota(jnp.int32, u.shape, 0)
    return jnp.where(t >= n, pltpu.roll(u, n, axis=0), 0.0)


def _shift_up(u, n):
    rows = u.shape[0]
    t = lax.broadcasted_iota(jnp.int32, u.shape, 0)
    return jnp.where(t < rows - n, pltpu.roll(u, rows - n, axis=0), 0.0)


def _conv(u, w, b):
    return w[2:3, :] * u + w[1:2, :] * _shift_down(u, 1) + w[0:1, :] * _shift_down(u, 2) + b


CONV_TC = 256
CONV_NB = D_FF // CONV_TC


def conv_glu_fwd(name, up, w_conv, b_conv):
    S = up.shape[0]

    def body(ug_ref, uv_ref, wg_ref, wv_ref, bg_ref, bv_ref, act_ref):
        gate = _conv(ug_ref[...], wg_ref[...], bg_ref[...])
        val = _conv(uv_ref[...], wv_ref[...], bv_ref[...])
        act_ref[...] = (_silu(gate) * val).astype(act_ref.dtype)

    col = lambda r, off: pl.BlockSpec((r, CONV_TC), lambda j: (0, j + off))
    return pl.pallas_call(
        body, name=name, grid=(CONV_NB,),
        in_specs=[col(S, 0), col(S, CONV_NB), col(3, 0), col(3, CONV_NB), col(1, 0), col(1, CONV_NB)],
        out_specs=col(S, 0), out_shape=jax.ShapeDtypeStruct((S, D_FF), MXU_DTYPE),
        compiler_params=_params())(up, up, w_conv, w_conv, b_conv, b_conv)


def conv_glu_bwd(name, up, w_conv, b_conv, dact):
    S = up.shape[0]

    def body(uo_ref, up_ref, wo_ref, wp_ref, bo_ref, bp_ref, da_ref, dup_ref, dw_ref, db_ref):
        j = pl.program_id(0)
        uo = uo_ref[...]
        own = _conv(uo, wo_ref[...], bo_ref[...])
        partner = _conv(up_ref[...], wp_ref[...], bp_ref[...])
        da = da_ref[...]
        sig = 1.0 / (1.0 + jnp.exp(-own))
        d_gate = da * partner * (sig * (1.0 + own * (1.0 - sig)))
        d_val = da * _silu(partner)
        du = jnp.where(j < CONV_NB, d_gate, d_val)
        w = wo_ref[...]
        dup_ref[...] = (w[2:3, :] * du + w[1:2, :] * _shift_up(du, 1) + w[0:1, :] * _shift_up(du, 2)).astype(dup_ref.dtype)
        dw_ref[...] = jnp.concatenate([_colsum(du * _shift_down(uo, 2)), _colsum(du * _shift_down(uo, 1)), _colsum(du * uo)], axis=0)
        db_ref[...] = _colsum(du)

    own = lambda r: pl.BlockSpec((r, CONV_TC), lambda j: (0, j))
    other = lambda r: pl.BlockSpec((r, CONV_TC), lambda j: (0, (j + CONV_NB) % (2 * CONV_NB)))
    return pl.pallas_call(
        body, name=name, grid=(2 * CONV_NB,),
        in_specs=[own(S), other(S), own(3), other(3), own(1), other(1), pl.BlockSpec((S, CONV_TC), lambda j: (0, j % CONV_NB))],
        out_specs=[own(S), own(3), own(1)],
        out_shape=[jax.ShapeDtypeStruct((S, 2 * D_FF), MXU_DTYPE), jax.ShapeDtypeStruct((3, 2 * D_FF), F32),
                   jax.ShapeDtypeStruct((1, 2 * D_FF), F32)],
        compiler_params=_params())(up, up, w_conv, w_conv, b_conv, b_conv, dact)


def adamw(name, w, gstack, m, v, tr=None):
    R, C = w.shape
    P = gstack.shape[0]
    tr = tr or R
    assert R % tr == 0
    c1 = 1.0 - ADAM_B1 ** ADAM_STEP
    c2 = 1.0 - ADAM_B2 ** ADAM_STEP

    def body(w_ref, g_ref, m_ref, v_ref, go_ref, d_ref, mo_ref, vo_ref):
        g = g_ref[0]
        for p in range(1, P):
            g = g + g_ref[p]
        m2 = ADAM_B1 * m_ref[...] + (1.0 - ADAM_B1) * g
        v2 = ADAM_B2 * v_ref[...] + (1.0 - ADAM_B2) * (g * g)
        go_ref[...] = g
        mo_ref[...] = m2
        vo_ref[...] = v2
        d_ref[...] = -ADAM_LR * ((m2 / c1) / (jnp.sqrt(v2 / c2) + ADAM_EPS) + ADAM_WD * w_ref[...])

    blk = pl.BlockSpec((tr, C), lambda i: (i, 0))
    return pl.pallas_call(
        body, name=name, grid=(R // tr,),
        in_specs=[blk, pl.BlockSpec((P, tr, C), lambda i: (0, i, 0)), blk, blk], out_specs=[blk] * 4,
        out_shape=[jax.ShapeDtypeStruct((R, C), F32)] * 4, compiler_params=_params())(w, gstack, m, v)


def _place():
    return lax.axis_index("x"), lax.axis_index("y"), lax.axis_index("c")


def all_gather(name, arrs):
    n = len(arrs)

    def body(*refs):
        ins, outs = refs[:n], refs[n:2 * n]
        send_sems, recv_sems, local_sems = refs[2 * n:]
        x, y, c = _place()
        me, sibling = (x, y, c), (x, y, 1 - c)
        chips = [(1 - x, y), (x, 1 - y), (1 - x, 1 - y)]
        sends = []
        for t in range(n):
            out = outs[t]

            def slot(px, py, pc, out=out):
                return out.at[4 * px + 2 * py + pc]

            def copy(k, block, to, src=None, t=t, slot=slot):
                return pltpu.make_async_remote_copy(
                    src_ref=slot(*block) if src is None else src, dst_ref=slot(*block),
                    send_sem=send_sems.at[7 * t + k], recv_sem=recv_sems.at[7 * t + k],
                    device_id=to, device_id_type=MESH_ID)

            mine = pltpu.make_async_copy(ins[t], slot(*me), local_sems.at[t])
            mine.start()
            first = [copy(0, me, sibling, src=ins[t])]
            first += [copy(1 + j, me, (*chip, c), src=ins[t]) for j, chip in enumerate(chips)]
            for cp in first:
                cp.start()
            sends.append((mine, first, copy))
        for t in range(n):
            mine, first, copy = sends[t]
            passed = [copy(4 + j, (*chip, c), sibling) for j, chip in enumerate(chips)]
            for j, chip in enumerate(chips):
                copy(1 + j, (*chip, c), me).wait_recv()
                passed[j].start()
            copy(0, sibling, me).wait_recv()
            for j, chip in enumerate(chips):
                copy(4 + j, (*chip, 1 - c), me).wait_recv()
            for cp in first + passed:
                cp.wait_send()
            mine.wait()

    any_spec = pl.BlockSpec(memory_space=pl.ANY)
    res = pl.pallas_call(
        body, name=name, in_specs=[any_spec] * n, out_specs=[any_spec] * n,
        out_shape=[jax.ShapeDtypeStruct((N_DEV,) + a.shape, a.dtype) for a in arrs],
        scratch_shapes=[pltpu.SemaphoreType.DMA((7 * n,)), pltpu.SemaphoreType.DMA((7 * n,)), pltpu.SemaphoreType.DMA((n,))],
        compiler_params=pltpu.CompilerParams(has_side_effects=True))(*arrs)
    return list(res)


def all_to_all(name, arrs):
    n = len(arrs)

    def body(*refs):
        ins, outs = refs[:n], refs[n:2 * n]
        send_sems, recv_sems, local_sems = refs[2 * n:]
        x, y, c = _place()
        me = 4 * x + 2 * y + c
        copies = []
        for t in range(n):
            mine = pltpu.make_async_copy(ins[t].at[me], outs[t].at[me], local_sems.at[t])
            mine.start()
            copies.append(mine)
        remote = []
        for t in range(n):
            for k in range(1, N_DEV):
                px, py, pc = x ^ (k >> 2), y ^ ((k >> 1) & 1), c ^ (k & 1)
                cp = pltpu.make_async_remote_copy(
                    src_ref=ins[t].at[4 * px + 2 * py + pc], dst_ref=outs[t].at[me],
                    send_sem=send_sems.at[7 * t + k - 1], recv_sem=recv_sems.at[7 * t + k - 1],
                    device_id=(px, py, pc), device_id_type=MESH_ID)
                cp.start()
                remote.append((cp, t, 4 * px + 2 * py + pc, k))
        for cp, t, peer, k in remote:
            pltpu.make_async_remote_copy(
                src_ref=ins[t].at[peer], dst_ref=outs[t].at[peer],
                send_sem=send_sems.at[7 * t + k - 1], recv_sem=recv_sems.at[7 * t + k - 1],
                device_id=(x, y, c), device_id_type=MESH_ID).wait_recv()
        for cp, t, peer, k in remote:
            cp.wait_send()
        for mine in copies:
            mine.wait()

    any_spec = pl.BlockSpec(memory_space=pl.ANY)
    res = pl.pallas_call(
        body, name=name, in_specs=[any_spec] * n, out_specs=[any_spec] * n,
        out_shape=[jax.ShapeDtypeStruct(a.shape, a.dtype) for a in arrs],
        scratch_shapes=[pltpu.SemaphoreType.DMA((7 * n,)), pltpu.SemaphoreType.DMA((7 * n,)), pltpu.SemaphoreType.DMA((n,))],
        compiler_params=pltpu.CompilerParams(has_side_effects=True))(*arrs)
    return list(res)


def _to_heads(t, h):
    s = t.shape[0]
    return t.reshape(s, h, -1).transpose(1, 0, 2)


def _from_heads(t):
    h, s, d = t.shape
    return t.transpose(1, 0, 2).reshape(s, h * d)


def _to_strided(t, dil):
    h, s, d = t.shape
    if dil == 1:
        return t
    return t.reshape(h, s // dil, dil, d).transpose(0, 2, 1, 3).reshape(h * dil, s // dil, d)


def _from_strided(t, dil):
    g, l, d = t.shape
    if dil == 1:
        return t
    h = g // dil
    return t.reshape(h, dil, l, d).transpose(0, 2, 1, 3).reshape(h, l * dil, d)


def _gather_cols(stack):
    p, k, n = stack.shape
    return stack.transpose(1, 0, 2).reshape(k, p * n)


def _scatter_cols(full):
    k, n = full.shape
    return full.reshape(k, N_DEV, n // N_DEV).transpose(1, 0, 2)


def _gather_rows(stack):
    p, r, n = stack.shape
    return stack.reshape(p * r, n)


def _scatter_rows(full):
    r, n = full.shape
    return full.reshape(N_DEV, r // N_DEV, n)


SMALL = (("loss", 1), ("b_ada", 6 * D_MODEL), ("g_mix_norm", D_MODEL), ("g_q_lat", Q_LORA), ("g_kv_lat", KV_LORA),
         ("g_mla_q_nope", NOPE), ("g_mla_q_pe", ROPE), ("g_mla_k_nope", NOPE), ("g_mla_k_pe", ROPE),
         ("g_dil_q", DIL_DIM), ("g_dil_k", DIL_DIM), ("g_ffn_norm", D_MODEL), ("b_conv", 2 * D_FF))
SMALL_ROWS = 16
SMALL_COLS = 1024


def _pack_small(values):
    parts = [values[name].reshape(-1).astype(F32) if name in values else jnp.zeros((n,), F32) for name, n in SMALL]
    flat = jnp.concatenate(parts)
    flat = jnp.pad(flat, (0, SMALL_ROWS * SMALL_COLS - flat.shape[0]))
    return flat.reshape(SMALL_ROWS, SMALL_COLS)


def _unpack_small(packed):
    flat = packed.reshape(-1)
    out, off = {}, 0
    for name, n in SMALL:
        out[name] = flat[off:off + n].reshape(1, n)
        off += n
    return out


def _local_step(x, pos, mod, target, w):
    S = SEQ
    H = HEADS
    sh1, sc1, g1, sh2, sc2, g2 = [mod[:, i * D_MODEL:(i + 1) * D_MODEL] for i in range(6)]

    def inv_freq(d):
        f = jnp.power(ROPE_THETA, -2.0 * jnp.arange(d // 2, dtype=F32) / d)
        return jnp.concatenate([f, f]).reshape(1, d)

    def tables_fn(rows, params):
        (p,), (f32_, f64_) = rows, params
        return [jnp.cos(p * f32_), jnp.sin(p * f32_), jnp.cos(p * f64_), jnp.sin(p * f64_)], []

    cos32, sin32, cos64, sin64 = rowwise("rope_tables", tables_fn, [pos], [inv_freq(ROPE), inv_freq(DIL_DIM)],
                                         [(ROPE, F32), (ROPE, F32), (DIL_DIM, F32), (DIL_DIM, F32)])

    def ln1_fn(rows, params):
        (xv,), (g, sc, sh) = rows, params
        y, _, _ = _rms(xv, g)
        return [y * (1.0 + sc) + sh], []

    (h,) = rowwise("ln1_fwd", ln1_fn, [x], [w["g_mix_norm"], sc1, sh1], [(D_MODEL, MXU_DTYPE)])
    proj = matmul("proj_fwd", h, w["w_in"], "nn", tm=512)
    q_lat = proj[:, :Q_LORA]
    kv_lat = proj[:, Q_LORA:Q_LORA + KV_LORA]
    k_pe = proj[:, Q_LORA + KV_LORA:Q_LORA + KV_LORA + ROPE]
    o0 = Q_LORA + KV_LORA + ROPE
    qd = _to_heads(proj[:, o0:o0 + DIL_WIDTH], H).reshape(H * S, DIL_DIM)
    kd = _to_heads(proj[:, o0 + DIL_WIDTH:o0 + 2 * DIL_WIDTH], H).reshape(H * S, DIL_DIM)
    vd = _to_heads(proj[:, o0 + 2 * DIL_WIDTH:], H).astype(MXU_DTYPE)

    def lat_fn(rows, params):
        (ql, kvl, kp, c32, s32), (gq, gkv, gkp) = rows, params
        return [_rms(ql, gq)[0], _rms(kvl, gkv)[0], _rope(_rms(kp, gkp)[0], c32, s32)], []

    qln, kvn, kper = rowwise("latent_fwd", lat_fn, [q_lat, kv_lat, k_pe, cos32, sin32],
                             [w["g_q_lat"], w["g_kv_lat"], w["g_mla_k_pe"]],
                             [(Q_LORA, MXU_DTYPE), (KV_LORA, MXU_DTYPE), (ROPE, MXU_DTYPE)])
    q = matmul("q_fwd", qln, w["w_q_b"], "nn", tm=1024)
    kv = matmul("kv_fwd", kvn, w["w_kv_b"], "nn", tm=1024)
    qh = _to_heads(q, H)
    kvh = _to_heads(kv, H)
    q_nope = qh[..., :NOPE].reshape(H * S, NOPE)
    q_pe = qh[..., NOPE:].reshape(H * S, ROPE)
    k_nope = kvh[..., :NOPE].reshape(H * S, NOPE)
    v_mla = kvh[..., NOPE:].astype(MXU_DTYPE)

    def heads_fn(rows, params):
        (qn, qp, kn, qdv, kdv, c32, s32, c64, s64), (gqn, gqp, gkn, gdq, gdk) = rows, params
        return [_rms(qn, gqn)[0], _rope(_rms(qp, gqp)[0], c32, s32), _rms(kn, gkn)[0],
                _rope(_rms(qdv, gdq)[0], c64, s64), _rope(_rms(kdv, gdk)[0], c64, s64)], []

    head_gains = [w["g_mla_q_nope"], w["g_mla_q_pe"], w["g_mla_k_nope"], w["g_dil_q"], w["g_dil_k"]]
    q_nope_n, q_pe_r, k_nope_n, qd_r, kd_r = rowwise(
        "heads_fwd", heads_fn, [q_nope, q_pe, k_nope, qd, kd, cos32, sin32, cos64, sin64], head_gains,
        [(NOPE, MXU_DTYPE), (ROPE, MXU_DTYPE), (NOPE, MXU_DTYPE), (DIL_DIM, MXU_DTYPE), (DIL_DIM, MXU_DTYPE)], tm=S)
    q_mla = jnp.concatenate([q_nope_n.reshape(H, S, NOPE), q_pe_r.reshape(H, S, ROPE)], axis=-1)
    k_mla = jnp.concatenate([k_nope_n.reshape(H, S, NOPE), jnp.broadcast_to(kper[None], (H, S, ROPE))], axis=-1)
    mla_scale = (NOPE + ROPE) ** -0.5
    o_mla, lse_mla = attn_fwd("mla_fwd", q_mla, k_mla, v_mla, mla_scale)

    qd_r = qd_r.reshape(H, S, DIL_DIM)
    kd_r = kd_r.reshape(H, S, DIL_DIM)
    band = []
    for dil in DILATIONS:
        L = S // dil
        gb = {2048: 1, 512: 4, 128: 16}[L]
        qs, ks, vs = _to_strided(qd_r, dil), _to_strided(kd_r, dil), _to_strided(vd, dil)
        o_s, lse_s = band_fwd(f"band{dil}_fwd", qs, ks, vs, gb)
        band.append((qs, ks, vs, o_s, lse_s, gb))
    o_nat = [_from_strided(b[3], dil).reshape(H * S, DIL_DIM) for b, dil in zip(band, DILATIONS)]
    lse_nat = [_from_strided(b[4], dil).reshape(H * S, 1) for b, dil in zip(band, DILATIONS)]

    def mix_weights(ls):
        m = jnp.maximum(jnp.maximum(ls[0], ls[1]), ls[2])
        e = [jnp.exp(l - m) for l in ls]
        den = e[0] + e[1] + e[2]
        return [ei / den for ei in e]

    def combine_fn(rows, params):
        os_, ls = rows[:3], rows[3:]
        wt = mix_weights(ls)
        return [wt[0] * os_[0] + wt[1] * os_[1] + wt[2] * os_[2]], []

    (o_dil,) = rowwise("dil_combine_fwd", combine_fn, o_nat + lse_nat, [], [(DIL_DIM, F32)], tm=S)
    o_cat = jnp.concatenate([_from_heads(o_mla), _from_heads(o_dil.reshape(H, S, DIL_DIM))], axis=-1).astype(MXU_DTYPE)
    mix = matmul("mix_fwd", o_cat, w["w_o"], "nn", tm=512)

    def mid_fn(rows, params):
        (xv, mx), (gate1, g, sc, sh) = rows, params
        x1 = xv + gate1 * mx
        y, _, _ = _rms(x1, g)
        return [x1, y * (1.0 + sc) + sh], []

    x1, h2 = rowwise("mid_fwd", mid_fn, [x, mix], [g1, w["g_ffn_norm"], sc2, sh2], [(D_MODEL, F32), (D_MODEL, MXU_DTYPE)])
    up = matmul("up_fwd", h2, w["w_up"], "nn", tm=512, tn=1408)
    act = conv_glu_fwd("conv_fwd", up, w["w_conv"], w["b_conv"])
    dn = matmul("down_fwd", act, w["w_down"], "nn", tm=512)

    def final_fn(rows, params):
        (x1v, dnv, tgt), (gate2,) = rows, params
        r = x1v + gate2 * dnv - tgt
        dy = r * (1.0 / D_MODEL)
        loss = jnp.sum(_colsum(r * r), axis=-1, keepdims=True) * (0.5 / D_MODEL)
        return [dy, gate2 * dy], [loss, _colsum(dy * dnv)]

    dy, d_dn, loss, dg2 = rowwise("loss_head", final_fn, [x1, dn, target], [g2], [(D_MODEL, F32), (D_MODEL, MXU_DTYPE)],
                                  [1, D_MODEL])
    gw = {}
    gw["w_down"] = matmul("down_wgrad", act, d_dn, "tn", tm=1408)
    dact = matmul("down_dgrad", d_dn, w["w_down"], "nt", tm=512)
    dup, gw["w_conv"], gs_b_conv = conv_glu_bwd("conv_bwd", up, w["w_conv"], w["b_conv"], dact)
    gw["w_up"] = matmul("up_wgrad", h2, dup, "tn", tn=1408)
    dh2 = matmul("up_dgrad", dup, w["w_up"], "nt", tm=512, tk=2816)

    def mid_bwd_fn(rows, params):
        (dh2v, dyv, x1v, mx), (gate1, g, sc) = rows, params
        yn, n, rstd = _rms(x1v, g)
        dx_n, dg = _rms_bwd(dh2v * (1.0 + sc), n, rstd, g)
        dx1 = dyv + dx_n
        return [dx1, gate1 * dx1], [dg, _colsum(dh2v * yn), _colsum(dh2v), _colsum(dx1 * mx)]

    dx1, dmix, dg_ffn, dsc2, dsh2, dg1 = rowwise(
        "mid_bwd", mid_bwd_fn, [dh2, dy, x1, mix], [g1, w["g_ffn_norm"], sc2], [(D_MODEL, F32), (D_MODEL, MXU_DTYPE)],
        [D_MODEL] * 4)

    gw["w_o"] = matmul("mix_wgrad", o_cat, dmix, "tn")
    do_cat = matmul("mix_dgrad", dmix, w["w_o"], "nt", tm=512)
    do_mla = _to_heads(do_cat[:, :HEADS * V_DIM], H)
    do_dil = _to_heads(do_cat[:, HEADS * V_DIM:], H).reshape(H * S, DIL_DIM)

    def combine_bwd_fn(rows, params):
        os_, ls, dout = rows[:3], rows[3:6], rows[6]
        wt = mix_weights(ls)
        dw = [jnp.sum(dout * o, axis=-1, keepdims=True) for o in os_]
        mean_dw = wt[0] * dw[0] + wt[1] * dw[1] + wt[2] * dw[2]
        return [wt[i] * dout for i in range(3)] + [wt[i] * (dw[i] - mean_dw) for i in range(3)], []

    cb = rowwise("dil_combine_bwd", combine_bwd_fn, o_nat + lse_nat + [do_dil], [],
                 [(DIL_DIM, F32)] * 3 + [(1, F32)] * 3, tm=S)
    dqd_parts, dkd_parts, dvd_parts = [], [], []
    for i, dil in enumerate(DILATIONS):
        qs, ks, vs, o_s, lse_s, gb = band[i]
        do_s = _to_strided(cb[i].reshape(H, S, DIL_DIM), dil)
        dlse_s = _to_strided(cb[3 + i].reshape(H, S, 1), dil)
        dq_s, dk_s, dv_s = band_bwd(f"band{dil}_bwd", qs, ks, vs, o_s, lse_s, do_s, dlse_s, gb)
        dqd_parts.append(_from_strided(dq_s, dil).reshape(H * S, DIL_DIM))
        dkd_parts.append(_from_strided(dk_s, dil).reshape(H * S, DIL_DIM))
        dvd_parts.append(_from_strided(dv_s, dil).reshape(H * S, DIL_DIM))

    dq_mla, dk_mla, dv_mla, dkper = attn_bwd("mla_bwd", q_mla, k_mla, v_mla, o_mla, do_mla, lse_mla, mla_scale)
    dq_nope_n = dq_mla[..., :NOPE].reshape(H * S, NOPE)
    dq_pe_r = dq_mla[..., NOPE:].reshape(H * S, ROPE)
    dk_nope_n = dk_mla[..., :NOPE].reshape(H * S, NOPE)

    def heads_bwd_fn(rows, params):
        dqn, dqp, dkn = rows[0:3]
        dqd_ = rows[3] + rows[4] + rows[5]
        dkd_ = rows[6] + rows[7] + rows[8]
        dvd_ = rows[9] + rows[10] + rows[11]
        qn, qp, kn, qdv, kdv, c32, s32, c64, s64 = rows[12:]
        gqn, gqp, gkn, gdq, gdk = params
        r1 = _norm_bwd(dqn, qn, gqn)
        r2 = _norm_rope_bwd(dqp, qp, gqp, c32, s32)
        r3 = _norm_bwd(dkn, kn, gkn)
        r4 = _norm_rope_bwd(dqd_, qdv, gdq, c64, s64)
        r5 = _norm_rope_bwd(dkd_, kdv, gdk, c64, s64)
        rs = [r1, r2, r3, r4, r5]
        return [r[0] for r in rs] + [dvd_], [r[1] for r in rs]

    hb = rowwise("heads_bwd", heads_bwd_fn,
                 [dq_nope_n, dq_pe_r, dk_nope_n] + dqd_parts + dkd_parts + dvd_parts
                 + [q_nope, q_pe, k_nope, qd, kd, cos32, sin32, cos64, sin64],
                 head_gains, [(NOPE, F32), (ROPE, F32), (NOPE, F32), (DIL_DIM, F32), (DIL_DIM, F32), (DIL_DIM, MXU_DTYPE)],
                 [NOPE, ROPE, NOPE, DIL_DIM, DIL_DIM], tm=S)
    dq_nope, dq_pe, dk_nope, dqd, dkd, dvd = hb[:6]
    dg_q_nope, dg_q_pe, dg_k_nope, dg_dil_q, dg_dil_k = hb[6:]
    dq = _from_heads(jnp.concatenate([dq_nope.reshape(H, S, NOPE), dq_pe.reshape(H, S, ROPE)], axis=-1)).astype(MXU_DTYPE)
    dkv = _from_heads(jnp.concatenate([dk_nope.reshape(H, S, NOPE), dv_mla], axis=-1)).astype(MXU_DTYPE)
    gw["w_q_b"] = matmul("q_wgrad", qln, dq, "tn")
    gw["w_kv_b"] = matmul("kv_wgrad", kvn, dkv, "tn")
    dqln = matmul("q_dgrad", dq, w["w_q_b"], "nt", tm=1024)
    dkvn = matmul("kv_dgrad", dkv, w["w_kv_b"], "nt", tm=1024)

    def lat_bwd_fn(rows, params):
        (dql, dkvl, dkp, ql, kvl, kp, c32, s32), (gq, gkv, gkp) = rows, params
        r1 = _norm_bwd(dql, ql, gq)
        r2 = _norm_bwd(dkvl, kvl, gkv)
        r3 = _norm_rope_bwd(dkp, kp, gkp, c32, s32)
        return [r1[0], r2[0], r3[0]], [r1[1], r2[1], r3[1]]

    dq_lat, dkv_lat, dk_pe, dg_q_lat, dg_kv_lat, dg_k_pe = rowwise(
        "latent_bwd", lat_bwd_fn, [dqln, dkvn, dkper, q_lat, kv_lat, k_pe, cos32, sin32],
        [w["g_q_lat"], w["g_kv_lat"], w["g_mla_k_pe"]],
        [(Q_LORA, MXU_DTYPE), (KV_LORA, MXU_DTYPE), (ROPE, MXU_DTYPE)], [Q_LORA, KV_LORA, ROPE])
    dproj = jnp.concatenate([dq_lat, dkv_lat, dk_pe,
                             _from_heads(dqd.reshape(H, S, DIL_DIM)).astype(MXU_DTYPE),
                             _from_heads(dkd.reshape(H, S, DIL_DIM)).astype(MXU_DTYPE),
                             _from_heads(dvd.reshape(H, S, DIL_DIM))], axis=-1)
    gw["w_in"] = matmul("proj_wgrad", h, dproj, "tn", tm=256)
    dh = matmul("proj_dgrad", dproj, w["w_in"], "nt", tm=512)

    def ln1_bwd_fn(rows, params):
        (dhv, dres, xv), (g, sc) = rows, params
        yn, n, rstd = _rms(xv, g)
        dx_n, dg = _rms_bwd(dhv * (1.0 + sc), n, rstd, g)
        return [dres + dx_n], [dg, _colsum(dhv * yn), _colsum(dhv)]

    grad_x, dg_mix, dsc1, dsh1 = rowwise("ln1_bwd", ln1_bwd_fn, [dh, dx1, x], [w["g_mix_norm"], sc1], [(D_MODEL, F32)],
                                         [D_MODEL] * 3)
    dmod = jnp.concatenate([dsh1, dsc1, dg1, dsh2, dsc2, dg2], axis=-1)
    small = {"loss": loss, "b_ada": dmod, "g_mix_norm": dg_mix, "g_q_lat": dg_q_lat, "g_kv_lat": dg_kv_lat,
             "g_mla_q_nope": dg_q_nope, "g_mla_q_pe": dg_q_pe, "g_mla_k_nope": dg_k_nope, "g_mla_k_pe": dg_k_pe,
             "g_dil_q": dg_dil_q, "g_dil_k": dg_dil_k, "g_ffn_norm": dg_ffn, "b_conv": gs_b_conv}
    return grad_x, gw, small


COL_SHARDED = ("w_in", "w_q_b", "w_kv_b", "w_up", "w_conv")
ROW_SHARDED = ("w_o", "w_down")
ADAM_TILE = {"w_ada": 256, "w_in": 256, "w_up": 256, "w_down": 176}
OUT_WEIGHTS = ("w_ada", "b_ada", "g_mix_norm", "w_in", "g_q_lat", "w_q_b", "g_kv_lat", "w_kv_b", "g_mla_q_nope", "g_mla_q_pe",
               "g_mla_k_nope", "g_mla_k_pe", "g_dil_q", "g_dil_k", "w_o", "g_ffn_norm", "w_up", "w_conv", "b_conv", "w_down")


def kernel(x, c, positions, w_ada, b_ada, g_mix_norm, w_in, g_q_lat, w_q_b, g_kv_lat, w_kv_b, g_mla_q_nope, g_mla_q_pe, g_mla_k_nope, g_mla_k_pe, g_dil_q, g_dil_k, w_o, g_ffn_norm, w_up, w_conv, b_conv, w_down, loss_target, m_w_ada, m_b_ada, m_g_mix_norm, m_w_in, m_g_q_lat, m_w_q_b, m_g_kv_lat, m_w_kv_b, m_g_mla_q_nope, m_g_mla_q_pe, m_g_mla_k_nope, m_g_mla_k_pe, m_g_dil_q, m_g_dil_k, m_w_o, m_g_ffn_norm, m_w_up, m_w_conv, m_b_conv, m_w_down, v_w_ada, v_b_ada, v_g_mix_norm, v_w_in, v_g_q_lat, v_w_q_b, v_g_kv_lat, v_w_kv_b, v_g_mla_q_nope, v_g_mla_q_pe, v_g_mla_k_nope, v_g_mla_k_pe, v_g_dil_q, v_g_dil_k, v_w_o, v_g_ffn_norm, v_w_up, v_w_conv, v_b_conv, v_w_down):
    args = dict(locals())
    xi, yi, ci = _place()
    me = 4 * xi + 2 * yi + ci
    shard = {n: args[n][0] for n in COL_SHARDED + ROW_SHARDED + ("w_ada",)}
    small_w = {n: args[n] for n, _ in SMALL if n != "loss"}

    big = [n for n in COL_SHARDED + ROW_SHARDED if n != "w_conv"]
    gathered = all_gather("gather_weights", [c, shard["w_conv"]] + [shard[n].astype(MXU_DTYPE) for n in big])
    c_all = gathered[0].reshape(N_DEV, D_MODEL)
    full = {"w_conv": _gather_cols(gathered[1])}
    for n, g in zip(big, gathered[2:]):
        full[n] = _gather_cols(g) if n in COL_SHARDED else _gather_rows(g)
    full.update(small_w)

    (sc_all,) = rowwise("silu_c", lambda rows, params: ([_silu(rows[0])], []), [c_all], [], [(D_MODEL, MXU_DTYPE)])
    mod_part = matmul("ada_fwd", sc_all, shard["w_ada"], "nn")
    (mod_all,) = all_gather("gather_mod", [mod_part])
    mod_row = lax.dynamic_index_in_dim(mod_all, me, axis=1, keepdims=False).reshape(1, 6 * D_MODEL)
    (mod,) = rowwise("ada_bias", lambda rows, params: ([rows[0] + rows[1]], []), [mod_row, b_ada], [], [(6 * D_MODEL, F32)])

    pos = positions.reshape(SEQ, 1).astype(F32)
    grad_x, gw, small = _local_step(x[0], pos, mod, loss_target[0], full)

    order = list(COL_SHARDED + ROW_SHARDED)
    send = [_scatter_cols(gw[n]) if n in COL_SHARDED else _scatter_rows(gw[n]) for n in order]
    recv = all_to_all("scatter_grads", send)
    (small_all,) = all_gather("gather_small", [_pack_small(small)])

    res = {}
    for n, stack in zip(order, recv):
        res[n] = adamw(f"adamw_{n}", shard[n], stack, args["m_" + n][0], args["v_" + n][0], ADAM_TILE.get(n))
    packed = adamw("adamw_small", _pack_small(small_w), small_all, _pack_small({n: args["m_" + n] for n in small_w}),
                   _pack_small({n: args["v_" + n] for n in small_w}))
    small_res = [_unpack_small(p) for p in packed]
    dmod_all = small_all.reshape(N_DEV, -1)[:, 1:1 + 6 * D_MODEL]
    dmod_mine = lax.dynamic_slice_in_dim(dmod_all, me * (6 * D_MODEL // N_DEV), 6 * D_MODEL // N_DEV, axis=1)
    g_w_ada = matmul("ada_wgrad", sc_all, dmod_mine, "tn")
    res["w_ada"] = adamw("adamw_w_ada", shard["w_ada"], g_w_ada[None], m_w_ada[0], v_w_ada[0], ADAM_TILE["w_ada"])

    def leaf(kind, n):
        if n in res:
            return res[n][kind][None]
        return small_res[kind][n]

    loss = small_res[0]["loss"].reshape(())
    return (loss, grad_x[None], *[leaf(k, n) for k in range(4) for n in OUT_WEIGHTS])
```

```python
import functools
import math

import jax
import jax.numpy as jnp
from jax import lax
from jax.experimental import pallas as pl
from jax.experimental.pallas import tpu as pltpu

F32 = jnp.float32
MXU_DTYPE = jnp.bfloat16

N_DEV = 8
D_MODEL = 1024
SEQ = 2048
HEADS = 8
NOPE = 64
ROPE = 32
V_DIM = 64
Q_LORA = 512
KV_LORA = 256
DIL_DIM = 64
DIL_WIDTH = HEADS * DIL_DIM
DILATIONS = (1, 4, 16)
SPAN = 128
IN_COLS = Q_LORA + KV_LORA + ROPE + 3 * DIL_WIDTH
D_FF = 2816
ROPE_THETA = 10000.0
EPS = 1e-6
NEG_INF = -1e30
ADAM_LR, ADAM_B1, ADAM_B2, ADAM_EPS, ADAM_WD, ADAM_STEP = 0.001, 0.9, 0.999, 1e-08, 0.01, 10
VMEM_LIMIT = 56 * 1024 * 1024
MESH_ID = pl.DeviceIdType.MESH


def _params(**kw):
    return pltpu.CompilerParams(vmem_limit_bytes=VMEM_LIMIT, **kw)


def rowwise(name, fn, rows, params, out_rows, out_accs=(), tm=512, dep=None):
    deps = [] if dep is None else [dep]
    R = rows[0].shape[0]
    tm = min(tm, R)
    steps = R // tm
    assert steps * tm == R
    in_specs = []
    for a in rows:
        ri, di = a.shape
        if ri == R:
            in_specs.append(pl.BlockSpec((tm, di), lambda i: (i, 0)))
        else:
            per = ri // tm
            assert per * tm == ri
            in_specs.append(pl.BlockSpec((tm, di), lambda i, per=per: (i % per, 0)))
    for p in params:
        in_specs.append(pl.BlockSpec(p.shape, lambda i: (0,) * p.ndim))
    in_specs += [pl.BlockSpec(memory_space=pl.ANY)] * len(deps)
    out_shape = [jax.ShapeDtypeStruct((R, d), dt) for d, dt in out_rows]
    out_specs = [pl.BlockSpec((tm, d), lambda i: (i, 0)) for d, _ in out_rows]
    out_shape += [jax.ShapeDtypeStruct((1, n), F32) for n in out_accs]
    out_specs += [pl.BlockSpec((1, n), lambda i: (0, 0)) for n in out_accs]
    nr, npar, no, na = len(rows), len(params), len(out_rows), len(out_accs)

    def body(*refs):
        rvals = [r[...] for r in refs[:nr]]
        pvals = [r[...] for r in refs[nr:nr + npar]]
        outs, accs = fn(rvals, pvals)
        first_out = nr + npar + len(deps)
        for ref, v in zip(refs[first_out:first_out + no], outs, strict=True):
            ref[...] = v.astype(ref.dtype)
        if na:
            acc_refs = refs[first_out + no:]
            i = pl.program_id(0)

            @pl.when(i == 0)
            def _():
                for ref, v in zip(acc_refs, accs, strict=True):
                    ref[...] = v

            @pl.when(i > 0)
            def _():
                for ref, v in zip(acc_refs, accs, strict=True):
                    ref[...] += v

    res = pl.pallas_call(body, name=name, grid=(steps,), in_specs=in_specs, out_specs=out_specs,
                         out_shape=out_shape, compiler_params=_params())(*rows, *params, *deps)
    return list(res)


_DIMS = {"nn": ((1,), (0,)), "nt": ((1,), (1,)), "tn": ((0,), (0,))}


def _dot(a, b, mode="nn"):
    return lax.dot_general(a.astype(MXU_DTYPE), b.astype(MXU_DTYPE), (_DIMS[mode], ((), ())),
                           preferred_element_type=F32)


def matmul(name, a, b, mode, tm=None, tn=None, tk=None, out_dtype=F32, dep=None):
    if mode == "tn":
        K, M = a.shape
    else:
        M, K = a.shape
    N = b.shape[0] if mode == "nt" else b.shape[1]
    tm, tn, tk = tm or M, tn or N, tk or K
    nm, nn, nk = M // tm, N // tn, K // tk
    assert nm * tm == M and nn * tn == N and nk * tk == K
    a_spec = pl.BlockSpec((tk, tm), lambda i, j, k: (k, i)) if mode == "tn" else pl.BlockSpec((tm, tk), lambda i, j, k: (i, k))
    b_spec = pl.BlockSpec((tn, tk), lambda i, j, k: (j, k)) if mode == "nt" else pl.BlockSpec((tk, tn), lambda i, j, k: (k, j))

    deps = [] if dep is None else [dep]

    def body(a_ref, b_ref, *rest):
        o_ref, scratch = rest[len(deps)], rest[len(deps) + 1:]
        p = _dot(a_ref[...], b_ref[...], mode)
        if nk == 1:
            o_ref[...] = p.astype(o_ref.dtype)
        else:
            acc = scratch[0]
            k = pl.program_id(2)

            @pl.when(k == 0)
            def _():
                acc[...] = p

            @pl.when(k > 0)
            def _():
                acc[...] += p

            @pl.when(k == nk - 1)
            def _():
                o_ref[...] = acc[...].astype(o_ref.dtype)

    return pl.pallas_call(
        body, name=name, grid=(nm, nn, nk), in_specs=[a_spec, b_spec] + [pl.BlockSpec(memory_space=pl.ANY)] * len(deps),
        out_specs=pl.BlockSpec((tm, tn), lambda i, j, k: (i, j)),
        out_shape=jax.ShapeDtypeStruct((M, N), out_dtype),
        scratch_shapes=[pltpu.VMEM((tm, tn), F32)] if nk > 1 else [],
        compiler_params=_params())(a, b, *deps)


def _rms(x, g):
    rstd = lax.rsqrt(jnp.mean(x * x, axis=-1, keepdims=True) + EPS)
    n = x * rstd
    return n * g, n, rstd


def _rms_bwd(dy, n, rstd, g):
    dg = jnp.sum(dy * n, axis=0, keepdims=True)
    dn = dy * g
    dx = rstd * (dn - n * jnp.mean(dn * n, axis=-1, keepdims=True))
    return dx, dg


def _rot(x):
    h = x.shape[-1] // 2
    return jnp.concatenate([-x[:, h:], x[:, :h]], axis=-1)


def _rot_t(z):
    h = z.shape[-1] // 2
    return jnp.concatenate([z[:, h:], -z[:, :h]], axis=-1)


def _rope(x, cos, sin):
    return x * cos + _rot(x) * sin


def _rope_bwd(dy, cos, sin):
    return dy * cos + _rot_t(dy * sin)


def _norm_rope_bwd(dy, x, g, cos, sin):
    _, n, rstd = _rms(x, g)
    return _rms_bwd(_rope_bwd(dy, cos, sin), n, rstd, g)


def _norm_bwd(dy, x, g):
    _, n, rstd = _rms(x, g)
    return _rms_bwd(dy, n, rstd, g)


def _colsum(v):
    return jnp.sum(v, axis=0, keepdims=True)


def _silu(x):
    return x * (1.0 / (1.0 + jnp.exp(-x)))


def attn_fwd(name, q, k, v, scale, tq=512):
    H, S, Dk = q.shape
    Dv = v.shape[-1]

    def body(q_ref, k_ref, v_ref, o_ref, lse_ref):
        for i in range(S // tq):
            kext = (i + 1) * tq
            blk = slice(i * tq, kext)
            s = _dot(q_ref[0, blk, :], k_ref[0, :kext, :], "nt") * scale
            row = lax.broadcasted_iota(jnp.int32, s.shape, 0) + i * tq
            col = lax.broadcasted_iota(jnp.int32, s.shape, 1)
            s = jnp.where(col <= row, s, NEG_INF)
            m = jnp.max(s, axis=-1, keepdims=True)
            e = jnp.exp(s - m)
            l = jnp.sum(e, axis=-1, keepdims=True)
            o_ref[0, blk, :] = _dot(e / l, v_ref[0, :kext, :])
            lse_ref[0, blk, :] = m + jnp.log(l)

    spec = lambda d: pl.BlockSpec((1, S, d), lambda h: (h, 0, 0))
    return pl.pallas_call(
        body, name=name, grid=(H,), in_specs=[spec(Dk), spec(Dk), spec(Dv)], out_specs=[spec(Dv), spec(1)],
        out_shape=[jax.ShapeDtypeStruct((H, S, Dv), F32), jax.ShapeDtypeStruct((H, S, 1), F32)],
        compiler_params=_params())(q, k, v)


def attn_bwd(name, q, k, v, o, do, lse, scale, tq=512):
    H, S, Dk = q.shape
    Dv = v.shape[-1]

    def body(q_ref, k_ref, v_ref, o_ref, do_ref, lse_ref, dq_ref, dk_ref, dv_ref, dkpe_ref, dk_acc, dv_acc):
        dk_acc[...] = jnp.zeros_like(dk_acc)
        dv_acc[...] = jnp.zeros_like(dv_acc)
        for i in range(S // tq):
            kext = (i + 1) * tq
            blk = slice(i * tq, kext)
            qi, kk, vv = q_ref[0, blk, :], k_ref[0, :kext, :], v_ref[0, :kext, :]
            doi = do_ref[0, blk, :]
            s = _dot(qi, kk, "nt") * scale
            row = lax.broadcasted_iota(jnp.int32, s.shape, 0) + i * tq
            col = lax.broadcasted_iota(jnp.int32, s.shape, 1)
            p = jnp.where(col <= row, jnp.exp(s - lse_ref[0, blk, :]), 0.0)
            dp = _dot(doi, vv, "nt")
            delta = jnp.sum(doi * o_ref[0, blk, :], axis=-1, keepdims=True)
            ds = p * (dp - delta) * scale
            dq_ref[0, blk, :] = _dot(ds, kk)
            dk_acc[:kext, :] += _dot(ds, qi, "tn")
            dv_acc[:kext, :] += _dot(p, doi, "tn")
        dk_ref[0] = dk_acc[...]
        dv_ref[0] = dv_acc[...]
        h = pl.program_id(0)

        @pl.when(h == 0)
        def _():
            dkpe_ref[...] = dk_acc[:, NOPE:NOPE + ROPE]

        @pl.when(h > 0)
        def _():
            dkpe_ref[...] += dk_acc[:, NOPE:NOPE + ROPE]

    spec = lambda d: pl.BlockSpec((1, S, d), lambda h: (h, 0, 0))
    return pl.pallas_call(
        body, name=name, grid=(H,),
        in_specs=[spec(Dk), spec(Dk), spec(Dv), spec(Dv), spec(Dv), spec(1)],
        out_specs=[spec(Dk), spec(Dk), spec(Dv), pl.BlockSpec((S, ROPE), lambda h: (0, 0))],
        out_shape=[jax.ShapeDtypeStruct((H, S, Dk), F32), jax.ShapeDtypeStruct((H, S, Dk), F32),
                   jax.ShapeDtypeStruct((H, S, Dv), F32), jax.ShapeDtypeStruct((S, ROPE), F32)],
        scratch_shapes=[pltpu.VMEM((S, Dk), F32), pltpu.VMEM((S, Dv), F32)],
        compiler_params=_params())(q, k, v, o, do, lse)


def _band_blocks(L, tq):
    out = []
    for i in range(L // tq):
        out.append((i * tq, (i + 1) * tq, max(0, i * tq - SPAN)))
    return out


def _band_mask(q0, q1, k0):
    shape = (q1 - q0, q1 - k0)
    dist = (lax.broadcasted_iota(jnp.int32, shape, 0) + q0) - (lax.broadcasted_iota(jnp.int32, shape, 1) + k0)
    return (dist >= 0) & (dist <= SPAN)


def band_fwd(name, q, k, v, gb):
    G, L, D = q.shape
    tq = min(L, 512)
    scale = D ** -0.5

    def body(q_ref, k_ref, v_ref, o_ref, lse_ref):
        for g in range(gb):
            for q0, q1, k0 in _band_blocks(L, tq):
                s = _dot(q_ref[g, q0:q1, :], k_ref[g, k0:q1, :], "nt") * scale
                s = jnp.where(_band_mask(q0, q1, k0), s, NEG_INF)
                m = jnp.max(s, axis=-1, keepdims=True)
                e = jnp.exp(s - m)
                l = jnp.sum(e, axis=-1, keepdims=True)
                o_ref[g, q0:q1, :] = _dot(e / l, v_ref[g, k0:q1, :])
                lse_ref[g, q0:q1, :] = m + jnp.log(l)

    spec = lambda d: pl.BlockSpec((gb, L, d), lambda i: (i, 0, 0))
    return pl.pallas_call(
        body, name=name, grid=(G // gb,), in_specs=[spec(D)] * 3, out_specs=[spec(D), spec(1)],
        out_shape=[jax.ShapeDtypeStruct((G, L, D), F32), jax.ShapeDtypeStruct((G, L, 1), F32)],
        compiler_params=_params())(q, k, v)


def band_bwd(name, q, k, v, o, lse, do, dlse, gb):
    G, L, D = q.shape
    tq = min(L, 512)
    scale = D ** -0.5

    def body(q_ref, k_ref, v_ref, o_ref, lse_ref, do_ref, dlse_ref, dq_ref, dk_ref, dv_ref):
        dk_ref[...] = jnp.zeros_like(dk_ref)
        dv_ref[...] = jnp.zeros_like(dv_ref)
        for g in range(gb):
            for q0, q1, k0 in _band_blocks(L, tq):
                qi, kk, vv = q_ref[g, q0:q1, :], k_ref[g, k0:q1, :], v_ref[g, k0:q1, :]
                doi = do_ref[g, q0:q1, :]
                s = _dot(qi, kk, "nt") * scale
                p = jnp.where(_band_mask(q0, q1, k0), jnp.exp(s - lse_ref[g, q0:q1, :]), 0.0)
                dp = _dot(doi, vv, "nt")
                delta = jnp.sum(doi * o_ref[g, q0:q1, :], axis=-1, keepdims=True)
                ds = p * (dp - delta + dlse_ref[g, q0:q1, :]) * scale
                dq_ref[g, q0:q1, :] = _dot(ds, kk)
                dk_ref[g, k0:q1, :] += _dot(ds, qi, "tn")
                dv_ref[g, k0:q1, :] += _dot(p, doi, "tn")

    spec = lambda d: pl.BlockSpec((gb, L, d), lambda i: (i, 0, 0))
    return pl.pallas_call(
        body, name=name, grid=(G // gb,), in_specs=[spec(D)] * 4 + [spec(1), spec(D), spec(1)], out_specs=[spec(D)] * 3,
        out_shape=[jax.ShapeDtypeStruct((G, L, D), F32)] * 3, compiler_params=_params())(q, k, v, o, lse, do, dlse)


def _shift_down(u, n):
    t = lax.broadcasted_iota(jnp.int32, u.shape, 0)
    return jnp.where(t >= n, pltpu.roll(u, n, axis=0), 0.0)


def _shift_up(u, n):
    rows = u.shape[0]
    t = lax.broadcasted_iota(jnp.int32, u.shape, 0)
    return jnp.where(t < rows - n, pltpu.roll(u, rows - n, axis=0), 0.0)


def _conv(u, w, b):
    return w[2:3, :] * u + w[1:2, :] * _shift_down(u, 1) + w[0:1, :] * _shift_down(u, 2) + b


CONV_TC = 256
CONV_NB = D_FF // CONV_TC


def conv_glu_fwd(name, up, w_conv, b_conv):
    S = up.shape[0]

    def body(ug_ref, uv_ref, wg_ref, wv_ref, bg_ref, bv_ref, act_ref):
        gate = _conv(ug_ref[...], wg_ref[...], bg_ref[...])
        val = _conv(uv_ref[...], wv_ref[...], bv_ref[...])
        act_ref[...] = (_silu(gate) * val).astype(act_ref.dtype)

    col = lambda r, off: pl.BlockSpec((r, CONV_TC), lambda j: (0, j + off))
    return pl.pallas_call(
        body, name=name, grid=(CONV_NB,),
        in_specs=[col(S, 0), col(S, CONV_NB), col(3, 0), col(3, CONV_NB), col(1, 0), col(1, CONV_NB)],
        out_specs=col(S, 0), out_shape=jax.ShapeDtypeStruct((S, D_FF), MXU_DTYPE),
        compiler_params=_params())(up, up, w_conv, w_conv, b_conv, b_conv)


def conv_glu_bwd(name, up, w_conv, b_conv, dact):
    S = up.shape[0]

    def body(uo_ref, up_ref, wo_ref, wp_ref, bo_ref, bp_ref, da_ref, dup_ref, dw_ref, db_ref):
        j = pl.program_id(0)
        uo = uo_ref[...]
        own = _conv(uo, wo_ref[...], bo_ref[...])
        partner = _conv(up_ref[...], wp_ref[...], bp_ref[...])
        da = da_ref[...]
        sig = 1.0 / (1.0 + jnp.exp(-own))
        d_gate = da * partner * (sig * (1.0 + own * (1.0 - sig)))
        d_val = da * _silu(partner)
        du = jnp.where(j < CONV_NB, d_gate, d_val)
        w = wo_ref[...]
        dup_ref[...] = (w[2:3, :] * du + w[1:2, :] * _shift_up(du, 1) + w[0:1, :] * _shift_up(du, 2)).astype(dup_ref.dtype)
        dw_ref[...] = jnp.concatenate([_colsum(du * _shift_down(uo, 2)), _colsum(du * _shift_down(uo, 1)), _colsum(du * uo)], axis=0)
        db_ref[...] = _colsum(du)

    own = lambda r: pl.BlockSpec((r, CONV_TC), lambda j: (0, j))
    other = lambda r: pl.BlockSpec((r, CONV_TC), lambda j: (0, (j + CONV_NB) % (2 * CONV_NB)))
    return pl.pallas_call(
        body, name=name, grid=(2 * CONV_NB,),
        in_specs=[own(S), other(S), own(3), other(3), own(1), other(1), pl.BlockSpec((S, CONV_TC), lambda j: (0, j % CONV_NB))],
        out_specs=[own(S), own(3), own(1)],
        out_shape=[jax.ShapeDtypeStruct((S, 2 * D_FF), MXU_DTYPE), jax.ShapeDtypeStruct((3, 2 * D_FF), F32),
                   jax.ShapeDtypeStruct((1, 2 * D_FF), F32)],
        compiler_params=_params())(up, up, w_conv, w_conv, b_conv, b_conv, dact)


def adamw(name, w, parts, m, v, tr=None):
    R, C = w.shape
    tr = tr or R
    assert R % tr == 0
    c1 = 1.0 - ADAM_B1 ** ADAM_STEP
    c2 = 1.0 - ADAM_B2 ** ADAM_STEP
    np_ = len(parts)

    def body(*refs):
        w_ref, m_ref, v_ref = refs[0], refs[1 + np_], refs[2 + np_]
        go_ref, d_ref, mo_ref, vo_ref = refs[3 + np_:]
        terms = []
        for part, ref in zip(parts, refs[1:1 + np_], strict=True):
            terms += [ref[...]] if part.ndim == 2 else [ref[p] for p in range(part.shape[0])]
        g = terms[0].astype(F32)
        for term in terms[1:]:
            g = g + term.astype(F32)
        m2 = ADAM_B1 * m_ref[...] + (1.0 - ADAM_B1) * g
        v2 = ADAM_B2 * v_ref[...] + (1.0 - ADAM_B2) * (g * g)
        go_ref[...] = g
        mo_ref[...] = m2
        vo_ref[...] = v2
        d_ref[...] = -ADAM_LR * ((m2 / c1) / (jnp.sqrt(v2 / c2) + ADAM_EPS) + ADAM_WD * w_ref[...])

    blk = pl.BlockSpec((tr, C), lambda i: (i, 0))
    part_specs = [blk if p.ndim == 2 else pl.BlockSpec((p.shape[0], tr, C), lambda i: (0, i, 0)) for p in parts]
    return pl.pallas_call(
        body, name=name, grid=(R // tr,),
        in_specs=[blk] + part_specs + [blk, blk], out_specs=[blk] * 4,
        out_shape=[jax.ShapeDtypeStruct((R, C), F32)] * 4, compiler_params=_params())(w, *parts, m, v)


def _place():
    return lax.axis_index("x"), lax.axis_index("y"), lax.axis_index("c")


def all_gather(name, arrs):
    n = len(arrs)

    def body(*refs):
        ins, outs = refs[:n], refs[n:2 * n]
        send_sems, recv_sems, local_sems = refs[2 * n:]
        x, y, c = _place()
        me, sibling = (x, y, c), (x, y, 1 - c)
        chips = [(1 - x, y), (x, 1 - y), (1 - x, 1 - y)]
        sends = []
        for t in range(n):
            out = outs[t]

            def slot(px, py, pc, out=out):
                return out.at[4 * px + 2 * py + pc]

            def copy(k, block, to, src=None, t=t, slot=slot):
                return pltpu.make_async_remote_copy(
                    src_ref=slot(*block) if src is None else src, dst_ref=slot(*block),
                    send_sem=send_sems.at[7 * t + k], recv_sem=recv_sems.at[7 * t + k],
                    device_id=to, device_id_type=MESH_ID)

            mine = pltpu.make_async_copy(ins[t], slot(*me), local_sems.at[t])
            mine.start()
            first = [copy(0, me, sibling, src=ins[t])]
            first += [copy(1 + j, me, (*chip, c), src=ins[t]) for j, chip in enumerate(chips)]
            for cp in first:
                cp.start()
            sends.append((mine, first, copy))
        for t in range(n):
            mine, first, copy = sends[t]
            passed = [copy(4 + j, (*chip, c), sibling) for j, chip in enumerate(chips)]
            for j, chip in enumerate(chips):
                copy(1 + j, (*chip, c), me).wait_recv()
                passed[j].start()
            copy(0, sibling, me).wait_recv()
            for j, chip in enumerate(chips):
                copy(4 + j, (*chip, 1 - c), me).wait_recv()
            for cp in first + passed:
                cp.wait_send()
            mine.wait()

    any_spec = pl.BlockSpec(memory_space=pl.ANY)
    res = pl.pallas_call(
        body, name=name, in_specs=[any_spec] * n, out_specs=[any_spec] * n,
        out_shape=[jax.ShapeDtypeStruct((N_DEV,) + a.shape, a.dtype) for a in arrs],
        scratch_shapes=[pltpu.SemaphoreType.DMA((7 * n,)), pltpu.SemaphoreType.DMA((7 * n,)), pltpu.SemaphoreType.DMA((n,))],
        compiler_params=pltpu.CompilerParams(has_side_effects=True))(*arrs)
    return list(res)


HBM_SPEC = pl.BlockSpec(memory_space=pltpu.HBM)
SEM_SPEC = pl.BlockSpec(memory_space=pltpu.SEMAPHORE)
DATAFLOW = pltpu.SideEffectType.DATAFLOW_SIDE_EFFECTING


def _exchange_copies(srcs, lands, send_sems, recv_sems, gather):
    x, y, c = _place()
    me = 4 * x + 2 * y + c
    out = []
    for t, (src, land) in enumerate(zip(srcs, lands, strict=True)):
        for k in range(1, N_DEV):
            px, py, pc = x ^ (k >> 2), y ^ ((k >> 1) & 1), c ^ (k & 1)
            out.append(pltpu.make_async_remote_copy(
                src_ref=src if gather else src.at[4 * px + 2 * py + pc],
                dst_ref=land.at[me] if gather else land.at[k - 1],
                send_sem=send_sems.at[7 * t + k - 1], recv_sem=recv_sems.at[7 * t + k - 1],
                device_id=(px, py, pc), device_id_type=MESH_ID))
    return out


def exchange_start(name, arrs, gather):
    n = len(arrs)
    lands = [lax.empty(((N_DEV,) + a.shape) if gather else ((N_DEV - 1,) + a.shape[1:]), a.dtype) for a in arrs]

    def body(*refs):
        srcs, land_refs = refs[:n], refs[n:2 * n]
        send_sems, recv_sems = refs[2 * n], refs[2 * n + 1]
        token = refs[-1]
        for cp in _exchange_copies(srcs, land_refs, send_sems, recv_sems, gather):
            cp.start()
        token[...] = jnp.zeros_like(token)

    hbm = lambda a: pltpu.HBM(a.shape, a.dtype)
    res = pl.pallas_call(
        body, name=name,
        out_shape=(pltpu.SemaphoreType.DMA((7 * n,)), pltpu.SemaphoreType.DMA((7 * n,)), *[hbm(a) for a in arrs],
                   *[hbm(l) for l in lands], jax.ShapeDtypeStruct((8, 128), F32)),
        in_specs=[HBM_SPEC] * (2 * n),
        out_specs=(SEM_SPEC, SEM_SPEC, *[HBM_SPEC] * (2 * n), pl.BlockSpec(memory_space=pltpu.VMEM)),
        input_output_aliases={i: 2 + i for i in range(2 * n)},
        compiler_params=pltpu.CompilerParams(has_side_effects=DATAFLOW),
    )(*[pltpu.with_memory_space_constraint(a, pltpu.HBM) for a in arrs + lands])
    return res[0], res[1], list(res[2:2 + n]), list(res[2 + n:2 + 2 * n]), res[-1]


def exchange_wait(name, started, gather, after):
    send_sems, recv_sems, srcs, lands, _ = started
    n = len(srcs)

    def body(*refs):
        src_refs, land_refs = refs[:n], refs[n:2 * n]
        copies = _exchange_copies(src_refs, land_refs, refs[2 * n], refs[2 * n + 1], gather)
        for cp in copies:
            cp.wait_send()
        for cp in copies:
            cp.wait_recv()

    hbm = lambda a: pltpu.HBM(a.shape, a.dtype)
    res = pl.pallas_call(
        body, name=name, out_shape=tuple(hbm(a) for a in srcs + lands),
        in_specs=[HBM_SPEC] * (2 * n) + [SEM_SPEC, SEM_SPEC, pl.BlockSpec(memory_space=pl.ANY)],
        out_specs=tuple([HBM_SPEC] * (2 * n)), input_output_aliases={i: i for i in range(2 * n)},
        compiler_params=pltpu.CompilerParams(has_side_effects=DATAFLOW),
    )(*srcs, *lands, send_sems, recv_sems, after)
    return list(res[:n]), list(res[n:])


def _to_heads(t, h):
    s = t.shape[0]
    return t.reshape(s, h, -1).transpose(1, 0, 2)


def _from_heads(t):
    h, s, d = t.shape
    return t.transpose(1, 0, 2).reshape(s, h * d)


def _to_strided(t, dil):
    h, s, d = t.shape
    if dil == 1:
        return t
    return t.reshape(h, s // dil, dil, d).transpose(0, 2, 1, 3).reshape(h * dil, s // dil, d)


def _from_strided(t, dil):
    g, l, d = t.shape
    if dil == 1:
        return t
    h = g // dil
    return t.reshape(h, dil, l, d).transpose(0, 2, 1, 3).reshape(h, l * dil, d)


def _gather_cols(stack):
    p, k, n = stack.shape
    return stack.transpose(1, 0, 2).reshape(k, p * n)


def _scatter_cols(full):
    k, n = full.shape
    return full.reshape(k, N_DEV, n // N_DEV).transpose(1, 0, 2)


def _gather_rows(stack):
    p, r, n = stack.shape
    return stack.reshape(p * r, n)


def _scatter_rows(full):
    r, n = full.shape
    return full.reshape(N_DEV, r // N_DEV, n)


SMALL = (("loss", 1), ("b_ada", 6 * D_MODEL), ("g_mix_norm", D_MODEL), ("g_q_lat", Q_LORA), ("g_kv_lat", KV_LORA),
         ("g_mla_q_nope", NOPE), ("g_mla_q_pe", ROPE), ("g_mla_k_nope", NOPE), ("g_mla_k_pe", ROPE),
         ("g_dil_q", DIL_DIM), ("g_dil_k", DIL_DIM), ("g_ffn_norm", D_MODEL), ("b_conv", 2 * D_FF))
SMALL_ROWS = 16
SMALL_COLS = 1024


def _pack_small(values):
    parts = [values[name].reshape(-1).astype(F32) if name in values else jnp.zeros((n,), F32) for name, n in SMALL]
    flat = jnp.concatenate(parts)
    flat = jnp.pad(flat, (0, SMALL_ROWS * SMALL_COLS - flat.shape[0]))
    return flat.reshape(SMALL_ROWS, SMALL_COLS)


def _unpack_small(packed):
    flat = packed.reshape(-1)
    out, off = {}, 0
    for name, n in SMALL:
        out[name] = flat[off:off + n].reshape(1, n)
        off += n
    return out


def _local_step(x, pos, mod, target, w, fetch, emit):
    S = SEQ
    H = HEADS
    sh1, sc1, g1, sh2, sc2, g2 = [mod[:, i * D_MODEL:(i + 1) * D_MODEL] for i in range(6)]

    def inv_freq(d):
        f = jnp.power(ROPE_THETA, -2.0 * jnp.arange(d // 2, dtype=F32) / d)
        return jnp.concatenate([f, f]).reshape(1, d)

    def tables_fn(rows, params):
        (p,), (f32_, f64_) = rows, params
        return [jnp.cos(p * f32_), jnp.sin(p * f32_), jnp.cos(p * f64_), jnp.sin(p * f64_)], []

    cos32, sin32, cos64, sin64 = rowwise("rope_tables", tables_fn, [pos], [inv_freq(ROPE), inv_freq(DIL_DIM)],
                                         [(ROPE, F32), (ROPE, F32), (DIL_DIM, F32), (DIL_DIM, F32)])

    def ln1_fn(rows, params):
        (xv,), (g, sc, sh) = rows, params
        y, _, _ = _rms(xv, g)
        return [y * (1.0 + sc) + sh], []

    (h,) = rowwise("ln1_fwd", ln1_fn, [x], [w["g_mix_norm"], sc1, sh1], [(D_MODEL, MXU_DTYPE)])
    w_in = fetch("w_in", h)
    proj = matmul("proj_fwd", h, w_in, "nn", tm=512)
    q_lat = proj[:, :Q_LORA]
    kv_lat = proj[:, Q_LORA:Q_LORA + KV_LORA]
    k_pe = proj[:, Q_LORA + KV_LORA:Q_LORA + KV_LORA + ROPE]
    o0 = Q_LORA + KV_LORA + ROPE
    qd = _to_heads(proj[:, o0:o0 + DIL_WIDTH], H).reshape(H * S, DIL_DIM)
    kd = _to_heads(proj[:, o0 + DIL_WIDTH:o0 + 2 * DIL_WIDTH], H).reshape(H * S, DIL_DIM)
    vd = _to_heads(proj[:, o0 + 2 * DIL_WIDTH:], H).astype(MXU_DTYPE)

    def lat_fn(rows, params):
        (ql, kvl, kp, c32, s32), (gq, gkv, gkp) = rows, params
        return [_rms(ql, gq)[0], _rms(kvl, gkv)[0], _rope(_rms(kp, gkp)[0], c32, s32)], []

    qln, kvn, kper = rowwise("latent_fwd", lat_fn, [q_lat, kv_lat, k_pe, cos32, sin32],
                             [w["g_q_lat"], w["g_kv_lat"], w["g_mla_k_pe"]],
                             [(Q_LORA, MXU_DTYPE), (KV_LORA, MXU_DTYPE), (ROPE, MXU_DTYPE)])
    w_q_b, w_kv_b = fetch("w_q_b", qln), fetch("w_kv_b", kvn)
    q = matmul("q_fwd", qln, w_q_b, "nn", tm=1024)
    kv = matmul("kv_fwd", kvn, w_kv_b, "nn", tm=1024)
    qh = _to_heads(q, H)
    kvh = _to_heads(kv, H)
    q_nope = qh[..., :NOPE].reshape(H * S, NOPE)
    q_pe = qh[..., NOPE:].reshape(H * S, ROPE)
    k_nope = kvh[..., :NOPE].reshape(H * S, NOPE)
    v_mla = kvh[..., NOPE:].astype(MXU_DTYPE)

    def heads_fn(rows, params):
        (qn, qp, kn, qdv, kdv, c32, s32, c64, s64), (gqn, gqp, gkn, gdq, gdk) = rows, params
        return [_rms(qn, gqn)[0], _rope(_rms(qp, gqp)[0], c32, s32), _rms(kn, gkn)[0],
                _rope(_rms(qdv, gdq)[0], c64, s64), _rope(_rms(kdv, gdk)[0], c64, s64)], []

    head_gains = [w["g_mla_q_nope"], w["g_mla_q_pe"], w["g_mla_k_nope"], w["g_dil_q"], w["g_dil_k"]]
    q_nope_n, q_pe_r, k_nope_n, qd_r, kd_r = rowwise(
        "heads_fwd", heads_fn, [q_nope, q_pe, k_nope, qd, kd, cos32, sin32, cos64, sin64], head_gains,
        [(NOPE, MXU_DTYPE), (ROPE, MXU_DTYPE), (NOPE, MXU_DTYPE), (DIL_DIM, MXU_DTYPE), (DIL_DIM, MXU_DTYPE)], tm=S)
    q_mla = jnp.concatenate([q_nope_n.reshape(H, S, NOPE), q_pe_r.reshape(H, S, ROPE)], axis=-1)
    k_mla = jnp.concatenate([k_nope_n.reshape(H, S, NOPE), jnp.broadcast_to(kper[None], (H, S, ROPE))], axis=-1)
    mla_scale = (NOPE + ROPE) ** -0.5
    o_mla, lse_mla = attn_fwd("mla_fwd", q_mla, k_mla, v_mla, mla_scale)

    qd_r = qd_r.reshape(H, S, DIL_DIM)
    kd_r = kd_r.reshape(H, S, DIL_DIM)
    band = []
    for dil in DILATIONS:
        L = S // dil
        gb = {2048: 1, 512: 4, 128: 16}[L]
        qs, ks, vs = _to_strided(qd_r, dil), _to_strided(kd_r, dil), _to_strided(vd, dil)
        o_s, lse_s = band_fwd(f"band{dil}_fwd", qs, ks, vs, gb)
        band.append((qs, ks, vs, o_s, lse_s, gb))
    o_nat = [_from_strided(b[3], dil).reshape(H * S, DIL_DIM) for b, dil in zip(band, DILATIONS)]
    lse_nat = [_from_strided(b[4], dil).reshape(H * S, 1) for b, dil in zip(band, DILATIONS)]

    def mix_weights(ls):
        m = jnp.maximum(jnp.maximum(ls[0], ls[1]), ls[2])
        e = [jnp.exp(l - m) for l in ls]
        den = e[0] + e[1] + e[2]
        return [ei / den for ei in e]

    def combine_fn(rows, params):
        os_, ls = rows[:3], rows[3:]
        wt = mix_weights(ls)
        return [wt[0] * os_[0] + wt[1] * os_[1] + wt[2] * os_[2]], []

    (o_dil,) = rowwise("dil_combine_fwd", combine_fn, o_nat + lse_nat, [], [(DIL_DIM, F32)], tm=S)
    o_cat = jnp.concatenate([_from_heads(o_mla), _from_heads(o_dil.reshape(H, S, DIL_DIM))], axis=-1).astype(MXU_DTYPE)
    w_o = fetch("w_o", o_cat)
    mix = matmul("mix_fwd", o_cat, w_o, "nn", tm=512)

    def mid_fn(rows, params):
        (xv, mx), (gate1, g, sc, sh) = rows, params
        x1 = xv + gate1 * mx
        y, _, _ = _rms(x1, g)
        return [x1, y * (1.0 + sc) + sh], []

    x1, h2 = rowwise("mid_fwd", mid_fn, [x, mix], [g1, w["g_ffn_norm"], sc2, sh2], [(D_MODEL, F32), (D_MODEL, MXU_DTYPE)])
    w_up, w_conv, w_down = fetch("w_up", h2), fetch("w_conv", h2), fetch("w_down", h2)
    up = matmul("up_fwd", h2, w_up, "nn", tm=512, tn=1408)
    act = conv_glu_fwd("conv_fwd", up, w_conv, w["b_conv"])
    dn = matmul("down_fwd", act, w_down, "nn", tm=512)

    def final_fn(rows, params):
        (x1v, dnv, tgt), (gate2,) = rows, params
        r = x1v + gate2 * dnv - tgt
        dy = r * (1.0 / D_MODEL)
        loss = jnp.sum(_colsum(r * r), axis=-1, keepdims=True) * (0.5 / D_MODEL)
        return [dy, gate2 * dy], [loss, _colsum(dy * dnv)]

    dy, d_dn, loss, dg2 = rowwise("loss_head", final_fn, [x1, dn, target], [g2], [(D_MODEL, F32), (D_MODEL, MXU_DTYPE)],
                                  [1, D_MODEL])
    emit("w_down", matmul("down_wgrad", act, d_dn, "tn", tm=1408, out_dtype=MXU_DTYPE))
    dact = matmul("down_dgrad", d_dn, w_down, "nt", tm=512)
    dup, g_w_conv, gs_b_conv = conv_glu_bwd("conv_bwd", up, w_conv, w["b_conv"], dact)
    emit("w_conv", g_w_conv)
    sent = emit("w_up", matmul("up_wgrad", h2, dup, "tn", tn=1408, out_dtype=MXU_DTYPE))
    dh2 = matmul("up_dgrad", dup, w_up, "nt", tm=512, tk=2816, dep=sent)

    def mid_bwd_fn(rows, params):
        (dh2v, dyv, x1v, mx), (gate1, g, sc) = rows, params
        yn, n, rstd = _rms(x1v, g)
        dx_n, dg = _rms_bwd(dh2v * (1.0 + sc), n, rstd, g)
        dx1 = dyv + dx_n
        return [dx1, gate1 * dx1], [dg, _colsum(dh2v * yn), _colsum(dh2v), _colsum(dx1 * mx)]

    dx1, dmix, dg_ffn, dsc2, dsh2, dg1 = rowwise(
        "mid_bwd", mid_bwd_fn, [dh2, dy, x1, mix], [g1, w["g_ffn_norm"], sc2], [(D_MODEL, F32), (D_MODEL, MXU_DTYPE)],
        [D_MODEL] * 4)

    emit("w_o", matmul("mix_wgrad", o_cat, dmix, "tn", out_dtype=MXU_DTYPE))
    do_cat = matmul("mix_dgrad", dmix, w_o, "nt", tm=512)
    do_mla = _to_heads(do_cat[:, :HEADS * V_DIM], H)
    do_dil = _to_heads(do_cat[:, HEADS * V_DIM:], H).reshape(H * S, DIL_DIM)

    def combine_bwd_fn(rows, params):
        os_, ls, dout = rows[:3], rows[3:6], rows[6]
        wt = mix_weights(ls)
        dw = [jnp.sum(dout * o, axis=-1, keepdims=True) for o in os_]
        mean_dw = wt[0] * dw[0] + wt[1] * dw[1] + wt[2] * dw[2]
        return [wt[i] * dout for i in range(3)] + [wt[i] * (dw[i] - mean_dw) for i in range(3)], []

    cb = rowwise("dil_combine_bwd", combine_bwd_fn, o_nat + lse_nat + [do_dil], [],
                 [(DIL_DIM, F32)] * 3 + [(1, F32)] * 3, tm=S)
    dqd_parts, dkd_parts, dvd_parts = [], [], []
    for i, dil in enumerate(DILATIONS):
        qs, ks, vs, o_s, lse_s, gb = band[i]
        do_s = _to_strided(cb[i].reshape(H, S, DIL_DIM), dil)
        dlse_s = _to_strided(cb[3 + i].reshape(H, S, 1), dil)
        dq_s, dk_s, dv_s = band_bwd(f"band{dil}_bwd", qs, ks, vs, o_s, lse_s, do_s, dlse_s, gb)
        dqd_parts.append(_from_strided(dq_s, dil).reshape(H * S, DIL_DIM))
        dkd_parts.append(_from_strided(dk_s, dil).reshape(H * S, DIL_DIM))
        dvd_parts.append(_from_strided(dv_s, dil).reshape(H * S, DIL_DIM))

    dq_mla, dk_mla, dv_mla, dkper = attn_bwd("mla_bwd", q_mla, k_mla, v_mla, o_mla, do_mla, lse_mla, mla_scale)
    dq_nope_n = dq_mla[..., :NOPE].reshape(H * S, NOPE)
    dq_pe_r = dq_mla[..., NOPE:].reshape(H * S, ROPE)
    dk_nope_n = dk_mla[..., :NOPE].reshape(H * S, NOPE)

    def heads_bwd_fn(rows, params):
        dqn, dqp, dkn = rows[0:3]
        dqd_ = rows[3] + rows[4] + rows[5]
        dkd_ = rows[6] + rows[7] + rows[8]
        dvd_ = rows[9] + rows[10] + rows[11]
        qn, qp, kn, qdv, kdv, c32, s32, c64, s64 = rows[12:]
        gqn, gqp, gkn, gdq, gdk = params
        r1 = _norm_bwd(dqn, qn, gqn)
        r2 = _norm_rope_bwd(dqp, qp, gqp, c32, s32)
        r3 = _norm_bwd(dkn, kn, gkn)
        r4 = _norm_rope_bwd(dqd_, qdv, gdq, c64, s64)
        r5 = _norm_rope_bwd(dkd_, kdv, gdk, c64, s64)
        rs = [r1, r2, r3, r4, r5]
        return [r[0] for r in rs] + [dvd_], [r[1] for r in rs]

    hb = rowwise("heads_bwd", heads_bwd_fn,
                 [dq_nope_n, dq_pe_r, dk_nope_n] + dqd_parts + dkd_parts + dvd_parts
                 + [q_nope, q_pe, k_nope, qd, kd, cos32, sin32, cos64, sin64],
                 head_gains, [(NOPE, F32), (ROPE, F32), (NOPE, F32), (DIL_DIM, F32), (DIL_DIM, F32), (DIL_DIM, MXU_DTYPE)],
                 [NOPE, ROPE, NOPE, DIL_DIM, DIL_DIM], tm=S)
    dq_nope, dq_pe, dk_nope, dqd, dkd, dvd = hb[:6]
    dg_q_nope, dg_q_pe, dg_k_nope, dg_dil_q, dg_dil_k = hb[6:]
    dq = _from_heads(jnp.concatenate([dq_nope.reshape(H, S, NOPE), dq_pe.reshape(H, S, ROPE)], axis=-1)).astype(MXU_DTYPE)
    dkv = _from_heads(jnp.concatenate([dk_nope.reshape(H, S, NOPE), dv_mla], axis=-1)).astype(MXU_DTYPE)
    emit("w_q_b", matmul("q_wgrad", qln, dq, "tn", out_dtype=MXU_DTYPE))
    emit("w_kv_b", matmul("kv_wgrad", kvn, dkv, "tn", out_dtype=MXU_DTYPE))
    dqln = matmul("q_dgrad", dq, w_q_b, "nt", tm=1024)
    dkvn = matmul("kv_dgrad", dkv, w_kv_b, "nt", tm=1024)

    def lat_bwd_fn(rows, params):
        (dql, dkvl, dkp, ql, kvl, kp, c32, s32), (gq, gkv, gkp) = rows, params
        r1 = _norm_bwd(dql, ql, gq)
        r2 = _norm_bwd(dkvl, kvl, gkv)
        r3 = _norm_rope_bwd(dkp, kp, gkp, c32, s32)
        return [r1[0], r2[0], r3[0]], [r1[1], r2[1], r3[1]]

    dq_lat, dkv_lat, dk_pe, dg_q_lat, dg_kv_lat, dg_k_pe = rowwise(
        "latent_bwd", lat_bwd_fn, [dqln, dkvn, dkper, q_lat, kv_lat, k_pe, cos32, sin32],
        [w["g_q_lat"], w["g_kv_lat"], w["g_mla_k_pe"]],
        [(Q_LORA, MXU_DTYPE), (KV_LORA, MXU_DTYPE), (ROPE, MXU_DTYPE)], [Q_LORA, KV_LORA, ROPE])
    dproj = jnp.concatenate([dq_lat, dkv_lat, dk_pe,
                             _from_heads(dqd.reshape(H, S, DIL_DIM)).astype(MXU_DTYPE),
                             _from_heads(dkd.reshape(H, S, DIL_DIM)).astype(MXU_DTYPE),
                             _from_heads(dvd.reshape(H, S, DIL_DIM))], axis=-1)
    sent = emit("w_in", matmul("proj_wgrad", h, dproj, "tn", tm=256, out_dtype=MXU_DTYPE))
    dh = matmul("proj_dgrad", dproj, w_in, "nt", tm=512, dep=sent)

    def ln1_bwd_fn(rows, params):
        (dhv, dres, xv), (g, sc) = rows, params
        yn, n, rstd = _rms(xv, g)
        dx_n, dg = _rms_bwd(dhv * (1.0 + sc), n, rstd, g)
        return [dres + dx_n], [dg, _colsum(dhv * yn), _colsum(dhv)]

    grad_x, dg_mix, dsc1, dsh1 = rowwise("ln1_bwd", ln1_bwd_fn, [dh, dx1, x], [w["g_mix_norm"], sc1], [(D_MODEL, F32)],
                                         [D_MODEL] * 3)
    dmod = jnp.concatenate([dsh1, dsc1, dg1, dsh2, dsc2, dg2], axis=-1)
    small = {"loss": loss, "b_ada": dmod, "g_mix_norm": dg_mix, "g_q_lat": dg_q_lat, "g_kv_lat": dg_kv_lat,
             "g_mla_q_nope": dg_q_nope, "g_mla_q_pe": dg_q_pe, "g_mla_k_nope": dg_k_nope, "g_mla_k_pe": dg_k_pe,
             "g_dil_q": dg_dil_q, "g_dil_k": dg_dil_k, "g_ffn_norm": dg_ffn, "b_conv": gs_b_conv}
    return grad_x, small


COL_SHARDED = ("w_in", "w_q_b", "w_kv_b", "w_up", "w_conv")
ROW_SHARDED = ("w_o", "w_down")
ADAM_TILE = {"w_ada": 256, "w_in": 256, "w_up": 256, "w_down": 176}
GATHER_GROUPS = (("w_in", "w_q_b", "w_kv_b"), ("w_o", "w_up", "w_conv", "w_down"))
SCATTER_GROUPS = (("w_down", "w_conv", "w_up"), ("w_o", "w_q_b", "w_kv_b", "w_in"))
OUT_WEIGHTS = ("w_ada", "b_ada", "g_mix_norm", "w_in", "g_q_lat", "w_q_b", "g_kv_lat", "w_kv_b", "g_mla_q_nope", "g_mla_q_pe",
               "g_mla_k_nope", "g_mla_k_pe", "g_dil_q", "g_dil_k", "w_o", "g_ffn_norm", "w_up", "w_conv", "b_conv", "w_down")


def kernel(x, c, positions, w_ada, b_ada, g_mix_norm, w_in, g_q_lat, w_q_b, g_kv_lat, w_kv_b, g_mla_q_nope, g_mla_q_pe, g_mla_k_nope, g_mla_k_pe, g_dil_q, g_dil_k, w_o, g_ffn_norm, w_up, w_conv, b_conv, w_down, loss_target, m_w_ada, m_b_ada, m_g_mix_norm, m_w_in, m_g_q_lat, m_w_q_b, m_g_kv_lat, m_w_kv_b, m_g_mla_q_nope, m_g_mla_q_pe, m_g_mla_k_nope, m_g_mla_k_pe, m_g_dil_q, m_g_dil_k, m_w_o, m_g_ffn_norm, m_w_up, m_w_conv, m_b_conv, m_w_down, v_w_ada, v_b_ada, v_g_mix_norm, v_w_in, v_g_q_lat, v_w_q_b, v_g_kv_lat, v_w_kv_b, v_g_mla_q_nope, v_g_mla_q_pe, v_g_mla_k_nope, v_g_mla_k_pe, v_g_dil_q, v_g_dil_k, v_w_o, v_g_ffn_norm, v_w_up, v_w_conv, v_b_conv, v_w_down):
    args = dict(locals())
    xi, yi, ci = _place()
    me = 4 * xi + 2 * yi + ci
    shard = {n: args[n][0] for n in COL_SHARDED + ROW_SHARDED + ("w_ada",)}
    small_w = {n: args[n] for n, _ in SMALL if n != "loss"}

    payload = {n: shard[n] if n == "w_conv" else shard[n].astype(MXU_DTYPE) for n in COL_SHARDED + ROW_SHARDED}
    gathers = [exchange_start(f"gather{i}_start", [payload[n] for n in grp], gather=True) for i, grp in enumerate(GATHER_GROUPS)]
    full = {}

    def fetch(name, after):
        if name not in full:
            (i, grp), = [(i, grp) for i, grp in enumerate(GATHER_GROUPS) if name in grp]
            srcs, lands = exchange_wait(f"gather{i}_wait", gathers[i], True, after)
            for n, src, land in zip(grp, srcs, lands, strict=True):
                stack = lax.dynamic_update_index_in_dim(land, src, me, 0)
                full[n] = _gather_cols(stack) if n in COL_SHARDED else _gather_rows(stack)
        return full[name]

    (c_all,) = all_gather("gather_c", [c])
    (sc_all,) = rowwise("silu_c", lambda rows, params: ([_silu(rows[0])], []), [c_all.reshape(N_DEV, D_MODEL)], [],
                        [(D_MODEL, MXU_DTYPE)], dep=gathers[0][-1])
    mod_part = matmul("ada_fwd", sc_all, shard["w_ada"], "nn", dep=gathers[1][-1])
    (mod_all,) = all_gather("gather_mod", [mod_part])
    mod_row = lax.dynamic_index_in_dim(mod_all, me, axis=1, keepdims=False).reshape(1, 6 * D_MODEL)
    (mod,) = rowwise("ada_bias", lambda rows, params: ([rows[0] + rows[1]], []), [mod_row, b_ada], [], [(6 * D_MODEL, F32)])

    own, pending, scatters = {}, {}, {}

    def emit(name, grad):
        parts = _scatter_cols(grad) if name in COL_SHARDED else _scatter_rows(grad)
        own[name] = lax.dynamic_index_in_dim(parts, me, 0, keepdims=False)
        pending[name] = parts
        for i, grp in enumerate(SCATTER_GROUPS):
            if name == grp[-1]:
                scatters[i] = exchange_start(f"scatter{i}_start", [pending[n] for n in grp], gather=False)
                return scatters[i][-1]
        return None

    pos = positions.reshape(SEQ, 1).astype(F32)
    grad_x, small = _local_step(x[0], pos, mod, loss_target[0], small_w, fetch, emit)

    (small_all,) = all_gather("gather_small", [_pack_small(small)])
    res = {}
    for i, grp in enumerate(SCATTER_GROUPS):
        _, lands = exchange_wait(f"scatter{i}_wait", scatters[i], False, grad_x)
        for n, land in zip(grp, lands, strict=True):
            res[n] = adamw(f"adamw_{n}", shard[n], [own[n], land], args["m_" + n][0], args["v_" + n][0], ADAM_TILE.get(n))
    packed = adamw("adamw_small", _pack_small(small_w), [small_all], _pack_small({n: args["m_" + n] for n in small_w}),
                   _pack_small({n: args["v_" + n] for n in small_w}))
    small_res = [_unpack_small(p) for p in packed]
    dmod_all = small_all.reshape(N_DEV, -1)[:, 1:1 + 6 * D_MODEL]
    dmod_mine = lax.dynamic_slice_in_dim(dmod_all, me * (6 * D_MODEL // N_DEV), 6 * D_MODEL // N_DEV, axis=1)
    g_w_ada = matmul("ada_wgrad", sc_all, dmod_mine, "tn")
    res["w_ada"] = adamw("adamw_w_ada", shard["w_ada"], [g_w_ada], m_w_ada[0], v_w_ada[0], ADAM_TILE["w_ada"])

    def leaf(kind, n):
        if n in res:
            return res[n][kind][None]
        return small_res[kind][n]

    loss = small_res[0]["loss"].reshape(())
    return (loss, grad_x[None], *[leaf(k, n) for k in range(4) for n in OUT_WEIGHTS])
```

```python
import functools
import math

import jax
import jax.numpy as jnp
from jax import lax
from jax.experimental import pallas as pl
from jax.experimental.pallas import tpu as pltpu

F32 = jnp.float32
MXU_DTYPE = jnp.bfloat16

N_DEV = 8
D_MODEL = 1024
SEQ = 2048
HEADS = 8
NOPE = 64
ROPE = 32
V_DIM = 64
Q_LORA = 512
KV_LORA = 256
DIL_DIM = 64
DIL_WIDTH = HEADS * DIL_DIM
DILATIONS = (1, 4, 16)
SPAN = 128
IN_COLS = Q_LORA + KV_LORA + ROPE + 3 * DIL_WIDTH
D_FF = 2816
ROPE_THETA = 10000.0
EPS = 1e-6
NEG_INF = -1e30
ADAM_LR, ADAM_B1, ADAM_B2, ADAM_EPS, ADAM_WD, ADAM_STEP = 0.001, 0.9, 0.999, 1e-08, 0.01, 10
VMEM_LIMIT = 56 * 1024 * 1024
MESH_ID = pl.DeviceIdType.MESH


def _params(**kw):
    return pltpu.CompilerParams(vmem_limit_bytes=VMEM_LIMIT, **kw)


def rowwise(name, fn, rows, params, out_rows, out_accs=(), tm=512, dep=None):
    deps = [] if dep is None else [dep]
    R = rows[0].shape[0]
    tm = min(tm, R)
    steps = R // tm
    assert steps * tm == R
    in_specs = []
    for a in rows:
        ri, di = a.shape
        if ri == R:
            in_specs.append(pl.BlockSpec((tm, di), lambda i: (i, 0)))
        else:
            per = ri // tm
            assert per * tm == ri
            in_specs.append(pl.BlockSpec((tm, di), lambda i, per=per: (i % per, 0)))
    for p in params:
        in_specs.append(pl.BlockSpec(p.shape, lambda i: (0,) * p.ndim))
    in_specs += [pl.BlockSpec(memory_space=pl.ANY)] * len(deps)
    out_shape = [jax.ShapeDtypeStruct((R, d), dt) for d, dt in out_rows]
    out_specs = [pl.BlockSpec((tm, d), lambda i: (i, 0)) for d, _ in out_rows]
    out_shape += [jax.ShapeDtypeStruct((1, n), F32) for n in out_accs]
    out_specs += [pl.BlockSpec((1, n), lambda i: (0, 0)) for n in out_accs]
    nr, npar, no, na = len(rows), len(params), len(out_rows), len(out_accs)

    def body(*refs):
        rvals = [r[...] for r in refs[:nr]]
        pvals = [r[...] for r in refs[nr:nr + npar]]
        outs, accs = fn(rvals, pvals)
        first_out = nr + npar + len(deps)
        for ref, v in zip(refs[first_out:first_out + no], outs, strict=True):
            ref[...] = v.astype(ref.dtype)
        if na:
            acc_refs = refs[first_out + no:]
            i = pl.program_id(0)

            @pl.when(i == 0)
            def _():
                for ref, v in zip(acc_refs, accs, strict=True):
                    ref[...] = v

            @pl.when(i > 0)
            def _():
                for ref, v in zip(acc_refs, accs, strict=True):
                    ref[...] += v

    res = pl.pallas_call(body, name=name, grid=(steps,), in_specs=in_specs, out_specs=out_specs,
                         out_shape=out_shape, compiler_params=_params())(*rows, *params, *deps)
    return list(res)


_DIMS = {"nn": ((1,), (0,)), "nt": ((1,), (1,)), "tn": ((0,), (0,))}


def _dot(a, b, mode="nn"):
    return lax.dot_general(a.astype(MXU_DTYPE), b.astype(MXU_DTYPE), (_DIMS[mode], ((), ())),
                           preferred_element_type=F32)


def matmul(name, a, b, mode, tm=None, tn=None, tk=None, out_dtype=F32, dep=None):
    if mode == "tn":
        K, M = a.shape
    else:
        M, K = a.shape
    N = b.shape[0] if mode == "nt" else b.shape[1]
    tm, tn, tk = tm or M, tn or N, tk or K
    nm, nn, nk = M // tm, N // tn, K // tk
    assert nm * tm == M and nn * tn == N and nk * tk == K
    a_spec = pl.BlockSpec((tk, tm), lambda i, j, k: (k, i)) if mode == "tn" else pl.BlockSpec((tm, tk), lambda i, j, k: (i, k))
    b_spec = pl.BlockSpec((tn, tk), lambda i, j, k: (j, k)) if mode == "nt" else pl.BlockSpec((tk, tn), lambda i, j, k: (k, j))

    deps = [] if dep is None else [dep]

    def body(a_ref, b_ref, *rest):
        o_ref, scratch = rest[len(deps)], rest[len(deps) + 1:]
        p = _dot(a_ref[...], b_ref[...], mode)
        if nk == 1:
            o_ref[...] = p.astype(o_ref.dtype)
        else:
            acc = scratch[0]
            k = pl.program_id(2)

            @pl.when(k == 0)
            def _():
                acc[...] = p

            @pl.when(k > 0)
            def _():
                acc[...] += p

            @pl.when(k == nk - 1)
            def _():
                o_ref[...] = acc[...].astype(o_ref.dtype)

    return pl.pallas_call(
        body, name=name, grid=(nm, nn, nk), in_specs=[a_spec, b_spec] + [pl.BlockSpec(memory_space=pl.ANY)] * len(deps),
        out_specs=pl.BlockSpec((tm, tn), lambda i, j, k: (i, j)),
        out_shape=jax.ShapeDtypeStruct((M, N), out_dtype),
        scratch_shapes=[pltpu.VMEM((tm, tn), F32)] if nk > 1 else [],
        compiler_params=_params())(a, b, *deps)


def _rms(x, g):
    rstd = lax.rsqrt(jnp.mean(x * x, axis=-1, keepdims=True) + EPS)
    n = x * rstd
    return n * g, n, rstd


def _rms_bwd(dy, n, rstd, g):
    dg = jnp.sum(dy * n, axis=0, keepdims=True)
    dn = dy * g
    dx = rstd * (dn - n * jnp.mean(dn * n, axis=-1, keepdims=True))
    return dx, dg


def _rot(x):
    h = x.shape[-1] // 2
    return jnp.concatenate([-x[:, h:], x[:, :h]], axis=-1)


def _rot_t(z):
    h = z.shape[-1] // 2
    return jnp.concatenate([z[:, h:], -z[:, :h]], axis=-1)


def _rope(x, cos, sin):
    return x * cos + _rot(x) * sin


def _rope_bwd(dy, cos, sin):
    return dy * cos + _rot_t(dy * sin)


def _norm_rope_bwd(dy, x, g, cos, sin):
    _, n, rstd = _rms(x, g)
    return _rms_bwd(_rope_bwd(dy, cos, sin), n, rstd, g)


def _norm_bwd(dy, x, g):
    _, n, rstd = _rms(x, g)
    return _rms_bwd(dy, n, rstd, g)


def _colsum(v):
    return jnp.sum(v, axis=0, keepdims=True)


def _silu(x):
    return x * (1.0 / (1.0 + jnp.exp(-x)))


def attn_fwd(name, q, k, v, scale, tq=512):
    H, S, Dk = q.shape
    Dv = v.shape[-1]

    def body(q_ref, k_ref, v_ref, o_ref, lse_ref):
        for i in range(S // tq):
            kext = (i + 1) * tq
            blk = slice(i * tq, kext)
            s = _dot(q_ref[0, blk, :], k_ref[0, :kext, :], "nt") * scale
            row = lax.broadcasted_iota(jnp.int32, s.shape, 0) + i * tq
            col = lax.broadcasted_iota(jnp.int32, s.shape, 1)
            s = jnp.where(col <= row, s, NEG_INF)
            m = jnp.max(s, axis=-1, keepdims=True)
            e = jnp.exp(s - m)
            l = jnp.sum(e, axis=-1, keepdims=True)
            o_ref[0, blk, :] = _dot(e / l, v_ref[0, :kext, :])
            lse_ref[0, blk, :] = m + jnp.log(l)

    spec = lambda d: pl.BlockSpec((1, S, d), lambda h: (h, 0, 0))
    return pl.pallas_call(
        body, name=name, grid=(H,), in_specs=[spec(Dk), spec(Dk), spec(Dv)], out_specs=[spec(Dv), spec(1)],
        out_shape=[jax.ShapeDtypeStruct((H, S, Dv), F32), jax.ShapeDtypeStruct((H, S, 1), F32)],
        compiler_params=_params())(q, k, v)


def attn_bwd(name, q, k, v, o, do, lse, scale, tq=512):
    H, S, Dk = q.shape
    Dv = v.shape[-1]

    def body(q_ref, k_ref, v_ref, o_ref, do_ref, lse_ref, dq_ref, dk_ref, dv_ref, dkpe_ref, dk_acc, dv_acc):
        dk_acc[...] = jnp.zeros_like(dk_acc)
        dv_acc[...] = jnp.zeros_like(dv_acc)
        for i in range(S // tq):
            kext = (i + 1) * tq
            blk = slice(i * tq, kext)
            qi, kk, vv = q_ref[0, blk, :], k_ref[0, :kext, :], v_ref[0, :kext, :]
            doi = do_ref[0, blk, :]
            s = _dot(qi, kk, "nt") * scale
            row = lax.broadcasted_iota(jnp.int32, s.shape, 0) + i * tq
            col = lax.broadcasted_iota(jnp.int32, s.shape, 1)
            p = jnp.where(col <= row, jnp.exp(s - lse_ref[0, blk, :]), 0.0)
            dp = _dot(doi, vv, "nt")
            delta = jnp.sum(doi * o_ref[0, blk, :], axis=-1, keepdims=True)
            ds = p * (dp - delta) * scale
            dq_ref[0, blk, :] = _dot(ds, kk)
            dk_acc[:kext, :] += _dot(ds, qi, "tn")
            dv_acc[:kext, :] += _dot(p, doi, "tn")
        dk_ref[0] = dk_acc[...]
        dv_ref[0] = dv_acc[...]
        h = pl.program_id(0)

        @pl.when(h == 0)
        def _():
            dkpe_ref[...] = dk_acc[:, NOPE:NOPE + ROPE]

        @pl.when(h > 0)
        def _():
            dkpe_ref[...] += dk_acc[:, NOPE:NOPE + ROPE]

    spec = lambda d: pl.BlockSpec((1, S, d), lambda h: (h, 0, 0))
    return pl.pallas_call(
        body, name=name, grid=(H,),
        in_specs=[spec(Dk), spec(Dk), spec(Dv), spec(Dv), spec(Dv), spec(1)],
        out_specs=[spec(Dk), spec(Dk), spec(Dv), pl.BlockSpec((S, ROPE), lambda h: (0, 0))],
        out_shape=[jax.ShapeDtypeStruct((H, S, Dk), F32), jax.ShapeDtypeStruct((H, S, Dk), F32),
                   jax.ShapeDtypeStruct((H, S, Dv), F32), jax.ShapeDtypeStruct((S, ROPE), F32)],
        scratch_shapes=[pltpu.VMEM((S, Dk), F32), pltpu.VMEM((S, Dv), F32)],
        compiler_params=_params())(q, k, v, o, do, lse)


def _band_blocks(L, tq):
    out = []
    for i in range(L // tq):
        out.append((i * tq, (i + 1) * tq, max(0, i * tq - SPAN)))
    return out


def _band_mask(q0, q1, k0):
    shape = (q1 - q0, q1 - k0)
    dist = (lax.broadcasted_iota(jnp.int32, shape, 0) + q0) - (lax.broadcasted_iota(jnp.int32, shape, 1) + k0)
    return (dist >= 0) & (dist <= SPAN)


def band_fwd(name, q, k, v, gb):
    G, L, D = q.shape
    tq = min(L, 512)
    scale = D ** -0.5

    def body(q_ref, k_ref, v_ref, o_ref, lse_ref):
        for g in range(gb):
            for q0, q1, k0 in _band_blocks(L, tq):
                s = _dot(q_ref[g, q0:q1, :], k_ref[g, k0:q1, :], "nt") * scale
                s = jnp.where(_band_mask(q0, q1, k0), s, NEG_INF)
                m = jnp.max(s, axis=-1, keepdims=True)
                e = jnp.exp(s - m)
                l = jnp.sum(e, axis=-1, keepdims=True)
                o_ref[g, q0:q1, :] = _dot(e / l, v_ref[g, k0:q1, :])
                lse_ref[g, q0:q1, :] = m + jnp.log(l)

    spec = lambda d: pl.BlockSpec((gb, L, d), lambda i: (i, 0, 0))
    return pl.pallas_call(
        body, name=name, grid=(G // gb,), in_specs=[spec(D)] * 3, out_specs=[spec(D), spec(1)],
        out_shape=[jax.ShapeDtypeStruct((G, L, D), F32), jax.ShapeDtypeStruct((G, L, 1), F32)],
        compiler_params=_params())(q, k, v)


def band_bwd(name, q, k, v, o, lse, do, dlse, gb):
    G, L, D = q.shape
    tq = min(L, 512)
    scale = D ** -0.5

    def body(q_ref, k_ref, v_ref, o_ref, lse_ref, do_ref, dlse_ref, dq_ref, dk_ref, dv_ref):
        dk_ref[...] = jnp.zeros_like(dk_ref)
        dv_ref[...] = jnp.zeros_like(dv_ref)
        for g in range(gb):
            for q0, q1, k0 in _band_blocks(L, tq):
                qi, kk, vv = q_ref[g, q0:q1, :], k_ref[g, k0:q1, :], v_ref[g, k0:q1, :]
                doi = do_ref[g, q0:q1, :]
                s = _dot(qi, kk, "nt") * scale
                p = jnp.where(_band_mask(q0, q1, k0), jnp.exp(s - lse_ref[g, q0:q1, :]), 0.0)
                dp = _dot(doi, vv, "nt")
                delta = jnp.sum(doi * o_ref[g, q0:q1, :], axis=-1, keepdims=True)
                ds = p * (dp - delta + dlse_ref[g, q0:q1, :]) * scale
                dq_ref[g, q0:q1, :] = _dot(ds, kk)
                dk_ref[g, k0:q1, :] += _dot(ds, qi, "tn")
                dv_ref[g, k0:q1, :] += _dot(p, doi, "tn")

    spec = lambda d: pl.BlockSpec((gb, L, d), lambda i: (i, 0, 0))
    return pl.pallas_call(
        body, name=name, grid=(G // gb,), in_specs=[spec(D)] * 4 + [spec(1), spec(D), spec(1)], out_specs=[spec(D)] * 3,
        out_shape=[jax.ShapeDtypeStruct((G, L, D), F32)] * 3, compiler_params=_params())(q, k, v, o, lse, do, dlse)


def _shift_down(u, n):
    t = lax.broadcasted_iota(jnp.int32, u.shape, 0)
    return jnp.where(t >= n, pltpu.roll(u, n, axis=0), 0.0)


def _shift_up(u, n):
    rows = u.shape[0]
    t = lax.broadcasted_iota(jnp.int32, u.shape, 0)
    return jnp.where(t < rows - n, pltpu.roll(u, rows - n, axis=0), 0.0)


def _conv(u, w, b):
    return w[2:3, :] * u + w[1:2, :] * _shift_down(u, 1) + w[0:1, :] * _shift_down(u, 2) + b


CONV_TC = 256
CONV_NB = D_FF // CONV_TC


def conv_glu_fwd(name, up, w_conv, b_conv):
    S = up.shape[0]

    def body(ug_ref, uv_ref, wg_ref, wv_ref, bg_ref, bv_ref, act_ref):
        gate = _conv(ug_ref[...], wg_ref[...], bg_ref[...])
        val = _conv(uv_ref[...], wv_ref[...], bv_ref[...])
        act_ref[...] = (_silu(gate) * val).astype(act_ref.dtype)

    col = lambda r, off: pl.BlockSpec((r, CONV_TC), lambda j: (0, j + off))
    return pl.pallas_call(
        body, name=name, grid=(CONV_NB,),
        in_specs=[col(S, 0), col(S, CONV_NB), col(3, 0), col(3, CONV_NB), col(1, 0), col(1, CONV_NB)],
        out_specs=col(S, 0), out_shape=jax.ShapeDtypeStruct((S, D_FF), MXU_DTYPE),
        compiler_params=_params())(up, up, w_conv, w_conv, b_conv, b_conv)


def conv_glu_bwd(name, up, w_conv, b_conv, dact):
    S = up.shape[0]

    def body(uo_ref, up_ref, wo_ref, wp_ref, bo_ref, bp_ref, da_ref, dup_ref, dw_ref, db_ref):
        j = pl.program_id(0)
        uo = uo_ref[...]
        own = _conv(uo, wo_ref[...], bo_ref[...])
        partner = _conv(up_ref[...], wp_ref[...], bp_ref[...])
        da = da_ref[...]
        sig = 1.0 / (1.0 + jnp.exp(-own))
        d_gate = da * partner * (sig * (1.0 + own * (1.0 - sig)))
        d_val = da * _silu(partner)
        du = jnp.where(j < CONV_NB, d_gate, d_val)
        w = wo_ref[...]
        dup_ref[...] = (w[2:3, :] * du + w[1:2, :] * _shift_up(du, 1) + w[0:1, :] * _shift_up(du, 2)).astype(dup_ref.dtype)
        dw_ref[...] = jnp.concatenate([_colsum(du * _shift_down(uo, 2)), _colsum(du * _shift_down(uo, 1)), _colsum(du * uo)], axis=0)
        db_ref[...] = _colsum(du)

    own = lambda r: pl.BlockSpec((r, CONV_TC), lambda j: (0, j))
    other = lambda r: pl.BlockSpec((r, CONV_TC), lambda j: (0, (j + CONV_NB) % (2 * CONV_NB)))
    return pl.pallas_call(
        body, name=name, grid=(2 * CONV_NB,),
        in_specs=[own(S), other(S), own(3), other(3), own(1), other(1), pl.BlockSpec((S, CONV_TC), lambda j: (0, j % CONV_NB))],
        out_specs=[own(S), own(3), own(1)],
        out_shape=[jax.ShapeDtypeStruct((S, 2 * D_FF), MXU_DTYPE), jax.ShapeDtypeStruct((3, 2 * D_FF), F32),
                   jax.ShapeDtypeStruct((1, 2 * D_FF), F32)],
        compiler_params=_params())(up, up, w_conv, w_conv, b_conv, b_conv, dact)


def adamw(name, w, parts, m, v, tr=None):
    R, C = w.shape
    tr = tr or R
    assert R % tr == 0
    c1 = 1.0 - ADAM_B1 ** ADAM_STEP
    c2 = 1.0 - ADAM_B2 ** ADAM_STEP
    np_ = len(parts)

    def body(*refs):
        w_ref, m_ref, v_ref = refs[0], refs[1 + np_], refs[2 + np_]
        go_ref, d_ref, mo_ref, vo_ref = refs[3 + np_:]
        terms = []
        for part, ref in zip(parts, refs[1:1 + np_], strict=True):
            terms += [ref[...]] if part.ndim == 2 else [ref[p] for p in range(part.shape[0])]
        g = terms[0].astype(F32)
        for term in terms[1:]:
            g = g + term.astype(F32)
        m2 = ADAM_B1 * m_ref[...] + (1.0 - ADAM_B1) * g
        v2 = ADAM_B2 * v_ref[...] + (1.0 - ADAM_B2) * (g * g)
        go_ref[...] = g
        mo_ref[...] = m2
        vo_ref[...] = v2
        d_ref[...] = -ADAM_LR * ((m2 / c1) / (jnp.sqrt(v2 / c2) + ADAM_EPS) + ADAM_WD * w_ref[...])

    blk = pl.BlockSpec((tr, C), lambda i: (i, 0))
    part_specs = [blk if p.ndim == 2 else pl.BlockSpec((p.shape[0], tr, C), lambda i: (0, i, 0)) for p in parts]
    return pl.pallas_call(
        body, name=name, grid=(R // tr,),
        in_specs=[blk] + part_specs + [blk, blk], out_specs=[blk] * 4,
        out_shape=[jax.ShapeDtypeStruct((R, C), F32)] * 4, compiler_params=_params())(w, *parts, m, v)


def _place():
    return lax.axis_index("x"), lax.axis_index("y"), lax.axis_index("c")


def all_gather(name, arrs, after=None):
    n = len(arrs)
    deps = [] if after is None else [after]

    def body(*refs):
        ins, outs = refs[:n], refs[n + len(deps):2 * n + len(deps)]
        send_sems, recv_sems, local_sems = refs[2 * n + len(deps):]
        x, y, c = _place()
        me, sibling = (x, y, c), (x, y, 1 - c)
        chips = [(1 - x, y), (x, 1 - y), (1 - x, 1 - y)]
        sends = []
        for t in range(n):
            out = outs[t]

            def slot(px, py, pc, out=out):
                return out.at[4 * px + 2 * py + pc]

            def copy(k, block, to, src=None, t=t, slot=slot):
                return pltpu.make_async_remote_copy(
                    src_ref=slot(*block) if src is None else src, dst_ref=slot(*block),
                    send_sem=send_sems.at[7 * t + k], recv_sem=recv_sems.at[7 * t + k],
                    device_id=to, device_id_type=MESH_ID)

            mine = pltpu.make_async_copy(ins[t], slot(*me), local_sems.at[t])
            mine.start()
            first = [copy(0, me, sibling, src=ins[t])]
            first += [copy(1 + j, me, (*chip, c), src=ins[t]) for j, chip in enumerate(chips)]
            for cp in first:
                cp.start()
            sends.append((mine, first, copy))
        for t in range(n):
            mine, first, copy = sends[t]
            passed = [copy(4 + j, (*chip, c), sibling) for j, chip in enumerate(chips)]
            for j, chip in enumerate(chips):
                copy(1 + j, (*chip, c), me).wait_recv()
                passed[j].start()
            copy(0, sibling, me).wait_recv()
            for j, chip in enumerate(chips):
                copy(4 + j, (*chip, 1 - c), me).wait_recv()
            for cp in first + passed:
                cp.wait_send()
            mine.wait()

    any_spec = pl.BlockSpec(memory_space=pl.ANY)
    res = pl.pallas_call(
        body, name=name, in_specs=[any_spec] * (n + len(deps)), out_specs=[any_spec] * n,
        out_shape=[jax.ShapeDtypeStruct((N_DEV,) + a.shape, a.dtype) for a in arrs],
        scratch_shapes=[pltpu.SemaphoreType.DMA((7 * n,)), pltpu.SemaphoreType.DMA((7 * n,)), pltpu.SemaphoreType.DMA((n,))],
        compiler_params=pltpu.CompilerParams(has_side_effects=True))(*arrs, *deps)
    return list(res)


HBM_SPEC = pl.BlockSpec(memory_space=pltpu.HBM)
SEM_SPEC = pl.BlockSpec(memory_space=pltpu.SEMAPHORE)
DATAFLOW = pltpu.SideEffectType.DATAFLOW_SIDE_EFFECTING


def _exchange_copies(srcs, lands, send_sems, recv_sems, gather):
    x, y, c = _place()
    me = 4 * x + 2 * y + c
    out = []
    for t, (src, land) in enumerate(zip(srcs, lands, strict=True)):
        for k in range(1, N_DEV):
            px, py, pc = x ^ (k >> 2), y ^ ((k >> 1) & 1), c ^ (k & 1)
            out.append(pltpu.make_async_remote_copy(
                src_ref=src if gather else src.at[4 * px + 2 * py + pc],
                dst_ref=land.at[me] if gather else land.at[k - 1],
                send_sem=send_sems.at[7 * t + k - 1], recv_sem=recv_sems.at[7 * t + k - 1],
                device_id=(px, py, pc), device_id_type=MESH_ID))
    return out


def exchange_start(name, arrs, gather, after=None):
    n = len(arrs)
    lands = [lax.empty(((N_DEV,) + a.shape) if gather else ((N_DEV - 1,) + a.shape[1:]), a.dtype) for a in arrs]
    deps = [] if after is None else [after]

    def body(*refs):
        srcs, land_refs = refs[:n], refs[n:2 * n]
        send_sems, recv_sems = refs[2 * n + len(deps)], refs[2 * n + len(deps) + 1]
        token = refs[-1]
        for cp in _exchange_copies(srcs, land_refs, send_sems, recv_sems, gather):
            cp.start()
        token[...] = jnp.zeros_like(token)

    hbm = lambda a: pltpu.HBM(a.shape, a.dtype)
    res = pl.pallas_call(
        body, name=name,
        out_shape=(pltpu.SemaphoreType.DMA((7 * n,)), pltpu.SemaphoreType.DMA((7 * n,)), *[hbm(a) for a in arrs],
                   *[hbm(l) for l in lands], jax.ShapeDtypeStruct((8, 128), F32)),
        in_specs=[HBM_SPEC] * (2 * n) + [pl.BlockSpec(memory_space=pl.ANY)] * len(deps),
        out_specs=(SEM_SPEC, SEM_SPEC, *[HBM_SPEC] * (2 * n), pl.BlockSpec(memory_space=pltpu.VMEM)),
        input_output_aliases={i: 2 + i for i in range(2 * n)},
        compiler_params=pltpu.CompilerParams(has_side_effects=DATAFLOW),
    )(*[pltpu.with_memory_space_constraint(a, pltpu.HBM) for a in arrs + lands], *deps)
    return res[0], res[1], list(res[2:2 + n]), list(res[2 + n:2 + 2 * n]), res[-1]


def exchange_wait(name, started, gather, after):
    send_sems, recv_sems, srcs, lands, _ = started
    n = len(srcs)

    def body(*refs):
        src_refs, land_refs = refs[:n], refs[n:2 * n]
        copies = _exchange_copies(src_refs, land_refs, refs[2 * n], refs[2 * n + 1], gather)
        for cp in copies:
            cp.wait_send()
        for cp in copies:
            cp.wait_recv()

    hbm = lambda a: pltpu.HBM(a.shape, a.dtype)
    res = pl.pallas_call(
        body, name=name, out_shape=tuple(hbm(a) for a in srcs + lands),
        in_specs=[HBM_SPEC] * (2 * n) + [SEM_SPEC, SEM_SPEC, pl.BlockSpec(memory_space=pl.ANY)],
        out_specs=tuple([HBM_SPEC] * (2 * n)), input_output_aliases={i: i for i in range(2 * n)},
        compiler_params=pltpu.CompilerParams(has_side_effects=DATAFLOW),
    )(*srcs, *lands, send_sems, recv_sems, after)
    return list(res[:n]), list(res[n:])


def _to_heads(t, h):
    s = t.shape[0]
    return t.reshape(s, h, -1).transpose(1, 0, 2)


def _from_heads(t):
    h, s, d = t.shape
    return t.transpose(1, 0, 2).reshape(s, h * d)


def _to_strided(t, dil):
    h, s, d = t.shape
    if dil == 1:
        return t
    return t.reshape(h, s // dil, dil, d).transpose(0, 2, 1, 3).reshape(h * dil, s // dil, d)


def _from_strided(t, dil):
    g, l, d = t.shape
    if dil == 1:
        return t
    h = g // dil
    return t.reshape(h, dil, l, d).transpose(0, 2, 1, 3).reshape(h, l * dil, d)


def _gather_cols(stack):
    p, k, n = stack.shape
    return stack.transpose(1, 0, 2).reshape(k, p * n)


def _scatter_cols(full):
    k, n = full.shape
    return full.reshape(k, N_DEV, n // N_DEV).transpose(1, 0, 2)


def _gather_rows(stack):
    p, r, n = stack.shape
    return stack.reshape(p * r, n)


def _scatter_rows(full):
    r, n = full.shape
    return full.reshape(N_DEV, r // N_DEV, n)


SMALL = (("loss", 1), ("b_ada", 6 * D_MODEL), ("g_mix_norm", D_MODEL), ("g_q_lat", Q_LORA), ("g_kv_lat", KV_LORA),
         ("g_mla_q_nope", NOPE), ("g_mla_q_pe", ROPE), ("g_mla_k_nope", NOPE), ("g_mla_k_pe", ROPE),
         ("g_dil_q", DIL_DIM), ("g_dil_k", DIL_DIM), ("g_ffn_norm", D_MODEL), ("b_conv", 2 * D_FF))
SMALL_ROWS = 16
SMALL_COLS = 1024


def _pack_small(values):
    parts = [values[name].reshape(-1).astype(F32) if name in values else jnp.zeros((n,), F32) for name, n in SMALL]
    flat = jnp.concatenate(parts)
    flat = jnp.pad(flat, (0, SMALL_ROWS * SMALL_COLS - flat.shape[0]))
    return flat.reshape(SMALL_ROWS, SMALL_COLS)


def _unpack_small(packed):
    flat = packed.reshape(-1)
    out, off = {}, 0
    for name, n in SMALL:
        out[name] = flat[off:off + n].reshape(1, n)
        off += n
    return out


def _local_step(x, pos, mod, target, w, fetch, emit):
    S = SEQ
    H = HEADS
    sh1, sc1, g1, sh2, sc2, g2 = [mod[:, i * D_MODEL:(i + 1) * D_MODEL] for i in range(6)]

    def inv_freq(d):
        f = jnp.power(ROPE_THETA, -2.0 * jnp.arange(d // 2, dtype=F32) / d)
        return jnp.concatenate([f, f]).reshape(1, d)

    def tables_fn(rows, params):
        (p,), (f32_, f64_) = rows, params
        return [jnp.cos(p * f32_), jnp.sin(p * f32_), jnp.cos(p * f64_), jnp.sin(p * f64_)], []

    cos32, sin32, cos64, sin64 = rowwise("rope_tables", tables_fn, [pos], [inv_freq(ROPE), inv_freq(DIL_DIM)],
                                         [(ROPE, F32), (ROPE, F32), (DIL_DIM, F32), (DIL_DIM, F32)])

    def ln1_fn(rows, params):
        (xv,), (g, sc, sh) = rows, params
        y, _, _ = _rms(xv, g)
        return [y * (1.0 + sc) + sh], []

    (h,) = rowwise("ln1_fwd", ln1_fn, [x], [w["g_mix_norm"], sc1, sh1], [(D_MODEL, MXU_DTYPE)])
    w_in = fetch("w_in", h)
    proj = matmul("proj_fwd", h, w_in, "nn", tm=512)
    q_lat = proj[:, :Q_LORA]
    kv_lat = proj[:, Q_LORA:Q_LORA + KV_LORA]
    k_pe = proj[:, Q_LORA + KV_LORA:Q_LORA + KV_LORA + ROPE]
    o0 = Q_LORA + KV_LORA + ROPE
    qd = _to_heads(proj[:, o0:o0 + DIL_WIDTH], H).reshape(H * S, DIL_DIM)
    kd = _to_heads(proj[:, o0 + DIL_WIDTH:o0 + 2 * DIL_WIDTH], H).reshape(H * S, DIL_DIM)
    vd = _to_heads(proj[:, o0 + 2 * DIL_WIDTH:], H).astype(MXU_DTYPE)

    def lat_fn(rows, params):
        (ql, kvl, kp, c32, s32), (gq, gkv, gkp) = rows, params
        return [_rms(ql, gq)[0], _rms(kvl, gkv)[0], _rope(_rms(kp, gkp)[0], c32, s32)], []

    qln, kvn, kper = rowwise("latent_fwd", lat_fn, [q_lat, kv_lat, k_pe, cos32, sin32],
                             [w["g_q_lat"], w["g_kv_lat"], w["g_mla_k_pe"]],
                             [(Q_LORA, MXU_DTYPE), (KV_LORA, MXU_DTYPE), (ROPE, MXU_DTYPE)])
    w_q_b, w_kv_b = fetch("w_q_b", qln), fetch("w_kv_b", kvn)
    q = matmul("q_fwd", qln, w_q_b, "nn", tm=1024)
    kv = matmul("kv_fwd", kvn, w_kv_b, "nn", tm=1024)
    qh = _to_heads(q, H)
    kvh = _to_heads(kv, H)
    q_nope = qh[..., :NOPE].reshape(H * S, NOPE)
    q_pe = qh[..., NOPE:].reshape(H * S, ROPE)
    k_nope = kvh[..., :NOPE].reshape(H * S, NOPE)
    v_mla = kvh[..., NOPE:].astype(MXU_DTYPE)

    def heads_fn(rows, params):
        (qn, qp, kn, qdv, kdv, c32, s32, c64, s64), (gqn, gqp, gkn, gdq, gdk) = rows, params
        return [_rms(qn, gqn)[0], _rope(_rms(qp, gqp)[0], c32, s32), _rms(kn, gkn)[0],
                _rope(_rms(qdv, gdq)[0], c64, s64), _rope(_rms(kdv, gdk)[0], c64, s64)], []

    head_gains = [w["g_mla_q_nope"], w["g_mla_q_pe"], w["g_mla_k_nope"], w["g_dil_q"], w["g_dil_k"]]
    q_nope_n, q_pe_r, k_nope_n, qd_r, kd_r = rowwise(
        "heads_fwd", heads_fn, [q_nope, q_pe, k_nope, qd, kd, cos32, sin32, cos64, sin64], head_gains,
        [(NOPE, MXU_DTYPE), (ROPE, MXU_DTYPE), (NOPE, MXU_DTYPE), (DIL_DIM, MXU_DTYPE), (DIL_DIM, MXU_DTYPE)], tm=S)
    q_mla = jnp.concatenate([q_nope_n.reshape(H, S, NOPE), q_pe_r.reshape(H, S, ROPE)], axis=-1)
    k_mla = jnp.concatenate([k_nope_n.reshape(H, S, NOPE), jnp.broadcast_to(kper[None], (H, S, ROPE))], axis=-1)
    mla_scale = (NOPE + ROPE) ** -0.5
    o_mla, lse_mla = attn_fwd("mla_fwd", q_mla, k_mla, v_mla, mla_scale)

    qd_r = qd_r.reshape(H, S, DIL_DIM)
    kd_r = kd_r.reshape(H, S, DIL_DIM)
    band = []
    for dil in DILATIONS:
        L = S // dil
        gb = {2048: 1, 512: 4, 128: 16}[L]
        qs, ks, vs = _to_strided(qd_r, dil), _to_strided(kd_r, dil), _to_strided(vd, dil)
        o_s, lse_s = band_fwd(f"band{dil}_fwd", qs, ks, vs, gb)
        band.append((qs, ks, vs, o_s, lse_s, gb))
    o_nat = [_from_strided(b[3], dil).reshape(H * S, DIL_DIM) for b, dil in zip(band, DILATIONS)]
    lse_nat = [_from_strided(b[4], dil).reshape(H * S, 1) for b, dil in zip(band, DILATIONS)]

    def mix_weights(ls):
        m = jnp.maximum(jnp.maximum(ls[0], ls[1]), ls[2])
        e = [jnp.exp(l - m) for l in ls]
        den = e[0] + e[1] + e[2]
        return [ei / den for ei in e]

    def combine_fn(rows, params):
        os_, ls = rows[:3], rows[3:]
        wt = mix_weights(ls)
        return [wt[0] * os_[0] + wt[1] * os_[1] + wt[2] * os_[2]], []

    (o_dil,) = rowwise("dil_combine_fwd", combine_fn, o_nat + lse_nat, [], [(DIL_DIM, F32)], tm=S)
    o_cat = jnp.concatenate([_from_heads(o_mla), _from_heads(o_dil.reshape(H, S, DIL_DIM))], axis=-1).astype(MXU_DTYPE)
    w_o = fetch("w_o", o_cat)
    mix = matmul("mix_fwd", o_cat, w_o, "nn", tm=512)

    def mid_fn(rows, params):
        (xv, mx), (gate1, g, sc, sh) = rows, params
        x1 = xv + gate1 * mx
        y, _, _ = _rms(x1, g)
        return [x1, y * (1.0 + sc) + sh], []

    x1, h2 = rowwise("mid_fwd", mid_fn, [x, mix], [g1, w["g_ffn_norm"], sc2, sh2], [(D_MODEL, F32), (D_MODEL, MXU_DTYPE)])
    w_up, w_conv, w_down = fetch("w_up", h2), fetch("w_conv", h2), fetch("w_down", h2)
    up = matmul("up_fwd", h2, w_up, "nn", tm=512, tn=1408)
    act = conv_glu_fwd("conv_fwd", up, w_conv, w["b_conv"])
    dn = matmul("down_fwd", act, w_down, "nn", tm=512)

    def final_fn(rows, params):
        (x1v, dnv, tgt), (gate2,) = rows, params
        r = x1v + gate2 * dnv - tgt
        dy = r * (1.0 / D_MODEL)
        loss = jnp.sum(_colsum(r * r), axis=-1, keepdims=True) * (0.5 / D_MODEL)
        return [dy, gate2 * dy], [loss, _colsum(dy * dnv)]

    dy, d_dn, loss, dg2 = rowwise("loss_head", final_fn, [x1, dn, target], [g2], [(D_MODEL, F32), (D_MODEL, MXU_DTYPE)],
                                  [1, D_MODEL])
    emit("w_down", matmul("down_wgrad", act, d_dn, "tn", tm=1408, out_dtype=MXU_DTYPE))
    dact = matmul("down_dgrad", d_dn, w_down, "nt", tm=512)
    dup, g_w_conv, gs_b_conv = conv_glu_bwd("conv_bwd", up, w_conv, w["b_conv"], dact)
    emit("w_conv", g_w_conv)
    sent = emit("w_up", matmul("up_wgrad", h2, dup, "tn", tn=1408, out_dtype=MXU_DTYPE))
    dh2 = matmul("up_dgrad", dup, w_up, "nt", tm=512, tk=2816, dep=sent)

    def mid_bwd_fn(rows, params):
        (dh2v, dyv, x1v, mx), (gate1, g, sc) = rows, params
        yn, n, rstd = _rms(x1v, g)
        dx_n, dg = _rms_bwd(dh2v * (1.0 + sc), n, rstd, g)
        dx1 = dyv + dx_n
        return [dx1, gate1 * dx1], [dg, _colsum(dh2v * yn), _colsum(dh2v), _colsum(dx1 * mx)]

    dx1, dmix, dg_ffn, dsc2, dsh2, dg1 = rowwise(
        "mid_bwd", mid_bwd_fn, [dh2, dy, x1, mix], [g1, w["g_ffn_norm"], sc2], [(D_MODEL, F32), (D_MODEL, MXU_DTYPE)],
        [D_MODEL] * 4)

    sent = emit("w_o", matmul("mix_wgrad", o_cat, dmix, "tn", out_dtype=MXU_DTYPE))
    do_cat = matmul("mix_dgrad", dmix, w_o, "nt", tm=512, dep=sent)
    do_mla = _to_heads(do_cat[:, :HEADS * V_DIM], H)
    do_dil = _to_heads(do_cat[:, HEADS * V_DIM:], H).reshape(H * S, DIL_DIM)

    def combine_bwd_fn(rows, params):
        os_, ls, dout = rows[:3], rows[3:6], rows[6]
        wt = mix_weights(ls)
        dw = [jnp.sum(dout * o, axis=-1, keepdims=True) for o in os_]
        mean_dw = wt[0] * dw[0] + wt[1] * dw[1] + wt[2] * dw[2]
        return [wt[i] * dout for i in range(3)] + [wt[i] * (dw[i] - mean_dw) for i in range(3)], []

    cb = rowwise("dil_combine_bwd", combine_bwd_fn, o_nat + lse_nat + [do_dil], [],
                 [(DIL_DIM, F32)] * 3 + [(1, F32)] * 3, tm=S)
    dqd_parts, dkd_parts, dvd_parts = [], [], []
    for i, dil in enumerate(DILATIONS):
        qs, ks, vs, o_s, lse_s, gb = band[i]
        do_s = _to_strided(cb[i].reshape(H, S, DIL_DIM), dil)
        dlse_s = _to_strided(cb[3 + i].reshape(H, S, 1), dil)
        dq_s, dk_s, dv_s = band_bwd(f"band{dil}_bwd", qs, ks, vs, o_s, lse_s, do_s, dlse_s, gb)
        dqd_parts.append(_from_strided(dq_s, dil).reshape(H * S, DIL_DIM))
        dkd_parts.append(_from_strided(dk_s, dil).reshape(H * S, DIL_DIM))
        dvd_parts.append(_from_strided(dv_s, dil).reshape(H * S, DIL_DIM))

    dq_mla, dk_mla, dv_mla, dkper = attn_bwd("mla_bwd", q_mla, k_mla, v_mla, o_mla, do_mla, lse_mla, mla_scale)
    dq_nope_n = dq_mla[..., :NOPE].reshape(H * S, NOPE)
    dq_pe_r = dq_mla[..., NOPE:].reshape(H * S, ROPE)
    dk_nope_n = dk_mla[..., :NOPE].reshape(H * S, NOPE)

    def heads_bwd_fn(rows, params):
        dqn, dqp, dkn = rows[0:3]
        dqd_ = rows[3] + rows[4] + rows[5]
        dkd_ = rows[6] + rows[7] + rows[8]
        dvd_ = rows[9] + rows[10] + rows[11]
        qn, qp, kn, qdv, kdv, c32, s32, c64, s64 = rows[12:]
        gqn, gqp, gkn, gdq, gdk = params
        r1 = _norm_bwd(dqn, qn, gqn)
        r2 = _norm_rope_bwd(dqp, qp, gqp, c32, s32)
        r3 = _norm_bwd(dkn, kn, gkn)
        r4 = _norm_rope_bwd(dqd_, qdv, gdq, c64, s64)
        r5 = _norm_rope_bwd(dkd_, kdv, gdk, c64, s64)
        rs = [r1, r2, r3, r4, r5]
        return [r[0] for r in rs] + [dvd_], [r[1] for r in rs]

    hb = rowwise("heads_bwd", heads_bwd_fn,
                 [dq_nope_n, dq_pe_r, dk_nope_n] + dqd_parts + dkd_parts + dvd_parts
                 + [q_nope, q_pe, k_nope, qd, kd, cos32, sin32, cos64, sin64],
                 head_gains, [(NOPE, F32), (ROPE, F32), (NOPE, F32), (DIL_DIM, F32), (DIL_DIM, F32), (DIL_DIM, MXU_DTYPE)],
                 [NOPE, ROPE, NOPE, DIL_DIM, DIL_DIM], tm=S)
    dq_nope, dq_pe, dk_nope, dqd, dkd, dvd = hb[:6]
    dg_q_nope, dg_q_pe, dg_k_nope, dg_dil_q, dg_dil_k = hb[6:]
    dq = _from_heads(jnp.concatenate([dq_nope.reshape(H, S, NOPE), dq_pe.reshape(H, S, ROPE)], axis=-1)).astype(MXU_DTYPE)
    dkv = _from_heads(jnp.concatenate([dk_nope.reshape(H, S, NOPE), dv_mla], axis=-1)).astype(MXU_DTYPE)
    emit("w_q_b", matmul("q_wgrad", qln, dq, "tn", out_dtype=MXU_DTYPE))
    emit("w_kv_b", matmul("kv_wgrad", kvn, dkv, "tn", out_dtype=MXU_DTYPE))
    dqln = matmul("q_dgrad", dq, w_q_b, "nt", tm=1024)
    dkvn = matmul("kv_dgrad", dkv, w_kv_b, "nt", tm=1024)

    def lat_bwd_fn(rows, params):
        (dql, dkvl, dkp, ql, kvl, kp, c32, s32), (gq, gkv, gkp) = rows, params
        r1 = _norm_bwd(dql, ql, gq)
        r2 = _norm_bwd(dkvl, kvl, gkv)
        r3 = _norm_rope_bwd(dkp, kp, gkp, c32, s32)
        return [r1[0], r2[0], r3[0]], [r1[1], r2[1], r3[1]]

    dq_lat, dkv_lat, dk_pe, dg_q_lat, dg_kv_lat, dg_k_pe = rowwise(
        "latent_bwd", lat_bwd_fn, [dqln, dkvn, dkper, q_lat, kv_lat, k_pe, cos32, sin32],
        [w["g_q_lat"], w["g_kv_lat"], w["g_mla_k_pe"]],
        [(Q_LORA, MXU_DTYPE), (KV_LORA, MXU_DTYPE), (ROPE, MXU_DTYPE)], [Q_LORA, KV_LORA, ROPE])
    dproj = jnp.concatenate([dq_lat, dkv_lat, dk_pe,
                             _from_heads(dqd.reshape(H, S, DIL_DIM)).astype(MXU_DTYPE),
                             _from_heads(dkd.reshape(H, S, DIL_DIM)).astype(MXU_DTYPE),
                             _from_heads(dvd.reshape(H, S, DIL_DIM))], axis=-1)
    sent = emit("w_in", matmul("proj_wgrad", h, dproj, "tn", tm=256, out_dtype=MXU_DTYPE))
    dh = matmul("proj_dgrad", dproj, w_in, "nt", tm=512, dep=sent)

    def ln1_bwd_fn(rows, params):
        (dhv, dres, xv), (g, sc) = rows, params
        yn, n, rstd = _rms(xv, g)
        dx_n, dg = _rms_bwd(dhv * (1.0 + sc), n, rstd, g)
        return [dres + dx_n], [dg, _colsum(dhv * yn), _colsum(dhv)]

    grad_x, dg_mix, dsc1, dsh1 = rowwise("ln1_bwd", ln1_bwd_fn, [dh, dx1, x], [w["g_mix_norm"], sc1], [(D_MODEL, F32)],
                                         [D_MODEL] * 3)
    dmod = jnp.concatenate([dsh1, dsc1, dg1, dsh2, dsc2, dg2], axis=-1)
    small = {"loss": loss, "b_ada": dmod, "g_mix_norm": dg_mix, "g_q_lat": dg_q_lat, "g_kv_lat": dg_kv_lat,
             "g_mla_q_nope": dg_q_nope, "g_mla_q_pe": dg_q_pe, "g_mla_k_nope": dg_k_nope, "g_mla_k_pe": dg_k_pe,
             "g_dil_q": dg_dil_q, "g_dil_k": dg_dil_k, "g_ffn_norm": dg_ffn, "b_conv": gs_b_conv}
    return grad_x, small


COL_SHARDED = ("w_in", "w_q_b", "w_kv_b", "w_up", "w_conv")
ROW_SHARDED = ("w_o", "w_down")
ADAM_TILE = {"w_ada": 256, "w_in": 256, "w_up": 256, "w_down": 176}
GATHER_GROUPS = (("w_in", "w_q_b", "w_kv_b"), ("w_o", "w_up", "w_conv", "w_down"))
SCATTER_GROUPS = (("w_down", "w_conv", "w_up"), ("w_o",), ("w_q_b", "w_kv_b", "w_in"))
OUT_WEIGHTS = ("w_ada", "b_ada", "g_mix_norm", "w_in", "g_q_lat", "w_q_b", "g_kv_lat", "w_kv_b", "g_mla_q_nope", "g_mla_q_pe",
               "g_mla_k_nope", "g_mla_k_pe", "g_dil_q", "g_dil_k", "w_o", "g_ffn_norm", "w_up", "w_conv", "b_conv", "w_down")


def kernel(x, c, positions, w_ada, b_ada, g_mix_norm, w_in, g_q_lat, w_q_b, g_kv_lat, w_kv_b, g_mla_q_nope, g_mla_q_pe, g_mla_k_nope, g_mla_k_pe, g_dil_q, g_dil_k, w_o, g_ffn_norm, w_up, w_conv, b_conv, w_down, loss_target, m_w_ada, m_b_ada, m_g_mix_norm, m_w_in, m_g_q_lat, m_w_q_b, m_g_kv_lat, m_w_kv_b, m_g_mla_q_nope, m_g_mla_q_pe, m_g_mla_k_nope, m_g_mla_k_pe, m_g_dil_q, m_g_dil_k, m_w_o, m_g_ffn_norm, m_w_up, m_w_conv, m_b_conv, m_w_down, v_w_ada, v_b_ada, v_g_mix_norm, v_w_in, v_g_q_lat, v_w_q_b, v_g_kv_lat, v_w_kv_b, v_g_mla_q_nope, v_g_mla_q_pe, v_g_mla_k_nope, v_g_mla_k_pe, v_g_dil_q, v_g_dil_k, v_w_o, v_g_ffn_norm, v_w_up, v_w_conv, v_b_conv, v_w_down):
    args = dict(locals())
    xi, yi, ci = _place()
    me = 4 * xi + 2 * yi + ci
    shard = {n: args[n][0] for n in COL_SHARDED + ROW_SHARDED + ("w_ada",)}
    small_w = {n: args[n] for n, _ in SMALL if n != "loss"}

    (c_all,) = all_gather("gather_c", [c])
    (sc_all,) = rowwise("silu_c", lambda rows, params: ([_silu(rows[0])], []), [c_all.reshape(N_DEV, D_MODEL)], [],
                        [(D_MODEL, MXU_DTYPE)])
    mod_part = matmul("ada_fwd", sc_all, shard["w_ada"], "nn")
    (mod_all,) = all_gather("gather_mod", [mod_part])

    payload = {n: shard[n] if n == "w_conv" else shard[n].astype(MXU_DTYPE) for n in COL_SHARDED + ROW_SHARDED}
    gathers, after_start = [], mod_all
    for i, grp in enumerate(GATHER_GROUPS):
        gathers.append(exchange_start(f"gather{i}_start", [payload[n] for n in grp], gather=True, after=after_start))
        after_start = gathers[-1][-1]
    full = {}

    def fetch(name, after):
        if name not in full:
            (i, grp), = [(i, grp) for i, grp in enumerate(GATHER_GROUPS) if name in grp]
            srcs, lands = exchange_wait(f"gather{i}_wait", gathers[i], True, after)
            for n, src, land in zip(grp, srcs, lands, strict=True):
                stack = lax.dynamic_update_index_in_dim(land, src, me, 0)
                full[n] = _gather_cols(stack) if n in COL_SHARDED else _gather_rows(stack)
        return full[name]

    mod_row = lax.dynamic_index_in_dim(mod_all, me, axis=1, keepdims=False).reshape(1, 6 * D_MODEL)
    (mod,) = rowwise("ada_bias", lambda rows, params: ([rows[0] + rows[1]], []), [mod_row, b_ada], [], [(6 * D_MODEL, F32)],
                     dep=after_start)

    own, pending, scatters = {}, {}, {}

    def emit(name, grad):
        parts = _scatter_cols(grad) if name in COL_SHARDED else _scatter_rows(grad)
        own[name] = lax.dynamic_index_in_dim(parts, me, 0, keepdims=False)
        pending[name] = parts
        for i, grp in enumerate(SCATTER_GROUPS):
            if name == grp[-1]:
                scatters[i] = exchange_start(f"scatter{i}_start", [pending[n] for n in grp], gather=False)
                return scatters[i][-1]
        return None

    pos = positions.reshape(SEQ, 1).astype(F32)
    grad_x, small = _local_step(x[0], pos, mod, loss_target[0], small_w, fetch, emit)

    res, done = {}, grad_x
    for i, grp in enumerate(SCATTER_GROUPS):
        _, lands = exchange_wait(f"scatter{i}_wait", scatters[i], False, done)
        for n, land in zip(grp, lands, strict=True):
            res[n] = adamw(f"adamw_{n}", shard[n], [own[n], land], args["m_" + n][0], args["v_" + n][0], ADAM_TILE.get(n))
            done = res[n][0]
    (small_all,) = all_gather("gather_small", [_pack_small(small)], after=done)
    packed = adamw("adamw_small", _pack_small(small_w), [small_all], _pack_small({n: args["m_" + n] for n in small_w}),
                   _pack_small({n: args["v_" + n] for n in small_w}))
    small_res = [_unpack_small(p) for p in packed]
    dmod_all = small_all.reshape(N_DEV, -1)[:, 1:1 + 6 * D_MODEL]
    dmod_mine = lax.dynamic_slice_in_dim(dmod_all, me * (6 * D_MODEL // N_DEV), 6 * D_MODEL // N_DEV, axis=1)
    g_w_ada = matmul("ada_wgrad", sc_all, dmod_mine, "tn")
    res["w_ada"] = adamw("adamw_w_ada", shard["w_ada"], [g_w_ada], m_w_ada[0], v_w_ada[0], ADAM_TILE["w_ada"])

    def leaf(kind, n):
        if n in res:
            return res[n][kind][None]
        return small_res[kind][n]

    loss = small_res[0]["loss"].reshape(())
    return (loss, grad_x[None], *[leaf(k, n) for k in range(4) for n in OUT_WEIGHTS])
```

```python
import jax
import jax.numpy as jnp
from jax import lax
from jax.experimental import pallas as pl
from jax.experimental.pallas import tpu as pltpu

F32 = jnp.float32
MXU_DTYPE = jnp.bfloat16

N_DEV = 8
D_MODEL = 1024
SEQ = 2048
HEADS = 8
NOPE = 64
ROPE = 32
Q_LORA = 512
KV_LORA = 256
DIL_DIM = 64
DIL_WIDTH = HEADS * DIL_DIM
DILATIONS = (1, 4, 16)
SPAN = 128
D_FF = 2816
LANES = 128
ROPE_THETA = 10000.0
EPS = 1e-6
NEG_INF = -1e30
ADAM_LR, ADAM_B1, ADAM_B2, ADAM_EPS, ADAM_WD, ADAM_STEP = 0.001, 0.9, 0.999, 1e-08, 0.01, 10
VMEM_LIMIT = 56 * 1024 * 1024
MESH_ID = pl.DeviceIdType.MESH

P_QLAT, P_KVLAT, P_KPE, P_QD, P_KD, P_VD, P_END = 0, 512, 768, 896, 1408, 1920, 2432
KPE_LO = 64
MIX_IN = HEADS * LANES + DIL_WIDTH


def _params(**kw):
    return pltpu.CompilerParams(vmem_limit_bytes=VMEM_LIMIT, **kw)


def rowwise(name, fn, rows, params, out_rows, out_accs=(), tm=512, dep=None):
    deps = [] if dep is None else [dep]
    rows = [r if isinstance(r, tuple) else (r, r.shape[1], 0) for r in rows]
    R = rows[0][0].shape[0]
    tm = min(tm, R)
    steps = R // tm
    assert steps * tm == R
    in_specs = []
    for a, width, cb in rows:
        ri = a.shape[0]
        per = ri // tm
        assert per * tm == ri
        if ri == R:
            in_specs.append(pl.BlockSpec((tm, width), lambda i, cb=cb: (i, cb)))
        else:
            in_specs.append(pl.BlockSpec((tm, width), lambda i, per=per, cb=cb: (i % per, cb)))
    for p in params:
        in_specs.append(pl.BlockSpec(p.shape, lambda i: (0,) * p.ndim))
    in_specs += [pl.BlockSpec(memory_space=pl.ANY)] * len(deps)
    out_shape = [jax.ShapeDtypeStruct((R, d), dt) for d, dt in out_rows]
    out_specs = [pl.BlockSpec((tm, d), lambda i: (i, 0)) for d, _ in out_rows]
    out_shape += [jax.ShapeDtypeStruct((1, n), F32) for n in out_accs]
    out_specs += [pl.BlockSpec((1, n), lambda i: (0, 0)) for n in out_accs]
    nr, npar, no, na = len(rows), len(params), len(out_rows), len(out_accs)

    def body(*refs):
        rvals = [r[...] for r in refs[:nr]]
        pvals = [r[...] for r in refs[nr:nr + npar]]
        outs, accs = fn(rvals, pvals)
        first_out = nr + npar + len(deps)
        for ref, v in zip(refs[first_out:first_out + no], outs, strict=True):
            ref[...] = v.astype(ref.dtype)
        if na:
            acc_refs = refs[first_out + no:]
            i = pl.program_id(0)

            @pl.when(i == 0)
            def _():
                for ref, v in zip(acc_refs, accs, strict=True):
                    ref[...] = v

            @pl.when(i > 0)
            def _():
                for ref, v in zip(acc_refs, accs, strict=True):
                    ref[...] += v

    res = pl.pallas_call(body, name=name, grid=(steps,), in_specs=in_specs, out_specs=out_specs,
                         out_shape=out_shape, compiler_params=_params())(*[r[0] for r in rows], *params, *deps)
    return list(res)


_DIMS = {"nn": ((1,), (0,)), "nt": ((1,), (1,)), "tn": ((0,), (0,))}


def _dot(a, b, mode="nn"):
    return lax.dot_general(a.astype(MXU_DTYPE), b.astype(MXU_DTYPE), (_DIMS[mode], ((), ())),
                           preferred_element_type=F32)


def matmul(name, a, b, mode, tm=None, tn=None, tk=None, out_dtype=F32, dep=None):
    if mode == "tn":
        K, M = a.shape
    else:
        M, K = a.shape
    N = b.shape[0] if mode == "nt" else b.shape[1]
    tm, tn, tk = tm or M, tn or N, tk or K
    nm, nn, nk = M // tm, N // tn, K // tk
    assert nm * tm == M and nn * tn == N and nk * tk == K
    a_spec = pl.BlockSpec((tk, tm), lambda i, j, k: (k, i)) if mode == "tn" else pl.BlockSpec((tm, tk), lambda i, j, k: (i, k))
    b_spec = pl.BlockSpec((tn, tk), lambda i, j, k: (j, k)) if mode == "nt" else pl.BlockSpec((tk, tn), lambda i, j, k: (k, j))
    deps = [] if dep is None else [dep]

    def body(a_ref, b_ref, *rest):
        o_ref, scratch = rest[len(deps)], rest[len(deps) + 1:]
        p = _dot(a_ref[...], b_ref[...], mode)
        if nk == 1:
            o_ref[...] = p.astype(o_ref.dtype)
        else:
            acc = scratch[0]
            k = pl.program_id(2)

            @pl.when(k == 0)
            def _():
                acc[...] = p

            @pl.when(k > 0)
            def _():
                acc[...] += p

            @pl.when(k == nk - 1)
            def _():
                o_ref[...] = acc[...].astype(o_ref.dtype)

    return pl.pallas_call(
        body, name=name, grid=(nm, nn, nk), in_specs=[a_spec, b_spec] + [pl.BlockSpec(memory_space=pl.ANY)] * len(deps),
        out_specs=pl.BlockSpec((tm, tn), lambda i, j, k: (i, j)),
        out_shape=jax.ShapeDtypeStruct((M, N), out_dtype),
        scratch_shapes=[pltpu.VMEM((tm, tn), F32)] if nk > 1 else [],
        compiler_params=_params())(a, b, *deps)


def _rms(x, g):
    rstd = lax.rsqrt(jnp.mean(x * x, axis=-1, keepdims=True) + EPS)
    n = x * rstd
    return n * g, n, rstd


def _rms_bwd(dy, n, rstd, g):
    dg = jnp.sum(dy * n, axis=0, keepdims=True)
    dn = dy * g
    dx = rstd * (dn - n * jnp.mean(dn * n, axis=-1, keepdims=True))
    return dx, dg


def _norm_bwd(dy, x, g):
    _, n, rstd = _rms(x, g)
    return _rms_bwd(dy, n, rstd, g)


def _colsum(v):
    return jnp.sum(v, axis=0, keepdims=True)


def _silu(x):
    return x * (1.0 / (1.0 + jnp.exp(-x)))


def _lane(shape):
    return lax.broadcasted_iota(jnp.int32, shape, 1)


def _group_mean(v, groups):
    lane = _lane(v.shape)
    out = jnp.zeros_like(v)
    for lo, hi in groups:
        m = (lane >= lo) & (lane < hi)
        out = jnp.where(m, jnp.sum(jnp.where(m, v, 0.0), axis=-1, keepdims=True) * (1.0 / (hi - lo)), out)
    return out


def _in_groups(shape, groups):
    lane = _lane(shape)
    m = jnp.zeros(shape, jnp.bool_)
    for lo, hi in groups:
        m = m | ((lane >= lo) & (lane < hi))
    return m


def _grms(x, g, groups):
    rstd = lax.rsqrt(_group_mean(x * x, groups) + EPS)
    n = jnp.where(_in_groups(x.shape, groups), x * rstd, 0.0)
    return n * g, n, rstd


def _grms_bwd(dy, n, rstd, g, groups):
    dn = dy * g
    return rstd * (dn - n * _group_mean(dn * n, groups)), _colsum(dy * n)


def _rot(x, half, transpose=False):
    first = (_lane(x.shape) % (2 * half)) < half
    up = pltpu.roll(x, LANES - half, axis=1)
    down = pltpu.roll(x, half, axis=1)
    return jnp.where(first, up, -down) if transpose else jnp.where(first, -up, down)


def _rope(x, cos, sin, half):
    return x * cos + _rot(x, half) * sin


def _rope_bwd(dy, cos, sin, half):
    return dy * cos + _rot(dy * sin, half, transpose=True)


def _chunks(x):
    return [x[:, i:i + LANES] for i in range(0, x.shape[1], LANES)]


Q_GROUPS = ((0, NOPE), (NOPE, NOPE + ROPE))
K_GROUPS = ((0, NOPE),)
KPE_GROUPS = ((KPE_LO, KPE_LO + ROPE),)
DIL_GROUPS = ((0, DIL_DIM), (DIL_DIM, 2 * DIL_DIM))


def _col(width, rows=SEQ):
    return pl.BlockSpec((rows, width), lambda h: (0, h))


def mla_fwd(name, q, k, v, scale, tq=512):
    S = q.shape[0]

    def body(q_ref, k_ref, v_ref, o_ref, lse_ref):
        for i in range(S // tq):
            kext = (i + 1) * tq
            blk = slice(i * tq, kext)
            s = _dot(q_ref[blk, :], k_ref[:kext, :], "nt") * scale
            row = lax.broadcasted_iota(jnp.int32, s.shape, 0) + i * tq
            col = lax.broadcasted_iota(jnp.int32, s.shape, 1)
            s = jnp.where(col <= row, s, NEG_INF)
            m = jnp.max(s, axis=-1, keepdims=True)
            e = jnp.exp(s - m)
            l = jnp.sum(e, axis=-1, keepdims=True)
            o_ref[blk, :] = _dot(e / l, v_ref[:kext, :])
            lse_ref[0, blk, :] = m + jnp.log(l)

    return pl.pallas_call(
        body, name=name, grid=(HEADS,), in_specs=[_col(LANES)] * 3,
        out_specs=[_col(LANES), pl.BlockSpec((1, S, 1), lambda h: (h, 0, 0))],
        out_shape=[jax.ShapeDtypeStruct((S, MIX_IN), F32), jax.ShapeDtypeStruct((HEADS, S, 1), F32)],
        compiler_params=_params())(q, k, v)


def mla_bwd(name, q, k, v, o, do, lse, scale, tq=512):
    S = q.shape[0]

    def body(q_ref, k_ref, v_ref, o_ref, do_ref, lse_ref, dq_ref, dkv_ref, dkpe_ref, dk_acc, dv_acc):
        dk_acc[...] = jnp.zeros_like(dk_acc)
        dv_acc[...] = jnp.zeros_like(dv_acc)
        for i in range(S // tq):
            kext = (i + 1) * tq
            blk = slice(i * tq, kext)
            qi, kk, vv = q_ref[blk, :], k_ref[:kext, :], v_ref[:kext, :]
            doi = do_ref[blk, :]
            s = _dot(qi, kk, "nt") * scale
            row = lax.broadcasted_iota(jnp.int32, s.shape, 0) + i * tq
            col = lax.broadcasted_iota(jnp.int32, s.shape, 1)
            p = jnp.where(col <= row, jnp.exp(s - lse_ref[0, blk, :]), 0.0)
            dp = _dot(doi, vv, "nt")
            delta = jnp.sum(doi * o_ref[blk, :], axis=-1, keepdims=True)
            ds = p * (dp - delta) * scale
            dq_ref[blk, :] = _dot(ds, kk)
            dk_acc[:kext, :] += _dot(ds, qi, "tn")
            dv_acc[:kext, :] += _dot(p, doi, "tn")
        dk = dk_acc[...]
        lane = _lane(dk.shape)
        dkv_ref[...] = jnp.where(lane < NOPE, dk, 0.0) + dv_acc[...]
        dkpe = jnp.where((lane >= KPE_LO) & (lane < KPE_LO + ROPE), dk, 0.0)
        h = pl.program_id(0)

        @pl.when(h == 0)
        def _():
            dkpe_ref[...] = dkpe

        @pl.when(h > 0)
        def _():
            dkpe_ref[...] += dkpe

    return pl.pallas_call(
        body, name=name, grid=(HEADS,),
        in_specs=[_col(LANES)] * 5 + [pl.BlockSpec((1, S, 1), lambda h: (h, 0, 0))],
        out_specs=[_col(LANES), _col(LANES), pl.BlockSpec((S, LANES), lambda h: (0, 0))],
        out_shape=[jax.ShapeDtypeStruct((S, HEADS * LANES), F32), jax.ShapeDtypeStruct((S, HEADS * LANES), F32),
                   jax.ShapeDtypeStruct((S, LANES), F32)],
        scratch_shapes=[pltpu.VMEM((S, LANES), F32), pltpu.VMEM((S, LANES), F32)],
        compiler_params=_params())(q, k, v, o, do, lse)


def _band_blocks(L, tq):
    return [(i * tq, (i + 1) * tq, max(0, i * tq - SPAN)) for i in range(L // tq)]


def _band_mask(q0, q1, k0):
    shape = (q1 - q0, q1 - k0)
    dist = (lax.broadcasted_iota(jnp.int32, shape, 0) + q0) - (lax.broadcasted_iota(jnp.int32, shape, 1) + k0)
    return (dist >= 0) & (dist <= SPAN)


def _class_view(a, dil):
    return a.reshape(a.shape[0] // dil, dil * a.shape[1])


def _class_spec(a, dil, width, col0=0):
    L, C = a.shape[0] // dil, a.shape[1]
    per = DIL_WIDTH // width
    return pl.BlockSpec((L, width), lambda j: (0, (j // per) * (C // width) + col0 // width + j % per))


BAND_PAIRS = {1: 1, 4: 2, 16: 4}


def band_fwd(name, q, k, v, dil):
    S = q.shape[0]
    L = S // dil
    pairs = BAND_PAIRS[dil]
    width = LANES * pairs
    tq = min(L, 512)
    scale = DIL_DIM ** -0.5

    def body(q_ref, k_ref, v_ref, o_ref, lse_ref):
        for pi in range(pairs):
            cs = slice(LANES * pi, LANES * (pi + 1))
            for q0, q1, k0 in _band_blocks(L, tq):
                qb, kb, vb = q_ref[q0:q1, cs], k_ref[k0:q1, cs], v_ref[k0:q1, cs]
                lo = _lane(qb.shape) < DIL_DIM
                band = _band_mask(q0, q1, k0)
                res = []
                for m in (lo, ~lo):
                    s = _dot(jnp.where(m, qb, jnp.zeros_like(qb)), kb, "nt") * scale
                    s = jnp.where(band, s, NEG_INF)
                    mx = jnp.max(s, axis=-1, keepdims=True)
                    e = jnp.exp(s - mx)
                    l = jnp.sum(e, axis=-1, keepdims=True)
                    res.append((_dot(e / l, vb), mx + jnp.log(l)))
                o_ref[q0:q1, cs] = jnp.where(lo, res[0][0], res[1][0])
                lse_ref[q0:q1, cs] = jnp.where(lo, res[0][1], res[1][1])

    spec = _class_spec(q, dil, width)
    o, lse = pl.pallas_call(
        body, name=name, grid=(dil * DIL_WIDTH // width,), in_specs=[spec] * 3, out_specs=[spec] * 2,
        out_shape=[jax.ShapeDtypeStruct((L, dil * DIL_WIDTH), F32)] * 2,
        compiler_params=_params())(*[_class_view(a, dil) for a in (q, k, v)])
    return o.reshape(S, DIL_WIDTH), lse.reshape(S, DIL_WIDTH)


def band_bwd(name, q, k, v, lse, lse_mix, o_cat, do_cat, dil):
    S = q.shape[0]
    L = S // dil
    pairs = BAND_PAIRS[dil]
    width = LANES * pairs
    tq = min(L, 512)
    scale = DIL_DIM ** -0.5

    def body(q_ref, k_ref, v_ref, lse_ref, mix_ref, o_ref, do_ref, dq_ref, dk_ref, dv_ref):
        dk_ref[...] = jnp.zeros_like(dk_ref)
        dv_ref[...] = jnp.zeros_like(dv_ref)
        for pi in range(pairs):
            cs = slice(LANES * pi, LANES * (pi + 1))
            for q0, q1, k0 in _band_blocks(L, tq):
                qb, kb, vb = q_ref[q0:q1, cs], k_ref[k0:q1, cs], v_ref[k0:q1, cs]
                lse_p, dout = lse_ref[q0:q1, cs], do_ref[q0:q1, cs]
                wgt = jnp.exp(lse_p - mix_ref[q0:q1, cs])
                dd = dout * o_ref[q0:q1, cs]
                lo = _lane(qb.shape) < DIL_DIM
                band = _band_mask(q0, q1, k0)
                dqs = []
                for m, c0 in ((lo, 0), (~lo, DIL_DIM)):
                    w_h = wgt[:, c0:c0 + 1]
                    big_d = jnp.sum(jnp.where(m, dd, 0.0), axis=-1, keepdims=True)
                    qm = jnp.where(m, qb, jnp.zeros_like(qb))
                    s = _dot(qm, kb, "nt") * scale
                    p = jnp.where(band, jnp.exp(s - lse_p[:, c0:c0 + 1]), 0.0)
                    dom = jnp.where(m, dout, 0.0) * w_h
                    ds = p * (_dot(dom, vb, "nt") - w_h * big_d) * scale
                    dqs.append(_dot(ds, kb))
                    dk_ref[k0:q1, cs] += _dot(ds, qm, "tn")
                    dv_ref[k0:q1, cs] += _dot(p, dom, "tn")
                dq_ref[q0:q1, cs] = jnp.where(lo, dqs[0], dqs[1])

    spec = _class_spec(q, dil, width)
    cat_spec = _class_spec(o_cat, dil, width, col0=HEADS * LANES)
    res = pl.pallas_call(
        body, name=name, grid=(dil * DIL_WIDTH // width,), in_specs=[spec] * 5 + [cat_spec] * 2, out_specs=[spec] * 3,
        out_shape=[jax.ShapeDtypeStruct((L, dil * DIL_WIDTH), F32)] * 3,
        compiler_params=_params())(*[_class_view(a, dil) for a in (q, k, v, lse, lse_mix, o_cat, do_cat)])
    return [r.reshape(S, DIL_WIDTH) for r in res]


def combine_fwd(name, outs, lses, o_cat, tm=512):
    S = outs[0].shape[0]

    def body(o1, o2, o3, l1, l2, l3, cat_in, cat_out, mix_ref):
        ls = [l1[...], l2[...], l3[...]]
        m = jnp.maximum(jnp.maximum(ls[0], ls[1]), ls[2])
        e = [jnp.exp(l - m) for l in ls]
        den = e[0] + e[1] + e[2]
        cat_out[...] = (e[0] / den) * o1[...] + (e[1] / den) * o2[...] + (e[2] / den) * o3[...]
        mix_ref[...] = m + jnp.log(den)

    row = pl.BlockSpec((tm, DIL_WIDTH), lambda i: (i, 0))
    return pl.pallas_call(
        body, name=name, grid=(S // tm,), in_specs=[row] * 6 + [pl.BlockSpec(memory_space=pl.ANY)],
        out_specs=[pl.BlockSpec((tm, DIL_WIDTH), lambda i: (i, HEADS * LANES // DIL_WIDTH)), row],
        out_shape=[jax.ShapeDtypeStruct(o_cat.shape, F32), jax.ShapeDtypeStruct((S, DIL_WIDTH), F32)],
        input_output_aliases={6: 0}, compiler_params=_params())(*outs, *lses, o_cat)


def _shift_down(u, n):
    t = lax.broadcasted_iota(jnp.int32, u.shape, 0)
    return jnp.where(t >= n, pltpu.roll(u, n, axis=0), 0.0)


def _shift_up(u, n):
    rows = u.shape[0]
    t = lax.broadcasted_iota(jnp.int32, u.shape, 0)
    return jnp.where(t < rows - n, pltpu.roll(u, rows - n, axis=0), 0.0)


def _conv(u, w, b):
    return w[2:3, :] * u + w[1:2, :] * _shift_down(u, 1) + w[0:1, :] * _shift_down(u, 2) + b


CONV_TC = 256
CONV_NB = D_FF // CONV_TC


def _pair_spec(rows):
    return pl.BlockSpec((rows, 2 * CONV_TC), lambda j: (0, j))


def conv_glu_fwd(name, up, w_conv, b_conv):
    S = up.shape[0]

    def body(u_ref, w_ref, b_ref, act_ref):
        u = _conv(u_ref[...], w_ref[...], b_ref[...])
        act_ref[...] = (_silu(u[:, :CONV_TC]) * u[:, CONV_TC:]).astype(act_ref.dtype)

    return pl.pallas_call(
        body, name=name, grid=(CONV_NB,), in_specs=[_pair_spec(S), _pair_spec(3), _pair_spec(1)],
        out_specs=pl.BlockSpec((S, CONV_TC), lambda j: (0, j)), out_shape=jax.ShapeDtypeStruct((S, D_FF), MXU_DTYPE),
        compiler_params=_params())(up, w_conv, b_conv)


def conv_glu_bwd(name, up, w_conv, b_conv, dact):
    S = up.shape[0]

    def body(u_ref, w_ref, b_ref, da_ref, dup_ref, dw_ref, db_ref):
        uin, w = u_ref[...], w_ref[...]
        u = _conv(uin, w, b_ref[...])
        gate, val, da = u[:, :CONV_TC], u[:, CONV_TC:], da_ref[...]
        sig = 1.0 / (1.0 + jnp.exp(-gate))
        du = jnp.concatenate([da * val * (sig * (1.0 + gate * (1.0 - sig))), da * (gate * sig)], axis=1)
        dup_ref[...] = (w[2:3, :] * du + w[1:2, :] * _shift_up(du, 1) + w[0:1, :] * _shift_up(du, 2)).astype(dup_ref.dtype)
        dw_ref[...] = jnp.concatenate([_colsum(du * _shift_down(uin, 2)), _colsum(du * _shift_down(uin, 1)), _colsum(du * uin)], axis=0)
        db_ref[...] = _colsum(du)

    return pl.pallas_call(
        body, name=name, grid=(CONV_NB,),
        in_specs=[_pair_spec(S), _pair_spec(3), _pair_spec(1), pl.BlockSpec((S, CONV_TC), lambda j: (0, j))],
        out_specs=[_pair_spec(S), _pair_spec(3), _pair_spec(1)],
        out_shape=[jax.ShapeDtypeStruct((S, 2 * D_FF), MXU_DTYPE), jax.ShapeDtypeStruct((3, 2 * D_FF), F32),
                   jax.ShapeDtypeStruct((1, 2 * D_FF), F32)],
        compiler_params=_params())(up, w_conv, b_conv, dact)


def adamw(name, w, parts, m, v, tr=None):
    R, C = w.shape
    tr = tr or R
    assert R % tr == 0
    c1 = 1.0 - ADAM_B1 ** ADAM_STEP
    c2 = 1.0 - ADAM_B2 ** ADAM_STEP
    np_ = len(parts)

    def body(*refs):
        w_ref, m_ref, v_ref = refs[0], refs[1 + np_], refs[2 + np_]
        go_ref, d_ref, mo_ref, vo_ref = refs[3 + np_:]
        terms = []
        for part, ref in zip(parts, refs[1:1 + np_], strict=True):
            terms += [ref[...]] if part.ndim == 2 else [ref[p] for p in range(part.shape[0])]
        g = terms[0].astype(F32)
        for term in terms[1:]:
            g = g + term.astype(F32)
        m2 = ADAM_B1 * m_ref[...] + (1.0 - ADAM_B1) * g
        v2 = ADAM_B2 * v_ref[...] + (1.0 - ADAM_B2) * (g * g)
        go_ref[...] = g
        mo_ref[...] = m2
        vo_ref[...] = v2
        d_ref[...] = -ADAM_LR * ((m2 / c1) / (jnp.sqrt(v2 / c2) + ADAM_EPS) + ADAM_WD * w_ref[...])

    blk = pl.BlockSpec((tr, C), lambda i: (i, 0))
    part_specs = [blk if p.ndim == 2 else pl.BlockSpec((p.shape[0], tr, C), lambda i: (0, i, 0)) for p in parts]
    return pl.pallas_call(
        body, name=name, grid=(R // tr,),
        in_specs=[blk] + part_specs + [blk, blk], out_specs=[blk] * 4,
        out_shape=[jax.ShapeDtypeStruct((R, C), F32)] * 4, compiler_params=_params())(w, *parts, m, v)


def _place():
    return lax.axis_index("x"), lax.axis_index("y"), lax.axis_index("c")


def all_gather(name, arrs, after=None):
    n = len(arrs)
    deps = [] if after is None else [after]

    def body(*refs):
        ins, outs = refs[:n], refs[n + len(deps):2 * n + len(deps)]
        send_sems, recv_sems, local_sems = refs[2 * n + len(deps):]
        x, y, c = _place()
        me, sibling = (x, y, c), (x, y, 1 - c)
        chips = [(1 - x, y), (x, 1 - y), (1 - x, 1 - y)]
        sends = []
        for t in range(n):
            out = outs[t]

            def slot(px, py, pc, out=out):
                return out.at[4 * px + 2 * py + pc]

            def copy(k, block, to, src=None, t=t, slot=slot):
                return pltpu.make_async_remote_copy(
                    src_ref=slot(*block) if src is None else src, dst_ref=slot(*block),
                    send_sem=send_sems.at[7 * t + k], recv_sem=recv_sems.at[7 * t + k],
                    device_id=to, device_id_type=MESH_ID)

            mine = pltpu.make_async_copy(ins[t], slot(*me), local_sems.at[t])
            mine.start()
            first = [copy(0, me, sibling, src=ins[t])]
            first += [copy(1 + j, me, (*chip, c), src=ins[t]) for j, chip in enumerate(chips)]
            for cp in first:
                cp.start()
            sends.append((mine, first, copy))
        for t in range(n):
            mine, first, copy = sends[t]
            passed = [copy(4 + j, (*chip, c), sibling) for j, chip in enumerate(chips)]
            for j, chip in enumerate(chips):
                copy(1 + j, (*chip, c), me).wait_recv()
                passed[j].start()
            copy(0, sibling, me).wait_recv()
            for j, chip in enumerate(chips):
                copy(4 + j, (*chip, 1 - c), me).wait_recv()
            for cp in first + passed:
                cp.wait_send()
            mine.wait()

    any_spec = pl.BlockSpec(memory_space=pl.ANY)
    res = pl.pallas_call(
        body, name=name, in_specs=[any_spec] * (n + len(deps)), out_specs=[any_spec] * n,
        out_shape=[jax.ShapeDtypeStruct((N_DEV,) + a.shape, a.dtype) for a in arrs],
        scratch_shapes=[pltpu.SemaphoreType.DMA((7 * n,)), pltpu.SemaphoreType.DMA((7 * n,)), pltpu.SemaphoreType.DMA((n,))],
        compiler_params=pltpu.CompilerParams(has_side_effects=True))(*arrs, *deps)
    return list(res)


HBM_SPEC = pl.BlockSpec(memory_space=pltpu.HBM)
SEM_SPEC = pl.BlockSpec(memory_space=pltpu.SEMAPHORE)
DATAFLOW = pltpu.SideEffectType.DATAFLOW_SIDE_EFFECTING


def _exchange_copies(srcs, lands, send_sems, recv_sems, gather):
    x, y, c = _place()
    me = 4 * x + 2 * y + c
    out = []
    for t, (src, land) in enumerate(zip(srcs, lands, strict=True)):
        for k in range(1, N_DEV):
            px, py, pc = x ^ (k >> 2), y ^ ((k >> 1) & 1), c ^ (k & 1)
            out.append(pltpu.make_async_remote_copy(
                src_ref=src if gather else src.at[4 * px + 2 * py + pc],
                dst_ref=land.at[me] if gather else land.at[k - 1],
                send_sem=send_sems.at[7 * t + k - 1], recv_sem=recv_sems.at[7 * t + k - 1],
                device_id=(px, py, pc), device_id_type=MESH_ID))
    return out


def exchange_start(name, arrs, gather, after=None):
    n = len(arrs)
    lands = [lax.empty(((N_DEV,) + a.shape) if gather else ((N_DEV - 1,) + a.shape[1:]), a.dtype) for a in arrs]
    deps = [] if after is None else [after]

    def body(*refs):
        srcs, land_refs = refs[:n], refs[n:2 * n]
        send_sems, recv_sems = refs[2 * n + len(deps)], refs[2 * n + len(deps) + 1]
        token = refs[-1]
        for cp in _exchange_copies(srcs, land_refs, send_sems, recv_sems, gather):
            cp.start()
        token[...] = jnp.zeros_like(token)

    hbm = lambda a: pltpu.HBM(a.shape, a.dtype)
    res = pl.pallas_call(
        body, name=name,
        out_shape=(pltpu.SemaphoreType.DMA((7 * n,)), pltpu.SemaphoreType.DMA((7 * n,)), *[hbm(a) for a in arrs],
                   *[hbm(l) for l in lands], jax.ShapeDtypeStruct((8, 128), F32)),
        in_specs=[HBM_SPEC] * (2 * n) + [pl.BlockSpec(memory_space=pl.ANY)] * len(deps),
        out_specs=(SEM_SPEC, SEM_SPEC, *[HBM_SPEC] * (2 * n), pl.BlockSpec(memory_space=pltpu.VMEM)),
        input_output_aliases={i: 2 + i for i in range(2 * n)},
        compiler_params=pltpu.CompilerParams(has_side_effects=DATAFLOW),
    )(*[pltpu.with_memory_space_constraint(a, pltpu.HBM) for a in arrs + lands], *deps)
    return res[0], res[1], list(res[2:2 + n]), list(res[2 + n:2 + 2 * n]), res[-1]


def exchange_wait(name, started, gather, after):
    send_sems, recv_sems, srcs, lands, _ = started
    n = len(srcs)

    def body(*refs):
        src_refs, land_refs = refs[:n], refs[n:2 * n]
        copies = _exchange_copies(src_refs, land_refs, refs[2 * n], refs[2 * n + 1], gather)
        for cp in copies:
            cp.wait_send()
        for cp in copies:
            cp.wait_recv()

    hbm = lambda a: pltpu.HBM(a.shape, a.dtype)
    res = pl.pallas_call(
        body, name=name, out_shape=tuple(hbm(a) for a in srcs + lands),
        in_specs=[HBM_SPEC] * (2 * n) + [SEM_SPEC, SEM_SPEC, pl.BlockSpec(memory_space=pl.ANY)],
        out_specs=tuple([HBM_SPEC] * (2 * n)), input_output_aliases={i: i for i in range(2 * n)},
        compiler_params=pltpu.CompilerParams(has_side_effects=DATAFLOW),
    )(*srcs, *lands, send_sems, recv_sems, after)
    return list(res[:n]), list(res[n:])


def _gather_cols(stack):
    p, k, n = stack.shape
    return stack.transpose(1, 0, 2).reshape(k, p * n)


def _scatter_cols(full):
    k, n = full.shape
    return full.reshape(k, N_DEV, n // N_DEV).transpose(1, 0, 2)


def _gather_rows(stack):
    p, r, n = stack.shape
    return stack.reshape(p * r, n)


def _scatter_rows(full):
    r, n = full.shape
    return full.reshape(N_DEV, r // N_DEV, n)


_IN_NAT = Q_LORA + KV_LORA


def to_kernel_layout(name, w):
    if name == "w_in":
        z = lambda n: jnp.zeros((w.shape[0], n), w.dtype)
        return jnp.concatenate([w[:, :_IN_NAT], z(KPE_LO), w[:, _IN_NAT:_IN_NAT + ROPE], z(LANES - KPE_LO - ROPE),
                                w[:, _IN_NAT + ROPE:]], axis=1)
    if name == "w_q_b":
        return jnp.pad(w.reshape(w.shape[0], HEADS, NOPE + ROPE), ((0, 0), (0, 0), (0, LANES - NOPE - ROPE))).reshape(w.shape[0], HEADS * LANES)
    if name == "w_o":
        mla = jnp.pad(w[:HEADS * NOPE].reshape(HEADS, NOPE, -1), ((0, 0), (LANES - NOPE, 0), (0, 0))).reshape(HEADS * LANES, -1)
        return jnp.concatenate([mla, w[HEADS * NOPE:]], axis=0)
    if name in ("w_up", "w_conv", "b_conv"):
        return w.reshape(w.shape[0], 2, CONV_NB, CONV_TC).transpose(0, 2, 1, 3).reshape(w.shape[0], 2 * D_FF)
    return w


def from_kernel_layout(name, g):
    if name == "w_in":
        return jnp.concatenate([g[:, :_IN_NAT], g[:, P_KPE + KPE_LO:P_KPE + KPE_LO + ROPE], g[:, P_QD:]], axis=1)
    if name == "w_q_b":
        return g.reshape(g.shape[0], HEADS, LANES)[:, :, :NOPE + ROPE].reshape(g.shape[0], HEADS * (NOPE + ROPE))
    if name == "w_o":
        mla = g[:HEADS * LANES].reshape(HEADS, LANES, -1)[:, LANES - NOPE:, :].reshape(HEADS * NOPE, -1)
        return jnp.concatenate([mla, g[HEADS * LANES:]], axis=0)
    if name in ("w_up", "w_conv", "b_conv"):
        return g.reshape(g.shape[0], CONV_NB, 2, CONV_TC).transpose(0, 2, 1, 3).reshape(g.shape[0], 2 * D_FF)
    return g


SMALL = (("loss", 1), ("b_ada", 6 * D_MODEL), ("g_mix_norm", D_MODEL), ("g_q_lat", Q_LORA), ("g_kv_lat", KV_LORA),
         ("g_mla_q_nope", NOPE), ("g_mla_q_pe", ROPE), ("g_mla_k_nope", NOPE), ("g_mla_k_pe", ROPE),
         ("g_dil_q", DIL_DIM), ("g_dil_k", DIL_DIM), ("g_ffn_norm", D_MODEL), ("b_conv", 2 * D_FF))
SMALL_ROWS = 16
SMALL_COLS = 1024


def _pack_small(values):
    parts = [values[name].reshape(-1).astype(F32) if name in values else jnp.zeros((n,), F32) for name, n in SMALL]
    flat = jnp.concatenate(parts)
    flat = jnp.pad(flat, (0, SMALL_ROWS * SMALL_COLS - flat.shape[0]))
    return flat.reshape(SMALL_ROWS, SMALL_COLS)


def _unpack_small(packed):
    flat = packed.reshape(-1)
    out, off = {}, 0
    for name, n in SMALL:
        out[name] = flat[off:off + n].reshape(1, n)
        off += n
    return out


def _local_step(x, pos, mod, target, w, fetch, emit):
    S = SEQ
    sh1, sc1, g1, sh2, sc2, g2 = [mod[:, i * D_MODEL:(i + 1) * D_MODEL] for i in range(6)]
    zeros = lambda n: jnp.zeros((1, n), F32)
    g_q = jnp.concatenate([w["g_mla_q_nope"], w["g_mla_q_pe"], zeros(LANES - NOPE - ROPE)], axis=1)
    g_k = jnp.concatenate([w["g_mla_k_nope"], zeros(LANES - NOPE)], axis=1)
    g_kpe = jnp.concatenate([zeros(KPE_LO), w["g_mla_k_pe"], zeros(LANES - KPE_LO - ROPE)], axis=1)
    g_dq = jnp.concatenate([w["g_dil_q"]] * 2, axis=1)
    g_dk = jnp.concatenate([w["g_dil_k"]] * 2, axis=1)
    b_conv = to_kernel_layout("b_conv", w["b_conv"])

    def inv_freq(d):
        return jnp.power(ROPE_THETA, -2.0 * jnp.arange(d // 2, dtype=F32) / d)

    f_mla = jnp.concatenate([jnp.zeros((KPE_LO,), F32), inv_freq(ROPE), inv_freq(ROPE), jnp.zeros((LANES - KPE_LO - ROPE,), F32)])
    f_dil = jnp.concatenate([inv_freq(DIL_DIM)] * 4)

    def tables_fn(rows, params):
        (p,), (fa, fb) = rows, params
        return [jnp.cos(p * fa), jnp.sin(p * fa), jnp.cos(p * fb), jnp.sin(p * fb)], []

    cos_m, sin_m, cos_d, sin_d = rowwise("rope_tables", tables_fn, [pos], [f_mla.reshape(1, LANES), f_dil.reshape(1, LANES)],
                                         [(LANES, F32)] * 4)
    tables = [cos_m, sin_m, cos_d, sin_d]
    H_M, H_D = ROPE // 2, DIL_DIM // 2

    def ln1_fn(rows, params):
        (xv,), (g, sc, sh) = rows, params
        y, _, _ = _rms(xv, g)
        return [y * (1.0 + sc) + sh], []

    (h,) = rowwise("ln1_fwd", ln1_fn, [x], [w["g_mix_norm"], sc1, sh1], [(D_MODEL, MXU_DTYPE)])
    w_in = fetch("w_in", h)
    proj = matmul("proj_fwd", h, w_in, "nn", tm=512)

    def post_fn(rows, params):
        (pv, cm, sm, cd, sd), (gq, gkv, gkp, gdq, gdk) = rows, params
        kper = _rope(_grms(pv[:, P_KPE:P_QD], gkp, KPE_GROUPS)[0], cm, sm, H_M)
        qd = [_rope(_grms(c, gdq, DIL_GROUPS)[0], cd, sd, H_D) for c in _chunks(pv[:, P_QD:P_KD])]
        kd = [_rope(_grms(c, gdk, DIL_GROUPS)[0], cd, sd, H_D) for c in _chunks(pv[:, P_KD:P_VD])]
        return [_rms(pv[:, P_QLAT:P_KVLAT], gq)[0], _rms(pv[:, P_KVLAT:P_KPE], gkv)[0], kper,
                jnp.concatenate(qd, axis=1), jnp.concatenate(kd, axis=1), pv[:, P_VD:P_END]], []

    post_params = [w["g_q_lat"], w["g_kv_lat"], g_kpe, g_dq, g_dk]
    qln, kvn, kper, qd_r, kd_r, vd = rowwise(
        "proj_post", post_fn, [proj] + tables, post_params,
        [(Q_LORA, MXU_DTYPE), (KV_LORA, MXU_DTYPE), (LANES, MXU_DTYPE)] + [(DIL_WIDTH, MXU_DTYPE)] * 3, tm=256)
    w_q_b, w_kv_b = fetch("w_q_b", qln), fetch("w_kv_b", kvn)
    q = matmul("q_fwd", qln, w_q_b, "nn", tm=1024)
    kv = matmul("kv_fwd", kvn, w_kv_b, "nn", tm=1024)

    def mla_prep_fn(rows, params):
        (qv, kvv, kp, cm, sm), (gq, gk) = rows, params
        value_lanes = _lane(kp.shape) >= NOPE
        qs, ks, vs = [], [], []
        for qc, kc in zip(_chunks(qv), _chunks(kvv), strict=True):
            qs.append(_rope(_grms(qc, gq, Q_GROUPS)[0], cm, sm, H_M))
            ks.append(_grms(kc, gk, K_GROUPS)[0] + kp)
            vs.append(jnp.where(value_lanes, kc, 0.0))
        return [jnp.concatenate(t, axis=1) for t in (qs, ks, vs)], []

    q_mla, k_mla, v_mla = rowwise("mla_prep", mla_prep_fn, [q, kv, kper, cos_m, sin_m], [g_q, g_k],
                                  [(HEADS * LANES, MXU_DTYPE)] * 3, tm=256)
    mla_scale = (NOPE + ROPE) ** -0.5
    o_cat, lse_mla = mla_fwd("mla_fwd", q_mla, k_mla, v_mla, mla_scale)

    band = [band_fwd(f"band{dil}_fwd", qd_r, kd_r, vd, dil) for dil in DILATIONS]
    o_cat, lse_mix = combine_fwd("dil_combine", [b[0] for b in band], [b[1] for b in band], o_cat)
    w_o = fetch("w_o", o_cat)
    mix = matmul("mix_fwd", o_cat, w_o, "nn", tm=512)

    def mid_fn(rows, params):
        (xv, mx), (gate1, g, sc, sh) = rows, params
        x1 = xv + gate1 * mx
        y, _, _ = _rms(x1, g)
        return [x1, y * (1.0 + sc) + sh], []

    x1, h2 = rowwise("mid_fwd", mid_fn, [x, mix], [g1, w["g_ffn_norm"], sc2, sh2], [(D_MODEL, F32), (D_MODEL, MXU_DTYPE)])
    w_up, w_conv, w_down = fetch("w_up", h2), fetch("w_conv", h2), fetch("w_down", h2)
    up = matmul("up_fwd", h2, w_up, "nn", tm=512, tn=1408)
    act = conv_glu_fwd("conv_fwd", up, w_conv, b_conv)
    dn = matmul("down_fwd", act, w_down, "nn", tm=512)

    def final_fn(rows, params):
        (x1v, dnv, tgt), (gate2,) = rows, params
        r = x1v + gate2 * dnv - tgt
        dy = r * (1.0 / D_MODEL)
        loss = jnp.sum(_colsum(r * r), axis=-1, keepdims=True) * (0.5 / D_MODEL)
        return [dy, gate2 * dy], [loss, _colsum(dy * dnv)]

    dy, d_dn, loss, dg2 = rowwise("loss_head", final_fn, [x1, dn, target], [g2], [(D_MODEL, F32), (D_MODEL, MXU_DTYPE)],
                                  [1, D_MODEL])
    emit("w_down", matmul("down_wgrad", act, d_dn, "tn", tm=1408, out_dtype=MXU_DTYPE))
    dact = matmul("down_dgrad", d_dn, w_down, "nt", tm=512)
    dup, g_w_conv, g_b_conv = conv_glu_bwd("conv_bwd", up, w_conv, b_conv, dact)
    emit("w_conv", g_w_conv)
    sent = emit("w_up", matmul("up_wgrad", h2, dup, "tn", tn=1408, out_dtype=MXU_DTYPE))
    dh2 = matmul("up_dgrad", dup, w_up, "nt", tm=512, tk=2816, dep=sent)

    def mid_bwd_fn(rows, params):
        (dh2v, dyv, x1v, mx), (gate1, g, sc) = rows, params
        yn, n, rstd = _rms(x1v, g)
        dx_n, dg = _rms_bwd(dh2v * (1.0 + sc), n, rstd, g)
        dx1 = dyv + dx_n
        return [dx1, gate1 * dx1], [dg, _colsum(dh2v * yn), _colsum(dh2v), _colsum(dx1 * mx)]

    dx1, dmix, dg_ffn, dsc2, dsh2, dg1 = rowwise(
        "mid_bwd", mid_bwd_fn, [dh2, dy, x1, mix], [g1, w["g_ffn_norm"], sc2], [(D_MODEL, F32), (D_MODEL, MXU_DTYPE)],
        [D_MODEL] * 4)

    sent = emit("w_o", matmul("mix_wgrad", o_cat, dmix, "tn", tm=512, out_dtype=MXU_DTYPE))
    do_cat = matmul("mix_dgrad", dmix, w_o, "nt", tm=512, dep=sent)
    dband = [band_bwd(f"band{dil}_bwd", qd_r, kd_r, vd, b[1], lse_mix, o_cat, do_cat, dil) for dil, b in zip(DILATIONS, band)]
    dq_mla, dkv_mla, dkper = mla_bwd("mla_bwd", q_mla, k_mla, v_mla, o_cat, do_cat, lse_mla, mla_scale)

    def mla_prep_bwd_fn(rows, params):
        (dqv, dkvv, qv, kvv, cm, sm), (gq, gk) = rows, params
        nope_lanes = _lane(cm.shape) < NOPE
        dqs, dkvs, dgq, dgk = [], [], 0.0, 0.0
        for dqc, dkc, qc, kc in zip(_chunks(dqv), _chunks(dkvv), _chunks(qv), _chunks(kvv), strict=True):
            _, n, rstd = _grms(qc, gq, Q_GROUPS)
            dx, dg = _grms_bwd(_rope_bwd(dqc, cm, sm, H_M), n, rstd, gq, Q_GROUPS)
            dqs.append(dx)
            dgq = dgq + dg
            _, n, rstd = _grms(kc, gk, K_GROUPS)
            dx, dg = _grms_bwd(dkc, n, rstd, gk, K_GROUPS)
            dkvs.append(jnp.where(nope_lanes, dx, dkc))
            dgk = dgk + dg
        return [jnp.concatenate(dqs, axis=1), jnp.concatenate(dkvs, axis=1)], [dgq, dgk]

    dq, dkv, dg_q, dg_k = rowwise("mla_prep_bwd", mla_prep_bwd_fn, [dq_mla, dkv_mla, q, kv, cos_m, sin_m], [g_q, g_k],
                                  [(HEADS * LANES, MXU_DTYPE)] * 2, [LANES, LANES], tm=256)
    emit("w_q_b", matmul("q_wgrad", qln, dq, "tn", out_dtype=MXU_DTYPE))
    emit("w_kv_b", matmul("kv_wgrad", kvn, dkv, "tn", out_dtype=MXU_DTYPE))
    dqln = matmul("q_dgrad", dq, w_q_b, "nt", tm=1024)
    dkvn = matmul("kv_dgrad", dkv, w_kv_b, "nt", tm=1024)

    def pre_bwd_fn(rows, params):
        dql, dkvl, dkp = rows[0:3]
        dqd_, dkd_, dvd_ = [rows[3 + 3 * i] + rows[4 + 3 * i] + rows[5 + 3 * i] for i in range(3)]
        pv, cm, sm, cd, sd = rows[12:]
        gq, gkv, gkp, gdq, gdk = params
        r_q = _norm_bwd(dql, pv[:, P_QLAT:P_KVLAT], gq)
        r_kv = _norm_bwd(dkvl, pv[:, P_KVLAT:P_KPE], gkv)
        _, n, rstd = _grms(pv[:, P_KPE:P_QD], gkp, KPE_GROUPS)
        r_kp = _grms_bwd(_rope_bwd(dkp, cm, sm, H_M), n, rstd, gkp, KPE_GROUPS)
        outs, dgs = [r_q[0], r_kv[0], r_kp[0]], []
        for dval, lo, g in ((dqd_, P_QD, gdq), (dkd_, P_KD, gdk)):
            dg_sum = 0.0
            for dc, xc in zip(_chunks(dval), _chunks(pv[:, lo:lo + DIL_WIDTH]), strict=True):
                _, n, rstd = _grms(xc, g, DIL_GROUPS)
                dx, dg = _grms_bwd(_rope_bwd(dc, cd, sd, H_D), n, rstd, g, DIL_GROUPS)
                outs.append(dx)
                dg_sum = dg_sum + dg
            dgs.append(dg_sum)
        return [jnp.concatenate(outs + [dvd_], axis=1)], [r_q[1], r_kv[1], r_kp[1]] + dgs

    dproj, dg_q_lat, dg_kv_lat, dg_kpe, dg_dq, dg_dk = rowwise(
        "proj_pre_bwd", pre_bwd_fn,
        [dqln, dkvn, dkper] + [d[i] for i in range(3) for d in dband] + [proj] + tables, post_params,
        [(P_END, MXU_DTYPE)], [Q_LORA, KV_LORA, LANES, LANES, LANES], tm=256)
    sent = emit("w_in", matmul("proj_wgrad", h, dproj, "tn", tm=256, out_dtype=MXU_DTYPE))
    dh = matmul("proj_dgrad", dproj, w_in, "nt", tm=512, dep=sent)

    def ln1_bwd_fn(rows, params):
        (dhv, dres, xv), (g, sc) = rows, params
        yn, n, rstd = _rms(xv, g)
        dx_n, dg = _rms_bwd(dhv * (1.0 + sc), n, rstd, g)
        return [dres + dx_n], [dg, _colsum(dhv * yn), _colsum(dhv)]

    grad_x, dg_mix, dsc1, dsh1 = rowwise("ln1_bwd", ln1_bwd_fn, [dh, dx1, x], [w["g_mix_norm"], sc1], [(D_MODEL, F32)],
                                         [D_MODEL] * 3)
    dmod = jnp.concatenate([dsh1, dsc1, dg1, dsh2, dsc2, dg2], axis=-1)
    small = {"loss": loss, "b_ada": dmod, "g_mix_norm": dg_mix, "g_q_lat": dg_q_lat, "g_kv_lat": dg_kv_lat,
             "g_mla_q_nope": dg_q[:, :NOPE], "g_mla_q_pe": dg_q[:, NOPE:NOPE + ROPE], "g_mla_k_nope": dg_k[:, :NOPE],
             "g_mla_k_pe": dg_kpe[:, KPE_LO:KPE_LO + ROPE], "g_dil_q": dg_dq[:, :DIL_DIM] + dg_dq[:, DIL_DIM:],
             "g_dil_k": dg_dk[:, :DIL_DIM] + dg_dk[:, DIL_DIM:], "g_ffn_norm": dg_ffn,
             "b_conv": from_kernel_layout("b_conv", g_b_conv)}
    return grad_x, small


COL_SHARDED = ("w_in", "w_q_b", "w_kv_b", "w_up", "w_conv")
ROW_SHARDED = ("w_o", "w_down")
ADAM_TILE = {"w_ada": 256, "w_in": 256, "w_up": 256, "w_down": 176}
GATHER_GROUPS = (("w_in", "w_q_b", "w_kv_b"), ("w_o", "w_up", "w_conv", "w_down"))
SCATTER_GROUPS = (("w_down", "w_conv", "w_up"), ("w_o",), ("w_q_b", "w_kv_b", "w_in"))
OUT_WEIGHTS = ("w_ada", "b_ada", "g_mix_norm", "w_in", "g_q_lat", "w_q_b", "g_kv_lat", "w_kv_b", "g_mla_q_nope", "g_mla_q_pe",
               "g_mla_k_nope", "g_mla_k_pe", "g_dil_q", "g_dil_k", "w_o", "g_ffn_norm", "w_up", "w_conv", "b_conv", "w_down")


def kernel(x, c, positions, w_ada, b_ada, g_mix_norm, w_in, g_q_lat, w_q_b, g_kv_lat, w_kv_b, g_mla_q_nope, g_mla_q_pe, g_mla_k_nope, g_mla_k_pe, g_dil_q, g_dil_k, w_o, g_ffn_norm, w_up, w_conv, b_conv, w_down, loss_target, m_w_ada, m_b_ada, m_g_mix_norm, m_w_in, m_g_q_lat, m_w_q_b, m_g_kv_lat, m_w_kv_b, m_g_mla_q_nope, m_g_mla_q_pe, m_g_mla_k_nope, m_g_mla_k_pe, m_g_dil_q, m_g_dil_k, m_w_o, m_g_ffn_norm, m_w_up, m_w_conv, m_b_conv, m_w_down, v_w_ada, v_b_ada, v_g_mix_norm, v_w_in, v_g_q_lat, v_w_q_b, v_g_kv_lat, v_w_kv_b, v_g_mla_q_nope, v_g_mla_q_pe, v_g_mla_k_nope, v_g_mla_k_pe, v_g_dil_q, v_g_dil_k, v_w_o, v_g_ffn_norm, v_w_up, v_w_conv, v_b_conv, v_w_down):
    args = dict(locals())
    xi, yi, ci = _place()
    me = 4 * xi + 2 * yi + ci
    shard = {n: args[n][0] for n in COL_SHARDED + ROW_SHARDED + ("w_ada",)}
    small_w = {n: args[n] for n, _ in SMALL if n != "loss"}

    (c_all,) = all_gather("gather_c", [c])
    (sc_all,) = rowwise("silu_c", lambda rows, params: ([_silu(rows[0])], []), [c_all.reshape(N_DEV, D_MODEL)], [],
                        [(D_MODEL, MXU_DTYPE)])
    mod_part = matmul("ada_fwd", sc_all, shard["w_ada"], "nn")
    (mod_all,) = all_gather("gather_mod", [mod_part])

    payload = {n: shard[n] if n == "w_conv" else shard[n].astype(MXU_DTYPE) for n in COL_SHARDED + ROW_SHARDED}
    gathers, after_start = [], mod_all
    for i, grp in enumerate(GATHER_GROUPS):
        gathers.append(exchange_start(f"gather{i}_start", [payload[n] for n in grp], gather=True, after=after_start))
        after_start = gathers[-1][-1]
    full = {}

    def fetch(name, after):
        if name not in full:
            (i, grp), = [(i, grp) for i, grp in enumerate(GATHER_GROUPS) if name in grp]
            srcs, lands = exchange_wait(f"gather{i}_wait", gathers[i], True, after)
            for n, src, land in zip(grp, srcs, lands, strict=True):
                stack = lax.dynamic_update_index_in_dim(land, src, me, 0)
                full[n] = to_kernel_layout(n, _gather_cols(stack) if n in COL_SHARDED else _gather_rows(stack))
        return full[name]

    mod_row = lax.dynamic_index_in_dim(mod_all, me, axis=1, keepdims=False).reshape(1, 6 * D_MODEL)
    (mod,) = rowwise("ada_bias", lambda rows, params: ([rows[0] + rows[1]], []), [mod_row, b_ada], [], [(6 * D_MODEL, F32)],
                     dep=after_start)

    own, pending, scatters = {}, {}, {}

    def emit(name, grad):
        grad = from_kernel_layout(name, grad)
        parts = _scatter_cols(grad) if name in COL_SHARDED else _scatter_rows(grad)
        own[name] = lax.dynamic_index_in_dim(parts, me, 0, keepdims=False)
        pending[name] = parts
        for i, grp in enumerate(SCATTER_GROUPS):
            if name == grp[-1]:
                scatters[i] = exchange_start(f"scatter{i}_start", [pending[n] for n in grp], gather=False)
                return scatters[i][-1]
        return None

    pos = positions.reshape(SEQ, 1).astype(F32)
    grad_x, small = _local_step(x[0], pos, mod, loss_target[0], small_w, fetch, emit)

    res, done = {}, grad_x
    for i, grp in enumerate(SCATTER_GROUPS):
        _, lands = exchange_wait(f"scatter{i}_wait", scatters[i], False, done)
        for n, land in zip(grp, lands, strict=True):
            res[n] = adamw(f"adamw_{n}", shard[n], [own[n], land], args["m_" + n][0], args["v_" + n][0], ADAM_TILE.get(n))
            done = res[n][0]
    (small_all,) = all_gather("gather_small", [_pack_small(small)], after=done)
    packed = adamw("adamw_small", _pack_small(small_w), [small_all], _pack_small({n: args["m_" + n] for n in small_w}),
                   _pack_small({n: args["v_" + n] for n in small_w}))
    small_res = [_unpack_small(p) for p in packed]
    dmod_all = small_all.reshape(N_DEV, -1)[:, 1:1 + 6 * D_MODEL]
    dmod_mine = lax.dynamic_slice_in_dim(dmod_all, me * (6 * D_MODEL // N_DEV), 6 * D_MODEL // N_DEV, axis=1)
    g_w_ada = matmul("ada_wgrad", sc_all, dmod_mine, "tn")
    res["w_ada"] = adamw("adamw_w_ada", shard["w_ada"], [g_w_ada], m_w_ada[0], v_w_ada[0], ADAM_TILE["w_ada"])

    def leaf(kind, n):
        if n in res:
            return res[n][kind][None]
        return small_res[kind][n]

    loss = small_res[0]["loss"].reshape(())
    return (loss, grad_x[None], *[leaf(k, n) for k in range(4) for n in OUT_WEIGHTS])
```

```python
import jax
import jax.numpy as jnp
from jax import lax
from jax.experimental import pallas as pl
from jax.experimental.pallas import tpu as pltpu

F32 = jnp.float32
MXU_DTYPE = jnp.bfloat16

N_DEV = 8
D_MODEL = 1024
SEQ = 2048
HEADS = 8
NOPE = 64
ROPE = 32
Q_LORA = 512
KV_LORA = 256
DIL_DIM = 64
DIL_WIDTH = HEADS * DIL_DIM
DILATIONS = (1, 4, 16)
SPAN = 128
D_FF = 2816
LANES = 128
ROPE_THETA = 10000.0
EPS = 1e-6
NEG_INF = -1e30
ADAM_LR, ADAM_B1, ADAM_B2, ADAM_EPS, ADAM_WD, ADAM_STEP = 0.001, 0.9, 0.999, 1e-08, 0.01, 10
VMEM_LIMIT = 56 * 1024 * 1024
MESH_ID = pl.DeviceIdType.MESH

P_QLAT, P_KVLAT, P_KPE, P_QD, P_KD, P_VD, P_END = 0, 512, 768, 896, 1408, 1920, 2432
KPE_LO = 64
MIX_IN = HEADS * LANES + DIL_WIDTH


def _params(**kw):
    return pltpu.CompilerParams(vmem_limit_bytes=VMEM_LIMIT, **kw)


def rowwise(name, fn, rows, params, out_rows, out_accs=(), tm=512, dep=None):
    deps = [] if dep is None else [dep]
    rows = [r if isinstance(r, tuple) else (r, r.shape[1], 0) for r in rows]
    R = rows[0][0].shape[0]
    tm = min(tm, R)
    steps = R // tm
    assert steps * tm == R
    in_specs = []
    for a, width, cb in rows:
        ri = a.shape[0]
        per = ri // tm
        assert per * tm == ri
        if ri == R:
            in_specs.append(pl.BlockSpec((tm, width), lambda i, cb=cb: (i, cb)))
        else:
            in_specs.append(pl.BlockSpec((tm, width), lambda i, per=per, cb=cb: (i % per, cb)))
    for p in params:
        in_specs.append(pl.BlockSpec(p.shape, lambda i: (0,) * p.ndim))
    in_specs += [pl.BlockSpec(memory_space=pl.ANY)] * len(deps)
    out_shape = [jax.ShapeDtypeStruct((R, d), dt) for d, dt in out_rows]
    out_specs = [pl.BlockSpec((tm, d), lambda i: (i, 0)) for d, _ in out_rows]
    out_shape += [jax.ShapeDtypeStruct((1, n), F32) for n in out_accs]
    out_specs += [pl.BlockSpec((1, n), lambda i: (0, 0)) for n in out_accs]
    nr, npar, no, na = len(rows), len(params), len(out_rows), len(out_accs)

    def body(*refs):
        rvals = [r[...] for r in refs[:nr]]
        pvals = [r[...] for r in refs[nr:nr + npar]]
        outs, accs = fn(rvals, pvals)
        first_out = nr + npar + len(deps)
        for ref, v in zip(refs[first_out:first_out + no], outs, strict=True):
            ref[...] = v.astype(ref.dtype)
        if na:
            acc_refs = refs[first_out + no:]
            i = pl.program_id(0)

            @pl.when(i == 0)
            def _():
                for ref, v in zip(acc_refs, accs, strict=True):
                    ref[...] = v

            @pl.when(i > 0)
            def _():
                for ref, v in zip(acc_refs, accs, strict=True):
                    ref[...] += v

    res = pl.pallas_call(body, name=name, grid=(steps,), in_specs=in_specs, out_specs=out_specs,
                         out_shape=out_shape, compiler_params=_params())(*[r[0] for r in rows], *params, *deps)
    return list(res)


_DIMS = {"nn": ((1,), (0,)), "nt": ((1,), (1,)), "tn": ((0,), (0,))}


def _dot(a, b, mode="nn"):
    return lax.dot_general(a.astype(MXU_DTYPE), b.astype(MXU_DTYPE), (_DIMS[mode], ((), ())),
                           preferred_element_type=F32)


def matmul(name, a, b, mode, tm=None, tn=None, tk=None, out_dtype=F32, dep=None):
    if mode == "tn":
        K, M = a.shape
    else:
        M, K = a.shape
    N = b.shape[0] if mode == "nt" else b.shape[1]
    tm, tn, tk = tm or M, tn or N, tk or K
    nm, nn, nk = M // tm, N // tn, K // tk
    assert nm * tm == M and nn * tn == N and nk * tk == K
    a_spec = pl.BlockSpec((tk, tm), lambda i, j, k: (k, i)) if mode == "tn" else pl.BlockSpec((tm, tk), lambda i, j, k: (i, k))
    b_spec = pl.BlockSpec((tn, tk), lambda i, j, k: (j, k)) if mode == "nt" else pl.BlockSpec((tk, tn), lambda i, j, k: (k, j))
    deps = [] if dep is None else [dep]

    def body(a_ref, b_ref, *rest):
        o_ref, scratch = rest[len(deps)], rest[len(deps) + 1:]
        p = _dot(a_ref[...], b_ref[...], mode)
        if nk == 1:
            o_ref[...] = p.astype(o_ref.dtype)
        else:
            acc = scratch[0]
            k = pl.program_id(2)

            @pl.when(k == 0)
            def _():
                acc[...] = p

            @pl.when(k > 0)
            def _():
                acc[...] += p

            @pl.when(k == nk - 1)
            def _():
                o_ref[...] = acc[...].astype(o_ref.dtype)

    return pl.pallas_call(
        body, name=name, grid=(nm, nn, nk), in_specs=[a_spec, b_spec] + [pl.BlockSpec(memory_space=pl.ANY)] * len(deps),
        out_specs=pl.BlockSpec((tm, tn), lambda i, j, k: (i, j)),
        out_shape=jax.ShapeDtypeStruct((M, N), out_dtype),
        scratch_shapes=[pltpu.VMEM((tm, tn), F32)] if nk > 1 else [],
        compiler_params=_params())(a, b, *deps)


def _rms(x, g):
    rstd = lax.rsqrt(jnp.mean(x * x, axis=-1, keepdims=True) + EPS)
    n = x * rstd
    return n * g, n, rstd


def _rms_bwd(dy, n, rstd, g):
    dg = jnp.sum(dy * n, axis=0, keepdims=True)
    dn = dy * g
    dx = rstd * (dn - n * jnp.mean(dn * n, axis=-1, keepdims=True))
    return dx, dg


def _norm_bwd(dy, x, g):
    _, n, rstd = _rms(x, g)
    return _rms_bwd(dy, n, rstd, g)


def _colsum(v):
    return jnp.sum(v, axis=0, keepdims=True)


def _silu(x):
    return x * (1.0 / (1.0 + jnp.exp(-x)))


def _lane(shape):
    return lax.broadcasted_iota(jnp.int32, shape, 1)


def _group_mean(v, groups):
    lane = _lane(v.shape)
    out = jnp.zeros_like(v)
    for lo, hi in groups:
        m = (lane >= lo) & (lane < hi)
        out = jnp.where(m, jnp.sum(jnp.where(m, v, 0.0), axis=-1, keepdims=True) * (1.0 / (hi - lo)), out)
    return out


def _in_groups(shape, groups):
    lane = _lane(shape)
    m = jnp.zeros(shape, jnp.bool_)
    for lo, hi in groups:
        m = m | ((lane >= lo) & (lane < hi))
    return m


def _grms(x, g, groups):
    rstd = lax.rsqrt(_group_mean(x * x, groups) + EPS)
    n = jnp.where(_in_groups(x.shape, groups), x * rstd, 0.0)
    return n * g, n, rstd


def _grms_bwd(dy, n, rstd, g, groups):
    dn = dy * g
    return rstd * (dn - n * _group_mean(dn * n, groups)), _colsum(dy * n)


def _rot(x, half, transpose=False):
    first = (_lane(x.shape) % (2 * half)) < half
    up = pltpu.roll(x, LANES - half, axis=1)
    down = pltpu.roll(x, half, axis=1)
    return jnp.where(first, up, -down) if transpose else jnp.where(first, -up, down)


def _rope(x, cos, sin, half):
    return x * cos + _rot(x, half) * sin


def _rope_bwd(dy, cos, sin, half):
    return dy * cos + _rot(dy * sin, half, transpose=True)


def _chunks(x):
    return [x[:, i:i + LANES] for i in range(0, x.shape[1], LANES)]


Q_GROUPS = ((0, NOPE), (NOPE, NOPE + ROPE))
K_GROUPS = ((0, NOPE),)
KPE_GROUPS = ((KPE_LO, KPE_LO + ROPE),)
DIL_GROUPS = ((0, DIL_DIM), (DIL_DIM, 2 * DIL_DIM))


def _col(width, rows=SEQ):
    return pl.BlockSpec((rows, width), lambda h: (0, h))


def mla_fwd(name, q, k, v, scale, tq=512):
    S = q.shape[0]

    def body(q_ref, k_ref, v_ref, o_ref, lse_ref):
        for i in range(S // tq):
            kext = (i + 1) * tq
            blk = slice(i * tq, kext)
            s = _dot(q_ref[blk, :], k_ref[:kext, :], "nt") * scale
            row = lax.broadcasted_iota(jnp.int32, s.shape, 0) + i * tq
            col = lax.broadcasted_iota(jnp.int32, s.shape, 1)
            s = jnp.where(col <= row, s, NEG_INF)
            m = jnp.max(s, axis=-1, keepdims=True)
            e = jnp.exp(s - m)
            l = jnp.sum(e, axis=-1, keepdims=True)
            o_ref[blk, :] = _dot(e / l, v_ref[:kext, :])
            lse_ref[0, blk, :] = m + jnp.log(l)

    return pl.pallas_call(
        body, name=name, grid=(HEADS,), in_specs=[_col(LANES)] * 3,
        out_specs=[_col(LANES), pl.BlockSpec((1, S, 1), lambda h: (h, 0, 0))],
        out_shape=[jax.ShapeDtypeStruct((S, MIX_IN), F32), jax.ShapeDtypeStruct((HEADS, S, 1), F32)],
        compiler_params=_params())(q, k, v)


def mla_bwd(name, q, k, v, o, do, lse, scale, tq=512):
    S = q.shape[0]

    def body(q_ref, k_ref, v_ref, o_ref, do_ref, lse_ref, dq_ref, dkv_ref, dkpe_ref, dk_acc, dv_acc):
        dk_acc[...] = jnp.zeros_like(dk_acc)
        dv_acc[...] = jnp.zeros_like(dv_acc)
        for i in range(S // tq):
            kext = (i + 1) * tq
            blk = slice(i * tq, kext)
            qi, kk, vv = q_ref[blk, :], k_ref[:kext, :], v_ref[:kext, :]
            doi = do_ref[blk, :]
            s = _dot(qi, kk, "nt") * scale
            row = lax.broadcasted_iota(jnp.int32, s.shape, 0) + i * tq
            col = lax.broadcasted_iota(jnp.int32, s.shape, 1)
            p = jnp.where(col <= row, jnp.exp(s - lse_ref[0, blk, :]), 0.0)
            dp = _dot(doi, vv, "nt")
            delta = jnp.sum(doi * o_ref[blk, :], axis=-1, keepdims=True)
            ds = p * (dp - delta) * scale
            dq_ref[blk, :] = _dot(ds, kk)
            dk_acc[:kext, :] += _dot(ds, qi, "tn")
            dv_acc[:kext, :] += _dot(p, doi, "tn")
        dk = dk_acc[...]
        lane = _lane(dk.shape)
        dkv_ref[...] = jnp.where(lane < NOPE, dk, 0.0) + dv_acc[...]
        dkpe = jnp.where((lane >= KPE_LO) & (lane < KPE_LO + ROPE), dk, 0.0)
        h = pl.program_id(0)

        @pl.when(h == 0)
        def _():
            dkpe_ref[...] = dkpe

        @pl.when(h > 0)
        def _():
            dkpe_ref[...] += dkpe

    return pl.pallas_call(
        body, name=name, grid=(HEADS,),
        in_specs=[_col(LANES)] * 5 + [pl.BlockSpec((1, S, 1), lambda h: (h, 0, 0))],
        out_specs=[_col(LANES), _col(LANES), pl.BlockSpec((S, LANES), lambda h: (0, 0))],
        out_shape=[jax.ShapeDtypeStruct((S, HEADS * LANES), F32), jax.ShapeDtypeStruct((S, HEADS * LANES), F32),
                   jax.ShapeDtypeStruct((S, LANES), F32)],
        scratch_shapes=[pltpu.VMEM((S, LANES), F32), pltpu.VMEM((S, LANES), F32)],
        compiler_params=_params())(q, k, v, o, do, lse)


def _band_blocks(L, tq):
    return [(i * tq, (i + 1) * tq, max(0, i * tq - SPAN)) for i in range(L // tq)]


def _band_mask(q0, q1, k0):
    shape = (q1 - q0, q1 - k0)
    dist = (lax.broadcasted_iota(jnp.int32, shape, 0) + q0) - (lax.broadcasted_iota(jnp.int32, shape, 1) + k0)
    return (dist >= 0) & (dist <= SPAN)


def _class_rows(r, dil, lo, hi):
    return pl.ds(r + dil * lo, hi - lo, stride=dil) if dil > 1 else pl.ds(lo, hi - lo)


def _pair_col(col0=0):
    return pl.BlockSpec((SEQ, LANES), lambda j: (0, col0 // LANES + j))


def band_fwd(name, q, k, v, dil):
    S = q.shape[0]
    L = S // dil
    tq = min(L, 512)
    scale = DIL_DIM ** -0.5

    def body(q_ref, k_ref, v_ref, o_ref, lse_ref):
        for r in range(dil):
            for q0, q1, k0 in _band_blocks(L, tq):
                qrows, krows = _class_rows(r, dil, q0, q1), _class_rows(r, dil, k0, q1)
                qb, kb, vb = q_ref[qrows, :].astype(MXU_DTYPE), k_ref[krows, :], v_ref[krows, :]
                lo = _lane(qb.shape) < DIL_DIM
                band = _band_mask(q0, q1, k0)
                res = []
                for m in (lo, ~lo):
                    s = _dot(jnp.where(m, qb, jnp.zeros_like(qb)), kb, "nt") * scale
                    s = jnp.where(band, s, NEG_INF)
                    mx = jnp.max(s, axis=-1, keepdims=True)
                    e = jnp.exp(s - mx)
                    l = jnp.sum(e, axis=-1, keepdims=True)
                    res.append((_dot(e / l, vb), mx + jnp.log(l)))
                o_ref[qrows, :] = jnp.where(lo, res[0][0], res[1][0])
                lse_ref[qrows, :] = jnp.where(lo, res[0][1], res[1][1])

    return pl.pallas_call(
        body, name=name, grid=(DIL_WIDTH // LANES,), in_specs=[_pair_col()] * 2 + [_pair_col(P_VD)], out_specs=[_pair_col()] * 2,
        out_shape=[jax.ShapeDtypeStruct((S, DIL_WIDTH), F32)] * 2, compiler_params=_params())(q, k, v)


def band_bwd(name, q, k, v, lse, lse_mix, o_cat, do_cat, dil):
    S = q.shape[0]
    L = S // dil
    tq = min(L, 512)
    scale = DIL_DIM ** -0.5

    def body(q_ref, k_ref, v_ref, lse_ref, mix_ref, o_ref, do_ref, dq_ref, dk_ref, dv_ref):
        dk_ref[...] = jnp.zeros_like(dk_ref)
        dv_ref[...] = jnp.zeros_like(dv_ref)
        for r in range(dil):
            for q0, q1, k0 in _band_blocks(L, tq):
                qrows, krows = _class_rows(r, dil, q0, q1), _class_rows(r, dil, k0, q1)
                qb, kb, vb = q_ref[qrows, :].astype(MXU_DTYPE), k_ref[krows, :], v_ref[krows, :]
                lse_p, dout = lse_ref[qrows, :], do_ref[qrows, :]
                wgt = jnp.exp(lse_p - mix_ref[qrows, :])
                dd = dout * o_ref[qrows, :]
                lo = _lane(qb.shape) < DIL_DIM
                band = _band_mask(q0, q1, k0)
                dqs = []
                for m, c0 in ((lo, 0), (~lo, DIL_DIM)):
                    w_h = wgt[:, c0:c0 + 1]
                    big_d = jnp.sum(jnp.where(m, dd, 0.0), axis=-1, keepdims=True)
                    qm = jnp.where(m, qb, jnp.zeros_like(qb))
                    s = _dot(qm, kb, "nt") * scale
                    p = jnp.where(band, jnp.exp(s - lse_p[:, c0:c0 + 1]), 0.0)
                    dom = jnp.where(m, dout, 0.0) * w_h
                    ds = p * (_dot(dom, vb, "nt") - w_h * big_d) * scale
                    dqs.append(_dot(ds, kb))
                    dk_ref[krows, :] += _dot(ds, qm, "tn")
                    dv_ref[krows, :] += _dot(p, dom, "tn")
                dq_ref[qrows, :] = jnp.where(lo, dqs[0], dqs[1])

    cat = _pair_col(HEADS * LANES)
    return pl.pallas_call(
        body, name=name, grid=(DIL_WIDTH // LANES,),
        in_specs=[_pair_col()] * 2 + [_pair_col(P_VD)] + [_pair_col()] * 2 + [cat] * 2, out_specs=[_pair_col()] * 3,
        out_shape=[jax.ShapeDtypeStruct((S, DIL_WIDTH), F32)] * 3,
        compiler_params=_params())(q, k, v, lse, lse_mix, o_cat, do_cat)


def combine_fwd(name, outs, lses, o_cat, tm=512):
    S = outs[0].shape[0]

    def body(o1, o2, o3, l1, l2, l3, cat_in, cat_out, mix_ref):
        ls = [l1[...], l2[...], l3[...]]
        m = jnp.maximum(jnp.maximum(ls[0], ls[1]), ls[2])
        e = [jnp.exp(l - m) for l in ls]
        den = e[0] + e[1] + e[2]
        cat_out[...] = (e[0] / den) * o1[...] + (e[1] / den) * o2[...] + (e[2] / den) * o3[...]
        mix_ref[...] = m + jnp.log(den)

    row = pl.BlockSpec((tm, DIL_WIDTH), lambda i: (i, 0))
    return pl.pallas_call(
        body, name=name, grid=(S // tm,), in_specs=[row] * 6 + [pl.BlockSpec(memory_space=pl.ANY)],
        out_specs=[pl.BlockSpec((tm, DIL_WIDTH), lambda i: (i, HEADS * LANES // DIL_WIDTH)), row],
        out_shape=[jax.ShapeDtypeStruct(o_cat.shape, F32), jax.ShapeDtypeStruct((S, DIL_WIDTH), F32)],
        input_output_aliases={6: 0}, compiler_params=_params())(*outs, *lses, o_cat)


def _shift_down(u, n):
    t = lax.broadcasted_iota(jnp.int32, u.shape, 0)
    return jnp.where(t >= n, pltpu.roll(u, n, axis=0), 0.0)


def _shift_up(u, n):
    rows = u.shape[0]
    t = lax.broadcasted_iota(jnp.int32, u.shape, 0)
    return jnp.where(t < rows - n, pltpu.roll(u, rows - n, axis=0), 0.0)


def _conv(u, w, b):
    return w[2:3, :] * u + w[1:2, :] * _shift_down(u, 1) + w[0:1, :] * _shift_down(u, 2) + b


CONV_TC = 256
CONV_NB = D_FF // CONV_TC


def _pair_spec(rows):
    return pl.BlockSpec((rows, 2 * CONV_TC), lambda j: (0, j))


def conv_glu_fwd(name, up, w_conv, b_conv):
    S = up.shape[0]

    def body(u_ref, w_ref, b_ref, act_ref):
        u = _conv(u_ref[...], w_ref[...], b_ref[...])
        act_ref[...] = (_silu(u[:, :CONV_TC]) * u[:, CONV_TC:]).astype(act_ref.dtype)

    return pl.pallas_call(
        body, name=name, grid=(CONV_NB,), in_specs=[_pair_spec(S), _pair_spec(3), _pair_spec(1)],
        out_specs=pl.BlockSpec((S, CONV_TC), lambda j: (0, j)), out_shape=jax.ShapeDtypeStruct((S, D_FF), MXU_DTYPE),
        compiler_params=_params())(up, w_conv, b_conv)


def conv_glu_bwd(name, up, w_conv, b_conv, dact):
    S = up.shape[0]

    def body(u_ref, w_ref, b_ref, da_ref, dup_ref, dw_ref, db_ref):
        uin, w = u_ref[...], w_ref[...]
        u = _conv(uin, w, b_ref[...])
        gate, val, da = u[:, :CONV_TC], u[:, CONV_TC:], da_ref[...]
        sig = 1.0 / (1.0 + jnp.exp(-gate))
        du = jnp.concatenate([da * val * (sig * (1.0 + gate * (1.0 - sig))), da * (gate * sig)], axis=1)
        dup_ref[...] = (w[2:3, :] * du + w[1:2, :] * _shift_up(du, 1) + w[0:1, :] * _shift_up(du, 2)).astype(dup_ref.dtype)
        dw_ref[...] = jnp.concatenate([_colsum(du * _shift_down(uin, 2)), _colsum(du * _shift_down(uin, 1)), _colsum(du * uin)], axis=0)
        db_ref[...] = _colsum(du)

    return pl.pallas_call(
        body, name=name, grid=(CONV_NB,),
        in_specs=[_pair_spec(S), _pair_spec(3), _pair_spec(1), pl.BlockSpec((S, CONV_TC), lambda j: (0, j))],
        out_specs=[_pair_spec(S), _pair_spec(3), _pair_spec(1)],
        out_shape=[jax.ShapeDtypeStruct((S, 2 * D_FF), MXU_DTYPE), jax.ShapeDtypeStruct((3, 2 * D_FF), F32),
                   jax.ShapeDtypeStruct((1, 2 * D_FF), F32)],
        compiler_params=_params())(up, w_conv, b_conv, dact)


def adamw(name, w, parts, m, v, tr=None):
    R, C = w.shape
    tr = tr or R
    assert R % tr == 0
    c1 = 1.0 - ADAM_B1 ** ADAM_STEP
    c2 = 1.0 - ADAM_B2 ** ADAM_STEP
    np_ = len(parts)

    def body(*refs):
        w_ref, m_ref, v_ref = refs[0], refs[1 + np_], refs[2 + np_]
        go_ref, d_ref, mo_ref, vo_ref = refs[3 + np_:]
        terms = []
        for part, ref in zip(parts, refs[1:1 + np_], strict=True):
            terms += [ref[...]] if part.ndim == 2 else [ref[p] for p in range(part.shape[0])]
        g = terms[0].astype(F32)
        for term in terms[1:]:
            g = g + term.astype(F32)
        m2 = ADAM_B1 * m_ref[...] + (1.0 - ADAM_B1) * g
        v2 = ADAM_B2 * v_ref[...] + (1.0 - ADAM_B2) * (g * g)
        go_ref[...] = g
        mo_ref[...] = m2
        vo_ref[...] = v2
        d_ref[...] = -ADAM_LR * ((m2 / c1) / (jnp.sqrt(v2 / c2) + ADAM_EPS) + ADAM_WD * w_ref[...])

    blk = pl.BlockSpec((tr, C), lambda i: (i, 0))
    part_specs = [blk if p.ndim == 2 else pl.BlockSpec((p.shape[0], tr, C), lambda i: (0, i, 0)) for p in parts]
    return pl.pallas_call(
        body, name=name, grid=(R // tr,),
        in_specs=[blk] + part_specs + [blk, blk], out_specs=[blk] * 4,
        out_shape=[jax.ShapeDtypeStruct((R, C), F32)] * 4, compiler_params=_params())(w, *parts, m, v)


def _place():
    return lax.axis_index("x"), lax.axis_index("y"), lax.axis_index("c")


def all_gather(name, arrs, after=None):
    n = len(arrs)
    deps = [] if after is None else [after]

    def body(*refs):
        ins, outs = refs[:n], refs[n + len(deps):2 * n + len(deps)]
        send_sems, recv_sems, local_sems = refs[2 * n + len(deps):]
        x, y, c = _place()
        me, sibling = (x, y, c), (x, y, 1 - c)
        chips = [(1 - x, y), (x, 1 - y), (1 - x, 1 - y)]
        sends = []
        for t in range(n):
            out = outs[t]

            def slot(px, py, pc, out=out):
                return out.at[4 * px + 2 * py + pc]

            def copy(k, block, to, src=None, t=t, slot=slot):
                return pltpu.make_async_remote_copy(
                    src_ref=slot(*block) if src is None else src, dst_ref=slot(*block),
                    send_sem=send_sems.at[7 * t + k], recv_sem=recv_sems.at[7 * t + k],
                    device_id=to, device_id_type=MESH_ID)

            mine = pltpu.make_async_copy(ins[t], slot(*me), local_sems.at[t])
            mine.start()
            first = [copy(0, me, sibling, src=ins[t])]
            first += [copy(1 + j, me, (*chip, c), src=ins[t]) for j, chip in enumerate(chips)]
            for cp in first:
                cp.start()
            sends.append((mine, first, copy))
        for t in range(n):
            mine, first, copy = sends[t]
            passed = [copy(4 + j, (*chip, c), sibling) for j, chip in enumerate(chips)]
            for j, chip in enumerate(chips):
                copy(1 + j, (*chip, c), me).wait_recv()
                passed[j].start()
            copy(0, sibling, me).wait_recv()
            for j, chip in enumerate(chips):
                copy(4 + j, (*chip, 1 - c), me).wait_recv()
            for cp in first + passed:
                cp.wait_send()
            mine.wait()

    any_spec = pl.BlockSpec(memory_space=pl.ANY)
    res = pl.pallas_call(
        body, name=name, in_specs=[any_spec] * (n + len(deps)), out_specs=[any_spec] * n,
        out_shape=[jax.ShapeDtypeStruct((N_DEV,) + a.shape, a.dtype) for a in arrs],
        scratch_shapes=[pltpu.SemaphoreType.DMA((7 * n,)), pltpu.SemaphoreType.DMA((7 * n,)), pltpu.SemaphoreType.DMA((n,))],
        compiler_params=pltpu.CompilerParams(has_side_effects=True))(*arrs, *deps)
    return list(res)


HBM_SPEC = pl.BlockSpec(memory_space=pltpu.HBM)
SEM_SPEC = pl.BlockSpec(memory_space=pltpu.SEMAPHORE)
DATAFLOW = pltpu.SideEffectType.DATAFLOW_SIDE_EFFECTING


def _exchange_copies(srcs, lands, send_sems, recv_sems, gather):
    x, y, c = _place()
    me = 4 * x + 2 * y + c
    out = []
    for t, (src, land) in enumerate(zip(srcs, lands, strict=True)):
        for k in range(1, N_DEV):
            px, py, pc = x ^ (k >> 2), y ^ ((k >> 1) & 1), c ^ (k & 1)
            out.append(pltpu.make_async_remote_copy(
                src_ref=src if gather else src.at[4 * px + 2 * py + pc],
                dst_ref=land.at[me] if gather else land.at[k - 1],
                send_sem=send_sems.at[7 * t + k - 1], recv_sem=recv_sems.at[7 * t + k - 1],
                device_id=(px, py, pc), device_id_type=MESH_ID))
    return out


def exchange_start(name, arrs, gather, after=None):
    n = len(arrs)
    lands = [lax.empty(((N_DEV,) + a.shape) if gather else ((N_DEV - 1,) + a.shape[1:]), a.dtype) for a in arrs]
    deps = [] if after is None else [after]

    def body(*refs):
        srcs, land_refs = refs[:n], refs[n:2 * n]
        send_sems, recv_sems = refs[2 * n + len(deps)], refs[2 * n + len(deps) + 1]
        token = refs[-1]
        for cp in _exchange_copies(srcs, land_refs, send_sems, recv_sems, gather):
            cp.start()
        token[...] = jnp.zeros_like(token)

    hbm = lambda a: pltpu.HBM(a.shape, a.dtype)
    res = pl.pallas_call(
        body, name=name,
        out_shape=(pltpu.SemaphoreType.DMA((7 * n,)), pltpu.SemaphoreType.DMA((7 * n,)), *[hbm(a) for a in arrs],
                   *[hbm(l) for l in lands], jax.ShapeDtypeStruct((8, 128), F32)),
        in_specs=[HBM_SPEC] * (2 * n) + [pl.BlockSpec(memory_space=pl.ANY)] * len(deps),
        out_specs=(SEM_SPEC, SEM_SPEC, *[HBM_SPEC] * (2 * n), pl.BlockSpec(memory_space=pltpu.VMEM)),
        input_output_aliases={i: 2 + i for i in range(2 * n)},
        compiler_params=pltpu.CompilerParams(has_side_effects=DATAFLOW),
    )(*[pltpu.with_memory_space_constraint(a, pltpu.HBM) for a in arrs + lands], *deps)
    return res[0], res[1], list(res[2:2 + n]), list(res[2 + n:2 + 2 * n]), res[-1]


def exchange_wait(name, started, gather, after):
    send_sems, recv_sems, srcs, lands, _ = started
    n = len(srcs)

    def body(*refs):
        src_refs, land_refs = refs[:n], refs[n:2 * n]
        copies = _exchange_copies(src_refs, land_refs, refs[2 * n], refs[2 * n + 1], gather)
        for cp in copies:
            cp.wait_send()
        for cp in copies:
            cp.wait_recv()

    hbm = lambda a: pltpu.HBM(a.shape, a.dtype)
    res = pl.pallas_call(
        body, name=name, out_shape=tuple(hbm(a) for a in srcs + lands),
        in_specs=[HBM_SPEC] * (2 * n) + [SEM_SPEC, SEM_SPEC, pl.BlockSpec(memory_space=pl.ANY)],
        out_specs=tuple([HBM_SPEC] * (2 * n)), input_output_aliases={i: i for i in range(2 * n)},
        compiler_params=pltpu.CompilerParams(has_side_effects=DATAFLOW),
    )(*srcs, *lands, send_sems, recv_sems, after)
    return list(res[:n]), list(res[n:])


def _gather_cols(stack):
    p, k, n = stack.shape
    return stack.transpose(1, 0, 2).reshape(k, p * n)


def _scatter_cols(full):
    k, n = full.shape
    return full.reshape(k, N_DEV, n // N_DEV).transpose(1, 0, 2)


def _gather_rows(stack):
    p, r, n = stack.shape
    return stack.reshape(p * r, n)


def _scatter_rows(full):
    r, n = full.shape
    return full.reshape(N_DEV, r // N_DEV, n)


_IN_NAT = Q_LORA + KV_LORA


def to_kernel_layout(name, w):
    if name == "w_in":
        z = lambda n: jnp.zeros((w.shape[0], n), w.dtype)
        return jnp.concatenate([w[:, :_IN_NAT], z(KPE_LO), w[:, _IN_NAT:_IN_NAT + ROPE], z(LANES - KPE_LO - ROPE),
                                w[:, _IN_NAT + ROPE:]], axis=1)
    if name == "w_q_b":
        return jnp.pad(w.reshape(w.shape[0], HEADS, NOPE + ROPE), ((0, 0), (0, 0), (0, LANES - NOPE - ROPE))).reshape(w.shape[0], HEADS * LANES)
    if name == "w_o":
        mla = jnp.pad(w[:HEADS * NOPE].reshape(HEADS, NOPE, -1), ((0, 0), (LANES - NOPE, 0), (0, 0))).reshape(HEADS * LANES, -1)
        return jnp.concatenate([mla, w[HEADS * NOPE:]], axis=0)
    if name in ("w_up", "w_conv", "b_conv"):
        return w.reshape(w.shape[0], 2, CONV_NB, CONV_TC).transpose(0, 2, 1, 3).reshape(w.shape[0], 2 * D_FF)
    return w


def from_kernel_layout(name, g):
    if name == "w_in":
        return jnp.concatenate([g[:, :_IN_NAT], g[:, P_KPE + KPE_LO:P_KPE + KPE_LO + ROPE], g[:, P_QD:]], axis=1)
    if name == "w_q_b":
        return g.reshape(g.shape[0], HEADS, LANES)[:, :, :NOPE + ROPE].reshape(g.shape[0], HEADS * (NOPE + ROPE))
    if name == "w_o":
        mla = g[:HEADS * LANES].reshape(HEADS, LANES, -1)[:, LANES - NOPE:, :].reshape(HEADS * NOPE, -1)
        return jnp.concatenate([mla, g[HEADS * LANES:]], axis=0)
    if name in ("w_up", "w_conv", "b_conv"):
        return g.reshape(g.shape[0], CONV_NB, 2, CONV_TC).transpose(0, 2, 1, 3).reshape(g.shape[0], 2 * D_FF)
    return g


SMALL = (("loss", 1), ("b_ada", 6 * D_MODEL), ("g_mix_norm", D_MODEL), ("g_q_lat", Q_LORA), ("g_kv_lat", KV_LORA),
         ("g_mla_q_nope", NOPE), ("g_mla_q_pe", ROPE), ("g_mla_k_nope", NOPE), ("g_mla_k_pe", ROPE),
         ("g_dil_q", DIL_DIM), ("g_dil_k", DIL_DIM), ("g_ffn_norm", D_MODEL), ("b_conv", 2 * D_FF))
SMALL_ROWS = 16
SMALL_COLS = 1024


def _pack_small(values):
    parts = [values[name].reshape(-1).astype(F32) if name in values else jnp.zeros((n,), F32) for name, n in SMALL]
    flat = jnp.concatenate(parts)
    flat = jnp.pad(flat, (0, SMALL_ROWS * SMALL_COLS - flat.shape[0]))
    return flat.reshape(SMALL_ROWS, SMALL_COLS)


def _unpack_small(packed):
    flat = packed.reshape(-1)
    out, off = {}, 0
    for name, n in SMALL:
        out[name] = flat[off:off + n].reshape(1, n)
        off += n
    return out


def _local_step(x, pos, mod, target, w, fetch, emit):
    S = SEQ
    sh1, sc1, g1, sh2, sc2, g2 = [mod[:, i * D_MODEL:(i + 1) * D_MODEL] for i in range(6)]
    zeros = lambda n: jnp.zeros((1, n), F32)
    g_q = jnp.concatenate([w["g_mla_q_nope"], w["g_mla_q_pe"], zeros(LANES - NOPE - ROPE)], axis=1)
    g_k = jnp.concatenate([w["g_mla_k_nope"], zeros(LANES - NOPE)], axis=1)
    g_kpe = jnp.concatenate([zeros(KPE_LO), w["g_mla_k_pe"], zeros(LANES - KPE_LO - ROPE)], axis=1)
    g_dq = jnp.concatenate([w["g_dil_q"]] * 2, axis=1)
    g_dk = jnp.concatenate([w["g_dil_k"]] * 2, axis=1)
    b_conv = to_kernel_layout("b_conv", w["b_conv"])

    def inv_freq(d):
        return jnp.power(ROPE_THETA, -2.0 * jnp.arange(d // 2, dtype=F32) / d)

    f_mla = jnp.concatenate([jnp.zeros((KPE_LO,), F32), inv_freq(ROPE), inv_freq(ROPE), jnp.zeros((LANES - KPE_LO - ROPE,), F32)])
    f_dil = jnp.concatenate([inv_freq(DIL_DIM)] * 4)

    def tables_fn(rows, params):
        (p,), (fa, fb) = rows, params
        return [jnp.cos(p * fa), jnp.sin(p * fa), jnp.cos(p * fb), jnp.sin(p * fb)], []

    cos_m, sin_m, cos_d, sin_d = rowwise("rope_tables", tables_fn, [pos], [f_mla.reshape(1, LANES), f_dil.reshape(1, LANES)],
                                         [(LANES, F32)] * 4)
    tables = [cos_m, sin_m, cos_d, sin_d]
    H_M, H_D = ROPE // 2, DIL_DIM // 2

    def ln1_fn(rows, params):
        (xv,), (g, sc, sh) = rows, params
        y, _, _ = _rms(xv, g)
        return [y * (1.0 + sc) + sh], []

    (h,) = rowwise("ln1_fwd", ln1_fn, [x], [w["g_mix_norm"], sc1, sh1], [(D_MODEL, MXU_DTYPE)])
    w_in = fetch("w_in", h)
    proj = matmul("proj_fwd", h, w_in, "nn", tm=512)

    def post_fn(rows, params):
        (pv, cm, sm, cd, sd), (gq, gkv, gkp, gdq, gdk) = rows, params
        kper = _rope(_grms(pv[:, P_KPE:P_QD], gkp, KPE_GROUPS)[0], cm, sm, H_M)
        qd = [_rope(_grms(c, gdq, DIL_GROUPS)[0], cd, sd, H_D) for c in _chunks(pv[:, P_QD:P_KD])]
        kd = [_rope(_grms(c, gdk, DIL_GROUPS)[0], cd, sd, H_D) for c in _chunks(pv[:, P_KD:P_VD])]
        return [_rms(pv[:, P_QLAT:P_KVLAT], gq)[0], _rms(pv[:, P_KVLAT:P_KPE], gkv)[0], kper,
                jnp.concatenate(qd, axis=1), jnp.concatenate(kd, axis=1)], []

    post_params = [w["g_q_lat"], w["g_kv_lat"], g_kpe, g_dq, g_dk]
    qln, kvn, kper, qd_r, kd_r = rowwise(
        "proj_post", post_fn, [proj] + tables, post_params,
        [(Q_LORA, MXU_DTYPE), (KV_LORA, MXU_DTYPE), (LANES, MXU_DTYPE)] + [(DIL_WIDTH, F32)] * 2, tm=256)
    w_q_b, w_kv_b = fetch("w_q_b", qln), fetch("w_kv_b", kvn)
    q = matmul("q_fwd", qln, w_q_b, "nn", tm=1024)
    kv = matmul("kv_fwd", kvn, w_kv_b, "nn", tm=1024)

    def mla_prep_fn(rows, params):
        (qv, kvv, kp, cm, sm), (gq, gk) = rows, params
        value_lanes = _lane(kp.shape) >= NOPE
        qs, ks, vs = [], [], []
        for qc, kc in zip(_chunks(qv), _chunks(kvv), strict=True):
            qs.append(_rope(_grms(qc, gq, Q_GROUPS)[0], cm, sm, H_M))
            ks.append(_grms(kc, gk, K_GROUPS)[0] + kp)
            vs.append(jnp.where(value_lanes, kc, 0.0))
        return [jnp.concatenate(t, axis=1) for t in (qs, ks, vs)], []

    q_mla, k_mla, v_mla = rowwise("mla_prep", mla_prep_fn, [q, kv, kper, cos_m, sin_m], [g_q, g_k],
                                  [(HEADS * LANES, MXU_DTYPE)] * 3, tm=256)
    mla_scale = (NOPE + ROPE) ** -0.5
    o_cat, lse_mla = mla_fwd("mla_fwd", q_mla, k_mla, v_mla, mla_scale)

    band = [band_fwd(f"band{dil}_fwd", qd_r, kd_r, proj, dil) for dil in DILATIONS]
    o_cat, lse_mix = combine_fwd("dil_combine", [b[0] for b in band], [b[1] for b in band], o_cat)
    w_o = fetch("w_o", o_cat)
    mix = matmul("mix_fwd", o_cat, w_o, "nn", tm=512)

    def mid_fn(rows, params):
        (xv, mx), (gate1, g, sc, sh) = rows, params
        x1 = xv + gate1 * mx
        y, _, _ = _rms(x1, g)
        return [x1, y * (1.0 + sc) + sh], []

    x1, h2 = rowwise("mid_fwd", mid_fn, [x, mix], [g1, w["g_ffn_norm"], sc2, sh2], [(D_MODEL, F32), (D_MODEL, MXU_DTYPE)])
    w_up, w_conv, w_down = fetch("w_up", h2), fetch("w_conv", h2), fetch("w_down", h2)
    up = matmul("up_fwd", h2, w_up, "nn", tm=512, tn=1408)
    act = conv_glu_fwd("conv_fwd", up, w_conv, b_conv)
    dn = matmul("down_fwd", act, w_down, "nn", tm=512)

    def final_fn(rows, params):
        (x1v, dnv, tgt), (gate2,) = rows, params
        r = x1v + gate2 * dnv - tgt
        dy = r * (1.0 / D_MODEL)
        loss = jnp.sum(_colsum(r * r), axis=-1, keepdims=True) * (0.5 / D_MODEL)
        return [dy, gate2 * dy], [loss, _colsum(dy * dnv)]

    dy, d_dn, loss, dg2 = rowwise("loss_head", final_fn, [x1, dn, target], [g2], [(D_MODEL, F32), (D_MODEL, MXU_DTYPE)],
                                  [1, D_MODEL])
    emit("w_down", matmul("down_wgrad", act, d_dn, "tn", tm=1408, out_dtype=MXU_DTYPE))
    dact = matmul("down_dgrad", d_dn, w_down, "nt", tm=512)
    dup, g_w_conv, g_b_conv = conv_glu_bwd("conv_bwd", up, w_conv, b_conv, dact)
    emit("w_conv", g_w_conv)
    sent = emit("w_up", matmul("up_wgrad", h2, dup, "tn", tn=1408, out_dtype=MXU_DTYPE))
    dh2 = matmul("up_dgrad", dup, w_up, "nt", tm=512, tk=2816, dep=sent)

    def mid_bwd_fn(rows, params):
        (dh2v, dyv, x1v, mx), (gate1, g, sc) = rows, params
        yn, n, rstd = _rms(x1v, g)
        dx_n, dg = _rms_bwd(dh2v * (1.0 + sc), n, rstd, g)
        dx1 = dyv + dx_n
        return [dx1, gate1 * dx1], [dg, _colsum(dh2v * yn), _colsum(dh2v), _colsum(dx1 * mx)]

    dx1, dmix, dg_ffn, dsc2, dsh2, dg1 = rowwise(
        "mid_bwd", mid_bwd_fn, [dh2, dy, x1, mix], [g1, w["g_ffn_norm"], sc2], [(D_MODEL, F32), (D_MODEL, MXU_DTYPE)],
        [D_MODEL] * 4)

    sent = emit("w_o", matmul("mix_wgrad", o_cat, dmix, "tn", tm=512, out_dtype=MXU_DTYPE))
    do_cat = matmul("mix_dgrad", dmix, w_o, "nt", tm=512, dep=sent)
    dband = [band_bwd(f"band{dil}_bwd", qd_r, kd_r, proj, b[1], lse_mix, o_cat, do_cat, dil) for dil, b in zip(DILATIONS, band)]
    dq_mla, dkv_mla, dkper = mla_bwd("mla_bwd", q_mla, k_mla, v_mla, o_cat, do_cat, lse_mla, mla_scale)

    def mla_prep_bwd_fn(rows, params):
        (dqv, dkvv, qv, kvv, cm, sm), (gq, gk) = rows, params
        nope_lanes = _lane(cm.shape) < NOPE
        dqs, dkvs, dgq, dgk = [], [], 0.0, 0.0
        for dqc, dkc, qc, kc in zip(_chunks(dqv), _chunks(dkvv), _chunks(qv), _chunks(kvv), strict=True):
            _, n, rstd = _grms(qc, gq, Q_GROUPS)
            dx, dg = _grms_bwd(_rope_bwd(dqc, cm, sm, H_M), n, rstd, gq, Q_GROUPS)
            dqs.append(dx)
            dgq = dgq + dg
            _, n, rstd = _grms(kc, gk, K_GROUPS)
            dx, dg = _grms_bwd(dkc, n, rstd, gk, K_GROUPS)
            dkvs.append(jnp.where(nope_lanes, dx, dkc))
            dgk = dgk + dg
        return [jnp.concatenate(dqs, axis=1), jnp.concatenate(dkvs, axis=1)], [dgq, dgk]

    dq, dkv, dg_q, dg_k = rowwise("mla_prep_bwd", mla_prep_bwd_fn, [dq_mla, dkv_mla, q, kv, cos_m, sin_m], [g_q, g_k],
                                  [(HEADS * LANES, MXU_DTYPE)] * 2, [LANES, LANES], tm=256)
    emit("w_q_b", matmul("q_wgrad", qln, dq, "tn", out_dtype=MXU_DTYPE))
    emit("w_kv_b", matmul("kv_wgrad", kvn, dkv, "tn", out_dtype=MXU_DTYPE))
    dqln = matmul("q_dgrad", dq, w_q_b, "nt", tm=1024)
    dkvn = matmul("kv_dgrad", dkv, w_kv_b, "nt", tm=1024)

    def pre_bwd_fn(rows, params):
        dql, dkvl, dkp = rows[0:3]
        dqd_, dkd_, dvd_ = [rows[3 + 3 * i] + rows[4 + 3 * i] + rows[5 + 3 * i] for i in range(3)]
        pv, cm, sm, cd, sd = rows[12:]
        gq, gkv, gkp, gdq, gdk = params
        r_q = _norm_bwd(dql, pv[:, P_QLAT:P_KVLAT], gq)
        r_kv = _norm_bwd(dkvl, pv[:, P_KVLAT:P_KPE], gkv)
        _, n, rstd = _grms(pv[:, P_KPE:P_QD], gkp, KPE_GROUPS)
        r_kp = _grms_bwd(_rope_bwd(dkp, cm, sm, H_M), n, rstd, gkp, KPE_GROUPS)
        outs, dgs = [r_q[0], r_kv[0], r_kp[0]], []
        for dval, lo, g in ((dqd_, P_QD, gdq), (dkd_, P_KD, gdk)):
            dg_sum = 0.0
            for dc, xc in zip(_chunks(dval), _chunks(pv[:, lo:lo + DIL_WIDTH]), strict=True):
                _, n, rstd = _grms(xc, g, DIL_GROUPS)
                dx, dg = _grms_bwd(_rope_bwd(dc, cd, sd, H_D), n, rstd, g, DIL_GROUPS)
                outs.append(dx)
                dg_sum = dg_sum + dg
            dgs.append(dg_sum)
        return [jnp.concatenate(outs + [dvd_], axis=1)], [r_q[1], r_kv[1], r_kp[1]] + dgs

    dproj, dg_q_lat, dg_kv_lat, dg_kpe, dg_dq, dg_dk = rowwise(
        "proj_pre_bwd", pre_bwd_fn,
        [dqln, dkvn, dkper] + [d[i] for i in range(3) for d in dband] + [proj] + tables, post_params,
        [(P_END, MXU_DTYPE)], [Q_LORA, KV_LORA, LANES, LANES, LANES], tm=256)
    sent = emit("w_in", matmul("proj_wgrad", h, dproj, "tn", tm=256, out_dtype=MXU_DTYPE))
    dh = matmul("proj_dgrad", dproj, w_in, "nt", tm=512, dep=sent)

    def ln1_bwd_fn(rows, params):
        (dhv, dres, xv), (g, sc) = rows, params
        yn, n, rstd = _rms(xv, g)
        dx_n, dg = _rms_bwd(dhv * (1.0 + sc), n, rstd, g)
        return [dres + dx_n], [dg, _colsum(dhv * yn), _colsum(dhv)]

    grad_x, dg_mix, dsc1, dsh1 = rowwise("ln1_bwd", ln1_bwd_fn, [dh, dx1, x], [w["g_mix_norm"], sc1], [(D_MODEL, F32)],
                                         [D_MODEL] * 3)
    dmod = jnp.concatenate([dsh1, dsc1, dg1, dsh2, dsc2, dg2], axis=-1)
    small = {"loss": loss, "b_ada": dmod, "g_mix_norm": dg_mix, "g_q_lat": dg_q_lat, "g_kv_lat": dg_kv_lat,
             "g_mla_q_nope": dg_q[:, :NOPE], "g_mla_q_pe": dg_q[:, NOPE:NOPE + ROPE], "g_mla_k_nope": dg_k[:, :NOPE],
             "g_mla_k_pe": dg_kpe[:, KPE_LO:KPE_LO + ROPE], "g_dil_q": dg_dq[:, :DIL_DIM] + dg_dq[:, DIL_DIM:],
             "g_dil_k": dg_dk[:, :DIL_DIM] + dg_dk[:, DIL_DIM:], "g_ffn_norm": dg_ffn,
             "b_conv": from_kernel_layout("b_conv", g_b_conv)}
    return grad_x, small


COL_SHARDED = ("w_in", "w_q_b", "w_kv_b", "w_up", "w_conv")
ROW_SHARDED = ("w_o", "w_down")
ADAM_TILE = {"w_ada": 256, "w_in": 256, "w_up": 256, "w_down": 176}
GATHER_GROUPS = (("w_in", "w_q_b", "w_kv_b"), ("w_o", "w_up", "w_conv", "w_down"))
SCATTER_GROUPS = (("w_down", "w_conv", "w_up"), ("w_o",), ("w_q_b", "w_kv_b", "w_in"))
OUT_WEIGHTS = ("w_ada", "b_ada", "g_mix_norm", "w_in", "g_q_lat", "w_q_b", "g_kv_lat", "w_kv_b", "g_mla_q_nope", "g_mla_q_pe",
               "g_mla_k_nope", "g_mla_k_pe", "g_dil_q", "g_dil_k", "w_o", "g_ffn_norm", "w_up", "w_conv", "b_conv", "w_down")


def kernel(x, c, positions, w_ada, b_ada, g_mix_norm, w_in, g_q_lat, w_q_b, g_kv_lat, w_kv_b, g_mla_q_nope, g_mla_q_pe, g_mla_k_nope, g_mla_k_pe, g_dil_q, g_dil_k, w_o, g_ffn_norm, w_up, w_conv, b_conv, w_down, loss_target, m_w_ada, m_b_ada, m_g_mix_norm, m_w_in, m_g_q_lat, m_w_q_b, m_g_kv_lat, m_w_kv_b, m_g_mla_q_nope, m_g_mla_q_pe, m_g_mla_k_nope, m_g_mla_k_pe, m_g_dil_q, m_g_dil_k, m_w_o, m_g_ffn_norm, m_w_up, m_w_conv, m_b_conv, m_w_down, v_w_ada, v_b_ada, v_g_mix_norm, v_w_in, v_g_q_lat, v_w_q_b, v_g_kv_lat, v_w_kv_b, v_g_mla_q_nope, v_g_mla_q_pe, v_g_mla_k_nope, v_g_mla_k_pe, v_g_dil_q, v_g_dil_k, v_w_o, v_g_ffn_norm, v_w_up, v_w_conv, v_b_conv, v_w_down):
    args = dict(locals())
    xi, yi, ci = _place()
    me = 4 * xi + 2 * yi + ci
    shard = {n: args[n][0] for n in COL_SHARDED + ROW_SHARDED + ("w_ada",)}
    small_w = {n: args[n] for n, _ in SMALL if n != "loss"}

    (c_all,) = all_gather("gather_c", [c])
    (sc_all,) = rowwise("silu_c", lambda rows, params: ([_silu(rows[0])], []), [c_all.reshape(N_DEV, D_MODEL)], [],
                        [(D_MODEL, MXU_DTYPE)])
    mod_part = matmul("ada_fwd", sc_all, shard["w_ada"], "nn")
    (mod_all,) = all_gather("gather_mod", [mod_part])

    payload = {n: shard[n] if n == "w_conv" else shard[n].astype(MXU_DTYPE) for n in COL_SHARDED + ROW_SHARDED}
    gathers, after_start = [], mod_all
    for i, grp in enumerate(GATHER_GROUPS):
        gathers.append(exchange_start(f"gather{i}_start", [payload[n] for n in grp], gather=True, after=after_start))
        after_start = gathers[-1][-1]
    full = {}

    def fetch(name, after):
        if name not in full:
            (i, grp), = [(i, grp) for i, grp in enumerate(GATHER_GROUPS) if name in grp]
            srcs, lands = exchange_wait(f"gather{i}_wait", gathers[i], True, after)
            for n, src, land in zip(grp, srcs, lands, strict=True):
                stack = lax.dynamic_update_index_in_dim(land, src, me, 0)
                full[n] = to_kernel_layout(n, _gather_cols(stack) if n in COL_SHARDED else _gather_rows(stack))
        return full[name]

    mod_row = lax.dynamic_index_in_dim(mod_all, me, axis=1, keepdims=False).reshape(1, 6 * D_MODEL)
    (mod,) = rowwise("ada_bias", lambda rows, params: ([rows[0] + rows[1]], []), [mod_row, b_ada], [], [(6 * D_MODEL, F32)],
                     dep=after_start)

    own, pending, scatters = {}, {}, {}

    def emit(name, grad):
        grad = from_kernel_layout(name, grad)
        parts = _scatter_cols(grad) if name in COL_SHARDED else _scatter_rows(grad)
        own[name] = lax.dynamic_index_in_dim(parts, me, 0, keepdims=False)
        pending[name] = parts
        for i, grp in enumerate(SCATTER_GROUPS):
            if name == grp[-1]:
                scatters[i] = exchange_start(f"scatter{i}_start", [pending[n] for n in grp], gather=False)
                return scatters[i][-1]
        return None

    pos = positions.reshape(SEQ, 1).astype(F32)
    grad_x, small = _local_step(x[0], pos, mod, loss_target[0], small_w, fetch, emit)

    res, done = {}, grad_x
    for i, grp in enumerate(SCATTER_GROUPS):
        _, lands = exchange_wait(f"scatter{i}_wait", scatters[i], False, done)
        for n, land in zip(grp, lands, strict=True):
            res[n] = adamw(f"adamw_{n}", shard[n], [own[n], land], args["m_" + n][0], args["v_" + n][0], ADAM_TILE.get(n))
            done = res[n][0]
    (small_all,) = all_gather("gather_small", [_pack_small(small)], after=done)
    packed = adamw("adamw_small", _pack_small(small_w), [small_all], _pack_small({n: args["m_" + n] for n in small_w}),
                   _pack_small({n: args["v_" + n] for n in small_w}))
    small_res = [_unpack_small(p) for p in packed]
    dmod_all = small_all.reshape(N_DEV, -1)[:, 1:1 + 6 * D_MODEL]
    dmod_mine = lax.dynamic_slice_in_dim(dmod_all, me * (6 * D_MODEL // N_DEV), 6 * D_MODEL // N_DEV, axis=1)
    g_w_ada = matmul("ada_wgrad", sc_all, dmod_mine, "tn")
    res["w_ada"] = adamw("adamw_w_ada", shard["w_ada"], [g_w_ada], m_w_ada[0], v_w_ada[0], ADAM_TILE["w_ada"])

    def leaf(kind, n):
        if n in res:
            return res[n][kind][None]
        return small_res[kind][n]

    loss = small_res[0]["loss"].reshape(())
    return (loss, grad_x[None], *[leaf(k, n) for k in range(4) for n in OUT_WEIGHTS])
```

```python
import jax
import jax.numpy as jnp
from jax import lax
from jax.experimental import pallas as pl
from jax.experimental.pallas import tpu as pltpu

F32 = jnp.float32
MXU_DTYPE = jnp.bfloat16

N_DEV = 8
D_MODEL = 1024
SEQ = 2048
HEADS = 8
NOPE = 64
ROPE = 32
Q_LORA = 512
KV_LORA = 256
DIL_DIM = 64
DIL_WIDTH = HEADS * DIL_DIM
DILATIONS = (1, 4, 16)
SPAN = 128
D_FF = 2816
LANES = 128
ROPE_THETA = 10000.0
EPS = 1e-6
NEG_INF = -1e30
ADAM_LR, ADAM_B1, ADAM_B2, ADAM_EPS, ADAM_WD, ADAM_STEP = 0.001, 0.9, 0.999, 1e-08, 0.01, 10
VMEM_LIMIT = 56 * 1024 * 1024
MESH_ID = pl.DeviceIdType.MESH

P_QLAT, P_KVLAT, P_KPE, P_QD, P_KD, P_VD, P_END = 0, 512, 768, 896, 1408, 1920, 2432
KPE_LO = 64
MIX_IN = HEADS * LANES + DIL_WIDTH


def _params(**kw):
    return pltpu.CompilerParams(vmem_limit_bytes=VMEM_LIMIT, **kw)


def rowwise(name, fn, rows, params, out_rows, out_accs=(), tm=512, dep=None):
    deps = [] if dep is None else [dep]
    rows = [r if isinstance(r, tuple) else (r, r.shape[1], 0) for r in rows]
    R = rows[0][0].shape[0]
    tm = min(tm, R)
    steps = R // tm
    assert steps * tm == R
    in_specs = []
    for a, width, cb in rows:
        ri = a.shape[0]
        per = ri // tm
        assert per * tm == ri
        if ri == R:
            in_specs.append(pl.BlockSpec((tm, width), lambda i, cb=cb: (i, cb)))
        else:
            in_specs.append(pl.BlockSpec((tm, width), lambda i, per=per, cb=cb: (i % per, cb)))
    for p in params:
        in_specs.append(pl.BlockSpec(p.shape, lambda i: (0,) * p.ndim))
    in_specs += [pl.BlockSpec(memory_space=pl.ANY)] * len(deps)
    out_shape = [jax.ShapeDtypeStruct((R, d), dt) for d, dt in out_rows]
    out_specs = [pl.BlockSpec((tm, d), lambda i: (i, 0)) for d, _ in out_rows]
    out_shape += [jax.ShapeDtypeStruct((1, n), F32) for n in out_accs]
    out_specs += [pl.BlockSpec((1, n), lambda i: (0, 0)) for n in out_accs]
    nr, npar, no, na = len(rows), len(params), len(out_rows), len(out_accs)

    def body(*refs):
        rvals = [r[...] for r in refs[:nr]]
        pvals = [r[...] for r in refs[nr:nr + npar]]
        outs, accs = fn(rvals, pvals)
        first_out = nr + npar + len(deps)
        for ref, v in zip(refs[first_out:first_out + no], outs, strict=True):
            ref[...] = v.astype(ref.dtype)
        if na:
            acc_refs = refs[first_out + no:]
            i = pl.program_id(0)

            @pl.when(i == 0)
            def _():
                for ref, v in zip(acc_refs, accs, strict=True):
                    ref[...] = v

            @pl.when(i > 0)
            def _():
                for ref, v in zip(acc_refs, accs, strict=True):
                    ref[...] += v

    res = pl.pallas_call(body, name=name, grid=(steps,), in_specs=in_specs, out_specs=out_specs,
                         out_shape=out_shape, compiler_params=_params())(*[r[0] for r in rows], *params, *deps)
    return list(res)


_DIMS = {"nn": ((1,), (0,)), "nt": ((1,), (1,)), "tn": ((0,), (0,))}


def _dot(a, b, mode="nn"):
    return lax.dot_general(a.astype(MXU_DTYPE), b.astype(MXU_DTYPE), (_DIMS[mode], ((), ())),
                           preferred_element_type=F32)


def matmul(name, a, b, mode, tm=None, tn=None, tk=None, out_dtype=F32, dep=None, b_nmap=None, b_kmap=None):
    if mode == "tn":
        K, M = a.shape
    else:
        M, K = a.shape
    N = b.shape[0] if mode == "nt" else b.shape[1]
    tm, tn, tk = tm or M, tn or N, tk or K
    nm, nn, nk = M // tm, N // tn, K // tk
    assert nm * tm == M and nn * tn == N and nk * tk == K
    same = lambda idx: idx
    b_nmap, b_kmap = b_nmap or same, b_kmap or same
    a_spec = pl.BlockSpec((tk, tm), lambda i, j, k: (k, i)) if mode == "tn" else pl.BlockSpec((tm, tk), lambda i, j, k: (i, k))
    if mode == "nt":
        b_spec = pl.BlockSpec((tn, tk), lambda i, j, k: (b_nmap(j), b_kmap(k)))
    else:
        b_spec = pl.BlockSpec((tk, tn), lambda i, j, k: (b_kmap(k), b_nmap(j)))
    deps = [] if dep is None else [dep]

    def body(a_ref, b_ref, *rest):
        o_ref, scratch = rest[len(deps)], rest[len(deps) + 1:]
        p = _dot(a_ref[...], b_ref[...], mode)
        if nk == 1:
            o_ref[...] = p.astype(o_ref.dtype)
        else:
            acc = scratch[0]
            k = pl.program_id(2)

            @pl.when(k == 0)
            def _():
                acc[...] = p

            @pl.when(k > 0)
            def _():
                acc[...] += p

            @pl.when(k == nk - 1)
            def _():
                o_ref[...] = acc[...].astype(o_ref.dtype)

    return pl.pallas_call(
        body, name=name, grid=(nm, nn, nk), in_specs=[a_spec, b_spec] + [pl.BlockSpec(memory_space=pl.ANY)] * len(deps),
        out_specs=pl.BlockSpec((tm, tn), lambda i, j, k: (i, j)),
        out_shape=jax.ShapeDtypeStruct((M, N), out_dtype),
        scratch_shapes=[pltpu.VMEM((tm, tn), F32)] if nk > 1 else [],
        compiler_params=_params())(a, b, *deps)


def _rms(x, g):
    rstd = lax.rsqrt(jnp.mean(x * x, axis=-1, keepdims=True) + EPS)
    n = x * rstd
    return n * g, n, rstd


def _rms_bwd(dy, n, rstd, g):
    dg = jnp.sum(dy * n, axis=0, keepdims=True)
    dn = dy * g
    dx = rstd * (dn - n * jnp.mean(dn * n, axis=-1, keepdims=True))
    return dx, dg


def _norm_bwd(dy, x, g):
    _, n, rstd = _rms(x, g)
    return _rms_bwd(dy, n, rstd, g)


def _colsum(v):
    return jnp.sum(v, axis=0, keepdims=True)


def _silu(x):
    return x * (1.0 / (1.0 + jnp.exp(-x)))


def _lane(shape):
    return lax.broadcasted_iota(jnp.int32, shape, 1)


def _group_mean(v, groups):
    lane = _lane(v.shape)
    out = jnp.zeros_like(v)
    for lo, hi in groups:
        m = (lane >= lo) & (lane < hi)
        out = jnp.where(m, jnp.sum(jnp.where(m, v, 0.0), axis=-1, keepdims=True) * (1.0 / (hi - lo)), out)
    return out


def _in_groups(shape, groups):
    lane = _lane(shape)
    m = jnp.zeros(shape, jnp.bool_)
    for lo, hi in groups:
        m = m | ((lane >= lo) & (lane < hi))
    return m


def _grms(x, g, groups):
    rstd = lax.rsqrt(_group_mean(x * x, groups) + EPS)
    n = jnp.where(_in_groups(x.shape, groups), x * rstd, 0.0)
    return n * g, n, rstd


def _grms_bwd(dy, n, rstd, g, groups):
    dn = dy * g
    return rstd * (dn - n * _group_mean(dn * n, groups)), _colsum(dy * n)


def _rot(x, half, transpose=False):
    first = (_lane(x.shape) % (2 * half)) < half
    up = pltpu.roll(x, LANES - half, axis=1)
    down = pltpu.roll(x, half, axis=1)
    return jnp.where(first, up, -down) if transpose else jnp.where(first, -up, down)


def _rope(x, cos, sin, half):
    return x * cos + _rot(x, half) * sin


def _rope_bwd(dy, cos, sin, half):
    return dy * cos + _rot(dy * sin, half, transpose=True)


def _chunks(x):
    return [x[:, i:i + LANES] for i in range(0, x.shape[1], LANES)]


Q_GROUPS = ((0, NOPE), (NOPE, NOPE + ROPE))
K_GROUPS = ((0, NOPE),)
KPE_GROUPS = ((KPE_LO, KPE_LO + ROPE),)
DIL_GROUPS = ((0, DIL_DIM), (DIL_DIM, 2 * DIL_DIM))


def _col(width, rows=SEQ):
    return pl.BlockSpec((rows, width), lambda h: (0, h))


def _causal_tail(s, tq, fill):
    diag = s[:, s.shape[1] - tq:]
    keep = lax.broadcasted_iota(jnp.int32, diag.shape, 1) <= lax.broadcasted_iota(jnp.int32, diag.shape, 0)
    diag = jnp.where(keep, diag, fill)
    return diag if s.shape[1] == tq else jnp.concatenate([s[:, :s.shape[1] - tq], diag], axis=1)


def mla_fwd(name, q, k, v, scale, tq=512):
    S = q.shape[0]

    def body(q_ref, k_ref, v_ref, o_ref, lse_ref):
        for i in range(S // tq):
            kext = (i + 1) * tq
            blk = slice(i * tq, kext)
            s = _causal_tail(_dot(q_ref[blk, :], k_ref[:kext, :], "nt") * scale, tq, NEG_INF)
            m = jnp.max(s, axis=-1, keepdims=True)
            e = jnp.exp(s - m)
            l = jnp.sum(e, axis=-1, keepdims=True)
            o_ref[blk, :] = _dot(e / l, v_ref[:kext, :])
            lse_ref[0, blk, :] = m + jnp.log(l)

    return pl.pallas_call(
        body, name=name, grid=(HEADS,), in_specs=[_col(LANES)] * 3,
        out_specs=[_col(LANES), pl.BlockSpec((1, S, 1), lambda h: (h, 0, 0))],
        out_shape=[jax.ShapeDtypeStruct((S, MIX_IN), F32), jax.ShapeDtypeStruct((HEADS, S, 1), F32)],
        compiler_params=_params())(q, k, v)


def mla_bwd(name, q, k, v, o, do, lse, scale, tq=512):
    S = q.shape[0]

    def body(q_ref, k_ref, v_ref, o_ref, do_ref, lse_ref, dq_ref, dkv_ref, dkpe_ref, dk_acc, dv_acc):
        dk_acc[...] = jnp.zeros_like(dk_acc)
        dv_acc[...] = jnp.zeros_like(dv_acc)
        for i in range(S // tq):
            kext = (i + 1) * tq
            blk = slice(i * tq, kext)
            qi, kk, vv = q_ref[blk, :], k_ref[:kext, :], v_ref[:kext, :]
            doi = do_ref[blk, :]
            s = _causal_tail(_dot(qi, kk, "nt") * scale, tq, NEG_INF)
            p = jnp.exp(s - lse_ref[0, blk, :])
            dp = _dot(doi, vv, "nt")
            delta = jnp.sum(doi * o_ref[blk, :], axis=-1, keepdims=True)
            ds = p * (dp - delta) * scale
            dq_ref[blk, :] = _dot(ds, kk)
            dk_acc[:kext, :] += _dot(ds, qi, "tn")
            dv_acc[:kext, :] += _dot(p, doi, "tn")
        dk = dk_acc[...]
        lane = _lane(dk.shape)
        dkv_ref[...] = jnp.where(lane < NOPE, dk, 0.0) + dv_acc[...]
        dkpe = jnp.where((lane >= KPE_LO) & (lane < KPE_LO + ROPE), dk, 0.0)
        h = pl.program_id(0)

        @pl.when(h == 0)
        def _():
            dkpe_ref[...] = dkpe

        @pl.when(h > 0)
        def _():
            dkpe_ref[...] += dkpe

    return pl.pallas_call(
        body, name=name, grid=(HEADS,),
        in_specs=[_col(LANES)] * 5 + [pl.BlockSpec((1, S, 1), lambda h: (h, 0, 0))],
        out_specs=[_col(LANES), _col(LANES), pl.BlockSpec((S, LANES), lambda h: (0, 0))],
        out_shape=[jax.ShapeDtypeStruct((S, HEADS * LANES), F32), jax.ShapeDtypeStruct((S, HEADS * LANES), F32),
                   jax.ShapeDtypeStruct((S, LANES), F32)],
        scratch_shapes=[pltpu.VMEM((S, LANES), F32), pltpu.VMEM((S, LANES), F32)],
        compiler_params=_params())(q, k, v, o, do, lse)


def _band_blocks(L, tq):
    return [(i * tq, (i + 1) * tq, max(0, i * tq - SPAN)) for i in range(L // tq)]


def _band_mask(q0, q1, k0):
    shape = (q1 - q0, q1 - k0)
    dist = (lax.broadcasted_iota(jnp.int32, shape, 0) + q0) - (lax.broadcasted_iota(jnp.int32, shape, 1) + k0)
    return (dist >= 0) & (dist <= SPAN)


def _class_rows(r, dil, lo, hi):
    return pl.ds(r + dil * lo, hi - lo, stride=dil) if dil > 1 else pl.ds(lo, hi - lo)


def _stack_heads(t, lo):
    zero = jnp.zeros_like(t)
    return jnp.concatenate([jnp.where(lo, t, zero), jnp.where(lo, zero, t)], axis=0)


def _band_mask2(q0, q1, k0):
    n = q1 - q0
    shape = (2 * n, q1 - k0)
    i = lax.broadcasted_iota(jnp.int32, shape, 0)
    dist = (jnp.where(i >= n, i - n, i) + q0) - (lax.broadcasted_iota(jnp.int32, shape, 1) + k0)
    return (dist >= 0) & (dist <= SPAN)


def _pair_col(col0=0):
    return pl.BlockSpec((SEQ, LANES), lambda j: (0, col0 // LANES + j))


def band_fwd(name, q, k, v, dil):
    S = q.shape[0]
    L = S // dil
    tq = min(L, 512)
    scale = DIL_DIM ** -0.5

    def body(q_ref, k_ref, v_ref, o_ref, lse_ref):
        for r in range(dil):
            for q0, q1, k0 in _band_blocks(L, tq):
                qrows, krows = _class_rows(r, dil, q0, q1), _class_rows(r, dil, k0, q1)
                qb, kb, vb = q_ref[qrows, :].astype(MXU_DTYPE), k_ref[krows, :], v_ref[krows, :]
                n = q1 - q0
                lo = _lane(qb.shape) < DIL_DIM
                s = _dot(_stack_heads(qb, lo), kb, "nt") * scale
                s = jnp.where(_band_mask2(q0, q1, k0), s, NEG_INF)
                mx = jnp.max(s, axis=-1, keepdims=True)
                e = jnp.exp(s - mx)
                l = jnp.sum(e, axis=-1, keepdims=True)
                pv, lse = _dot(e / l, vb), mx + jnp.log(l)
                o_ref[qrows, :] = jnp.where(lo, pv[:n], pv[n:])
                lse_ref[qrows, :] = jnp.where(lo, lse[:n], lse[n:])

    return pl.pallas_call(
        body, name=name, grid=(DIL_WIDTH // LANES,), in_specs=[_pair_col()] * 2 + [_pair_col(P_VD)], out_specs=[_pair_col()] * 2,
        out_shape=[jax.ShapeDtypeStruct((S, DIL_WIDTH), F32)] * 2, compiler_params=_params())(q, k, v)


def band_bwd(name, q, k, v, lse, lse_mix, o_cat, do_cat, dil):
    S = q.shape[0]
    L = S // dil
    tq = min(L, 512)
    scale = DIL_DIM ** -0.5

    def body(q_ref, k_ref, v_ref, lse_ref, mix_ref, o_ref, do_ref, dq_ref, dk_ref, dv_ref):
        dk_ref[...] = jnp.zeros_like(dk_ref)
        dv_ref[...] = jnp.zeros_like(dv_ref)
        for r in range(dil):
            for q0, q1, k0 in _band_blocks(L, tq):
                qrows, krows = _class_rows(r, dil, q0, q1), _class_rows(r, dil, k0, q1)
                qb, kb, vb = q_ref[qrows, :].astype(MXU_DTYPE), k_ref[krows, :], v_ref[krows, :]
                lse_p, dout = lse_ref[qrows, :], do_ref[qrows, :]
                n = q1 - q0
                lo = _lane(qb.shape) < DIL_DIM
                per_head = lambda t: jnp.concatenate([t[:, 0:1], t[:, DIL_DIM:DIL_DIM + 1]], axis=0)
                w2 = per_head(jnp.exp(lse_p - mix_ref[qrows, :]))
                dd = dout * o_ref[qrows, :]
                big_d = jnp.concatenate([jnp.sum(jnp.where(lo, dd, 0.0), axis=-1, keepdims=True),
                                         jnp.sum(jnp.where(lo, 0.0, dd), axis=-1, keepdims=True)], axis=0)
                q2 = _stack_heads(qb, lo)
                s = _dot(q2, kb, "nt") * scale
                p = jnp.where(_band_mask2(q0, q1, k0), jnp.exp(s - per_head(lse_p)), 0.0)
                dom = _stack_heads(dout, lo) * w2
                ds = p * (_dot(dom, vb, "nt") - w2 * big_d) * scale
                dq2 = _dot(ds, kb)
                dq_ref[qrows, :] = jnp.where(lo, dq2[:n], dq2[n:])
                dk_ref[krows, :] += _dot(ds, q2, "tn")
                dv_ref[krows, :] += _dot(p, dom, "tn")

    cat = _pair_col(HEADS * LANES)
    return pl.pallas_call(
        body, name=name, grid=(DIL_WIDTH // LANES,),
        in_specs=[_pair_col()] * 2 + [_pair_col(P_VD)] + [_pair_col()] * 2 + [cat] * 2, out_specs=[_pair_col()] * 3,
        out_shape=[jax.ShapeDtypeStruct((S, DIL_WIDTH), F32)] * 3,
        compiler_params=_params())(q, k, v, lse, lse_mix, o_cat, do_cat)


def combine_fwd(name, outs, lses, o_cat, tm=512):
    S = outs[0].shape[0]

    def body(o1, o2, o3, l1, l2, l3, cat_in, cat_out, mix_ref):
        ls = [l1[...], l2[...], l3[...]]
        m = jnp.maximum(jnp.maximum(ls[0], ls[1]), ls[2])
        e = [jnp.exp(l - m) for l in ls]
        den = e[0] + e[1] + e[2]
        cat_out[...] = (e[0] / den) * o1[...] + (e[1] / den) * o2[...] + (e[2] / den) * o3[...]
        mix_ref[...] = m + jnp.log(den)

    row = pl.BlockSpec((tm, DIL_WIDTH), lambda i: (i, 0))
    return pl.pallas_call(
        body, name=name, grid=(S // tm,), in_specs=[row] * 6 + [pl.BlockSpec(memory_space=pl.ANY)],
        out_specs=[pl.BlockSpec((tm, DIL_WIDTH), lambda i: (i, HEADS * LANES // DIL_WIDTH)), row],
        out_shape=[jax.ShapeDtypeStruct(o_cat.shape, F32), jax.ShapeDtypeStruct((S, DIL_WIDTH), F32)],
        input_output_aliases={6: 0}, compiler_params=_params())(*outs, *lses, o_cat)


def _shift_down(u, n):
    t = lax.broadcasted_iota(jnp.int32, u.shape, 0)
    return jnp.where(t >= n, pltpu.roll(u, n, axis=0), 0.0)


def _shift_up(u, n):
    rows = u.shape[0]
    t = lax.broadcasted_iota(jnp.int32, u.shape, 0)
    return jnp.where(t < rows - n, pltpu.roll(u, rows - n, axis=0), 0.0)


def _conv(u, w, b):
    return w[2:3, :] * u + w[1:2, :] * _shift_down(u, 1) + w[0:1, :] * _shift_down(u, 2) + b


CONV_TC = 256
CONV_NB = D_FF // CONV_TC


def paired_block(j):
    return (j % 2) * CONV_NB + j // 2


def natural_block(j):
    return jnp.where(j < CONV_NB, 2 * j, 2 * (j - CONV_NB) + 1)


def _pair_spec(rows):
    return pl.BlockSpec((rows, 2 * CONV_TC), lambda j: (0, j))


def _half_specs(rows):
    return [pl.BlockSpec((rows, CONV_TC), lambda j: (0, j)), pl.BlockSpec((rows, CONV_TC), lambda j: (0, j + CONV_NB))]


def conv_glu_fwd(name, up, w_conv, b_conv):
    S = up.shape[0]

    def body(u_ref, wg_ref, wv_ref, bg_ref, bv_ref, act_ref):
        w = jnp.concatenate([wg_ref[...], wv_ref[...]], axis=1)
        u = _conv(u_ref[...], w, jnp.concatenate([bg_ref[...], bv_ref[...]], axis=1))
        act_ref[...] = (_silu(u[:, :CONV_TC]) * u[:, CONV_TC:]).astype(act_ref.dtype)

    return pl.pallas_call(
        body, name=name, grid=(CONV_NB,), in_specs=[_pair_spec(S)] + _half_specs(3) + _half_specs(1),
        out_specs=pl.BlockSpec((S, CONV_TC), lambda j: (0, j)), out_shape=jax.ShapeDtypeStruct((S, D_FF), MXU_DTYPE),
        compiler_params=_params())(up, w_conv, w_conv, b_conv, b_conv)


def conv_glu_bwd(name, up, w_conv, b_conv, dact):
    S = up.shape[0]

    def body(u_ref, wg_ref, wv_ref, bg_ref, bv_ref, da_ref, dup_ref, dwg_ref, dwv_ref, dbg_ref, dbv_ref):
        uin = u_ref[...]
        w = jnp.concatenate([wg_ref[...], wv_ref[...]], axis=1)
        u = _conv(uin, w, jnp.concatenate([bg_ref[...], bv_ref[...]], axis=1))
        gate, val, da = u[:, :CONV_TC], u[:, CONV_TC:], da_ref[...]
        sig = 1.0 / (1.0 + jnp.exp(-gate))
        du = jnp.concatenate([da * val * (sig * (1.0 + gate * (1.0 - sig))), da * (gate * sig)], axis=1)
        dup_ref[...] = (w[2:3, :] * du + w[1:2, :] * _shift_up(du, 1) + w[0:1, :] * _shift_up(du, 2)).astype(dup_ref.dtype)
        dw = jnp.concatenate([_colsum(du * _shift_down(uin, 2)), _colsum(du * _shift_down(uin, 1)), _colsum(du * uin)], axis=0)
        db = _colsum(du)
        dwg_ref[...], dwv_ref[...] = dw[:, :CONV_TC], dw[:, CONV_TC:]
        dbg_ref[...], dbv_ref[...] = db[:, :CONV_TC], db[:, CONV_TC:]

    half = lambda rows: pl.BlockSpec((rows, CONV_TC), lambda j: (0, j))
    dup, dwg, dwv, dbg, dbv = pl.pallas_call(
        body, name=name, grid=(CONV_NB,),
        in_specs=[_pair_spec(S)] + _half_specs(3) + _half_specs(1) + [half(S)],
        out_specs=[_pair_spec(S), half(3), half(3), half(1), half(1)],
        out_shape=[jax.ShapeDtypeStruct((S, 2 * D_FF), MXU_DTYPE)] + [jax.ShapeDtypeStruct((3, D_FF), F32)] * 2
        + [jax.ShapeDtypeStruct((1, D_FF), F32)] * 2,
        compiler_params=_params())(up, w_conv, w_conv, b_conv, b_conv, dact)
    return dup, jnp.concatenate([dwg, dwv], axis=1), jnp.concatenate([dbg, dbv], axis=1)


def adamw(name, w, parts, m, v, tr=None):
    R, C = w.shape
    tr = tr or R
    assert R % tr == 0
    c1 = 1.0 - ADAM_B1 ** ADAM_STEP
    c2 = 1.0 - ADAM_B2 ** ADAM_STEP
    np_ = len(parts)

    def body(*refs):
        w_ref, m_ref, v_ref = refs[0], refs[1 + np_], refs[2 + np_]
        go_ref, d_ref, mo_ref, vo_ref = refs[3 + np_:]
        terms = []
        for part, ref in zip(parts, refs[1:1 + np_], strict=True):
            terms += [ref[...]] if part.ndim == 2 else [ref[p] for p in range(part.shape[0])]
        g = terms[0].astype(F32)
        for term in terms[1:]:
            g = g + term.astype(F32)
        m2 = ADAM_B1 * m_ref[...] + (1.0 - ADAM_B1) * g
        v2 = ADAM_B2 * v_ref[...] + (1.0 - ADAM_B2) * (g * g)
        go_ref[...] = g
        mo_ref[...] = m2
        vo_ref[...] = v2
        d_ref[...] = -ADAM_LR * ((m2 / c1) / (jnp.sqrt(v2 / c2) + ADAM_EPS) + ADAM_WD * w_ref[...])

    blk = pl.BlockSpec((tr, C), lambda i: (i, 0))
    part_specs = [blk if p.ndim == 2 else pl.BlockSpec((p.shape[0], tr, C), lambda i: (0, i, 0)) for p in parts]
    return pl.pallas_call(
        body, name=name, grid=(R // tr,),
        in_specs=[blk] + part_specs + [blk, blk], out_specs=[blk] * 4,
        out_shape=[jax.ShapeDtypeStruct((R, C), F32)] * 4, compiler_params=_params())(w, *parts, m, v)


def _place():
    return lax.axis_index("x"), lax.axis_index("y"), lax.axis_index("c")


def all_gather(name, arrs, after=None):
    n = len(arrs)
    deps = [] if after is None else [after]

    def body(*refs):
        ins, outs = refs[:n], refs[n + len(deps):2 * n + len(deps)]
        send_sems, recv_sems, local_sems = refs[2 * n + len(deps):]
        x, y, c = _place()
        me, sibling = (x, y, c), (x, y, 1 - c)
        chips = [(1 - x, y), (x, 1 - y), (1 - x, 1 - y)]
        sends = []
        for t in range(n):
            out = outs[t]

            def slot(px, py, pc, out=out):
                return out.at[4 * px + 2 * py + pc]

            def copy(k, block, to, src=None, t=t, slot=slot):
                return pltpu.make_async_remote_copy(
                    src_ref=slot(*block) if src is None else src, dst_ref=slot(*block),
                    send_sem=send_sems.at[7 * t + k], recv_sem=recv_sems.at[7 * t + k],
                    device_id=to, device_id_type=MESH_ID)

            mine = pltpu.make_async_copy(ins[t], slot(*me), local_sems.at[t])
            mine.start()
            first = [copy(0, me, sibling, src=ins[t])]
            first += [copy(1 + j, me, (*chip, c), src=ins[t]) for j, chip in enumerate(chips)]
            for cp in first:
                cp.start()
            sends.append((mine, first, copy))
        for t in range(n):
            mine, first, copy = sends[t]
            passed = [copy(4 + j, (*chip, c), sibling) for j, chip in enumerate(chips)]
            for j, chip in enumerate(chips):
                copy(1 + j, (*chip, c), me).wait_recv()
                passed[j].start()
            copy(0, sibling, me).wait_recv()
            for j, chip in enumerate(chips):
                copy(4 + j, (*chip, 1 - c), me).wait_recv()
            for cp in first + passed:
                cp.wait_send()
            mine.wait()

    any_spec = pl.BlockSpec(memory_space=pl.ANY)
    res = pl.pallas_call(
        body, name=name, in_specs=[any_spec] * (n + len(deps)), out_specs=[any_spec] * n,
        out_shape=[jax.ShapeDtypeStruct((N_DEV,) + a.shape, a.dtype) for a in arrs],
        scratch_shapes=[pltpu.SemaphoreType.DMA((7 * n,)), pltpu.SemaphoreType.DMA((7 * n,)), pltpu.SemaphoreType.DMA((n,))],
        compiler_params=pltpu.CompilerParams(has_side_effects=True))(*arrs, *deps)
    return list(res)


HBM_SPEC = pl.BlockSpec(memory_space=pltpu.HBM)
SEM_SPEC = pl.BlockSpec(memory_space=pltpu.SEMAPHORE)
DATAFLOW = pltpu.SideEffectType.DATAFLOW_SIDE_EFFECTING


def _exchange_copies(srcs, lands, send_sems, recv_sems, gather):
    x, y, c = _place()
    me = 4 * x + 2 * y + c
    out = []
    for t, (src, land) in enumerate(zip(srcs, lands, strict=True)):
        for k in range(1, N_DEV):
            px, py, pc = x ^ (k >> 2), y ^ ((k >> 1) & 1), c ^ (k & 1)
            out.append(pltpu.make_async_remote_copy(
                src_ref=src if gather else src.at[4 * px + 2 * py + pc],
                dst_ref=land.at[me] if gather else land.at[k - 1],
                send_sem=send_sems.at[7 * t + k - 1], recv_sem=recv_sems.at[7 * t + k - 1],
                device_id=(px, py, pc), device_id_type=MESH_ID))
    return out


def exchange_start(name, arrs, gather, after=None):
    n = len(arrs)
    lands = [lax.empty(((N_DEV,) + a.shape) if gather else ((N_DEV - 1,) + a.shape[1:]), a.dtype) for a in arrs]
    deps = [] if after is None else [after]

    def body(*refs):
        srcs, land_refs = refs[:n], refs[n:2 * n]
        send_sems, recv_sems = refs[2 * n + len(deps)], refs[2 * n + len(deps) + 1]
        token = refs[-1]
        for cp in _exchange_copies(srcs, land_refs, send_sems, recv_sems, gather):
            cp.start()
        token[...] = jnp.zeros_like(token)

    hbm = lambda a: pltpu.HBM(a.shape, a.dtype)
    res = pl.pallas_call(
        body, name=name,
        out_shape=(pltpu.SemaphoreType.DMA((7 * n,)), pltpu.SemaphoreType.DMA((7 * n,)), *[hbm(a) for a in arrs],
                   *[hbm(l) for l in lands], jax.ShapeDtypeStruct((8, 128), F32)),
        in_specs=[HBM_SPEC] * (2 * n) + [pl.BlockSpec(memory_space=pl.ANY)] * len(deps),
        out_specs=(SEM_SPEC, SEM_SPEC, *[HBM_SPEC] * (2 * n), pl.BlockSpec(memory_space=pltpu.VMEM)),
        input_output_aliases={i: 2 + i for i in range(2 * n)},
        compiler_params=pltpu.CompilerParams(has_side_effects=DATAFLOW),
    )(*[pltpu.with_memory_space_constraint(a, pltpu.HBM) for a in arrs + lands], *deps)
    return res[0], res[1], list(res[2:2 + n]), list(res[2 + n:2 + 2 * n]), res[-1]


def exchange_wait(name, started, gather, after):
    send_sems, recv_sems, srcs, lands, _ = started
    n = len(srcs)

    def body(*refs):
        src_refs, land_refs = refs[:n], refs[n:2 * n]
        copies = _exchange_copies(src_refs, land_refs, refs[2 * n], refs[2 * n + 1], gather)
        for cp in copies:
            cp.wait_send()
        for cp in copies:
            cp.wait_recv()

    hbm = lambda a: pltpu.HBM(a.shape, a.dtype)
    res = pl.pallas_call(
        body, name=name, out_shape=tuple(hbm(a) for a in srcs + lands),
        in_specs=[HBM_SPEC] * (2 * n) + [SEM_SPEC, SEM_SPEC, pl.BlockSpec(memory_space=pl.ANY)],
        out_specs=tuple([HBM_SPEC] * (2 * n)), input_output_aliases={i: i for i in range(2 * n)},
        compiler_params=pltpu.CompilerParams(has_side_effects=DATAFLOW),
    )(*srcs, *lands, send_sems, recv_sems, after)
    return list(res[:n]), list(res[n:])


def _gather_cols(stack):
    p, k, n = stack.shape
    return stack.transpose(1, 0, 2).reshape(k, p * n)


def _scatter_cols(full):
    k, n = full.shape
    return full.reshape(k, N_DEV, n // N_DEV).transpose(1, 0, 2)


def _gather_rows(stack):
    p, r, n = stack.shape
    return stack.reshape(p * r, n)


def _scatter_rows(full):
    r, n = full.shape
    return full.reshape(N_DEV, r // N_DEV, n)


_IN_NAT = Q_LORA + KV_LORA


def to_kernel_layout(name, w):
    if name == "w_in":
        z = lambda n: jnp.zeros((w.shape[0], n), w.dtype)
        return jnp.concatenate([w[:, :_IN_NAT], z(KPE_LO), w[:, _IN_NAT:_IN_NAT + ROPE], z(LANES - KPE_LO - ROPE),
                                w[:, _IN_NAT + ROPE:]], axis=1)
    if name == "w_q_b":
        return jnp.pad(w.reshape(w.shape[0], HEADS, NOPE + ROPE), ((0, 0), (0, 0), (0, LANES - NOPE - ROPE))).reshape(w.shape[0], HEADS * LANES)
    if name == "w_o":
        mla = jnp.pad(w[:HEADS * NOPE].reshape(HEADS, NOPE, -1), ((0, 0), (LANES - NOPE, 0), (0, 0))).reshape(HEADS * LANES, -1)
        return jnp.concatenate([mla, w[HEADS * NOPE:]], axis=0)
    return w


def from_kernel_layout(name, g):
    if name == "w_in":
        return jnp.concatenate([g[:, :_IN_NAT], g[:, P_KPE + KPE_LO:P_KPE + KPE_LO + ROPE], g[:, P_QD:]], axis=1)
    if name == "w_q_b":
        return g.reshape(g.shape[0], HEADS, LANES)[:, :, :NOPE + ROPE].reshape(g.shape[0], HEADS * (NOPE + ROPE))
    if name == "w_o":
        mla = g[:HEADS * LANES].reshape(HEADS, LANES, -1)[:, LANES - NOPE:, :].reshape(HEADS * NOPE, -1)
        return jnp.concatenate([mla, g[HEADS * LANES:]], axis=0)
    return g


SMALL = (("loss", 1), ("b_ada", 6 * D_MODEL), ("g_mix_norm", D_MODEL), ("g_q_lat", Q_LORA), ("g_kv_lat", KV_LORA),
         ("g_mla_q_nope", NOPE), ("g_mla_q_pe", ROPE), ("g_mla_k_nope", NOPE), ("g_mla_k_pe", ROPE),
         ("g_dil_q", DIL_DIM), ("g_dil_k", DIL_DIM), ("g_ffn_norm", D_MODEL), ("b_conv", 2 * D_FF))
SMALL_ROWS = 16
SMALL_COLS = 1024


def _pack_small(values):
    parts = [values[name].reshape(-1).astype(F32) if name in values else jnp.zeros((n,), F32) for name, n in SMALL]
    flat = jnp.concatenate(parts)
    flat = jnp.pad(flat, (0, SMALL_ROWS * SMALL_COLS - flat.shape[0]))
    return flat.reshape(SMALL_ROWS, SMALL_COLS)


def _unpack_small(packed):
    flat = packed.reshape(-1)
    out, off = {}, 0
    for name, n in SMALL:
        out[name] = flat[off:off + n].reshape(1, n)
        off += n
    return out


def _local_step(x, pos, mod, target, w, fetch, emit):
    S = SEQ
    sh1, sc1, g1, sh2, sc2, g2 = [mod[:, i * D_MODEL:(i + 1) * D_MODEL] for i in range(6)]
    zeros = lambda n: jnp.zeros((1, n), F32)
    g_q = jnp.concatenate([w["g_mla_q_nope"], w["g_mla_q_pe"], zeros(LANES - NOPE - ROPE)], axis=1)
    g_k = jnp.concatenate([w["g_mla_k_nope"], zeros(LANES - NOPE)], axis=1)
    g_kpe = jnp.concatenate([zeros(KPE_LO), w["g_mla_k_pe"], zeros(LANES - KPE_LO - ROPE)], axis=1)
    g_dq = jnp.concatenate([w["g_dil_q"]] * 2, axis=1)
    g_dk = jnp.concatenate([w["g_dil_k"]] * 2, axis=1)
    b_conv = w["b_conv"]

    def inv_freq(d):
        return jnp.power(ROPE_THETA, -2.0 * jnp.arange(d // 2, dtype=F32) / d)

    f_mla = jnp.concatenate([jnp.zeros((KPE_LO,), F32), inv_freq(ROPE), inv_freq(ROPE), jnp.zeros((LANES - KPE_LO - ROPE,), F32)])
    f_dil = jnp.concatenate([inv_freq(DIL_DIM)] * 4)

    def tables_fn(rows, params):
        (p,), (fa, fb) = rows, params
        return [jnp.cos(p * fa), jnp.sin(p * fa), jnp.cos(p * fb), jnp.sin(p * fb)], []

    cos_m, sin_m, cos_d, sin_d = rowwise("rope_tables", tables_fn, [pos], [f_mla.reshape(1, LANES), f_dil.reshape(1, LANES)],
                                         [(LANES, F32)] * 4)
    tables = [cos_m, sin_m, cos_d, sin_d]
    H_M, H_D = ROPE // 2, DIL_DIM // 2

    def ln1_fn(rows, params):
        (xv,), (g, sc, sh) = rows, params
        y, _, _ = _rms(xv, g)
        return [y * (1.0 + sc) + sh], []

    (h,) = rowwise("ln1_fwd", ln1_fn, [x], [w["g_mix_norm"], sc1, sh1], [(D_MODEL, MXU_DTYPE)])
    w_in = fetch("w_in", h)
    proj = matmul("proj_fwd", h, w_in, "nn", tm=512)

    def post_fn(rows, params):
        (pv, cm, sm, cd, sd), (gq, gkv, gkp, gdq, gdk) = rows, params
        kper = _rope(_grms(pv[:, P_KPE:P_QD], gkp, KPE_GROUPS)[0], cm, sm, H_M)
        qd = [_rope(_grms(c, gdq, DIL_GROUPS)[0], cd, sd, H_D) for c in _chunks(pv[:, P_QD:P_KD])]
        kd = [_rope(_grms(c, gdk, DIL_GROUPS)[0], cd, sd, H_D) for c in _chunks(pv[:, P_KD:P_VD])]
        return [_rms(pv[:, P_QLAT:P_KVLAT], gq)[0], _rms(pv[:, P_KVLAT:P_KPE], gkv)[0], kper,
                jnp.concatenate(qd, axis=1), jnp.concatenate(kd, axis=1)], []

    post_params = [w["g_q_lat"], w["g_kv_lat"], g_kpe, g_dq, g_dk]
    qln, kvn, kper, qd_r, kd_r = rowwise(
        "proj_post", post_fn, [proj] + tables, post_params,
        [(Q_LORA, MXU_DTYPE), (KV_LORA, MXU_DTYPE), (LANES, MXU_DTYPE)] + [(DIL_WIDTH, F32)] * 2, tm=256)
    w_q_b, w_kv_b = fetch("w_q_b", qln), fetch("w_kv_b", kvn)
    q = matmul("q_fwd", qln, w_q_b, "nn", tm=1024)
    kv = matmul("kv_fwd", kvn, w_kv_b, "nn", tm=1024)

    def mla_prep_fn(rows, params):
        (qv, kvv, kp, cm, sm), (gq, gk) = rows, params
        value_lanes = _lane(kp.shape) >= NOPE
        qs, ks, vs = [], [], []
        for qc, kc in zip(_chunks(qv), _chunks(kvv), strict=True):
            qs.append(_rope(_grms(qc, gq, Q_GROUPS)[0], cm, sm, H_M))
            ks.append(_grms(kc, gk, K_GROUPS)[0] + kp)
            vs.append(jnp.where(value_lanes, kc, 0.0))
        return [jnp.concatenate(t, axis=1) for t in (qs, ks, vs)], []

    q_mla, k_mla, v_mla = rowwise("mla_prep", mla_prep_fn, [q, kv, kper, cos_m, sin_m], [g_q, g_k],
                                  [(HEADS * LANES, MXU_DTYPE)] * 3, tm=256)
    mla_scale = (NOPE + ROPE) ** -0.5
    o_cat, lse_mla = mla_fwd("mla_fwd", q_mla, k_mla, v_mla, mla_scale)

    band = [band_fwd(f"band{dil}_fwd", qd_r, kd_r, proj, dil) for dil in DILATIONS]
    o_cat, lse_mix = combine_fwd("dil_combine", [b[0] for b in band], [b[1] for b in band], o_cat)
    w_o = fetch("w_o", o_cat)
    mix = matmul("mix_fwd", o_cat, w_o, "nn", tm=512)

    def mid_fn(rows, params):
        (xv, mx), (gate1, g, sc, sh) = rows, params
        x1 = xv + gate1 * mx
        y, _, _ = _rms(x1, g)
        return [x1, y * (1.0 + sc) + sh], []

    x1, h2 = rowwise("mid_fwd", mid_fn, [x, mix], [g1, w["g_ffn_norm"], sc2, sh2], [(D_MODEL, F32), (D_MODEL, MXU_DTYPE)])
    w_up, w_conv, w_down = fetch("w_up", h2), fetch("w_conv", h2), fetch("w_down", h2)
    up = matmul("up_fwd", h2, w_up, "nn", tn=CONV_TC, b_nmap=paired_block)
    act = conv_glu_fwd("conv_fwd", up, w_conv, b_conv)
    dn = matmul("down_fwd", act, w_down, "nn", tm=512)

    def final_fn(rows, params):
        (x1v, dnv, tgt), (gate2,) = rows, params
        r = x1v + gate2 * dnv - tgt
        dy = r * (1.0 / D_MODEL)
        loss = jnp.sum(_colsum(r * r), axis=-1, keepdims=True) * (0.5 / D_MODEL)
        return [dy, gate2 * dy], [loss, _colsum(dy * dnv)]

    dy, d_dn, loss, dg2 = rowwise("loss_head", final_fn, [x1, dn, target], [g2], [(D_MODEL, F32), (D_MODEL, MXU_DTYPE)],
                                  [1, D_MODEL])
    emit("w_down", matmul("down_wgrad", act, d_dn, "tn", tm=1408, out_dtype=MXU_DTYPE))
    dact = matmul("down_dgrad", d_dn, w_down, "nt", tm=512)
    dup, g_w_conv, g_b_conv = conv_glu_bwd("conv_bwd", up, w_conv, b_conv, dact)
    emit("w_conv", g_w_conv)
    sent = emit("w_up", matmul("up_wgrad", h2, dup, "tn", tn=CONV_TC, out_dtype=MXU_DTYPE, b_nmap=natural_block))
    dh2 = matmul("up_dgrad", dup, w_up, "nt", tk=CONV_TC, dep=sent, b_kmap=paired_block)

    def mid_bwd_fn(rows, params):
        (dh2v, dyv, x1v, mx), (gate1, g, sc) = rows, params
        yn, n, rstd = _rms(x1v, g)
        dx_n, dg = _rms_bwd(dh2v * (1.0 + sc), n, rstd, g)
        dx1 = dyv + dx_n
        return [dx1, gate1 * dx1], [dg, _colsum(dh2v * yn), _colsum(dh2v), _colsum(dx1 * mx)]

    dx1, dmix, dg_ffn, dsc2, dsh2, dg1 = rowwise(
        "mid_bwd", mid_bwd_fn, [dh2, dy, x1, mix], [g1, w["g_ffn_norm"], sc2], [(D_MODEL, F32), (D_MODEL, MXU_DTYPE)],
        [D_MODEL] * 4)

    sent = emit("w_o", matmul("mix_wgrad", o_cat, dmix, "tn", tm=512, out_dtype=MXU_DTYPE))
    do_cat = matmul("mix_dgrad", dmix, w_o, "nt", tm=512, dep=sent)
    dband = [band_bwd(f"band{dil}_bwd", qd_r, kd_r, proj, b[1], lse_mix, o_cat, do_cat, dil) for dil, b in zip(DILATIONS, band)]
    dq_mla, dkv_mla, dkper = mla_bwd("mla_bwd", q_mla, k_mla, v_mla, o_cat, do_cat, lse_mla, mla_scale)

    def mla_prep_bwd_fn(rows, params):
        (dqv, dkvv, qv, kvv, cm, sm), (gq, gk) = rows, params
        nope_lanes = _lane(cm.shape) < NOPE
        dqs, dkvs, dgq, dgk = [], [], 0.0, 0.0
        for dqc, dkc, qc, kc in zip(_chunks(dqv), _chunks(dkvv), _chunks(qv), _chunks(kvv), strict=True):
            _, n, rstd = _grms(qc, gq, Q_GROUPS)
            dx, dg = _grms_bwd(_rope_bwd(dqc, cm, sm, H_M), n, rstd, gq, Q_GROUPS)
            dqs.append(dx)
            dgq = dgq + dg
            _, n, rstd = _grms(kc, gk, K_GROUPS)
            dx, dg = _grms_bwd(dkc, n, rstd, gk, K_GROUPS)
            dkvs.append(jnp.where(nope_lanes, dx, dkc))
            dgk = dgk + dg
        return [jnp.concatenate(dqs, axis=1), jnp.concatenate(dkvs, axis=1)], [dgq, dgk]

    dq, dkv, dg_q, dg_k = rowwise("mla_prep_bwd", mla_prep_bwd_fn, [dq_mla, dkv_mla, q, kv, cos_m, sin_m], [g_q, g_k],
                                  [(HEADS * LANES, MXU_DTYPE)] * 2, [LANES, LANES], tm=256)
    emit("w_q_b", matmul("q_wgrad", qln, dq, "tn", out_dtype=MXU_DTYPE))
    emit("w_kv_b", matmul("kv_wgrad", kvn, dkv, "tn", out_dtype=MXU_DTYPE))
    dqln = matmul("q_dgrad", dq, w_q_b, "nt", tm=1024)
    dkvn = matmul("kv_dgrad", dkv, w_kv_b, "nt", tm=1024)

    def pre_bwd_fn(rows, params):
        dql, dkvl, dkp = rows[0:3]
        dqd_, dkd_, dvd_ = [rows[3 + 3 * i] + rows[4 + 3 * i] + rows[5 + 3 * i] for i in range(3)]
        pv, cm, sm, cd, sd = rows[12:]
        gq, gkv, gkp, gdq, gdk = params
        r_q = _norm_bwd(dql, pv[:, P_QLAT:P_KVLAT], gq)
        r_kv = _norm_bwd(dkvl, pv[:, P_KVLAT:P_KPE], gkv)
        _, n, rstd = _grms(pv[:, P_KPE:P_QD], gkp, KPE_GROUPS)
        r_kp = _grms_bwd(_rope_bwd(dkp, cm, sm, H_M), n, rstd, gkp, KPE_GROUPS)
        outs, dgs = [r_q[0], r_kv[0], r_kp[0]], []
        for dval, lo, g in ((dqd_, P_QD, gdq), (dkd_, P_KD, gdk)):
            dg_sum = 0.0
            for dc, xc in zip(_chunks(dval), _chunks(pv[:, lo:lo + DIL_WIDTH]), strict=True):
                _, n, rstd = _grms(xc, g, DIL_GROUPS)
                dx, dg = _grms_bwd(_rope_bwd(dc, cd, sd, H_D), n, rstd, g, DIL_GROUPS)
                outs.append(dx)
                dg_sum = dg_sum + dg
            dgs.append(dg_sum)
        return [jnp.concatenate(outs + [dvd_], axis=1)], [r_q[1], r_kv[1], r_kp[1]] + dgs

    dproj, dg_q_lat, dg_kv_lat, dg_kpe, dg_dq, dg_dk = rowwise(
        "proj_pre_bwd", pre_bwd_fn,
        [dqln, dkvn, dkper] + [d[i] for i in range(3) for d in dband] + [proj] + tables, post_params,
        [(P_END, MXU_DTYPE)], [Q_LORA, KV_LORA, LANES, LANES, LANES], tm=256)
    sent = emit("w_in", matmul("proj_wgrad", h, dproj, "tn", tm=256, out_dtype=MXU_DTYPE))
    dh = matmul("proj_dgrad", dproj, w_in, "nt", tm=512, dep=sent)

    def ln1_bwd_fn(rows, params):
        (dhv, dres, xv), (g, sc) = rows, params
        yn, n, rstd = _rms(xv, g)
        dx_n, dg = _rms_bwd(dhv * (1.0 + sc), n, rstd, g)
        return [dres + dx_n], [dg, _colsum(dhv * yn), _colsum(dhv)]

    grad_x, dg_mix, dsc1, dsh1 = rowwise("ln1_bwd", ln1_bwd_fn, [dh, dx1, x], [w["g_mix_norm"], sc1], [(D_MODEL, F32)],
                                         [D_MODEL] * 3)
    dmod = jnp.concatenate([dsh1, dsc1, dg1, dsh2, dsc2, dg2], axis=-1)
    small = {"loss": loss, "b_ada": dmod, "g_mix_norm": dg_mix, "g_q_lat": dg_q_lat, "g_kv_lat": dg_kv_lat,
             "g_mla_q_nope": dg_q[:, :NOPE], "g_mla_q_pe": dg_q[:, NOPE:NOPE + ROPE], "g_mla_k_nope": dg_k[:, :NOPE],
             "g_mla_k_pe": dg_kpe[:, KPE_LO:KPE_LO + ROPE], "g_dil_q": dg_dq[:, :DIL_DIM] + dg_dq[:, DIL_DIM:],
             "g_dil_k": dg_dk[:, :DIL_DIM] + dg_dk[:, DIL_DIM:], "g_ffn_norm": dg_ffn,
             "b_conv": g_b_conv}
    return grad_x, small


COL_SHARDED = ("w_in", "w_q_b", "w_kv_b", "w_up", "w_conv")
ROW_SHARDED = ("w_o", "w_down")
ADAM_TILE = {"w_ada": 256, "w_in": 256, "w_up": 256, "w_down": 176}
GATHER_GROUPS = (("w_in",), ("w_q_b", "w_kv_b"), ("w_o", "w_up", "w_conv", "w_down"))
SCATTER_GROUPS = (("w_down", "w_conv", "w_up"), ("w_o",), ("w_q_b", "w_kv_b", "w_in"))
OUT_WEIGHTS = ("w_ada", "b_ada", "g_mix_norm", "w_in", "g_q_lat", "w_q_b", "g_kv_lat", "w_kv_b", "g_mla_q_nope", "g_mla_q_pe",
               "g_mla_k_nope", "g_mla_k_pe", "g_dil_q", "g_dil_k", "w_o", "g_ffn_norm", "w_up", "w_conv", "b_conv", "w_down")


def kernel(x, c, positions, w_ada, b_ada, g_mix_norm, w_in, g_q_lat, w_q_b, g_kv_lat, w_kv_b, g_mla_q_nope, g_mla_q_pe, g_mla_k_nope, g_mla_k_pe, g_dil_q, g_dil_k, w_o, g_ffn_norm, w_up, w_conv, b_conv, w_down, loss_target, m_w_ada, m_b_ada, m_g_mix_norm, m_w_in, m_g_q_lat, m_w_q_b, m_g_kv_lat, m_w_kv_b, m_g_mla_q_nope, m_g_mla_q_pe, m_g_mla_k_nope, m_g_mla_k_pe, m_g_dil_q, m_g_dil_k, m_w_o, m_g_ffn_norm, m_w_up, m_w_conv, m_b_conv, m_w_down, v_w_ada, v_b_ada, v_g_mix_norm, v_w_in, v_g_q_lat, v_w_q_b, v_g_kv_lat, v_w_kv_b, v_g_mla_q_nope, v_g_mla_q_pe, v_g_mla_k_nope, v_g_mla_k_pe, v_g_dil_q, v_g_dil_k, v_w_o, v_g_ffn_norm, v_w_up, v_w_conv, v_b_conv, v_w_down):
    args = dict(locals())
    xi, yi, ci = _place()
    me = 4 * xi + 2 * yi + ci
    shard = {n: args[n][0] for n in COL_SHARDED + ROW_SHARDED + ("w_ada",)}
    small_w = {n: args[n] for n, _ in SMALL if n != "loss"}

    (c_all,) = all_gather("gather_c", [c])
    (sc_all,) = rowwise("silu_c", lambda rows, params: ([_silu(rows[0])], []), [c_all.reshape(N_DEV, D_MODEL)], [],
                        [(D_MODEL, MXU_DTYPE)])
    mod_part = matmul("ada_fwd", sc_all, shard["w_ada"], "nn")
    (mod_all,) = all_gather("gather_mod", [mod_part])

    payload = {n: shard[n] if n == "w_conv" else shard[n].astype(MXU_DTYPE) for n in COL_SHARDED + ROW_SHARDED}
    gathers, after_start = [], mod_all
    for i, grp in enumerate(GATHER_GROUPS):
        gathers.append(exchange_start(f"gather{i}_start", [payload[n] for n in grp], gather=True, after=after_start))
        after_start = gathers[-1][-1]
    full = {}

    def fetch(name, after):
        if name not in full:
            (i, grp), = [(i, grp) for i, grp in enumerate(GATHER_GROUPS) if name in grp]
            srcs, lands = exchange_wait(f"gather{i}_wait", gathers[i], True, after)
            for n, src, land in zip(grp, srcs, lands, strict=True):
                stack = lax.dynamic_update_index_in_dim(land, src, me, 0)
                full[n] = to_kernel_layout(n, _gather_cols(stack) if n in COL_SHARDED else _gather_rows(stack))
        return full[name]

    mod_row = lax.dynamic_index_in_dim(mod_all, me, axis=1, keepdims=False).reshape(1, 6 * D_MODEL)
    (mod,) = rowwise("ada_bias", lambda rows, params: ([rows[0] + rows[1]], []), [mod_row, b_ada], [], [(6 * D_MODEL, F32)],
                     dep=after_start)

    own, pending, scatters = {}, {}, {}

    def emit(name, grad):
        grad = from_kernel_layout(name, grad)
        parts = _scatter_cols(grad) if name in COL_SHARDED else _scatter_rows(grad)
        own[name] = lax.dynamic_index_in_dim(parts, me, 0, keepdims=False)
        pending[name] = parts
        for i, grp in enumerate(SCATTER_GROUPS):
            if name == grp[-1]:
                scatters[i] = exchange_start(f"scatter{i}_start", [pending[n] for n in grp], gather=False)
                return scatters[i][-1]
        return None

    pos = positions.reshape(SEQ, 1).astype(F32)
    grad_x, small = _local_step(x[0], pos, mod, loss_target[0], small_w, fetch, emit)

    res, done = {}, grad_x
    for i, grp in enumerate(SCATTER_GROUPS):
        _, lands = exchange_wait(f"scatter{i}_wait", scatters[i], False, done)
        for n, land in zip(grp, lands, strict=True):
            res[n] = adamw(f"adamw_{n}", shard[n], [own[n], land], args["m_" + n][0], args["v_" + n][0], ADAM_TILE.get(n))
            done = res[n][0]
    (small_all,) = all_gather("gather_small", [_pack_small(small)], after=done)
    packed = adamw("adamw_small", _pack_small(small_w), [small_all], _pack_small({n: args["m_" + n] for n in small_w}),
                   _pack_small({n: args["v_" + n] for n in small_w}))
    small_res = [_unpack_small(p) for p in packed]
    dmod_all = small_all.reshape(N_DEV, -1)[:, 1:1 + 6 * D_MODEL]
    dmod_mine = lax.dynamic_slice_in_dim(dmod_all, me * (6 * D_MODEL // N_DEV), 6 * D_MODEL // N_DEV, axis=1)
    g_w_ada = matmul("ada_wgrad", sc_all, dmod_mine, "tn")
    res["w_ada"] = adamw("adamw_w_ada", shard["w_ada"], [g_w_ada], m_w_ada[0], v_w_ada[0], ADAM_TILE["w_ada"])

    def leaf(kind, n):
        if n in res:
            return res[n][kind][None]
        return small_res[kind][n]

    loss = small_res[0]["loss"].reshape(())
    return (loss, grad_x[None], *[leaf(k, n) for k in range(4) for n in OUT_WEIGHTS])
```

```python
import jax
import jax.numpy as jnp
from jax import lax
from jax.experimental import pallas as pl
from jax.experimental.pallas import tpu as pltpu

F32 = jnp.float32
MXU_DTYPE = jnp.bfloat16

N_DEV = 8
D_MODEL = 1024
SEQ = 2048
HEADS = 8
NOPE = 64
ROPE = 32
Q_LORA = 512
KV_LORA = 256
DIL_DIM = 64
DIL_WIDTH = HEADS * DIL_DIM
DILATIONS = (1, 4, 16)
SPAN = 128
D_FF = 2816
LANES = 128
ROPE_THETA = 10000.0
EPS = 1e-6
NEG_INF = -1e30
ADAM_LR, ADAM_B1, ADAM_B2, ADAM_EPS, ADAM_WD, ADAM_STEP = 0.001, 0.9, 0.999, 1e-08, 0.01, 10
VMEM_LIMIT = 56 * 1024 * 1024
MESH_ID = pl.DeviceIdType.MESH

P_QLAT, P_KVLAT, P_KPE, P_QD, P_KD, P_VD, P_END = 0, 512, 768, 896, 1408, 1920, 2432
KPE_LO = 64
MIX_IN = HEADS * LANES + DIL_WIDTH


def _params(**kw):
    return pltpu.CompilerParams(vmem_limit_bytes=VMEM_LIMIT, **kw)


def rowwise(name, fn, rows, params, out_rows, out_accs=(), tm=512, dep=None):
    deps = [] if dep is None else [dep]
    rows = [r if isinstance(r, tuple) else (r, r.shape[1], 0) for r in rows]
    R = rows[0][0].shape[0]
    tm = min(tm, R)
    steps = R // tm
    assert steps * tm == R
    in_specs = []
    for a, width, cb in rows:
        ri = a.shape[0]
        per = ri // tm
        assert per * tm == ri
        if ri == R:
            in_specs.append(pl.BlockSpec((tm, width), lambda i, cb=cb: (i, cb)))
        else:
            in_specs.append(pl.BlockSpec((tm, width), lambda i, per=per, cb=cb: (i % per, cb)))
    for p in params:
        in_specs.append(pl.BlockSpec(p.shape, lambda i: (0,) * p.ndim))
    in_specs += [pl.BlockSpec(memory_space=pl.ANY)] * len(deps)
    out_shape = [jax.ShapeDtypeStruct((R, d), dt) for d, dt in out_rows]
    out_specs = [pl.BlockSpec((tm, d), lambda i: (i, 0)) for d, _ in out_rows]
    out_shape += [jax.ShapeDtypeStruct((1, n), F32) for n in out_accs]
    out_specs += [pl.BlockSpec((1, n), lambda i: (0, 0)) for n in out_accs]
    nr, npar, no, na = len(rows), len(params), len(out_rows), len(out_accs)

    def body(*refs):
        rvals = [r[...] for r in refs[:nr]]
        pvals = [r[...] for r in refs[nr:nr + npar]]
        outs, accs = fn(rvals, pvals)
        first_out = nr + npar + len(deps)
        for ref, v in zip(refs[first_out:first_out + no], outs, strict=True):
            ref[...] = v.astype(ref.dtype)
        if na:
            acc_refs = refs[first_out + no:]
            i = pl.program_id(0)

            @pl.when(i == 0)
            def _():
                for ref, v in zip(acc_refs, accs, strict=True):
                    ref[...] = v

            @pl.when(i > 0)
            def _():
                for ref, v in zip(acc_refs, accs, strict=True):
                    ref[...] += v

    res = pl.pallas_call(body, name=name, grid=(steps,), in_specs=in_specs, out_specs=out_specs,
                         out_shape=out_shape, compiler_params=_params())(*[r[0] for r in rows], *params, *deps)
    return list(res)


_DIMS = {"nn": ((1,), (0,)), "nt": ((1,), (1,)), "tn": ((0,), (0,))}


def _dot(a, b, mode="nn"):
    return lax.dot_general(a.astype(MXU_DTYPE), b.astype(MXU_DTYPE), (_DIMS[mode], ((), ())),
                           preferred_element_type=F32)


def matmul(name, a, b, mode, tm=None, tn=None, tk=None, out_dtype=F32, dep=None, a_mmap=None, b_nmap=None, b_kmap=None):
    if mode == "tn":
        K, M = a.shape
    else:
        M, K = a.shape
    N = b.shape[0] if mode == "nt" else b.shape[1]
    tm, tn, tk = tm or M, tn or N, tk or K
    nm, nn, nk = M // tm, N // tn, K // tk
    assert nm * tm == M and nn * tn == N and nk * tk == K
    same = lambda idx: idx
    a_mmap, b_nmap, b_kmap = a_mmap or same, b_nmap or same, b_kmap or same
    if mode == "tn":
        a_spec = pl.BlockSpec((tk, tm), lambda i, j, k: (k, a_mmap(i)))
    else:
        a_spec = pl.BlockSpec((tm, tk), lambda i, j, k: (a_mmap(i), k))
    if mode == "nt":
        b_spec = pl.BlockSpec((tn, tk), lambda i, j, k: (b_nmap(j), b_kmap(k)))
    else:
        b_spec = pl.BlockSpec((tk, tn), lambda i, j, k: (b_kmap(k), b_nmap(j)))
    deps = [] if dep is None else [dep]

    def body(a_ref, b_ref, *rest):
        o_ref, scratch = rest[len(deps)], rest[len(deps) + 1:]
        p = _dot(a_ref[...], b_ref[...], mode)
        if nk == 1:
            o_ref[...] = p.astype(o_ref.dtype)
        else:
            acc = scratch[0]
            k = pl.program_id(2)

            @pl.when(k == 0)
            def _():
                acc[...] = p

            @pl.when(k > 0)
            def _():
                acc[...] += p

            @pl.when(k == nk - 1)
            def _():
                o_ref[...] = acc[...].astype(o_ref.dtype)

    return pl.pallas_call(
        body, name=name, grid=(nm, nn, nk), in_specs=[a_spec, b_spec] + [pl.BlockSpec(memory_space=pl.ANY)] * len(deps),
        out_specs=pl.BlockSpec((tm, tn), lambda i, j, k: (i, j)),
        out_shape=jax.ShapeDtypeStruct((M, N), out_dtype),
        scratch_shapes=[pltpu.VMEM((tm, tn), F32)] if nk > 1 else [],
        compiler_params=_params())(a, b, *deps)


def _rms(x, g):
    rstd = lax.rsqrt(jnp.mean(x * x, axis=-1, keepdims=True) + EPS)
    n = x * rstd
    return n * g, n, rstd


def _rms_bwd(dy, n, rstd, g):
    dg = jnp.sum(dy * n, axis=0, keepdims=True)
    dn = dy * g
    dx = rstd * (dn - n * jnp.mean(dn * n, axis=-1, keepdims=True))
    return dx, dg


def _norm_bwd(dy, x, g):
    _, n, rstd = _rms(x, g)
    return _rms_bwd(dy, n, rstd, g)


def _colsum(v):
    return jnp.sum(v, axis=0, keepdims=True)


def _silu(x):
    return x * (1.0 / (1.0 + jnp.exp(-x)))


def _lane(shape):
    return lax.broadcasted_iota(jnp.int32, shape, 1)


def _group_mean(v, groups):
    lane = _lane(v.shape)
    out = jnp.zeros_like(v)
    for lo, hi in groups:
        m = (lane >= lo) & (lane < hi)
        out = jnp.where(m, jnp.sum(jnp.where(m, v, 0.0), axis=-1, keepdims=True) * (1.0 / (hi - lo)), out)
    return out


def _in_groups(shape, groups):
    lane = _lane(shape)
    m = jnp.zeros(shape, jnp.bool_)
    for lo, hi in groups:
        m = m | ((lane >= lo) & (lane < hi))
    return m


def _grms(x, g, groups):
    rstd = lax.rsqrt(_group_mean(x * x, groups) + EPS)
    n = jnp.where(_in_groups(x.shape, groups), x * rstd, 0.0)
    return n * g, n, rstd


def _grms_bwd(dy, n, rstd, g, groups):
    dn = dy * g
    return rstd * (dn - n * _group_mean(dn * n, groups)), _colsum(dy * n)


def _rot(x, half, transpose=False):
    first = (_lane(x.shape) % (2 * half)) < half
    up = pltpu.roll(x, LANES - half, axis=1)
    down = pltpu.roll(x, half, axis=1)
    return jnp.where(first, up, -down) if transpose else jnp.where(first, -up, down)


def _rope(x, cos, sin, half):
    return x * cos + _rot(x, half) * sin


def _rope_bwd(dy, cos, sin, half):
    return dy * cos + _rot(dy * sin, half, transpose=True)


def _chunks(x):
    return [x[:, i:i + LANES] for i in range(0, x.shape[1], LANES)]


Q_GROUPS = ((0, NOPE), (NOPE, NOPE + ROPE))
K_GROUPS = ((0, NOPE),)
KPE_GROUPS = ((KPE_LO, KPE_LO + ROPE),)
DIL_GROUPS = ((0, DIL_DIM), (DIL_DIM, 2 * DIL_DIM))


def _col(width, rows=SEQ):
    return pl.BlockSpec((rows, width), lambda h: (0, h))


def _causal_tail(s, tq, fill):
    diag = s[:, s.shape[1] - tq:]
    keep = lax.broadcasted_iota(jnp.int32, diag.shape, 1) <= lax.broadcasted_iota(jnp.int32, diag.shape, 0)
    diag = jnp.where(keep, diag, fill)
    return diag if s.shape[1] == tq else jnp.concatenate([s[:, :s.shape[1] - tq], diag], axis=1)


def mla_fwd(name, q, k, v, scale, tq=512):
    S = q.shape[0]

    def body(q_ref, k_ref, v_ref, o_ref, lse_ref):
        for i in range(S // tq):
            kext = (i + 1) * tq
            blk = slice(i * tq, kext)
            s = _causal_tail(_dot(q_ref[blk, :], k_ref[:kext, :], "nt") * scale, tq, NEG_INF)
            m = jnp.max(s, axis=-1, keepdims=True)
            e = jnp.exp(s - m)
            l = jnp.sum(e, axis=-1, keepdims=True)
            o_ref[blk, :] = _dot(e / l, v_ref[:kext, :])
            lse_ref[0, blk, :] = m + jnp.log(l)

    return pl.pallas_call(
        body, name=name, grid=(HEADS,), in_specs=[_col(LANES)] * 3,
        out_specs=[_col(LANES), pl.BlockSpec((1, S, 1), lambda h: (h, 0, 0))],
        out_shape=[jax.ShapeDtypeStruct((S, MIX_IN), F32), jax.ShapeDtypeStruct((HEADS, S, 1), F32)],
        compiler_params=_params())(q, k, v)


def mla_bwd(name, q, k, v, o, do, lse, scale, tq=512):
    S = q.shape[0]

    def body(q_ref, k_ref, v_ref, o_ref, do_ref, lse_ref, dq_ref, dkv_ref, dkpe_ref, dk_acc, dv_acc):
        dk_acc[...] = jnp.zeros_like(dk_acc)
        dv_acc[...] = jnp.zeros_like(dv_acc)
        for i in range(S // tq):
            kext = (i + 1) * tq
            blk = slice(i * tq, kext)
            qi, kk, vv = q_ref[blk, :], k_ref[:kext, :], v_ref[:kext, :]
            doi = do_ref[blk, :]
            s = _causal_tail(_dot(qi, kk, "nt") * scale, tq, NEG_INF)
            p = jnp.exp(s - lse_ref[0, blk, :])
            dp = _dot(doi, vv, "nt")
            delta = jnp.sum(doi * o_ref[blk, :], axis=-1, keepdims=True)
            ds = p * (dp - delta) * scale
            dq_ref[blk, :] = _dot(ds, kk)
            dk_acc[:kext, :] += _dot(ds, qi, "tn")
            dv_acc[:kext, :] += _dot(p, doi, "tn")
        dk = dk_acc[...]
        lane = _lane(dk.shape)
        dkv_ref[...] = jnp.where(lane < NOPE, dk, 0.0) + dv_acc[...]
        dkpe = jnp.where((lane >= KPE_LO) & (lane < KPE_LO + ROPE), dk, 0.0)
        h = pl.program_id(0)

        @pl.when(h == 0)
        def _():
            dkpe_ref[...] = dkpe

        @pl.when(h > 0)
        def _():
            dkpe_ref[...] += dkpe

    return pl.pallas_call(
        body, name=name, grid=(HEADS,),
        in_specs=[_col(LANES)] * 5 + [pl.BlockSpec((1, S, 1), lambda h: (h, 0, 0))],
        out_specs=[_col(LANES), _col(LANES), pl.BlockSpec((S, LANES), lambda h: (0, 0))],
        out_shape=[jax.ShapeDtypeStruct((S, HEADS * LANES), F32), jax.ShapeDtypeStruct((S, HEADS * LANES), F32),
                   jax.ShapeDtypeStruct((S, LANES), F32)],
        scratch_shapes=[pltpu.VMEM((S, LANES), F32), pltpu.VMEM((S, LANES), F32)],
        compiler_params=_params())(q, k, v, o, do, lse)


def _band_blocks(L, tq):
    return [(i * tq, (i + 1) * tq, max(0, i * tq - SPAN)) for i in range(L // tq)]


def _band_mask(q0, q1, k0):
    shape = (q1 - q0, q1 - k0)
    dist = (lax.broadcasted_iota(jnp.int32, shape, 0) + q0) - (lax.broadcasted_iota(jnp.int32, shape, 1) + k0)
    return (dist >= 0) & (dist <= SPAN)


def _class_rows(r, dil, lo, hi):
    return pl.ds(r + dil * lo, hi - lo, stride=dil) if dil > 1 else pl.ds(lo, hi - lo)


def _stack_heads(t, lo):
    zero = jnp.zeros_like(t)
    return jnp.concatenate([jnp.where(lo, t, zero), jnp.where(lo, zero, t)], axis=0)


def _band_mask2(q0, q1, k0):
    n = q1 - q0
    shape = (2 * n, q1 - k0)
    i = lax.broadcasted_iota(jnp.int32, shape, 0)
    dist = (jnp.where(i >= n, i - n, i) + q0) - (lax.broadcasted_iota(jnp.int32, shape, 1) + k0)
    return (dist >= 0) & (dist <= SPAN)


def _pair_col(col0=0):
    return pl.BlockSpec((SEQ, LANES), lambda j: (0, col0 // LANES + j))


def band_fwd(name, q, k, v, dil):
    S = q.shape[0]
    L = S // dil
    tq = min(L, 512)
    scale = DIL_DIM ** -0.5

    def body(q_ref, k_ref, v_ref, o_ref, lse_ref):
        for r in range(dil):
            for q0, q1, k0 in _band_blocks(L, tq):
                qrows, krows = _class_rows(r, dil, q0, q1), _class_rows(r, dil, k0, q1)
                qb, kb, vb = q_ref[qrows, :].astype(MXU_DTYPE), k_ref[krows, :], v_ref[krows, :]
                n = q1 - q0
                lo = _lane(qb.shape) < DIL_DIM
                s = _dot(_stack_heads(qb, lo), kb, "nt") * scale
                s = jnp.where(_band_mask2(q0, q1, k0), s, NEG_INF)
                mx = jnp.max(s, axis=-1, keepdims=True)
                e = jnp.exp(s - mx)
                l = jnp.sum(e, axis=-1, keepdims=True)
                pv, lse = _dot(e / l, vb), mx + jnp.log(l)
                o_ref[qrows, :] = jnp.where(lo, pv[:n], pv[n:])
                lse_ref[qrows, :] = jnp.where(lo, lse[:n], lse[n:])

    return pl.pallas_call(
        body, name=name, grid=(DIL_WIDTH // LANES,), in_specs=[_pair_col()] * 2 + [_pair_col(P_VD)], out_specs=[_pair_col()] * 2,
        out_shape=[jax.ShapeDtypeStruct((S, DIL_WIDTH), F32)] * 2, compiler_params=_params())(q, k, v)


def band_bwd(name, q, k, v, lse, lse_mix, o_cat, do_cat, dil):
    S = q.shape[0]
    L = S // dil
    tq = min(L, 512)
    scale = DIL_DIM ** -0.5

    def body(q_ref, k_ref, v_ref, lse_ref, mix_ref, o_ref, do_ref, dq_ref, dk_ref, dv_ref):
        dk_ref[...] = jnp.zeros_like(dk_ref)
        dv_ref[...] = jnp.zeros_like(dv_ref)
        for r in range(dil):
            for q0, q1, k0 in _band_blocks(L, tq):
                qrows, krows = _class_rows(r, dil, q0, q1), _class_rows(r, dil, k0, q1)
                qb, kb, vb = q_ref[qrows, :].astype(MXU_DTYPE), k_ref[krows, :], v_ref[krows, :]
                lse_p, dout = lse_ref[qrows, :], do_ref[qrows, :]
                n = q1 - q0
                lo = _lane(qb.shape) < DIL_DIM
                per_head = lambda t: jnp.concatenate([t[:, 0:1], t[:, DIL_DIM:DIL_DIM + 1]], axis=0)
                w2 = per_head(jnp.exp(lse_p - mix_ref[qrows, :]))
                dd = dout * o_ref[qrows, :]
                big_d = jnp.concatenate([jnp.sum(jnp.where(lo, dd, 0.0), axis=-1, keepdims=True),
                                         jnp.sum(jnp.where(lo, 0.0, dd), axis=-1, keepdims=True)], axis=0)
                q2 = _stack_heads(qb, lo)
                s = _dot(q2, kb, "nt") * scale
                p = jnp.where(_band_mask2(q0, q1, k0), jnp.exp(s - per_head(lse_p)), 0.0)
                dom = _stack_heads(dout, lo) * w2
                ds = p * (_dot(dom, vb, "nt") - w2 * big_d) * scale
                dq2 = _dot(ds, kb)
                dq_ref[qrows, :] = jnp.where(lo, dq2[:n], dq2[n:])
                dk_ref[krows, :] += _dot(ds, q2, "tn")
                dv_ref[krows, :] += _dot(p, dom, "tn")

    cat = _pair_col(HEADS * LANES)
    return pl.pallas_call(
        body, name=name, grid=(DIL_WIDTH // LANES,),
        in_specs=[_pair_col()] * 2 + [_pair_col(P_VD)] + [_pair_col()] * 2 + [cat] * 2, out_specs=[_pair_col()] * 3,
        out_shape=[jax.ShapeDtypeStruct((S, DIL_WIDTH), F32)] * 3,
        compiler_params=_params())(q, k, v, lse, lse_mix, o_cat, do_cat)


def combine_fwd(name, outs, lses, o_cat, tm=512):
    S = outs[0].shape[0]

    def body(o1, o2, o3, l1, l2, l3, cat_in, cat_out, mix_ref):
        ls = [l1[...], l2[...], l3[...]]
        m = jnp.maximum(jnp.maximum(ls[0], ls[1]), ls[2])
        e = [jnp.exp(l - m) for l in ls]
        den = e[0] + e[1] + e[2]
        cat_out[...] = (e[0] / den) * o1[...] + (e[1] / den) * o2[...] + (e[2] / den) * o3[...]
        mix_ref[...] = m + jnp.log(den)

    row = pl.BlockSpec((tm, DIL_WIDTH), lambda i: (i, 0))
    return pl.pallas_call(
        body, name=name, grid=(S // tm,), in_specs=[row] * 6 + [pl.BlockSpec(memory_space=pl.ANY)],
        out_specs=[pl.BlockSpec((tm, DIL_WIDTH), lambda i: (i, HEADS * LANES // DIL_WIDTH)), row],
        out_shape=[jax.ShapeDtypeStruct(o_cat.shape, F32), jax.ShapeDtypeStruct((S, DIL_WIDTH), F32)],
        input_output_aliases={6: 0}, compiler_params=_params())(*outs, *lses, o_cat)


def _shift_down(u, n):
    t = lax.broadcasted_iota(jnp.int32, u.shape, 0)
    return jnp.where(t >= n, pltpu.roll(u, n, axis=0), 0.0)


def _shift_up(u, n):
    rows = u.shape[0]
    t = lax.broadcasted_iota(jnp.int32, u.shape, 0)
    return jnp.where(t < rows - n, pltpu.roll(u, rows - n, axis=0), 0.0)


def _conv(u, w, b):
    return w[2:3, :] * u + w[1:2, :] * _shift_down(u, 1) + w[0:1, :] * _shift_down(u, 2) + b


CONV_TC = 256
CONV_NB = D_FF // CONV_TC


def paired_block(j):
    return (j % 2) * CONV_NB + j // 2


def natural_block(j):
    return jnp.where(j < CONV_NB, 2 * j, 2 * (j - CONV_NB) + 1)


def _pair_spec(rows):
    return pl.BlockSpec((rows, 2 * CONV_TC), lambda j: (0, j))


def _half_specs(rows):
    return [pl.BlockSpec((rows, CONV_TC), lambda j: (0, j)), pl.BlockSpec((rows, CONV_TC), lambda j: (0, j + CONV_NB))]


def pair_dgrad(name, dup, w_t, dep=None):
    S, D = dup.shape[0], w_t.shape[1]
    deps = [] if dep is None else [dep]

    def body(a_ref, bg_ref, bv_ref, *rest):
        o_ref = rest[-1]
        p = _dot(a_ref[:, :CONV_TC], bg_ref[...]) + _dot(a_ref[:, CONV_TC:], bv_ref[...])
        j = pl.program_id(0)

        @pl.when(j == 0)
        def _():
            o_ref[...] = p

        @pl.when(j > 0)
        def _():
            o_ref[...] += p

    return pl.pallas_call(
        body, name=name, grid=(CONV_NB,),
        in_specs=[_pair_spec(S), pl.BlockSpec((CONV_TC, D), lambda j: (j, 0)), pl.BlockSpec((CONV_TC, D), lambda j: (j + CONV_NB, 0))]
        + [pl.BlockSpec(memory_space=pl.ANY)] * len(deps),
        out_specs=pl.BlockSpec((S, D), lambda j: (0, 0)), out_shape=jax.ShapeDtypeStruct((S, D), F32),
        compiler_params=_params())(dup, w_t, w_t, *deps)


def conv_glu_fwd(name, up, w_conv, b_conv):
    S = up.shape[0]

    def body(u_ref, wg_ref, wv_ref, bg_ref, bv_ref, act_ref):
        w = jnp.concatenate([wg_ref[...], wv_ref[...]], axis=1)
        u = _conv(u_ref[...], w, jnp.concatenate([bg_ref[...], bv_ref[...]], axis=1))
        act_ref[...] = (_silu(u[:, :CONV_TC]) * u[:, CONV_TC:]).astype(act_ref.dtype)

    return pl.pallas_call(
        body, name=name, grid=(CONV_NB,), in_specs=[_pair_spec(S)] + _half_specs(3) + _half_specs(1),
        out_specs=pl.BlockSpec((S, CONV_TC), lambda j: (0, j)), out_shape=jax.ShapeDtypeStruct((S, D_FF), MXU_DTYPE),
        compiler_params=_params())(up, w_conv, w_conv, b_conv, b_conv)


def conv_glu_bwd(name, up, w_conv, b_conv, dact):
    S = up.shape[0]

    def body(u_ref, wg_ref, wv_ref, bg_ref, bv_ref, da_ref, dup_ref, dwg_ref, dwv_ref, dbg_ref, dbv_ref):
        uin = u_ref[...]
        w = jnp.concatenate([wg_ref[...], wv_ref[...]], axis=1)
        u = _conv(uin, w, jnp.concatenate([bg_ref[...], bv_ref[...]], axis=1))
        gate, val, da = u[:, :CONV_TC], u[:, CONV_TC:], da_ref[...]
        sig = 1.0 / (1.0 + jnp.exp(-gate))
        du = jnp.concatenate([da * val * (sig * (1.0 + gate * (1.0 - sig))), da * (gate * sig)], axis=1)
        dup_ref[...] = (w[2:3, :] * du + w[1:2, :] * _shift_up(du, 1) + w[0:1, :] * _shift_up(du, 2)).astype(dup_ref.dtype)
        dw = jnp.concatenate([_colsum(du * _shift_down(uin, 2)), _colsum(du * _shift_down(uin, 1)), _colsum(du * uin)], axis=0)
        db = _colsum(du)
        dwg_ref[...], dwv_ref[...] = dw[:, :CONV_TC], dw[:, CONV_TC:]
        dbg_ref[...], dbv_ref[...] = db[:, :CONV_TC], db[:, CONV_TC:]

    half = lambda rows: pl.BlockSpec((rows, CONV_TC), lambda j: (0, j))
    dup, dwg, dwv, dbg, dbv = pl.pallas_call(
        body, name=name, grid=(CONV_NB,),
        in_specs=[_pair_spec(S)] + _half_specs(3) + _half_specs(1) + [half(S)],
        out_specs=[_pair_spec(S), half(3), half(3), half(1), half(1)],
        out_shape=[jax.ShapeDtypeStruct((S, 2 * D_FF), MXU_DTYPE)] + [jax.ShapeDtypeStruct((3, D_FF), F32)] * 2
        + [jax.ShapeDtypeStruct((1, D_FF), F32)] * 2,
        compiler_params=_params())(up, w_conv, w_conv, b_conv, b_conv, dact)
    return dup, jnp.concatenate([dwg, dwv], axis=1), jnp.concatenate([dbg, dbv], axis=1)


def adamw(name, w, parts, m, v, tr=None):
    R, C = w.shape
    tr = tr or R
    assert R % tr == 0
    c1 = 1.0 - ADAM_B1 ** ADAM_STEP
    c2 = 1.0 - ADAM_B2 ** ADAM_STEP
    np_ = len(parts)

    def body(*refs):
        w_ref, m_ref, v_ref = refs[0], refs[1 + np_], refs[2 + np_]
        go_ref, d_ref, mo_ref, vo_ref = refs[3 + np_:]
        terms = []
        for part, ref in zip(parts, refs[1:1 + np_], strict=True):
            terms += [ref[...]] if part.ndim == 2 else [ref[p] for p in range(part.shape[0])]
        g = terms[0].astype(F32)
        for term in terms[1:]:
            g = g + term.astype(F32)
        m2 = ADAM_B1 * m_ref[...] + (1.0 - ADAM_B1) * g
        v2 = ADAM_B2 * v_ref[...] + (1.0 - ADAM_B2) * (g * g)
        go_ref[...] = g
        mo_ref[...] = m2
        vo_ref[...] = v2
        d_ref[...] = -ADAM_LR * ((m2 / c1) / (jnp.sqrt(v2 / c2) + ADAM_EPS) + ADAM_WD * w_ref[...])

    blk = pl.BlockSpec((tr, C), lambda i: (i, 0))
    part_specs = [blk if p.ndim == 2 else pl.BlockSpec((p.shape[0], tr, C), lambda i: (0, i, 0)) for p in parts]
    return pl.pallas_call(
        body, name=name, grid=(R // tr,),
        in_specs=[blk] + part_specs + [blk, blk], out_specs=[blk] * 4,
        out_shape=[jax.ShapeDtypeStruct((R, C), F32)] * 4, compiler_params=_params())(w, *parts, m, v)


def _place():
    return lax.axis_index("x"), lax.axis_index("y"), lax.axis_index("c")


def all_gather(name, arrs, after=None):
    n = len(arrs)
    deps = [] if after is None else [after]

    def body(*refs):
        ins, outs = refs[:n], refs[n + len(deps):2 * n + len(deps)]
        send_sems, recv_sems, local_sems = refs[2 * n + len(deps):]
        x, y, c = _place()
        me, sibling = (x, y, c), (x, y, 1 - c)
        chips = [(1 - x, y), (x, 1 - y), (1 - x, 1 - y)]
        sends = []
        for t in range(n):
            out = outs[t]

            def slot(px, py, pc, out=out):
                return out.at[4 * px + 2 * py + pc]

            def copy(k, block, to, src=None, t=t, slot=slot):
                return pltpu.make_async_remote_copy(
                    src_ref=slot(*block) if src is None else src, dst_ref=slot(*block),
                    send_sem=send_sems.at[7 * t + k], recv_sem=recv_sems.at[7 * t + k],
                    device_id=to, device_id_type=MESH_ID)

            mine = pltpu.make_async_copy(ins[t], slot(*me), local_sems.at[t])
            mine.start()
            first = [copy(0, me, sibling, src=ins[t])]
            first += [copy(1 + j, me, (*chip, c), src=ins[t]) for j, chip in enumerate(chips)]
            for cp in first:
                cp.start()
            sends.append((mine, first, copy))
        for t in range(n):
            mine, first, copy = sends[t]
            passed = [copy(4 + j, (*chip, c), sibling) for j, chip in enumerate(chips)]
            for j, chip in enumerate(chips):
                copy(1 + j, (*chip, c), me).wait_recv()
                passed[j].start()
            copy(0, sibling, me).wait_recv()
            for j, chip in enumerate(chips):
                copy(4 + j, (*chip, 1 - c), me).wait_recv()
            for cp in first + passed:
                cp.wait_send()
            mine.wait()

    any_spec = pl.BlockSpec(memory_space=pl.ANY)
    res = pl.pallas_call(
        body, name=name, in_specs=[any_spec] * (n + len(deps)), out_specs=[any_spec] * n,
        out_shape=[jax.ShapeDtypeStruct((N_DEV,) + a.shape, a.dtype) for a in arrs],
        scratch_shapes=[pltpu.SemaphoreType.DMA((7 * n,)), pltpu.SemaphoreType.DMA((7 * n,)), pltpu.SemaphoreType.DMA((n,))],
        compiler_params=pltpu.CompilerParams(has_side_effects=True))(*arrs, *deps)
    return list(res)


HBM_SPEC = pl.BlockSpec(memory_space=pltpu.HBM)
SEM_SPEC = pl.BlockSpec(memory_space=pltpu.SEMAPHORE)
DATAFLOW = pltpu.SideEffectType.DATAFLOW_SIDE_EFFECTING


def _exchange_copies(srcs, lands, send_sems, recv_sems, gather):
    x, y, c = _place()
    me = 4 * x + 2 * y + c
    out = []
    for t, (src, land) in enumerate(zip(srcs, lands, strict=True)):
        for k in range(1, N_DEV):
            px, py, pc = x ^ (k >> 2), y ^ ((k >> 1) & 1), c ^ (k & 1)
            out.append(pltpu.make_async_remote_copy(
                src_ref=src if gather else src.at[4 * px + 2 * py + pc],
                dst_ref=land.at[me] if gather else land.at[k - 1],
                send_sem=send_sems.at[7 * t + k - 1], recv_sem=recv_sems.at[7 * t + k - 1],
                device_id=(px, py, pc), device_id_type=MESH_ID))
    return out


def exchange_start(name, arrs, gather, after=None):
    n = len(arrs)
    lands = [lax.empty(((N_DEV,) + a.shape) if gather else ((N_DEV - 1,) + a.shape[1:]), a.dtype) for a in arrs]
    deps = [] if after is None else [after]

    def body(*refs):
        srcs, land_refs = refs[:n], refs[n:2 * n]
        send_sems, recv_sems = refs[2 * n + len(deps)], refs[2 * n + len(deps) + 1]
        token = refs[-1]
        for cp in _exchange_copies(srcs, land_refs, send_sems, recv_sems, gather):
            cp.start()
        token[...] = jnp.zeros_like(token)

    hbm = lambda a: pltpu.HBM(a.shape, a.dtype)
    res = pl.pallas_call(
        body, name=name,
        out_shape=(pltpu.SemaphoreType.DMA((7 * n,)), pltpu.SemaphoreType.DMA((7 * n,)), *[hbm(a) for a in arrs],
                   *[hbm(l) for l in lands], jax.ShapeDtypeStruct((8, 128), F32)),
        in_specs=[HBM_SPEC] * (2 * n) + [pl.BlockSpec(memory_space=pl.ANY)] * len(deps),
        out_specs=(SEM_SPEC, SEM_SPEC, *[HBM_SPEC] * (2 * n), pl.BlockSpec(memory_space=pltpu.VMEM)),
        input_output_aliases={i: 2 + i for i in range(2 * n)},
        compiler_params=pltpu.CompilerParams(has_side_effects=DATAFLOW),
    )(*[pltpu.with_memory_space_constraint(a, pltpu.HBM) for a in arrs + lands], *deps)
    return res[0], res[1], list(res[2:2 + n]), list(res[2 + n:2 + 2 * n]), res[-1]


def exchange_wait(name, started, gather, after):
    send_sems, recv_sems, srcs, lands, _ = started
    n = len(srcs)

    def body(*refs):
        src_refs, land_refs = refs[:n], refs[n:2 * n]
        copies = _exchange_copies(src_refs, land_refs, refs[2 * n], refs[2 * n + 1], gather)
        for cp in copies:
            cp.wait_send()
        for cp in copies:
            cp.wait_recv()

    hbm = lambda a: pltpu.HBM(a.shape, a.dtype)
    res = pl.pallas_call(
        body, name=name, out_shape=tuple(hbm(a) for a in srcs + lands),
        in_specs=[HBM_SPEC] * (2 * n) + [SEM_SPEC, SEM_SPEC, pl.BlockSpec(memory_space=pl.ANY)],
        out_specs=tuple([HBM_SPEC] * (2 * n)), input_output_aliases={i: i for i in range(2 * n)},
        compiler_params=pltpu.CompilerParams(has_side_effects=DATAFLOW),
    )(*srcs, *lands, send_sems, recv_sems, after)
    return list(res[:n]), list(res[n:])


def _gather_cols(stack):
    p, k, n = stack.shape
    return stack.transpose(1, 0, 2).reshape(k, p * n)


def _scatter_cols(full):
    k, n = full.shape
    return full.reshape(k, N_DEV, n // N_DEV).transpose(1, 0, 2)


def _gather_rows(stack):
    p, r, n = stack.shape
    return stack.reshape(p * r, n)


def _scatter_rows(full):
    r, n = full.shape
    return full.reshape(N_DEV, r // N_DEV, n)


_IN_NAT = Q_LORA + KV_LORA
TRANSPOSED = ("w_in", "w_q_b", "w_up")


def to_kernel_layout(name, w):
    if name == "w_in":
        z = lambda n: jnp.zeros((n, w.shape[1]), w.dtype)
        return jnp.concatenate([w[:_IN_NAT], z(KPE_LO), w[_IN_NAT:_IN_NAT + ROPE], z(LANES - KPE_LO - ROPE), w[_IN_NAT + ROPE:]], axis=0)
    if name == "w_q_b":
        return jnp.pad(w.reshape(HEADS, NOPE + ROPE, -1), ((0, 0), (0, LANES - NOPE - ROPE), (0, 0))).reshape(HEADS * LANES, -1)
    if name == "w_o":
        mla = jnp.pad(w[:HEADS * NOPE].reshape(HEADS, NOPE, -1), ((0, 0), (LANES - NOPE, 0), (0, 0))).reshape(HEADS * LANES, -1)
        return jnp.concatenate([mla, w[HEADS * NOPE:]], axis=0)
    return w


def from_kernel_layout(name, g):
    if name == "w_in":
        return jnp.concatenate([g[:_IN_NAT], g[P_KPE + KPE_LO:P_KPE + KPE_LO + ROPE], g[P_QD:]], axis=0)
    if name == "w_q_b":
        return g.reshape(HEADS, LANES, -1)[:, :NOPE + ROPE, :].reshape(HEADS * (NOPE + ROPE), -1)
    if name == "w_o":
        mla = g[:HEADS * LANES].reshape(HEADS, LANES, -1)[:, LANES - NOPE:, :].reshape(HEADS * NOPE, -1)
        return jnp.concatenate([mla, g[HEADS * LANES:]], axis=0)
    return g


SMALL = (("loss", 1), ("b_ada", 6 * D_MODEL), ("g_mix_norm", D_MODEL), ("g_q_lat", Q_LORA), ("g_kv_lat", KV_LORA),
         ("g_mla_q_nope", NOPE), ("g_mla_q_pe", ROPE), ("g_mla_k_nope", NOPE), ("g_mla_k_pe", ROPE),
         ("g_dil_q", DIL_DIM), ("g_dil_k", DIL_DIM), ("g_ffn_norm", D_MODEL), ("b_conv", 2 * D_FF))
SMALL_ROWS = 16
SMALL_COLS = 1024


def _pack_small(values):
    parts = [values[name].reshape(-1).astype(F32) if name in values else jnp.zeros((n,), F32) for name, n in SMALL]
    flat = jnp.concatenate(parts)
    flat = jnp.pad(flat, (0, SMALL_ROWS * SMALL_COLS - flat.shape[0]))
    return flat.reshape(SMALL_ROWS, SMALL_COLS)


def _unpack_small(packed):
    flat = packed.reshape(-1)
    out, off = {}, 0
    for name, n in SMALL:
        out[name] = flat[off:off + n].reshape(1, n)
        off += n
    return out


def _local_step(x, pos, mod, target, w, fetch, emit):
    S = SEQ
    sh1, sc1, g1, sh2, sc2, g2 = [mod[:, i * D_MODEL:(i + 1) * D_MODEL] for i in range(6)]
    zeros = lambda n: jnp.zeros((1, n), F32)
    g_q = jnp.concatenate([w["g_mla_q_nope"], w["g_mla_q_pe"], zeros(LANES - NOPE - ROPE)], axis=1)
    g_k = jnp.concatenate([w["g_mla_k_nope"], zeros(LANES - NOPE)], axis=1)
    g_kpe = jnp.concatenate([zeros(KPE_LO), w["g_mla_k_pe"], zeros(LANES - KPE_LO - ROPE)], axis=1)
    g_dq = jnp.concatenate([w["g_dil_q"]] * 2, axis=1)
    g_dk = jnp.concatenate([w["g_dil_k"]] * 2, axis=1)
    b_conv = w["b_conv"]

    def inv_freq(d):
        return jnp.power(ROPE_THETA, -2.0 * jnp.arange(d // 2, dtype=F32) / d)

    f_mla = jnp.concatenate([jnp.zeros((KPE_LO,), F32), inv_freq(ROPE), inv_freq(ROPE), jnp.zeros((LANES - KPE_LO - ROPE,), F32)])
    f_dil = jnp.concatenate([inv_freq(DIL_DIM)] * 4)

    def tables_fn(rows, params):
        (p,), (fa, fb) = rows, params
        return [jnp.cos(p * fa), jnp.sin(p * fa), jnp.cos(p * fb), jnp.sin(p * fb)], []

    cos_m, sin_m, cos_d, sin_d = rowwise("rope_tables", tables_fn, [pos], [f_mla.reshape(1, LANES), f_dil.reshape(1, LANES)],
                                         [(LANES, F32)] * 4)
    tables = [cos_m, sin_m, cos_d, sin_d]
    H_M, H_D = ROPE // 2, DIL_DIM // 2

    def ln1_fn(rows, params):
        (xv,), (g, sc, sh) = rows, params
        y, _, _ = _rms(xv, g)
        return [y * (1.0 + sc) + sh], []

    (h,) = rowwise("ln1_fwd", ln1_fn, [x], [w["g_mix_norm"], sc1, sh1], [(D_MODEL, MXU_DTYPE)])
    w_in = fetch("w_in", h)
    proj = matmul("proj_fwd", h, w_in, "nt", tm=512)

    def post_fn(rows, params):
        (pv, cm, sm, cd, sd), (gq, gkv, gkp, gdq, gdk) = rows, params
        kper = _rope(_grms(pv[:, P_KPE:P_QD], gkp, KPE_GROUPS)[0], cm, sm, H_M)
        qd = [_rope(_grms(c, gdq, DIL_GROUPS)[0], cd, sd, H_D) for c in _chunks(pv[:, P_QD:P_KD])]
        kd = [_rope(_grms(c, gdk, DIL_GROUPS)[0], cd, sd, H_D) for c in _chunks(pv[:, P_KD:P_VD])]
        return [_rms(pv[:, P_QLAT:P_KVLAT], gq)[0], _rms(pv[:, P_KVLAT:P_KPE], gkv)[0], kper,
                jnp.concatenate(qd, axis=1), jnp.concatenate(kd, axis=1)], []

    post_params = [w["g_q_lat"], w["g_kv_lat"], g_kpe, g_dq, g_dk]
    qln, kvn, kper, qd_r, kd_r = rowwise(
        "proj_post", post_fn, [proj] + tables, post_params,
        [(Q_LORA, MXU_DTYPE), (KV_LORA, MXU_DTYPE), (LANES, MXU_DTYPE)] + [(DIL_WIDTH, F32)] * 2, tm=256)
    w_q_b, w_kv_b = fetch("w_q_b", qln), fetch("w_kv_b", kvn)
    q = matmul("q_fwd", qln, w_q_b, "nt", tm=1024)
    kv = matmul("kv_fwd", kvn, w_kv_b, "nn", tm=1024)

    def mla_prep_fn(rows, params):
        (qv, kvv, kp, cm, sm), (gq, gk) = rows, params
        value_lanes = _lane(kp.shape) >= NOPE
        qs, ks, vs = [], [], []
        for qc, kc in zip(_chunks(qv), _chunks(kvv), strict=True):
            qs.append(_rope(_grms(qc, gq, Q_GROUPS)[0], cm, sm, H_M))
            ks.append(_grms(kc, gk, K_GROUPS)[0] + kp)
            vs.append(jnp.where(value_lanes, kc, 0.0))
        return [jnp.concatenate(t, axis=1) for t in (qs, ks, vs)], []

    q_mla, k_mla, v_mla = rowwise("mla_prep", mla_prep_fn, [q, kv, kper, cos_m, sin_m], [g_q, g_k],
                                  [(HEADS * LANES, MXU_DTYPE)] * 3, tm=256)
    mla_scale = (NOPE + ROPE) ** -0.5
    o_cat, lse_mla = mla_fwd("mla_fwd", q_mla, k_mla, v_mla, mla_scale)

    band = [band_fwd(f"band{dil}_fwd", qd_r, kd_r, proj, dil) for dil in DILATIONS]
    o_cat, lse_mix = combine_fwd("dil_combine", [b[0] for b in band], [b[1] for b in band], o_cat)
    w_o = fetch("w_o", o_cat)
    mix = matmul("mix_fwd", o_cat, w_o, "nn", tm=512)

    def mid_fn(rows, params):
        (xv, mx), (gate1, g, sc, sh) = rows, params
        x1 = xv + gate1 * mx
        y, _, _ = _rms(x1, g)
        return [x1, y * (1.0 + sc) + sh], []

    x1, h2 = rowwise("mid_fwd", mid_fn, [x, mix], [g1, w["g_ffn_norm"], sc2, sh2], [(D_MODEL, F32), (D_MODEL, MXU_DTYPE)])
    w_up, w_conv, w_down = fetch("w_up", h2), fetch("w_conv", h2), fetch("w_down", h2)
    up = matmul("up_fwd", h2, w_up, "nt", tn=CONV_TC, b_nmap=paired_block)
    act = conv_glu_fwd("conv_fwd", up, w_conv, b_conv)
    dn = matmul("down_fwd", act, w_down, "nn", tm=512)

    def final_fn(rows, params):
        (x1v, dnv, tgt), (gate2,) = rows, params
        r = x1v + gate2 * dnv - tgt
        dy = r * (1.0 / D_MODEL)
        loss = jnp.sum(_colsum(r * r), axis=-1, keepdims=True) * (0.5 / D_MODEL)
        return [dy, gate2 * dy], [loss, _colsum(dy * dnv)]

    dy, d_dn, loss, dg2 = rowwise("loss_head", final_fn, [x1, dn, target], [g2], [(D_MODEL, F32), (D_MODEL, MXU_DTYPE)],
                                  [1, D_MODEL])
    emit("w_down", matmul("down_wgrad", act, d_dn, "tn", tm=1408, out_dtype=MXU_DTYPE))
    dact = matmul("down_dgrad", d_dn, w_down, "nt", tm=512)
    dup, g_w_conv, g_b_conv = conv_glu_bwd("conv_bwd", up, w_conv, b_conv, dact)
    emit("w_conv", g_w_conv)
    sent = emit("w_up", matmul("up_wgrad", dup, h2, "tn", tm=CONV_TC, out_dtype=MXU_DTYPE, a_mmap=natural_block))
    dh2 = pair_dgrad("up_dgrad", dup, w_up, dep=sent)

    def mid_bwd_fn(rows, params):
        (dh2v, dyv, x1v, mx), (gate1, g, sc) = rows, params
        yn, n, rstd = _rms(x1v, g)
        dx_n, dg = _rms_bwd(dh2v * (1.0 + sc), n, rstd, g)
        dx1 = dyv + dx_n
        return [dx1, gate1 * dx1], [dg, _colsum(dh2v * yn), _colsum(dh2v), _colsum(dx1 * mx)]

    dx1, dmix, dg_ffn, dsc2, dsh2, dg1 = rowwise(
        "mid_bwd", mid_bwd_fn, [dh2, dy, x1, mix], [g1, w["g_ffn_norm"], sc2], [(D_MODEL, F32), (D_MODEL, MXU_DTYPE)],
        [D_MODEL] * 4)

    sent = emit("w_o", matmul("mix_wgrad", o_cat, dmix, "tn", tm=512, out_dtype=MXU_DTYPE))
    do_cat = matmul("mix_dgrad", dmix, w_o, "nt", tm=512, dep=sent)
    dband = [band_bwd(f"band{dil}_bwd", qd_r, kd_r, proj, b[1], lse_mix, o_cat, do_cat, dil) for dil, b in zip(DILATIONS, band)]
    dq_mla, dkv_mla, dkper = mla_bwd("mla_bwd", q_mla, k_mla, v_mla, o_cat, do_cat, lse_mla, mla_scale)

    def mla_prep_bwd_fn(rows, params):
        (dqv, dkvv, qv, kvv, cm, sm), (gq, gk) = rows, params
        nope_lanes = _lane(cm.shape) < NOPE
        dqs, dkvs, dgq, dgk = [], [], 0.0, 0.0
        for dqc, dkc, qc, kc in zip(_chunks(dqv), _chunks(dkvv), _chunks(qv), _chunks(kvv), strict=True):
            _, n, rstd = _grms(qc, gq, Q_GROUPS)
            dx, dg = _grms_bwd(_rope_bwd(dqc, cm, sm, H_M), n, rstd, gq, Q_GROUPS)
            dqs.append(dx)
            dgq = dgq + dg
            _, n, rstd = _grms(kc, gk, K_GROUPS)
            dx, dg = _grms_bwd(dkc, n, rstd, gk, K_GROUPS)
            dkvs.append(jnp.where(nope_lanes, dx, dkc))
            dgk = dgk + dg
        return [jnp.concatenate(dqs, axis=1), jnp.concatenate(dkvs, axis=1)], [dgq, dgk]

    dq, dkv, dg_q, dg_k = rowwise("mla_prep_bwd", mla_prep_bwd_fn, [dq_mla, dkv_mla, q, kv, cos_m, sin_m], [g_q, g_k],
                                  [(HEADS * LANES, MXU_DTYPE)] * 2, [LANES, LANES], tm=256)
    emit("w_q_b", matmul("q_wgrad", dq, qln, "tn", out_dtype=MXU_DTYPE))
    emit("w_kv_b", matmul("kv_wgrad", kvn, dkv, "tn", out_dtype=MXU_DTYPE))
    dqln = matmul("q_dgrad", dq, w_q_b, "nn", tm=1024)
    dkvn = matmul("kv_dgrad", dkv, w_kv_b, "nt", tm=1024)

    def pre_bwd_fn(rows, params):
        dql, dkvl, dkp = rows[0:3]
        dqd_, dkd_, dvd_ = [rows[3 + 3 * i] + rows[4 + 3 * i] + rows[5 + 3 * i] for i in range(3)]
        pv, cm, sm, cd, sd = rows[12:]
        gq, gkv, gkp, gdq, gdk = params
        r_q = _norm_bwd(dql, pv[:, P_QLAT:P_KVLAT], gq)
        r_kv = _norm_bwd(dkvl, pv[:, P_KVLAT:P_KPE], gkv)
        _, n, rstd = _grms(pv[:, P_KPE:P_QD], gkp, KPE_GROUPS)
        r_kp = _grms_bwd(_rope_bwd(dkp, cm, sm, H_M), n, rstd, gkp, KPE_GROUPS)
        outs, dgs = [r_q[0], r_kv[0], r_kp[0]], []
        for dval, lo, g in ((dqd_, P_QD, gdq), (dkd_, P_KD, gdk)):
            dg_sum = 0.0
            for dc, xc in zip(_chunks(dval), _chunks(pv[:, lo:lo + DIL_WIDTH]), strict=True):
                _, n, rstd = _grms(xc, g, DIL_GROUPS)
                dx, dg = _grms_bwd(_rope_bwd(dc, cd, sd, H_D), n, rstd, g, DIL_GROUPS)
                outs.append(dx)
                dg_sum = dg_sum + dg
            dgs.append(dg_sum)
        return [jnp.concatenate(outs + [dvd_], axis=1)], [r_q[1], r_kv[1], r_kp[1]] + dgs

    dproj, dg_q_lat, dg_kv_lat, dg_kpe, dg_dq, dg_dk = rowwise(
        "proj_pre_bwd", pre_bwd_fn,
        [dqln, dkvn, dkper] + [d[i] for i in range(3) for d in dband] + [proj] + tables, post_params,
        [(P_END, MXU_DTYPE)], [Q_LORA, KV_LORA, LANES, LANES, LANES], tm=256)
    sent = emit("w_in", matmul("proj_wgrad", dproj, h, "tn", tn=512, out_dtype=MXU_DTYPE))
    dh = matmul("proj_dgrad", dproj, w_in, "nn", tm=512, dep=sent)

    def ln1_bwd_fn(rows, params):
        (dhv, dres, xv), (g, sc) = rows, params
        yn, n, rstd = _rms(xv, g)
        dx_n, dg = _rms_bwd(dhv * (1.0 + sc), n, rstd, g)
        return [dres + dx_n], [dg, _colsum(dhv * yn), _colsum(dhv)]

    grad_x, dg_mix, dsc1, dsh1 = rowwise("ln1_bwd", ln1_bwd_fn, [dh, dx1, x], [w["g_mix_norm"], sc1], [(D_MODEL, F32)],
                                         [D_MODEL] * 3)
    dmod = jnp.concatenate([dsh1, dsc1, dg1, dsh2, dsc2, dg2], axis=-1)
    small = {"loss": loss, "b_ada": dmod, "g_mix_norm": dg_mix, "g_q_lat": dg_q_lat, "g_kv_lat": dg_kv_lat,
             "g_mla_q_nope": dg_q[:, :NOPE], "g_mla_q_pe": dg_q[:, NOPE:NOPE + ROPE], "g_mla_k_nope": dg_k[:, :NOPE],
             "g_mla_k_pe": dg_kpe[:, KPE_LO:KPE_LO + ROPE], "g_dil_q": dg_dq[:, :DIL_DIM] + dg_dq[:, DIL_DIM:],
             "g_dil_k": dg_dk[:, :DIL_DIM] + dg_dk[:, DIL_DIM:], "g_ffn_norm": dg_ffn,
             "b_conv": g_b_conv}
    return grad_x, small


COL_SHARDED = ("w_kv_b", "w_conv")
ROW_SHARDED = ("w_o", "w_down") + TRANSPOSED
ADAM_TILE = {"w_ada": 256, "w_up": 176, "w_down": 176}
GATHER_GROUPS = (("w_in",), ("w_q_b", "w_kv_b"), ("w_o", "w_up", "w_conv", "w_down"))
SCATTER_GROUPS = (("w_down", "w_conv", "w_up"), ("w_o",), ("w_q_b", "w_kv_b", "w_in"))
OUT_WEIGHTS = ("w_ada", "b_ada", "g_mix_norm", "w_in", "g_q_lat", "w_q_b", "g_kv_lat", "w_kv_b", "g_mla_q_nope", "g_mla_q_pe",
               "g_mla_k_nope", "g_mla_k_pe", "g_dil_q", "g_dil_k", "w_o", "g_ffn_norm", "w_up", "w_conv", "b_conv", "w_down")


def kernel(x, c, positions, w_ada, b_ada, g_mix_norm, w_in, g_q_lat, w_q_b, g_kv_lat, w_kv_b, g_mla_q_nope, g_mla_q_pe, g_mla_k_nope, g_mla_k_pe, g_dil_q, g_dil_k, w_o, g_ffn_norm, w_up, w_conv, b_conv, w_down, loss_target, m_w_ada, m_b_ada, m_g_mix_norm, m_w_in, m_g_q_lat, m_w_q_b, m_g_kv_lat, m_w_kv_b, m_g_mla_q_nope, m_g_mla_q_pe, m_g_mla_k_nope, m_g_mla_k_pe, m_g_dil_q, m_g_dil_k, m_w_o, m_g_ffn_norm, m_w_up, m_w_conv, m_b_conv, m_w_down, v_w_ada, v_b_ada, v_g_mix_norm, v_w_in, v_g_q_lat, v_w_q_b, v_g_kv_lat, v_w_kv_b, v_g_mla_q_nope, v_g_mla_q_pe, v_g_mla_k_nope, v_g_mla_k_pe, v_g_dil_q, v_g_dil_k, v_w_o, v_g_ffn_norm, v_w_up, v_w_conv, v_b_conv, v_w_down):
    args = dict(locals())
    xi, yi, ci = _place()
    me = 4 * xi + 2 * yi + ci
    def local(prefix, n):
        a = args[prefix + n][0]
        return a.T if n in TRANSPOSED else a

    shard = {n: local("", n) for n in COL_SHARDED + ROW_SHARDED + ("w_ada",)}
    small_w = {n: args[n] for n, _ in SMALL if n != "loss"}

    (c_all,) = all_gather("gather_c", [c])
    (sc_all,) = rowwise("silu_c", lambda rows, params: ([_silu(rows[0])], []), [c_all.reshape(N_DEV, D_MODEL)], [],
                        [(D_MODEL, MXU_DTYPE)])
    mod_part = matmul("ada_fwd", sc_all, shard["w_ada"], "nn")
    (mod_all,) = all_gather("gather_mod", [mod_part])

    payload = {n: shard[n] if n == "w_conv" else shard[n].astype(MXU_DTYPE) for n in COL_SHARDED + ROW_SHARDED}
    gathers, after_start = [], mod_all
    for i, grp in enumerate(GATHER_GROUPS):
        gathers.append(exchange_start(f"gather{i}_start", [payload[n] for n in grp], gather=True, after=after_start))
        after_start = gathers[-1][-1]
    full = {}

    def fetch(name, after):
        if name not in full:
            (i, grp), = [(i, grp) for i, grp in enumerate(GATHER_GROUPS) if name in grp]
            srcs, lands = exchange_wait(f"gather{i}_wait", gathers[i], True, after)
            for n, src, land in zip(grp, srcs, lands, strict=True):
                stack = lax.dynamic_update_index_in_dim(land, src, me, 0)
                full[n] = to_kernel_layout(n, _gather_cols(stack) if n in COL_SHARDED else _gather_rows(stack))
        return full[name]

    mod_row = lax.dynamic_index_in_dim(mod_all, me, axis=1, keepdims=False).reshape(1, 6 * D_MODEL)
    (mod,) = rowwise("ada_bias", lambda rows, params: ([rows[0] + rows[1]], []), [mod_row, b_ada], [], [(6 * D_MODEL, F32)],
                     dep=after_start)

    own, pending, scatters = {}, {}, {}

    def emit(name, grad):
        grad = from_kernel_layout(name, grad)
        parts = _scatter_cols(grad) if name in COL_SHARDED else _scatter_rows(grad)
        own[name] = lax.dynamic_index_in_dim(parts, me, 0, keepdims=False)
        pending[name] = parts
        for i, grp in enumerate(SCATTER_GROUPS):
            if name == grp[-1]:
                scatters[i] = exchange_start(f"scatter{i}_start", [pending[n] for n in grp], gather=False)
                return scatters[i][-1]
        return None

    pos = positions.reshape(SEQ, 1).astype(F32)
    grad_x, small = _local_step(x[0], pos, mod, loss_target[0], small_w, fetch, emit)

    res, done = {}, grad_x
    for i, grp in enumerate(SCATTER_GROUPS):
        _, lands = exchange_wait(f"scatter{i}_wait", scatters[i], False, done)
        for n, land in zip(grp, lands, strict=True):
            res[n] = adamw(f"adamw_{n}", shard[n], [own[n], land], local("m_", n), local("v_", n), ADAM_TILE.get(n))
            done = res[n][0]
            if n in TRANSPOSED:
                res[n] = [r.T for r in res[n]]
    (small_all,) = all_gather("gather_small", [_pack_small(small)], after=done)
    packed = adamw("adamw_small", _pack_small(small_w), [small_all], _pack_small({n: args["m_" + n] for n in small_w}),
                   _pack_small({n: args["v_" + n] for n in small_w}))
    small_res = [_unpack_small(p) for p in packed]
    dmod_all = small_all.reshape(N_DEV, -1)[:, 1:1 + 6 * D_MODEL]
    dmod_mine = lax.dynamic_slice_in_dim(dmod_all, me * (6 * D_MODEL // N_DEV), 6 * D_MODEL // N_DEV, axis=1)
    g_w_ada = matmul("ada_wgrad", sc_all, dmod_mine, "tn")
    res["w_ada"] = adamw("adamw_w_ada", shard["w_ada"], [g_w_ada], m_w_ada[0], v_w_ada[0], ADAM_TILE["w_ada"])

    def leaf(kind, n):
        if n in res:
            return res[n][kind][None]
        return small_res[kind][n]

    loss = small_res[0]["loss"].reshape(())
    return (loss, grad_x[None], *[leaf(k, n) for k in range(4) for n in OUT_WEIGHTS])
```

```python
import jax
import jax.numpy as jnp
from jax import lax
from jax.experimental import pallas as pl
from jax.experimental.pallas import tpu as pltpu

F32 = jnp.float32
MXU_DTYPE = jnp.bfloat16

N_DEV = 8
D_MODEL = 1024
SEQ = 2048
HEADS = 8
NOPE = 64
ROPE = 32
Q_LORA = 512
KV_LORA = 256
DIL_DIM = 64
DIL_WIDTH = HEADS * DIL_DIM
DILATIONS = (1, 4, 16)
SPAN = 128
D_FF = 2816
LANES = 128
ROPE_THETA = 10000.0
EPS = 1e-6
NEG_INF = -1e30
ADAM_LR, ADAM_B1, ADAM_B2, ADAM_EPS, ADAM_WD, ADAM_STEP = 0.001, 0.9, 0.999, 1e-08, 0.01, 10
VMEM_LIMIT = 56 * 1024 * 1024
MESH_ID = pl.DeviceIdType.MESH

P_QLAT, P_KVLAT, P_KPE, P_QD, P_KD, P_VD, P_END = 0, 512, 768, 896, 1408, 1920, 2432
KPE_LO = 64
MIX_IN = HEADS * LANES + DIL_WIDTH


def _params(**kw):
    return pltpu.CompilerParams(vmem_limit_bytes=VMEM_LIMIT, **kw)


def rowwise(name, fn, rows, params, out_rows, out_accs=(), tm=512, dep=None):
    deps = [] if dep is None else [dep]
    rows = [r if isinstance(r, tuple) else (r, r.shape[1], 0) for r in rows]
    R = rows[0][0].shape[0]
    tm = min(tm, R)
    steps = R // tm
    assert steps * tm == R
    in_specs = []
    for a, width, cb in rows:
        ri = a.shape[0]
        per = ri // tm
        assert per * tm == ri
        if ri == R:
            in_specs.append(pl.BlockSpec((tm, width), lambda i, cb=cb: (i, cb)))
        else:
            in_specs.append(pl.BlockSpec((tm, width), lambda i, per=per, cb=cb: (i % per, cb)))
    for p in params:
        in_specs.append(pl.BlockSpec(p.shape, lambda i: (0,) * p.ndim))
    in_specs += [pl.BlockSpec(memory_space=pl.ANY)] * len(deps)
    out_shape = [jax.ShapeDtypeStruct((R, d), dt) for d, dt in out_rows]
    out_specs = [pl.BlockSpec((tm, d), lambda i: (i, 0)) for d, _ in out_rows]
    out_shape += [jax.ShapeDtypeStruct((1, n), F32) for n in out_accs]
    out_specs += [pl.BlockSpec((1, n), lambda i: (0, 0)) for n in out_accs]
    nr, npar, no, na = len(rows), len(params), len(out_rows), len(out_accs)

    def body(*refs):
        rvals = [r[...] for r in refs[:nr]]
        pvals = [r[...] for r in refs[nr:nr + npar]]
        outs, accs = fn(rvals, pvals)
        first_out = nr + npar + len(deps)
        for ref, v in zip(refs[first_out:first_out + no], outs, strict=True):
            ref[...] = v.astype(ref.dtype)
        if na:
            acc_refs = refs[first_out + no:]
            i = pl.program_id(0)

            @pl.when(i == 0)
            def _():
                for ref, v in zip(acc_refs, accs, strict=True):
                    ref[...] = v

            @pl.when(i > 0)
            def _():
                for ref, v in zip(acc_refs, accs, strict=True):
                    ref[...] += v

    res = pl.pallas_call(body, name=name, grid=(steps,), in_specs=in_specs, out_specs=out_specs,
                         out_shape=out_shape, compiler_params=_params())(*[r[0] for r in rows], *params, *deps)
    return list(res)


_DIMS = {"nn": ((1,), (0,)), "nt": ((1,), (1,)), "tn": ((0,), (0,))}


def _dot(a, b, mode="nn"):
    return lax.dot_general(a.astype(MXU_DTYPE), b.astype(MXU_DTYPE), (_DIMS[mode], ((), ())),
                           preferred_element_type=F32)


def matmul(name, a, b, mode, tm=None, tn=None, tk=None, out_dtype=F32, dep=None, a_mmap=None, b_nmap=None, b_kmap=None):
    if mode == "tn":
        K, M = a.shape
    else:
        M, K = a.shape
    N = b.shape[0] if mode == "nt" else b.shape[1]
    tm, tn, tk = tm or M, tn or N, tk or K
    nm, nn, nk = M // tm, N // tn, K // tk
    assert nm * tm == M and nn * tn == N and nk * tk == K
    same = lambda idx: idx
    a_mmap, b_nmap, b_kmap = a_mmap or same, b_nmap or same, b_kmap or same
    if mode == "tn":
        a_spec = pl.BlockSpec((tk, tm), lambda i, j, k: (k, a_mmap(i)))
    else:
        a_spec = pl.BlockSpec((tm, tk), lambda i, j, k: (a_mmap(i), k))
    if mode == "nt":
        b_spec = pl.BlockSpec((tn, tk), lambda i, j, k: (b_nmap(j), b_kmap(k)))
    else:
        b_spec = pl.BlockSpec((tk, tn), lambda i, j, k: (b_kmap(k), b_nmap(j)))
    deps = [] if dep is None else [dep]

    def body(a_ref, b_ref, *rest):
        o_ref, scratch = rest[len(deps)], rest[len(deps) + 1:]
        p = _dot(a_ref[...], b_ref[...], mode)
        if nk == 1:
            o_ref[...] = p.astype(o_ref.dtype)
        else:
            acc = scratch[0]
            k = pl.program_id(2)

            @pl.when(k == 0)
            def _():
                acc[...] = p

            @pl.when(k > 0)
            def _():
                acc[...] += p

            @pl.when(k == nk - 1)
            def _():
                o_ref[...] = acc[...].astype(o_ref.dtype)

    return pl.pallas_call(
        body, name=name, grid=(nm, nn, nk), in_specs=[a_spec, b_spec] + [pl.BlockSpec(memory_space=pl.ANY)] * len(deps),
        out_specs=pl.BlockSpec((tm, tn), lambda i, j, k: (i, j)),
        out_shape=jax.ShapeDtypeStruct((M, N), out_dtype),
        scratch_shapes=[pltpu.VMEM((tm, tn), F32)] if nk > 1 else [],
        compiler_params=_params())(a, b, *deps)


def _rms(x, g):
    rstd = lax.rsqrt(jnp.mean(x * x, axis=-1, keepdims=True) + EPS)
    n = x * rstd
    return n * g, n, rstd


def _rms_bwd(dy, n, rstd, g):
    dg = jnp.sum(dy * n, axis=0, keepdims=True)
    dn = dy * g
    dx = rstd * (dn - n * jnp.mean(dn * n, axis=-1, keepdims=True))
    return dx, dg


def _norm_bwd(dy, x, g):
    _, n, rstd = _rms(x, g)
    return _rms_bwd(dy, n, rstd, g)


def _colsum(v):
    return jnp.sum(v, axis=0, keepdims=True)


def _silu(x):
    return x * (1.0 / (1.0 + jnp.exp(-x)))


def _lane(shape):
    return lax.broadcasted_iota(jnp.int32, shape, 1)


def _group_mean(v, groups):
    lane = _lane(v.shape)
    out = jnp.zeros_like(v)
    for lo, hi in groups:
        m = (lane >= lo) & (lane < hi)
        out = jnp.where(m, jnp.sum(jnp.where(m, v, 0.0), axis=-1, keepdims=True) * (1.0 / (hi - lo)), out)
    return out


def _in_groups(shape, groups):
    lane = _lane(shape)
    m = jnp.zeros(shape, jnp.bool_)
    for lo, hi in groups:
        m = m | ((lane >= lo) & (lane < hi))
    return m


def _grms(x, g, groups):
    rstd = lax.rsqrt(_group_mean(x * x, groups) + EPS)
    n = jnp.where(_in_groups(x.shape, groups), x * rstd, 0.0)
    return n * g, n, rstd


def _grms_bwd(dy, n, rstd, g, groups):
    dn = dy * g
    return rstd * (dn - n * _group_mean(dn * n, groups)), _colsum(dy * n)


def _rot(x, half, transpose=False):
    first = (_lane(x.shape) % (2 * half)) < half
    up = pltpu.roll(x, LANES - half, axis=1)
    down = pltpu.roll(x, half, axis=1)
    return jnp.where(first, up, -down) if transpose else jnp.where(first, -up, down)


def _rope(x, cos, sin, half):
    return x * cos + _rot(x, half) * sin


def _rope_bwd(dy, cos, sin, half):
    return dy * cos + _rot(dy * sin, half, transpose=True)


def _chunks(x):
    return [x[:, i:i + LANES] for i in range(0, x.shape[1], LANES)]


Q_GROUPS = ((0, NOPE), (NOPE, NOPE + ROPE))
K_GROUPS = ((0, NOPE),)
KPE_GROUPS = ((KPE_LO, KPE_LO + ROPE),)
DIL_GROUPS = ((0, DIL_DIM), (DIL_DIM, 2 * DIL_DIM))


def _col(width, rows=SEQ):
    return pl.BlockSpec((rows, width), lambda h: (0, h))


def _causal_tail(s, tq, fill):
    diag = s[:, s.shape[1] - tq:]
    keep = lax.broadcasted_iota(jnp.int32, diag.shape, 1) <= lax.broadcasted_iota(jnp.int32, diag.shape, 0)
    diag = jnp.where(keep, diag, fill)
    return diag if s.shape[1] == tq else jnp.concatenate([s[:, :s.shape[1] - tq], diag], axis=1)


def mla_fwd(name, q, k, v, scale, tq=512):
    S = q.shape[0]

    def body(q_ref, k_ref, v_ref, o_ref, lse_ref):
        for i in range(S // tq):
            kext = (i + 1) * tq
            blk = slice(i * tq, kext)
            s = _causal_tail(_dot(q_ref[blk, :], k_ref[:kext, :], "nt") * scale, tq, NEG_INF)
            m = jnp.max(s, axis=-1, keepdims=True)
            e = jnp.exp(s - m)
            l = jnp.sum(e, axis=-1, keepdims=True)
            o_ref[blk, :] = _dot(e / l, v_ref[:kext, :])
            lse_ref[0, blk, :] = m + jnp.log(l)

    return pl.pallas_call(
        body, name=name, grid=(HEADS,), in_specs=[_col(LANES)] * 3,
        out_specs=[_col(LANES), pl.BlockSpec((1, S, 1), lambda h: (h, 0, 0))],
        out_shape=[jax.ShapeDtypeStruct((S, MIX_IN), F32), jax.ShapeDtypeStruct((HEADS, S, 1), F32)],
        compiler_params=_params())(q, k, v)


def mla_bwd(name, q, k, v, o, do, lse, scale, tq=512):
    S = q.shape[0]

    def body(q_ref, k_ref, v_ref, o_ref, do_ref, lse_ref, dq_ref, dkv_ref, dkpe_ref, dk_acc, dv_acc):
        dk_acc[...] = jnp.zeros_like(dk_acc)
        dv_acc[...] = jnp.zeros_like(dv_acc)
        for i in range(S // tq):
            kext = (i + 1) * tq
            blk = slice(i * tq, kext)
            qi, kk, vv = q_ref[blk, :], k_ref[:kext, :], v_ref[:kext, :]
            doi = do_ref[blk, :]
            s = _causal_tail(_dot(qi, kk, "nt") * scale, tq, NEG_INF)
            p = jnp.exp(s - lse_ref[0, blk, :])
            dp = _dot(doi, vv, "nt")
            delta = jnp.sum(doi * o_ref[blk, :], axis=-1, keepdims=True)
            ds = p * (dp - delta) * scale
            dq_ref[blk, :] = _dot(ds, kk)
            dk_acc[:kext, :] += _dot(ds, qi, "tn")
            dv_acc[:kext, :] += _dot(p, doi, "tn")
        dk = dk_acc[...]
        lane = _lane(dk.shape)
        dkv_ref[...] = jnp.where(lane < NOPE, dk, 0.0) + dv_acc[...]
        dkpe = jnp.where((lane >= KPE_LO) & (lane < KPE_LO + ROPE), dk, 0.0)
        h = pl.program_id(0)

        @pl.when(h == 0)
        def _():
            dkpe_ref[...] = dkpe

        @pl.when(h > 0)
        def _():
            dkpe_ref[...] += dkpe

    return pl.pallas_call(
        body, name=name, grid=(HEADS,),
        in_specs=[_col(LANES)] * 5 + [pl.BlockSpec((1, S, 1), lambda h: (h, 0, 0))],
        out_specs=[_col(LANES), _col(LANES), pl.BlockSpec((S, LANES), lambda h: (0, 0))],
        out_shape=[jax.ShapeDtypeStruct((S, HEADS * LANES), F32), jax.ShapeDtypeStruct((S, HEADS * LANES), F32),
                   jax.ShapeDtypeStruct((S, LANES), F32)],
        scratch_shapes=[pltpu.VMEM((S, LANES), F32), pltpu.VMEM((S, LANES), F32)],
        compiler_params=_params())(q, k, v, o, do, lse)


BAND_TQ = SPAN


def _band_blocks(L, tq):
    return [(i * tq, (i + 1) * tq, max(0, i * tq - SPAN)) for i in range(L // tq)]


def _band_mask(q0, q1, k0):
    shape = (q1 - q0, q1 - k0)
    dist = (lax.broadcasted_iota(jnp.int32, shape, 0) + q0) - (lax.broadcasted_iota(jnp.int32, shape, 1) + k0)
    return (dist >= 0) & (dist <= SPAN)


def _class_rows(r, dil, lo, hi):
    return pl.ds(r + dil * lo, hi - lo, stride=dil) if dil > 1 else pl.ds(lo, hi - lo)


def _stack_heads(t, lo):
    zero = jnp.zeros_like(t)
    return jnp.concatenate([jnp.where(lo, t, zero), jnp.where(lo, zero, t)], axis=0)


def _band_mask2(q0, q1, k0):
    n = q1 - q0
    shape = (2 * n, q1 - k0)
    i = lax.broadcasted_iota(jnp.int32, shape, 0)
    dist = (jnp.where(i >= n, i - n, i) + q0) - (lax.broadcasted_iota(jnp.int32, shape, 1) + k0)
    return (dist >= 0) & (dist <= SPAN)


def _pair_col(col0=0):
    return pl.BlockSpec((SEQ, LANES), lambda j: (0, col0 // LANES + j))


def band_fwd(name, q, k, v, dil):
    S = q.shape[0]
    L = S // dil
    tq = BAND_TQ
    scale = DIL_DIM ** -0.5

    def body(q_ref, k_ref, v_ref, o_ref, lse_ref):
        for r in range(dil):
            for q0, q1, k0 in _band_blocks(L, tq):
                qrows, krows = _class_rows(r, dil, q0, q1), _class_rows(r, dil, k0, q1)
                qb, kb, vb = q_ref[qrows, :].astype(MXU_DTYPE), k_ref[krows, :], v_ref[krows, :]
                n = q1 - q0
                lo = _lane(qb.shape) < DIL_DIM
                s = _dot(_stack_heads(qb, lo), kb, "nt") * scale
                s = jnp.where(_band_mask2(q0, q1, k0), s, NEG_INF)
                mx = jnp.max(s, axis=-1, keepdims=True)
                e = jnp.exp(s - mx)
                l = jnp.sum(e, axis=-1, keepdims=True)
                pv, lse = _dot(e / l, vb), mx + jnp.log(l)
                o_ref[qrows, :] = jnp.where(lo, pv[:n], pv[n:])
                lse_ref[qrows, :] = jnp.where(lo, lse[:n], lse[n:])

    return pl.pallas_call(
        body, name=name, grid=(DIL_WIDTH // LANES,), in_specs=[_pair_col()] * 2 + [_pair_col(P_VD)], out_specs=[_pair_col()] * 2,
        out_shape=[jax.ShapeDtypeStruct((S, DIL_WIDTH), F32)] * 2, compiler_params=_params())(q, k, v)


def band_bwd(name, q, k, v, lse, lse_mix, o_cat, do_cat, dil):
    S = q.shape[0]
    L = S // dil
    tq = BAND_TQ
    scale = DIL_DIM ** -0.5

    def body(q_ref, k_ref, v_ref, lse_ref, mix_ref, o_ref, do_ref, dq_ref, dk_ref, dv_ref):
        dk_ref[...] = jnp.zeros_like(dk_ref)
        dv_ref[...] = jnp.zeros_like(dv_ref)
        for r in range(dil):
            for q0, q1, k0 in _band_blocks(L, tq):
                qrows, krows = _class_rows(r, dil, q0, q1), _class_rows(r, dil, k0, q1)
                qb, kb, vb = q_ref[qrows, :].astype(MXU_DTYPE), k_ref[krows, :], v_ref[krows, :]
                lse_p, dout = lse_ref[qrows, :], do_ref[qrows, :]
                n = q1 - q0
                lo = _lane(qb.shape) < DIL_DIM
                per_head = lambda t: jnp.concatenate([t[:, 0:1], t[:, DIL_DIM:DIL_DIM + 1]], axis=0)
                w2 = per_head(jnp.exp(lse_p - mix_ref[qrows, :]))
                dd = dout * o_ref[qrows, :]
                big_d = jnp.concatenate([jnp.sum(jnp.where(lo, dd, 0.0), axis=-1, keepdims=True),
                                         jnp.sum(jnp.where(lo, 0.0, dd), axis=-1, keepdims=True)], axis=0)
                q2 = _stack_heads(qb, lo)
                s = _dot(q2, kb, "nt") * scale
                p = jnp.where(_band_mask2(q0, q1, k0), jnp.exp(s - per_head(lse_p)), 0.0)
                dom = _stack_heads(dout, lo) * w2
                ds = p * (_dot(dom, vb, "nt") - w2 * big_d) * scale
                dq2 = _dot(ds, kb)
                dq_ref[qrows, :] = jnp.where(lo, dq2[:n], dq2[n:])
                dk_ref[krows, :] += _dot(ds, q2, "tn")
                dv_ref[krows, :] += _dot(p, dom, "tn")

    cat = _pair_col(HEADS * LANES)
    return pl.pallas_call(
        body, name=name, grid=(DIL_WIDTH // LANES,),
        in_specs=[_pair_col()] * 2 + [_pair_col(P_VD)] + [_pair_col()] * 2 + [cat] * 2, out_specs=[_pair_col()] * 3,
        out_shape=[jax.ShapeDtypeStruct((S, DIL_WIDTH), F32)] * 3,
        compiler_params=_params())(q, k, v, lse, lse_mix, o_cat, do_cat)


def combine_fwd(name, outs, lses, o_cat, tm=512):
    S = outs[0].shape[0]

    def body(o1, o2, o3, l1, l2, l3, cat_in, cat_out, mix_ref):
        ls = [l1[...], l2[...], l3[...]]
        m = jnp.maximum(jnp.maximum(ls[0], ls[1]), ls[2])
        e = [jnp.exp(l - m) for l in ls]
        den = e[0] + e[1] + e[2]
        cat_out[...] = (e[0] / den) * o1[...] + (e[1] / den) * o2[...] + (e[2] / den) * o3[...]
        mix_ref[...] = m + jnp.log(den)

    row = pl.BlockSpec((tm, DIL_WIDTH), lambda i: (i, 0))
    return pl.pallas_call(
        body, name=name, grid=(S // tm,), in_specs=[row] * 6 + [pl.BlockSpec(memory_space=pl.ANY)],
        out_specs=[pl.BlockSpec((tm, DIL_WIDTH), lambda i: (i, HEADS * LANES // DIL_WIDTH)), row],
        out_shape=[jax.ShapeDtypeStruct(o_cat.shape, F32), jax.ShapeDtypeStruct((S, DIL_WIDTH), F32)],
        input_output_aliases={6: 0}, compiler_params=_params())(*outs, *lses, o_cat)


def _shift_down(u, n):
    t = lax.broadcasted_iota(jnp.int32, u.shape, 0)
    return jnp.where(t >= n, pltpu.roll(u, n, axis=0), 0.0)


def _shift_up(u, n):
    rows = u.shape[0]
    t = lax.broadcasted_iota(jnp.int32, u.shape, 0)
    return jnp.where(t < rows - n, pltpu.roll(u, rows - n, axis=0), 0.0)


def _conv(u, w, b):
    return w[2:3, :] * u + w[1:2, :] * _shift_down(u, 1) + w[0:1, :] * _shift_down(u, 2) + b


CONV_TC = 256
CONV_NB = D_FF // CONV_TC


def paired_block(j):
    return (j % 2) * CONV_NB + j // 2


def natural_block(j):
    return jnp.where(j < CONV_NB, 2 * j, 2 * (j - CONV_NB) + 1)


def _pair_spec(rows):
    return pl.BlockSpec((rows, 2 * CONV_TC), lambda j: (0, j))


def _half_specs(rows, rows_axis=False):
    if rows_axis:
        return [pl.BlockSpec((rows, D_MODEL), lambda j: (j, 0)), pl.BlockSpec((rows, D_MODEL), lambda j: (j + CONV_NB, 0))]
    return [pl.BlockSpec((rows, CONV_TC), lambda j: (0, j)), pl.BlockSpec((rows, CONV_TC), lambda j: (0, j + CONV_NB))]


def pair_dgrad(name, dup, w_t, dep=None):
    S, D = dup.shape[0], w_t.shape[1]
    deps = [] if dep is None else [dep]

    def body(a_ref, bg_ref, bv_ref, *rest):
        o_ref = rest[-1]
        p = _dot(a_ref[:, :CONV_TC], bg_ref[...]) + _dot(a_ref[:, CONV_TC:], bv_ref[...])
        j = pl.program_id(0)

        @pl.when(j == 0)
        def _():
            o_ref[...] = p

        @pl.when(j > 0)
        def _():
            o_ref[...] += p

    return pl.pallas_call(
        body, name=name, grid=(CONV_NB,),
        in_specs=[_pair_spec(S), pl.BlockSpec((CONV_TC, D), lambda j: (j, 0)), pl.BlockSpec((CONV_TC, D), lambda j: (j + CONV_NB, 0))]
        + [pl.BlockSpec(memory_space=pl.ANY)] * len(deps),
        out_specs=pl.BlockSpec((S, D), lambda j: (0, 0)), out_shape=jax.ShapeDtypeStruct((S, D), F32),
        compiler_params=_params())(dup, w_t, w_t, *deps)


def _whole(a):
    return pl.BlockSpec(a.shape, lambda j: (0,) * a.ndim)


def _up_pair(h_ref, ug_ref, uv_ref):
    h = h_ref[...]
    return jnp.concatenate([_dot(h, ug_ref[...], "nt"), _dot(h, uv_ref[...], "nt")], axis=1)


def conv_glu_fwd(name, h, w_up_t, w_conv, b_conv):
    S = h.shape[0]

    def body(h_ref, ug_ref, uv_ref, wg_ref, wv_ref, bg_ref, bv_ref, act_ref):
        w = jnp.concatenate([wg_ref[...], wv_ref[...]], axis=1)
        u = _conv(_up_pair(h_ref, ug_ref, uv_ref), w, jnp.concatenate([bg_ref[...], bv_ref[...]], axis=1))
        act_ref[...] = (_silu(u[:, :CONV_TC]) * u[:, CONV_TC:]).astype(act_ref.dtype)

    return pl.pallas_call(
        body, name=name, grid=(CONV_NB,),
        in_specs=[_whole(h)] + _half_specs(CONV_TC, rows_axis=True) + _half_specs(3) + _half_specs(1),
        out_specs=pl.BlockSpec((S, CONV_TC), lambda j: (0, j)), out_shape=jax.ShapeDtypeStruct((S, D_FF), MXU_DTYPE),
        compiler_params=_params())(h, w_up_t, w_up_t, w_conv, w_conv, b_conv, b_conv)


def conv_glu_bwd(name, h, w_up_t, w_conv, b_conv, d_dn, w_down):
    S = h.shape[0]

    def body(h_ref, ug_ref, uv_ref, wg_ref, wv_ref, bg_ref, bv_ref, dd_ref, wd_ref, dup_ref, dwg_ref, dwv_ref, dbg_ref, dbv_ref):
        uin = _up_pair(h_ref, ug_ref, uv_ref)
        w = jnp.concatenate([wg_ref[...], wv_ref[...]], axis=1)
        u = _conv(uin, w, jnp.concatenate([bg_ref[...], bv_ref[...]], axis=1))
        gate, val, da = u[:, :CONV_TC], u[:, CONV_TC:], _dot(dd_ref[...], wd_ref[...], "nt")
        sig = 1.0 / (1.0 + jnp.exp(-gate))
        du = jnp.concatenate([da * val * (sig * (1.0 + gate * (1.0 - sig))), da * (gate * sig)], axis=1)
        dup_ref[...] = (w[2:3, :] * du + w[1:2, :] * _shift_up(du, 1) + w[0:1, :] * _shift_up(du, 2)).astype(dup_ref.dtype)
        dw = jnp.concatenate([_colsum(du * _shift_down(uin, 2)), _colsum(du * _shift_down(uin, 1)), _colsum(du * uin)], axis=0)
        db = _colsum(du)
        dwg_ref[...], dwv_ref[...] = dw[:, :CONV_TC], dw[:, CONV_TC:]
        dbg_ref[...], dbv_ref[...] = db[:, :CONV_TC], db[:, CONV_TC:]

    half = lambda rows: pl.BlockSpec((rows, CONV_TC), lambda j: (0, j))
    dup, dwg, dwv, dbg, dbv = pl.pallas_call(
        body, name=name, grid=(CONV_NB,),
        in_specs=[_whole(h)] + _half_specs(CONV_TC, rows_axis=True) + _half_specs(3) + _half_specs(1)
        + [_whole(d_dn), pl.BlockSpec((CONV_TC, w_down.shape[1]), lambda j: (j, 0))],
        out_specs=[_pair_spec(S), half(3), half(3), half(1), half(1)],
        out_shape=[jax.ShapeDtypeStruct((S, 2 * D_FF), MXU_DTYPE)] + [jax.ShapeDtypeStruct((3, D_FF), F32)] * 2
        + [jax.ShapeDtypeStruct((1, D_FF), F32)] * 2,
        compiler_params=_params())(h, w_up_t, w_up_t, w_conv, w_conv, b_conv, b_conv, d_dn, w_down)
    return dup, jnp.concatenate([dwg, dwv], axis=1), jnp.concatenate([dbg, dbv], axis=1)


def adamw(name, w, parts, m, v, tr=None):
    R, C = w.shape
    tr = tr or R
    assert R % tr == 0
    c1 = 1.0 - ADAM_B1 ** ADAM_STEP
    c2 = 1.0 - ADAM_B2 ** ADAM_STEP
    np_ = len(parts)

    def body(*refs):
        w_ref, m_ref, v_ref = refs[0], refs[1 + np_], refs[2 + np_]
        go_ref, d_ref, mo_ref, vo_ref = refs[3 + np_:]
        terms = []
        for part, ref in zip(parts, refs[1:1 + np_], strict=True):
            terms += [ref[...]] if part.ndim == 2 else [ref[p] for p in range(part.shape[0])]
        g = terms[0].astype(F32)
        for term in terms[1:]:
            g = g + term.astype(F32)
        m2 = ADAM_B1 * m_ref[...] + (1.0 - ADAM_B1) * g
        v2 = ADAM_B2 * v_ref[...] + (1.0 - ADAM_B2) * (g * g)
        go_ref[...] = g
        mo_ref[...] = m2
        vo_ref[...] = v2
        d_ref[...] = -ADAM_LR * ((m2 / c1) / (jnp.sqrt(v2 / c2) + ADAM_EPS) + ADAM_WD * w_ref[...])

    blk = pl.BlockSpec((tr, C), lambda i: (i, 0))
    part_specs = [blk if p.ndim == 2 else pl.BlockSpec((p.shape[0], tr, C), lambda i: (0, i, 0)) for p in parts]
    return pl.pallas_call(
        body, name=name, grid=(R // tr,),
        in_specs=[blk] + part_specs + [blk, blk], out_specs=[blk] * 4,
        out_shape=[jax.ShapeDtypeStruct((R, C), F32)] * 4, compiler_params=_params())(w, *parts, m, v)


def _place():
    return lax.axis_index("x"), lax.axis_index("y"), lax.axis_index("c")


def all_gather(name, arrs, after=None):
    n = len(arrs)
    deps = [] if after is None else [after]

    def body(*refs):
        ins, outs = refs[:n], refs[n + len(deps):2 * n + len(deps)]
        send_sems, recv_sems, local_sems = refs[2 * n + len(deps):]
        x, y, c = _place()
        me, sibling = (x, y, c), (x, y, 1 - c)
        chips = [(1 - x, y), (x, 1 - y), (1 - x, 1 - y)]
        sends = []
        for t in range(n):
            out = outs[t]

            def slot(px, py, pc, out=out):
                return out.at[4 * px + 2 * py + pc]

            def copy(k, block, to, src=None, t=t, slot=slot):
                return pltpu.make_async_remote_copy(
                    src_ref=slot(*block) if src is None else src, dst_ref=slot(*block),
                    send_sem=send_sems.at[7 * t + k], recv_sem=recv_sems.at[7 * t + k],
                    device_id=to, device_id_type=MESH_ID)

            mine = pltpu.make_async_copy(ins[t], slot(*me), local_sems.at[t])
            mine.start()
            first = [copy(0, me, sibling, src=ins[t])]
            first += [copy(1 + j, me, (*chip, c), src=ins[t]) for j, chip in enumerate(chips)]
            for cp in first:
                cp.start()
            sends.append((mine, first, copy))
        for t in range(n):
            mine, first, copy = sends[t]
            passed = [copy(4 + j, (*chip, c), sibling) for j, chip in enumerate(chips)]
            for j, chip in enumerate(chips):
                copy(1 + j, (*chip, c), me).wait_recv()
                passed[j].start()
            copy(0, sibling, me).wait_recv()
            for j, chip in enumerate(chips):
                copy(4 + j, (*chip, 1 - c), me).wait_recv()
            for cp in first + passed:
                cp.wait_send()
            mine.wait()

    any_spec = pl.BlockSpec(memory_space=pl.ANY)
    res = pl.pallas_call(
        body, name=name, in_specs=[any_spec] * (n + len(deps)), out_specs=[any_spec] * n,
        out_shape=[jax.ShapeDtypeStruct((N_DEV,) + a.shape, a.dtype) for a in arrs],
        scratch_shapes=[pltpu.SemaphoreType.DMA((7 * n,)), pltpu.SemaphoreType.DMA((7 * n,)), pltpu.SemaphoreType.DMA((n,))],
        compiler_params=pltpu.CompilerParams(has_side_effects=True))(*arrs, *deps)
    return list(res)


HBM_SPEC = pl.BlockSpec(memory_space=pltpu.HBM)
SEM_SPEC = pl.BlockSpec(memory_space=pltpu.SEMAPHORE)
DATAFLOW = pltpu.SideEffectType.DATAFLOW_SIDE_EFFECTING


def _exchange_copies(srcs, lands, send_sems, recv_sems, gather):
    x, y, c = _place()
    me = 4 * x + 2 * y + c
    out = []
    for t, (src, land) in enumerate(zip(srcs, lands, strict=True)):
        for k in range(1, N_DEV):
            px, py, pc = x ^ (k >> 2), y ^ ((k >> 1) & 1), c ^ (k & 1)
            out.append(pltpu.make_async_remote_copy(
                src_ref=src if gather else src.at[4 * px + 2 * py + pc],
                dst_ref=land.at[me] if gather else land.at[k - 1],
                send_sem=send_sems.at[7 * t + k - 1], recv_sem=recv_sems.at[7 * t + k - 1],
                device_id=(px, py, pc), device_id_type=MESH_ID))
    return out


def exchange_start(name, arrs, gather, after=None):
    n = len(arrs)
    lands = [lax.empty(((N_DEV,) + a.shape) if gather else ((N_DEV - 1,) + a.shape[1:]), a.dtype) for a in arrs]
    deps = [] if after is None else [after]

    def body(*refs):
        srcs, land_refs = refs[:n], refs[n:2 * n]
        send_sems, recv_sems = refs[2 * n + len(deps)], refs[2 * n + len(deps) + 1]
        token = refs[-1]
        for cp in _exchange_copies(srcs, land_refs, send_sems, recv_sems, gather):
            cp.start()
        token[...] = jnp.zeros_like(token)

    hbm = lambda a: pltpu.HBM(a.shape, a.dtype)
    res = pl.pallas_call(
        body, name=name,
        out_shape=(pltpu.SemaphoreType.DMA((7 * n,)), pltpu.SemaphoreType.DMA((7 * n,)), *[hbm(a) for a in arrs],
                   *[hbm(l) for l in lands], jax.ShapeDtypeStruct((8, 128), F32)),
        in_specs=[HBM_SPEC] * (2 * n) + [pl.BlockSpec(memory_space=pl.ANY)] * len(deps),
        out_specs=(SEM_SPEC, SEM_SPEC, *[HBM_SPEC] * (2 * n), pl.BlockSpec(memory_space=pltpu.VMEM)),
        input_output_aliases={i: 2 + i for i in range(2 * n)},
        compiler_params=pltpu.CompilerParams(has_side_effects=DATAFLOW),
    )(*[pltpu.with_memory_space_constraint(a, pltpu.HBM) for a in arrs + lands], *deps)
    return res[0], res[1], list(res[2:2 + n]), list(res[2 + n:2 + 2 * n]), res[-1]


def exchange_wait(name, started, gather, after):
    send_sems, recv_sems, srcs, lands, _ = started
    n = len(srcs)

    def body(*refs):
        src_refs, land_refs = refs[:n], refs[n:2 * n]
        copies = _exchange_copies(src_refs, land_refs, refs[2 * n], refs[2 * n + 1], gather)
        for cp in copies:
            cp.wait_send()
        for cp in copies:
            cp.wait_recv()

    hbm = lambda a: pltpu.HBM(a.shape, a.dtype)
    res = pl.pallas_call(
        body, name=name, out_shape=tuple(hbm(a) for a in srcs + lands),
        in_specs=[HBM_SPEC] * (2 * n) + [SEM_SPEC, SEM_SPEC, pl.BlockSpec(memory_space=pl.ANY)],
        out_specs=tuple([HBM_SPEC] * (2 * n)), input_output_aliases={i: i for i in range(2 * n)},
        compiler_params=pltpu.CompilerParams(has_side_effects=DATAFLOW),
    )(*srcs, *lands, send_sems, recv_sems, after)
    return list(res[:n]), list(res[n:])


def _gather_cols(stack):
    p, k, n = stack.shape
    return stack.transpose(1, 0, 2).reshape(k, p * n)


def _scatter_cols(full):
    k, n = full.shape
    return full.reshape(k, N_DEV, n // N_DEV).transpose(1, 0, 2)


def _gather_rows(stack):
    p, r, n = stack.shape
    return stack.reshape(p * r, n)


def _scatter_rows(full):
    r, n = full.shape
    return full.reshape(N_DEV, r // N_DEV, n)


_IN_NAT = Q_LORA + KV_LORA
TRANSPOSED = ("w_in", "w_q_b", "w_up")


def to_kernel_layout(name, w):
    if name == "w_in":
        z = lambda n: jnp.zeros((n, w.shape[1]), w.dtype)
        return jnp.concatenate([w[:_IN_NAT], z(KPE_LO), w[_IN_NAT:_IN_NAT + ROPE], z(LANES - KPE_LO - ROPE), w[_IN_NAT + ROPE:]], axis=0)
    if name == "w_q_b":
        return jnp.pad(w.reshape(HEADS, NOPE + ROPE, -1), ((0, 0), (0, LANES - NOPE - ROPE), (0, 0))).reshape(HEADS * LANES, -1)
    if name == "w_o":
        mla = jnp.pad(w[:HEADS * NOPE].reshape(HEADS, NOPE, -1), ((0, 0), (LANES - NOPE, 0), (0, 0))).reshape(HEADS * LANES, -1)
        return jnp.concatenate([mla, w[HEADS * NOPE:]], axis=0)
    return w


def from_kernel_layout(name, g):
    if name == "w_in":
        return jnp.concatenate([g[:_IN_NAT], g[P_KPE + KPE_LO:P_KPE + KPE_LO + ROPE], g[P_QD:]], axis=0)
    if name == "w_q_b":
        return g.reshape(HEADS, LANES, -1)[:, :NOPE + ROPE, :].reshape(HEADS * (NOPE + ROPE), -1)
    if name == "w_o":
        mla = g[:HEADS * LANES].reshape(HEADS, LANES, -1)[:, LANES - NOPE:, :].reshape(HEADS * NOPE, -1)
        return jnp.concatenate([mla, g[HEADS * LANES:]], axis=0)
    return g


SMALL = (("loss", 1), ("b_ada", 6 * D_MODEL), ("g_mix_norm", D_MODEL), ("g_q_lat", Q_LORA), ("g_kv_lat", KV_LORA),
         ("g_mla_q_nope", NOPE), ("g_mla_q_pe", ROPE), ("g_mla_k_nope", NOPE), ("g_mla_k_pe", ROPE),
         ("g_dil_q", DIL_DIM), ("g_dil_k", DIL_DIM), ("g_ffn_norm", D_MODEL), ("b_conv", 2 * D_FF))
SMALL_ROWS = 16
SMALL_COLS = 1024


def _pack_small(values):
    parts = [values[name].reshape(-1).astype(F32) if name in values else jnp.zeros((n,), F32) for name, n in SMALL]
    flat = jnp.concatenate(parts)
    flat = jnp.pad(flat, (0, SMALL_ROWS * SMALL_COLS - flat.shape[0]))
    return flat.reshape(SMALL_ROWS, SMALL_COLS)


def _unpack_small(packed):
    flat = packed.reshape(-1)
    out, off = {}, 0
    for name, n in SMALL:
        out[name] = flat[off:off + n].reshape(1, n)
        off += n
    return out


def _local_step(x, pos, mod, target, w, fetch, emit):
    S = SEQ
    sh1, sc1, g1, sh2, sc2, g2 = [mod[:, i * D_MODEL:(i + 1) * D_MODEL] for i in range(6)]
    zeros = lambda n: jnp.zeros((1, n), F32)
    g_q = jnp.concatenate([w["g_mla_q_nope"], w["g_mla_q_pe"], zeros(LANES - NOPE - ROPE)], axis=1)
    g_k = jnp.concatenate([w["g_mla_k_nope"], zeros(LANES - NOPE)], axis=1)
    g_kpe = jnp.concatenate([zeros(KPE_LO), w["g_mla_k_pe"], zeros(LANES - KPE_LO - ROPE)], axis=1)
    g_dq = jnp.concatenate([w["g_dil_q"]] * 2, axis=1)
    g_dk = jnp.concatenate([w["g_dil_k"]] * 2, axis=1)
    b_conv = w["b_conv"]

    def inv_freq(d):
        return jnp.power(ROPE_THETA, -2.0 * jnp.arange(d // 2, dtype=F32) / d)

    f_mla = jnp.concatenate([jnp.zeros((KPE_LO,), F32), inv_freq(ROPE), inv_freq(ROPE), jnp.zeros((LANES - KPE_LO - ROPE,), F32)])
    f_dil = jnp.concatenate([inv_freq(DIL_DIM)] * 4)

    def tables_fn(rows, params):
        (p,), (fa, fb) = rows, params
        return [jnp.cos(p * fa), jnp.sin(p * fa), jnp.cos(p * fb), jnp.sin(p * fb)], []

    cos_m, sin_m, cos_d, sin_d = rowwise("rope_tables", tables_fn, [pos], [f_mla.reshape(1, LANES), f_dil.reshape(1, LANES)],
                                         [(LANES, F32)] * 4)
    tables = [cos_m, sin_m, cos_d, sin_d]
    H_M, H_D = ROPE // 2, DIL_DIM // 2

    def ln1_fn(rows, params):
        (xv,), (g, sc, sh) = rows, params
        y, _, _ = _rms(xv, g)
        return [y * (1.0 + sc) + sh], []

    (h,) = rowwise("ln1_fwd", ln1_fn, [x], [w["g_mix_norm"], sc1, sh1], [(D_MODEL, MXU_DTYPE)])
    w_in = fetch("w_in", h)
    proj = matmul("proj_fwd", h, w_in, "nt", tm=512)

    def post_fn(rows, params):
        (pv, cm, sm, cd, sd), (gq, gkv, gkp, gdq, gdk) = rows, params
        kper = _rope(_grms(pv[:, P_KPE:P_QD], gkp, KPE_GROUPS)[0], cm, sm, H_M)
        qd = [_rope(_grms(c, gdq, DIL_GROUPS)[0], cd, sd, H_D) for c in _chunks(pv[:, P_QD:P_KD])]
        kd = [_rope(_grms(c, gdk, DIL_GROUPS)[0], cd, sd, H_D) for c in _chunks(pv[:, P_KD:P_VD])]
        return [_rms(pv[:, P_QLAT:P_KVLAT], gq)[0], _rms(pv[:, P_KVLAT:P_KPE], gkv)[0], kper,
                jnp.concatenate(qd, axis=1), jnp.concatenate(kd, axis=1)], []

    post_params = [w["g_q_lat"], w["g_kv_lat"], g_kpe, g_dq, g_dk]
    qln, kvn, kper, qd_r, kd_r = rowwise(
        "proj_post", post_fn, [proj] + tables, post_params,
        [(Q_LORA, MXU_DTYPE), (KV_LORA, MXU_DTYPE), (LANES, MXU_DTYPE)] + [(DIL_WIDTH, F32)] * 2, tm=256)
    w_q_b, w_kv_b = fetch("w_q_b", qln), fetch("w_kv_b", kvn)
    q = matmul("q_fwd", qln, w_q_b, "nt", tm=1024)
    kv = matmul("kv_fwd", kvn, w_kv_b, "nn", tm=1024)

    def mla_prep_fn(rows, params):
        (qv, kvv, kp, cm, sm), (gq, gk) = rows, params
        value_lanes = _lane(kp.shape) >= NOPE
        qs, ks, vs = [], [], []
        for qc, kc in zip(_chunks(qv), _chunks(kvv), strict=True):
            qs.append(_rope(_grms(qc, gq, Q_GROUPS)[0], cm, sm, H_M))
            ks.append(_grms(kc, gk, K_GROUPS)[0] + kp)
            vs.append(jnp.where(value_lanes, kc, 0.0))
        return [jnp.concatenate(t, axis=1) for t in (qs, ks, vs)], []

    q_mla, k_mla, v_mla = rowwise("mla_prep", mla_prep_fn, [q, kv, kper, cos_m, sin_m], [g_q, g_k],
                                  [(HEADS * LANES, MXU_DTYPE)] * 3, tm=256)
    mla_scale = (NOPE + ROPE) ** -0.5
    o_cat, lse_mla = mla_fwd("mla_fwd", q_mla, k_mla, v_mla, mla_scale)

    band = [band_fwd(f"band{dil}_fwd", qd_r, kd_r, proj, dil) for dil in DILATIONS]
    o_cat, lse_mix = combine_fwd("dil_combine", [b[0] for b in band], [b[1] for b in band], o_cat)
    w_o = fetch("w_o", o_cat)
    mix = matmul("mix_fwd", o_cat, w_o, "nn", tm=512)

    def mid_fn(rows, params):
        (xv, mx), (gate1, g, sc, sh) = rows, params
        x1 = xv + gate1 * mx
        y, _, _ = _rms(x1, g)
        return [x1, y * (1.0 + sc) + sh], []

    x1, h2 = rowwise("mid_fwd", mid_fn, [x, mix], [g1, w["g_ffn_norm"], sc2, sh2], [(D_MODEL, F32), (D_MODEL, MXU_DTYPE)])
    w_up, w_conv, w_down = fetch("w_up", h2), fetch("w_conv", h2), fetch("w_down", h2)
    act = conv_glu_fwd("conv_fwd", h2, w_up, w_conv, b_conv)
    dn = matmul("down_fwd", act, w_down, "nn", tm=512)

    def final_fn(rows, params):
        (x1v, dnv, tgt), (gate2,) = rows, params
        r = x1v + gate2 * dnv - tgt
        dy = r * (1.0 / D_MODEL)
        loss = jnp.sum(_colsum(r * r), axis=-1, keepdims=True) * (0.5 / D_MODEL)
        return [dy, gate2 * dy], [loss, _colsum(dy * dnv)]

    dy, d_dn, loss, dg2 = rowwise("loss_head", final_fn, [x1, dn, target], [g2], [(D_MODEL, F32), (D_MODEL, MXU_DTYPE)],
                                  [1, D_MODEL])
    emit("w_down", matmul("down_wgrad", act, d_dn, "tn", tm=1408, out_dtype=MXU_DTYPE))
    dup, g_w_conv, g_b_conv = conv_glu_bwd("conv_bwd", h2, w_up, w_conv, b_conv, d_dn, w_down)
    emit("w_conv", g_w_conv)
    sent = emit("w_up", matmul("up_wgrad", dup, h2, "tn", tm=CONV_TC, out_dtype=MXU_DTYPE, a_mmap=natural_block))
    dh2 = pair_dgrad("up_dgrad", dup, w_up, dep=sent)

    def mid_bwd_fn(rows, params):
        (dh2v, dyv, x1v, mx), (gate1, g, sc) = rows, params
        yn, n, rstd = _rms(x1v, g)
        dx_n, dg = _rms_bwd(dh2v * (1.0 + sc), n, rstd, g)
        dx1 = dyv + dx_n
        return [dx1, gate1 * dx1], [dg, _colsum(dh2v * yn), _colsum(dh2v), _colsum(dx1 * mx)]

    dx1, dmix, dg_ffn, dsc2, dsh2, dg1 = rowwise(
        "mid_bwd", mid_bwd_fn, [dh2, dy, x1, mix], [g1, w["g_ffn_norm"], sc2], [(D_MODEL, F32), (D_MODEL, MXU_DTYPE)],
        [D_MODEL] * 4)

    sent = emit("w_o", matmul("mix_wgrad", o_cat, dmix, "tn", tm=512, out_dtype=MXU_DTYPE))
    do_cat = matmul("mix_dgrad", dmix, w_o, "nt", tm=512, dep=sent)
    dband = [band_bwd(f"band{dil}_bwd", qd_r, kd_r, proj, b[1], lse_mix, o_cat, do_cat, dil) for dil, b in zip(DILATIONS, band)]
    dq_mla, dkv_mla, dkper = mla_bwd("mla_bwd", q_mla, k_mla, v_mla, o_cat, do_cat, lse_mla, mla_scale)

    def mla_prep_bwd_fn(rows, params):
        (dqv, dkvv, qv, kvv, cm, sm), (gq, gk) = rows, params
        nope_lanes = _lane(cm.shape) < NOPE
        dqs, dkvs, dgq, dgk = [], [], 0.0, 0.0
        for dqc, dkc, qc, kc in zip(_chunks(dqv), _chunks(dkvv), _chunks(qv), _chunks(kvv), strict=True):
            _, n, rstd = _grms(qc, gq, Q_GROUPS)
            dx, dg = _grms_bwd(_rope_bwd(dqc, cm, sm, H_M), n, rstd, gq, Q_GROUPS)
            dqs.append(dx)
            dgq = dgq + dg
            _, n, rstd = _grms(kc, gk, K_GROUPS)
            dx, dg = _grms_bwd(dkc, n, rstd, gk, K_GROUPS)
            dkvs.append(jnp.where(nope_lanes, dx, dkc))
            dgk = dgk + dg
        return [jnp.concatenate(dqs, axis=1), jnp.concatenate(dkvs, axis=1)], [dgq, dgk]

    dq, dkv, dg_q, dg_k = rowwise("mla_prep_bwd", mla_prep_bwd_fn, [dq_mla, dkv_mla, q, kv, cos_m, sin_m], [g_q, g_k],
                                  [(HEADS * LANES, MXU_DTYPE)] * 2, [LANES, LANES], tm=256)
    emit("w_q_b", matmul("q_wgrad", dq, qln, "tn", out_dtype=MXU_DTYPE))
    emit("w_kv_b", matmul("kv_wgrad", kvn, dkv, "tn", out_dtype=MXU_DTYPE))
    dqln = matmul("q_dgrad", dq, w_q_b, "nn", tm=1024)
    dkvn = matmul("kv_dgrad", dkv, w_kv_b, "nt", tm=1024)

    def pre_bwd_fn(rows, params):
        dql, dkvl, dkp = rows[0:3]
        dqd_, dkd_, dvd_ = [rows[3 + 3 * i] + rows[4 + 3 * i] + rows[5 + 3 * i] for i in range(3)]
        pv, cm, sm, cd, sd = rows[12:]
        gq, gkv, gkp, gdq, gdk = params
        r_q = _norm_bwd(dql, pv[:, P_QLAT:P_KVLAT], gq)
        r_kv = _norm_bwd(dkvl, pv[:, P_KVLAT:P_KPE], gkv)
        _, n, rstd = _grms(pv[:, P_KPE:P_QD], gkp, KPE_GROUPS)
        r_kp = _grms_bwd(_rope_bwd(dkp, cm, sm, H_M), n, rstd, gkp, KPE_GROUPS)
        outs, dgs = [r_q[0], r_kv[0], r_kp[0]], []
        for dval, lo, g in ((dqd_, P_QD, gdq), (dkd_, P_KD, gdk)):
            dg_sum = 0.0
            for dc, xc in zip(_chunks(dval), _chunks(pv[:, lo:lo + DIL_WIDTH]), strict=True):
                _, n, rstd = _grms(xc, g, DIL_GROUPS)
                dx, dg = _grms_bwd(_rope_bwd(dc, cd, sd, H_D), n, rstd, g, DIL_GROUPS)
                outs.append(dx)
                dg_sum = dg_sum + dg
            dgs.append(dg_sum)
        return [jnp.concatenate(outs + [dvd_], axis=1)], [r_q[1], r_kv[1], r_kp[1]] + dgs

    dproj, dg_q_lat, dg_kv_lat, dg_kpe, dg_dq, dg_dk = rowwise(
        "proj_pre_bwd", pre_bwd_fn,
        [dqln, dkvn, dkper] + [d[i] for i in range(3) for d in dband] + [proj] + tables, post_params,
        [(P_END, MXU_DTYPE)], [Q_LORA, KV_LORA, LANES, LANES, LANES], tm=256)
    sent = emit("w_in", matmul("proj_wgrad", dproj, h, "tn", tn=512, out_dtype=MXU_DTYPE))
    dh = matmul("proj_dgrad", dproj, w_in, "nn", tm=512, dep=sent)

    def ln1_bwd_fn(rows, params):
        (dhv, dres, xv), (g, sc) = rows, params
        yn, n, rstd = _rms(xv, g)
        dx_n, dg = _rms_bwd(dhv * (1.0 + sc), n, rstd, g)
        return [dres + dx_n], [dg, _colsum(dhv * yn), _colsum(dhv)]

    grad_x, dg_mix, dsc1, dsh1 = rowwise("ln1_bwd", ln1_bwd_fn, [dh, dx1, x], [w["g_mix_norm"], sc1], [(D_MODEL, F32)],
                                         [D_MODEL] * 3)
    dmod = jnp.concatenate([dsh1, dsc1, dg1, dsh2, dsc2, dg2], axis=-1)
    small = {"loss": loss, "b_ada": dmod, "g_mix_norm": dg_mix, "g_q_lat": dg_q_lat, "g_kv_lat": dg_kv_lat,
             "g_mla_q_nope": dg_q[:, :NOPE], "g_mla_q_pe": dg_q[:, NOPE:NOPE + ROPE], "g_mla_k_nope": dg_k[:, :NOPE],
             "g_mla_k_pe": dg_kpe[:, KPE_LO:KPE_LO + ROPE], "g_dil_q": dg_dq[:, :DIL_DIM] + dg_dq[:, DIL_DIM:],
             "g_dil_k": dg_dk[:, :DIL_DIM] + dg_dk[:, DIL_DIM:], "g_ffn_norm": dg_ffn,
             "b_conv": g_b_conv}
    return grad_x, small


COL_SHARDED = ("w_kv_b", "w_conv")
ROW_SHARDED = ("w_o", "w_down") + TRANSPOSED
ADAM_TILE = {"w_ada": 256, "w_up": 176, "w_down": 176}
GATHER_GROUPS = (("w_in",), ("w_q_b", "w_kv_b"), ("w_o", "w_up", "w_conv", "w_down"))
SCATTER_GROUPS = (("w_down", "w_conv", "w_up"), ("w_o",), ("w_q_b", "w_kv_b", "w_in"))
OUT_WEIGHTS = ("w_ada", "b_ada", "g_mix_norm", "w_in", "g_q_lat", "w_q_b", "g_kv_lat", "w_kv_b", "g_mla_q_nope", "g_mla_q_pe",
               "g_mla_k_nope", "g_mla_k_pe", "g_dil_q", "g_dil_k", "w_o", "g_ffn_norm", "w_up", "w_conv", "b_conv", "w_down")


def kernel(x, c, positions, w_ada, b_ada, g_mix_norm, w_in, g_q_lat, w_q_b, g_kv_lat, w_kv_b, g_mla_q_nope, g_mla_q_pe, g_mla_k_nope, g_mla_k_pe, g_dil_q, g_dil_k, w_o, g_ffn_norm, w_up, w_conv, b_conv, w_down, loss_target, m_w_ada, m_b_ada, m_g_mix_norm, m_w_in, m_g_q_lat, m_w_q_b, m_g_kv_lat, m_w_kv_b, m_g_mla_q_nope, m_g_mla_q_pe, m_g_mla_k_nope, m_g_mla_k_pe, m_g_dil_q, m_g_dil_k, m_w_o, m_g_ffn_norm, m_w_up, m_w_conv, m_b_conv, m_w_down, v_w_ada, v_b_ada, v_g_mix_norm, v_w_in, v_g_q_lat, v_w_q_b, v_g_kv_lat, v_w_kv_b, v_g_mla_q_nope, v_g_mla_q_pe, v_g_mla_k_nope, v_g_mla_k_pe, v_g_dil_q, v_g_dil_k, v_w_o, v_g_ffn_norm, v_w_up, v_w_conv, v_b_conv, v_w_down):
    args = dict(locals())
    xi, yi, ci = _place()
    me = 4 * xi + 2 * yi + ci
    def local(prefix, n):
        a = args[prefix + n][0]
        return a.T if n in TRANSPOSED else a

    shard = {n: local("", n) for n in COL_SHARDED + ROW_SHARDED + ("w_ada",)}
    small_w = {n: args[n] for n, _ in SMALL if n != "loss"}

    (c_all,) = all_gather("gather_c", [c])
    (sc_all,) = rowwise("silu_c", lambda rows, params: ([_silu(rows[0])], []), [c_all.reshape(N_DEV, D_MODEL)], [],
                        [(D_MODEL, MXU_DTYPE)])
    mod_part = matmul("ada_fwd", sc_all, shard["w_ada"], "nn")
    (mod_all,) = all_gather("gather_mod", [mod_part])

    payload = {n: shard[n] if n == "w_conv" else shard[n].astype(MXU_DTYPE) for n in COL_SHARDED + ROW_SHARDED}
    gathers, after_start = [], mod_all
    for i, grp in enumerate(GATHER_GROUPS):
        gathers.append(exchange_start(f"gather{i}_start", [payload[n] for n in grp], gather=True, after=after_start))
        after_start = gathers[-1][-1]
    full = {}

    def fetch(name, after):
        if name not in full:
            (i, grp), = [(i, grp) for i, grp in enumerate(GATHER_GROUPS) if name in grp]
            srcs, lands = exchange_wait(f"gather{i}_wait", gathers[i], True, after)
            for n, src, land in zip(grp, srcs, lands, strict=True):
                stack = lax.dynamic_update_index_in_dim(land, src, me, 0)
                full[n] = to_kernel_layout(n, _gather_cols(stack) if n in COL_SHARDED else _gather_rows(stack))
        return full[name]

    mod_row = lax.dynamic_index_in_dim(mod_all, me, axis=1, keepdims=False).reshape(1, 6 * D_MODEL)
    (mod,) = rowwise("ada_bias", lambda rows, params: ([rows[0] + rows[1]], []), [mod_row, b_ada], [], [(6 * D_MODEL, F32)],
                     dep=after_start)

    own, pending, scatters = {}, {}, {}

    def emit(name, grad):
        grad = from_kernel_layout(name, grad)
        parts = _scatter_cols(grad) if name in COL_SHARDED else _scatter_rows(grad)
        own[name] = lax.dynamic_index_in_dim(parts, me, 0, keepdims=False)
        pending[name] = parts
        for i, grp in enumerate(SCATTER_GROUPS):
            if name == grp[-1]:
                scatters[i] = exchange_start(f"scatter{i}_start", [pending[n] for n in grp], gather=False)
                return scatters[i][-1]
        return None

    pos = positions.reshape(SEQ, 1).astype(F32)
    grad_x, small = _local_step(x[0], pos, mod, loss_target[0], small_w, fetch, emit)

    res, done = {}, grad_x
    for i, grp in enumerate(SCATTER_GROUPS):
        _, lands = exchange_wait(f"scatter{i}_wait", scatters[i], False, done)
        for n, land in zip(grp, lands, strict=True):
            res[n] = adamw(f"adamw_{n}", shard[n], [own[n], land], local("m_", n), local("v_", n), ADAM_TILE.get(n))
            done = res[n][0]
            if n in TRANSPOSED:
                res[n] = [r.T for r in res[n]]
    (small_all,) = all_gather("gather_small", [_pack_small(small)], after=done)
    packed = adamw("adamw_small", _pack_small(small_w), [small_all], _pack_small({n: args["m_" + n] for n in small_w}),
                   _pack_small({n: args["v_" + n] for n in small_w}))
    small_res = [_unpack_small(p) for p in packed]
    dmod_all = small_all.reshape(N_DEV, -1)[:, 1:1 + 6 * D_MODEL]
    dmod_mine = lax.dynamic_slice_in_dim(dmod_all, me * (6 * D_MODEL // N_DEV), 6 * D_MODEL // N_DEV, axis=1)
    g_w_ada = matmul("ada_wgrad", sc_all, dmod_mine, "tn")
    res["w_ada"] = adamw("adamw_w_ada", shard["w_ada"], [g_w_ada], m_w_ada[0], v_w_ada[0], ADAM_TILE["w_ada"])

    def leaf(kind, n):
        if n in res:
            return res[n][kind][None]
        return small_res[kind][n]

    loss = small_res[0]["loss"].reshape(())
    return (loss, grad_x[None], *[leaf(k, n) for k in range(4) for n in OUT_WEIGHTS])
```

```python
import jax
import jax.numpy as jnp
from jax import lax
from jax.experimental import pallas as pl
from jax.experimental.pallas import tpu as pltpu

F32 = jnp.float32
MXU_DTYPE = jnp.bfloat16

N_DEV = 8
D_MODEL = 1024
SEQ = 2048
HEADS = 8
NOPE = 64
ROPE = 32
Q_LORA = 512
KV_LORA = 256
DIL_DIM = 64
DIL_WIDTH = HEADS * DIL_DIM
DILATIONS = (1, 4, 16)
SPAN = 128
D_FF = 2816
LANES = 128
ROPE_THETA = 10000.0
EPS = 1e-6
NEG_INF = -1e30
ADAM_LR, ADAM_B1, ADAM_B2, ADAM_EPS, ADAM_WD, ADAM_STEP = 0.001, 0.9, 0.999, 1e-08, 0.01, 10
VMEM_LIMIT = 56 * 1024 * 1024
MESH_ID = pl.DeviceIdType.MESH

P_QLAT, P_KVLAT, P_KPE, P_QD, P_KD, P_VD, P_END = 0, 512, 768, 896, 1408, 1920, 2432
KPE_LO = 64
MIX_IN = HEADS * LANES + DIL_WIDTH


def _params(**kw):
    return pltpu.CompilerParams(vmem_limit_bytes=VMEM_LIMIT, **kw)


def rowwise(name, fn, rows, params, out_rows, out_accs=(), tm=512, dep=None):
    deps = [] if dep is None else [dep]
    rows = [r if isinstance(r, tuple) else (r, r.shape[1], 0) for r in rows]
    R = rows[0][0].shape[0]
    tm = min(tm, R)
    steps = R // tm
    assert steps * tm == R
    in_specs = []
    for a, width, cb in rows:
        ri = a.shape[0]
        per = ri // tm
        assert per * tm == ri
        if ri == R:
            in_specs.append(pl.BlockSpec((tm, width), lambda i, cb=cb: (i, cb)))
        else:
            in_specs.append(pl.BlockSpec((tm, width), lambda i, per=per, cb=cb: (i % per, cb)))
    for p in params:
        in_specs.append(pl.BlockSpec(p.shape, lambda i: (0,) * p.ndim))
    in_specs += [pl.BlockSpec(memory_space=pl.ANY)] * len(deps)
    out_shape = [jax.ShapeDtypeStruct((R, d), dt) for d, dt in out_rows]
    out_specs = [pl.BlockSpec((tm, d), lambda i: (i, 0)) for d, _ in out_rows]
    out_shape += [jax.ShapeDtypeStruct((1, n), F32) for n in out_accs]
    out_specs += [pl.BlockSpec((1, n), lambda i: (0, 0)) for n in out_accs]
    nr, npar, no, na = len(rows), len(params), len(out_rows), len(out_accs)

    def body(*refs):
        rvals = [r[...] for r in refs[:nr]]
        pvals = [r[...] for r in refs[nr:nr + npar]]
        outs, accs = fn(rvals, pvals)
        first_out = nr + npar + len(deps)
        for ref, v in zip(refs[first_out:first_out + no], outs, strict=True):
            ref[...] = v.astype(ref.dtype)
        if na:
            acc_refs = refs[first_out + no:]
            i = pl.program_id(0)

            @pl.when(i == 0)
            def _():
                for ref, v in zip(acc_refs, accs, strict=True):
                    ref[...] = v

            @pl.when(i > 0)
            def _():
                for ref, v in zip(acc_refs, accs, strict=True):
                    ref[...] += v

    res = pl.pallas_call(body, name=name, grid=(steps,), in_specs=in_specs, out_specs=out_specs,
                         out_shape=out_shape, compiler_params=_params())(*[r[0] for r in rows], *params, *deps)
    return list(res)


_DIMS = {"nn": ((1,), (0,)), "nt": ((1,), (1,)), "tn": ((0,), (0,))}


def _dot(a, b, mode="nn"):
    return lax.dot_general(a.astype(MXU_DTYPE), b.astype(MXU_DTYPE), (_DIMS[mode], ((), ())),
                           preferred_element_type=F32)


def matmul(name, a, b, mode, tm=None, tn=None, tk=None, out_dtype=F32, dep=None, a_mmap=None, b_nmap=None, b_kmap=None):
    if mode == "tn":
        K, M = a.shape
    else:
        M, K = a.shape
    N = b.shape[0] if mode == "nt" else b.shape[1]
    tm, tn, tk = tm or M, tn or N, tk or K
    nm, nn, nk = M // tm, N // tn, K // tk
    assert nm * tm == M and nn * tn == N and nk * tk == K
    same = lambda idx: idx
    a_mmap, b_nmap, b_kmap = a_mmap or same, b_nmap or same, b_kmap or same
    if mode == "tn":
        a_spec = pl.BlockSpec((tk, tm), lambda i, j, k: (k, a_mmap(i)))
    else:
        a_spec = pl.BlockSpec((tm, tk), lambda i, j, k: (a_mmap(i), k))
    if mode == "nt":
        b_spec = pl.BlockSpec((tn, tk), lambda i, j, k: (b_nmap(j), b_kmap(k)))
    else:
        b_spec = pl.BlockSpec((tk, tn), lambda i, j, k: (b_kmap(k), b_nmap(j)))
    deps = [] if dep is None else [dep]

    def body(a_ref, b_ref, *rest):
        o_ref, scratch = rest[len(deps)], rest[len(deps) + 1:]
        p = _dot(a_ref[...], b_ref[...], mode)
        if nk == 1:
            o_ref[...] = p.astype(o_ref.dtype)
        else:
            acc = scratch[0]
            k = pl.program_id(2)

            @pl.when(k == 0)
            def _():
                acc[...] = p

            @pl.when(k > 0)
            def _():
                acc[...] += p

            @pl.when(k == nk - 1)
            def _():
                o_ref[...] = acc[...].astype(o_ref.dtype)

    return pl.pallas_call(
        body, name=name, grid=(nm, nn, nk), in_specs=[a_spec, b_spec] + [pl.BlockSpec(memory_space=pl.ANY)] * len(deps),
        out_specs=pl.BlockSpec((tm, tn), lambda i, j, k: (i, j)),
        out_shape=jax.ShapeDtypeStruct((M, N), out_dtype),
        scratch_shapes=[pltpu.VMEM((tm, tn), F32)] if nk > 1 else [],
        compiler_params=_params())(a, b, *deps)


def _rms(x, g):
    rstd = lax.rsqrt(jnp.mean(x * x, axis=-1, keepdims=True) + EPS)
    n = x * rstd
    return n * g, n, rstd


def _rms_bwd(dy, n, rstd, g):
    dg = jnp.sum(dy * n, axis=0, keepdims=True)
    dn = dy * g
    dx = rstd * (dn - n * jnp.mean(dn * n, axis=-1, keepdims=True))
    return dx, dg


def _norm_bwd(dy, x, g):
    _, n, rstd = _rms(x, g)
    return _rms_bwd(dy, n, rstd, g)


def _colsum(v):
    return jnp.sum(v, axis=0, keepdims=True)


def _silu(x):
    return x * (1.0 / (1.0 + jnp.exp(-x)))


def _lane(shape):
    return lax.broadcasted_iota(jnp.int32, shape, 1)


def _group_mean(v, groups):
    lane = _lane(v.shape)
    out = jnp.zeros_like(v)
    for lo, hi in groups:
        m = (lane >= lo) & (lane < hi)
        out = jnp.where(m, jnp.sum(jnp.where(m, v, 0.0), axis=-1, keepdims=True) * (1.0 / (hi - lo)), out)
    return out


def _in_groups(shape, groups):
    lane = _lane(shape)
    m = jnp.zeros(shape, jnp.bool_)
    for lo, hi in groups:
        m = m | ((lane >= lo) & (lane < hi))
    return m


def _grms(x, g, groups):
    rstd = lax.rsqrt(_group_mean(x * x, groups) + EPS)
    n = jnp.where(_in_groups(x.shape, groups), x * rstd, 0.0)
    return n * g, n, rstd


def _grms_bwd(dy, n, rstd, g, groups):
    dn = dy * g
    return rstd * (dn - n * _group_mean(dn * n, groups)), _colsum(dy * n)


def _rot(x, half, transpose=False):
    first = (_lane(x.shape) % (2 * half)) < half
    up = pltpu.roll(x, LANES - half, axis=1)
    down = pltpu.roll(x, half, axis=1)
    return jnp.where(first, up, -down) if transpose else jnp.where(first, -up, down)


def _rope(x, cos, sin, half):
    return x * cos + _rot(x, half) * sin


def _rope_bwd(dy, cos, sin, half):
    return dy * cos + _rot(dy * sin, half, transpose=True)


def _chunks(x):
    return [x[:, i:i + LANES] for i in range(0, x.shape[1], LANES)]


Q_GROUPS = ((0, NOPE), (NOPE, NOPE + ROPE))
K_GROUPS = ((0, NOPE),)
KPE_GROUPS = ((KPE_LO, KPE_LO + ROPE),)
DIL_GROUPS = ((0, DIL_DIM), (DIL_DIM, 2 * DIL_DIM))


def _col(width, rows=SEQ):
    return pl.BlockSpec((rows, width), lambda h: (0, h))


def _causal_tail(s, tq, fill):
    diag = s[:, s.shape[1] - tq:]
    keep = lax.broadcasted_iota(jnp.int32, diag.shape, 1) <= lax.broadcasted_iota(jnp.int32, diag.shape, 0)
    diag = jnp.where(keep, diag, fill)
    return diag if s.shape[1] == tq else jnp.concatenate([s[:, :s.shape[1] - tq], diag], axis=1)


def mla_fwd(name, q, k, v, scale, tq=256):
    S = q.shape[0]

    def body(q_ref, k_ref, v_ref, o_ref, lse_ref):
        for i in range(S // tq):
            kext = (i + 1) * tq
            blk = slice(i * tq, kext)
            s = _causal_tail(_dot(q_ref[blk, :], k_ref[:kext, :], "nt") * scale, tq, NEG_INF)
            m = jnp.max(s, axis=-1, keepdims=True)
            e = jnp.exp(s - m)
            l = jnp.sum(e, axis=-1, keepdims=True)
            o_ref[blk, :] = _dot(e * (1.0 / l), v_ref[:kext, :])
            lse_ref[0, blk, :] = m + jnp.log(l)

    return pl.pallas_call(
        body, name=name, grid=(HEADS,), in_specs=[_col(LANES)] * 3,
        out_specs=[_col(LANES), pl.BlockSpec((1, S, 1), lambda h: (h, 0, 0))],
        out_shape=[jax.ShapeDtypeStruct((S, MIX_IN), F32), jax.ShapeDtypeStruct((HEADS, S, 1), F32)],
        compiler_params=_params())(q, k, v)


def mla_bwd(name, q, k, v, o, do, lse, scale, tq=256):
    S = q.shape[0]

    def body(q_ref, k_ref, v_ref, o_ref, do_ref, lse_ref, dq_ref, dkv_ref, dkpe_ref, dk_acc, dv_acc):
        dk_acc[...] = jnp.zeros_like(dk_acc)
        dv_acc[...] = jnp.zeros_like(dv_acc)
        for i in range(S // tq):
            kext = (i + 1) * tq
            blk = slice(i * tq, kext)
            qi, kk, vv = q_ref[blk, :], k_ref[:kext, :], v_ref[:kext, :]
            doi = do_ref[blk, :]
            s = _causal_tail(_dot(qi, kk, "nt") * scale, tq, NEG_INF)
            p = jnp.exp(s - lse_ref[0, blk, :])
            dp = _dot(doi, vv, "nt")
            delta = jnp.sum(doi * o_ref[blk, :], axis=-1, keepdims=True)
            ds = p * (dp - delta) * scale
            dq_ref[blk, :] = _dot(ds, kk)
            dk_acc[:kext, :] += _dot(ds, qi, "tn")
            dv_acc[:kext, :] += _dot(p, doi, "tn")
        dk = dk_acc[...]
        lane = _lane(dk.shape)
        dkv_ref[...] = jnp.where(lane < NOPE, dk, 0.0) + dv_acc[...]
        dkpe = jnp.where((lane >= KPE_LO) & (lane < KPE_LO + ROPE), dk, 0.0)
        h = pl.program_id(0)

        @pl.when(h == 0)
        def _():
            dkpe_ref[...] = dkpe

        @pl.when(h > 0)
        def _():
            dkpe_ref[...] += dkpe

    return pl.pallas_call(
        body, name=name, grid=(HEADS,),
        in_specs=[_col(LANES)] * 5 + [pl.BlockSpec((1, S, 1), lambda h: (h, 0, 0))],
        out_specs=[_col(LANES), _col(LANES), pl.BlockSpec((S, LANES), lambda h: (0, 0))],
        out_shape=[jax.ShapeDtypeStruct((S, HEADS * LANES), F32), jax.ShapeDtypeStruct((S, HEADS * LANES), F32),
                   jax.ShapeDtypeStruct((S, LANES), F32)],
        scratch_shapes=[pltpu.VMEM((S, LANES), F32), pltpu.VMEM((S, LANES), F32)],
        compiler_params=_params())(q, k, v, o, do, lse)


BAND_TQ = SPAN


def _band_blocks(L, tq):
    return [(i * tq, (i + 1) * tq, max(0, i * tq - SPAN)) for i in range(L // tq)]


def _band_mask(q0, q1, k0):
    shape = (q1 - q0, q1 - k0)
    dist = (lax.broadcasted_iota(jnp.int32, shape, 0) + q0) - (lax.broadcasted_iota(jnp.int32, shape, 1) + k0)
    return (dist >= 0) & (dist <= SPAN)


def _class_rows(r, dil, lo, hi):
    return pl.ds(r + dil * lo, hi - lo, stride=dil) if dil > 1 else pl.ds(lo, hi - lo)


def _stack_heads(t, lo):
    zero = jnp.zeros_like(t)
    return jnp.concatenate([jnp.where(lo, t, zero), jnp.where(lo, zero, t)], axis=0)


def _band_mask2(q0, q1, k0):
    n = q1 - q0
    shape = (2 * n, q1 - k0)
    i = lax.broadcasted_iota(jnp.int32, shape, 0)
    dist = (jnp.where(i >= n, i - n, i) + q0) - (lax.broadcasted_iota(jnp.int32, shape, 1) + k0)
    return (dist >= 0) & (dist <= SPAN)


def _pair_col(col0=0):
    return pl.BlockSpec((SEQ, LANES), lambda j: (0, col0 // LANES + j))


def band_fwd(name, q, k, v, dil):
    S = q.shape[0]
    L = S // dil
    tq = BAND_TQ
    scale = DIL_DIM ** -0.5

    def body(q_ref, k_ref, v_ref, o_ref, lse_ref):
        for r in range(dil):
            for q0, q1, k0 in _band_blocks(L, tq):
                qrows, krows = _class_rows(r, dil, q0, q1), _class_rows(r, dil, k0, q1)
                qb, kb, vb = q_ref[qrows, :].astype(MXU_DTYPE), k_ref[krows, :], v_ref[krows, :]
                n = q1 - q0
                lo = _lane(qb.shape) < DIL_DIM
                s = _dot(_stack_heads(qb, lo), kb, "nt") * scale
                s = jnp.where(_band_mask2(q0, q1, k0), s, NEG_INF)
                mx = jnp.max(s, axis=-1, keepdims=True)
                e = jnp.exp(s - mx)
                l = jnp.sum(e, axis=-1, keepdims=True)
                pv, lse = _dot(e * (1.0 / l), vb), mx + jnp.log(l)
                o_ref[qrows, :] = jnp.where(lo, pv[:n], pv[n:])
                lse_ref[qrows, :] = jnp.where(lo, lse[:n], lse[n:])

    return pl.pallas_call(
        body, name=name, grid=(DIL_WIDTH // LANES,), in_specs=[_pair_col()] * 2 + [_pair_col(P_VD)], out_specs=[_pair_col()] * 2,
        out_shape=[jax.ShapeDtypeStruct((S, DIL_WIDTH), F32)] * 2, compiler_params=_params())(q, k, v)


def band_bwd(name, q, k, v, lse, lse_mix, o_cat, do_cat, dil):
    S = q.shape[0]
    L = S // dil
    tq = BAND_TQ
    scale = DIL_DIM ** -0.5

    def body(q_ref, k_ref, v_ref, lse_ref, mix_ref, o_ref, do_ref, dq_ref, dk_ref, dv_ref):
        dk_ref[...] = jnp.zeros_like(dk_ref)
        dv_ref[...] = jnp.zeros_like(dv_ref)
        for r in range(dil):
            for q0, q1, k0 in _band_blocks(L, tq):
                qrows, krows = _class_rows(r, dil, q0, q1), _class_rows(r, dil, k0, q1)
                qb, kb, vb = q_ref[qrows, :].astype(MXU_DTYPE), k_ref[krows, :], v_ref[krows, :]
                lse_p, dout = lse_ref[qrows, :], do_ref[qrows, :]
                n = q1 - q0
                lo = _lane(qb.shape) < DIL_DIM
                per_head = lambda t: jnp.concatenate([t[:, 0:1], t[:, DIL_DIM:DIL_DIM + 1]], axis=0)
                w2 = per_head(jnp.exp(lse_p - mix_ref[qrows, :]))
                dd = dout * o_ref[qrows, :]
                big_d = jnp.concatenate([jnp.sum(jnp.where(lo, dd, 0.0), axis=-1, keepdims=True),
                                         jnp.sum(jnp.where(lo, 0.0, dd), axis=-1, keepdims=True)], axis=0)
                q2 = _stack_heads(qb, lo)
                s = _dot(q2, kb, "nt") * scale
                p = jnp.where(_band_mask2(q0, q1, k0), jnp.exp(s - per_head(lse_p)), 0.0)
                dom = _stack_heads(dout, lo) * w2
                ds = p * (_dot(dom, vb, "nt") - w2 * big_d) * scale
                dq2 = _dot(ds, kb)
                dq_ref[qrows, :] = jnp.where(lo, dq2[:n], dq2[n:])
                dk_ref[krows, :] += _dot(ds, q2, "tn")
                dv_ref[krows, :] += _dot(p, dom, "tn")

    cat = _pair_col(HEADS * LANES)
    return pl.pallas_call(
        body, name=name, grid=(DIL_WIDTH // LANES,),
        in_specs=[_pair_col()] * 2 + [_pair_col(P_VD)] + [_pair_col()] * 2 + [cat] * 2, out_specs=[_pair_col()] * 3,
        out_shape=[jax.ShapeDtypeStruct((S, DIL_WIDTH), F32)] * 3,
        compiler_params=_params())(q, k, v, lse, lse_mix, o_cat, do_cat)


def combine_fwd(name, outs, lses, o_cat, tm=512):
    S = outs[0].shape[0]

    def body(o1, o2, o3, l1, l2, l3, cat_in, cat_out, mix_ref):
        ls = [l1[...], l2[...], l3[...]]
        m = jnp.maximum(jnp.maximum(ls[0], ls[1]), ls[2])
        e = [jnp.exp(l - m) for l in ls]
        den = e[0] + e[1] + e[2]
        cat_out[...] = (e[0] / den) * o1[...] + (e[1] / den) * o2[...] + (e[2] / den) * o3[...]
        mix_ref[...] = m + jnp.log(den)

    row = pl.BlockSpec((tm, DIL_WIDTH), lambda i: (i, 0))
    return pl.pallas_call(
        body, name=name, grid=(S // tm,), in_specs=[row] * 6 + [pl.BlockSpec(memory_space=pl.ANY)],
        out_specs=[pl.BlockSpec((tm, DIL_WIDTH), lambda i: (i, HEADS * LANES // DIL_WIDTH)), row],
        out_shape=[jax.ShapeDtypeStruct(o_cat.shape, F32), jax.ShapeDtypeStruct((S, DIL_WIDTH), F32)],
        input_output_aliases={6: 0}, compiler_params=_params())(*outs, *lses, o_cat)


def _shift_down(u, n):
    t = lax.broadcasted_iota(jnp.int32, u.shape, 0)
    return jnp.where(t >= n, pltpu.roll(u, n, axis=0), 0.0)


def _shift_up(u, n):
    rows = u.shape[0]
    t = lax.broadcasted_iota(jnp.int32, u.shape, 0)
    return jnp.where(t < rows - n, pltpu.roll(u, rows - n, axis=0), 0.0)


def _conv(u, w, b):
    return w[2:3, :] * u + w[1:2, :] * _shift_down(u, 1) + w[0:1, :] * _shift_down(u, 2) + b


CONV_TC = 256
CONV_NB = D_FF // CONV_TC


def paired_block(j):
    return (j % 2) * CONV_NB + j // 2


def natural_block(j):
    return jnp.where(j < CONV_NB, 2 * j, 2 * (j - CONV_NB) + 1)


def _pair_spec(rows):
    return pl.BlockSpec((rows, 2 * CONV_TC), lambda j: (0, j))


def _half_specs(rows, rows_axis=False):
    if rows_axis:
        return [pl.BlockSpec((rows, D_MODEL), lambda j: (j, 0)), pl.BlockSpec((rows, D_MODEL), lambda j: (j + CONV_NB, 0))]
    return [pl.BlockSpec((rows, CONV_TC), lambda j: (0, j)), pl.BlockSpec((rows, CONV_TC), lambda j: (0, j + CONV_NB))]


def pair_dgrad(name, dup, w_t, dep=None):
    S, D = dup.shape[0], w_t.shape[1]
    deps = [] if dep is None else [dep]

    def body(a_ref, bg_ref, bv_ref, *rest):
        o_ref = rest[-1]
        p = _dot(a_ref[:, :CONV_TC], bg_ref[...]) + _dot(a_ref[:, CONV_TC:], bv_ref[...])
        j = pl.program_id(0)

        @pl.when(j == 0)
        def _():
            o_ref[...] = p

        @pl.when(j > 0)
        def _():
            o_ref[...] += p

    return pl.pallas_call(
        body, name=name, grid=(CONV_NB,),
        in_specs=[_pair_spec(S), pl.BlockSpec((CONV_TC, D), lambda j: (j, 0)), pl.BlockSpec((CONV_TC, D), lambda j: (j + CONV_NB, 0))]
        + [pl.BlockSpec(memory_space=pl.ANY)] * len(deps),
        out_specs=pl.BlockSpec((S, D), lambda j: (0, 0)), out_shape=jax.ShapeDtypeStruct((S, D), F32),
        compiler_params=_params())(dup, w_t, w_t, *deps)


def _whole(a):
    return pl.BlockSpec(a.shape, lambda j: (0,) * a.ndim)


def _up_pair(h_ref, ug_ref, uv_ref):
    h = h_ref[...]
    return jnp.concatenate([_dot(h, ug_ref[...], "nt"), _dot(h, uv_ref[...], "nt")], axis=1)


def conv_glu_fwd(name, h, w_up_t, w_conv, b_conv):
    S = h.shape[0]

    def body(h_ref, ug_ref, uv_ref, wg_ref, wv_ref, bg_ref, bv_ref, act_ref):
        w = jnp.concatenate([wg_ref[...], wv_ref[...]], axis=1)
        u = _conv(_up_pair(h_ref, ug_ref, uv_ref), w, jnp.concatenate([bg_ref[...], bv_ref[...]], axis=1))
        act_ref[...] = (_silu(u[:, :CONV_TC]) * u[:, CONV_TC:]).astype(act_ref.dtype)

    return pl.pallas_call(
        body, name=name, grid=(CONV_NB,),
        in_specs=[_whole(h)] + _half_specs(CONV_TC, rows_axis=True) + _half_specs(3) + _half_specs(1),
        out_specs=pl.BlockSpec((S, CONV_TC), lambda j: (0, j)), out_shape=jax.ShapeDtypeStruct((S, D_FF), MXU_DTYPE),
        compiler_params=_params())(h, w_up_t, w_up_t, w_conv, w_conv, b_conv, b_conv)


def conv_glu_bwd(name, h, w_up_t, w_conv, b_conv, d_dn, w_down):
    S = h.shape[0]

    def body(h_ref, ug_ref, uv_ref, wg_ref, wv_ref, bg_ref, bv_ref, dd_ref, wd_ref, dup_ref, dwg_ref, dwv_ref, dbg_ref, dbv_ref):
        uin = _up_pair(h_ref, ug_ref, uv_ref)
        w = jnp.concatenate([wg_ref[...], wv_ref[...]], axis=1)
        u1, u2 = _shift_down(uin, 1), _shift_down(uin, 2)
        u = w[2:3, :] * uin + w[1:2, :] * u1 + w[0:1, :] * u2 + jnp.concatenate([bg_ref[...], bv_ref[...]], axis=1)
        gate, val, da = u[:, :CONV_TC], u[:, CONV_TC:], _dot(dd_ref[...], wd_ref[...], "nt")
        sig = 1.0 / (1.0 + jnp.exp(-gate))
        du = jnp.concatenate([da * val * (sig * (1.0 + gate * (1.0 - sig))), da * (gate * sig)], axis=1)
        dup_ref[...] = (w[2:3, :] * du + w[1:2, :] * _shift_up(du, 1) + w[0:1, :] * _shift_up(du, 2)).astype(dup_ref.dtype)
        dw = jnp.concatenate([_colsum(du * u2), _colsum(du * u1), _colsum(du * uin)], axis=0)
        db = _colsum(du)
        dwg_ref[...], dwv_ref[...] = dw[:, :CONV_TC], dw[:, CONV_TC:]
        dbg_ref[...], dbv_ref[...] = db[:, :CONV_TC], db[:, CONV_TC:]

    half = lambda rows: pl.BlockSpec((rows, CONV_TC), lambda j: (0, j))
    dup, dwg, dwv, dbg, dbv = pl.pallas_call(
        body, name=name, grid=(CONV_NB,),
        in_specs=[_whole(h)] + _half_specs(CONV_TC, rows_axis=True) + _half_specs(3) + _half_specs(1)
        + [_whole(d_dn), pl.BlockSpec((CONV_TC, w_down.shape[1]), lambda j: (j, 0))],
        out_specs=[_pair_spec(S), half(3), half(3), half(1), half(1)],
        out_shape=[jax.ShapeDtypeStruct((S, 2 * D_FF), MXU_DTYPE)] + [jax.ShapeDtypeStruct((3, D_FF), F32)] * 2
        + [jax.ShapeDtypeStruct((1, D_FF), F32)] * 2,
        compiler_params=_params())(h, w_up_t, w_up_t, w_conv, w_conv, b_conv, b_conv, d_dn, w_down)
    return dup, jnp.concatenate([dwg, dwv], axis=1), jnp.concatenate([dbg, dbv], axis=1)


def adamw(name, w, parts, m, v, tr=None):
    R, C = w.shape
    tr = tr or R
    assert R % tr == 0
    c1 = 1.0 - ADAM_B1 ** ADAM_STEP
    c2 = 1.0 - ADAM_B2 ** ADAM_STEP
    np_ = len(parts)

    def body(*refs):
        w_ref, m_ref, v_ref = refs[0], refs[1 + np_], refs[2 + np_]
        go_ref, d_ref, mo_ref, vo_ref = refs[3 + np_:]
        terms = []
        for part, ref in zip(parts, refs[1:1 + np_], strict=True):
            terms += [ref[...]] if part.ndim == 2 else [ref[p] for p in range(part.shape[0])]
        g = terms[0].astype(F32)
        for term in terms[1:]:
            g = g + term.astype(F32)
        m2 = ADAM_B1 * m_ref[...] + (1.0 - ADAM_B1) * g
        v2 = ADAM_B2 * v_ref[...] + (1.0 - ADAM_B2) * (g * g)
        go_ref[...] = g
        mo_ref[...] = m2
        vo_ref[...] = v2
        d_ref[...] = -ADAM_LR * ((m2 / c1) / (jnp.sqrt(v2 / c2) + ADAM_EPS) + ADAM_WD * w_ref[...])

    blk = pl.BlockSpec((tr, C), lambda i: (i, 0))
    part_specs = [blk if p.ndim == 2 else pl.BlockSpec((p.shape[0], tr, C), lambda i: (0, i, 0)) for p in parts]
    return pl.pallas_call(
        body, name=name, grid=(R // tr,),
        in_specs=[blk] + part_specs + [blk, blk], out_specs=[blk] * 4,
        out_shape=[jax.ShapeDtypeStruct((R, C), F32)] * 4, compiler_params=_params())(w, *parts, m, v)


def _place():
    return lax.axis_index("x"), lax.axis_index("y"), lax.axis_index("c")


def all_gather(name, arrs, after=None):
    n = len(arrs)
    deps = [] if after is None else [after]

    def body(*refs):
        ins, outs = refs[:n], refs[n + len(deps):2 * n + len(deps)]
        send_sems, recv_sems, local_sems = refs[2 * n + len(deps):]
        x, y, c = _place()
        me, sibling = (x, y, c), (x, y, 1 - c)
        chips = [(1 - x, y), (x, 1 - y), (1 - x, 1 - y)]
        sends = []
        for t in range(n):
            out = outs[t]

            def slot(px, py, pc, out=out):
                return out.at[4 * px + 2 * py + pc]

            def copy(k, block, to, src=None, t=t, slot=slot):
                return pltpu.make_async_remote_copy(
                    src_ref=slot(*block) if src is None else src, dst_ref=slot(*block),
                    send_sem=send_sems.at[7 * t + k], recv_sem=recv_sems.at[7 * t + k],
                    device_id=to, device_id_type=MESH_ID)

            mine = pltpu.make_async_copy(ins[t], slot(*me), local_sems.at[t])
            mine.start()
            first = [copy(0, me, sibling, src=ins[t])]
            first += [copy(1 + j, me, (*chip, c), src=ins[t]) for j, chip in enumerate(chips)]
            for cp in first:
                cp.start()
            sends.append((mine, first, copy))
        for t in range(n):
            mine, first, copy = sends[t]
            passed = [copy(4 + j, (*chip, c), sibling) for j, chip in enumerate(chips)]
            for j, chip in enumerate(chips):
                copy(1 + j, (*chip, c), me).wait_recv()
                passed[j].start()
            copy(0, sibling, me).wait_recv()
            for j, chip in enumerate(chips):
                copy(4 + j, (*chip, 1 - c), me).wait_recv()
            for cp in first + passed:
                cp.wait_send()
            mine.wait()

    any_spec = pl.BlockSpec(memory_space=pl.ANY)
    res = pl.pallas_call(
        body, name=name, in_specs=[any_spec] * (n + len(deps)), out_specs=[any_spec] * n,
        out_shape=[jax.ShapeDtypeStruct((N_DEV,) + a.shape, a.dtype) for a in arrs],
        scratch_shapes=[pltpu.SemaphoreType.DMA((7 * n,)), pltpu.SemaphoreType.DMA((7 * n,)), pltpu.SemaphoreType.DMA((n,))],
        compiler_params=pltpu.CompilerParams(has_side_effects=True))(*arrs, *deps)
    return list(res)


HBM_SPEC = pl.BlockSpec(memory_space=pltpu.HBM)
SEM_SPEC = pl.BlockSpec(memory_space=pltpu.SEMAPHORE)
DATAFLOW = pltpu.SideEffectType.DATAFLOW_SIDE_EFFECTING


def _exchange_copies(srcs, lands, send_sems, recv_sems, gather):
    x, y, c = _place()
    me = 4 * x + 2 * y + c
    out = []
    for t, (src, land) in enumerate(zip(srcs, lands, strict=True)):
        for k in range(1, N_DEV):
            px, py, pc = x ^ (k >> 2), y ^ ((k >> 1) & 1), c ^ (k & 1)
            out.append(pltpu.make_async_remote_copy(
                src_ref=src if gather else src.at[4 * px + 2 * py + pc],
                dst_ref=land.at[me] if gather else land.at[k - 1],
                send_sem=send_sems.at[7 * t + k - 1], recv_sem=recv_sems.at[7 * t + k - 1],
                device_id=(px, py, pc), device_id_type=MESH_ID))
    return out


def exchange_start(name, arrs, gather, after=None):
    n = len(arrs)
    lands = [lax.empty(((N_DEV,) + a.shape) if gather else ((N_DEV - 1,) + a.shape[1:]), a.dtype) for a in arrs]
    deps = [] if after is None else [after]

    def body(*refs):
        srcs, land_refs = refs[:n], refs[n:2 * n]
        send_sems, recv_sems = refs[2 * n + len(deps)], refs[2 * n + len(deps) + 1]
        token = refs[-1]
        for cp in _exchange_copies(srcs, land_refs, send_sems, recv_sems, gather):
            cp.start()
        token[...] = jnp.zeros_like(token)

    hbm = lambda a: pltpu.HBM(a.shape, a.dtype)
    res = pl.pallas_call(
        body, name=name,
        out_shape=(pltpu.SemaphoreType.DMA((7 * n,)), pltpu.SemaphoreType.DMA((7 * n,)), *[hbm(a) for a in arrs],
                   *[hbm(l) for l in lands], jax.ShapeDtypeStruct((8, 128), F32)),
        in_specs=[HBM_SPEC] * (2 * n) + [pl.BlockSpec(memory_space=pl.ANY)] * len(deps),
        out_specs=(SEM_SPEC, SEM_SPEC, *[HBM_SPEC] * (2 * n), pl.BlockSpec(memory_space=pltpu.VMEM)),
        input_output_aliases={i: 2 + i for i in range(2 * n)},
        compiler_params=pltpu.CompilerParams(has_side_effects=DATAFLOW),
    )(*[pltpu.with_memory_space_constraint(a, pltpu.HBM) for a in arrs + lands], *deps)
    return res[0], res[1], list(res[2:2 + n]), list(res[2 + n:2 + 2 * n]), res[-1]


def exchange_wait(name, started, gather, after):
    send_sems, recv_sems, srcs, lands, _ = started
    n = len(srcs)

    def body(*refs):
        src_refs, land_refs = refs[:n], refs[n:2 * n]
        copies = _exchange_copies(src_refs, land_refs, refs[2 * n], refs[2 * n + 1], gather)
        for cp in copies:
            cp.wait_send()
        for cp in copies:
            cp.wait_recv()

    hbm = lambda a: pltpu.HBM(a.shape, a.dtype)
    res = pl.pallas_call(
        body, name=name, out_shape=tuple(hbm(a) for a in srcs + lands),
        in_specs=[HBM_SPEC] * (2 * n) + [SEM_SPEC, SEM_SPEC, pl.BlockSpec(memory_space=pl.ANY)],
        out_specs=tuple([HBM_SPEC] * (2 * n)), input_output_aliases={i: i for i in range(2 * n)},
        compiler_params=pltpu.CompilerParams(has_side_effects=DATAFLOW),
    )(*srcs, *lands, send_sems, recv_sems, after)
    return list(res[:n]), list(res[n:])


def _gather_cols(stack):
    p, k, n = stack.shape
    return stack.transpose(1, 0, 2).reshape(k, p * n)


def _scatter_cols(full):
    k, n = full.shape
    return full.reshape(k, N_DEV, n // N_DEV).transpose(1, 0, 2)


def _gather_rows(stack):
    p, r, n = stack.shape
    return stack.reshape(p * r, n)


def _scatter_rows(full):
    r, n = full.shape
    return full.reshape(N_DEV, r // N_DEV, n)


_IN_NAT = Q_LORA + KV_LORA
TRANSPOSED = ("w_in", "w_q_b", "w_up")


def to_kernel_layout(name, w):
    if name == "w_in":
        z = lambda n: jnp.zeros((n, w.shape[1]), w.dtype)
        return jnp.concatenate([w[:_IN_NAT], z(KPE_LO), w[_IN_NAT:_IN_NAT + ROPE], z(LANES - KPE_LO - ROPE), w[_IN_NAT + ROPE:]], axis=0)
    if name == "w_q_b":
        return jnp.pad(w.reshape(HEADS, NOPE + ROPE, -1), ((0, 0), (0, LANES - NOPE - ROPE), (0, 0))).reshape(HEADS * LANES, -1)
    if name == "w_o":
        mla = jnp.pad(w[:HEADS * NOPE].reshape(HEADS, NOPE, -1), ((0, 0), (LANES - NOPE, 0), (0, 0))).reshape(HEADS * LANES, -1)
        return jnp.concatenate([mla, w[HEADS * NOPE:]], axis=0)
    return w


def from_kernel_layout(name, g):
    if name == "w_in":
        return jnp.concatenate([g[:_IN_NAT], g[P_KPE + KPE_LO:P_KPE + KPE_LO + ROPE], g[P_QD:]], axis=0)
    if name == "w_q_b":
        return g.reshape(HEADS, LANES, -1)[:, :NOPE + ROPE, :].reshape(HEADS * (NOPE + ROPE), -1)
    if name == "w_o":
        mla = g[:HEADS * LANES].reshape(HEADS, LANES, -1)[:, LANES - NOPE:, :].reshape(HEADS * NOPE, -1)
        return jnp.concatenate([mla, g[HEADS * LANES:]], axis=0)
    return g


SMALL_COLS = 1024
SMALL_ROWS = 24
SMALL_AT = {"loss": (0, 0, 1), "b_ada": (1, 0, 6 * D_MODEL), "g_mix_norm": (7, 0, D_MODEL), "g_q_lat": (8, 0, Q_LORA),
            "g_kv_lat": (9, 0, KV_LORA), "g_mla_q_nope": (10, 0, NOPE), "g_mla_q_pe": (10, 128, ROPE),
            "g_mla_k_nope": (10, 256, NOPE), "g_mla_k_pe": (10, 384, ROPE), "g_dil_q": (10, 512, DIL_DIM),
            "g_dil_k": (10, 640, DIL_DIM), "g_ffn_norm": (11, 0, D_MODEL), "b_conv": (12, 0, 2 * D_FF)}
SMALL_PARAMS = tuple(n for n in SMALL_AT if n != "loss")


def _pack_small(values):
    by_row = {}
    for name, (row, off, n) in SMALL_AT.items():
        by_row.setdefault(row, []).append((off, values[name].reshape(-1).astype(F32)))
    out = []
    for row in sorted(by_row):
        pieces, at = [], 0
        for off, v in sorted(by_row[row], key=lambda t: t[0]):
            pieces += [jnp.zeros((off - at,), F32), v]
            at = off + v.shape[0]
        flat = jnp.concatenate(pieces)
        nrows = -(-flat.shape[0] // SMALL_COLS)
        out.append(jnp.pad(flat, (0, nrows * SMALL_COLS - flat.shape[0])).reshape(nrows, SMALL_COLS))
    packed = jnp.concatenate(out, axis=0)
    return jnp.pad(packed, ((0, SMALL_ROWS - packed.shape[0]), (0, 0)))


def _adam(w, g, m, v):
    c1 = 1.0 - ADAM_B1 ** ADAM_STEP
    c2 = 1.0 - ADAM_B2 ** ADAM_STEP
    m2 = ADAM_B1 * m + (1.0 - ADAM_B1) * g
    v2 = ADAM_B2 * v + (1.0 - ADAM_B2) * (g * g)
    return -ADAM_LR * ((m2 / c1) / (jnp.sqrt(v2 / c2) + ADAM_EPS) + ADAM_WD * w), m2, v2


def adamw_small(name, stack, params):
    flat = [a for n in SMALL_PARAMS for a in params[n]]

    def body(stack_ref, *refs):
        ins, outs = refs[:len(flat)], refs[len(flat):]
        g_all = stack_ref[0]
        for p in range(1, N_DEV):
            g_all = g_all + stack_ref[p]
        outs[0][...] = g_all[0:1, 0:1]
        for i, pname in enumerate(SMALL_PARAMS):
            row, off, n = SMALL_AT[pname]
            w_ref, m_ref, v_ref = ins[3 * i:3 * i + 3]
            go_ref, d_ref, mo_ref, vo_ref = outs[1 + 4 * i:5 + 4 * i]
            for c0 in range(0, n, SMALL_COLS):
                cn = min(SMALL_COLS, n - c0)
                r = row + c0 // SMALL_COLS
                g = g_all[r:r + 1, off:off + cn]
                cols = (slice(None), slice(c0, c0 + cn))
                d, m2, v2 = _adam(w_ref[cols], g, m_ref[cols], v_ref[cols])
                go_ref[cols], d_ref[cols], mo_ref[cols], vo_ref[cols] = g, d, m2, v2

    whole = lambda a: pl.BlockSpec(a.shape, lambda: (0,) * a.ndim)
    out_shape = [jax.ShapeDtypeStruct((1, 1), F32)] + [jax.ShapeDtypeStruct(a.shape, F32) for n in SMALL_PARAMS for a in params[n][:1] * 4]
    res = pl.pallas_call(body, name=name, in_specs=[whole(stack)] + [whole(a) for a in flat],
                         out_specs=[pl.BlockSpec(s.shape, lambda s=s: (0,) * len(s.shape)) for s in out_shape],
                         out_shape=out_shape, compiler_params=_params())(stack, *flat)
    return res[0], {n: res[1 + 4 * i:5 + 4 * i] for i, n in enumerate(SMALL_PARAMS)}


def _local_step(x, pos, mod, target, w, fetch, emit):
    S = SEQ
    sh1, sc1, g1, sh2, sc2, g2 = [mod[:, i * D_MODEL:(i + 1) * D_MODEL] for i in range(6)]
    zeros = lambda n: jnp.zeros((1, n), F32)
    g_q = jnp.concatenate([w["g_mla_q_nope"], w["g_mla_q_pe"], zeros(LANES - NOPE - ROPE)], axis=1)
    g_k = jnp.concatenate([w["g_mla_k_nope"], zeros(LANES - NOPE)], axis=1)
    g_kpe = jnp.concatenate([zeros(KPE_LO), w["g_mla_k_pe"], zeros(LANES - KPE_LO - ROPE)], axis=1)
    g_dq = jnp.concatenate([w["g_dil_q"]] * 2, axis=1)
    g_dk = jnp.concatenate([w["g_dil_k"]] * 2, axis=1)
    b_conv = w["b_conv"]

    def inv_freq(d):
        return jnp.power(ROPE_THETA, -2.0 * jnp.arange(d // 2, dtype=F32) / d)

    f_mla = jnp.concatenate([jnp.zeros((KPE_LO,), F32), inv_freq(ROPE), inv_freq(ROPE), jnp.zeros((LANES - KPE_LO - ROPE,), F32)])
    f_dil = jnp.concatenate([inv_freq(DIL_DIM)] * 4)

    def tables_fn(rows, params):
        (p,), (fa, fb) = rows, params
        return [jnp.cos(p * fa), jnp.sin(p * fa), jnp.cos(p * fb), jnp.sin(p * fb)], []

    cos_m, sin_m, cos_d, sin_d = rowwise("rope_tables", tables_fn, [pos], [f_mla.reshape(1, LANES), f_dil.reshape(1, LANES)],
                                         [(LANES, F32)] * 4)
    tables = [cos_m, sin_m, cos_d, sin_d]
    H_M, H_D = ROPE // 2, DIL_DIM // 2

    def ln1_fn(rows, params):
        (xv,), (g, sc, sh) = rows, params
        y, _, _ = _rms(xv, g)
        return [y * (1.0 + sc) + sh], []

    (h,) = rowwise("ln1_fwd", ln1_fn, [x], [w["g_mix_norm"], sc1, sh1], [(D_MODEL, MXU_DTYPE)])
    w_in = fetch("w_in", h)
    proj = matmul("proj_fwd", h, w_in, "nt", tm=512)

    def post_fn(rows, params):
        (pv, cm, sm, cd, sd), (gq, gkv, gkp, gdq, gdk) = rows, params
        kper = _rope(_grms(pv[:, P_KPE:P_QD], gkp, KPE_GROUPS)[0], cm, sm, H_M)
        qd = [_rope(_grms(c, gdq, DIL_GROUPS)[0], cd, sd, H_D) for c in _chunks(pv[:, P_QD:P_KD])]
        kd = [_rope(_grms(c, gdk, DIL_GROUPS)[0], cd, sd, H_D) for c in _chunks(pv[:, P_KD:P_VD])]
        return [_rms(pv[:, P_QLAT:P_KVLAT], gq)[0], _rms(pv[:, P_KVLAT:P_KPE], gkv)[0], kper,
                jnp.concatenate(qd, axis=1), jnp.concatenate(kd, axis=1)], []

    post_params = [w["g_q_lat"], w["g_kv_lat"], g_kpe, g_dq, g_dk]
    qln, kvn, kper, qd_r, kd_r = rowwise(
        "proj_post", post_fn, [proj] + tables, post_params,
        [(Q_LORA, MXU_DTYPE), (KV_LORA, MXU_DTYPE), (LANES, MXU_DTYPE)] + [(DIL_WIDTH, F32)] * 2, tm=256)
    w_q_b, w_kv_b = fetch("w_q_b", qln), fetch("w_kv_b", kvn)
    q = matmul("q_fwd", qln, w_q_b, "nt", tm=1024)
    kv = matmul("kv_fwd", kvn, w_kv_b, "nn", tm=1024)

    def mla_prep_fn(rows, params):
        (qv, kvv, kp, cm, sm), (gq, gk) = rows, params
        value_lanes = _lane(kp.shape) >= NOPE
        qs, ks, vs = [], [], []
        for qc, kc in zip(_chunks(qv), _chunks(kvv), strict=True):
            qs.append(_rope(_grms(qc, gq, Q_GROUPS)[0], cm, sm, H_M))
            ks.append(_grms(kc, gk, K_GROUPS)[0] + kp)
            vs.append(jnp.where(value_lanes, kc, 0.0))
        return [jnp.concatenate(t, axis=1) for t in (qs, ks, vs)], []

    q_mla, k_mla, v_mla = rowwise("mla_prep", mla_prep_fn, [q, kv, kper, cos_m, sin_m], [g_q, g_k],
                                  [(HEADS * LANES, MXU_DTYPE)] * 3, tm=256)
    mla_scale = (NOPE + ROPE) ** -0.5
    o_cat, lse_mla = mla_fwd("mla_fwd", q_mla, k_mla, v_mla, mla_scale)

    band = [band_fwd(f"band{dil}_fwd", qd_r, kd_r, proj, dil) for dil in DILATIONS]
    o_cat, lse_mix = combine_fwd("dil_combine", [b[0] for b in band], [b[1] for b in band], o_cat)
    w_o = fetch("w_o", o_cat)
    mix = matmul("mix_fwd", o_cat, w_o, "nn", tm=512)

    def mid_fn(rows, params):
        (xv, mx), (gate1, g, sc, sh) = rows, params
        x1 = xv + gate1 * mx
        y, _, _ = _rms(x1, g)
        return [x1, y * (1.0 + sc) + sh], []

    x1, h2 = rowwise("mid_fwd", mid_fn, [x, mix], [g1, w["g_ffn_norm"], sc2, sh2], [(D_MODEL, F32), (D_MODEL, MXU_DTYPE)])
    w_up, w_conv, w_down = fetch("w_up", h2), fetch("w_conv", h2), fetch("w_down", h2)
    act = conv_glu_fwd("conv_fwd", h2, w_up, w_conv, b_conv)
    dn = matmul("down_fwd", act, w_down, "nn", tm=512)

    def final_fn(rows, params):
        (x1v, dnv, tgt), (gate2,) = rows, params
        r = x1v + gate2 * dnv - tgt
        dy = r * (1.0 / D_MODEL)
        loss = jnp.sum(_colsum(r * r), axis=-1, keepdims=True) * (0.5 / D_MODEL)
        return [dy, gate2 * dy], [loss, _colsum(dy * dnv)]

    dy, d_dn, loss, dg2 = rowwise("loss_head", final_fn, [x1, dn, target], [g2], [(D_MODEL, F32), (D_MODEL, MXU_DTYPE)],
                                  [1, D_MODEL])
    emit("w_down", matmul("down_wgrad", act, d_dn, "tn", tm=1408, out_dtype=MXU_DTYPE))
    dup, g_w_conv, g_b_conv = conv_glu_bwd("conv_bwd", h2, w_up, w_conv, b_conv, d_dn, w_down)
    emit("w_conv", g_w_conv)
    sent = emit("w_up", matmul("up_wgrad", dup, h2, "tn", tm=CONV_TC, out_dtype=MXU_DTYPE, a_mmap=natural_block))
    dh2 = pair_dgrad("up_dgrad", dup, w_up, dep=sent)

    def mid_bwd_fn(rows, params):
        (dh2v, dyv, x1v, mx), (gate1, g, sc) = rows, params
        yn, n, rstd = _rms(x1v, g)
        dx_n, dg = _rms_bwd(dh2v * (1.0 + sc), n, rstd, g)
        dx1 = dyv + dx_n
        return [dx1, gate1 * dx1], [dg, _colsum(dh2v * yn), _colsum(dh2v), _colsum(dx1 * mx)]

    dx1, dmix, dg_ffn, dsc2, dsh2, dg1 = rowwise(
        "mid_bwd", mid_bwd_fn, [dh2, dy, x1, mix], [g1, w["g_ffn_norm"], sc2], [(D_MODEL, F32), (D_MODEL, MXU_DTYPE)],
        [D_MODEL] * 4)

    sent = emit("w_o", matmul("mix_wgrad", o_cat, dmix, "tn", tm=512, out_dtype=MXU_DTYPE))
    do_cat = matmul("mix_dgrad", dmix, w_o, "nt", tm=512, dep=sent)
    dband = [band_bwd(f"band{dil}_bwd", qd_r, kd_r, proj, b[1], lse_mix, o_cat, do_cat, dil) for dil, b in zip(DILATIONS, band)]
    dq_mla, dkv_mla, dkper = mla_bwd("mla_bwd", q_mla, k_mla, v_mla, o_cat, do_cat, lse_mla, mla_scale)

    def mla_prep_bwd_fn(rows, params):
        (dqv, dkvv, qv, kvv, cm, sm), (gq, gk) = rows, params
        nope_lanes = _lane(cm.shape) < NOPE
        dqs, dkvs, dgq, dgk = [], [], 0.0, 0.0
        for dqc, dkc, qc, kc in zip(_chunks(dqv), _chunks(dkvv), _chunks(qv), _chunks(kvv), strict=True):
            _, n, rstd = _grms(qc, gq, Q_GROUPS)
            dx, dg = _grms_bwd(_rope_bwd(dqc, cm, sm, H_M), n, rstd, gq, Q_GROUPS)
            dqs.append(dx)
            dgq = dgq + dg
            _, n, rstd = _grms(kc, gk, K_GROUPS)
            dx, dg = _grms_bwd(dkc, n, rstd, gk, K_GROUPS)
            dkvs.append(jnp.where(nope_lanes, dx, dkc))
            dgk = dgk + dg
        return [jnp.concatenate(dqs, axis=1), jnp.concatenate(dkvs, axis=1)], [dgq, dgk]

    dq, dkv, dg_q, dg_k = rowwise("mla_prep_bwd", mla_prep_bwd_fn, [dq_mla, dkv_mla, q, kv, cos_m, sin_m], [g_q, g_k],
                                  [(HEADS * LANES, MXU_DTYPE)] * 2, [LANES, LANES], tm=256)
    emit("w_q_b", matmul("q_wgrad", dq, qln, "tn", out_dtype=MXU_DTYPE))
    emit("w_kv_b", matmul("kv_wgrad", kvn, dkv, "tn", out_dtype=MXU_DTYPE))
    dqln = matmul("q_dgrad", dq, w_q_b, "nn", tm=1024)
    dkvn = matmul("kv_dgrad", dkv, w_kv_b, "nt", tm=1024)

    def pre_bwd_fn(rows, params):
        dql, dkvl, dkp = rows[0:3]
        dqd_, dkd_, dvd_ = [rows[3 + 3 * i] + rows[4 + 3 * i] + rows[5 + 3 * i] for i in range(3)]
        pv, cm, sm, cd, sd = rows[12:]
        gq, gkv, gkp, gdq, gdk = params
        r_q = _norm_bwd(dql, pv[:, P_QLAT:P_KVLAT], gq)
        r_kv = _norm_bwd(dkvl, pv[:, P_KVLAT:P_KPE], gkv)
        _, n, rstd = _grms(pv[:, P_KPE:P_QD], gkp, KPE_GROUPS)
        r_kp = _grms_bwd(_rope_bwd(dkp, cm, sm, H_M), n, rstd, gkp, KPE_GROUPS)
        outs, dgs = [r_q[0], r_kv[0], r_kp[0]], []
        for dval, lo, g in ((dqd_, P_QD, gdq), (dkd_, P_KD, gdk)):
            dg_sum = 0.0
            for dc, xc in zip(_chunks(dval), _chunks(pv[:, lo:lo + DIL_WIDTH]), strict=True):
                _, n, rstd = _grms(xc, g, DIL_GROUPS)
                dx, dg = _grms_bwd(_rope_bwd(dc, cd, sd, H_D), n, rstd, g, DIL_GROUPS)
                outs.append(dx)
                dg_sum = dg_sum + dg
            dgs.append(dg_sum)
        return [jnp.concatenate(outs + [dvd_], axis=1)], [r_q[1], r_kv[1], r_kp[1]] + dgs

    dproj, dg_q_lat, dg_kv_lat, dg_kpe, dg_dq, dg_dk = rowwise(
        "proj_pre_bwd", pre_bwd_fn,
        [dqln, dkvn, dkper] + [d[i] for i in range(3) for d in dband] + [proj] + tables, post_params,
        [(P_END, MXU_DTYPE)], [Q_LORA, KV_LORA, LANES, LANES, LANES], tm=256)
    sent = emit("w_in", matmul("proj_wgrad", dproj, h, "tn", tn=512, out_dtype=MXU_DTYPE))
    dh = matmul("proj_dgrad", dproj, w_in, "nn", tm=512, dep=sent)

    def ln1_bwd_fn(rows, params):
        (dhv, dres, xv), (g, sc) = rows, params
        yn, n, rstd = _rms(xv, g)
        dx_n, dg = _rms_bwd(dhv * (1.0 + sc), n, rstd, g)
        return [dres + dx_n], [dg, _colsum(dhv * yn), _colsum(dhv)]

    grad_x, dg_mix, dsc1, dsh1 = rowwise("ln1_bwd", ln1_bwd_fn, [dh, dx1, x], [w["g_mix_norm"], sc1], [(D_MODEL, F32)],
                                         [D_MODEL] * 3)
    dmod = jnp.concatenate([dsh1, dsc1, dg1, dsh2, dsc2, dg2], axis=-1)
    small = {"loss": loss, "b_ada": dmod, "g_mix_norm": dg_mix, "g_q_lat": dg_q_lat, "g_kv_lat": dg_kv_lat,
             "g_mla_q_nope": dg_q[:, :NOPE], "g_mla_q_pe": dg_q[:, NOPE:NOPE + ROPE], "g_mla_k_nope": dg_k[:, :NOPE],
             "g_mla_k_pe": dg_kpe[:, KPE_LO:KPE_LO + ROPE], "g_dil_q": dg_dq[:, :DIL_DIM] + dg_dq[:, DIL_DIM:],
             "g_dil_k": dg_dk[:, :DIL_DIM] + dg_dk[:, DIL_DIM:], "g_ffn_norm": dg_ffn,
             "b_conv": g_b_conv}
    return grad_x, small


COL_SHARDED = ("w_kv_b", "w_conv")
ROW_SHARDED = ("w_o", "w_down") + TRANSPOSED
ADAM_TILE = {"w_ada": 256, "w_up": 176, "w_down": 176}
GATHER_GROUPS = (("w_in",), ("w_q_b", "w_kv_b"), ("w_o", "w_up", "w_conv", "w_down"))
SCATTER_GROUPS = (("w_down", "w_conv", "w_up"), ("w_o",), ("w_q_b", "w_kv_b", "w_in"))
OUT_WEIGHTS = ("w_ada", "b_ada", "g_mix_norm", "w_in", "g_q_lat", "w_q_b", "g_kv_lat", "w_kv_b", "g_mla_q_nope", "g_mla_q_pe",
               "g_mla_k_nope", "g_mla_k_pe", "g_dil_q", "g_dil_k", "w_o", "g_ffn_norm", "w_up", "w_conv", "b_conv", "w_down")


def kernel(x, c, positions, w_ada, b_ada, g_mix_norm, w_in, g_q_lat, w_q_b, g_kv_lat, w_kv_b, g_mla_q_nope, g_mla_q_pe, g_mla_k_nope, g_mla_k_pe, g_dil_q, g_dil_k, w_o, g_ffn_norm, w_up, w_conv, b_conv, w_down, loss_target, m_w_ada, m_b_ada, m_g_mix_norm, m_w_in, m_g_q_lat, m_w_q_b, m_g_kv_lat, m_w_kv_b, m_g_mla_q_nope, m_g_mla_q_pe, m_g_mla_k_nope, m_g_mla_k_pe, m_g_dil_q, m_g_dil_k, m_w_o, m_g_ffn_norm, m_w_up, m_w_conv, m_b_conv, m_w_down, v_w_ada, v_b_ada, v_g_mix_norm, v_w_in, v_g_q_lat, v_w_q_b, v_g_kv_lat, v_w_kv_b, v_g_mla_q_nope, v_g_mla_q_pe, v_g_mla_k_nope, v_g_mla_k_pe, v_g_dil_q, v_g_dil_k, v_w_o, v_g_ffn_norm, v_w_up, v_w_conv, v_b_conv, v_w_down):
    args = dict(locals())
    xi, yi, ci = _place()
    me = 4 * xi + 2 * yi + ci
    def local(prefix, n):
        a = args[prefix + n][0]
        return a.T if n in TRANSPOSED else a

    shard = {n: local("", n) for n in COL_SHARDED + ROW_SHARDED + ("w_ada",)}
    small_w = {n: args[n] for n in SMALL_PARAMS}

    (c_all,) = all_gather("gather_c", [c])
    (sc_all,) = rowwise("silu_c", lambda rows, params: ([_silu(rows[0])], []), [c_all.reshape(N_DEV, D_MODEL)], [],
                        [(D_MODEL, MXU_DTYPE)])
    mod_part = matmul("ada_fwd", sc_all, shard["w_ada"], "nn")
    (mod_all,) = all_gather("gather_mod", [mod_part])

    payload = {n: shard[n] if n == "w_conv" else shard[n].astype(MXU_DTYPE) for n in COL_SHARDED + ROW_SHARDED}
    gathers, after_start = [], mod_all
    for i, grp in enumerate(GATHER_GROUPS):
        gathers.append(exchange_start(f"gather{i}_start", [payload[n] for n in grp], gather=True, after=after_start))
        after_start = gathers[-1][-1]
    full = {}

    def fetch(name, after):
        if name not in full:
            (i, grp), = [(i, grp) for i, grp in enumerate(GATHER_GROUPS) if name in grp]
            srcs, lands = exchange_wait(f"gather{i}_wait", gathers[i], True, after)
            for n, src, land in zip(grp, srcs, lands, strict=True):
                stack = lax.dynamic_update_index_in_dim(land, src, me, 0)
                full[n] = to_kernel_layout(n, _gather_cols(stack) if n in COL_SHARDED else _gather_rows(stack))
        return full[name]

    mod_row = lax.dynamic_index_in_dim(mod_all, me, axis=1, keepdims=False).reshape(1, 6 * D_MODEL)
    (mod,) = rowwise("ada_bias", lambda rows, params: ([rows[0] + rows[1]], []), [mod_row, b_ada], [], [(6 * D_MODEL, F32)],
                     dep=after_start)

    own, pending, scatters = {}, {}, {}

    def emit(name, grad):
        grad = from_kernel_layout(name, grad)
        parts = _scatter_cols(grad) if name in COL_SHARDED else _scatter_rows(grad)
        own[name] = lax.dynamic_index_in_dim(parts, me, 0, keepdims=False)
        pending[name] = parts
        for i, grp in enumerate(SCATTER_GROUPS):
            if name == grp[-1]:
                scatters[i] = exchange_start(f"scatter{i}_start", [pending[n] for n in grp], gather=False)
                return scatters[i][-1]
        return None

    pos = positions.reshape(SEQ, 1).astype(F32)
    grad_x, small = _local_step(x[0], pos, mod, loss_target[0], small_w, fetch, emit)

    res, done = {}, grad_x
    for i, grp in enumerate(SCATTER_GROUPS):
        _, lands = exchange_wait(f"scatter{i}_wait", scatters[i], False, done)
        for n, land in zip(grp, lands, strict=True):
            res[n] = adamw(f"adamw_{n}", shard[n], [own[n], land], local("m_", n), local("v_", n), ADAM_TILE.get(n))
            done = res[n][0]
            if n in TRANSPOSED:
                res[n] = [r.T for r in res[n]]
    (small_all,) = all_gather("gather_small", [_pack_small(small)], after=done)
    loss, small_res = adamw_small("adamw_small", small_all, {n: (args[n], args["m_" + n], args["v_" + n]) for n in SMALL_PARAMS})
    row, _, n_mod = SMALL_AT["b_ada"]
    dmod_all = small_all[:, row:row + n_mod // SMALL_COLS, :].reshape(N_DEV, n_mod)
    dmod_mine = lax.dynamic_slice_in_dim(dmod_all, me * (6 * D_MODEL // N_DEV), 6 * D_MODEL // N_DEV, axis=1)
    g_w_ada = matmul("ada_wgrad", sc_all, dmod_mine, "tn")
    res["w_ada"] = adamw("adamw_w_ada", shard["w_ada"], [g_w_ada], m_w_ada[0], v_w_ada[0], ADAM_TILE["w_ada"])

    def leaf(kind, n):
        if n in res:
            return res[n][kind][None]
        return small_res[n][kind]

    return (loss.reshape(()), grad_x[None], *[leaf(k, n) for k in range(4) for n in OUT_WEIGHTS])
```

```python
import jax
import jax.numpy as jnp
from jax import lax
from jax.experimental import pallas as pl
from jax.experimental.pallas import tpu as pltpu

F32 = jnp.float32
MXU_DTYPE = jnp.bfloat16

N_DEV = 8
D_MODEL = 1024
SEQ = 2048
HEADS = 8
NOPE = 64
ROPE = 32
Q_LORA = 512
KV_LORA = 256
DIL_DIM = 64
DIL_WIDTH = HEADS * DIL_DIM
DILATIONS = (1, 4, 16)
SPAN = 128
D_FF = 2816
LANES = 128
ROPE_THETA = 10000.0
EPS = 1e-6
NEG_INF = -1e30
ADAM_LR, ADAM_B1, ADAM_B2, ADAM_EPS, ADAM_WD, ADAM_STEP = 0.001, 0.9, 0.999, 1e-08, 0.01, 10
VMEM_LIMIT = 56 * 1024 * 1024
MESH_ID = pl.DeviceIdType.MESH

P_QLAT, P_KVLAT, P_KPE, P_QD, P_KD, P_VD, P_END = 0, 512, 768, 896, 1408, 1920, 2432
KPE_LO = 64
MIX_IN = HEADS * LANES + DIL_WIDTH


def _params(**kw):
    return pltpu.CompilerParams(vmem_limit_bytes=VMEM_LIMIT, **kw)


def rowwise(name, fn, rows, params, out_rows, out_accs=(), tm=512, dep=None):
    deps = [] if dep is None else [dep]
    rows = [r if isinstance(r, tuple) else (r, r.shape[1], 0) for r in rows]
    R = rows[0][0].shape[0]
    tm = min(tm, R)
    steps = R // tm
    assert steps * tm == R
    in_specs = []
    for a, width, cb in rows:
        ri = a.shape[0]
        per = ri // tm
        assert per * tm == ri
        if ri == R:
            in_specs.append(pl.BlockSpec((tm, width), lambda i, cb=cb: (i, cb)))
        else:
            in_specs.append(pl.BlockSpec((tm, width), lambda i, per=per, cb=cb: (i % per, cb)))
    for p in params:
        in_specs.append(pl.BlockSpec(p.shape, lambda i: (0,) * p.ndim))
    in_specs += [pl.BlockSpec(memory_space=pl.ANY)] * len(deps)
    out_shape = [jax.ShapeDtypeStruct((R, d), dt) for d, dt in out_rows]
    out_specs = [pl.BlockSpec((tm, d), lambda i: (i, 0)) for d, _ in out_rows]
    out_shape += [jax.ShapeDtypeStruct((1, n), F32) for n in out_accs]
    out_specs += [pl.BlockSpec((1, n), lambda i: (0, 0)) for n in out_accs]
    nr, npar, no, na = len(rows), len(params), len(out_rows), len(out_accs)

    def body(*refs):
        rvals = [r[...] for r in refs[:nr]]
        pvals = [r[...] for r in refs[nr:nr + npar]]
        outs, accs = fn(rvals, pvals)
        first_out = nr + npar + len(deps)
        for ref, v in zip(refs[first_out:first_out + no], outs, strict=True):
            ref[...] = v.astype(ref.dtype)
        if na:
            acc_refs = refs[first_out + no:]
            i = pl.program_id(0)

            @pl.when(i == 0)
            def _():
                for ref, v in zip(acc_refs, accs, strict=True):
                    ref[...] = v

            @pl.when(i > 0)
            def _():
                for ref, v in zip(acc_refs, accs, strict=True):
                    ref[...] += v

    res = pl.pallas_call(body, name=name, grid=(steps,), in_specs=in_specs, out_specs=out_specs,
                         out_shape=out_shape, compiler_params=_params())(*[r[0] for r in rows], *params, *deps)
    return list(res)


_DIMS = {"nn": ((1,), (0,)), "nt": ((1,), (1,)), "tn": ((0,), (0,))}


def _dot(a, b, mode="nn"):
    return lax.dot_general(a.astype(MXU_DTYPE), b.astype(MXU_DTYPE), (_DIMS[mode], ((), ())),
                           preferred_element_type=F32)


def matmul(name, a, b, mode, tm=None, tn=None, tk=None, out_dtype=F32, dep=None, a_mmap=None, b_nmap=None, b_kmap=None):
    if mode == "tn":
        K, M = a.shape
    else:
        M, K = a.shape
    N = b.shape[0] if mode == "nt" else b.shape[1]
    tm, tn, tk = tm or M, tn or N, tk or K
    nm, nn, nk = M // tm, N // tn, K // tk
    assert nm * tm == M and nn * tn == N and nk * tk == K
    same = lambda idx: idx
    a_mmap, b_nmap, b_kmap = a_mmap or same, b_nmap or same, b_kmap or same
    if mode == "tn":
        a_spec = pl.BlockSpec((tk, tm), lambda i, j, k: (k, a_mmap(i)))
    else:
        a_spec = pl.BlockSpec((tm, tk), lambda i, j, k: (a_mmap(i), k))
    if mode == "nt":
        b_spec = pl.BlockSpec((tn, tk), lambda i, j, k: (b_nmap(j), b_kmap(k)))
    else:
        b_spec = pl.BlockSpec((tk, tn), lambda i, j, k: (b_kmap(k), b_nmap(j)))
    deps = [] if dep is None else [dep]

    def body(a_ref, b_ref, *rest):
        o_ref, scratch = rest[len(deps)], rest[len(deps) + 1:]
        p = _dot(a_ref[...], b_ref[...], mode)
        if nk == 1:
            o_ref[...] = p.astype(o_ref.dtype)
        else:
            acc = scratch[0]
            k = pl.program_id(2)

            @pl.when(k == 0)
            def _():
                acc[...] = p

            @pl.when(k > 0)
            def _():
                acc[...] += p

            @pl.when(k == nk - 1)
            def _():
                o_ref[...] = acc[...].astype(o_ref.dtype)

    return pl.pallas_call(
        body, name=name, grid=(nm, nn, nk), in_specs=[a_spec, b_spec] + [pl.BlockSpec(memory_space=pl.ANY)] * len(deps),
        out_specs=pl.BlockSpec((tm, tn), lambda i, j, k: (i, j)),
        out_shape=jax.ShapeDtypeStruct((M, N), out_dtype),
        scratch_shapes=[pltpu.VMEM((tm, tn), F32)] if nk > 1 else [],
        compiler_params=_params())(a, b, *deps)


def _rms(x, g):
    rstd = lax.rsqrt(jnp.mean(x * x, axis=-1, keepdims=True) + EPS)
    n = x * rstd
    return n * g, n, rstd


def _rms_bwd(dy, n, rstd, g):
    dg = jnp.sum(dy * n, axis=0, keepdims=True)
    dn = dy * g
    dx = rstd * (dn - n * jnp.mean(dn * n, axis=-1, keepdims=True))
    return dx, dg


def _norm_bwd(dy, x, g):
    _, n, rstd = _rms(x, g)
    return _rms_bwd(dy, n, rstd, g)


def _colsum(v):
    return jnp.sum(v, axis=0, keepdims=True)


def _silu(x):
    return x * (1.0 / (1.0 + jnp.exp(-x)))


def _lane(shape):
    return lax.broadcasted_iota(jnp.int32, shape, 1)


def _group_mean(v, groups):
    lane = _lane(v.shape)
    out = jnp.zeros_like(v)
    for lo, hi in groups:
        m = (lane >= lo) & (lane < hi)
        out = jnp.where(m, jnp.sum(jnp.where(m, v, 0.0), axis=-1, keepdims=True) * (1.0 / (hi - lo)), out)
    return out


def _in_groups(shape, groups):
    lane = _lane(shape)
    m = jnp.zeros(shape, jnp.bool_)
    for lo, hi in groups:
        m = m | ((lane >= lo) & (lane < hi))
    return m


def _grms(x, g, groups):
    rstd = lax.rsqrt(_group_mean(x * x, groups) + EPS)
    n = jnp.where(_in_groups(x.shape, groups), x * rstd, 0.0)
    return n * g, n, rstd


def _grms_bwd(dy, n, rstd, g, groups):
    dn = dy * g
    return rstd * (dn - n * _group_mean(dn * n, groups)), _colsum(dy * n)


def _rot(x, half, transpose=False):
    first = (_lane(x.shape) % (2 * half)) < half
    up = pltpu.roll(x, LANES - half, axis=1)
    down = pltpu.roll(x, half, axis=1)
    return jnp.where(first, up, -down) if transpose else jnp.where(first, -up, down)


def _rope(x, cos, sin, half):
    return x * cos + _rot(x, half) * sin


def _rope_bwd(dy, cos, sin, half):
    return dy * cos + _rot(dy * sin, half, transpose=True)


def _chunks(x):
    return [x[:, i:i + LANES] for i in range(0, x.shape[1], LANES)]


Q_GROUPS = ((0, NOPE), (NOPE, NOPE + ROPE))
K_GROUPS = ((0, NOPE),)
KPE_GROUPS = ((KPE_LO, KPE_LO + ROPE),)
DIL_GROUPS = ((0, DIL_DIM), (DIL_DIM, 2 * DIL_DIM))


def _col(width, rows=SEQ):
    return pl.BlockSpec((rows, width), lambda h: (0, h))


def _causal_tail(s, tq, fill):
    diag = s[:, s.shape[1] - tq:]
    keep = lax.broadcasted_iota(jnp.int32, diag.shape, 1) <= lax.broadcasted_iota(jnp.int32, diag.shape, 0)
    diag = jnp.where(keep, diag, fill)
    return diag if s.shape[1] == tq else jnp.concatenate([s[:, :s.shape[1] - tq], diag], axis=1)


def mla_fwd(name, q, k, v, scale, tq=256):
    S = q.shape[0]

    def body(q_ref, k_ref, v_ref, o_ref, lse_ref):
        for i in range(S // tq):
            kext = (i + 1) * tq
            blk = slice(i * tq, kext)
            s = _causal_tail(_dot(q_ref[blk, :], k_ref[:kext, :], "nt") * scale, tq, NEG_INF)
            m = jnp.max(s, axis=-1, keepdims=True)
            e = jnp.exp(s - m)
            l = jnp.sum(e, axis=-1, keepdims=True)
            o_ref[blk, :] = _dot(e * (1.0 / l), v_ref[:kext, :])
            lse_ref[0, blk, :] = m + jnp.log(l)

    return pl.pallas_call(
        body, name=name, grid=(HEADS,), in_specs=[_col(LANES)] * 3,
        out_specs=[_col(LANES), pl.BlockSpec((1, S, 1), lambda h: (h, 0, 0))],
        out_shape=[jax.ShapeDtypeStruct((S, MIX_IN), F32), jax.ShapeDtypeStruct((HEADS, S, 1), F32)],
        compiler_params=_params())(q, k, v)


def mla_bwd(name, q, k, v, o, do, lse, scale, tq=256):
    S = q.shape[0]

    def body(q_ref, k_ref, v_ref, o_ref, do_ref, lse_ref, dq_ref, dkv_ref, dkpe_ref, dk_acc, dv_acc):
        dk_acc[...] = jnp.zeros_like(dk_acc)
        dv_acc[...] = jnp.zeros_like(dv_acc)
        for i in range(S // tq):
            kext = (i + 1) * tq
            blk = slice(i * tq, kext)
            qi, kk, vv = q_ref[blk, :], k_ref[:kext, :], v_ref[:kext, :]
            doi = do_ref[blk, :]
            s = _causal_tail(_dot(qi, kk, "nt") * scale, tq, NEG_INF)
            p = jnp.exp(s - lse_ref[0, blk, :])
            dp = _dot(doi, vv, "nt")
            delta = jnp.sum(doi * o_ref[blk, :], axis=-1, keepdims=True)
            ds = p * (dp - delta) * scale
            dq_ref[blk, :] = _dot(ds, kk)
            dk_acc[:kext, :] += _dot(ds, qi, "tn")
            dv_acc[:kext, :] += _dot(p, doi, "tn")
        dk = dk_acc[...]
        lane = _lane(dk.shape)
        dkv_ref[...] = jnp.where(lane < NOPE, dk, 0.0) + dv_acc[...]
        dkpe = jnp.where((lane >= KPE_LO) & (lane < KPE_LO + ROPE), dk, 0.0)
        h = pl.program_id(0)

        @pl.when(h == 0)
        def _():
            dkpe_ref[...] = dkpe

        @pl.when(h > 0)
        def _():
            dkpe_ref[...] += dkpe

    return pl.pallas_call(
        body, name=name, grid=(HEADS,),
        in_specs=[_col(LANES)] * 5 + [pl.BlockSpec((1, S, 1), lambda h: (h, 0, 0))],
        out_specs=[_col(LANES), _col(LANES), pl.BlockSpec((S, LANES), lambda h: (0, 0))],
        out_shape=[jax.ShapeDtypeStruct((S, HEADS * LANES), F32), jax.ShapeDtypeStruct((S, HEADS * LANES), F32),
                   jax.ShapeDtypeStruct((S, LANES), F32)],
        scratch_shapes=[pltpu.VMEM((S, LANES), F32), pltpu.VMEM((S, LANES), F32)],
        compiler_params=_params())(q, k, v, o, do, lse)


BAND_TQ = SPAN


def _band_blocks(L, tq):
    return [(i * tq, (i + 1) * tq, max(0, i * tq - SPAN)) for i in range(L // tq)]


def _band_mask(q0, q1, k0):
    shape = (q1 - q0, q1 - k0)
    dist = (lax.broadcasted_iota(jnp.int32, shape, 0) + q0) - (lax.broadcasted_iota(jnp.int32, shape, 1) + k0)
    return (dist >= 0) & (dist <= SPAN)


def _class_rows(r, dil, lo, hi):
    return pl.ds(r + dil * lo, hi - lo, stride=dil) if dil > 1 else pl.ds(lo, hi - lo)


def _stack_heads(t, lo):
    zero = jnp.zeros_like(t)
    return jnp.concatenate([jnp.where(lo, t, zero), jnp.where(lo, zero, t)], axis=0)


def _band_mask2(q0, q1, k0):
    n = q1 - q0
    shape = (2 * n, q1 - k0)
    i = lax.broadcasted_iota(jnp.int32, shape, 0)
    dist = (jnp.where(i >= n, i - n, i) + q0) - (lax.broadcasted_iota(jnp.int32, shape, 1) + k0)
    return (dist >= 0) & (dist <= SPAN)


def _pair_col(col0=0):
    return pl.BlockSpec((SEQ, LANES), lambda j: (0, col0 // LANES + j))


def band_fwd(name, q, k, v, dil):
    S = q.shape[0]
    L = S // dil
    tq = BAND_TQ
    scale = DIL_DIM ** -0.5

    def body(q_ref, k_ref, v_ref, o_ref, lse_ref):
        for r in range(dil):
            for q0, q1, k0 in _band_blocks(L, tq):
                qrows, krows = _class_rows(r, dil, q0, q1), _class_rows(r, dil, k0, q1)
                qb, kb, vb = q_ref[qrows, :].astype(MXU_DTYPE), k_ref[krows, :], v_ref[krows, :]
                n = q1 - q0
                lo = _lane(qb.shape) < DIL_DIM
                s = _dot(_stack_heads(qb, lo), kb, "nt") * scale
                s = jnp.where(_band_mask2(q0, q1, k0), s, NEG_INF)
                mx = jnp.max(s, axis=-1, keepdims=True)
                e = jnp.exp(s - mx)
                l = jnp.sum(e, axis=-1, keepdims=True)
                pv, lse = _dot(e * (1.0 / l), vb), mx + jnp.log(l)
                o_ref[qrows, :] = jnp.where(lo, pv[:n], pv[n:])
                lse_ref[qrows, :] = jnp.where(lo, lse[:n], lse[n:])

    return pl.pallas_call(
        body, name=name, grid=(DIL_WIDTH // LANES,), in_specs=[_pair_col()] * 2 + [_pair_col(P_VD)], out_specs=[_pair_col()] * 2,
        out_shape=[jax.ShapeDtypeStruct((S, DIL_WIDTH), F32)] * 2, compiler_params=_params())(q, k, v)


def band_bwd(name, q, k, v, lse, lse_mix, o_cat, do_cat, dil):
    S = q.shape[0]
    L = S // dil
    tq = BAND_TQ
    scale = DIL_DIM ** -0.5

    def body(q_ref, k_ref, v_ref, lse_ref, mix_ref, o_ref, do_ref, dq_ref, dk_ref, dv_ref):
        dk_ref[...] = jnp.zeros_like(dk_ref)
        dv_ref[...] = jnp.zeros_like(dv_ref)
        for r in range(dil):
            for q0, q1, k0 in _band_blocks(L, tq):
                qrows, krows = _class_rows(r, dil, q0, q1), _class_rows(r, dil, k0, q1)
                qb, kb, vb = q_ref[qrows, :].astype(MXU_DTYPE), k_ref[krows, :], v_ref[krows, :]
                lse_p, dout = lse_ref[qrows, :], do_ref[qrows, :]
                n = q1 - q0
                lo = _lane(qb.shape) < DIL_DIM
                per_head = lambda t: jnp.concatenate([t[:, 0:1], t[:, DIL_DIM:DIL_DIM + 1]], axis=0)
                w2 = per_head(jnp.exp(lse_p - mix_ref[qrows, :]))
                dd = dout * o_ref[qrows, :]
                big_d = jnp.concatenate([jnp.sum(jnp.where(lo, dd, 0.0), axis=-1, keepdims=True),
                                         jnp.sum(jnp.where(lo, 0.0, dd), axis=-1, keepdims=True)], axis=0)
                q2 = _stack_heads(qb, lo)
                s = _dot(q2, kb, "nt") * scale
                p = jnp.where(_band_mask2(q0, q1, k0), jnp.exp(s - per_head(lse_p)), 0.0)
                dom = _stack_heads(dout, lo) * w2
                ds = p * (_dot(dom, vb, "nt") - w2 * big_d) * scale
                dq2 = _dot(ds, kb)
                dq_ref[qrows, :] = jnp.where(lo, dq2[:n], dq2[n:])
                dk_ref[krows, :] += _dot(ds, q2, "tn")
                dv_ref[krows, :] += _dot(p, dom, "tn")

    cat = _pair_col(HEADS * LANES)
    return pl.pallas_call(
        body, name=name, grid=(DIL_WIDTH // LANES,),
        in_specs=[_pair_col()] * 2 + [_pair_col(P_VD)] + [_pair_col()] * 2 + [cat] * 2, out_specs=[_pair_col()] * 3,
        out_shape=[jax.ShapeDtypeStruct((S, DIL_WIDTH), F32)] * 3,
        compiler_params=_params())(q, k, v, lse, lse_mix, o_cat, do_cat)


def combine_fwd(name, outs, lses, o_cat, tm=512):
    S = outs[0].shape[0]

    def body(o1, o2, o3, l1, l2, l3, cat_in, cat_out, mix_ref):
        ls = [l1[...], l2[...], l3[...]]
        m = jnp.maximum(jnp.maximum(ls[0], ls[1]), ls[2])
        e = [jnp.exp(l - m) for l in ls]
        den = e[0] + e[1] + e[2]
        cat_out[...] = (e[0] / den) * o1[...] + (e[1] / den) * o2[...] + (e[2] / den) * o3[...]
        mix_ref[...] = m + jnp.log(den)

    row = pl.BlockSpec((tm, DIL_WIDTH), lambda i: (i, 0))
    return pl.pallas_call(
        body, name=name, grid=(S // tm,), in_specs=[row] * 6 + [pl.BlockSpec(memory_space=pl.ANY)],
        out_specs=[pl.BlockSpec((tm, DIL_WIDTH), lambda i: (i, HEADS * LANES // DIL_WIDTH)), row],
        out_shape=[jax.ShapeDtypeStruct(o_cat.shape, F32), jax.ShapeDtypeStruct((S, DIL_WIDTH), F32)],
        input_output_aliases={6: 0}, compiler_params=_params())(*outs, *lses, o_cat)


def _shift_down(u, n, zero_head):
    out = pltpu.roll(u, n, axis=0)
    return jnp.where(lax.broadcasted_iota(jnp.int32, u.shape, 0) >= n, out, 0.0) if zero_head else out


def _shift_up(u, n, zero_tail):
    rows = u.shape[0]
    out = pltpu.roll(u, rows - n, axis=0)
    return jnp.where(lax.broadcasted_iota(jnp.int32, u.shape, 0) < rows - n, out, 0.0) if zero_tail else out


CONV_ROWS = 512
CONV_HALO = 16


def _conv_chunks(S, tail):
    out = []
    for r0 in range(0, S, CONV_ROWS):
        lo, hi = max(0, r0 - CONV_HALO), min(S, r0 + CONV_ROWS + (CONV_HALO if tail else 0))
        out.append((lo, hi, r0 - lo, CONV_ROWS))
    return out


CONV_TC = 256
CONV_NB = D_FF // CONV_TC


def paired_block(j):
    return (j % 2) * CONV_NB + j // 2


def natural_block(j):
    return jnp.where(j < CONV_NB, 2 * j, 2 * (j - CONV_NB) + 1)


def _pair_spec(rows):
    return pl.BlockSpec((rows, 2 * CONV_TC), lambda j: (0, j))


def _half_specs(rows, rows_axis=False):
    if rows_axis:
        return [pl.BlockSpec((rows, D_MODEL), lambda j: (j, 0)), pl.BlockSpec((rows, D_MODEL), lambda j: (j + CONV_NB, 0))]
    return [pl.BlockSpec((rows, CONV_TC), lambda j: (0, j)), pl.BlockSpec((rows, CONV_TC), lambda j: (0, j + CONV_NB))]


def pair_dgrad(name, dup, w_t, dep=None):
    S, D = dup.shape[0], w_t.shape[1]
    deps = [] if dep is None else [dep]

    def body(a_ref, bg_ref, bv_ref, *rest):
        o_ref = rest[-1]
        p = _dot(a_ref[:, :CONV_TC], bg_ref[...]) + _dot(a_ref[:, CONV_TC:], bv_ref[...])
        j = pl.program_id(0)

        @pl.when(j == 0)
        def _():
            o_ref[...] = p

        @pl.when(j > 0)
        def _():
            o_ref[...] += p

    return pl.pallas_call(
        body, name=name, grid=(CONV_NB,),
        in_specs=[_pair_spec(S), pl.BlockSpec((CONV_TC, D), lambda j: (j, 0)), pl.BlockSpec((CONV_TC, D), lambda j: (j + CONV_NB, 0))]
        + [pl.BlockSpec(memory_space=pl.ANY)] * len(deps),
        out_specs=pl.BlockSpec((S, D), lambda j: (0, 0)), out_shape=jax.ShapeDtypeStruct((S, D), F32),
        compiler_params=_params())(dup, w_t, w_t, *deps)


def _whole(a):
    return pl.BlockSpec(a.shape, lambda j: (0,) * a.ndim)


def _up_conv(h, ug_ref, uv_ref, w, b, starts):
    uin = jnp.concatenate([_dot(h, ug_ref[...], "nt"), _dot(h, uv_ref[...], "nt")], axis=1)
    u1, u2 = _shift_down(uin, 1, starts), _shift_down(uin, 2, starts)
    return uin, u1, u2, w[2:3, :] * uin + w[1:2, :] * u1 + w[0:1, :] * u2 + b


def conv_glu_fwd(name, h, w_up_t, w_conv, b_conv):
    S = h.shape[0]

    def body(h_ref, ug_ref, uv_ref, wg_ref, wv_ref, bg_ref, bv_ref, act_ref):
        w = jnp.concatenate([wg_ref[...], wv_ref[...]], axis=1)
        b = jnp.concatenate([bg_ref[...], bv_ref[...]], axis=1)
        for lo, hi, keep, rows in _conv_chunks(S, tail=False):
            u = _up_conv(h_ref[lo:hi, :], ug_ref, uv_ref, w, b, lo == 0)[3][keep:keep + rows]
            act_ref[lo + keep:lo + keep + rows, :] = (_silu(u[:, :CONV_TC]) * u[:, CONV_TC:]).astype(act_ref.dtype)

    return pl.pallas_call(
        body, name=name, grid=(CONV_NB,),
        in_specs=[_whole(h)] + _half_specs(CONV_TC, rows_axis=True) + _half_specs(3) + _half_specs(1),
        out_specs=pl.BlockSpec((S, CONV_TC), lambda j: (0, j)), out_shape=jax.ShapeDtypeStruct((S, D_FF), MXU_DTYPE),
        compiler_params=_params())(h, w_up_t, w_up_t, w_conv, w_conv, b_conv, b_conv)


def conv_glu_bwd(name, h, w_up_t, w_conv, b_conv, d_dn, w_down):
    S = h.shape[0]

    def body(h_ref, ug_ref, uv_ref, wg_ref, wv_ref, bg_ref, bv_ref, dd_ref, wd_ref, dup_ref, dwg_ref, dwv_ref, dbg_ref, dbv_ref):
        w = jnp.concatenate([wg_ref[...], wv_ref[...]], axis=1)
        b = jnp.concatenate([bg_ref[...], bv_ref[...]], axis=1)
        dw, db = 0.0, 0.0
        for lo, hi, keep, rows in _conv_chunks(S, tail=True):
            uin, u1, u2, u = _up_conv(h_ref[lo:hi, :], ug_ref, uv_ref, w, b, lo == 0)
            gate, val, da = u[:, :CONV_TC], u[:, CONV_TC:], _dot(dd_ref[lo:hi, :], wd_ref[...], "nt")
            sig = 1.0 / (1.0 + jnp.exp(-gate))
            du = jnp.concatenate([da * val * (sig * (1.0 + gate * (1.0 - sig))), da * (gate * sig)], axis=1)
            dup = w[2:3, :] * du + w[1:2, :] * _shift_up(du, 1, hi == S) + w[0:1, :] * _shift_up(du, 2, hi == S)
            kept = slice(keep, keep + rows)
            dup_ref[lo + keep:lo + keep + rows, :] = dup[kept].astype(dup_ref.dtype)
            du = du[kept]
            dw = dw + jnp.concatenate([_colsum(du * u2[kept]), _colsum(du * u1[kept]), _colsum(du * uin[kept])], axis=0)
            db = db + _colsum(du)
        dwg_ref[...], dwv_ref[...] = dw[:, :CONV_TC], dw[:, CONV_TC:]
        dbg_ref[...], dbv_ref[...] = db[:, :CONV_TC], db[:, CONV_TC:]

    half = lambda rows: pl.BlockSpec((rows, CONV_TC), lambda j: (0, j))
    dup, dwg, dwv, dbg, dbv = pl.pallas_call(
        body, name=name, grid=(CONV_NB,),
        in_specs=[_whole(h)] + _half_specs(CONV_TC, rows_axis=True) + _half_specs(3) + _half_specs(1)
        + [_whole(d_dn), pl.BlockSpec((CONV_TC, w_down.shape[1]), lambda j: (j, 0))],
        out_specs=[_pair_spec(S), half(3), half(3), half(1), half(1)],
        out_shape=[jax.ShapeDtypeStruct((S, 2 * D_FF), MXU_DTYPE)] + [jax.ShapeDtypeStruct((3, D_FF), F32)] * 2
        + [jax.ShapeDtypeStruct((1, D_FF), F32)] * 2,
        compiler_params=_params())(h, w_up_t, w_up_t, w_conv, w_conv, b_conv, b_conv, d_dn, w_down)
    return dup, jnp.concatenate([dwg, dwv], axis=1), jnp.concatenate([dbg, dbv], axis=1)


def ffn_fwd(name, h, w_up_t, w_conv, b_conv, w_down):
    S = h.shape[0]

    def body(h_ref, ug_ref, uv_ref, wg_ref, wv_ref, bg_ref, bv_ref, wd_ref, dn_ref):
        @pl.when(pl.program_id(0) == 0)
        def _():
            dn_ref[...] = jnp.zeros_like(dn_ref)

        w = jnp.concatenate([wg_ref[...], wv_ref[...]], axis=1)
        b = jnp.concatenate([bg_ref[...], bv_ref[...]], axis=1)
        for lo, hi, keep, rows in _conv_chunks(S, tail=False):
            u = _up_conv(h_ref[lo:hi, :], ug_ref, uv_ref, w, b, lo == 0)[3][keep:keep + rows]
            dn_ref[lo + keep:lo + keep + rows, :] += _dot(_silu(u[:, :CONV_TC]) * u[:, CONV_TC:], wd_ref[...])

    return pl.pallas_call(
        body, name=name, grid=(CONV_NB,),
        in_specs=[_whole(h)] + _half_specs(CONV_TC, rows_axis=True) + _half_specs(3) + _half_specs(1)
        + [pl.BlockSpec((CONV_TC, w_down.shape[1]), lambda j: (j, 0))],
        out_specs=pl.BlockSpec((S, w_down.shape[1]), lambda j: (0, 0)), out_shape=jax.ShapeDtypeStruct((S, w_down.shape[1]), F32),
        compiler_params=_params())(h, w_up_t, w_up_t, w_conv, w_conv, b_conv, b_conv, w_down)


def ffn_bwd(name, h, w_up_t, w_conv, b_conv, d_dn, w_down):
    S, D = h.shape

    def body(h_ref, ug_ref, uv_ref, wg_ref, wv_ref, bg_ref, bv_ref, dd_ref, wd_ref,
             dh_ref, gug_ref, guv_ref, gd_ref, dwg_ref, dwv_ref, dbg_ref, dbv_ref):
        @pl.when(pl.program_id(0) == 0)
        def _():
            dh_ref[...] = jnp.zeros_like(dh_ref)

        w = jnp.concatenate([wg_ref[...], wv_ref[...]], axis=1)
        b = jnp.concatenate([bg_ref[...], bv_ref[...]], axis=1)
        w_pair = jnp.concatenate([ug_ref[...], uv_ref[...]], axis=0)
        dw, db, g_up, g_dn = 0.0, 0.0, 0.0, 0.0
        for lo, hi, keep, rows in _conv_chunks(S, tail=True):
            uin, u1, u2, u = _up_conv(h_ref[lo:hi, :], ug_ref, uv_ref, w, b, lo == 0)
            gate, val, da = u[:, :CONV_TC], u[:, CONV_TC:], _dot(dd_ref[lo:hi, :], wd_ref[...], "nt")
            sig = 1.0 / (1.0 + jnp.exp(-gate))
            du = jnp.concatenate([da * val * (sig * (1.0 + gate * (1.0 - sig))), da * (gate * sig)], axis=1)
            dup = w[2:3, :] * du + w[1:2, :] * _shift_up(du, 1, hi == S) + w[0:1, :] * _shift_up(du, 2, hi == S)
            kept, out_rows = slice(keep, keep + rows), slice(lo + keep, lo + keep + rows)
            dup, du = dup[kept], du[kept]
            dh_ref[out_rows, :] += _dot(dup, w_pair)
            g_up = g_up + _dot(dup, h_ref[out_rows, :], "tn")
            g_dn = g_dn + _dot((gate * sig * val)[kept], dd_ref[out_rows, :], "tn")
            dw = dw + jnp.concatenate([_colsum(du * u2[kept]), _colsum(du * u1[kept]), _colsum(du * uin[kept])], axis=0)
            db = db + _colsum(du)
        gug_ref[...], guv_ref[...] = g_up[:CONV_TC].astype(gug_ref.dtype), g_up[CONV_TC:].astype(guv_ref.dtype)
        gd_ref[...] = g_dn.astype(gd_ref.dtype)
        dwg_ref[...], dwv_ref[...] = dw[:, :CONV_TC], dw[:, CONV_TC:]
        dbg_ref[...], dbv_ref[...] = db[:, :CONV_TC], db[:, CONV_TC:]

    half = lambda rows: pl.BlockSpec((rows, CONV_TC), lambda j: (0, j))
    rows_blk = pl.BlockSpec((CONV_TC, D), lambda j: (j, 0))
    dh, gug, guv, gd, dwg, dwv, dbg, dbv = pl.pallas_call(
        body, name=name, grid=(CONV_NB,),
        in_specs=[_whole(h)] + _half_specs(CONV_TC, rows_axis=True) + _half_specs(3) + _half_specs(1) + [_whole(d_dn), rows_blk],
        out_specs=[pl.BlockSpec((S, D), lambda j: (0, 0)), rows_blk, rows_blk, rows_blk, half(3), half(3), half(1), half(1)],
        out_shape=[jax.ShapeDtypeStruct((S, D), F32)] + [jax.ShapeDtypeStruct((D_FF, D), MXU_DTYPE)] * 3
        + [jax.ShapeDtypeStruct((3, D_FF), F32)] * 2 + [jax.ShapeDtypeStruct((1, D_FF), F32)] * 2,
        compiler_params=_params())(h, w_up_t, w_up_t, w_conv, w_conv, b_conv, b_conv, d_dn, w_down)
    return dh, gug, guv, gd, jnp.concatenate([dwg, dwv], axis=1), jnp.concatenate([dbg, dbv], axis=1)


def adamw(name, w, parts, m, v, tr=None):
    R, C = w.shape
    tr = tr or R
    assert R % tr == 0
    c1 = 1.0 - ADAM_B1 ** ADAM_STEP
    c2 = 1.0 - ADAM_B2 ** ADAM_STEP
    np_ = len(parts)

    def body(*refs):
        w_ref, m_ref, v_ref = refs[0], refs[1 + np_], refs[2 + np_]
        go_ref, d_ref, mo_ref, vo_ref = refs[3 + np_:]
        terms = []
        for part, ref in zip(parts, refs[1:1 + np_], strict=True):
            terms += [ref[...]] if part.ndim == 2 else [ref[p] for p in range(part.shape[0])]
        g = terms[0].astype(F32)
        for term in terms[1:]:
            g = g + term.astype(F32)
        m2 = ADAM_B1 * m_ref[...] + (1.0 - ADAM_B1) * g
        v2 = ADAM_B2 * v_ref[...] + (1.0 - ADAM_B2) * (g * g)
        go_ref[...] = g
        mo_ref[...] = m2
        vo_ref[...] = v2
        d_ref[...] = -ADAM_LR * ((m2 / c1) / (jnp.sqrt(v2 / c2) + ADAM_EPS) + ADAM_WD * w_ref[...])

    blk = pl.BlockSpec((tr, C), lambda i: (i, 0))
    part_specs = [blk if p.ndim == 2 else pl.BlockSpec((p.shape[0], tr, C), lambda i: (0, i, 0)) for p in parts]
    return pl.pallas_call(
        body, name=name, grid=(R // tr,),
        in_specs=[blk] + part_specs + [blk, blk], out_specs=[blk] * 4,
        out_shape=[jax.ShapeDtypeStruct((R, C), F32)] * 4, compiler_params=_params())(w, *parts, m, v)


def _place():
    return lax.axis_index("x"), lax.axis_index("y"), lax.axis_index("c")


def all_gather(name, arrs, after=None):
    n = len(arrs)
    deps = [] if after is None else [after]

    def body(*refs):
        ins, outs = refs[:n], refs[n + len(deps):2 * n + len(deps)]
        send_sems, recv_sems, local_sems = refs[2 * n + len(deps):]
        x, y, c = _place()
        me, sibling = (x, y, c), (x, y, 1 - c)
        chips = [(1 - x, y), (x, 1 - y), (1 - x, 1 - y)]
        sends = []
        for t in range(n):
            out = outs[t]

            def slot(px, py, pc, out=out):
                return out.at[4 * px + 2 * py + pc]

            def copy(k, block, to, src=None, t=t, slot=slot):
                return pltpu.make_async_remote_copy(
                    src_ref=slot(*block) if src is None else src, dst_ref=slot(*block),
                    send_sem=send_sems.at[7 * t + k], recv_sem=recv_sems.at[7 * t + k],
                    device_id=to, device_id_type=MESH_ID)

            mine = pltpu.make_async_copy(ins[t], slot(*me), local_sems.at[t])
            mine.start()
            first = [copy(0, me, sibling, src=ins[t])]
            first += [copy(1 + j, me, (*chip, c), src=ins[t]) for j, chip in enumerate(chips)]
            for cp in first:
                cp.start()
            sends.append((mine, first, copy))
        for t in range(n):
            mine, first, copy = sends[t]
            passed = [copy(4 + j, (*chip, c), sibling) for j, chip in enumerate(chips)]
            for j, chip in enumerate(chips):
                copy(1 + j, (*chip, c), me).wait_recv()
                passed[j].start()
            copy(0, sibling, me).wait_recv()
            for j, chip in enumerate(chips):
                copy(4 + j, (*chip, 1 - c), me).wait_recv()
            for cp in first + passed:
                cp.wait_send()
            mine.wait()

    any_spec = pl.BlockSpec(memory_space=pl.ANY)
    res = pl.pallas_call(
        body, name=name, in_specs=[any_spec] * (n + len(deps)), out_specs=[any_spec] * n,
        out_shape=[jax.ShapeDtypeStruct((N_DEV,) + a.shape, a.dtype) for a in arrs],
        scratch_shapes=[pltpu.SemaphoreType.DMA((7 * n,)), pltpu.SemaphoreType.DMA((7 * n,)), pltpu.SemaphoreType.DMA((n,))],
        compiler_params=pltpu.CompilerParams(has_side_effects=True))(*arrs, *deps)
    return list(res)


HBM_SPEC = pl.BlockSpec(memory_space=pltpu.HBM)
SEM_SPEC = pl.BlockSpec(memory_space=pltpu.SEMAPHORE)
DATAFLOW = pltpu.SideEffectType.DATAFLOW_SIDE_EFFECTING


def _exchange_copies(srcs, lands, send_sems, recv_sems, gather):
    x, y, c = _place()
    me = 4 * x + 2 * y + c
    out = []
    for t, (src, land) in enumerate(zip(srcs, lands, strict=True)):
        for k in range(1, N_DEV):
            px, py, pc = x ^ (k >> 2), y ^ ((k >> 1) & 1), c ^ (k & 1)
            out.append(pltpu.make_async_remote_copy(
                src_ref=src if gather else src.at[4 * px + 2 * py + pc],
                dst_ref=land.at[me] if gather else land.at[k - 1],
                send_sem=send_sems.at[7 * t + k - 1], recv_sem=recv_sems.at[7 * t + k - 1],
                device_id=(px, py, pc), device_id_type=MESH_ID))
    return out


def exchange_start(name, arrs, gather, after=None):
    n = len(arrs)
    lands = [lax.empty(((N_DEV,) + a.shape) if gather else ((N_DEV - 1,) + a.shape[1:]), a.dtype) for a in arrs]
    deps = [] if after is None else [after]

    def body(*refs):
        srcs, land_refs = refs[:n], refs[n:2 * n]
        send_sems, recv_sems = refs[2 * n + len(deps)], refs[2 * n + len(deps) + 1]
        token = refs[-1]
        for cp in _exchange_copies(srcs, land_refs, send_sems, recv_sems, gather):
            cp.start()
        token[...] = jnp.zeros_like(token)

    hbm = lambda a: pltpu.HBM(a.shape, a.dtype)
    res = pl.pallas_call(
        body, name=name,
        out_shape=(pltpu.SemaphoreType.DMA((7 * n,)), pltpu.SemaphoreType.DMA((7 * n,)), *[hbm(a) for a in arrs],
                   *[hbm(l) for l in lands], jax.ShapeDtypeStruct((8, 128), F32)),
        in_specs=[HBM_SPEC] * (2 * n) + [pl.BlockSpec(memory_space=pl.ANY)] * len(deps),
        out_specs=(SEM_SPEC, SEM_SPEC, *[HBM_SPEC] * (2 * n), pl.BlockSpec(memory_space=pltpu.VMEM)),
        input_output_aliases={i: 2 + i for i in range(2 * n)},
        compiler_params=pltpu.CompilerParams(has_side_effects=DATAFLOW),
    )(*[pltpu.with_memory_space_constraint(a, pltpu.HBM) for a in arrs + lands], *deps)
    return res[0], res[1], list(res[2:2 + n]), list(res[2 + n:2 + 2 * n]), res[-1]


def exchange_wait(name, started, gather, after):
    send_sems, recv_sems, srcs, lands, _ = started
    n = len(srcs)

    def body(*refs):
        src_refs, land_refs = refs[:n], refs[n:2 * n]
        copies = _exchange_copies(src_refs, land_refs, refs[2 * n], refs[2 * n + 1], gather)
        for cp in copies:
            cp.wait_send()
        for cp in copies:
            cp.wait_recv()

    hbm = lambda a: pltpu.HBM(a.shape, a.dtype)
    res = pl.pallas_call(
        body, name=name, out_shape=tuple(hbm(a) for a in srcs + lands),
        in_specs=[HBM_SPEC] * (2 * n) + [SEM_SPEC, SEM_SPEC, pl.BlockSpec(memory_space=pl.ANY)],
        out_specs=tuple([HBM_SPEC] * (2 * n)), input_output_aliases={i: i for i in range(2 * n)},
        compiler_params=pltpu.CompilerParams(has_side_effects=DATAFLOW),
    )(*srcs, *lands, send_sems, recv_sems, after)
    return list(res[:n]), list(res[n:])


def _gather_cols(stack):
    p, k, n = stack.shape
    return stack.transpose(1, 0, 2).reshape(k, p * n)


def _scatter_cols(full):
    k, n = full.shape
    return full.reshape(k, N_DEV, n // N_DEV).transpose(1, 0, 2)


def _gather_rows(stack):
    p, r, n = stack.shape
    return stack.reshape(p * r, n)


def _scatter_rows(full):
    r, n = full.shape
    return full.reshape(N_DEV, r // N_DEV, n)


_IN_NAT = Q_LORA + KV_LORA
TRANSPOSED = ("w_in", "w_q_b", "w_up")


def to_kernel_layout(name, w):
    if name == "w_in":
        z = lambda n: jnp.zeros((n, w.shape[1]), w.dtype)
        return jnp.concatenate([w[:_IN_NAT], z(KPE_LO), w[_IN_NAT:_IN_NAT + ROPE], z(LANES - KPE_LO - ROPE), w[_IN_NAT + ROPE:]], axis=0)
    if name == "w_q_b":
        return jnp.pad(w.reshape(HEADS, NOPE + ROPE, -1), ((0, 0), (0, LANES - NOPE - ROPE), (0, 0))).reshape(HEADS * LANES, -1)
    if name == "w_o":
        mla = jnp.pad(w[:HEADS * NOPE].reshape(HEADS, NOPE, -1), ((0, 0), (LANES - NOPE, 0), (0, 0))).reshape(HEADS * LANES, -1)
        return jnp.concatenate([mla, w[HEADS * NOPE:]], axis=0)
    return w


def from_kernel_layout(name, g):
    if name == "w_in":
        return jnp.concatenate([g[:_IN_NAT], g[P_KPE + KPE_LO:P_KPE + KPE_LO + ROPE], g[P_QD:]], axis=0)
    if name == "w_q_b":
        return g.reshape(HEADS, LANES, -1)[:, :NOPE + ROPE, :].reshape(HEADS * (NOPE + ROPE), -1)
    if name == "w_o":
        mla = g[:HEADS * LANES].reshape(HEADS, LANES, -1)[:, LANES - NOPE:, :].reshape(HEADS * NOPE, -1)
        return jnp.concatenate([mla, g[HEADS * LANES:]], axis=0)
    return g


SMALL_COLS = 1024
SMALL_ROWS = 24
SMALL_AT = {"loss": (0, 0, 1), "b_ada": (1, 0, 6 * D_MODEL), "g_mix_norm": (7, 0, D_MODEL), "g_q_lat": (8, 0, Q_LORA),
            "g_kv_lat": (9, 0, KV_LORA), "g_mla_q_nope": (10, 0, NOPE), "g_mla_q_pe": (10, 128, ROPE),
            "g_mla_k_nope": (10, 256, NOPE), "g_mla_k_pe": (10, 384, ROPE), "g_dil_q": (10, 512, DIL_DIM),
            "g_dil_k": (10, 640, DIL_DIM), "g_ffn_norm": (11, 0, D_MODEL), "b_conv": (12, 0, 2 * D_FF)}
SMALL_PARAMS = tuple(n for n in SMALL_AT if n != "loss")


def _pack_small(values):
    by_row = {}
    for name, (row, off, n) in SMALL_AT.items():
        by_row.setdefault(row, []).append((off, values[name].reshape(-1).astype(F32)))
    out = []
    for row in sorted(by_row):
        pieces, at = [], 0
        for off, v in sorted(by_row[row], key=lambda t: t[0]):
            pieces += [jnp.zeros((off - at,), F32), v]
            at = off + v.shape[0]
        flat = jnp.concatenate(pieces)
        nrows = -(-flat.shape[0] // SMALL_COLS)
        out.append(jnp.pad(flat, (0, nrows * SMALL_COLS - flat.shape[0])).reshape(nrows, SMALL_COLS))
    packed = jnp.concatenate(out, axis=0)
    return jnp.pad(packed, ((0, SMALL_ROWS - packed.shape[0]), (0, 0)))


def _adam(w, g, m, v):
    c1 = 1.0 - ADAM_B1 ** ADAM_STEP
    c2 = 1.0 - ADAM_B2 ** ADAM_STEP
    m2 = ADAM_B1 * m + (1.0 - ADAM_B1) * g
    v2 = ADAM_B2 * v + (1.0 - ADAM_B2) * (g * g)
    return -ADAM_LR * ((m2 / c1) / (jnp.sqrt(v2 / c2) + ADAM_EPS) + ADAM_WD * w), m2, v2


def adamw_small(name, stack, params):
    flat = [a for n in SMALL_PARAMS for a in params[n]]

    def body(stack_ref, *refs):
        ins, outs = refs[:len(flat)], refs[len(flat):]
        g_all = stack_ref[0]
        for p in range(1, N_DEV):
            g_all = g_all + stack_ref[p]
        outs[0][...] = g_all[0:1, 0:1]
        for i, pname in enumerate(SMALL_PARAMS):
            row, off, n = SMALL_AT[pname]
            w_ref, m_ref, v_ref = ins[3 * i:3 * i + 3]
            go_ref, d_ref, mo_ref, vo_ref = outs[1 + 4 * i:5 + 4 * i]
            for c0 in range(0, n, SMALL_COLS):
                cn = min(SMALL_COLS, n - c0)
                r = row + c0 // SMALL_COLS
                g = g_all[r:r + 1, off:off + cn]
                cols = (slice(None), slice(c0, c0 + cn))
                d, m2, v2 = _adam(w_ref[cols], g, m_ref[cols], v_ref[cols])
                go_ref[cols], d_ref[cols], mo_ref[cols], vo_ref[cols] = g, d, m2, v2

    whole = lambda a: pl.BlockSpec(a.shape, lambda: (0,) * a.ndim)
    out_shape = [jax.ShapeDtypeStruct((1, 1), F32)] + [jax.ShapeDtypeStruct(a.shape, F32) for n in SMALL_PARAMS for a in params[n][:1] * 4]
    res = pl.pallas_call(body, name=name, in_specs=[whole(stack)] + [whole(a) for a in flat],
                         out_specs=[pl.BlockSpec(s.shape, lambda s=s: (0,) * len(s.shape)) for s in out_shape],
                         out_shape=out_shape, compiler_params=_params())(stack, *flat)
    return res[0], {n: res[1 + 4 * i:5 + 4 * i] for i, n in enumerate(SMALL_PARAMS)}


def _local_step(x, pos, mod, target, w, fetch, emit):
    S = SEQ
    sh1, sc1, g1, sh2, sc2, g2 = [mod[:, i * D_MODEL:(i + 1) * D_MODEL] for i in range(6)]
    zeros = lambda n: jnp.zeros((1, n), F32)
    g_q = jnp.concatenate([w["g_mla_q_nope"], w["g_mla_q_pe"], zeros(LANES - NOPE - ROPE)], axis=1)
    g_k = jnp.concatenate([w["g_mla_k_nope"], zeros(LANES - NOPE)], axis=1)
    g_kpe = jnp.concatenate([zeros(KPE_LO), w["g_mla_k_pe"], zeros(LANES - KPE_LO - ROPE)], axis=1)
    g_dq = jnp.concatenate([w["g_dil_q"]] * 2, axis=1)
    g_dk = jnp.concatenate([w["g_dil_k"]] * 2, axis=1)
    b_conv = w["b_conv"]

    def inv_freq(d):
        return jnp.power(ROPE_THETA, -2.0 * jnp.arange(d // 2, dtype=F32) / d)

    f_mla = jnp.concatenate([jnp.zeros((KPE_LO,), F32), inv_freq(ROPE), inv_freq(ROPE), jnp.zeros((LANES - KPE_LO - ROPE,), F32)])
    f_dil = jnp.concatenate([inv_freq(DIL_DIM)] * 4)

    def tables_fn(rows, params):
        (p,), (fa, fb) = rows, params
        return [jnp.cos(p * fa), jnp.sin(p * fa), jnp.cos(p * fb), jnp.sin(p * fb)], []

    cos_m, sin_m, cos_d, sin_d = rowwise("rope_tables", tables_fn, [pos], [f_mla.reshape(1, LANES), f_dil.reshape(1, LANES)],
                                         [(LANES, F32)] * 4)
    tables = [cos_m, sin_m, cos_d, sin_d]
    H_M, H_D = ROPE // 2, DIL_DIM // 2

    def ln1_fn(rows, params):
        (xv,), (g, sc, sh) = rows, params
        y, _, _ = _rms(xv, g)
        return [y * (1.0 + sc) + sh], []

    (h,) = rowwise("ln1_fwd", ln1_fn, [x], [w["g_mix_norm"], sc1, sh1], [(D_MODEL, MXU_DTYPE)])
    w_in = fetch("w_in", h)
    proj = matmul("proj_fwd", h, w_in, "nt", tm=512)

    def post_fn(rows, params):
        (pv, cm, sm, cd, sd), (gq, gkv, gkp, gdq, gdk) = rows, params
        kper = _rope(_grms(pv[:, P_KPE:P_QD], gkp, KPE_GROUPS)[0], cm, sm, H_M)
        qd = [_rope(_grms(c, gdq, DIL_GROUPS)[0], cd, sd, H_D) for c in _chunks(pv[:, P_QD:P_KD])]
        kd = [_rope(_grms(c, gdk, DIL_GROUPS)[0], cd, sd, H_D) for c in _chunks(pv[:, P_KD:P_VD])]
        return [_rms(pv[:, P_QLAT:P_KVLAT], gq)[0], _rms(pv[:, P_KVLAT:P_KPE], gkv)[0], kper,
                jnp.concatenate(qd, axis=1), jnp.concatenate(kd, axis=1)], []

    post_params = [w["g_q_lat"], w["g_kv_lat"], g_kpe, g_dq, g_dk]
    qln, kvn, kper, qd_r, kd_r = rowwise(
        "proj_post", post_fn, [proj] + tables, post_params,
        [(Q_LORA, MXU_DTYPE), (KV_LORA, MXU_DTYPE), (LANES, MXU_DTYPE)] + [(DIL_WIDTH, F32)] * 2, tm=256)
    w_q_b, w_kv_b = fetch("w_q_b", qln), fetch("w_kv_b", kvn)
    q = matmul("q_fwd", qln, w_q_b, "nt", tm=1024)
    kv = matmul("kv_fwd", kvn, w_kv_b, "nn", tm=1024)

    def mla_prep_fn(rows, params):
        (qv, kvv, kp, cm, sm), (gq, gk) = rows, params
        value_lanes = _lane(kp.shape) >= NOPE
        qs, ks, vs = [], [], []
        for qc, kc in zip(_chunks(qv), _chunks(kvv), strict=True):
            qs.append(_rope(_grms(qc, gq, Q_GROUPS)[0], cm, sm, H_M))
            ks.append(_grms(kc, gk, K_GROUPS)[0] + kp)
            vs.append(jnp.where(value_lanes, kc, 0.0))
        return [jnp.concatenate(t, axis=1) for t in (qs, ks, vs)], []

    q_mla, k_mla, v_mla = rowwise("mla_prep", mla_prep_fn, [q, kv, kper, cos_m, sin_m], [g_q, g_k],
                                  [(HEADS * LANES, MXU_DTYPE)] * 3, tm=256)
    mla_scale = (NOPE + ROPE) ** -0.5
    o_cat, lse_mla = mla_fwd("mla_fwd", q_mla, k_mla, v_mla, mla_scale)

    band = [band_fwd(f"band{dil}_fwd", qd_r, kd_r, proj, dil) for dil in DILATIONS]
    o_cat, lse_mix = combine_fwd("dil_combine", [b[0] for b in band], [b[1] for b in band], o_cat)
    w_o = fetch("w_o", o_cat)
    mix = matmul("mix_fwd", o_cat, w_o, "nn", tm=512)

    def mid_fn(rows, params):
        (xv, mx), (gate1, g, sc, sh) = rows, params
        x1 = xv + gate1 * mx
        y, _, _ = _rms(x1, g)
        return [x1, y * (1.0 + sc) + sh], []

    x1, h2 = rowwise("mid_fwd", mid_fn, [x, mix], [g1, w["g_ffn_norm"], sc2, sh2], [(D_MODEL, F32), (D_MODEL, MXU_DTYPE)])
    w_up, w_conv, w_down = fetch("w_up", h2), fetch("w_conv", h2), fetch("w_down", h2)
    dn = ffn_fwd("ffn_fwd", h2, w_up, w_conv, b_conv, w_down)

    def final_fn(rows, params):
        (x1v, dnv, tgt), (gate2,) = rows, params
        r = x1v + gate2 * dnv - tgt
        dy = r * (1.0 / D_MODEL)
        loss = jnp.sum(_colsum(r * r), axis=-1, keepdims=True) * (0.5 / D_MODEL)
        return [dy, gate2 * dy], [loss, _colsum(dy * dnv)]

    dy, d_dn, loss, dg2 = rowwise("loss_head", final_fn, [x1, dn, target], [g2], [(D_MODEL, F32), (D_MODEL, MXU_DTYPE)],
                                  [1, D_MODEL])
    dh2, g_up_gate, g_up_val, g_down, g_w_conv, g_b_conv = ffn_bwd("ffn_bwd", h2, w_up, w_conv, b_conv, d_dn, w_down)
    emit("w_down", g_down)
    emit("w_conv", g_w_conv)
    sent = emit("w_up", jnp.concatenate([g_up_gate, g_up_val], axis=0))

    def mid_bwd_fn(rows, params):
        (dh2v, dyv, x1v, mx), (gate1, g, sc) = rows, params
        yn, n, rstd = _rms(x1v, g)
        dx_n, dg = _rms_bwd(dh2v * (1.0 + sc), n, rstd, g)
        dx1 = dyv + dx_n
        return [dx1, gate1 * dx1], [dg, _colsum(dh2v * yn), _colsum(dh2v), _colsum(dx1 * mx)]

    dx1, dmix, dg_ffn, dsc2, dsh2, dg1 = rowwise(
        "mid_bwd", mid_bwd_fn, [dh2, dy, x1, mix], [g1, w["g_ffn_norm"], sc2], [(D_MODEL, F32), (D_MODEL, MXU_DTYPE)],
        [D_MODEL] * 4, dep=sent)

    sent = emit("w_o", matmul("mix_wgrad", o_cat, dmix, "tn", tm=512, out_dtype=MXU_DTYPE))
    do_cat = matmul("mix_dgrad", dmix, w_o, "nt", tm=512, dep=sent)
    dband = [band_bwd(f"band{dil}_bwd", qd_r, kd_r, proj, b[1], lse_mix, o_cat, do_cat, dil) for dil, b in zip(DILATIONS, band)]
    dq_mla, dkv_mla, dkper = mla_bwd("mla_bwd", q_mla, k_mla, v_mla, o_cat, do_cat, lse_mla, mla_scale)

    def mla_prep_bwd_fn(rows, params):
        (dqv, dkvv, qv, kvv, cm, sm), (gq, gk) = rows, params
        nope_lanes = _lane(cm.shape) < NOPE
        dqs, dkvs, dgq, dgk = [], [], 0.0, 0.0
        for dqc, dkc, qc, kc in zip(_chunks(dqv), _chunks(dkvv), _chunks(qv), _chunks(kvv), strict=True):
            _, n, rstd = _grms(qc, gq, Q_GROUPS)
            dx, dg = _grms_bwd(_rope_bwd(dqc, cm, sm, H_M), n, rstd, gq, Q_GROUPS)
            dqs.append(dx)
            dgq = dgq + dg
            _, n, rstd = _grms(kc, gk, K_GROUPS)
            dx, dg = _grms_bwd(dkc, n, rstd, gk, K_GROUPS)
            dkvs.append(jnp.where(nope_lanes, dx, dkc))
            dgk = dgk + dg
        return [jnp.concatenate(dqs, axis=1), jnp.concatenate(dkvs, axis=1)], [dgq, dgk]

    dq, dkv, dg_q, dg_k = rowwise("mla_prep_bwd", mla_prep_bwd_fn, [dq_mla, dkv_mla, q, kv, cos_m, sin_m], [g_q, g_k],
                                  [(HEADS * LANES, MXU_DTYPE)] * 2, [LANES, LANES], tm=256)
    emit("w_q_b", matmul("q_wgrad", dq, qln, "tn", out_dtype=MXU_DTYPE))
    emit("w_kv_b", matmul("kv_wgrad", kvn, dkv, "tn", out_dtype=MXU_DTYPE))
    dqln = matmul("q_dgrad", dq, w_q_b, "nn", tm=1024)
    dkvn = matmul("kv_dgrad", dkv, w_kv_b, "nt", tm=1024)

    def pre_bwd_fn(rows, params):
        dql, dkvl, dkp = rows[0:3]
        dqd_, dkd_, dvd_ = [rows[3 + 3 * i] + rows[4 + 3 * i] + rows[5 + 3 * i] for i in range(3)]
        pv, cm, sm, cd, sd = rows[12:]
        gq, gkv, gkp, gdq, gdk = params
        r_q = _norm_bwd(dql, pv[:, P_QLAT:P_KVLAT], gq)
        r_kv = _norm_bwd(dkvl, pv[:, P_KVLAT:P_KPE], gkv)
        _, n, rstd = _grms(pv[:, P_KPE:P_QD], gkp, KPE_GROUPS)
        r_kp = _grms_bwd(_rope_bwd(dkp, cm, sm, H_M), n, rstd, gkp, KPE_GROUPS)
        outs, dgs = [r_q[0], r_kv[0], r_kp[0]], []
        for dval, lo, g in ((dqd_, P_QD, gdq), (dkd_, P_KD, gdk)):
            dg_sum = 0.0
            for dc, xc in zip(_chunks(dval), _chunks(pv[:, lo:lo + DIL_WIDTH]), strict=True):
                _, n, rstd = _grms(xc, g, DIL_GROUPS)
                dx, dg = _grms_bwd(_rope_bwd(dc, cd, sd, H_D), n, rstd, g, DIL_GROUPS)
                outs.append(dx)
                dg_sum = dg_sum + dg
            dgs.append(dg_sum)
        return [jnp.concatenate(outs + [dvd_], axis=1)], [r_q[1], r_kv[1], r_kp[1]] + dgs

    dproj, dg_q_lat, dg_kv_lat, dg_kpe, dg_dq, dg_dk = rowwise(
        "proj_pre_bwd", pre_bwd_fn,
        [dqln, dkvn, dkper] + [d[i] for i in range(3) for d in dband] + [proj] + tables, post_params,
        [(P_END, MXU_DTYPE)], [Q_LORA, KV_LORA, LANES, LANES, LANES], tm=256)
    sent = emit("w_in", matmul("proj_wgrad", dproj, h, "tn", tn=512, out_dtype=MXU_DTYPE))
    dh = matmul("proj_dgrad", dproj, w_in, "nn", tm=512, dep=sent)

    def ln1_bwd_fn(rows, params):
        (dhv, dres, xv), (g, sc) = rows, params
        yn, n, rstd = _rms(xv, g)
        dx_n, dg = _rms_bwd(dhv * (1.0 + sc), n, rstd, g)
        return [dres + dx_n], [dg, _colsum(dhv * yn), _colsum(dhv)]

    grad_x, dg_mix, dsc1, dsh1 = rowwise("ln1_bwd", ln1_bwd_fn, [dh, dx1, x], [w["g_mix_norm"], sc1], [(D_MODEL, F32)],
                                         [D_MODEL] * 3)
    dmod = jnp.concatenate([dsh1, dsc1, dg1, dsh2, dsc2, dg2], axis=-1)
    small = {"loss": loss, "b_ada": dmod, "g_mix_norm": dg_mix, "g_q_lat": dg_q_lat, "g_kv_lat": dg_kv_lat,
             "g_mla_q_nope": dg_q[:, :NOPE], "g_mla_q_pe": dg_q[:, NOPE:NOPE + ROPE], "g_mla_k_nope": dg_k[:, :NOPE],
             "g_mla_k_pe": dg_kpe[:, KPE_LO:KPE_LO + ROPE], "g_dil_q": dg_dq[:, :DIL_DIM] + dg_dq[:, DIL_DIM:],
             "g_dil_k": dg_dk[:, :DIL_DIM] + dg_dk[:, DIL_DIM:], "g_ffn_norm": dg_ffn,
             "b_conv": g_b_conv}
    return grad_x, small


COL_SHARDED = ("w_kv_b", "w_conv")
ROW_SHARDED = ("w_o", "w_down") + TRANSPOSED
ADAM_TILE = {"w_ada": 256, "w_up": 176, "w_down": 176}
GATHER_GROUPS = (("w_in",), ("w_q_b", "w_kv_b"), ("w_o", "w_up", "w_conv", "w_down"))
SCATTER_GROUPS = (("w_down", "w_conv", "w_up"), ("w_o",), ("w_q_b", "w_kv_b", "w_in"))
OUT_WEIGHTS = ("w_ada", "b_ada", "g_mix_norm", "w_in", "g_q_lat", "w_q_b", "g_kv_lat", "w_kv_b", "g_mla_q_nope", "g_mla_q_pe",
               "g_mla_k_nope", "g_mla_k_pe", "g_dil_q", "g_dil_k", "w_o", "g_ffn_norm", "w_up", "w_conv", "b_conv", "w_down")


def kernel(x, c, positions, w_ada, b_ada, g_mix_norm, w_in, g_q_lat, w_q_b, g_kv_lat, w_kv_b, g_mla_q_nope, g_mla_q_pe, g_mla_k_nope, g_mla_k_pe, g_dil_q, g_dil_k, w_o, g_ffn_norm, w_up, w_conv, b_conv, w_down, loss_target, m_w_ada, m_b_ada, m_g_mix_norm, m_w_in, m_g_q_lat, m_w_q_b, m_g_kv_lat, m_w_kv_b, m_g_mla_q_nope, m_g_mla_q_pe, m_g_mla_k_nope, m_g_mla_k_pe, m_g_dil_q, m_g_dil_k, m_w_o, m_g_ffn_norm, m_w_up, m_w_conv, m_b_conv, m_w_down, v_w_ada, v_b_ada, v_g_mix_norm, v_w_in, v_g_q_lat, v_w_q_b, v_g_kv_lat, v_w_kv_b, v_g_mla_q_nope, v_g_mla_q_pe, v_g_mla_k_nope, v_g_mla_k_pe, v_g_dil_q, v_g_dil_k, v_w_o, v_g_ffn_norm, v_w_up, v_w_conv, v_b_conv, v_w_down):
    args = dict(locals())
    xi, yi, ci = _place()
    me = 4 * xi + 2 * yi + ci
    def local(prefix, n):
        a = args[prefix + n][0]
        return a.T if n in TRANSPOSED else a

    shard = {n: local("", n) for n in COL_SHARDED + ROW_SHARDED + ("w_ada",)}
    small_w = {n: args[n] for n in SMALL_PARAMS}

    (c_all,) = all_gather("gather_c", [c])
    (sc_all,) = rowwise("silu_c", lambda rows, params: ([_silu(rows[0])], []), [c_all.reshape(N_DEV, D_MODEL)], [],
                        [(D_MODEL, MXU_DTYPE)])
    mod_part = matmul("ada_fwd", sc_all, shard["w_ada"], "nn")
    (mod_all,) = all_gather("gather_mod", [mod_part])

    payload = {n: shard[n] if n == "w_conv" else shard[n].astype(MXU_DTYPE) for n in COL_SHARDED + ROW_SHARDED}
    gathers, after_start = [], mod_all
    for i, grp in enumerate(GATHER_GROUPS):
        gathers.append(exchange_start(f"gather{i}_start", [payload[n] for n in grp], gather=True, after=after_start))
        after_start = gathers[-1][-1]
    full = {}

    def fetch(name, after):
        if name not in full:
            (i, grp), = [(i, grp) for i, grp in enumerate(GATHER_GROUPS) if name in grp]
            srcs, lands = exchange_wait(f"gather{i}_wait", gathers[i], True, after)
            for n, src, land in zip(grp, srcs, lands, strict=True):
                stack = lax.dynamic_update_index_in_dim(land, src, me, 0)
                full[n] = to_kernel_layout(n, _gather_cols(stack) if n in COL_SHARDED else _gather_rows(stack))
        return full[name]

    mod_row = lax.dynamic_index_in_dim(mod_all, me, axis=1, keepdims=False).reshape(1, 6 * D_MODEL)
    (mod,) = rowwise("ada_bias", lambda rows, params: ([rows[0] + rows[1]], []), [mod_row, b_ada], [], [(6 * D_MODEL, F32)],
                     dep=after_start)

    own, pending, scatters = {}, {}, {}

    def emit(name, grad):
        grad = from_kernel_layout(name, grad)
        parts = _scatter_cols(grad) if name in COL_SHARDED else _scatter_rows(grad)
        own[name] = lax.dynamic_index_in_dim(parts, me, 0, keepdims=False)
        pending[name] = parts
        for i, grp in enumerate(SCATTER_GROUPS):
            if name == grp[-1]:
                scatters[i] = exchange_start(f"scatter{i}_start", [pending[n] for n in grp], gather=False)
                return scatters[i][-1]
        return None

    pos = positions.reshape(SEQ, 1).astype(F32)
    grad_x, small = _local_step(x[0], pos, mod, loss_target[0], small_w, fetch, emit)

    res, done = {}, grad_x
    for i, grp in enumerate(SCATTER_GROUPS):
        _, lands = exchange_wait(f"scatter{i}_wait", scatters[i], False, done)
        for n, land in zip(grp, lands, strict=True):
            res[n] = adamw(f"adamw_{n}", shard[n], [own[n], land], local("m_", n), local("v_", n), ADAM_TILE.get(n))
            done = res[n][0]
            if n in TRANSPOSED:
                res[n] = [r.T for r in res[n]]
    (small_all,) = all_gather("gather_small", [_pack_small(small)], after=done)
    loss, small_res = adamw_small("adamw_small", small_all, {n: (args[n], args["m_" + n], args["v_" + n]) for n in SMALL_PARAMS})
    row, _, n_mod = SMALL_AT["b_ada"]
    dmod_all = small_all[:, row:row + n_mod // SMALL_COLS, :].reshape(N_DEV, n_mod)
    dmod_mine = lax.dynamic_slice_in_dim(dmod_all, me * (6 * D_MODEL // N_DEV), 6 * D_MODEL // N_DEV, axis=1)
    g_w_ada = matmul("ada_wgrad", sc_all, dmod_mine, "tn")
    res["w_ada"] = adamw("adamw_w_ada", shard["w_ada"], [g_w_ada], m_w_ada[0], v_w_ada[0], ADAM_TILE["w_ada"])

    def leaf(kind, n):
        if n in res:
            return res[n][kind][None]
        return small_res[n][kind]

    return (loss.reshape(()), grad_x[None], *[leaf(k, n) for k in range(4) for n in OUT_WEIGHTS])
```

```python
import jax
import jax.numpy as jnp
from jax import lax
from jax.experimental import pallas as pl
from jax.experimental.pallas import tpu as pltpu

F32 = jnp.float32
MXU_DTYPE = jnp.bfloat16

N_DEV = 8
D_MODEL = 1024
SEQ = 2048
HEADS = 8
NOPE = 64
ROPE = 32
Q_LORA = 512
KV_LORA = 256
DIL_DIM = 64
DIL_WIDTH = HEADS * DIL_DIM
DILATIONS = (1, 4, 16)
SPAN = 128
D_FF = 2816
LANES = 128
ROPE_THETA = 10000.0
EPS = 1e-6
NEG_INF = -1e30
ADAM_LR, ADAM_B1, ADAM_B2, ADAM_EPS, ADAM_WD, ADAM_STEP = 0.001, 0.9, 0.999, 1e-08, 0.01, 10
VMEM_LIMIT = 56 * 1024 * 1024
MESH_ID = pl.DeviceIdType.MESH

P_QLAT, P_KVLAT, P_KPE, P_QD, P_KD, P_VD, P_END = 0, 512, 768, 896, 1408, 1920, 2432
KPE_LO = 64
MIX_IN = HEADS * LANES + DIL_WIDTH


def _params(**kw):
    return pltpu.CompilerParams(vmem_limit_bytes=VMEM_LIMIT, **kw)


def rowwise(name, fn, rows, params, out_rows, out_accs=(), tm=512, dep=None):
    deps = [] if dep is None else [dep]
    rows = [r if isinstance(r, tuple) else (r, r.shape[1], 0) for r in rows]
    R = rows[0][0].shape[0]
    tm = min(tm, R)
    steps = R // tm
    assert steps * tm == R
    in_specs = []
    for a, width, cb in rows:
        ri = a.shape[0]
        per = ri // tm
        assert per * tm == ri
        if ri == R:
            in_specs.append(pl.BlockSpec((tm, width), lambda i, cb=cb: (i, cb)))
        else:
            in_specs.append(pl.BlockSpec((tm, width), lambda i, per=per, cb=cb: (i % per, cb)))
    for p in params:
        in_specs.append(pl.BlockSpec(p.shape, lambda i: (0,) * p.ndim))
    in_specs += [pl.BlockSpec(memory_space=pl.ANY)] * len(deps)
    out_shape = [jax.ShapeDtypeStruct((R, d), dt) for d, dt in out_rows]
    out_specs = [pl.BlockSpec((tm, d), lambda i: (i, 0)) for d, _ in out_rows]
    out_shape += [jax.ShapeDtypeStruct((1, n), F32) for n in out_accs]
    out_specs += [pl.BlockSpec((1, n), lambda i: (0, 0)) for n in out_accs]
    nr, npar, no, na = len(rows), len(params), len(out_rows), len(out_accs)

    def body(*refs):
        rvals = [r[...] for r in refs[:nr]]
        pvals = [r[...] for r in refs[nr:nr + npar]]
        outs, accs = fn(rvals, pvals)
        first_out = nr + npar + len(deps)
        for ref, v in zip(refs[first_out:first_out + no], outs, strict=True):
            ref[...] = v.astype(ref.dtype)
        if na:
            acc_refs = refs[first_out + no:]
            i = pl.program_id(0)

            @pl.when(i == 0)
            def _():
                for ref, v in zip(acc_refs, accs, strict=True):
                    ref[...] = v

            @pl.when(i > 0)
            def _():
                for ref, v in zip(acc_refs, accs, strict=True):
                    ref[...] += v

    res = pl.pallas_call(body, name=name, grid=(steps,), in_specs=in_specs, out_specs=out_specs,
                         out_shape=out_shape, compiler_params=_params())(*[r[0] for r in rows], *params, *deps)
    return list(res)


_DIMS = {"nn": ((1,), (0,)), "nt": ((1,), (1,)), "tn": ((0,), (0,))}


def _dot(a, b, mode="nn"):
    return lax.dot_general(a.astype(MXU_DTYPE), b.astype(MXU_DTYPE), (_DIMS[mode], ((), ())),
                           preferred_element_type=F32)


def matmul(name, a, b, mode, tm=None, tn=None, tk=None, out_dtype=F32, dep=None, a_mmap=None, b_nmap=None, b_kmap=None):
    if mode == "tn":
        K, M = a.shape
    else:
        M, K = a.shape
    N = b.shape[0] if mode == "nt" else b.shape[1]
    tm, tn, tk = tm or M, tn or N, tk or K
    nm, nn, nk = M // tm, N // tn, K // tk
    assert nm * tm == M and nn * tn == N and nk * tk == K
    same = lambda idx: idx
    a_mmap, b_nmap, b_kmap = a_mmap or same, b_nmap or same, b_kmap or same
    if mode == "tn":
        a_spec = pl.BlockSpec((tk, tm), lambda i, j, k: (k, a_mmap(i)))
    else:
        a_spec = pl.BlockSpec((tm, tk), lambda i, j, k: (a_mmap(i), k))
    if mode == "nt":
        b_spec = pl.BlockSpec((tn, tk), lambda i, j, k: (b_nmap(j), b_kmap(k)))
    else:
        b_spec = pl.BlockSpec((tk, tn), lambda i, j, k: (b_kmap(k), b_nmap(j)))
    deps = [] if dep is None else [dep]

    def body(a_ref, b_ref, *rest):
        o_ref, scratch = rest[len(deps)], rest[len(deps) + 1:]
        p = _dot(a_ref[...], b_ref[...], mode)
        if nk == 1:
            o_ref[...] = p.astype(o_ref.dtype)
        else:
            acc = scratch[0]
            k = pl.program_id(2)

            @pl.when(k == 0)
            def _():
                acc[...] = p

            @pl.when(k > 0)
            def _():
                acc[...] += p

            @pl.when(k == nk - 1)
            def _():
                o_ref[...] = acc[...].astype(o_ref.dtype)

    return pl.pallas_call(
        body, name=name, grid=(nm, nn, nk), in_specs=[a_spec, b_spec] + [pl.BlockSpec(memory_space=pl.ANY)] * len(deps),
        out_specs=pl.BlockSpec((tm, tn), lambda i, j, k: (i, j)),
        out_shape=jax.ShapeDtypeStruct((M, N), out_dtype),
        scratch_shapes=[pltpu.VMEM((tm, tn), F32)] if nk > 1 else [],
        compiler_params=_params())(a, b, *deps)


def _rms(x, g):
    rstd = lax.rsqrt(jnp.mean(x * x, axis=-1, keepdims=True) + EPS)
    n = x * rstd
    return n * g, n, rstd


def _rms_bwd(dy, n, rstd, g):
    dg = jnp.sum(dy * n, axis=0, keepdims=True)
    dn = dy * g
    dx = rstd * (dn - n * jnp.mean(dn * n, axis=-1, keepdims=True))
    return dx, dg


def _norm_bwd(dy, x, g):
    _, n, rstd = _rms(x, g)
    return _rms_bwd(dy, n, rstd, g)


def _colsum(v):
    return jnp.sum(v, axis=0, keepdims=True)


def _silu(x):
    return x * (1.0 / (1.0 + jnp.exp(-x)))


def _lane(shape):
    return lax.broadcasted_iota(jnp.int32, shape, 1)


def _group_mean(v, groups):
    lane = _lane(v.shape)
    out = jnp.zeros_like(v)
    for lo, hi in groups:
        m = (lane >= lo) & (lane < hi)
        out = jnp.where(m, jnp.sum(jnp.where(m, v, 0.0), axis=-1, keepdims=True) * (1.0 / (hi - lo)), out)
    return out


def _in_groups(shape, groups):
    lane = _lane(shape)
    m = jnp.zeros(shape, jnp.bool_)
    for lo, hi in groups:
        m = m | ((lane >= lo) & (lane < hi))
    return m


def _grms(x, g, groups):
    rstd = lax.rsqrt(_group_mean(x * x, groups) + EPS)
    n = jnp.where(_in_groups(x.shape, groups), x * rstd, 0.0)
    return n * g, n, rstd


def _grms_bwd(dy, n, rstd, g, groups):
    dn = dy * g
    return rstd * (dn - n * _group_mean(dn * n, groups)), _colsum(dy * n)


def _rot(x, half, transpose=False):
    first = (_lane(x.shape) % (2 * half)) < half
    up = pltpu.roll(x, LANES - half, axis=1)
    down = pltpu.roll(x, half, axis=1)
    return jnp.where(first, up, -down) if transpose else jnp.where(first, -up, down)


def _rope(x, cos, sin, half):
    return x * cos + _rot(x, half) * sin


def _rope_bwd(dy, cos, sin, half):
    return dy * cos + _rot(dy * sin, half, transpose=True)


def _chunks(x):
    return [x[:, i:i + LANES] for i in range(0, x.shape[1], LANES)]


Q_GROUPS = ((0, NOPE), (NOPE, NOPE + ROPE))
K_GROUPS = ((0, NOPE),)
KPE_GROUPS = ((KPE_LO, KPE_LO + ROPE),)
DIL_GROUPS = ((0, DIL_DIM), (DIL_DIM, 2 * DIL_DIM))


def _col(width, rows=SEQ):
    return pl.BlockSpec((rows, width), lambda h: (0, h))


def _causal_tail(s, tq, fill):
    diag = s[:, s.shape[1] - tq:]
    keep = lax.broadcasted_iota(jnp.int32, diag.shape, 1) <= lax.broadcasted_iota(jnp.int32, diag.shape, 0)
    diag = jnp.where(keep, diag, fill)
    return diag if s.shape[1] == tq else jnp.concatenate([s[:, :s.shape[1] - tq], diag], axis=1)


def mla_fwd(name, q, k, v, scale, tq=256):
    S = q.shape[0]

    def body(q_ref, k_ref, v_ref, o_ref, lse_ref):
        nb = S // tq
        blk = lambda i: slice(i * tq, (i + 1) * tq)

        def scores(i):
            return _dot(q_ref[blk(i), :], k_ref[:(i + 1) * tq, :], "nt")

        def softmax(i, s):
            s = _causal_tail(s * scale, tq, NEG_INF)
            m = jnp.max(s, axis=-1, keepdims=True)
            e = jnp.exp(s - m)
            l = jnp.sum(e, axis=-1, keepdims=True)
            lse_ref[0, blk(i), :] = m + jnp.log(l)
            return (e * (1.0 / l)).astype(MXU_DTYPE)

        def weighted(i, p):
            o_ref[blk(i), :] = _dot(p, v_ref[:(i + 1) * tq, :])

        s, p_prev = scores(0), None
        for i in range(nb):
            s_next = scores(i + 1) if i + 1 < nb else None
            if p_prev is not None:
                weighted(i - 1, p_prev)
            p_prev, s = softmax(i, s), s_next
        weighted(nb - 1, p_prev)

    return pl.pallas_call(
        body, name=name, grid=(HEADS,), in_specs=[_col(LANES)] * 3,
        out_specs=[_col(LANES), pl.BlockSpec((1, S, 1), lambda h: (h, 0, 0))],
        out_shape=[jax.ShapeDtypeStruct((S, MIX_IN), F32), jax.ShapeDtypeStruct((HEADS, S, 1), F32)],
        compiler_params=_params())(q, k, v)


def mla_bwd(name, q, k, v, o, do, lse, scale, tq=256):
    S = q.shape[0]

    def body(q_ref, k_ref, v_ref, o_ref, do_ref, lse_ref, dq_ref, dkv_ref, dkpe_ref, dk_acc, dv_acc):
        dk_acc[...] = jnp.zeros_like(dk_acc)
        dv_acc[...] = jnp.zeros_like(dv_acc)
        for i in range(S // tq):
            kext = (i + 1) * tq
            blk = slice(i * tq, kext)
            qi, kk, vv = q_ref[blk, :], k_ref[:kext, :], v_ref[:kext, :]
            doi = do_ref[blk, :]
            s = _causal_tail(_dot(qi, kk, "nt") * scale, tq, NEG_INF)
            p = jnp.exp(s - lse_ref[0, blk, :])
            dp = _dot(doi, vv, "nt")
            delta = jnp.sum(doi * o_ref[blk, :], axis=-1, keepdims=True)
            ds = p * (dp - delta) * scale
            dq_ref[blk, :] = _dot(ds, kk)
            dk_acc[:kext, :] += _dot(ds, qi, "tn")
            dv_acc[:kext, :] += _dot(p, doi, "tn")
        dk = dk_acc[...]
        lane = _lane(dk.shape)
        dkv_ref[...] = jnp.where(lane < NOPE, dk, 0.0) + dv_acc[...]
        dkpe = jnp.where((lane >= KPE_LO) & (lane < KPE_LO + ROPE), dk, 0.0)
        h = pl.program_id(0)

        @pl.when(h == 0)
        def _():
            dkpe_ref[...] = dkpe

        @pl.when(h > 0)
        def _():
            dkpe_ref[...] += dkpe

    return pl.pallas_call(
        body, name=name, grid=(HEADS,),
        in_specs=[_col(LANES)] * 5 + [pl.BlockSpec((1, S, 1), lambda h: (h, 0, 0))],
        out_specs=[_col(LANES), _col(LANES), pl.BlockSpec((S, LANES), lambda h: (0, 0))],
        out_shape=[jax.ShapeDtypeStruct((S, HEADS * LANES), F32), jax.ShapeDtypeStruct((S, HEADS * LANES), F32),
                   jax.ShapeDtypeStruct((S, LANES), F32)],
        scratch_shapes=[pltpu.VMEM((S, LANES), F32), pltpu.VMEM((S, LANES), F32)],
        compiler_params=_params())(q, k, v, o, do, lse)


BAND_TQ = SPAN


def _band_blocks(L, tq):
    return [(i * tq, (i + 1) * tq, max(0, i * tq - SPAN)) for i in range(L // tq)]


def _band_mask(q0, q1, k0):
    shape = (q1 - q0, q1 - k0)
    dist = (lax.broadcasted_iota(jnp.int32, shape, 0) + q0) - (lax.broadcasted_iota(jnp.int32, shape, 1) + k0)
    return (dist >= 0) & (dist <= SPAN)


def _class_rows(r, dil, lo, hi):
    return pl.ds(r + dil * lo, hi - lo, stride=dil) if dil > 1 else pl.ds(lo, hi - lo)


def _stack_heads(t, lo):
    zero = jnp.zeros_like(t)
    return jnp.concatenate([jnp.where(lo, t, zero), jnp.where(lo, zero, t)], axis=0)


def _band_mask2(q0, q1, k0):
    n = q1 - q0
    shape = (2 * n, q1 - k0)
    i = lax.broadcasted_iota(jnp.int32, shape, 0)
    dist = (jnp.where(i >= n, i - n, i) + q0) - (lax.broadcasted_iota(jnp.int32, shape, 1) + k0)
    return (dist >= 0) & (dist <= SPAN)


def _pair_col(col0=0):
    return pl.BlockSpec((SEQ, LANES), lambda j: (0, col0 // LANES + j))


def band_fwd(name, q, k, v, dil):
    S = q.shape[0]
    L = S // dil
    tq = BAND_TQ
    scale = DIL_DIM ** -0.5

    def body(q_ref, k_ref, v_ref, o_ref, lse_ref):
        items = [(r, blk) for r in range(dil) for blk in _band_blocks(L, tq)]
        lo = _lane((tq, LANES)) < DIL_DIM

        def scores(item):
            r, (q0, q1, k0) = item
            qb = q_ref[_class_rows(r, dil, q0, q1), :].astype(MXU_DTYPE)
            return _dot(_stack_heads(qb, lo), k_ref[_class_rows(r, dil, k0, q1), :], "nt")

        def softmax(item, s):
            _, (q0, q1, k0) = item
            s = jnp.where(_band_mask2(q0, q1, k0), s * scale, NEG_INF)
            mx = jnp.max(s, axis=-1, keepdims=True)
            e = jnp.exp(s - mx)
            l = jnp.sum(e, axis=-1, keepdims=True)
            return (e * (1.0 / l)).astype(MXU_DTYPE), mx + jnp.log(l)

        def weighted(item, p, lse):
            r, (q0, q1, k0) = item
            pv = _dot(p, v_ref[_class_rows(r, dil, k0, q1), :])
            o_ref[_class_rows(r, dil, q0, q1), :] = jnp.where(lo, pv[:tq], pv[tq:])
            lse_ref[_class_rows(r, dil, q0, q1), :] = jnp.where(lo, lse[:tq], lse[tq:])

        s, prev = scores(items[0]), None
        for i, item in enumerate(items):
            s_next = scores(items[i + 1]) if i + 1 < len(items) else None
            if prev is not None:
                weighted(items[i - 1], *prev)
            prev, s = softmax(item, s), s_next
        weighted(items[-1], *prev)

    return pl.pallas_call(
        body, name=name, grid=(DIL_WIDTH // LANES,), in_specs=[_pair_col()] * 2 + [_pair_col(P_VD)], out_specs=[_pair_col()] * 2,
        out_shape=[jax.ShapeDtypeStruct((S, DIL_WIDTH), F32)] * 2, compiler_params=_params())(q, k, v)


def band_bwd(name, q, k, v, lse, lse_mix, o_cat, do_cat, dil):
    S = q.shape[0]
    L = S // dil
    tq = BAND_TQ
    scale = DIL_DIM ** -0.5

    def body(q_ref, k_ref, v_ref, lse_ref, mix_ref, o_ref, do_ref, dq_ref, dk_ref, dv_ref):
        dk_ref[...] = jnp.zeros_like(dk_ref)
        dv_ref[...] = jnp.zeros_like(dv_ref)
        items = [(r, blk) for r in range(dil) for blk in _band_blocks(L, tq)]
        lo = _lane((tq, LANES)) < DIL_DIM
        per_head = lambda t: jnp.concatenate([t[:, 0:1], t[:, DIL_DIM:DIL_DIM + 1]], axis=0)

        def scores(item):
            r, (q0, q1, k0) = item
            qrows, krows = _class_rows(r, dil, q0, q1), _class_rows(r, dil, k0, q1)
            lse_p, dout = lse_ref[qrows, :], do_ref[qrows, :]
            w2 = per_head(jnp.exp(lse_p - mix_ref[qrows, :]))
            dd = dout * o_ref[qrows, :]
            big_d = jnp.concatenate([jnp.sum(jnp.where(lo, dd, 0.0), axis=-1, keepdims=True),
                                     jnp.sum(jnp.where(lo, 0.0, dd), axis=-1, keepdims=True)], axis=0)
            q2 = _stack_heads(q_ref[qrows, :].astype(MXU_DTYPE), lo)
            dom = (_stack_heads(dout, lo) * w2).astype(MXU_DTYPE)
            return (_dot(q2, k_ref[krows, :], "nt"), _dot(dom, v_ref[krows, :], "nt"), per_head(lse_p), w2 * big_d, q2, dom)

        def softmax_bwd(item, s, dp, lse2, wd2, q2, dom):
            _, (q0, q1, k0) = item
            p = jnp.where(_band_mask2(q0, q1, k0), jnp.exp(s * scale - lse2), 0.0)
            return p.astype(MXU_DTYPE), (p * (dp - wd2) * scale).astype(MXU_DTYPE), q2, dom

        def grads(item, p, ds, q2, dom):
            r, (q0, q1, k0) = item
            qrows, krows = _class_rows(r, dil, q0, q1), _class_rows(r, dil, k0, q1)
            dq2 = _dot(ds, k_ref[krows, :])
            dq_ref[qrows, :] = jnp.where(lo, dq2[:tq], dq2[tq:])
            dk_ref[krows, :] += _dot(ds, q2, "tn")
            dv_ref[krows, :] += _dot(p, dom, "tn")

        sc, prev = scores(items[0]), None
        for i, item in enumerate(items):
            sc_next = scores(items[i + 1]) if i + 1 < len(items) else None
            if prev is not None:
                grads(items[i - 1], *prev)
            prev, sc = softmax_bwd(item, *sc), sc_next
        grads(items[-1], *prev)

    cat = _pair_col(HEADS * LANES)
    return pl.pallas_call(
        body, name=name, grid=(DIL_WIDTH // LANES,),
        in_specs=[_pair_col()] * 2 + [_pair_col(P_VD)] + [_pair_col()] * 2 + [cat] * 2, out_specs=[_pair_col()] * 3,
        out_shape=[jax.ShapeDtypeStruct((S, DIL_WIDTH), F32)] * 3,
        compiler_params=_params())(q, k, v, lse, lse_mix, o_cat, do_cat)


def combine_fwd(name, outs, lses, o_cat, tm=512):
    S = outs[0].shape[0]

    def body(o1, o2, o3, l1, l2, l3, cat_in, cat_out, mix_ref):
        ls = [l1[...], l2[...], l3[...]]
        m = jnp.maximum(jnp.maximum(ls[0], ls[1]), ls[2])
        e = [jnp.exp(l - m) for l in ls]
        den = e[0] + e[1] + e[2]
        cat_out[...] = (e[0] / den) * o1[...] + (e[1] / den) * o2[...] + (e[2] / den) * o3[...]
        mix_ref[...] = m + jnp.log(den)

    row = pl.BlockSpec((tm, DIL_WIDTH), lambda i: (i, 0))
    return pl.pallas_call(
        body, name=name, grid=(S // tm,), in_specs=[row] * 6 + [pl.BlockSpec(memory_space=pl.ANY)],
        out_specs=[pl.BlockSpec((tm, DIL_WIDTH), lambda i: (i, HEADS * LANES // DIL_WIDTH)), row],
        out_shape=[jax.ShapeDtypeStruct(o_cat.shape, F32), jax.ShapeDtypeStruct((S, DIL_WIDTH), F32)],
        input_output_aliases={6: 0}, compiler_params=_params())(*outs, *lses, o_cat)


def _shift_down(u, n, zero_head):
    out = pltpu.roll(u, n, axis=0)
    return jnp.where(lax.broadcasted_iota(jnp.int32, u.shape, 0) >= n, out, 0.0) if zero_head else out


def _shift_up(u, n, zero_tail):
    rows = u.shape[0]
    out = pltpu.roll(u, rows - n, axis=0)
    return jnp.where(lax.broadcasted_iota(jnp.int32, u.shape, 0) < rows - n, out, 0.0) if zero_tail else out


CONV_ROWS = 512
CONV_HALO = 16


def _conv_chunks(S, tail):
    out = []
    for r0 in range(0, S, CONV_ROWS):
        lo, hi = max(0, r0 - CONV_HALO), min(S, r0 + CONV_ROWS + (CONV_HALO if tail else 0))
        out.append((lo, hi, r0 - lo, CONV_ROWS))
    return out


CONV_TC = 256
CONV_NB = D_FF // CONV_TC


def _half_specs(rows, rows_axis=False):
    if rows_axis:
        return [pl.BlockSpec((rows, D_MODEL), lambda j: (j, 0)), pl.BlockSpec((rows, D_MODEL), lambda j: (j + CONV_NB, 0))]
    return [pl.BlockSpec((rows, CONV_TC), lambda j: (0, j)), pl.BlockSpec((rows, CONV_TC), lambda j: (0, j + CONV_NB))]


def _whole(a):
    return pl.BlockSpec(a.shape, lambda j: (0,) * a.ndim)


def _up_pair(h, ug_ref, uv_ref):
    return jnp.concatenate([_dot(h, ug_ref[...], "nt"), _dot(h, uv_ref[...], "nt")], axis=1)


def _conv_taps(uin, w, b, starts):
    u1, u2 = _shift_down(uin, 1, starts), _shift_down(uin, 2, starts)
    return u1, u2, w[2:3, :] * uin + w[1:2, :] * u1 + w[0:1, :] * u2 + b


def ffn_fwd(name, h, w_up_t, w_conv, b_conv, w_down):
    S = h.shape[0]

    def body(h_ref, ug_ref, uv_ref, wg_ref, wv_ref, bg_ref, bv_ref, wd_ref, dn_ref):
        @pl.when(pl.program_id(0) == 0)
        def _():
            dn_ref[...] = jnp.zeros_like(dn_ref)

        w = jnp.concatenate([wg_ref[...], wv_ref[...]], axis=1)
        b = jnp.concatenate([bg_ref[...], bv_ref[...]], axis=1)
        chunks = _conv_chunks(S, tail=False)

        def project(c):
            return _up_pair(h_ref[c[0]:c[1], :], ug_ref, uv_ref)

        def gate(c, uin):
            lo, hi, keep, rows = c
            u = _conv_taps(uin, w, b, lo == 0)[2][keep:keep + rows]
            return (_silu(u[:, :CONV_TC]) * u[:, CONV_TC:]).astype(MXU_DTYPE)

        def project_down(c, act):
            dn_ref[c[0] + c[2]:c[0] + c[2] + c[3], :] += _dot(act, wd_ref[...])

        uin, act_prev = project(chunks[0]), None
        for i, c in enumerate(chunks):
            uin_next = project(chunks[i + 1]) if i + 1 < len(chunks) else None
            if act_prev is not None:
                project_down(chunks[i - 1], act_prev)
            act_prev = gate(c, uin)
            uin = uin_next
        project_down(chunks[-1], act_prev)

    return pl.pallas_call(
        body, name=name, grid=(CONV_NB,),
        in_specs=[_whole(h)] + _half_specs(CONV_TC, rows_axis=True) + _half_specs(3) + _half_specs(1)
        + [pl.BlockSpec((CONV_TC, w_down.shape[1]), lambda j: (j, 0))],
        out_specs=pl.BlockSpec((S, w_down.shape[1]), lambda j: (0, 0)), out_shape=jax.ShapeDtypeStruct((S, w_down.shape[1]), F32),
        compiler_params=_params())(h, w_up_t, w_up_t, w_conv, w_conv, b_conv, b_conv, w_down)


def ffn_bwd(name, h, w_up_t, w_conv, b_conv, d_dn, w_down):
    S, D = h.shape

    def body(h_ref, ug_ref, uv_ref, wg_ref, wv_ref, bg_ref, bv_ref, dd_ref, wd_ref,
             dh_ref, gug_ref, guv_ref, gd_ref, dwg_ref, dwv_ref, dbg_ref, dbv_ref):
        @pl.when(pl.program_id(0) == 0)
        def _():
            dh_ref[...] = jnp.zeros_like(dh_ref)

        w = jnp.concatenate([wg_ref[...], wv_ref[...]], axis=1)
        b = jnp.concatenate([bg_ref[...], bv_ref[...]], axis=1)
        w_pair = jnp.concatenate([ug_ref[...], uv_ref[...]], axis=0)
        chunks = _conv_chunks(S, tail=True)

        def project(c):
            return _up_pair(h_ref[c[0]:c[1], :], ug_ref, uv_ref), _dot(dd_ref[c[0]:c[1], :], wd_ref[...], "nt")

        def through_conv(c, uin, da):
            lo, hi, keep, rows = c
            u1, u2, u = _conv_taps(uin, w, b, lo == 0)
            gate, val = u[:, :CONV_TC], u[:, CONV_TC:]
            sig = 1.0 / (1.0 + jnp.exp(-gate))
            du = jnp.concatenate([da * val * (sig * (1.0 + gate * (1.0 - sig))), da * (gate * sig)], axis=1)
            dup = w[2:3, :] * du + w[1:2, :] * _shift_up(du, 1, hi == S) + w[0:1, :] * _shift_up(du, 2, hi == S)
            kept = slice(keep, keep + rows)
            du = du[kept]
            dw = jnp.concatenate([_colsum(du * u2[kept]), _colsum(du * u1[kept]), _colsum(du * uin[kept])], axis=0)
            return dup[kept].astype(MXU_DTYPE), (gate * sig * val)[kept].astype(MXU_DTYPE), dw, _colsum(du)

        def weight_grads(c, dup, act):
            out_rows = slice(c[0] + c[2], c[0] + c[2] + c[3])
            dh_ref[out_rows, :] += _dot(dup, w_pair)
            return _dot(dup, h_ref[out_rows, :], "tn"), _dot(act, dd_ref[out_rows, :], "tn")

        dw, db, g_up, g_dn = 0.0, 0.0, 0.0, 0.0
        proj, done = project(chunks[0]), None
        for i, c in enumerate(chunks):
            proj_next = project(chunks[i + 1]) if i + 1 < len(chunks) else None
            if done is not None:
                gu, gd = weight_grads(chunks[i - 1], *done)
                g_up, g_dn = g_up + gu, g_dn + gd
            dup, act, dw_c, db_c = through_conv(c, *proj)
            dw, db, done, proj = dw + dw_c, db + db_c, (dup, act), proj_next
        gu, gd = weight_grads(chunks[-1], *done)
        g_up, g_dn = g_up + gu, g_dn + gd
        gug_ref[...], guv_ref[...] = g_up[:CONV_TC].astype(gug_ref.dtype), g_up[CONV_TC:].astype(guv_ref.dtype)
        gd_ref[...] = g_dn.astype(gd_ref.dtype)
        dwg_ref[...], dwv_ref[...] = dw[:, :CONV_TC], dw[:, CONV_TC:]
        dbg_ref[...], dbv_ref[...] = db[:, :CONV_TC], db[:, CONV_TC:]

    half = lambda rows: pl.BlockSpec((rows, CONV_TC), lambda j: (0, j))
    rows_blk = pl.BlockSpec((CONV_TC, D), lambda j: (j, 0))
    dh, gug, guv, gd, dwg, dwv, dbg, dbv = pl.pallas_call(
        body, name=name, grid=(CONV_NB,),
        in_specs=[_whole(h)] + _half_specs(CONV_TC, rows_axis=True) + _half_specs(3) + _half_specs(1) + [_whole(d_dn), rows_blk],
        out_specs=[pl.BlockSpec((S, D), lambda j: (0, 0)), rows_blk, rows_blk, rows_blk, half(3), half(3), half(1), half(1)],
        out_shape=[jax.ShapeDtypeStruct((S, D), F32)] + [jax.ShapeDtypeStruct((D_FF, D), MXU_DTYPE)] * 3
        + [jax.ShapeDtypeStruct((3, D_FF), F32)] * 2 + [jax.ShapeDtypeStruct((1, D_FF), F32)] * 2,
        compiler_params=_params())(h, w_up_t, w_up_t, w_conv, w_conv, b_conv, b_conv, d_dn, w_down)
    return dh, gug, guv, gd, jnp.concatenate([dwg, dwv], axis=1), jnp.concatenate([dbg, dbv], axis=1)


def adamw(name, w, parts, m, v, tr=None):
    R, C = w.shape
    tr = tr or R
    assert R % tr == 0
    c1 = 1.0 - ADAM_B1 ** ADAM_STEP
    c2 = 1.0 - ADAM_B2 ** ADAM_STEP
    np_ = len(parts)

    def body(*refs):
        w_ref, m_ref, v_ref = refs[0], refs[1 + np_], refs[2 + np_]
        go_ref, d_ref, mo_ref, vo_ref = refs[3 + np_:]
        terms = []
        for part, ref in zip(parts, refs[1:1 + np_], strict=True):
            terms += [ref[...]] if part.ndim == 2 else [ref[p] for p in range(part.shape[0])]
        g = terms[0].astype(F32)
        for term in terms[1:]:
            g = g + term.astype(F32)
        m2 = ADAM_B1 * m_ref[...] + (1.0 - ADAM_B1) * g
        v2 = ADAM_B2 * v_ref[...] + (1.0 - ADAM_B2) * (g * g)
        go_ref[...] = g
        mo_ref[...] = m2
        vo_ref[...] = v2
        d_ref[...] = -ADAM_LR * ((m2 / c1) / (jnp.sqrt(v2 / c2) + ADAM_EPS) + ADAM_WD * w_ref[...])

    blk = pl.BlockSpec((tr, C), lambda i: (i, 0))
    part_specs = [blk if p.ndim == 2 else pl.BlockSpec((p.shape[0], tr, C), lambda i: (0, i, 0)) for p in parts]
    return pl.pallas_call(
        body, name=name, grid=(R // tr,),
        in_specs=[blk] + part_specs + [blk, blk], out_specs=[blk] * 4,
        out_shape=[jax.ShapeDtypeStruct((R, C), F32)] * 4, compiler_params=_params())(w, *parts, m, v)


def _place():
    return lax.axis_index("x"), lax.axis_index("y"), lax.axis_index("c")


def all_gather(name, arrs, after=None):
    n = len(arrs)
    deps = [] if after is None else [after]

    def body(*refs):
        ins, outs = refs[:n], refs[n + len(deps):2 * n + len(deps)]
        send_sems, recv_sems, local_sems = refs[2 * n + len(deps):]
        x, y, c = _place()
        me, sibling = (x, y, c), (x, y, 1 - c)
        chips = [(1 - x, y), (x, 1 - y), (1 - x, 1 - y)]
        sends = []
        for t in range(n):
            out = outs[t]

            def slot(px, py, pc, out=out):
                return out.at[4 * px + 2 * py + pc]

            def copy(k, block, to, src=None, t=t, slot=slot):
                return pltpu.make_async_remote_copy(
                    src_ref=slot(*block) if src is None else src, dst_ref=slot(*block),
                    send_sem=send_sems.at[7 * t + k], recv_sem=recv_sems.at[7 * t + k],
                    device_id=to, device_id_type=MESH_ID)

            mine = pltpu.make_async_copy(ins[t], slot(*me), local_sems.at[t])
            mine.start()
            first = [copy(0, me, sibling, src=ins[t])]
            first += [copy(1 + j, me, (*chip, c), src=ins[t]) for j, chip in enumerate(chips)]
            for cp in first:
                cp.start()
            sends.append((mine, first, copy))
        for t in range(n):
            mine, first, copy = sends[t]
            passed = [copy(4 + j, (*chip, c), sibling) for j, chip in enumerate(chips)]
            for j, chip in enumerate(chips):
                copy(1 + j, (*chip, c), me).wait_recv()
                passed[j].start()
            copy(0, sibling, me).wait_recv()
            for j, chip in enumerate(chips):
                copy(4 + j, (*chip, 1 - c), me).wait_recv()
            for cp in first + passed:
                cp.wait_send()
            mine.wait()

    any_spec = pl.BlockSpec(memory_space=pl.ANY)
    res = pl.pallas_call(
        body, name=name, in_specs=[any_spec] * (n + len(deps)), out_specs=[any_spec] * n,
        out_shape=[jax.ShapeDtypeStruct((N_DEV,) + a.shape, a.dtype) for a in arrs],
        scratch_shapes=[pltpu.SemaphoreType.DMA((7 * n,)), pltpu.SemaphoreType.DMA((7 * n,)), pltpu.SemaphoreType.DMA((n,))],
        compiler_params=pltpu.CompilerParams(has_side_effects=True))(*arrs, *deps)
    return list(res)


HBM_SPEC = pl.BlockSpec(memory_space=pltpu.HBM)
SEM_SPEC = pl.BlockSpec(memory_space=pltpu.SEMAPHORE)
DATAFLOW = pltpu.SideEffectType.DATAFLOW_SIDE_EFFECTING


def _exchange_copies(srcs, lands, send_sems, recv_sems, gather):
    x, y, c = _place()
    me = 4 * x + 2 * y + c
    out = []
    for t, (src, land) in enumerate(zip(srcs, lands, strict=True)):
        for k in range(1, N_DEV):
            px, py, pc = x ^ (k >> 2), y ^ ((k >> 1) & 1), c ^ (k & 1)
            out.append(pltpu.make_async_remote_copy(
                src_ref=src if gather else src.at[4 * px + 2 * py + pc],
                dst_ref=land.at[me] if gather else land.at[k - 1],
                send_sem=send_sems.at[7 * t + k - 1], recv_sem=recv_sems.at[7 * t + k - 1],
                device_id=(px, py, pc), device_id_type=MESH_ID))
    return out


def exchange_start(name, arrs, gather, after=None):
    n = len(arrs)
    lands = [lax.empty(((N_DEV,) + a.shape) if gather else ((N_DEV - 1,) + a.shape[1:]), a.dtype) for a in arrs]
    deps = [] if after is None else [after]

    def body(*refs):
        srcs, land_refs = refs[:n], refs[n:2 * n]
        send_sems, recv_sems = refs[2 * n + len(deps)], refs[2 * n + len(deps) + 1]
        token = refs[-1]
        for cp in _exchange_copies(srcs, land_refs, send_sems, recv_sems, gather):
            cp.start()
        token[...] = jnp.zeros_like(token)

    hbm = lambda a: pltpu.HBM(a.shape, a.dtype)
    res = pl.pallas_call(
        body, name=name,
        out_shape=(pltpu.SemaphoreType.DMA((7 * n,)), pltpu.SemaphoreType.DMA((7 * n,)), *[hbm(a) for a in arrs],
                   *[hbm(l) for l in lands], jax.ShapeDtypeStruct((8, 128), F32)),
        in_specs=[HBM_SPEC] * (2 * n) + [pl.BlockSpec(memory_space=pl.ANY)] * len(deps),
        out_specs=(SEM_SPEC, SEM_SPEC, *[HBM_SPEC] * (2 * n), pl.BlockSpec(memory_space=pltpu.VMEM)),
        input_output_aliases={i: 2 + i for i in range(2 * n)},
        compiler_params=pltpu.CompilerParams(has_side_effects=DATAFLOW),
    )(*[pltpu.with_memory_space_constraint(a, pltpu.HBM) for a in arrs + lands], *deps)
    return res[0], res[1], list(res[2:2 + n]), list(res[2 + n:2 + 2 * n]), res[-1]


def exchange_wait(name, started, gather, after):
    send_sems, recv_sems, srcs, lands, _ = started
    n = len(srcs)

    def body(*refs):
        src_refs, land_refs = refs[:n], refs[n:2 * n]
        copies = _exchange_copies(src_refs, land_refs, refs[2 * n], refs[2 * n + 1], gather)
        for cp in copies:
            cp.wait_send()
        for cp in copies:
            cp.wait_recv()

    hbm = lambda a: pltpu.HBM(a.shape, a.dtype)
    res = pl.pallas_call(
        body, name=name, out_shape=tuple(hbm(a) for a in srcs + lands),
        in_specs=[HBM_SPEC] * (2 * n) + [SEM_SPEC, SEM_SPEC, pl.BlockSpec(memory_space=pl.ANY)],
        out_specs=tuple([HBM_SPEC] * (2 * n)), input_output_aliases={i: i for i in range(2 * n)},
        compiler_params=pltpu.CompilerParams(has_side_effects=DATAFLOW),
    )(*srcs, *lands, send_sems, recv_sems, after)
    return list(res[:n]), list(res[n:])


def _gather_cols(stack):
    p, k, n = stack.shape
    return stack.transpose(1, 0, 2).reshape(k, p * n)


def _scatter_cols(full):
    k, n = full.shape
    return full.reshape(k, N_DEV, n // N_DEV).transpose(1, 0, 2)


def _gather_rows(stack):
    p, r, n = stack.shape
    return stack.reshape(p * r, n)


def _scatter_rows(full):
    r, n = full.shape
    return full.reshape(N_DEV, r // N_DEV, n)


_IN_NAT = Q_LORA + KV_LORA
TRANSPOSED = ("w_in", "w_q_b", "w_up")


def to_kernel_layout(name, w):
    if name == "w_in":
        z = lambda n: jnp.zeros((n, w.shape[1]), w.dtype)
        return jnp.concatenate([w[:_IN_NAT], z(KPE_LO), w[_IN_NAT:_IN_NAT + ROPE], z(LANES - KPE_LO - ROPE), w[_IN_NAT + ROPE:]], axis=0)
    if name == "w_q_b":
        return jnp.pad(w.reshape(HEADS, NOPE + ROPE, -1), ((0, 0), (0, LANES - NOPE - ROPE), (0, 0))).reshape(HEADS * LANES, -1)
    if name == "w_o":
        mla = jnp.pad(w[:HEADS * NOPE].reshape(HEADS, NOPE, -1), ((0, 0), (LANES - NOPE, 0), (0, 0))).reshape(HEADS * LANES, -1)
        return jnp.concatenate([mla, w[HEADS * NOPE:]], axis=0)
    return w


def from_kernel_layout(name, g):
    if name == "w_in":
        return jnp.concatenate([g[:_IN_NAT], g[P_KPE + KPE_LO:P_KPE + KPE_LO + ROPE], g[P_QD:]], axis=0)
    if name == "w_q_b":
        return g.reshape(HEADS, LANES, -1)[:, :NOPE + ROPE, :].reshape(HEADS * (NOPE + ROPE), -1)
    if name == "w_o":
        mla = g[:HEADS * LANES].reshape(HEADS, LANES, -1)[:, LANES - NOPE:, :].reshape(HEADS * NOPE, -1)
        return jnp.concatenate([mla, g[HEADS * LANES:]], axis=0)
    return g


SMALL_COLS = 1024
SMALL_ROWS = 24
SMALL_AT = {"loss": (0, 0, 1), "b_ada": (1, 0, 6 * D_MODEL), "g_mix_norm": (7, 0, D_MODEL), "g_q_lat": (8, 0, Q_LORA),
            "g_kv_lat": (9, 0, KV_LORA), "g_mla_q_nope": (10, 0, NOPE), "g_mla_q_pe": (10, 128, ROPE),
            "g_mla_k_nope": (10, 256, NOPE), "g_mla_k_pe": (10, 384, ROPE), "g_dil_q": (10, 512, DIL_DIM),
            "g_dil_k": (10, 640, DIL_DIM), "g_ffn_norm": (11, 0, D_MODEL), "b_conv": (12, 0, 2 * D_FF)}
SMALL_PARAMS = tuple(n for n in SMALL_AT if n != "loss")


def _pack_small(values):
    by_row = {}
    for name, (row, off, n) in SMALL_AT.items():
        by_row.setdefault(row, []).append((off, values[name].reshape(-1).astype(F32)))
    out = []
    for row in sorted(by_row):
        pieces, at = [], 0
        for off, v in sorted(by_row[row], key=lambda t: t[0]):
            pieces += [jnp.zeros((off - at,), F32), v]
            at = off + v.shape[0]
        flat = jnp.concatenate(pieces)
        nrows = -(-flat.shape[0] // SMALL_COLS)
        out.append(jnp.pad(flat, (0, nrows * SMALL_COLS - flat.shape[0])).reshape(nrows, SMALL_COLS))
    packed = jnp.concatenate(out, axis=0)
    return jnp.pad(packed, ((0, SMALL_ROWS - packed.shape[0]), (0, 0)))


def _adam(w, g, m, v):
    c1 = 1.0 - ADAM_B1 ** ADAM_STEP
    c2 = 1.0 - ADAM_B2 ** ADAM_STEP
    m2 = ADAM_B1 * m + (1.0 - ADAM_B1) * g
    v2 = ADAM_B2 * v + (1.0 - ADAM_B2) * (g * g)
    return -ADAM_LR * ((m2 / c1) / (jnp.sqrt(v2 / c2) + ADAM_EPS) + ADAM_WD * w), m2, v2


def adamw_small(name, stack, params):
    flat = [a for n in SMALL_PARAMS for a in params[n]]

    def body(stack_ref, *refs):
        ins, outs = refs[:len(flat)], refs[len(flat):]
        g_all = stack_ref[0]
        for p in range(1, N_DEV):
            g_all = g_all + stack_ref[p]
        outs[0][...] = g_all[0:1, 0:1]
        for i, pname in enumerate(SMALL_PARAMS):
            row, off, n = SMALL_AT[pname]
            w_ref, m_ref, v_ref = ins[3 * i:3 * i + 3]
            go_ref, d_ref, mo_ref, vo_ref = outs[1 + 4 * i:5 + 4 * i]
            for c0 in range(0, n, SMALL_COLS):
                cn = min(SMALL_COLS, n - c0)
                r = row + c0 // SMALL_COLS
                g = g_all[r:r + 1, off:off + cn]
                cols = (slice(None), slice(c0, c0 + cn))
                d, m2, v2 = _adam(w_ref[cols], g, m_ref[cols], v_ref[cols])
                go_ref[cols], d_ref[cols], mo_ref[cols], vo_ref[cols] = g, d, m2, v2

    whole = lambda a: pl.BlockSpec(a.shape, lambda: (0,) * a.ndim)
    out_shape = [jax.ShapeDtypeStruct((1, 1), F32)] + [jax.ShapeDtypeStruct(a.shape, F32) for n in SMALL_PARAMS for a in params[n][:1] * 4]
    res = pl.pallas_call(body, name=name, in_specs=[whole(stack)] + [whole(a) for a in flat],
                         out_specs=[pl.BlockSpec(s.shape, lambda s=s: (0,) * len(s.shape)) for s in out_shape],
                         out_shape=out_shape, compiler_params=_params())(stack, *flat)
    return res[0], {n: res[1 + 4 * i:5 + 4 * i] for i, n in enumerate(SMALL_PARAMS)}


def _local_step(x, pos, mod, target, w, fetch, emit):
    S = SEQ
    sh1, sc1, g1, sh2, sc2, g2 = [mod[:, i * D_MODEL:(i + 1) * D_MODEL] for i in range(6)]
    zeros = lambda n: jnp.zeros((1, n), F32)
    g_q = jnp.concatenate([w["g_mla_q_nope"], w["g_mla_q_pe"], zeros(LANES - NOPE - ROPE)], axis=1)
    g_k = jnp.concatenate([w["g_mla_k_nope"], zeros(LANES - NOPE)], axis=1)
    g_kpe = jnp.concatenate([zeros(KPE_LO), w["g_mla_k_pe"], zeros(LANES - KPE_LO - ROPE)], axis=1)
    g_dq = jnp.concatenate([w["g_dil_q"]] * 2, axis=1)
    g_dk = jnp.concatenate([w["g_dil_k"]] * 2, axis=1)
    b_conv = w["b_conv"]

    def inv_freq(d):
        return jnp.power(ROPE_THETA, -2.0 * jnp.arange(d // 2, dtype=F32) / d)

    f_mla = jnp.concatenate([jnp.zeros((KPE_LO,), F32), inv_freq(ROPE), inv_freq(ROPE), jnp.zeros((LANES - KPE_LO - ROPE,), F32)])
    f_dil = jnp.concatenate([inv_freq(DIL_DIM)] * 4)

    def tables_fn(rows, params):
        (p,), (fa, fb) = rows, params
        return [jnp.cos(p * fa), jnp.sin(p * fa), jnp.cos(p * fb), jnp.sin(p * fb)], []

    cos_m, sin_m, cos_d, sin_d = rowwise("rope_tables", tables_fn, [pos], [f_mla.reshape(1, LANES), f_dil.reshape(1, LANES)],
                                         [(LANES, F32)] * 4)
    tables = [cos_m, sin_m, cos_d, sin_d]
    H_M, H_D = ROPE // 2, DIL_DIM // 2

    def ln1_fn(rows, params):
        (xv,), (g, sc, sh) = rows, params
        y, _, _ = _rms(xv, g)
        return [y * (1.0 + sc) + sh], []

    (h,) = rowwise("ln1_fwd", ln1_fn, [x], [w["g_mix_norm"], sc1, sh1], [(D_MODEL, MXU_DTYPE)])
    w_in = fetch("w_in", h)
    proj = matmul("proj_fwd", h, w_in, "nt", tm=512)

    def post_fn(rows, params):
        (pv, cm, sm, cd, sd), (gq, gkv, gkp, gdq, gdk) = rows, params
        kper = _rope(_grms(pv[:, P_KPE:P_QD], gkp, KPE_GROUPS)[0], cm, sm, H_M)
        qd = [_rope(_grms(c, gdq, DIL_GROUPS)[0], cd, sd, H_D) for c in _chunks(pv[:, P_QD:P_KD])]
        kd = [_rope(_grms(c, gdk, DIL_GROUPS)[0], cd, sd, H_D) for c in _chunks(pv[:, P_KD:P_VD])]
        return [_rms(pv[:, P_QLAT:P_KVLAT], gq)[0], _rms(pv[:, P_KVLAT:P_KPE], gkv)[0], kper,
                jnp.concatenate(qd, axis=1), jnp.concatenate(kd, axis=1)], []

    post_params = [w["g_q_lat"], w["g_kv_lat"], g_kpe, g_dq, g_dk]
    qln, kvn, kper, qd_r, kd_r = rowwise(
        "proj_post", post_fn, [proj] + tables, post_params,
        [(Q_LORA, MXU_DTYPE), (KV_LORA, MXU_DTYPE), (LANES, MXU_DTYPE)] + [(DIL_WIDTH, F32)] * 2, tm=256)
    w_q_b, w_kv_b = fetch("w_q_b", qln), fetch("w_kv_b", kvn)
    q = matmul("q_fwd", qln, w_q_b, "nt", tm=1024)
    kv = matmul("kv_fwd", kvn, w_kv_b, "nn", tm=1024)

    def mla_prep_fn(rows, params):
        (qv, kvv, kp, cm, sm), (gq, gk) = rows, params
        value_lanes = _lane(kp.shape) >= NOPE
        qs, ks, vs = [], [], []
        for qc, kc in zip(_chunks(qv), _chunks(kvv), strict=True):
            qs.append(_rope(_grms(qc, gq, Q_GROUPS)[0], cm, sm, H_M))
            ks.append(_grms(kc, gk, K_GROUPS)[0] + kp)
            vs.append(jnp.where(value_lanes, kc, 0.0))
        return [jnp.concatenate(t, axis=1) for t in (qs, ks, vs)], []

    q_mla, k_mla, v_mla = rowwise("mla_prep", mla_prep_fn, [q, kv, kper, cos_m, sin_m], [g_q, g_k],
                                  [(HEADS * LANES, MXU_DTYPE)] * 3, tm=256)
    mla_scale = (NOPE + ROPE) ** -0.5
    o_cat, lse_mla = mla_fwd("mla_fwd", q_mla, k_mla, v_mla, mla_scale)

    band = [band_fwd(f"band{dil}_fwd", qd_r, kd_r, proj, dil) for dil in DILATIONS]
    o_cat, lse_mix = combine_fwd("dil_combine", [b[0] for b in band], [b[1] for b in band], o_cat)
    w_o = fetch("w_o", o_cat)
    mix = matmul("mix_fwd", o_cat, w_o, "nn", tm=512)

    def mid_fn(rows, params):
        (xv, mx), (gate1, g, sc, sh) = rows, params
        x1 = xv + gate1 * mx
        y, _, _ = _rms(x1, g)
        return [x1, y * (1.0 + sc) + sh], []

    x1, h2 = rowwise("mid_fwd", mid_fn, [x, mix], [g1, w["g_ffn_norm"], sc2, sh2], [(D_MODEL, F32), (D_MODEL, MXU_DTYPE)])
    w_up, w_conv, w_down = fetch("w_up", h2), fetch("w_conv", h2), fetch("w_down", h2)
    dn = ffn_fwd("ffn_fwd", h2, w_up, w_conv, b_conv, w_down)

    def final_fn(rows, params):
        (x1v, dnv, tgt), (gate2,) = rows, params
        r = x1v + gate2 * dnv - tgt
        dy = r * (1.0 / D_MODEL)
        loss = jnp.sum(_colsum(r * r), axis=-1, keepdims=True) * (0.5 / D_MODEL)
        return [dy, gate2 * dy], [loss, _colsum(dy * dnv)]

    dy, d_dn, loss, dg2 = rowwise("loss_head", final_fn, [x1, dn, target], [g2], [(D_MODEL, F32), (D_MODEL, MXU_DTYPE)],
                                  [1, D_MODEL])
    dh2, g_up_gate, g_up_val, g_down, g_w_conv, g_b_conv = ffn_bwd("ffn_bwd", h2, w_up, w_conv, b_conv, d_dn, w_down)
    emit("w_down", g_down)
    emit("w_conv", g_w_conv)
    sent = emit("w_up", jnp.concatenate([g_up_gate, g_up_val], axis=0))

    def mid_bwd_fn(rows, params):
        (dh2v, dyv, x1v, mx), (gate1, g, sc) = rows, params
        yn, n, rstd = _rms(x1v, g)
        dx_n, dg = _rms_bwd(dh2v * (1.0 + sc), n, rstd, g)
        dx1 = dyv + dx_n
        return [dx1, gate1 * dx1], [dg, _colsum(dh2v * yn), _colsum(dh2v), _colsum(dx1 * mx)]

    dx1, dmix, dg_ffn, dsc2, dsh2, dg1 = rowwise(
        "mid_bwd", mid_bwd_fn, [dh2, dy, x1, mix], [g1, w["g_ffn_norm"], sc2], [(D_MODEL, F32), (D_MODEL, MXU_DTYPE)],
        [D_MODEL] * 4, dep=sent)

    sent = emit("w_o", matmul("mix_wgrad", o_cat, dmix, "tn", tm=512, out_dtype=MXU_DTYPE))
    do_cat = matmul("mix_dgrad", dmix, w_o, "nt", tm=512, dep=sent)
    dband = [band_bwd(f"band{dil}_bwd", qd_r, kd_r, proj, b[1], lse_mix, o_cat, do_cat, dil) for dil, b in zip(DILATIONS, band)]
    dq_mla, dkv_mla, dkper = mla_bwd("mla_bwd", q_mla, k_mla, v_mla, o_cat, do_cat, lse_mla, mla_scale)

    def mla_prep_bwd_fn(rows, params):
        (dqv, dkvv, qv, kvv, cm, sm), (gq, gk) = rows, params
        nope_lanes = _lane(cm.shape) < NOPE
        dqs, dkvs, dgq, dgk = [], [], 0.0, 0.0
        for dqc, dkc, qc, kc in zip(_chunks(dqv), _chunks(dkvv), _chunks(qv), _chunks(kvv), strict=True):
            _, n, rstd = _grms(qc, gq, Q_GROUPS)
            dx, dg = _grms_bwd(_rope_bwd(dqc, cm, sm, H_M), n, rstd, gq, Q_GROUPS)
            dqs.append(dx)
            dgq = dgq + dg
            _, n, rstd = _grms(kc, gk, K_GROUPS)
            dx, dg = _grms_bwd(dkc, n, rstd, gk, K_GROUPS)
            dkvs.append(jnp.where(nope_lanes, dx, dkc))
            dgk = dgk + dg
        return [jnp.concatenate(dqs, axis=1), jnp.concatenate(dkvs, axis=1)], [dgq, dgk]

    dq, dkv, dg_q, dg_k = rowwise("mla_prep_bwd", mla_prep_bwd_fn, [dq_mla, dkv_mla, q, kv, cos_m, sin_m], [g_q, g_k],
                                  [(HEADS * LANES, MXU_DTYPE)] * 2, [LANES, LANES], tm=256)
    emit("w_q_b", matmul("q_wgrad", dq, qln, "tn", out_dtype=MXU_DTYPE))
    emit("w_kv_b", matmul("kv_wgrad", kvn, dkv, "tn", out_dtype=MXU_DTYPE))
    dqln = matmul("q_dgrad", dq, w_q_b, "nn", tm=1024)
    dkvn = matmul("kv_dgrad", dkv, w_kv_b, "nt", tm=1024)

    def pre_bwd_fn(rows, params):
        dql, dkvl, dkp = rows[0:3]
        dqd_, dkd_, dvd_ = [rows[3 + 3 * i] + rows[4 + 3 * i] + rows[5 + 3 * i] for i in range(3)]
        pv, cm, sm, cd, sd = rows[12:]
        gq, gkv, gkp, gdq, gdk = params
        r_q = _norm_bwd(dql, pv[:, P_QLAT:P_KVLAT], gq)
        r_kv = _norm_bwd(dkvl, pv[:, P_KVLAT:P_KPE], gkv)
        _, n, rstd = _grms(pv[:, P_KPE:P_QD], gkp, KPE_GROUPS)
        r_kp = _grms_bwd(_rope_bwd(dkp, cm, sm, H_M), n, rstd, gkp, KPE_GROUPS)
        outs, dgs = [r_q[0], r_kv[0], r_kp[0]], []
        for dval, lo, g in ((dqd_, P_QD, gdq), (dkd_, P_KD, gdk)):
            dg_sum = 0.0
            for dc, xc in zip(_chunks(dval), _chunks(pv[:, lo:lo + DIL_WIDTH]), strict=True):
                _, n, rstd = _grms(xc, g, DIL_GROUPS)
                dx, dg = _grms_bwd(_rope_bwd(dc, cd, sd, H_D), n, rstd, g, DIL_GROUPS)
                outs.append(dx)
                dg_sum = dg_sum + dg
            dgs.append(dg_sum)
        return [jnp.concatenate(outs + [dvd_], axis=1)], [r_q[1], r_kv[1], r_kp[1]] + dgs

    dproj, dg_q_lat, dg_kv_lat, dg_kpe, dg_dq, dg_dk = rowwise(
        "proj_pre_bwd", pre_bwd_fn,
        [dqln, dkvn, dkper] + [d[i] for i in range(3) for d in dband] + [proj] + tables, post_params,
        [(P_END, MXU_DTYPE)], [Q_LORA, KV_LORA, LANES, LANES, LANES], tm=256)
    sent = emit("w_in", matmul("proj_wgrad", dproj, h, "tn", tn=512, out_dtype=MXU_DTYPE))
    dh = matmul("proj_dgrad", dproj, w_in, "nn", tm=512, dep=sent)

    def ln1_bwd_fn(rows, params):
        (dhv, dres, xv), (g, sc) = rows, params
        yn, n, rstd = _rms(xv, g)
        dx_n, dg = _rms_bwd(dhv * (1.0 + sc), n, rstd, g)
        return [dres + dx_n], [dg, _colsum(dhv * yn), _colsum(dhv)]

    grad_x, dg_mix, dsc1, dsh1 = rowwise("ln1_bwd", ln1_bwd_fn, [dh, dx1, x], [w["g_mix_norm"], sc1], [(D_MODEL, F32)],
                                         [D_MODEL] * 3)
    dmod = jnp.concatenate([dsh1, dsc1, dg1, dsh2, dsc2, dg2], axis=-1)
    small = {"loss": loss, "b_ada": dmod, "g_mix_norm": dg_mix, "g_q_lat": dg_q_lat, "g_kv_lat": dg_kv_lat,
             "g_mla_q_nope": dg_q[:, :NOPE], "g_mla_q_pe": dg_q[:, NOPE:NOPE + ROPE], "g_mla_k_nope": dg_k[:, :NOPE],
             "g_mla_k_pe": dg_kpe[:, KPE_LO:KPE_LO + ROPE], "g_dil_q": dg_dq[:, :DIL_DIM] + dg_dq[:, DIL_DIM:],
             "g_dil_k": dg_dk[:, :DIL_DIM] + dg_dk[:, DIL_DIM:], "g_ffn_norm": dg_ffn,
             "b_conv": g_b_conv}
    return grad_x, small


COL_SHARDED = ("w_kv_b", "w_conv")
ROW_SHARDED = ("w_o", "w_down") + TRANSPOSED
ADAM_TILE = {"w_ada": 256, "w_up": 176, "w_down": 176}
GATHER_GROUPS = (("w_in",), ("w_q_b", "w_kv_b"), ("w_o", "w_up", "w_conv", "w_down"))
SCATTER_GROUPS = (("w_down", "w_conv", "w_up"), ("w_o",), ("w_q_b", "w_kv_b", "w_in"))
OUT_WEIGHTS = ("w_ada", "b_ada", "g_mix_norm", "w_in", "g_q_lat", "w_q_b", "g_kv_lat", "w_kv_b", "g_mla_q_nope", "g_mla_q_pe",
               "g_mla_k_nope", "g_mla_k_pe", "g_dil_q", "g_dil_k", "w_o", "g_ffn_norm", "w_up", "w_conv", "b_conv", "w_down")


def kernel(x, c, positions, w_ada, b_ada, g_mix_norm, w_in, g_q_lat, w_q_b, g_kv_lat, w_kv_b, g_mla_q_nope, g_mla_q_pe, g_mla_k_nope, g_mla_k_pe, g_dil_q, g_dil_k, w_o, g_ffn_norm, w_up, w_conv, b_conv, w_down, loss_target, m_w_ada, m_b_ada, m_g_mix_norm, m_w_in, m_g_q_lat, m_w_q_b, m_g_kv_lat, m_w_kv_b, m_g_mla_q_nope, m_g_mla_q_pe, m_g_mla_k_nope, m_g_mla_k_pe, m_g_dil_q, m_g_dil_k, m_w_o, m_g_ffn_norm, m_w_up, m_w_conv, m_b_conv, m_w_down, v_w_ada, v_b_ada, v_g_mix_norm, v_w_in, v_g_q_lat, v_w_q_b, v_g_kv_lat, v_w_kv_b, v_g_mla_q_nope, v_g_mla_q_pe, v_g_mla_k_nope, v_g_mla_k_pe, v_g_dil_q, v_g_dil_k, v_w_o, v_g_ffn_norm, v_w_up, v_w_conv, v_b_conv, v_w_down):
    args = dict(locals())
    xi, yi, ci = _place()
    me = 4 * xi + 2 * yi + ci
    def local(prefix, n):
        a = args[prefix + n][0]
        return a.T if n in TRANSPOSED else a

    shard = {n: local("", n) for n in COL_SHARDED + ROW_SHARDED + ("w_ada",)}
    small_w = {n: args[n] for n in SMALL_PARAMS}

    (c_all,) = all_gather("gather_c", [c])
    (sc_all,) = rowwise("silu_c", lambda rows, params: ([_silu(rows[0])], []), [c_all.reshape(N_DEV, D_MODEL)], [],
                        [(D_MODEL, MXU_DTYPE)])
    mod_part = matmul("ada_fwd", sc_all, shard["w_ada"], "nn")
    (mod_all,) = all_gather("gather_mod", [mod_part])

    payload = {n: shard[n] if n == "w_conv" else shard[n].astype(MXU_DTYPE) for n in COL_SHARDED + ROW_SHARDED}
    gathers, after_start = [], mod_all
    for i, grp in enumerate(GATHER_GROUPS):
        gathers.append(exchange_start(f"gather{i}_start", [payload[n] for n in grp], gather=True, after=after_start))
        after_start = gathers[-1][-1]
    full = {}

    def fetch(name, after):
        if name not in full:
            (i, grp), = [(i, grp) for i, grp in enumerate(GATHER_GROUPS) if name in grp]
            srcs, lands = exchange_wait(f"gather{i}_wait", gathers[i], True, after)
            for n, src, land in zip(grp, srcs, lands, strict=True):
                stack = lax.dynamic_update_index_in_dim(land, src, me, 0)
                full[n] = to_kernel_layout(n, _gather_cols(stack) if n in COL_SHARDED else _gather_rows(stack))
        return full[name]

    mod_row = lax.dynamic_index_in_dim(mod_all, me, axis=1, keepdims=False).reshape(1, 6 * D_MODEL)
    (mod,) = rowwise("ada_bias", lambda rows, params: ([rows[0] + rows[1]], []), [mod_row, b_ada], [], [(6 * D_MODEL, F32)],
                     dep=after_start)

    own, pending, scatters = {}, {}, {}

    def emit(name, grad):
        grad = from_kernel_layout(name, grad)
        parts = _scatter_cols(grad) if name in COL_SHARDED else _scatter_rows(grad)
        own[name] = lax.dynamic_index_in_dim(parts, me, 0, keepdims=False)
        pending[name] = parts
        for i, grp in enumerate(SCATTER_GROUPS):
            if name == grp[-1]:
                scatters[i] = exchange_start(f"scatter{i}_start", [pending[n] for n in grp], gather=False)
                return scatters[i][-1]
        return None

    pos = positions.reshape(SEQ, 1).astype(F32)
    grad_x, small = _local_step(x[0], pos, mod, loss_target[0], small_w, fetch, emit)

    res, done = {}, grad_x
    for i, grp in enumerate(SCATTER_GROUPS):
        _, lands = exchange_wait(f"scatter{i}_wait", scatters[i], False, done)
        for n, land in zip(grp, lands, strict=True):
            res[n] = adamw(f"adamw_{n}", shard[n], [own[n], land], local("m_", n), local("v_", n), ADAM_TILE.get(n))
            done = res[n][0]
            if n in TRANSPOSED:
                res[n] = [r.T for r in res[n]]
    (small_all,) = all_gather("gather_small", [_pack_small(small)], after=done)
    loss, small_res = adamw_small("adamw_small", small_all, {n: (args[n], args["m_" + n], args["v_" + n]) for n in SMALL_PARAMS})
    row, _, n_mod = SMALL_AT["b_ada"]
    dmod_all = small_all[:, row:row + n_mod // SMALL_COLS, :].reshape(N_DEV, n_mod)
    dmod_mine = lax.dynamic_slice_in_dim(dmod_all, me * (6 * D_MODEL // N_DEV), 6 * D_MODEL // N_DEV, axis=1)
    g_w_ada = matmul("ada_wgrad", sc_all, dmod_mine, "tn")
    res["w_ada"] = adamw("adamw_w_ada", shard["w_ada"], [g_w_ada], m_w_ada[0], v_w_ada[0], ADAM_TILE["w_ada"])

    def leaf(kind, n):
        if n in res:
            return res[n][kind][None]
        return small_res[n][kind]

    return (loss.reshape(()), grad_x[None], *[leaf(k, n) for k in range(4) for n in OUT_WEIGHTS])
```

```python
import jax
import jax.numpy as jnp
from jax import lax
from jax.experimental import pallas as pl
from jax.experimental.pallas import tpu as pltpu

F32 = jnp.float32
MXU_DTYPE = jnp.bfloat16

N_DEV = 8
D_MODEL = 1024
SEQ = 2048
HEADS = 8
NOPE = 64
ROPE = 32
Q_LORA = 512
KV_LORA = 256
DIL_DIM = 64
DIL_WIDTH = HEADS * DIL_DIM
DILATIONS = (1, 4, 16)
SPAN = 128
D_FF = 2816
LANES = 128
ROPE_THETA = 10000.0
EPS = 1e-6
NEG_INF = -1e30
ADAM_LR, ADAM_B1, ADAM_B2, ADAM_EPS, ADAM_WD, ADAM_STEP = 0.001, 0.9, 0.999, 1e-08, 0.01, 10
VMEM_LIMIT = 56 * 1024 * 1024
MESH_ID = pl.DeviceIdType.MESH

P_QLAT, P_KVLAT, P_KPE, P_QD, P_KD, P_VD, P_END = 0, 512, 768, 896, 1408, 1920, 2432
KPE_LO = 64
MIX_IN = HEADS * LANES + DIL_WIDTH


def _params(**kw):
    return pltpu.CompilerParams(vmem_limit_bytes=VMEM_LIMIT, **kw)


def rowwise(name, fn, rows, params, out_rows, out_accs=(), tm=512, dep=None):
    deps = [] if dep is None else [dep]
    rows = [r if isinstance(r, tuple) else (r, r.shape[1], 0) for r in rows]
    R = rows[0][0].shape[0]
    tm = min(tm, R)
    steps = R // tm
    assert steps * tm == R
    in_specs = []
    for a, width, cb in rows:
        ri = a.shape[0]
        per = ri // tm
        assert per * tm == ri
        if ri == R:
            in_specs.append(pl.BlockSpec((tm, width), lambda i, cb=cb: (i, cb)))
        else:
            in_specs.append(pl.BlockSpec((tm, width), lambda i, per=per, cb=cb: (i % per, cb)))
    for p in params:
        in_specs.append(pl.BlockSpec(p.shape, lambda i: (0,) * p.ndim))
    in_specs += [pl.BlockSpec(memory_space=pl.ANY)] * len(deps)
    out_shape = [jax.ShapeDtypeStruct((R, d), dt) for d, dt in out_rows]
    out_specs = [pl.BlockSpec((tm, d), lambda i: (i, 0)) for d, _ in out_rows]
    out_shape += [jax.ShapeDtypeStruct((1, n), F32) for n in out_accs]
    out_specs += [pl.BlockSpec((1, n), lambda i: (0, 0)) for n in out_accs]
    nr, npar, no, na = len(rows), len(params), len(out_rows), len(out_accs)

    def body(*refs):
        rvals = [r[...] for r in refs[:nr]]
        pvals = [r[...] for r in refs[nr:nr + npar]]
        outs, accs = fn(rvals, pvals)
        first_out = nr + npar + len(deps)
        for ref, v in zip(refs[first_out:first_out + no], outs, strict=True):
            ref[...] = v.astype(ref.dtype)
        if na:
            acc_refs = refs[first_out + no:]
            i = pl.program_id(0)

            @pl.when(i == 0)
            def _():
                for ref, v in zip(acc_refs, accs, strict=True):
                    ref[...] = v

            @pl.when(i > 0)
            def _():
                for ref, v in zip(acc_refs, accs, strict=True):
                    ref[...] += v

    res = pl.pallas_call(body, name=name, grid=(steps,), in_specs=in_specs, out_specs=out_specs,
                         out_shape=out_shape, compiler_params=_params())(*[r[0] for r in rows], *params, *deps)
    return list(res)


_DIMS = {"nn": ((1,), (0,)), "nt": ((1,), (1,)), "tn": ((0,), (0,))}


def _dot(a, b, mode="nn"):
    return lax.dot_general(a.astype(MXU_DTYPE), b.astype(MXU_DTYPE), (_DIMS[mode], ((), ())),
                           preferred_element_type=F32)


def matmul(name, a, b, mode, tm=None, tn=None, tk=None, out_dtype=F32, dep=None, a_mmap=None, b_nmap=None, b_kmap=None):
    if mode == "tn":
        K, M = a.shape
    else:
        M, K = a.shape
    N = b.shape[0] if mode == "nt" else b.shape[1]
    tm, tn, tk = tm or M, tn or N, tk or K
    nm, nn, nk = M // tm, N // tn, K // tk
    assert nm * tm == M and nn * tn == N and nk * tk == K
    same = lambda idx: idx
    a_mmap, b_nmap, b_kmap = a_mmap or same, b_nmap or same, b_kmap or same
    if mode == "tn":
        a_spec = pl.BlockSpec((tk, tm), lambda i, j, k: (k, a_mmap(i)))
    else:
        a_spec = pl.BlockSpec((tm, tk), lambda i, j, k: (a_mmap(i), k))
    if mode == "nt":
        b_spec = pl.BlockSpec((tn, tk), lambda i, j, k: (b_nmap(j), b_kmap(k)))
    else:
        b_spec = pl.BlockSpec((tk, tn), lambda i, j, k: (b_kmap(k), b_nmap(j)))
    deps = [] if dep is None else [dep]

    def body(a_ref, b_ref, *rest):
        o_ref, scratch = rest[len(deps)], rest[len(deps) + 1:]
        p = _dot(a_ref[...], b_ref[...], mode)
        if nk == 1:
            o_ref[...] = p.astype(o_ref.dtype)
        else:
            acc = scratch[0]
            k = pl.program_id(2)

            @pl.when(k == 0)
            def _():
                acc[...] = p

            @pl.when(k > 0)
            def _():
                acc[...] += p

            @pl.when(k == nk - 1)
            def _():
                o_ref[...] = acc[...].astype(o_ref.dtype)

    return pl.pallas_call(
        body, name=name, grid=(nm, nn, nk), in_specs=[a_spec, b_spec] + [pl.BlockSpec(memory_space=pl.ANY)] * len(deps),
        out_specs=pl.BlockSpec((tm, tn), lambda i, j, k: (i, j)),
        out_shape=jax.ShapeDtypeStruct((M, N), out_dtype),
        scratch_shapes=[pltpu.VMEM((tm, tn), F32)] if nk > 1 else [],
        compiler_params=_params())(a, b, *deps)


def _rms(x, g):
    rstd = lax.rsqrt(jnp.mean(x * x, axis=-1, keepdims=True) + EPS)
    n = x * rstd
    return n * g, n, rstd


def _rms_bwd(dy, n, rstd, g):
    dg = jnp.sum(dy * n, axis=0, keepdims=True)
    dn = dy * g
    dx = rstd * (dn - n * jnp.mean(dn * n, axis=-1, keepdims=True))
    return dx, dg


def _norm_bwd(dy, x, g):
    _, n, rstd = _rms(x, g)
    return _rms_bwd(dy, n, rstd, g)


def _colsum(v):
    return jnp.sum(v, axis=0, keepdims=True)


def _silu(x):
    return x * (1.0 / (1.0 + jnp.exp(-x)))


def _lane(shape):
    return lax.broadcasted_iota(jnp.int32, shape, 1)


def _group_mean(v, groups):
    i = lax.broadcasted_iota(jnp.int32, (LANES, LANES), 0)
    j = lax.broadcasted_iota(jnp.int32, (LANES, LANES), 1)
    g = jnp.zeros((LANES, LANES), F32)
    for lo, hi in groups:
        g = jnp.where((i >= lo) & (i < hi) & (j >= lo) & (j < hi), 1.0 / (hi - lo), g)
    head = v.astype(MXU_DTYPE)
    return _dot(head, g) + _dot(v - head.astype(F32), g)


def _in_groups(shape, groups):
    lane = _lane(shape)
    m = jnp.zeros(shape, jnp.bool_)
    for lo, hi in groups:
        m = m | ((lane >= lo) & (lane < hi))
    return m


def _grms(x, g, groups):
    rstd = lax.rsqrt(_group_mean(x * x, groups) + EPS)
    n = jnp.where(_in_groups(x.shape, groups), x * rstd, 0.0)
    return n * g, n, rstd


def _grms_bwd(dy, n, rstd, g, groups):
    dn = dy * g
    return rstd * (dn - n * _group_mean(dn * n, groups)), _colsum(dy * n)


def _rot(x, half, transpose=False):
    first = (_lane(x.shape) % (2 * half)) < half
    up = pltpu.roll(x, LANES - half, axis=1)
    down = pltpu.roll(x, half, axis=1)
    return jnp.where(first, up, -down) if transpose else jnp.where(first, -up, down)


def _rope(x, cos, sin, half):
    return x * cos + _rot(x, half) * sin


def _rope_bwd(dy, cos, sin, half):
    return dy * cos + _rot(dy * sin, half, transpose=True)


def _chunks(x):
    return [x[:, i:i + LANES] for i in range(0, x.shape[1], LANES)]


Q_GROUPS = ((0, NOPE), (NOPE, NOPE + ROPE))
K_GROUPS = ((0, NOPE),)
KPE_GROUPS = ((KPE_LO, KPE_LO + ROPE),)
DIL_GROUPS = ((0, DIL_DIM), (DIL_DIM, 2 * DIL_DIM))


def _col(width, rows=SEQ):
    return pl.BlockSpec((rows, width), lambda h: (0, h))


def _causal_tail(s, tq, fill):
    diag = s[:, s.shape[1] - tq:]
    keep = lax.broadcasted_iota(jnp.int32, diag.shape, 1) <= lax.broadcasted_iota(jnp.int32, diag.shape, 0)
    diag = jnp.where(keep, diag, fill)
    return diag if s.shape[1] == tq else jnp.concatenate([s[:, :s.shape[1] - tq], diag], axis=1)


def mla_fwd(name, q, k, v, scale, tq=256):
    S = q.shape[0]

    def body(q_ref, k_ref, v_ref, o_ref, lse_ref):
        nb = S // tq
        blk = lambda i: slice(i * tq, (i + 1) * tq)

        def scores(i):
            return _dot(q_ref[blk(i), :], k_ref[:(i + 1) * tq, :], "nt")

        def softmax(i, s):
            s = _causal_tail(s * scale, tq, NEG_INF)
            m = jnp.max(s, axis=-1, keepdims=True)
            e = jnp.exp(s - m)
            l = jnp.sum(e, axis=-1, keepdims=True)
            lse_ref[0, blk(i), :] = m + jnp.log(l)
            return (e * (1.0 / l)).astype(MXU_DTYPE)

        def weighted(i, p):
            o_ref[blk(i), :] = _dot(p, v_ref[:(i + 1) * tq, :])

        s, p_prev = scores(0), None
        for i in range(nb):
            s_next = scores(i + 1) if i + 1 < nb else None
            if p_prev is not None:
                weighted(i - 1, p_prev)
            p_prev, s = softmax(i, s), s_next
        weighted(nb - 1, p_prev)

    return pl.pallas_call(
        body, name=name, grid=(HEADS,), in_specs=[_col(LANES)] * 3,
        out_specs=[_col(LANES), pl.BlockSpec((1, S, 1), lambda h: (h, 0, 0))],
        out_shape=[jax.ShapeDtypeStruct((S, MIX_IN), F32), jax.ShapeDtypeStruct((HEADS, S, 1), F32)],
        compiler_params=_params())(q, k, v)


def mla_bwd(name, q, k, v, o, do, lse, scale, tq=256):
    S = q.shape[0]

    def body(q_ref, k_ref, v_ref, o_ref, do_ref, lse_ref, dq_ref, dkv_ref, dkpe_ref, dk_acc, dv_acc):
        dk_acc[...] = jnp.zeros_like(dk_acc)
        dv_acc[...] = jnp.zeros_like(dv_acc)
        for i in range(S // tq):
            kext = (i + 1) * tq
            blk = slice(i * tq, kext)
            qi, kk, vv = q_ref[blk, :], k_ref[:kext, :], v_ref[:kext, :]
            doi = do_ref[blk, :]
            s = _causal_tail(_dot(qi, kk, "nt") * scale, tq, NEG_INF)
            p = jnp.exp(s - lse_ref[0, blk, :])
            dp = _dot(doi, vv, "nt")
            delta = jnp.sum(doi * o_ref[blk, :], axis=-1, keepdims=True)
            ds = p * (dp - delta) * scale
            dq_ref[blk, :] = _dot(ds, kk)
            dk_acc[:kext, :] += _dot(ds, qi, "tn")
            dv_acc[:kext, :] += _dot(p, doi, "tn")
        dk = dk_acc[...]
        lane = _lane(dk.shape)
        dkv_ref[...] = jnp.where(lane < NOPE, dk, 0.0) + dv_acc[...]
        dkpe = jnp.where((lane >= KPE_LO) & (lane < KPE_LO + ROPE), dk, 0.0)
        h = pl.program_id(0)

        @pl.when(h == 0)
        def _():
            dkpe_ref[...] = dkpe

        @pl.when(h > 0)
        def _():
            dkpe_ref[...] += dkpe

    return pl.pallas_call(
        body, name=name, grid=(HEADS,),
        in_specs=[_col(LANES)] * 5 + [pl.BlockSpec((1, S, 1), lambda h: (h, 0, 0))],
        out_specs=[_col(LANES), _col(LANES), pl.BlockSpec((S, LANES), lambda h: (0, 0))],
        out_shape=[jax.ShapeDtypeStruct((S, HEADS * LANES), F32), jax.ShapeDtypeStruct((S, HEADS * LANES), F32),
                   jax.ShapeDtypeStruct((S, LANES), F32)],
        scratch_shapes=[pltpu.VMEM((S, LANES), F32), pltpu.VMEM((S, LANES), F32)],
        compiler_params=_params())(q, k, v, o, do, lse)


BAND_TQ = SPAN


def _band_blocks(L, tq):
    return [(i * tq, (i + 1) * tq, max(0, i * tq - SPAN)) for i in range(L // tq)]


def _band_mask(q0, q1, k0):
    shape = (q1 - q0, q1 - k0)
    dist = (lax.broadcasted_iota(jnp.int32, shape, 0) + q0) - (lax.broadcasted_iota(jnp.int32, shape, 1) + k0)
    return (dist >= 0) & (dist <= SPAN)


def _class_rows(r, dil, lo, hi):
    return pl.ds(r + dil * lo, hi - lo, stride=dil) if dil > 1 else pl.ds(lo, hi - lo)


def _stack_heads(t, lo):
    zero = jnp.zeros_like(t)
    return jnp.concatenate([jnp.where(lo, t, zero), jnp.where(lo, zero, t)], axis=0)


def _band_mask2(q0, q1, k0):
    n = q1 - q0
    shape = (2 * n, q1 - k0)
    i = lax.broadcasted_iota(jnp.int32, shape, 0)
    dist = (jnp.where(i >= n, i - n, i) + q0) - (lax.broadcasted_iota(jnp.int32, shape, 1) + k0)
    return (dist >= 0) & (dist <= SPAN)


def _pair_col(col0=0):
    return pl.BlockSpec((SEQ, LANES), lambda j: (0, col0 // LANES + j))


def band_fwd(name, q, k, v, dil):
    S = q.shape[0]
    L = S // dil
    tq = BAND_TQ
    scale = DIL_DIM ** -0.5

    def body(q_ref, k_ref, v_ref, o_ref, lse_ref):
        items = [(r, blk) for r in range(dil) for blk in _band_blocks(L, tq)]
        lo = _lane((tq, LANES)) < DIL_DIM

        def scores(item):
            r, (q0, q1, k0) = item
            qb = q_ref[_class_rows(r, dil, q0, q1), :].astype(MXU_DTYPE)
            return _dot(_stack_heads(qb, lo), k_ref[_class_rows(r, dil, k0, q1), :], "nt")

        def softmax(item, s):
            _, (q0, q1, k0) = item
            s = jnp.where(_band_mask2(q0, q1, k0), s * scale, NEG_INF)
            mx = jnp.max(s, axis=-1, keepdims=True)
            e = jnp.exp(s - mx)
            l = jnp.sum(e, axis=-1, keepdims=True)
            return (e * (1.0 / l)).astype(MXU_DTYPE), mx + jnp.log(l)

        def weighted(item, p, lse):
            r, (q0, q1, k0) = item
            pv = _dot(p, v_ref[_class_rows(r, dil, k0, q1), :])
            o_ref[_class_rows(r, dil, q0, q1), :] = jnp.where(lo, pv[:tq], pv[tq:])
            lse_ref[_class_rows(r, dil, q0, q1), :] = jnp.where(lo, lse[:tq], lse[tq:])

        s, prev = scores(items[0]), None
        for i, item in enumerate(items):
            s_next = scores(items[i + 1]) if i + 1 < len(items) else None
            if prev is not None:
                weighted(items[i - 1], *prev)
            prev, s = softmax(item, s), s_next
        weighted(items[-1], *prev)

    return pl.pallas_call(
        body, name=name, grid=(DIL_WIDTH // LANES,), in_specs=[_pair_col()] * 2 + [_pair_col(P_VD)], out_specs=[_pair_col()] * 2,
        out_shape=[jax.ShapeDtypeStruct((S, DIL_WIDTH), F32)] * 2, compiler_params=_params())(q, k, v)


def band_bwd(name, q, k, v, lse, lse_mix, o_cat, do_cat, dil):
    S = q.shape[0]
    L = S // dil
    tq = BAND_TQ
    scale = DIL_DIM ** -0.5

    def body(q_ref, k_ref, v_ref, lse_ref, mix_ref, o_ref, do_ref, dq_ref, dk_ref, dv_ref):
        dk_ref[...] = jnp.zeros_like(dk_ref)
        dv_ref[...] = jnp.zeros_like(dv_ref)
        items = [(r, blk) for r in range(dil) for blk in _band_blocks(L, tq)]
        lo = _lane((tq, LANES)) < DIL_DIM
        per_head = lambda t: jnp.concatenate([t[:, 0:1], t[:, DIL_DIM:DIL_DIM + 1]], axis=0)

        def scores(item):
            r, (q0, q1, k0) = item
            qrows, krows = _class_rows(r, dil, q0, q1), _class_rows(r, dil, k0, q1)
            lse_p, dout = lse_ref[qrows, :], do_ref[qrows, :]
            w2 = per_head(jnp.exp(lse_p - mix_ref[qrows, :]))
            dd = dout * o_ref[qrows, :]
            big_d = jnp.concatenate([jnp.sum(jnp.where(lo, dd, 0.0), axis=-1, keepdims=True),
                                     jnp.sum(jnp.where(lo, 0.0, dd), axis=-1, keepdims=True)], axis=0)
            q2 = _stack_heads(q_ref[qrows, :].astype(MXU_DTYPE), lo)
            dom = (_stack_heads(dout, lo) * w2).astype(MXU_DTYPE)
            return (_dot(q2, k_ref[krows, :], "nt"), _dot(dom, v_ref[krows, :], "nt"), per_head(lse_p), w2 * big_d, q2, dom)

        def softmax_bwd(item, s, dp, lse2, wd2, q2, dom):
            _, (q0, q1, k0) = item
            p = jnp.where(_band_mask2(q0, q1, k0), jnp.exp(s * scale - lse2), 0.0)
            return p.astype(MXU_DTYPE), (p * (dp - wd2) * scale).astype(MXU_DTYPE), q2, dom

        def grads(item, p, ds, q2, dom):
            r, (q0, q1, k0) = item
            qrows, krows = _class_rows(r, dil, q0, q1), _class_rows(r, dil, k0, q1)
            dq2 = _dot(ds, k_ref[krows, :])
            dq_ref[qrows, :] = jnp.where(lo, dq2[:tq], dq2[tq:])
            dk_ref[krows, :] += _dot(ds, q2, "tn")
            dv_ref[krows, :] += _dot(p, dom, "tn")

        sc, prev = scores(items[0]), None
        for i, item in enumerate(items):
            sc_next = scores(items[i + 1]) if i + 1 < len(items) else None
            if prev is not None:
                grads(items[i - 1], *prev)
            prev, sc = softmax_bwd(item, *sc), sc_next
        grads(items[-1], *prev)

    cat = _pair_col(HEADS * LANES)
    return pl.pallas_call(
        body, name=name, grid=(DIL_WIDTH // LANES,),
        in_specs=[_pair_col()] * 2 + [_pair_col(P_VD)] + [_pair_col()] * 2 + [cat] * 2, out_specs=[_pair_col()] * 3,
        out_shape=[jax.ShapeDtypeStruct((S, DIL_WIDTH), F32)] * 3,
        compiler_params=_params())(q, k, v, lse, lse_mix, o_cat, do_cat)


def combine_fwd(name, outs, lses, o_cat, tm=512):
    S = outs[0].shape[0]

    def body(o1, o2, o3, l1, l2, l3, cat_in, cat_out, mix_ref):
        ls = [l1[...], l2[...], l3[...]]
        m = jnp.maximum(jnp.maximum(ls[0], ls[1]), ls[2])
        e = [jnp.exp(l - m) for l in ls]
        den = e[0] + e[1] + e[2]
        cat_out[...] = (e[0] / den) * o1[...] + (e[1] / den) * o2[...] + (e[2] / den) * o3[...]
        mix_ref[...] = m + jnp.log(den)

    row = pl.BlockSpec((tm, DIL_WIDTH), lambda i: (i, 0))
    return pl.pallas_call(
        body, name=name, grid=(S // tm,), in_specs=[row] * 6 + [pl.BlockSpec(memory_space=pl.ANY)],
        out_specs=[pl.BlockSpec((tm, DIL_WIDTH), lambda i: (i, HEADS * LANES // DIL_WIDTH)), row],
        out_shape=[jax.ShapeDtypeStruct(o_cat.shape, F32), jax.ShapeDtypeStruct((S, DIL_WIDTH), F32)],
        input_output_aliases={6: 0}, compiler_params=_params())(*outs, *lses, o_cat)


def _shift_down(u, n, zero_head):
    out = pltpu.roll(u, n, axis=0)
    return jnp.where(lax.broadcasted_iota(jnp.int32, u.shape, 0) >= n, out, 0.0) if zero_head else out


def _shift_up(u, n, zero_tail):
    rows = u.shape[0]
    out = pltpu.roll(u, rows - n, axis=0)
    return jnp.where(lax.broadcasted_iota(jnp.int32, u.shape, 0) < rows - n, out, 0.0) if zero_tail else out


CONV_ROWS = 512
CONV_HALO = 16


def _conv_chunks(S, tail):
    out = []
    for r0 in range(0, S, CONV_ROWS):
        lo, hi = max(0, r0 - CONV_HALO), min(S, r0 + CONV_ROWS + (CONV_HALO if tail else 0))
        out.append((lo, hi, r0 - lo, CONV_ROWS))
    return out


CONV_TC = 256
CONV_NB = D_FF // CONV_TC


def _half_specs(rows, rows_axis=False):
    if rows_axis:
        return [pl.BlockSpec((rows, D_MODEL), lambda j: (j, 0)), pl.BlockSpec((rows, D_MODEL), lambda j: (j + CONV_NB, 0))]
    return [pl.BlockSpec((rows, CONV_TC), lambda j: (0, j)), pl.BlockSpec((rows, CONV_TC), lambda j: (0, j + CONV_NB))]


def _whole(a):
    return pl.BlockSpec(a.shape, lambda j: (0,) * a.ndim)


def _up_pair(h, ug_ref, uv_ref):
    return jnp.concatenate([_dot(h, ug_ref[...], "nt"), _dot(h, uv_ref[...], "nt")], axis=1)


def _conv_taps(uin, w, b, starts):
    u1, u2 = _shift_down(uin, 1, starts), _shift_down(uin, 2, starts)
    return u1, u2, w[2:3, :] * uin + w[1:2, :] * u1 + w[0:1, :] * u2 + b


def ffn_fwd(name, h, w_up_t, w_conv, b_conv, w_down):
    S = h.shape[0]

    def body(h_ref, ug_ref, uv_ref, wg_ref, wv_ref, bg_ref, bv_ref, wd_ref, dn_ref):
        @pl.when(pl.program_id(0) == 0)
        def _():
            dn_ref[...] = jnp.zeros_like(dn_ref)

        w = jnp.concatenate([wg_ref[...], wv_ref[...]], axis=1)
        b = jnp.concatenate([bg_ref[...], bv_ref[...]], axis=1)
        chunks = _conv_chunks(S, tail=False)

        def project(c):
            return _up_pair(h_ref[c[0]:c[1], :], ug_ref, uv_ref)

        def gate(c, uin):
            lo, hi, keep, rows = c
            u = _conv_taps(uin, w, b, lo == 0)[2][keep:keep + rows]
            return (_silu(u[:, :CONV_TC]) * u[:, CONV_TC:]).astype(MXU_DTYPE)

        def project_down(c, act):
            dn_ref[c[0] + c[2]:c[0] + c[2] + c[3], :] += _dot(act, wd_ref[...])

        uin, act_prev = project(chunks[0]), None
        for i, c in enumerate(chunks):
            uin_next = project(chunks[i + 1]) if i + 1 < len(chunks) else None
            if act_prev is not None:
                project_down(chunks[i - 1], act_prev)
            act_prev = gate(c, uin)
            uin = uin_next
        project_down(chunks[-1], act_prev)

    return pl.pallas_call(
        body, name=name, grid=(CONV_NB,),
        in_specs=[_whole(h)] + _half_specs(CONV_TC, rows_axis=True) + _half_specs(3) + _half_specs(1)
        + [pl.BlockSpec((CONV_TC, w_down.shape[1]), lambda j: (j, 0))],
        out_specs=pl.BlockSpec((S, w_down.shape[1]), lambda j: (0, 0)), out_shape=jax.ShapeDtypeStruct((S, w_down.shape[1]), F32),
        compiler_params=_params())(h, w_up_t, w_up_t, w_conv, w_conv, b_conv, b_conv, w_down)


def ffn_bwd(name, h, w_up_t, w_conv, b_conv, d_dn, w_down):
    S, D = h.shape

    def body(h_ref, ug_ref, uv_ref, wg_ref, wv_ref, bg_ref, bv_ref, dd_ref, wd_ref,
             dh_ref, gug_ref, guv_ref, gd_ref, dwg_ref, dwv_ref, dbg_ref, dbv_ref):
        @pl.when(pl.program_id(0) == 0)
        def _():
            dh_ref[...] = jnp.zeros_like(dh_ref)

        w = jnp.concatenate([wg_ref[...], wv_ref[...]], axis=1)
        b = jnp.concatenate([bg_ref[...], bv_ref[...]], axis=1)
        w_pair = jnp.concatenate([ug_ref[...], uv_ref[...]], axis=0)
        chunks = _conv_chunks(S, tail=True)

        def project(c):
            return _up_pair(h_ref[c[0]:c[1], :], ug_ref, uv_ref), _dot(dd_ref[c[0]:c[1], :], wd_ref[...], "nt")

        def through_conv(c, uin, da):
            lo, hi, keep, rows = c
            u1, u2, u = _conv_taps(uin, w, b, lo == 0)
            gate, val = u[:, :CONV_TC], u[:, CONV_TC:]
            sig = 1.0 / (1.0 + jnp.exp(-gate))
            du = jnp.concatenate([da * val * (sig * (1.0 + gate * (1.0 - sig))), da * (gate * sig)], axis=1)
            dup = w[2:3, :] * du + w[1:2, :] * _shift_up(du, 1, hi == S) + w[0:1, :] * _shift_up(du, 2, hi == S)
            kept = slice(keep, keep + rows)
            du = du[kept]
            dw = jnp.concatenate([_colsum(du * u2[kept]), _colsum(du * u1[kept]), _colsum(du * uin[kept])], axis=0)
            return dup[kept].astype(MXU_DTYPE), (gate * sig * val)[kept].astype(MXU_DTYPE), dw, _colsum(du)

        def weight_grads(c, dup, act):
            out_rows = slice(c[0] + c[2], c[0] + c[2] + c[3])
            dh_ref[out_rows, :] += _dot(dup, w_pair)
            return _dot(dup, h_ref[out_rows, :], "tn"), _dot(act, dd_ref[out_rows, :], "tn")

        dw, db, g_up, g_dn = 0.0, 0.0, 0.0, 0.0
        proj, done = project(chunks[0]), None
        for i, c in enumerate(chunks):
            proj_next = project(chunks[i + 1]) if i + 1 < len(chunks) else None
            if done is not None:
                gu, gd = weight_grads(chunks[i - 1], *done)
                g_up, g_dn = g_up + gu, g_dn + gd
            dup, act, dw_c, db_c = through_conv(c, *proj)
            dw, db, done, proj = dw + dw_c, db + db_c, (dup, act), proj_next
        gu, gd = weight_grads(chunks[-1], *done)
        g_up, g_dn = g_up + gu, g_dn + gd
        gug_ref[...], guv_ref[...] = g_up[:CONV_TC].astype(gug_ref.dtype), g_up[CONV_TC:].astype(guv_ref.dtype)
        gd_ref[...] = g_dn.astype(gd_ref.dtype)
        dwg_ref[...], dwv_ref[...] = dw[:, :CONV_TC], dw[:, CONV_TC:]
        dbg_ref[...], dbv_ref[...] = db[:, :CONV_TC], db[:, CONV_TC:]

    half = lambda rows: pl.BlockSpec((rows, CONV_TC), lambda j: (0, j))
    rows_blk = pl.BlockSpec((CONV_TC, D), lambda j: (j, 0))
    dh, gug, guv, gd, dwg, dwv, dbg, dbv = pl.pallas_call(
        body, name=name, grid=(CONV_NB,),
        in_specs=[_whole(h)] + _half_specs(CONV_TC, rows_axis=True) + _half_specs(3) + _half_specs(1) + [_whole(d_dn), rows_blk],
        out_specs=[pl.BlockSpec((S, D), lambda j: (0, 0)), rows_blk, rows_blk, rows_blk, half(3), half(3), half(1), half(1)],
        out_shape=[jax.ShapeDtypeStruct((S, D), F32)] + [jax.ShapeDtypeStruct((D_FF, D), MXU_DTYPE)] * 3
        + [jax.ShapeDtypeStruct((3, D_FF), F32)] * 2 + [jax.ShapeDtypeStruct((1, D_FF), F32)] * 2,
        compiler_params=_params())(h, w_up_t, w_up_t, w_conv, w_conv, b_conv, b_conv, d_dn, w_down)
    return dh, gug, guv, gd, jnp.concatenate([dwg, dwv], axis=1), jnp.concatenate([dbg, dbv], axis=1)


def adamw(name, w, parts, m, v, tr=None):
    R, C = w.shape
    tr = tr or R
    assert R % tr == 0
    c1 = 1.0 - ADAM_B1 ** ADAM_STEP
    c2 = 1.0 - ADAM_B2 ** ADAM_STEP
    np_ = len(parts)

    def body(*refs):
        w_ref, m_ref, v_ref = refs[0], refs[1 + np_], refs[2 + np_]
        go_ref, d_ref, mo_ref, vo_ref = refs[3 + np_:]
        terms = []
        for part, ref in zip(parts, refs[1:1 + np_], strict=True):
            terms += [ref[...]] if part.ndim == 2 else [ref[p] for p in range(part.shape[0])]
        g = terms[0].astype(F32)
        for term in terms[1:]:
            g = g + term.astype(F32)
        m2 = ADAM_B1 * m_ref[...] + (1.0 - ADAM_B1) * g
        v2 = ADAM_B2 * v_ref[...] + (1.0 - ADAM_B2) * (g * g)
        go_ref[...] = g
        mo_ref[...] = m2
        vo_ref[...] = v2
        d_ref[...] = -ADAM_LR * ((m2 / c1) / (jnp.sqrt(v2 / c2) + ADAM_EPS) + ADAM_WD * w_ref[...])

    blk = pl.BlockSpec((tr, C), lambda i: (i, 0))
    part_specs = [blk if p.ndim == 2 else pl.BlockSpec((p.shape[0], tr, C), lambda i: (0, i, 0)) for p in parts]
    return pl.pallas_call(
        body, name=name, grid=(R // tr,),
        in_specs=[blk] + part_specs + [blk, blk], out_specs=[blk] * 4,
        out_shape=[jax.ShapeDtypeStruct((R, C), F32)] * 4, compiler_params=_params())(w, *parts, m, v)


def _place():
    return lax.axis_index("x"), lax.axis_index("y"), lax.axis_index("c")


def all_gather(name, arrs, after=None):
    n = len(arrs)
    deps = [] if after is None else [after]

    def body(*refs):
        ins, outs = refs[:n], refs[n + len(deps):2 * n + len(deps)]
        send_sems, recv_sems, local_sems = refs[2 * n + len(deps):]
        x, y, c = _place()
        me, sibling = (x, y, c), (x, y, 1 - c)
        chips = [(1 - x, y), (x, 1 - y), (1 - x, 1 - y)]
        sends = []
        for t in range(n):
            out = outs[t]

            def slot(px, py, pc, out=out):
                return out.at[4 * px + 2 * py + pc]

            def copy(k, block, to, src=None, t=t, slot=slot):
                return pltpu.make_async_remote_copy(
                    src_ref=slot(*block) if src is None else src, dst_ref=slot(*block),
                    send_sem=send_sems.at[7 * t + k], recv_sem=recv_sems.at[7 * t + k],
                    device_id=to, device_id_type=MESH_ID)

            mine = pltpu.make_async_copy(ins[t], slot(*me), local_sems.at[t])
            mine.start()
            first = [copy(0, me, sibling, src=ins[t])]
            first += [copy(1 + j, me, (*chip, c), src=ins[t]) for j, chip in enumerate(chips)]
            for cp in first:
                cp.start()
            sends.append((mine, first, copy))
        for t in range(n):
            mine, first, copy = sends[t]
            passed = [copy(4 + j, (*chip, c), sibling) for j, chip in enumerate(chips)]
            for j, chip in enumerate(chips):
                copy(1 + j, (*chip, c), me).wait_recv()
                passed[j].start()
            copy(0, sibling, me).wait_recv()
            for j, chip in enumerate(chips):
                copy(4 + j, (*chip, 1 - c), me).wait_recv()
            for cp in first + passed:
                cp.wait_send()
            mine.wait()

    any_spec = pl.BlockSpec(memory_space=pl.ANY)
    res = pl.pallas_call(
        body, name=name, in_specs=[any_spec] * (n + len(deps)), out_specs=[any_spec] * n,
        out_shape=[jax.ShapeDtypeStruct((N_DEV,) + a.shape, a.dtype) for a in arrs],
        scratch_shapes=[pltpu.SemaphoreType.DMA((7 * n,)), pltpu.SemaphoreType.DMA((7 * n,)), pltpu.SemaphoreType.DMA((n,))],
        compiler_params=pltpu.CompilerParams(has_side_effects=True))(*arrs, *deps)
    return list(res)


HBM_SPEC = pl.BlockSpec(memory_space=pltpu.HBM)
SEM_SPEC = pl.BlockSpec(memory_space=pltpu.SEMAPHORE)
DATAFLOW = pltpu.SideEffectType.DATAFLOW_SIDE_EFFECTING


def _exchange_copies(srcs, lands, send_sems, recv_sems, gather, first=0):
    x, y, c = _place()
    me = 4 * x + 2 * y + c
    out = []
    for t, (src, land) in enumerate(zip(srcs, lands, strict=True)):
        for k in range(1, N_DEV):
            px, py, pc = x ^ (k >> 2), y ^ ((k >> 1) & 1), c ^ (k & 1)
            sem = 7 * (first + t) + k - 1
            out.append(pltpu.make_async_remote_copy(
                src_ref=src if gather else src.at[4 * px + 2 * py + pc],
                dst_ref=land.at[me] if gather else land.at[k - 1],
                send_sem=send_sems.at[sem], recv_sem=recv_sems.at[sem],
                device_id=(px, py, pc), device_id_type=MESH_ID))
    return out


def exchange_start(name, arrs, gather, after=None):
    n = len(arrs)
    lands = [lax.empty(((N_DEV,) + a.shape) if gather else ((N_DEV - 1,) + a.shape[1:]), a.dtype) for a in arrs]
    deps = [] if after is None else [after]

    def body(*refs):
        srcs, land_refs = refs[:n], refs[n:2 * n]
        send_sems, recv_sems = refs[2 * n + len(deps)], refs[2 * n + len(deps) + 1]
        token = refs[-1]
        for cp in _exchange_copies(srcs, land_refs, send_sems, recv_sems, gather):
            cp.start()
        token[...] = jnp.zeros_like(token)

    hbm = lambda a: pltpu.HBM(a.shape, a.dtype)
    res = pl.pallas_call(
        body, name=name,
        out_shape=(pltpu.SemaphoreType.DMA((7 * n,)), pltpu.SemaphoreType.DMA((7 * n,)), *[hbm(a) for a in arrs],
                   *[hbm(l) for l in lands], jax.ShapeDtypeStruct((8, 128), F32)),
        in_specs=[HBM_SPEC] * (2 * n) + [pl.BlockSpec(memory_space=pl.ANY)] * len(deps),
        out_specs=(SEM_SPEC, SEM_SPEC, *[HBM_SPEC] * (2 * n), pl.BlockSpec(memory_space=pltpu.VMEM)),
        input_output_aliases={i: 2 + i for i in range(2 * n)},
        compiler_params=pltpu.CompilerParams(has_side_effects=DATAFLOW),
    )(*[pltpu.with_memory_space_constraint(a, pltpu.HBM) for a in arrs + lands], *deps)
    return res[0], res[1], list(res[2:2 + n]), list(res[2 + n:2 + 2 * n]), res[-1]


def exchange_wait(name, started, gather, after, first=0, count=None):
    send_sems, recv_sems, srcs, lands, _ = started
    count = len(srcs) - first if count is None else count
    srcs, lands = srcs[first:first + count], lands[first:first + count]
    n = len(srcs)

    def body(*refs):
        src_refs, land_refs = refs[:n], refs[n:2 * n]
        copies = _exchange_copies(src_refs, land_refs, refs[2 * n], refs[2 * n + 1], gather, first)
        for cp in copies:
            cp.wait_send()
        for cp in copies:
            cp.wait_recv()

    hbm = lambda a: pltpu.HBM(a.shape, a.dtype)
    res = pl.pallas_call(
        body, name=name, out_shape=tuple(hbm(a) for a in srcs + lands),
        in_specs=[HBM_SPEC] * (2 * n) + [SEM_SPEC, SEM_SPEC, pl.BlockSpec(memory_space=pl.ANY)],
        out_specs=tuple([HBM_SPEC] * (2 * n)), input_output_aliases={i: i for i in range(2 * n)},
        compiler_params=pltpu.CompilerParams(has_side_effects=DATAFLOW),
    )(*srcs, *lands, send_sems, recv_sems, after)
    return list(res[:n]), list(res[n:])


def _gather_cols(stack):
    p, k, n = stack.shape
    return stack.transpose(1, 0, 2).reshape(k, p * n)


def _scatter_cols(full):
    k, n = full.shape
    return full.reshape(k, N_DEV, n // N_DEV).transpose(1, 0, 2)


def _gather_rows(stack):
    p, r, n = stack.shape
    return stack.reshape(p * r, n)


def _scatter_rows(full):
    r, n = full.shape
    return full.reshape(N_DEV, r // N_DEV, n)


_IN_NAT = Q_LORA + KV_LORA
TRANSPOSED = ("w_in", "w_q_b", "w_up")


def to_kernel_layout(name, w):
    if name == "w_in":
        z = lambda n: jnp.zeros((n, w.shape[1]), w.dtype)
        return jnp.concatenate([w[:_IN_NAT], z(KPE_LO), w[_IN_NAT:_IN_NAT + ROPE], z(LANES - KPE_LO - ROPE), w[_IN_NAT + ROPE:]], axis=0)
    if name == "w_q_b":
        return jnp.pad(w.reshape(HEADS, NOPE + ROPE, -1), ((0, 0), (0, LANES - NOPE - ROPE), (0, 0))).reshape(HEADS * LANES, -1)
    if name == "w_o":
        mla = jnp.pad(w[:HEADS * NOPE].reshape(HEADS, NOPE, -1), ((0, 0), (LANES - NOPE, 0), (0, 0))).reshape(HEADS * LANES, -1)
        return jnp.concatenate([mla, w[HEADS * NOPE:]], axis=0)
    return w


def from_kernel_layout(name, g):
    if name == "w_in":
        return jnp.concatenate([g[:_IN_NAT], g[P_KPE + KPE_LO:P_KPE + KPE_LO + ROPE], g[P_QD:]], axis=0)
    if name == "w_q_b":
        return g.reshape(HEADS, LANES, -1)[:, :NOPE + ROPE, :].reshape(HEADS * (NOPE + ROPE), -1)
    if name == "w_o":
        mla = g[:HEADS * LANES].reshape(HEADS, LANES, -1)[:, LANES - NOPE:, :].reshape(HEADS * NOPE, -1)
        return jnp.concatenate([mla, g[HEADS * LANES:]], axis=0)
    return g


SMALL_COLS = 1024
SMALL_ROWS = 24
SMALL_AT = {"loss": (0, 0, 1), "b_ada": (1, 0, 6 * D_MODEL), "g_mix_norm": (7, 0, D_MODEL), "g_q_lat": (8, 0, Q_LORA),
            "g_kv_lat": (9, 0, KV_LORA), "g_mla_q_nope": (10, 0, NOPE), "g_mla_q_pe": (10, 128, ROPE),
            "g_mla_k_nope": (10, 256, NOPE), "g_mla_k_pe": (10, 384, ROPE), "g_dil_q": (10, 512, DIL_DIM),
            "g_dil_k": (10, 640, DIL_DIM), "g_ffn_norm": (11, 0, D_MODEL), "b_conv": (12, 0, 2 * D_FF)}
SMALL_PARAMS = tuple(n for n in SMALL_AT if n != "loss")


def _pack_small(values):
    by_row = {}
    for name, (row, off, n) in SMALL_AT.items():
        by_row.setdefault(row, []).append((off, values[name].reshape(-1).astype(F32)))
    out = []
    for row in sorted(by_row):
        pieces, at = [], 0
        for off, v in sorted(by_row[row], key=lambda t: t[0]):
            pieces += [jnp.zeros((off - at,), F32), v]
            at = off + v.shape[0]
        flat = jnp.concatenate(pieces)
        nrows = -(-flat.shape[0] // SMALL_COLS)
        out.append(jnp.pad(flat, (0, nrows * SMALL_COLS - flat.shape[0])).reshape(nrows, SMALL_COLS))
    packed = jnp.concatenate(out, axis=0)
    return jnp.pad(packed, ((0, SMALL_ROWS - packed.shape[0]), (0, 0)))


def _adam(w, g, m, v):
    c1 = 1.0 - ADAM_B1 ** ADAM_STEP
    c2 = 1.0 - ADAM_B2 ** ADAM_STEP
    m2 = ADAM_B1 * m + (1.0 - ADAM_B1) * g
    v2 = ADAM_B2 * v + (1.0 - ADAM_B2) * (g * g)
    return -ADAM_LR * ((m2 / c1) / (jnp.sqrt(v2 / c2) + ADAM_EPS) + ADAM_WD * w), m2, v2


def adamw_small(name, stack, params):
    flat = [a for n in SMALL_PARAMS for a in params[n]]

    def body(stack_ref, *refs):
        ins, outs = refs[:len(flat)], refs[len(flat):]
        g_all = stack_ref[0]
        for p in range(1, N_DEV):
            g_all = g_all + stack_ref[p]
        outs[0][...] = g_all[0:1, 0:1]
        for i, pname in enumerate(SMALL_PARAMS):
            row, off, n = SMALL_AT[pname]
            w_ref, m_ref, v_ref = ins[3 * i:3 * i + 3]
            go_ref, d_ref, mo_ref, vo_ref = outs[1 + 4 * i:5 + 4 * i]
            for c0 in range(0, n, SMALL_COLS):
                cn = min(SMALL_COLS, n - c0)
                r = row + c0 // SMALL_COLS
                g = g_all[r:r + 1, off:off + cn]
                cols = (slice(None), slice(c0, c0 + cn))
                d, m2, v2 = _adam(w_ref[cols], g, m_ref[cols], v_ref[cols])
                go_ref[cols], d_ref[cols], mo_ref[cols], vo_ref[cols] = g, d, m2, v2

    whole = lambda a: pl.BlockSpec(a.shape, lambda: (0,) * a.ndim)
    out_shape = [jax.ShapeDtypeStruct((1, 1), F32)] + [jax.ShapeDtypeStruct(a.shape, F32) for n in SMALL_PARAMS for a in params[n][:1] * 4]
    res = pl.pallas_call(body, name=name, in_specs=[whole(stack)] + [whole(a) for a in flat],
                         out_specs=[pl.BlockSpec(s.shape, lambda s=s: (0,) * len(s.shape)) for s in out_shape],
                         out_shape=out_shape, compiler_params=_params())(stack, *flat)
    return res[0], {n: res[1 + 4 * i:5 + 4 * i] for i, n in enumerate(SMALL_PARAMS)}


def _local_step(x, pos, mod, target, w, fetch, emit):
    S = SEQ
    sh1, sc1, g1, sh2, sc2, g2 = [mod[:, i * D_MODEL:(i + 1) * D_MODEL] for i in range(6)]
    zeros = lambda n: jnp.zeros((1, n), F32)
    g_q = jnp.concatenate([w["g_mla_q_nope"], w["g_mla_q_pe"], zeros(LANES - NOPE - ROPE)], axis=1)
    g_k = jnp.concatenate([w["g_mla_k_nope"], zeros(LANES - NOPE)], axis=1)
    g_kpe = jnp.concatenate([zeros(KPE_LO), w["g_mla_k_pe"], zeros(LANES - KPE_LO - ROPE)], axis=1)
    g_dq = jnp.concatenate([w["g_dil_q"]] * 2, axis=1)
    g_dk = jnp.concatenate([w["g_dil_k"]] * 2, axis=1)
    b_conv = w["b_conv"]

    def inv_freq(d):
        return jnp.power(ROPE_THETA, -2.0 * jnp.arange(d // 2, dtype=F32) / d)

    f_mla = jnp.concatenate([jnp.zeros((KPE_LO,), F32), inv_freq(ROPE), inv_freq(ROPE), jnp.zeros((LANES - KPE_LO - ROPE,), F32)])
    f_dil = jnp.concatenate([inv_freq(DIL_DIM)] * 4)

    def tables_fn(rows, params):
        (p,), (fa, fb) = rows, params
        return [jnp.cos(p * fa), jnp.sin(p * fa), jnp.cos(p * fb), jnp.sin(p * fb)], []

    cos_m, sin_m, cos_d, sin_d = rowwise("rope_tables", tables_fn, [pos], [f_mla.reshape(1, LANES), f_dil.reshape(1, LANES)],
                                         [(LANES, F32)] * 4)
    tables = [cos_m, sin_m, cos_d, sin_d]
    H_M, H_D = ROPE // 2, DIL_DIM // 2

    def ln1_fn(rows, params):
        (xv,), (g, sc, sh) = rows, params
        y, _, _ = _rms(xv, g)
        return [y * (1.0 + sc) + sh], []

    (h,) = rowwise("ln1_fwd", ln1_fn, [x], [w["g_mix_norm"], sc1, sh1], [(D_MODEL, MXU_DTYPE)])
    w_in = fetch("w_in", h)
    proj = matmul("proj_fwd", h, w_in, "nt", tm=512)

    def post_fn(rows, params):
        (pv, cm, sm, cd, sd), (gq, gkv, gkp, gdq, gdk) = rows, params
        kper = _rope(_grms(pv[:, P_KPE:P_QD], gkp, KPE_GROUPS)[0], cm, sm, H_M)
        qd = [_rope(_grms(c, gdq, DIL_GROUPS)[0], cd, sd, H_D) for c in _chunks(pv[:, P_QD:P_KD])]
        kd = [_rope(_grms(c, gdk, DIL_GROUPS)[0], cd, sd, H_D) for c in _chunks(pv[:, P_KD:P_VD])]
        return [_rms(pv[:, P_QLAT:P_KVLAT], gq)[0], _rms(pv[:, P_KVLAT:P_KPE], gkv)[0], kper,
                jnp.concatenate(qd, axis=1), jnp.concatenate(kd, axis=1)], []

    post_params = [w["g_q_lat"], w["g_kv_lat"], g_kpe, g_dq, g_dk]
    qln, kvn, kper, qd_r, kd_r = rowwise(
        "proj_post", post_fn, [proj] + tables, post_params,
        [(Q_LORA, MXU_DTYPE), (KV_LORA, MXU_DTYPE), (LANES, MXU_DTYPE)] + [(DIL_WIDTH, F32)] * 2, tm=256)
    w_q_b, w_kv_b = fetch("w_q_b", qln), fetch("w_kv_b", kvn)
    q = matmul("q_fwd", qln, w_q_b, "nt", tm=1024)
    kv = matmul("kv_fwd", kvn, w_kv_b, "nn", tm=1024)

    def mla_prep_fn(rows, params):
        (qv, kvv, kp, cm, sm), (gq, gk) = rows, params
        value_lanes = _lane(kp.shape) >= NOPE
        qs, ks, vs = [], [], []
        for qc, kc in zip(_chunks(qv), _chunks(kvv), strict=True):
            qs.append(_rope(_grms(qc, gq, Q_GROUPS)[0], cm, sm, H_M))
            ks.append(_grms(kc, gk, K_GROUPS)[0] + kp)
            vs.append(jnp.where(value_lanes, kc, 0.0))
        return [jnp.concatenate(t, axis=1) for t in (qs, ks, vs)], []

    q_mla, k_mla, v_mla = rowwise("mla_prep", mla_prep_fn, [q, kv, kper, cos_m, sin_m], [g_q, g_k],
                                  [(HEADS * LANES, MXU_DTYPE)] * 3, tm=256)
    mla_scale = (NOPE + ROPE) ** -0.5
    o_cat, lse_mla = mla_fwd("mla_fwd", q_mla, k_mla, v_mla, mla_scale)

    band = [band_fwd(f"band{dil}_fwd", qd_r, kd_r, proj, dil) for dil in DILATIONS]
    o_cat, lse_mix = combine_fwd("dil_combine", [b[0] for b in band], [b[1] for b in band], o_cat)
    w_o = fetch("w_o", o_cat)
    mix = matmul("mix_fwd", o_cat, w_o, "nn", tm=512)

    def mid_fn(rows, params):
        (xv, mx), (gate1, g, sc, sh) = rows, params
        x1 = xv + gate1 * mx
        y, _, _ = _rms(x1, g)
        return [x1, y * (1.0 + sc) + sh], []

    x1, h2 = rowwise("mid_fwd", mid_fn, [x, mix], [g1, w["g_ffn_norm"], sc2, sh2], [(D_MODEL, F32), (D_MODEL, MXU_DTYPE)])
    w_up, w_conv, w_down = fetch("w_up", h2), fetch("w_conv", h2), fetch("w_down", h2)
    dn = ffn_fwd("ffn_fwd", h2, w_up, w_conv, b_conv, w_down)

    def final_fn(rows, params):
        (x1v, dnv, tgt), (gate2,) = rows, params
        r = x1v + gate2 * dnv - tgt
        dy = r * (1.0 / D_MODEL)
        loss = jnp.sum(_colsum(r * r), axis=-1, keepdims=True) * (0.5 / D_MODEL)
        return [dy, gate2 * dy], [loss, _colsum(dy * dnv)]

    dy, d_dn, loss, dg2 = rowwise("loss_head", final_fn, [x1, dn, target], [g2], [(D_MODEL, F32), (D_MODEL, MXU_DTYPE)],
                                  [1, D_MODEL])
    dh2, g_up_gate, g_up_val, g_down, g_w_conv, g_b_conv = ffn_bwd("ffn_bwd", h2, w_up, w_conv, b_conv, d_dn, w_down)
    emit("w_down", g_down)
    emit("w_conv", g_w_conv)
    sent = emit("w_up", jnp.concatenate([g_up_gate, g_up_val], axis=0))

    def mid_bwd_fn(rows, params):
        (dh2v, dyv, x1v, mx), (gate1, g, sc) = rows, params
        yn, n, rstd = _rms(x1v, g)
        dx_n, dg = _rms_bwd(dh2v * (1.0 + sc), n, rstd, g)
        dx1 = dyv + dx_n
        return [dx1, gate1 * dx1], [dg, _colsum(dh2v * yn), _colsum(dh2v), _colsum(dx1 * mx)]

    dx1, dmix, dg_ffn, dsc2, dsh2, dg1 = rowwise(
        "mid_bwd", mid_bwd_fn, [dh2, dy, x1, mix], [g1, w["g_ffn_norm"], sc2], [(D_MODEL, F32), (D_MODEL, MXU_DTYPE)],
        [D_MODEL] * 4, dep=sent)

    sent = emit("w_o", matmul("mix_wgrad", o_cat, dmix, "tn", tm=512, out_dtype=MXU_DTYPE))
    do_cat = matmul("mix_dgrad", dmix, w_o, "nt", tm=512, dep=sent)
    dband = [band_bwd(f"band{dil}_bwd", qd_r, kd_r, proj, b[1], lse_mix, o_cat, do_cat, dil) for dil, b in zip(DILATIONS, band)]
    dq_mla, dkv_mla, dkper = mla_bwd("mla_bwd", q_mla, k_mla, v_mla, o_cat, do_cat, lse_mla, mla_scale)

    def mla_prep_bwd_fn(rows, params):
        (dqv, dkvv, qv, kvv, cm, sm), (gq, gk) = rows, params
        nope_lanes = _lane(cm.shape) < NOPE
        dqs, dkvs, dgq, dgk = [], [], 0.0, 0.0
        for dqc, dkc, qc, kc in zip(_chunks(dqv), _chunks(dkvv), _chunks(qv), _chunks(kvv), strict=True):
            _, n, rstd = _grms(qc, gq, Q_GROUPS)
            dx, dg = _grms_bwd(_rope_bwd(dqc, cm, sm, H_M), n, rstd, gq, Q_GROUPS)
            dqs.append(dx)
            dgq = dgq + dg
            _, n, rstd = _grms(kc, gk, K_GROUPS)
            dx, dg = _grms_bwd(dkc, n, rstd, gk, K_GROUPS)
            dkvs.append(jnp.where(nope_lanes, dx, dkc))
            dgk = dgk + dg
        return [jnp.concatenate(dqs, axis=1), jnp.concatenate(dkvs, axis=1)], [dgq, dgk]

    dq, dkv, dg_q, dg_k = rowwise("mla_prep_bwd", mla_prep_bwd_fn, [dq_mla, dkv_mla, q, kv, cos_m, sin_m], [g_q, g_k],
                                  [(HEADS * LANES, MXU_DTYPE)] * 2, [LANES, LANES], tm=256)
    emit("w_q_b", matmul("q_wgrad", dq, qln, "tn", out_dtype=MXU_DTYPE))
    emit("w_kv_b", matmul("kv_wgrad", kvn, dkv, "tn", out_dtype=MXU_DTYPE))
    dqln = matmul("q_dgrad", dq, w_q_b, "nn", tm=1024)
    dkvn = matmul("kv_dgrad", dkv, w_kv_b, "nt", tm=1024)

    def pre_bwd_fn(rows, params):
        dql, dkvl, dkp = rows[0:3]
        dqd_, dkd_, dvd_ = [rows[3 + 3 * i] + rows[4 + 3 * i] + rows[5 + 3 * i] for i in range(3)]
        pv, cm, sm, cd, sd = rows[12:]
        gq, gkv, gkp, gdq, gdk = params
        r_q = _norm_bwd(dql, pv[:, P_QLAT:P_KVLAT], gq)
        r_kv = _norm_bwd(dkvl, pv[:, P_KVLAT:P_KPE], gkv)
        _, n, rstd = _grms(pv[:, P_KPE:P_QD], gkp, KPE_GROUPS)
        r_kp = _grms_bwd(_rope_bwd(dkp, cm, sm, H_M), n, rstd, gkp, KPE_GROUPS)
        outs, dgs = [r_q[0], r_kv[0], r_kp[0]], []
        for dval, lo, g in ((dqd_, P_QD, gdq), (dkd_, P_KD, gdk)):
            dg_sum = 0.0
            for dc, xc in zip(_chunks(dval), _chunks(pv[:, lo:lo + DIL_WIDTH]), strict=True):
                _, n, rstd = _grms(xc, g, DIL_GROUPS)
                dx, dg = _grms_bwd(_rope_bwd(dc, cd, sd, H_D), n, rstd, g, DIL_GROUPS)
                outs.append(dx)
                dg_sum = dg_sum + dg
            dgs.append(dg_sum)
        return [jnp.concatenate(outs + [dvd_], axis=1)], [r_q[1], r_kv[1], r_kp[1]] + dgs

    dproj, dg_q_lat, dg_kv_lat, dg_kpe, dg_dq, dg_dk = rowwise(
        "proj_pre_bwd", pre_bwd_fn,
        [dqln, dkvn, dkper] + [d[i] for i in range(3) for d in dband] + [proj] + tables, post_params,
        [(P_END, MXU_DTYPE)], [Q_LORA, KV_LORA, LANES, LANES, LANES], tm=256)
    sent = emit("w_in", matmul("proj_wgrad", dproj, h, "tn", tn=512, out_dtype=MXU_DTYPE))
    dh = matmul("proj_dgrad", dproj, w_in, "nn", tm=512, dep=sent)

    def ln1_bwd_fn(rows, params):
        (dhv, dres, xv), (g, sc) = rows, params
        yn, n, rstd = _rms(xv, g)
        dx_n, dg = _rms_bwd(dhv * (1.0 + sc), n, rstd, g)
        return [dres + dx_n], [dg, _colsum(dhv * yn), _colsum(dhv)]

    grad_x, dg_mix, dsc1, dsh1 = rowwise("ln1_bwd", ln1_bwd_fn, [dh, dx1, x], [w["g_mix_norm"], sc1], [(D_MODEL, F32)],
                                         [D_MODEL] * 3)
    dmod = jnp.concatenate([dsh1, dsc1, dg1, dsh2, dsc2, dg2], axis=-1)
    small = {"loss": loss, "b_ada": dmod, "g_mix_norm": dg_mix, "g_q_lat": dg_q_lat, "g_kv_lat": dg_kv_lat,
             "g_mla_q_nope": dg_q[:, :NOPE], "g_mla_q_pe": dg_q[:, NOPE:NOPE + ROPE], "g_mla_k_nope": dg_k[:, :NOPE],
             "g_mla_k_pe": dg_kpe[:, KPE_LO:KPE_LO + ROPE], "g_dil_q": dg_dq[:, :DIL_DIM] + dg_dq[:, DIL_DIM:],
             "g_dil_k": dg_dk[:, :DIL_DIM] + dg_dk[:, DIL_DIM:], "g_ffn_norm": dg_ffn,
             "b_conv": g_b_conv}
    return grad_x, small


COL_SHARDED = ("w_kv_b", "w_conv")
ROW_SHARDED = ("w_o", "w_down") + TRANSPOSED
ADAM_TILE = {"w_ada": 256, "w_up": 176, "w_down": 176}
GATHER_GROUPS = (("w_in",), ("w_q_b", "w_kv_b"), ("w_o",), ("w_up", "w_conv", "w_down"))
SCATTER_GROUPS = (("w_down", "w_conv", "w_up"), ("w_o",), ("w_q_b", "w_kv_b", "w_in"))
OUT_WEIGHTS = ("w_ada", "b_ada", "g_mix_norm", "w_in", "g_q_lat", "w_q_b", "g_kv_lat", "w_kv_b", "g_mla_q_nope", "g_mla_q_pe",
               "g_mla_k_nope", "g_mla_k_pe", "g_dil_q", "g_dil_k", "w_o", "g_ffn_norm", "w_up", "w_conv", "b_conv", "w_down")


def kernel(x, c, positions, w_ada, b_ada, g_mix_norm, w_in, g_q_lat, w_q_b, g_kv_lat, w_kv_b, g_mla_q_nope, g_mla_q_pe, g_mla_k_nope, g_mla_k_pe, g_dil_q, g_dil_k, w_o, g_ffn_norm, w_up, w_conv, b_conv, w_down, loss_target, m_w_ada, m_b_ada, m_g_mix_norm, m_w_in, m_g_q_lat, m_w_q_b, m_g_kv_lat, m_w_kv_b, m_g_mla_q_nope, m_g_mla_q_pe, m_g_mla_k_nope, m_g_mla_k_pe, m_g_dil_q, m_g_dil_k, m_w_o, m_g_ffn_norm, m_w_up, m_w_conv, m_b_conv, m_w_down, v_w_ada, v_b_ada, v_g_mix_norm, v_w_in, v_g_q_lat, v_w_q_b, v_g_kv_lat, v_w_kv_b, v_g_mla_q_nope, v_g_mla_q_pe, v_g_mla_k_nope, v_g_mla_k_pe, v_g_dil_q, v_g_dil_k, v_w_o, v_g_ffn_norm, v_w_up, v_w_conv, v_b_conv, v_w_down):
    args = dict(locals())
    xi, yi, ci = _place()
    me = 4 * xi + 2 * yi + ci
    def local(prefix, n):
        a = args[prefix + n][0]
        return a.T if n in TRANSPOSED else a

    shard = {n: local("", n) for n in COL_SHARDED + ROW_SHARDED + ("w_ada",)}
    small_w = {n: args[n] for n in SMALL_PARAMS}

    (c_all,) = all_gather("gather_c", [c])
    (sc_all,) = rowwise("silu_c", lambda rows, params: ([_silu(rows[0])], []), [c_all.reshape(N_DEV, D_MODEL)], [],
                        [(D_MODEL, MXU_DTYPE)])
    mod_part = matmul("ada_fwd", sc_all, shard["w_ada"], "nn")
    (mod_all,) = all_gather("gather_mod", [mod_part])

    payload = {n: shard[n] if n == "w_conv" else shard[n].astype(MXU_DTYPE) for n in COL_SHARDED + ROW_SHARDED}
    gather_order = [n for grp in GATHER_GROUPS for n in grp]
    gathered = exchange_start("gather_start", [payload[n] for n in gather_order], gather=True, after=mod_all)
    after_start = gathered[-1]
    full = {}

    def fetch(name, after):
        if name not in full:
            (i, grp), = [(i, grp) for i, grp in enumerate(GATHER_GROUPS) if name in grp]
            srcs, lands = exchange_wait(f"gather{i}_wait", gathered, True, after, gather_order.index(grp[0]), len(grp))
            for n, src, land in zip(grp, srcs, lands, strict=True):
                stack = lax.dynamic_update_index_in_dim(land, src, me, 0)
                full[n] = to_kernel_layout(n, _gather_cols(stack) if n in COL_SHARDED else _gather_rows(stack))
        return full[name]

    mod_row = lax.dynamic_index_in_dim(mod_all, me, axis=1, keepdims=False).reshape(1, 6 * D_MODEL)
    (mod,) = rowwise("ada_bias", lambda rows, params: ([rows[0] + rows[1]], []), [mod_row, b_ada], [], [(6 * D_MODEL, F32)],
                     dep=after_start)

    own, pending, scatters = {}, {}, {}

    def emit(name, grad):
        grad = from_kernel_layout(name, grad)
        parts = _scatter_cols(grad) if name in COL_SHARDED else _scatter_rows(grad)
        own[name] = lax.dynamic_index_in_dim(parts, me, 0, keepdims=False)
        pending[name] = parts
        for i, grp in enumerate(SCATTER_GROUPS):
            if name == grp[-1]:
                scatters[i] = exchange_start(f"scatter{i}_start", [pending[n] for n in grp], gather=False)
                return scatters[i][-1]
        return None

    pos = positions.reshape(SEQ, 1).astype(F32)
    grad_x, small = _local_step(x[0], pos, mod, loss_target[0], small_w, fetch, emit)

    res, done = {}, grad_x
    for i, grp in enumerate(SCATTER_GROUPS):
        _, lands = exchange_wait(f"scatter{i}_wait", scatters[i], False, done)
        for n, land in zip(grp, lands, strict=True):
            res[n] = adamw(f"adamw_{n}", shard[n], [own[n], land], local("m_", n), local("v_", n), ADAM_TILE.get(n))
            done = res[n][0]
            if n in TRANSPOSED:
                res[n] = [r.T for r in res[n]]
    (small_all,) = all_gather("gather_small", [_pack_small(small)], after=done)
    loss, small_res = adamw_small("adamw_small", small_all, {n: (args[n], args["m_" + n], args["v_" + n]) for n in SMALL_PARAMS})
    row, _, n_mod = SMALL_AT["b_ada"]
    dmod_all = small_all[:, row:row + n_mod // SMALL_COLS, :].reshape(N_DEV, n_mod)
    dmod_mine = lax.dynamic_slice_in_dim(dmod_all, me * (6 * D_MODEL // N_DEV), 6 * D_MODEL // N_DEV, axis=1)
    g_w_ada = matmul("ada_wgrad", sc_all, dmod_mine, "tn")
    res["w_ada"] = adamw("adamw_w_ada", shard["w_ada"], [g_w_ada], m_w_ada[0], v_w_ada[0], ADAM_TILE["w_ada"])

    def leaf(kind, n):
        if n in res:
            return res[n][kind][None]
        return small_res[n][kind]

    return (loss.reshape(()), grad_x[None], *[leaf(k, n) for k in range(4) for n in OUT_WEIGHTS])
```

```python
import jax
import jax.numpy as jnp
from jax import lax
from jax.experimental import pallas as pl
from jax.experimental.pallas import tpu as pltpu

F32 = jnp.float32
MXU_DTYPE = jnp.bfloat16

N_DEV = 8
D_MODEL = 1024
SEQ = 2048
HEADS = 8
NOPE = 64
ROPE = 32
Q_LORA = 512
KV_LORA = 256
DIL_DIM = 64
DIL_WIDTH = HEADS * DIL_DIM
DILATIONS = (1, 4, 16)
SPAN = 128
D_FF = 2816
LANES = 128
ROPE_THETA = 10000.0
EPS = 1e-6
NEG_INF = -1e30
ADAM_LR, ADAM_B1, ADAM_B2, ADAM_EPS, ADAM_WD, ADAM_STEP = 0.001, 0.9, 0.999, 1e-08, 0.01, 10
VMEM_LIMIT = 56 * 1024 * 1024
MESH_ID = pl.DeviceIdType.MESH

P_QLAT, P_KVLAT, P_KPE, P_QD, P_KD, P_VD, P_END = 0, 512, 768, 896, 1408, 1920, 2432
KPE_LO = 64
MIX_IN = HEADS * LANES + DIL_WIDTH


def _params(**kw):
    return pltpu.CompilerParams(vmem_limit_bytes=VMEM_LIMIT, **kw)


def rowwise(name, fn, rows, params, out_rows, out_accs=(), tm=512, dep=None):
    deps = [] if dep is None else [dep]
    rows = [r if isinstance(r, tuple) else (r, r.shape[1], 0) for r in rows]
    R = rows[0][0].shape[0]
    tm = min(tm, R)
    steps = R // tm
    assert steps * tm == R
    in_specs = []
    for a, width, cb in rows:
        ri = a.shape[0]
        per = ri // tm
        assert per * tm == ri
        if ri == R:
            in_specs.append(pl.BlockSpec((tm, width), lambda i, cb=cb: (i, cb)))
        else:
            in_specs.append(pl.BlockSpec((tm, width), lambda i, per=per, cb=cb: (i % per, cb)))
    for p in params:
        in_specs.append(pl.BlockSpec(p.shape, lambda i: (0,) * p.ndim))
    in_specs += [pl.BlockSpec(memory_space=pl.ANY)] * len(deps)
    out_shape = [jax.ShapeDtypeStruct((R, d), dt) for d, dt in out_rows]
    out_specs = [pl.BlockSpec((tm, d), lambda i: (i, 0)) for d, _ in out_rows]
    out_shape += [jax.ShapeDtypeStruct((1, n), F32) for n in out_accs]
    out_specs += [pl.BlockSpec((1, n), lambda i: (0, 0)) for n in out_accs]
    nr, npar, no, na = len(rows), len(params), len(out_rows), len(out_accs)

    def body(*refs):
        rvals = [r[...] for r in refs[:nr]]
        pvals = [r[...] for r in refs[nr:nr + npar]]
        outs, accs = fn(rvals, pvals)
        first_out = nr + npar + len(deps)
        for ref, v in zip(refs[first_out:first_out + no], outs, strict=True):
            ref[...] = v.astype(ref.dtype)
        if na:
            acc_refs = refs[first_out + no:]
            i = pl.program_id(0)

            @pl.when(i == 0)
            def _():
                for ref, v in zip(acc_refs, accs, strict=True):
                    ref[...] = v

            @pl.when(i > 0)
            def _():
                for ref, v in zip(acc_refs, accs, strict=True):
                    ref[...] += v

    res = pl.pallas_call(body, name=name, grid=(steps,), in_specs=in_specs, out_specs=out_specs,
                         out_shape=out_shape, compiler_params=_params())(*[r[0] for r in rows], *params, *deps)
    return list(res)


_DIMS = {"nn": ((1,), (0,)), "nt": ((1,), (1,)), "tn": ((0,), (0,))}


def _dot(a, b, mode="nn"):
    return lax.dot_general(a.astype(MXU_DTYPE), b.astype(MXU_DTYPE), (_DIMS[mode], ((), ())),
                           preferred_element_type=F32)


def matmul(name, a, b, mode, tm=None, tn=None, tk=None, out_dtype=F32, dep=None, a_mmap=None, b_nmap=None, b_kmap=None):
    if mode == "tn":
        K, M = a.shape
    else:
        M, K = a.shape
    N = b.shape[0] if mode == "nt" else b.shape[1]
    tm, tn, tk = tm or M, tn or N, tk or K
    nm, nn, nk = M // tm, N // tn, K // tk
    assert nm * tm == M and nn * tn == N and nk * tk == K
    same = lambda idx: idx
    a_mmap, b_nmap, b_kmap = a_mmap or same, b_nmap or same, b_kmap or same
    if mode == "tn":
        a_spec = pl.BlockSpec((tk, tm), lambda i, j, k: (k, a_mmap(i)))
    else:
        a_spec = pl.BlockSpec((tm, tk), lambda i, j, k: (a_mmap(i), k))
    if mode == "nt":
        b_spec = pl.BlockSpec((tn, tk), lambda i, j, k: (b_nmap(j), b_kmap(k)))
    else:
        b_spec = pl.BlockSpec((tk, tn), lambda i, j, k: (b_kmap(k), b_nmap(j)))
    deps = [] if dep is None else [dep]

    def body(a_ref, b_ref, *rest):
        o_ref, scratch = rest[len(deps)], rest[len(deps) + 1:]
        p = _dot(a_ref[...], b_ref[...], mode)
        if nk == 1:
            o_ref[...] = p.astype(o_ref.dtype)
        else:
            acc = scratch[0]
            k = pl.program_id(2)

            @pl.when(k == 0)
            def _():
                acc[...] = p

            @pl.when(k > 0)
            def _():
                acc[...] += p

            @pl.when(k == nk - 1)
            def _():
                o_ref[...] = acc[...].astype(o_ref.dtype)

    return pl.pallas_call(
        body, name=name, grid=(nm, nn, nk), in_specs=[a_spec, b_spec] + [pl.BlockSpec(memory_space=pl.ANY)] * len(deps),
        out_specs=pl.BlockSpec((tm, tn), lambda i, j, k: (i, j)),
        out_shape=jax.ShapeDtypeStruct((M, N), out_dtype),
        scratch_shapes=[pltpu.VMEM((tm, tn), F32)] if nk > 1 else [],
        compiler_params=_params())(a, b, *deps)


def _rms(x, g):
    rstd = lax.rsqrt(jnp.mean(x * x, axis=-1, keepdims=True) + EPS)
    n = x * rstd
    return n * g, n, rstd


def _rms_bwd(dy, n, rstd, g):
    dg = jnp.sum(dy * n, axis=0, keepdims=True)
    dn = dy * g
    dx = rstd * (dn - n * jnp.mean(dn * n, axis=-1, keepdims=True))
    return dx, dg


def _norm_bwd(dy, x, g):
    _, n, rstd = _rms(x, g)
    return _rms_bwd(dy, n, rstd, g)


def _colsum(v):
    return jnp.sum(v, axis=0, keepdims=True)


def _silu(x):
    return x * (1.0 / (1.0 + jnp.exp(-x)))


def _lane(shape):
    return lax.broadcasted_iota(jnp.int32, shape, 1)


def _group_mean(v, groups):
    i = lax.broadcasted_iota(jnp.int32, (LANES, LANES), 0)
    j = lax.broadcasted_iota(jnp.int32, (LANES, LANES), 1)
    g = jnp.zeros((LANES, LANES), F32)
    for lo, hi in groups:
        g = jnp.where((i >= lo) & (i < hi) & (j >= lo) & (j < hi), 1.0 / (hi - lo), g)
    head = v.astype(MXU_DTYPE)
    return _dot(head, g) + _dot(v - head.astype(F32), g)


def _in_groups(shape, groups):
    lane = _lane(shape)
    m = jnp.zeros(shape, jnp.bool_)
    for lo, hi in groups:
        m = m | ((lane >= lo) & (lane < hi))
    return m


def _grms(x, g, groups):
    rstd = lax.rsqrt(_group_mean(x * x, groups) + EPS)
    n = jnp.where(_in_groups(x.shape, groups), x * rstd, 0.0)
    return n * g, n, rstd


def _grms_bwd(dy, n, rstd, g, groups):
    dn = dy * g
    return rstd * (dn - n * _group_mean(dn * n, groups)), _colsum(dy * n)


def _rot(x, half, transpose=False):
    first = (_lane(x.shape) % (2 * half)) < half
    up = pltpu.roll(x, LANES - half, axis=1)
    down = pltpu.roll(x, half, axis=1)
    return jnp.where(first, up, -down) if transpose else jnp.where(first, -up, down)


def _rope(x, cos, sin, half):
    return x * cos + _rot(x, half) * sin


def _rope_bwd(dy, cos, sin, half):
    return dy * cos + _rot(dy * sin, half, transpose=True)


def _chunks(x):
    return [x[:, i:i + LANES] for i in range(0, x.shape[1], LANES)]


Q_GROUPS = ((0, NOPE), (NOPE, NOPE + ROPE))
K_GROUPS = ((0, NOPE),)
KPE_GROUPS = ((KPE_LO, KPE_LO + ROPE),)
DIL_GROUPS = ((0, DIL_DIM), (DIL_DIM, 2 * DIL_DIM))


def _col(width, rows=SEQ):
    return pl.BlockSpec((rows, width), lambda h: (0, h))


def _causal_tail(s, tq, fill):
    diag = s[:, s.shape[1] - tq:]
    keep = lax.broadcasted_iota(jnp.int32, diag.shape, 1) <= lax.broadcasted_iota(jnp.int32, diag.shape, 0)
    diag = jnp.where(keep, diag, fill)
    return diag if s.shape[1] == tq else jnp.concatenate([s[:, :s.shape[1] - tq], diag], axis=1)


def mla_fwd(name, q, k, v, scale, tq=256):
    S = q.shape[0]

    def body(q_ref, k_ref, v_ref, o_ref, lse_ref):
        nb = S // tq
        blk = lambda i: slice(i * tq, (i + 1) * tq)

        def scores(i):
            return _dot(q_ref[blk(i), :], k_ref[:(i + 1) * tq, :], "nt")

        def softmax(i, s):
            s = _causal_tail(s * scale, tq, NEG_INF)
            m = jnp.max(s, axis=-1, keepdims=True)
            e = jnp.exp(s - m)
            l = jnp.sum(e, axis=-1, keepdims=True)
            lse_ref[0, blk(i), :] = m + jnp.log(l)
            return (e * (1.0 / l)).astype(MXU_DTYPE)

        def weighted(i, p):
            o_ref[blk(i), :] = _dot(p, v_ref[:(i + 1) * tq, :])

        s, p_prev = scores(0), None
        for i in range(nb):
            s_next = scores(i + 1) if i + 1 < nb else None
            if p_prev is not None:
                weighted(i - 1, p_prev)
            p_prev, s = softmax(i, s), s_next
        weighted(nb - 1, p_prev)

    return pl.pallas_call(
        body, name=name, grid=(HEADS,), in_specs=[_col(LANES)] * 3,
        out_specs=[_col(LANES), pl.BlockSpec((1, S, 1), lambda h: (h, 0, 0))],
        out_shape=[jax.ShapeDtypeStruct((S, MIX_IN), F32), jax.ShapeDtypeStruct((HEADS, S, 1), F32)],
        compiler_params=_params())(q, k, v)


def mla_bwd(name, q, k, v, o, do, lse, scale, tq=256):
    S = q.shape[0]

    def body(q_ref, k_ref, v_ref, o_ref, do_ref, lse_ref, dq_ref, dkv_ref, dkpe_ref, dk_acc, dv_acc):
        dk_acc[...] = jnp.zeros_like(dk_acc)
        dv_acc[...] = jnp.zeros_like(dv_acc)
        for i in range(S // tq):
            kext = (i + 1) * tq
            blk = slice(i * tq, kext)
            qi, kk, vv = q_ref[blk, :], k_ref[:kext, :], v_ref[:kext, :]
            doi = do_ref[blk, :]
            s = _causal_tail(_dot(qi, kk, "nt") * scale, tq, NEG_INF)
            p = jnp.exp(s - lse_ref[0, blk, :])
            dp = _dot(doi, vv, "nt")
            delta = jnp.sum(doi * o_ref[blk, :], axis=-1, keepdims=True)
            ds = p * (dp - delta) * scale
            dq_ref[blk, :] = _dot(ds, kk)
            dk_acc[:kext, :] += _dot(ds, qi, "tn")
            dv_acc[:kext, :] += _dot(p, doi, "tn")
        dk = dk_acc[...]
        lane = _lane(dk.shape)
        dkv_ref[...] = jnp.where(lane < NOPE, dk, 0.0) + dv_acc[...]
        dkpe = jnp.where((lane >= KPE_LO) & (lane < KPE_LO + ROPE), dk, 0.0)
        h = pl.program_id(0)

        @pl.when(h == 0)
        def _():
            dkpe_ref[...] = dkpe

        @pl.when(h > 0)
        def _():
            dkpe_ref[...] += dkpe

    return pl.pallas_call(
        body, name=name, grid=(HEADS,),
        in_specs=[_col(LANES)] * 5 + [pl.BlockSpec((1, S, 1), lambda h: (h, 0, 0))],
        out_specs=[_col(LANES), _col(LANES), pl.BlockSpec((S, LANES), lambda h: (0, 0))],
        out_shape=[jax.ShapeDtypeStruct((S, HEADS * LANES), F32), jax.ShapeDtypeStruct((S, HEADS * LANES), F32),
                   jax.ShapeDtypeStruct((S, LANES), F32)],
        scratch_shapes=[pltpu.VMEM((S, LANES), F32), pltpu.VMEM((S, LANES), F32)],
        compiler_params=_params())(q, k, v, o, do, lse)


BAND_TQ = SPAN


def _band_blocks(L, tq):
    return [(i * tq, (i + 1) * tq, max(0, i * tq - SPAN)) for i in range(L // tq)]


def _band_mask(q0, q1, k0):
    shape = (q1 - q0, q1 - k0)
    dist = (lax.broadcasted_iota(jnp.int32, shape, 0) + q0) - (lax.broadcasted_iota(jnp.int32, shape, 1) + k0)
    return (dist >= 0) & (dist <= SPAN)


def _class_rows(r, dil, lo, hi):
    return pl.ds(r + dil * lo, hi - lo, stride=dil) if dil > 1 else pl.ds(lo, hi - lo)


def _stack_heads(t, lo):
    zero = jnp.zeros_like(t)
    return jnp.concatenate([jnp.where(lo, t, zero), jnp.where(lo, zero, t)], axis=0)


def _band_mask2(q0, q1, k0):
    n = q1 - q0
    shape = (2 * n, q1 - k0)
    i = lax.broadcasted_iota(jnp.int32, shape, 0)
    dist = (jnp.where(i >= n, i - n, i) + q0) - (lax.broadcasted_iota(jnp.int32, shape, 1) + k0)
    return (dist >= 0) & (dist <= SPAN)


def _pair_col(col0=0):
    return pl.BlockSpec((SEQ, LANES), lambda j: (0, col0 // LANES + j))


def band_fwd(name, q, k, v, dil):
    S = q.shape[0]
    L = S // dil
    tq = BAND_TQ
    scale = DIL_DIM ** -0.5

    def body(q_ref, k_ref, v_ref, o_ref, lse_ref):
        items = [(r, blk) for r in range(dil) for blk in _band_blocks(L, tq)]
        lo = _lane((tq, LANES)) < DIL_DIM

        def scores(item):
            r, (q0, q1, k0) = item
            qb = q_ref[_class_rows(r, dil, q0, q1), :].astype(MXU_DTYPE)
            return _dot(_stack_heads(qb, lo), k_ref[_class_rows(r, dil, k0, q1), :], "nt")

        def softmax(item, s):
            _, (q0, q1, k0) = item
            s = jnp.where(_band_mask2(q0, q1, k0), s * scale, NEG_INF)
            mx = jnp.max(s, axis=-1, keepdims=True)
            e = jnp.exp(s - mx)
            l = jnp.sum(e, axis=-1, keepdims=True)
            return (e * (1.0 / l)).astype(MXU_DTYPE), mx + jnp.log(l)

        def weighted(item, p, lse):
            r, (q0, q1, k0) = item
            pv = _dot(p, v_ref[_class_rows(r, dil, k0, q1), :])
            o_ref[_class_rows(r, dil, q0, q1), :] = jnp.where(lo, pv[:tq], pv[tq:])
            lse_ref[_class_rows(r, dil, q0, q1), :] = jnp.where(lo, lse[:tq], lse[tq:])

        s, prev = scores(items[0]), None
        for i, item in enumerate(items):
            s_next = scores(items[i + 1]) if i + 1 < len(items) else None
            if prev is not None:
                weighted(items[i - 1], *prev)
            prev, s = softmax(item, s), s_next
        weighted(items[-1], *prev)

    return pl.pallas_call(
        body, name=name, grid=(DIL_WIDTH // LANES,), in_specs=[_pair_col()] * 2 + [_pair_col(P_VD)], out_specs=[_pair_col()] * 2,
        out_shape=[jax.ShapeDtypeStruct((S, DIL_WIDTH), F32)] * 2, compiler_params=_params())(q, k, v)


def band_bwd(name, q, k, v, lse, lse_mix, o_cat, do_cat, dil):
    S = q.shape[0]
    L = S // dil
    tq = BAND_TQ
    scale = DIL_DIM ** -0.5

    def body(q_ref, k_ref, v_ref, lse_ref, mix_ref, o_ref, do_ref, dq_ref, dk_ref, dv_ref):
        dk_ref[...] = jnp.zeros_like(dk_ref)
        dv_ref[...] = jnp.zeros_like(dv_ref)
        items = [(r, blk) for r in range(dil) for blk in _band_blocks(L, tq)]
        lo = _lane((tq, LANES)) < DIL_DIM
        per_head = lambda t: jnp.concatenate([t[:, 0:1], t[:, DIL_DIM:DIL_DIM + 1]], axis=0)

        def scores(item):
            r, (q0, q1, k0) = item
            qrows, krows = _class_rows(r, dil, q0, q1), _class_rows(r, dil, k0, q1)
            lse_p, dout = lse_ref[qrows, :], do_ref[qrows, :]
            w2 = per_head(jnp.exp(lse_p - mix_ref[qrows, :]))
            dd = dout * o_ref[qrows, :]
            big_d = jnp.concatenate([jnp.sum(jnp.where(lo, dd, 0.0), axis=-1, keepdims=True),
                                     jnp.sum(jnp.where(lo, 0.0, dd), axis=-1, keepdims=True)], axis=0)
            q2 = _stack_heads(q_ref[qrows, :].astype(MXU_DTYPE), lo)
            dom = (_stack_heads(dout, lo) * w2).astype(MXU_DTYPE)
            return (_dot(q2, k_ref[krows, :], "nt"), _dot(dom, v_ref[krows, :], "nt"), per_head(lse_p), w2 * big_d, q2, dom)

        def softmax_bwd(item, s, dp, lse2, wd2, q2, dom):
            _, (q0, q1, k0) = item
            p = jnp.where(_band_mask2(q0, q1, k0), jnp.exp(s * scale - lse2), 0.0)
            return p.astype(MXU_DTYPE), (p * (dp - wd2) * scale).astype(MXU_DTYPE), q2, dom

        def grads(item, p, ds, q2, dom):
            r, (q0, q1, k0) = item
            qrows, krows = _class_rows(r, dil, q0, q1), _class_rows(r, dil, k0, q1)
            dq2 = _dot(ds, k_ref[krows, :])
            dq_ref[qrows, :] = jnp.where(lo, dq2[:tq], dq2[tq:])
            dk_ref[krows, :] += _dot(ds, q2, "tn")
            dv_ref[krows, :] += _dot(p, dom, "tn")

        sc, prev = scores(items[0]), None
        for i, item in enumerate(items):
            sc_next = scores(items[i + 1]) if i + 1 < len(items) else None
            if prev is not None:
                grads(items[i - 1], *prev)
            prev, sc = softmax_bwd(item, *sc), sc_next
        grads(items[-1], *prev)

    cat = _pair_col(HEADS * LANES)
    return pl.pallas_call(
        body, name=name, grid=(DIL_WIDTH // LANES,),
        in_specs=[_pair_col()] * 2 + [_pair_col(P_VD)] + [_pair_col()] * 2 + [cat] * 2, out_specs=[_pair_col()] * 3,
        out_shape=[jax.ShapeDtypeStruct((S, DIL_WIDTH), F32)] * 3,
        compiler_params=_params())(q, k, v, lse, lse_mix, o_cat, do_cat)


def combine_fwd(name, outs, lses, o_cat, tm=512):
    S = outs[0].shape[0]

    def body(o1, o2, o3, l1, l2, l3, cat_in, cat_out, mix_ref):
        ls = [l1[...], l2[...], l3[...]]
        m = jnp.maximum(jnp.maximum(ls[0], ls[1]), ls[2])
        e = [jnp.exp(l - m) for l in ls]
        den = e[0] + e[1] + e[2]
        cat_out[...] = (e[0] / den) * o1[...] + (e[1] / den) * o2[...] + (e[2] / den) * o3[...]
        mix_ref[...] = m + jnp.log(den)

    row = pl.BlockSpec((tm, DIL_WIDTH), lambda i: (i, 0))
    return pl.pallas_call(
        body, name=name, grid=(S // tm,), in_specs=[row] * 6 + [pl.BlockSpec(memory_space=pl.ANY)],
        out_specs=[pl.BlockSpec((tm, DIL_WIDTH), lambda i: (i, HEADS * LANES // DIL_WIDTH)), row],
        out_shape=[jax.ShapeDtypeStruct(o_cat.shape, F32), jax.ShapeDtypeStruct((S, DIL_WIDTH), F32)],
        input_output_aliases={6: 0}, compiler_params=_params())(*outs, *lses, o_cat)


def _shift_down(u, n, zero_head):
    out = pltpu.roll(u, n, axis=0)
    return jnp.where(lax.broadcasted_iota(jnp.int32, u.shape, 0) >= n, out, 0.0) if zero_head else out


def _shift_up(u, n, zero_tail):
    rows = u.shape[0]
    out = pltpu.roll(u, rows - n, axis=0)
    return jnp.where(lax.broadcasted_iota(jnp.int32, u.shape, 0) < rows - n, out, 0.0) if zero_tail else out


CONV_ROWS = 512
CONV_HALO = 16


def _conv_chunks(S, tail):
    out = []
    for r0 in range(0, S, CONV_ROWS):
        lo, hi = max(0, r0 - CONV_HALO), min(S, r0 + CONV_ROWS + (CONV_HALO if tail else 0))
        out.append((lo, hi, r0 - lo, CONV_ROWS))
    return out


CONV_TC = 256
CONV_NB = D_FF // CONV_TC


def _half_specs(rows, rows_axis=False):
    if rows_axis:
        return [pl.BlockSpec((rows, D_MODEL), lambda j: (j, 0)), pl.BlockSpec((rows, D_MODEL), lambda j: (j + CONV_NB, 0))]
    return [pl.BlockSpec((rows, CONV_TC), lambda j: (0, j)), pl.BlockSpec((rows, CONV_TC), lambda j: (0, j + CONV_NB))]


def _whole(a):
    return pl.BlockSpec(a.shape, lambda j: (0,) * a.ndim)


def _up_pair(h, ug_ref, uv_ref):
    return jnp.concatenate([_dot(h, ug_ref[...], "nt"), _dot(h, uv_ref[...], "nt")], axis=1)


def _conv_taps(uin, w, b, starts):
    u1, u2 = _shift_down(uin, 1, starts), _shift_down(uin, 2, starts)
    return u1, u2, w[2:3, :] * uin + w[1:2, :] * u1 + w[0:1, :] * u2 + b


def ffn_fwd(name, h, w_up_t, w_conv, b_conv, w_down):
    S = h.shape[0]

    def body(h_ref, ug_ref, uv_ref, wg_ref, wv_ref, bg_ref, bv_ref, wd_ref, dn_ref, up_ref):
        @pl.when(pl.program_id(0) == 0)
        def _():
            dn_ref[...] = jnp.zeros_like(dn_ref)

        w = jnp.concatenate([wg_ref[...], wv_ref[...]], axis=1)
        b = jnp.concatenate([bg_ref[...], bv_ref[...]], axis=1)
        chunks = _conv_chunks(S, tail=False)

        def project(c):
            lo, hi, keep, rows = c
            uin = _up_pair(h_ref[lo:hi, :], ug_ref, uv_ref)
            up_ref[lo + keep:lo + keep + rows, :] = uin[keep:keep + rows]
            return uin

        def gate(c, uin):
            lo, hi, keep, rows = c
            u = _conv_taps(uin, w, b, lo == 0)[2][keep:keep + rows]
            return (_silu(u[:, :CONV_TC]) * u[:, CONV_TC:]).astype(MXU_DTYPE)

        def project_down(c, act):
            dn_ref[c[0] + c[2]:c[0] + c[2] + c[3], :] += _dot(act, wd_ref[...])

        uin, act_prev = project(chunks[0]), None
        for i, c in enumerate(chunks):
            uin_next = project(chunks[i + 1]) if i + 1 < len(chunks) else None
            if act_prev is not None:
                project_down(chunks[i - 1], act_prev)
            act_prev = gate(c, uin)
            uin = uin_next
        project_down(chunks[-1], act_prev)

    return pl.pallas_call(
        body, name=name, grid=(CONV_NB,),
        in_specs=[_whole(h)] + _half_specs(CONV_TC, rows_axis=True) + _half_specs(3) + _half_specs(1)
        + [pl.BlockSpec((CONV_TC, w_down.shape[1]), lambda j: (j, 0))],
        out_specs=[pl.BlockSpec((S, w_down.shape[1]), lambda j: (0, 0)), pl.BlockSpec((S, 2 * CONV_TC), lambda j: (0, j))],
        out_shape=[jax.ShapeDtypeStruct((S, w_down.shape[1]), F32), jax.ShapeDtypeStruct((S, 2 * D_FF), F32)],
        compiler_params=_params())(h, w_up_t, w_up_t, w_conv, w_conv, b_conv, b_conv, w_down)


def ffn_bwd(name, h, up, w_up_t, w_conv, b_conv, d_dn, w_down):
    S, D = h.shape

    def body(h_ref, up_ref, ug_ref, uv_ref, wg_ref, wv_ref, bg_ref, bv_ref, dd_ref, wd_ref,
             dh_ref, gup_ref, gd_ref, dwg_ref, dwv_ref, dbg_ref, dbv_ref):
        @pl.when(pl.program_id(0) == 0)
        def _():
            dh_ref[...] = jnp.zeros_like(dh_ref)

        w = jnp.concatenate([wg_ref[...], wv_ref[...]], axis=1)
        b = jnp.concatenate([bg_ref[...], bv_ref[...]], axis=1)
        w_pair = jnp.concatenate([ug_ref[...], uv_ref[...]], axis=0)
        chunks = _conv_chunks(S, tail=True)

        def project(c):
            return up_ref[c[0]:c[1], :], _dot(dd_ref[c[0]:c[1], :], wd_ref[...], "nt")

        def through_conv(c, uin, da):
            lo, hi, keep, rows = c
            u1, u2, u = _conv_taps(uin, w, b, lo == 0)
            gate, val = u[:, :CONV_TC], u[:, CONV_TC:]
            sig = 1.0 / (1.0 + jnp.exp(-gate))
            du = jnp.concatenate([da * val * (sig * (1.0 + gate * (1.0 - sig))), da * (gate * sig)], axis=1)
            dup = w[2:3, :] * du + w[1:2, :] * _shift_up(du, 1, hi == S) + w[0:1, :] * _shift_up(du, 2, hi == S)
            kept = slice(keep, keep + rows)
            du = du[kept]
            dw = jnp.concatenate([_colsum(du * u2[kept]), _colsum(du * u1[kept]), _colsum(du * uin[kept])], axis=0)
            return dup[kept].astype(MXU_DTYPE), (gate * sig * val)[kept].astype(MXU_DTYPE), dw, _colsum(du)

        def weight_grads(c, dup, act):
            out_rows = slice(c[0] + c[2], c[0] + c[2] + c[3])
            dh_ref[out_rows, :] += _dot(dup, w_pair)
            return _dot(dup, h_ref[out_rows, :], "tn"), _dot(act, dd_ref[out_rows, :], "tn")

        dw, db, g_up, g_dn = 0.0, 0.0, 0.0, 0.0
        proj, done = project(chunks[0]), None
        for i, c in enumerate(chunks):
            proj_next = project(chunks[i + 1]) if i + 1 < len(chunks) else None
            if done is not None:
                gu, gd = weight_grads(chunks[i - 1], *done)
                g_up, g_dn = g_up + gu, g_dn + gd
            dup, act, dw_c, db_c = through_conv(c, *proj)
            dw, db, done, proj = dw + dw_c, db + db_c, (dup, act), proj_next
        gu, gd = weight_grads(chunks[-1], *done)
        g_up, g_dn = g_up + gu, g_dn + gd
        gup_ref[0], gup_ref[1] = g_up[:CONV_TC].astype(gup_ref.dtype), g_up[CONV_TC:].astype(gup_ref.dtype)
        gd_ref[...] = g_dn.astype(gd_ref.dtype)
        dwg_ref[...], dwv_ref[...] = dw[:, :CONV_TC], dw[:, CONV_TC:]
        dbg_ref[...], dbv_ref[...] = db[:, :CONV_TC], db[:, CONV_TC:]

    half = lambda rows: pl.BlockSpec((rows, CONV_TC), lambda j: (0, j))
    rows_blk = pl.BlockSpec((CONV_TC, D), lambda j: (j, 0))
    dh, gup, gd, dwg, dwv, dbg, dbv = pl.pallas_call(
        body, name=name, grid=(CONV_NB,),
        in_specs=[_whole(h), pl.BlockSpec((S, 2 * CONV_TC), lambda j: (0, j))] + _half_specs(CONV_TC, rows_axis=True) + _half_specs(3)
        + _half_specs(1) + [_whole(d_dn), rows_blk],
        out_specs=[pl.BlockSpec((S, D), lambda j: (0, 0)), pl.BlockSpec((2, CONV_TC, D), lambda j: (0, j, 0)), rows_blk,
                   half(3), half(3), half(1), half(1)],
        out_shape=[jax.ShapeDtypeStruct((S, D), F32), jax.ShapeDtypeStruct((2, D_FF, D), MXU_DTYPE),
                   jax.ShapeDtypeStruct((D_FF, D), MXU_DTYPE)]
        + [jax.ShapeDtypeStruct((3, D_FF), F32)] * 2 + [jax.ShapeDtypeStruct((1, D_FF), F32)] * 2,
        compiler_params=_params())(h, up, w_up_t, w_up_t, w_conv, w_conv, b_conv, b_conv, d_dn, w_down)
    return dh, gup.reshape(2 * D_FF, D), gd, jnp.concatenate([dwg, dwv], axis=1), jnp.concatenate([dbg, dbv], axis=1)


def adamw(name, w, parts, m, v, tr=None):
    R, C = w.shape
    tr = tr or R
    assert R % tr == 0
    c1 = 1.0 - ADAM_B1 ** ADAM_STEP
    c2 = 1.0 - ADAM_B2 ** ADAM_STEP
    np_ = len(parts)

    def body(*refs):
        w_ref, m_ref, v_ref = refs[0], refs[1 + np_], refs[2 + np_]
        go_ref, d_ref, mo_ref, vo_ref = refs[3 + np_:]
        terms = []
        for part, ref in zip(parts, refs[1:1 + np_], strict=True):
            terms += [ref[...]] if part.ndim == 2 else [ref[p] for p in range(part.shape[0])]
        g = terms[0].astype(F32)
        for term in terms[1:]:
            g = g + term.astype(F32)
        m2 = ADAM_B1 * m_ref[...] + (1.0 - ADAM_B1) * g
        v2 = ADAM_B2 * v_ref[...] + (1.0 - ADAM_B2) * (g * g)
        go_ref[...] = g
        mo_ref[...] = m2
        vo_ref[...] = v2
        d_ref[...] = -ADAM_LR * ((m2 / c1) / (jnp.sqrt(v2 / c2) + ADAM_EPS) + ADAM_WD * w_ref[...])

    blk = pl.BlockSpec((tr, C), lambda i: (i, 0))
    part_specs = [blk if p.ndim == 2 else pl.BlockSpec((p.shape[0], tr, C), lambda i: (0, i, 0)) for p in parts]
    return pl.pallas_call(
        body, name=name, grid=(R // tr,),
        in_specs=[blk] + part_specs + [blk, blk], out_specs=[blk] * 4,
        out_shape=[jax.ShapeDtypeStruct((R, C), F32)] * 4, compiler_params=_params())(w, *parts, m, v)


def _place():
    return lax.axis_index("x"), lax.axis_index("y"), lax.axis_index("c")


def all_gather(name, arrs, after=None):
    n = len(arrs)
    deps = [] if after is None else [after]

    def body(*refs):
        ins, outs = refs[:n], refs[n + len(deps):2 * n + len(deps)]
        send_sems, recv_sems, local_sems = refs[2 * n + len(deps):]
        x, y, c = _place()
        me, sibling = (x, y, c), (x, y, 1 - c)
        chips = [(1 - x, y), (x, 1 - y), (1 - x, 1 - y)]
        sends = []
        for t in range(n):
            out = outs[t]

            def slot(px, py, pc, out=out):
                return out.at[4 * px + 2 * py + pc]

            def copy(k, block, to, src=None, t=t, slot=slot):
                return pltpu.make_async_remote_copy(
                    src_ref=slot(*block) if src is None else src, dst_ref=slot(*block),
                    send_sem=send_sems.at[7 * t + k], recv_sem=recv_sems.at[7 * t + k],
                    device_id=to, device_id_type=MESH_ID)

            mine = pltpu.make_async_copy(ins[t], slot(*me), local_sems.at[t])
            mine.start()
            first = [copy(0, me, sibling, src=ins[t])]
            first += [copy(1 + j, me, (*chip, c), src=ins[t]) for j, chip in enumerate(chips)]
            for cp in first:
                cp.start()
            sends.append((mine, first, copy))
        for t in range(n):
            mine, first, copy = sends[t]
            passed = [copy(4 + j, (*chip, c), sibling) for j, chip in enumerate(chips)]
            for j, chip in enumerate(chips):
                copy(1 + j, (*chip, c), me).wait_recv()
                passed[j].start()
            copy(0, sibling, me).wait_recv()
            for j, chip in enumerate(chips):
                copy(4 + j, (*chip, 1 - c), me).wait_recv()
            for cp in first + passed:
                cp.wait_send()
            mine.wait()

    any_spec = pl.BlockSpec(memory_space=pl.ANY)
    res = pl.pallas_call(
        body, name=name, in_specs=[any_spec] * (n + len(deps)), out_specs=[any_spec] * n,
        out_shape=[jax.ShapeDtypeStruct((N_DEV,) + a.shape, a.dtype) for a in arrs],
        scratch_shapes=[pltpu.SemaphoreType.DMA((7 * n,)), pltpu.SemaphoreType.DMA((7 * n,)), pltpu.SemaphoreType.DMA((n,))],
        compiler_params=pltpu.CompilerParams(has_side_effects=True))(*arrs, *deps)
    return list(res)


HBM_SPEC = pl.BlockSpec(memory_space=pltpu.HBM)
SEM_SPEC = pl.BlockSpec(memory_space=pltpu.SEMAPHORE)
DATAFLOW = pltpu.SideEffectType.DATAFLOW_SIDE_EFFECTING


def _exchange_copies(srcs, lands, send_sems, recv_sems, gather, first=0):
    x, y, c = _place()
    me = 4 * x + 2 * y + c
    out = []
    for t, (src, land) in enumerate(zip(srcs, lands, strict=True)):
        for k in range(1, N_DEV):
            px, py, pc = x ^ (k >> 2), y ^ ((k >> 1) & 1), c ^ (k & 1)
            sem = 7 * (first + t) + k - 1
            out.append(pltpu.make_async_remote_copy(
                src_ref=src if gather else src.at[4 * px + 2 * py + pc],
                dst_ref=land.at[me] if gather else land.at[k - 1],
                send_sem=send_sems.at[sem], recv_sem=recv_sems.at[sem],
                device_id=(px, py, pc), device_id_type=MESH_ID))
    return out


def exchange_start(name, arrs, gather, after=None):
    n = len(arrs)
    lands = [lax.empty(((N_DEV,) + a.shape) if gather else ((N_DEV - 1,) + a.shape[1:]), a.dtype) for a in arrs]
    deps = [] if after is None else [after]

    def body(*refs):
        srcs, land_refs = refs[:n], refs[n:2 * n]
        send_sems, recv_sems = refs[2 * n + len(deps)], refs[2 * n + len(deps) + 1]
        token = refs[-1]
        for cp in _exchange_copies(srcs, land_refs, send_sems, recv_sems, gather):
            cp.start()
        token[...] = jnp.zeros_like(token)

    hbm = lambda a: pltpu.HBM(a.shape, a.dtype)
    res = pl.pallas_call(
        body, name=name,
        out_shape=(pltpu.SemaphoreType.DMA((7 * n,)), pltpu.SemaphoreType.DMA((7 * n,)), *[hbm(a) for a in arrs],
                   *[hbm(l) for l in lands], jax.ShapeDtypeStruct((8, 128), F32)),
        in_specs=[HBM_SPEC] * (2 * n) + [pl.BlockSpec(memory_space=pl.ANY)] * len(deps),
        out_specs=(SEM_SPEC, SEM_SPEC, *[HBM_SPEC] * (2 * n), pl.BlockSpec(memory_space=pltpu.VMEM)),
        input_output_aliases={i: 2 + i for i in range(2 * n)},
        compiler_params=pltpu.CompilerParams(has_side_effects=DATAFLOW),
    )(*[pltpu.with_memory_space_constraint(a, pltpu.HBM) for a in arrs + lands], *deps)
    return res[0], res[1], list(res[2:2 + n]), list(res[2 + n:2 + 2 * n]), res[-1]


def exchange_wait(name, started, gather, after, first=0, count=None):
    send_sems, recv_sems, srcs, lands, _ = started
    count = len(srcs) - first if count is None else count
    srcs, lands = srcs[first:first + count], lands[first:first + count]
    n = len(srcs)

    def body(*refs):
        src_refs, land_refs = refs[:n], refs[n:2 * n]
        copies = _exchange_copies(src_refs, land_refs, refs[2 * n], refs[2 * n + 1], gather, first)
        for cp in copies:
            cp.wait_send()
        for cp in copies:
            cp.wait_recv()

    hbm = lambda a: pltpu.HBM(a.shape, a.dtype)
    res = pl.pallas_call(
        body, name=name, out_shape=tuple(hbm(a) for a in srcs + lands),
        in_specs=[HBM_SPEC] * (2 * n) + [SEM_SPEC, SEM_SPEC, pl.BlockSpec(memory_space=pl.ANY)],
        out_specs=tuple([HBM_SPEC] * (2 * n)), input_output_aliases={i: i for i in range(2 * n)},
        compiler_params=pltpu.CompilerParams(has_side_effects=DATAFLOW),
    )(*srcs, *lands, send_sems, recv_sems, after)
    return list(res[:n]), list(res[n:])


def _gather_cols(stack):
    p, k, n = stack.shape
    return stack.transpose(1, 0, 2).reshape(k, p * n)


def _scatter_cols(full):
    k, n = full.shape
    return full.reshape(k, N_DEV, n // N_DEV).transpose(1, 0, 2)


def _gather_rows(stack):
    p, r, n = stack.shape
    return stack.reshape(p * r, n)


def _scatter_rows(full):
    r, n = full.shape
    return full.reshape(N_DEV, r // N_DEV, n)


_IN_NAT = Q_LORA + KV_LORA
TRANSPOSED = ("w_in", "w_q_b", "w_up")


def to_kernel_layout(name, w):
    if name == "w_in":
        z = lambda n: jnp.zeros((n, w.shape[1]), w.dtype)
        return jnp.concatenate([w[:_IN_NAT], z(KPE_LO), w[_IN_NAT:_IN_NAT + ROPE], z(LANES - KPE_LO - ROPE), w[_IN_NAT + ROPE:]], axis=0)
    if name == "w_q_b":
        return jnp.pad(w.reshape(HEADS, NOPE + ROPE, -1), ((0, 0), (0, LANES - NOPE - ROPE), (0, 0))).reshape(HEADS * LANES, -1)
    if name == "w_o":
        mla = jnp.pad(w[:HEADS * NOPE].reshape(HEADS, NOPE, -1), ((0, 0), (LANES - NOPE, 0), (0, 0))).reshape(HEADS * LANES, -1)
        return jnp.concatenate([mla, w[HEADS * NOPE:]], axis=0)
    return w


def from_kernel_layout(name, g):
    if name == "w_in":
        return jnp.concatenate([g[:_IN_NAT], g[P_KPE + KPE_LO:P_KPE + KPE_LO + ROPE], g[P_QD:]], axis=0)
    if name == "w_q_b":
        return g.reshape(HEADS, LANES, -1)[:, :NOPE + ROPE, :].reshape(HEADS * (NOPE + ROPE), -1)
    if name == "w_o":
        mla = g[:HEADS * LANES].reshape(HEADS, LANES, -1)[:, LANES - NOPE:, :].reshape(HEADS * NOPE, -1)
        return jnp.concatenate([mla, g[HEADS * LANES:]], axis=0)
    return g


SMALL_COLS = 1024
SMALL_ROWS = 24
SMALL_AT = {"loss": (0, 0, 1), "b_ada": (1, 0, 6 * D_MODEL), "g_mix_norm": (7, 0, D_MODEL), "g_q_lat": (8, 0, Q_LORA),
            "g_kv_lat": (9, 0, KV_LORA), "g_mla_q_nope": (10, 0, NOPE), "g_mla_q_pe": (10, 128, ROPE),
            "g_mla_k_nope": (10, 256, NOPE), "g_mla_k_pe": (10, 384, ROPE), "g_dil_q": (10, 512, DIL_DIM),
            "g_dil_k": (10, 640, DIL_DIM), "g_ffn_norm": (11, 0, D_MODEL), "b_conv": (12, 0, 2 * D_FF)}
SMALL_PARAMS = tuple(n for n in SMALL_AT if n != "loss")


def _pack_small(values):
    by_row = {}
    for name, (row, off, n) in SMALL_AT.items():
        by_row.setdefault(row, []).append((off, values[name].reshape(-1).astype(F32)))
    out = []
    for row in sorted(by_row):
        pieces, at = [], 0
        for off, v in sorted(by_row[row], key=lambda t: t[0]):
            pieces += [jnp.zeros((off - at,), F32), v]
            at = off + v.shape[0]
        flat = jnp.concatenate(pieces)
        nrows = -(-flat.shape[0] // SMALL_COLS)
        out.append(jnp.pad(flat, (0, nrows * SMALL_COLS - flat.shape[0])).reshape(nrows, SMALL_COLS))
    packed = jnp.concatenate(out, axis=0)
    return jnp.pad(packed, ((0, SMALL_ROWS - packed.shape[0]), (0, 0)))


def _adam(w, g, m, v):
    c1 = 1.0 - ADAM_B1 ** ADAM_STEP
    c2 = 1.0 - ADAM_B2 ** ADAM_STEP
    m2 = ADAM_B1 * m + (1.0 - ADAM_B1) * g
    v2 = ADAM_B2 * v + (1.0 - ADAM_B2) * (g * g)
    return -ADAM_LR * ((m2 / c1) / (jnp.sqrt(v2 / c2) + ADAM_EPS) + ADAM_WD * w), m2, v2


def adamw_small(name, stack, params):
    flat = [a for n in SMALL_PARAMS for a in params[n]]

    def body(stack_ref, *refs):
        ins, outs = refs[:len(flat)], refs[len(flat):]
        g_all = stack_ref[0]
        for p in range(1, N_DEV):
            g_all = g_all + stack_ref[p]
        outs[0][...] = g_all[0:1, 0:1]
        for i, pname in enumerate(SMALL_PARAMS):
            row, off, n = SMALL_AT[pname]
            w_ref, m_ref, v_ref = ins[3 * i:3 * i + 3]
            go_ref, d_ref, mo_ref, vo_ref = outs[1 + 4 * i:5 + 4 * i]
            for c0 in range(0, n, SMALL_COLS):
                cn = min(SMALL_COLS, n - c0)
                r = row + c0 // SMALL_COLS
                g = g_all[r:r + 1, off:off + cn]
                cols = (slice(None), slice(c0, c0 + cn))
                d, m2, v2 = _adam(w_ref[cols], g, m_ref[cols], v_ref[cols])
                go_ref[cols], d_ref[cols], mo_ref[cols], vo_ref[cols] = g, d, m2, v2

    whole = lambda a: pl.BlockSpec(a.shape, lambda: (0,) * a.ndim)
    out_shape = [jax.ShapeDtypeStruct((1, 1), F32)] + [jax.ShapeDtypeStruct(a.shape, F32) for n in SMALL_PARAMS for a in params[n][:1] * 4]
    res = pl.pallas_call(body, name=name, in_specs=[whole(stack)] + [whole(a) for a in flat],
                         out_specs=[pl.BlockSpec(s.shape, lambda s=s: (0,) * len(s.shape)) for s in out_shape],
                         out_shape=out_shape, compiler_params=_params())(stack, *flat)
    return res[0], {n: res[1 + 4 * i:5 + 4 * i] for i, n in enumerate(SMALL_PARAMS)}


def _local_step(x, pos, mod, target, w, fetch, emit):
    S = SEQ
    sh1, sc1, g1, sh2, sc2, g2 = [mod[:, i * D_MODEL:(i + 1) * D_MODEL] for i in range(6)]
    zeros = lambda n: jnp.zeros((1, n), F32)
    g_q = jnp.concatenate([w["g_mla_q_nope"], w["g_mla_q_pe"], zeros(LANES - NOPE - ROPE)], axis=1)
    g_k = jnp.concatenate([w["g_mla_k_nope"], zeros(LANES - NOPE)], axis=1)
    g_kpe = jnp.concatenate([zeros(KPE_LO), w["g_mla_k_pe"], zeros(LANES - KPE_LO - ROPE)], axis=1)
    g_dq = jnp.concatenate([w["g_dil_q"]] * 2, axis=1)
    g_dk = jnp.concatenate([w["g_dil_k"]] * 2, axis=1)
    b_conv = w["b_conv"]

    def inv_freq(d):
        return jnp.power(ROPE_THETA, -2.0 * jnp.arange(d // 2, dtype=F32) / d)

    f_mla = jnp.concatenate([jnp.zeros((KPE_LO,), F32), inv_freq(ROPE), inv_freq(ROPE), jnp.zeros((LANES - KPE_LO - ROPE,), F32)])
    f_dil = jnp.concatenate([inv_freq(DIL_DIM)] * 4)

    def tables_fn(rows, params):
        (p,), (fa, fb) = rows, params
        return [jnp.cos(p * fa), jnp.sin(p * fa), jnp.cos(p * fb), jnp.sin(p * fb)], []

    cos_m, sin_m, cos_d, sin_d = rowwise("rope_tables", tables_fn, [pos], [f_mla.reshape(1, LANES), f_dil.reshape(1, LANES)],
                                         [(LANES, F32)] * 4)
    tables = [cos_m, sin_m, cos_d, sin_d]
    H_M, H_D = ROPE // 2, DIL_DIM // 2

    def ln1_fn(rows, params):
        (xv,), (g, sc, sh) = rows, params
        y, _, _ = _rms(xv, g)
        return [y * (1.0 + sc) + sh], []

    (h,) = rowwise("ln1_fwd", ln1_fn, [x], [w["g_mix_norm"], sc1, sh1], [(D_MODEL, MXU_DTYPE)])
    w_in = fetch("w_in", h)
    proj = matmul("proj_fwd", h, w_in, "nt", tm=512)

    def post_fn(rows, params):
        (pv, cm, sm, cd, sd), (gq, gkv, gkp, gdq, gdk) = rows, params
        kper = _rope(_grms(pv[:, P_KPE:P_QD], gkp, KPE_GROUPS)[0], cm, sm, H_M)
        qd = [_rope(_grms(c, gdq, DIL_GROUPS)[0], cd, sd, H_D) for c in _chunks(pv[:, P_QD:P_KD])]
        kd = [_rope(_grms(c, gdk, DIL_GROUPS)[0], cd, sd, H_D) for c in _chunks(pv[:, P_KD:P_VD])]
        return [_rms(pv[:, P_QLAT:P_KVLAT], gq)[0], _rms(pv[:, P_KVLAT:P_KPE], gkv)[0], kper,
                jnp.concatenate(qd, axis=1), jnp.concatenate(kd, axis=1)], []

    post_params = [w["g_q_lat"], w["g_kv_lat"], g_kpe, g_dq, g_dk]
    qln, kvn, kper, qd_r, kd_r = rowwise(
        "proj_post", post_fn, [proj] + tables, post_params,
        [(Q_LORA, MXU_DTYPE), (KV_LORA, MXU_DTYPE), (LANES, MXU_DTYPE)] + [(DIL_WIDTH, F32)] * 2, tm=256)
    w_q_b, w_kv_b = fetch("w_q_b", qln), fetch("w_kv_b", kvn)
    q = matmul("q_fwd", qln, w_q_b, "nt", tm=1024)
    kv = matmul("kv_fwd", kvn, w_kv_b, "nn", tm=1024)

    def mla_prep_fn(rows, params):
        (qv, kvv, kp, cm, sm), (gq, gk) = rows, params
        value_lanes = _lane(kp.shape) >= NOPE
        qs, ks, vs = [], [], []
        for qc, kc in zip(_chunks(qv), _chunks(kvv), strict=True):
            qs.append(_rope(_grms(qc, gq, Q_GROUPS)[0], cm, sm, H_M))
            ks.append(_grms(kc, gk, K_GROUPS)[0] + kp)
            vs.append(jnp.where(value_lanes, kc, 0.0))
        return [jnp.concatenate(t, axis=1) for t in (qs, ks, vs)], []

    q_mla, k_mla, v_mla = rowwise("mla_prep", mla_prep_fn, [q, kv, kper, cos_m, sin_m], [g_q, g_k],
                                  [(HEADS * LANES, MXU_DTYPE)] * 3, tm=256)
    mla_scale = (NOPE + ROPE) ** -0.5
    o_cat, lse_mla = mla_fwd("mla_fwd", q_mla, k_mla, v_mla, mla_scale)

    band = [band_fwd(f"band{dil}_fwd", qd_r, kd_r, proj, dil) for dil in DILATIONS]
    o_cat, lse_mix = combine_fwd("dil_combine", [b[0] for b in band], [b[1] for b in band], o_cat)
    w_o = fetch("w_o", o_cat)
    mix = matmul("mix_fwd", o_cat, w_o, "nn", tm=512)

    def mid_fn(rows, params):
        (xv, mx), (gate1, g, sc, sh) = rows, params
        x1 = xv + gate1 * mx
        y, _, _ = _rms(x1, g)
        return [x1, y * (1.0 + sc) + sh], []

    x1, h2 = rowwise("mid_fwd", mid_fn, [x, mix], [g1, w["g_ffn_norm"], sc2, sh2], [(D_MODEL, F32), (D_MODEL, MXU_DTYPE)])
    w_up, w_conv, w_down = fetch("w_up", h2), fetch("w_conv", h2), fetch("w_down", h2)
    dn, up = ffn_fwd("ffn_fwd", h2, w_up, w_conv, b_conv, w_down)

    def final_fn(rows, params):
        (x1v, dnv, tgt), (gate2,) = rows, params
        r = x1v + gate2 * dnv - tgt
        dy = r * (1.0 / D_MODEL)
        loss = jnp.sum(_colsum(r * r), axis=-1, keepdims=True) * (0.5 / D_MODEL)
        return [dy, gate2 * dy], [loss, _colsum(dy * dnv)]

    dy, d_dn, loss, dg2 = rowwise("loss_head", final_fn, [x1, dn, target], [g2], [(D_MODEL, F32), (D_MODEL, MXU_DTYPE)],
                                  [1, D_MODEL])
    dh2, g_up, g_down, g_w_conv, g_b_conv = ffn_bwd("ffn_bwd", h2, up, w_up, w_conv, b_conv, d_dn, w_down)
    emit("w_down", g_down)
    emit("w_conv", g_w_conv)
    sent = emit("w_up", g_up)

    def mid_bwd_fn(rows, params):
        (dh2v, dyv, x1v, mx), (gate1, g, sc) = rows, params
        yn, n, rstd = _rms(x1v, g)
        dx_n, dg = _rms_bwd(dh2v * (1.0 + sc), n, rstd, g)
        dx1 = dyv + dx_n
        return [dx1, gate1 * dx1], [dg, _colsum(dh2v * yn), _colsum(dh2v), _colsum(dx1 * mx)]

    dx1, dmix, dg_ffn, dsc2, dsh2, dg1 = rowwise(
        "mid_bwd", mid_bwd_fn, [dh2, dy, x1, mix], [g1, w["g_ffn_norm"], sc2], [(D_MODEL, F32), (D_MODEL, MXU_DTYPE)],
        [D_MODEL] * 4, dep=sent)

    sent = emit("w_o", matmul("mix_wgrad", o_cat, dmix, "tn", tm=512, out_dtype=MXU_DTYPE))
    do_cat = matmul("mix_dgrad", dmix, w_o, "nt", tm=512, dep=sent)
    dband = [band_bwd(f"band{dil}_bwd", qd_r, kd_r, proj, b[1], lse_mix, o_cat, do_cat, dil) for dil, b in zip(DILATIONS, band)]
    dq_mla, dkv_mla, dkper = mla_bwd("mla_bwd", q_mla, k_mla, v_mla, o_cat, do_cat, lse_mla, mla_scale)

    def mla_prep_bwd_fn(rows, params):
        (dqv, dkvv, qv, kvv, cm, sm), (gq, gk) = rows, params
        nope_lanes = _lane(cm.shape) < NOPE
        dqs, dkvs, dgq, dgk = [], [], 0.0, 0.0
        for dqc, dkc, qc, kc in zip(_chunks(dqv), _chunks(dkvv), _chunks(qv), _chunks(kvv), strict=True):
            _, n, rstd = _grms(qc, gq, Q_GROUPS)
            dx, dg = _grms_bwd(_rope_bwd(dqc, cm, sm, H_M), n, rstd, gq, Q_GROUPS)
            dqs.append(dx)
            dgq = dgq + dg
            _, n, rstd = _grms(kc, gk, K_GROUPS)
            dx, dg = _grms_bwd(dkc, n, rstd, gk, K_GROUPS)
            dkvs.append(jnp.where(nope_lanes, dx, dkc))
            dgk = dgk + dg
        return [jnp.concatenate(dqs, axis=1), jnp.concatenate(dkvs, axis=1)], [dgq, dgk]

    dq, dkv, dg_q, dg_k = rowwise("mla_prep_bwd", mla_prep_bwd_fn, [dq_mla, dkv_mla, q, kv, cos_m, sin_m], [g_q, g_k],
                                  [(HEADS * LANES, MXU_DTYPE)] * 2, [LANES, LANES], tm=256)
    emit("w_q_b", matmul("q_wgrad", dq, qln, "tn", out_dtype=MXU_DTYPE))
    emit("w_kv_b", matmul("kv_wgrad", kvn, dkv, "tn", out_dtype=MXU_DTYPE))
    dqln = matmul("q_dgrad", dq, w_q_b, "nn", tm=1024)
    dkvn = matmul("kv_dgrad", dkv, w_kv_b, "nt", tm=1024)

    def pre_bwd_fn(rows, params):
        dql, dkvl, dkp = rows[0:3]
        dqd_, dkd_, dvd_ = [rows[3 + 3 * i] + rows[4 + 3 * i] + rows[5 + 3 * i] for i in range(3)]
        pv, cm, sm, cd, sd = rows[12:]
        gq, gkv, gkp, gdq, gdk = params
        r_q = _norm_bwd(dql, pv[:, P_QLAT:P_KVLAT], gq)
        r_kv = _norm_bwd(dkvl, pv[:, P_KVLAT:P_KPE], gkv)
        _, n, rstd = _grms(pv[:, P_KPE:P_QD], gkp, KPE_GROUPS)
        r_kp = _grms_bwd(_rope_bwd(dkp, cm, sm, H_M), n, rstd, gkp, KPE_GROUPS)
        outs, dgs = [r_q[0], r_kv[0], r_kp[0]], []
        for dval, lo, g in ((dqd_, P_QD, gdq), (dkd_, P_KD, gdk)):
            dg_sum = 0.0
            for dc, xc in zip(_chunks(dval), _chunks(pv[:, lo:lo + DIL_WIDTH]), strict=True):
                _, n, rstd = _grms(xc, g, DIL_GROUPS)
                dx, dg = _grms_bwd(_rope_bwd(dc, cd, sd, H_D), n, rstd, g, DIL_GROUPS)
                outs.append(dx)
                dg_sum = dg_sum + dg
            dgs.append(dg_sum)
        return [jnp.concatenate(outs + [dvd_], axis=1)], [r_q[1], r_kv[1], r_kp[1]] + dgs

    dproj, dg_q_lat, dg_kv_lat, dg_kpe, dg_dq, dg_dk = rowwise(
        "proj_pre_bwd", pre_bwd_fn,
        [dqln, dkvn, dkper] + [d[i] for i in range(3) for d in dband] + [proj] + tables, post_params,
        [(P_END, MXU_DTYPE)], [Q_LORA, KV_LORA, LANES, LANES, LANES], tm=256)
    sent = emit("w_in", matmul("proj_wgrad", dproj, h, "tn", tn=512, out_dtype=MXU_DTYPE))
    dh = matmul("proj_dgrad", dproj, w_in, "nn", tm=512, dep=sent)

    def ln1_bwd_fn(rows, params):
        (dhv, dres, xv), (g, sc) = rows, params
        yn, n, rstd = _rms(xv, g)
        dx_n, dg = _rms_bwd(dhv * (1.0 + sc), n, rstd, g)
        return [dres + dx_n], [dg, _colsum(dhv * yn), _colsum(dhv)]

    grad_x, dg_mix, dsc1, dsh1 = rowwise("ln1_bwd", ln1_bwd_fn, [dh, dx1, x], [w["g_mix_norm"], sc1], [(D_MODEL, F32)],
                                         [D_MODEL] * 3)
    dmod = jnp.concatenate([dsh1, dsc1, dg1, dsh2, dsc2, dg2], axis=-1)
    small = {"loss": loss, "b_ada": dmod, "g_mix_norm": dg_mix, "g_q_lat": dg_q_lat, "g_kv_lat": dg_kv_lat,
             "g_mla_q_nope": dg_q[:, :NOPE], "g_mla_q_pe": dg_q[:, NOPE:NOPE + ROPE], "g_mla_k_nope": dg_k[:, :NOPE],
             "g_mla_k_pe": dg_kpe[:, KPE_LO:KPE_LO + ROPE], "g_dil_q": dg_dq[:, :DIL_DIM] + dg_dq[:, DIL_DIM:],
             "g_dil_k": dg_dk[:, :DIL_DIM] + dg_dk[:, DIL_DIM:], "g_ffn_norm": dg_ffn,
             "b_conv": g_b_conv}
    return grad_x, small


COL_SHARDED = ("w_kv_b", "w_conv")
ROW_SHARDED = ("w_o", "w_down") + TRANSPOSED
ADAM_TILE = {"w_ada": 256, "w_up": 176, "w_down": 176}
GATHER_GROUPS = (("w_in",), ("w_q_b", "w_kv_b"), ("w_o",), ("w_up", "w_conv", "w_down"))
SCATTER_GROUPS = (("w_down", "w_conv", "w_up"), ("w_o",), ("w_q_b", "w_kv_b", "w_in"))
OUT_WEIGHTS = ("w_ada", "b_ada", "g_mix_norm", "w_in", "g_q_lat", "w_q_b", "g_kv_lat", "w_kv_b", "g_mla_q_nope", "g_mla_q_pe",
               "g_mla_k_nope", "g_mla_k_pe", "g_dil_q", "g_dil_k", "w_o", "g_ffn_norm", "w_up", "w_conv", "b_conv", "w_down")


def kernel(x, c, positions, w_ada, b_ada, g_mix_norm, w_in, g_q_lat, w_q_b, g_kv_lat, w_kv_b, g_mla_q_nope, g_mla_q_pe, g_mla_k_nope, g_mla_k_pe, g_dil_q, g_dil_k, w_o, g_ffn_norm, w_up, w_conv, b_conv, w_down, loss_target, m_w_ada, m_b_ada, m_g_mix_norm, m_w_in, m_g_q_lat, m_w_q_b, m_g_kv_lat, m_w_kv_b, m_g_mla_q_nope, m_g_mla_q_pe, m_g_mla_k_nope, m_g_mla_k_pe, m_g_dil_q, m_g_dil_k, m_w_o, m_g_ffn_norm, m_w_up, m_w_conv, m_b_conv, m_w_down, v_w_ada, v_b_ada, v_g_mix_norm, v_w_in, v_g_q_lat, v_w_q_b, v_g_kv_lat, v_w_kv_b, v_g_mla_q_nope, v_g_mla_q_pe, v_g_mla_k_nope, v_g_mla_k_pe, v_g_dil_q, v_g_dil_k, v_w_o, v_g_ffn_norm, v_w_up, v_w_conv, v_b_conv, v_w_down):
    args = dict(locals())
    xi, yi, ci = _place()
    me = 4 * xi + 2 * yi + ci
    def local(prefix, n):
        a = args[prefix + n][0]
        return a.T if n in TRANSPOSED else a

    shard = {n: local("", n) for n in COL_SHARDED + ROW_SHARDED + ("w_ada",)}
    small_w = {n: args[n] for n in SMALL_PARAMS}

    (c_all,) = all_gather("gather_c", [c])
    (sc_all,) = rowwise("silu_c", lambda rows, params: ([_silu(rows[0])], []), [c_all.reshape(N_DEV, D_MODEL)], [],
                        [(D_MODEL, MXU_DTYPE)])
    mod_part = matmul("ada_fwd", sc_all, shard["w_ada"], "nn")
    (mod_all,) = all_gather("gather_mod", [mod_part])

    payload = {n: shard[n] if n == "w_conv" else shard[n].astype(MXU_DTYPE) for n in COL_SHARDED + ROW_SHARDED}
    gather_order = [n for grp in GATHER_GROUPS for n in grp]
    gathered = exchange_start("gather_start", [payload[n] for n in gather_order], gather=True, after=mod_all)
    after_start = gathered[-1]
    full = {}

    def fetch(name, after):
        if name not in full:
            (i, grp), = [(i, grp) for i, grp in enumerate(GATHER_GROUPS) if name in grp]
            srcs, lands = exchange_wait(f"gather{i}_wait", gathered, True, after, gather_order.index(grp[0]), len(grp))
            for n, src, land in zip(grp, srcs, lands, strict=True):
                stack = lax.dynamic_update_index_in_dim(land, src, me, 0)
                full[n] = to_kernel_layout(n, _gather_cols(stack) if n in COL_SHARDED else _gather_rows(stack))
        return full[name]

    mod_row = lax.dynamic_index_in_dim(mod_all, me, axis=1, keepdims=False).reshape(1, 6 * D_MODEL)
    (mod,) = rowwise("ada_bias", lambda rows, params: ([rows[0] + rows[1]], []), [mod_row, b_ada], [], [(6 * D_MODEL, F32)],
                     dep=after_start)

    own, pending, scatters = {}, {}, {}

    def emit(name, grad):
        grad = from_kernel_layout(name, grad)
        parts = _scatter_cols(grad) if name in COL_SHARDED else _scatter_rows(grad)
        own[name] = lax.dynamic_index_in_dim(parts, me, 0, keepdims=False)
        pending[name] = parts
        for i, grp in enumerate(SCATTER_GROUPS):
            if name == grp[-1]:
                scatters[i] = exchange_start(f"scatter{i}_start", [pending[n] for n in grp], gather=False)
                return scatters[i][-1]
        return None

    pos = positions.reshape(SEQ, 1).astype(F32)
    grad_x, small = _local_step(x[0], pos, mod, loss_target[0], small_w, fetch, emit)

    res, done = {}, grad_x
    for i, grp in enumerate(SCATTER_GROUPS):
        _, lands = exchange_wait(f"scatter{i}_wait", scatters[i], False, done)
        for n, land in zip(grp, lands, strict=True):
            res[n] = adamw(f"adamw_{n}", shard[n], [own[n], land], local("m_", n), local("v_", n), ADAM_TILE.get(n))
            done = res[n][0]
            if n in TRANSPOSED:
                res[n] = [r.T for r in res[n]]
    (small_all,) = all_gather("gather_small", [_pack_small(small)], after=done)
    loss, small_res = adamw_small("adamw_small", small_all, {n: (args[n], args["m_" + n], args["v_" + n]) for n in SMALL_PARAMS})
    row, _, n_mod = SMALL_AT["b_ada"]
    dmod_all = small_all[:, row:row + n_mod // SMALL_COLS, :].reshape(N_DEV, n_mod)
    dmod_mine = lax.dynamic_slice_in_dim(dmod_all, me * (6 * D_MODEL // N_DEV), 6 * D_MODEL // N_DEV, axis=1)
    g_w_ada = matmul("ada_wgrad", sc_all, dmod_mine, "tn")
    res["w_ada"] = adamw("adamw_w_ada", shard["w_ada"], [g_w_ada], m_w_ada[0], v_w_ada[0], ADAM_TILE["w_ada"])

    def leaf(kind, n):
        if n in res:
            return res[n][kind][None]
        return small_res[n][kind]

    return (loss.reshape(()), grad_x[None], *[leaf(k, n) for k in range(4) for n in OUT_WEIGHTS])
```

```python
import jax
import jax.numpy as jnp
from jax import lax
from jax.experimental import pallas as pl
from jax.experimental.pallas import tpu as pltpu

F32 = jnp.float32
MXU_DTYPE = jnp.bfloat16

N_DEV = 8
D_MODEL = 1024
SEQ = 2048
HEADS = 8
NOPE = 64
ROPE = 32
Q_LORA = 512
KV_LORA = 256
DIL_DIM = 64
DIL_WIDTH = HEADS * DIL_DIM
DILATIONS = (1, 4, 16)
SPAN = 128
D_FF = 2816
LANES = 128
ROPE_THETA = 10000.0
EPS = 1e-6
NEG_INF = -1e30
ADAM_LR, ADAM_B1, ADAM_B2, ADAM_EPS, ADAM_WD, ADAM_STEP = 0.001, 0.9, 0.999, 1e-08, 0.01, 10
VMEM_LIMIT = 56 * 1024 * 1024
MESH_ID = pl.DeviceIdType.MESH

P_QLAT, P_KVLAT, P_KPE, P_QD, P_KD, P_VD, P_END = 0, 512, 768, 896, 1408, 1920, 2432
KPE_LO = 64
MIX_IN = HEADS * LANES + DIL_WIDTH


def _params(**kw):
    return pltpu.CompilerParams(vmem_limit_bytes=VMEM_LIMIT, **kw)


def rowwise(name, fn, rows, params, out_rows, out_accs=(), tm=512, dep=None):
    deps = [] if dep is None else [dep]
    rows = [r if isinstance(r, tuple) else (r, r.shape[1], 0) for r in rows]
    R = rows[0][0].shape[0]
    tm = min(tm, R)
    steps = R // tm
    assert steps * tm == R
    in_specs = []
    for a, width, cb in rows:
        ri = a.shape[0]
        per = ri // tm
        assert per * tm == ri
        if ri == R:
            in_specs.append(pl.BlockSpec((tm, width), lambda i, cb=cb: (i, cb)))
        else:
            in_specs.append(pl.BlockSpec((tm, width), lambda i, per=per, cb=cb: (i % per, cb)))
    for p in params:
        in_specs.append(pl.BlockSpec(p.shape, lambda i: (0,) * p.ndim))
    in_specs += [pl.BlockSpec(memory_space=pl.ANY)] * len(deps)
    out_shape = [jax.ShapeDtypeStruct((R, d), dt) for d, dt in out_rows]
    out_specs = [pl.BlockSpec((tm, d), lambda i: (i, 0)) for d, _ in out_rows]
    out_shape += [jax.ShapeDtypeStruct((1, n), F32) for n in out_accs]
    out_specs += [pl.BlockSpec((1, n), lambda i: (0, 0)) for n in out_accs]
    nr, npar, no, na = len(rows), len(params), len(out_rows), len(out_accs)

    def body(*refs):
        rvals = [r[...] for r in refs[:nr]]
        pvals = [r[...] for r in refs[nr:nr + npar]]
        outs, accs = fn(rvals, pvals)
        first_out = nr + npar + len(deps)
        for ref, v in zip(refs[first_out:first_out + no], outs, strict=True):
            ref[...] = v.astype(ref.dtype)
        if na:
            acc_refs = refs[first_out + no:]
            i = pl.program_id(0)

            @pl.when(i == 0)
            def _():
                for ref, v in zip(acc_refs, accs, strict=True):
                    ref[...] = v

            @pl.when(i > 0)
            def _():
                for ref, v in zip(acc_refs, accs, strict=True):
                    ref[...] += v

    res = pl.pallas_call(body, name=name, grid=(steps,), in_specs=in_specs, out_specs=out_specs,
                         out_shape=out_shape, compiler_params=_params())(*[r[0] for r in rows], *params, *deps)
    return list(res)


_DIMS = {"nn": ((1,), (0,)), "nt": ((1,), (1,)), "tn": ((0,), (0,))}


def _dot(a, b, mode="nn"):
    return lax.dot_general(a.astype(MXU_DTYPE), b.astype(MXU_DTYPE), (_DIMS[mode], ((), ())),
                           preferred_element_type=F32)


def matmul(name, a, b, mode, tm=None, tn=None, tk=None, out_dtype=F32, dep=None, a_mmap=None, b_nmap=None, b_kmap=None):
    if mode == "tn":
        K, M = a.shape
    else:
        M, K = a.shape
    N = b.shape[0] if mode == "nt" else b.shape[1]
    tm, tn, tk = tm or M, tn or N, tk or K
    nm, nn, nk = M // tm, N // tn, K // tk
    assert nm * tm == M and nn * tn == N and nk * tk == K
    same = lambda idx: idx
    a_mmap, b_nmap, b_kmap = a_mmap or same, b_nmap or same, b_kmap or same
    if mode == "tn":
        a_spec = pl.BlockSpec((tk, tm), lambda i, j, k: (k, a_mmap(i)))
    else:
        a_spec = pl.BlockSpec((tm, tk), lambda i, j, k: (a_mmap(i), k))
    if mode == "nt":
        b_spec = pl.BlockSpec((tn, tk), lambda i, j, k: (b_nmap(j), b_kmap(k)))
    else:
        b_spec = pl.BlockSpec((tk, tn), lambda i, j, k: (b_kmap(k), b_nmap(j)))
    deps = [] if dep is None else [dep]

    def body(a_ref, b_ref, *rest):
        o_ref, scratch = rest[len(deps)], rest[len(deps) + 1:]
        p = _dot(a_ref[...], b_ref[...], mode)
        if nk == 1:
            o_ref[...] = p.astype(o_ref.dtype)
        else:
            acc = scratch[0]
            k = pl.program_id(2)

            @pl.when(k == 0)
            def _():
                acc[...] = p

            @pl.when(k > 0)
            def _():
                acc[...] += p

            @pl.when(k == nk - 1)
            def _():
                o_ref[...] = acc[...].astype(o_ref.dtype)

    return pl.pallas_call(
        body, name=name, grid=(nm, nn, nk), in_specs=[a_spec, b_spec] + [pl.BlockSpec(memory_space=pl.ANY)] * len(deps),
        out_specs=pl.BlockSpec((tm, tn), lambda i, j, k: (i, j)),
        out_shape=jax.ShapeDtypeStruct((M, N), out_dtype),
        scratch_shapes=[pltpu.VMEM((tm, tn), F32)] if nk > 1 else [],
        compiler_params=_params())(a, b, *deps)


def _rms(x, g):
    rstd = lax.rsqrt(jnp.mean(x * x, axis=-1, keepdims=True) + EPS)
    n = x * rstd
    return n * g, n, rstd


def _rms_bwd(dy, n, rstd, g):
    dg = jnp.sum(dy * n, axis=0, keepdims=True)
    dn = dy * g
    dx = rstd * (dn - n * jnp.mean(dn * n, axis=-1, keepdims=True))
    return dx, dg


def _norm_bwd(dy, x, g):
    _, n, rstd = _rms(x, g)
    return _rms_bwd(dy, n, rstd, g)


def _colsum(v):
    return jnp.sum(v, axis=0, keepdims=True)


def _silu(x):
    return x * (1.0 / (1.0 + jnp.exp(-x)))


def _lane(shape):
    return lax.broadcasted_iota(jnp.int32, shape, 1)


def _group_mean(v, groups):
    i = lax.broadcasted_iota(jnp.int32, (LANES, LANES), 0)
    j = lax.broadcasted_iota(jnp.int32, (LANES, LANES), 1)
    g = jnp.zeros((LANES, LANES), F32)
    for lo, hi in groups:
        g = jnp.where((i >= lo) & (i < hi) & (j >= lo) & (j < hi), 1.0 / (hi - lo), g)
    head = v.astype(MXU_DTYPE)
    return _dot(head, g) + _dot(v - head.astype(F32), g)


def _in_groups(shape, groups):
    lane = _lane(shape)
    m = jnp.zeros(shape, jnp.bool_)
    for lo, hi in groups:
        m = m | ((lane >= lo) & (lane < hi))
    return m


def _grms(x, g, groups):
    rstd = lax.rsqrt(_group_mean(x * x, groups) + EPS)
    n = jnp.where(_in_groups(x.shape, groups), x * rstd, 0.0)
    return n * g, n, rstd


def _grms_bwd(dy, n, rstd, g, groups):
    dn = dy * g
    return rstd * (dn - n * _group_mean(dn * n, groups)), _colsum(dy * n)


def _rot(x, half, transpose=False):
    first = (_lane(x.shape) % (2 * half)) < half
    up = pltpu.roll(x, LANES - half, axis=1)
    down = pltpu.roll(x, half, axis=1)
    return jnp.where(first, up, -down) if transpose else jnp.where(first, -up, down)


def _rope(x, cos, sin, half):
    return x * cos + _rot(x, half) * sin


def _rope_bwd(dy, cos, sin, half):
    return dy * cos + _rot(dy * sin, half, transpose=True)


def _chunks(x):
    return [x[:, i:i + LANES] for i in range(0, x.shape[1], LANES)]


Q_GROUPS = ((0, NOPE), (NOPE, NOPE + ROPE))
K_GROUPS = ((0, NOPE),)
KPE_GROUPS = ((KPE_LO, KPE_LO + ROPE),)
DIL_GROUPS = ((0, DIL_DIM), (DIL_DIM, 2 * DIL_DIM))


def _col(width, rows=SEQ):
    return pl.BlockSpec((rows, width), lambda h: (0, h))


def _causal_tail(s, tq, fill):
    diag = s[:, s.shape[1] - tq:]
    keep = lax.broadcasted_iota(jnp.int32, diag.shape, 1) <= lax.broadcasted_iota(jnp.int32, diag.shape, 0)
    diag = jnp.where(keep, diag, fill)
    return diag if s.shape[1] == tq else jnp.concatenate([s[:, :s.shape[1] - tq], diag], axis=1)


def mla_fwd(name, q, k, v, scale, tq=256):
    S = q.shape[0]

    def body(q_ref, k_ref, v_ref, o_ref, lse_ref):
        nb = S // tq
        blk = lambda i: slice(i * tq, (i + 1) * tq)

        def scores(i):
            return _dot(q_ref[blk(i), :], k_ref[:(i + 1) * tq, :], "nt")

        def softmax(i, s):
            s = _causal_tail(s * scale, tq, NEG_INF)
            m = jnp.max(s, axis=-1, keepdims=True)
            e = jnp.exp(s - m)
            l = jnp.sum(e, axis=-1, keepdims=True)
            lse_ref[0, blk(i), :] = m + jnp.log(l)
            return (e * (1.0 / l)).astype(MXU_DTYPE)

        def weighted(i, p):
            o_ref[blk(i), :] = _dot(p, v_ref[:(i + 1) * tq, :])

        s, p_prev = scores(0), None
        for i in range(nb):
            s_next = scores(i + 1) if i + 1 < nb else None
            if p_prev is not None:
                weighted(i - 1, p_prev)
            p_prev, s = softmax(i, s), s_next
        weighted(nb - 1, p_prev)

    return pl.pallas_call(
        body, name=name, grid=(HEADS,), in_specs=[_col(LANES)] * 3,
        out_specs=[_col(LANES), pl.BlockSpec((1, S, 1), lambda h: (h, 0, 0))],
        out_shape=[jax.ShapeDtypeStruct((S, MIX_IN), F32), jax.ShapeDtypeStruct((HEADS, S, 1), F32)],
        compiler_params=_params())(q, k, v)


def mla_bwd(name, q, k, v, o, do, lse, scale, tq=256):
    S = q.shape[0]

    def body(q_ref, k_ref, v_ref, o_ref, do_ref, lse_ref, dq_ref, dkv_ref, dkpe_ref, dk_acc, dv_acc):
        dk_acc[...] = jnp.zeros_like(dk_acc)
        dv_acc[...] = jnp.zeros_like(dv_acc)
        for i in range(S // tq):
            kext = (i + 1) * tq
            blk = slice(i * tq, kext)
            qi, kk, vv = q_ref[blk, :], k_ref[:kext, :], v_ref[:kext, :]
            doi = do_ref[blk, :]
            s = _causal_tail(_dot(qi, kk, "nt") * scale, tq, NEG_INF)
            p = jnp.exp(s - lse_ref[0, blk, :])
            dp = _dot(doi, vv, "nt")
            delta = jnp.sum(doi * o_ref[blk, :], axis=-1, keepdims=True)
            ds = p * (dp - delta) * scale
            dq_ref[blk, :] = _dot(ds, kk)
            dk_acc[:kext, :] += _dot(ds, qi, "tn")
            dv_acc[:kext, :] += _dot(p, doi, "tn")
        dk = dk_acc[...]
        lane = _lane(dk.shape)
        dkv_ref[...] = jnp.where(lane < NOPE, dk, 0.0) + dv_acc[...]
        dkpe = jnp.where((lane >= KPE_LO) & (lane < KPE_LO + ROPE), dk, 0.0)
        h = pl.program_id(0)

        @pl.when(h == 0)
        def _():
            dkpe_ref[...] = dkpe

        @pl.when(h > 0)
        def _():
            dkpe_ref[...] += dkpe

    return pl.pallas_call(
        body, name=name, grid=(HEADS,),
        in_specs=[_col(LANES)] * 5 + [pl.BlockSpec((1, S, 1), lambda h: (h, 0, 0))],
        out_specs=[_col(LANES), _col(LANES), pl.BlockSpec((S, LANES), lambda h: (0, 0))],
        out_shape=[jax.ShapeDtypeStruct((S, HEADS * LANES), F32), jax.ShapeDtypeStruct((S, HEADS * LANES), F32),
                   jax.ShapeDtypeStruct((S, LANES), F32)],
        scratch_shapes=[pltpu.VMEM((S, LANES), F32), pltpu.VMEM((S, LANES), F32)],
        compiler_params=_params())(q, k, v, o, do, lse)


BAND_TQ = SPAN


def _band_blocks(L, tq):
    return [(i * tq, (i + 1) * tq, max(0, i * tq - SPAN)) for i in range(L // tq)]


def _band_mask(q0, q1, k0):
    shape = (q1 - q0, q1 - k0)
    dist = (lax.broadcasted_iota(jnp.int32, shape, 0) + q0) - (lax.broadcasted_iota(jnp.int32, shape, 1) + k0)
    return (dist >= 0) & (dist <= SPAN)


def _class_rows(r, dil, lo, hi):
    return pl.ds(r + dil * lo, hi - lo, stride=dil) if dil > 1 else pl.ds(lo, hi - lo)


def _stack_heads(t, lo):
    zero = jnp.zeros_like(t)
    return jnp.concatenate([jnp.where(lo, t, zero), jnp.where(lo, zero, t)], axis=0)


def _band_mask2(q0, q1, k0):
    n = q1 - q0
    shape = (2 * n, q1 - k0)
    i = lax.broadcasted_iota(jnp.int32, shape, 0)
    dist = (jnp.where(i >= n, i - n, i) + q0) - (lax.broadcasted_iota(jnp.int32, shape, 1) + k0)
    return (dist >= 0) & (dist <= SPAN)


def _pair_col(col0=0):
    return pl.BlockSpec((SEQ, LANES), lambda j: (0, col0 // LANES + j))


def band_fwd(name, q, k, v, dil):
    S = q.shape[0]
    L = S // dil
    tq = BAND_TQ
    scale = DIL_DIM ** -0.5

    def body(q_ref, k_ref, v_ref, o_ref, lse_ref):
        items = [(r, blk) for r in range(dil) for blk in _band_blocks(L, tq)]
        lo = _lane((tq, LANES)) < DIL_DIM

        def scores(item):
            r, (q0, q1, k0) = item
            qb = q_ref[_class_rows(r, dil, q0, q1), :].astype(MXU_DTYPE)
            return _dot(_stack_heads(qb, lo), k_ref[_class_rows(r, dil, k0, q1), :], "nt")

        def softmax(item, s):
            _, (q0, q1, k0) = item
            s = jnp.where(_band_mask2(q0, q1, k0), s * scale, NEG_INF)
            mx = jnp.max(s, axis=-1, keepdims=True)
            e = jnp.exp(s - mx)
            l = jnp.sum(e, axis=-1, keepdims=True)
            return (e * (1.0 / l)).astype(MXU_DTYPE), mx + jnp.log(l)

        def weighted(item, p, lse):
            r, (q0, q1, k0) = item
            pv = _dot(p, v_ref[_class_rows(r, dil, k0, q1), :])
            o_ref[_class_rows(r, dil, q0, q1), :] = jnp.where(lo, pv[:tq], pv[tq:])
            lse_ref[_class_rows(r, dil, q0, q1), :] = jnp.where(lo, lse[:tq], lse[tq:])

        s, prev = scores(items[0]), None
        for i, item in enumerate(items):
            s_next = scores(items[i + 1]) if i + 1 < len(items) else None
            if prev is not None:
                weighted(items[i - 1], *prev)
            prev, s = softmax(item, s), s_next
        weighted(items[-1], *prev)

    return pl.pallas_call(
        body, name=name, grid=(DIL_WIDTH // LANES,), in_specs=[_pair_col()] * 2 + [_pair_col(P_VD)], out_specs=[_pair_col()] * 2,
        out_shape=[jax.ShapeDtypeStruct((S, DIL_WIDTH), F32)] * 2, compiler_params=_params())(q, k, v)


def band_bwd(name, q, k, v, lse, lse_mix, o_cat, do_cat, dil):
    S = q.shape[0]
    L = S // dil
    tq = BAND_TQ
    scale = DIL_DIM ** -0.5

    def body(q_ref, k_ref, v_ref, lse_ref, mix_ref, o_ref, do_ref, dq_ref, dk_ref, dv_ref):
        dk_ref[...] = jnp.zeros_like(dk_ref)
        dv_ref[...] = jnp.zeros_like(dv_ref)
        items = [(r, blk) for r in range(dil) for blk in _band_blocks(L, tq)]
        lo = _lane((tq, LANES)) < DIL_DIM
        per_head = lambda t: jnp.concatenate([t[:, 0:1], t[:, DIL_DIM:DIL_DIM + 1]], axis=0)

        def scores(item):
            r, (q0, q1, k0) = item
            qrows, krows = _class_rows(r, dil, q0, q1), _class_rows(r, dil, k0, q1)
            lse_p, dout = lse_ref[qrows, :], do_ref[qrows, :]
            w2 = per_head(jnp.exp(lse_p - mix_ref[qrows, :]))
            dd = dout * o_ref[qrows, :]
            big_d = jnp.concatenate([jnp.sum(jnp.where(lo, dd, 0.0), axis=-1, keepdims=True),
                                     jnp.sum(jnp.where(lo, 0.0, dd), axis=-1, keepdims=True)], axis=0)
            q2 = _stack_heads(q_ref[qrows, :].astype(MXU_DTYPE), lo)
            dom = (_stack_heads(dout, lo) * w2).astype(MXU_DTYPE)
            return (_dot(q2, k_ref[krows, :], "nt"), _dot(dom, v_ref[krows, :], "nt"), per_head(lse_p), w2 * big_d, q2, dom)

        def softmax_bwd(item, s, dp, lse2, wd2, q2, dom):
            _, (q0, q1, k0) = item
            p = jnp.where(_band_mask2(q0, q1, k0), jnp.exp(s * scale - lse2), 0.0)
            return p.astype(MXU_DTYPE), (p * (dp - wd2) * scale).astype(MXU_DTYPE), q2, dom

        def grads(item, p, ds, q2, dom):
            r, (q0, q1, k0) = item
            qrows, krows = _class_rows(r, dil, q0, q1), _class_rows(r, dil, k0, q1)
            dq2 = _dot(ds, k_ref[krows, :])
            dq_ref[qrows, :] = jnp.where(lo, dq2[:tq], dq2[tq:])
            dk_ref[krows, :] += _dot(ds, q2, "tn")
            dv_ref[krows, :] += _dot(p, dom, "tn")

        sc, prev = scores(items[0]), None
        for i, item in enumerate(items):
            sc_next = scores(items[i + 1]) if i + 1 < len(items) else None
            if prev is not None:
                grads(items[i - 1], *prev)
            prev, sc = softmax_bwd(item, *sc), sc_next
        grads(items[-1], *prev)

    cat = _pair_col(HEADS * LANES)
    return pl.pallas_call(
        body, name=name, grid=(DIL_WIDTH // LANES,),
        in_specs=[_pair_col()] * 2 + [_pair_col(P_VD)] + [_pair_col()] * 2 + [cat] * 2, out_specs=[_pair_col()] * 3,
        out_shape=[jax.ShapeDtypeStruct((S, DIL_WIDTH), F32)] * 3,
        compiler_params=_params())(q, k, v, lse, lse_mix, o_cat, do_cat)


def combine_fwd(name, outs, lses, o_cat, tm=512):
    S = outs[0].shape[0]

    def body(o1, o2, o3, l1, l2, l3, cat_in, cat_out, mix_ref):
        ls = [l1[...], l2[...], l3[...]]
        m = jnp.maximum(jnp.maximum(ls[0], ls[1]), ls[2])
        e = [jnp.exp(l - m) for l in ls]
        den = e[0] + e[1] + e[2]
        cat_out[...] = (e[0] / den) * o1[...] + (e[1] / den) * o2[...] + (e[2] / den) * o3[...]
        mix_ref[...] = m + jnp.log(den)

    row = pl.BlockSpec((tm, DIL_WIDTH), lambda i: (i, 0))
    return pl.pallas_call(
        body, name=name, grid=(S // tm,), in_specs=[row] * 6 + [pl.BlockSpec(memory_space=pl.ANY)],
        out_specs=[pl.BlockSpec((tm, DIL_WIDTH), lambda i: (i, HEADS * LANES // DIL_WIDTH)), row],
        out_shape=[jax.ShapeDtypeStruct(o_cat.shape, F32), jax.ShapeDtypeStruct((S, DIL_WIDTH), F32)],
        input_output_aliases={6: 0}, compiler_params=_params())(*outs, *lses, o_cat)


def _shift_down(u, n, zero_head):
    out = pltpu.roll(u, n, axis=0)
    return jnp.where(lax.broadcasted_iota(jnp.int32, u.shape, 0) >= n, out, 0.0) if zero_head else out


def _shift_up(u, n, zero_tail):
    rows = u.shape[0]
    out = pltpu.roll(u, rows - n, axis=0)
    return jnp.where(lax.broadcasted_iota(jnp.int32, u.shape, 0) < rows - n, out, 0.0) if zero_tail else out


CONV_ROWS = 512
CONV_HALO = 16


def _conv_chunks(S, tail):
    out = []
    for r0 in range(0, S, CONV_ROWS):
        lo, hi = max(0, r0 - CONV_HALO), min(S, r0 + CONV_ROWS + (CONV_HALO if tail else 0))
        out.append((lo, hi, r0 - lo, CONV_ROWS))
    return out


CONV_TC = 256
CONV_NB = D_FF // CONV_TC


def _half_specs(rows, rows_axis=False):
    if rows_axis:
        return [pl.BlockSpec((rows, D_MODEL), lambda j: (j, 0)), pl.BlockSpec((rows, D_MODEL), lambda j: (j + CONV_NB, 0))]
    return [pl.BlockSpec((rows, CONV_TC), lambda j: (0, j)), pl.BlockSpec((rows, CONV_TC), lambda j: (0, j + CONV_NB))]


def _whole(a):
    return pl.BlockSpec(a.shape, lambda j: (0,) * a.ndim)


def _up_pair(h, ug_ref, uv_ref):
    return jnp.concatenate([_dot(h, ug_ref[...], "nt"), _dot(h, uv_ref[...], "nt")], axis=1)


def _conv_taps(uin, w, b, starts):
    u1, u2 = _shift_down(uin, 1, starts), _shift_down(uin, 2, starts)
    return u1, u2, w[2:3, :] * uin + w[1:2, :] * u1 + w[0:1, :] * u2 + b


def ffn_fwd(name, h, w_up_t, w_conv, b_conv, w_down):
    S = h.shape[0]

    def body(h_ref, ug_ref, uv_ref, wg_ref, wv_ref, bg_ref, bv_ref, wd_ref, dn_ref, up_ref):
        @pl.when(pl.program_id(0) == 0)
        def _():
            dn_ref[...] = jnp.zeros_like(dn_ref)

        w = jnp.concatenate([wg_ref[...], wv_ref[...]], axis=1)
        b = jnp.concatenate([bg_ref[...], bv_ref[...]], axis=1)
        chunks = _conv_chunks(S, tail=False)

        def project(c):
            lo, hi, keep, rows = c
            uin = _up_pair(h_ref[lo:hi, :], ug_ref, uv_ref)
            up_ref[lo + keep:lo + keep + rows, :] = uin[keep:keep + rows]
            return uin

        def gate(c, uin):
            lo, hi, keep, rows = c
            u = _conv_taps(uin, w, b, lo == 0)[2][keep:keep + rows]
            return (_silu(u[:, :CONV_TC]) * u[:, CONV_TC:]).astype(MXU_DTYPE)

        def project_down(c, act):
            dn_ref[c[0] + c[2]:c[0] + c[2] + c[3], :] += _dot(act, wd_ref[...])

        uin, act_prev = project(chunks[0]), None
        for i, c in enumerate(chunks):
            uin_next = project(chunks[i + 1]) if i + 1 < len(chunks) else None
            if act_prev is not None:
                project_down(chunks[i - 1], act_prev)
            act_prev = gate(c, uin)
            uin = uin_next
        project_down(chunks[-1], act_prev)

    return pl.pallas_call(
        body, name=name, grid=(CONV_NB,),
        in_specs=[_whole(h)] + _half_specs(CONV_TC, rows_axis=True) + _half_specs(3) + _half_specs(1)
        + [pl.BlockSpec((CONV_TC, w_down.shape[1]), lambda j: (j, 0))],
        out_specs=[pl.BlockSpec((S, w_down.shape[1]), lambda j: (0, 0)), pl.BlockSpec((S, 2 * CONV_TC), lambda j: (0, j))],
        out_shape=[jax.ShapeDtypeStruct((S, w_down.shape[1]), F32), jax.ShapeDtypeStruct((S, 2 * D_FF), F32)],
        compiler_params=_params())(h, w_up_t, w_up_t, w_conv, w_conv, b_conv, b_conv, w_down)


def ffn_bwd(name, h, up, w_up_t, w_conv, b_conv, d_dn, w_down):
    S, D = h.shape

    def body(h_ref, up_ref, ug_ref, uv_ref, wg_ref, wv_ref, bg_ref, bv_ref, dd_ref, wd_ref,
             dh_ref, gup_ref, gd_ref, dwg_ref, dwv_ref, dbg_ref, dbv_ref):
        @pl.when(pl.program_id(0) == 0)
        def _():
            dh_ref[...] = jnp.zeros_like(dh_ref)

        w = jnp.concatenate([wg_ref[...], wv_ref[...]], axis=1)
        b = jnp.concatenate([bg_ref[...], bv_ref[...]], axis=1)
        w_pair = jnp.concatenate([ug_ref[...], uv_ref[...]], axis=0)
        chunks = _conv_chunks(S, tail=True)

        def project(c):
            return up_ref[c[0]:c[1], :], _dot(dd_ref[c[0]:c[1], :], wd_ref[...], "nt")

        def through_conv(c, uin, da):
            lo, hi, keep, rows = c
            u1, u2, u = _conv_taps(uin, w, b, lo == 0)
            gate, val = u[:, :CONV_TC], u[:, CONV_TC:]
            sig = 1.0 / (1.0 + jnp.exp(-gate))
            du = jnp.concatenate([da * val * (sig * (1.0 + gate * (1.0 - sig))), da * (gate * sig)], axis=1)
            dup = w[2:3, :] * du + w[1:2, :] * _shift_up(du, 1, hi == S) + w[0:1, :] * _shift_up(du, 2, hi == S)
            kept = slice(keep, keep + rows)
            du = du[kept]
            dw = jnp.concatenate([_colsum(du * u2[kept]), _colsum(du * u1[kept]), _colsum(du * uin[kept])], axis=0)
            return dup[kept].astype(MXU_DTYPE), (gate * sig * val)[kept].astype(MXU_DTYPE), dw, _colsum(du)

        def weight_grads(c, dup, act):
            out_rows = slice(c[0] + c[2], c[0] + c[2] + c[3])
            dh_ref[out_rows, :] += _dot(dup, w_pair)
            return _dot(dup, h_ref[out_rows, :], "tn"), _dot(act, dd_ref[out_rows, :], "tn")

        dw, db, g_up, g_dn = 0.0, 0.0, 0.0, 0.0
        proj, done = project(chunks[0]), None
        for i, c in enumerate(chunks):
            proj_next = project(chunks[i + 1]) if i + 1 < len(chunks) else None
            if done is not None:
                gu, gd = weight_grads(chunks[i - 1], *done)
                g_up, g_dn = g_up + gu, g_dn + gd
            dup, act, dw_c, db_c = through_conv(c, *proj)
            dw, db, done, proj = dw + dw_c, db + db_c, (dup, act), proj_next
        gu, gd = weight_grads(chunks[-1], *done)
        g_up, g_dn = g_up + gu, g_dn + gd
        gup_ref[0], gup_ref[1] = g_up[:CONV_TC].astype(gup_ref.dtype), g_up[CONV_TC:].astype(gup_ref.dtype)
        gd_ref[...] = g_dn.astype(gd_ref.dtype)
        dwg_ref[...], dwv_ref[...] = dw[:, :CONV_TC], dw[:, CONV_TC:]
        dbg_ref[...], dbv_ref[...] = db[:, :CONV_TC], db[:, CONV_TC:]

    half = lambda rows: pl.BlockSpec((rows, CONV_TC), lambda j: (0, j))
    rows_blk = pl.BlockSpec((CONV_TC, D), lambda j: (j, 0))
    dh, gup, gd, dwg, dwv, dbg, dbv = pl.pallas_call(
        body, name=name, grid=(CONV_NB,),
        in_specs=[_whole(h), pl.BlockSpec((S, 2 * CONV_TC), lambda j: (0, j))] + _half_specs(CONV_TC, rows_axis=True) + _half_specs(3)
        + _half_specs(1) + [_whole(d_dn), rows_blk],
        out_specs=[pl.BlockSpec((S, D), lambda j: (0, 0)), pl.BlockSpec((2, CONV_TC, D), lambda j: (0, j, 0)), rows_blk,
                   half(3), half(3), half(1), half(1)],
        out_shape=[jax.ShapeDtypeStruct((S, D), F32), jax.ShapeDtypeStruct((2, D_FF, D), MXU_DTYPE),
                   jax.ShapeDtypeStruct((D_FF, D), MXU_DTYPE)]
        + [jax.ShapeDtypeStruct((3, D_FF), F32)] * 2 + [jax.ShapeDtypeStruct((1, D_FF), F32)] * 2,
        compiler_params=_params())(h, up, w_up_t, w_up_t, w_conv, w_conv, b_conv, b_conv, d_dn, w_down)
    return dh, gup.reshape(2 * D_FF, D), gd, jnp.concatenate([dwg, dwv], axis=1), jnp.concatenate([dbg, dbv], axis=1)


def adamw(name, w, parts, m, v, tr=None):
    apart = w.ndim == 3
    R, C = w.shape[0], w.shape[-1]
    tr = tr or R
    assert R % tr == 0
    c1 = 1.0 - ADAM_B1 ** ADAM_STEP
    c2 = 1.0 - ADAM_B2 ** ADAM_STEP
    np_ = len(parts)

    def body(*refs):
        w_ref, m_ref, v_ref = refs[0], refs[1 + np_], refs[2 + np_]
        go_ref, d_ref, mo_ref, vo_ref = refs[3 + np_:]
        terms = []
        for part, ref in zip(parts, refs[1:1 + np_], strict=True):
            terms += [ref[...]] if part.ndim == 2 else [ref[p] for p in range(part.shape[0])]
        g = terms[0].astype(F32)
        for term in terms[1:]:
            g = g + term.astype(F32)
        m2 = ADAM_B1 * m_ref[...] + (1.0 - ADAM_B1) * g
        v2 = ADAM_B2 * v_ref[...] + (1.0 - ADAM_B2) * (g * g)
        go_ref[...] = g
        mo_ref[...] = m2
        vo_ref[...] = v2
        d_ref[...] = -ADAM_LR * ((m2 / c1) / (jnp.sqrt(v2 / c2) + ADAM_EPS) + ADAM_WD * w_ref[...])

    blk = pl.BlockSpec((tr, C), lambda i: (i, 0))
    own = pl.BlockSpec((tr, None, C), lambda i: (i, 0, 0)) if apart else blk
    part_specs = [blk if p.ndim == 2 else pl.BlockSpec((p.shape[0], tr, C), lambda i: (0, i, 0)) for p in parts]
    return pl.pallas_call(
        body, name=name, grid=(R // tr,),
        in_specs=[own] + part_specs + [own, own], out_specs=[own] * 4,
        out_shape=[jax.ShapeDtypeStruct(w.shape, F32)] * 4, compiler_params=_params())(w, *parts, m, v)


def _place():
    return lax.axis_index("x"), lax.axis_index("y"), lax.axis_index("c")


def all_gather(name, arrs, after=None):
    n = len(arrs)
    deps = [] if after is None else [after]

    def body(*refs):
        ins, outs = refs[:n], refs[n + len(deps):2 * n + len(deps)]
        send_sems, recv_sems, local_sems = refs[2 * n + len(deps):]
        x, y, c = _place()
        me, sibling = (x, y, c), (x, y, 1 - c)
        chips = [(1 - x, y), (x, 1 - y), (1 - x, 1 - y)]
        sends = []
        for t in range(n):
            out = outs[t]

            def slot(px, py, pc, out=out):
                return out.at[4 * px + 2 * py + pc]

            def copy(k, block, to, src=None, t=t, slot=slot):
                return pltpu.make_async_remote_copy(
                    src_ref=slot(*block) if src is None else src, dst_ref=slot(*block),
                    send_sem=send_sems.at[7 * t + k], recv_sem=recv_sems.at[7 * t + k],
                    device_id=to, device_id_type=MESH_ID)

            mine = pltpu.make_async_copy(ins[t], slot(*me), local_sems.at[t])
            mine.start()
            first = [copy(0, me, sibling, src=ins[t])]
            first += [copy(1 + j, me, (*chip, c), src=ins[t]) for j, chip in enumerate(chips)]
            for cp in first:
                cp.start()
            sends.append((mine, first, copy))
        for t in range(n):
            mine, first, copy = sends[t]
            passed = [copy(4 + j, (*chip, c), sibling) for j, chip in enumerate(chips)]
            for j, chip in enumerate(chips):
                copy(1 + j, (*chip, c), me).wait_recv()
                passed[j].start()
            copy(0, sibling, me).wait_recv()
            for j, chip in enumerate(chips):
                copy(4 + j, (*chip, 1 - c), me).wait_recv()
            for cp in first + passed:
                cp.wait_send()
            mine.wait()

    any_spec = pl.BlockSpec(memory_space=pl.ANY)
    res = pl.pallas_call(
        body, name=name, in_specs=[any_spec] * (n + len(deps)), out_specs=[any_spec] * n,
        out_shape=[jax.ShapeDtypeStruct((N_DEV,) + a.shape, a.dtype) for a in arrs],
        scratch_shapes=[pltpu.SemaphoreType.DMA((7 * n,)), pltpu.SemaphoreType.DMA((7 * n,)), pltpu.SemaphoreType.DMA((n,))],
        compiler_params=pltpu.CompilerParams(has_side_effects=True))(*arrs, *deps)
    return list(res)


HBM_SPEC = pl.BlockSpec(memory_space=pltpu.HBM)
SEM_SPEC = pl.BlockSpec(memory_space=pltpu.SEMAPHORE)
DATAFLOW = pltpu.SideEffectType.DATAFLOW_SIDE_EFFECTING


def _exchange_copies(srcs, lands, send_sems, recv_sems, gather, first=0):
    x, y, c = _place()
    me = 4 * x + 2 * y + c
    out = []
    for t, (src, land) in enumerate(zip(srcs, lands, strict=True)):
        for k in range(1, N_DEV):
            px, py, pc = x ^ (k >> 2), y ^ ((k >> 1) & 1), c ^ (k & 1)
            sem = 7 * (first + t) + k - 1
            out.append(pltpu.make_async_remote_copy(
                src_ref=src if gather else src.at[4 * px + 2 * py + pc],
                dst_ref=land.at[me] if gather else land.at[k - 1],
                send_sem=send_sems.at[sem], recv_sem=recv_sems.at[sem],
                device_id=(px, py, pc), device_id_type=MESH_ID))
    return out


def exchange_start(name, arrs, gather, after=None):
    n = len(arrs)
    lands = [lax.empty(((N_DEV,) + a.shape) if gather else ((N_DEV - 1,) + a.shape[1:]), a.dtype) for a in arrs]
    deps = [] if after is None else [after]

    def body(*refs):
        srcs, land_refs = refs[:n], refs[n:2 * n]
        send_sems, recv_sems = refs[2 * n + len(deps)], refs[2 * n + len(deps) + 1]
        token = refs[-1]
        for cp in _exchange_copies(srcs, land_refs, send_sems, recv_sems, gather):
            cp.start()
        token[...] = jnp.zeros_like(token)

    hbm = lambda a: pltpu.HBM(a.shape, a.dtype)
    res = pl.pallas_call(
        body, name=name,
        out_shape=(pltpu.SemaphoreType.DMA((7 * n,)), pltpu.SemaphoreType.DMA((7 * n,)), *[hbm(a) for a in arrs],
                   *[hbm(l) for l in lands], jax.ShapeDtypeStruct((8, 128), F32)),
        in_specs=[HBM_SPEC] * (2 * n) + [pl.BlockSpec(memory_space=pl.ANY)] * len(deps),
        out_specs=(SEM_SPEC, SEM_SPEC, *[HBM_SPEC] * (2 * n), pl.BlockSpec(memory_space=pltpu.VMEM)),
        input_output_aliases={i: 2 + i for i in range(2 * n)},
        compiler_params=pltpu.CompilerParams(has_side_effects=DATAFLOW),
    )(*[pltpu.with_memory_space_constraint(a, pltpu.HBM) for a in arrs + lands], *deps)
    return res[0], res[1], list(res[2:2 + n]), list(res[2 + n:2 + 2 * n]), res[-1]


def exchange_wait(name, started, gather, after, first=0, count=None):
    send_sems, recv_sems, srcs, lands, _ = started
    count = len(srcs) - first if count is None else count
    srcs, lands = srcs[first:first + count], lands[first:first + count]
    n = len(srcs)

    def body(*refs):
        src_refs, land_refs = refs[:n], refs[n:2 * n]
        copies = _exchange_copies(src_refs, land_refs, refs[2 * n], refs[2 * n + 1], gather, first)
        for cp in copies:
            cp.wait_send()
        for cp in copies:
            cp.wait_recv()

    hbm = lambda a: pltpu.HBM(a.shape, a.dtype)
    res = pl.pallas_call(
        body, name=name, out_shape=tuple(hbm(a) for a in srcs + lands),
        in_specs=[HBM_SPEC] * (2 * n) + [SEM_SPEC, SEM_SPEC, pl.BlockSpec(memory_space=pl.ANY)],
        out_specs=tuple([HBM_SPEC] * (2 * n)), input_output_aliases={i: i for i in range(2 * n)},
        compiler_params=pltpu.CompilerParams(has_side_effects=DATAFLOW),
    )(*srcs, *lands, send_sems, recv_sems, after)
    return list(res[:n]), list(res[n:])


def _gather_cols(stack):
    p, k, n = stack.shape
    return stack.transpose(1, 0, 2).reshape(k, p * n)


def _scatter_cols(full):
    k, n = full.shape
    return full.reshape(k, N_DEV, n // N_DEV).transpose(1, 0, 2)


def _gather_rows(stack):
    p, r, n = stack.shape
    return stack.reshape(p * r, n)


def _scatter_rows(full):
    r, n = full.shape
    return full.reshape(N_DEV, r // N_DEV, n)


_IN_NAT = Q_LORA + KV_LORA
TRANSPOSED = ("w_in", "w_q_b", "w_up")
ROWS_APART = ("w_in", "w_conv")


def to_kernel_layout(name, w):
    if name == "w_in":
        z = lambda n: jnp.zeros((n, w.shape[1]), w.dtype)
        return jnp.concatenate([w[:_IN_NAT], z(KPE_LO), w[_IN_NAT:_IN_NAT + ROPE], z(LANES - KPE_LO - ROPE), w[_IN_NAT + ROPE:]], axis=0)
    if name == "w_q_b":
        return jnp.pad(w.reshape(HEADS, NOPE + ROPE, -1), ((0, 0), (0, LANES - NOPE - ROPE), (0, 0))).reshape(HEADS * LANES, -1)
    if name == "w_o":
        mla = jnp.pad(w[:HEADS * NOPE].reshape(HEADS, NOPE, -1), ((0, 0), (LANES - NOPE, 0), (0, 0))).reshape(HEADS * LANES, -1)
        return jnp.concatenate([mla, w[HEADS * NOPE:]], axis=0)
    return w


def from_kernel_layout(name, g):
    if name == "w_in":
        return jnp.concatenate([g[:_IN_NAT], g[P_KPE + KPE_LO:P_KPE + KPE_LO + ROPE], g[P_QD:]], axis=0)
    if name == "w_q_b":
        return g.reshape(HEADS, LANES, -1)[:, :NOPE + ROPE, :].reshape(HEADS * (NOPE + ROPE), -1)
    if name == "w_o":
        mla = g[:HEADS * LANES].reshape(HEADS, LANES, -1)[:, LANES - NOPE:, :].reshape(HEADS * NOPE, -1)
        return jnp.concatenate([mla, g[HEADS * LANES:]], axis=0)
    return g


SMALL_COLS = 1024
SMALL_ROWS = 24
SMALL_AT = {"loss": (0, 0, 1), "b_ada": (1, 0, 6 * D_MODEL), "g_mix_norm": (7, 0, D_MODEL), "g_q_lat": (8, 0, Q_LORA),
            "g_kv_lat": (9, 0, KV_LORA), "g_mla_q_nope": (10, 0, NOPE), "g_mla_q_pe": (10, 128, ROPE),
            "g_mla_k_nope": (10, 256, NOPE), "g_mla_k_pe": (10, 384, ROPE), "g_dil_q": (10, 512, DIL_DIM),
            "g_dil_k": (10, 640, DIL_DIM), "g_ffn_norm": (11, 0, D_MODEL), "b_conv": (12, 0, 2 * D_FF)}
SMALL_PARAMS = tuple(n for n in SMALL_AT if n != "loss")


def _pack_small(values):
    by_row = {}
    for name, (row, off, n) in SMALL_AT.items():
        by_row.setdefault(row, []).append((off, values[name].reshape(-1).astype(F32)))
    out = []
    for row in sorted(by_row):
        pieces, at = [], 0
        for off, v in sorted(by_row[row], key=lambda t: t[0]):
            pieces += [jnp.zeros((off - at,), F32), v]
            at = off + v.shape[0]
        flat = jnp.concatenate(pieces)
        nrows = -(-flat.shape[0] // SMALL_COLS)
        out.append(jnp.pad(flat, (0, nrows * SMALL_COLS - flat.shape[0])).reshape(nrows, SMALL_COLS))
    packed = jnp.concatenate(out, axis=0)
    return jnp.pad(packed, ((0, SMALL_ROWS - packed.shape[0]), (0, 0)))


def _adam(w, g, m, v):
    c1 = 1.0 - ADAM_B1 ** ADAM_STEP
    c2 = 1.0 - ADAM_B2 ** ADAM_STEP
    m2 = ADAM_B1 * m + (1.0 - ADAM_B1) * g
    v2 = ADAM_B2 * v + (1.0 - ADAM_B2) * (g * g)
    return -ADAM_LR * ((m2 / c1) / (jnp.sqrt(v2 / c2) + ADAM_EPS) + ADAM_WD * w), m2, v2


def adamw_small(name, stack, params):
    flat = [a for n in SMALL_PARAMS for a in params[n]]

    def body(stack_ref, *refs):
        ins, outs = refs[:len(flat)], refs[len(flat):]
        g_all = stack_ref[0]
        for p in range(1, N_DEV):
            g_all = g_all + stack_ref[p]
        outs[0][...] = g_all[0:1, 0:1]
        for i, pname in enumerate(SMALL_PARAMS):
            row, off, n = SMALL_AT[pname]
            w_ref, m_ref, v_ref = ins[3 * i:3 * i + 3]
            go_ref, d_ref, mo_ref, vo_ref = outs[1 + 4 * i:5 + 4 * i]
            for c0 in range(0, n, SMALL_COLS):
                cn = min(SMALL_COLS, n - c0)
                r = row + c0 // SMALL_COLS
                g = g_all[r:r + 1, off:off + cn]
                cols = (slice(None), slice(c0, c0 + cn))
                d, m2, v2 = _adam(w_ref[cols], g, m_ref[cols], v_ref[cols])
                go_ref[cols], d_ref[cols], mo_ref[cols], vo_ref[cols] = g, d, m2, v2

    whole = lambda a: pl.BlockSpec(a.shape, lambda: (0,) * a.ndim)
    out_shape = [jax.ShapeDtypeStruct((1, 1), F32)] + [jax.ShapeDtypeStruct(a.shape, F32) for n in SMALL_PARAMS for a in params[n][:1] * 4]
    res = pl.pallas_call(body, name=name, in_specs=[whole(stack)] + [whole(a) for a in flat],
                         out_specs=[pl.BlockSpec(s.shape, lambda s=s: (0,) * len(s.shape)) for s in out_shape],
                         out_shape=out_shape, compiler_params=_params())(stack, *flat)
    return res[0], {n: res[1 + 4 * i:5 + 4 * i] for i, n in enumerate(SMALL_PARAMS)}


def _local_step(x, pos, mod, target, w, fetch, emit):
    S = SEQ
    sh1, sc1, g1, sh2, sc2, g2 = [mod[:, i * D_MODEL:(i + 1) * D_MODEL] for i in range(6)]
    zeros = lambda n: jnp.zeros((1, n), F32)
    g_q = jnp.concatenate([w["g_mla_q_nope"], w["g_mla_q_pe"], zeros(LANES - NOPE - ROPE)], axis=1)
    g_k = jnp.concatenate([w["g_mla_k_nope"], zeros(LANES - NOPE)], axis=1)
    g_kpe = jnp.concatenate([zeros(KPE_LO), w["g_mla_k_pe"], zeros(LANES - KPE_LO - ROPE)], axis=1)
    g_dq = jnp.concatenate([w["g_dil_q"]] * 2, axis=1)
    g_dk = jnp.concatenate([w["g_dil_k"]] * 2, axis=1)
    b_conv = w["b_conv"]

    def inv_freq(d):
        return jnp.power(ROPE_THETA, -2.0 * jnp.arange(d // 2, dtype=F32) / d)

    f_mla = jnp.concatenate([jnp.zeros((KPE_LO,), F32), inv_freq(ROPE), inv_freq(ROPE), jnp.zeros((LANES - KPE_LO - ROPE,), F32)])
    f_dil = jnp.concatenate([inv_freq(DIL_DIM)] * 4)

    def tables_fn(rows, params):
        (p,), (fa, fb) = rows, params
        return [jnp.cos(p * fa), jnp.sin(p * fa), jnp.cos(p * fb), jnp.sin(p * fb)], []

    cos_m, sin_m, cos_d, sin_d = rowwise("rope_tables", tables_fn, [pos], [f_mla.reshape(1, LANES), f_dil.reshape(1, LANES)],
                                         [(LANES, F32)] * 4)
    tables = [cos_m, sin_m, cos_d, sin_d]
    H_M, H_D = ROPE // 2, DIL_DIM // 2

    def ln1_fn(rows, params):
        (xv,), (g, sc, sh) = rows, params
        y, _, _ = _rms(xv, g)
        return [y * (1.0 + sc) + sh], []

    (h,) = rowwise("ln1_fwd", ln1_fn, [x], [w["g_mix_norm"], sc1, sh1], [(D_MODEL, MXU_DTYPE)])
    w_in = fetch("w_in", h)
    proj = matmul("proj_fwd", h, w_in, "nt", tm=512)

    def post_fn(rows, params):
        (pv, cm, sm, cd, sd), (gq, gkv, gkp, gdq, gdk) = rows, params
        kper = _rope(_grms(pv[:, P_KPE:P_QD], gkp, KPE_GROUPS)[0], cm, sm, H_M)
        qd = [_rope(_grms(c, gdq, DIL_GROUPS)[0], cd, sd, H_D) for c in _chunks(pv[:, P_QD:P_KD])]
        kd = [_rope(_grms(c, gdk, DIL_GROUPS)[0], cd, sd, H_D) for c in _chunks(pv[:, P_KD:P_VD])]
        return [_rms(pv[:, P_QLAT:P_KVLAT], gq)[0], _rms(pv[:, P_KVLAT:P_KPE], gkv)[0], kper,
                jnp.concatenate(qd, axis=1), jnp.concatenate(kd, axis=1)], []

    post_params = [w["g_q_lat"], w["g_kv_lat"], g_kpe, g_dq, g_dk]
    qln, kvn, kper, qd_r, kd_r = rowwise(
        "proj_post", post_fn, [proj] + tables, post_params,
        [(Q_LORA, MXU_DTYPE), (KV_LORA, MXU_DTYPE), (LANES, MXU_DTYPE)] + [(DIL_WIDTH, F32)] * 2, tm=256)
    w_q_b, w_kv_b = fetch("w_q_b", qln), fetch("w_kv_b", kvn)
    q = matmul("q_fwd", qln, w_q_b, "nt", tm=1024)
    kv = matmul("kv_fwd", kvn, w_kv_b, "nn", tm=1024)

    def mla_prep_fn(rows, params):
        (qv, kvv, kp, cm, sm), (gq, gk) = rows, params
        value_lanes = _lane(kp.shape) >= NOPE
        qs, ks, vs = [], [], []
        for qc, kc in zip(_chunks(qv), _chunks(kvv), strict=True):
            qs.append(_rope(_grms(qc, gq, Q_GROUPS)[0], cm, sm, H_M))
            ks.append(_grms(kc, gk, K_GROUPS)[0] + kp)
            vs.append(jnp.where(value_lanes, kc, 0.0))
        return [jnp.concatenate(t, axis=1) for t in (qs, ks, vs)], []

    q_mla, k_mla, v_mla = rowwise("mla_prep", mla_prep_fn, [q, kv, kper, cos_m, sin_m], [g_q, g_k],
                                  [(HEADS * LANES, MXU_DTYPE)] * 3, tm=256)
    mla_scale = (NOPE + ROPE) ** -0.5
    o_cat, lse_mla = mla_fwd("mla_fwd", q_mla, k_mla, v_mla, mla_scale)

    band = [band_fwd(f"band{dil}_fwd", qd_r, kd_r, proj, dil) for dil in DILATIONS]
    o_cat, lse_mix = combine_fwd("dil_combine", [b[0] for b in band], [b[1] for b in band], o_cat)
    w_o = fetch("w_o", o_cat)
    mix = matmul("mix_fwd", o_cat, w_o, "nn", tm=512)

    def mid_fn(rows, params):
        (xv, mx), (gate1, g, sc, sh) = rows, params
        x1 = xv + gate1 * mx
        y, _, _ = _rms(x1, g)
        return [x1, y * (1.0 + sc) + sh], []

    x1, h2 = rowwise("mid_fwd", mid_fn, [x, mix], [g1, w["g_ffn_norm"], sc2, sh2], [(D_MODEL, F32), (D_MODEL, MXU_DTYPE)])
    w_up, w_conv, w_down = fetch("w_up", h2), fetch("w_conv", h2), fetch("w_down", h2)
    dn, up = ffn_fwd("ffn_fwd", h2, w_up, w_conv, b_conv, w_down)

    def final_fn(rows, params):
        (x1v, dnv, tgt), (gate2,) = rows, params
        r = x1v + gate2 * dnv - tgt
        dy = r * (1.0 / D_MODEL)
        loss = jnp.sum(_colsum(r * r), axis=-1, keepdims=True) * (0.5 / D_MODEL)
        return [dy, gate2 * dy], [loss, _colsum(dy * dnv)]

    dy, d_dn, loss, dg2 = rowwise("loss_head", final_fn, [x1, dn, target], [g2], [(D_MODEL, F32), (D_MODEL, MXU_DTYPE)],
                                  [1, D_MODEL])
    dh2, g_up, g_down, g_w_conv, g_b_conv = ffn_bwd("ffn_bwd", h2, up, w_up, w_conv, b_conv, d_dn, w_down)
    emit("w_down", g_down)
    emit("w_conv", g_w_conv)
    sent = emit("w_up", g_up)

    def mid_bwd_fn(rows, params):
        (dh2v, dyv, x1v, mx), (gate1, g, sc) = rows, params
        yn, n, rstd = _rms(x1v, g)
        dx_n, dg = _rms_bwd(dh2v * (1.0 + sc), n, rstd, g)
        dx1 = dyv + dx_n
        return [dx1, gate1 * dx1], [dg, _colsum(dh2v * yn), _colsum(dh2v), _colsum(dx1 * mx)]

    dx1, dmix, dg_ffn, dsc2, dsh2, dg1 = rowwise(
        "mid_bwd", mid_bwd_fn, [dh2, dy, x1, mix], [g1, w["g_ffn_norm"], sc2], [(D_MODEL, F32), (D_MODEL, MXU_DTYPE)],
        [D_MODEL] * 4, dep=sent)

    sent = emit("w_o", matmul("mix_wgrad", o_cat, dmix, "tn", tm=512, out_dtype=MXU_DTYPE))
    do_cat = matmul("mix_dgrad", dmix, w_o, "nt", tm=512, dep=sent)
    dband = [band_bwd(f"band{dil}_bwd", qd_r, kd_r, proj, b[1], lse_mix, o_cat, do_cat, dil) for dil, b in zip(DILATIONS, band)]
    dq_mla, dkv_mla, dkper = mla_bwd("mla_bwd", q_mla, k_mla, v_mla, o_cat, do_cat, lse_mla, mla_scale)

    def mla_prep_bwd_fn(rows, params):
        (dqv, dkvv, qv, kvv, cm, sm), (gq, gk) = rows, params
        nope_lanes = _lane(cm.shape) < NOPE
        dqs, dkvs, dgq, dgk = [], [], 0.0, 0.0
        for dqc, dkc, qc, kc in zip(_chunks(dqv), _chunks(dkvv), _chunks(qv), _chunks(kvv), strict=True):
            _, n, rstd = _grms(qc, gq, Q_GROUPS)
            dx, dg = _grms_bwd(_rope_bwd(dqc, cm, sm, H_M), n, rstd, gq, Q_GROUPS)
            dqs.append(dx)
            dgq = dgq + dg
            _, n, rstd = _grms(kc, gk, K_GROUPS)
            dx, dg = _grms_bwd(dkc, n, rstd, gk, K_GROUPS)
            dkvs.append(jnp.where(nope_lanes, dx, dkc))
            dgk = dgk + dg
        return [jnp.concatenate(dqs, axis=1), jnp.concatenate(dkvs, axis=1)], [dgq, dgk]

    dq, dkv, dg_q, dg_k = rowwise("mla_prep_bwd", mla_prep_bwd_fn, [dq_mla, dkv_mla, q, kv, cos_m, sin_m], [g_q, g_k],
                                  [(HEADS * LANES, MXU_DTYPE)] * 2, [LANES, LANES], tm=256)
    emit("w_q_b", matmul("q_wgrad", dq, qln, "tn", out_dtype=MXU_DTYPE))
    emit("w_kv_b", matmul("kv_wgrad", kvn, dkv, "tn", out_dtype=MXU_DTYPE))
    dqln = matmul("q_dgrad", dq, w_q_b, "nn", tm=1024)
    dkvn = matmul("kv_dgrad", dkv, w_kv_b, "nt", tm=1024)

    def pre_bwd_fn(rows, params):
        dql, dkvl, dkp = rows[0:3]
        dqd_, dkd_, dvd_ = [rows[3 + 3 * i] + rows[4 + 3 * i] + rows[5 + 3 * i] for i in range(3)]
        pv, cm, sm, cd, sd = rows[12:]
        gq, gkv, gkp, gdq, gdk = params
        r_q = _norm_bwd(dql, pv[:, P_QLAT:P_KVLAT], gq)
        r_kv = _norm_bwd(dkvl, pv[:, P_KVLAT:P_KPE], gkv)
        _, n, rstd = _grms(pv[:, P_KPE:P_QD], gkp, KPE_GROUPS)
        r_kp = _grms_bwd(_rope_bwd(dkp, cm, sm, H_M), n, rstd, gkp, KPE_GROUPS)
        outs, dgs = [r_q[0], r_kv[0], r_kp[0]], []
        for dval, lo, g in ((dqd_, P_QD, gdq), (dkd_, P_KD, gdk)):
            dg_sum = 0.0
            for dc, xc in zip(_chunks(dval), _chunks(pv[:, lo:lo + DIL_WIDTH]), strict=True):
                _, n, rstd = _grms(xc, g, DIL_GROUPS)
                dx, dg = _grms_bwd(_rope_bwd(dc, cd, sd, H_D), n, rstd, g, DIL_GROUPS)
                outs.append(dx)
                dg_sum = dg_sum + dg
            dgs.append(dg_sum)
        return [jnp.concatenate(outs + [dvd_], axis=1)], [r_q[1], r_kv[1], r_kp[1]] + dgs

    dproj, dg_q_lat, dg_kv_lat, dg_kpe, dg_dq, dg_dk = rowwise(
        "proj_pre_bwd", pre_bwd_fn,
        [dqln, dkvn, dkper] + [d[i] for i in range(3) for d in dband] + [proj] + tables, post_params,
        [(P_END, MXU_DTYPE)], [Q_LORA, KV_LORA, LANES, LANES, LANES], tm=256)
    sent = emit("w_in", matmul("proj_wgrad", dproj, h, "tn", tn=512, out_dtype=MXU_DTYPE))
    dh = matmul("proj_dgrad", dproj, w_in, "nn", tm=512, dep=sent)

    def ln1_bwd_fn(rows, params):
        (dhv, dres, xv), (g, sc) = rows, params
        yn, n, rstd = _rms(xv, g)
        dx_n, dg = _rms_bwd(dhv * (1.0 + sc), n, rstd, g)
        return [dres + dx_n], [dg, _colsum(dhv * yn), _colsum(dhv)]

    grad_x, dg_mix, dsc1, dsh1 = rowwise("ln1_bwd", ln1_bwd_fn, [dh, dx1, x], [w["g_mix_norm"], sc1], [(D_MODEL, F32)],
                                         [D_MODEL] * 3)
    dmod = jnp.concatenate([dsh1, dsc1, dg1, dsh2, dsc2, dg2], axis=-1)
    small = {"loss": loss, "b_ada": dmod, "g_mix_norm": dg_mix, "g_q_lat": dg_q_lat, "g_kv_lat": dg_kv_lat,
             "g_mla_q_nope": dg_q[:, :NOPE], "g_mla_q_pe": dg_q[:, NOPE:NOPE + ROPE], "g_mla_k_nope": dg_k[:, :NOPE],
             "g_mla_k_pe": dg_kpe[:, KPE_LO:KPE_LO + ROPE], "g_dil_q": dg_dq[:, :DIL_DIM] + dg_dq[:, DIL_DIM:],
             "g_dil_k": dg_dk[:, :DIL_DIM] + dg_dk[:, DIL_DIM:], "g_ffn_norm": dg_ffn,
             "b_conv": g_b_conv}
    return grad_x, small


COL_SHARDED = ("w_kv_b", "w_conv")
ROW_SHARDED = ("w_o", "w_down") + TRANSPOSED
ADAM_TILE = {"w_ada": 256, "w_up": 176, "w_down": 176}
GATHER_GROUPS = (("w_in",), ("w_q_b", "w_kv_b"), ("w_o",), ("w_up", "w_conv", "w_down"))
SCATTER_GROUPS = (("w_down", "w_conv", "w_up"), ("w_o",), ("w_q_b", "w_kv_b", "w_in"))
OUT_WEIGHTS = ("w_ada", "b_ada", "g_mix_norm", "w_in", "g_q_lat", "w_q_b", "g_kv_lat", "w_kv_b", "g_mla_q_nope", "g_mla_q_pe",
               "g_mla_k_nope", "g_mla_k_pe", "g_dil_q", "g_dil_k", "w_o", "g_ffn_norm", "w_up", "w_conv", "b_conv", "w_down")


def kernel(x, c, positions, w_ada, b_ada, g_mix_norm, w_in, g_q_lat, w_q_b, g_kv_lat, w_kv_b, g_mla_q_nope, g_mla_q_pe, g_mla_k_nope, g_mla_k_pe, g_dil_q, g_dil_k, w_o, g_ffn_norm, w_up, w_conv, b_conv, w_down, loss_target, m_w_ada, m_b_ada, m_g_mix_norm, m_w_in, m_g_q_lat, m_w_q_b, m_g_kv_lat, m_w_kv_b, m_g_mla_q_nope, m_g_mla_q_pe, m_g_mla_k_nope, m_g_mla_k_pe, m_g_dil_q, m_g_dil_k, m_w_o, m_g_ffn_norm, m_w_up, m_w_conv, m_b_conv, m_w_down, v_w_ada, v_b_ada, v_g_mix_norm, v_w_in, v_g_q_lat, v_w_q_b, v_g_kv_lat, v_w_kv_b, v_g_mla_q_nope, v_g_mla_q_pe, v_g_mla_k_nope, v_g_mla_k_pe, v_g_dil_q, v_g_dil_k, v_w_o, v_g_ffn_norm, v_w_up, v_w_conv, v_b_conv, v_w_down):
    args = dict(locals())
    xi, yi, ci = _place()
    me = 4 * xi + 2 * yi + ci
    def local(prefix, n):
        a = args[prefix + n]
        if n in ROWS_APART:
            return jnp.transpose(a, (2, 0, 1) if n in TRANSPOSED else (1, 0, 2))
        return a[0].T if n in TRANSPOSED else a[0]

    def as_output(n, r):
        if n in ROWS_APART:
            return jnp.transpose(r, (1, 2, 0) if n in TRANSPOSED else (1, 0, 2))
        return (r.T if n in TRANSPOSED else r)[None]

    shard = {n: local("", n) for n in COL_SHARDED + ROW_SHARDED + ("w_ada",)}
    flat = lambda n, a: a.reshape(a.shape[0], a.shape[-1]) if n in ROWS_APART else a
    small_w = {n: args[n] for n in SMALL_PARAMS}

    (c_all,) = all_gather("gather_c", [c])
    (sc_all,) = rowwise("silu_c", lambda rows, params: ([_silu(rows[0])], []), [c_all.reshape(N_DEV, D_MODEL)], [],
                        [(D_MODEL, MXU_DTYPE)])
    mod_part = matmul("ada_fwd", sc_all, shard["w_ada"], "nn")
    (mod_all,) = all_gather("gather_mod", [mod_part])

    payload = {n: flat(n, shard[n]) if n == "w_conv" else flat(n, shard[n]).astype(MXU_DTYPE) for n in COL_SHARDED + ROW_SHARDED}
    gather_order = [n for grp in GATHER_GROUPS for n in grp]
    gathered = exchange_start("gather_start", [payload[n] for n in gather_order], gather=True, after=mod_all)
    after_start = gathered[-1]
    full = {}

    def fetch(name, after):
        if name not in full:
            (i, grp), = [(i, grp) for i, grp in enumerate(GATHER_GROUPS) if name in grp]
            srcs, lands = exchange_wait(f"gather{i}_wait", gathered, True, after, gather_order.index(grp[0]), len(grp))
            for n, src, land in zip(grp, srcs, lands, strict=True):
                stack = lax.dynamic_update_index_in_dim(land, src, me, 0)
                full[n] = to_kernel_layout(n, _gather_cols(stack) if n in COL_SHARDED else _gather_rows(stack))
        return full[name]

    mod_row = lax.dynamic_index_in_dim(mod_all, me, axis=1, keepdims=False).reshape(1, 6 * D_MODEL)
    (mod,) = rowwise("ada_bias", lambda rows, params: ([rows[0] + rows[1]], []), [mod_row, b_ada], [], [(6 * D_MODEL, F32)],
                     dep=after_start)

    own, pending, scatters = {}, {}, {}

    def emit(name, grad):
        grad = from_kernel_layout(name, grad)
        parts = _scatter_cols(grad) if name in COL_SHARDED else _scatter_rows(grad)
        own[name] = lax.dynamic_index_in_dim(parts, me, 0, keepdims=False)
        pending[name] = parts
        for i, grp in enumerate(SCATTER_GROUPS):
            if name == grp[-1]:
                scatters[i] = exchange_start(f"scatter{i}_start", [pending[n] for n in grp], gather=False)
                return scatters[i][-1]
        return None

    pos = positions.reshape(SEQ, 1).astype(F32)
    grad_x, small = _local_step(x[0], pos, mod, loss_target[0], small_w, fetch, emit)

    res, done = {}, grad_x
    for i, grp in enumerate(SCATTER_GROUPS):
        _, lands = exchange_wait(f"scatter{i}_wait", scatters[i], False, done)
        for n, land in zip(grp, lands, strict=True):
            res[n] = adamw(f"adamw_{n}", shard[n], [own[n], land], local("m_", n), local("v_", n), ADAM_TILE.get(n))
            done = res[n][0]
            res[n] = [as_output(n, r) for r in res[n]]
    (small_all,) = all_gather("gather_small", [_pack_small(small)], after=done)
    loss, small_res = adamw_small("adamw_small", small_all, {n: (args[n], args["m_" + n], args["v_" + n]) for n in SMALL_PARAMS})
    row, _, n_mod = SMALL_AT["b_ada"]
    dmod_all = small_all[:, row:row + n_mod // SMALL_COLS, :].reshape(N_DEV, n_mod)
    dmod_mine = lax.dynamic_slice_in_dim(dmod_all, me * (6 * D_MODEL // N_DEV), 6 * D_MODEL // N_DEV, axis=1)
    g_w_ada = matmul("ada_wgrad", sc_all, dmod_mine, "tn")
    res["w_ada"] = [r[None] for r in adamw("adamw_w_ada", shard["w_ada"], [g_w_ada], m_w_ada[0], v_w_ada[0], ADAM_TILE["w_ada"])]

    def leaf(kind, n):
        return res[n][kind] if n in res else small_res[n][kind]

    return (loss.reshape(()), grad_x[None], *[leaf(k, n) for k in range(4) for n in OUT_WEIGHTS])
```

```python
import jax
import jax.numpy as jnp
from jax import lax
from jax.experimental import pallas as pl
from jax.experimental.pallas import tpu as pltpu

F32 = jnp.float32
MXU_DTYPE = jnp.bfloat16

N_DEV = 8
D_MODEL = 1024
SEQ = 2048
HEADS = 8
NOPE = 64
ROPE = 32
Q_LORA = 512
KV_LORA = 256
DIL_DIM = 64
DIL_WIDTH = HEADS * DIL_DIM
DILATIONS = (1, 4, 16)
SPAN = 128
D_FF = 2816
LANES = 128
ROPE_THETA = 10000.0
EPS = 1e-6
NEG_INF = -1e30
ADAM_LR, ADAM_B1, ADAM_B2, ADAM_EPS, ADAM_WD, ADAM_STEP = 0.001, 0.9, 0.999, 1e-08, 0.01, 10
VMEM_LIMIT = 56 * 1024 * 1024
MESH_ID = pl.DeviceIdType.MESH

P_QLAT, P_KVLAT, P_KPE, P_QD, P_KD, P_VD, P_END = 0, 512, 768, 896, 1408, 1920, 2432
KPE_LO = 64
MIX_IN = HEADS * LANES + DIL_WIDTH


def _params(**kw):
    return pltpu.CompilerParams(vmem_limit_bytes=VMEM_LIMIT, **kw)


def rowwise(name, fn, rows, params, out_rows, out_accs=(), tm=512, dep=None):
    deps = [] if dep is None else [dep]
    rows = [r if isinstance(r, tuple) else (r, r.shape[1], 0) for r in rows]
    R = rows[0][0].shape[0]
    tm = min(tm, R)
    steps = R // tm
    assert steps * tm == R
    in_specs = []
    for a, width, cb in rows:
        ri = a.shape[0]
        per = ri // tm
        assert per * tm == ri
        if ri == R:
            in_specs.append(pl.BlockSpec((tm, width), lambda i, cb=cb: (i, cb)))
        else:
            in_specs.append(pl.BlockSpec((tm, width), lambda i, per=per, cb=cb: (i % per, cb)))
    for p in params:
        in_specs.append(pl.BlockSpec(p.shape, lambda i: (0,) * p.ndim))
    in_specs += [pl.BlockSpec(memory_space=pl.ANY)] * len(deps)
    out_shape = [jax.ShapeDtypeStruct((R, d), dt) for d, dt in out_rows]
    out_specs = [pl.BlockSpec((tm, d), lambda i: (i, 0)) for d, _ in out_rows]
    out_shape += [jax.ShapeDtypeStruct((1, n), F32) for n in out_accs]
    out_specs += [pl.BlockSpec((1, n), lambda i: (0, 0)) for n in out_accs]
    nr, npar, no, na = len(rows), len(params), len(out_rows), len(out_accs)

    def body(*refs):
        rvals = [r[...] for r in refs[:nr]]
        pvals = [r[...] for r in refs[nr:nr + npar]]
        outs, accs = fn(rvals, pvals)
        first_out = nr + npar + len(deps)
        for ref, v in zip(refs[first_out:first_out + no], outs, strict=True):
            ref[...] = v.astype(ref.dtype)
        if na:
            acc_refs = refs[first_out + no:]
            i = pl.program_id(0)

            @pl.when(i == 0)
            def _():
                for ref, v in zip(acc_refs, accs, strict=True):
                    ref[...] = v

            @pl.when(i > 0)
            def _():
                for ref, v in zip(acc_refs, accs, strict=True):
                    ref[...] += v

    res = pl.pallas_call(body, name=name, grid=(steps,), in_specs=in_specs, out_specs=out_specs,
                         out_shape=out_shape, compiler_params=_params())(*[r[0] for r in rows], *params, *deps)
    return list(res)


_DIMS = {"nn": ((1,), (0,)), "nt": ((1,), (1,)), "tn": ((0,), (0,))}


def _dot(a, b, mode="nn"):
    return lax.dot_general(a.astype(MXU_DTYPE), b.astype(MXU_DTYPE), (_DIMS[mode], ((), ())),
                           preferred_element_type=F32)


def matmul(name, a, b, mode, tm=None, tn=None, tk=None, out_dtype=F32, dep=None):
    if mode == "tn":
        K, M = a.shape
    else:
        M, K = a.shape
    N = b.shape[0] if mode == "nt" else b.shape[1]
    tm, tn, tk = tm or M, tn or N, tk or K
    nm, nn, nk = M // tm, N // tn, K // tk
    assert nm * tm == M and nn * tn == N and nk * tk == K
    a_spec = pl.BlockSpec((tk, tm), lambda i, j, k: (k, i)) if mode == "tn" else pl.BlockSpec((tm, tk), lambda i, j, k: (i, k))
    b_spec = pl.BlockSpec((tn, tk), lambda i, j, k: (j, k)) if mode == "nt" else pl.BlockSpec((tk, tn), lambda i, j, k: (k, j))
    deps = [] if dep is None else [dep]

    def body(a_ref, b_ref, *rest):
        o_ref, scratch = rest[len(deps)], rest[len(deps) + 1:]
        p = _dot(a_ref[...], b_ref[...], mode)
        if nk == 1:
            o_ref[...] = p.astype(o_ref.dtype)
        else:
            acc = scratch[0]
            k = pl.program_id(2)

            @pl.when(k == 0)
            def _():
                acc[...] = p

            @pl.when(k > 0)
            def _():
                acc[...] += p

            @pl.when(k == nk - 1)
            def _():
                o_ref[...] = acc[...].astype(o_ref.dtype)

    return pl.pallas_call(
        body, name=name, grid=(nm, nn, nk), in_specs=[a_spec, b_spec] + [pl.BlockSpec(memory_space=pl.ANY)] * len(deps),
        out_specs=pl.BlockSpec((tm, tn), lambda i, j, k: (i, j)),
        out_shape=jax.ShapeDtypeStruct((M, N), out_dtype),
        scratch_shapes=[pltpu.VMEM((tm, tn), F32)] if nk > 1 else [],
        compiler_params=_params())(a, b, *deps)


def _rms(x, g):
    rstd = lax.rsqrt(jnp.mean(x * x, axis=-1, keepdims=True) + EPS)
    n = x * rstd
    return n * g, n, rstd


def _rms_bwd(dy, n, rstd, g):
    dg = jnp.sum(dy * n, axis=0, keepdims=True)
    dn = dy * g
    dx = rstd * (dn - n * jnp.mean(dn * n, axis=-1, keepdims=True))
    return dx, dg


def _norm_bwd(dy, x, g):
    _, n, rstd = _rms(x, g)
    return _rms_bwd(dy, n, rstd, g)


def _colsum(v):
    return jnp.sum(v, axis=0, keepdims=True)


def _silu(x):
    return x * (1.0 / (1.0 + jnp.exp(-x)))


def _lane(shape):
    return lax.broadcasted_iota(jnp.int32, shape, 1)


def _group_mean(v, groups):
    i = lax.broadcasted_iota(jnp.int32, (LANES, LANES), 0)
    j = lax.broadcasted_iota(jnp.int32, (LANES, LANES), 1)
    g = jnp.zeros((LANES, LANES), F32)
    for lo, hi in groups:
        g = jnp.where((i >= lo) & (i < hi) & (j >= lo) & (j < hi), 1.0 / (hi - lo), g)
    head = v.astype(MXU_DTYPE)
    return _dot(head, g) + _dot(v - head.astype(F32), g)


def _in_groups(shape, groups):
    lane = _lane(shape)
    m = jnp.zeros(shape, jnp.bool_)
    for lo, hi in groups:
        m = m | ((lane >= lo) & (lane < hi))
    return m


def _grms(x, g, groups):
    rstd = lax.rsqrt(_group_mean(x * x, groups) + EPS)
    n = jnp.where(_in_groups(x.shape, groups), x * rstd, 0.0)
    return n * g, n, rstd


def _grms_bwd(dy, n, rstd, g, groups):
    dn = dy * g
    return rstd * (dn - n * _group_mean(dn * n, groups)), _colsum(dy * n)


def _rot(x, half, transpose=False):
    first = (_lane(x.shape) % (2 * half)) < half
    up = pltpu.roll(x, LANES - half, axis=1)
    down = pltpu.roll(x, half, axis=1)
    return jnp.where(first, up, -down) if transpose else jnp.where(first, -up, down)


def _rope(x, cos, sin, half):
    return x * cos + _rot(x, half) * sin


def _rope_bwd(dy, cos, sin, half):
    return dy * cos + _rot(dy * sin, half, transpose=True)


def _chunks(x):
    return [x[:, i:i + LANES] for i in range(0, x.shape[1], LANES)]


Q_GROUPS = ((0, NOPE), (NOPE, NOPE + ROPE))
K_GROUPS = ((0, NOPE),)
KPE_GROUPS = ((KPE_LO, KPE_LO + ROPE),)
DIL_GROUPS = ((0, DIL_DIM), (DIL_DIM, 2 * DIL_DIM))


def _col(width, rows=SEQ):
    return pl.BlockSpec((rows, width), lambda h: (0, h))


def _causal_tail(s, tq, fill):
    diag = s[:, s.shape[1] - tq:]
    keep = lax.broadcasted_iota(jnp.int32, diag.shape, 1) <= lax.broadcasted_iota(jnp.int32, diag.shape, 0)
    diag = jnp.where(keep, diag, fill)
    return diag if s.shape[1] == tq else jnp.concatenate([s[:, :s.shape[1] - tq], diag], axis=1)


def mla_fwd(name, q, k, v, scale, tq=256):
    S = q.shape[0]

    def body(q_ref, k_ref, v_ref, o_ref, lse_ref):
        nb = S // tq
        blk = lambda i: slice(i * tq, (i + 1) * tq)

        def scores(i):
            return _dot(q_ref[blk(i), :], k_ref[:(i + 1) * tq, :], "nt")

        def softmax(i, s):
            s = _causal_tail(s * scale, tq, NEG_INF)
            m = jnp.max(s, axis=-1, keepdims=True)
            e = jnp.exp(s - m)
            l = jnp.sum(e, axis=-1, keepdims=True)
            lse_ref[0, blk(i), :] = m + jnp.log(l)
            return (e * (1.0 / l)).astype(MXU_DTYPE)

        def weighted(i, p):
            o_ref[blk(i), :] = _dot(p, v_ref[:(i + 1) * tq, :])

        s, p_prev = scores(0), None
        for i in range(nb):
            s_next = scores(i + 1) if i + 1 < nb else None
            if p_prev is not None:
                weighted(i - 1, p_prev)
            p_prev, s = softmax(i, s), s_next
        weighted(nb - 1, p_prev)

    return pl.pallas_call(
        body, name=name, grid=(HEADS,), in_specs=[_col(LANES)] * 3,
        out_specs=[_col(LANES), pl.BlockSpec((1, S, 1), lambda h: (h, 0, 0))],
        out_shape=[jax.ShapeDtypeStruct((S, MIX_IN), F32), jax.ShapeDtypeStruct((HEADS, S, 1), F32)],
        compiler_params=_params())(q, k, v)


def mla_bwd(name, q, k, v, o, do, lse, scale, tq=256):
    S = q.shape[0]

    def body(q_ref, k_ref, v_ref, o_ref, do_ref, lse_ref, dq_ref, dkv_ref, dkpe_ref, dk_acc, dv_acc):
        dk_acc[...] = jnp.zeros_like(dk_acc)
        dv_acc[...] = jnp.zeros_like(dv_acc)
        for i in range(S // tq):
            kext = (i + 1) * tq
            blk = slice(i * tq, kext)
            qi, kk, vv = q_ref[blk, :], k_ref[:kext, :], v_ref[:kext, :]
            doi = do_ref[blk, :]
            s = _causal_tail(_dot(qi, kk, "nt") * scale, tq, NEG_INF)
            p = jnp.exp(s - lse_ref[0, blk, :])
            dp = _dot(doi, vv, "nt")
            delta = jnp.sum(doi * o_ref[blk, :], axis=-1, keepdims=True)
            ds = p * (dp - delta) * scale
            dq_ref[blk, :] = _dot(ds, kk)
            dk_acc[:kext, :] += _dot(ds, qi, "tn")
            dv_acc[:kext, :] += _dot(p, doi, "tn")
        dk = dk_acc[...]
        lane = _lane(dk.shape)
        dkv_ref[...] = jnp.where(lane < NOPE, dk, 0.0) + dv_acc[...]
        dkpe = jnp.where((lane >= KPE_LO) & (lane < KPE_LO + ROPE), dk, 0.0)
        h = pl.program_id(0)

        @pl.when(h == 0)
        def _():
            dkpe_ref[...] = dkpe

        @pl.when(h > 0)
        def _():
            dkpe_ref[...] += dkpe

    return pl.pallas_call(
        body, name=name, grid=(HEADS,),
        in_specs=[_col(LANES)] * 5 + [pl.BlockSpec((1, S, 1), lambda h: (h, 0, 0))],
        out_specs=[_col(LANES), _col(LANES), pl.BlockSpec((S, LANES), lambda h: (0, 0))],
        out_shape=[jax.ShapeDtypeStruct((S, HEADS * LANES), F32), jax.ShapeDtypeStruct((S, HEADS * LANES), F32),
                   jax.ShapeDtypeStruct((S, LANES), F32)],
        scratch_shapes=[pltpu.VMEM((S, LANES), F32), pltpu.VMEM((S, LANES), F32)],
        compiler_params=_params())(q, k, v, o, do, lse)


BAND_TQ = SPAN


def _band_blocks(L, tq):
    return [(i * tq, (i + 1) * tq, max(0, i * tq - SPAN)) for i in range(L // tq)]


def _class_rows(r, dil, lo, hi):
    return pl.ds(r + dil * lo, hi - lo, stride=dil) if dil > 1 else pl.ds(lo, hi - lo)


def _stack_heads(t, lo):
    zero = jnp.zeros_like(t)
    return jnp.concatenate([jnp.where(lo, t, zero), jnp.where(lo, zero, t)], axis=0)


def _band_mask2(q0, q1, k0):
    n = q1 - q0
    shape = (2 * n, q1 - k0)
    i = lax.broadcasted_iota(jnp.int32, shape, 0)
    dist = (jnp.where(i >= n, i - n, i) + q0) - (lax.broadcasted_iota(jnp.int32, shape, 1) + k0)
    return (dist >= 0) & (dist <= SPAN)


def _pair_col(col0=0):
    return pl.BlockSpec((SEQ, LANES), lambda j: (0, col0 // LANES + j))


def band_fwd(name, q, k, v, dil):
    S = q.shape[0]
    L = S // dil
    tq = BAND_TQ
    scale = DIL_DIM ** -0.5

    def body(q_ref, k_ref, v_ref, o_ref, lse_ref):
        items = [(r, blk) for r in range(dil) for blk in _band_blocks(L, tq)]
        lo = _lane((tq, LANES)) < DIL_DIM

        def scores(item):
            r, (q0, q1, k0) = item
            qb = q_ref[_class_rows(r, dil, q0, q1), :].astype(MXU_DTYPE)
            return _dot(_stack_heads(qb, lo), k_ref[_class_rows(r, dil, k0, q1), :], "nt")

        def softmax(item, s):
            _, (q0, q1, k0) = item
            s = jnp.where(_band_mask2(q0, q1, k0), s * scale, NEG_INF)
            mx = jnp.max(s, axis=-1, keepdims=True)
            e = jnp.exp(s - mx)
            l = jnp.sum(e, axis=-1, keepdims=True)
            return (e * (1.0 / l)).astype(MXU_DTYPE), mx + jnp.log(l)

        def weighted(item, p, lse):
            r, (q0, q1, k0) = item
            pv = _dot(p, v_ref[_class_rows(r, dil, k0, q1), :])
            o_ref[_class_rows(r, dil, q0, q1), :] = jnp.where(lo, pv[:tq], pv[tq:])
            lse_ref[_class_rows(r, dil, q0, q1), :] = jnp.where(lo, lse[:tq], lse[tq:])

        s, prev = scores(items[0]), None
        for i, item in enumerate(items):
            s_next = scores(items[i + 1]) if i + 1 < len(items) else None
            if prev is not None:
                weighted(items[i - 1], *prev)
            prev, s = softmax(item, s), s_next
        weighted(items[-1], *prev)

    return pl.pallas_call(
        body, name=name, grid=(DIL_WIDTH // LANES,), in_specs=[_pair_col()] * 2 + [_pair_col(P_VD)], out_specs=[_pair_col()] * 2,
        out_shape=[jax.ShapeDtypeStruct((S, DIL_WIDTH), F32)] * 2, compiler_params=_params())(q, k, v)


def band_bwd(name, q, k, v, lse, lse_mix, o_cat, do_cat, dil):
    S = q.shape[0]
    L = S // dil
    tq = BAND_TQ
    scale = DIL_DIM ** -0.5

    def body(q_ref, k_ref, v_ref, lse_ref, mix_ref, o_ref, do_ref, dq_ref, dk_ref, dv_ref):
        dk_ref[...] = jnp.zeros_like(dk_ref)
        dv_ref[...] = jnp.zeros_like(dv_ref)
        items = [(r, blk) for r in range(dil) for blk in _band_blocks(L, tq)]
        lo = _lane((tq, LANES)) < DIL_DIM
        per_head = lambda t: jnp.concatenate([t[:, 0:1], t[:, DIL_DIM:DIL_DIM + 1]], axis=0)

        def scores(item):
            r, (q0, q1, k0) = item
            qrows, krows = _class_rows(r, dil, q0, q1), _class_rows(r, dil, k0, q1)
            lse_p, dout = lse_ref[qrows, :], do_ref[qrows, :]
            w2 = per_head(jnp.exp(lse_p - mix_ref[qrows, :]))
            dd = dout * o_ref[qrows, :]
            big_d = jnp.concatenate([jnp.sum(jnp.where(lo, dd, 0.0), axis=-1, keepdims=True),
                                     jnp.sum(jnp.where(lo, 0.0, dd), axis=-1, keepdims=True)], axis=0)
            q2 = _stack_heads(q_ref[qrows, :].astype(MXU_DTYPE), lo)
            dom = (_stack_heads(dout, lo) * w2).astype(MXU_DTYPE)
            return (_dot(q2, k_ref[krows, :], "nt"), _dot(dom, v_ref[krows, :], "nt"), per_head(lse_p), w2 * big_d, q2, dom)

        def softmax_bwd(item, s, dp, lse2, wd2, q2, dom):
            _, (q0, q1, k0) = item
            p = jnp.where(_band_mask2(q0, q1, k0), jnp.exp(s * scale - lse2), 0.0)
            return p.astype(MXU_DTYPE), (p * (dp - wd2) * scale).astype(MXU_DTYPE), q2, dom

        def grads(item, p, ds, q2, dom):
            r, (q0, q1, k0) = item
            qrows, krows = _class_rows(r, dil, q0, q1), _class_rows(r, dil, k0, q1)
            dq2 = _dot(ds, k_ref[krows, :])
            dq_ref[qrows, :] = jnp.where(lo, dq2[:tq], dq2[tq:])
            dk_ref[krows, :] += _dot(ds, q2, "tn")
            dv_ref[krows, :] += _dot(p, dom, "tn")

        sc, prev = scores(items[0]), None
        for i, item in enumerate(items):
            sc_next = scores(items[i + 1]) if i + 1 < len(items) else None
            if prev is not None:
                grads(items[i - 1], *prev)
            prev, sc = softmax_bwd(item, *sc), sc_next
        grads(items[-1], *prev)

    cat = _pair_col(HEADS * LANES)
    return pl.pallas_call(
        body, name=name, grid=(DIL_WIDTH // LANES,),
        in_specs=[_pair_col()] * 2 + [_pair_col(P_VD)] + [_pair_col()] * 2 + [cat] * 2, out_specs=[_pair_col()] * 3,
        out_shape=[jax.ShapeDtypeStruct((S, DIL_WIDTH), F32)] * 3,
        compiler_params=_params())(q, k, v, lse, lse_mix, o_cat, do_cat)


def combine_fwd(name, outs, lses, o_cat, tm=512):
    S = outs[0].shape[0]

    def body(o1, o2, o3, l1, l2, l3, cat_in, cat_out, mix_ref):
        ls = [l1[...], l2[...], l3[...]]
        m = jnp.maximum(jnp.maximum(ls[0], ls[1]), ls[2])
        e = [jnp.exp(l - m) for l in ls]
        den = e[0] + e[1] + e[2]
        cat_out[...] = (e[0] / den) * o1[...] + (e[1] / den) * o2[...] + (e[2] / den) * o3[...]
        mix_ref[...] = m + jnp.log(den)

    row = pl.BlockSpec((tm, DIL_WIDTH), lambda i: (i, 0))
    return pl.pallas_call(
        body, name=name, grid=(S // tm,), in_specs=[row] * 6 + [pl.BlockSpec(memory_space=pl.ANY)],
        out_specs=[pl.BlockSpec((tm, DIL_WIDTH), lambda i: (i, HEADS * LANES // DIL_WIDTH)), row],
        out_shape=[jax.ShapeDtypeStruct(o_cat.shape, F32), jax.ShapeDtypeStruct((S, DIL_WIDTH), F32)],
        input_output_aliases={6: 0}, compiler_params=_params())(*outs, *lses, o_cat)


def _shift_down(u, n, zero_head):
    out = pltpu.roll(u, n, axis=0)
    return jnp.where(lax.broadcasted_iota(jnp.int32, u.shape, 0) >= n, out, 0.0) if zero_head else out


def _shift_up(u, n, zero_tail):
    rows = u.shape[0]
    out = pltpu.roll(u, rows - n, axis=0)
    return jnp.where(lax.broadcasted_iota(jnp.int32, u.shape, 0) < rows - n, out, 0.0) if zero_tail else out


CONV_ROWS = 512
CONV_HALO = 16


def _conv_chunks(S, tail):
    out = []
    for r0 in range(0, S, CONV_ROWS):
        lo, hi = max(0, r0 - CONV_HALO), min(S, r0 + CONV_ROWS + (CONV_HALO if tail else 0))
        out.append((lo, hi, r0 - lo, CONV_ROWS))
    return out


CONV_TC = 256
CONV_NB = D_FF // CONV_TC


def _half_specs(rows, rows_axis=False):
    if rows_axis:
        return [pl.BlockSpec((rows, D_MODEL), lambda j: (j, 0)), pl.BlockSpec((rows, D_MODEL), lambda j: (j + CONV_NB, 0))]
    return [pl.BlockSpec((rows, CONV_TC), lambda j: (0, j)), pl.BlockSpec((rows, CONV_TC), lambda j: (0, j + CONV_NB))]


def _whole(a):
    return pl.BlockSpec(a.shape, lambda j: (0,) * a.ndim)


def _up_pair(h, ug_ref, uv_ref):
    return jnp.concatenate([_dot(h, ug_ref[...], "nt"), _dot(h, uv_ref[...], "nt")], axis=1)


def _conv_taps(uin, w, b, starts):
    u1, u2 = _shift_down(uin, 1, starts), _shift_down(uin, 2, starts)
    return u1, u2, w[2:3, :] * uin + w[1:2, :] * u1 + w[0:1, :] * u2 + b


def ffn_fwd(name, h, w_up_t, w_conv, b_conv, w_down):
    S = h.shape[0]

    def body(h_ref, ug_ref, uv_ref, wg_ref, wv_ref, bg_ref, bv_ref, wd_ref, dn_ref, up_ref):
        @pl.when(pl.program_id(0) == 0)
        def _():
            dn_ref[...] = jnp.zeros_like(dn_ref)

        w = jnp.concatenate([wg_ref[...], wv_ref[...]], axis=1)
        b = jnp.concatenate([bg_ref[...], bv_ref[...]], axis=1)
        chunks = _conv_chunks(S, tail=False)

        def project(c):
            lo, hi, keep, rows = c
            uin = _up_pair(h_ref[lo:hi, :], ug_ref, uv_ref)
            up_ref[lo + keep:lo + keep + rows, :] = uin[keep:keep + rows]
            return uin

        def gate(c, uin):
            lo, hi, keep, rows = c
            u = _conv_taps(uin, w, b, lo == 0)[2][keep:keep + rows]
            return (_silu(u[:, :CONV_TC]) * u[:, CONV_TC:]).astype(MXU_DTYPE)

        def project_down(c, act):
            dn_ref[c[0] + c[2]:c[0] + c[2] + c[3], :] += _dot(act, wd_ref[...])

        uin, act_prev = project(chunks[0]), None
        for i, c in enumerate(chunks):
            uin_next = project(chunks[i + 1]) if i + 1 < len(chunks) else None
            if act_prev is not None:
                project_down(chunks[i - 1], act_prev)
            act_prev = gate(c, uin)
            uin = uin_next
        project_down(chunks[-1], act_prev)

    return pl.pallas_call(
        body, name=name, grid=(CONV_NB,),
        in_specs=[_whole(h)] + _half_specs(CONV_TC, rows_axis=True) + _half_specs(3) + _half_specs(1)
        + [pl.BlockSpec((CONV_TC, w_down.shape[1]), lambda j: (j, 0))],
        out_specs=[pl.BlockSpec((S, w_down.shape[1]), lambda j: (0, 0)), pl.BlockSpec((S, 2 * CONV_TC), lambda j: (0, j))],
        out_shape=[jax.ShapeDtypeStruct((S, w_down.shape[1]), F32), jax.ShapeDtypeStruct((S, 2 * D_FF), F32)],
        compiler_params=_params())(h, w_up_t, w_up_t, w_conv, w_conv, b_conv, b_conv, w_down)


def ffn_bwd(name, h, up, w_up_t, w_conv, b_conv, d_dn, w_down):
    S, D = h.shape

    def body(h_ref, up_ref, ug_ref, uv_ref, wg_ref, wv_ref, bg_ref, bv_ref, dd_ref, wd_ref,
             dh_ref, gup_ref, gd_ref, dwg_ref, dwv_ref, dbg_ref, dbv_ref):
        @pl.when(pl.program_id(0) == 0)
        def _():
            dh_ref[...] = jnp.zeros_like(dh_ref)

        w = jnp.concatenate([wg_ref[...], wv_ref[...]], axis=1)
        b = jnp.concatenate([bg_ref[...], bv_ref[...]], axis=1)
        w_pair = jnp.concatenate([ug_ref[...], uv_ref[...]], axis=0)
        chunks = _conv_chunks(S, tail=True)

        def project(c):
            return up_ref[c[0]:c[1], :], _dot(dd_ref[c[0]:c[1], :], wd_ref[...], "nt")

        def through_conv(c, uin, da):
            lo, hi, keep, rows = c
            u1, u2, u = _conv_taps(uin, w, b, lo == 0)
            gate, val = u[:, :CONV_TC], u[:, CONV_TC:]
            sig = 1.0 / (1.0 + jnp.exp(-gate))
            du = jnp.concatenate([da * val * (sig * (1.0 + gate * (1.0 - sig))), da * (gate * sig)], axis=1)
            dup = w[2:3, :] * du + w[1:2, :] * _shift_up(du, 1, hi == S) + w[0:1, :] * _shift_up(du, 2, hi == S)
            kept = slice(keep, keep + rows)
            du = du[kept]
            dw = jnp.concatenate([_colsum(du * u2[kept]), _colsum(du * u1[kept]), _colsum(du * uin[kept])], axis=0)
            return dup[kept].astype(MXU_DTYPE), (gate * sig * val)[kept].astype(MXU_DTYPE), dw, _colsum(du)

        def weight_grads(c, dup, act):
            out_rows = slice(c[0] + c[2], c[0] + c[2] + c[3])
            dh_ref[out_rows, :] += _dot(dup, w_pair)
            return _dot(dup, h_ref[out_rows, :], "tn"), _dot(act, dd_ref[out_rows, :], "tn")

        dw, db, g_up, g_dn = 0.0, 0.0, 0.0, 0.0
        proj, done = project(chunks[0]), None
        for i, c in enumerate(chunks):
            proj_next = project(chunks[i + 1]) if i + 1 < len(chunks) else None
            if done is not None:
                gu, gd = weight_grads(chunks[i - 1], *done)
                g_up, g_dn = g_up + gu, g_dn + gd
            dup, act, dw_c, db_c = through_conv(c, *proj)
            dw, db, done, proj = dw + dw_c, db + db_c, (dup, act), proj_next
        gu, gd = weight_grads(chunks[-1], *done)
        g_up, g_dn = g_up + gu, g_dn + gd
        gup_ref[0], gup_ref[1] = g_up[:CONV_TC].astype(gup_ref.dtype), g_up[CONV_TC:].astype(gup_ref.dtype)
        gd_ref[...] = g_dn.astype(gd_ref.dtype)
        dwg_ref[...], dwv_ref[...] = dw[:, :CONV_TC], dw[:, CONV_TC:]
        dbg_ref[...], dbv_ref[...] = db[:, :CONV_TC], db[:, CONV_TC:]

    half = lambda rows: pl.BlockSpec((rows, CONV_TC), lambda j: (0, j))
    rows_blk = pl.BlockSpec((CONV_TC, D), lambda j: (j, 0))
    dh, gup, gd, dwg, dwv, dbg, dbv = pl.pallas_call(
        body, name=name, grid=(CONV_NB,),
        in_specs=[_whole(h), pl.BlockSpec((S, 2 * CONV_TC), lambda j: (0, j))] + _half_specs(CONV_TC, rows_axis=True) + _half_specs(3)
        + _half_specs(1) + [_whole(d_dn), rows_blk],
        out_specs=[pl.BlockSpec((S, D), lambda j: (0, 0)), pl.BlockSpec((2, CONV_TC, D), lambda j: (0, j, 0)), rows_blk,
                   half(3), half(3), half(1), half(1)],
        out_shape=[jax.ShapeDtypeStruct((S, D), F32), jax.ShapeDtypeStruct((2, D_FF, D), MXU_DTYPE),
                   jax.ShapeDtypeStruct((D_FF, D), MXU_DTYPE)]
        + [jax.ShapeDtypeStruct((3, D_FF), F32)] * 2 + [jax.ShapeDtypeStruct((1, D_FF), F32)] * 2,
        compiler_params=_params())(h, up, w_up_t, w_up_t, w_conv, w_conv, b_conv, b_conv, d_dn, w_down)
    return dh, gup.reshape(2 * D_FF, D), gd, jnp.concatenate([dwg, dwv], axis=1), jnp.concatenate([dbg, dbv], axis=1)


def adamw(name, w, parts, m, v, tr=None):
    apart = w.ndim == 3
    R, C = w.shape[0], w.shape[-1]
    tr = tr or R
    assert R % tr == 0
    c1 = 1.0 - ADAM_B1 ** ADAM_STEP
    c2 = 1.0 - ADAM_B2 ** ADAM_STEP
    np_ = len(parts)

    def body(*refs):
        w_ref, m_ref, v_ref = refs[0], refs[1 + np_], refs[2 + np_]
        go_ref, d_ref, mo_ref, vo_ref = refs[3 + np_:]
        terms = []
        for part, ref in zip(parts, refs[1:1 + np_], strict=True):
            terms += [ref[...]] if part.ndim == 2 else [ref[p] for p in range(part.shape[0])]
        g = terms[0].astype(F32)
        for term in terms[1:]:
            g = g + term.astype(F32)
        m2 = ADAM_B1 * m_ref[...] + (1.0 - ADAM_B1) * g
        v2 = ADAM_B2 * v_ref[...] + (1.0 - ADAM_B2) * (g * g)
        go_ref[...] = g
        mo_ref[...] = m2
        vo_ref[...] = v2
        d_ref[...] = -ADAM_LR * ((m2 / c1) / (jnp.sqrt(v2 / c2) + ADAM_EPS) + ADAM_WD * w_ref[...])

    blk = pl.BlockSpec((tr, C), lambda i: (i, 0))
    own = pl.BlockSpec((tr, None, C), lambda i: (i, 0, 0)) if apart else blk
    part_specs = [blk if p.ndim == 2 else pl.BlockSpec((p.shape[0], tr, C), lambda i: (0, i, 0)) for p in parts]
    return pl.pallas_call(
        body, name=name, grid=(R // tr,),
        in_specs=[own] + part_specs + [own, own], out_specs=[own] * 4,
        out_shape=[jax.ShapeDtypeStruct(w.shape, F32)] * 4, compiler_params=_params())(w, *parts, m, v)


def _place():
    return lax.axis_index("x"), lax.axis_index("y"), lax.axis_index("c")


def all_gather(name, arrs, after=None):
    n = len(arrs)
    deps = [] if after is None else [after]

    def body(*refs):
        ins, outs = refs[:n], refs[n + len(deps):2 * n + len(deps)]
        send_sems, recv_sems, local_sems = refs[2 * n + len(deps):]
        x, y, c = _place()
        me, sibling = (x, y, c), (x, y, 1 - c)
        chips = [(1 - x, y), (x, 1 - y), (1 - x, 1 - y)]
        sends = []
        for t in range(n):
            out = outs[t]

            def slot(px, py, pc, out=out):
                return out.at[4 * px + 2 * py + pc]

            def copy(k, block, to, src=None, t=t, slot=slot):
                return pltpu.make_async_remote_copy(
                    src_ref=slot(*block) if src is None else src, dst_ref=slot(*block),
                    send_sem=send_sems.at[7 * t + k], recv_sem=recv_sems.at[7 * t + k],
                    device_id=to, device_id_type=MESH_ID)

            mine = pltpu.make_async_copy(ins[t], slot(*me), local_sems.at[t])
            mine.start()
            first = [copy(0, me, sibling, src=ins[t])]
            first += [copy(1 + j, me, (*chip, c), src=ins[t]) for j, chip in enumerate(chips)]
            for cp in first:
                cp.start()
            sends.append((mine, first, copy))
        for t in range(n):
            mine, first, copy = sends[t]
            passed = [copy(4 + j, (*chip, c), sibling) for j, chip in enumerate(chips)]
            for j, chip in enumerate(chips):
                copy(1 + j, (*chip, c), me).wait_recv()
                passed[j].start()
            copy(0, sibling, me).wait_recv()
            for j, chip in enumerate(chips):
                copy(4 + j, (*chip, 1 - c), me).wait_recv()
            for cp in first + passed:
                cp.wait_send()
            mine.wait()

    any_spec = pl.BlockSpec(memory_space=pl.ANY)
    res = pl.pallas_call(
        body, name=name, in_specs=[any_spec] * (n + len(deps)), out_specs=[any_spec] * n,
        out_shape=[jax.ShapeDtypeStruct((N_DEV,) + a.shape, a.dtype) for a in arrs],
        scratch_shapes=[pltpu.SemaphoreType.DMA((7 * n,)), pltpu.SemaphoreType.DMA((7 * n,)), pltpu.SemaphoreType.DMA((n,))],
        compiler_params=pltpu.CompilerParams(has_side_effects=True))(*arrs, *deps)
    return list(res)


def ada_modulation(name, c, w_ada):
    n_mod = w_ada.shape[1]

    def exchange(src_ref, dst_ref, send_sems, recv_sems):
        x, y, c_ = _place()
        me = 4 * x + 2 * y + c_
        copies = []
        for k in range(1, N_DEV):
            px, py, pc = x ^ (k >> 2), y ^ ((k >> 1) & 1), c_ ^ (k & 1)
            copies.append(pltpu.make_async_remote_copy(
                src_ref=src_ref, dst_ref=dst_ref.at[me], send_sem=send_sems.at[k - 1], recv_sem=recv_sems.at[k - 1],
                device_id=(px, py, pc), device_id_type=MESH_ID))
        for cp in copies:
            cp.start()
        for cp in copies:
            cp.wait_recv()
        for cp in copies:
            cp.wait_send()
        return me

    def body(c_ref, w_ref, sc_ref, mod_ref, c_all, send_c, recv_c, send_m, recv_m):
        me = exchange(c_ref, c_all, send_c, recv_c)
        c_all[me] = c_ref[...]
        sc = _silu(jnp.concatenate([c_all[p] for p in range(N_DEV)], axis=0))
        sc_ref[...] = sc.astype(sc_ref.dtype)
        mod_ref[me] = _dot(sc, w_ref[...])
        exchange(mod_ref.at[me], mod_ref, send_m, recv_m)

    vmem = pl.BlockSpec(memory_space=pltpu.VMEM)
    return pl.pallas_call(
        body, name=name, in_specs=[vmem, vmem], out_specs=[vmem, vmem],
        out_shape=[jax.ShapeDtypeStruct((N_DEV, c.shape[1]), MXU_DTYPE), jax.ShapeDtypeStruct((N_DEV, N_DEV, n_mod), F32)],
        scratch_shapes=[pltpu.VMEM((N_DEV, 1, c.shape[1]), F32)] + [pltpu.SemaphoreType.DMA((N_DEV - 1,))] * 4,
        compiler_params=pltpu.CompilerParams(has_side_effects=True, vmem_limit_bytes=VMEM_LIMIT))(c, w_ada)


HBM_SPEC = pl.BlockSpec(memory_space=pltpu.HBM)
SEM_SPEC = pl.BlockSpec(memory_space=pltpu.SEMAPHORE)
DATAFLOW = pltpu.SideEffectType.DATAFLOW_SIDE_EFFECTING


def _exchange_copies(srcs, lands, send_sems, recv_sems, gather, first=0):
    x, y, c = _place()
    me = 4 * x + 2 * y + c
    out = []
    for t, (src, land) in enumerate(zip(srcs, lands, strict=True)):
        for k in range(1, N_DEV):
            px, py, pc = x ^ (k >> 2), y ^ ((k >> 1) & 1), c ^ (k & 1)
            sem = 7 * (first + t) + k - 1
            out.append(pltpu.make_async_remote_copy(
                src_ref=src if gather else src.at[4 * px + 2 * py + pc],
                dst_ref=land.at[me] if gather else land.at[k - 1],
                send_sem=send_sems.at[sem], recv_sem=recv_sems.at[sem],
                device_id=(px, py, pc), device_id_type=MESH_ID))
    return out


def exchange_start(name, arrs, gather, after=None):
    n = len(arrs)
    lands = [lax.empty(((N_DEV,) + a.shape) if gather else ((N_DEV - 1,) + a.shape[1:]), a.dtype) for a in arrs]
    deps = [] if after is None else [after]

    def body(*refs):
        srcs, land_refs = refs[:n], refs[n:2 * n]
        send_sems, recv_sems = refs[2 * n + len(deps)], refs[2 * n + len(deps) + 1]
        token = refs[-1]
        for cp in _exchange_copies(srcs, land_refs, send_sems, recv_sems, gather):
            cp.start()
        token[...] = jnp.zeros_like(token)

    hbm = lambda a: pltpu.HBM(a.shape, a.dtype)
    res = pl.pallas_call(
        body, name=name,
        out_shape=(pltpu.SemaphoreType.DMA((7 * n,)), pltpu.SemaphoreType.DMA((7 * n,)), *[hbm(a) for a in arrs],
                   *[hbm(l) for l in lands], jax.ShapeDtypeStruct((8, 128), F32)),
        in_specs=[HBM_SPEC] * (2 * n) + [pl.BlockSpec(memory_space=pl.ANY)] * len(deps),
        out_specs=(SEM_SPEC, SEM_SPEC, *[HBM_SPEC] * (2 * n), pl.BlockSpec(memory_space=pltpu.VMEM)),
        input_output_aliases={i: 2 + i for i in range(2 * n)},
        compiler_params=pltpu.CompilerParams(has_side_effects=DATAFLOW),
    )(*[pltpu.with_memory_space_constraint(a, pltpu.HBM) for a in arrs + lands], *deps)
    return res[0], res[1], list(res[2:2 + n]), list(res[2 + n:2 + 2 * n]), res[-1]


def exchange_wait(name, started, gather, after, first=0, count=None):
    send_sems, recv_sems, srcs, lands, _ = started
    count = len(srcs) - first if count is None else count
    srcs, lands = srcs[first:first + count], lands[first:first + count]
    n = len(srcs)

    def body(*refs):
        src_refs, land_refs = refs[:n], refs[n:2 * n]
        copies = _exchange_copies(src_refs, land_refs, refs[2 * n], refs[2 * n + 1], gather, first)
        for cp in copies:
            cp.wait_send()
        for cp in copies:
            cp.wait_recv()

    hbm = lambda a: pltpu.HBM(a.shape, a.dtype)
    res = pl.pallas_call(
        body, name=name, out_shape=tuple(hbm(a) for a in srcs + lands),
        in_specs=[HBM_SPEC] * (2 * n) + [SEM_SPEC, SEM_SPEC, pl.BlockSpec(memory_space=pl.ANY)],
        out_specs=tuple([HBM_SPEC] * (2 * n)), input_output_aliases={i: i for i in range(2 * n)},
        compiler_params=pltpu.CompilerParams(has_side_effects=DATAFLOW),
    )(*srcs, *lands, send_sems, recv_sems, after)
    return list(res[:n]), list(res[n:])


def _gather_cols(stack):
    p, k, n = stack.shape
    return stack.transpose(1, 0, 2).reshape(k, p * n)


def _scatter_cols(full):
    k, n = full.shape
    return full.reshape(k, N_DEV, n // N_DEV).transpose(1, 0, 2)


def _gather_rows(stack):
    p, r, n = stack.shape
    return stack.reshape(p * r, n)


def _scatter_rows(full):
    r, n = full.shape
    return full.reshape(N_DEV, r // N_DEV, n)


_IN_NAT = Q_LORA + KV_LORA
TRANSPOSED = ("w_in", "w_q_b", "w_up")
ROWS_APART = ("w_in", "w_conv")


def to_kernel_layout(name, w):
    if name == "w_in":
        z = lambda n: jnp.zeros((n, w.shape[1]), w.dtype)
        return jnp.concatenate([w[:_IN_NAT], z(KPE_LO), w[_IN_NAT:_IN_NAT + ROPE], z(LANES - KPE_LO - ROPE), w[_IN_NAT + ROPE:]], axis=0)
    if name == "w_q_b":
        return jnp.pad(w.reshape(HEADS, NOPE + ROPE, -1), ((0, 0), (0, LANES - NOPE - ROPE), (0, 0))).reshape(HEADS * LANES, -1)
    if name == "w_o":
        mla = jnp.pad(w[:HEADS * NOPE].reshape(HEADS, NOPE, -1), ((0, 0), (LANES - NOPE, 0), (0, 0))).reshape(HEADS * LANES, -1)
        return jnp.concatenate([mla, w[HEADS * NOPE:]], axis=0)
    return w


def from_kernel_layout(name, g):
    if name == "w_in":
        return jnp.concatenate([g[:_IN_NAT], g[P_KPE + KPE_LO:P_KPE + KPE_LO + ROPE], g[P_QD:]], axis=0)
    if name == "w_q_b":
        return g.reshape(HEADS, LANES, -1)[:, :NOPE + ROPE, :].reshape(HEADS * (NOPE + ROPE), -1)
    if name == "w_o":
        mla = g[:HEADS * LANES].reshape(HEADS, LANES, -1)[:, LANES - NOPE:, :].reshape(HEADS * NOPE, -1)
        return jnp.concatenate([mla, g[HEADS * LANES:]], axis=0)
    return g


SMALL_COLS = 1024
SMALL_ROWS = 24
SMALL_AT = {"loss": (0, 0, 1), "b_ada": (1, 0, 6 * D_MODEL), "g_mix_norm": (7, 0, D_MODEL), "g_q_lat": (8, 0, Q_LORA),
            "g_kv_lat": (9, 0, KV_LORA), "g_mla_q_nope": (10, 0, NOPE), "g_mla_q_pe": (10, 128, ROPE),
            "g_mla_k_nope": (10, 256, NOPE), "g_mla_k_pe": (10, 384, ROPE), "g_dil_q": (10, 512, DIL_DIM),
            "g_dil_k": (10, 640, DIL_DIM), "g_ffn_norm": (11, 0, D_MODEL), "b_conv": (12, 0, 2 * D_FF)}
SMALL_PARAMS = tuple(n for n in SMALL_AT if n != "loss")


def _pack_small(values):
    by_row = {}
    for name, (row, off, n) in SMALL_AT.items():
        by_row.setdefault(row, []).append((off, values[name].reshape(-1).astype(F32)))
    out = []
    for row in sorted(by_row):
        pieces, at = [], 0
        for off, v in sorted(by_row[row], key=lambda t: t[0]):
            pieces += [jnp.zeros((off - at,), F32), v]
            at = off + v.shape[0]
        flat = jnp.concatenate(pieces)
        nrows = -(-flat.shape[0] // SMALL_COLS)
        out.append(jnp.pad(flat, (0, nrows * SMALL_COLS - flat.shape[0])).reshape(nrows, SMALL_COLS))
    packed = jnp.concatenate(out, axis=0)
    return jnp.pad(packed, ((0, SMALL_ROWS - packed.shape[0]), (0, 0)))


def _adam(w, g, m, v):
    c1 = 1.0 - ADAM_B1 ** ADAM_STEP
    c2 = 1.0 - ADAM_B2 ** ADAM_STEP
    m2 = ADAM_B1 * m + (1.0 - ADAM_B1) * g
    v2 = ADAM_B2 * v + (1.0 - ADAM_B2) * (g * g)
    return -ADAM_LR * ((m2 / c1) / (jnp.sqrt(v2 / c2) + ADAM_EPS) + ADAM_WD * w), m2, v2


def adamw_small(name, stack, params):
    flat = [a for n in SMALL_PARAMS for a in params[n]]

    def body(stack_ref, *refs):
        ins, outs = refs[:len(flat)], refs[len(flat):]
        g_all = stack_ref[0]
        for p in range(1, N_DEV):
            g_all = g_all + stack_ref[p]
        outs[0][...] = g_all[0:1, 0:1]
        for i, pname in enumerate(SMALL_PARAMS):
            row, off, n = SMALL_AT[pname]
            w_ref, m_ref, v_ref = ins[3 * i:3 * i + 3]
            go_ref, d_ref, mo_ref, vo_ref = outs[1 + 4 * i:5 + 4 * i]
            for c0 in range(0, n, SMALL_COLS):
                cn = min(SMALL_COLS, n - c0)
                r = row + c0 // SMALL_COLS
                g = g_all[r:r + 1, off:off + cn]
                cols = (slice(None), slice(c0, c0 + cn))
                d, m2, v2 = _adam(w_ref[cols], g, m_ref[cols], v_ref[cols])
                go_ref[cols], d_ref[cols], mo_ref[cols], vo_ref[cols] = g, d, m2, v2

    whole = lambda a: pl.BlockSpec(a.shape, lambda: (0,) * a.ndim)
    out_shape = [jax.ShapeDtypeStruct((1, 1), F32)] + [jax.ShapeDtypeStruct(a.shape, F32) for n in SMALL_PARAMS for a in params[n][:1] * 4]
    res = pl.pallas_call(body, name=name, in_specs=[whole(stack)] + [whole(a) for a in flat],
                         out_specs=[pl.BlockSpec(s.shape, lambda s=s: (0,) * len(s.shape)) for s in out_shape],
                         out_shape=out_shape, compiler_params=_params())(stack, *flat)
    return res[0], {n: res[1 + 4 * i:5 + 4 * i] for i, n in enumerate(SMALL_PARAMS)}


def _local_step(x, pos, mod, target, w, fetch, emit):
    S = SEQ
    sh1, sc1, g1, sh2, sc2, g2 = [mod[:, i * D_MODEL:(i + 1) * D_MODEL] for i in range(6)]
    zeros = lambda n: jnp.zeros((1, n), F32)
    g_q = jnp.concatenate([w["g_mla_q_nope"], w["g_mla_q_pe"], zeros(LANES - NOPE - ROPE)], axis=1)
    g_k = jnp.concatenate([w["g_mla_k_nope"], zeros(LANES - NOPE)], axis=1)
    g_kpe = jnp.concatenate([zeros(KPE_LO), w["g_mla_k_pe"], zeros(LANES - KPE_LO - ROPE)], axis=1)
    g_dq = jnp.concatenate([w["g_dil_q"]] * 2, axis=1)
    g_dk = jnp.concatenate([w["g_dil_k"]] * 2, axis=1)
    b_conv = w["b_conv"]

    def inv_freq(d):
        return jnp.power(ROPE_THETA, -2.0 * jnp.arange(d // 2, dtype=F32) / d)

    f_mla = jnp.concatenate([jnp.zeros((KPE_LO,), F32), inv_freq(ROPE), inv_freq(ROPE), jnp.zeros((LANES - KPE_LO - ROPE,), F32)])
    f_dil = jnp.concatenate([inv_freq(DIL_DIM)] * 4)

    def tables_fn(rows, params):
        (p,), (fa, fb) = rows, params
        return [jnp.cos(p * fa), jnp.sin(p * fa), jnp.cos(p * fb), jnp.sin(p * fb)], []

    cos_m, sin_m, cos_d, sin_d = rowwise("rope_tables", tables_fn, [pos], [f_mla.reshape(1, LANES), f_dil.reshape(1, LANES)],
                                         [(LANES, F32)] * 4)
    tables = [cos_m, sin_m, cos_d, sin_d]
    H_M, H_D = ROPE // 2, DIL_DIM // 2

    def ln1_fn(rows, params):
        (xv,), (g, sc, sh) = rows, params
        y, _, _ = _rms(xv, g)
        return [y * (1.0 + sc) + sh], []

    (h,) = rowwise("ln1_fwd", ln1_fn, [x], [w["g_mix_norm"], sc1, sh1], [(D_MODEL, MXU_DTYPE)])
    w_in = fetch("w_in", h)
    proj = matmul("proj_fwd", h, w_in, "nt", tm=512)

    def post_fn(rows, params):
        (pv, cm, sm, cd, sd), (gq, gkv, gkp, gdq, gdk) = rows, params
        kper = _rope(_grms(pv[:, P_KPE:P_QD], gkp, KPE_GROUPS)[0], cm, sm, H_M)
        qd = [_rope(_grms(c, gdq, DIL_GROUPS)[0], cd, sd, H_D) for c in _chunks(pv[:, P_QD:P_KD])]
        kd = [_rope(_grms(c, gdk, DIL_GROUPS)[0], cd, sd, H_D) for c in _chunks(pv[:, P_KD:P_VD])]
        return [_rms(pv[:, P_QLAT:P_KVLAT], gq)[0], _rms(pv[:, P_KVLAT:P_KPE], gkv)[0], kper,
                jnp.concatenate(qd, axis=1), jnp.concatenate(kd, axis=1)], []

    post_params = [w["g_q_lat"], w["g_kv_lat"], g_kpe, g_dq, g_dk]
    qln, kvn, kper, qd_r, kd_r = rowwise(
        "proj_post", post_fn, [proj] + tables, post_params,
        [(Q_LORA, MXU_DTYPE), (KV_LORA, MXU_DTYPE), (LANES, MXU_DTYPE)] + [(DIL_WIDTH, F32)] * 2, tm=256)
    w_q_b, w_kv_b = fetch("w_q_b", qln), fetch("w_kv_b", kvn)
    q = matmul("q_fwd", qln, w_q_b, "nt", tm=1024)
    kv = matmul("kv_fwd", kvn, w_kv_b, "nn", tm=1024)

    def mla_prep_fn(rows, params):
        (qv, kvv, kp, cm, sm), (gq, gk) = rows, params
        value_lanes = _lane(kp.shape) >= NOPE
        qs, ks, vs = [], [], []
        for qc, kc in zip(_chunks(qv), _chunks(kvv), strict=True):
            qs.append(_rope(_grms(qc, gq, Q_GROUPS)[0], cm, sm, H_M))
            ks.append(_grms(kc, gk, K_GROUPS)[0] + kp)
            vs.append(jnp.where(value_lanes, kc, 0.0))
        return [jnp.concatenate(t, axis=1) for t in (qs, ks, vs)], []

    q_mla, k_mla, v_mla = rowwise("mla_prep", mla_prep_fn, [q, kv, kper, cos_m, sin_m], [g_q, g_k],
                                  [(HEADS * LANES, MXU_DTYPE)] * 3, tm=256)
    mla_scale = (NOPE + ROPE) ** -0.5
    o_cat, lse_mla = mla_fwd("mla_fwd", q_mla, k_mla, v_mla, mla_scale)

    band = [band_fwd(f"band{dil}_fwd", qd_r, kd_r, proj, dil) for dil in DILATIONS]
    o_cat, lse_mix = combine_fwd("dil_combine", [b[0] for b in band], [b[1] for b in band], o_cat)
    w_o = fetch("w_o", o_cat)
    mix = matmul("mix_fwd", o_cat, w_o, "nn", tm=512)

    def mid_fn(rows, params):
        (xv, mx), (gate1, g, sc, sh) = rows, params
        x1 = xv + gate1 * mx
        y, _, _ = _rms(x1, g)
        return [x1, y * (1.0 + sc) + sh], []

    x1, h2 = rowwise("mid_fwd", mid_fn, [x, mix], [g1, w["g_ffn_norm"], sc2, sh2], [(D_MODEL, F32), (D_MODEL, MXU_DTYPE)])
    w_up, w_conv, w_down = fetch("w_up", h2), fetch("w_conv", h2), fetch("w_down", h2)
    dn, up = ffn_fwd("ffn_fwd", h2, w_up, w_conv, b_conv, w_down)

    def final_fn(rows, params):
        (x1v, dnv, tgt), (gate2,) = rows, params
        r = x1v + gate2 * dnv - tgt
        dy = r * (1.0 / D_MODEL)
        loss = jnp.sum(_colsum(r * r), axis=-1, keepdims=True) * (0.5 / D_MODEL)
        return [dy, gate2 * dy], [loss, _colsum(dy * dnv)]

    dy, d_dn, loss, dg2 = rowwise("loss_head", final_fn, [x1, dn, target], [g2], [(D_MODEL, F32), (D_MODEL, MXU_DTYPE)],
                                  [1, D_MODEL])
    dh2, g_up, g_down, g_w_conv, g_b_conv = ffn_bwd("ffn_bwd", h2, up, w_up, w_conv, b_conv, d_dn, w_down)
    emit("w_down", g_down)
    emit("w_conv", g_w_conv)
    sent = emit("w_up", g_up)

    def mid_bwd_fn(rows, params):
        (dh2v, dyv, x1v, mx), (gate1, g, sc) = rows, params
        yn, n, rstd = _rms(x1v, g)
        dx_n, dg = _rms_bwd(dh2v * (1.0 + sc), n, rstd, g)
        dx1 = dyv + dx_n
        return [dx1, gate1 * dx1], [dg, _colsum(dh2v * yn), _colsum(dh2v), _colsum(dx1 * mx)]

    dx1, dmix, dg_ffn, dsc2, dsh2, dg1 = rowwise(
        "mid_bwd", mid_bwd_fn, [dh2, dy, x1, mix], [g1, w["g_ffn_norm"], sc2], [(D_MODEL, F32), (D_MODEL, MXU_DTYPE)],
        [D_MODEL] * 4, dep=sent)

    sent = emit("w_o", matmul("mix_wgrad", o_cat, dmix, "tn", tm=512, out_dtype=MXU_DTYPE))
    do_cat = matmul("mix_dgrad", dmix, w_o, "nt", tm=512, dep=sent)
    dband = [band_bwd(f"band{dil}_bwd", qd_r, kd_r, proj, b[1], lse_mix, o_cat, do_cat, dil) for dil, b in zip(DILATIONS, band)]
    dq_mla, dkv_mla, dkper = mla_bwd("mla_bwd", q_mla, k_mla, v_mla, o_cat, do_cat, lse_mla, mla_scale)

    def mla_prep_bwd_fn(rows, params):
        (dqv, dkvv, qv, kvv, cm, sm), (gq, gk) = rows, params
        nope_lanes = _lane(cm.shape) < NOPE
        dqs, dkvs, dgq, dgk = [], [], 0.0, 0.0
        for dqc, dkc, qc, kc in zip(_chunks(dqv), _chunks(dkvv), _chunks(qv), _chunks(kvv), strict=True):
            _, n, rstd = _grms(qc, gq, Q_GROUPS)
            dx, dg = _grms_bwd(_rope_bwd(dqc, cm, sm, H_M), n, rstd, gq, Q_GROUPS)
            dqs.append(dx)
            dgq = dgq + dg
            _, n, rstd = _grms(kc, gk, K_GROUPS)
            dx, dg = _grms_bwd(dkc, n, rstd, gk, K_GROUPS)
            dkvs.append(jnp.where(nope_lanes, dx, dkc))
            dgk = dgk + dg
        return [jnp.concatenate(dqs, axis=1), jnp.concatenate(dkvs, axis=1)], [dgq, dgk]

    dq, dkv, dg_q, dg_k = rowwise("mla_prep_bwd", mla_prep_bwd_fn, [dq_mla, dkv_mla, q, kv, cos_m, sin_m], [g_q, g_k],
                                  [(HEADS * LANES, MXU_DTYPE)] * 2, [LANES, LANES], tm=256)
    emit("w_q_b", matmul("q_wgrad", dq, qln, "tn", out_dtype=MXU_DTYPE))
    emit("w_kv_b", matmul("kv_wgrad", kvn, dkv, "tn", out_dtype=MXU_DTYPE))
    dqln = matmul("q_dgrad", dq, w_q_b, "nn", tm=1024)
    dkvn = matmul("kv_dgrad", dkv, w_kv_b, "nt", tm=1024)

    def pre_bwd_fn(rows, params):
        dql, dkvl, dkp = rows[0:3]
        dqd_, dkd_, dvd_ = [rows[3 + 3 * i] + rows[4 + 3 * i] + rows[5 + 3 * i] for i in range(3)]
        pv, cm, sm, cd, sd = rows[12:]
        gq, gkv, gkp, gdq, gdk = params
        r_q = _norm_bwd(dql, pv[:, P_QLAT:P_KVLAT], gq)
        r_kv = _norm_bwd(dkvl, pv[:, P_KVLAT:P_KPE], gkv)
        _, n, rstd = _grms(pv[:, P_KPE:P_QD], gkp, KPE_GROUPS)
        r_kp = _grms_bwd(_rope_bwd(dkp, cm, sm, H_M), n, rstd, gkp, KPE_GROUPS)
        outs, dgs = [r_q[0], r_kv[0], r_kp[0]], []
        for dval, lo, g in ((dqd_, P_QD, gdq), (dkd_, P_KD, gdk)):
            dg_sum = 0.0
            for dc, xc in zip(_chunks(dval), _chunks(pv[:, lo:lo + DIL_WIDTH]), strict=True):
                _, n, rstd = _grms(xc, g, DIL_GROUPS)
                dx, dg = _grms_bwd(_rope_bwd(dc, cd, sd, H_D), n, rstd, g, DIL_GROUPS)
                outs.append(dx)
                dg_sum = dg_sum + dg
            dgs.append(dg_sum)
        return [jnp.concatenate(outs + [dvd_], axis=1)], [r_q[1], r_kv[1], r_kp[1]] + dgs

    dproj, dg_q_lat, dg_kv_lat, dg_kpe, dg_dq, dg_dk = rowwise(
        "proj_pre_bwd", pre_bwd_fn,
        [dqln, dkvn, dkper] + [d[i] for i in range(3) for d in dband] + [proj] + tables, post_params,
        [(P_END, MXU_DTYPE)], [Q_LORA, KV_LORA, LANES, LANES, LANES], tm=256)
    sent = emit("w_in", matmul("proj_wgrad", dproj, h, "tn", tn=512, out_dtype=MXU_DTYPE))
    dh = matmul("proj_dgrad", dproj, w_in, "nn", tm=512, dep=sent)

    def ln1_bwd_fn(rows, params):
        (dhv, dres, xv), (g, sc) = rows, params
        yn, n, rstd = _rms(xv, g)
        dx_n, dg = _rms_bwd(dhv * (1.0 + sc), n, rstd, g)
        return [dres + dx_n], [dg, _colsum(dhv * yn), _colsum(dhv)]

    grad_x, dg_mix, dsc1, dsh1 = rowwise("ln1_bwd", ln1_bwd_fn, [dh, dx1, x], [w["g_mix_norm"], sc1], [(D_MODEL, F32)],
                                         [D_MODEL] * 3)
    dmod = jnp.concatenate([dsh1, dsc1, dg1, dsh2, dsc2, dg2], axis=-1)
    small = {"loss": loss, "b_ada": dmod, "g_mix_norm": dg_mix, "g_q_lat": dg_q_lat, "g_kv_lat": dg_kv_lat,
             "g_mla_q_nope": dg_q[:, :NOPE], "g_mla_q_pe": dg_q[:, NOPE:NOPE + ROPE], "g_mla_k_nope": dg_k[:, :NOPE],
             "g_mla_k_pe": dg_kpe[:, KPE_LO:KPE_LO + ROPE], "g_dil_q": dg_dq[:, :DIL_DIM] + dg_dq[:, DIL_DIM:],
             "g_dil_k": dg_dk[:, :DIL_DIM] + dg_dk[:, DIL_DIM:], "g_ffn_norm": dg_ffn,
             "b_conv": g_b_conv}
    return grad_x, small


COL_SHARDED = ("w_kv_b", "w_conv")
ROW_SHARDED = ("w_o", "w_down") + TRANSPOSED
ADAM_TILE = {"w_ada": 256, "w_up": 176, "w_down": 176}
GATHER_GROUPS = (("w_in",), ("w_q_b", "w_kv_b"), ("w_o",), ("w_up", "w_conv", "w_down"))
SCATTER_GROUPS = (("w_down", "w_conv", "w_up"), ("w_o",), ("w_q_b", "w_kv_b", "w_in"))
OUT_WEIGHTS = ("w_ada", "b_ada", "g_mix_norm", "w_in", "g_q_lat", "w_q_b", "g_kv_lat", "w_kv_b", "g_mla_q_nope", "g_mla_q_pe",
               "g_mla_k_nope", "g_mla_k_pe", "g_dil_q", "g_dil_k", "w_o", "g_ffn_norm", "w_up", "w_conv", "b_conv", "w_down")


def kernel(x, c, positions, w_ada, b_ada, g_mix_norm, w_in, g_q_lat, w_q_b, g_kv_lat, w_kv_b, g_mla_q_nope, g_mla_q_pe, g_mla_k_nope, g_mla_k_pe, g_dil_q, g_dil_k, w_o, g_ffn_norm, w_up, w_conv, b_conv, w_down, loss_target, m_w_ada, m_b_ada, m_g_mix_norm, m_w_in, m_g_q_lat, m_w_q_b, m_g_kv_lat, m_w_kv_b, m_g_mla_q_nope, m_g_mla_q_pe, m_g_mla_k_nope, m_g_mla_k_pe, m_g_dil_q, m_g_dil_k, m_w_o, m_g_ffn_norm, m_w_up, m_w_conv, m_b_conv, m_w_down, v_w_ada, v_b_ada, v_g_mix_norm, v_w_in, v_g_q_lat, v_w_q_b, v_g_kv_lat, v_w_kv_b, v_g_mla_q_nope, v_g_mla_q_pe, v_g_mla_k_nope, v_g_mla_k_pe, v_g_dil_q, v_g_dil_k, v_w_o, v_g_ffn_norm, v_w_up, v_w_conv, v_b_conv, v_w_down):
    args = dict(locals())
    xi, yi, ci = _place()
    me = 4 * xi + 2 * yi + ci
    def local(prefix, n):
        a = args[prefix + n]
        if n in ROWS_APART:
            return jnp.transpose(a, (2, 0, 1) if n in TRANSPOSED else (1, 0, 2))
        return a[0].T if n in TRANSPOSED else a[0]

    def as_output(n, r):
        if n in ROWS_APART:
            return jnp.transpose(r, (1, 2, 0) if n in TRANSPOSED else (1, 0, 2))
        return (r.T if n in TRANSPOSED else r)[None]

    shard = {n: local("", n) for n in COL_SHARDED + ROW_SHARDED + ("w_ada",)}
    flat = lambda n, a: a.reshape(a.shape[0], a.shape[-1]) if n in ROWS_APART else a
    small_w = {n: args[n] for n in SMALL_PARAMS}

    sc_all, mod_all = ada_modulation("ada_mod", c, shard["w_ada"])

    payload = {n: flat(n, shard[n]) if n == "w_conv" else flat(n, shard[n]).astype(MXU_DTYPE) for n in COL_SHARDED + ROW_SHARDED}
    gather_order = [n for grp in GATHER_GROUPS for n in grp]
    gathered = exchange_start("gather_start", [payload[n] for n in gather_order], gather=True, after=mod_all)
    after_start = gathered[-1]
    full = {}

    def fetch(name, after):
        if name not in full:
            (i, grp), = [(i, grp) for i, grp in enumerate(GATHER_GROUPS) if name in grp]
            srcs, lands = exchange_wait(f"gather{i}_wait", gathered, True, after, gather_order.index(grp[0]), len(grp))
            for n, src, land in zip(grp, srcs, lands, strict=True):
                stack = lax.dynamic_update_index_in_dim(land, src, me, 0)
                full[n] = to_kernel_layout(n, _gather_cols(stack) if n in COL_SHARDED else _gather_rows(stack))
        return full[name]

    mod_row = lax.dynamic_index_in_dim(mod_all, me, axis=1, keepdims=False).reshape(1, 6 * D_MODEL)
    (mod,) = rowwise("ada_bias", lambda rows, params: ([rows[0] + rows[1]], []), [mod_row, b_ada], [], [(6 * D_MODEL, F32)],
                     dep=after_start)

    own, pending, scatters = {}, {}, {}

    def emit(name, grad):
        grad = from_kernel_layout(name, grad)
        parts = _scatter_cols(grad) if name in COL_SHARDED else _scatter_rows(grad)
        own[name] = lax.dynamic_index_in_dim(parts, me, 0, keepdims=False)
        pending[name] = parts
        for i, grp in enumerate(SCATTER_GROUPS):
            if name == grp[-1]:
                scatters[i] = exchange_start(f"scatter{i}_start", [pending[n] for n in grp], gather=False)
                return scatters[i][-1]
        return None

    pos = positions.reshape(SEQ, 1).astype(F32)
    grad_x, small = _local_step(x[0], pos, mod, loss_target[0], small_w, fetch, emit)

    res, done = {}, grad_x
    for i, grp in enumerate(SCATTER_GROUPS):
        _, lands = exchange_wait(f"scatter{i}_wait", scatters[i], False, done)
        for n, land in zip(grp, lands, strict=True):
            res[n] = adamw(f"adamw_{n}", shard[n], [own[n], land], local("m_", n), local("v_", n), ADAM_TILE.get(n))
            done = res[n][0]
            res[n] = [as_output(n, r) for r in res[n]]
    (small_all,) = all_gather("gather_small", [_pack_small(small)], after=done)
    loss, small_res = adamw_small("adamw_small", small_all, {n: (args[n], args["m_" + n], args["v_" + n]) for n in SMALL_PARAMS})
    row, _, n_mod = SMALL_AT["b_ada"]
    dmod_all = small_all[:, row:row + n_mod // SMALL_COLS, :].reshape(N_DEV, n_mod)
    dmod_mine = lax.dynamic_slice_in_dim(dmod_all, me * (6 * D_MODEL // N_DEV), 6 * D_MODEL // N_DEV, axis=1)
    g_w_ada = matmul("ada_wgrad", sc_all, dmod_mine, "tn")
    res["w_ada"] = [r[None] for r in adamw("adamw_w_ada", shard["w_ada"], [g_w_ada], m_w_ada[0], v_w_ada[0], ADAM_TILE["w_ada"])]

    def leaf(kind, n):
        return res[n][kind] if n in res else small_res[n][kind]

    return (loss.reshape(()), grad_x[None], *[leaf(k, n) for k in range(4) for n in OUT_WEIGHTS])
```

```python
import jax
import jax.numpy as jnp
from jax import lax
from jax.experimental import pallas as pl
from jax.experimental.pallas import tpu as pltpu

F32 = jnp.float32
MXU_DTYPE = jnp.bfloat16

N_DEV = 8
D_MODEL = 1024
SEQ = 2048
HEADS = 8
NOPE = 64
ROPE = 32
Q_LORA = 512
KV_LORA = 256
DIL_DIM = 64
DIL_WIDTH = HEADS * DIL_DIM
DILATIONS = (1, 4, 16)
SPAN = 128
D_FF = 2816
LANES = 128
ROPE_THETA = 10000.0
EPS = 1e-6
NEG_INF = -1e30
ADAM_LR, ADAM_B1, ADAM_B2, ADAM_EPS, ADAM_WD, ADAM_STEP = 0.001, 0.9, 0.999, 1e-08, 0.01, 10
VMEM_LIMIT = 56 * 1024 * 1024
MESH_ID = pl.DeviceIdType.MESH

P_QLAT, P_KVLAT, P_KPE, P_QD, P_KD, P_VD, P_END = 0, 512, 768, 896, 1408, 1920, 2432
KPE_LO = 64
MIX_IN = HEADS * LANES + DIL_WIDTH


def _params(**kw):
    return pltpu.CompilerParams(vmem_limit_bytes=VMEM_LIMIT, **kw)


def rowwise(name, fn, rows, params, out_rows, out_accs=(), tm=512, dep=None):
    deps = [] if dep is None else [dep]
    rows = [r if isinstance(r, tuple) else (r, r.shape[1], 0) for r in rows]
    R = rows[0][0].shape[0]
    tm = min(tm, R)
    steps = R // tm
    assert steps * tm == R
    in_specs = []
    for a, width, cb in rows:
        ri = a.shape[0]
        per = ri // tm
        assert per * tm == ri
        if ri == R:
            in_specs.append(pl.BlockSpec((tm, width), lambda i, cb=cb: (i, cb)))
        else:
            in_specs.append(pl.BlockSpec((tm, width), lambda i, per=per, cb=cb: (i % per, cb)))
    for p in params:
        in_specs.append(pl.BlockSpec(p.shape, lambda i: (0,) * p.ndim))
    in_specs += [pl.BlockSpec(memory_space=pl.ANY)] * len(deps)
    out_shape = [jax.ShapeDtypeStruct((R, d), dt) for d, dt in out_rows]
    out_specs = [pl.BlockSpec((tm, d), lambda i: (i, 0)) for d, _ in out_rows]
    out_shape += [jax.ShapeDtypeStruct((1, n), F32) for n in out_accs]
    out_specs += [pl.BlockSpec((1, n), lambda i: (0, 0)) for n in out_accs]
    nr, npar, no, na = len(rows), len(params), len(out_rows), len(out_accs)

    def body(*refs):
        rvals = [r[...] for r in refs[:nr]]
        pvals = [r[...] for r in refs[nr:nr + npar]]
        outs, accs = fn(rvals, pvals)
        first_out = nr + npar + len(deps)
        for ref, v in zip(refs[first_out:first_out + no], outs, strict=True):
            ref[...] = v.astype(ref.dtype)
        if na:
            acc_refs = refs[first_out + no:]
            i = pl.program_id(0)

            @pl.when(i == 0)
            def _():
                for ref, v in zip(acc_refs, accs, strict=True):
                    ref[...] = v

            @pl.when(i > 0)
            def _():
                for ref, v in zip(acc_refs, accs, strict=True):
                    ref[...] += v

    res = pl.pallas_call(body, name=name, grid=(steps,), in_specs=in_specs, out_specs=out_specs,
                         out_shape=out_shape, compiler_params=_params())(*[r[0] for r in rows], *params, *deps)
    return list(res)


_DIMS = {"nn": ((1,), (0,)), "nt": ((1,), (1,)), "tn": ((0,), (0,))}


def _dot(a, b, mode="nn"):
    return lax.dot_general(a.astype(MXU_DTYPE), b.astype(MXU_DTYPE), (_DIMS[mode], ((), ())),
                           preferred_element_type=F32)


def matmul(name, a, b, mode, tm=None, tn=None, tk=None, out_dtype=F32, dep=None):
    if mode == "tn":
        K, M = a.shape
    else:
        M, K = a.shape
    N = b.shape[0] if mode == "nt" else b.shape[1]
    tm, tn, tk = tm or M, tn or N, tk or K
    nm, nn, nk = M // tm, N // tn, K // tk
    assert nm * tm == M and nn * tn == N and nk * tk == K
    a_spec = pl.BlockSpec((tk, tm), lambda i, j, k: (k, i)) if mode == "tn" else pl.BlockSpec((tm, tk), lambda i, j, k: (i, k))
    b_spec = pl.BlockSpec((tn, tk), lambda i, j, k: (j, k)) if mode == "nt" else pl.BlockSpec((tk, tn), lambda i, j, k: (k, j))
    deps = [] if dep is None else [dep]

    def body(a_ref, b_ref, *rest):
        o_ref, scratch = rest[len(deps)], rest[len(deps) + 1:]
        p = _dot(a_ref[...], b_ref[...], mode)
        if nk == 1:
            o_ref[...] = p.astype(o_ref.dtype)
        else:
            acc = scratch[0]
            k = pl.program_id(2)

            @pl.when(k == 0)
            def _():
                acc[...] = p

            @pl.when(k > 0)
            def _():
                acc[...] += p

            @pl.when(k == nk - 1)
            def _():
                o_ref[...] = acc[...].astype(o_ref.dtype)

    return pl.pallas_call(
        body, name=name, grid=(nm, nn, nk), in_specs=[a_spec, b_spec] + [pl.BlockSpec(memory_space=pl.ANY)] * len(deps),
        out_specs=pl.BlockSpec((tm, tn), lambda i, j, k: (i, j)),
        out_shape=jax.ShapeDtypeStruct((M, N), out_dtype),
        scratch_shapes=[pltpu.VMEM((tm, tn), F32)] if nk > 1 else [],
        compiler_params=_params())(a, b, *deps)


def _rms(x, g):
    rstd = lax.rsqrt(jnp.mean(x * x, axis=-1, keepdims=True) + EPS)
    n = x * rstd
    return n * g, n, rstd


def _rms_bwd(dy, n, rstd, g):
    dg = jnp.sum(dy * n, axis=0, keepdims=True)
    dn = dy * g
    dx = rstd * (dn - n * jnp.mean(dn * n, axis=-1, keepdims=True))
    return dx, dg


def _norm_bwd(dy, x, g):
    _, n, rstd = _rms(x, g)
    return _rms_bwd(dy, n, rstd, g)


def _colsum(v):
    return jnp.sum(v, axis=0, keepdims=True)


def _silu(x):
    return x * (1.0 / (1.0 + jnp.exp(-x)))


def _lane(shape):
    return lax.broadcasted_iota(jnp.int32, shape, 1)


def _group_mean(v, groups):
    i = lax.broadcasted_iota(jnp.int32, (LANES, LANES), 0)
    j = lax.broadcasted_iota(jnp.int32, (LANES, LANES), 1)
    g = jnp.zeros((LANES, LANES), F32)
    for lo, hi in groups:
        g = jnp.where((i >= lo) & (i < hi) & (j >= lo) & (j < hi), 1.0 / (hi - lo), g)
    head = v.astype(MXU_DTYPE)
    return _dot(head, g) + _dot(v - head.astype(F32), g)


def _in_groups(shape, groups):
    lane = _lane(shape)
    m = jnp.zeros(shape, jnp.bool_)
    for lo, hi in groups:
        m = m | ((lane >= lo) & (lane < hi))
    return m


def _grms(x, g, groups):
    rstd = lax.rsqrt(_group_mean(x * x, groups) + EPS)
    n = jnp.where(_in_groups(x.shape, groups), x * rstd, 0.0)
    return n * g, n, rstd


def _grms_bwd(dy, n, rstd, g, groups):
    dn = dy * g
    return rstd * (dn - n * _group_mean(dn * n, groups)), _colsum(dy * n)


def _rot(x, half, transpose=False):
    first = (_lane(x.shape) % (2 * half)) < half
    up = pltpu.roll(x, LANES - half, axis=1)
    down = pltpu.roll(x, half, axis=1)
    return jnp.where(first, up, -down) if transpose else jnp.where(first, -up, down)


def _rope(x, cos, sin, half):
    return x * cos + _rot(x, half) * sin


def _rope_bwd(dy, cos, sin, half):
    return dy * cos + _rot(dy * sin, half, transpose=True)


def _chunks(x):
    return [x[:, i:i + LANES] for i in range(0, x.shape[1], LANES)]


Q_GROUPS = ((0, NOPE), (NOPE, NOPE + ROPE))
K_GROUPS = ((0, NOPE),)
KPE_GROUPS = ((KPE_LO, KPE_LO + ROPE),)
DIL_GROUPS = ((0, DIL_DIM), (DIL_DIM, 2 * DIL_DIM))


def _col(width, rows=SEQ):
    return pl.BlockSpec((rows, width), lambda h: (0, h))


def _causal_tail(s, tq, fill):
    diag = s[:, s.shape[1] - tq:]
    keep = lax.broadcasted_iota(jnp.int32, diag.shape, 1) <= lax.broadcasted_iota(jnp.int32, diag.shape, 0)
    diag = jnp.where(keep, diag, fill)
    return diag if s.shape[1] == tq else jnp.concatenate([s[:, :s.shape[1] - tq], diag], axis=1)


def mla_fwd(name, q, k, v, scale, tq=256):
    S = q.shape[0]

    def body(q_ref, k_ref, v_ref, o_ref, lse_ref):
        nb = S // tq
        blk = lambda i: slice(i * tq, (i + 1) * tq)

        def scores(i):
            return _dot(q_ref[blk(i), :], k_ref[:(i + 1) * tq, :], "nt")

        def softmax(i, s):
            s = _causal_tail(s * scale, tq, NEG_INF)
            m = jnp.max(s, axis=-1, keepdims=True)
            e = jnp.exp(s - m)
            l = jnp.sum(e, axis=-1, keepdims=True)
            lse_ref[0, blk(i), :] = m + jnp.log(l)
            return (e * (1.0 / l)).astype(MXU_DTYPE)

        def weighted(i, p):
            o_ref[blk(i), :] = _dot(p, v_ref[:(i + 1) * tq, :])

        s, p_prev = scores(0), None
        for i in range(nb):
            s_next = scores(i + 1) if i + 1 < nb else None
            if p_prev is not None:
                weighted(i - 1, p_prev)
            p_prev, s = softmax(i, s), s_next
        weighted(nb - 1, p_prev)

    return pl.pallas_call(
        body, name=name, grid=(HEADS,), in_specs=[_col(LANES)] * 3,
        out_specs=[_col(LANES), pl.BlockSpec((1, S, 1), lambda h: (h, 0, 0))],
        out_shape=[jax.ShapeDtypeStruct((S, MIX_IN), F32), jax.ShapeDtypeStruct((HEADS, S, 1), F32)],
        compiler_params=_params())(q, k, v)


def mla_bwd(name, q, k, v, o, do, lse, scale, tq=256):
    S = q.shape[0]

    def body(q_ref, k_ref, v_ref, o_ref, do_ref, lse_ref, dq_ref, dkv_ref, dkpe_ref, dk_acc, dv_acc):
        dk_acc[...] = jnp.zeros_like(dk_acc)
        dv_acc[...] = jnp.zeros_like(dv_acc)
        for i in range(S // tq):
            kext = (i + 1) * tq
            blk = slice(i * tq, kext)
            qi, kk, vv = q_ref[blk, :], k_ref[:kext, :], v_ref[:kext, :]
            doi = do_ref[blk, :]
            s = _causal_tail(_dot(qi, kk, "nt") * scale, tq, NEG_INF)
            p = jnp.exp(s - lse_ref[0, blk, :])
            dp = _dot(doi, vv, "nt")
            delta = jnp.sum(doi * o_ref[blk, :], axis=-1, keepdims=True)
            ds = p * (dp - delta) * scale
            dq_ref[blk, :] = _dot(ds, kk)
            dk_acc[:kext, :] += _dot(ds, qi, "tn")
            dv_acc[:kext, :] += _dot(p, doi, "tn")
        dk = dk_acc[...]
        lane = _lane(dk.shape)
        dkv_ref[...] = jnp.where(lane < NOPE, dk, 0.0) + dv_acc[...]
        dkpe = jnp.where((lane >= KPE_LO) & (lane < KPE_LO + ROPE), dk, 0.0)
        h = pl.program_id(0)

        @pl.when(h == 0)
        def _():
            dkpe_ref[...] = dkpe

        @pl.when(h > 0)
        def _():
            dkpe_ref[...] += dkpe

    return pl.pallas_call(
        body, name=name, grid=(HEADS,),
        in_specs=[_col(LANES)] * 5 + [pl.BlockSpec((1, S, 1), lambda h: (h, 0, 0))],
        out_specs=[_col(LANES), _col(LANES), pl.BlockSpec((S, LANES), lambda h: (0, 0))],
        out_shape=[jax.ShapeDtypeStruct((S, HEADS * LANES), F32), jax.ShapeDtypeStruct((S, HEADS * LANES), F32),
                   jax.ShapeDtypeStruct((S, LANES), F32)],
        scratch_shapes=[pltpu.VMEM((S, LANES), F32), pltpu.VMEM((S, LANES), F32)],
        compiler_params=_params())(q, k, v, o, do, lse)


BAND_TQ = SPAN


def _band_blocks(L, tq):
    return [(i * tq, (i + 1) * tq, max(0, i * tq - SPAN)) for i in range(L // tq)]


def _class_rows(r, dil, lo, hi):
    return pl.ds(r + dil * lo, hi - lo, stride=dil) if dil > 1 else pl.ds(lo, hi - lo)


def _stack_heads(t, lo):
    zero = jnp.zeros_like(t)
    return jnp.concatenate([jnp.where(lo, t, zero), jnp.where(lo, zero, t)], axis=0)


def _band_mask2(q0, q1, k0):
    n = q1 - q0
    shape = (2 * n, q1 - k0)
    i = lax.broadcasted_iota(jnp.int32, shape, 0)
    dist = (jnp.where(i >= n, i - n, i) + q0) - (lax.broadcasted_iota(jnp.int32, shape, 1) + k0)
    return (dist >= 0) & (dist <= SPAN)


def _pair_col(col0=0):
    return pl.BlockSpec((SEQ, LANES), lambda j: (0, col0 // LANES + j))


def band_fwd(name, q, k, v, dil):
    S = q.shape[0]
    L = S // dil
    tq = BAND_TQ
    scale = DIL_DIM ** -0.5

    def body(q_ref, k_ref, v_ref, o_ref, lse_ref):
        items = [(r, blk) for r in range(dil) for blk in _band_blocks(L, tq)]
        lo = _lane((tq, LANES)) < DIL_DIM

        def scores(item):
            r, (q0, q1, k0) = item
            qb = q_ref[_class_rows(r, dil, q0, q1), :].astype(MXU_DTYPE)
            return _dot(_stack_heads(qb, lo), k_ref[_class_rows(r, dil, k0, q1), :], "nt")

        def softmax(item, s):
            _, (q0, q1, k0) = item
            s = jnp.where(_band_mask2(q0, q1, k0), s * scale, NEG_INF)
            mx = jnp.max(s, axis=-1, keepdims=True)
            e = jnp.exp(s - mx)
            l = jnp.sum(e, axis=-1, keepdims=True)
            return (e * (1.0 / l)).astype(MXU_DTYPE), mx + jnp.log(l)

        def weighted(item, p, lse):
            r, (q0, q1, k0) = item
            pv = _dot(p, v_ref[_class_rows(r, dil, k0, q1), :])
            o_ref[_class_rows(r, dil, q0, q1), :] = jnp.where(lo, pv[:tq], pv[tq:])
            lse_ref[_class_rows(r, dil, q0, q1), :] = jnp.where(lo, lse[:tq], lse[tq:])

        s, prev = scores(items[0]), None
        for i, item in enumerate(items):
            s_next = scores(items[i + 1]) if i + 1 < len(items) else None
            if prev is not None:
                weighted(items[i - 1], *prev)
            prev, s = softmax(item, s), s_next
        weighted(items[-1], *prev)

    return pl.pallas_call(
        body, name=name, grid=(DIL_WIDTH // LANES,), in_specs=[_pair_col()] * 2 + [_pair_col(P_VD)], out_specs=[_pair_col()] * 2,
        out_shape=[jax.ShapeDtypeStruct((S, DIL_WIDTH), F32)] * 2, compiler_params=_params())(q, k, v)


def band_bwd(name, q, k, v, lse, lse_mix, o_cat, do_cat, dil, before=None):
    S = q.shape[0]
    L = S // dil
    tq = BAND_TQ
    scale = DIL_DIM ** -0.5
    before = list(before or [])

    def body(q_ref, k_ref, v_ref, lse_ref, mix_ref, o_ref, do_ref, *rest):
        dq_ref, dk_ref, dv_ref = rest[len(before):]
        if before:
            dq0_ref, dk0_ref, dv0_ref = rest[:3]
            dk_ref[...] = dk0_ref[...]
            dv_ref[...] = dv0_ref[...]
        else:
            dk_ref[...] = jnp.zeros_like(dk_ref)
            dv_ref[...] = jnp.zeros_like(dv_ref)
        items = [(r, blk) for r in range(dil) for blk in _band_blocks(L, tq)]
        lo = _lane((tq, LANES)) < DIL_DIM
        per_head = lambda t: jnp.concatenate([t[:, 0:1], t[:, DIL_DIM:DIL_DIM + 1]], axis=0)

        def scores(item):
            r, (q0, q1, k0) = item
            qrows, krows = _class_rows(r, dil, q0, q1), _class_rows(r, dil, k0, q1)
            lse_p, dout = lse_ref[qrows, :], do_ref[qrows, :]
            w2 = per_head(jnp.exp(lse_p - mix_ref[qrows, :]))
            dd = dout * o_ref[qrows, :]
            big_d = jnp.concatenate([jnp.sum(jnp.where(lo, dd, 0.0), axis=-1, keepdims=True),
                                     jnp.sum(jnp.where(lo, 0.0, dd), axis=-1, keepdims=True)], axis=0)
            q2 = _stack_heads(q_ref[qrows, :].astype(MXU_DTYPE), lo)
            dom = (_stack_heads(dout, lo) * w2).astype(MXU_DTYPE)
            return (_dot(q2, k_ref[krows, :], "nt"), _dot(dom, v_ref[krows, :], "nt"), per_head(lse_p), w2 * big_d, q2, dom)

        def softmax_bwd(item, s, dp, lse2, wd2, q2, dom):
            _, (q0, q1, k0) = item
            p = jnp.where(_band_mask2(q0, q1, k0), jnp.exp(s * scale - lse2), 0.0)
            return p.astype(MXU_DTYPE), (p * (dp - wd2) * scale).astype(MXU_DTYPE), q2, dom

        def grads(item, p, ds, q2, dom):
            r, (q0, q1, k0) = item
            qrows, krows = _class_rows(r, dil, q0, q1), _class_rows(r, dil, k0, q1)
            dq2 = _dot(ds, k_ref[krows, :])
            dq = jnp.where(lo, dq2[:tq], dq2[tq:])
            dq_ref[qrows, :] = dq + dq0_ref[qrows, :] if before else dq
            dk_ref[krows, :] += _dot(ds, q2, "tn")
            dv_ref[krows, :] += _dot(p, dom, "tn")

        sc, prev = scores(items[0]), None
        for i, item in enumerate(items):
            sc_next = scores(items[i + 1]) if i + 1 < len(items) else None
            if prev is not None:
                grads(items[i - 1], *prev)
            prev, sc = softmax_bwd(item, *sc), sc_next
        grads(items[-1], *prev)

    cat = _pair_col(HEADS * LANES)
    return pl.pallas_call(
        body, name=name, grid=(DIL_WIDTH // LANES,),
        in_specs=[_pair_col()] * 2 + [_pair_col(P_VD)] + [_pair_col()] * 2 + [cat] * 2 + [_pair_col()] * len(before),
        out_specs=[_pair_col()] * 3, out_shape=[jax.ShapeDtypeStruct((S, DIL_WIDTH), F32)] * 3,
        compiler_params=_params())(q, k, v, lse, lse_mix, o_cat, do_cat, *before)


def combine_fwd(name, outs, lses, o_cat, tm=512):
    S = outs[0].shape[0]

    def body(o1, o2, o3, l1, l2, l3, cat_in, cat_out, mix_ref):
        ls = [l1[...], l2[...], l3[...]]
        m = jnp.maximum(jnp.maximum(ls[0], ls[1]), ls[2])
        e = [jnp.exp(l - m) for l in ls]
        den = e[0] + e[1] + e[2]
        cat_out[...] = (e[0] / den) * o1[...] + (e[1] / den) * o2[...] + (e[2] / den) * o3[...]
        mix_ref[...] = m + jnp.log(den)

    row = pl.BlockSpec((tm, DIL_WIDTH), lambda i: (i, 0))
    return pl.pallas_call(
        body, name=name, grid=(S // tm,), in_specs=[row] * 6 + [pl.BlockSpec(memory_space=pl.ANY)],
        out_specs=[pl.BlockSpec((tm, DIL_WIDTH), lambda i: (i, HEADS * LANES // DIL_WIDTH)), row],
        out_shape=[jax.ShapeDtypeStruct(o_cat.shape, F32), jax.ShapeDtypeStruct((S, DIL_WIDTH), F32)],
        input_output_aliases={6: 0}, compiler_params=_params())(*outs, *lses, o_cat)


def _shift_down(u, n, zero_head):
    out = pltpu.roll(u, n, axis=0)
    return jnp.where(lax.broadcasted_iota(jnp.int32, u.shape, 0) >= n, out, 0.0) if zero_head else out


def _shift_up(u, n, zero_tail):
    rows = u.shape[0]
    out = pltpu.roll(u, rows - n, axis=0)
    return jnp.where(lax.broadcasted_iota(jnp.int32, u.shape, 0) < rows - n, out, 0.0) if zero_tail else out


CONV_ROWS = 512
CONV_HALO = 16


def _conv_chunks(S, tail):
    out = []
    for r0 in range(0, S, CONV_ROWS):
        lo, hi = max(0, r0 - CONV_HALO), min(S, r0 + CONV_ROWS + (CONV_HALO if tail else 0))
        out.append((lo, hi, r0 - lo, CONV_ROWS))
    return out


CONV_TC = 256
CONV_NB = D_FF // CONV_TC


def _half_specs(rows, rows_axis=False):
    if rows_axis:
        return [pl.BlockSpec((rows, D_MODEL), lambda j: (j, 0)), pl.BlockSpec((rows, D_MODEL), lambda j: (j + CONV_NB, 0))]
    return [pl.BlockSpec((rows, CONV_TC), lambda j: (0, j)), pl.BlockSpec((rows, CONV_TC), lambda j: (0, j + CONV_NB))]


def _whole(a):
    return pl.BlockSpec(a.shape, lambda j: (0,) * a.ndim)


def _up_pair(h, ug_ref, uv_ref):
    return jnp.concatenate([_dot(h, ug_ref[...], "nt"), _dot(h, uv_ref[...], "nt")], axis=1)


def _conv_taps(uin, w, b, starts):
    u1, u2 = _shift_down(uin, 1, starts), _shift_down(uin, 2, starts)
    return u1, u2, w[2:3, :] * uin + w[1:2, :] * u1 + w[0:1, :] * u2 + b


def ffn_fwd(name, h, w_up_t, w_conv, b_conv, w_down):
    S = h.shape[0]

    def body(h_ref, ug_ref, uv_ref, wg_ref, wv_ref, bg_ref, bv_ref, wd_ref, dn_ref, up_ref):
        @pl.when(pl.program_id(0) == 0)
        def _():
            dn_ref[...] = jnp.zeros_like(dn_ref)

        w = jnp.concatenate([wg_ref[...], wv_ref[...]], axis=1)
        b = jnp.concatenate([bg_ref[...], bv_ref[...]], axis=1)
        chunks = _conv_chunks(S, tail=False)

        def project(c):
            lo, hi, keep, rows = c
            uin = _up_pair(h_ref[lo:hi, :], ug_ref, uv_ref)
            up_ref[lo + keep:lo + keep + rows, :] = uin[keep:keep + rows]
            return uin

        def gate(c, uin):
            lo, hi, keep, rows = c
            u = _conv_taps(uin, w, b, lo == 0)[2][keep:keep + rows]
            return (_silu(u[:, :CONV_TC]) * u[:, CONV_TC:]).astype(MXU_DTYPE)

        def project_down(c, act):
            dn_ref[c[0] + c[2]:c[0] + c[2] + c[3], :] += _dot(act, wd_ref[...])

        uin, act_prev = project(chunks[0]), None
        for i, c in enumerate(chunks):
            uin_next = project(chunks[i + 1]) if i + 1 < len(chunks) else None
            if act_prev is not None:
                project_down(chunks[i - 1], act_prev)
            act_prev = gate(c, uin)
            uin = uin_next
        project_down(chunks[-1], act_prev)

    return pl.pallas_call(
        body, name=name, grid=(CONV_NB,),
        in_specs=[_whole(h)] + _half_specs(CONV_TC, rows_axis=True) + _half_specs(3) + _half_specs(1)
        + [pl.BlockSpec((CONV_TC, w_down.shape[1]), lambda j: (j, 0))],
        out_specs=[pl.BlockSpec((S, w_down.shape[1]), lambda j: (0, 0)), pl.BlockSpec((S, 2 * CONV_TC), lambda j: (0, j))],
        out_shape=[jax.ShapeDtypeStruct((S, w_down.shape[1]), F32), jax.ShapeDtypeStruct((S, 2 * D_FF), F32)],
        compiler_params=_params())(h, w_up_t, w_up_t, w_conv, w_conv, b_conv, b_conv, w_down)


def ffn_bwd(name, h, up, w_up_t, w_conv, b_conv, d_dn, w_down):
    S, D = h.shape

    def body(h_ref, up_ref, ug_ref, uv_ref, wg_ref, wv_ref, bg_ref, bv_ref, dd_ref, wd_ref,
             dh_ref, gup_ref, gd_ref, dwg_ref, dwv_ref, dbg_ref, dbv_ref):
        @pl.when(pl.program_id(0) == 0)
        def _():
            dh_ref[...] = jnp.zeros_like(dh_ref)

        w = jnp.concatenate([wg_ref[...], wv_ref[...]], axis=1)
        b = jnp.concatenate([bg_ref[...], bv_ref[...]], axis=1)
        w_pair = jnp.concatenate([ug_ref[...], uv_ref[...]], axis=0)
        chunks = _conv_chunks(S, tail=True)

        def project(c):
            return up_ref[c[0]:c[1], :], _dot(dd_ref[c[0]:c[1], :], wd_ref[...], "nt")

        def through_conv(c, uin, da):
            lo, hi, keep, rows = c
            u1, u2, u = _conv_taps(uin, w, b, lo == 0)
            gate, val = u[:, :CONV_TC], u[:, CONV_TC:]
            sig = 1.0 / (1.0 + jnp.exp(-gate))
            du = jnp.concatenate([da * val * (sig * (1.0 + gate * (1.0 - sig))), da * (gate * sig)], axis=1)
            dup = w[2:3, :] * du + w[1:2, :] * _shift_up(du, 1, hi == S) + w[0:1, :] * _shift_up(du, 2, hi == S)
            kept = slice(keep, keep + rows)
            du = du[kept]
            dw = jnp.concatenate([_colsum(du * u2[kept]), _colsum(du * u1[kept]), _colsum(du * uin[kept])], axis=0)
            return dup[kept].astype(MXU_DTYPE), (gate * sig * val)[kept].astype(MXU_DTYPE), dw, _colsum(du)

        def weight_grads(c, dup, act):
            out_rows = slice(c[0] + c[2], c[0] + c[2] + c[3])
            dh_ref[out_rows, :] += _dot(dup, w_pair)
            return _dot(dup, h_ref[out_rows, :], "tn"), _dot(act, dd_ref[out_rows, :], "tn")

        dw, db, g_up, g_dn = 0.0, 0.0, 0.0, 0.0
        proj, done = project(chunks[0]), None
        for i, c in enumerate(chunks):
            proj_next = project(chunks[i + 1]) if i + 1 < len(chunks) else None
            if done is not None:
                gu, gd = weight_grads(chunks[i - 1], *done)
                g_up, g_dn = g_up + gu, g_dn + gd
            dup, act, dw_c, db_c = through_conv(c, *proj)
            dw, db, done, proj = dw + dw_c, db + db_c, (dup, act), proj_next
        gu, gd = weight_grads(chunks[-1], *done)
        g_up, g_dn = g_up + gu, g_dn + gd
        gup_ref[0], gup_ref[1] = g_up[:CONV_TC].astype(gup_ref.dtype), g_up[CONV_TC:].astype(gup_ref.dtype)
        gd_ref[...] = g_dn.astype(gd_ref.dtype)
        dwg_ref[...], dwv_ref[...] = dw[:, :CONV_TC], dw[:, CONV_TC:]
        dbg_ref[...], dbv_ref[...] = db[:, :CONV_TC], db[:, CONV_TC:]

    half = lambda rows: pl.BlockSpec((rows, CONV_TC), lambda j: (0, j))
    rows_blk = pl.BlockSpec((CONV_TC, D), lambda j: (j, 0))
    dh, gup, gd, dwg, dwv, dbg, dbv = pl.pallas_call(
        body, name=name, grid=(CONV_NB,),
        in_specs=[_whole(h), pl.BlockSpec((S, 2 * CONV_TC), lambda j: (0, j))] + _half_specs(CONV_TC, rows_axis=True) + _half_specs(3)
        + _half_specs(1) + [_whole(d_dn), rows_blk],
        out_specs=[pl.BlockSpec((S, D), lambda j: (0, 0)), pl.BlockSpec((2, CONV_TC, D), lambda j: (0, j, 0)), rows_blk,
                   half(3), half(3), half(1), half(1)],
        out_shape=[jax.ShapeDtypeStruct((S, D), F32), jax.ShapeDtypeStruct((2, D_FF, D), MXU_DTYPE),
                   jax.ShapeDtypeStruct((D_FF, D), MXU_DTYPE)]
        + [jax.ShapeDtypeStruct((3, D_FF), F32)] * 2 + [jax.ShapeDtypeStruct((1, D_FF), F32)] * 2,
        compiler_params=_params())(h, up, w_up_t, w_up_t, w_conv, w_conv, b_conv, b_conv, d_dn, w_down)
    return dh, gup.reshape(2 * D_FF, D), gd, jnp.concatenate([dwg, dwv], axis=1), jnp.concatenate([dbg, dbv], axis=1)


def adamw(name, w, parts, m, v, tr=None):
    apart = w.ndim == 3
    R, C = w.shape[0], w.shape[-1]
    tr = tr or R
    assert R % tr == 0
    c1 = 1.0 - ADAM_B1 ** ADAM_STEP
    c2 = 1.0 - ADAM_B2 ** ADAM_STEP
    np_ = len(parts)

    def body(*refs):
        w_ref, m_ref, v_ref = refs[0], refs[1 + np_], refs[2 + np_]
        go_ref, d_ref, mo_ref, vo_ref = refs[3 + np_:]
        terms = []
        for part, ref in zip(parts, refs[1:1 + np_], strict=True):
            terms += [ref[...]] if part.ndim == 2 else [ref[p] for p in range(part.shape[0])]
        g = terms[0].astype(F32)
        for term in terms[1:]:
            g = g + term.astype(F32)
        m2 = ADAM_B1 * m_ref[...] + (1.0 - ADAM_B1) * g
        v2 = ADAM_B2 * v_ref[...] + (1.0 - ADAM_B2) * (g * g)
        go_ref[...] = g
        mo_ref[...] = m2
        vo_ref[...] = v2
        d_ref[...] = -ADAM_LR * ((m2 / c1) / (jnp.sqrt(v2 / c2) + ADAM_EPS) + ADAM_WD * w_ref[...])

    blk = pl.BlockSpec((tr, C), lambda i: (i, 0))
    own = pl.BlockSpec((tr, None, C), lambda i: (i, 0, 0)) if apart else blk
    part_specs = [blk if p.ndim == 2 else pl.BlockSpec((p.shape[0], tr, C), lambda i: (0, i, 0)) for p in parts]
    return pl.pallas_call(
        body, name=name, grid=(R // tr,),
        in_specs=[own] + part_specs + [own, own], out_specs=[own] * 4,
        out_shape=[jax.ShapeDtypeStruct(w.shape, F32)] * 4, compiler_params=_params())(w, *parts, m, v)


def _place():
    return lax.axis_index("x"), lax.axis_index("y"), lax.axis_index("c")


def all_gather(name, arrs, after=None):
    n = len(arrs)
    deps = [] if after is None else [after]

    def body(*refs):
        ins, outs = refs[:n], refs[n + len(deps):2 * n + len(deps)]
        send_sems, recv_sems, local_sems = refs[2 * n + len(deps):]
        x, y, c = _place()
        me, sibling = (x, y, c), (x, y, 1 - c)
        chips = [(1 - x, y), (x, 1 - y), (1 - x, 1 - y)]
        sends = []
        for t in range(n):
            out = outs[t]

            def slot(px, py, pc, out=out):
                return out.at[4 * px + 2 * py + pc]

            def copy(k, block, to, src=None, t=t, slot=slot):
                return pltpu.make_async_remote_copy(
                    src_ref=slot(*block) if src is None else src, dst_ref=slot(*block),
                    send_sem=send_sems.at[7 * t + k], recv_sem=recv_sems.at[7 * t + k],
                    device_id=to, device_id_type=MESH_ID)

            mine = pltpu.make_async_copy(ins[t], slot(*me), local_sems.at[t])
            mine.start()
            first = [copy(0, me, sibling, src=ins[t])]
            first += [copy(1 + j, me, (*chip, c), src=ins[t]) for j, chip in enumerate(chips)]
            for cp in first:
                cp.start()
            sends.append((mine, first, copy))
        for t in range(n):
            mine, first, copy = sends[t]
            passed = [copy(4 + j, (*chip, c), sibling) for j, chip in enumerate(chips)]
            for j, chip in enumerate(chips):
                copy(1 + j, (*chip, c), me).wait_recv()
                passed[j].start()
            copy(0, sibling, me).wait_recv()
            for j, chip in enumerate(chips):
                copy(4 + j, (*chip, 1 - c), me).wait_recv()
            for cp in first + passed:
                cp.wait_send()
            mine.wait()

    any_spec = pl.BlockSpec(memory_space=pl.ANY)
    res = pl.pallas_call(
        body, name=name, in_specs=[any_spec] * (n + len(deps)), out_specs=[any_spec] * n,
        out_shape=[jax.ShapeDtypeStruct((N_DEV,) + a.shape, a.dtype) for a in arrs],
        scratch_shapes=[pltpu.SemaphoreType.DMA((7 * n,)), pltpu.SemaphoreType.DMA((7 * n,)), pltpu.SemaphoreType.DMA((n,))],
        compiler_params=pltpu.CompilerParams(has_side_effects=True))(*arrs, *deps)
    return list(res)


def ada_modulation(name, c, w_ada):
    n_mod = w_ada.shape[1]

    def exchange(src_ref, dst_ref, send_sems, recv_sems):
        x, y, c_ = _place()
        me = 4 * x + 2 * y + c_
        copies = []
        for k in range(1, N_DEV):
            px, py, pc = x ^ (k >> 2), y ^ ((k >> 1) & 1), c_ ^ (k & 1)
            copies.append(pltpu.make_async_remote_copy(
                src_ref=src_ref, dst_ref=dst_ref.at[me], send_sem=send_sems.at[k - 1], recv_sem=recv_sems.at[k - 1],
                device_id=(px, py, pc), device_id_type=MESH_ID))
        for cp in copies:
            cp.start()
        for cp in copies:
            cp.wait_recv()
        for cp in copies:
            cp.wait_send()
        return me

    def body(c_ref, w_ref, sc_ref, mod_ref, c_all, send_c, recv_c, send_m, recv_m):
        me = exchange(c_ref, c_all, send_c, recv_c)
        c_all[me] = c_ref[...]
        sc = _silu(jnp.concatenate([c_all[p] for p in range(N_DEV)], axis=0))
        sc_ref[...] = sc.astype(sc_ref.dtype)
        mod_ref[me] = _dot(sc, w_ref[...])
        exchange(mod_ref.at[me], mod_ref, send_m, recv_m)

    vmem = pl.BlockSpec(memory_space=pltpu.VMEM)
    return pl.pallas_call(
        body, name=name, in_specs=[vmem, vmem], out_specs=[vmem, vmem],
        out_shape=[jax.ShapeDtypeStruct((N_DEV, c.shape[1]), MXU_DTYPE), jax.ShapeDtypeStruct((N_DEV, N_DEV, n_mod), F32)],
        scratch_shapes=[pltpu.VMEM((N_DEV, 1, c.shape[1]), F32)] + [pltpu.SemaphoreType.DMA((N_DEV - 1,))] * 4,
        compiler_params=pltpu.CompilerParams(has_side_effects=True, vmem_limit_bytes=VMEM_LIMIT))(c, w_ada)


HBM_SPEC = pl.BlockSpec(memory_space=pltpu.HBM)
SEM_SPEC = pl.BlockSpec(memory_space=pltpu.SEMAPHORE)
DATAFLOW = pltpu.SideEffectType.DATAFLOW_SIDE_EFFECTING


def _exchange_copies(srcs, lands, send_sems, recv_sems, gather, first=0):
    x, y, c = _place()
    me = 4 * x + 2 * y + c
    out = []
    for t, (src, land) in enumerate(zip(srcs, lands, strict=True)):
        for k in range(1, N_DEV):
            px, py, pc = x ^ (k >> 2), y ^ ((k >> 1) & 1), c ^ (k & 1)
            sem = 7 * (first + t) + k - 1
            out.append(pltpu.make_async_remote_copy(
                src_ref=src if gather else src.at[4 * px + 2 * py + pc],
                dst_ref=land.at[me] if gather else land.at[k - 1],
                send_sem=send_sems.at[sem], recv_sem=recv_sems.at[sem],
                device_id=(px, py, pc), device_id_type=MESH_ID))
    return out


def exchange_start(name, arrs, gather, after=None):
    n = len(arrs)
    lands = [lax.empty(((N_DEV,) + a.shape) if gather else ((N_DEV - 1,) + a.shape[1:]), a.dtype) for a in arrs]
    deps = [] if after is None else [after]

    def body(*refs):
        srcs, land_refs = refs[:n], refs[n:2 * n]
        send_sems, recv_sems = refs[2 * n + len(deps)], refs[2 * n + len(deps) + 1]
        token = refs[-1]
        for cp in _exchange_copies(srcs, land_refs, send_sems, recv_sems, gather):
            cp.start()
        token[...] = jnp.zeros_like(token)

    hbm = lambda a: pltpu.HBM(a.shape, a.dtype)
    res = pl.pallas_call(
        body, name=name,
        out_shape=(pltpu.SemaphoreType.DMA((7 * n,)), pltpu.SemaphoreType.DMA((7 * n,)), *[hbm(a) for a in arrs],
                   *[hbm(l) for l in lands], jax.ShapeDtypeStruct((8, 128), F32)),
        in_specs=[HBM_SPEC] * (2 * n) + [pl.BlockSpec(memory_space=pl.ANY)] * len(deps),
        out_specs=(SEM_SPEC, SEM_SPEC, *[HBM_SPEC] * (2 * n), pl.BlockSpec(memory_space=pltpu.VMEM)),
        input_output_aliases={i: 2 + i for i in range(2 * n)},
        compiler_params=pltpu.CompilerParams(has_side_effects=DATAFLOW),
    )(*[pltpu.with_memory_space_constraint(a, pltpu.HBM) for a in arrs + lands], *deps)
    return res[0], res[1], list(res[2:2 + n]), list(res[2 + n:2 + 2 * n]), res[-1]


def exchange_wait(name, started, gather, after, first=0, count=None):
    send_sems, recv_sems, srcs, lands, _ = started
    count = len(srcs) - first if count is None else count
    srcs, lands = srcs[first:first + count], lands[first:first + count]
    n = len(srcs)

    def body(*refs):
        src_refs, land_refs = refs[:n], refs[n:2 * n]
        copies = _exchange_copies(src_refs, land_refs, refs[2 * n], refs[2 * n + 1], gather, first)
        for cp in copies:
            cp.wait_send()
        for cp in copies:
            cp.wait_recv()

    hbm = lambda a: pltpu.HBM(a.shape, a.dtype)
    res = pl.pallas_call(
        body, name=name, out_shape=tuple(hbm(a) for a in srcs + lands),
        in_specs=[HBM_SPEC] * (2 * n) + [SEM_SPEC, SEM_SPEC, pl.BlockSpec(memory_space=pl.ANY)],
        out_specs=tuple([HBM_SPEC] * (2 * n)), input_output_aliases={i: i for i in range(2 * n)},
        compiler_params=pltpu.CompilerParams(has_side_effects=DATAFLOW),
    )(*srcs, *lands, send_sems, recv_sems, after)
    return list(res[:n]), list(res[n:])


def _gather_cols(stack):
    p, k, n = stack.shape
    return stack.transpose(1, 0, 2).reshape(k, p * n)


def _scatter_cols(full):
    k, n = full.shape
    return full.reshape(k, N_DEV, n // N_DEV).transpose(1, 0, 2)


def _gather_rows(stack):
    p, r, n = stack.shape
    return stack.reshape(p * r, n)


def _scatter_rows(full):
    r, n = full.shape
    return full.reshape(N_DEV, r // N_DEV, n)


_IN_NAT = Q_LORA + KV_LORA
TRANSPOSED = ("w_in", "w_q_b", "w_up")
ROWS_APART = ("w_in", "w_conv")


def to_kernel_layout(name, w):
    if name == "w_in":
        z = lambda n: jnp.zeros((n, w.shape[1]), w.dtype)
        return jnp.concatenate([w[:_IN_NAT], z(KPE_LO), w[_IN_NAT:_IN_NAT + ROPE], z(LANES - KPE_LO - ROPE), w[_IN_NAT + ROPE:]], axis=0)
    if name == "w_q_b":
        return jnp.pad(w.reshape(HEADS, NOPE + ROPE, -1), ((0, 0), (0, LANES - NOPE - ROPE), (0, 0))).reshape(HEADS * LANES, -1)
    if name == "w_o":
        mla = jnp.pad(w[:HEADS * NOPE].reshape(HEADS, NOPE, -1), ((0, 0), (LANES - NOPE, 0), (0, 0))).reshape(HEADS * LANES, -1)
        return jnp.concatenate([mla, w[HEADS * NOPE:]], axis=0)
    return w


def from_kernel_layout(name, g):
    if name == "w_in":
        return jnp.concatenate([g[:_IN_NAT], g[P_KPE + KPE_LO:P_KPE + KPE_LO + ROPE], g[P_QD:]], axis=0)
    if name == "w_q_b":
        return g.reshape(HEADS, LANES, -1)[:, :NOPE + ROPE, :].reshape(HEADS * (NOPE + ROPE), -1)
    if name == "w_o":
        mla = g[:HEADS * LANES].reshape(HEADS, LANES, -1)[:, LANES - NOPE:, :].reshape(HEADS * NOPE, -1)
        return jnp.concatenate([mla, g[HEADS * LANES:]], axis=0)
    return g


SMALL_COLS = 1024
SMALL_ROWS = 24
SMALL_AT = {"loss": (0, 0, 1), "b_ada": (1, 0, 6 * D_MODEL), "g_mix_norm": (7, 0, D_MODEL), "g_q_lat": (8, 0, Q_LORA),
            "g_kv_lat": (9, 0, KV_LORA), "g_mla_q_nope": (10, 0, NOPE), "g_mla_q_pe": (10, 128, ROPE),
            "g_mla_k_nope": (10, 256, NOPE), "g_mla_k_pe": (10, 384, ROPE), "g_dil_q": (10, 512, DIL_DIM),
            "g_dil_k": (10, 640, DIL_DIM), "g_ffn_norm": (11, 0, D_MODEL), "b_conv": (12, 0, 2 * D_FF)}
SMALL_PARAMS = tuple(n for n in SMALL_AT if n != "loss")


def _pack_small(values):
    by_row = {}
    for name, (row, off, n) in SMALL_AT.items():
        by_row.setdefault(row, []).append((off, values[name].reshape(-1).astype(F32)))
    out = []
    for row in sorted(by_row):
        pieces, at = [], 0
        for off, v in sorted(by_row[row], key=lambda t: t[0]):
            pieces += [jnp.zeros((off - at,), F32), v]
            at = off + v.shape[0]
        flat = jnp.concatenate(pieces)
        nrows = -(-flat.shape[0] // SMALL_COLS)
        out.append(jnp.pad(flat, (0, nrows * SMALL_COLS - flat.shape[0])).reshape(nrows, SMALL_COLS))
    packed = jnp.concatenate(out, axis=0)
    return jnp.pad(packed, ((0, SMALL_ROWS - packed.shape[0]), (0, 0)))


def _adam(w, g, m, v):
    c1 = 1.0 - ADAM_B1 ** ADAM_STEP
    c2 = 1.0 - ADAM_B2 ** ADAM_STEP
    m2 = ADAM_B1 * m + (1.0 - ADAM_B1) * g
    v2 = ADAM_B2 * v + (1.0 - ADAM_B2) * (g * g)
    return -ADAM_LR * ((m2 / c1) / (jnp.sqrt(v2 / c2) + ADAM_EPS) + ADAM_WD * w), m2, v2


def adamw_small(name, stack, params):
    flat = [a for n in SMALL_PARAMS for a in params[n]]

    def body(stack_ref, *refs):
        ins, outs = refs[:len(flat)], refs[len(flat):]
        g_all = stack_ref[0]
        for p in range(1, N_DEV):
            g_all = g_all + stack_ref[p]
        outs[0][...] = g_all[0:1, 0:1]
        for i, pname in enumerate(SMALL_PARAMS):
            row, off, n = SMALL_AT[pname]
            w_ref, m_ref, v_ref = ins[3 * i:3 * i + 3]
            go_ref, d_ref, mo_ref, vo_ref = outs[1 + 4 * i:5 + 4 * i]
            for c0 in range(0, n, SMALL_COLS):
                cn = min(SMALL_COLS, n - c0)
                r = row + c0 // SMALL_COLS
                g = g_all[r:r + 1, off:off + cn]
                cols = (slice(None), slice(c0, c0 + cn))
                d, m2, v2 = _adam(w_ref[cols], g, m_ref[cols], v_ref[cols])
                go_ref[cols], d_ref[cols], mo_ref[cols], vo_ref[cols] = g, d, m2, v2

    whole = lambda a: pl.BlockSpec(a.shape, lambda: (0,) * a.ndim)
    out_shape = [jax.ShapeDtypeStruct((1, 1), F32)] + [jax.ShapeDtypeStruct(a.shape, F32) for n in SMALL_PARAMS for a in params[n][:1] * 4]
    res = pl.pallas_call(body, name=name, in_specs=[whole(stack)] + [whole(a) for a in flat],
                         out_specs=[pl.BlockSpec(s.shape, lambda s=s: (0,) * len(s.shape)) for s in out_shape],
                         out_shape=out_shape, compiler_params=_params())(stack, *flat)
    return res[0], {n: res[1 + 4 * i:5 + 4 * i] for i, n in enumerate(SMALL_PARAMS)}


def _local_step(x, pos, mod, target, w, fetch, emit):
    S = SEQ
    sh1, sc1, g1, sh2, sc2, g2 = [mod[:, i * D_MODEL:(i + 1) * D_MODEL] for i in range(6)]
    zeros = lambda n: jnp.zeros((1, n), F32)
    g_q = jnp.concatenate([w["g_mla_q_nope"], w["g_mla_q_pe"], zeros(LANES - NOPE - ROPE)], axis=1)
    g_k = jnp.concatenate([w["g_mla_k_nope"], zeros(LANES - NOPE)], axis=1)
    g_kpe = jnp.concatenate([zeros(KPE_LO), w["g_mla_k_pe"], zeros(LANES - KPE_LO - ROPE)], axis=1)
    g_dq = jnp.concatenate([w["g_dil_q"]] * 2, axis=1)
    g_dk = jnp.concatenate([w["g_dil_k"]] * 2, axis=1)
    b_conv = w["b_conv"]

    def inv_freq(d):
        return jnp.power(ROPE_THETA, -2.0 * jnp.arange(d // 2, dtype=F32) / d)

    n_m, n_d = ROPE // 2, DIL_DIM // 2
    freqs = jnp.concatenate([inv_freq(ROPE), inv_freq(DIL_DIM), jnp.zeros((LANES - n_m - n_d,), F32)]).reshape(1, LANES)

    def tables_fn(rows, params):
        (p,), (f,) = rows, params
        c, s = jnp.cos(p * f), jnp.sin(p * f)
        one, zero = jnp.ones_like(c), jnp.zeros_like(c)
        mla = lambda t, fill: jnp.concatenate([fill[:, :KPE_LO], t[:, :n_m], t[:, :n_m], fill[:, :LANES - KPE_LO - ROPE]], axis=1)
        dil = lambda t: jnp.concatenate([t[:, n_m:n_m + n_d]] * 4, axis=1)
        return [mla(c, one), mla(s, zero), dil(c), dil(s)], []

    cos_m, sin_m, cos_d, sin_d = rowwise("rope_tables", tables_fn, [pos], [freqs], [(LANES, F32)] * 4)
    tables = [cos_m, sin_m, cos_d, sin_d]
    H_M, H_D = ROPE // 2, DIL_DIM // 2

    def ln1_fn(rows, params):
        (xv,), (g, sc, sh) = rows, params
        y, _, _ = _rms(xv, g)
        return [y * (1.0 + sc) + sh], []

    (h,) = rowwise("ln1_fwd", ln1_fn, [x], [w["g_mix_norm"], sc1, sh1], [(D_MODEL, MXU_DTYPE)])
    w_in = fetch("w_in", h)
    proj = matmul("proj_fwd", h, w_in, "nt", tm=512)

    def post_fn(rows, params):
        (pv, cm, sm, cd, sd), (gq, gkv, gkp, gdq, gdk) = rows, params
        kper = _rope(_grms(pv[:, P_KPE:P_QD], gkp, KPE_GROUPS)[0], cm, sm, H_M)
        qd = [_rope(_grms(c, gdq, DIL_GROUPS)[0], cd, sd, H_D) for c in _chunks(pv[:, P_QD:P_KD])]
        kd = [_rope(_grms(c, gdk, DIL_GROUPS)[0], cd, sd, H_D) for c in _chunks(pv[:, P_KD:P_VD])]
        return [_rms(pv[:, P_QLAT:P_KVLAT], gq)[0], _rms(pv[:, P_KVLAT:P_KPE], gkv)[0], kper,
                jnp.concatenate(qd, axis=1), jnp.concatenate(kd, axis=1)], []

    post_params = [w["g_q_lat"], w["g_kv_lat"], g_kpe, g_dq, g_dk]
    qln, kvn, kper, qd_r, kd_r = rowwise(
        "proj_post", post_fn, [proj] + tables, post_params,
        [(Q_LORA, MXU_DTYPE), (KV_LORA, MXU_DTYPE), (LANES, MXU_DTYPE)] + [(DIL_WIDTH, F32)] * 2, tm=256)
    w_q_b, w_kv_b = fetch("w_q_b", qln), fetch("w_kv_b", kvn)
    q = matmul("q_fwd", qln, w_q_b, "nt", tm=1024)
    kv = matmul("kv_fwd", kvn, w_kv_b, "nn", tm=1024)

    def mla_prep_fn(rows, params):
        (qv, kvv, kp, cm, sm), (gq, gk) = rows, params
        value_lanes = _lane(kp.shape) >= NOPE
        qs, ks, vs = [], [], []
        for qc, kc in zip(_chunks(qv), _chunks(kvv), strict=True):
            qs.append(_rope(_grms(qc, gq, Q_GROUPS)[0], cm, sm, H_M))
            ks.append(_grms(kc, gk, K_GROUPS)[0] + kp)
            vs.append(jnp.where(value_lanes, kc, 0.0))
        return [jnp.concatenate(t, axis=1) for t in (qs, ks, vs)], []

    q_mla, k_mla, v_mla = rowwise("mla_prep", mla_prep_fn, [q, kv, kper, cos_m, sin_m], [g_q, g_k],
                                  [(HEADS * LANES, MXU_DTYPE)] * 3, tm=256)
    mla_scale = (NOPE + ROPE) ** -0.5
    o_cat, lse_mla = mla_fwd("mla_fwd", q_mla, k_mla, v_mla, mla_scale)

    band = [band_fwd(f"band{dil}_fwd", qd_r, kd_r, proj, dil) for dil in DILATIONS]
    o_cat, lse_mix = combine_fwd("dil_combine", [b[0] for b in band], [b[1] for b in band], o_cat)
    w_o = fetch("w_o", o_cat)
    mix = matmul("mix_fwd", o_cat, w_o, "nn", tm=512)

    def mid_fn(rows, params):
        (xv, mx), (gate1, g, sc, sh) = rows, params
        x1 = xv + gate1 * mx
        y, _, _ = _rms(x1, g)
        return [x1, y * (1.0 + sc) + sh], []

    x1, h2 = rowwise("mid_fwd", mid_fn, [x, mix], [g1, w["g_ffn_norm"], sc2, sh2], [(D_MODEL, F32), (D_MODEL, MXU_DTYPE)])
    w_up, w_conv, w_down = fetch("w_up", h2), fetch("w_conv", h2), fetch("w_down", h2)
    dn, up = ffn_fwd("ffn_fwd", h2, w_up, w_conv, b_conv, w_down)

    def final_fn(rows, params):
        (x1v, dnv, tgt), (gate2,) = rows, params
        r = x1v + gate2 * dnv - tgt
        dy = r * (1.0 / D_MODEL)
        loss = jnp.sum(_colsum(r * r), axis=-1, keepdims=True) * (0.5 / D_MODEL)
        return [dy, gate2 * dy], [loss, _colsum(dy * dnv)]

    dy, d_dn, loss, dg2 = rowwise("loss_head", final_fn, [x1, dn, target], [g2], [(D_MODEL, F32), (D_MODEL, MXU_DTYPE)],
                                  [1, D_MODEL])
    dh2, g_up, g_down, g_w_conv, g_b_conv = ffn_bwd("ffn_bwd", h2, up, w_up, w_conv, b_conv, d_dn, w_down)
    emit("w_down", g_down)
    emit("w_conv", g_w_conv)
    sent = emit("w_up", g_up)

    def mid_bwd_fn(rows, params):
        (dh2v, dyv, x1v, mx), (gate1, g, sc) = rows, params
        yn, n, rstd = _rms(x1v, g)
        dx_n, dg = _rms_bwd(dh2v * (1.0 + sc), n, rstd, g)
        dx1 = dyv + dx_n
        return [dx1, gate1 * dx1], [dg, _colsum(dh2v * yn), _colsum(dh2v), _colsum(dx1 * mx)]

    dx1, dmix, dg_ffn, dsc2, dsh2, dg1 = rowwise(
        "mid_bwd", mid_bwd_fn, [dh2, dy, x1, mix], [g1, w["g_ffn_norm"], sc2], [(D_MODEL, F32), (D_MODEL, MXU_DTYPE)],
        [D_MODEL] * 4, dep=sent)

    sent = emit("w_o", matmul("mix_wgrad", o_cat, dmix, "tn", tm=512, out_dtype=MXU_DTYPE))
    do_cat = matmul("mix_dgrad", dmix, w_o, "nt", tm=512, dep=sent)
    dband = None
    for dil, b in zip(DILATIONS, band):
        dband = band_bwd(f"band{dil}_bwd", qd_r, kd_r, proj, b[1], lse_mix, o_cat, do_cat, dil, before=dband)
    dq_mla, dkv_mla, dkper = mla_bwd("mla_bwd", q_mla, k_mla, v_mla, o_cat, do_cat, lse_mla, mla_scale)

    def mla_prep_bwd_fn(rows, params):
        (dqv, dkvv, qv, kvv, cm, sm), (gq, gk) = rows, params
        nope_lanes = _lane(cm.shape) < NOPE
        dqs, dkvs, dgq, dgk = [], [], 0.0, 0.0
        for dqc, dkc, qc, kc in zip(_chunks(dqv), _chunks(dkvv), _chunks(qv), _chunks(kvv), strict=True):
            _, n, rstd = _grms(qc, gq, Q_GROUPS)
            dx, dg = _grms_bwd(_rope_bwd(dqc, cm, sm, H_M), n, rstd, gq, Q_GROUPS)
            dqs.append(dx)
            dgq = dgq + dg
            _, n, rstd = _grms(kc, gk, K_GROUPS)
            dx, dg = _grms_bwd(dkc, n, rstd, gk, K_GROUPS)
            dkvs.append(jnp.where(nope_lanes, dx, dkc))
            dgk = dgk + dg
        return [jnp.concatenate(dqs, axis=1), jnp.concatenate(dkvs, axis=1)], [dgq, dgk]

    dq, dkv, dg_q, dg_k = rowwise("mla_prep_bwd", mla_prep_bwd_fn, [dq_mla, dkv_mla, q, kv, cos_m, sin_m], [g_q, g_k],
                                  [(HEADS * LANES, MXU_DTYPE)] * 2, [LANES, LANES], tm=256)
    emit("w_q_b", matmul("q_wgrad", dq, qln, "tn", out_dtype=MXU_DTYPE))
    emit("w_kv_b", matmul("kv_wgrad", kvn, dkv, "tn", out_dtype=MXU_DTYPE))
    dqln = matmul("q_dgrad", dq, w_q_b, "nn", tm=1024)
    dkvn = matmul("kv_dgrad", dkv, w_kv_b, "nt", tm=1024)

    def pre_bwd_fn(rows, params):
        dql, dkvl, dkp, dqd_, dkd_, dvd_, pv, cm, sm, cd, sd = rows
        gq, gkv, gkp, gdq, gdk = params
        r_q = _norm_bwd(dql, pv[:, P_QLAT:P_KVLAT], gq)
        r_kv = _norm_bwd(dkvl, pv[:, P_KVLAT:P_KPE], gkv)
        _, n, rstd = _grms(pv[:, P_KPE:P_QD], gkp, KPE_GROUPS)
        r_kp = _grms_bwd(_rope_bwd(dkp, cm, sm, H_M), n, rstd, gkp, KPE_GROUPS)
        outs, dgs = [r_q[0], r_kv[0], r_kp[0]], []
        for dval, lo, g in ((dqd_, P_QD, gdq), (dkd_, P_KD, gdk)):
            dg_sum = 0.0
            for dc, xc in zip(_chunks(dval), _chunks(pv[:, lo:lo + DIL_WIDTH]), strict=True):
                _, n, rstd = _grms(xc, g, DIL_GROUPS)
                dx, dg = _grms_bwd(_rope_bwd(dc, cd, sd, H_D), n, rstd, g, DIL_GROUPS)
                outs.append(dx)
                dg_sum = dg_sum + dg
            dgs.append(dg_sum)
        return [jnp.concatenate(outs + [dvd_], axis=1)], [r_q[1], r_kv[1], r_kp[1]] + dgs

    dproj, dg_q_lat, dg_kv_lat, dg_kpe, dg_dq, dg_dk = rowwise(
        "proj_pre_bwd", pre_bwd_fn,
        [dqln, dkvn, dkper] + list(dband) + [proj] + tables, post_params,
        [(P_END, MXU_DTYPE)], [Q_LORA, KV_LORA, LANES, LANES, LANES], tm=256)
    sent = emit("w_in", matmul("proj_wgrad", dproj, h, "tn", tn=512, out_dtype=MXU_DTYPE))
    dh = matmul("proj_dgrad", dproj, w_in, "nn", tm=512, dep=sent)

    def ln1_bwd_fn(rows, params):
        (dhv, dres, xv), (g, sc) = rows, params
        yn, n, rstd = _rms(xv, g)
        dx_n, dg = _rms_bwd(dhv * (1.0 + sc), n, rstd, g)
        return [dres + dx_n], [dg, _colsum(dhv * yn), _colsum(dhv)]

    grad_x, dg_mix, dsc1, dsh1 = rowwise("ln1_bwd", ln1_bwd_fn, [dh, dx1, x], [w["g_mix_norm"], sc1], [(D_MODEL, F32)],
                                         [D_MODEL] * 3)
    dmod = jnp.concatenate([dsh1, dsc1, dg1, dsh2, dsc2, dg2], axis=-1)
    small = {"loss": loss, "b_ada": dmod, "g_mix_norm": dg_mix, "g_q_lat": dg_q_lat, "g_kv_lat": dg_kv_lat,
             "g_mla_q_nope": dg_q[:, :NOPE], "g_mla_q_pe": dg_q[:, NOPE:NOPE + ROPE], "g_mla_k_nope": dg_k[:, :NOPE],
             "g_mla_k_pe": dg_kpe[:, KPE_LO:KPE_LO + ROPE], "g_dil_q": dg_dq[:, :DIL_DIM] + dg_dq[:, DIL_DIM:],
             "g_dil_k": dg_dk[:, :DIL_DIM] + dg_dk[:, DIL_DIM:], "g_ffn_norm": dg_ffn,
             "b_conv": g_b_conv}
    return grad_x, small


COL_SHARDED = ("w_kv_b", "w_conv")
ROW_SHARDED = ("w_o", "w_down") + TRANSPOSED
ADAM_TILE = {"w_ada": 256, "w_up": 176, "w_down": 176}
GATHER_GROUPS = (("w_in",), ("w_q_b", "w_kv_b"), ("w_o",), ("w_up", "w_conv", "w_down"))
SCATTER_GROUPS = (("w_down", "w_conv", "w_up"), ("w_o",), ("w_q_b", "w_kv_b", "w_in"))
OUT_WEIGHTS = ("w_ada", "b_ada", "g_mix_norm", "w_in", "g_q_lat", "w_q_b", "g_kv_lat", "w_kv_b", "g_mla_q_nope", "g_mla_q_pe",
               "g_mla_k_nope", "g_mla_k_pe", "g_dil_q", "g_dil_k", "w_o", "g_ffn_norm", "w_up", "w_conv", "b_conv", "w_down")


def kernel(x, c, positions, w_ada, b_ada, g_mix_norm, w_in, g_q_lat, w_q_b, g_kv_lat, w_kv_b, g_mla_q_nope, g_mla_q_pe, g_mla_k_nope, g_mla_k_pe, g_dil_q, g_dil_k, w_o, g_ffn_norm, w_up, w_conv, b_conv, w_down, loss_target, m_w_ada, m_b_ada, m_g_mix_norm, m_w_in, m_g_q_lat, m_w_q_b, m_g_kv_lat, m_w_kv_b, m_g_mla_q_nope, m_g_mla_q_pe, m_g_mla_k_nope, m_g_mla_k_pe, m_g_dil_q, m_g_dil_k, m_w_o, m_g_ffn_norm, m_w_up, m_w_conv, m_b_conv, m_w_down, v_w_ada, v_b_ada, v_g_mix_norm, v_w_in, v_g_q_lat, v_w_q_b, v_g_kv_lat, v_w_kv_b, v_g_mla_q_nope, v_g_mla_q_pe, v_g_mla_k_nope, v_g_mla_k_pe, v_g_dil_q, v_g_dil_k, v_w_o, v_g_ffn_norm, v_w_up, v_w_conv, v_b_conv, v_w_down):
    args = dict(locals())
    xi, yi, ci = _place()
    me = 4 * xi + 2 * yi + ci
    def local(prefix, n):
        a = args[prefix + n]
        if n in ROWS_APART:
            return jnp.transpose(a, (2, 0, 1) if n in TRANSPOSED else (1, 0, 2))
        return a[0].T if n in TRANSPOSED else a[0]

    def as_output(n, r):
        if n in ROWS_APART:
            return jnp.transpose(r, (1, 2, 0) if n in TRANSPOSED else (1, 0, 2))
        return (r.T if n in TRANSPOSED else r)[None]

    shard = {n: local("", n) for n in COL_SHARDED + ROW_SHARDED + ("w_ada",)}
    flat = lambda n, a: a.reshape(a.shape[0], a.shape[-1]) if n in ROWS_APART else a
    small_w = {n: args[n] for n in SMALL_PARAMS}

    sc_all, mod_all = ada_modulation("ada_mod", c, shard["w_ada"])

    payload = {n: flat(n, shard[n]) if n == "w_conv" else flat(n, shard[n]).astype(MXU_DTYPE) for n in COL_SHARDED + ROW_SHARDED}
    gather_order = [n for grp in GATHER_GROUPS for n in grp]
    gathered = exchange_start("gather_start", [payload[n] for n in gather_order], gather=True, after=mod_all)
    after_start = gathered[-1]
    full = {}

    def fetch(name, after):
        if name not in full:
            (i, grp), = [(i, grp) for i, grp in enumerate(GATHER_GROUPS) if name in grp]
            srcs, lands = exchange_wait(f"gather{i}_wait", gathered, True, after, gather_order.index(grp[0]), len(grp))
            for n, src, land in zip(grp, srcs, lands, strict=True):
                stack = lax.dynamic_update_index_in_dim(land, src, me, 0)
                full[n] = to_kernel_layout(n, _gather_cols(stack) if n in COL_SHARDED else _gather_rows(stack))
        return full[name]

    mod_row = lax.dynamic_index_in_dim(mod_all, me, axis=1, keepdims=False).reshape(1, 6 * D_MODEL)
    (mod,) = rowwise("ada_bias", lambda rows, params: ([rows[0] + rows[1]], []), [mod_row, b_ada], [], [(6 * D_MODEL, F32)],
                     dep=after_start)

    own, pending, scatters = {}, {}, {}

    def emit(name, grad):
        grad = from_kernel_layout(name, grad)
        parts = _scatter_cols(grad) if name in COL_SHARDED else _scatter_rows(grad)
        own[name] = lax.dynamic_index_in_dim(parts, me, 0, keepdims=False)
        pending[name] = parts
        for i, grp in enumerate(SCATTER_GROUPS):
            if name == grp[-1]:
                scatters[i] = exchange_start(f"scatter{i}_start", [pending[n] for n in grp], gather=False)
                return scatters[i][-1]
        return None

    pos = positions.reshape(SEQ, 1).astype(F32)
    grad_x, small = _local_step(x[0], pos, mod, loss_target[0], small_w, fetch, emit)

    res, done = {}, grad_x
    for i, grp in enumerate(SCATTER_GROUPS):
        _, lands = exchange_wait(f"scatter{i}_wait", scatters[i], False, done)
        for n, land in zip(grp, lands, strict=True):
            res[n] = adamw(f"adamw_{n}", shard[n], [own[n], land], local("m_", n), local("v_", n), ADAM_TILE.get(n))
            done = res[n][0]
            res[n] = [as_output(n, r) for r in res[n]]
    (small_all,) = all_gather("gather_small", [_pack_small(small)], after=done)
    loss, small_res = adamw_small("adamw_small", small_all, {n: (args[n], args["m_" + n], args["v_" + n]) for n in SMALL_PARAMS})
    row, _, n_mod = SMALL_AT["b_ada"]
    dmod_all = small_all[:, row:row + n_mod // SMALL_COLS, :].reshape(N_DEV, n_mod)
    dmod_mine = lax.dynamic_slice_in_dim(dmod_all, me * (6 * D_MODEL // N_DEV), 6 * D_MODEL // N_DEV, axis=1)
    g_w_ada = matmul("ada_wgrad", sc_all, dmod_mine, "tn")
    res["w_ada"] = [r[None] for r in adamw("adamw_w_ada", shard["w_ada"], [g_w_ada], m_w_ada[0], v_w_ada[0], ADAM_TILE["w_ada"])]

    def leaf(kind, n):
        return res[n][kind] if n in res else small_res[n][kind]

    return (loss.reshape(()), grad_x[None], *[leaf(k, n) for k in range(4) for n in OUT_WEIGHTS])
```

```python
import jax
import jax.numpy as jnp
from jax import lax
from jax.experimental import pallas as pl
from jax.experimental.pallas import tpu as pltpu

F32 = jnp.float32
MXU_DTYPE = jnp.bfloat16

N_DEV = 8
D_MODEL = 1024
SEQ = 2048
HEADS = 8
NOPE = 64
ROPE = 32
Q_LORA = 512
KV_LORA = 256
DIL_DIM = 64
DIL_WIDTH = HEADS * DIL_DIM
DILATIONS = (1, 4, 16)
SPAN = 128
D_FF = 2816
LANES = 128
ROPE_THETA = 10000.0
EPS = 1e-6
NEG_INF = -1e30
ADAM_LR, ADAM_B1, ADAM_B2, ADAM_EPS, ADAM_WD, ADAM_STEP = 0.001, 0.9, 0.999, 1e-08, 0.01, 10
VMEM_LIMIT = 56 * 1024 * 1024
MESH_ID = pl.DeviceIdType.MESH

P_QLAT, P_KVLAT, P_KPE, P_QD, P_KD, P_VD, P_END = 0, 512, 768, 896, 1408, 1920, 2432
KPE_LO = 64
MIX_IN = HEADS * LANES + DIL_WIDTH


def _params(**kw):
    return pltpu.CompilerParams(vmem_limit_bytes=VMEM_LIMIT, **kw)


def rowwise(name, fn, rows, params, out_rows, out_accs=(), tm=512, dep=None):
    deps = [] if dep is None else [dep]
    rows = [r if isinstance(r, tuple) else (r, r.shape[1], 0) for r in rows]
    R = rows[0][0].shape[0]
    tm = min(tm, R)
    steps = R // tm
    assert steps * tm == R
    in_specs = []
    for a, width, cb in rows:
        ri = a.shape[0]
        per = ri // tm
        assert per * tm == ri
        if ri == R:
            in_specs.append(pl.BlockSpec((tm, width), lambda i, cb=cb: (i, cb)))
        else:
            in_specs.append(pl.BlockSpec((tm, width), lambda i, per=per, cb=cb: (i % per, cb)))
    for p in params:
        in_specs.append(pl.BlockSpec(p.shape, lambda i: (0,) * p.ndim))
    in_specs += [pl.BlockSpec(memory_space=pl.ANY)] * len(deps)
    out_shape = [jax.ShapeDtypeStruct((R, d), dt) for d, dt in out_rows]
    out_specs = [pl.BlockSpec((tm, d), lambda i: (i, 0)) for d, _ in out_rows]
    out_shape += [jax.ShapeDtypeStruct((1, n), F32) for n in out_accs]
    out_specs += [pl.BlockSpec((1, n), lambda i: (0, 0)) for n in out_accs]
    nr, npar, no, na = len(rows), len(params), len(out_rows), len(out_accs)

    def body(*refs):
        rvals = [r[...] for r in refs[:nr]]
        pvals = [r[...] for r in refs[nr:nr + npar]]
        outs, accs = fn(rvals, pvals)
        first_out = nr + npar + len(deps)
        for ref, v in zip(refs[first_out:first_out + no], outs, strict=True):
            ref[...] = v.astype(ref.dtype)
        if na:
            acc_refs = refs[first_out + no:]
            i = pl.program_id(0)

            @pl.when(i == 0)
            def _():
                for ref, v in zip(acc_refs, accs, strict=True):
                    ref[...] = v

            @pl.when(i > 0)
            def _():
                for ref, v in zip(acc_refs, accs, strict=True):
                    ref[...] += v

    res = pl.pallas_call(body, name=name, grid=(steps,), in_specs=in_specs, out_specs=out_specs,
                         out_shape=out_shape, compiler_params=_params())(*[r[0] for r in rows], *params, *deps)
    return list(res)


_DIMS = {"nn": ((1,), (0,)), "nt": ((1,), (1,)), "tn": ((0,), (0,))}


def _dot(a, b, mode="nn"):
    return lax.dot_general(a.astype(MXU_DTYPE), b.astype(MXU_DTYPE), (_DIMS[mode], ((), ())),
                           preferred_element_type=F32)


def matmul(name, a, b, mode, tm=None, tn=None, tk=None, out_dtype=F32, dep=None):
    if mode == "tn":
        K, M = a.shape
    else:
        M, K = a.shape
    N = b.shape[0] if mode == "nt" else b.shape[1]
    tm, tn, tk = tm or M, tn or N, tk or K
    nm, nn, nk = M // tm, N // tn, K // tk
    assert nm * tm == M and nn * tn == N and nk * tk == K
    a_spec = pl.BlockSpec((tk, tm), lambda i, j, k: (k, i)) if mode == "tn" else pl.BlockSpec((tm, tk), lambda i, j, k: (i, k))
    b_spec = pl.BlockSpec((tn, tk), lambda i, j, k: (j, k)) if mode == "nt" else pl.BlockSpec((tk, tn), lambda i, j, k: (k, j))
    deps = [] if dep is None else [dep]

    def body(a_ref, b_ref, *rest):
        o_ref, scratch = rest[len(deps)], rest[len(deps) + 1:]
        p = _dot(a_ref[...], b_ref[...], mode)
        if nk == 1:
            o_ref[...] = p.astype(o_ref.dtype)
        else:
            acc = scratch[0]
            k = pl.program_id(2)

            @pl.when(k == 0)
            def _():
                acc[...] = p

            @pl.when(k > 0)
            def _():
                acc[...] += p

            @pl.when(k == nk - 1)
            def _():
                o_ref[...] = acc[...].astype(o_ref.dtype)

    return pl.pallas_call(
        body, name=name, grid=(nm, nn, nk), in_specs=[a_spec, b_spec] + [pl.BlockSpec(memory_space=pl.ANY)] * len(deps),
        out_specs=pl.BlockSpec((tm, tn), lambda i, j, k: (i, j)),
        out_shape=jax.ShapeDtypeStruct((M, N), out_dtype),
        scratch_shapes=[pltpu.VMEM((tm, tn), F32)] if nk > 1 else [],
        compiler_params=_params())(a, b, *deps)


def _rms(x, g):
    rstd = lax.rsqrt(jnp.mean(x * x, axis=-1, keepdims=True) + EPS)
    n = x * rstd
    return n * g, n, rstd


def _rms_bwd(dy, n, rstd, g):
    dg = jnp.sum(dy * n, axis=0, keepdims=True)
    dn = dy * g
    dx = rstd * (dn - n * jnp.mean(dn * n, axis=-1, keepdims=True))
    return dx, dg


def _norm_bwd(dy, x, g):
    _, n, rstd = _rms(x, g)
    return _rms_bwd(dy, n, rstd, g)


def _colsum(v):
    return jnp.sum(v, axis=0, keepdims=True)


def _silu(x):
    return x * (1.0 / (1.0 + jnp.exp(-x)))


def _lane(shape):
    return lax.broadcasted_iota(jnp.int32, shape, 1)


def _group_mean(v, groups):
    i = lax.broadcasted_iota(jnp.int32, (LANES, LANES), 0)
    j = lax.broadcasted_iota(jnp.int32, (LANES, LANES), 1)
    g = jnp.zeros((LANES, LANES), F32)
    for lo, hi in groups:
        g = jnp.where((i >= lo) & (i < hi) & (j >= lo) & (j < hi), 1.0 / (hi - lo), g)
    head = v.astype(MXU_DTYPE)
    return _dot(head, g) + _dot(v - head.astype(F32), g)


def _in_groups(shape, groups):
    lane = _lane(shape)
    m = jnp.zeros(shape, jnp.bool_)
    for lo, hi in groups:
        m = m | ((lane >= lo) & (lane < hi))
    return m


def _grms(x, g, groups):
    rstd = lax.rsqrt(_group_mean(x * x, groups) + EPS)
    n = jnp.where(_in_groups(x.shape, groups), x * rstd, 0.0)
    return n * g, n, rstd


def _grms_bwd(dy, n, rstd, g, groups):
    dn = dy * g
    return rstd * (dn - n * _group_mean(dn * n, groups)), _colsum(dy * n)


def _rot(x, half, transpose=False):
    first = (_lane(x.shape) % (2 * half)) < half
    up = pltpu.roll(x, LANES - half, axis=1)
    down = pltpu.roll(x, half, axis=1)
    return jnp.where(first, up, -down) if transpose else jnp.where(first, -up, down)


def _rope(x, cos, sin, half):
    return x * cos + _rot(x, half) * sin


def _rope_bwd(dy, cos, sin, half):
    return dy * cos + _rot(dy * sin, half, transpose=True)


def _chunks(x):
    return [x[:, i:i + LANES] for i in range(0, x.shape[1], LANES)]


Q_GROUPS = ((0, NOPE), (NOPE, NOPE + ROPE))
K_GROUPS = ((0, NOPE),)
KPE_GROUPS = ((KPE_LO, KPE_LO + ROPE),)
DIL_GROUPS = ((0, DIL_DIM), (DIL_DIM, 2 * DIL_DIM))


def _col(width, rows=SEQ):
    return pl.BlockSpec((rows, width), lambda h: (0, h))


def _causal_tail(s, tq, fill):
    diag = s[:, s.shape[1] - tq:]
    keep = lax.broadcasted_iota(jnp.int32, diag.shape, 1) <= lax.broadcasted_iota(jnp.int32, diag.shape, 0)
    diag = jnp.where(keep, diag, fill)
    return diag if s.shape[1] == tq else jnp.concatenate([s[:, :s.shape[1] - tq], diag], axis=1)


def mla_fwd(name, q, k, v, scale, tq=256):
    S = q.shape[0]

    def body(q_ref, k_ref, v_ref, o_ref, lse_ref):
        nb = S // tq
        blk = lambda i: slice(i * tq, (i + 1) * tq)

        def scores(i):
            return _dot(q_ref[blk(i), :], k_ref[:(i + 1) * tq, :], "nt")

        def softmax(i, s):
            s = _causal_tail(s * scale, tq, NEG_INF)
            m = jnp.max(s, axis=-1, keepdims=True)
            e = jnp.exp(s - m)
            l = jnp.sum(e, axis=-1, keepdims=True)
            lse_ref[0, blk(i), :] = m + jnp.log(l)
            return (e * (1.0 / l)).astype(MXU_DTYPE)

        def weighted(i, p):
            o_ref[blk(i), :] = _dot(p, v_ref[:(i + 1) * tq, :])

        s, p_prev = scores(0), None
        for i in range(nb):
            s_next = scores(i + 1) if i + 1 < nb else None
            if p_prev is not None:
                weighted(i - 1, p_prev)
            p_prev, s = softmax(i, s), s_next
        weighted(nb - 1, p_prev)

    return pl.pallas_call(
        body, name=name, grid=(HEADS,), in_specs=[_col(LANES)] * 3,
        out_specs=[_col(LANES), pl.BlockSpec((1, S, 1), lambda h: (h, 0, 0))],
        out_shape=[jax.ShapeDtypeStruct((S, MIX_IN), F32), jax.ShapeDtypeStruct((HEADS, S, 1), F32)],
        compiler_params=_params())(q, k, v)


def mla_bwd(name, q, k, v, o, do, lse, scale, tq=256):
    S = q.shape[0]

    def body(q_ref, k_ref, v_ref, o_ref, do_ref, lse_ref, dq_ref, dkv_ref, dkpe_ref, dk_acc, dv_acc):
        dk_acc[...] = jnp.zeros_like(dk_acc)
        dv_acc[...] = jnp.zeros_like(dv_acc)
        for i in range(S // tq):
            kext = (i + 1) * tq
            blk = slice(i * tq, kext)
            qi, kk, vv = q_ref[blk, :], k_ref[:kext, :], v_ref[:kext, :]
            doi = do_ref[blk, :]
            s = _causal_tail(_dot(qi, kk, "nt") * scale, tq, NEG_INF)
            p = jnp.exp(s - lse_ref[0, blk, :])
            dp = _dot(doi, vv, "nt")
            delta = jnp.sum(doi * o_ref[blk, :], axis=-1, keepdims=True)
            ds = p * (dp - delta) * scale
            dq_ref[blk, :] = _dot(ds, kk)
            dk_acc[:kext, :] += _dot(ds, qi, "tn")
            dv_acc[:kext, :] += _dot(p, doi, "tn")
        dk = dk_acc[...]
        lane = _lane(dk.shape)
        dkv_ref[...] = jnp.where(lane < NOPE, dk, 0.0) + dv_acc[...]
        dkpe = jnp.where((lane >= KPE_LO) & (lane < KPE_LO + ROPE), dk, 0.0)
        h = pl.program_id(0)

        @pl.when(h == 0)
        def _():
            dkpe_ref[...] = dkpe

        @pl.when(h > 0)
        def _():
            dkpe_ref[...] += dkpe

    return pl.pallas_call(
        body, name=name, grid=(HEADS,),
        in_specs=[_col(LANES)] * 5 + [pl.BlockSpec((1, S, 1), lambda h: (h, 0, 0))],
        out_specs=[_col(LANES), _col(LANES), pl.BlockSpec((S, LANES), lambda h: (0, 0))],
        out_shape=[jax.ShapeDtypeStruct((S, HEADS * LANES), F32), jax.ShapeDtypeStruct((S, HEADS * LANES), F32),
                   jax.ShapeDtypeStruct((S, LANES), F32)],
        scratch_shapes=[pltpu.VMEM((S, LANES), F32), pltpu.VMEM((S, LANES), F32)],
        compiler_params=_params())(q, k, v, o, do, lse)


BAND_TQ = SPAN


def _band_blocks(L, tq):
    return [(i * tq, (i + 1) * tq, max(0, i * tq - SPAN)) for i in range(L // tq)]


def _class_rows(r, dil, lo, hi):
    return pl.ds(r + dil * lo, hi - lo, stride=dil) if dil > 1 else pl.ds(lo, hi - lo)


def _stack_heads(t, lo):
    zero = jnp.zeros_like(t)
    return jnp.concatenate([jnp.where(lo, t, zero), jnp.where(lo, zero, t)], axis=0)


def _band_mask2(q0, q1, k0):
    n = q1 - q0
    shape = (2 * n, q1 - k0)
    i = lax.broadcasted_iota(jnp.int32, shape, 0)
    dist = (jnp.where(i >= n, i - n, i) + q0) - (lax.broadcasted_iota(jnp.int32, shape, 1) + k0)
    return (dist >= 0) & (dist <= SPAN)


def _pair_col(col0=0):
    return pl.BlockSpec((SEQ, LANES), lambda j: (0, col0 // LANES + j))


def band_fwd(name, q, k, v, dil):
    S = q.shape[0]
    L = S // dil
    tq = BAND_TQ
    scale = DIL_DIM ** -0.5

    def body(q_ref, k_ref, v_ref, o_ref, lse_ref):
        items = [(r, blk) for r in range(dil) for blk in _band_blocks(L, tq)]
        lo = _lane((tq, LANES)) < DIL_DIM

        def scores(item):
            r, (q0, q1, k0) = item
            qb = q_ref[_class_rows(r, dil, q0, q1), :].astype(MXU_DTYPE)
            return _dot(_stack_heads(qb, lo), k_ref[_class_rows(r, dil, k0, q1), :], "nt")

        def softmax(item, s):
            _, (q0, q1, k0) = item
            s = jnp.where(_band_mask2(q0, q1, k0), s * scale, NEG_INF)
            mx = jnp.max(s, axis=-1, keepdims=True)
            e = jnp.exp(s - mx)
            l = jnp.sum(e, axis=-1, keepdims=True)
            return (e * (1.0 / l)).astype(MXU_DTYPE), mx + jnp.log(l)

        def weighted(item, p, lse):
            r, (q0, q1, k0) = item
            pv = _dot(p, v_ref[_class_rows(r, dil, k0, q1), :])
            o_ref[_class_rows(r, dil, q0, q1), :] = jnp.where(lo, pv[:tq], pv[tq:])
            lse_ref[_class_rows(r, dil, q0, q1), :] = jnp.where(lo, lse[:tq], lse[tq:])

        s, prev = scores(items[0]), None
        for i, item in enumerate(items):
            s_next = scores(items[i + 1]) if i + 1 < len(items) else None
            if prev is not None:
                weighted(items[i - 1], *prev)
            prev, s = softmax(item, s), s_next
        weighted(items[-1], *prev)

    return pl.pallas_call(
        body, name=name, grid=(DIL_WIDTH // LANES,), in_specs=[_pair_col()] * 2 + [_pair_col(P_VD)], out_specs=[_pair_col()] * 2,
        out_shape=[jax.ShapeDtypeStruct((S, DIL_WIDTH), F32)] * 2, compiler_params=_params())(q, k, v)


def band_bwd(name, q, k, v, lse, lse_mix, o_cat, do_cat, dil, before=None):
    S = q.shape[0]
    L = S // dil
    tq = BAND_TQ
    scale = DIL_DIM ** -0.5
    before = list(before or [])

    def body(q_ref, k_ref, v_ref, lse_ref, mix_ref, o_ref, do_ref, *rest):
        dq_ref, dk_ref, dv_ref = rest[len(before):]
        if before:
            dq0_ref, dk0_ref, dv0_ref = rest[:3]
            dk_ref[...] = dk0_ref[...]
            dv_ref[...] = dv0_ref[...]
        else:
            dk_ref[...] = jnp.zeros_like(dk_ref)
            dv_ref[...] = jnp.zeros_like(dv_ref)
        items = [(r, blk) for r in range(dil) for blk in _band_blocks(L, tq)]
        lo = _lane((tq, LANES)) < DIL_DIM
        per_head = lambda t: jnp.concatenate([t[:, 0:1], t[:, DIL_DIM:DIL_DIM + 1]], axis=0)

        def scores(item):
            r, (q0, q1, k0) = item
            qrows, krows = _class_rows(r, dil, q0, q1), _class_rows(r, dil, k0, q1)
            lse_p, dout = lse_ref[qrows, :], do_ref[qrows, :]
            w2 = per_head(jnp.exp(lse_p - mix_ref[qrows, :]))
            dd = dout * o_ref[qrows, :]
            big_d = jnp.concatenate([jnp.sum(jnp.where(lo, dd, 0.0), axis=-1, keepdims=True),
                                     jnp.sum(jnp.where(lo, 0.0, dd), axis=-1, keepdims=True)], axis=0)
            q2 = _stack_heads(q_ref[qrows, :].astype(MXU_DTYPE), lo)
            dom = (_stack_heads(dout, lo) * w2).astype(MXU_DTYPE)
            return (_dot(q2, k_ref[krows, :], "nt"), _dot(dom, v_ref[krows, :], "nt"), per_head(lse_p), w2 * big_d, q2, dom)

        def softmax_bwd(item, s, dp, lse2, wd2, q2, dom):
            _, (q0, q1, k0) = item
            p = jnp.where(_band_mask2(q0, q1, k0), jnp.exp(s * scale - lse2), 0.0)
            return p.astype(MXU_DTYPE), (p * (dp - wd2) * scale).astype(MXU_DTYPE), q2, dom

        def grads(item, p, ds, q2, dom):
            r, (q0, q1, k0) = item
            qrows, krows = _class_rows(r, dil, q0, q1), _class_rows(r, dil, k0, q1)
            dq2 = _dot(ds, k_ref[krows, :])
            dq = jnp.where(lo, dq2[:tq], dq2[tq:])
            dq_ref[qrows, :] = dq + dq0_ref[qrows, :] if before else dq
            dk_ref[krows, :] += _dot(ds, q2, "tn")
            dv_ref[krows, :] += _dot(p, dom, "tn")

        sc, prev = scores(items[0]), None
        for i, item in enumerate(items):
            sc_next = scores(items[i + 1]) if i + 1 < len(items) else None
            if prev is not None:
                grads(items[i - 1], *prev)
            prev, sc = softmax_bwd(item, *sc), sc_next
        grads(items[-1], *prev)

    cat = _pair_col(HEADS * LANES)
    return pl.pallas_call(
        body, name=name, grid=(DIL_WIDTH // LANES,),
        in_specs=[_pair_col()] * 2 + [_pair_col(P_VD)] + [_pair_col()] * 2 + [cat] * 2 + [_pair_col()] * len(before),
        out_specs=[_pair_col()] * 3, out_shape=[jax.ShapeDtypeStruct((S, DIL_WIDTH), F32)] * 3,
        compiler_params=_params())(q, k, v, lse, lse_mix, o_cat, do_cat, *before)


def combine_fwd(name, outs, lses, o_cat, tm=512):
    S = outs[0].shape[0]

    def body(o1, o2, o3, l1, l2, l3, cat_in, cat_out, mix_ref):
        ls = [l1[...], l2[...], l3[...]]
        m = jnp.maximum(jnp.maximum(ls[0], ls[1]), ls[2])
        e = [jnp.exp(l - m) for l in ls]
        den = e[0] + e[1] + e[2]
        cat_out[...] = (e[0] / den) * o1[...] + (e[1] / den) * o2[...] + (e[2] / den) * o3[...]
        mix_ref[...] = m + jnp.log(den)

    row = pl.BlockSpec((tm, DIL_WIDTH), lambda i: (i, 0))
    return pl.pallas_call(
        body, name=name, grid=(S // tm,), in_specs=[row] * 6 + [pl.BlockSpec(memory_space=pl.ANY)],
        out_specs=[pl.BlockSpec((tm, DIL_WIDTH), lambda i: (i, HEADS * LANES // DIL_WIDTH)), row],
        out_shape=[jax.ShapeDtypeStruct(o_cat.shape, F32), jax.ShapeDtypeStruct((S, DIL_WIDTH), F32)],
        input_output_aliases={6: 0}, compiler_params=_params())(*outs, *lses, o_cat)


def _shift_down(u, n, zero_head):
    out = pltpu.roll(u, n, axis=0)
    return jnp.where(lax.broadcasted_iota(jnp.int32, u.shape, 0) >= n, out, 0.0) if zero_head else out


def _shift_up(u, n, zero_tail):
    rows = u.shape[0]
    out = pltpu.roll(u, rows - n, axis=0)
    return jnp.where(lax.broadcasted_iota(jnp.int32, u.shape, 0) < rows - n, out, 0.0) if zero_tail else out


CONV_ROWS = 512
CONV_HALO = 16


def _conv_chunks(S, tail):
    out = []
    for r0 in range(0, S, CONV_ROWS):
        lo, hi = max(0, r0 - CONV_HALO), min(S, r0 + CONV_ROWS + (CONV_HALO if tail else 0))
        out.append((lo, hi, r0 - lo, CONV_ROWS))
    return out


CONV_TC = 256
CONV_NB = D_FF // CONV_TC


def _half_specs(rows, rows_axis=False):
    if rows_axis:
        return [pl.BlockSpec((rows, D_MODEL), lambda j: (j, 0)), pl.BlockSpec((rows, D_MODEL), lambda j: (j + CONV_NB, 0))]
    return [pl.BlockSpec((rows, CONV_TC), lambda j: (0, j)), pl.BlockSpec((rows, CONV_TC), lambda j: (0, j + CONV_NB))]


def _whole(a):
    return pl.BlockSpec(a.shape, lambda j: (0,) * a.ndim)


def _up_pair(h, ug_ref, uv_ref):
    return jnp.concatenate([_dot(h, ug_ref[...], "nt"), _dot(h, uv_ref[...], "nt")], axis=1)


def _conv_taps(uin, w, b, starts):
    u1, u2 = _shift_down(uin, 1, starts), _shift_down(uin, 2, starts)
    return u1, u2, w[2:3, :] * uin + w[1:2, :] * u1 + w[0:1, :] * u2 + b


def ffn_fwd(name, h, w_up_t, w_conv, b_conv, w_down):
    S = h.shape[0]

    def body(h_ref, ug_ref, uv_ref, wg_ref, wv_ref, bg_ref, bv_ref, wd_ref, dn_ref, up_ref):
        @pl.when(pl.program_id(0) == 0)
        def _():
            dn_ref[...] = jnp.zeros_like(dn_ref)

        w = jnp.concatenate([wg_ref[...], wv_ref[...]], axis=1)
        b = jnp.concatenate([bg_ref[...], bv_ref[...]], axis=1)
        chunks = _conv_chunks(S, tail=False)

        def project(c):
            lo, hi, keep, rows = c
            uin = _up_pair(h_ref[lo:hi, :], ug_ref, uv_ref)
            up_ref[lo + keep:lo + keep + rows, :] = uin[keep:keep + rows]
            return uin

        def gate(c, uin):
            lo, hi, keep, rows = c
            u = _conv_taps(uin, w, b, lo == 0)[2][keep:keep + rows]
            return (_silu(u[:, :CONV_TC]) * u[:, CONV_TC:]).astype(MXU_DTYPE)

        def project_down(c, act):
            dn_ref[c[0] + c[2]:c[0] + c[2] + c[3], :] += _dot(act, wd_ref[...])

        uin, act_prev = project(chunks[0]), None
        for i, c in enumerate(chunks):
            uin_next = project(chunks[i + 1]) if i + 1 < len(chunks) else None
            if act_prev is not None:
                project_down(chunks[i - 1], act_prev)
            act_prev = gate(c, uin)
            uin = uin_next
        project_down(chunks[-1], act_prev)

    return pl.pallas_call(
        body, name=name, grid=(CONV_NB,),
        in_specs=[_whole(h)] + _half_specs(CONV_TC, rows_axis=True) + _half_specs(3) + _half_specs(1)
        + [pl.BlockSpec((CONV_TC, w_down.shape[1]), lambda j: (j, 0))],
        out_specs=[pl.BlockSpec((S, w_down.shape[1]), lambda j: (0, 0)), pl.BlockSpec((S, 2 * CONV_TC), lambda j: (0, j))],
        out_shape=[jax.ShapeDtypeStruct((S, w_down.shape[1]), F32), jax.ShapeDtypeStruct((S, 2 * D_FF), F32)],
        compiler_params=_params())(h, w_up_t, w_up_t, w_conv, w_conv, b_conv, b_conv, w_down)


def ffn_bwd(name, h, up, w_up_t, w_conv, b_conv, d_dn, w_down):
    S, D = h.shape

    def body(h_ref, up_ref, ug_ref, uv_ref, wg_ref, wv_ref, bg_ref, bv_ref, dd_ref, wd_ref,
             dh_ref, gup_ref, gd_ref, dwg_ref, dwv_ref, dbg_ref, dbv_ref):
        @pl.when(pl.program_id(0) == 0)
        def _():
            dh_ref[...] = jnp.zeros_like(dh_ref)

        w = jnp.concatenate([wg_ref[...], wv_ref[...]], axis=1)
        b = jnp.concatenate([bg_ref[...], bv_ref[...]], axis=1)
        w_pair = jnp.concatenate([ug_ref[...], uv_ref[...]], axis=0)
        chunks = _conv_chunks(S, tail=True)

        def project(c):
            return up_ref[c[0]:c[1], :], _dot(dd_ref[c[0]:c[1], :], wd_ref[...], "nt")

        def through_conv(c, uin, da):
            lo, hi, keep, rows = c
            u1, u2, u = _conv_taps(uin, w, b, lo == 0)
            gate, val = u[:, :CONV_TC], u[:, CONV_TC:]
            sig = 1.0 / (1.0 + jnp.exp(-gate))
            du = jnp.concatenate([da * val * (sig * (1.0 + gate * (1.0 - sig))), da * (gate * sig)], axis=1)
            dup = w[2:3, :] * du + w[1:2, :] * _shift_up(du, 1, hi == S) + w[0:1, :] * _shift_up(du, 2, hi == S)
            kept = slice(keep, keep + rows)
            du = du[kept]
            dw = jnp.concatenate([_colsum(du * u2[kept]), _colsum(du * u1[kept]), _colsum(du * uin[kept])], axis=0)
            return dup[kept].astype(MXU_DTYPE), (gate * sig * val)[kept].astype(MXU_DTYPE), dw, _colsum(du)

        def weight_grads(c, dup, act):
            out_rows = slice(c[0] + c[2], c[0] + c[2] + c[3])
            dh_ref[out_rows, :] += _dot(dup, w_pair)
            return _dot(dup, h_ref[out_rows, :], "tn"), _dot(act, dd_ref[out_rows, :], "tn")

        dw, db, g_up, g_dn = 0.0, 0.0, 0.0, 0.0
        proj, done = project(chunks[0]), None
        for i, c in enumerate(chunks):
            proj_next = project(chunks[i + 1]) if i + 1 < len(chunks) else None
            if done is not None:
                gu, gd = weight_grads(chunks[i - 1], *done)
                g_up, g_dn = g_up + gu, g_dn + gd
            dup, act, dw_c, db_c = through_conv(c, *proj)
            dw, db, done, proj = dw + dw_c, db + db_c, (dup, act), proj_next
        gu, gd = weight_grads(chunks[-1], *done)
        g_up, g_dn = g_up + gu, g_dn + gd
        gup_ref[0], gup_ref[1] = g_up[:CONV_TC].astype(gup_ref.dtype), g_up[CONV_TC:].astype(gup_ref.dtype)
        gd_ref[...] = g_dn.astype(gd_ref.dtype)
        dwg_ref[...], dwv_ref[...] = dw[:, :CONV_TC], dw[:, CONV_TC:]
        dbg_ref[...], dbv_ref[...] = db[:, :CONV_TC], db[:, CONV_TC:]

    half = lambda rows: pl.BlockSpec((rows, CONV_TC), lambda j: (0, j))
    rows_blk = pl.BlockSpec((CONV_TC, D), lambda j: (j, 0))
    dh, gup, gd, dwg, dwv, dbg, dbv = pl.pallas_call(
        body, name=name, grid=(CONV_NB,),
        in_specs=[_whole(h), pl.BlockSpec((S, 2 * CONV_TC), lambda j: (0, j))] + _half_specs(CONV_TC, rows_axis=True) + _half_specs(3)
        + _half_specs(1) + [_whole(d_dn), rows_blk],
        out_specs=[pl.BlockSpec((S, D), lambda j: (0, 0)), pl.BlockSpec((2, CONV_TC, D), lambda j: (0, j, 0)), rows_blk,
                   half(3), half(3), half(1), half(1)],
        out_shape=[jax.ShapeDtypeStruct((S, D), F32), jax.ShapeDtypeStruct((2, D_FF, D), MXU_DTYPE),
                   jax.ShapeDtypeStruct((D_FF, D), MXU_DTYPE)]
        + [jax.ShapeDtypeStruct((3, D_FF), F32)] * 2 + [jax.ShapeDtypeStruct((1, D_FF), F32)] * 2,
        compiler_params=_params())(h, up, w_up_t, w_up_t, w_conv, w_conv, b_conv, b_conv, d_dn, w_down)
    return dh, gup.reshape(2 * D_FF, D), gd, jnp.concatenate([dwg, dwv], axis=1), jnp.concatenate([dbg, dbv], axis=1)


def adamw(name, w, parts, m, v, tr=None):
    apart = w.ndim == 3
    R, C = w.shape[0], w.shape[-1]
    tr = tr or R
    assert R % tr == 0
    c1 = 1.0 - ADAM_B1 ** ADAM_STEP
    c2 = 1.0 - ADAM_B2 ** ADAM_STEP
    np_ = len(parts)

    def body(*refs):
        w_ref, m_ref, v_ref = refs[0], refs[1 + np_], refs[2 + np_]
        go_ref, d_ref, mo_ref, vo_ref = refs[3 + np_:]
        terms = []
        for part, ref in zip(parts, refs[1:1 + np_], strict=True):
            terms += [ref[...]] if part.ndim == 2 else [ref[p] for p in range(part.shape[0])]
        g = terms[0].astype(F32)
        for term in terms[1:]:
            g = g + term.astype(F32)
        m2 = ADAM_B1 * m_ref[...] + (1.0 - ADAM_B1) * g
        v2 = ADAM_B2 * v_ref[...] + (1.0 - ADAM_B2) * (g * g)
        go_ref[...] = g
        mo_ref[...] = m2
        vo_ref[...] = v2
        d_ref[...] = -ADAM_LR * ((m2 / c1) / (jnp.sqrt(v2 / c2) + ADAM_EPS) + ADAM_WD * w_ref[...])

    blk = pl.BlockSpec((tr, C), lambda i: (i, 0))
    own = pl.BlockSpec((tr, None, C), lambda i: (i, 0, 0)) if apart else blk
    part_specs = [blk if p.ndim == 2 else pl.BlockSpec((p.shape[0], tr, C), lambda i: (0, i, 0)) for p in parts]
    return pl.pallas_call(
        body, name=name, grid=(R // tr,),
        in_specs=[own] + part_specs + [own, own], out_specs=[own] * 4,
        out_shape=[jax.ShapeDtypeStruct(w.shape, F32)] * 4, compiler_params=_params())(w, *parts, m, v)


def _place():
    return lax.axis_index("x"), lax.axis_index("y"), lax.axis_index("c")


def all_gather(name, arrs, after=None):
    n = len(arrs)
    deps = [] if after is None else [after]

    def body(*refs):
        ins, outs = refs[:n], refs[n + len(deps):2 * n + len(deps)]
        send_sems, recv_sems, local_sems = refs[2 * n + len(deps):]
        x, y, c = _place()
        me, sibling = (x, y, c), (x, y, 1 - c)
        chips = [(1 - x, y), (x, 1 - y), (1 - x, 1 - y)]
        sends = []
        for t in range(n):
            out = outs[t]

            def slot(px, py, pc, out=out):
                return out.at[4 * px + 2 * py + pc]

            def copy(k, block, to, src=None, t=t, slot=slot):
                return pltpu.make_async_remote_copy(
                    src_ref=slot(*block) if src is None else src, dst_ref=slot(*block),
                    send_sem=send_sems.at[7 * t + k], recv_sem=recv_sems.at[7 * t + k],
                    device_id=to, device_id_type=MESH_ID)

            mine = pltpu.make_async_copy(ins[t], slot(*me), local_sems.at[t])
            mine.start()
            first = [copy(0, me, sibling, src=ins[t])]
            first += [copy(1 + j, me, (*chip, c), src=ins[t]) for j, chip in enumerate(chips)]
            for cp in first:
                cp.start()
            sends.append((mine, first, copy))
        for t in range(n):
            mine, first, copy = sends[t]
            passed = [copy(4 + j, (*chip, c), sibling) for j, chip in enumerate(chips)]
            for j, chip in enumerate(chips):
                copy(1 + j, (*chip, c), me).wait_recv()
                passed[j].start()
            copy(0, sibling, me).wait_recv()
            for j, chip in enumerate(chips):
                copy(4 + j, (*chip, 1 - c), me).wait_recv()
            for cp in first + passed:
                cp.wait_send()
            mine.wait()

    any_spec = pl.BlockSpec(memory_space=pl.ANY)
    res = pl.pallas_call(
        body, name=name, in_specs=[any_spec] * (n + len(deps)), out_specs=[any_spec] * n,
        out_shape=[jax.ShapeDtypeStruct((N_DEV,) + a.shape, a.dtype) for a in arrs],
        scratch_shapes=[pltpu.SemaphoreType.DMA((7 * n,)), pltpu.SemaphoreType.DMA((7 * n,)), pltpu.SemaphoreType.DMA((n,))],
        compiler_params=pltpu.CompilerParams(has_side_effects=True))(*arrs, *deps)
    return list(res)


def ada_modulation(name, c, w_ada):
    n_mod = w_ada.shape[1]

    def exchange(src_ref, dst_ref, send_sems, recv_sems):
        x, y, c_ = _place()
        me = 4 * x + 2 * y + c_
        copies = []
        for k in range(1, N_DEV):
            px, py, pc = x ^ (k >> 2), y ^ ((k >> 1) & 1), c_ ^ (k & 1)
            copies.append(pltpu.make_async_remote_copy(
                src_ref=src_ref, dst_ref=dst_ref.at[me], send_sem=send_sems.at[k - 1], recv_sem=recv_sems.at[k - 1],
                device_id=(px, py, pc), device_id_type=MESH_ID))
        for cp in copies:
            cp.start()
        for cp in copies:
            cp.wait_recv()
        for cp in copies:
            cp.wait_send()
        return me

    def body(c_ref, w_ref, sc_ref, mod_ref, c_all, send_c, recv_c, send_m, recv_m):
        me = exchange(c_ref, c_all, send_c, recv_c)
        c_all[me] = c_ref[...]
        sc = _silu(jnp.concatenate([c_all[p] for p in range(N_DEV)], axis=0))
        sc_ref[...] = sc.astype(sc_ref.dtype)
        mod_ref[me] = _dot(sc, w_ref[...])
        exchange(mod_ref.at[me], mod_ref, send_m, recv_m)

    vmem = pl.BlockSpec(memory_space=pltpu.VMEM)
    return pl.pallas_call(
        body, name=name, in_specs=[vmem, vmem], out_specs=[vmem, vmem],
        out_shape=[jax.ShapeDtypeStruct((N_DEV, c.shape[1]), MXU_DTYPE), jax.ShapeDtypeStruct((N_DEV, N_DEV, n_mod), F32)],
        scratch_shapes=[pltpu.VMEM((N_DEV, 1, c.shape[1]), F32)] + [pltpu.SemaphoreType.DMA((N_DEV - 1,))] * 4,
        compiler_params=pltpu.CompilerParams(has_side_effects=True, vmem_limit_bytes=VMEM_LIMIT))(c, w_ada)


HBM_SPEC = pl.BlockSpec(memory_space=pltpu.HBM)
SEM_SPEC = pl.BlockSpec(memory_space=pltpu.SEMAPHORE)
DATAFLOW = pltpu.SideEffectType.DATAFLOW_SIDE_EFFECTING


def _exchange_copies(srcs, lands, send_sems, recv_sems, gather, first=0):
    x, y, c = _place()
    me = 4 * x + 2 * y + c
    out = []
    for t, (src, land) in enumerate(zip(srcs, lands, strict=True)):
        for k in range(1, N_DEV):
            px, py, pc = x ^ (k >> 2), y ^ ((k >> 1) & 1), c ^ (k & 1)
            sem = 7 * (first + t) + k - 1
            out.append(pltpu.make_async_remote_copy(
                src_ref=src if gather else src.at[4 * px + 2 * py + pc],
                dst_ref=land.at[me] if gather else land.at[k - 1],
                send_sem=send_sems.at[sem], recv_sem=recv_sems.at[sem],
                device_id=(px, py, pc), device_id_type=MESH_ID))
    return out


def exchange_start(name, arrs, gather, after=None):
    n = len(arrs)
    lands = [lax.empty(((N_DEV,) + a.shape) if gather else ((N_DEV - 1,) + a.shape[1:]), a.dtype) for a in arrs]
    deps = [] if after is None else [after]

    def body(*refs):
        srcs, land_refs = refs[:n], refs[n:2 * n]
        send_sems, recv_sems = refs[2 * n + len(deps)], refs[2 * n + len(deps) + 1]
        token = refs[-1]
        for cp in _exchange_copies(srcs, land_refs, send_sems, recv_sems, gather):
            cp.start()
        token[...] = jnp.zeros_like(token)

    hbm = lambda a: pltpu.HBM(a.shape, a.dtype)
    res = pl.pallas_call(
        body, name=name,
        out_shape=(pltpu.SemaphoreType.DMA((7 * n,)), pltpu.SemaphoreType.DMA((7 * n,)), *[hbm(a) for a in arrs],
                   *[hbm(l) for l in lands], jax.ShapeDtypeStruct((8, 128), F32)),
        in_specs=[HBM_SPEC] * (2 * n) + [pl.BlockSpec(memory_space=pl.ANY)] * len(deps),
        out_specs=(SEM_SPEC, SEM_SPEC, *[HBM_SPEC] * (2 * n), pl.BlockSpec(memory_space=pltpu.VMEM)),
        input_output_aliases={i: 2 + i for i in range(2 * n)},
        compiler_params=pltpu.CompilerParams(has_side_effects=DATAFLOW),
    )(*[pltpu.with_memory_space_constraint(a, pltpu.HBM) for a in arrs + lands], *deps)
    return res[0], res[1], list(res[2:2 + n]), list(res[2 + n:2 + 2 * n]), res[-1]


def exchange_wait(name, started, gather, after, first=0, count=None):
    send_sems, recv_sems, srcs, lands, _ = started
    count = len(srcs) - first if count is None else count
    srcs, lands = srcs[first:first + count], lands[first:first + count]
    n = len(srcs)

    def body(*refs):
        src_refs, land_refs = refs[:n], refs[n:2 * n]
        copies = _exchange_copies(src_refs, land_refs, refs[2 * n], refs[2 * n + 1], gather, first)
        for cp in copies:
            cp.wait_send()
        for cp in copies:
            cp.wait_recv()

    hbm = lambda a: pltpu.HBM(a.shape, a.dtype)
    res = pl.pallas_call(
        body, name=name, out_shape=tuple(hbm(a) for a in srcs + lands),
        in_specs=[HBM_SPEC] * (2 * n) + [SEM_SPEC, SEM_SPEC, pl.BlockSpec(memory_space=pl.ANY)],
        out_specs=tuple([HBM_SPEC] * (2 * n)), input_output_aliases={i: i for i in range(2 * n)},
        compiler_params=pltpu.CompilerParams(has_side_effects=DATAFLOW),
    )(*srcs, *lands, send_sems, recv_sems, after)
    return list(res[:n]), list(res[n:])


def _gather_cols(stack):
    p, k, n = stack.shape
    return stack.transpose(1, 0, 2).reshape(k, p * n)


def _scatter_cols(full):
    k, n = full.shape
    return full.reshape(k, N_DEV, n // N_DEV).transpose(1, 0, 2)


def _gather_rows(stack):
    p, r, n = stack.shape
    return stack.reshape(p * r, n)


def _scatter_rows(full):
    r, n = full.shape
    return full.reshape(N_DEV, r // N_DEV, n)


_IN_NAT = Q_LORA + KV_LORA
TRANSPOSED = ("w_in", "w_q_b", "w_up")
ROWS_APART = ("w_in", "w_conv")


def to_kernel_layout(name, w):
    if name == "w_in":
        z = lambda n: jnp.zeros((n, w.shape[1]), w.dtype)
        return jnp.concatenate([w[:_IN_NAT], z(KPE_LO), w[_IN_NAT:_IN_NAT + ROPE], z(LANES - KPE_LO - ROPE), w[_IN_NAT + ROPE:]], axis=0)
    if name == "w_q_b":
        return jnp.pad(w.reshape(HEADS, NOPE + ROPE, -1), ((0, 0), (0, LANES - NOPE - ROPE), (0, 0))).reshape(HEADS * LANES, -1)
    if name == "w_o":
        mla = jnp.pad(w[:HEADS * NOPE].reshape(HEADS, NOPE, -1), ((0, 0), (LANES - NOPE, 0), (0, 0))).reshape(HEADS * LANES, -1)
        return jnp.concatenate([mla, w[HEADS * NOPE:]], axis=0)
    return w


def from_kernel_layout(name, g):
    if name == "w_in":
        return jnp.concatenate([g[:_IN_NAT], g[P_KPE + KPE_LO:P_KPE + KPE_LO + ROPE], g[P_QD:]], axis=0)
    if name == "w_q_b":
        return g.reshape(HEADS, LANES, -1)[:, :NOPE + ROPE, :].reshape(HEADS * (NOPE + ROPE), -1)
    if name == "w_o":
        mla = g[:HEADS * LANES].reshape(HEADS, LANES, -1)[:, LANES - NOPE:, :].reshape(HEADS * NOPE, -1)
        return jnp.concatenate([mla, g[HEADS * LANES:]], axis=0)
    return g


SMALL_COLS = 1024
SMALL_ROWS = 24
SMALL_AT = {"loss": (0, 0, 1), "b_ada": (1, 0, 6 * D_MODEL), "g_mix_norm": (7, 0, D_MODEL), "g_q_lat": (8, 0, Q_LORA),
            "g_kv_lat": (9, 0, KV_LORA), "g_mla_q_nope": (10, 0, NOPE), "g_mla_q_pe": (10, 128, ROPE),
            "g_mla_k_nope": (10, 256, NOPE), "g_mla_k_pe": (10, 384, ROPE), "g_dil_q": (10, 512, DIL_DIM),
            "g_dil_k": (10, 640, DIL_DIM), "g_ffn_norm": (11, 0, D_MODEL), "b_conv": (12, 0, 2 * D_FF)}
SMALL_PARAMS = tuple(n for n in SMALL_AT if n != "loss")


def _pack_small(values):
    by_row = {}
    for name, (row, off, n) in SMALL_AT.items():
        by_row.setdefault(row, []).append((off, values[name].reshape(-1).astype(F32)))
    out = []
    for row in sorted(by_row):
        pieces, at = [], 0
        for off, v in sorted(by_row[row], key=lambda t: t[0]):
            pieces += [jnp.zeros((off - at,), F32), v]
            at = off + v.shape[0]
        flat = jnp.concatenate(pieces)
        nrows = -(-flat.shape[0] // SMALL_COLS)
        out.append(jnp.pad(flat, (0, nrows * SMALL_COLS - flat.shape[0])).reshape(nrows, SMALL_COLS))
    packed = jnp.concatenate(out, axis=0)
    return jnp.pad(packed, ((0, SMALL_ROWS - packed.shape[0]), (0, 0)))


def _adam(w, g, m, v):
    c1 = 1.0 - ADAM_B1 ** ADAM_STEP
    c2 = 1.0 - ADAM_B2 ** ADAM_STEP
    m2 = ADAM_B1 * m + (1.0 - ADAM_B1) * g
    v2 = ADAM_B2 * v + (1.0 - ADAM_B2) * (g * g)
    return -ADAM_LR * ((m2 / c1) / (jnp.sqrt(v2 / c2) + ADAM_EPS) + ADAM_WD * w), m2, v2


def adamw_small(name, stack, params):
    flat = [a for n in SMALL_PARAMS for a in params[n]]

    def body(stack_ref, *refs):
        ins, outs = refs[:len(flat)], refs[len(flat):]
        g_all = stack_ref[0]
        for p in range(1, N_DEV):
            g_all = g_all + stack_ref[p]
        outs[0][...] = g_all[0:1, 0:1]
        for i, pname in enumerate(SMALL_PARAMS):
            row, off, n = SMALL_AT[pname]
            w_ref, m_ref, v_ref = ins[3 * i:3 * i + 3]
            go_ref, d_ref, mo_ref, vo_ref = outs[1 + 4 * i:5 + 4 * i]
            for c0 in range(0, n, SMALL_COLS):
                cn = min(SMALL_COLS, n - c0)
                r = row + c0 // SMALL_COLS
                g = g_all[r:r + 1, off:off + cn]
                cols = (slice(None), slice(c0, c0 + cn))
                d, m2, v2 = _adam(w_ref[cols], g, m_ref[cols], v_ref[cols])
                go_ref[cols], d_ref[cols], mo_ref[cols], vo_ref[cols] = g, d, m2, v2

    whole = lambda a: pl.BlockSpec(a.shape, lambda: (0,) * a.ndim)
    out_shape = [jax.ShapeDtypeStruct((1, 1), F32)] + [jax.ShapeDtypeStruct(a.shape, F32) for n in SMALL_PARAMS for a in params[n][:1] * 4]
    res = pl.pallas_call(body, name=name, in_specs=[whole(stack)] + [whole(a) for a in flat],
                         out_specs=[pl.BlockSpec(s.shape, lambda s=s: (0,) * len(s.shape)) for s in out_shape],
                         out_shape=out_shape, compiler_params=_params())(stack, *flat)
    return res[0], {n: res[1 + 4 * i:5 + 4 * i] for i, n in enumerate(SMALL_PARAMS)}


def _local_step(x, pos, mod, target, w, fetch, emit):
    S = SEQ
    sh1, sc1, g1, sh2, sc2, g2 = [mod[:, i * D_MODEL:(i + 1) * D_MODEL] for i in range(6)]
    zeros = lambda n: jnp.zeros((1, n), F32)
    g_q = jnp.concatenate([w["g_mla_q_nope"], w["g_mla_q_pe"], zeros(LANES - NOPE - ROPE)], axis=1)
    g_k = jnp.concatenate([w["g_mla_k_nope"], zeros(LANES - NOPE)], axis=1)
    g_kpe = jnp.concatenate([zeros(KPE_LO), w["g_mla_k_pe"], zeros(LANES - KPE_LO - ROPE)], axis=1)
    g_dq = jnp.concatenate([w["g_dil_q"]] * 2, axis=1)
    g_dk = jnp.concatenate([w["g_dil_k"]] * 2, axis=1)
    b_conv = w["b_conv"]

    def inv_freq(d):
        return jnp.power(ROPE_THETA, -2.0 * jnp.arange(d // 2, dtype=F32) / d)

    n_m, n_d = ROPE // 2, DIL_DIM // 2
    freqs = jnp.concatenate([inv_freq(ROPE), inv_freq(DIL_DIM), jnp.zeros((LANES - n_m - n_d,), F32)]).reshape(1, LANES)

    def tables_fn(rows, params):
        (p,), (f,) = rows, params
        c, s = jnp.cos(p * f), jnp.sin(p * f)
        one, zero = jnp.ones_like(c), jnp.zeros_like(c)
        mla = lambda t, fill: jnp.concatenate([fill[:, :KPE_LO], t[:, :n_m], t[:, :n_m], fill[:, :LANES - KPE_LO - ROPE]], axis=1)
        dil = lambda t: jnp.concatenate([t[:, n_m:n_m + n_d]] * 4, axis=1)
        return [mla(c, one), mla(s, zero), dil(c), dil(s)], []

    cos_m, sin_m, cos_d, sin_d = rowwise("rope_tables", tables_fn, [pos], [freqs], [(LANES, F32)] * 4)
    tables = [cos_m, sin_m, cos_d, sin_d]
    H_M, H_D = ROPE // 2, DIL_DIM // 2

    def ln1_fn(rows, params):
        (xv,), (g, sc, sh) = rows, params
        y, _, _ = _rms(xv, g)
        return [y * (1.0 + sc) + sh], []

    (h,) = rowwise("ln1_fwd", ln1_fn, [x], [w["g_mix_norm"], sc1, sh1], [(D_MODEL, MXU_DTYPE)])
    w_in = fetch("w_in", h)

    def proj_fn(rows, params):
        (hv, cm, sm, cd, sd), (w_t, gq, gkv, gkp, gdq, gdk) = rows, params
        pv = _dot(hv, w_t, "nt")
        kper = _rope(_grms(pv[:, P_KPE:P_QD], gkp, KPE_GROUPS)[0], cm, sm, H_M)
        qd = [_rope(_grms(c, gdq, DIL_GROUPS)[0], cd, sd, H_D) for c in _chunks(pv[:, P_QD:P_KD])]
        kd = [_rope(_grms(c, gdk, DIL_GROUPS)[0], cd, sd, H_D) for c in _chunks(pv[:, P_KD:P_VD])]
        return [pv, _rms(pv[:, P_QLAT:P_KVLAT], gq)[0], _rms(pv[:, P_KVLAT:P_KPE], gkv)[0], kper,
                jnp.concatenate(qd, axis=1), jnp.concatenate(kd, axis=1)], []

    post_params = [w["g_q_lat"], w["g_kv_lat"], g_kpe, g_dq, g_dk]
    proj, qln, kvn, kper, qd_r, kd_r = rowwise(
        "proj_fwd", proj_fn, [h] + tables, [w_in] + post_params,
        [(P_END, F32), (Q_LORA, MXU_DTYPE), (KV_LORA, MXU_DTYPE), (LANES, MXU_DTYPE)] + [(DIL_WIDTH, F32)] * 2, tm=256)
    w_q_b, w_kv_b = fetch("w_q_b", qln), fetch("w_kv_b", kvn)

    def mla_proj_fn(rows, params):
        (qlv, kvlv, kp, cm, sm), (wq_t, wkv, gq, gk) = rows, params
        qv, kvv = _dot(qlv, wq_t, "nt"), _dot(kvlv, wkv)
        value_lanes = _lane(kp.shape) >= NOPE
        qs, ks, vs = [], [], []
        for qc, kc in zip(_chunks(qv), _chunks(kvv), strict=True):
            qs.append(_rope(_grms(qc, gq, Q_GROUPS)[0], cm, sm, H_M))
            ks.append(_grms(kc, gk, K_GROUPS)[0] + kp)
            vs.append(jnp.where(value_lanes, kc, 0.0))
        return [qv, kvv] + [jnp.concatenate(t, axis=1) for t in (qs, ks, vs)], []

    q, kv, q_mla, k_mla, v_mla = rowwise(
        "mla_proj", mla_proj_fn, [qln, kvn, kper, cos_m, sin_m], [w_q_b, w_kv_b, g_q, g_k],
        [(HEADS * LANES, F32)] * 2 + [(HEADS * LANES, MXU_DTYPE)] * 3, tm=256)
    mla_scale = (NOPE + ROPE) ** -0.5
    o_cat, lse_mla = mla_fwd("mla_fwd", q_mla, k_mla, v_mla, mla_scale)

    band = [band_fwd(f"band{dil}_fwd", qd_r, kd_r, proj, dil) for dil in DILATIONS]
    o_cat, lse_mix = combine_fwd("dil_combine", [b[0] for b in band], [b[1] for b in band], o_cat)
    w_o = fetch("w_o", o_cat)

    def mid_fn(rows, params):
        (ov, xv), (w_out, gate1, g, sc, sh) = rows, params
        mx = _dot(ov, w_out)
        x1 = xv + gate1 * mx
        y, _, _ = _rms(x1, g)
        return [mx, x1, y * (1.0 + sc) + sh], []

    mix, x1, h2 = rowwise("mix_fwd", mid_fn, [o_cat, x], [w_o, g1, w["g_ffn_norm"], sc2, sh2],
                          [(D_MODEL, F32), (D_MODEL, F32), (D_MODEL, MXU_DTYPE)], tm=256)
    w_up, w_conv, w_down = fetch("w_up", h2), fetch("w_conv", h2), fetch("w_down", h2)
    dn, up = ffn_fwd("ffn_fwd", h2, w_up, w_conv, b_conv, w_down)

    def final_fn(rows, params):
        (x1v, dnv, tgt), (gate2,) = rows, params
        r = x1v + gate2 * dnv - tgt
        dy = r * (1.0 / D_MODEL)
        loss = jnp.sum(_colsum(r * r), axis=-1, keepdims=True) * (0.5 / D_MODEL)
        return [dy, gate2 * dy], [loss, _colsum(dy * dnv)]

    dy, d_dn, loss, dg2 = rowwise("loss_head", final_fn, [x1, dn, target], [g2], [(D_MODEL, F32), (D_MODEL, MXU_DTYPE)],
                                  [1, D_MODEL])
    dh2, g_up, g_down, g_w_conv, g_b_conv = ffn_bwd("ffn_bwd", h2, up, w_up, w_conv, b_conv, d_dn, w_down)
    emit("w_down", g_down)
    emit("w_conv", g_w_conv)
    sent = emit("w_up", g_up)

    def mid_bwd_fn(rows, params):
        (dh2v, dyv, x1v, mx), (gate1, g, sc) = rows, params
        yn, n, rstd = _rms(x1v, g)
        dx_n, dg = _rms_bwd(dh2v * (1.0 + sc), n, rstd, g)
        dx1 = dyv + dx_n
        return [dx1, gate1 * dx1], [dg, _colsum(dh2v * yn), _colsum(dh2v), _colsum(dx1 * mx)]

    dx1, dmix, dg_ffn, dsc2, dsh2, dg1 = rowwise(
        "mid_bwd", mid_bwd_fn, [dh2, dy, x1, mix], [g1, w["g_ffn_norm"], sc2], [(D_MODEL, F32), (D_MODEL, MXU_DTYPE)],
        [D_MODEL] * 4, dep=sent)

    sent = emit("w_o", matmul("mix_wgrad", o_cat, dmix, "tn", tm=512, out_dtype=MXU_DTYPE))
    do_cat = matmul("mix_dgrad", dmix, w_o, "nt", tm=512, dep=sent)
    dband = None
    for dil, b in zip(DILATIONS, band):
        dband = band_bwd(f"band{dil}_bwd", qd_r, kd_r, proj, b[1], lse_mix, o_cat, do_cat, dil, before=dband)
    dq_mla, dkv_mla, dkper = mla_bwd("mla_bwd", q_mla, k_mla, v_mla, o_cat, do_cat, lse_mla, mla_scale)

    def mla_prep_bwd_fn(rows, params):
        (dqv, dkvv, qv, kvv, cm, sm), (gq, gk) = rows, params
        nope_lanes = _lane(cm.shape) < NOPE
        dqs, dkvs, dgq, dgk = [], [], 0.0, 0.0
        for dqc, dkc, qc, kc in zip(_chunks(dqv), _chunks(dkvv), _chunks(qv), _chunks(kvv), strict=True):
            _, n, rstd = _grms(qc, gq, Q_GROUPS)
            dx, dg = _grms_bwd(_rope_bwd(dqc, cm, sm, H_M), n, rstd, gq, Q_GROUPS)
            dqs.append(dx)
            dgq = dgq + dg
            _, n, rstd = _grms(kc, gk, K_GROUPS)
            dx, dg = _grms_bwd(dkc, n, rstd, gk, K_GROUPS)
            dkvs.append(jnp.where(nope_lanes, dx, dkc))
            dgk = dgk + dg
        return [jnp.concatenate(dqs, axis=1), jnp.concatenate(dkvs, axis=1)], [dgq, dgk]

    dq, dkv, dg_q, dg_k = rowwise("mla_prep_bwd", mla_prep_bwd_fn, [dq_mla, dkv_mla, q, kv, cos_m, sin_m], [g_q, g_k],
                                  [(HEADS * LANES, MXU_DTYPE)] * 2, [LANES, LANES], tm=256)
    emit("w_q_b", matmul("q_wgrad", dq, qln, "tn", out_dtype=MXU_DTYPE))
    emit("w_kv_b", matmul("kv_wgrad", kvn, dkv, "tn", out_dtype=MXU_DTYPE))

    def pre_bwd_fn(rows, params):
        dqv, dkvv, dkp, dqd_, dkd_, dvd_, pv, cm, sm, cd, sd = rows
        wq_t, wkv, gq, gkv, gkp, gdq, gdk = params
        dql, dkvl = _dot(dqv, wq_t), _dot(dkvv, wkv, "nt")
        r_q = _norm_bwd(dql, pv[:, P_QLAT:P_KVLAT], gq)
        r_kv = _norm_bwd(dkvl, pv[:, P_KVLAT:P_KPE], gkv)
        _, n, rstd = _grms(pv[:, P_KPE:P_QD], gkp, KPE_GROUPS)
        r_kp = _grms_bwd(_rope_bwd(dkp, cm, sm, H_M), n, rstd, gkp, KPE_GROUPS)
        outs, dgs = [r_q[0], r_kv[0], r_kp[0]], []
        for dval, lo, g in ((dqd_, P_QD, gdq), (dkd_, P_KD, gdk)):
            dg_sum = 0.0
            for dc, xc in zip(_chunks(dval), _chunks(pv[:, lo:lo + DIL_WIDTH]), strict=True):
                _, n, rstd = _grms(xc, g, DIL_GROUPS)
                dx, dg = _grms_bwd(_rope_bwd(dc, cd, sd, H_D), n, rstd, g, DIL_GROUPS)
                outs.append(dx)
                dg_sum = dg_sum + dg
            dgs.append(dg_sum)
        return [jnp.concatenate(outs + [dvd_], axis=1)], [r_q[1], r_kv[1], r_kp[1]] + dgs

    dproj, dg_q_lat, dg_kv_lat, dg_kpe, dg_dq, dg_dk = rowwise(
        "proj_pre_bwd", pre_bwd_fn,
        [dq, dkv, dkper] + list(dband) + [proj] + tables, [w_q_b, w_kv_b] + post_params,
        [(P_END, MXU_DTYPE)], [Q_LORA, KV_LORA, LANES, LANES, LANES], tm=256)
    sent = emit("w_in", matmul("proj_wgrad", dproj, h, "tn", tn=512, out_dtype=MXU_DTYPE))

    def ln1_bwd_fn(rows, params):
        (dpv, dres, xv), (w_t, g, sc) = rows, params
        dhv = _dot(dpv, w_t)
        yn, n, rstd = _rms(xv, g)
        dx_n, dg = _rms_bwd(dhv * (1.0 + sc), n, rstd, g)
        return [dres + dx_n], [dg, _colsum(dhv * yn), _colsum(dhv)]

    grad_x, dg_mix, dsc1, dsh1 = rowwise("proj_dgrad", ln1_bwd_fn, [dproj, dx1, x], [w_in, w["g_mix_norm"], sc1],
                                         [(D_MODEL, F32)], [D_MODEL] * 3, tm=256, dep=sent)
    dmod = jnp.concatenate([dsh1, dsc1, dg1, dsh2, dsc2, dg2], axis=-1)
    small = {"loss": loss, "b_ada": dmod, "g_mix_norm": dg_mix, "g_q_lat": dg_q_lat, "g_kv_lat": dg_kv_lat,
             "g_mla_q_nope": dg_q[:, :NOPE], "g_mla_q_pe": dg_q[:, NOPE:NOPE + ROPE], "g_mla_k_nope": dg_k[:, :NOPE],
             "g_mla_k_pe": dg_kpe[:, KPE_LO:KPE_LO + ROPE], "g_dil_q": dg_dq[:, :DIL_DIM] + dg_dq[:, DIL_DIM:],
             "g_dil_k": dg_dk[:, :DIL_DIM] + dg_dk[:, DIL_DIM:], "g_ffn_norm": dg_ffn,
             "b_conv": g_b_conv}
    return grad_x, small


COL_SHARDED = ("w_kv_b", "w_conv")
ROW_SHARDED = ("w_o", "w_down") + TRANSPOSED
ADAM_TILE = {"w_ada": 256, "w_up": 176, "w_down": 176}
GATHER_GROUPS = (("w_in",), ("w_q_b", "w_kv_b"), ("w_o",), ("w_up", "w_conv", "w_down"))
SCATTER_GROUPS = (("w_down", "w_conv", "w_up"), ("w_o",), ("w_q_b", "w_kv_b", "w_in"))
OUT_WEIGHTS = ("w_ada", "b_ada", "g_mix_norm", "w_in", "g_q_lat", "w_q_b", "g_kv_lat", "w_kv_b", "g_mla_q_nope", "g_mla_q_pe",
               "g_mla_k_nope", "g_mla_k_pe", "g_dil_q", "g_dil_k", "w_o", "g_ffn_norm", "w_up", "w_conv", "b_conv", "w_down")


def kernel(x, c, positions, w_ada, b_ada, g_mix_norm, w_in, g_q_lat, w_q_b, g_kv_lat, w_kv_b, g_mla_q_nope, g_mla_q_pe, g_mla_k_nope, g_mla_k_pe, g_dil_q, g_dil_k, w_o, g_ffn_norm, w_up, w_conv, b_conv, w_down, loss_target, m_w_ada, m_b_ada, m_g_mix_norm, m_w_in, m_g_q_lat, m_w_q_b, m_g_kv_lat, m_w_kv_b, m_g_mla_q_nope, m_g_mla_q_pe, m_g_mla_k_nope, m_g_mla_k_pe, m_g_dil_q, m_g_dil_k, m_w_o, m_g_ffn_norm, m_w_up, m_w_conv, m_b_conv, m_w_down, v_w_ada, v_b_ada, v_g_mix_norm, v_w_in, v_g_q_lat, v_w_q_b, v_g_kv_lat, v_w_kv_b, v_g_mla_q_nope, v_g_mla_q_pe, v_g_mla_k_nope, v_g_mla_k_pe, v_g_dil_q, v_g_dil_k, v_w_o, v_g_ffn_norm, v_w_up, v_w_conv, v_b_conv, v_w_down):
    args = dict(locals())
    xi, yi, ci = _place()
    me = 4 * xi + 2 * yi + ci
    def local(prefix, n):
        a = args[prefix + n]
        if n in ROWS_APART:
            return jnp.transpose(a, (2, 0, 1) if n in TRANSPOSED else (1, 0, 2))
        return a[0].T if n in TRANSPOSED else a[0]

    def as_output(n, r):
        if n in ROWS_APART:
            return jnp.transpose(r, (1, 2, 0) if n in TRANSPOSED else (1, 0, 2))
        return (r.T if n in TRANSPOSED else r)[None]

    shard = {n: local("", n) for n in COL_SHARDED + ROW_SHARDED + ("w_ada",)}
    flat = lambda n, a: a.reshape(a.shape[0], a.shape[-1]) if n in ROWS_APART else a
    small_w = {n: args[n] for n in SMALL_PARAMS}

    sc_all, mod_all = ada_modulation("ada_mod", c, shard["w_ada"])

    payload = {n: flat(n, shard[n]) if n == "w_conv" else flat(n, shard[n]).astype(MXU_DTYPE) for n in COL_SHARDED + ROW_SHARDED}
    gather_order = [n for grp in GATHER_GROUPS for n in grp]
    gathered = exchange_start("gather_start", [payload[n] for n in gather_order], gather=True, after=mod_all)
    after_start = gathered[-1]
    full = {}

    def fetch(name, after):
        if name not in full:
            (i, grp), = [(i, grp) for i, grp in enumerate(GATHER_GROUPS) if name in grp]
            srcs, lands = exchange_wait(f"gather{i}_wait", gathered, True, after, gather_order.index(grp[0]), len(grp))
            for n, src, land in zip(grp, srcs, lands, strict=True):
                stack = lax.dynamic_update_index_in_dim(land, src, me, 0)
                full[n] = to_kernel_layout(n, _gather_cols(stack) if n in COL_SHARDED else _gather_rows(stack))
        return full[name]

    mod_row = lax.dynamic_index_in_dim(mod_all, me, axis=1, keepdims=False).reshape(1, 6 * D_MODEL)
    (mod,) = rowwise("ada_bias", lambda rows, params: ([rows[0] + rows[1]], []), [mod_row, b_ada], [], [(6 * D_MODEL, F32)],
                     dep=after_start)

    own, pending, scatters = {}, {}, {}

    def emit(name, grad):
        grad = from_kernel_layout(name, grad)
        parts = _scatter_cols(grad) if name in COL_SHARDED else _scatter_rows(grad)
        own[name] = lax.dynamic_index_in_dim(parts, me, 0, keepdims=False)
        pending[name] = parts
        for i, grp in enumerate(SCATTER_GROUPS):
            if name == grp[-1]:
                scatters[i] = exchange_start(f"scatter{i}_start", [pending[n] for n in grp], gather=False)
                return scatters[i][-1]
        return None

    pos = positions.reshape(SEQ, 1).astype(F32)
    grad_x, small = _local_step(x[0], pos, mod, loss_target[0], small_w, fetch, emit)

    res, done = {}, grad_x
    for i, grp in enumerate(SCATTER_GROUPS):
        _, lands = exchange_wait(f"scatter{i}_wait", scatters[i], False, done)
        for n, land in zip(grp, lands, strict=True):
            res[n] = adamw(f"adamw_{n}", shard[n], [own[n], land], local("m_", n), local("v_", n), ADAM_TILE.get(n))
            done = res[n][0]
            res[n] = [as_output(n, r) for r in res[n]]
    (small_all,) = all_gather("gather_small", [_pack_small(small)], after=done)
    loss, small_res = adamw_small("adamw_small", small_all, {n: (args[n], args["m_" + n], args["v_" + n]) for n in SMALL_PARAMS})
    row, _, n_mod = SMALL_AT["b_ada"]
    dmod_all = small_all[:, row:row + n_mod // SMALL_COLS, :].reshape(N_DEV, n_mod)
    dmod_mine = lax.dynamic_slice_in_dim(dmod_all, me * (6 * D_MODEL // N_DEV), 6 * D_MODEL // N_DEV, axis=1)
    g_w_ada = matmul("ada_wgrad", sc_all, dmod_mine, "tn")
    res["w_ada"] = [r[None] for r in adamw("adamw_w_ada", shard["w_ada"], [g_w_ada], m_w_ada[0], v_w_ada[0], ADAM_TILE["w_ada"])]

    def leaf(kind, n):
        return res[n][kind] if n in res else small_res[n][kind]

    return (loss.reshape(()), grad_x[None], *[leaf(k, n) for k in range(4) for n in OUT_WEIGHTS])
```

```python
import jax
import jax.numpy as jnp
from jax import lax
from jax.experimental import pallas as pl
from jax.experimental.pallas import tpu as pltpu

F32 = jnp.float32
MXU_DTYPE = jnp.bfloat16

N_DEV = 8
D_MODEL = 1024
SEQ = 2048
HEADS = 8
NOPE = 64
ROPE = 32
Q_LORA = 512
KV_LORA = 256
DIL_DIM = 64
DIL_WIDTH = HEADS * DIL_DIM
DILATIONS = (1, 4, 16)
SPAN = 128
D_FF = 2816
LANES = 128
ROPE_THETA = 10000.0
EPS = 1e-6
NEG_INF = -1e30
ADAM_LR, ADAM_B1, ADAM_B2, ADAM_EPS, ADAM_WD, ADAM_STEP = 0.001, 0.9, 0.999, 1e-08, 0.01, 10
VMEM_LIMIT = 56 * 1024 * 1024
MESH_ID = pl.DeviceIdType.MESH

P_QLAT, P_KVLAT, P_KPE, P_QD, P_KD, P_VD, P_END = 0, 512, 768, 896, 1408, 1920, 2432
KPE_LO = 64
MIX_IN = HEADS * LANES + DIL_WIDTH


def _params(**kw):
    return pltpu.CompilerParams(vmem_limit_bytes=VMEM_LIMIT, **kw)


def rowwise(name, fn, rows, params, out_rows, out_accs=(), tm=512, dep=None):
    deps = [] if dep is None else [dep]
    rows = [r if isinstance(r, tuple) else (r, r.shape[1], 0) for r in rows]
    R = rows[0][0].shape[0]
    tm = min(tm, R)
    steps = R // tm
    assert steps * tm == R
    in_specs = []
    for a, width, cb in rows:
        ri = a.shape[0]
        per = ri // tm
        assert per * tm == ri
        if ri == R:
            in_specs.append(pl.BlockSpec((tm, width), lambda i, cb=cb: (i, cb)))
        else:
            in_specs.append(pl.BlockSpec((tm, width), lambda i, per=per, cb=cb: (i % per, cb)))
    for p in params:
        in_specs.append(pl.BlockSpec(p.shape, lambda i: (0,) * p.ndim))
    in_specs += [pl.BlockSpec(memory_space=pl.ANY)] * len(deps)
    out_shape = [jax.ShapeDtypeStruct((R, d), dt) for d, dt in out_rows]
    out_specs = [pl.BlockSpec((tm, d), lambda i: (i, 0)) for d, _ in out_rows]
    out_shape += [jax.ShapeDtypeStruct((1, n), F32) for n in out_accs]
    out_specs += [pl.BlockSpec((1, n), lambda i: (0, 0)) for n in out_accs]
    nr, npar, no, na = len(rows), len(params), len(out_rows), len(out_accs)

    def body(*refs):
        rvals = [r[...] for r in refs[:nr]]
        pvals = [r[...] for r in refs[nr:nr + npar]]
        outs, accs = fn(rvals, pvals)
        first_out = nr + npar + len(deps)
        for ref, v in zip(refs[first_out:first_out + no], outs, strict=True):
            ref[...] = v.astype(ref.dtype)
        if na:
            acc_refs = refs[first_out + no:]
            i = pl.program_id(0)

            @pl.when(i == 0)
            def _():
                for ref, v in zip(acc_refs, accs, strict=True):
                    ref[...] = v

            @pl.when(i > 0)
            def _():
                for ref, v in zip(acc_refs, accs, strict=True):
                    ref[...] += v

    res = pl.pallas_call(body, name=name, grid=(steps,), in_specs=in_specs, out_specs=out_specs,
                         out_shape=out_shape, compiler_params=_params())(*[r[0] for r in rows], *params, *deps)
    return list(res)


_DIMS = {"nn": ((1,), (0,)), "nt": ((1,), (1,)), "tn": ((0,), (0,))}


def _dot(a, b, mode="nn"):
    return lax.dot_general(a.astype(MXU_DTYPE), b.astype(MXU_DTYPE), (_DIMS[mode], ((), ())),
                           preferred_element_type=F32)


def matmul(name, a, b, mode, tm=None, tn=None, tk=None, out_dtype=F32, dep=None):
    if mode == "tn":
        K, M = a.shape
    else:
        M, K = a.shape
    N = b.shape[0] if mode == "nt" else b.shape[1]
    tm, tn, tk = tm or M, tn or N, tk or K
    nm, nn, nk = M // tm, N // tn, K // tk
    assert nm * tm == M and nn * tn == N and nk * tk == K
    a_spec = pl.BlockSpec((tk, tm), lambda i, j, k: (k, i)) if mode == "tn" else pl.BlockSpec((tm, tk), lambda i, j, k: (i, k))
    b_spec = pl.BlockSpec((tn, tk), lambda i, j, k: (j, k)) if mode == "nt" else pl.BlockSpec((tk, tn), lambda i, j, k: (k, j))
    deps = [] if dep is None else [dep]

    def body(a_ref, b_ref, *rest):
        o_ref, scratch = rest[len(deps)], rest[len(deps) + 1:]
        p = _dot(a_ref[...], b_ref[...], mode)
        if nk == 1:
            o_ref[...] = p.astype(o_ref.dtype)
        else:
            acc = scratch[0]
            k = pl.program_id(2)

            @pl.when(k == 0)
            def _():
                acc[...] = p

            @pl.when(k > 0)
            def _():
                acc[...] += p

            @pl.when(k == nk - 1)
            def _():
                o_ref[...] = acc[...].astype(o_ref.dtype)

    return pl.pallas_call(
        body, name=name, grid=(nm, nn, nk), in_specs=[a_spec, b_spec] + [pl.BlockSpec(memory_space=pl.ANY)] * len(deps),
        out_specs=pl.BlockSpec((tm, tn), lambda i, j, k: (i, j)),
        out_shape=jax.ShapeDtypeStruct((M, N), out_dtype),
        scratch_shapes=[pltpu.VMEM((tm, tn), F32)] if nk > 1 else [],
        compiler_params=_params())(a, b, *deps)


def _rms(x, g):
    rstd = lax.rsqrt(jnp.mean(x * x, axis=-1, keepdims=True) + EPS)
    n = x * rstd
    return n * g, n, rstd


def _rms_bwd(dy, n, rstd, g):
    dg = jnp.sum(dy * n, axis=0, keepdims=True)
    dn = dy * g
    dx = rstd * (dn - n * jnp.mean(dn * n, axis=-1, keepdims=True))
    return dx, dg


def _norm_bwd(dy, x, g):
    _, n, rstd = _rms(x, g)
    return _rms_bwd(dy, n, rstd, g)


def _colsum(v):
    return jnp.sum(v, axis=0, keepdims=True)


def _silu(x):
    return x * (1.0 / (1.0 + jnp.exp(-x)))


def _lane(shape):
    return lax.broadcasted_iota(jnp.int32, shape, 1)


def _group_mean(v, groups):
    i = lax.broadcasted_iota(jnp.int32, (LANES, LANES), 0)
    j = lax.broadcasted_iota(jnp.int32, (LANES, LANES), 1)
    g = jnp.zeros((LANES, LANES), F32)
    for lo, hi in groups:
        g = jnp.where((i >= lo) & (i < hi) & (j >= lo) & (j < hi), 1.0 / (hi - lo), g)
    head = v.astype(MXU_DTYPE)
    return _dot(head, g) + _dot(v - head.astype(F32), g)


def _in_groups(shape, groups):
    lane = _lane(shape)
    m = jnp.zeros(shape, jnp.bool_)
    for lo, hi in groups:
        m = m | ((lane >= lo) & (lane < hi))
    return m


def _grms(x, g, groups):
    rstd = lax.rsqrt(_group_mean(x * x, groups) + EPS)
    n = jnp.where(_in_groups(x.shape, groups), x * rstd, 0.0)
    return n * g, n, rstd


def _grms_bwd(dy, n, rstd, g, groups):
    dn = dy * g
    return rstd * (dn - n * _group_mean(dn * n, groups)), _colsum(dy * n)


def _rot(x, half, transpose=False):
    first = (_lane(x.shape) % (2 * half)) < half
    up = pltpu.roll(x, LANES - half, axis=1)
    down = pltpu.roll(x, half, axis=1)
    return jnp.where(first, up, -down) if transpose else jnp.where(first, -up, down)


def _rope(x, cos, sin, half):
    return x * cos + _rot(x, half) * sin


def _rope_bwd(dy, cos, sin, half):
    return dy * cos + _rot(dy * sin, half, transpose=True)


def _chunks(x):
    return [x[:, i:i + LANES] for i in range(0, x.shape[1], LANES)]


Q_GROUPS = ((0, NOPE), (NOPE, NOPE + ROPE))
K_GROUPS = ((0, NOPE),)
KPE_GROUPS = ((KPE_LO, KPE_LO + ROPE),)
DIL_GROUPS = ((0, DIL_DIM), (DIL_DIM, 2 * DIL_DIM))


def _col(width, rows=SEQ):
    return pl.BlockSpec((rows, width), lambda h: (0, h))


def _causal_tail(s, tq, fill):
    diag = s[:, s.shape[1] - tq:]
    keep = lax.broadcasted_iota(jnp.int32, diag.shape, 1) <= lax.broadcasted_iota(jnp.int32, diag.shape, 0)
    diag = jnp.where(keep, diag, fill)
    return diag if s.shape[1] == tq else jnp.concatenate([s[:, :s.shape[1] - tq], diag], axis=1)


def mla_fwd(name, q, k, v, scale, tq=256):
    S = q.shape[0]

    def body(q_ref, k_ref, v_ref, o_ref, lse_ref):
        nb = S // tq
        blk = lambda i: slice(i * tq, (i + 1) * tq)

        def scores(i):
            return _dot(q_ref[blk(i), :], k_ref[:(i + 1) * tq, :], "nt")

        def softmax(i, s):
            s = _causal_tail(s * scale, tq, NEG_INF)
            m = jnp.max(s, axis=-1, keepdims=True)
            e = jnp.exp(s - m)
            l = jnp.sum(e, axis=-1, keepdims=True)
            lse_ref[0, blk(i), :] = m + jnp.log(l)
            return (e * (1.0 / l)).astype(MXU_DTYPE)

        def weighted(i, p):
            o_ref[blk(i), :] = _dot(p, v_ref[:(i + 1) * tq, :])

        s, p_prev = scores(0), None
        for i in range(nb):
            s_next = scores(i + 1) if i + 1 < nb else None
            if p_prev is not None:
                weighted(i - 1, p_prev)
            p_prev, s = softmax(i, s), s_next
        weighted(nb - 1, p_prev)

    return pl.pallas_call(
        body, name=name, grid=(HEADS,), in_specs=[_col(LANES)] * 3,
        out_specs=[_col(LANES), pl.BlockSpec((1, S, 1), lambda h: (h, 0, 0))],
        out_shape=[jax.ShapeDtypeStruct((S, MIX_IN), F32), jax.ShapeDtypeStruct((HEADS, S, 1), F32)],
        compiler_params=_params())(q, k, v)


def mla_bwd(name, q, k, v, o, do, lse, scale, tq=256):
    S = q.shape[0]

    def body(q_ref, k_ref, v_ref, o_ref, do_ref, lse_ref, dq_ref, dkv_ref, dkpe_ref, dk_acc, dv_acc):
        dk_acc[...] = jnp.zeros_like(dk_acc)
        dv_acc[...] = jnp.zeros_like(dv_acc)
        for i in range(S // tq):
            kext = (i + 1) * tq
            blk = slice(i * tq, kext)
            qi, kk, vv = q_ref[blk, :], k_ref[:kext, :], v_ref[:kext, :]
            doi = do_ref[blk, :]
            s = _causal_tail(_dot(qi, kk, "nt") * scale, tq, NEG_INF)
            p = jnp.exp(s - lse_ref[0, blk, :])
            dp = _dot(doi, vv, "nt")
            delta = jnp.sum(doi * o_ref[blk, :], axis=-1, keepdims=True)
            ds = p * (dp - delta) * scale
            dq_ref[blk, :] = _dot(ds, kk)
            dk_acc[:kext, :] += _dot(ds, qi, "tn")
            dv_acc[:kext, :] += _dot(p, doi, "tn")
        dk = dk_acc[...]
        lane = _lane(dk.shape)
        dkv_ref[...] = jnp.where(lane < NOPE, dk, 0.0) + dv_acc[...]
        dkpe = jnp.where((lane >= KPE_LO) & (lane < KPE_LO + ROPE), dk, 0.0)
        h = pl.program_id(0)

        @pl.when(h == 0)
        def _():
            dkpe_ref[...] = dkpe

        @pl.when(h > 0)
        def _():
            dkpe_ref[...] += dkpe

    return pl.pallas_call(
        body, name=name, grid=(HEADS,),
        in_specs=[_col(LANES)] * 5 + [pl.BlockSpec((1, S, 1), lambda h: (h, 0, 0))],
        out_specs=[_col(LANES), _col(LANES), pl.BlockSpec((S, LANES), lambda h: (0, 0))],
        out_shape=[jax.ShapeDtypeStruct((S, HEADS * LANES), F32), jax.ShapeDtypeStruct((S, HEADS * LANES), F32),
                   jax.ShapeDtypeStruct((S, LANES), F32)],
        scratch_shapes=[pltpu.VMEM((S, LANES), F32), pltpu.VMEM((S, LANES), F32)],
        compiler_params=_params())(q, k, v, o, do, lse)


BAND_TQ = SPAN


def _band_blocks(L, tq):
    return [(i * tq, (i + 1) * tq, max(0, i * tq - SPAN)) for i in range(L // tq)]


def _class_rows(r, dil, lo, hi):
    return pl.ds(r + dil * lo, hi - lo, stride=dil) if dil > 1 else pl.ds(lo, hi - lo)


def _stack_heads(t, lo):
    zero = jnp.zeros_like(t)
    return jnp.concatenate([jnp.where(lo, t, zero), jnp.where(lo, zero, t)], axis=0)


def _band_mask2(q0, q1, k0):
    n = q1 - q0
    shape = (2 * n, q1 - k0)
    i = lax.broadcasted_iota(jnp.int32, shape, 0)
    dist = (jnp.where(i >= n, i - n, i) + q0) - (lax.broadcasted_iota(jnp.int32, shape, 1) + k0)
    return (dist >= 0) & (dist <= SPAN)


def _pair_col(col0=0):
    return pl.BlockSpec((SEQ, LANES), lambda j: (0, col0 // LANES + j))


def band_fwd(name, q, k, v, dil, dep=None):
    S = q.shape[0]
    L = S // dil
    tq = BAND_TQ
    scale = DIL_DIM ** -0.5
    deps = [] if dep is None else [dep]

    def body(q_ref, k_ref, v_ref, *rest):
        o_ref, lse_ref = rest[len(deps):]
        items = [(r, blk) for r in range(dil) for blk in _band_blocks(L, tq)]
        lo = _lane((tq, LANES)) < DIL_DIM

        def scores(item):
            r, (q0, q1, k0) = item
            qb = q_ref[_class_rows(r, dil, q0, q1), :].astype(MXU_DTYPE)
            return _dot(_stack_heads(qb, lo), k_ref[_class_rows(r, dil, k0, q1), :], "nt")

        def softmax(item, s):
            _, (q0, q1, k0) = item
            s = jnp.where(_band_mask2(q0, q1, k0), s * scale, NEG_INF)
            mx = jnp.max(s, axis=-1, keepdims=True)
            e = jnp.exp(s - mx)
            l = jnp.sum(e, axis=-1, keepdims=True)
            return (e * (1.0 / l)).astype(MXU_DTYPE), mx + jnp.log(l)

        def weighted(item, p, lse):
            r, (q0, q1, k0) = item
            pv = _dot(p, v_ref[_class_rows(r, dil, k0, q1), :])
            o_ref[_class_rows(r, dil, q0, q1), :] = jnp.where(lo, pv[:tq], pv[tq:])
            lse_ref[_class_rows(r, dil, q0, q1), :] = jnp.where(lo, lse[:tq], lse[tq:])

        s, prev = scores(items[0]), None
        for i, item in enumerate(items):
            s_next = scores(items[i + 1]) if i + 1 < len(items) else None
            if prev is not None:
                weighted(items[i - 1], *prev)
            prev, s = softmax(item, s), s_next
        weighted(items[-1], *prev)

    return pl.pallas_call(
        body, name=name, grid=(DIL_WIDTH // LANES,),
        in_specs=[_pair_col()] * 2 + [_pair_col(P_VD)] + [pl.BlockSpec(memory_space=pl.ANY)] * len(deps), out_specs=[_pair_col()] * 2,
        out_shape=[jax.ShapeDtypeStruct((S, DIL_WIDTH), F32)] * 2, compiler_params=_params())(q, k, v, *deps)


def band_bwd(name, q, k, v, lse, lse_mix, o_cat, do_cat, dil, before=None):
    S = q.shape[0]
    L = S // dil
    tq = BAND_TQ
    scale = DIL_DIM ** -0.5
    before = list(before or [])

    def body(q_ref, k_ref, v_ref, lse_ref, mix_ref, o_ref, do_ref, *rest):
        dq_ref, dk_ref, dv_ref = rest[len(before):]
        if before:
            dq0_ref, dk0_ref, dv0_ref = rest[:3]
            dk_ref[...] = dk0_ref[...]
            dv_ref[...] = dv0_ref[...]
        else:
            dk_ref[...] = jnp.zeros_like(dk_ref)
            dv_ref[...] = jnp.zeros_like(dv_ref)
        items = [(r, blk) for r in range(dil) for blk in _band_blocks(L, tq)]
        lo = _lane((tq, LANES)) < DIL_DIM
        per_head = lambda t: jnp.concatenate([t[:, 0:1], t[:, DIL_DIM:DIL_DIM + 1]], axis=0)

        def scores(item):
            r, (q0, q1, k0) = item
            qrows, krows = _class_rows(r, dil, q0, q1), _class_rows(r, dil, k0, q1)
            lse_p, dout = lse_ref[qrows, :], do_ref[qrows, :]
            w2 = per_head(jnp.exp(lse_p - mix_ref[qrows, :]))
            dd = dout * o_ref[qrows, :]
            big_d = jnp.concatenate([jnp.sum(jnp.where(lo, dd, 0.0), axis=-1, keepdims=True),
                                     jnp.sum(jnp.where(lo, 0.0, dd), axis=-1, keepdims=True)], axis=0)
            q2 = _stack_heads(q_ref[qrows, :].astype(MXU_DTYPE), lo)
            dom = (_stack_heads(dout, lo) * w2).astype(MXU_DTYPE)
            return (_dot(q2, k_ref[krows, :], "nt"), _dot(dom, v_ref[krows, :], "nt"), per_head(lse_p), w2 * big_d, q2, dom)

        def softmax_bwd(item, s, dp, lse2, wd2, q2, dom):
            _, (q0, q1, k0) = item
            p = jnp.where(_band_mask2(q0, q1, k0), jnp.exp(s * scale - lse2), 0.0)
            return p.astype(MXU_DTYPE), (p * (dp - wd2) * scale).astype(MXU_DTYPE), q2, dom

        def grads(item, p, ds, q2, dom):
            r, (q0, q1, k0) = item
            qrows, krows = _class_rows(r, dil, q0, q1), _class_rows(r, dil, k0, q1)
            dq2 = _dot(ds, k_ref[krows, :])
            dq = jnp.where(lo, dq2[:tq], dq2[tq:])
            dq_ref[qrows, :] = dq + dq0_ref[qrows, :] if before else dq
            dk_ref[krows, :] += _dot(ds, q2, "tn")
            dv_ref[krows, :] += _dot(p, dom, "tn")

        sc, prev = scores(items[0]), None
        for i, item in enumerate(items):
            sc_next = scores(items[i + 1]) if i + 1 < len(items) else None
            if prev is not None:
                grads(items[i - 1], *prev)
            prev, sc = softmax_bwd(item, *sc), sc_next
        grads(items[-1], *prev)

    cat = _pair_col(HEADS * LANES)
    return pl.pallas_call(
        body, name=name, grid=(DIL_WIDTH // LANES,),
        in_specs=[_pair_col()] * 2 + [_pair_col(P_VD)] + [_pair_col()] * 2 + [cat] * 2 + [_pair_col()] * len(before),
        out_specs=[_pair_col()] * 3, out_shape=[jax.ShapeDtypeStruct((S, DIL_WIDTH), F32)] * 3,
        compiler_params=_params())(q, k, v, lse, lse_mix, o_cat, do_cat, *before)


def combine_fwd(name, outs, lses, o_cat, tm=512):
    S = outs[0].shape[0]

    def body(o1, o2, o3, l1, l2, l3, cat_in, cat_out, mix_ref):
        ls = [l1[...], l2[...], l3[...]]
        m = jnp.maximum(jnp.maximum(ls[0], ls[1]), ls[2])
        e = [jnp.exp(l - m) for l in ls]
        den = e[0] + e[1] + e[2]
        cat_out[...] = (e[0] / den) * o1[...] + (e[1] / den) * o2[...] + (e[2] / den) * o3[...]
        mix_ref[...] = m + jnp.log(den)

    row = pl.BlockSpec((tm, DIL_WIDTH), lambda i: (i, 0))
    return pl.pallas_call(
        body, name=name, grid=(S // tm,), in_specs=[row] * 6 + [pl.BlockSpec(memory_space=pl.ANY)],
        out_specs=[pl.BlockSpec((tm, DIL_WIDTH), lambda i: (i, HEADS * LANES // DIL_WIDTH)), row],
        out_shape=[jax.ShapeDtypeStruct(o_cat.shape, F32), jax.ShapeDtypeStruct((S, DIL_WIDTH), F32)],
        input_output_aliases={6: 0}, compiler_params=_params())(*outs, *lses, o_cat)


def _shift_down(u, n, zero_head):
    out = pltpu.roll(u, n, axis=0)
    return jnp.where(lax.broadcasted_iota(jnp.int32, u.shape, 0) >= n, out, 0.0) if zero_head else out


def _shift_up(u, n, zero_tail):
    rows = u.shape[0]
    out = pltpu.roll(u, rows - n, axis=0)
    return jnp.where(lax.broadcasted_iota(jnp.int32, u.shape, 0) < rows - n, out, 0.0) if zero_tail else out


CONV_ROWS = 512
CONV_HALO = 16


def _conv_chunks(S, tail):
    out = []
    for r0 in range(0, S, CONV_ROWS):
        lo, hi = max(0, r0 - CONV_HALO), min(S, r0 + CONV_ROWS + (CONV_HALO if tail else 0))
        out.append((lo, hi, r0 - lo, CONV_ROWS))
    return out


CONV_TC = 256
CONV_NB = D_FF // CONV_TC


def _half_specs(rows, rows_axis=False):
    if rows_axis:
        return [pl.BlockSpec((rows, D_MODEL), lambda j: (j, 0)), pl.BlockSpec((rows, D_MODEL), lambda j: (j + CONV_NB, 0))]
    return [pl.BlockSpec((rows, CONV_TC), lambda j: (0, j)), pl.BlockSpec((rows, CONV_TC), lambda j: (0, j + CONV_NB))]


def _whole(a):
    return pl.BlockSpec(a.shape, lambda j: (0,) * a.ndim)


def _up_pair(h, ug_ref, uv_ref):
    return jnp.concatenate([_dot(h, ug_ref[...], "nt"), _dot(h, uv_ref[...], "nt")], axis=1)


def _conv_taps(uin, w, b, starts):
    u1, u2 = _shift_down(uin, 1, starts), _shift_down(uin, 2, starts)
    return u1, u2, w[2:3, :] * uin + w[1:2, :] * u1 + w[0:1, :] * u2 + b


def ffn_fwd(name, h, w_up_t, w_conv, b_conv, w_down):
    S = h.shape[0]

    def body(h_ref, ug_ref, uv_ref, wg_ref, wv_ref, bg_ref, bv_ref, wd_ref, dn_ref, up_ref):
        @pl.when(pl.program_id(0) == 0)
        def _():
            dn_ref[...] = jnp.zeros_like(dn_ref)

        w = jnp.concatenate([wg_ref[...], wv_ref[...]], axis=1)
        b = jnp.concatenate([bg_ref[...], bv_ref[...]], axis=1)
        chunks = _conv_chunks(S, tail=False)

        def project(c):
            lo, hi, keep, rows = c
            uin = _up_pair(h_ref[lo:hi, :], ug_ref, uv_ref)
            up_ref[lo + keep:lo + keep + rows, :] = uin[keep:keep + rows]
            return uin

        def gate(c, uin):
            lo, hi, keep, rows = c
            u = _conv_taps(uin, w, b, lo == 0)[2][keep:keep + rows]
            return (_silu(u[:, :CONV_TC]) * u[:, CONV_TC:]).astype(MXU_DTYPE)

        def project_down(c, act):
            dn_ref[c[0] + c[2]:c[0] + c[2] + c[3], :] += _dot(act, wd_ref[...])

        uin, act_prev = project(chunks[0]), None
        for i, c in enumerate(chunks):
            uin_next = project(chunks[i + 1]) if i + 1 < len(chunks) else None
            if act_prev is not None:
                project_down(chunks[i - 1], act_prev)
            act_prev = gate(c, uin)
            uin = uin_next
        project_down(chunks[-1], act_prev)

    return pl.pallas_call(
        body, name=name, grid=(CONV_NB,),
        in_specs=[_whole(h)] + _half_specs(CONV_TC, rows_axis=True) + _half_specs(3) + _half_specs(1)
        + [pl.BlockSpec((CONV_TC, w_down.shape[1]), lambda j: (j, 0))],
        out_specs=[pl.BlockSpec((S, w_down.shape[1]), lambda j: (0, 0)), pl.BlockSpec((S, 2 * CONV_TC), lambda j: (0, j))],
        out_shape=[jax.ShapeDtypeStruct((S, w_down.shape[1]), F32), jax.ShapeDtypeStruct((S, 2 * D_FF), F32)],
        compiler_params=_params())(h, w_up_t, w_up_t, w_conv, w_conv, b_conv, b_conv, w_down)


def ffn_bwd(name, h, up, w_up_t, w_conv, b_conv, d_dn, w_down):
    S, D = h.shape

    def body(h_ref, up_ref, ug_ref, uv_ref, wg_ref, wv_ref, bg_ref, bv_ref, dd_ref, wd_ref,
             dh_ref, gup_ref, gd_ref, dwg_ref, dwv_ref, dbg_ref, dbv_ref):
        @pl.when(pl.program_id(0) == 0)
        def _():
            dh_ref[...] = jnp.zeros_like(dh_ref)

        w = jnp.concatenate([wg_ref[...], wv_ref[...]], axis=1)
        b = jnp.concatenate([bg_ref[...], bv_ref[...]], axis=1)
        w_pair = jnp.concatenate([ug_ref[...], uv_ref[...]], axis=0)
        chunks = _conv_chunks(S, tail=True)

        def project(c):
            return up_ref[c[0]:c[1], :], _dot(dd_ref[c[0]:c[1], :], wd_ref[...], "nt")

        def through_conv(c, uin, da):
            lo, hi, keep, rows = c
            u1, u2, u = _conv_taps(uin, w, b, lo == 0)
            gate, val = u[:, :CONV_TC], u[:, CONV_TC:]
            sig = 1.0 / (1.0 + jnp.exp(-gate))
            du = jnp.concatenate([da * val * (sig * (1.0 + gate * (1.0 - sig))), da * (gate * sig)], axis=1)
            dup = w[2:3, :] * du + w[1:2, :] * _shift_up(du, 1, hi == S) + w[0:1, :] * _shift_up(du, 2, hi == S)
            kept = slice(keep, keep + rows)
            du = du[kept]
            dw = jnp.concatenate([_colsum(du * u2[kept]), _colsum(du * u1[kept]), _colsum(du * uin[kept])], axis=0)
            return dup[kept].astype(MXU_DTYPE), (gate * sig * val)[kept].astype(MXU_DTYPE), dw, _colsum(du)

        def weight_grads(c, dup, act):
            out_rows = slice(c[0] + c[2], c[0] + c[2] + c[3])
            dh_ref[out_rows, :] += _dot(dup, w_pair)
            return _dot(dup, h_ref[out_rows, :], "tn"), _dot(act, dd_ref[out_rows, :], "tn")

        dw, db, g_up, g_dn = 0.0, 0.0, 0.0, 0.0
        proj, done = project(chunks[0]), None
        for i, c in enumerate(chunks):
            proj_next = project(chunks[i + 1]) if i + 1 < len(chunks) else None
            if done is not None:
                gu, gd = weight_grads(chunks[i - 1], *done)
                g_up, g_dn = g_up + gu, g_dn + gd
            dup, act, dw_c, db_c = through_conv(c, *proj)
            dw, db, done, proj = dw + dw_c, db + db_c, (dup, act), proj_next
        gu, gd = weight_grads(chunks[-1], *done)
        g_up, g_dn = g_up + gu, g_dn + gd
        gup_ref[0], gup_ref[1] = g_up[:CONV_TC].astype(gup_ref.dtype), g_up[CONV_TC:].astype(gup_ref.dtype)
        gd_ref[...] = g_dn.astype(gd_ref.dtype)
        dwg_ref[...], dwv_ref[...] = dw[:, :CONV_TC], dw[:, CONV_TC:]
        dbg_ref[...], dbv_ref[...] = db[:, :CONV_TC], db[:, CONV_TC:]

    half = lambda rows: pl.BlockSpec((rows, CONV_TC), lambda j: (0, j))
    rows_blk = pl.BlockSpec((CONV_TC, D), lambda j: (j, 0))
    dh, gup, gd, dwg, dwv, dbg, dbv = pl.pallas_call(
        body, name=name, grid=(CONV_NB,),
        in_specs=[_whole(h), pl.BlockSpec((S, 2 * CONV_TC), lambda j: (0, j))] + _half_specs(CONV_TC, rows_axis=True) + _half_specs(3)
        + _half_specs(1) + [_whole(d_dn), rows_blk],
        out_specs=[pl.BlockSpec((S, D), lambda j: (0, 0)), pl.BlockSpec((2, CONV_TC, D), lambda j: (0, j, 0)), rows_blk,
                   half(3), half(3), half(1), half(1)],
        out_shape=[jax.ShapeDtypeStruct((S, D), F32), jax.ShapeDtypeStruct((2, D_FF, D), MXU_DTYPE),
                   jax.ShapeDtypeStruct((D_FF, D), MXU_DTYPE)]
        + [jax.ShapeDtypeStruct((3, D_FF), F32)] * 2 + [jax.ShapeDtypeStruct((1, D_FF), F32)] * 2,
        compiler_params=_params())(h, up, w_up_t, w_up_t, w_conv, w_conv, b_conv, b_conv, d_dn, w_down)
    return dh, gup.reshape(2 * D_FF, D), gd, jnp.concatenate([dwg, dwv], axis=1), jnp.concatenate([dbg, dbv], axis=1)


def adamw(name, w, parts, m, v, tr=None):
    apart = w.ndim == 3
    R, C = w.shape[0], w.shape[-1]
    tr = tr or R
    assert R % tr == 0
    c1 = 1.0 - ADAM_B1 ** ADAM_STEP
    c2 = 1.0 - ADAM_B2 ** ADAM_STEP
    np_ = len(parts)

    def body(*refs):
        w_ref, m_ref, v_ref = refs[0], refs[1 + np_], refs[2 + np_]
        go_ref, d_ref, mo_ref, vo_ref = refs[3 + np_:]
        terms = []
        for part, ref in zip(parts, refs[1:1 + np_], strict=True):
            terms += [ref[...]] if part.ndim == 2 else [ref[p] for p in range(part.shape[0])]
        g = terms[0].astype(F32)
        for term in terms[1:]:
            g = g + term.astype(F32)
        m2 = ADAM_B1 * m_ref[...] + (1.0 - ADAM_B1) * g
        v2 = ADAM_B2 * v_ref[...] + (1.0 - ADAM_B2) * (g * g)
        go_ref[...] = g
        mo_ref[...] = m2
        vo_ref[...] = v2
        d_ref[...] = -ADAM_LR * ((m2 / c1) / (jnp.sqrt(v2 / c2) + ADAM_EPS) + ADAM_WD * w_ref[...])

    blk = pl.BlockSpec((tr, C), lambda i: (i, 0))
    own = pl.BlockSpec((tr, None, C), lambda i: (i, 0, 0)) if apart else blk
    part_specs = [blk if p.ndim == 2 else pl.BlockSpec((p.shape[0], tr, C), lambda i: (0, i, 0)) for p in parts]
    return pl.pallas_call(
        body, name=name, grid=(R // tr,),
        in_specs=[own] + part_specs + [own, own], out_specs=[own] * 4,
        out_shape=[jax.ShapeDtypeStruct(w.shape, F32)] * 4, compiler_params=_params())(w, *parts, m, v)


def _place():
    return lax.axis_index("x"), lax.axis_index("y"), lax.axis_index("c")


def all_gather(name, arrs, after=None):
    n = len(arrs)
    deps = [] if after is None else [after]

    def body(*refs):
        ins, outs = refs[:n], refs[n + len(deps):2 * n + len(deps)]
        send_sems, recv_sems, local_sems = refs[2 * n + len(deps):]
        x, y, c = _place()
        me, sibling = (x, y, c), (x, y, 1 - c)
        chips = [(1 - x, y), (x, 1 - y), (1 - x, 1 - y)]
        sends = []
        for t in range(n):
            out = outs[t]

            def slot(px, py, pc, out=out):
                return out.at[4 * px + 2 * py + pc]

            def copy(k, block, to, src=None, t=t, slot=slot):
                return pltpu.make_async_remote_copy(
                    src_ref=slot(*block) if src is None else src, dst_ref=slot(*block),
                    send_sem=send_sems.at[7 * t + k], recv_sem=recv_sems.at[7 * t + k],
                    device_id=to, device_id_type=MESH_ID)

            mine = pltpu.make_async_copy(ins[t], slot(*me), local_sems.at[t])
            mine.start()
            first = [copy(0, me, sibling, src=ins[t])]
            first += [copy(1 + j, me, (*chip, c), src=ins[t]) for j, chip in enumerate(chips)]
            for cp in first:
                cp.start()
            sends.append((mine, first, copy))
        for t in range(n):
            mine, first, copy = sends[t]
            passed = [copy(4 + j, (*chip, c), sibling) for j, chip in enumerate(chips)]
            for j, chip in enumerate(chips):
                copy(1 + j, (*chip, c), me).wait_recv()
                passed[j].start()
            copy(0, sibling, me).wait_recv()
            for j, chip in enumerate(chips):
                copy(4 + j, (*chip, 1 - c), me).wait_recv()
            for cp in first + passed:
                cp.wait_send()
            mine.wait()

    any_spec = pl.BlockSpec(memory_space=pl.ANY)
    res = pl.pallas_call(
        body, name=name, in_specs=[any_spec] * (n + len(deps)), out_specs=[any_spec] * n,
        out_shape=[jax.ShapeDtypeStruct((N_DEV,) + a.shape, a.dtype) for a in arrs],
        scratch_shapes=[pltpu.SemaphoreType.DMA((7 * n,)), pltpu.SemaphoreType.DMA((7 * n,)), pltpu.SemaphoreType.DMA((n,))],
        compiler_params=pltpu.CompilerParams(has_side_effects=True))(*arrs, *deps)
    return list(res)


def ada_modulation(name, c, w_ada):
    n_mod = w_ada.shape[1]

    def exchange(src_ref, dst_ref, send_sems, recv_sems):
        x, y, c_ = _place()
        me = 4 * x + 2 * y + c_
        copies = []
        for k in range(1, N_DEV):
            px, py, pc = x ^ (k >> 2), y ^ ((k >> 1) & 1), c_ ^ (k & 1)
            copies.append(pltpu.make_async_remote_copy(
                src_ref=src_ref, dst_ref=dst_ref.at[me], send_sem=send_sems.at[k - 1], recv_sem=recv_sems.at[k - 1],
                device_id=(px, py, pc), device_id_type=MESH_ID))
        for cp in copies:
            cp.start()
        for cp in copies:
            cp.wait_recv()
        for cp in copies:
            cp.wait_send()
        return me

    def body(c_ref, w_ref, sc_ref, mod_ref, c_all, send_c, recv_c, send_m, recv_m):
        me = exchange(c_ref, c_all, send_c, recv_c)
        c_all[me] = c_ref[...]
        sc = _silu(jnp.concatenate([c_all[p] for p in range(N_DEV)], axis=0))
        sc_ref[...] = sc.astype(sc_ref.dtype)
        mod_ref[me] = _dot(sc, w_ref[...])
        exchange(mod_ref.at[me], mod_ref, send_m, recv_m)

    vmem = pl.BlockSpec(memory_space=pltpu.VMEM)
    return pl.pallas_call(
        body, name=name, in_specs=[vmem, vmem], out_specs=[vmem, vmem],
        out_shape=[jax.ShapeDtypeStruct((N_DEV, c.shape[1]), MXU_DTYPE), jax.ShapeDtypeStruct((N_DEV, N_DEV, n_mod), F32)],
        scratch_shapes=[pltpu.VMEM((N_DEV, 1, c.shape[1]), F32)] + [pltpu.SemaphoreType.DMA((N_DEV - 1,))] * 4,
        compiler_params=pltpu.CompilerParams(has_side_effects=True, vmem_limit_bytes=VMEM_LIMIT))(c, w_ada)


HBM_SPEC = pl.BlockSpec(memory_space=pltpu.HBM)
SEM_SPEC = pl.BlockSpec(memory_space=pltpu.SEMAPHORE)
DATAFLOW = pltpu.SideEffectType.DATAFLOW_SIDE_EFFECTING


def _exchange_copies(srcs, lands, send_sems, recv_sems, gather, first=0):
    x, y, c = _place()
    me = 4 * x + 2 * y + c
    out = []
    for t, (src, land) in enumerate(zip(srcs, lands, strict=True)):
        for k in range(1, N_DEV):
            px, py, pc = x ^ (k >> 2), y ^ ((k >> 1) & 1), c ^ (k & 1)
            sem = 7 * (first + t) + k - 1
            out.append((k, pltpu.make_async_remote_copy(
                src_ref=src if gather else src.at[4 * px + 2 * py + pc],
                dst_ref=land.at[me] if gather else land.at[k - 1],
                send_sem=send_sems.at[sem], recv_sem=recv_sems.at[sem],
                device_id=(px, py, pc), device_id_type=MESH_ID)))
    return out


TREE_DIRECT = (1, 2, 4, 6)
TREE_FORWARDED = (3, 5, 7)


def exchange_start(name, arrs, gather, after=None, tree=False):
    n = len(arrs)
    lands = [lax.empty(((N_DEV,) + a.shape) if gather else ((N_DEV - 1,) + a.shape[1:]), a.dtype) for a in arrs]
    deps = [] if after is None else [after]

    def body(*refs):
        srcs, land_refs = refs[:n], refs[n:2 * n]
        send_sems, recv_sems = refs[2 * n + len(deps)], refs[2 * n + len(deps) + 1]
        token = refs[-1]
        for k, cp in _exchange_copies(srcs, land_refs, send_sems, recv_sems, gather):
            if not tree or k in TREE_DIRECT:
                cp.start()
        token[...] = jnp.zeros_like(token)

    hbm = lambda a: pltpu.HBM(a.shape, a.dtype)
    res = pl.pallas_call(
        body, name=name,
        out_shape=(pltpu.SemaphoreType.DMA((7 * n,)), pltpu.SemaphoreType.DMA((7 * n,)), *[hbm(a) for a in arrs],
                   *[hbm(l) for l in lands], jax.ShapeDtypeStruct((8, 128), F32)),
        in_specs=[HBM_SPEC] * (2 * n) + [pl.BlockSpec(memory_space=pl.ANY)] * len(deps),
        out_specs=(SEM_SPEC, SEM_SPEC, *[HBM_SPEC] * (2 * n), pl.BlockSpec(memory_space=pltpu.VMEM)),
        input_output_aliases={i: 2 + i for i in range(2 * n)},
        compiler_params=pltpu.CompilerParams(has_side_effects=DATAFLOW),
    )(*[pltpu.with_memory_space_constraint(a, pltpu.HBM) for a in arrs + lands], *deps)
    return res[0], res[1], list(res[2:2 + n]), list(res[2 + n:2 + 2 * n]), res[-1]


def exchange_forward(name, started, after, first=0, count=None):
    send_sems, recv_sems, srcs, lands, _ = started
    count = len(srcs) - first if count is None else count
    mine = lands[first:first + count]
    n = len(mine)

    def copies(land_refs, send_ref, recv_ref):
        x, y, c = _place()
        out = []
        for t, land in enumerate(land_refs):
            for k in (2, 4, 6):
                slot = land.at[4 * (x ^ (k >> 2)) + 2 * (y ^ ((k >> 1) & 1)) + c]
                came, goes = 7 * (first + t) + k - 1, 7 * (first + t) + (k ^ 1) - 1
                out.append((
                    pltpu.make_async_remote_copy(src_ref=slot, dst_ref=slot, send_sem=send_ref.at[came], recv_sem=recv_ref.at[came],
                                                 device_id=(x, y, c), device_id_type=MESH_ID),
                    pltpu.make_async_remote_copy(src_ref=slot, dst_ref=slot, send_sem=send_ref.at[goes], recv_sem=recv_ref.at[goes],
                                                 device_id=(x, y, 1 - c), device_id_type=MESH_ID)))
        return out

    def arrived(*refs):
        for came, _ in copies(refs[:n], refs[n], refs[n + 1]):
            came.wait_recv()

    def pass_on(*refs):
        for _, goes in copies(refs[:n], refs[n], refs[n + 1]):
            goes.start()
        refs[-1][...] = jnp.zeros_like(refs[-1])

    hbm = lambda a: pltpu.HBM(a.shape, a.dtype)
    here = pl.pallas_call(
        arrived, name=name + "_arrived", out_shape=tuple(hbm(a) for a in mine),
        in_specs=[HBM_SPEC] * n + [SEM_SPEC, SEM_SPEC, pl.BlockSpec(memory_space=pl.ANY)],
        out_specs=tuple([HBM_SPEC] * n), input_output_aliases={i: i for i in range(n)},
        compiler_params=pltpu.CompilerParams(has_side_effects=DATAFLOW),
    )(*mine, send_sems, recv_sems, after)
    res = pl.pallas_call(
        pass_on, name=name, out_shape=(*[hbm(a) for a in mine], jax.ShapeDtypeStruct((8, 128), F32)),
        in_specs=[HBM_SPEC] * n + [SEM_SPEC, SEM_SPEC],
        out_specs=(*[HBM_SPEC] * n, pl.BlockSpec(memory_space=pltpu.VMEM)), input_output_aliases={i: i for i in range(n)},
        compiler_params=pltpu.CompilerParams(has_side_effects=DATAFLOW),
    )(*here, send_sems, recv_sems)
    lands = lands[:first] + list(res[:n]) + lands[first + count:]
    return (send_sems, recv_sems, srcs, lands, res[-1])


def exchange_wait(name, started, gather, after, first=0, count=None, tree=False):
    send_sems, recv_sems, srcs, lands, _ = started
    count = len(srcs) - first if count is None else count
    srcs, lands = srcs[first:first + count], lands[first:first + count]
    n = len(srcs)

    def body(*refs):
        src_refs, land_refs = refs[:n], refs[n:2 * n]
        copies = _exchange_copies(src_refs, land_refs, refs[2 * n], refs[2 * n + 1], gather, first)
        for _, cp in copies:
            cp.wait_send()
        for k, cp in copies:
            if not tree or k in (1,) + TREE_FORWARDED:
                cp.wait_recv()

    hbm = lambda a: pltpu.HBM(a.shape, a.dtype)
    res = pl.pallas_call(
        body, name=name, out_shape=tuple(hbm(a) for a in srcs + lands),
        in_specs=[HBM_SPEC] * (2 * n) + [SEM_SPEC, SEM_SPEC, pl.BlockSpec(memory_space=pl.ANY)],
        out_specs=tuple([HBM_SPEC] * (2 * n)), input_output_aliases={i: i for i in range(2 * n)},
        compiler_params=pltpu.CompilerParams(has_side_effects=DATAFLOW),
    )(*srcs, *lands, send_sems, recv_sems, after)
    return list(res[:n]), list(res[n:])


def _gather_cols(stack):
    p, k, n = stack.shape
    return stack.transpose(1, 0, 2).reshape(k, p * n)


def _scatter_cols(full):
    k, n = full.shape
    return full.reshape(k, N_DEV, n // N_DEV).transpose(1, 0, 2)


def _gather_rows(stack):
    p, r, n = stack.shape
    return stack.reshape(p * r, n)


def _scatter_rows(full):
    r, n = full.shape
    return full.reshape(N_DEV, r // N_DEV, n)


_IN_NAT = Q_LORA + KV_LORA
TRANSPOSED = ("w_in", "w_q_b", "w_up")
ROWS_APART = ("w_in", "w_conv")


def to_kernel_layout(name, w):
    if name == "w_in":
        z = lambda n: jnp.zeros((n, w.shape[1]), w.dtype)
        return jnp.concatenate([w[:_IN_NAT], z(KPE_LO), w[_IN_NAT:_IN_NAT + ROPE], z(LANES - KPE_LO - ROPE), w[_IN_NAT + ROPE:]], axis=0)
    if name == "w_q_b":
        return jnp.pad(w.reshape(HEADS, NOPE + ROPE, -1), ((0, 0), (0, LANES - NOPE - ROPE), (0, 0))).reshape(HEADS * LANES, -1)
    if name == "w_o":
        mla = jnp.pad(w[:HEADS * NOPE].reshape(HEADS, NOPE, -1), ((0, 0), (LANES - NOPE, 0), (0, 0))).reshape(HEADS * LANES, -1)
        return jnp.concatenate([mla, w[HEADS * NOPE:]], axis=0)
    return w


def from_kernel_layout(name, g):
    if name == "w_in":
        return jnp.concatenate([g[:_IN_NAT], g[P_KPE + KPE_LO:P_KPE + KPE_LO + ROPE], g[P_QD:]], axis=0)
    if name == "w_q_b":
        return g.reshape(HEADS, LANES, -1)[:, :NOPE + ROPE, :].reshape(HEADS * (NOPE + ROPE), -1)
    if name == "w_o":
        mla = g[:HEADS * LANES].reshape(HEADS, LANES, -1)[:, LANES - NOPE:, :].reshape(HEADS * NOPE, -1)
        return jnp.concatenate([mla, g[HEADS * LANES:]], axis=0)
    return g


SMALL_COLS = 1024
SMALL_ROWS = 24
SMALL_AT = {"loss": (0, 0, 1), "b_ada": (1, 0, 6 * D_MODEL), "g_mix_norm": (7, 0, D_MODEL), "g_q_lat": (8, 0, Q_LORA),
            "g_kv_lat": (9, 0, KV_LORA), "g_mla_q_nope": (10, 0, NOPE), "g_mla_q_pe": (10, 128, ROPE),
            "g_mla_k_nope": (10, 256, NOPE), "g_mla_k_pe": (10, 384, ROPE), "g_dil_q": (10, 512, DIL_DIM),
            "g_dil_k": (10, 640, DIL_DIM), "g_ffn_norm": (11, 0, D_MODEL), "b_conv": (12, 0, 2 * D_FF)}
SMALL_PARAMS = tuple(n for n in SMALL_AT if n != "loss")


def _pack_small(values):
    by_row = {}
    for name, (row, off, n) in SMALL_AT.items():
        by_row.setdefault(row, []).append((off, values[name].reshape(-1).astype(F32)))
    out = []
    for row in sorted(by_row):
        pieces, at = [], 0
        for off, v in sorted(by_row[row], key=lambda t: t[0]):
            pieces += [jnp.zeros((off - at,), F32), v]
            at = off + v.shape[0]
        flat = jnp.concatenate(pieces)
        nrows = -(-flat.shape[0] // SMALL_COLS)
        out.append(jnp.pad(flat, (0, nrows * SMALL_COLS - flat.shape[0])).reshape(nrows, SMALL_COLS))
    packed = jnp.concatenate(out, axis=0)
    return jnp.pad(packed, ((0, SMALL_ROWS - packed.shape[0]), (0, 0)))


def _adam(w, g, m, v):
    c1 = 1.0 - ADAM_B1 ** ADAM_STEP
    c2 = 1.0 - ADAM_B2 ** ADAM_STEP
    m2 = ADAM_B1 * m + (1.0 - ADAM_B1) * g
    v2 = ADAM_B2 * v + (1.0 - ADAM_B2) * (g * g)
    return -ADAM_LR * ((m2 / c1) / (jnp.sqrt(v2 / c2) + ADAM_EPS) + ADAM_WD * w), m2, v2


def adamw_small(name, stack, params):
    flat = [a for n in SMALL_PARAMS for a in params[n]]

    def body(stack_ref, *refs):
        ins, outs = refs[:len(flat)], refs[len(flat):]
        g_all = stack_ref[0]
        for p in range(1, N_DEV):
            g_all = g_all + stack_ref[p]
        outs[0][...] = g_all[0:1, 0:1]
        for i, pname in enumerate(SMALL_PARAMS):
            row, off, n = SMALL_AT[pname]
            w_ref, m_ref, v_ref = ins[3 * i:3 * i + 3]
            go_ref, d_ref, mo_ref, vo_ref = outs[1 + 4 * i:5 + 4 * i]
            for c0 in range(0, n, SMALL_COLS):
                cn = min(SMALL_COLS, n - c0)
                r = row + c0 // SMALL_COLS
                g = g_all[r:r + 1, off:off + cn]
                cols = (slice(None), slice(c0, c0 + cn))
                d, m2, v2 = _adam(w_ref[cols], g, m_ref[cols], v_ref[cols])
                go_ref[cols], d_ref[cols], mo_ref[cols], vo_ref[cols] = g, d, m2, v2

    whole = lambda a: pl.BlockSpec(a.shape, lambda: (0,) * a.ndim)
    out_shape = [jax.ShapeDtypeStruct((1, 1), F32)] + [jax.ShapeDtypeStruct(a.shape, F32) for n in SMALL_PARAMS for a in params[n][:1] * 4]
    res = pl.pallas_call(body, name=name, in_specs=[whole(stack)] + [whole(a) for a in flat],
                         out_specs=[pl.BlockSpec(s.shape, lambda s=s: (0,) * len(s.shape)) for s in out_shape],
                         out_shape=out_shape, compiler_params=_params())(stack, *flat)
    return res[0], {n: res[1 + 4 * i:5 + 4 * i] for i, n in enumerate(SMALL_PARAMS)}


def _local_step(x, pos, mod, target, w, fetch, emit, halfway=lambda after: None):
    S = SEQ
    sh1, sc1, g1, sh2, sc2, g2 = [mod[:, i * D_MODEL:(i + 1) * D_MODEL] for i in range(6)]
    zeros = lambda n: jnp.zeros((1, n), F32)
    g_q = jnp.concatenate([w["g_mla_q_nope"], w["g_mla_q_pe"], zeros(LANES - NOPE - ROPE)], axis=1)
    g_k = jnp.concatenate([w["g_mla_k_nope"], zeros(LANES - NOPE)], axis=1)
    g_kpe = jnp.concatenate([zeros(KPE_LO), w["g_mla_k_pe"], zeros(LANES - KPE_LO - ROPE)], axis=1)
    g_dq = jnp.concatenate([w["g_dil_q"]] * 2, axis=1)
    g_dk = jnp.concatenate([w["g_dil_k"]] * 2, axis=1)
    b_conv = w["b_conv"]

    def inv_freq(d):
        return jnp.power(ROPE_THETA, -2.0 * jnp.arange(d // 2, dtype=F32) / d)

    n_m, n_d = ROPE // 2, DIL_DIM // 2
    freqs = jnp.concatenate([inv_freq(ROPE), inv_freq(DIL_DIM), jnp.zeros((LANES - n_m - n_d,), F32)]).reshape(1, LANES)

    def tables_fn(rows, params):
        (p,), (f,) = rows, params
        c, s = jnp.cos(p * f), jnp.sin(p * f)
        one, zero = jnp.ones_like(c), jnp.zeros_like(c)
        mla = lambda t, fill: jnp.concatenate([fill[:, :KPE_LO], t[:, :n_m], t[:, :n_m], fill[:, :LANES - KPE_LO - ROPE]], axis=1)
        dil = lambda t: jnp.concatenate([t[:, n_m:n_m + n_d]] * 4, axis=1)
        return [mla(c, one), mla(s, zero), dil(c), dil(s)], []

    cos_m, sin_m, cos_d, sin_d = rowwise("rope_tables", tables_fn, [pos], [freqs], [(LANES, F32)] * 4)
    tables = [cos_m, sin_m, cos_d, sin_d]
    H_M, H_D = ROPE // 2, DIL_DIM // 2

    def ln1_fn(rows, params):
        (xv,), (g, sc, sh) = rows, params
        y, _, _ = _rms(xv, g)
        return [y * (1.0 + sc) + sh], []

    (h,) = rowwise("ln1_fwd", ln1_fn, [x], [w["g_mix_norm"], sc1, sh1], [(D_MODEL, MXU_DTYPE)])
    w_in = fetch("w_in", h)

    def proj_fn(rows, params):
        (hv, cm, sm, cd, sd), (w_t, gq, gkv, gkp, gdq, gdk) = rows, params
        pv = _dot(hv, w_t, "nt")
        kper = _rope(_grms(pv[:, P_KPE:P_QD], gkp, KPE_GROUPS)[0], cm, sm, H_M)
        qd = [_rope(_grms(c, gdq, DIL_GROUPS)[0], cd, sd, H_D) for c in _chunks(pv[:, P_QD:P_KD])]
        kd = [_rope(_grms(c, gdk, DIL_GROUPS)[0], cd, sd, H_D) for c in _chunks(pv[:, P_KD:P_VD])]
        return [pv, _rms(pv[:, P_QLAT:P_KVLAT], gq)[0], _rms(pv[:, P_KVLAT:P_KPE], gkv)[0], kper,
                jnp.concatenate(qd, axis=1), jnp.concatenate(kd, axis=1)], []

    post_params = [w["g_q_lat"], w["g_kv_lat"], g_kpe, g_dq, g_dk]
    proj, qln, kvn, kper, qd_r, kd_r = rowwise(
        "proj_fwd", proj_fn, [h] + tables, [w_in] + post_params,
        [(P_END, F32), (Q_LORA, MXU_DTYPE), (KV_LORA, MXU_DTYPE), (LANES, MXU_DTYPE)] + [(DIL_WIDTH, F32)] * 2, tm=256)
    w_q_b, w_kv_b = fetch("w_q_b", qln), fetch("w_kv_b", kvn)

    def mla_proj_fn(rows, params):
        (qlv, kvlv, kp, cm, sm), (wq_t, wkv, gq, gk) = rows, params
        qv, kvv = _dot(qlv, wq_t, "nt"), _dot(kvlv, wkv)
        value_lanes = _lane(kp.shape) >= NOPE
        qs, ks, vs = [], [], []
        for qc, kc in zip(_chunks(qv), _chunks(kvv), strict=True):
            qs.append(_rope(_grms(qc, gq, Q_GROUPS)[0], cm, sm, H_M))
            ks.append(_grms(kc, gk, K_GROUPS)[0] + kp)
            vs.append(jnp.where(value_lanes, kc, 0.0))
        return [qv, kvv] + [jnp.concatenate(t, axis=1) for t in (qs, ks, vs)], []

    q, kv, q_mla, k_mla, v_mla = rowwise(
        "mla_proj", mla_proj_fn, [qln, kvn, kper, cos_m, sin_m], [w_q_b, w_kv_b, g_q, g_k],
        [(HEADS * LANES, F32)] * 2 + [(HEADS * LANES, MXU_DTYPE)] * 3, tm=256)
    mla_scale = (NOPE + ROPE) ** -0.5
    o_cat, lse_mla = mla_fwd("mla_fwd", q_mla, k_mla, v_mla, mla_scale)
    passed = halfway(lse_mla)

    band = [band_fwd(f"band{dil}_fwd", qd_r, kd_r, proj, dil, dep=passed) for dil in DILATIONS]
    o_cat, lse_mix = combine_fwd("dil_combine", [b[0] for b in band], [b[1] for b in band], o_cat)
    w_o = fetch("w_o", o_cat)

    def mid_fn(rows, params):
        (ov, xv), (w_out, gate1, g, sc, sh) = rows, params
        mx = _dot(ov, w_out)
        x1 = xv + gate1 * mx
        y, _, _ = _rms(x1, g)
        return [mx, x1, y * (1.0 + sc) + sh], []

    mix, x1, h2 = rowwise("mix_fwd", mid_fn, [o_cat, x], [w_o, g1, w["g_ffn_norm"], sc2, sh2],
                          [(D_MODEL, F32), (D_MODEL, F32), (D_MODEL, MXU_DTYPE)], tm=256)
    w_up, w_conv, w_down = fetch("w_up", h2), fetch("w_conv", h2), fetch("w_down", h2)
    dn, up = ffn_fwd("ffn_fwd", h2, w_up, w_conv, b_conv, w_down)

    def final_fn(rows, params):
        (x1v, dnv, tgt), (gate2,) = rows, params
        r = x1v + gate2 * dnv - tgt
        dy = r * (1.0 / D_MODEL)
        loss = jnp.sum(_colsum(r * r), axis=-1, keepdims=True) * (0.5 / D_MODEL)
        return [dy, gate2 * dy], [loss, _colsum(dy * dnv)]

    dy, d_dn, loss, dg2 = rowwise("loss_head", final_fn, [x1, dn, target], [g2], [(D_MODEL, F32), (D_MODEL, MXU_DTYPE)],
                                  [1, D_MODEL])
    dh2, g_up, g_down, g_w_conv, g_b_conv = ffn_bwd("ffn_bwd", h2, up, w_up, w_conv, b_conv, d_dn, w_down)
    emit("w_down", g_down)
    emit("w_conv", g_w_conv)
    sent = emit("w_up", g_up)

    def mid_bwd_fn(rows, params):
        (dh2v, dyv, x1v, mx), (gate1, g, sc) = rows, params
        yn, n, rstd = _rms(x1v, g)
        dx_n, dg = _rms_bwd(dh2v * (1.0 + sc), n, rstd, g)
        dx1 = dyv + dx_n
        return [dx1, gate1 * dx1], [dg, _colsum(dh2v * yn), _colsum(dh2v), _colsum(dx1 * mx)]

    dx1, dmix, dg_ffn, dsc2, dsh2, dg1 = rowwise(
        "mid_bwd", mid_bwd_fn, [dh2, dy, x1, mix], [g1, w["g_ffn_norm"], sc2], [(D_MODEL, F32), (D_MODEL, MXU_DTYPE)],
        [D_MODEL] * 4, dep=sent)

    sent = emit("w_o", matmul("mix_wgrad", o_cat, dmix, "tn", tm=512, out_dtype=MXU_DTYPE))
    do_cat = matmul("mix_dgrad", dmix, w_o, "nt", tm=512, dep=sent)
    dband = None
    for dil, b in zip(DILATIONS, band):
        dband = band_bwd(f"band{dil}_bwd", qd_r, kd_r, proj, b[1], lse_mix, o_cat, do_cat, dil, before=dband)
    dq_mla, dkv_mla, dkper = mla_bwd("mla_bwd", q_mla, k_mla, v_mla, o_cat, do_cat, lse_mla, mla_scale)

    def mla_prep_bwd_fn(rows, params):
        (dqv, dkvv, qv, kvv, cm, sm), (gq, gk) = rows, params
        nope_lanes = _lane(cm.shape) < NOPE
        dqs, dkvs, dgq, dgk = [], [], 0.0, 0.0
        for dqc, dkc, qc, kc in zip(_chunks(dqv), _chunks(dkvv), _chunks(qv), _chunks(kvv), strict=True):
            _, n, rstd = _grms(qc, gq, Q_GROUPS)
            dx, dg = _grms_bwd(_rope_bwd(dqc, cm, sm, H_M), n, rstd, gq, Q_GROUPS)
            dqs.append(dx)
            dgq = dgq + dg
            _, n, rstd = _grms(kc, gk, K_GROUPS)
            dx, dg = _grms_bwd(dkc, n, rstd, gk, K_GROUPS)
            dkvs.append(jnp.where(nope_lanes, dx, dkc))
            dgk = dgk + dg
        return [jnp.concatenate(dqs, axis=1), jnp.concatenate(dkvs, axis=1)], [dgq, dgk]

    dq, dkv, dg_q, dg_k = rowwise("mla_prep_bwd", mla_prep_bwd_fn, [dq_mla, dkv_mla, q, kv, cos_m, sin_m], [g_q, g_k],
                                  [(HEADS * LANES, MXU_DTYPE)] * 2, [LANES, LANES], tm=256)
    emit("w_q_b", matmul("q_wgrad", dq, qln, "tn", out_dtype=MXU_DTYPE))
    emit("w_kv_b", matmul("kv_wgrad", kvn, dkv, "tn", out_dtype=MXU_DTYPE))

    def pre_bwd_fn(rows, params):
        dqv, dkvv, dkp, dqd_, dkd_, dvd_, pv, cm, sm, cd, sd = rows
        wq_t, wkv, gq, gkv, gkp, gdq, gdk = params
        dql, dkvl = _dot(dqv, wq_t), _dot(dkvv, wkv, "nt")
        r_q = _norm_bwd(dql, pv[:, P_QLAT:P_KVLAT], gq)
        r_kv = _norm_bwd(dkvl, pv[:, P_KVLAT:P_KPE], gkv)
        _, n, rstd = _grms(pv[:, P_KPE:P_QD], gkp, KPE_GROUPS)
        r_kp = _grms_bwd(_rope_bwd(dkp, cm, sm, H_M), n, rstd, gkp, KPE_GROUPS)
        outs, dgs = [r_q[0], r_kv[0], r_kp[0]], []
        for dval, lo, g in ((dqd_, P_QD, gdq), (dkd_, P_KD, gdk)):
            dg_sum = 0.0
            for dc, xc in zip(_chunks(dval), _chunks(pv[:, lo:lo + DIL_WIDTH]), strict=True):
                _, n, rstd = _grms(xc, g, DIL_GROUPS)
                dx, dg = _grms_bwd(_rope_bwd(dc, cd, sd, H_D), n, rstd, g, DIL_GROUPS)
                outs.append(dx)
                dg_sum = dg_sum + dg
            dgs.append(dg_sum)
        return [jnp.concatenate(outs + [dvd_], axis=1)], [r_q[1], r_kv[1], r_kp[1]] + dgs

    dproj, dg_q_lat, dg_kv_lat, dg_kpe, dg_dq, dg_dk = rowwise(
        "proj_pre_bwd", pre_bwd_fn,
        [dq, dkv, dkper] + list(dband) + [proj] + tables, [w_q_b, w_kv_b] + post_params,
        [(P_END, MXU_DTYPE)], [Q_LORA, KV_LORA, LANES, LANES, LANES], tm=256)
    sent = emit("w_in", matmul("proj_wgrad", dproj, h, "tn", tn=512, out_dtype=MXU_DTYPE))

    def ln1_bwd_fn(rows, params):
        (dpv, dres, xv), (w_t, g, sc) = rows, params
        dhv = _dot(dpv, w_t)
        yn, n, rstd = _rms(xv, g)
        dx_n, dg = _rms_bwd(dhv * (1.0 + sc), n, rstd, g)
        return [dres + dx_n], [dg, _colsum(dhv * yn), _colsum(dhv)]

    grad_x, dg_mix, dsc1, dsh1 = rowwise("proj_dgrad", ln1_bwd_fn, [dproj, dx1, x], [w_in, w["g_mix_norm"], sc1],
                                         [(D_MODEL, F32)], [D_MODEL] * 3, tm=256, dep=sent)
    dmod = jnp.concatenate([dsh1, dsc1, dg1, dsh2, dsc2, dg2], axis=-1)
    small = {"loss": loss, "b_ada": dmod, "g_mix_norm": dg_mix, "g_q_lat": dg_q_lat, "g_kv_lat": dg_kv_lat,
             "g_mla_q_nope": dg_q[:, :NOPE], "g_mla_q_pe": dg_q[:, NOPE:NOPE + ROPE], "g_mla_k_nope": dg_k[:, :NOPE],
             "g_mla_k_pe": dg_kpe[:, KPE_LO:KPE_LO + ROPE], "g_dil_q": dg_dq[:, :DIL_DIM] + dg_dq[:, DIL_DIM:],
             "g_dil_k": dg_dk[:, :DIL_DIM] + dg_dk[:, DIL_DIM:], "g_ffn_norm": dg_ffn,
             "b_conv": g_b_conv}
    return grad_x, small


COL_SHARDED = ("w_kv_b", "w_conv")
ROW_SHARDED = ("w_o", "w_down") + TRANSPOSED
ADAM_TILE = {"w_ada": 256, "w_up": 176, "w_down": 176}
GATHER_GROUPS = (("w_in",), ("w_q_b", "w_kv_b"), ("w_o",), ("w_up", "w_conv", "w_down"))
FORWARD_STAGES = ((0, 1), (2, 3))
SCATTER_GROUPS = (("w_down", "w_conv", "w_up"), ("w_o",), ("w_q_b", "w_kv_b", "w_in"))
OUT_WEIGHTS = ("w_ada", "b_ada", "g_mix_norm", "w_in", "g_q_lat", "w_q_b", "g_kv_lat", "w_kv_b", "g_mla_q_nope", "g_mla_q_pe",
               "g_mla_k_nope", "g_mla_k_pe", "g_dil_q", "g_dil_k", "w_o", "g_ffn_norm", "w_up", "w_conv", "b_conv", "w_down")


def kernel(x, c, positions, w_ada, b_ada, g_mix_norm, w_in, g_q_lat, w_q_b, g_kv_lat, w_kv_b, g_mla_q_nope, g_mla_q_pe, g_mla_k_nope, g_mla_k_pe, g_dil_q, g_dil_k, w_o, g_ffn_norm, w_up, w_conv, b_conv, w_down, loss_target, m_w_ada, m_b_ada, m_g_mix_norm, m_w_in, m_g_q_lat, m_w_q_b, m_g_kv_lat, m_w_kv_b, m_g_mla_q_nope, m_g_mla_q_pe, m_g_mla_k_nope, m_g_mla_k_pe, m_g_dil_q, m_g_dil_k, m_w_o, m_g_ffn_norm, m_w_up, m_w_conv, m_b_conv, m_w_down, v_w_ada, v_b_ada, v_g_mix_norm, v_w_in, v_g_q_lat, v_w_q_b, v_g_kv_lat, v_w_kv_b, v_g_mla_q_nope, v_g_mla_q_pe, v_g_mla_k_nope, v_g_mla_k_pe, v_g_dil_q, v_g_dil_k, v_w_o, v_g_ffn_norm, v_w_up, v_w_conv, v_b_conv, v_w_down):
    args = dict(locals())
    xi, yi, ci = _place()
    me = 4 * xi + 2 * yi + ci
    def local(prefix, n):
        a = args[prefix + n]
        if n in ROWS_APART:
            return jnp.transpose(a, (2, 0, 1) if n in TRANSPOSED else (1, 0, 2))
        return a[0].T if n in TRANSPOSED else a[0]

    def as_output(n, r):
        if n in ROWS_APART:
            return jnp.transpose(r, (1, 2, 0) if n in TRANSPOSED else (1, 0, 2))
        return (r.T if n in TRANSPOSED else r)[None]

    shard = {n: local("", n) for n in COL_SHARDED + ROW_SHARDED + ("w_ada",)}
    flat = lambda n, a: a.reshape(a.shape[0], a.shape[-1]) if n in ROWS_APART else a
    small_w = {n: args[n] for n in SMALL_PARAMS}

    sc_all, mod_all = ada_modulation("ada_mod", c, shard["w_ada"])

    payload = {n: flat(n, shard[n]) if n == "w_conv" else flat(n, shard[n]).astype(MXU_DTYPE) for n in COL_SHARDED + ROW_SHARDED}
    gather_order = [n for grp in GATHER_GROUPS for n in grp]
    gathered = [exchange_start("gather_start", [payload[n] for n in gather_order], gather=True, after=mod_all, tree=True)]
    after_start = gathered[0][-1]
    full, forwarded = {}, set()

    def forward(stage, after):
        if stage not in forwarded:
            forwarded.add(stage)
            first = sum(len(g) for g in GATHER_GROUPS[:FORWARD_STAGES[stage][0]])
            count = sum(len(GATHER_GROUPS[i]) for i in FORWARD_STAGES[stage])
            gathered[0] = exchange_forward(f"gather_forward{stage}", gathered[0], after, first, count)
        return gathered[0][-1]

    def fetch(name, after):
        if name not in full:
            (i, grp), = [(i, grp) for i, grp in enumerate(GATHER_GROUPS) if name in grp]
            forward([s for s, groups in enumerate(FORWARD_STAGES) if i in groups][0], after)
            srcs, lands = exchange_wait(f"gather{i}_wait", gathered[0], True, after, gather_order.index(grp[0]), len(grp), tree=True)
            for n, src, land in zip(grp, srcs, lands, strict=True):
                stack = lax.dynamic_update_index_in_dim(land, src, me, 0)
                full[n] = to_kernel_layout(n, _gather_cols(stack) if n in COL_SHARDED else _gather_rows(stack))
        return full[name]

    mod_row = lax.dynamic_index_in_dim(mod_all, me, axis=1, keepdims=False).reshape(1, 6 * D_MODEL)
    (mod,) = rowwise("ada_bias", lambda rows, params: ([rows[0] + rows[1]], []), [mod_row, b_ada], [], [(6 * D_MODEL, F32)],
                     dep=after_start)

    own, pending, scatters = {}, {}, {}

    def emit(name, grad):
        grad = from_kernel_layout(name, grad)
        parts = _scatter_cols(grad) if name in COL_SHARDED else _scatter_rows(grad)
        own[name] = lax.dynamic_index_in_dim(parts, me, 0, keepdims=False)
        pending[name] = parts
        for i, grp in enumerate(SCATTER_GROUPS):
            if name == grp[-1]:
                scatters[i] = exchange_start(f"scatter{i}_start", [pending[n] for n in grp], gather=False)
                return scatters[i][-1]
        return None

    pos = positions.reshape(SEQ, 1).astype(F32)
    grad_x, small = _local_step(x[0], pos, mod, loss_target[0], small_w, fetch, emit, halfway=lambda after: forward(1, after))

    res, done = {}, grad_x
    for i, grp in enumerate(SCATTER_GROUPS):
        _, lands = exchange_wait(f"scatter{i}_wait", scatters[i], False, done)
        for n, land in zip(grp, lands, strict=True):
            res[n] = adamw(f"adamw_{n}", shard[n], [own[n], land], local("m_", n), local("v_", n), ADAM_TILE.get(n))
            done = res[n][0]
            res[n] = [as_output(n, r) for r in res[n]]
    (small_all,) = all_gather("gather_small", [_pack_small(small)], after=done)
    loss, small_res = adamw_small("adamw_small", small_all, {n: (args[n], args["m_" + n], args["v_" + n]) for n in SMALL_PARAMS})
    row, _, n_mod = SMALL_AT["b_ada"]
    dmod_all = small_all[:, row:row + n_mod // SMALL_COLS, :].reshape(N_DEV, n_mod)
    dmod_mine = lax.dynamic_slice_in_dim(dmod_all, me * (6 * D_MODEL // N_DEV), 6 * D_MODEL // N_DEV, axis=1)
    g_w_ada = matmul("ada_wgrad", sc_all, dmod_mine, "tn")
    res["w_ada"] = [r[None] for r in adamw("adamw_w_ada", shard["w_ada"], [g_w_ada], m_w_ada[0], v_w_ada[0], ADAM_TILE["w_ada"])]

    def leaf(kind, n):
        return res[n][kind] if n in res else small_res[n][kind]

    return (loss.reshape(()), grad_x[None], *[leaf(k, n) for k in range(4) for n in OUT_WEIGHTS])
```

```python
import jax
import jax.numpy as jnp
from jax import lax
from jax.experimental import pallas as pl
from jax.experimental.pallas import tpu as pltpu

F32 = jnp.float32
MXU_DTYPE = jnp.bfloat16

N_DEV = 8
D_MODEL = 1024
SEQ = 2048
HEADS = 8
NOPE = 64
ROPE = 32
Q_LORA = 512
KV_LORA = 256
DIL_DIM = 64
DIL_WIDTH = HEADS * DIL_DIM
DILATIONS = (1, 4, 16)
SPAN = 128
D_FF = 2816
LANES = 128
ROPE_THETA = 10000.0
EPS = 1e-6
NEG_INF = -1e30
ADAM_LR, ADAM_B1, ADAM_B2, ADAM_EPS, ADAM_WD, ADAM_STEP = 0.001, 0.9, 0.999, 1e-08, 0.01, 10
VMEM_LIMIT = 56 * 1024 * 1024
MESH_ID = pl.DeviceIdType.MESH

P_QLAT, P_KVLAT, P_KPE, P_QD, P_KD, P_VD, P_END = 0, 512, 768, 896, 1408, 1920, 2432
KPE_LO = 64
MIX_IN = HEADS * LANES + DIL_WIDTH


def _params(**kw):
    return pltpu.CompilerParams(vmem_limit_bytes=VMEM_LIMIT, **kw)


def rowwise(name, fn, rows, params, out_rows, out_accs=(), tm=512, dep=None):
    deps = [] if dep is None else [dep]
    rows = [r if isinstance(r, tuple) else (r, r.shape[1], 0) for r in rows]
    R = rows[0][0].shape[0]
    tm = min(tm, R)
    steps = R // tm
    assert steps * tm == R
    in_specs = []
    for a, width, cb in rows:
        ri = a.shape[0]
        per = ri // tm
        assert per * tm == ri
        if ri == R:
            in_specs.append(pl.BlockSpec((tm, width), lambda i, cb=cb: (i, cb)))
        else:
            in_specs.append(pl.BlockSpec((tm, width), lambda i, per=per, cb=cb: (i % per, cb)))
    for p in params:
        in_specs.append(pl.BlockSpec(p.shape, lambda i: (0,) * p.ndim))
    in_specs += [pl.BlockSpec(memory_space=pl.ANY)] * len(deps)
    out_shape = [jax.ShapeDtypeStruct((R, d), dt) for d, dt in out_rows]
    out_specs = [pl.BlockSpec((tm, d), lambda i: (i, 0)) for d, _ in out_rows]
    out_shape += [jax.ShapeDtypeStruct((1, n), F32) for n in out_accs]
    out_specs += [pl.BlockSpec((1, n), lambda i: (0, 0)) for n in out_accs]
    nr, npar, no, na = len(rows), len(params), len(out_rows), len(out_accs)

    def body(*refs):
        rvals = [r[...] for r in refs[:nr]]
        pvals = [r[...] for r in refs[nr:nr + npar]]
        outs, accs = fn(rvals, pvals)
        first_out = nr + npar + len(deps)
        for ref, v in zip(refs[first_out:first_out + no], outs, strict=True):
            ref[...] = v.astype(ref.dtype)
        if na:
            acc_refs = refs[first_out + no:]
            i = pl.program_id(0)

            @pl.when(i == 0)
            def _():
                for ref, v in zip(acc_refs, accs, strict=True):
                    ref[...] = v

            @pl.when(i > 0)
            def _():
                for ref, v in zip(acc_refs, accs, strict=True):
                    ref[...] += v

    res = pl.pallas_call(body, name=name, grid=(steps,), in_specs=in_specs, out_specs=out_specs,
                         out_shape=out_shape, compiler_params=_params())(*[r[0] for r in rows], *params, *deps)
    return list(res)


_DIMS = {"nn": ((1,), (0,)), "nt": ((1,), (1,)), "tn": ((0,), (0,))}


def _dot(a, b, mode="nn"):
    return lax.dot_general(a.astype(MXU_DTYPE), b.astype(MXU_DTYPE), (_DIMS[mode], ((), ())),
                           preferred_element_type=F32)


def matmul(name, a, b, mode, tm=None, tn=None, tk=None, out_dtype=F32, dep=None):
    if mode == "tn":
        K, M = a.shape
    else:
        M, K = a.shape
    N = b.shape[0] if mode == "nt" else b.shape[1]
    tm, tn, tk = tm or M, tn or N, tk or K
    nm, nn, nk = M // tm, N // tn, K // tk
    assert nm * tm == M and nn * tn == N and nk * tk == K
    a_spec = pl.BlockSpec((tk, tm), lambda i, j, k: (k, i)) if mode == "tn" else pl.BlockSpec((tm, tk), lambda i, j, k: (i, k))
    b_spec = pl.BlockSpec((tn, tk), lambda i, j, k: (j, k)) if mode == "nt" else pl.BlockSpec((tk, tn), lambda i, j, k: (k, j))
    deps = [] if dep is None else [dep]

    def body(a_ref, b_ref, *rest):
        o_ref, scratch = rest[len(deps)], rest[len(deps) + 1:]
        p = _dot(a_ref[...], b_ref[...], mode)
        if nk == 1:
            o_ref[...] = p.astype(o_ref.dtype)
        else:
            acc = scratch[0]
            k = pl.program_id(2)

            @pl.when(k == 0)
            def _():
                acc[...] = p

            @pl.when(k > 0)
            def _():
                acc[...] += p

            @pl.when(k == nk - 1)
            def _():
                o_ref[...] = acc[...].astype(o_ref.dtype)

    return pl.pallas_call(
        body, name=name, grid=(nm, nn, nk), in_specs=[a_spec, b_spec] + [pl.BlockSpec(memory_space=pl.ANY)] * len(deps),
        out_specs=pl.BlockSpec((tm, tn), lambda i, j, k: (i, j)),
        out_shape=jax.ShapeDtypeStruct((M, N), out_dtype),
        scratch_shapes=[pltpu.VMEM((tm, tn), F32)] if nk > 1 else [],
        compiler_params=_params())(a, b, *deps)


def _rms(x, g):
    rstd = lax.rsqrt(jnp.mean(x * x, axis=-1, keepdims=True) + EPS)
    n = x * rstd
    return n * g, n, rstd


def _rms_bwd(dy, n, rstd, g):
    dg = jnp.sum(dy * n, axis=0, keepdims=True)
    dn = dy * g
    dx = rstd * (dn - n * jnp.mean(dn * n, axis=-1, keepdims=True))
    return dx, dg


def _norm_bwd(dy, x, g):
    _, n, rstd = _rms(x, g)
    return _rms_bwd(dy, n, rstd, g)


def _colsum(v):
    return jnp.sum(v, axis=0, keepdims=True)


def _silu(x):
    return x * (1.0 / (1.0 + jnp.exp(-x)))


def _lane(shape):
    return lax.broadcasted_iota(jnp.int32, shape, 1)


def _group_mean(v, groups):
    i = lax.broadcasted_iota(jnp.int32, (LANES, LANES), 0)
    j = lax.broadcasted_iota(jnp.int32, (LANES, LANES), 1)
    g = jnp.zeros((LANES, LANES), F32)
    for lo, hi in groups:
        g = jnp.where((i >= lo) & (i < hi) & (j >= lo) & (j < hi), 1.0 / (hi - lo), g)
    head = v.astype(MXU_DTYPE)
    return _dot(head, g) + _dot(v - head.astype(F32), g)


def _in_groups(shape, groups):
    lane = _lane(shape)
    m = jnp.zeros(shape, jnp.bool_)
    for lo, hi in groups:
        m = m | ((lane >= lo) & (lane < hi))
    return m


def _grms(x, g, groups):
    rstd = lax.rsqrt(_group_mean(x * x, groups) + EPS)
    n = jnp.where(_in_groups(x.shape, groups), x * rstd, 0.0)
    return n * g, n, rstd


def _grms_bwd(dy, n, rstd, g, groups):
    dn = dy * g
    return rstd * (dn - n * _group_mean(dn * n, groups)), _colsum(dy * n)


def _rot(x, half, transpose=False):
    first = (_lane(x.shape) % (2 * half)) < half
    up = pltpu.roll(x, LANES - half, axis=1)
    down = pltpu.roll(x, half, axis=1)
    return jnp.where(first, up, -down) if transpose else jnp.where(first, -up, down)


def _rope(x, cos, sin, half):
    return x * cos + _rot(x, half) * sin


def _rope_bwd(dy, cos, sin, half):
    return dy * cos + _rot(dy * sin, half, transpose=True)


def _chunks(x):
    return [x[:, i:i + LANES] for i in range(0, x.shape[1], LANES)]


Q_GROUPS = ((0, NOPE), (NOPE, NOPE + ROPE))
K_GROUPS = ((0, NOPE),)
KPE_GROUPS = ((KPE_LO, KPE_LO + ROPE),)
DIL_GROUPS = ((0, DIL_DIM), (DIL_DIM, 2 * DIL_DIM))


def _col(width, rows=SEQ):
    return pl.BlockSpec((rows, width), lambda h: (0, h))


def _causal_tail(s, tq, fill):
    diag = s[:, s.shape[1] - tq:]
    keep = lax.broadcasted_iota(jnp.int32, diag.shape, 1) <= lax.broadcasted_iota(jnp.int32, diag.shape, 0)
    diag = jnp.where(keep, diag, fill)
    return diag if s.shape[1] == tq else jnp.concatenate([s[:, :s.shape[1] - tq], diag], axis=1)


def mla_fwd(name, q, k, v, scale, tq=256):
    S = q.shape[0]

    def body(q_ref, k_ref, v_ref, o_ref, lse_ref):
        nb = S // tq
        blk = lambda i: slice(i * tq, (i + 1) * tq)

        def scores(i):
            return _dot(q_ref[blk(i), :], k_ref[:(i + 1) * tq, :], "nt")

        def softmax(i, s):
            s = _causal_tail(s * scale, tq, NEG_INF)
            m = jnp.max(s, axis=-1, keepdims=True)
            e = jnp.exp(s - m)
            l = jnp.sum(e, axis=-1, keepdims=True)
            lse_ref[0, blk(i), :] = m + jnp.log(l)
            return (e * (1.0 / l)).astype(MXU_DTYPE)

        def weighted(i, p):
            o_ref[blk(i), :] = _dot(p, v_ref[:(i + 1) * tq, :])

        s, p_prev = scores(0), None
        for i in range(nb):
            s_next = scores(i + 1) if i + 1 < nb else None
            if p_prev is not None:
                weighted(i - 1, p_prev)
            p_prev, s = softmax(i, s), s_next
        weighted(nb - 1, p_prev)

    return pl.pallas_call(
        body, name=name, grid=(HEADS,), in_specs=[_col(LANES)] * 3,
        out_specs=[_col(LANES), pl.BlockSpec((1, S, 1), lambda h: (h, 0, 0))],
        out_shape=[jax.ShapeDtypeStruct((S, MIX_IN), F32), jax.ShapeDtypeStruct((HEADS, S, 1), F32)],
        compiler_params=_params())(q, k, v)


def mla_bwd(name, q, k, v, o, do, lse, scale, tq=256):
    S = q.shape[0]

    def body(q_ref, k_ref, v_ref, o_ref, do_ref, lse_ref, dq_ref, dkv_ref, dkpe_ref, dk_acc, dv_acc):
        dk_acc[...] = jnp.zeros_like(dk_acc)
        dv_acc[...] = jnp.zeros_like(dv_acc)
        for i in range(S // tq):
            kext = (i + 1) * tq
            blk = slice(i * tq, kext)
            qi, kk, vv = q_ref[blk, :], k_ref[:kext, :], v_ref[:kext, :]
            doi = do_ref[blk, :]
            s = _causal_tail(_dot(qi, kk, "nt") * scale, tq, NEG_INF)
            p = jnp.exp(s - lse_ref[0, blk, :])
            dp = _dot(doi, vv, "nt")
            delta = jnp.sum(doi * o_ref[blk, :], axis=-1, keepdims=True)
            ds = p * (dp - delta) * scale
            dq_ref[blk, :] = _dot(ds, kk)
            dk_acc[:kext, :] += _dot(ds, qi, "tn")
            dv_acc[:kext, :] += _dot(p, doi, "tn")
        dk = dk_acc[...]
        lane = _lane(dk.shape)
        dkv_ref[...] = jnp.where(lane < NOPE, dk, 0.0) + dv_acc[...]
        dkpe = jnp.where((lane >= KPE_LO) & (lane < KPE_LO + ROPE), dk, 0.0)
        h = pl.program_id(0)

        @pl.when(h == 0)
        def _():
            dkpe_ref[...] = dkpe

        @pl.when(h > 0)
        def _():
            dkpe_ref[...] += dkpe

    return pl.pallas_call(
        body, name=name, grid=(HEADS,),
        in_specs=[_col(LANES)] * 5 + [pl.BlockSpec((1, S, 1), lambda h: (h, 0, 0))],
        out_specs=[_col(LANES), _col(LANES), pl.BlockSpec((S, LANES), lambda h: (0, 0))],
        out_shape=[jax.ShapeDtypeStruct((S, HEADS * LANES), F32), jax.ShapeDtypeStruct((S, HEADS * LANES), F32),
                   jax.ShapeDtypeStruct((S, LANES), F32)],
        scratch_shapes=[pltpu.VMEM((S, LANES), F32), pltpu.VMEM((S, LANES), F32)],
        compiler_params=_params())(q, k, v, o, do, lse)


BAND_TQ = SPAN


def _band_blocks(L, tq):
    return [(i * tq, (i + 1) * tq, max(0, i * tq - SPAN)) for i in range(L // tq)]


def _class_rows(r, dil, lo, hi):
    return pl.ds(r + dil * lo, hi - lo, stride=dil) if dil > 1 else pl.ds(lo, hi - lo)


def _stack_heads(t, lo):
    zero = jnp.zeros_like(t)
    return jnp.concatenate([jnp.where(lo, t, zero), jnp.where(lo, zero, t)], axis=0)


def _band_mask2(q0, q1, k0):
    n = q1 - q0
    shape = (2 * n, q1 - k0)
    i = lax.broadcasted_iota(jnp.int32, shape, 0)
    dist = (jnp.where(i >= n, i - n, i) + q0) - (lax.broadcasted_iota(jnp.int32, shape, 1) + k0)
    return (dist >= 0) & (dist <= SPAN)


def _pair_col(col0=0):
    return pl.BlockSpec((SEQ, LANES), lambda j: (0, col0 // LANES + j))


def band_fwd(name, q, k, v, dil, dep=None):
    S = q.shape[0]
    L = S // dil
    tq = BAND_TQ
    scale = DIL_DIM ** -0.5
    deps = [] if dep is None else [dep]

    def body(q_ref, k_ref, v_ref, *rest):
        o_ref, lse_ref = rest[len(deps):]
        items = [(r, blk) for r in range(dil) for blk in _band_blocks(L, tq)]
        lo = _lane((tq, LANES)) < DIL_DIM

        def scores(item):
            r, (q0, q1, k0) = item
            qb = q_ref[_class_rows(r, dil, q0, q1), :].astype(MXU_DTYPE)
            return _dot(_stack_heads(qb, lo), k_ref[_class_rows(r, dil, k0, q1), :], "nt")

        def softmax(item, s):
            _, (q0, q1, k0) = item
            s = jnp.where(_band_mask2(q0, q1, k0), s * scale, NEG_INF)
            mx = jnp.max(s, axis=-1, keepdims=True)
            e = jnp.exp(s - mx)
            l = jnp.sum(e, axis=-1, keepdims=True)
            return (e * (1.0 / l)).astype(MXU_DTYPE), mx + jnp.log(l)

        def weighted(item, p, lse):
            r, (q0, q1, k0) = item
            pv = _dot(p, v_ref[_class_rows(r, dil, k0, q1), :])
            o_ref[_class_rows(r, dil, q0, q1), :] = jnp.where(lo, pv[:tq], pv[tq:])
            lse_ref[_class_rows(r, dil, q0, q1), :] = jnp.where(lo, lse[:tq], lse[tq:])

        s, prev = scores(items[0]), None
        for i, item in enumerate(items):
            s_next = scores(items[i + 1]) if i + 1 < len(items) else None
            if prev is not None:
                weighted(items[i - 1], *prev)
            prev, s = softmax(item, s), s_next
        weighted(items[-1], *prev)

    return pl.pallas_call(
        body, name=name, grid=(DIL_WIDTH // LANES,),
        in_specs=[_pair_col()] * 2 + [_pair_col(P_VD)] + [pl.BlockSpec(memory_space=pl.ANY)] * len(deps), out_specs=[_pair_col()] * 2,
        out_shape=[jax.ShapeDtypeStruct((S, DIL_WIDTH), F32)] * 2, compiler_params=_params())(q, k, v, *deps)


def band_bwd(name, q, k, v, lse, lse_mix, o_cat, do_cat, dil, before=None):
    S = q.shape[0]
    L = S // dil
    tq = BAND_TQ
    scale = DIL_DIM ** -0.5
    before = list(before or [])

    def body(q_ref, k_ref, v_ref, lse_ref, mix_ref, o_ref, do_ref, *rest):
        dq_ref, dk_ref, dv_ref = rest[len(before):]
        if before:
            dq0_ref, dk0_ref, dv0_ref = rest[:3]
            dk_ref[...] = dk0_ref[...]
            dv_ref[...] = dv0_ref[...]
        else:
            dk_ref[...] = jnp.zeros_like(dk_ref)
            dv_ref[...] = jnp.zeros_like(dv_ref)
        items = [(r, blk) for r in range(dil) for blk in _band_blocks(L, tq)]
        lo = _lane((tq, LANES)) < DIL_DIM
        per_head = lambda t: jnp.concatenate([t[:, 0:1], t[:, DIL_DIM:DIL_DIM + 1]], axis=0)

        def scores(item):
            r, (q0, q1, k0) = item
            qrows, krows = _class_rows(r, dil, q0, q1), _class_rows(r, dil, k0, q1)
            lse_p, dout = lse_ref[qrows, :], do_ref[qrows, :]
            w2 = per_head(jnp.exp(lse_p - mix_ref[qrows, :]))
            dd = dout * o_ref[qrows, :]
            big_d = jnp.concatenate([jnp.sum(jnp.where(lo, dd, 0.0), axis=-1, keepdims=True),
                                     jnp.sum(jnp.where(lo, 0.0, dd), axis=-1, keepdims=True)], axis=0)
            q2 = _stack_heads(q_ref[qrows, :].astype(MXU_DTYPE), lo)
            dom = (_stack_heads(dout, lo) * w2).astype(MXU_DTYPE)
            return (_dot(q2, k_ref[krows, :], "nt"), _dot(dom, v_ref[krows, :], "nt"), per_head(lse_p), w2 * big_d, q2, dom)

        def softmax_bwd(item, s, dp, lse2, wd2, q2, dom):
            _, (q0, q1, k0) = item
            p = jnp.where(_band_mask2(q0, q1, k0), jnp.exp(s * scale - lse2), 0.0)
            return p.astype(MXU_DTYPE), (p * (dp - wd2) * scale).astype(MXU_DTYPE), q2, dom

        def grads(item, p, ds, q2, dom):
            r, (q0, q1, k0) = item
            qrows, krows = _class_rows(r, dil, q0, q1), _class_rows(r, dil, k0, q1)
            dq2 = _dot(ds, k_ref[krows, :])
            dq = jnp.where(lo, dq2[:tq], dq2[tq:])
            dq_ref[qrows, :] = dq + dq0_ref[qrows, :] if before else dq
            dk_ref[krows, :] += _dot(ds, q2, "tn")
            dv_ref[krows, :] += _dot(p, dom, "tn")

        sc, prev = scores(items[0]), None
        for i, item in enumerate(items):
            sc_next = scores(items[i + 1]) if i + 1 < len(items) else None
            if prev is not None:
                grads(items[i - 1], *prev)
            prev, sc = softmax_bwd(item, *sc), sc_next
        grads(items[-1], *prev)

    cat = _pair_col(HEADS * LANES)
    return pl.pallas_call(
        body, name=name, grid=(DIL_WIDTH // LANES,),
        in_specs=[_pair_col()] * 2 + [_pair_col(P_VD)] + [_pair_col()] * 2 + [cat] * 2 + [_pair_col()] * len(before),
        out_specs=[_pair_col()] * 3, out_shape=[jax.ShapeDtypeStruct((S, DIL_WIDTH), F32)] * 3,
        compiler_params=_params())(q, k, v, lse, lse_mix, o_cat, do_cat, *before)


def combine_fwd(name, outs, lses, o_cat, tm=512):
    S = outs[0].shape[0]

    def body(o1, o2, o3, l1, l2, l3, cat_in, cat_out, mix_ref):
        ls = [l1[...], l2[...], l3[...]]
        m = jnp.maximum(jnp.maximum(ls[0], ls[1]), ls[2])
        e = [jnp.exp(l - m) for l in ls]
        den = e[0] + e[1] + e[2]
        cat_out[...] = (e[0] / den) * o1[...] + (e[1] / den) * o2[...] + (e[2] / den) * o3[...]
        mix_ref[...] = m + jnp.log(den)

    row = pl.BlockSpec((tm, DIL_WIDTH), lambda i: (i, 0))
    return pl.pallas_call(
        body, name=name, grid=(S // tm,), in_specs=[row] * 6 + [pl.BlockSpec(memory_space=pl.ANY)],
        out_specs=[pl.BlockSpec((tm, DIL_WIDTH), lambda i: (i, HEADS * LANES // DIL_WIDTH)), row],
        out_shape=[jax.ShapeDtypeStruct(o_cat.shape, F32), jax.ShapeDtypeStruct((S, DIL_WIDTH), F32)],
        input_output_aliases={6: 0}, compiler_params=_params())(*outs, *lses, o_cat)


def _shift_down(u, n, zero_head):
    out = pltpu.roll(u, n, axis=0)
    return jnp.where(lax.broadcasted_iota(jnp.int32, u.shape, 0) >= n, out, 0.0) if zero_head else out


def _shift_up(u, n, zero_tail):
    rows = u.shape[0]
    out = pltpu.roll(u, rows - n, axis=0)
    return jnp.where(lax.broadcasted_iota(jnp.int32, u.shape, 0) < rows - n, out, 0.0) if zero_tail else out


CONV_ROWS = 512
CONV_HALO = 16


def _conv_chunks(S, tail):
    out = []
    for r0 in range(0, S, CONV_ROWS):
        lo, hi = max(0, r0 - CONV_HALO), min(S, r0 + CONV_ROWS + (CONV_HALO if tail else 0))
        out.append((lo, hi, r0 - lo, CONV_ROWS))
    return out


CONV_TC = 256
CONV_NB = D_FF // CONV_TC


def _half_specs(rows, rows_axis=False):
    if rows_axis:
        return [pl.BlockSpec((rows, D_MODEL), lambda j: (j, 0)), pl.BlockSpec((rows, D_MODEL), lambda j: (j + CONV_NB, 0))]
    return [pl.BlockSpec((rows, CONV_TC), lambda j: (0, j)), pl.BlockSpec((rows, CONV_TC), lambda j: (0, j + CONV_NB))]


def _whole(a):
    return pl.BlockSpec(a.shape, lambda j: (0,) * a.ndim)


def _up_pair(h, ug_ref, uv_ref):
    return jnp.concatenate([_dot(h, ug_ref[...], "nt"), _dot(h, uv_ref[...], "nt")], axis=1)


def _conv_taps(uin, w, b, starts):
    u1, u2 = _shift_down(uin, 1, starts), _shift_down(uin, 2, starts)
    return u1, u2, w[2:3, :] * uin + w[1:2, :] * u1 + w[0:1, :] * u2 + b


def ffn_fwd(name, h, w_up_t, w_conv, b_conv, w_down):
    S = h.shape[0]

    def body(h_ref, ug_ref, uv_ref, wg_ref, wv_ref, bg_ref, bv_ref, wd_ref, dn_ref, up_ref):
        @pl.when(pl.program_id(0) == 0)
        def _():
            dn_ref[...] = jnp.zeros_like(dn_ref)

        w = jnp.concatenate([wg_ref[...], wv_ref[...]], axis=1)
        b = jnp.concatenate([bg_ref[...], bv_ref[...]], axis=1)
        chunks = _conv_chunks(S, tail=False)

        def project(c):
            lo, hi, keep, rows = c
            uin = _up_pair(h_ref[lo:hi, :], ug_ref, uv_ref)
            up_ref[lo + keep:lo + keep + rows, :] = uin[keep:keep + rows]
            return uin

        def gate(c, uin):
            lo, hi, keep, rows = c
            u = _conv_taps(uin, w, b, lo == 0)[2][keep:keep + rows]
            return (_silu(u[:, :CONV_TC]) * u[:, CONV_TC:]).astype(MXU_DTYPE)

        def project_down(c, act):
            dn_ref[c[0] + c[2]:c[0] + c[2] + c[3], :] += _dot(act, wd_ref[...])

        uin, act_prev = project(chunks[0]), None
        for i, c in enumerate(chunks):
            uin_next = project(chunks[i + 1]) if i + 1 < len(chunks) else None
            if act_prev is not None:
                project_down(chunks[i - 1], act_prev)
            act_prev = gate(c, uin)
            uin = uin_next
        project_down(chunks[-1], act_prev)

    return pl.pallas_call(
        body, name=name, grid=(CONV_NB,),
        in_specs=[_whole(h)] + _half_specs(CONV_TC, rows_axis=True) + _half_specs(3) + _half_specs(1)
        + [pl.BlockSpec((CONV_TC, w_down.shape[1]), lambda j: (j, 0))],
        out_specs=[pl.BlockSpec((S, w_down.shape[1]), lambda j: (0, 0)), pl.BlockSpec((S, 2 * CONV_TC), lambda j: (0, j))],
        out_shape=[jax.ShapeDtypeStruct((S, w_down.shape[1]), F32), jax.ShapeDtypeStruct((S, 2 * D_FF), F32)],
        compiler_params=_params())(h, w_up_t, w_up_t, w_conv, w_conv, b_conv, b_conv, w_down)


def ffn_bwd(name, h, up, w_up_t, w_conv, b_conv, d_dn, w_down):
    S, D = h.shape

    def body(h_ref, up_ref, ug_ref, uv_ref, wg_ref, wv_ref, bg_ref, bv_ref, dd_ref, wd_ref,
             dh_ref, gup_ref, gd_ref, dwg_ref, dwv_ref, dbg_ref, dbv_ref):
        @pl.when(pl.program_id(0) == 0)
        def _():
            dh_ref[...] = jnp.zeros_like(dh_ref)

        w = jnp.concatenate([wg_ref[...], wv_ref[...]], axis=1)
        b = jnp.concatenate([bg_ref[...], bv_ref[...]], axis=1)
        w_pair = jnp.concatenate([ug_ref[...], uv_ref[...]], axis=0)
        chunks = _conv_chunks(S, tail=True)

        def project(c):
            return up_ref[c[0]:c[1], :], _dot(dd_ref[c[0]:c[1], :], wd_ref[...], "nt")

        def through_conv(c, uin, da):
            lo, hi, keep, rows = c
            u1, u2, u = _conv_taps(uin, w, b, lo == 0)
            gate, val = u[:, :CONV_TC], u[:, CONV_TC:]
            sig = 1.0 / (1.0 + jnp.exp(-gate))
            du = jnp.concatenate([da * val * (sig * (1.0 + gate * (1.0 - sig))), da * (gate * sig)], axis=1)
            dup = w[2:3, :] * du + w[1:2, :] * _shift_up(du, 1, hi == S) + w[0:1, :] * _shift_up(du, 2, hi == S)
            kept = slice(keep, keep + rows)
            du = du[kept]
            dw = jnp.concatenate([_colsum(du * u2[kept]), _colsum(du * u1[kept]), _colsum(du * uin[kept])], axis=0)
            return dup[kept].astype(MXU_DTYPE), (gate * sig * val)[kept].astype(MXU_DTYPE), dw, _colsum(du)

        def weight_grads(c, dup, act):
            out_rows = slice(c[0] + c[2], c[0] + c[2] + c[3])
            dh_ref[out_rows, :] += _dot(dup, w_pair)
            return _dot(dup, h_ref[out_rows, :], "tn"), _dot(act, dd_ref[out_rows, :], "tn")

        dw, db, g_up, g_dn = 0.0, 0.0, 0.0, 0.0
        proj, done = project(chunks[0]), None
        for i, c in enumerate(chunks):
            proj_next = project(chunks[i + 1]) if i + 1 < len(chunks) else None
            if done is not None:
                gu, gd = weight_grads(chunks[i - 1], *done)
                g_up, g_dn = g_up + gu, g_dn + gd
            dup, act, dw_c, db_c = through_conv(c, *proj)
            dw, db, done, proj = dw + dw_c, db + db_c, (dup, act), proj_next
        gu, gd = weight_grads(chunks[-1], *done)
        g_up, g_dn = g_up + gu, g_dn + gd
        gup_ref[0], gup_ref[1] = g_up[:CONV_TC].astype(gup_ref.dtype), g_up[CONV_TC:].astype(gup_ref.dtype)
        gd_ref[...] = g_dn.astype(gd_ref.dtype)
        dwg_ref[...], dwv_ref[...] = dw[:, :CONV_TC], dw[:, CONV_TC:]
        dbg_ref[...], dbv_ref[...] = db[:, :CONV_TC], db[:, CONV_TC:]

    half = lambda rows: pl.BlockSpec((rows, CONV_TC), lambda j: (0, j))
    rows_blk = pl.BlockSpec((CONV_TC, D), lambda j: (j, 0))
    dh, gup, gd, dwg, dwv, dbg, dbv = pl.pallas_call(
        body, name=name, grid=(CONV_NB,),
        in_specs=[_whole(h), pl.BlockSpec((S, 2 * CONV_TC), lambda j: (0, j))] + _half_specs(CONV_TC, rows_axis=True) + _half_specs(3)
        + _half_specs(1) + [_whole(d_dn), rows_blk],
        out_specs=[pl.BlockSpec((S, D), lambda j: (0, 0)), pl.BlockSpec((2, CONV_TC, D), lambda j: (0, j, 0)), rows_blk,
                   half(3), half(3), half(1), half(1)],
        out_shape=[jax.ShapeDtypeStruct((S, D), F32), jax.ShapeDtypeStruct((2, D_FF, D), MXU_DTYPE),
                   jax.ShapeDtypeStruct((D_FF, D), MXU_DTYPE)]
        + [jax.ShapeDtypeStruct((3, D_FF), F32)] * 2 + [jax.ShapeDtypeStruct((1, D_FF), F32)] * 2,
        compiler_params=_params())(h, up, w_up_t, w_up_t, w_conv, w_conv, b_conv, b_conv, d_dn, w_down)
    return dh, gup.reshape(2 * D_FF, D), gd, jnp.concatenate([dwg, dwv], axis=1), jnp.concatenate([dbg, dbv], axis=1)


def adamw(name, w, parts, m, v, tr=None):
    apart = w.ndim == 3
    R, C = w.shape[0], w.shape[-1]
    tr = tr or R
    assert R % tr == 0
    c1 = 1.0 - ADAM_B1 ** ADAM_STEP
    c2 = 1.0 - ADAM_B2 ** ADAM_STEP
    np_ = len(parts)

    def body(*refs):
        w_ref, m_ref, v_ref = refs[0], refs[1 + np_], refs[2 + np_]
        go_ref, d_ref, mo_ref, vo_ref = refs[3 + np_:]
        terms = []
        for part, ref in zip(parts, refs[1:1 + np_], strict=True):
            terms += [ref[...]] if part.ndim == 2 else [ref[p] for p in range(part.shape[0])]
        g = terms[0].astype(F32)
        for term in terms[1:]:
            g = g + term.astype(F32)
        m2 = ADAM_B1 * m_ref[...] + (1.0 - ADAM_B1) * g
        v2 = ADAM_B2 * v_ref[...] + (1.0 - ADAM_B2) * (g * g)
        go_ref[...] = g
        mo_ref[...] = m2
        vo_ref[...] = v2
        d_ref[...] = -ADAM_LR * ((m2 / c1) / (jnp.sqrt(v2 / c2) + ADAM_EPS) + ADAM_WD * w_ref[...])

    blk = pl.BlockSpec((tr, C), lambda i: (i, 0))
    own = pl.BlockSpec((tr, None, C), lambda i: (i, 0, 0)) if apart else blk
    part_specs = [blk if p.ndim == 2 else pl.BlockSpec((p.shape[0], tr, C), lambda i: (0, i, 0)) for p in parts]
    return pl.pallas_call(
        body, name=name, grid=(R // tr,),
        in_specs=[own] + part_specs + [own, own], out_specs=[own] * 4,
        out_shape=[jax.ShapeDtypeStruct(w.shape, F32)] * 4, compiler_params=_params())(w, *parts, m, v)


def _place():
    return lax.axis_index("x"), lax.axis_index("y"), lax.axis_index("c")


def all_gather(name, arrs, after=None):
    n = len(arrs)
    deps = [] if after is None else [after]

    def body(*refs):
        ins, outs = refs[:n], refs[n + len(deps):2 * n + len(deps)]
        send_sems, recv_sems, local_sems = refs[2 * n + len(deps):]
        x, y, c = _place()
        me, sibling = (x, y, c), (x, y, 1 - c)
        chips = [(1 - x, y), (x, 1 - y), (1 - x, 1 - y)]
        sends = []
        for t in range(n):
            out = outs[t]

            def slot(px, py, pc, out=out):
                return out.at[4 * px + 2 * py + pc]

            def copy(k, block, to, src=None, t=t, slot=slot):
                return pltpu.make_async_remote_copy(
                    src_ref=slot(*block) if src is None else src, dst_ref=slot(*block),
                    send_sem=send_sems.at[7 * t + k], recv_sem=recv_sems.at[7 * t + k],
                    device_id=to, device_id_type=MESH_ID)

            mine = pltpu.make_async_copy(ins[t], slot(*me), local_sems.at[t])
            mine.start()
            first = [copy(0, me, sibling, src=ins[t])]
            first += [copy(1 + j, me, (*chip, c), src=ins[t]) for j, chip in enumerate(chips)]
            for cp in first:
                cp.start()
            sends.append((mine, first, copy))
        for t in range(n):
            mine, first, copy = sends[t]
            passed = [copy(4 + j, (*chip, c), sibling) for j, chip in enumerate(chips)]
            for j, chip in enumerate(chips):
                copy(1 + j, (*chip, c), me).wait_recv()
                passed[j].start()
            copy(0, sibling, me).wait_recv()
            for j, chip in enumerate(chips):
                copy(4 + j, (*chip, 1 - c), me).wait_recv()
            for cp in first + passed:
                cp.wait_send()
            mine.wait()

    any_spec = pl.BlockSpec(memory_space=pl.ANY)
    res = pl.pallas_call(
        body, name=name, in_specs=[any_spec] * (n + len(deps)), out_specs=[any_spec] * n,
        out_shape=[jax.ShapeDtypeStruct((N_DEV,) + a.shape, a.dtype) for a in arrs],
        scratch_shapes=[pltpu.SemaphoreType.DMA((7 * n,)), pltpu.SemaphoreType.DMA((7 * n,)), pltpu.SemaphoreType.DMA((n,))],
        compiler_params=pltpu.CompilerParams(has_side_effects=True))(*arrs, *deps)
    return list(res)


def ada_modulation(name, c, w_ada):
    n_mod = w_ada.shape[1]

    def exchange(src_ref, dst_ref, send_sems, recv_sems):
        x, y, c_ = _place()
        me = 4 * x + 2 * y + c_
        copies = []
        for k in range(1, N_DEV):
            px, py, pc = x ^ (k >> 2), y ^ ((k >> 1) & 1), c_ ^ (k & 1)
            copies.append(pltpu.make_async_remote_copy(
                src_ref=src_ref, dst_ref=dst_ref.at[me], send_sem=send_sems.at[k - 1], recv_sem=recv_sems.at[k - 1],
                device_id=(px, py, pc), device_id_type=MESH_ID))
        for cp in copies:
            cp.start()
        for cp in copies:
            cp.wait_recv()
        for cp in copies:
            cp.wait_send()
        return me

    def body(c_ref, w_ref, sc_ref, mod_ref, c_all, send_c, recv_c, send_m, recv_m):
        me = exchange(c_ref, c_all, send_c, recv_c)
        c_all[me] = c_ref[...]
        sc = _silu(jnp.concatenate([c_all[p] for p in range(N_DEV)], axis=0))
        sc_ref[...] = sc.astype(sc_ref.dtype)
        mod_ref[me] = _dot(sc, w_ref[...])
        exchange(mod_ref.at[me], mod_ref, send_m, recv_m)

    vmem = pl.BlockSpec(memory_space=pltpu.VMEM)
    return pl.pallas_call(
        body, name=name, in_specs=[vmem, vmem], out_specs=[vmem, vmem],
        out_shape=[jax.ShapeDtypeStruct((N_DEV, c.shape[1]), MXU_DTYPE), jax.ShapeDtypeStruct((N_DEV, N_DEV, n_mod), F32)],
        scratch_shapes=[pltpu.VMEM((N_DEV, 1, c.shape[1]), F32)] + [pltpu.SemaphoreType.DMA((N_DEV - 1,))] * 4,
        compiler_params=pltpu.CompilerParams(has_side_effects=True, vmem_limit_bytes=VMEM_LIMIT))(c, w_ada)


HBM_SPEC = pl.BlockSpec(memory_space=pltpu.HBM)
SEM_SPEC = pl.BlockSpec(memory_space=pltpu.SEMAPHORE)
DATAFLOW = pltpu.SideEffectType.DATAFLOW_SIDE_EFFECTING


def _exchange_copies(srcs, lands, send_sems, recv_sems, gather, first=0):
    x, y, c = _place()
    me = 4 * x + 2 * y + c
    out = []
    for t, (src, land) in enumerate(zip(srcs, lands, strict=True)):
        for k in range(1, N_DEV):
            px, py, pc = x ^ (k >> 2), y ^ ((k >> 1) & 1), c ^ (k & 1)
            sem = 7 * (first + t) + k - 1
            out.append((k, pltpu.make_async_remote_copy(
                src_ref=src if gather else src.at[4 * px + 2 * py + pc],
                dst_ref=land.at[me] if gather else land.at[k - 1],
                send_sem=send_sems.at[sem], recv_sem=recv_sems.at[sem],
                device_id=(px, py, pc), device_id_type=MESH_ID)))
    return out


TREE_DIRECT = (1, 2, 4, 6)
TREE_FORWARDED = (3, 5, 7)


def exchange_start(name, arrs, gather, after=None, tree=False):
    n = len(arrs)
    lands = [lax.empty(((N_DEV,) + a.shape) if gather else ((N_DEV - 1,) + a.shape[1:]), a.dtype) for a in arrs]
    deps = [] if after is None else [after]

    def body(*refs):
        srcs, land_refs = refs[:n], refs[n:2 * n]
        send_sems, recv_sems = refs[2 * n + len(deps)], refs[2 * n + len(deps) + 1]
        token, local_sems = refs[-2], refs[-1]
        for k, cp in _exchange_copies(srcs, land_refs, send_sems, recv_sems, gather):
            if not tree or k in TREE_DIRECT:
                cp.start()
        if gather:
            x, y, c = _place()
            own = [pltpu.make_async_copy(src, land.at[4 * x + 2 * y + c], local_sems.at[t])
                   for t, (src, land) in enumerate(zip(srcs, land_refs, strict=True))]
            for cp in own:
                cp.start()
            for cp in own:
                cp.wait()
        token[...] = jnp.zeros_like(token)

    hbm = lambda a: pltpu.HBM(a.shape, a.dtype)
    res = pl.pallas_call(
        body, name=name,
        out_shape=(pltpu.SemaphoreType.DMA((7 * n,)), pltpu.SemaphoreType.DMA((7 * n,)), *[hbm(a) for a in arrs],
                   *[hbm(l) for l in lands], jax.ShapeDtypeStruct((8, 128), F32)),
        in_specs=[HBM_SPEC] * (2 * n) + [pl.BlockSpec(memory_space=pl.ANY)] * len(deps),
        out_specs=(SEM_SPEC, SEM_SPEC, *[HBM_SPEC] * (2 * n), pl.BlockSpec(memory_space=pltpu.VMEM)),
        input_output_aliases={i: 2 + i for i in range(2 * n)}, scratch_shapes=[pltpu.SemaphoreType.DMA((n,))],
        compiler_params=pltpu.CompilerParams(has_side_effects=DATAFLOW),
    )(*[pltpu.with_memory_space_constraint(a, pltpu.HBM) for a in arrs + lands], *deps)
    return res[0], res[1], list(res[2:2 + n]), list(res[2 + n:2 + 2 * n]), res[-1]


def exchange_forward(name, started, after, first=0, count=None):
    send_sems, recv_sems, srcs, lands, _ = started
    count = len(srcs) - first if count is None else count
    mine = lands[first:first + count]
    n = len(mine)

    def copies(land_refs, send_ref, recv_ref):
        x, y, c = _place()
        out = []
        for t, land in enumerate(land_refs):
            for k in (2, 4, 6):
                slot = land.at[4 * (x ^ (k >> 2)) + 2 * (y ^ ((k >> 1) & 1)) + c]
                came, goes = 7 * (first + t) + k - 1, 7 * (first + t) + (k ^ 1) - 1
                out.append((
                    pltpu.make_async_remote_copy(src_ref=slot, dst_ref=slot, send_sem=send_ref.at[came], recv_sem=recv_ref.at[came],
                                                 device_id=(x, y, c), device_id_type=MESH_ID),
                    pltpu.make_async_remote_copy(src_ref=slot, dst_ref=slot, send_sem=send_ref.at[goes], recv_sem=recv_ref.at[goes],
                                                 device_id=(x, y, 1 - c), device_id_type=MESH_ID)))
        return out

    def arrived(*refs):
        for came, _ in copies(refs[:n], refs[n], refs[n + 1]):
            came.wait_recv()

    def pass_on(*refs):
        for _, goes in copies(refs[:n], refs[n], refs[n + 1]):
            goes.start()
        refs[-1][...] = jnp.zeros_like(refs[-1])

    hbm = lambda a: pltpu.HBM(a.shape, a.dtype)
    here = pl.pallas_call(
        arrived, name=name + "_arrived", out_shape=tuple(hbm(a) for a in mine),
        in_specs=[HBM_SPEC] * n + [SEM_SPEC, SEM_SPEC, pl.BlockSpec(memory_space=pl.ANY)],
        out_specs=tuple([HBM_SPEC] * n), input_output_aliases={i: i for i in range(n)},
        compiler_params=pltpu.CompilerParams(has_side_effects=DATAFLOW),
    )(*mine, send_sems, recv_sems, after)
    res = pl.pallas_call(
        pass_on, name=name, out_shape=(*[hbm(a) for a in mine], jax.ShapeDtypeStruct((8, 128), F32)),
        in_specs=[HBM_SPEC] * n + [SEM_SPEC, SEM_SPEC],
        out_specs=(*[HBM_SPEC] * n, pl.BlockSpec(memory_space=pltpu.VMEM)), input_output_aliases={i: i for i in range(n)},
        compiler_params=pltpu.CompilerParams(has_side_effects=DATAFLOW),
    )(*here, send_sems, recv_sems)
    lands = lands[:first] + list(res[:n]) + lands[first + count:]
    return (send_sems, recv_sems, srcs, lands, res[-1])


def exchange_wait(name, started, gather, after, first=0, count=None, tree=False):
    send_sems, recv_sems, srcs, lands, _ = started
    count = len(srcs) - first if count is None else count
    srcs, lands = srcs[first:first + count], lands[first:first + count]
    n = len(srcs)

    def body(*refs):
        src_refs, land_refs = refs[:n], refs[n:2 * n]
        copies = _exchange_copies(src_refs, land_refs, refs[2 * n], refs[2 * n + 1], gather, first)
        for _, cp in copies:
            cp.wait_send()
        for k, cp in copies:
            if not tree or k in (1,) + TREE_FORWARDED:
                cp.wait_recv()

    hbm = lambda a: pltpu.HBM(a.shape, a.dtype)
    res = pl.pallas_call(
        body, name=name, out_shape=tuple(hbm(a) for a in srcs + lands),
        in_specs=[HBM_SPEC] * (2 * n) + [SEM_SPEC, SEM_SPEC, pl.BlockSpec(memory_space=pl.ANY)],
        out_specs=tuple([HBM_SPEC] * (2 * n)), input_output_aliases={i: i for i in range(2 * n)},
        compiler_params=pltpu.CompilerParams(has_side_effects=DATAFLOW),
    )(*srcs, *lands, send_sems, recv_sems, after)
    return list(res[:n]), list(res[n:])


def _gather_cols(stack):
    p, k, n = stack.shape
    return stack.transpose(1, 0, 2).reshape(k, p * n)


def _scatter_cols(full):
    k, n = full.shape
    return full.reshape(k, N_DEV, n // N_DEV).transpose(1, 0, 2)


def _gather_rows(stack):
    p, r, n = stack.shape
    return stack.reshape(p * r, n)


def _scatter_rows(full):
    r, n = full.shape
    return full.reshape(N_DEV, r // N_DEV, n)


_IN_NAT = Q_LORA + KV_LORA
TRANSPOSED = ("w_in", "w_q_b", "w_up")
ROWS_APART = ("w_in", "w_conv")


def to_kernel_layout(name, w):
    if name == "w_in":
        z = lambda n: jnp.zeros((n, w.shape[1]), w.dtype)
        return jnp.concatenate([w[:_IN_NAT], z(KPE_LO), w[_IN_NAT:_IN_NAT + ROPE], z(LANES - KPE_LO - ROPE), w[_IN_NAT + ROPE:]], axis=0)
    if name == "w_q_b":
        return jnp.pad(w.reshape(HEADS, NOPE + ROPE, -1), ((0, 0), (0, LANES - NOPE - ROPE), (0, 0))).reshape(HEADS * LANES, -1)
    if name == "w_o":
        mla = jnp.pad(w[:HEADS * NOPE].reshape(HEADS, NOPE, -1), ((0, 0), (LANES - NOPE, 0), (0, 0))).reshape(HEADS * LANES, -1)
        return jnp.concatenate([mla, w[HEADS * NOPE:]], axis=0)
    return w


def from_kernel_layout(name, g):
    if name == "w_in":
        return jnp.concatenate([g[:_IN_NAT], g[P_KPE + KPE_LO:P_KPE + KPE_LO + ROPE], g[P_QD:]], axis=0)
    if name == "w_q_b":
        return g.reshape(HEADS, LANES, -1)[:, :NOPE + ROPE, :].reshape(HEADS * (NOPE + ROPE), -1)
    if name == "w_o":
        mla = g[:HEADS * LANES].reshape(HEADS, LANES, -1)[:, LANES - NOPE:, :].reshape(HEADS * NOPE, -1)
        return jnp.concatenate([mla, g[HEADS * LANES:]], axis=0)
    return g


SMALL_COLS = 1024
SMALL_ROWS = 24
SMALL_AT = {"loss": (0, 0, 1), "b_ada": (1, 0, 6 * D_MODEL), "g_mix_norm": (7, 0, D_MODEL), "g_q_lat": (8, 0, Q_LORA),
            "g_kv_lat": (9, 0, KV_LORA), "g_mla_q_nope": (10, 0, NOPE), "g_mla_q_pe": (10, 128, ROPE),
            "g_mla_k_nope": (10, 256, NOPE), "g_mla_k_pe": (10, 384, ROPE), "g_dil_q": (10, 512, DIL_DIM),
            "g_dil_k": (10, 640, DIL_DIM), "g_ffn_norm": (11, 0, D_MODEL), "b_conv": (12, 0, 2 * D_FF)}
SMALL_PARAMS = tuple(n for n in SMALL_AT if n != "loss")


def _pack_small(values):
    by_row = {}
    for name, (row, off, n) in SMALL_AT.items():
        by_row.setdefault(row, []).append((off, values[name].reshape(-1).astype(F32)))
    out = []
    for row in sorted(by_row):
        pieces, at = [], 0
        for off, v in sorted(by_row[row], key=lambda t: t[0]):
            pieces += [jnp.zeros((off - at,), F32), v]
            at = off + v.shape[0]
        flat = jnp.concatenate(pieces)
        nrows = -(-flat.shape[0] // SMALL_COLS)
        out.append(jnp.pad(flat, (0, nrows * SMALL_COLS - flat.shape[0])).reshape(nrows, SMALL_COLS))
    packed = jnp.concatenate(out, axis=0)
    return jnp.pad(packed, ((0, SMALL_ROWS - packed.shape[0]), (0, 0)))


def _adam(w, g, m, v):
    c1 = 1.0 - ADAM_B1 ** ADAM_STEP
    c2 = 1.0 - ADAM_B2 ** ADAM_STEP
    m2 = ADAM_B1 * m + (1.0 - ADAM_B1) * g
    v2 = ADAM_B2 * v + (1.0 - ADAM_B2) * (g * g)
    return -ADAM_LR * ((m2 / c1) / (jnp.sqrt(v2 / c2) + ADAM_EPS) + ADAM_WD * w), m2, v2


def adamw_small(name, stack, params):
    flat = [a for n in SMALL_PARAMS for a in params[n]]

    def body(stack_ref, *refs):
        ins, outs = refs[:len(flat)], refs[len(flat):]
        g_all = stack_ref[0]
        for p in range(1, N_DEV):
            g_all = g_all + stack_ref[p]
        outs[0][...] = g_all[0:1, 0:1]
        for i, pname in enumerate(SMALL_PARAMS):
            row, off, n = SMALL_AT[pname]
            w_ref, m_ref, v_ref = ins[3 * i:3 * i + 3]
            go_ref, d_ref, mo_ref, vo_ref = outs[1 + 4 * i:5 + 4 * i]
            for c0 in range(0, n, SMALL_COLS):
                cn = min(SMALL_COLS, n - c0)
                r = row + c0 // SMALL_COLS
                g = g_all[r:r + 1, off:off + cn]
                cols = (slice(None), slice(c0, c0 + cn))
                d, m2, v2 = _adam(w_ref[cols], g, m_ref[cols], v_ref[cols])
                go_ref[cols], d_ref[cols], mo_ref[cols], vo_ref[cols] = g, d, m2, v2

    whole = lambda a: pl.BlockSpec(a.shape, lambda: (0,) * a.ndim)
    out_shape = [jax.ShapeDtypeStruct((1, 1), F32)] + [jax.ShapeDtypeStruct(a.shape, F32) for n in SMALL_PARAMS for a in params[n][:1] * 4]
    res = pl.pallas_call(body, name=name, in_specs=[whole(stack)] + [whole(a) for a in flat],
                         out_specs=[pl.BlockSpec(s.shape, lambda s=s: (0,) * len(s.shape)) for s in out_shape],
                         out_shape=out_shape, compiler_params=_params())(stack, *flat)
    return res[0], {n: res[1 + 4 * i:5 + 4 * i] for i, n in enumerate(SMALL_PARAMS)}


def _local_step(x, pos, mod, target, w, fetch, emit, halfway=lambda after: None):
    S = SEQ
    sh1, sc1, g1, sh2, sc2, g2 = [mod[:, i * D_MODEL:(i + 1) * D_MODEL] for i in range(6)]
    zeros = lambda n: jnp.zeros((1, n), F32)
    g_q = jnp.concatenate([w["g_mla_q_nope"], w["g_mla_q_pe"], zeros(LANES - NOPE - ROPE)], axis=1)
    g_k = jnp.concatenate([w["g_mla_k_nope"], zeros(LANES - NOPE)], axis=1)
    g_kpe = jnp.concatenate([zeros(KPE_LO), w["g_mla_k_pe"], zeros(LANES - KPE_LO - ROPE)], axis=1)
    g_dq = jnp.concatenate([w["g_dil_q"]] * 2, axis=1)
    g_dk = jnp.concatenate([w["g_dil_k"]] * 2, axis=1)
    b_conv = w["b_conv"]

    def inv_freq(d):
        return jnp.power(ROPE_THETA, -2.0 * jnp.arange(d // 2, dtype=F32) / d)

    n_m, n_d = ROPE // 2, DIL_DIM // 2
    freqs = jnp.concatenate([inv_freq(ROPE), inv_freq(DIL_DIM), jnp.zeros((LANES - n_m - n_d,), F32)]).reshape(1, LANES)

    def tables_fn(rows, params):
        (p,), (f,) = rows, params
        c, s = jnp.cos(p * f), jnp.sin(p * f)
        one, zero = jnp.ones_like(c), jnp.zeros_like(c)
        mla = lambda t, fill: jnp.concatenate([fill[:, :KPE_LO], t[:, :n_m], t[:, :n_m], fill[:, :LANES - KPE_LO - ROPE]], axis=1)
        dil = lambda t: jnp.concatenate([t[:, n_m:n_m + n_d]] * 4, axis=1)
        return [mla(c, one), mla(s, zero), dil(c), dil(s)], []

    cos_m, sin_m, cos_d, sin_d = rowwise("rope_tables", tables_fn, [pos], [freqs], [(LANES, F32)] * 4)
    tables = [cos_m, sin_m, cos_d, sin_d]
    H_M, H_D = ROPE // 2, DIL_DIM // 2

    def ln1_fn(rows, params):
        (xv,), (g, sc, sh) = rows, params
        y, _, _ = _rms(xv, g)
        return [y * (1.0 + sc) + sh], []

    (h,) = rowwise("ln1_fwd", ln1_fn, [x], [w["g_mix_norm"], sc1, sh1], [(D_MODEL, MXU_DTYPE)])
    w_in = fetch("w_in", h)

    def proj_fn(rows, params):
        (hv, cm, sm, cd, sd), (w_t, gq, gkv, gkp, gdq, gdk) = rows, params
        pv = _dot(hv, w_t, "nt")
        kper = _rope(_grms(pv[:, P_KPE:P_QD], gkp, KPE_GROUPS)[0], cm, sm, H_M)
        qd = [_rope(_grms(c, gdq, DIL_GROUPS)[0], cd, sd, H_D) for c in _chunks(pv[:, P_QD:P_KD])]
        kd = [_rope(_grms(c, gdk, DIL_GROUPS)[0], cd, sd, H_D) for c in _chunks(pv[:, P_KD:P_VD])]
        return [pv, _rms(pv[:, P_QLAT:P_KVLAT], gq)[0], _rms(pv[:, P_KVLAT:P_KPE], gkv)[0], kper,
                jnp.concatenate(qd, axis=1), jnp.concatenate(kd, axis=1)], []

    post_params = [w["g_q_lat"], w["g_kv_lat"], g_kpe, g_dq, g_dk]
    proj, qln, kvn, kper, qd_r, kd_r = rowwise(
        "proj_fwd", proj_fn, [h] + tables, [w_in] + post_params,
        [(P_END, F32), (Q_LORA, MXU_DTYPE), (KV_LORA, MXU_DTYPE), (LANES, MXU_DTYPE)] + [(DIL_WIDTH, F32)] * 2, tm=256)
    w_q_b, w_kv_b = fetch("w_q_b", qln), fetch("w_kv_b", kvn)

    def mla_proj_fn(rows, params):
        (qlv, kvlv, kp, cm, sm), (wq_t, wkv, gq, gk) = rows, params
        qv, kvv = _dot(qlv, wq_t, "nt"), _dot(kvlv, wkv)
        value_lanes = _lane(kp.shape) >= NOPE
        qs, ks, vs = [], [], []
        for qc, kc in zip(_chunks(qv), _chunks(kvv), strict=True):
            qs.append(_rope(_grms(qc, gq, Q_GROUPS)[0], cm, sm, H_M))
            ks.append(_grms(kc, gk, K_GROUPS)[0] + kp)
            vs.append(jnp.where(value_lanes, kc, 0.0))
        return [qv, kvv] + [jnp.concatenate(t, axis=1) for t in (qs, ks, vs)], []

    q, kv, q_mla, k_mla, v_mla = rowwise(
        "mla_proj", mla_proj_fn, [qln, kvn, kper, cos_m, sin_m], [w_q_b, w_kv_b, g_q, g_k],
        [(HEADS * LANES, F32)] * 2 + [(HEADS * LANES, MXU_DTYPE)] * 3, tm=256)
    mla_scale = (NOPE + ROPE) ** -0.5
    o_cat, lse_mla = mla_fwd("mla_fwd", q_mla, k_mla, v_mla, mla_scale)
    passed = halfway(lse_mla)

    band = [band_fwd(f"band{dil}_fwd", qd_r, kd_r, proj, dil, dep=passed) for dil in DILATIONS]
    o_cat, lse_mix = combine_fwd("dil_combine", [b[0] for b in band], [b[1] for b in band], o_cat)
    w_o = fetch("w_o", o_cat)

    def mid_fn(rows, params):
        (ov, xv), (w_out, gate1, g, sc, sh) = rows, params
        mx = _dot(ov, w_out)
        x1 = xv + gate1 * mx
        y, _, _ = _rms(x1, g)
        return [mx, x1, y * (1.0 + sc) + sh], []

    mix, x1, h2 = rowwise("mix_fwd", mid_fn, [o_cat, x], [w_o, g1, w["g_ffn_norm"], sc2, sh2],
                          [(D_MODEL, F32), (D_MODEL, F32), (D_MODEL, MXU_DTYPE)], tm=256)
    w_up, w_conv, w_down = fetch("w_up", h2), fetch("w_conv", h2), fetch("w_down", h2)
    dn, up = ffn_fwd("ffn_fwd", h2, w_up, w_conv, b_conv, w_down)

    def final_fn(rows, params):
        (x1v, dnv, tgt), (gate2,) = rows, params
        r = x1v + gate2 * dnv - tgt
        dy = r * (1.0 / D_MODEL)
        loss = jnp.sum(_colsum(r * r), axis=-1, keepdims=True) * (0.5 / D_MODEL)
        return [dy, gate2 * dy], [loss, _colsum(dy * dnv)]

    dy, d_dn, loss, dg2 = rowwise("loss_head", final_fn, [x1, dn, target], [g2], [(D_MODEL, F32), (D_MODEL, MXU_DTYPE)],
                                  [1, D_MODEL])
    dh2, g_up, g_down, g_w_conv, g_b_conv = ffn_bwd("ffn_bwd", h2, up, w_up, w_conv, b_conv, d_dn, w_down)
    emit("w_down", g_down)
    emit("w_conv", g_w_conv)
    sent = emit("w_up", g_up)

    def mid_bwd_fn(rows, params):
        (dh2v, dyv, x1v, mx), (gate1, g, sc) = rows, params
        yn, n, rstd = _rms(x1v, g)
        dx_n, dg = _rms_bwd(dh2v * (1.0 + sc), n, rstd, g)
        dx1 = dyv + dx_n
        return [dx1, gate1 * dx1], [dg, _colsum(dh2v * yn), _colsum(dh2v), _colsum(dx1 * mx)]

    dx1, dmix, dg_ffn, dsc2, dsh2, dg1 = rowwise(
        "mid_bwd", mid_bwd_fn, [dh2, dy, x1, mix], [g1, w["g_ffn_norm"], sc2], [(D_MODEL, F32), (D_MODEL, MXU_DTYPE)],
        [D_MODEL] * 4, dep=sent)

    sent = emit("w_o", matmul("mix_wgrad", o_cat, dmix, "tn", tm=512, out_dtype=MXU_DTYPE))
    do_cat = matmul("mix_dgrad", dmix, w_o, "nt", tm=512, dep=sent)
    dband = None
    for dil, b in zip(DILATIONS, band):
        dband = band_bwd(f"band{dil}_bwd", qd_r, kd_r, proj, b[1], lse_mix, o_cat, do_cat, dil, before=dband)
    dq_mla, dkv_mla, dkper = mla_bwd("mla_bwd", q_mla, k_mla, v_mla, o_cat, do_cat, lse_mla, mla_scale)

    def mla_prep_bwd_fn(rows, params):
        (dqv, dkvv, qv, kvv, cm, sm), (gq, gk) = rows, params
        nope_lanes = _lane(cm.shape) < NOPE
        dqs, dkvs, dgq, dgk = [], [], 0.0, 0.0
        for dqc, dkc, qc, kc in zip(_chunks(dqv), _chunks(dkvv), _chunks(qv), _chunks(kvv), strict=True):
            _, n, rstd = _grms(qc, gq, Q_GROUPS)
            dx, dg = _grms_bwd(_rope_bwd(dqc, cm, sm, H_M), n, rstd, gq, Q_GROUPS)
            dqs.append(dx)
            dgq = dgq + dg
            _, n, rstd = _grms(kc, gk, K_GROUPS)
            dx, dg = _grms_bwd(dkc, n, rstd, gk, K_GROUPS)
            dkvs.append(jnp.where(nope_lanes, dx, dkc))
            dgk = dgk + dg
        return [jnp.concatenate(dqs, axis=1), jnp.concatenate(dkvs, axis=1)], [dgq, dgk]

    dq, dkv, dg_q, dg_k = rowwise("mla_prep_bwd", mla_prep_bwd_fn, [dq_mla, dkv_mla, q, kv, cos_m, sin_m], [g_q, g_k],
                                  [(HEADS * LANES, MXU_DTYPE)] * 2, [LANES, LANES], tm=256)
    emit("w_q_b", matmul("q_wgrad", dq, qln, "tn", out_dtype=MXU_DTYPE))
    emit("w_kv_b", matmul("kv_wgrad", kvn, dkv, "tn", out_dtype=MXU_DTYPE))

    def pre_bwd_fn(rows, params):
        dqv, dkvv, dkp, dqd_, dkd_, dvd_, pv, cm, sm, cd, sd = rows
        wq_t, wkv, gq, gkv, gkp, gdq, gdk = params
        dql, dkvl = _dot(dqv, wq_t), _dot(dkvv, wkv, "nt")
        r_q = _norm_bwd(dql, pv[:, P_QLAT:P_KVLAT], gq)
        r_kv = _norm_bwd(dkvl, pv[:, P_KVLAT:P_KPE], gkv)
        _, n, rstd = _grms(pv[:, P_KPE:P_QD], gkp, KPE_GROUPS)
        r_kp = _grms_bwd(_rope_bwd(dkp, cm, sm, H_M), n, rstd, gkp, KPE_GROUPS)
        outs, dgs = [r_q[0], r_kv[0], r_kp[0]], []
        for dval, lo, g in ((dqd_, P_QD, gdq), (dkd_, P_KD, gdk)):
            dg_sum = 0.0
            for dc, xc in zip(_chunks(dval), _chunks(pv[:, lo:lo + DIL_WIDTH]), strict=True):
                _, n, rstd = _grms(xc, g, DIL_GROUPS)
                dx, dg = _grms_bwd(_rope_bwd(dc, cd, sd, H_D), n, rstd, g, DIL_GROUPS)
                outs.append(dx)
                dg_sum = dg_sum + dg
            dgs.append(dg_sum)
        return [jnp.concatenate(outs + [dvd_], axis=1)], [r_q[1], r_kv[1], r_kp[1]] + dgs

    dproj, dg_q_lat, dg_kv_lat, dg_kpe, dg_dq, dg_dk = rowwise(
        "proj_pre_bwd", pre_bwd_fn,
        [dq, dkv, dkper] + list(dband) + [proj] + tables, [w_q_b, w_kv_b] + post_params,
        [(P_END, MXU_DTYPE)], [Q_LORA, KV_LORA, LANES, LANES, LANES], tm=256)
    sent = emit("w_in", matmul("proj_wgrad", dproj, h, "tn", tn=512, out_dtype=MXU_DTYPE))

    def ln1_bwd_fn(rows, params):
        (dpv, dres, xv), (w_t, g, sc) = rows, params
        dhv = _dot(dpv, w_t)
        yn, n, rstd = _rms(xv, g)
        dx_n, dg = _rms_bwd(dhv * (1.0 + sc), n, rstd, g)
        return [dres + dx_n], [dg, _colsum(dhv * yn), _colsum(dhv)]

    grad_x, dg_mix, dsc1, dsh1 = rowwise("proj_dgrad", ln1_bwd_fn, [dproj, dx1, x], [w_in, w["g_mix_norm"], sc1],
                                         [(D_MODEL, F32)], [D_MODEL] * 3, tm=256, dep=sent)
    dmod = jnp.concatenate([dsh1, dsc1, dg1, dsh2, dsc2, dg2], axis=-1)
    small = {"loss": loss, "b_ada": dmod, "g_mix_norm": dg_mix, "g_q_lat": dg_q_lat, "g_kv_lat": dg_kv_lat,
             "g_mla_q_nope": dg_q[:, :NOPE], "g_mla_q_pe": dg_q[:, NOPE:NOPE + ROPE], "g_mla_k_nope": dg_k[:, :NOPE],
             "g_mla_k_pe": dg_kpe[:, KPE_LO:KPE_LO + ROPE], "g_dil_q": dg_dq[:, :DIL_DIM] + dg_dq[:, DIL_DIM:],
             "g_dil_k": dg_dk[:, :DIL_DIM] + dg_dk[:, DIL_DIM:], "g_ffn_norm": dg_ffn,
             "b_conv": g_b_conv}
    return grad_x, small


COL_SHARDED = ("w_kv_b", "w_conv")
ROW_SHARDED = ("w_o", "w_down") + TRANSPOSED
ADAM_TILE = {"w_ada": 256, "w_up": 176, "w_down": 176}
GATHER_GROUPS = (("w_in",), ("w_q_b", "w_kv_b"), ("w_o",), ("w_up", "w_conv", "w_down"))
FORWARD_STAGES = ((0, 1), (2, 3))
SCATTER_GROUPS = (("w_down", "w_conv", "w_up"), ("w_o",), ("w_q_b", "w_kv_b", "w_in"))
OUT_WEIGHTS = ("w_ada", "b_ada", "g_mix_norm", "w_in", "g_q_lat", "w_q_b", "g_kv_lat", "w_kv_b", "g_mla_q_nope", "g_mla_q_pe",
               "g_mla_k_nope", "g_mla_k_pe", "g_dil_q", "g_dil_k", "w_o", "g_ffn_norm", "w_up", "w_conv", "b_conv", "w_down")


def kernel(x, c, positions, w_ada, b_ada, g_mix_norm, w_in, g_q_lat, w_q_b, g_kv_lat, w_kv_b, g_mla_q_nope, g_mla_q_pe, g_mla_k_nope, g_mla_k_pe, g_dil_q, g_dil_k, w_o, g_ffn_norm, w_up, w_conv, b_conv, w_down, loss_target, m_w_ada, m_b_ada, m_g_mix_norm, m_w_in, m_g_q_lat, m_w_q_b, m_g_kv_lat, m_w_kv_b, m_g_mla_q_nope, m_g_mla_q_pe, m_g_mla_k_nope, m_g_mla_k_pe, m_g_dil_q, m_g_dil_k, m_w_o, m_g_ffn_norm, m_w_up, m_w_conv, m_b_conv, m_w_down, v_w_ada, v_b_ada, v_g_mix_norm, v_w_in, v_g_q_lat, v_w_q_b, v_g_kv_lat, v_w_kv_b, v_g_mla_q_nope, v_g_mla_q_pe, v_g_mla_k_nope, v_g_mla_k_pe, v_g_dil_q, v_g_dil_k, v_w_o, v_g_ffn_norm, v_w_up, v_w_conv, v_b_conv, v_w_down):
    args = dict(locals())
    xi, yi, ci = _place()
    me = 4 * xi + 2 * yi + ci
    def local(prefix, n):
        a = args[prefix + n]
        if n in ROWS_APART:
            return jnp.transpose(a, (2, 0, 1) if n in TRANSPOSED else (1, 0, 2))
        return a[0].T if n in TRANSPOSED else a[0]

    def as_output(n, r):
        if n in ROWS_APART:
            return jnp.transpose(r, (1, 2, 0) if n in TRANSPOSED else (1, 0, 2))
        return (r.T if n in TRANSPOSED else r)[None]

    shard = {n: local("", n) for n in COL_SHARDED + ROW_SHARDED + ("w_ada",)}
    flat = lambda n, a: a.reshape(a.shape[0], a.shape[-1]) if n in ROWS_APART else a
    small_w = {n: args[n] for n in SMALL_PARAMS}

    sc_all, mod_all = ada_modulation("ada_mod", c, shard["w_ada"])

    payload = {n: flat(n, shard[n]) if n == "w_conv" else flat(n, shard[n]).astype(MXU_DTYPE) for n in COL_SHARDED + ROW_SHARDED}
    gather_order = [n for grp in GATHER_GROUPS for n in grp]
    gathered = [exchange_start("gather_start", [payload[n] for n in gather_order], gather=True, after=mod_all, tree=True)]
    after_start = gathered[0][-1]
    full, forwarded = {}, set()

    def forward(stage, after):
        if stage not in forwarded:
            forwarded.add(stage)
            first = sum(len(g) for g in GATHER_GROUPS[:FORWARD_STAGES[stage][0]])
            count = sum(len(GATHER_GROUPS[i]) for i in FORWARD_STAGES[stage])
            gathered[0] = exchange_forward(f"gather_forward{stage}", gathered[0], after, first, count)
        return gathered[0][-1]

    def fetch(name, after):
        if name not in full:
            (i, grp), = [(i, grp) for i, grp in enumerate(GATHER_GROUPS) if name in grp]
            forward([s for s, groups in enumerate(FORWARD_STAGES) if i in groups][0], after)
            srcs, lands = exchange_wait(f"gather{i}_wait", gathered[0], True, after, gather_order.index(grp[0]), len(grp), tree=True)
            for n, land in zip(grp, lands, strict=True):
                full[n] = to_kernel_layout(n, _gather_cols(land) if n in COL_SHARDED else _gather_rows(land))
        return full[name]

    mod_row = lax.dynamic_index_in_dim(mod_all, me, axis=1, keepdims=False).reshape(1, 6 * D_MODEL)
    (mod,) = rowwise("ada_bias", lambda rows, params: ([rows[0] + rows[1]], []), [mod_row, b_ada], [], [(6 * D_MODEL, F32)],
                     dep=after_start)

    own, pending, scatters = {}, {}, {}

    def emit(name, grad):
        grad = from_kernel_layout(name, grad)
        parts = _scatter_cols(grad) if name in COL_SHARDED else _scatter_rows(grad)
        own[name] = lax.dynamic_index_in_dim(parts, me, 0, keepdims=False)
        pending[name] = parts
        for i, grp in enumerate(SCATTER_GROUPS):
            if name == grp[-1]:
                scatters[i] = exchange_start(f"scatter{i}_start", [pending[n] for n in grp], gather=False)
                return scatters[i][-1]
        return None

    pos = positions.reshape(SEQ, 1).astype(F32)
    grad_x, small = _local_step(x[0], pos, mod, loss_target[0], small_w, fetch, emit, halfway=lambda after: forward(1, after))

    res, done = {}, grad_x
    for i, grp in enumerate(SCATTER_GROUPS):
        _, lands = exchange_wait(f"scatter{i}_wait", scatters[i], False, done)
        for n, land in zip(grp, lands, strict=True):
            res[n] = adamw(f"adamw_{n}", shard[n], [own[n], land], local("m_", n), local("v_", n), ADAM_TILE.get(n))
            done = res[n][0]
            res[n] = [as_output(n, r) for r in res[n]]
    (small_all,) = all_gather("gather_small", [_pack_small(small)], after=done)
    loss, small_res = adamw_small("adamw_small", small_all, {n: (args[n], args["m_" + n], args["v_" + n]) for n in SMALL_PARAMS})
    row, _, n_mod = SMALL_AT["b_ada"]
    dmod_all = small_all[:, row:row + n_mod // SMALL_COLS, :].reshape(N_DEV, n_mod)
    dmod_mine = lax.dynamic_slice_in_dim(dmod_all, me * (6 * D_MODEL // N_DEV), 6 * D_MODEL // N_DEV, axis=1)
    g_w_ada = matmul("ada_wgrad", sc_all, dmod_mine, "tn")
    res["w_ada"] = [r[None] for r in adamw("adamw_w_ada", shard["w_ada"], [g_w_ada], m_w_ada[0], v_w_ada[0], ADAM_TILE["w_ada"])]

    def leaf(kind, n):
        return res[n][kind] if n in res else small_res[n][kind]

    return (loss.reshape(()), grad_x[None], *[leaf(k, n) for k in range(4) for n in OUT_WEIGHTS])
```

```python
import jax
import jax.numpy as jnp
from jax import lax
from jax.experimental import pallas as pl
from jax.experimental.pallas import tpu as pltpu

F32 = jnp.float32
MXU_DTYPE = jnp.bfloat16

N_DEV = 8
D_MODEL = 1024
SEQ = 2048
HEADS = 8
NOPE = 64
ROPE = 32
Q_LORA = 512
KV_LORA = 256
DIL_DIM = 64
DIL_WIDTH = HEADS * DIL_DIM
DILATIONS = (1, 4, 16)
SPAN = 128
D_FF = 2816
LANES = 128
ROPE_THETA = 10000.0
EPS = 1e-6
NEG_INF = -1e30
ADAM_LR, ADAM_B1, ADAM_B2, ADAM_EPS, ADAM_WD, ADAM_STEP = 0.001, 0.9, 0.999, 1e-08, 0.01, 10
VMEM_LIMIT = 56 * 1024 * 1024
MESH_ID = pl.DeviceIdType.MESH

P_QLAT, P_KVLAT, P_KPE, P_QD, P_KD, P_VD, P_END = 0, 512, 768, 896, 1408, 1920, 2432
KPE_LO = 64
MIX_IN = HEADS * LANES + DIL_WIDTH


def _params(**kw):
    return pltpu.CompilerParams(vmem_limit_bytes=VMEM_LIMIT, **kw)


def rowwise(name, fn, rows, params, out_rows, out_accs=(), tm=512, dep=None):
    deps = [] if dep is None else [dep]
    rows = [r if isinstance(r, tuple) else (r, r.shape[1], 0) for r in rows]
    R = rows[0][0].shape[0]
    tm = min(tm, R)
    steps = R // tm
    assert steps * tm == R
    in_specs = []
    for a, width, cb in rows:
        ri = a.shape[0]
        per = ri // tm
        assert per * tm == ri
        if ri == R:
            in_specs.append(pl.BlockSpec((tm, width), lambda i, cb=cb: (i, cb)))
        else:
            in_specs.append(pl.BlockSpec((tm, width), lambda i, per=per, cb=cb: (i % per, cb)))
    for p in params:
        in_specs.append(pl.BlockSpec(p.shape, lambda i: (0,) * p.ndim))
    in_specs += [pl.BlockSpec(memory_space=pl.ANY)] * len(deps)
    out_shape = [jax.ShapeDtypeStruct((R, d), dt) for d, dt in out_rows]
    out_specs = [pl.BlockSpec((tm, d), lambda i: (i, 0)) for d, _ in out_rows]
    out_shape += [jax.ShapeDtypeStruct((1, n), F32) for n in out_accs]
    out_specs += [pl.BlockSpec((1, n), lambda i: (0, 0)) for n in out_accs]
    nr, npar, no, na = len(rows), len(params), len(out_rows), len(out_accs)

    def body(*refs):
        rvals = [r[...] for r in refs[:nr]]
        pvals = [r[...] for r in refs[nr:nr + npar]]
        outs, accs = fn(rvals, pvals)
        first_out = nr + npar + len(deps)
        for ref, v in zip(refs[first_out:first_out + no], outs, strict=True):
            ref[...] = v.astype(ref.dtype)
        if na:
            acc_refs = refs[first_out + no:]
            i = pl.program_id(0)

            @pl.when(i == 0)
            def _():
                for ref, v in zip(acc_refs, accs, strict=True):
                    ref[...] = v

            @pl.when(i > 0)
            def _():
                for ref, v in zip(acc_refs, accs, strict=True):
                    ref[...] += v

    res = pl.pallas_call(body, name=name, grid=(steps,), in_specs=in_specs, out_specs=out_specs,
                         out_shape=out_shape, compiler_params=_params())(*[r[0] for r in rows], *params, *deps)
    return list(res)


_DIMS = {"nn": ((1,), (0,)), "nt": ((1,), (1,)), "tn": ((0,), (0,))}


def _dot(a, b, mode="nn"):
    return lax.dot_general(a.astype(MXU_DTYPE), b.astype(MXU_DTYPE), (_DIMS[mode], ((), ())),
                           preferred_element_type=F32)


def matmul(name, a, b, mode, tm=None, tn=None, tk=None, out_dtype=F32, dep=None):
    if mode == "tn":
        K, M = a.shape
    else:
        M, K = a.shape
    N = b.shape[0] if mode == "nt" else b.shape[1]
    tm, tn, tk = tm or M, tn or N, tk or K
    nm, nn, nk = M // tm, N // tn, K // tk
    assert nm * tm == M and nn * tn == N and nk * tk == K
    a_spec = pl.BlockSpec((tk, tm), lambda i, j, k: (k, i)) if mode == "tn" else pl.BlockSpec((tm, tk), lambda i, j, k: (i, k))
    b_spec = pl.BlockSpec((tn, tk), lambda i, j, k: (j, k)) if mode == "nt" else pl.BlockSpec((tk, tn), lambda i, j, k: (k, j))
    deps = [] if dep is None else [dep]

    def body(a_ref, b_ref, *rest):
        o_ref, scratch = rest[len(deps)], rest[len(deps) + 1:]
        p = _dot(a_ref[...], b_ref[...], mode)
        if nk == 1:
            o_ref[...] = p.astype(o_ref.dtype)
        else:
            acc = scratch[0]
            k = pl.program_id(2)

            @pl.when(k == 0)
            def _():
                acc[...] = p

            @pl.when(k > 0)
            def _():
                acc[...] += p

            @pl.when(k == nk - 1)
            def _():
                o_ref[...] = acc[...].astype(o_ref.dtype)

    return pl.pallas_call(
        body, name=name, grid=(nm, nn, nk), in_specs=[a_spec, b_spec] + [pl.BlockSpec(memory_space=pl.ANY)] * len(deps),
        out_specs=pl.BlockSpec((tm, tn), lambda i, j, k: (i, j)),
        out_shape=jax.ShapeDtypeStruct((M, N), out_dtype),
        scratch_shapes=[pltpu.VMEM((tm, tn), F32)] if nk > 1 else [],
        compiler_params=_params())(a, b, *deps)


def _rms(x, g):
    rstd = lax.rsqrt(jnp.mean(x * x, axis=-1, keepdims=True) + EPS)
    n = x * rstd
    return n * g, n, rstd


def _rms_bwd(dy, n, rstd, g):
    dg = jnp.sum(dy * n, axis=0, keepdims=True)
    dn = dy * g
    dx = rstd * (dn - n * jnp.mean(dn * n, axis=-1, keepdims=True))
    return dx, dg


def _norm_bwd(dy, x, g):
    _, n, rstd = _rms(x, g)
    return _rms_bwd(dy, n, rstd, g)


def _colsum(v):
    return jnp.sum(v, axis=0, keepdims=True)


def _silu(x):
    return x * (1.0 / (1.0 + jnp.exp(-x)))


def _lane(shape):
    return lax.broadcasted_iota(jnp.int32, shape, 1)


def _group_mean(v, groups):
    i = lax.broadcasted_iota(jnp.int32, (LANES, LANES), 0)
    j = lax.broadcasted_iota(jnp.int32, (LANES, LANES), 1)
    g = jnp.zeros((LANES, LANES), F32)
    for lo, hi in groups:
        g = jnp.where((i >= lo) & (i < hi) & (j >= lo) & (j < hi), 1.0 / (hi - lo), g)
    head = v.astype(MXU_DTYPE)
    return _dot(head, g) + _dot(v - head.astype(F32), g)


def _in_groups(shape, groups):
    lane = _lane(shape)
    m = jnp.zeros(shape, jnp.bool_)
    for lo, hi in groups:
        m = m | ((lane >= lo) & (lane < hi))
    return m


def _grms(x, g, groups):
    rstd = lax.rsqrt(_group_mean(x * x, groups) + EPS)
    n = jnp.where(_in_groups(x.shape, groups), x * rstd, 0.0)
    return n * g, n, rstd


def _grms_bwd(dy, n, rstd, g, groups):
    dn = dy * g
    return rstd * (dn - n * _group_mean(dn * n, groups)), _colsum(dy * n)


def _rot(x, half, transpose=False):
    first = (_lane(x.shape) % (2 * half)) < half
    up = pltpu.roll(x, LANES - half, axis=1)
    down = pltpu.roll(x, half, axis=1)
    return jnp.where(first, up, -down) if transpose else jnp.where(first, -up, down)


def _rope(x, cos, sin, half):
    return x * cos + _rot(x, half) * sin


def _rope_bwd(dy, cos, sin, half):
    return dy * cos + _rot(dy * sin, half, transpose=True)


def _chunks(x):
    return [x[:, i:i + LANES] for i in range(0, x.shape[1], LANES)]


Q_GROUPS = ((0, NOPE), (NOPE, NOPE + ROPE))
K_GROUPS = ((0, NOPE),)
KPE_GROUPS = ((KPE_LO, KPE_LO + ROPE),)
DIL_GROUPS = ((0, DIL_DIM), (DIL_DIM, 2 * DIL_DIM))


def _col(width, rows=SEQ):
    return pl.BlockSpec((rows, width), lambda h: (0, h))


def _causal_tail(s, tq, fill):
    diag = s[:, s.shape[1] - tq:]
    keep = lax.broadcasted_iota(jnp.int32, diag.shape, 1) <= lax.broadcasted_iota(jnp.int32, diag.shape, 0)
    diag = jnp.where(keep, diag, fill)
    return diag if s.shape[1] == tq else jnp.concatenate([s[:, :s.shape[1] - tq], diag], axis=1)


def mla_fwd(name, q, k, v, scale, tq=256):
    S = q.shape[0]

    def body(q_ref, k_ref, v_ref, o_ref, lse_ref):
        nb = S // tq
        blk = lambda i: slice(i * tq, (i + 1) * tq)

        def scores(i):
            return _dot(q_ref[blk(i), :], k_ref[:(i + 1) * tq, :], "nt")

        def softmax(i, s):
            s = _causal_tail(s * scale, tq, NEG_INF)
            m = jnp.max(s, axis=-1, keepdims=True)
            e = jnp.exp(s - m)
            l = jnp.sum(e, axis=-1, keepdims=True)
            lse_ref[0, blk(i), :] = m + jnp.log(l)
            return (e * (1.0 / l)).astype(MXU_DTYPE)

        def weighted(i, p):
            o_ref[blk(i), :] = _dot(p, v_ref[:(i + 1) * tq, :])

        s, p_prev = scores(0), None
        for i in range(nb):
            s_next = scores(i + 1) if i + 1 < nb else None
            if p_prev is not None:
                weighted(i - 1, p_prev)
            p_prev, s = softmax(i, s), s_next
        weighted(nb - 1, p_prev)

    return pl.pallas_call(
        body, name=name, grid=(HEADS,), in_specs=[_col(LANES)] * 3,
        out_specs=[_col(LANES), pl.BlockSpec((1, S, 1), lambda h: (h, 0, 0))],
        out_shape=[jax.ShapeDtypeStruct((S, MIX_IN), F32), jax.ShapeDtypeStruct((HEADS, S, 1), F32)],
        compiler_params=_params())(q, k, v)


def mla_bwd(name, q, k, v, o, do, lse, scale, tq=256):
    S = q.shape[0]

    def body(q_ref, k_ref, v_ref, o_ref, do_ref, lse_ref, dq_ref, dkv_ref, dkpe_ref, dk_acc, dv_acc):
        dk_acc[...] = jnp.zeros_like(dk_acc)
        dv_acc[...] = jnp.zeros_like(dv_acc)
        for i in range(S // tq):
            kext = (i + 1) * tq
            blk = slice(i * tq, kext)
            qi, kk, vv = q_ref[blk, :], k_ref[:kext, :], v_ref[:kext, :]
            doi = do_ref[blk, :]
            s = _causal_tail(_dot(qi, kk, "nt") * scale, tq, NEG_INF)
            p = jnp.exp(s - lse_ref[0, blk, :])
            dp = _dot(doi, vv, "nt")
            delta = jnp.sum(doi * o_ref[blk, :], axis=-1, keepdims=True)
            ds = p * (dp - delta) * scale
            dq_ref[blk, :] = _dot(ds, kk)
            dk_acc[:kext, :] += _dot(ds, qi, "tn")
            dv_acc[:kext, :] += _dot(p, doi, "tn")
        dk = dk_acc[...]
        lane = _lane(dk.shape)
        dkv_ref[...] = jnp.where(lane < NOPE, dk, 0.0) + dv_acc[...]
        dkpe = jnp.where((lane >= KPE_LO) & (lane < KPE_LO + ROPE), dk, 0.0)
        h = pl.program_id(0)

        @pl.when(h == 0)
        def _():
            dkpe_ref[...] = dkpe

        @pl.when(h > 0)
        def _():
            dkpe_ref[...] += dkpe

    return pl.pallas_call(
        body, name=name, grid=(HEADS,),
        in_specs=[_col(LANES)] * 5 + [pl.BlockSpec((1, S, 1), lambda h: (h, 0, 0))],
        out_specs=[_col(LANES), _col(LANES), pl.BlockSpec((S, LANES), lambda h: (0, 0))],
        out_shape=[jax.ShapeDtypeStruct((S, HEADS * LANES), F32), jax.ShapeDtypeStruct((S, HEADS * LANES), F32),
                   jax.ShapeDtypeStruct((S, LANES), F32)],
        scratch_shapes=[pltpu.VMEM((S, LANES), F32), pltpu.VMEM((S, LANES), F32)],
        compiler_params=_params())(q, k, v, o, do, lse)


BAND_TQ = SPAN


def _band_blocks(L, tq):
    return [(i * tq, (i + 1) * tq, max(0, i * tq - SPAN)) for i in range(L // tq)]


def _class_rows(r, dil, lo, hi):
    return pl.ds(r + dil * lo, hi - lo, stride=dil) if dil > 1 else pl.ds(lo, hi - lo)


def _stack_heads(t, lo):
    zero = jnp.zeros_like(t)
    return jnp.concatenate([jnp.where(lo, t, zero), jnp.where(lo, zero, t)], axis=0)


def _band_mask2(q0, q1, k0):
    n = q1 - q0
    shape = (2 * n, q1 - k0)
    i = lax.broadcasted_iota(jnp.int32, shape, 0)
    dist = (jnp.where(i >= n, i - n, i) + q0) - (lax.broadcasted_iota(jnp.int32, shape, 1) + k0)
    return (dist >= 0) & (dist <= SPAN)


def _pair_col(col0=0):
    return pl.BlockSpec((SEQ, LANES), lambda j: (0, col0 // LANES + j))


def band_fwd(name, q, k, v, dil, dep=None):
    S = q.shape[0]
    L = S // dil
    tq = BAND_TQ
    scale = DIL_DIM ** -0.5
    deps = [] if dep is None else [dep]

    def body(q_ref, k_ref, v_ref, *rest):
        o_ref, lse_ref = rest[len(deps):]
        items = [(r, blk) for r in range(dil) for blk in _band_blocks(L, tq)]
        lo = _lane((tq, LANES)) < DIL_DIM

        def scores(item):
            r, (q0, q1, k0) = item
            qb = q_ref[_class_rows(r, dil, q0, q1), :].astype(MXU_DTYPE)
            return _dot(_stack_heads(qb, lo), k_ref[_class_rows(r, dil, k0, q1), :], "nt")

        def softmax(item, s):
            _, (q0, q1, k0) = item
            s = jnp.where(_band_mask2(q0, q1, k0), s * scale, NEG_INF)
            mx = jnp.max(s, axis=-1, keepdims=True)
            e = jnp.exp(s - mx)
            l = jnp.sum(e, axis=-1, keepdims=True)
            return (e * (1.0 / l)).astype(MXU_DTYPE), mx + jnp.log(l)

        def weighted(item, p, lse):
            r, (q0, q1, k0) = item
            pv = _dot(p, v_ref[_class_rows(r, dil, k0, q1), :])
            o_ref[_class_rows(r, dil, q0, q1), :] = jnp.where(lo, pv[:tq], pv[tq:])
            lse_ref[_class_rows(r, dil, q0, q1), :] = jnp.where(lo, lse[:tq], lse[tq:])

        s, prev = scores(items[0]), None
        for i, item in enumerate(items):
            s_next = scores(items[i + 1]) if i + 1 < len(items) else None
            if prev is not None:
                weighted(items[i - 1], *prev)
            prev, s = softmax(item, s), s_next
        weighted(items[-1], *prev)

    return pl.pallas_call(
        body, name=name, grid=(DIL_WIDTH // LANES,),
        in_specs=[_pair_col()] * 2 + [_pair_col(P_VD)] + [pl.BlockSpec(memory_space=pl.ANY)] * len(deps), out_specs=[_pair_col()] * 2,
        out_shape=[jax.ShapeDtypeStruct((S, DIL_WIDTH), F32)] * 2, compiler_params=_params())(q, k, v, *deps)


def band_bwd(name, q, k, v, lse, lse_mix, o_cat, do_cat, dil, before=None):
    S = q.shape[0]
    L = S // dil
    tq = BAND_TQ
    scale = DIL_DIM ** -0.5
    before = list(before or [])

    def body(q_ref, k_ref, v_ref, lse_ref, mix_ref, o_ref, do_ref, *rest):
        dq_ref, dk_ref, dv_ref = rest[len(before):]
        if before:
            dq0_ref, dk0_ref, dv0_ref = rest[:3]
            dk_ref[...] = dk0_ref[...]
            dv_ref[...] = dv0_ref[...]
        else:
            dk_ref[...] = jnp.zeros_like(dk_ref)
            dv_ref[...] = jnp.zeros_like(dv_ref)
        items = [(r, blk) for r in range(dil) for blk in _band_blocks(L, tq)]
        lo = _lane((tq, LANES)) < DIL_DIM
        per_head = lambda t: jnp.concatenate([t[:, 0:1], t[:, DIL_DIM:DIL_DIM + 1]], axis=0)

        def scores(item):
            r, (q0, q1, k0) = item
            qrows, krows = _class_rows(r, dil, q0, q1), _class_rows(r, dil, k0, q1)
            lse_p, dout = lse_ref[qrows, :], do_ref[qrows, :]
            w2 = per_head(jnp.exp(lse_p - mix_ref[qrows, :]))
            dd = dout * o_ref[qrows, :]
            big_d = jnp.concatenate([jnp.sum(jnp.where(lo, dd, 0.0), axis=-1, keepdims=True),
                                     jnp.sum(jnp.where(lo, 0.0, dd), axis=-1, keepdims=True)], axis=0)
            q2 = _stack_heads(q_ref[qrows, :].astype(MXU_DTYPE), lo)
            dom = (_stack_heads(dout, lo) * w2).astype(MXU_DTYPE)
            return (_dot(q2, k_ref[krows, :], "nt"), _dot(dom, v_ref[krows, :], "nt"), per_head(lse_p), w2 * big_d, q2, dom)

        def softmax_bwd(item, s, dp, lse2, wd2, q2, dom):
            _, (q0, q1, k0) = item
            p = jnp.where(_band_mask2(q0, q1, k0), jnp.exp(s * scale - lse2), 0.0)
            return p.astype(MXU_DTYPE), (p * (dp - wd2) * scale).astype(MXU_DTYPE), q2, dom

        def grads(item, p, ds, q2, dom):
            r, (q0, q1, k0) = item
            qrows, krows = _class_rows(r, dil, q0, q1), _class_rows(r, dil, k0, q1)
            dq2 = _dot(ds, k_ref[krows, :])
            dq = jnp.where(lo, dq2[:tq], dq2[tq:])
            dq_ref[qrows, :] = dq + dq0_ref[qrows, :] if before else dq
            dk_ref[krows, :] += _dot(ds, q2, "tn")
            dv_ref[krows, :] += _dot(p, dom, "tn")

        sc, prev = scores(items[0]), None
        for i, item in enumerate(items):
            sc_next = scores(items[i + 1]) if i + 1 < len(items) else None
            if prev is not None:
                grads(items[i - 1], *prev)
            prev, sc = softmax_bwd(item, *sc), sc_next
        grads(items[-1], *prev)

    cat = _pair_col(HEADS * LANES)
    return pl.pallas_call(
        body, name=name, grid=(DIL_WIDTH // LANES,),
        in_specs=[_pair_col()] * 2 + [_pair_col(P_VD)] + [_pair_col()] * 2 + [cat] * 2 + [_pair_col()] * len(before),
        out_specs=[_pair_col()] * 3, out_shape=[jax.ShapeDtypeStruct((S, DIL_WIDTH), F32)] * 3,
        compiler_params=_params())(q, k, v, lse, lse_mix, o_cat, do_cat, *before)


def combine_fwd(name, outs, lses, o_cat, tm=512):
    S = outs[0].shape[0]

    def body(o1, o2, o3, l1, l2, l3, cat_in, cat_out, mix_ref):
        ls = [l1[...], l2[...], l3[...]]
        m = jnp.maximum(jnp.maximum(ls[0], ls[1]), ls[2])
        e = [jnp.exp(l - m) for l in ls]
        den = e[0] + e[1] + e[2]
        cat_out[...] = (e[0] / den) * o1[...] + (e[1] / den) * o2[...] + (e[2] / den) * o3[...]
        mix_ref[...] = m + jnp.log(den)

    row = pl.BlockSpec((tm, DIL_WIDTH), lambda i: (i, 0))
    return pl.pallas_call(
        body, name=name, grid=(S // tm,), in_specs=[row] * 6 + [pl.BlockSpec(memory_space=pl.ANY)],
        out_specs=[pl.BlockSpec((tm, DIL_WIDTH), lambda i: (i, HEADS * LANES // DIL_WIDTH)), row],
        out_shape=[jax.ShapeDtypeStruct(o_cat.shape, F32), jax.ShapeDtypeStruct((S, DIL_WIDTH), F32)],
        input_output_aliases={6: 0}, compiler_params=_params())(*outs, *lses, o_cat)


def _shift_down(u, n, zero_head):
    out = pltpu.roll(u, n, axis=0)
    return jnp.where(lax.broadcasted_iota(jnp.int32, u.shape, 0) >= n, out, 0.0) if zero_head else out


def _shift_up(u, n, zero_tail):
    rows = u.shape[0]
    out = pltpu.roll(u, rows - n, axis=0)
    return jnp.where(lax.broadcasted_iota(jnp.int32, u.shape, 0) < rows - n, out, 0.0) if zero_tail else out


CONV_ROWS = 512
CONV_HALO = 16


def _conv_chunks(S, tail):
    out = []
    for r0 in range(0, S, CONV_ROWS):
        lo, hi = max(0, r0 - CONV_HALO), min(S, r0 + CONV_ROWS + (CONV_HALO if tail else 0))
        out.append((lo, hi, r0 - lo, CONV_ROWS))
    return out


CONV_TC = 256
CONV_NB = D_FF // CONV_TC


def _half_specs(rows, rows_axis=False):
    if rows_axis:
        return [pl.BlockSpec((rows, D_MODEL), lambda j: (j, 0)), pl.BlockSpec((rows, D_MODEL), lambda j: (j + CONV_NB, 0))]
    return [pl.BlockSpec((rows, CONV_TC), lambda j: (0, j)), pl.BlockSpec((rows, CONV_TC), lambda j: (0, j + CONV_NB))]


def _whole(a):
    return pl.BlockSpec(a.shape, lambda j: (0,) * a.ndim)


def _up_pair(h, ug_ref, uv_ref):
    return jnp.concatenate([_dot(h, ug_ref[...], "nt"), _dot(h, uv_ref[...], "nt")], axis=1)


def _conv_taps(uin, w, b, starts):
    u1, u2 = _shift_down(uin, 1, starts), _shift_down(uin, 2, starts)
    return u1, u2, w[2:3, :] * uin + w[1:2, :] * u1 + w[0:1, :] * u2 + b


def ffn_fwd(name, h, w_up_t, w_conv, b_conv, w_down):
    S = h.shape[0]

    def body(h_ref, ug_ref, uv_ref, wg_ref, wv_ref, bg_ref, bv_ref, wd_ref, dn_ref, up_ref):
        @pl.when(pl.program_id(0) == 0)
        def _():
            dn_ref[...] = jnp.zeros_like(dn_ref)

        w = jnp.concatenate([wg_ref[...], wv_ref[...]], axis=1)
        b = jnp.concatenate([bg_ref[...], bv_ref[...]], axis=1)
        chunks = _conv_chunks(S, tail=False)

        def project(c):
            lo, hi, keep, rows = c
            uin = _up_pair(h_ref[lo:hi, :], ug_ref, uv_ref)
            up_ref[lo + keep:lo + keep + rows, :] = uin[keep:keep + rows]
            return uin

        def gate(c, uin):
            lo, hi, keep, rows = c
            u = _conv_taps(uin, w, b, lo == 0)[2][keep:keep + rows]
            return (_silu(u[:, :CONV_TC]) * u[:, CONV_TC:]).astype(MXU_DTYPE)

        def project_down(c, act):
            dn_ref[c[0] + c[2]:c[0] + c[2] + c[3], :] += _dot(act, wd_ref[...])

        uin, act_prev = project(chunks[0]), None
        for i, c in enumerate(chunks):
            uin_next = project(chunks[i + 1]) if i + 1 < len(chunks) else None
            if act_prev is not None:
                project_down(chunks[i - 1], act_prev)
            act_prev = gate(c, uin)
            uin = uin_next
        project_down(chunks[-1], act_prev)

    return pl.pallas_call(
        body, name=name, grid=(CONV_NB,),
        in_specs=[_whole(h)] + _half_specs(CONV_TC, rows_axis=True) + _half_specs(3) + _half_specs(1)
        + [pl.BlockSpec((CONV_TC, w_down.shape[1]), lambda j: (j, 0))],
        out_specs=[pl.BlockSpec((S, w_down.shape[1]), lambda j: (0, 0)), pl.BlockSpec((S, 2 * CONV_TC), lambda j: (0, j))],
        out_shape=[jax.ShapeDtypeStruct((S, w_down.shape[1]), F32), jax.ShapeDtypeStruct((S, 2 * D_FF), F32)],
        compiler_params=_params())(h, w_up_t, w_up_t, w_conv, w_conv, b_conv, b_conv, w_down)


def ffn_bwd(name, h, up, w_up_t, w_conv, b_conv, d_dn, w_down):
    S, D = h.shape

    def body(h_ref, up_ref, ug_ref, uv_ref, wg_ref, wv_ref, bg_ref, bv_ref, dd_ref, wd_ref,
             dh_ref, gup_ref, gd_ref, dwg_ref, dwv_ref, dbg_ref, dbv_ref):
        @pl.when(pl.program_id(0) == 0)
        def _():
            dh_ref[...] = jnp.zeros_like(dh_ref)

        w = jnp.concatenate([wg_ref[...], wv_ref[...]], axis=1)
        b = jnp.concatenate([bg_ref[...], bv_ref[...]], axis=1)
        w_pair = jnp.concatenate([ug_ref[...], uv_ref[...]], axis=0)
        chunks = _conv_chunks(S, tail=True)

        def project(c):
            return up_ref[c[0]:c[1], :], _dot(dd_ref[c[0]:c[1], :], wd_ref[...], "nt")

        def through_conv(c, uin, da):
            lo, hi, keep, rows = c
            u1, u2, u = _conv_taps(uin, w, b, lo == 0)
            gate, val = u[:, :CONV_TC], u[:, CONV_TC:]
            sig = 1.0 / (1.0 + jnp.exp(-gate))
            du = jnp.concatenate([da * val * (sig * (1.0 + gate * (1.0 - sig))), da * (gate * sig)], axis=1)
            dup = w[2:3, :] * du + w[1:2, :] * _shift_up(du, 1, hi == S) + w[0:1, :] * _shift_up(du, 2, hi == S)
            kept = slice(keep, keep + rows)
            du = du[kept]
            dw = jnp.concatenate([_colsum(du * u2[kept]), _colsum(du * u1[kept]), _colsum(du * uin[kept])], axis=0)
            return dup[kept].astype(MXU_DTYPE), (gate * sig * val)[kept].astype(MXU_DTYPE), dw, _colsum(du)

        def weight_grads(c, dup, act):
            out_rows = slice(c[0] + c[2], c[0] + c[2] + c[3])
            dh_ref[out_rows, :] += _dot(dup, w_pair)
            return _dot(dup, h_ref[out_rows, :], "tn"), _dot(act, dd_ref[out_rows, :], "tn")

        dw, db, g_up, g_dn = 0.0, 0.0, 0.0, 0.0
        proj, done = project(chunks[0]), None
        for i, c in enumerate(chunks):
            proj_next = project(chunks[i + 1]) if i + 1 < len(chunks) else None
            if done is not None:
                gu, gd = weight_grads(chunks[i - 1], *done)
                g_up, g_dn = g_up + gu, g_dn + gd
            dup, act, dw_c, db_c = through_conv(c, *proj)
            dw, db, done, proj = dw + dw_c, db + db_c, (dup, act), proj_next
        gu, gd = weight_grads(chunks[-1], *done)
        g_up, g_dn = g_up + gu, g_dn + gd
        gup_ref[0], gup_ref[1] = g_up[:CONV_TC].astype(gup_ref.dtype), g_up[CONV_TC:].astype(gup_ref.dtype)
        gd_ref[...] = g_dn.astype(gd_ref.dtype)
        dwg_ref[...], dwv_ref[...] = dw[:, :CONV_TC], dw[:, CONV_TC:]
        dbg_ref[...], dbv_ref[...] = db[:, :CONV_TC], db[:, CONV_TC:]

    half = lambda rows: pl.BlockSpec((rows, CONV_TC), lambda j: (0, j))
    rows_blk = pl.BlockSpec((CONV_TC, D), lambda j: (j, 0))
    dh, gup, gd, dwg, dwv, dbg, dbv = pl.pallas_call(
        body, name=name, grid=(CONV_NB,),
        in_specs=[_whole(h), pl.BlockSpec((S, 2 * CONV_TC), lambda j: (0, j))] + _half_specs(CONV_TC, rows_axis=True) + _half_specs(3)
        + _half_specs(1) + [_whole(d_dn), rows_blk],
        out_specs=[pl.BlockSpec((S, D), lambda j: (0, 0)), pl.BlockSpec((2, CONV_TC, D), lambda j: (0, j, 0)), rows_blk,
                   half(3), half(3), half(1), half(1)],
        out_shape=[jax.ShapeDtypeStruct((S, D), F32), jax.ShapeDtypeStruct((2, D_FF, D), MXU_DTYPE),
                   jax.ShapeDtypeStruct((D_FF, D), MXU_DTYPE)]
        + [jax.ShapeDtypeStruct((3, D_FF), F32)] * 2 + [jax.ShapeDtypeStruct((1, D_FF), F32)] * 2,
        compiler_params=_params())(h, up, w_up_t, w_up_t, w_conv, w_conv, b_conv, b_conv, d_dn, w_down)
    return dh, gup.reshape(2 * D_FF, D), gd, jnp.concatenate([dwg, dwv], axis=1), jnp.concatenate([dbg, dbv], axis=1)


def adamw(name, w, parts, m, v, tr=None):
    apart = w.ndim == 3
    R, C = w.shape[0], w.shape[-1]
    tr = tr or R
    assert R % tr == 0
    c1 = 1.0 - ADAM_B1 ** ADAM_STEP
    c2 = 1.0 - ADAM_B2 ** ADAM_STEP
    np_ = len(parts)

    def body(*refs):
        w_ref, m_ref, v_ref = refs[0], refs[1 + np_], refs[2 + np_]
        go_ref, d_ref, mo_ref, vo_ref = refs[3 + np_:]
        terms = []
        for part, ref in zip(parts, refs[1:1 + np_], strict=True):
            terms += [ref[...]] if part.ndim == 2 else [ref[p] for p in range(part.shape[0])]
        g = terms[0].astype(F32)
        for term in terms[1:]:
            g = g + term.astype(F32)
        m2 = ADAM_B1 * m_ref[...] + (1.0 - ADAM_B1) * g
        v2 = ADAM_B2 * v_ref[...] + (1.0 - ADAM_B2) * (g * g)
        go_ref[...] = g
        mo_ref[...] = m2
        vo_ref[...] = v2
        d_ref[...] = -ADAM_LR * ((m2 / c1) / (jnp.sqrt(v2 / c2) + ADAM_EPS) + ADAM_WD * w_ref[...])

    blk = pl.BlockSpec((tr, C), lambda i: (i, 0))
    own = pl.BlockSpec((tr, None, C), lambda i: (i, 0, 0)) if apart else blk
    part_specs = [blk if p.ndim == 2 else pl.BlockSpec((p.shape[0], tr, C), lambda i: (0, i, 0)) for p in parts]
    return pl.pallas_call(
        body, name=name, grid=(R // tr,),
        in_specs=[own] + part_specs + [own, own], out_specs=[own] * 4,
        out_shape=[jax.ShapeDtypeStruct(w.shape, F32)] * 4, compiler_params=_params())(w, *parts, m, v)


def _place():
    return lax.axis_index("x"), lax.axis_index("y"), lax.axis_index("c")


def all_gather(name, arrs, after=None):
    n = len(arrs)
    deps = [] if after is None else [after]

    def body(*refs):
        ins, outs = refs[:n], refs[n + len(deps):2 * n + len(deps)]
        send_sems, recv_sems, local_sems = refs[2 * n + len(deps):]
        x, y, c = _place()
        me, sibling = (x, y, c), (x, y, 1 - c)
        chips = [(1 - x, y), (x, 1 - y), (1 - x, 1 - y)]
        sends = []
        for t in range(n):
            out = outs[t]

            def slot(px, py, pc, out=out):
                return out.at[4 * px + 2 * py + pc]

            def copy(k, block, to, src=None, t=t, slot=slot):
                return pltpu.make_async_remote_copy(
                    src_ref=slot(*block) if src is None else src, dst_ref=slot(*block),
                    send_sem=send_sems.at[7 * t + k], recv_sem=recv_sems.at[7 * t + k],
                    device_id=to, device_id_type=MESH_ID)

            mine = pltpu.make_async_copy(ins[t], slot(*me), local_sems.at[t])
            mine.start()
            first = [copy(0, me, sibling, src=ins[t])]
            first += [copy(1 + j, me, (*chip, c), src=ins[t]) for j, chip in enumerate(chips)]
            for cp in first:
                cp.start()
            sends.append((mine, first, copy))
        for t in range(n):
            mine, first, copy = sends[t]
            passed = [copy(4 + j, (*chip, c), sibling) for j, chip in enumerate(chips)]
            for j, chip in enumerate(chips):
                copy(1 + j, (*chip, c), me).wait_recv()
                passed[j].start()
            copy(0, sibling, me).wait_recv()
            for j, chip in enumerate(chips):
                copy(4 + j, (*chip, 1 - c), me).wait_recv()
            for cp in first + passed:
                cp.wait_send()
            mine.wait()

    any_spec = pl.BlockSpec(memory_space=pl.ANY)
    res = pl.pallas_call(
        body, name=name, in_specs=[any_spec] * (n + len(deps)), out_specs=[any_spec] * n,
        out_shape=[jax.ShapeDtypeStruct((N_DEV,) + a.shape, a.dtype) for a in arrs],
        scratch_shapes=[pltpu.SemaphoreType.DMA((7 * n,)), pltpu.SemaphoreType.DMA((7 * n,)), pltpu.SemaphoreType.DMA((n,))],
        compiler_params=pltpu.CompilerParams(has_side_effects=True))(*arrs, *deps)
    return list(res)


def ada_modulation(name, c, w_ada):
    n_mod = w_ada.shape[1]

    def exchange(src_ref, dst_ref, send_sems, recv_sems):
        x, y, c_ = _place()
        me = 4 * x + 2 * y + c_
        copies = []
        for k in range(1, N_DEV):
            px, py, pc = x ^ (k >> 2), y ^ ((k >> 1) & 1), c_ ^ (k & 1)
            copies.append(pltpu.make_async_remote_copy(
                src_ref=src_ref, dst_ref=dst_ref.at[me], send_sem=send_sems.at[k - 1], recv_sem=recv_sems.at[k - 1],
                device_id=(px, py, pc), device_id_type=MESH_ID))
        for cp in copies:
            cp.start()
        for cp in copies:
            cp.wait_recv()
        for cp in copies:
            cp.wait_send()
        return me

    def body(c_ref, w_ref, sc_ref, mod_ref, c_all, send_c, recv_c, send_m, recv_m):
        me = exchange(c_ref, c_all, send_c, recv_c)
        c_all[me] = c_ref[...]
        sc = _silu(jnp.concatenate([c_all[p] for p in range(N_DEV)], axis=0))
        sc_ref[...] = sc.astype(sc_ref.dtype)
        mod_ref[me] = _dot(sc, w_ref[...])
        exchange(mod_ref.at[me], mod_ref, send_m, recv_m)

    vmem = pl.BlockSpec(memory_space=pltpu.VMEM)
    return pl.pallas_call(
        body, name=name, in_specs=[vmem, vmem], out_specs=[vmem, vmem],
        out_shape=[jax.ShapeDtypeStruct((N_DEV, c.shape[1]), MXU_DTYPE), jax.ShapeDtypeStruct((N_DEV, N_DEV, n_mod), F32)],
        scratch_shapes=[pltpu.VMEM((N_DEV, 1, c.shape[1]), F32)] + [pltpu.SemaphoreType.DMA((N_DEV - 1,))] * 4,
        compiler_params=pltpu.CompilerParams(has_side_effects=True, vmem_limit_bytes=VMEM_LIMIT))(c, w_ada)


HBM_SPEC = pl.BlockSpec(memory_space=pltpu.HBM)
SEM_SPEC = pl.BlockSpec(memory_space=pltpu.SEMAPHORE)
DATAFLOW = pltpu.SideEffectType.DATAFLOW_SIDE_EFFECTING


def _exchange_copies(srcs, lands, send_sems, recv_sems, gather, first=0):
    x, y, c = _place()
    me = 4 * x + 2 * y + c
    out = []
    for t, (src, land) in enumerate(zip(srcs, lands, strict=True)):
        for k in range(1, N_DEV):
            px, py, pc = x ^ (k >> 2), y ^ ((k >> 1) & 1), c ^ (k & 1)
            sem = 7 * (first + t) + k - 1
            out.append((k, pltpu.make_async_remote_copy(
                src_ref=src if gather else src.at[4 * px + 2 * py + pc],
                dst_ref=land.at[me] if gather else land.at[k - 1],
                send_sem=send_sems.at[sem], recv_sem=recv_sems.at[sem],
                device_id=(px, py, pc), device_id_type=MESH_ID)))
    return out


TREE_DIRECT = (1, 2, 4, 6)
TREE_FORWARDED = (3, 5, 7)


def exchange_start(name, arrs, gather, after=None, tree=False):
    n = len(arrs)
    lands = [lax.empty(((N_DEV,) + a.shape) if gather else ((N_DEV - 1,) + a.shape[1:]), a.dtype) for a in arrs]
    deps = [] if after is None else [after]

    def body(*refs):
        srcs, land_refs = refs[:n], refs[n:2 * n]
        send_sems, recv_sems = refs[2 * n + len(deps)], refs[2 * n + len(deps) + 1]
        token, local_sems = refs[-2], refs[-1]
        own = []
        if gather:
            x, y, c = _place()
            own = [pltpu.make_async_copy(src, land.at[4 * x + 2 * y + c], local_sems.at[t])
                   for t, (src, land) in enumerate(zip(srcs, land_refs, strict=True))]
        for cp in own:
            cp.start()
        for k, cp in _exchange_copies(srcs, land_refs, send_sems, recv_sems, gather):
            if not tree or k in TREE_DIRECT:
                cp.start()
        for cp in own:
            cp.wait()
        token[...] = jnp.zeros_like(token)

    hbm = lambda a: pltpu.HBM(a.shape, a.dtype)
    res = pl.pallas_call(
        body, name=name,
        out_shape=(pltpu.SemaphoreType.DMA((7 * n,)), pltpu.SemaphoreType.DMA((7 * n,)), *[hbm(a) for a in arrs],
                   *[hbm(l) for l in lands], jax.ShapeDtypeStruct((8, 128), F32)),
        in_specs=[HBM_SPEC] * (2 * n) + [pl.BlockSpec(memory_space=pl.ANY)] * len(deps),
        out_specs=(SEM_SPEC, SEM_SPEC, *[HBM_SPEC] * (2 * n), pl.BlockSpec(memory_space=pltpu.VMEM)),
        input_output_aliases={i: 2 + i for i in range(2 * n)}, scratch_shapes=[pltpu.SemaphoreType.DMA((n,))],
        compiler_params=pltpu.CompilerParams(has_side_effects=DATAFLOW),
    )(*[pltpu.with_memory_space_constraint(a, pltpu.HBM) for a in arrs + lands], *deps)
    return res[0], res[1], list(res[2:2 + n]), list(res[2 + n:2 + 2 * n]), res[-1]


def exchange_forward(name, started, after, first=0, count=None):
    send_sems, recv_sems, srcs, lands, _ = started
    count = len(srcs) - first if count is None else count
    mine = lands[first:first + count]
    n = len(mine)

    def copies(land_refs, send_ref, recv_ref):
        x, y, c = _place()
        out = []
        for t, land in enumerate(land_refs):
            for k in (2, 4, 6):
                slot = land.at[4 * (x ^ (k >> 2)) + 2 * (y ^ ((k >> 1) & 1)) + c]
                came, goes = 7 * (first + t) + k - 1, 7 * (first + t) + (k ^ 1) - 1
                out.append((
                    pltpu.make_async_remote_copy(src_ref=slot, dst_ref=slot, send_sem=send_ref.at[came], recv_sem=recv_ref.at[came],
                                                 device_id=(x, y, c), device_id_type=MESH_ID),
                    pltpu.make_async_remote_copy(src_ref=slot, dst_ref=slot, send_sem=send_ref.at[goes], recv_sem=recv_ref.at[goes],
                                                 device_id=(x, y, 1 - c), device_id_type=MESH_ID)))
        return out

    def arrived(*refs):
        for came, _ in copies(refs[:n], refs[n], refs[n + 1]):
            came.wait_recv()

    def pass_on(*refs):
        for _, goes in copies(refs[:n], refs[n], refs[n + 1]):
            goes.start()
        refs[-1][...] = jnp.zeros_like(refs[-1])

    hbm = lambda a: pltpu.HBM(a.shape, a.dtype)
    here = pl.pallas_call(
        arrived, name=name + "_arrived", out_shape=tuple(hbm(a) for a in mine),
        in_specs=[HBM_SPEC] * n + [SEM_SPEC, SEM_SPEC, pl.BlockSpec(memory_space=pl.ANY)],
        out_specs=tuple([HBM_SPEC] * n), input_output_aliases={i: i for i in range(n)},
        compiler_params=pltpu.CompilerParams(has_side_effects=DATAFLOW),
    )(*mine, send_sems, recv_sems, after)
    res = pl.pallas_call(
        pass_on, name=name, out_shape=(*[hbm(a) for a in mine], jax.ShapeDtypeStruct((8, 128), F32)),
        in_specs=[HBM_SPEC] * n + [SEM_SPEC, SEM_SPEC],
        out_specs=(*[HBM_SPEC] * n, pl.BlockSpec(memory_space=pltpu.VMEM)), input_output_aliases={i: i for i in range(n)},
        compiler_params=pltpu.CompilerParams(has_side_effects=DATAFLOW),
    )(*here, send_sems, recv_sems)
    lands = lands[:first] + list(res[:n]) + lands[first + count:]
    return (send_sems, recv_sems, srcs, lands, res[-1])


def exchange_wait(name, started, gather, after, first=0, count=None, tree=False):
    send_sems, recv_sems, srcs, lands, _ = started
    count = len(srcs) - first if count is None else count
    srcs, lands = srcs[first:first + count], lands[first:first + count]
    n = len(srcs)

    def body(*refs):
        src_refs, land_refs = refs[:n], refs[n:2 * n]
        copies = _exchange_copies(src_refs, land_refs, refs[2 * n], refs[2 * n + 1], gather, first)
        for _, cp in copies:
            cp.wait_send()
        for k, cp in copies:
            if not tree or k in (1,) + TREE_FORWARDED:
                cp.wait_recv()

    hbm = lambda a: pltpu.HBM(a.shape, a.dtype)
    res = pl.pallas_call(
        body, name=name, out_shape=tuple(hbm(a) for a in srcs + lands),
        in_specs=[HBM_SPEC] * (2 * n) + [SEM_SPEC, SEM_SPEC, pl.BlockSpec(memory_space=pl.ANY)],
        out_specs=tuple([HBM_SPEC] * (2 * n)), input_output_aliases={i: i for i in range(2 * n)},
        compiler_params=pltpu.CompilerParams(has_side_effects=DATAFLOW),
    )(*srcs, *lands, send_sems, recv_sems, after)
    return list(res[:n]), list(res[n:])


def _gather_cols(stack):
    p, k, n = stack.shape
    return stack.transpose(1, 0, 2).reshape(k, p * n)


def _scatter_cols(full):
    k, n = full.shape
    return full.reshape(k, N_DEV, n // N_DEV).transpose(1, 0, 2)


def _gather_rows(stack):
    p, r, n = stack.shape
    return stack.reshape(p * r, n)


def _scatter_rows(full):
    r, n = full.shape
    return full.reshape(N_DEV, r // N_DEV, n)


_IN_NAT = Q_LORA + KV_LORA
TRANSPOSED = ("w_in", "w_q_b", "w_up")
ROWS_APART = ("w_in", "w_conv")


def to_kernel_layout(name, w):
    if name == "w_in":
        z = lambda n: jnp.zeros((n, w.shape[1]), w.dtype)
        return jnp.concatenate([w[:_IN_NAT], z(KPE_LO), w[_IN_NAT:_IN_NAT + ROPE], z(LANES - KPE_LO - ROPE), w[_IN_NAT + ROPE:]], axis=0)
    if name == "w_q_b":
        return jnp.pad(w.reshape(HEADS, NOPE + ROPE, -1), ((0, 0), (0, LANES - NOPE - ROPE), (0, 0))).reshape(HEADS * LANES, -1)
    if name == "w_o":
        mla = jnp.pad(w[:HEADS * NOPE].reshape(HEADS, NOPE, -1), ((0, 0), (LANES - NOPE, 0), (0, 0))).reshape(HEADS * LANES, -1)
        return jnp.concatenate([mla, w[HEADS * NOPE:]], axis=0)
    return w


def from_kernel_layout(name, g):
    if name == "w_in":
        return jnp.concatenate([g[:_IN_NAT], g[P_KPE + KPE_LO:P_KPE + KPE_LO + ROPE], g[P_QD:]], axis=0)
    if name == "w_q_b":
        return g.reshape(HEADS, LANES, -1)[:, :NOPE + ROPE, :].reshape(HEADS * (NOPE + ROPE), -1)
    if name == "w_o":
        mla = g[:HEADS * LANES].reshape(HEADS, LANES, -1)[:, LANES - NOPE:, :].reshape(HEADS * NOPE, -1)
        return jnp.concatenate([mla, g[HEADS * LANES:]], axis=0)
    return g


SMALL_COLS = 1024
SMALL_ROWS = 24
SMALL_AT = {"loss": (0, 0, 1), "b_ada": (1, 0, 6 * D_MODEL), "g_mix_norm": (7, 0, D_MODEL), "g_q_lat": (8, 0, Q_LORA),
            "g_kv_lat": (9, 0, KV_LORA), "g_mla_q_nope": (10, 0, NOPE), "g_mla_q_pe": (10, 128, ROPE),
            "g_mla_k_nope": (10, 256, NOPE), "g_mla_k_pe": (10, 384, ROPE), "g_dil_q": (10, 512, DIL_DIM),
            "g_dil_k": (10, 640, DIL_DIM), "g_ffn_norm": (11, 0, D_MODEL), "b_conv": (12, 0, 2 * D_FF)}
SMALL_PARAMS = tuple(n for n in SMALL_AT if n != "loss")


def _pack_small(values):
    by_row = {}
    for name, (row, off, n) in SMALL_AT.items():
        by_row.setdefault(row, []).append((off, values[name].reshape(-1).astype(F32)))
    out = []
    for row in sorted(by_row):
        pieces, at = [], 0
        for off, v in sorted(by_row[row], key=lambda t: t[0]):
            pieces += [jnp.zeros((off - at,), F32), v]
            at = off + v.shape[0]
        flat = jnp.concatenate(pieces)
        nrows = -(-flat.shape[0] // SMALL_COLS)
        out.append(jnp.pad(flat, (0, nrows * SMALL_COLS - flat.shape[0])).reshape(nrows, SMALL_COLS))
    packed = jnp.concatenate(out, axis=0)
    return jnp.pad(packed, ((0, SMALL_ROWS - packed.shape[0]), (0, 0)))


def _adam(w, g, m, v):
    c1 = 1.0 - ADAM_B1 ** ADAM_STEP
    c2 = 1.0 - ADAM_B2 ** ADAM_STEP
    m2 = ADAM_B1 * m + (1.0 - ADAM_B1) * g
    v2 = ADAM_B2 * v + (1.0 - ADAM_B2) * (g * g)
    return -ADAM_LR * ((m2 / c1) / (jnp.sqrt(v2 / c2) + ADAM_EPS) + ADAM_WD * w), m2, v2


def adamw_small(name, stack, params):
    flat = [a for n in SMALL_PARAMS for a in params[n]]

    def body(stack_ref, *refs):
        ins, outs = refs[:len(flat)], refs[len(flat):]
        g_all = stack_ref[0]
        for p in range(1, N_DEV):
            g_all = g_all + stack_ref[p]
        outs[0][...] = g_all[0:1, 0:1]
        for i, pname in enumerate(SMALL_PARAMS):
            row, off, n = SMALL_AT[pname]
            w_ref, m_ref, v_ref = ins[3 * i:3 * i + 3]
            go_ref, d_ref, mo_ref, vo_ref = outs[1 + 4 * i:5 + 4 * i]
            for c0 in range(0, n, SMALL_COLS):
                cn = min(SMALL_COLS, n - c0)
                r = row + c0 // SMALL_COLS
                g = g_all[r:r + 1, off:off + cn]
                cols = (slice(None), slice(c0, c0 + cn))
                d, m2, v2 = _adam(w_ref[cols], g, m_ref[cols], v_ref[cols])
                go_ref[cols], d_ref[cols], mo_ref[cols], vo_ref[cols] = g, d, m2, v2

    whole = lambda a: pl.BlockSpec(a.shape, lambda: (0,) * a.ndim)
    out_shape = [jax.ShapeDtypeStruct((1, 1), F32)] + [jax.ShapeDtypeStruct(a.shape, F32) for n in SMALL_PARAMS for a in params[n][:1] * 4]
    res = pl.pallas_call(body, name=name, in_specs=[whole(stack)] + [whole(a) for a in flat],
                         out_specs=[pl.BlockSpec(s.shape, lambda s=s: (0,) * len(s.shape)) for s in out_shape],
                         out_shape=out_shape, compiler_params=_params())(stack, *flat)
    return res[0], {n: res[1 + 4 * i:5 + 4 * i] for i, n in enumerate(SMALL_PARAMS)}


def _local_step(x, pos, mod, target, w, fetch, emit, halfway=lambda after: None):
    S = SEQ
    sh1, sc1, g1, sh2, sc2, g2 = [mod[:, i * D_MODEL:(i + 1) * D_MODEL] for i in range(6)]
    zeros = lambda n: jnp.zeros((1, n), F32)
    g_q = jnp.concatenate([w["g_mla_q_nope"], w["g_mla_q_pe"], zeros(LANES - NOPE - ROPE)], axis=1)
    g_k = jnp.concatenate([w["g_mla_k_nope"], zeros(LANES - NOPE)], axis=1)
    g_kpe = jnp.concatenate([zeros(KPE_LO), w["g_mla_k_pe"], zeros(LANES - KPE_LO - ROPE)], axis=1)
    g_dq = jnp.concatenate([w["g_dil_q"]] * 2, axis=1)
    g_dk = jnp.concatenate([w["g_dil_k"]] * 2, axis=1)
    b_conv = w["b_conv"]

    def inv_freq(d):
        return jnp.power(ROPE_THETA, -2.0 * jnp.arange(d // 2, dtype=F32) / d)

    n_m, n_d = ROPE // 2, DIL_DIM // 2
    freqs = jnp.concatenate([inv_freq(ROPE), inv_freq(DIL_DIM), jnp.zeros((LANES - n_m - n_d,), F32)]).reshape(1, LANES)

    def tables_fn(rows, params):
        (p,), (f,) = rows, params
        c, s = jnp.cos(p * f), jnp.sin(p * f)
        one, zero = jnp.ones_like(c), jnp.zeros_like(c)
        mla = lambda t, fill: jnp.concatenate([fill[:, :KPE_LO], t[:, :n_m], t[:, :n_m], fill[:, :LANES - KPE_LO - ROPE]], axis=1)
        dil = lambda t: jnp.concatenate([t[:, n_m:n_m + n_d]] * 4, axis=1)
        return [mla(c, one), mla(s, zero), dil(c), dil(s)], []

    cos_m, sin_m, cos_d, sin_d = rowwise("rope_tables", tables_fn, [pos], [freqs], [(LANES, F32)] * 4)
    tables = [cos_m, sin_m, cos_d, sin_d]
    H_M, H_D = ROPE // 2, DIL_DIM // 2

    def ln1_fn(rows, params):
        (xv,), (g, sc, sh) = rows, params
        y, _, _ = _rms(xv, g)
        return [y * (1.0 + sc) + sh], []

    (h,) = rowwise("ln1_fwd", ln1_fn, [x], [w["g_mix_norm"], sc1, sh1], [(D_MODEL, MXU_DTYPE)])
    w_in = fetch("w_in", h)

    def proj_fn(rows, params):
        (hv, cm, sm, cd, sd), (w_t, gq, gkv, gkp, gdq, gdk) = rows, params
        pv = _dot(hv, w_t, "nt")
        kper = _rope(_grms(pv[:, P_KPE:P_QD], gkp, KPE_GROUPS)[0], cm, sm, H_M)
        qd = [_rope(_grms(c, gdq, DIL_GROUPS)[0], cd, sd, H_D) for c in _chunks(pv[:, P_QD:P_KD])]
        kd = [_rope(_grms(c, gdk, DIL_GROUPS)[0], cd, sd, H_D) for c in _chunks(pv[:, P_KD:P_VD])]
        return [pv, _rms(pv[:, P_QLAT:P_KVLAT], gq)[0], _rms(pv[:, P_KVLAT:P_KPE], gkv)[0], kper,
                jnp.concatenate(qd, axis=1), jnp.concatenate(kd, axis=1)], []

    post_params = [w["g_q_lat"], w["g_kv_lat"], g_kpe, g_dq, g_dk]
    proj, qln, kvn, kper, qd_r, kd_r = rowwise(
        "proj_fwd", proj_fn, [h] + tables, [w_in] + post_params,
        [(P_END, F32), (Q_LORA, MXU_DTYPE), (KV_LORA, MXU_DTYPE), (LANES, MXU_DTYPE)] + [(DIL_WIDTH, F32)] * 2, tm=256)
    w_q_b, w_kv_b = fetch("w_q_b", qln), fetch("w_kv_b", kvn)

    def mla_proj_fn(rows, params):
        (qlv, kvlv, kp, cm, sm), (wq_t, wkv, gq, gk) = rows, params
        qv, kvv = _dot(qlv, wq_t, "nt"), _dot(kvlv, wkv)
        value_lanes = _lane(kp.shape) >= NOPE
        qs, ks, vs = [], [], []
        for qc, kc in zip(_chunks(qv), _chunks(kvv), strict=True):
            qs.append(_rope(_grms(qc, gq, Q_GROUPS)[0], cm, sm, H_M))
            ks.append(_grms(kc, gk, K_GROUPS)[0] + kp)
            vs.append(jnp.where(value_lanes, kc, 0.0))
        return [qv, kvv] + [jnp.concatenate(t, axis=1) for t in (qs, ks, vs)], []

    q, kv, q_mla, k_mla, v_mla = rowwise(
        "mla_proj", mla_proj_fn, [qln, kvn, kper, cos_m, sin_m], [w_q_b, w_kv_b, g_q, g_k],
        [(HEADS * LANES, F32)] * 2 + [(HEADS * LANES, MXU_DTYPE)] * 3, tm=256)
    mla_scale = (NOPE + ROPE) ** -0.5
    o_cat, lse_mla = mla_fwd("mla_fwd", q_mla, k_mla, v_mla, mla_scale)
    passed = halfway(lse_mla)

    band = [band_fwd(f"band{dil}_fwd", qd_r, kd_r, proj, dil, dep=passed) for dil in DILATIONS]
    o_cat, lse_mix = combine_fwd("dil_combine", [b[0] for b in band], [b[1] for b in band], o_cat)
    w_o = fetch("w_o", o_cat)

    def mid_fn(rows, params):
        (ov, xv), (w_out, gate1, g, sc, sh) = rows, params
        mx = _dot(ov, w_out)
        x1 = xv + gate1 * mx
        y, _, _ = _rms(x1, g)
        return [mx, x1, y * (1.0 + sc) + sh], []

    mix, x1, h2 = rowwise("mix_fwd", mid_fn, [o_cat, x], [w_o, g1, w["g_ffn_norm"], sc2, sh2],
                          [(D_MODEL, F32), (D_MODEL, F32), (D_MODEL, MXU_DTYPE)], tm=256)
    w_up, w_conv, w_down = fetch("w_up", h2), fetch("w_conv", h2), fetch("w_down", h2)
    dn, up = ffn_fwd("ffn_fwd", h2, w_up, w_conv, b_conv, w_down)

    def final_fn(rows, params):
        (x1v, dnv, tgt), (gate2,) = rows, params
        r = x1v + gate2 * dnv - tgt
        dy = r * (1.0 / D_MODEL)
        loss = jnp.sum(_colsum(r * r), axis=-1, keepdims=True) * (0.5 / D_MODEL)
        return [dy, gate2 * dy], [loss, _colsum(dy * dnv)]

    dy, d_dn, loss, dg2 = rowwise("loss_head", final_fn, [x1, dn, target], [g2], [(D_MODEL, F32), (D_MODEL, MXU_DTYPE)],
                                  [1, D_MODEL])
    dh2, g_up, g_down, g_w_conv, g_b_conv = ffn_bwd("ffn_bwd", h2, up, w_up, w_conv, b_conv, d_dn, w_down)
    emit("w_down", g_down)
    emit("w_conv", g_w_conv)
    sent = emit("w_up", g_up)

    def mid_bwd_fn(rows, params):
        (dh2v, dyv, x1v, mx), (gate1, g, sc) = rows, params
        yn, n, rstd = _rms(x1v, g)
        dx_n, dg = _rms_bwd(dh2v * (1.0 + sc), n, rstd, g)
        dx1 = dyv + dx_n
        return [dx1, gate1 * dx1], [dg, _colsum(dh2v * yn), _colsum(dh2v), _colsum(dx1 * mx)]

    dx1, dmix, dg_ffn, dsc2, dsh2, dg1 = rowwise(
        "mid_bwd", mid_bwd_fn, [dh2, dy, x1, mix], [g1, w["g_ffn_norm"], sc2], [(D_MODEL, F32), (D_MODEL, MXU_DTYPE)],
        [D_MODEL] * 4, dep=sent)

    sent = emit("w_o", matmul("mix_wgrad", o_cat, dmix, "tn", tm=512, out_dtype=MXU_DTYPE))
    do_cat = matmul("mix_dgrad", dmix, w_o, "nt", tm=512, dep=sent)
    dband = None
    for dil, b in zip(DILATIONS, band):
        dband = band_bwd(f"band{dil}_bwd", qd_r, kd_r, proj, b[1], lse_mix, o_cat, do_cat, dil, before=dband)
    dq_mla, dkv_mla, dkper = mla_bwd("mla_bwd", q_mla, k_mla, v_mla, o_cat, do_cat, lse_mla, mla_scale)

    def mla_prep_bwd_fn(rows, params):
        (dqv, dkvv, qv, kvv, cm, sm), (gq, gk) = rows, params
        nope_lanes = _lane(cm.shape) < NOPE
        dqs, dkvs, dgq, dgk = [], [], 0.0, 0.0
        for dqc, dkc, qc, kc in zip(_chunks(dqv), _chunks(dkvv), _chunks(qv), _chunks(kvv), strict=True):
            _, n, rstd = _grms(qc, gq, Q_GROUPS)
            dx, dg = _grms_bwd(_rope_bwd(dqc, cm, sm, H_M), n, rstd, gq, Q_GROUPS)
            dqs.append(dx)
            dgq = dgq + dg
            _, n, rstd = _grms(kc, gk, K_GROUPS)
            dx, dg = _grms_bwd(dkc, n, rstd, gk, K_GROUPS)
            dkvs.append(jnp.where(nope_lanes, dx, dkc))
            dgk = dgk + dg
        return [jnp.concatenate(dqs, axis=1), jnp.concatenate(dkvs, axis=1)], [dgq, dgk]

    dq, dkv, dg_q, dg_k = rowwise("mla_prep_bwd", mla_prep_bwd_fn, [dq_mla, dkv_mla, q, kv, cos_m, sin_m], [g_q, g_k],
                                  [(HEADS * LANES, MXU_DTYPE)] * 2, [LANES, LANES], tm=256)
    emit("w_q_b", matmul("q_wgrad", dq, qln, "tn", out_dtype=MXU_DTYPE))
    emit("w_kv_b", matmul("kv_wgrad", kvn, dkv, "tn", out_dtype=MXU_DTYPE))

    def pre_bwd_fn(rows, params):
        dqv, dkvv, dkp, dqd_, dkd_, dvd_, pv, cm, sm, cd, sd = rows
        wq_t, wkv, gq, gkv, gkp, gdq, gdk = params
        dql, dkvl = _dot(dqv, wq_t), _dot(dkvv, wkv, "nt")
        r_q = _norm_bwd(dql, pv[:, P_QLAT:P_KVLAT], gq)
        r_kv = _norm_bwd(dkvl, pv[:, P_KVLAT:P_KPE], gkv)
        _, n, rstd = _grms(pv[:, P_KPE:P_QD], gkp, KPE_GROUPS)
        r_kp = _grms_bwd(_rope_bwd(dkp, cm, sm, H_M), n, rstd, gkp, KPE_GROUPS)
        outs, dgs = [r_q[0], r_kv[0], r_kp[0]], []
        for dval, lo, g in ((dqd_, P_QD, gdq), (dkd_, P_KD, gdk)):
            dg_sum = 0.0
            for dc, xc in zip(_chunks(dval), _chunks(pv[:, lo:lo + DIL_WIDTH]), strict=True):
                _, n, rstd = _grms(xc, g, DIL_GROUPS)
                dx, dg = _grms_bwd(_rope_bwd(dc, cd, sd, H_D), n, rstd, g, DIL_GROUPS)
                outs.append(dx)
                dg_sum = dg_sum + dg
            dgs.append(dg_sum)
        return [jnp.concatenate(outs + [dvd_], axis=1)], [r_q[1], r_kv[1], r_kp[1]] + dgs

    dproj, dg_q_lat, dg_kv_lat, dg_kpe, dg_dq, dg_dk = rowwise(
        "proj_pre_bwd", pre_bwd_fn,
        [dq, dkv, dkper] + list(dband) + [proj] + tables, [w_q_b, w_kv_b] + post_params,
        [(P_END, MXU_DTYPE)], [Q_LORA, KV_LORA, LANES, LANES, LANES], tm=256)
    sent = emit("w_in", matmul("proj_wgrad", dproj, h, "tn", tn=512, out_dtype=MXU_DTYPE))

    def ln1_bwd_fn(rows, params):
        (dpv, dres, xv), (w_t, g, sc) = rows, params
        dhv = _dot(dpv, w_t)
        yn, n, rstd = _rms(xv, g)
        dx_n, dg = _rms_bwd(dhv * (1.0 + sc), n, rstd, g)
        return [dres + dx_n], [dg, _colsum(dhv * yn), _colsum(dhv)]

    grad_x, dg_mix, dsc1, dsh1 = rowwise("proj_dgrad", ln1_bwd_fn, [dproj, dx1, x], [w_in, w["g_mix_norm"], sc1],
                                         [(D_MODEL, F32)], [D_MODEL] * 3, tm=256, dep=sent)
    dmod = jnp.concatenate([dsh1, dsc1, dg1, dsh2, dsc2, dg2], axis=-1)
    small = {"loss": loss, "b_ada": dmod, "g_mix_norm": dg_mix, "g_q_lat": dg_q_lat, "g_kv_lat": dg_kv_lat,
             "g_mla_q_nope": dg_q[:, :NOPE], "g_mla_q_pe": dg_q[:, NOPE:NOPE + ROPE], "g_mla_k_nope": dg_k[:, :NOPE],
             "g_mla_k_pe": dg_kpe[:, KPE_LO:KPE_LO + ROPE], "g_dil_q": dg_dq[:, :DIL_DIM] + dg_dq[:, DIL_DIM:],
             "g_dil_k": dg_dk[:, :DIL_DIM] + dg_dk[:, DIL_DIM:], "g_ffn_norm": dg_ffn,
             "b_conv": g_b_conv}
    return grad_x, small


COL_SHARDED = ("w_kv_b", "w_conv")
ROW_SHARDED = ("w_o", "w_down") + TRANSPOSED
ADAM_TILE = {"w_ada": 256, "w_up": 176, "w_down": 176}
GATHER_GROUPS = (("w_in",), ("w_q_b", "w_kv_b"), ("w_o",), ("w_up", "w_conv", "w_down"))
FORWARD_STAGES = ((0, 1), (2, 3))
SCATTER_GROUPS = (("w_down", "w_conv", "w_up"), ("w_o",), ("w_q_b", "w_kv_b", "w_in"))
OUT_WEIGHTS = ("w_ada", "b_ada", "g_mix_norm", "w_in", "g_q_lat", "w_q_b", "g_kv_lat", "w_kv_b", "g_mla_q_nope", "g_mla_q_pe",
               "g_mla_k_nope", "g_mla_k_pe", "g_dil_q", "g_dil_k", "w_o", "g_ffn_norm", "w_up", "w_conv", "b_conv", "w_down")


def kernel(x, c, positions, w_ada, b_ada, g_mix_norm, w_in, g_q_lat, w_q_b, g_kv_lat, w_kv_b, g_mla_q_nope, g_mla_q_pe, g_mla_k_nope, g_mla_k_pe, g_dil_q, g_dil_k, w_o, g_ffn_norm, w_up, w_conv, b_conv, w_down, loss_target, m_w_ada, m_b_ada, m_g_mix_norm, m_w_in, m_g_q_lat, m_w_q_b, m_g_kv_lat, m_w_kv_b, m_g_mla_q_nope, m_g_mla_q_pe, m_g_mla_k_nope, m_g_mla_k_pe, m_g_dil_q, m_g_dil_k, m_w_o, m_g_ffn_norm, m_w_up, m_w_conv, m_b_conv, m_w_down, v_w_ada, v_b_ada, v_g_mix_norm, v_w_in, v_g_q_lat, v_w_q_b, v_g_kv_lat, v_w_kv_b, v_g_mla_q_nope, v_g_mla_q_pe, v_g_mla_k_nope, v_g_mla_k_pe, v_g_dil_q, v_g_dil_k, v_w_o, v_g_ffn_norm, v_w_up, v_w_conv, v_b_conv, v_w_down):
    args = dict(locals())
    xi, yi, ci = _place()
    me = 4 * xi + 2 * yi + ci
    def local(prefix, n):
        a = args[prefix + n]
        if n in ROWS_APART:
            return jnp.transpose(a, (2, 0, 1) if n in TRANSPOSED else (1, 0, 2))
        return a[0].T if n in TRANSPOSED else a[0]

    def as_output(n, r):
        if n in ROWS_APART:
            return jnp.transpose(r, (1, 2, 0) if n in TRANSPOSED else (1, 0, 2))
        return (r.T if n in TRANSPOSED else r)[None]

    shard = {n: local("", n) for n in COL_SHARDED + ROW_SHARDED + ("w_ada",)}
    flat = lambda n, a: a.reshape(a.shape[0], a.shape[-1]) if n in ROWS_APART else a
    small_w = {n: args[n] for n in SMALL_PARAMS}

    sc_all, mod_all = ada_modulation("ada_mod", c, shard["w_ada"])

    payload = {n: flat(n, shard[n]) if n == "w_conv" else flat(n, shard[n]).astype(MXU_DTYPE) for n in COL_SHARDED + ROW_SHARDED}
    gather_order = [n for grp in GATHER_GROUPS for n in grp]
    gathered = [exchange_start("gather_start", [payload[n] for n in gather_order], gather=True, after=mod_all, tree=True)]
    after_start = gathered[0][-1]
    full, forwarded = {}, set()

    def forward(stage, after):
        if stage not in forwarded:
            forwarded.add(stage)
            first = sum(len(g) for g in GATHER_GROUPS[:FORWARD_STAGES[stage][0]])
            count = sum(len(GATHER_GROUPS[i]) for i in FORWARD_STAGES[stage])
            gathered[0] = exchange_forward(f"gather_forward{stage}", gathered[0], after, first, count)
        return gathered[0][-1]

    def fetch(name, after):
        if name not in full:
            (i, grp), = [(i, grp) for i, grp in enumerate(GATHER_GROUPS) if name in grp]
            forward([s for s, groups in enumerate(FORWARD_STAGES) if i in groups][0], after)
            srcs, lands = exchange_wait(f"gather{i}_wait", gathered[0], True, after, gather_order.index(grp[0]), len(grp), tree=True)
            for n, land in zip(grp, lands, strict=True):
                full[n] = to_kernel_layout(n, _gather_cols(land) if n in COL_SHARDED else _gather_rows(land))
        return full[name]

    mod_row = lax.dynamic_index_in_dim(mod_all, me, axis=1, keepdims=False).reshape(1, 6 * D_MODEL)
    (mod,) = rowwise("ada_bias", lambda rows, params: ([rows[0] + rows[1]], []), [mod_row, b_ada], [], [(6 * D_MODEL, F32)],
                     dep=after_start)

    own, pending, scatters = {}, {}, {}

    def emit(name, grad):
        grad = from_kernel_layout(name, grad)
        parts = _scatter_cols(grad) if name in COL_SHARDED else _scatter_rows(grad)
        own[name] = lax.dynamic_index_in_dim(parts, me, 0, keepdims=False)
        pending[name] = parts
        for i, grp in enumerate(SCATTER_GROUPS):
            if name == grp[-1]:
                scatters[i] = exchange_start(f"scatter{i}_start", [pending[n] for n in grp], gather=False)
                return scatters[i][-1]
        return None

    pos = positions.reshape(SEQ, 1).astype(F32)
    grad_x, small = _local_step(x[0], pos, mod, loss_target[0], small_w, fetch, emit, halfway=lambda after: forward(1, after))

    res, done = {}, grad_x
    for i, grp in enumerate(SCATTER_GROUPS):
        _, lands = exchange_wait(f"scatter{i}_wait", scatters[i], False, done)
        for n, land in zip(grp, lands, strict=True):
            res[n] = adamw(f"adamw_{n}", shard[n], [own[n], land], local("m_", n), local("v_", n), ADAM_TILE.get(n))
            done = res[n][0]
            res[n] = [as_output(n, r) for r in res[n]]
    (small_all,) = all_gather("gather_small", [_pack_small(small)], after=done)
    loss, small_res = adamw_small("adamw_small", small_all, {n: (args[n], args["m_" + n], args["v_" + n]) for n in SMALL_PARAMS})
    row, _, n_mod = SMALL_AT["b_ada"]
    dmod_all = small_all[:, row:row + n_mod // SMALL_COLS, :].reshape(N_DEV, n_mod)
    dmod_mine = lax.dynamic_slice_in_dim(dmod_all, me * (6 * D_MODEL // N_DEV), 6 * D_MODEL // N_DEV, axis=1)
    g_w_ada = matmul("ada_wgrad", sc_all, dmod_mine, "tn")
    res["w_ada"] = [r[None] for r in adamw("adamw_w_ada", shard["w_ada"], [g_w_ada], m_w_ada[0], v_w_ada[0], ADAM_TILE["w_ada"])]

    def leaf(kind, n):
        return res[n][kind] if n in res else small_res[n][kind]

    return (loss.reshape(()), grad_x[None], *[leaf(k, n) for k in range(4) for n in OUT_WEIGHTS])
```

```python
import jax
import jax.numpy as jnp
from jax import lax
from jax.experimental import pallas as pl
from jax.experimental.pallas import tpu as pltpu

F32 = jnp.float32
MXU_DTYPE = jnp.bfloat16

N_DEV = 8
D_MODEL = 1024
SEQ = 2048
HEADS = 8
NOPE = 64
ROPE = 32
Q_LORA = 512
KV_LORA = 256
DIL_DIM = 64
DIL_WIDTH = HEADS * DIL_DIM
DILATIONS = (1, 4, 16)
SPAN = 128
D_FF = 2816
LANES = 128
SUBLANES = 8
ROPE_THETA = 10000.0
EPS = 1e-6
NEG_INF = -1e30
ADAM_LR, ADAM_B1, ADAM_B2, ADAM_EPS, ADAM_WD, ADAM_STEP = 0.001, 0.9, 0.999, 1e-08, 0.01, 10
VMEM_LIMIT = 56 * 1024 * 1024
MESH_ID = pl.DeviceIdType.MESH

P_QLAT, P_KVLAT, P_KPE, P_QD, P_KD, P_VD, P_END = 0, 512, 768, 896, 1408, 1920, 2432
KPE_LO = 64
MIX_IN = HEADS * LANES + DIL_WIDTH


def _params(**kw):
    return pltpu.CompilerParams(vmem_limit_bytes=VMEM_LIMIT, **kw)


def rowwise(name, fn, rows, params, out_rows, out_accs=(), tm=512, dep=None):
    deps = [] if dep is None else [dep]
    rows = [r if isinstance(r, tuple) else (r, r.shape[1], 0) for r in rows]
    R = rows[0][0].shape[0]
    tm = min(tm, R)
    steps = R // tm
    assert steps * tm == R
    in_specs = []
    for a, width, cb in rows:
        ri = a.shape[0]
        per = ri // tm
        assert per * tm == ri
        if ri == R:
            in_specs.append(pl.BlockSpec((tm, width), lambda i, cb=cb: (i, cb)))
        else:
            in_specs.append(pl.BlockSpec((tm, width), lambda i, per=per, cb=cb: (i % per, cb)))
    for p in params:
        in_specs.append(pl.BlockSpec(p.shape, lambda i: (0,) * p.ndim))
    in_specs += [pl.BlockSpec(memory_space=pl.ANY)] * len(deps)
    out_shape = [jax.ShapeDtypeStruct((R, d), dt) for d, dt in out_rows]
    out_specs = [pl.BlockSpec((tm, d), lambda i: (i, 0)) for d, _ in out_rows]
    out_shape += [jax.ShapeDtypeStruct((1, n), F32) for n in out_accs]
    out_specs += [pl.BlockSpec((1, n), lambda i: (0, 0)) for n in out_accs]
    nr, npar, no, na = len(rows), len(params), len(out_rows), len(out_accs)

    def body(*refs):
        rvals = [r[...] for r in refs[:nr]]
        pvals = [r[...] for r in refs[nr:nr + npar]]
        outs, accs = fn(rvals, pvals)
        first_out = nr + npar + len(deps)
        for ref, v in zip(refs[first_out:first_out + no], outs, strict=True):
            ref[...] = v.astype(ref.dtype)
        if na:
            acc_refs = refs[first_out + no:]
            i = pl.program_id(0)

            @pl.when(i == 0)
            def _():
                for ref, v in zip(acc_refs, accs, strict=True):
                    ref[...] = v

            @pl.when(i > 0)
            def _():
                for ref, v in zip(acc_refs, accs, strict=True):
                    ref[...] += v

    res = pl.pallas_call(body, name=name, grid=(steps,), in_specs=in_specs, out_specs=out_specs,
                         out_shape=out_shape, compiler_params=_params())(*[r[0] for r in rows], *params, *deps)
    return list(res)


_DIMS = {"nn": ((1,), (0,)), "nt": ((1,), (1,)), "tn": ((0,), (0,))}


def _dot(a, b, mode="nn"):
    return lax.dot_general(a.astype(MXU_DTYPE), b.astype(MXU_DTYPE), (_DIMS[mode], ((), ())),
                           preferred_element_type=F32)


def matmul(name, a, b, mode, tm=None, tn=None, tk=None, out_dtype=F32, dep=None):
    if mode == "tn":
        K, M = a.shape
    else:
        M, K = a.shape
    N = b.shape[0] if mode == "nt" else b.shape[1]
    tm, tn, tk = tm or M, tn or N, tk or K
    nm, nn, nk = M // tm, N // tn, K // tk
    assert nm * tm == M and nn * tn == N and nk * tk == K
    a_spec = pl.BlockSpec((tk, tm), lambda i, j, k: (k, i)) if mode == "tn" else pl.BlockSpec((tm, tk), lambda i, j, k: (i, k))
    b_spec = pl.BlockSpec((tn, tk), lambda i, j, k: (j, k)) if mode == "nt" else pl.BlockSpec((tk, tn), lambda i, j, k: (k, j))
    deps = [] if dep is None else [dep]

    def body(a_ref, b_ref, *rest):
        o_ref, scratch = rest[len(deps)], rest[len(deps) + 1:]
        p = _dot(a_ref[...], b_ref[...], mode)
        if nk == 1:
            o_ref[...] = p.astype(o_ref.dtype)
        else:
            acc = scratch[0]
            k = pl.program_id(2)

            @pl.when(k == 0)
            def _():
                acc[...] = p

            @pl.when(k > 0)
            def _():
                acc[...] += p

            @pl.when(k == nk - 1)
            def _():
                o_ref[...] = acc[...].astype(o_ref.dtype)

    return pl.pallas_call(
        body, name=name, grid=(nm, nn, nk), in_specs=[a_spec, b_spec] + [pl.BlockSpec(memory_space=pl.ANY)] * len(deps),
        out_specs=pl.BlockSpec((tm, tn), lambda i, j, k: (i, j)),
        out_shape=jax.ShapeDtypeStruct((M, N), out_dtype),
        scratch_shapes=[pltpu.VMEM((tm, tn), F32)] if nk > 1 else [],
        compiler_params=_params())(a, b, *deps)


def _rms(x, g):
    rstd = lax.rsqrt(jnp.mean(x * x, axis=-1, keepdims=True) + EPS)
    n = x * rstd
    return n * g, n, rstd


def _rms_bwd(dy, n, rstd, g):
    dg = jnp.sum(dy * n, axis=0, keepdims=True)
    dn = dy * g
    dx = rstd * (dn - n * jnp.mean(dn * n, axis=-1, keepdims=True))
    return dx, dg


def _norm_bwd(dy, x, g):
    _, n, rstd = _rms(x, g)
    return _rms_bwd(dy, n, rstd, g)


def _colsum(v):
    return jnp.sum(v, axis=0, keepdims=True)


def _silu(x):
    return x * (1.0 / (1.0 + jnp.exp(-x)))


def _lane(shape):
    return lax.broadcasted_iota(jnp.int32, shape, 1)


def _group_mean(v, groups):
    i = lax.broadcasted_iota(jnp.int32, (LANES, LANES), 0)
    j = lax.broadcasted_iota(jnp.int32, (LANES, LANES), 1)
    g = jnp.zeros((LANES, LANES), F32)
    for lo, hi in groups:
        g = jnp.where((i >= lo) & (i < hi) & (j >= lo) & (j < hi), 1.0 / (hi - lo), g)
    head = v.astype(MXU_DTYPE)
    return _dot(head, g) + _dot(v - head.astype(F32), g)


def _in_groups(shape, groups):
    lane = _lane(shape)
    m = jnp.zeros(shape, jnp.bool_)
    for lo, hi in groups:
        m = m | ((lane >= lo) & (lane < hi))
    return m


def _grms(x, g, groups):
    rstd = lax.rsqrt(_group_mean(x * x, groups) + EPS)
    n = jnp.where(_in_groups(x.shape, groups), x * rstd, 0.0)
    return n * g, n, rstd


def _grms_bwd(dy, n, rstd, g, groups):
    dn = dy * g
    return rstd * (dn - n * _group_mean(dn * n, groups)), _colsum(dy * n)


def _rot(x, half, transpose=False):
    first = (_lane(x.shape) % (2 * half)) < half
    up = pltpu.roll(x, LANES - half, axis=1)
    down = pltpu.roll(x, half, axis=1)
    return jnp.where(first, up, -down) if transpose else jnp.where(first, -up, down)


def _rope(x, cos, sin, half):
    return x * cos + _rot(x, half) * sin


def _rope_bwd(dy, cos, sin, half):
    return dy * cos + _rot(dy * sin, half, transpose=True)


def _chunks(x):
    return [x[:, i:i + LANES] for i in range(0, x.shape[1], LANES)]


Q_GROUPS = ((0, NOPE), (NOPE, NOPE + ROPE))
K_GROUPS = ((0, NOPE),)
KPE_GROUPS = ((KPE_LO, KPE_LO + ROPE),)
DIL_GROUPS = ((0, DIL_DIM), (DIL_DIM, 2 * DIL_DIM))


def _col(width, rows=SEQ):
    return pl.BlockSpec((rows, width), lambda h: (0, h))


def _causal_tail(s, tq, fill):
    diag = s[:, s.shape[1] - tq:]
    keep = lax.broadcasted_iota(jnp.int32, diag.shape, 1) <= lax.broadcasted_iota(jnp.int32, diag.shape, 0)
    diag = jnp.where(keep, diag, fill)
    return diag if s.shape[1] == tq else jnp.concatenate([s[:, :s.shape[1] - tq], diag], axis=1)


def mla_fwd(name, q, k, v, scale, tq=256):
    S = q.shape[0]

    def body(q_ref, k_ref, v_ref, o_ref, lse_ref):
        nb = S // tq
        blk = lambda i: slice(i * tq, (i + 1) * tq)

        def scores(i):
            return _dot(q_ref[blk(i), :], k_ref[:(i + 1) * tq, :], "nt")

        def softmax(i, s):
            s = _causal_tail(s * scale, tq, NEG_INF)
            m = jnp.max(s, axis=-1, keepdims=True)
            e = jnp.exp(s - m)
            l = jnp.sum(e, axis=-1, keepdims=True)
            lse_ref[0, blk(i), :] = m + jnp.log(l)
            return (e * (1.0 / l)).astype(MXU_DTYPE)

        def weighted(i, p):
            o_ref[blk(i), :] = _dot(p, v_ref[:(i + 1) * tq, :])

        s, p_prev = scores(0), None
        for i in range(nb):
            s_next = scores(i + 1) if i + 1 < nb else None
            if p_prev is not None:
                weighted(i - 1, p_prev)
            p_prev, s = softmax(i, s), s_next
        weighted(nb - 1, p_prev)

    return pl.pallas_call(
        body, name=name, grid=(HEADS,), in_specs=[_col(LANES)] * 3,
        out_specs=[_col(LANES), pl.BlockSpec((1, S, 1), lambda h: (h, 0, 0))],
        out_shape=[jax.ShapeDtypeStruct((S, MIX_IN), F32), jax.ShapeDtypeStruct((HEADS, S, 1), F32)],
        compiler_params=_params())(q, k, v)


def mla_bwd(name, q, k, v, o, do, lse, scale, tq=256):
    S = q.shape[0]

    def body(q_ref, k_ref, v_ref, o_ref, do_ref, lse_ref, dq_ref, dkv_ref, dkpe_ref, dk_acc, dv_acc):
        dk_acc[...] = jnp.zeros_like(dk_acc)
        dv_acc[...] = jnp.zeros_like(dv_acc)
        for i in range(S // tq):
            kext = (i + 1) * tq
            blk = slice(i * tq, kext)
            qi, kk, vv = q_ref[blk, :], k_ref[:kext, :], v_ref[:kext, :]
            doi = do_ref[blk, :]
            s = _causal_tail(_dot(qi, kk, "nt") * scale, tq, NEG_INF)
            p = jnp.exp(s - lse_ref[0, blk, :])
            dp = _dot(doi, vv, "nt")
            delta = jnp.sum(doi * o_ref[blk, :], axis=-1, keepdims=True)
            ds = p * (dp - delta) * scale
            dq_ref[blk, :] = _dot(ds, kk)
            dk_acc[:kext, :] += _dot(ds, qi, "tn")
            dv_acc[:kext, :] += _dot(p, doi, "tn")
        dk = dk_acc[...]
        lane = _lane(dk.shape)
        dkv_ref[...] = jnp.where(lane < NOPE, dk, 0.0) + dv_acc[...]
        dkpe = jnp.where((lane >= KPE_LO) & (lane < KPE_LO + ROPE), dk, 0.0)
        h = pl.program_id(0)

        @pl.when(h == 0)
        def _():
            dkpe_ref[...] = dkpe

        @pl.when(h > 0)
        def _():
            dkpe_ref[...] += dkpe

    return pl.pallas_call(
        body, name=name, grid=(HEADS,),
        in_specs=[_col(LANES)] * 5 + [pl.BlockSpec((1, S, 1), lambda h: (h, 0, 0))],
        out_specs=[_col(LANES), _col(LANES), pl.BlockSpec((S, LANES), lambda h: (0, 0))],
        out_shape=[jax.ShapeDtypeStruct((S, HEADS * LANES), F32), jax.ShapeDtypeStruct((S, HEADS * LANES), F32),
                   jax.ShapeDtypeStruct((S, LANES), F32)],
        scratch_shapes=[pltpu.VMEM((S, LANES), F32), pltpu.VMEM((S, LANES), F32)],
        compiler_params=_params())(q, k, v, o, do, lse)


BAND_TQ = SPAN


def _band_blocks(L, tq):
    return [(i * tq, (i + 1) * tq, max(0, i * tq - SPAN)) for i in range(L // tq)]


def _class_rows(r, dil, lo, hi):
    return pl.ds(r + dil * lo, hi - lo, stride=dil) if dil > 1 else pl.ds(lo, hi - lo)


def _stack_heads(t, lo):
    zero = jnp.zeros_like(t)
    return jnp.concatenate([jnp.where(lo, t, zero), jnp.where(lo, zero, t)], axis=0)


def _band_mask2(q0, q1, k0):
    n = q1 - q0
    shape = (2 * n, q1 - k0)
    i = lax.broadcasted_iota(jnp.int32, shape, 0)
    dist = (jnp.where(i >= n, i - n, i) + q0) - (lax.broadcasted_iota(jnp.int32, shape, 1) + k0)
    return (dist >= 0) & (dist <= SPAN)


def _pair_col(col0=0):
    return pl.BlockSpec((SEQ, LANES), lambda j: (0, col0 // LANES + j))


def band_fwd(name, q, k, v, dil, dep=None):
    S = q.shape[0]
    L = S // dil
    tq = BAND_TQ
    scale = DIL_DIM ** -0.5
    deps = [] if dep is None else [dep]

    def body(q_ref, k_ref, v_ref, *rest):
        o_ref, lse_ref = rest[len(deps):]
        items = [(r, blk) for r in range(dil) for blk in _band_blocks(L, tq)]
        lo = _lane((tq, LANES)) < DIL_DIM

        def scores(item):
            r, (q0, q1, k0) = item
            qb = q_ref[_class_rows(r, dil, q0, q1), :].astype(MXU_DTYPE)
            return _dot(_stack_heads(qb, lo), k_ref[_class_rows(r, dil, k0, q1), :], "nt")

        def softmax(item, s):
            _, (q0, q1, k0) = item
            s = jnp.where(_band_mask2(q0, q1, k0), s * scale, NEG_INF)
            mx = jnp.max(s, axis=-1, keepdims=True)
            e = jnp.exp(s - mx)
            l = jnp.sum(e, axis=-1, keepdims=True)
            return (e * (1.0 / l)).astype(MXU_DTYPE), mx + jnp.log(l)

        def weighted(item, p, lse):
            r, (q0, q1, k0) = item
            pv = _dot(p, v_ref[_class_rows(r, dil, k0, q1), :])
            o_ref[_class_rows(r, dil, q0, q1), :] = jnp.where(lo, pv[:tq], pv[tq:])
            lse_ref[_class_rows(r, dil, q0, q1), :] = jnp.where(lo, lse[:tq], lse[tq:])

        s, prev = scores(items[0]), None
        for i, item in enumerate(items):
            s_next = scores(items[i + 1]) if i + 1 < len(items) else None
            if prev is not None:
                weighted(items[i - 1], *prev)
            prev, s = softmax(item, s), s_next
        weighted(items[-1], *prev)

    return pl.pallas_call(
        body, name=name, grid=(DIL_WIDTH // LANES,),
        in_specs=[_pair_col()] * 2 + [_pair_col(P_VD)] + [pl.BlockSpec(memory_space=pl.ANY)] * len(deps), out_specs=[_pair_col()] * 2,
        out_shape=[jax.ShapeDtypeStruct((S, DIL_WIDTH), F32)] * 2, compiler_params=_params())(q, k, v, *deps)


def band_bwd(name, q, k, v, lse, lse_mix, o_cat, do_cat, dil, before=None):
    S = q.shape[0]
    L = S // dil
    tq = BAND_TQ
    scale = DIL_DIM ** -0.5
    before = list(before or [])

    def body(q_ref, k_ref, v_ref, lse_ref, mix_ref, o_ref, do_ref, *rest):
        dq_ref, dk_ref, dv_ref = rest[len(before):]
        if before:
            dq0_ref, dk0_ref, dv0_ref = rest[:3]
            dk_ref[...] = dk0_ref[...]
            dv_ref[...] = dv0_ref[...]
        else:
            dk_ref[...] = jnp.zeros_like(dk_ref)
            dv_ref[...] = jnp.zeros_like(dv_ref)
        items = [(r, blk) for r in range(dil) for blk in _band_blocks(L, tq)]
        lo = _lane((tq, LANES)) < DIL_DIM
        per_head = lambda t: jnp.concatenate([t[:, 0:1], t[:, DIL_DIM:DIL_DIM + 1]], axis=0)

        def scores(item):
            r, (q0, q1, k0) = item
            qrows, krows = _class_rows(r, dil, q0, q1), _class_rows(r, dil, k0, q1)
            lse_p, dout = lse_ref[qrows, :], do_ref[qrows, :]
            w2 = per_head(jnp.exp(lse_p - mix_ref[qrows, :]))
            dd = dout * o_ref[qrows, :]
            big_d = jnp.concatenate([jnp.sum(jnp.where(lo, dd, 0.0), axis=-1, keepdims=True),
                                     jnp.sum(jnp.where(lo, 0.0, dd), axis=-1, keepdims=True)], axis=0)
            q2 = _stack_heads(q_ref[qrows, :].astype(MXU_DTYPE), lo)
            dom = (_stack_heads(dout, lo) * w2).astype(MXU_DTYPE)
            return (_dot(q2, k_ref[krows, :], "nt"), _dot(dom, v_ref[krows, :], "nt"), per_head(lse_p), w2 * big_d, q2, dom)

        def softmax_bwd(item, s, dp, lse2, wd2, q2, dom):
            _, (q0, q1, k0) = item
            p = jnp.where(_band_mask2(q0, q1, k0), jnp.exp(s * scale - lse2), 0.0)
            return p.astype(MXU_DTYPE), (p * (dp - wd2) * scale).astype(MXU_DTYPE), q2, dom

        def grads(item, p, ds, q2, dom):
            r, (q0, q1, k0) = item
            qrows, krows = _class_rows(r, dil, q0, q1), _class_rows(r, dil, k0, q1)
            dq2 = _dot(ds, k_ref[krows, :])
            dq = jnp.where(lo, dq2[:tq], dq2[tq:])
            dq_ref[qrows, :] = dq + dq0_ref[qrows, :] if before else dq
            dk_ref[krows, :] += _dot(ds, q2, "tn")
            dv_ref[krows, :] += _dot(p, dom, "tn")

        sc, prev = scores(items[0]), None
        for i, item in enumerate(items):
            sc_next = scores(items[i + 1]) if i + 1 < len(items) else None
            if prev is not None:
                grads(items[i - 1], *prev)
            prev, sc = softmax_bwd(item, *sc), sc_next
        grads(items[-1], *prev)

    cat = _pair_col(HEADS * LANES)
    return pl.pallas_call(
        body, name=name, grid=(DIL_WIDTH // LANES,),
        in_specs=[_pair_col()] * 2 + [_pair_col(P_VD)] + [_pair_col()] * 2 + [cat] * 2 + [_pair_col()] * len(before),
        out_specs=[_pair_col()] * 3, out_shape=[jax.ShapeDtypeStruct((S, DIL_WIDTH), F32)] * 3,
        compiler_params=_params())(q, k, v, lse, lse_mix, o_cat, do_cat, *before)


def combine_fwd(name, outs, lses, o_cat, tm=512):
    S = outs[0].shape[0]

    def body(o1, o2, o3, l1, l2, l3, cat_in, cat_out, mix_ref):
        ls = [l1[...], l2[...], l3[...]]
        m = jnp.maximum(jnp.maximum(ls[0], ls[1]), ls[2])
        e = [jnp.exp(l - m) for l in ls]
        den = e[0] + e[1] + e[2]
        cat_out[...] = (e[0] / den) * o1[...] + (e[1] / den) * o2[...] + (e[2] / den) * o3[...]
        mix_ref[...] = m + jnp.log(den)

    row = pl.BlockSpec((tm, DIL_WIDTH), lambda i: (i, 0))
    return pl.pallas_call(
        body, name=name, grid=(S // tm,), in_specs=[row] * 6 + [pl.BlockSpec(memory_space=pl.ANY)],
        out_specs=[pl.BlockSpec((tm, DIL_WIDTH), lambda i: (i, HEADS * LANES // DIL_WIDTH)), row],
        out_shape=[jax.ShapeDtypeStruct(o_cat.shape, F32), jax.ShapeDtypeStruct((S, DIL_WIDTH), F32)],
        input_output_aliases={6: 0}, compiler_params=_params())(*outs, *lses, o_cat)


FFN_FWD_ROWS = 512
FFN_BWD_ROWS = 256
CONV_PAD = SUBLANES


def _window(x, k):
    groups = x.reshape(-1, SUBLANES, x.shape[1])
    turned = pltpu.roll(groups, SUBLANES - k, axis=1)
    stays = lax.broadcasted_iota(jnp.int32, (groups.shape[0] - 1,) + groups.shape[1:], 1) < SUBLANES - k
    return jnp.where(stays, turned[:-1], turned[1:]).reshape(-1, x.shape[1])


def _earlier(ref, r0, rows, n):
    if r0 == 0:
        x = jnp.concatenate([jnp.zeros((SUBLANES, ref.shape[1]), F32), ref[:rows, :]], axis=0)
    else:
        x = ref[r0 - SUBLANES:r0 + rows, :]
    return _window(x, SUBLANES - n)


CONV_TC = 256
CONV_NB = D_FF // CONV_TC


def _half_specs(rows, rows_axis=False):
    if rows_axis:
        return [pl.BlockSpec((rows, D_MODEL), lambda j: (j, 0)), pl.BlockSpec((rows, D_MODEL), lambda j: (j + CONV_NB, 0))]
    return [pl.BlockSpec((rows, CONV_TC), lambda j: (0, j)), pl.BlockSpec((rows, CONV_TC), lambda j: (0, j + CONV_NB))]


def _whole(a):
    return pl.BlockSpec(a.shape, lambda j: (0,) * a.ndim)


def _up_pair(h, ug_ref, uv_ref):
    return jnp.concatenate([_dot(h, ug_ref[...], "nt"), _dot(h, uv_ref[...], "nt")], axis=1)


def _conv_taps(up_ref, r0, rows, w, b):
    uin, u1, u2 = up_ref[r0:r0 + rows, :], _earlier(up_ref, r0, rows, 1), _earlier(up_ref, r0, rows, 2)
    return uin, u1, u2, w[2:3, :] * uin + w[1:2, :] * u1 + w[0:1, :] * u2 + b


def ffn_fwd(name, h, w_up_t, w_conv, b_conv, w_down):
    S = h.shape[0]

    def body(h_ref, ug_ref, uv_ref, wg_ref, wv_ref, bg_ref, bv_ref, wd_ref, dn_ref, up_ref):
        @pl.when(pl.program_id(0) == 0)
        def _():
            dn_ref[...] = jnp.zeros_like(dn_ref)

        w = jnp.concatenate([wg_ref[...], wv_ref[...]], axis=1)
        b = jnp.concatenate([bg_ref[...], bv_ref[...]], axis=1)
        rows = FFN_FWD_ROWS
        starts = list(range(0, S, rows))

        def project(r0):
            up_ref[r0:r0 + rows, :] = _up_pair(h_ref[r0:r0 + rows, :], ug_ref, uv_ref)

        def gate(r0):
            u = _conv_taps(up_ref, r0, rows, w, b)[3]
            return (_silu(u[:, :CONV_TC]) * u[:, CONV_TC:]).astype(MXU_DTYPE)

        def project_down(r0, act):
            dn_ref[r0:r0 + rows, :] += _dot(act, wd_ref[...])

        project(starts[0])
        act_prev = None
        for i, r0 in enumerate(starts):
            if i + 1 < len(starts):
                project(starts[i + 1])
            if act_prev is not None:
                project_down(starts[i - 1], act_prev)
            act_prev = gate(r0)
        project_down(starts[-1], act_prev)

    return pl.pallas_call(
        body, name=name, grid=(CONV_NB,),
        in_specs=[_whole(h)] + _half_specs(CONV_TC, rows_axis=True) + _half_specs(3) + _half_specs(1)
        + [pl.BlockSpec((CONV_TC, w_down.shape[1]), lambda j: (j, 0))],
        out_specs=[pl.BlockSpec((S, w_down.shape[1]), lambda j: (0, 0)), pl.BlockSpec((S, 2 * CONV_TC), lambda j: (0, j))],
        out_shape=[jax.ShapeDtypeStruct((S, w_down.shape[1]), F32), jax.ShapeDtypeStruct((S, 2 * D_FF), F32)],
        compiler_params=_params())(h, w_up_t, w_up_t, w_conv, w_conv, b_conv, b_conv, w_down)


def ffn_bwd(name, h, up, w_up_t, w_conv, b_conv, d_dn, w_down):
    S, D = h.shape

    def body(h_ref, up_ref, ug_ref, uv_ref, wg_ref, wv_ref, bg_ref, bv_ref, dd_ref, wd_ref,
             dh_ref, gup_ref, gd_ref, dwg_ref, dwv_ref, dbg_ref, dbv_ref, du_ref, dup_ref, act_ref):
        @pl.when(pl.program_id(0) == 0)
        def _():
            dh_ref[...] = jnp.zeros_like(dh_ref)

        w = jnp.concatenate([wg_ref[...], wv_ref[...]], axis=1)
        b = jnp.concatenate([bg_ref[...], bv_ref[...]], axis=1)
        w_pair = jnp.concatenate([ug_ref[...], uv_ref[...]], axis=0)
        rows = FFN_BWD_ROWS
        starts = list(range(0, S, rows))
        du_ref[S:S + CONV_PAD, :] = jnp.zeros((CONV_PAD, 2 * CONV_TC), F32)

        def project(r0):
            return _dot(dd_ref[r0:r0 + rows, :], wd_ref[...], "nt")

        def through_conv(r0, da):
            uin, u1, u2, u = _conv_taps(up_ref, r0, rows, w, b)
            gate, val = u[:, :CONV_TC], u[:, CONV_TC:]
            sig = 1.0 / (1.0 + jnp.exp(-gate))
            du = jnp.concatenate([da * val * (sig * (1.0 + gate * (1.0 - sig))), da * (gate * sig)], axis=1)
            du_ref[r0:r0 + rows, :] = du
            act_ref[r0:r0 + rows, :] = (gate * sig * val).astype(MXU_DTYPE)
            dw = jnp.concatenate([_colsum(du * u2), _colsum(du * u1), _colsum(du * uin)], axis=0)
            return dw, _colsum(du)

        def back_up(r0):
            du = du_ref[r0:r0 + rows + CONV_PAD, :]
            dup = (w[2:3, :] * du[:rows] + w[1:2, :] * _window(du, 1) + w[0:1, :] * _window(du, 2)).astype(MXU_DTYPE)
            dup_ref[r0:r0 + rows, :] = dup
            dh_ref[r0:r0 + rows, :] += _dot(dup, w_pair)

        dw, db = 0.0, 0.0
        da = project(starts[0])
        for i, r0 in enumerate(starts):
            da_next = project(starts[i + 1]) if i + 1 < len(starts) else None
            dw_c, db_c = through_conv(r0, da)
            if i > 0:
                back_up(starts[i - 1])
            dw, db, da = dw + dw_c, db + db_c, da_next
        back_up(starts[-1])
        g_up, g_dn = _dot(dup_ref[...], h_ref[...], "tn"), _dot(act_ref[...], dd_ref[...], "tn")
        gup_ref[0], gup_ref[1] = g_up[:CONV_TC].astype(gup_ref.dtype), g_up[CONV_TC:].astype(gup_ref.dtype)
        gd_ref[...] = g_dn.astype(gd_ref.dtype)
        dwg_ref[...], dwv_ref[...] = dw[:, :CONV_TC], dw[:, CONV_TC:]
        dbg_ref[...], dbv_ref[...] = db[:, :CONV_TC], db[:, CONV_TC:]

    half = lambda rows: pl.BlockSpec((rows, CONV_TC), lambda j: (0, j))
    rows_blk = pl.BlockSpec((CONV_TC, D), lambda j: (j, 0))
    dh, gup, gd, dwg, dwv, dbg, dbv = pl.pallas_call(
        body, name=name, grid=(CONV_NB,),
        in_specs=[_whole(h), pl.BlockSpec((S, 2 * CONV_TC), lambda j: (0, j))] + _half_specs(CONV_TC, rows_axis=True) + _half_specs(3)
        + _half_specs(1) + [_whole(d_dn), rows_blk],
        out_specs=[pl.BlockSpec((S, D), lambda j: (0, 0)), pl.BlockSpec((2, CONV_TC, D), lambda j: (0, j, 0)), rows_blk,
                   half(3), half(3), half(1), half(1)],
        out_shape=[jax.ShapeDtypeStruct((S, D), F32), jax.ShapeDtypeStruct((2, D_FF, D), MXU_DTYPE),
                   jax.ShapeDtypeStruct((D_FF, D), MXU_DTYPE)]
        + [jax.ShapeDtypeStruct((3, D_FF), F32)] * 2 + [jax.ShapeDtypeStruct((1, D_FF), F32)] * 2,
        scratch_shapes=[pltpu.VMEM((S + CONV_PAD, 2 * CONV_TC), F32), pltpu.VMEM((S, 2 * CONV_TC), MXU_DTYPE),
                        pltpu.VMEM((S, CONV_TC), MXU_DTYPE)],
        compiler_params=_params())(h, up, w_up_t, w_up_t, w_conv, w_conv, b_conv, b_conv, d_dn, w_down)
    return dh, gup.reshape(2 * D_FF, D), gd, jnp.concatenate([dwg, dwv], axis=1), jnp.concatenate([dbg, dbv], axis=1)


def adamw(name, w, parts, m, v, tr=None):
    apart = w.ndim == 3
    R, C = w.shape[0], w.shape[-1]
    tr = tr or R
    assert R % tr == 0
    c1 = 1.0 - ADAM_B1 ** ADAM_STEP
    c2 = 1.0 - ADAM_B2 ** ADAM_STEP
    np_ = len(parts)

    def body(*refs):
        w_ref, m_ref, v_ref = refs[0], refs[1 + np_], refs[2 + np_]
        go_ref, d_ref, mo_ref, vo_ref = refs[3 + np_:]
        terms = []
        for part, ref in zip(parts, refs[1:1 + np_], strict=True):
            terms += [ref[...]] if part.ndim == 2 else [ref[p] for p in range(part.shape[0])]
        g = terms[0].astype(F32)
        for term in terms[1:]:
            g = g + term.astype(F32)
        m2 = ADAM_B1 * m_ref[...] + (1.0 - ADAM_B1) * g
        v2 = ADAM_B2 * v_ref[...] + (1.0 - ADAM_B2) * (g * g)
        go_ref[...] = g
        mo_ref[...] = m2
        vo_ref[...] = v2
        d_ref[...] = -ADAM_LR * ((m2 / c1) / (jnp.sqrt(v2 / c2) + ADAM_EPS) + ADAM_WD * w_ref[...])

    blk = pl.BlockSpec((tr, C), lambda i: (i, 0))
    own = pl.BlockSpec((tr, None, C), lambda i: (i, 0, 0)) if apart else blk
    part_specs = [blk if p.ndim == 2 else pl.BlockSpec((p.shape[0], tr, C), lambda i: (0, i, 0)) for p in parts]
    return pl.pallas_call(
        body, name=name, grid=(R // tr,),
        in_specs=[own] + part_specs + [own, own], out_specs=[own] * 4,
        out_shape=[jax.ShapeDtypeStruct(w.shape, F32)] * 4, compiler_params=_params())(w, *parts, m, v)


def _place():
    return lax.axis_index("x"), lax.axis_index("y"), lax.axis_index("c")


def all_gather(name, arrs, after=None):
    n = len(arrs)
    deps = [] if after is None else [after]

    def body(*refs):
        ins, outs = refs[:n], refs[n + len(deps):2 * n + len(deps)]
        send_sems, recv_sems, local_sems = refs[2 * n + len(deps):]
        x, y, c = _place()
        me, sibling = (x, y, c), (x, y, 1 - c)
        chips = [(1 - x, y), (x, 1 - y), (1 - x, 1 - y)]
        sends = []
        for t in range(n):
            out = outs[t]

            def slot(px, py, pc, out=out):
                return out.at[4 * px + 2 * py + pc]

            def copy(k, block, to, src=None, t=t, slot=slot):
                return pltpu.make_async_remote_copy(
                    src_ref=slot(*block) if src is None else src, dst_ref=slot(*block),
                    send_sem=send_sems.at[7 * t + k], recv_sem=recv_sems.at[7 * t + k],
                    device_id=to, device_id_type=MESH_ID)

            mine = pltpu.make_async_copy(ins[t], slot(*me), local_sems.at[t])
            mine.start()
            first = [copy(0, me, sibling, src=ins[t])]
            first += [copy(1 + j, me, (*chip, c), src=ins[t]) for j, chip in enumerate(chips)]
            for cp in first:
                cp.start()
            sends.append((mine, first, copy))
        for t in range(n):
            mine, first, copy = sends[t]
            passed = [copy(4 + j, (*chip, c), sibling) for j, chip in enumerate(chips)]
            for j, chip in enumerate(chips):
                copy(1 + j, (*chip, c), me).wait_recv()
                passed[j].start()
            copy(0, sibling, me).wait_recv()
            for j, chip in enumerate(chips):
                copy(4 + j, (*chip, 1 - c), me).wait_recv()
            for cp in first + passed:
                cp.wait_send()
            mine.wait()

    any_spec = pl.BlockSpec(memory_space=pl.ANY)
    res = pl.pallas_call(
        body, name=name, in_specs=[any_spec] * (n + len(deps)), out_specs=[any_spec] * n,
        out_shape=[jax.ShapeDtypeStruct((N_DEV,) + a.shape, a.dtype) for a in arrs],
        scratch_shapes=[pltpu.SemaphoreType.DMA((7 * n,)), pltpu.SemaphoreType.DMA((7 * n,)), pltpu.SemaphoreType.DMA((n,))],
        compiler_params=pltpu.CompilerParams(has_side_effects=True))(*arrs, *deps)
    return list(res)


def ada_modulation(name, c, w_ada):
    n_mod = w_ada.shape[1]

    def exchange(src_ref, dst_ref, send_sems, recv_sems):
        x, y, c_ = _place()
        me = 4 * x + 2 * y + c_
        copies = []
        for k in range(1, N_DEV):
            px, py, pc = x ^ (k >> 2), y ^ ((k >> 1) & 1), c_ ^ (k & 1)
            copies.append(pltpu.make_async_remote_copy(
                src_ref=src_ref, dst_ref=dst_ref.at[me], send_sem=send_sems.at[k - 1], recv_sem=recv_sems.at[k - 1],
                device_id=(px, py, pc), device_id_type=MESH_ID))
        for cp in copies:
            cp.start()
        for cp in copies:
            cp.wait_recv()
        for cp in copies:
            cp.wait_send()
        return me

    def body(c_ref, w_ref, sc_ref, mod_ref, c_all, send_c, recv_c, send_m, recv_m):
        me = exchange(c_ref, c_all, send_c, recv_c)
        c_all[me] = c_ref[...]
        sc = _silu(jnp.concatenate([c_all[p] for p in range(N_DEV)], axis=0))
        sc_ref[...] = sc.astype(sc_ref.dtype)
        mod_ref[me] = _dot(sc, w_ref[...])
        exchange(mod_ref.at[me], mod_ref, send_m, recv_m)

    vmem = pl.BlockSpec(memory_space=pltpu.VMEM)
    return pl.pallas_call(
        body, name=name, in_specs=[vmem, vmem], out_specs=[vmem, vmem],
        out_shape=[jax.ShapeDtypeStruct((N_DEV, c.shape[1]), MXU_DTYPE), jax.ShapeDtypeStruct((N_DEV, N_DEV, n_mod), F32)],
        scratch_shapes=[pltpu.VMEM((N_DEV, 1, c.shape[1]), F32)] + [pltpu.SemaphoreType.DMA((N_DEV - 1,))] * 4,
        compiler_params=pltpu.CompilerParams(has_side_effects=True, vmem_limit_bytes=VMEM_LIMIT))(c, w_ada)


HBM_SPEC = pl.BlockSpec(memory_space=pltpu.HBM)
SEM_SPEC = pl.BlockSpec(memory_space=pltpu.SEMAPHORE)
DATAFLOW = pltpu.SideEffectType.DATAFLOW_SIDE_EFFECTING


def _exchange_copies(srcs, lands, send_sems, recv_sems, gather, first=0):
    x, y, c = _place()
    me = 4 * x + 2 * y + c
    out = []
    for t, (src, land) in enumerate(zip(srcs, lands, strict=True)):
        for k in range(1, N_DEV):
            px, py, pc = x ^ (k >> 2), y ^ ((k >> 1) & 1), c ^ (k & 1)
            sem = 7 * (first + t) + k - 1
            out.append((k, pltpu.make_async_remote_copy(
                src_ref=src if gather else src.at[4 * px + 2 * py + pc],
                dst_ref=land.at[me] if gather else land.at[k - 1],
                send_sem=send_sems.at[sem], recv_sem=recv_sems.at[sem],
                device_id=(px, py, pc), device_id_type=MESH_ID)))
    return out


TREE_DIRECT = (1, 2, 4, 6)
TREE_FORWARDED = (3, 5, 7)


def exchange_start(name, arrs, gather, after=None, tree=False):
    n = len(arrs)
    lands = [lax.empty(((N_DEV,) + a.shape) if gather else ((N_DEV - 1,) + a.shape[1:]), a.dtype) for a in arrs]
    deps = [] if after is None else [after]

    def body(*refs):
        srcs, land_refs = refs[:n], refs[n:2 * n]
        send_sems, recv_sems = refs[2 * n + len(deps)], refs[2 * n + len(deps) + 1]
        token = refs[-1]
        for k, cp in _exchange_copies(srcs, land_refs, send_sems, recv_sems, gather):
            if not tree or k in TREE_DIRECT:
                cp.start()
        token[...] = jnp.zeros_like(token)

    hbm = lambda a: pltpu.HBM(a.shape, a.dtype)
    res = pl.pallas_call(
        body, name=name,
        out_shape=(pltpu.SemaphoreType.DMA((7 * n,)), pltpu.SemaphoreType.DMA((7 * n,)), *[hbm(a) for a in arrs],
                   *[hbm(l) for l in lands], jax.ShapeDtypeStruct((8, 128), F32)),
        in_specs=[HBM_SPEC] * (2 * n) + [pl.BlockSpec(memory_space=pl.ANY)] * len(deps),
        out_specs=(SEM_SPEC, SEM_SPEC, *[HBM_SPEC] * (2 * n), pl.BlockSpec(memory_space=pltpu.VMEM)),
        input_output_aliases={i: 2 + i for i in range(2 * n)},
        compiler_params=pltpu.CompilerParams(has_side_effects=DATAFLOW),
    )(*[pltpu.with_memory_space_constraint(a, pltpu.HBM) for a in arrs + lands], *deps)
    return res[0], res[1], list(res[2:2 + n]), list(res[2 + n:2 + 2 * n]), res[-1]


def exchange_forward(name, started, after, first=0, count=None):
    send_sems, recv_sems, srcs, lands, _ = started
    count = len(srcs) - first if count is None else count
    mine = lands[first:first + count]
    n = len(mine)

    def copies(land_refs, send_ref, recv_ref):
        x, y, c = _place()
        out = []
        for t, land in enumerate(land_refs):
            for k in (2, 4, 6):
                slot = land.at[4 * (x ^ (k >> 2)) + 2 * (y ^ ((k >> 1) & 1)) + c]
                came, goes = 7 * (first + t) + k - 1, 7 * (first + t) + (k ^ 1) - 1
                out.append((
                    pltpu.make_async_remote_copy(src_ref=slot, dst_ref=slot, send_sem=send_ref.at[came], recv_sem=recv_ref.at[came],
                                                 device_id=(x, y, c), device_id_type=MESH_ID),
                    pltpu.make_async_remote_copy(src_ref=slot, dst_ref=slot, send_sem=send_ref.at[goes], recv_sem=recv_ref.at[goes],
                                                 device_id=(x, y, 1 - c), device_id_type=MESH_ID)))
        return out

    def arrived(*refs):
        for came, _ in copies(refs[:n], refs[n], refs[n + 1]):
            came.wait_recv()

    def pass_on(*refs):
        for _, goes in copies(refs[:n], refs[n], refs[n + 1]):
            goes.start()
        refs[-1][...] = jnp.zeros_like(refs[-1])

    hbm = lambda a: pltpu.HBM(a.shape, a.dtype)
    here = pl.pallas_call(
        arrived, name=name + "_arrived", out_shape=tuple(hbm(a) for a in mine),
        in_specs=[HBM_SPEC] * n + [SEM_SPEC, SEM_SPEC, pl.BlockSpec(memory_space=pl.ANY)],
        out_specs=tuple([HBM_SPEC] * n), input_output_aliases={i: i for i in range(n)},
        compiler_params=pltpu.CompilerParams(has_side_effects=DATAFLOW),
    )(*mine, send_sems, recv_sems, after)
    res = pl.pallas_call(
        pass_on, name=name, out_shape=(*[hbm(a) for a in mine], jax.ShapeDtypeStruct((8, 128), F32)),
        in_specs=[HBM_SPEC] * n + [SEM_SPEC, SEM_SPEC],
        out_specs=(*[HBM_SPEC] * n, pl.BlockSpec(memory_space=pltpu.VMEM)), input_output_aliases={i: i for i in range(n)},
        compiler_params=pltpu.CompilerParams(has_side_effects=DATAFLOW),
    )(*here, send_sems, recv_sems)
    lands = lands[:first] + list(res[:n]) + lands[first + count:]
    return (send_sems, recv_sems, srcs, lands, res[-1])


def exchange_wait(name, started, gather, after, first=0, count=None, tree=False):
    send_sems, recv_sems, srcs, lands, _ = started
    count = len(srcs) - first if count is None else count
    srcs, lands = srcs[first:first + count], lands[first:first + count]
    n = len(srcs)

    def body(*refs):
        src_refs, land_refs = refs[:n], refs[n:2 * n]
        copies = _exchange_copies(src_refs, land_refs, refs[2 * n], refs[2 * n + 1], gather, first)
        for _, cp in copies:
            cp.wait_send()
        for k, cp in copies:
            if not tree or k in (1,) + TREE_FORWARDED:
                cp.wait_recv()

    hbm = lambda a: pltpu.HBM(a.shape, a.dtype)
    res = pl.pallas_call(
        body, name=name, out_shape=tuple(hbm(a) for a in srcs + lands),
        in_specs=[HBM_SPEC] * (2 * n) + [SEM_SPEC, SEM_SPEC, pl.BlockSpec(memory_space=pl.ANY)],
        out_specs=tuple([HBM_SPEC] * (2 * n)), input_output_aliases={i: i for i in range(2 * n)},
        compiler_params=pltpu.CompilerParams(has_side_effects=DATAFLOW),
    )(*srcs, *lands, send_sems, recv_sems, after)
    return list(res[:n]), list(res[n:])


def _gather_cols(stack):
    p, k, n = stack.shape
    return stack.transpose(1, 0, 2).reshape(k, p * n)


def _scatter_cols(full):
    k, n = full.shape
    return full.reshape(k, N_DEV, n // N_DEV).transpose(1, 0, 2)


def _gather_rows(stack):
    p, r, n = stack.shape
    return stack.reshape(p * r, n)


def _scatter_rows(full):
    r, n = full.shape
    return full.reshape(N_DEV, r // N_DEV, n)


_IN_NAT = Q_LORA + KV_LORA
TRANSPOSED = ("w_in", "w_q_b", "w_up")
ROWS_APART = ("w_in", "w_conv")


def to_kernel_layout(name, w):
    if name == "w_in":
        z = lambda n: jnp.zeros((n, w.shape[1]), w.dtype)
        return jnp.concatenate([w[:_IN_NAT], z(KPE_LO), w[_IN_NAT:_IN_NAT + ROPE], z(LANES - KPE_LO - ROPE), w[_IN_NAT + ROPE:]], axis=0)
    if name == "w_q_b":
        return jnp.pad(w.reshape(HEADS, NOPE + ROPE, -1), ((0, 0), (0, LANES - NOPE - ROPE), (0, 0))).reshape(HEADS * LANES, -1)
    if name == "w_o":
        mla = jnp.pad(w[:HEADS * NOPE].reshape(HEADS, NOPE, -1), ((0, 0), (LANES - NOPE, 0), (0, 0))).reshape(HEADS * LANES, -1)
        return jnp.concatenate([mla, w[HEADS * NOPE:]], axis=0)
    return w


def from_kernel_layout(name, g):
    if name == "w_in":
        return jnp.concatenate([g[:_IN_NAT], g[P_KPE + KPE_LO:P_KPE + KPE_LO + ROPE], g[P_QD:]], axis=0)
    if name == "w_q_b":
        return g.reshape(HEADS, LANES, -1)[:, :NOPE + ROPE, :].reshape(HEADS * (NOPE + ROPE), -1)
    if name == "w_o":
        mla = g[:HEADS * LANES].reshape(HEADS, LANES, -1)[:, LANES - NOPE:, :].reshape(HEADS * NOPE, -1)
        return jnp.concatenate([mla, g[HEADS * LANES:]], axis=0)
    return g


SMALL_COLS = 1024
SMALL_ROWS = 24
SMALL_AT = {"loss": (0, 0, 1), "b_ada": (1, 0, 6 * D_MODEL), "g_mix_norm": (7, 0, D_MODEL), "g_q_lat": (8, 0, Q_LORA),
            "g_kv_lat": (9, 0, KV_LORA), "g_mla_q_nope": (10, 0, NOPE), "g_mla_q_pe": (10, 128, ROPE),
            "g_mla_k_nope": (10, 256, NOPE), "g_mla_k_pe": (10, 384, ROPE), "g_dil_q": (10, 512, DIL_DIM),
            "g_dil_k": (10, 640, DIL_DIM), "g_ffn_norm": (11, 0, D_MODEL), "b_conv": (12, 0, 2 * D_FF)}
SMALL_PARAMS = tuple(n for n in SMALL_AT if n != "loss")


def _pack_small(values):
    by_row = {}
    for name, (row, off, n) in SMALL_AT.items():
        by_row.setdefault(row, []).append((off, values[name].reshape(-1).astype(F32)))
    out = []
    for row in sorted(by_row):
        pieces, at = [], 0
        for off, v in sorted(by_row[row], key=lambda t: t[0]):
            pieces += [jnp.zeros((off - at,), F32), v]
            at = off + v.shape[0]
        flat = jnp.concatenate(pieces)
        nrows = -(-flat.shape[0] // SMALL_COLS)
        out.append(jnp.pad(flat, (0, nrows * SMALL_COLS - flat.shape[0])).reshape(nrows, SMALL_COLS))
    packed = jnp.concatenate(out, axis=0)
    return jnp.pad(packed, ((0, SMALL_ROWS - packed.shape[0]), (0, 0)))


def _adam(w, g, m, v):
    c1 = 1.0 - ADAM_B1 ** ADAM_STEP
    c2 = 1.0 - ADAM_B2 ** ADAM_STEP
    m2 = ADAM_B1 * m + (1.0 - ADAM_B1) * g
    v2 = ADAM_B2 * v + (1.0 - ADAM_B2) * (g * g)
    return -ADAM_LR * ((m2 / c1) / (jnp.sqrt(v2 / c2) + ADAM_EPS) + ADAM_WD * w), m2, v2


def adamw_small(name, stack, params):
    flat = [a for n in SMALL_PARAMS for a in params[n]]

    def body(stack_ref, *refs):
        ins, outs = refs[:len(flat)], refs[len(flat):]
        g_all = stack_ref[0]
        for p in range(1, N_DEV):
            g_all = g_all + stack_ref[p]
        outs[0][...] = g_all[0:1, 0:1]
        for i, pname in enumerate(SMALL_PARAMS):
            row, off, n = SMALL_AT[pname]
            w_ref, m_ref, v_ref = ins[3 * i:3 * i + 3]
            go_ref, d_ref, mo_ref, vo_ref = outs[1 + 4 * i:5 + 4 * i]
            for c0 in range(0, n, SMALL_COLS):
                cn = min(SMALL_COLS, n - c0)
                r = row + c0 // SMALL_COLS
                g = g_all[r:r + 1, off:off + cn]
                cols = (slice(None), slice(c0, c0 + cn))
                d, m2, v2 = _adam(w_ref[cols], g, m_ref[cols], v_ref[cols])
                go_ref[cols], d_ref[cols], mo_ref[cols], vo_ref[cols] = g, d, m2, v2

    whole = lambda a: pl.BlockSpec(a.shape, lambda: (0,) * a.ndim)
    out_shape = [jax.ShapeDtypeStruct((1, 1), F32)] + [jax.ShapeDtypeStruct(a.shape, F32) for n in SMALL_PARAMS for a in params[n][:1] * 4]
    res = pl.pallas_call(body, name=name, in_specs=[whole(stack)] + [whole(a) for a in flat],
                         out_specs=[pl.BlockSpec(s.shape, lambda s=s: (0,) * len(s.shape)) for s in out_shape],
                         out_shape=out_shape, compiler_params=_params())(stack, *flat)
    return res[0], {n: res[1 + 4 * i:5 + 4 * i] for i, n in enumerate(SMALL_PARAMS)}


def _local_step(x, pos, mod, target, w, fetch, emit, halfway=lambda after: None):
    S = SEQ
    sh1, sc1, g1, sh2, sc2, g2 = [mod[:, i * D_MODEL:(i + 1) * D_MODEL] for i in range(6)]
    zeros = lambda n: jnp.zeros((1, n), F32)
    g_q = jnp.concatenate([w["g_mla_q_nope"], w["g_mla_q_pe"], zeros(LANES - NOPE - ROPE)], axis=1)
    g_k = jnp.concatenate([w["g_mla_k_nope"], zeros(LANES - NOPE)], axis=1)
    g_kpe = jnp.concatenate([zeros(KPE_LO), w["g_mla_k_pe"], zeros(LANES - KPE_LO - ROPE)], axis=1)
    g_dq = jnp.concatenate([w["g_dil_q"]] * 2, axis=1)
    g_dk = jnp.concatenate([w["g_dil_k"]] * 2, axis=1)
    b_conv = w["b_conv"]

    def inv_freq(d):
        return jnp.power(ROPE_THETA, -2.0 * jnp.arange(d // 2, dtype=F32) / d)

    n_m, n_d = ROPE // 2, DIL_DIM // 2
    freqs = jnp.concatenate([inv_freq(ROPE), inv_freq(DIL_DIM), jnp.zeros((LANES - n_m - n_d,), F32)]).reshape(1, LANES)

    def tables_fn(rows, params):
        (p,), (f,) = rows, params
        c, s = jnp.cos(p * f), jnp.sin(p * f)
        one, zero = jnp.ones_like(c), jnp.zeros_like(c)
        mla = lambda t, fill: jnp.concatenate([fill[:, :KPE_LO], t[:, :n_m], t[:, :n_m], fill[:, :LANES - KPE_LO - ROPE]], axis=1)
        dil = lambda t: jnp.concatenate([t[:, n_m:n_m + n_d]] * 4, axis=1)
        return [mla(c, one), mla(s, zero), dil(c), dil(s)], []

    cos_m, sin_m, cos_d, sin_d = rowwise("rope_tables", tables_fn, [pos], [freqs], [(LANES, F32)] * 4)
    tables = [cos_m, sin_m, cos_d, sin_d]
    H_M, H_D = ROPE // 2, DIL_DIM // 2

    def ln1_fn(rows, params):
        (xv,), (g, sc, sh) = rows, params
        y, _, _ = _rms(xv, g)
        return [y * (1.0 + sc) + sh], []

    (h,) = rowwise("ln1_fwd", ln1_fn, [x], [w["g_mix_norm"], sc1, sh1], [(D_MODEL, MXU_DTYPE)])
    w_in = fetch("w_in", h)

    def proj_fn(rows, params):
        (hv, cm, sm, cd, sd), (w_t, gq, gkv, gkp, gdq, gdk) = rows, params
        pv = _dot(hv, w_t, "nt")
        kper = _rope(_grms(pv[:, P_KPE:P_QD], gkp, KPE_GROUPS)[0], cm, sm, H_M)
        qd = [_rope(_grms(c, gdq, DIL_GROUPS)[0], cd, sd, H_D) for c in _chunks(pv[:, P_QD:P_KD])]
        kd = [_rope(_grms(c, gdk, DIL_GROUPS)[0], cd, sd, H_D) for c in _chunks(pv[:, P_KD:P_VD])]
        return [pv, _rms(pv[:, P_QLAT:P_KVLAT], gq)[0], _rms(pv[:, P_KVLAT:P_KPE], gkv)[0], kper,
                jnp.concatenate(qd, axis=1), jnp.concatenate(kd, axis=1)], []

    post_params = [w["g_q_lat"], w["g_kv_lat"], g_kpe, g_dq, g_dk]
    proj, qln, kvn, kper, qd_r, kd_r = rowwise(
        "proj_fwd", proj_fn, [h] + tables, [w_in] + post_params,
        [(P_END, F32), (Q_LORA, MXU_DTYPE), (KV_LORA, MXU_DTYPE), (LANES, MXU_DTYPE)] + [(DIL_WIDTH, F32)] * 2, tm=256)
    w_q_b, w_kv_b = fetch("w_q_b", qln), fetch("w_kv_b", kvn)

    def mla_proj_fn(rows, params):
        (qlv, kvlv, kp, cm, sm), (wq_t, wkv, gq, gk) = rows, params
        qv, kvv = _dot(qlv, wq_t, "nt"), _dot(kvlv, wkv)
        value_lanes = _lane(kp.shape) >= NOPE
        qs, ks, vs = [], [], []
        for qc, kc in zip(_chunks(qv), _chunks(kvv), strict=True):
            qs.append(_rope(_grms(qc, gq, Q_GROUPS)[0], cm, sm, H_M))
            ks.append(_grms(kc, gk, K_GROUPS)[0] + kp)
            vs.append(jnp.where(value_lanes, kc, 0.0))
        return [qv, kvv] + [jnp.concatenate(t, axis=1) for t in (qs, ks, vs)], []

    q, kv, q_mla, k_mla, v_mla = rowwise(
        "mla_proj", mla_proj_fn, [qln, kvn, kper, cos_m, sin_m], [w_q_b, w_kv_b, g_q, g_k],
        [(HEADS * LANES, F32)] * 2 + [(HEADS * LANES, MXU_DTYPE)] * 3, tm=256)
    mla_scale = (NOPE + ROPE) ** -0.5
    o_cat, lse_mla = mla_fwd("mla_fwd", q_mla, k_mla, v_mla, mla_scale)
    passed = halfway(lse_mla)

    band = [band_fwd(f"band{dil}_fwd", qd_r, kd_r, proj, dil, dep=passed) for dil in DILATIONS]
    o_cat, lse_mix = combine_fwd("dil_combine", [b[0] for b in band], [b[1] for b in band], o_cat)
    w_o = fetch("w_o", o_cat)

    def mid_fn(rows, params):
        (ov, xv), (w_out, gate1, g, sc, sh) = rows, params
        mx = _dot(ov, w_out)
        x1 = xv + gate1 * mx
        y, _, _ = _rms(x1, g)
        return [mx, x1, y * (1.0 + sc) + sh], []

    mix, x1, h2 = rowwise("mix_fwd", mid_fn, [o_cat, x], [w_o, g1, w["g_ffn_norm"], sc2, sh2],
                          [(D_MODEL, F32), (D_MODEL, F32), (D_MODEL, MXU_DTYPE)], tm=256)
    w_up, w_conv, w_down = fetch("w_up", h2), fetch("w_conv", h2), fetch("w_down", h2)
    dn, up = ffn_fwd("ffn_fwd", h2, w_up, w_conv, b_conv, w_down)

    def final_fn(rows, params):
        (x1v, dnv, tgt), (gate2,) = rows, params
        r = x1v + gate2 * dnv - tgt
        dy = r * (1.0 / D_MODEL)
        loss = jnp.sum(_colsum(r * r), axis=-1, keepdims=True) * (0.5 / D_MODEL)
        return [dy, gate2 * dy], [loss, _colsum(dy * dnv)]

    dy, d_dn, loss, dg2 = rowwise("loss_head", final_fn, [x1, dn, target], [g2], [(D_MODEL, F32), (D_MODEL, MXU_DTYPE)],
                                  [1, D_MODEL])
    dh2, g_up, g_down, g_w_conv, g_b_conv = ffn_bwd("ffn_bwd", h2, up, w_up, w_conv, b_conv, d_dn, w_down)
    emit("w_down", g_down)
    emit("w_conv", g_w_conv)
    sent = emit("w_up", g_up)

    def mid_bwd_fn(rows, params):
        (dh2v, dyv, x1v, mx), (gate1, g, sc) = rows, params
        yn, n, rstd = _rms(x1v, g)
        dx_n, dg = _rms_bwd(dh2v * (1.0 + sc), n, rstd, g)
        dx1 = dyv + dx_n
        return [dx1, gate1 * dx1], [dg, _colsum(dh2v * yn), _colsum(dh2v), _colsum(dx1 * mx)]

    dx1, dmix, dg_ffn, dsc2, dsh2, dg1 = rowwise(
        "mid_bwd", mid_bwd_fn, [dh2, dy, x1, mix], [g1, w["g_ffn_norm"], sc2], [(D_MODEL, F32), (D_MODEL, MXU_DTYPE)],
        [D_MODEL] * 4, dep=sent)

    sent = emit("w_o", matmul("mix_wgrad", o_cat, dmix, "tn", tm=512, out_dtype=MXU_DTYPE))
    do_cat = matmul("mix_dgrad", dmix, w_o, "nt", tm=512, dep=sent)
    dband = None
    for dil, b in zip(DILATIONS, band):
        dband = band_bwd(f"band{dil}_bwd", qd_r, kd_r, proj, b[1], lse_mix, o_cat, do_cat, dil, before=dband)
    dq_mla, dkv_mla, dkper = mla_bwd("mla_bwd", q_mla, k_mla, v_mla, o_cat, do_cat, lse_mla, mla_scale)

    def mla_prep_bwd_fn(rows, params):
        (dqv, dkvv, qv, kvv, cm, sm), (gq, gk) = rows, params
        nope_lanes = _lane(cm.shape) < NOPE
        dqs, dkvs, dgq, dgk = [], [], 0.0, 0.0
        for dqc, dkc, qc, kc in zip(_chunks(dqv), _chunks(dkvv), _chunks(qv), _chunks(kvv), strict=True):
            _, n, rstd = _grms(qc, gq, Q_GROUPS)
            dx, dg = _grms_bwd(_rope_bwd(dqc, cm, sm, H_M), n, rstd, gq, Q_GROUPS)
            dqs.append(dx)
            dgq = dgq + dg
            _, n, rstd = _grms(kc, gk, K_GROUPS)
            dx, dg = _grms_bwd(dkc, n, rstd, gk, K_GROUPS)
            dkvs.append(jnp.where(nope_lanes, dx, dkc))
            dgk = dgk + dg
        return [jnp.concatenate(dqs, axis=1), jnp.concatenate(dkvs, axis=1)], [dgq, dgk]

    dq, dkv, dg_q, dg_k = rowwise("mla_prep_bwd", mla_prep_bwd_fn, [dq_mla, dkv_mla, q, kv, cos_m, sin_m], [g_q, g_k],
                                  [(HEADS * LANES, MXU_DTYPE)] * 2, [LANES, LANES], tm=256)
    emit("w_q_b", matmul("q_wgrad", dq, qln, "tn", out_dtype=MXU_DTYPE))
    emit("w_kv_b", matmul("kv_wgrad", kvn, dkv, "tn", out_dtype=MXU_DTYPE))

    def pre_bwd_fn(rows, params):
        dqv, dkvv, dkp, dqd_, dkd_, dvd_, pv, cm, sm, cd, sd = rows
        wq_t, wkv, gq, gkv, gkp, gdq, gdk = params
        dql, dkvl = _dot(dqv, wq_t), _dot(dkvv, wkv, "nt")
        r_q = _norm_bwd(dql, pv[:, P_QLAT:P_KVLAT], gq)
        r_kv = _norm_bwd(dkvl, pv[:, P_KVLAT:P_KPE], gkv)
        _, n, rstd = _grms(pv[:, P_KPE:P_QD], gkp, KPE_GROUPS)
        r_kp = _grms_bwd(_rope_bwd(dkp, cm, sm, H_M), n, rstd, gkp, KPE_GROUPS)
        outs, dgs = [r_q[0], r_kv[0], r_kp[0]], []
        for dval, lo, g in ((dqd_, P_QD, gdq), (dkd_, P_KD, gdk)):
            dg_sum = 0.0
            for dc, xc in zip(_chunks(dval), _chunks(pv[:, lo:lo + DIL_WIDTH]), strict=True):
                _, n, rstd = _grms(xc, g, DIL_GROUPS)
                dx, dg = _grms_bwd(_rope_bwd(dc, cd, sd, H_D), n, rstd, g, DIL_GROUPS)
                outs.append(dx)
                dg_sum = dg_sum + dg
            dgs.append(dg_sum)
        return [jnp.concatenate(outs + [dvd_], axis=1)], [r_q[1], r_kv[1], r_kp[1]] + dgs

    dproj, dg_q_lat, dg_kv_lat, dg_kpe, dg_dq, dg_dk = rowwise(
        "proj_pre_bwd", pre_bwd_fn,
        [dq, dkv, dkper] + list(dband) + [proj] + tables, [w_q_b, w_kv_b] + post_params,
        [(P_END, MXU_DTYPE)], [Q_LORA, KV_LORA, LANES, LANES, LANES], tm=256)
    sent = emit("w_in", matmul("proj_wgrad", dproj, h, "tn", tn=512, out_dtype=MXU_DTYPE))

    def ln1_bwd_fn(rows, params):
        (dpv, dres, xv), (w_t, g, sc) = rows, params
        dhv = _dot(dpv, w_t)
        yn, n, rstd = _rms(xv, g)
        dx_n, dg = _rms_bwd(dhv * (1.0 + sc), n, rstd, g)
        return [dres + dx_n], [dg, _colsum(dhv * yn), _colsum(dhv)]

    grad_x, dg_mix, dsc1, dsh1 = rowwise("proj_dgrad", ln1_bwd_fn, [dproj, dx1, x], [w_in, w["g_mix_norm"], sc1],
                                         [(D_MODEL, F32)], [D_MODEL] * 3, tm=256, dep=sent)
    dmod = jnp.concatenate([dsh1, dsc1, dg1, dsh2, dsc2, dg2], axis=-1)
    small = {"loss": loss, "b_ada": dmod, "g_mix_norm": dg_mix, "g_q_lat": dg_q_lat, "g_kv_lat": dg_kv_lat,
             "g_mla_q_nope": dg_q[:, :NOPE], "g_mla_q_pe": dg_q[:, NOPE:NOPE + ROPE], "g_mla_k_nope": dg_k[:, :NOPE],
             "g_mla_k_pe": dg_kpe[:, KPE_LO:KPE_LO + ROPE], "g_dil_q": dg_dq[:, :DIL_DIM] + dg_dq[:, DIL_DIM:],
             "g_dil_k": dg_dk[:, :DIL_DIM] + dg_dk[:, DIL_DIM:], "g_ffn_norm": dg_ffn,
             "b_conv": g_b_conv}
    return grad_x, small


COL_SHARDED = ("w_kv_b", "w_conv")
ROW_SHARDED = ("w_o", "w_down") + TRANSPOSED
ADAM_TILE = {"w_ada": 256, "w_up": 176, "w_down": 176}
GATHER_GROUPS = (("w_in",), ("w_q_b", "w_kv_b"), ("w_o",), ("w_up", "w_conv", "w_down"))
FORWARD_STAGES = ((0, 1), (2, 3))
SCATTER_GROUPS = (("w_down", "w_conv", "w_up"), ("w_o",), ("w_q_b", "w_kv_b", "w_in"))
OUT_WEIGHTS = ("w_ada", "b_ada", "g_mix_norm", "w_in", "g_q_lat", "w_q_b", "g_kv_lat", "w_kv_b", "g_mla_q_nope", "g_mla_q_pe",
               "g_mla_k_nope", "g_mla_k_pe", "g_dil_q", "g_dil_k", "w_o", "g_ffn_norm", "w_up", "w_conv", "b_conv", "w_down")


def kernel(x, c, positions, w_ada, b_ada, g_mix_norm, w_in, g_q_lat, w_q_b, g_kv_lat, w_kv_b, g_mla_q_nope, g_mla_q_pe, g_mla_k_nope, g_mla_k_pe, g_dil_q, g_dil_k, w_o, g_ffn_norm, w_up, w_conv, b_conv, w_down, loss_target, m_w_ada, m_b_ada, m_g_mix_norm, m_w_in, m_g_q_lat, m_w_q_b, m_g_kv_lat, m_w_kv_b, m_g_mla_q_nope, m_g_mla_q_pe, m_g_mla_k_nope, m_g_mla_k_pe, m_g_dil_q, m_g_dil_k, m_w_o, m_g_ffn_norm, m_w_up, m_w_conv, m_b_conv, m_w_down, v_w_ada, v_b_ada, v_g_mix_norm, v_w_in, v_g_q_lat, v_w_q_b, v_g_kv_lat, v_w_kv_b, v_g_mla_q_nope, v_g_mla_q_pe, v_g_mla_k_nope, v_g_mla_k_pe, v_g_dil_q, v_g_dil_k, v_w_o, v_g_ffn_norm, v_w_up, v_w_conv, v_b_conv, v_w_down):
    args = dict(locals())
    xi, yi, ci = _place()
    me = 4 * xi + 2 * yi + ci
    def local(prefix, n):
        a = args[prefix + n]
        if n in ROWS_APART:
            return jnp.transpose(a, (2, 0, 1) if n in TRANSPOSED else (1, 0, 2))
        return a[0].T if n in TRANSPOSED else a[0]

    def as_output(n, r):
        if n in ROWS_APART:
            return jnp.transpose(r, (1, 2, 0) if n in TRANSPOSED else (1, 0, 2))
        return (r.T if n in TRANSPOSED else r)[None]

    shard = {n: local("", n) for n in COL_SHARDED + ROW_SHARDED + ("w_ada",)}
    flat = lambda n, a: a.reshape(a.shape[0], a.shape[-1]) if n in ROWS_APART else a
    small_w = {n: args[n] for n in SMALL_PARAMS}

    sc_all, mod_all = ada_modulation("ada_mod", c, shard["w_ada"])

    payload = {n: flat(n, shard[n]) if n == "w_conv" else flat(n, shard[n]).astype(MXU_DTYPE) for n in COL_SHARDED + ROW_SHARDED}
    gather_order = [n for grp in GATHER_GROUPS for n in grp]
    gathered = [exchange_start("gather_start", [payload[n] for n in gather_order], gather=True, after=mod_all, tree=True)]
    after_start = gathered[0][-1]
    full, forwarded = {}, set()

    def forward(stage, after):
        if stage not in forwarded:
            forwarded.add(stage)
            first = sum(len(g) for g in GATHER_GROUPS[:FORWARD_STAGES[stage][0]])
            count = sum(len(GATHER_GROUPS[i]) for i in FORWARD_STAGES[stage])
            gathered[0] = exchange_forward(f"gather_forward{stage}", gathered[0], after, first, count)
        return gathered[0][-1]

    def fetch(name, after):
        if name not in full:
            (i, grp), = [(i, grp) for i, grp in enumerate(GATHER_GROUPS) if name in grp]
            forward([s for s, groups in enumerate(FORWARD_STAGES) if i in groups][0], after)
            srcs, lands = exchange_wait(f"gather{i}_wait", gathered[0], True, after, gather_order.index(grp[0]), len(grp), tree=True)
            for n, src, land in zip(grp, srcs, lands, strict=True):
                stack = lax.dynamic_update_index_in_dim(land, src, me, 0)
                full[n] = to_kernel_layout(n, _gather_cols(stack) if n in COL_SHARDED else _gather_rows(stack))
        return full[name]

    mod_row = lax.dynamic_index_in_dim(mod_all, me, axis=1, keepdims=False).reshape(1, 6 * D_MODEL)
    (mod,) = rowwise("ada_bias", lambda rows, params: ([rows[0] + rows[1]], []), [mod_row, b_ada], [], [(6 * D_MODEL, F32)],
                     dep=after_start)

    own, pending, scatters = {}, {}, {}

    def emit(name, grad):
        grad = from_kernel_layout(name, grad)
        parts = _scatter_cols(grad) if name in COL_SHARDED else _scatter_rows(grad)
        own[name] = lax.dynamic_index_in_dim(parts, me, 0, keepdims=False)
        pending[name] = parts
        for i, grp in enumerate(SCATTER_GROUPS):
            if name == grp[-1]:
                scatters[i] = exchange_start(f"scatter{i}_start", [pending[n] for n in grp], gather=False)
                return scatters[i][-1]
        return None

    pos = positions.reshape(SEQ, 1).astype(F32)
    grad_x, small = _local_step(x[0], pos, mod, loss_target[0], small_w, fetch, emit, halfway=lambda after: forward(1, after))

    res, done = {}, grad_x
    for i, grp in enumerate(SCATTER_GROUPS):
        _, lands = exchange_wait(f"scatter{i}_wait", scatters[i], False, done)
        for n, land in zip(grp, lands, strict=True):
            res[n] = adamw(f"adamw_{n}", shard[n], [own[n], land], local("m_", n), local("v_", n), ADAM_TILE.get(n))
            done = res[n][0]
            res[n] = [as_output(n, r) for r in res[n]]
    (small_all,) = all_gather("gather_small", [_pack_small(small)], after=done)
    loss, small_res = adamw_small("adamw_small", small_all, {n: (args[n], args["m_" + n], args["v_" + n]) for n in SMALL_PARAMS})
    row, _, n_mod = SMALL_AT["b_ada"]
    dmod_all = small_all[:, row:row + n_mod // SMALL_COLS, :].reshape(N_DEV, n_mod)
    dmod_mine = lax.dynamic_slice_in_dim(dmod_all, me * (6 * D_MODEL // N_DEV), 6 * D_MODEL // N_DEV, axis=1)
    g_w_ada = matmul("ada_wgrad", sc_all, dmod_mine, "tn")
    res["w_ada"] = [r[None] for r in adamw("adamw_w_ada", shard["w_ada"], [g_w_ada], m_w_ada[0], v_w_ada[0], ADAM_TILE["w_ada"])]

    def leaf(kind, n):
        return res[n][kind] if n in res else small_res[n][kind]

    return (loss.reshape(()), grad_x[None], *[leaf(k, n) for k in range(4) for n in OUT_WEIGHTS])
```

```python
import jax
import jax.numpy as jnp
from jax import lax
from jax.experimental import pallas as pl
from jax.experimental.pallas import tpu as pltpu

F32 = jnp.float32
MXU_DTYPE = jnp.bfloat16

N_DEV = 8
D_MODEL = 1024
SEQ = 2048
HEADS = 8
NOPE = 64
ROPE = 32
Q_LORA = 512
KV_LORA = 256
DIL_DIM = 64
DIL_WIDTH = HEADS * DIL_DIM
DILATIONS = (1, 4, 16)
SPAN = 128
D_FF = 2816
LANES = 128
SUBLANES = 8
ROPE_THETA = 10000.0
EPS = 1e-6
NEG_INF = -1e30
ADAM_LR, ADAM_B1, ADAM_B2, ADAM_EPS, ADAM_WD, ADAM_STEP = 0.001, 0.9, 0.999, 1e-08, 0.01, 10
VMEM_LIMIT = 56 * 1024 * 1024
MESH_ID = pl.DeviceIdType.MESH

P_QLAT, P_KVLAT, P_KPE, P_QD, P_KD, P_VD, P_END = 0, 512, 768, 896, 1408, 1920, 2432
KPE_LO = 64
MIX_IN = HEADS * LANES + DIL_WIDTH


def _params(**kw):
    return pltpu.CompilerParams(vmem_limit_bytes=VMEM_LIMIT, **kw)


def rowwise(name, fn, rows, params, out_rows, out_accs=(), tm=512, dep=None):
    deps = [] if dep is None else [dep]
    rows = [r if isinstance(r, tuple) else (r, r.shape[1], 0) for r in rows]
    R = rows[0][0].shape[0]
    tm = min(tm, R)
    steps = R // tm
    assert steps * tm == R
    in_specs = []
    for a, width, cb in rows:
        ri = a.shape[0]
        per = ri // tm
        assert per * tm == ri
        if ri == R:
            in_specs.append(pl.BlockSpec((tm, width), lambda i, cb=cb: (i, cb)))
        else:
            in_specs.append(pl.BlockSpec((tm, width), lambda i, per=per, cb=cb: (i % per, cb)))
    for p in params:
        in_specs.append(pl.BlockSpec(p.shape, lambda i: (0,) * p.ndim))
    in_specs += [pl.BlockSpec(memory_space=pl.ANY)] * len(deps)
    out_shape = [jax.ShapeDtypeStruct((R, d), dt) for d, dt in out_rows]
    out_specs = [pl.BlockSpec((tm, d), lambda i: (i, 0)) for d, _ in out_rows]
    out_shape += [jax.ShapeDtypeStruct((1, n), F32) for n in out_accs]
    out_specs += [pl.BlockSpec((1, n), lambda i: (0, 0)) for n in out_accs]
    nr, npar, no, na = len(rows), len(params), len(out_rows), len(out_accs)

    def body(*refs):
        rvals = [r[...] for r in refs[:nr]]
        pvals = [r[...] for r in refs[nr:nr + npar]]
        outs, accs = fn(rvals, pvals)
        first_out = nr + npar + len(deps)
        for ref, v in zip(refs[first_out:first_out + no], outs, strict=True):
            ref[...] = v.astype(ref.dtype)
        if na:
            acc_refs = refs[first_out + no:]
            i = pl.program_id(0)

            @pl.when(i == 0)
            def _():
                for ref, v in zip(acc_refs, accs, strict=True):
                    ref[...] = v

            @pl.when(i > 0)
            def _():
                for ref, v in zip(acc_refs, accs, strict=True):
                    ref[...] += v

    res = pl.pallas_call(body, name=name, grid=(steps,), in_specs=in_specs, out_specs=out_specs,
                         out_shape=out_shape, compiler_params=_params())(*[r[0] for r in rows], *params, *deps)
    return list(res)


_DIMS = {"nn": ((1,), (0,)), "nt": ((1,), (1,)), "tn": ((0,), (0,))}


def _dot(a, b, mode="nn"):
    return lax.dot_general(a.astype(MXU_DTYPE), b.astype(MXU_DTYPE), (_DIMS[mode], ((), ())),
                           preferred_element_type=F32)


def matmul(name, a, b, mode, tm=None, tn=None, tk=None, out_dtype=F32, dep=None):
    if mode == "tn":
        K, M = a.shape
    else:
        M, K = a.shape
    N = b.shape[0] if mode == "nt" else b.shape[1]
    tm, tn, tk = tm or M, tn or N, tk or K
    nm, nn, nk = M // tm, N // tn, K // tk
    assert nm * tm == M and nn * tn == N and nk * tk == K
    a_spec = pl.BlockSpec((tk, tm), lambda i, j, k: (k, i)) if mode == "tn" else pl.BlockSpec((tm, tk), lambda i, j, k: (i, k))
    b_spec = pl.BlockSpec((tn, tk), lambda i, j, k: (j, k)) if mode == "nt" else pl.BlockSpec((tk, tn), lambda i, j, k: (k, j))
    deps = [] if dep is None else [dep]

    def body(a_ref, b_ref, *rest):
        o_ref, scratch = rest[len(deps)], rest[len(deps) + 1:]
        p = _dot(a_ref[...], b_ref[...], mode)
        if nk == 1:
            o_ref[...] = p.astype(o_ref.dtype)
        else:
            acc = scratch[0]
            k = pl.program_id(2)

            @pl.when(k == 0)
            def _():
                acc[...] = p

            @pl.when(k > 0)
            def _():
                acc[...] += p

            @pl.when(k == nk - 1)
            def _():
                o_ref[...] = acc[...].astype(o_ref.dtype)

    return pl.pallas_call(
        body, name=name, grid=(nm, nn, nk), in_specs=[a_spec, b_spec] + [pl.BlockSpec(memory_space=pl.ANY)] * len(deps),
        out_specs=pl.BlockSpec((tm, tn), lambda i, j, k: (i, j)),
        out_shape=jax.ShapeDtypeStruct((M, N), out_dtype),
        scratch_shapes=[pltpu.VMEM((tm, tn), F32)] if nk > 1 else [],
        compiler_params=_params())(a, b, *deps)


def _rms(x, g):
    rstd = lax.rsqrt(jnp.mean(x * x, axis=-1, keepdims=True) + EPS)
    n = x * rstd
    return n * g, n, rstd


def _rms_bwd(dy, n, rstd, g):
    dg = jnp.sum(dy * n, axis=0, keepdims=True)
    dn = dy * g
    dx = rstd * (dn - n * jnp.mean(dn * n, axis=-1, keepdims=True))
    return dx, dg


def _norm_bwd(dy, x, g):
    _, n, rstd = _rms(x, g)
    return _rms_bwd(dy, n, rstd, g)


def _colsum(v):
    return jnp.sum(v, axis=0, keepdims=True)


def _silu(x):
    return x * (1.0 / (1.0 + jnp.exp(-x)))


def _lane(shape):
    return lax.broadcasted_iota(jnp.int32, shape, 1)


def _group_mean(v, groups):
    i = lax.broadcasted_iota(jnp.int32, (LANES, LANES), 0)
    j = lax.broadcasted_iota(jnp.int32, (LANES, LANES), 1)
    g = jnp.zeros((LANES, LANES), F32)
    for lo, hi in groups:
        g = jnp.where((i >= lo) & (i < hi) & (j >= lo) & (j < hi), 1.0 / (hi - lo), g)
    head = v.astype(MXU_DTYPE)
    return _dot(head, g) + _dot(v - head.astype(F32), g)


def _in_groups(shape, groups):
    lane = _lane(shape)
    m = jnp.zeros(shape, jnp.bool_)
    for lo, hi in groups:
        m = m | ((lane >= lo) & (lane < hi))
    return m


def _grms(x, g, groups):
    rstd = lax.rsqrt(_group_mean(x * x, groups) + EPS)
    n = jnp.where(_in_groups(x.shape, groups), x * rstd, 0.0)
    return n * g, n, rstd


def _grms_bwd(dy, n, rstd, g, groups):
    dn = dy * g
    return rstd * (dn - n * _group_mean(dn * n, groups)), _colsum(dy * n)


def _rot(x, half, transpose=False):
    first = (_lane(x.shape) % (2 * half)) < half
    up = pltpu.roll(x, LANES - half, axis=1)
    down = pltpu.roll(x, half, axis=1)
    return jnp.where(first, up, -down) if transpose else jnp.where(first, -up, down)


def _rope(x, cos, sin, half):
    return x * cos + _rot(x, half) * sin


def _rope_bwd(dy, cos, sin, half):
    return dy * cos + _rot(dy * sin, half, transpose=True)


def _chunks(x):
    return [x[:, i:i + LANES] for i in range(0, x.shape[1], LANES)]


Q_GROUPS = ((0, NOPE), (NOPE, NOPE + ROPE))
K_GROUPS = ((0, NOPE),)
KPE_GROUPS = ((KPE_LO, KPE_LO + ROPE),)
DIL_GROUPS = ((0, DIL_DIM), (DIL_DIM, 2 * DIL_DIM))


def _col(width, rows=SEQ):
    return pl.BlockSpec((rows, width), lambda h: (0, h))


def _causal_tail(s, tq, fill):
    diag = s[:, s.shape[1] - tq:]
    keep = lax.broadcasted_iota(jnp.int32, diag.shape, 1) <= lax.broadcasted_iota(jnp.int32, diag.shape, 0)
    diag = jnp.where(keep, diag, fill)
    return diag if s.shape[1] == tq else jnp.concatenate([s[:, :s.shape[1] - tq], diag], axis=1)


def mla_fwd(name, q, k, v, scale, tq=256):
    S = q.shape[0]

    def body(q_ref, k_ref, v_ref, o_ref, lse_ref):
        nb = S // tq
        blk = lambda i: slice(i * tq, (i + 1) * tq)

        def scores(i):
            return _dot(q_ref[blk(i), :], k_ref[:(i + 1) * tq, :], "nt")

        def softmax(i, s):
            s = _causal_tail(s * scale, tq, NEG_INF)
            m = jnp.max(s, axis=-1, keepdims=True)
            e = jnp.exp(s - m)
            l = jnp.sum(e, axis=-1, keepdims=True)
            lse_ref[0, blk(i), :] = m + jnp.log(l)
            return (e * (1.0 / l)).astype(MXU_DTYPE)

        def weighted(i, p):
            o_ref[blk(i), :] = _dot(p, v_ref[:(i + 1) * tq, :])

        s, p_prev = scores(0), None
        for i in range(nb):
            s_next = scores(i + 1) if i + 1 < nb else None
            if p_prev is not None:
                weighted(i - 1, p_prev)
            p_prev, s = softmax(i, s), s_next
        weighted(nb - 1, p_prev)

    return pl.pallas_call(
        body, name=name, grid=(HEADS,), in_specs=[_col(LANES)] * 3,
        out_specs=[_col(LANES), pl.BlockSpec((1, S, 1), lambda h: (h, 0, 0))],
        out_shape=[jax.ShapeDtypeStruct((S, MIX_IN), F32), jax.ShapeDtypeStruct((HEADS, S, 1), F32)],
        compiler_params=_params())(q, k, v)


def mla_bwd(name, q, k, v, o, do, lse, scale, tq=256):
    S = q.shape[0]

    def body(q_ref, k_ref, v_ref, o_ref, do_ref, lse_ref, dq_ref, dkv_ref, dkpe_ref, dk_acc, dv_acc):
        dk_acc[...] = jnp.zeros_like(dk_acc)
        dv_acc[...] = jnp.zeros_like(dv_acc)
        for i in range(S // tq):
            kext = (i + 1) * tq
            blk = slice(i * tq, kext)
            qi, kk, vv = q_ref[blk, :], k_ref[:kext, :], v_ref[:kext, :]
            doi = do_ref[blk, :]
            s = _causal_tail(_dot(qi, kk, "nt") * scale, tq, NEG_INF)
            p = jnp.exp(s - lse_ref[0, blk, :])
            dp = _dot(doi, vv, "nt")
            delta = jnp.sum(doi * o_ref[blk, :], axis=-1, keepdims=True)
            ds = p * (dp - delta) * scale
            dq_ref[blk, :] = _dot(ds, kk)
            dk_acc[:kext, :] += _dot(ds, qi, "tn")
            dv_acc[:kext, :] += _dot(p, doi, "tn")
        dk = dk_acc[...]
        lane = _lane(dk.shape)
        dkv_ref[...] = jnp.where(lane < NOPE, dk, 0.0) + dv_acc[...]
        dkpe = jnp.where((lane >= KPE_LO) & (lane < KPE_LO + ROPE), dk, 0.0)
        h = pl.program_id(0)

        @pl.when(h == 0)
        def _():
            dkpe_ref[...] = dkpe

        @pl.when(h > 0)
        def _():
            dkpe_ref[...] += dkpe

    return pl.pallas_call(
        body, name=name, grid=(HEADS,),
        in_specs=[_col(LANES)] * 5 + [pl.BlockSpec((1, S, 1), lambda h: (h, 0, 0))],
        out_specs=[_col(LANES), _col(LANES), pl.BlockSpec((S, LANES), lambda h: (0, 0))],
        out_shape=[jax.ShapeDtypeStruct((S, HEADS * LANES), F32), jax.ShapeDtypeStruct((S, HEADS * LANES), F32),
                   jax.ShapeDtypeStruct((S, LANES), F32)],
        scratch_shapes=[pltpu.VMEM((S, LANES), F32), pltpu.VMEM((S, LANES), F32)],
        compiler_params=_params())(q, k, v, o, do, lse)


BAND_TQ = SPAN


def _band_blocks(L, tq):
    return [(i * tq, (i + 1) * tq, max(0, i * tq - SPAN)) for i in range(L // tq)]


def _class_rows(r, dil, lo, hi):
    return pl.ds(r + dil * lo, hi - lo, stride=dil) if dil > 1 else pl.ds(lo, hi - lo)


def _stack_heads(t, lo):
    zero = jnp.zeros_like(t)
    return jnp.concatenate([jnp.where(lo, t, zero), jnp.where(lo, zero, t)], axis=0)


def _band_mask2(q0, q1, k0):
    n = q1 - q0
    shape = (2 * n, q1 - k0)
    i = lax.broadcasted_iota(jnp.int32, shape, 0)
    dist = (jnp.where(i >= n, i - n, i) + q0) - (lax.broadcasted_iota(jnp.int32, shape, 1) + k0)
    return (dist >= 0) & (dist <= SPAN)


def _pair_col(col0=0):
    return pl.BlockSpec((SEQ, LANES), lambda j: (0, col0 // LANES + j))


def band_fwd(name, q, k, v, dil, dep=None):
    S = q.shape[0]
    L = S // dil
    tq = BAND_TQ
    scale = DIL_DIM ** -0.5
    deps = [] if dep is None else [dep]

    def body(q_ref, k_ref, v_ref, *rest):
        o_ref, lse_ref = rest[len(deps):]
        items = [(r, blk) for r in range(dil) for blk in _band_blocks(L, tq)]
        lo = _lane((tq, LANES)) < DIL_DIM

        def scores(item):
            r, (q0, q1, k0) = item
            qb = q_ref[_class_rows(r, dil, q0, q1), :].astype(MXU_DTYPE)
            return _dot(_stack_heads(qb, lo), k_ref[_class_rows(r, dil, k0, q1), :], "nt")

        def softmax(item, s):
            _, (q0, q1, k0) = item
            s = jnp.where(_band_mask2(q0, q1, k0), s * scale, NEG_INF)
            mx = jnp.max(s, axis=-1, keepdims=True)
            e = jnp.exp(s - mx)
            l = jnp.sum(e, axis=-1, keepdims=True)
            return (e * (1.0 / l)).astype(MXU_DTYPE), mx + jnp.log(l)

        def weighted(item, p, lse):
            r, (q0, q1, k0) = item
            pv = _dot(p, v_ref[_class_rows(r, dil, k0, q1), :])
            o_ref[_class_rows(r, dil, q0, q1), :] = jnp.where(lo, pv[:tq], pv[tq:])
            lse_ref[_class_rows(r, dil, q0, q1), :] = jnp.where(lo, lse[:tq], lse[tq:])

        s, prev = scores(items[0]), None
        for i, item in enumerate(items):
            s_next = scores(items[i + 1]) if i + 1 < len(items) else None
            if prev is not None:
                weighted(items[i - 1], *prev)
            prev, s = softmax(item, s), s_next
        weighted(items[-1], *prev)

    return pl.pallas_call(
        body, name=name, grid=(DIL_WIDTH // LANES,),
        in_specs=[_pair_col()] * 2 + [_pair_col(P_VD)] + [pl.BlockSpec(memory_space=pl.ANY)] * len(deps), out_specs=[_pair_col()] * 2,
        out_shape=[jax.ShapeDtypeStruct((S, DIL_WIDTH), F32)] * 2, compiler_params=_params())(q, k, v, *deps)


def band_bwd(name, q, k, v, lse, lse_mix, o_cat, do_cat, dil, before=None):
    S = q.shape[0]
    L = S // dil
    tq = BAND_TQ
    scale = DIL_DIM ** -0.5
    before = list(before or [])

    def body(q_ref, k_ref, v_ref, lse_ref, mix_ref, o_ref, do_ref, *rest):
        dq_ref, dk_ref, dv_ref = rest[len(before):]
        if before:
            dq0_ref, dk0_ref, dv0_ref = rest[:3]
            dk_ref[...] = dk0_ref[...]
            dv_ref[...] = dv0_ref[...]
        else:
            dk_ref[...] = jnp.zeros_like(dk_ref)
            dv_ref[...] = jnp.zeros_like(dv_ref)
        items = [(r, blk) for r in range(dil) for blk in _band_blocks(L, tq)]
        lo = _lane((tq, LANES)) < DIL_DIM
        per_head = lambda t: jnp.concatenate([t[:, 0:1], t[:, DIL_DIM:DIL_DIM + 1]], axis=0)

        def scores(item):
            r, (q0, q1, k0) = item
            qrows, krows = _class_rows(r, dil, q0, q1), _class_rows(r, dil, k0, q1)
            lse_p, dout = lse_ref[qrows, :], do_ref[qrows, :]
            w2 = per_head(jnp.exp(lse_p - mix_ref[qrows, :]))
            dd = dout * o_ref[qrows, :]
            big_d = jnp.concatenate([jnp.sum(jnp.where(lo, dd, 0.0), axis=-1, keepdims=True),
                                     jnp.sum(jnp.where(lo, 0.0, dd), axis=-1, keepdims=True)], axis=0)
            q2 = _stack_heads(q_ref[qrows, :].astype(MXU_DTYPE), lo)
            dom = (_stack_heads(dout, lo) * w2).astype(MXU_DTYPE)
            return (_dot(q2, k_ref[krows, :], "nt"), _dot(dom, v_ref[krows, :], "nt"), per_head(lse_p), w2 * big_d, q2, dom)

        def softmax_bwd(item, s, dp, lse2, wd2, q2, dom):
            _, (q0, q1, k0) = item
            p = jnp.where(_band_mask2(q0, q1, k0), jnp.exp(s * scale - lse2), 0.0)
            return p.astype(MXU_DTYPE), (p * (dp - wd2) * scale).astype(MXU_DTYPE), q2, dom

        def grads(item, p, ds, q2, dom):
            r, (q0, q1, k0) = item
            qrows, krows = _class_rows(r, dil, q0, q1), _class_rows(r, dil, k0, q1)
            dq2 = _dot(ds, k_ref[krows, :])
            dq = jnp.where(lo, dq2[:tq], dq2[tq:])
            dq_ref[qrows, :] = dq + dq0_ref[qrows, :] if before else dq
            dk_ref[krows, :] += _dot(ds, q2, "tn")
            dv_ref[krows, :] += _dot(p, dom, "tn")

        sc, prev = scores(items[0]), None
        for i, item in enumerate(items):
            sc_next = scores(items[i + 1]) if i + 1 < len(items) else None
            if prev is not None:
                grads(items[i - 1], *prev)
            prev, sc = softmax_bwd(item, *sc), sc_next
        grads(items[-1], *prev)

    cat = _pair_col(HEADS * LANES)
    return pl.pallas_call(
        body, name=name, grid=(DIL_WIDTH // LANES,),
        in_specs=[_pair_col()] * 2 + [_pair_col(P_VD)] + [_pair_col()] * 2 + [cat] * 2 + [_pair_col()] * len(before),
        out_specs=[_pair_col()] * 3, out_shape=[jax.ShapeDtypeStruct((S, DIL_WIDTH), F32)] * 3,
        compiler_params=_params())(q, k, v, lse, lse_mix, o_cat, do_cat, *before)


def combine_fwd(name, outs, lses, o_cat, tm=512):
    S = outs[0].shape[0]

    def body(o1, o2, o3, l1, l2, l3, cat_in, cat_out, mix_ref):
        ls = [l1[...], l2[...], l3[...]]
        m = jnp.maximum(jnp.maximum(ls[0], ls[1]), ls[2])
        e = [jnp.exp(l - m) for l in ls]
        den = e[0] + e[1] + e[2]
        cat_out[...] = (e[0] / den) * o1[...] + (e[1] / den) * o2[...] + (e[2] / den) * o3[...]
        mix_ref[...] = m + jnp.log(den)

    row = pl.BlockSpec((tm, DIL_WIDTH), lambda i: (i, 0))
    return pl.pallas_call(
        body, name=name, grid=(S // tm,), in_specs=[row] * 6 + [pl.BlockSpec(memory_space=pl.ANY)],
        out_specs=[pl.BlockSpec((tm, DIL_WIDTH), lambda i: (i, HEADS * LANES // DIL_WIDTH)), row],
        out_shape=[jax.ShapeDtypeStruct(o_cat.shape, F32), jax.ShapeDtypeStruct((S, DIL_WIDTH), F32)],
        input_output_aliases={6: 0}, compiler_params=_params())(*outs, *lses, o_cat)


FFN_FWD_ROWS = 512
FFN_BWD_ROWS = 256
CONV_PAD = SUBLANES


def _window(x, k):
    groups = x.reshape(-1, SUBLANES, x.shape[1])
    turned = pltpu.roll(groups, SUBLANES - k, axis=1)
    stays = lax.broadcasted_iota(jnp.int32, (groups.shape[0] - 1,) + groups.shape[1:], 1) < SUBLANES - k
    return jnp.where(stays, turned[:-1], turned[1:]).reshape(-1, x.shape[1])


def _earlier(ref, r0, rows, n):
    if r0 == 0:
        x = jnp.concatenate([jnp.zeros((SUBLANES, ref.shape[1]), F32), ref[:rows, :]], axis=0)
    else:
        x = ref[r0 - SUBLANES:r0 + rows, :]
    return _window(x, SUBLANES - n)


CONV_TC = 256
CONV_NB = D_FF // CONV_TC


def _half_specs(rows, rows_axis=False):
    if rows_axis:
        return [pl.BlockSpec((rows, D_MODEL), lambda j: (j, 0)), pl.BlockSpec((rows, D_MODEL), lambda j: (j + CONV_NB, 0))]
    return [pl.BlockSpec((rows, CONV_TC), lambda j: (0, j)), pl.BlockSpec((rows, CONV_TC), lambda j: (0, j + CONV_NB))]


def _whole(a):
    return pl.BlockSpec(a.shape, lambda j: (0,) * a.ndim)


def _up_pair(h, ug_ref, uv_ref):
    return jnp.concatenate([_dot(h, ug_ref[...], "nt"), _dot(h, uv_ref[...], "nt")], axis=1)


def _conv_taps(up_ref, r0, rows, w, b):
    uin, u1, u2 = up_ref[r0:r0 + rows, :], _earlier(up_ref, r0, rows, 1), _earlier(up_ref, r0, rows, 2)
    return uin, u1, u2, w[2:3, :] * uin + w[1:2, :] * u1 + w[0:1, :] * u2 + b


def ffn_fwd(name, h, w_up_t, w_conv, b_conv, w_down):
    S = h.shape[0]

    def body(h_ref, ug_ref, uv_ref, wg_ref, wv_ref, bg_ref, bv_ref, wd_ref, dn_ref, up_ref):
        @pl.when(pl.program_id(0) == 0)
        def _():
            dn_ref[...] = jnp.zeros_like(dn_ref)

        w = jnp.concatenate([wg_ref[...], wv_ref[...]], axis=1)
        b = jnp.concatenate([bg_ref[...], bv_ref[...]], axis=1)
        rows = FFN_FWD_ROWS
        starts = list(range(0, S, rows))

        def project(r0):
            up_ref[r0:r0 + rows, :] = _up_pair(h_ref[r0:r0 + rows, :], ug_ref, uv_ref)

        def gate(r0):
            u = _conv_taps(up_ref, r0, rows, w, b)[3]
            return (_silu(u[:, :CONV_TC]) * u[:, CONV_TC:]).astype(MXU_DTYPE)

        def project_down(r0, act):
            dn_ref[r0:r0 + rows, :] += _dot(act, wd_ref[...])

        project(starts[0])
        act_prev = None
        for i, r0 in enumerate(starts):
            if i + 1 < len(starts):
                project(starts[i + 1])
            if act_prev is not None:
                project_down(starts[i - 1], act_prev)
            act_prev = gate(r0)
        project_down(starts[-1], act_prev)

    return pl.pallas_call(
        body, name=name, grid=(CONV_NB,),
        in_specs=[_whole(h)] + _half_specs(CONV_TC, rows_axis=True) + _half_specs(3) + _half_specs(1)
        + [pl.BlockSpec((CONV_TC, w_down.shape[1]), lambda j: (j, 0))],
        out_specs=[pl.BlockSpec((S, w_down.shape[1]), lambda j: (0, 0)), pl.BlockSpec((S, 2 * CONV_TC), lambda j: (0, j))],
        out_shape=[jax.ShapeDtypeStruct((S, w_down.shape[1]), F32), jax.ShapeDtypeStruct((S, 2 * D_FF), F32)],
        compiler_params=_params())(h, w_up_t, w_up_t, w_conv, w_conv, b_conv, b_conv, w_down)


def ffn_bwd(name, h, up, w_up_t, w_conv, b_conv, d_dn, w_down):
    S, D = h.shape

    def body(h_ref, up_ref, ug_ref, uv_ref, wg_ref, wv_ref, bg_ref, bv_ref, dd_ref, wd_ref,
             dh_ref, gup_ref, gd_ref, dwg_ref, dwv_ref, dbg_ref, dbv_ref, du_ref, dup_ref, act_ref):
        @pl.when(pl.program_id(0) == 0)
        def _():
            dh_ref[...] = jnp.zeros_like(dh_ref)

        w = jnp.concatenate([wg_ref[...], wv_ref[...]], axis=1)
        b = jnp.concatenate([bg_ref[...], bv_ref[...]], axis=1)
        w_pair = jnp.concatenate([ug_ref[...], uv_ref[...]], axis=0)
        rows = FFN_BWD_ROWS
        starts = list(range(0, S, rows))
        du_ref[S:S + CONV_PAD, :] = jnp.zeros((CONV_PAD, 2 * CONV_TC), F32)

        def project(r0):
            return _dot(dd_ref[r0:r0 + rows, :], wd_ref[...], "nt")

        def through_conv(r0, da):
            uin, u1, u2, u = _conv_taps(up_ref, r0, rows, w, b)
            gate, val = u[:, :CONV_TC], u[:, CONV_TC:]
            sig = 1.0 / (1.0 + jnp.exp(-gate))
            du = jnp.concatenate([da * val * (sig * (1.0 + gate * (1.0 - sig))), da * (gate * sig)], axis=1)
            du_ref[r0:r0 + rows, :] = du
            act_ref[r0:r0 + rows, :] = (gate * sig * val).astype(MXU_DTYPE)
            dw = jnp.concatenate([_colsum(du * u2), _colsum(du * u1), _colsum(du * uin)], axis=0)
            return dw, _colsum(du)

        def back_up(r0):
            du = du_ref[r0:r0 + rows + CONV_PAD, :]
            dup = (w[2:3, :] * du[:rows] + w[1:2, :] * _window(du, 1) + w[0:1, :] * _window(du, 2)).astype(MXU_DTYPE)
            dup_ref[r0:r0 + rows, :] = dup
            dh_ref[r0:r0 + rows, :] += _dot(dup, w_pair)

        dw, db = 0.0, 0.0
        da = project(starts[0])
        for i, r0 in enumerate(starts):
            da_next = project(starts[i + 1]) if i + 1 < len(starts) else None
            dw_c, db_c = through_conv(r0, da)
            if i > 0:
                back_up(starts[i - 1])
            dw, db, da = dw + dw_c, db + db_c, da_next
        back_up(starts[-1])
        g_up, g_dn = _dot(dup_ref[...], h_ref[...], "tn"), _dot(act_ref[...], dd_ref[...], "tn")
        gup_ref[0], gup_ref[1] = g_up[:CONV_TC].astype(gup_ref.dtype), g_up[CONV_TC:].astype(gup_ref.dtype)
        gd_ref[...] = g_dn.astype(gd_ref.dtype)
        dwg_ref[...], dwv_ref[...] = dw[:, :CONV_TC], dw[:, CONV_TC:]
        dbg_ref[...], dbv_ref[...] = db[:, :CONV_TC], db[:, CONV_TC:]

    half = lambda rows: pl.BlockSpec((rows, CONV_TC), lambda j: (0, j))
    rows_blk = pl.BlockSpec((CONV_TC, D), lambda j: (j, 0))
    dh, gup, gd, dwg, dwv, dbg, dbv = pl.pallas_call(
        body, name=name, grid=(CONV_NB,),
        in_specs=[_whole(h), pl.BlockSpec((S, 2 * CONV_TC), lambda j: (0, j))] + _half_specs(CONV_TC, rows_axis=True) + _half_specs(3)
        + _half_specs(1) + [_whole(d_dn), rows_blk],
        out_specs=[pl.BlockSpec((S, D), lambda j: (0, 0)), pl.BlockSpec((2, CONV_TC, D), lambda j: (0, j, 0)), rows_blk,
                   half(3), half(3), half(1), half(1)],
        out_shape=[jax.ShapeDtypeStruct((S, D), F32), jax.ShapeDtypeStruct((2, D_FF, D), MXU_DTYPE),
                   jax.ShapeDtypeStruct((D_FF, D), MXU_DTYPE)]
        + [jax.ShapeDtypeStruct((3, D_FF), F32)] * 2 + [jax.ShapeDtypeStruct((1, D_FF), F32)] * 2,
        scratch_shapes=[pltpu.VMEM((S + CONV_PAD, 2 * CONV_TC), F32), pltpu.VMEM((S, 2 * CONV_TC), MXU_DTYPE),
                        pltpu.VMEM((S, CONV_TC), MXU_DTYPE)],
        compiler_params=_params())(h, up, w_up_t, w_up_t, w_conv, w_conv, b_conv, b_conv, d_dn, w_down)
    return dh, gup.reshape(2 * D_FF, D), gd, jnp.concatenate([dwg, dwv], axis=1), jnp.concatenate([dbg, dbv], axis=1)


def adamw(name, w, parts, m, v, tr=None):
    apart = w.ndim == 3
    R, C = w.shape[0], w.shape[-1]
    tr = tr or R
    assert R % tr == 0
    c1 = 1.0 - ADAM_B1 ** ADAM_STEP
    c2 = 1.0 - ADAM_B2 ** ADAM_STEP
    np_ = len(parts)

    def body(*refs):
        w_ref, m_ref, v_ref = refs[0], refs[1 + np_], refs[2 + np_]
        go_ref, d_ref, mo_ref, vo_ref = refs[3 + np_:]
        terms = []
        for part, ref in zip(parts, refs[1:1 + np_], strict=True):
            terms += [ref[...]] if part.ndim == 2 else [ref[p] for p in range(part.shape[0])]
        g = terms[0].astype(F32)
        for term in terms[1:]:
            g = g + term.astype(F32)
        m2 = ADAM_B1 * m_ref[...] + (1.0 - ADAM_B1) * g
        v2 = ADAM_B2 * v_ref[...] + (1.0 - ADAM_B2) * (g * g)
        go_ref[...] = g
        mo_ref[...] = m2
        vo_ref[...] = v2
        d_ref[...] = -ADAM_LR * ((m2 / c1) / (jnp.sqrt(v2 / c2) + ADAM_EPS) + ADAM_WD * w_ref[...])

    blk = pl.BlockSpec((tr, C), lambda i: (i, 0))
    own = pl.BlockSpec((tr, None, C), lambda i: (i, 0, 0)) if apart else blk
    part_specs = [blk if p.ndim == 2 else pl.BlockSpec((p.shape[0], tr, C), lambda i: (0, i, 0)) for p in parts]
    return pl.pallas_call(
        body, name=name, grid=(R // tr,),
        in_specs=[own] + part_specs + [own, own], out_specs=[own] * 4,
        out_shape=[jax.ShapeDtypeStruct(w.shape, F32)] * 4, compiler_params=_params())(w, *parts, m, v)


def _place():
    return lax.axis_index("x"), lax.axis_index("y"), lax.axis_index("c")


def all_gather(name, arrs, after=None):
    n = len(arrs)
    deps = [] if after is None else [after]

    def body(*refs):
        ins, outs = refs[:n], refs[n + len(deps):2 * n + len(deps)]
        send_sems, recv_sems, local_sems = refs[2 * n + len(deps):]
        x, y, c = _place()
        me, sibling = (x, y, c), (x, y, 1 - c)
        chips = [(1 - x, y), (x, 1 - y), (1 - x, 1 - y)]
        sends = []
        for t in range(n):
            out = outs[t]

            def slot(px, py, pc, out=out):
                return out.at[4 * px + 2 * py + pc]

            def copy(k, block, to, src=None, t=t, slot=slot):
                return pltpu.make_async_remote_copy(
                    src_ref=slot(*block) if src is None else src, dst_ref=slot(*block),
                    send_sem=send_sems.at[7 * t + k], recv_sem=recv_sems.at[7 * t + k],
                    device_id=to, device_id_type=MESH_ID)

            mine = pltpu.make_async_copy(ins[t], slot(*me), local_sems.at[t])
            mine.start()
            first = [copy(0, me, sibling, src=ins[t])]
            first += [copy(1 + j, me, (*chip, c), src=ins[t]) for j, chip in enumerate(chips)]
            for cp in first:
                cp.start()
            sends.append((mine, first, copy))
        for t in range(n):
            mine, first, copy = sends[t]
            passed = [copy(4 + j, (*chip, c), sibling) for j, chip in enumerate(chips)]
            for j, chip in enumerate(chips):
                copy(1 + j, (*chip, c), me).wait_recv()
                passed[j].start()
            copy(0, sibling, me).wait_recv()
            for j, chip in enumerate(chips):
                copy(4 + j, (*chip, 1 - c), me).wait_recv()
            for cp in first + passed:
                cp.wait_send()
            mine.wait()

    any_spec = pl.BlockSpec(memory_space=pl.ANY)
    res = pl.pallas_call(
        body, name=name, in_specs=[any_spec] * (n + len(deps)), out_specs=[any_spec] * n,
        out_shape=[jax.ShapeDtypeStruct((N_DEV,) + a.shape, a.dtype) for a in arrs],
        scratch_shapes=[pltpu.SemaphoreType.DMA((7 * n,)), pltpu.SemaphoreType.DMA((7 * n,)), pltpu.SemaphoreType.DMA((n,))],
        compiler_params=pltpu.CompilerParams(has_side_effects=True))(*arrs, *deps)
    return list(res)


def ada_modulation(name, c, w_ada):
    n_mod = w_ada.shape[1]

    def exchange(src_ref, dst_ref, send_sems, recv_sems):
        x, y, c_ = _place()
        me = 4 * x + 2 * y + c_
        copies = []
        for k in range(1, N_DEV):
            px, py, pc = x ^ (k >> 2), y ^ ((k >> 1) & 1), c_ ^ (k & 1)
            copies.append(pltpu.make_async_remote_copy(
                src_ref=src_ref, dst_ref=dst_ref.at[me], send_sem=send_sems.at[k - 1], recv_sem=recv_sems.at[k - 1],
                device_id=(px, py, pc), device_id_type=MESH_ID))
        for cp in copies:
            cp.start()
        for cp in copies:
            cp.wait_recv()
        for cp in copies:
            cp.wait_send()
        return me

    def body(c_ref, w_ref, sc_ref, mod_ref, c_all, send_c, recv_c, send_m, recv_m):
        me = exchange(c_ref, c_all, send_c, recv_c)
        c_all[me] = c_ref[...]
        sc = _silu(jnp.concatenate([c_all[p] for p in range(N_DEV)], axis=0))
        sc_ref[...] = sc.astype(sc_ref.dtype)
        mod_ref[me] = _dot(sc, w_ref[...])
        exchange(mod_ref.at[me], mod_ref, send_m, recv_m)

    vmem = pl.BlockSpec(memory_space=pltpu.VMEM)
    return pl.pallas_call(
        body, name=name, in_specs=[vmem, vmem], out_specs=[vmem, vmem],
        out_shape=[jax.ShapeDtypeStruct((N_DEV, c.shape[1]), MXU_DTYPE), jax.ShapeDtypeStruct((N_DEV, N_DEV, n_mod), F32)],
        scratch_shapes=[pltpu.VMEM((N_DEV, 1, c.shape[1]), F32)] + [pltpu.SemaphoreType.DMA((N_DEV - 1,))] * 4,
        compiler_params=pltpu.CompilerParams(has_side_effects=True, vmem_limit_bytes=VMEM_LIMIT))(c, w_ada)


HBM_SPEC = pl.BlockSpec(memory_space=pltpu.HBM)
SEM_SPEC = pl.BlockSpec(memory_space=pltpu.SEMAPHORE)
DATAFLOW = pltpu.SideEffectType.DATAFLOW_SIDE_EFFECTING


def _exchange_copies(srcs, lands, send_sems, recv_sems, gather, first=0):
    x, y, c = _place()
    me = 4 * x + 2 * y + c
    out = []
    for t, (src, land) in enumerate(zip(srcs, lands, strict=True)):
        for k in range(1, N_DEV):
            px, py, pc = x ^ (k >> 2), y ^ ((k >> 1) & 1), c ^ (k & 1)
            sem = 7 * (first + t) + k - 1
            out.append((k, pltpu.make_async_remote_copy(
                src_ref=src if gather else src.at[4 * px + 2 * py + pc],
                dst_ref=land.at[me] if gather else land.at[k - 1],
                send_sem=send_sems.at[sem], recv_sem=recv_sems.at[sem],
                device_id=(px, py, pc), device_id_type=MESH_ID)))
    return out


TREE_DIRECT = (1, 2, 4, 6)
TREE_FORWARDED = (3, 5, 7)


def exchange_start(name, arrs, gather, after=None, tree=False):
    n = len(arrs)
    lands = [lax.empty(((N_DEV,) + a.shape) if gather else ((N_DEV - 1,) + a.shape[1:]), a.dtype) for a in arrs]
    deps = [] if after is None else [after]

    def body(*refs):
        srcs, land_refs = refs[:n], refs[n:2 * n]
        send_sems, recv_sems = refs[2 * n + len(deps)], refs[2 * n + len(deps) + 1]
        token = refs[-1]
        for k, cp in _exchange_copies(srcs, land_refs, send_sems, recv_sems, gather):
            if not tree or k in TREE_DIRECT:
                cp.start()
        token[...] = jnp.zeros_like(token)

    hbm = lambda a: pltpu.HBM(a.shape, a.dtype)
    res = pl.pallas_call(
        body, name=name,
        out_shape=(pltpu.SemaphoreType.DMA((7 * n,)), pltpu.SemaphoreType.DMA((7 * n,)), *[hbm(a) for a in arrs],
                   *[hbm(l) for l in lands], jax.ShapeDtypeStruct((8, 128), F32)),
        in_specs=[HBM_SPEC] * (2 * n) + [pl.BlockSpec(memory_space=pl.ANY)] * len(deps),
        out_specs=(SEM_SPEC, SEM_SPEC, *[HBM_SPEC] * (2 * n), pl.BlockSpec(memory_space=pltpu.VMEM)),
        input_output_aliases={i: 2 + i for i in range(2 * n)},
        compiler_params=pltpu.CompilerParams(has_side_effects=DATAFLOW),
    )(*[pltpu.with_memory_space_constraint(a, pltpu.HBM) for a in arrs + lands], *deps)
    return res[0], res[1], list(res[2:2 + n]), list(res[2 + n:2 + 2 * n]), res[-1]


def exchange_forward(name, started, after, first=0, count=None):
    send_sems, recv_sems, srcs, lands, _ = started
    count = len(srcs) - first if count is None else count
    mine = lands[first:first + count]
    n = len(mine)

    def copies(land_refs, send_ref, recv_ref):
        x, y, c = _place()
        out = []
        for t, land in enumerate(land_refs):
            for k in (2, 4, 6):
                slot = land.at[4 * (x ^ (k >> 2)) + 2 * (y ^ ((k >> 1) & 1)) + c]
                came, goes = 7 * (first + t) + k - 1, 7 * (first + t) + (k ^ 1) - 1
                out.append((
                    pltpu.make_async_remote_copy(src_ref=slot, dst_ref=slot, send_sem=send_ref.at[came], recv_sem=recv_ref.at[came],
                                                 device_id=(x, y, c), device_id_type=MESH_ID),
                    pltpu.make_async_remote_copy(src_ref=slot, dst_ref=slot, send_sem=send_ref.at[goes], recv_sem=recv_ref.at[goes],
                                                 device_id=(x, y, 1 - c), device_id_type=MESH_ID)))
        return out

    def arrived(*refs):
        for came, _ in copies(refs[:n], refs[n], refs[n + 1]):
            came.wait_recv()

    def pass_on(*refs):
        for _, goes in copies(refs[:n], refs[n], refs[n + 1]):
            goes.start()
        refs[-1][...] = jnp.zeros_like(refs[-1])

    hbm = lambda a: pltpu.HBM(a.shape, a.dtype)
    here = pl.pallas_call(
        arrived, name=name + "_arrived", out_shape=tuple(hbm(a) for a in mine),
        in_specs=[HBM_SPEC] * n + [SEM_SPEC, SEM_SPEC, pl.BlockSpec(memory_space=pl.ANY)],
        out_specs=tuple([HBM_SPEC] * n), input_output_aliases={i: i for i in range(n)},
        compiler_params=pltpu.CompilerParams(has_side_effects=DATAFLOW),
    )(*mine, send_sems, recv_sems, after)
    res = pl.pallas_call(
        pass_on, name=name, out_shape=(*[hbm(a) for a in mine], jax.ShapeDtypeStruct((8, 128), F32)),
        in_specs=[HBM_SPEC] * n + [SEM_SPEC, SEM_SPEC],
        out_specs=(*[HBM_SPEC] * n, pl.BlockSpec(memory_space=pltpu.VMEM)), input_output_aliases={i: i for i in range(n)},
        compiler_params=pltpu.CompilerParams(has_side_effects=DATAFLOW),
    )(*here, send_sems, recv_sems)
    lands = lands[:first] + list(res[:n]) + lands[first + count:]
    return (send_sems, recv_sems, srcs, lands, res[-1])


def exchange_wait(name, started, gather, after, first=0, count=None, tree=False):
    send_sems, recv_sems, srcs, lands, _ = started
    count = len(srcs) - first if count is None else count
    srcs, lands = srcs[first:first + count], lands[first:first + count]
    n = len(srcs)

    def body(*refs):
        src_refs, land_refs = refs[:n], refs[n:2 * n]
        copies = _exchange_copies(src_refs, land_refs, refs[2 * n], refs[2 * n + 1], gather, first)
        for _, cp in copies:
            cp.wait_send()
        for k, cp in copies:
            if not tree or k in (1,) + TREE_FORWARDED:
                cp.wait_recv()

    hbm = lambda a: pltpu.HBM(a.shape, a.dtype)
    res = pl.pallas_call(
        body, name=name, out_shape=tuple(hbm(a) for a in srcs + lands),
        in_specs=[HBM_SPEC] * (2 * n) + [SEM_SPEC, SEM_SPEC, pl.BlockSpec(memory_space=pl.ANY)],
        out_specs=tuple([HBM_SPEC] * (2 * n)), input_output_aliases={i: i for i in range(2 * n)},
        compiler_params=pltpu.CompilerParams(has_side_effects=DATAFLOW),
    )(*srcs, *lands, send_sems, recv_sems, after)
    return list(res[:n]), list(res[n:])


def _gather_cols(stack):
    p, k, n = stack.shape
    return stack.transpose(1, 0, 2).reshape(k, p * n)


def _scatter_cols(full):
    k, n = full.shape
    return full.reshape(k, N_DEV, n // N_DEV).transpose(1, 0, 2)


def _gather_rows(stack):
    p, r, n = stack.shape
    return stack.reshape(p * r, n)


def _scatter_rows(full):
    r, n = full.shape
    return full.reshape(N_DEV, r // N_DEV, n)


_IN_NAT = Q_LORA + KV_LORA
TRANSPOSED = ("w_in", "w_q_b", "w_up")
ROWS_APART = ("w_in", "w_conv")


def to_kernel_layout(name, w):
    if name == "w_in":
        z = lambda n: jnp.zeros((n, w.shape[1]), w.dtype)
        return jnp.concatenate([w[:_IN_NAT], z(KPE_LO), w[_IN_NAT:_IN_NAT + ROPE], z(LANES - KPE_LO - ROPE), w[_IN_NAT + ROPE:]], axis=0)
    if name == "w_q_b":
        return jnp.pad(w.reshape(HEADS, NOPE + ROPE, -1), ((0, 0), (0, LANES - NOPE - ROPE), (0, 0))).reshape(HEADS * LANES, -1)
    if name == "w_o":
        mla = jnp.pad(w[:HEADS * NOPE].reshape(HEADS, NOPE, -1), ((0, 0), (LANES - NOPE, 0), (0, 0))).reshape(HEADS * LANES, -1)
        return jnp.concatenate([mla, w[HEADS * NOPE:]], axis=0)
    return w


def from_kernel_layout(name, g):
    if name == "w_in":
        return jnp.concatenate([g[:_IN_NAT], g[P_KPE + KPE_LO:P_KPE + KPE_LO + ROPE], g[P_QD:]], axis=0)
    if name == "w_q_b":
        return g.reshape(HEADS, LANES, -1)[:, :NOPE + ROPE, :].reshape(HEADS * (NOPE + ROPE), -1)
    if name == "w_o":
        mla = g[:HEADS * LANES].reshape(HEADS, LANES, -1)[:, LANES - NOPE:, :].reshape(HEADS * NOPE, -1)
        return jnp.concatenate([mla, g[HEADS * LANES:]], axis=0)
    return g


SMALL_COLS = 1024
SMALL_ROWS = 24
SMALL_AT = {"loss": (0, 0, 1), "b_ada": (1, 0, 6 * D_MODEL), "g_mix_norm": (7, 0, D_MODEL), "g_q_lat": (8, 0, Q_LORA),
            "g_kv_lat": (9, 0, KV_LORA), "g_mla_q_nope": (10, 0, NOPE), "g_mla_q_pe": (10, 128, ROPE),
            "g_mla_k_nope": (10, 256, NOPE), "g_mla_k_pe": (10, 384, ROPE), "g_dil_q": (10, 512, DIL_DIM),
            "g_dil_k": (10, 640, DIL_DIM), "g_ffn_norm": (11, 0, D_MODEL), "b_conv": (12, 0, 2 * D_FF)}
SMALL_PARAMS = tuple(n for n in SMALL_AT if n != "loss")


def _pack_small(values):
    by_row = {}
    for name, (row, off, n) in SMALL_AT.items():
        by_row.setdefault(row, []).append((off, values[name].reshape(-1).astype(F32)))
    out = []
    for row in sorted(by_row):
        pieces, at = [], 0
        for off, v in sorted(by_row[row], key=lambda t: t[0]):
            pieces += [jnp.zeros((off - at,), F32), v]
            at = off + v.shape[0]
        flat = jnp.concatenate(pieces)
        nrows = -(-flat.shape[0] // SMALL_COLS)
        out.append(jnp.pad(flat, (0, nrows * SMALL_COLS - flat.shape[0])).reshape(nrows, SMALL_COLS))
    packed = jnp.concatenate(out, axis=0)
    return jnp.pad(packed, ((0, SMALL_ROWS - packed.shape[0]), (0, 0)))


def _adam(w, g, m, v):
    c1 = 1.0 - ADAM_B1 ** ADAM_STEP
    c2 = 1.0 - ADAM_B2 ** ADAM_STEP
    m2 = ADAM_B1 * m + (1.0 - ADAM_B1) * g
    v2 = ADAM_B2 * v + (1.0 - ADAM_B2) * (g * g)
    return -ADAM_LR * ((m2 / c1) / (jnp.sqrt(v2 / c2) + ADAM_EPS) + ADAM_WD * w), m2, v2


def adamw_small(name, stack, params):
    flat = [a for n in SMALL_PARAMS for a in params[n]]

    def body(stack_ref, *refs):
        ins, outs = refs[:len(flat)], refs[len(flat):]
        g_all = stack_ref[0]
        for p in range(1, N_DEV):
            g_all = g_all + stack_ref[p]
        outs[0][...] = g_all[0:1, 0:1]
        for i, pname in enumerate(SMALL_PARAMS):
            row, off, n = SMALL_AT[pname]
            w_ref, m_ref, v_ref = ins[3 * i:3 * i + 3]
            go_ref, d_ref, mo_ref, vo_ref = outs[1 + 4 * i:5 + 4 * i]
            for c0 in range(0, n, SMALL_COLS):
                cn = min(SMALL_COLS, n - c0)
                r = row + c0 // SMALL_COLS
                g = g_all[r:r + 1, off:off + cn]
                cols = (slice(None), slice(c0, c0 + cn))
                d, m2, v2 = _adam(w_ref[cols], g, m_ref[cols], v_ref[cols])
                go_ref[cols], d_ref[cols], mo_ref[cols], vo_ref[cols] = g, d, m2, v2

    whole = lambda a: pl.BlockSpec(a.shape, lambda: (0,) * a.ndim)
    out_shape = [jax.ShapeDtypeStruct((1, 1), F32)] + [jax.ShapeDtypeStruct(a.shape, F32) for n in SMALL_PARAMS for a in params[n][:1] * 4]
    res = pl.pallas_call(body, name=name, in_specs=[whole(stack)] + [whole(a) for a in flat],
                         out_specs=[pl.BlockSpec(s.shape, lambda s=s: (0,) * len(s.shape)) for s in out_shape],
                         out_shape=out_shape, compiler_params=_params())(stack, *flat)
    return res[0], {n: res[1 + 4 * i:5 + 4 * i] for i, n in enumerate(SMALL_PARAMS)}


def _local_step(x, pos, mod, target, w, fetch, emit, halfway=lambda after: None):
    S = SEQ
    sh1, sc1, g1, sh2, sc2, g2 = [mod[:, i * D_MODEL:(i + 1) * D_MODEL] for i in range(6)]
    zeros = lambda n: jnp.zeros((1, n), F32)
    g_q = jnp.concatenate([w["g_mla_q_nope"], w["g_mla_q_pe"], zeros(LANES - NOPE - ROPE)], axis=1)
    g_k = jnp.concatenate([w["g_mla_k_nope"], zeros(LANES - NOPE)], axis=1)
    g_kpe = jnp.concatenate([zeros(KPE_LO), w["g_mla_k_pe"], zeros(LANES - KPE_LO - ROPE)], axis=1)
    g_dq = jnp.concatenate([w["g_dil_q"]] * 2, axis=1)
    g_dk = jnp.concatenate([w["g_dil_k"]] * 2, axis=1)
    b_conv = w["b_conv"]

    def inv_freq(d):
        return jnp.power(ROPE_THETA, -2.0 * jnp.arange(d // 2, dtype=F32) / d)

    n_m, n_d = ROPE // 2, DIL_DIM // 2
    freqs = jnp.concatenate([inv_freq(ROPE), inv_freq(DIL_DIM), jnp.zeros((LANES - n_m - n_d,), F32)]).reshape(1, LANES)

    def tables_fn(rows, params):
        (p,), (f,) = rows, params
        c, s = jnp.cos(p * f), jnp.sin(p * f)
        one, zero = jnp.ones_like(c), jnp.zeros_like(c)
        mla = lambda t, fill: jnp.concatenate([fill[:, :KPE_LO], t[:, :n_m], t[:, :n_m], fill[:, :LANES - KPE_LO - ROPE]], axis=1)
        dil = lambda t: jnp.concatenate([t[:, n_m:n_m + n_d]] * 4, axis=1)
        return [mla(c, one), mla(s, zero), dil(c), dil(s)], []

    cos_m, sin_m, cos_d, sin_d = rowwise("rope_tables", tables_fn, [pos], [freqs], [(LANES, F32)] * 4)
    tables = [cos_m, sin_m, cos_d, sin_d]
    H_M, H_D = ROPE // 2, DIL_DIM // 2

    def ln1_fn(rows, params):
        (xv,), (g, sc, sh) = rows, params
        y, _, _ = _rms(xv, g)
        return [y * (1.0 + sc) + sh], []

    (h,) = rowwise("ln1_fwd", ln1_fn, [x], [w["g_mix_norm"], sc1, sh1], [(D_MODEL, MXU_DTYPE)], dep=sin_d)
    w_in = fetch("w_in", h)

    def proj_fn(rows, params):
        (hv, cm, sm, cd, sd), (w_t, gq, gkv, gkp, gdq, gdk) = rows, params
        pv = _dot(hv, w_t, "nt")
        kper = _rope(_grms(pv[:, P_KPE:P_QD], gkp, KPE_GROUPS)[0], cm, sm, H_M)
        qd = [_rope(_grms(c, gdq, DIL_GROUPS)[0], cd, sd, H_D) for c in _chunks(pv[:, P_QD:P_KD])]
        kd = [_rope(_grms(c, gdk, DIL_GROUPS)[0], cd, sd, H_D) for c in _chunks(pv[:, P_KD:P_VD])]
        return [pv, _rms(pv[:, P_QLAT:P_KVLAT], gq)[0], _rms(pv[:, P_KVLAT:P_KPE], gkv)[0], kper,
                jnp.concatenate(qd, axis=1), jnp.concatenate(kd, axis=1)], []

    post_params = [w["g_q_lat"], w["g_kv_lat"], g_kpe, g_dq, g_dk]
    proj, qln, kvn, kper, qd_r, kd_r = rowwise(
        "proj_fwd", proj_fn, [h] + tables, [w_in] + post_params,
        [(P_END, F32), (Q_LORA, MXU_DTYPE), (KV_LORA, MXU_DTYPE), (LANES, MXU_DTYPE)] + [(DIL_WIDTH, F32)] * 2, tm=256)
    w_q_b, w_kv_b = fetch("w_q_b", qln), fetch("w_kv_b", kvn)

    def mla_proj_fn(rows, params):
        (qlv, kvlv, kp, cm, sm), (wq_t, wkv, gq, gk) = rows, params
        qv, kvv = _dot(qlv, wq_t, "nt"), _dot(kvlv, wkv)
        value_lanes = _lane(kp.shape) >= NOPE
        qs, ks, vs = [], [], []
        for qc, kc in zip(_chunks(qv), _chunks(kvv), strict=True):
            qs.append(_rope(_grms(qc, gq, Q_GROUPS)[0], cm, sm, H_M))
            ks.append(_grms(kc, gk, K_GROUPS)[0] + kp)
            vs.append(jnp.where(value_lanes, kc, 0.0))
        return [qv, kvv] + [jnp.concatenate(t, axis=1) for t in (qs, ks, vs)], []

    q, kv, q_mla, k_mla, v_mla = rowwise(
        "mla_proj", mla_proj_fn, [qln, kvn, kper, cos_m, sin_m], [w_q_b, w_kv_b, g_q, g_k],
        [(HEADS * LANES, F32)] * 2 + [(HEADS * LANES, MXU_DTYPE)] * 3, tm=256)
    mla_scale = (NOPE + ROPE) ** -0.5
    o_cat, lse_mla = mla_fwd("mla_fwd", q_mla, k_mla, v_mla, mla_scale)
    passed = halfway(lse_mla)

    band = [band_fwd(f"band{dil}_fwd", qd_r, kd_r, proj, dil, dep=passed) for dil in DILATIONS]
    o_cat, lse_mix = combine_fwd("dil_combine", [b[0] for b in band], [b[1] for b in band], o_cat)
    w_o = fetch("w_o", o_cat)

    def mid_fn(rows, params):
        (ov, xv), (w_out, gate1, g, sc, sh) = rows, params
        mx = _dot(ov, w_out)
        x1 = xv + gate1 * mx
        y, _, _ = _rms(x1, g)
        return [mx, x1, y * (1.0 + sc) + sh], []

    mix, x1, h2 = rowwise("mix_fwd", mid_fn, [o_cat, x], [w_o, g1, w["g_ffn_norm"], sc2, sh2],
                          [(D_MODEL, F32), (D_MODEL, F32), (D_MODEL, MXU_DTYPE)], tm=256)
    w_up, w_conv, w_down = fetch("w_up", h2), fetch("w_conv", h2), fetch("w_down", h2)
    dn, up = ffn_fwd("ffn_fwd", h2, w_up, w_conv, b_conv, w_down)

    def final_fn(rows, params):
        (x1v, dnv, tgt), (gate2,) = rows, params
        r = x1v + gate2 * dnv - tgt
        dy = r * (1.0 / D_MODEL)
        loss = jnp.sum(_colsum(r * r), axis=-1, keepdims=True) * (0.5 / D_MODEL)
        return [dy, gate2 * dy], [loss, _colsum(dy * dnv)]

    dy, d_dn, loss, dg2 = rowwise("loss_head", final_fn, [x1, dn, target], [g2], [(D_MODEL, F32), (D_MODEL, MXU_DTYPE)],
                                  [1, D_MODEL])
    dh2, g_up, g_down, g_w_conv, g_b_conv = ffn_bwd("ffn_bwd", h2, up, w_up, w_conv, b_conv, d_dn, w_down)
    emit("w_down", g_down)
    emit("w_conv", g_w_conv)
    sent = emit("w_up", g_up)

    def mid_bwd_fn(rows, params):
        (dh2v, dyv, x1v, mx), (gate1, g, sc) = rows, params
        yn, n, rstd = _rms(x1v, g)
        dx_n, dg = _rms_bwd(dh2v * (1.0 + sc), n, rstd, g)
        dx1 = dyv + dx_n
        return [dx1, gate1 * dx1], [dg, _colsum(dh2v * yn), _colsum(dh2v), _colsum(dx1 * mx)]

    dx1, dmix, dg_ffn, dsc2, dsh2, dg1 = rowwise(
        "mid_bwd", mid_bwd_fn, [dh2, dy, x1, mix], [g1, w["g_ffn_norm"], sc2], [(D_MODEL, F32), (D_MODEL, MXU_DTYPE)],
        [D_MODEL] * 4, dep=sent)

    sent = emit("w_o", matmul("mix_wgrad", o_cat, dmix, "tn", tm=512, out_dtype=MXU_DTYPE))
    do_cat = matmul("mix_dgrad", dmix, w_o, "nt", tm=512, dep=sent)
    dband = None
    for dil, b in zip(DILATIONS, band):
        dband = band_bwd(f"band{dil}_bwd", qd_r, kd_r, proj, b[1], lse_mix, o_cat, do_cat, dil, before=dband)
    dq_mla, dkv_mla, dkper = mla_bwd("mla_bwd", q_mla, k_mla, v_mla, o_cat, do_cat, lse_mla, mla_scale)

    def mla_prep_bwd_fn(rows, params):
        (dqv, dkvv, qv, kvv, cm, sm), (gq, gk) = rows, params
        nope_lanes = _lane(cm.shape) < NOPE
        dqs, dkvs, dgq, dgk = [], [], 0.0, 0.0
        for dqc, dkc, qc, kc in zip(_chunks(dqv), _chunks(dkvv), _chunks(qv), _chunks(kvv), strict=True):
            _, n, rstd = _grms(qc, gq, Q_GROUPS)
            dx, dg = _grms_bwd(_rope_bwd(dqc, cm, sm, H_M), n, rstd, gq, Q_GROUPS)
            dqs.append(dx)
            dgq = dgq + dg
            _, n, rstd = _grms(kc, gk, K_GROUPS)
            dx, dg = _grms_bwd(dkc, n, rstd, gk, K_GROUPS)
            dkvs.append(jnp.where(nope_lanes, dx, dkc))
            dgk = dgk + dg
        return [jnp.concatenate(dqs, axis=1), jnp.concatenate(dkvs, axis=1)], [dgq, dgk]

    dq, dkv, dg_q, dg_k = rowwise("mla_prep_bwd", mla_prep_bwd_fn, [dq_mla, dkv_mla, q, kv, cos_m, sin_m], [g_q, g_k],
                                  [(HEADS * LANES, MXU_DTYPE)] * 2, [LANES, LANES], tm=256)
    emit("w_q_b", matmul("q_wgrad", dq, qln, "tn", out_dtype=MXU_DTYPE))
    emit("w_kv_b", matmul("kv_wgrad", kvn, dkv, "tn", out_dtype=MXU_DTYPE))

    def pre_bwd_fn(rows, params):
        dqv, dkvv, dkp, dqd_, dkd_, dvd_, pv, cm, sm, cd, sd = rows
        wq_t, wkv, gq, gkv, gkp, gdq, gdk = params
        dql, dkvl = _dot(dqv, wq_t), _dot(dkvv, wkv, "nt")
        r_q = _norm_bwd(dql, pv[:, P_QLAT:P_KVLAT], gq)
        r_kv = _norm_bwd(dkvl, pv[:, P_KVLAT:P_KPE], gkv)
        _, n, rstd = _grms(pv[:, P_KPE:P_QD], gkp, KPE_GROUPS)
        r_kp = _grms_bwd(_rope_bwd(dkp, cm, sm, H_M), n, rstd, gkp, KPE_GROUPS)
        outs, dgs = [r_q[0], r_kv[0], r_kp[0]], []
        for dval, lo, g in ((dqd_, P_QD, gdq), (dkd_, P_KD, gdk)):
            dg_sum = 0.0
            for dc, xc in zip(_chunks(dval), _chunks(pv[:, lo:lo + DIL_WIDTH]), strict=True):
                _, n, rstd = _grms(xc, g, DIL_GROUPS)
                dx, dg = _grms_bwd(_rope_bwd(dc, cd, sd, H_D), n, rstd, g, DIL_GROUPS)
                outs.append(dx)
                dg_sum = dg_sum + dg
            dgs.append(dg_sum)
        return [jnp.concatenate(outs + [dvd_], axis=1)], [r_q[1], r_kv[1], r_kp[1]] + dgs

    dproj, dg_q_lat, dg_kv_lat, dg_kpe, dg_dq, dg_dk = rowwise(
        "proj_pre_bwd", pre_bwd_fn,
        [dq, dkv, dkper] + list(dband) + [proj] + tables, [w_q_b, w_kv_b] + post_params,
        [(P_END, MXU_DTYPE)], [Q_LORA, KV_LORA, LANES, LANES, LANES], tm=256)
    sent = emit("w_in", matmul("proj_wgrad", dproj, h, "tn", tn=512, out_dtype=MXU_DTYPE))

    def ln1_bwd_fn(rows, params):
        (dpv, dres, xv), (w_t, g, sc) = rows, params
        dhv = _dot(dpv, w_t)
        yn, n, rstd = _rms(xv, g)
        dx_n, dg = _rms_bwd(dhv * (1.0 + sc), n, rstd, g)
        return [dres + dx_n], [dg, _colsum(dhv * yn), _colsum(dhv)]

    grad_x, dg_mix, dsc1, dsh1 = rowwise("proj_dgrad", ln1_bwd_fn, [dproj, dx1, x], [w_in, w["g_mix_norm"], sc1],
                                         [(D_MODEL, F32)], [D_MODEL] * 3, tm=256, dep=sent)
    dmod = jnp.concatenate([dsh1, dsc1, dg1, dsh2, dsc2, dg2], axis=-1)
    small = {"loss": loss, "b_ada": dmod, "g_mix_norm": dg_mix, "g_q_lat": dg_q_lat, "g_kv_lat": dg_kv_lat,
             "g_mla_q_nope": dg_q[:, :NOPE], "g_mla_q_pe": dg_q[:, NOPE:NOPE + ROPE], "g_mla_k_nope": dg_k[:, :NOPE],
             "g_mla_k_pe": dg_kpe[:, KPE_LO:KPE_LO + ROPE], "g_dil_q": dg_dq[:, :DIL_DIM] + dg_dq[:, DIL_DIM:],
             "g_dil_k": dg_dk[:, :DIL_DIM] + dg_dk[:, DIL_DIM:], "g_ffn_norm": dg_ffn,
             "b_conv": g_b_conv}
    return grad_x, small


COL_SHARDED = ("w_kv_b", "w_conv")
ROW_SHARDED = ("w_o", "w_down") + TRANSPOSED
ADAM_TILE = {"w_ada": 256, "w_up": 176, "w_down": 176}
GATHER_GROUPS = (("w_in",), ("w_q_b", "w_kv_b"), ("w_o",), ("w_up", "w_conv", "w_down"))
FORWARD_STAGES = ((0, 1), (2, 3))
SCATTER_GROUPS = (("w_down", "w_conv", "w_up"), ("w_o",), ("w_q_b", "w_kv_b", "w_in"))
OUT_WEIGHTS = ("w_ada", "b_ada", "g_mix_norm", "w_in", "g_q_lat", "w_q_b", "g_kv_lat", "w_kv_b", "g_mla_q_nope", "g_mla_q_pe",
               "g_mla_k_nope", "g_mla_k_pe", "g_dil_q", "g_dil_k", "w_o", "g_ffn_norm", "w_up", "w_conv", "b_conv", "w_down")


def kernel(x, c, positions, w_ada, b_ada, g_mix_norm, w_in, g_q_lat, w_q_b, g_kv_lat, w_kv_b, g_mla_q_nope, g_mla_q_pe, g_mla_k_nope, g_mla_k_pe, g_dil_q, g_dil_k, w_o, g_ffn_norm, w_up, w_conv, b_conv, w_down, loss_target, m_w_ada, m_b_ada, m_g_mix_norm, m_w_in, m_g_q_lat, m_w_q_b, m_g_kv_lat, m_w_kv_b, m_g_mla_q_nope, m_g_mla_q_pe, m_g_mla_k_nope, m_g_mla_k_pe, m_g_dil_q, m_g_dil_k, m_w_o, m_g_ffn_norm, m_w_up, m_w_conv, m_b_conv, m_w_down, v_w_ada, v_b_ada, v_g_mix_norm, v_w_in, v_g_q_lat, v_w_q_b, v_g_kv_lat, v_w_kv_b, v_g_mla_q_nope, v_g_mla_q_pe, v_g_mla_k_nope, v_g_mla_k_pe, v_g_dil_q, v_g_dil_k, v_w_o, v_g_ffn_norm, v_w_up, v_w_conv, v_b_conv, v_w_down):
    args = dict(locals())
    xi, yi, ci = _place()
    me = 4 * xi + 2 * yi + ci
    def local(prefix, n):
        a = args[prefix + n]
        if n in ROWS_APART:
            return jnp.transpose(a, (2, 0, 1) if n in TRANSPOSED else (1, 0, 2))
        return a[0].T if n in TRANSPOSED else a[0]

    def as_output(n, r):
        if n in ROWS_APART:
            return jnp.transpose(r, (1, 2, 0) if n in TRANSPOSED else (1, 0, 2))
        return (r.T if n in TRANSPOSED else r)[None]

    shard = {n: local("", n) for n in COL_SHARDED + ROW_SHARDED + ("w_ada",)}
    flat = lambda n, a: a.reshape(a.shape[0], a.shape[-1]) if n in ROWS_APART else a
    small_w = {n: args[n] for n in SMALL_PARAMS}

    sc_all, mod_all = ada_modulation("ada_mod", c, shard["w_ada"])

    payload = {n: flat(n, shard[n]) if n == "w_conv" else flat(n, shard[n]).astype(MXU_DTYPE) for n in COL_SHARDED + ROW_SHARDED}
    stage_order = [[n for i in groups for n in GATHER_GROUPS[i]] for groups in FORWARD_STAGES]
    start_stage = lambda s, after: exchange_start(f"gather_start{s}", [payload[n] for n in stage_order[s]], gather=True,
                                                  after=after, tree=True)
    gathered = {0: start_stage(0, mod_all)}
    after_start = gathered[0][-1]
    full, forwarded = {}, set()

    def forward(stage, after):
        if stage not in forwarded:
            forwarded.add(stage)
            gathered[stage] = exchange_forward(f"gather_forward{stage}", gathered[stage], after)
            if stage + 1 < len(FORWARD_STAGES):
                gathered[stage + 1] = start_stage(stage + 1, gathered[stage][-1])
        return gathered[stage][-1]

    def fetch(name, after):
        if name not in full:
            (i, grp), = [(i, grp) for i, grp in enumerate(GATHER_GROUPS) if name in grp]
            (stage,) = [s for s, groups in enumerate(FORWARD_STAGES) if i in groups]
            forward(stage, after)
            srcs, lands = exchange_wait(f"gather{i}_wait", gathered[stage], True, after, stage_order[stage].index(grp[0]), len(grp),
                                        tree=True)
            for n, src, land in zip(grp, srcs, lands, strict=True):
                stack = lax.dynamic_update_index_in_dim(land, src, me, 0)
                full[n] = to_kernel_layout(n, _gather_cols(stack) if n in COL_SHARDED else _gather_rows(stack))
        return full[name]

    mod_row = lax.dynamic_index_in_dim(mod_all, me, axis=1, keepdims=False).reshape(1, 6 * D_MODEL)
    (mod,) = rowwise("ada_bias", lambda rows, params: ([rows[0] + rows[1]], []), [mod_row, b_ada], [], [(6 * D_MODEL, F32)],
                     dep=after_start)

    own, pending, scatters = {}, {}, {}

    def emit(name, grad):
        grad = from_kernel_layout(name, grad)
        parts = _scatter_cols(grad) if name in COL_SHARDED else _scatter_rows(grad)
        own[name] = lax.dynamic_index_in_dim(parts, me, 0, keepdims=False)
        pending[name] = parts
        for i, grp in enumerate(SCATTER_GROUPS):
            if name == grp[-1]:
                scatters[i] = exchange_start(f"scatter{i}_start", [pending[n] for n in grp], gather=False)
                return scatters[i][-1]
        return None

    pos = positions.reshape(SEQ, 1).astype(F32)
    grad_x, small = _local_step(x[0], pos, mod, loss_target[0], small_w, fetch, emit, halfway=lambda after: forward(1, after))

    res, done = {}, grad_x
    for i, grp in enumerate(SCATTER_GROUPS):
        _, lands = exchange_wait(f"scatter{i}_wait", scatters[i], False, done)
        for n, land in zip(grp, lands, strict=True):
            res[n] = adamw(f"adamw_{n}", shard[n], [own[n], land], local("m_", n), local("v_", n), ADAM_TILE.get(n))
            done = res[n][0]
            res[n] = [as_output(n, r) for r in res[n]]
    (small_all,) = all_gather("gather_small", [_pack_small(small)], after=done)
    loss, small_res = adamw_small("adamw_small", small_all, {n: (args[n], args["m_" + n], args["v_" + n]) for n in SMALL_PARAMS})
    row, _, n_mod = SMALL_AT["b_ada"]
    dmod_all = small_all[:, row:row + n_mod // SMALL_COLS, :].reshape(N_DEV, n_mod)
    dmod_mine = lax.dynamic_slice_in_dim(dmod_all, me * (6 * D_MODEL // N_DEV), 6 * D_MODEL // N_DEV, axis=1)
    g_w_ada = matmul("ada_wgrad", sc_all, dmod_mine, "tn")
    res["w_ada"] = [r[None] for r in adamw("adamw_w_ada", shard["w_ada"], [g_w_ada], m_w_ada[0], v_w_ada[0], ADAM_TILE["w_ada"])]

    def leaf(kind, n):
        return res[n][kind] if n in res else small_res[n][kind]

    return (loss.reshape(()), grad_x[None], *[leaf(k, n) for k in range(4) for n in OUT_WEIGHTS])
```

```python
import jax
import jax.numpy as jnp
from jax import lax
from jax.experimental import pallas as pl
from jax.experimental.pallas import tpu as pltpu

F32 = jnp.float32
MXU_DTYPE = jnp.bfloat16

N_DEV = 8
D_MODEL = 1024
SEQ = 2048
HEADS = 8
NOPE = 64
ROPE = 32
Q_LORA = 512
KV_LORA = 256
DIL_DIM = 64
DIL_WIDTH = HEADS * DIL_DIM
DILATIONS = (1, 4, 16)
SPAN = 128
D_FF = 2816
LANES = 128
SUBLANES = 8
ROPE_THETA = 10000.0
EPS = 1e-6
NEG_INF = -1e30
ADAM_LR, ADAM_B1, ADAM_B2, ADAM_EPS, ADAM_WD, ADAM_STEP = 0.001, 0.9, 0.999, 1e-08, 0.01, 10
VMEM_LIMIT = 56 * 1024 * 1024
MESH_ID = pl.DeviceIdType.MESH

P_QLAT, P_KVLAT, P_KPE, P_QD, P_KD, P_VD, P_END = 0, 512, 768, 896, 1408, 1920, 2432
KPE_LO = 64
MIX_IN = HEADS * LANES + DIL_WIDTH


def _params(**kw):
    return pltpu.CompilerParams(vmem_limit_bytes=VMEM_LIMIT, **kw)


def rowwise(name, fn, rows, params, out_rows, out_accs=(), tm=512, dep=None):
    deps = [] if dep is None else [dep]
    rows = [r if isinstance(r, tuple) else (r, r.shape[1], 0) for r in rows]
    R = rows[0][0].shape[0]
    tm = min(tm, R)
    steps = R // tm
    assert steps * tm == R
    in_specs = []
    for a, width, cb in rows:
        ri = a.shape[0]
        per = ri // tm
        assert per * tm == ri
        if ri == R:
            in_specs.append(pl.BlockSpec((tm, width), lambda i, cb=cb: (i, cb)))
        else:
            in_specs.append(pl.BlockSpec((tm, width), lambda i, per=per, cb=cb: (i % per, cb)))
    for p in params:
        in_specs.append(pl.BlockSpec(p.shape, lambda i: (0,) * p.ndim))
    in_specs += [pl.BlockSpec(memory_space=pl.ANY)] * len(deps)
    out_shape = [jax.ShapeDtypeStruct((R, d), dt) for d, dt in out_rows]
    out_specs = [pl.BlockSpec((tm, d), lambda i: (i, 0)) for d, _ in out_rows]
    out_shape += [jax.ShapeDtypeStruct((1, n), F32) for n in out_accs]
    out_specs += [pl.BlockSpec((1, n), lambda i: (0, 0)) for n in out_accs]
    nr, npar, no, na = len(rows), len(params), len(out_rows), len(out_accs)

    def body(*refs):
        rvals = [r[...] for r in refs[:nr]]
        pvals = [r[...] for r in refs[nr:nr + npar]]
        outs, accs = fn(rvals, pvals)
        first_out = nr + npar + len(deps)
        for ref, v in zip(refs[first_out:first_out + no], outs, strict=True):
            ref[...] = v.astype(ref.dtype)
        if na:
            acc_refs = refs[first_out + no:]
            i = pl.program_id(0)

            @pl.when(i == 0)
            def _():
                for ref, v in zip(acc_refs, accs, strict=True):
                    ref[...] = v

            @pl.when(i > 0)
            def _():
                for ref, v in zip(acc_refs, accs, strict=True):
                    ref[...] += v

    res = pl.pallas_call(body, name=name, grid=(steps,), in_specs=in_specs, out_specs=out_specs,
                         out_shape=out_shape, compiler_params=_params())(*[r[0] for r in rows], *params, *deps)
    return list(res)


_DIMS = {"nn": ((1,), (0,)), "nt": ((1,), (1,)), "tn": ((0,), (0,))}


def _dot(a, b, mode="nn"):
    return lax.dot_general(a.astype(MXU_DTYPE), b.astype(MXU_DTYPE), (_DIMS[mode], ((), ())),
                           preferred_element_type=F32)


def matmul(name, a, b, mode, tm=None, tn=None, tk=None, out_dtype=F32, dep=None):
    if mode == "tn":
        K, M = a.shape
    else:
        M, K = a.shape
    N = b.shape[0] if mode == "nt" else b.shape[1]
    tm, tn, tk = tm or M, tn or N, tk or K
    nm, nn, nk = M // tm, N // tn, K // tk
    assert nm * tm == M and nn * tn == N and nk * tk == K
    a_spec = pl.BlockSpec((tk, tm), lambda i, j, k: (k, i)) if mode == "tn" else pl.BlockSpec((tm, tk), lambda i, j, k: (i, k))
    b_spec = pl.BlockSpec((tn, tk), lambda i, j, k: (j, k)) if mode == "nt" else pl.BlockSpec((tk, tn), lambda i, j, k: (k, j))
    deps = [] if dep is None else [dep]

    def body(a_ref, b_ref, *rest):
        o_ref, scratch = rest[len(deps)], rest[len(deps) + 1:]
        p = _dot(a_ref[...], b_ref[...], mode)
        if nk == 1:
            o_ref[...] = p.astype(o_ref.dtype)
        else:
            acc = scratch[0]
            k = pl.program_id(2)

            @pl.when(k == 0)
            def _():
                acc[...] = p

            @pl.when(k > 0)
            def _():
                acc[...] += p

            @pl.when(k == nk - 1)
            def _():
                o_ref[...] = acc[...].astype(o_ref.dtype)

    return pl.pallas_call(
        body, name=name, grid=(nm, nn, nk), in_specs=[a_spec, b_spec] + [pl.BlockSpec(memory_space=pl.ANY)] * len(deps),
        out_specs=pl.BlockSpec((tm, tn), lambda i, j, k: (i, j)),
        out_shape=jax.ShapeDtypeStruct((M, N), out_dtype),
        scratch_shapes=[pltpu.VMEM((tm, tn), F32)] if nk > 1 else [],
        compiler_params=_params())(a, b, *deps)


def _rms(x, g):
    rstd = lax.rsqrt(jnp.mean(x * x, axis=-1, keepdims=True) + EPS)
    n = x * rstd
    return n * g, n, rstd


def _rms_bwd(dy, n, rstd, g):
    dg = jnp.sum(dy * n, axis=0, keepdims=True)
    dn = dy * g
    dx = rstd * (dn - n * jnp.mean(dn * n, axis=-1, keepdims=True))
    return dx, dg


def _norm_bwd(dy, x, g):
    _, n, rstd = _rms(x, g)
    return _rms_bwd(dy, n, rstd, g)


def _colsum(v):
    return jnp.sum(v, axis=0, keepdims=True)


def _silu(x):
    return x * (1.0 / (1.0 + jnp.exp(-x)))


def _lane(shape):
    return lax.broadcasted_iota(jnp.int32, shape, 1)


def _group_mean(v, groups):
    i = lax.broadcasted_iota(jnp.int32, (LANES, LANES), 0)
    j = lax.broadcasted_iota(jnp.int32, (LANES, LANES), 1)
    g = jnp.zeros((LANES, LANES), F32)
    for lo, hi in groups:
        g = jnp.where((i >= lo) & (i < hi) & (j >= lo) & (j < hi), 1.0 / (hi - lo), g)
    head = v.astype(MXU_DTYPE)
    return _dot(head, g) + _dot(v - head.astype(F32), g)


def _in_groups(shape, groups):
    lane = _lane(shape)
    m = jnp.zeros(shape, jnp.bool_)
    for lo, hi in groups:
        m = m | ((lane >= lo) & (lane < hi))
    return m


def _grms(x, g, groups):
    rstd = lax.rsqrt(_group_mean(x * x, groups) + EPS)
    n = jnp.where(_in_groups(x.shape, groups), x * rstd, 0.0)
    return n * g, n, rstd


def _grms_bwd(dy, n, rstd, g, groups):
    dn = dy * g
    return rstd * (dn - n * _group_mean(dn * n, groups)), _colsum(dy * n)


def _rot(x, half, transpose=False):
    first = (_lane(x.shape) % (2 * half)) < half
    up = pltpu.roll(x, LANES - half, axis=1)
    down = pltpu.roll(x, half, axis=1)
    return jnp.where(first, up, -down) if transpose else jnp.where(first, -up, down)


def _rope(x, cos, sin, half):
    return x * cos + _rot(x, half) * sin


def _rope_bwd(dy, cos, sin, half):
    return dy * cos + _rot(dy * sin, half, transpose=True)


def _chunks(x):
    return [x[:, i:i + LANES] for i in range(0, x.shape[1], LANES)]


Q_GROUPS = ((0, NOPE), (NOPE, NOPE + ROPE))
K_GROUPS = ((0, NOPE),)
KPE_GROUPS = ((KPE_LO, KPE_LO + ROPE),)
DIL_GROUPS = ((0, DIL_DIM), (DIL_DIM, 2 * DIL_DIM))


def _col(width, rows=SEQ):
    return pl.BlockSpec((rows, width), lambda h: (0, h))


def _causal_tail(s, tq, fill):
    diag = s[:, s.shape[1] - tq:]
    keep = lax.broadcasted_iota(jnp.int32, diag.shape, 1) <= lax.broadcasted_iota(jnp.int32, diag.shape, 0)
    diag = jnp.where(keep, diag, fill)
    return diag if s.shape[1] == tq else jnp.concatenate([s[:, :s.shape[1] - tq], diag], axis=1)


def mla_fwd(name, q, k, v, scale, tq=256):
    S = q.shape[0]

    def body(q_ref, k_ref, v_ref, o_ref, lse_ref):
        nb = S // tq
        blk = lambda i: slice(i * tq, (i + 1) * tq)

        def scores(i):
            return _dot(q_ref[blk(i), :], k_ref[:(i + 1) * tq, :], "nt")

        def softmax(i, s):
            s = _causal_tail(s * scale, tq, NEG_INF)
            m = jnp.max(s, axis=-1, keepdims=True)
            e = jnp.exp(s - m)
            l = jnp.sum(e, axis=-1, keepdims=True)
            lse_ref[0, blk(i), :] = m + jnp.log(l)
            return (e * (1.0 / l)).astype(MXU_DTYPE)

        def weighted(i, p):
            o_ref[blk(i), :] = _dot(p, v_ref[:(i + 1) * tq, :])

        s, p_prev = scores(0), None
        for i in range(nb):
            s_next = scores(i + 1) if i + 1 < nb else None
            if p_prev is not None:
                weighted(i - 1, p_prev)
            p_prev, s = softmax(i, s), s_next
        weighted(nb - 1, p_prev)

    return pl.pallas_call(
        body, name=name, grid=(HEADS,), in_specs=[_col(LANES)] * 3,
        out_specs=[_col(LANES), pl.BlockSpec((1, S, 1), lambda h: (h, 0, 0))],
        out_shape=[jax.ShapeDtypeStruct((S, MIX_IN), F32), jax.ShapeDtypeStruct((HEADS, S, 1), F32)],
        compiler_params=_params())(q, k, v)


def mla_bwd(name, q, k, v, o, do, lse, scale, tq=256):
    S = q.shape[0]

    def body(q_ref, k_ref, v_ref, o_ref, do_ref, lse_ref, dq_ref, dkv_ref, dkpe_ref, dk_acc, dv_acc):
        dk_acc[...] = jnp.zeros_like(dk_acc)
        dv_acc[...] = jnp.zeros_like(dv_acc)
        for i in range(S // tq):
            kext = (i + 1) * tq
            blk = slice(i * tq, kext)
            qi, kk, vv = q_ref[blk, :], k_ref[:kext, :], v_ref[:kext, :]
            doi = do_ref[blk, :]
            s = _causal_tail(_dot(qi, kk, "nt") * scale, tq, NEG_INF)
            p = jnp.exp(s - lse_ref[0, blk, :])
            dp = _dot(doi, vv, "nt")
            delta = jnp.sum(doi * o_ref[blk, :], axis=-1, keepdims=True)
            ds = p * (dp - delta) * scale
            dq_ref[blk, :] = _dot(ds, kk)
            dk_acc[:kext, :] += _dot(ds, qi, "tn")
            dv_acc[:kext, :] += _dot(p, doi, "tn")
        dk = dk_acc[...]
        lane = _lane(dk.shape)
        dkv_ref[...] = jnp.where(lane < NOPE, dk, 0.0) + dv_acc[...]
        dkpe = jnp.where((lane >= KPE_LO) & (lane < KPE_LO + ROPE), dk, 0.0)
        h = pl.program_id(0)

        @pl.when(h == 0)
        def _():
            dkpe_ref[...] = dkpe

        @pl.when(h > 0)
        def _():
            dkpe_ref[...] += dkpe

    return pl.pallas_call(
        body, name=name, grid=(HEADS,),
        in_specs=[_col(LANES)] * 5 + [pl.BlockSpec((1, S, 1), lambda h: (h, 0, 0))],
        out_specs=[_col(LANES), _col(LANES), pl.BlockSpec((S, LANES), lambda h: (0, 0))],
        out_shape=[jax.ShapeDtypeStruct((S, HEADS * LANES), F32), jax.ShapeDtypeStruct((S, HEADS * LANES), F32),
                   jax.ShapeDtypeStruct((S, LANES), F32)],
        scratch_shapes=[pltpu.VMEM((S, LANES), F32), pltpu.VMEM((S, LANES), F32)],
        compiler_params=_params())(q, k, v, o, do, lse)


BAND_TQ = SPAN


def _band_blocks(L, tq):
    return [(i * tq, (i + 1) * tq, max(0, i * tq - SPAN)) for i in range(L // tq)]


def _class_rows(r, dil, lo, hi):
    return pl.ds(r + dil * lo, hi - lo, stride=dil) if dil > 1 else pl.ds(lo, hi - lo)


def _stack_heads(t, lo):
    zero = jnp.zeros_like(t)
    return jnp.concatenate([jnp.where(lo, t, zero), jnp.where(lo, zero, t)], axis=0)


def _band_mask2(q0, q1, k0):
    n = q1 - q0
    shape = (2 * n, q1 - k0)
    i = lax.broadcasted_iota(jnp.int32, shape, 0)
    dist = (jnp.where(i >= n, i - n, i) + q0) - (lax.broadcasted_iota(jnp.int32, shape, 1) + k0)
    return (dist >= 0) & (dist <= SPAN)


def _pair_col(col0=0):
    return pl.BlockSpec((SEQ, LANES), lambda j: (0, col0 // LANES + j))


def band_fwd(name, q, k, v, dil, dep=None):
    S = q.shape[0]
    L = S // dil
    tq = BAND_TQ
    scale = DIL_DIM ** -0.5
    deps = [] if dep is None else [dep]

    def body(q_ref, k_ref, v_ref, *rest):
        o_ref, lse_ref = rest[len(deps):]
        items = [(r, blk) for r in range(dil) for blk in _band_blocks(L, tq)]
        lo = _lane((tq, LANES)) < DIL_DIM

        def scores(item):
            r, (q0, q1, k0) = item
            qb = q_ref[_class_rows(r, dil, q0, q1), :].astype(MXU_DTYPE)
            return _dot(_stack_heads(qb, lo), k_ref[_class_rows(r, dil, k0, q1), :], "nt")

        def softmax(item, s):
            _, (q0, q1, k0) = item
            s = jnp.where(_band_mask2(q0, q1, k0), s * scale, NEG_INF)
            mx = jnp.max(s, axis=-1, keepdims=True)
            e = jnp.exp(s - mx)
            l = jnp.sum(e, axis=-1, keepdims=True)
            return (e * (1.0 / l)).astype(MXU_DTYPE), mx + jnp.log(l)

        def weighted(item, p, lse):
            r, (q0, q1, k0) = item
            pv = _dot(p, v_ref[_class_rows(r, dil, k0, q1), :])
            o_ref[_class_rows(r, dil, q0, q1), :] = jnp.where(lo, pv[:tq], pv[tq:])
            lse_ref[_class_rows(r, dil, q0, q1), :] = jnp.where(lo, lse[:tq], lse[tq:])

        s, prev = scores(items[0]), None
        for i, item in enumerate(items):
            s_next = scores(items[i + 1]) if i + 1 < len(items) else None
            if prev is not None:
                weighted(items[i - 1], *prev)
            prev, s = softmax(item, s), s_next
        weighted(items[-1], *prev)

    return pl.pallas_call(
        body, name=name, grid=(DIL_WIDTH // LANES,),
        in_specs=[_pair_col()] * 2 + [_pair_col(P_VD)] + [pl.BlockSpec(memory_space=pl.ANY)] * len(deps), out_specs=[_pair_col()] * 2,
        out_shape=[jax.ShapeDtypeStruct((S, DIL_WIDTH), F32)] * 2, compiler_params=_params())(q, k, v, *deps)


def band_bwd(name, q, k, v, lse, lse_mix, o_cat, do_cat, dil, before=None):
    S = q.shape[0]
    L = S // dil
    tq = BAND_TQ
    scale = DIL_DIM ** -0.5
    before = list(before or [])

    def body(q_ref, k_ref, v_ref, lse_ref, mix_ref, o_ref, do_ref, *rest):
        dq_ref, dk_ref, dv_ref = rest[len(before):]
        if before:
            dq0_ref, dk0_ref, dv0_ref = rest[:3]
            dk_ref[...] = dk0_ref[...]
            dv_ref[...] = dv0_ref[...]
        else:
            dk_ref[...] = jnp.zeros_like(dk_ref)
            dv_ref[...] = jnp.zeros_like(dv_ref)
        items = [(r, blk) for r in range(dil) for blk in _band_blocks(L, tq)]
        lo = _lane((tq, LANES)) < DIL_DIM
        per_head = lambda t: jnp.concatenate([t[:, 0:1], t[:, DIL_DIM:DIL_DIM + 1]], axis=0)

        def scores(item):
            r, (q0, q1, k0) = item
            qrows, krows = _class_rows(r, dil, q0, q1), _class_rows(r, dil, k0, q1)
            lse_p, dout = lse_ref[qrows, :], do_ref[qrows, :]
            w2 = per_head(jnp.exp(lse_p - mix_ref[qrows, :]))
            dd = dout * o_ref[qrows, :]
            big_d = jnp.concatenate([jnp.sum(jnp.where(lo, dd, 0.0), axis=-1, keepdims=True),
                                     jnp.sum(jnp.where(lo, 0.0, dd), axis=-1, keepdims=True)], axis=0)
            q2 = _stack_heads(q_ref[qrows, :].astype(MXU_DTYPE), lo)
            dom = (_stack_heads(dout, lo) * w2).astype(MXU_DTYPE)
            return (_dot(q2, k_ref[krows, :], "nt"), _dot(dom, v_ref[krows, :], "nt"), per_head(lse_p), w2 * big_d, q2, dom)

        def softmax_bwd(item, s, dp, lse2, wd2, q2, dom):
            _, (q0, q1, k0) = item
            p = jnp.where(_band_mask2(q0, q1, k0), jnp.exp(s * scale - lse2), 0.0)
            return p.astype(MXU_DTYPE), (p * (dp - wd2) * scale).astype(MXU_DTYPE), q2, dom

        def grads(item, p, ds, q2, dom):
            r, (q0, q1, k0) = item
            qrows, krows = _class_rows(r, dil, q0, q1), _class_rows(r, dil, k0, q1)
            dq2 = _dot(ds, k_ref[krows, :])
            dq = jnp.where(lo, dq2[:tq], dq2[tq:])
            dq_ref[qrows, :] = dq + dq0_ref[qrows, :] if before else dq
            dk_ref[krows, :] += _dot(ds, q2, "tn")
            dv_ref[krows, :] += _dot(p, dom, "tn")

        sc, prev = scores(items[0]), None
        for i, item in enumerate(items):
            sc_next = scores(items[i + 1]) if i + 1 < len(items) else None
            if prev is not None:
                grads(items[i - 1], *prev)
            prev, sc = softmax_bwd(item, *sc), sc_next
        grads(items[-1], *prev)

    cat = _pair_col(HEADS * LANES)
    return pl.pallas_call(
        body, name=name, grid=(DIL_WIDTH // LANES,),
        in_specs=[_pair_col()] * 2 + [_pair_col(P_VD)] + [_pair_col()] * 2 + [cat] * 2 + [_pair_col()] * len(before),
        out_specs=[_pair_col()] * 3, out_shape=[jax.ShapeDtypeStruct((S, DIL_WIDTH), F32)] * 3,
        compiler_params=_params())(q, k, v, lse, lse_mix, o_cat, do_cat, *before)


def combine_fwd(name, outs, lses, o_cat, tm=512):
    S = outs[0].shape[0]

    def body(o1, o2, o3, l1, l2, l3, cat_in, cat_out, mix_ref):
        ls = [l1[...], l2[...], l3[...]]
        m = jnp.maximum(jnp.maximum(ls[0], ls[1]), ls[2])
        e = [jnp.exp(l - m) for l in ls]
        den = e[0] + e[1] + e[2]
        cat_out[...] = (e[0] / den) * o1[...] + (e[1] / den) * o2[...] + (e[2] / den) * o3[...]
        mix_ref[...] = m + jnp.log(den)

    row = pl.BlockSpec((tm, DIL_WIDTH), lambda i: (i, 0))
    return pl.pallas_call(
        body, name=name, grid=(S // tm,), in_specs=[row] * 6 + [pl.BlockSpec(memory_space=pl.ANY)],
        out_specs=[pl.BlockSpec((tm, DIL_WIDTH), lambda i: (i, HEADS * LANES // DIL_WIDTH)), row],
        out_shape=[jax.ShapeDtypeStruct(o_cat.shape, F32), jax.ShapeDtypeStruct((S, DIL_WIDTH), F32)],
        input_output_aliases={6: 0}, compiler_params=_params())(*outs, *lses, o_cat)


FFN_FWD_ROWS = 512
FFN_BWD_ROWS = 256
CONV_PAD = SUBLANES


def _window(x, k):
    groups = x.reshape(-1, SUBLANES, x.shape[1])
    turned = pltpu.roll(groups, SUBLANES - k, axis=1)
    stays = lax.broadcasted_iota(jnp.int32, (groups.shape[0] - 1,) + groups.shape[1:], 1) < SUBLANES - k
    return jnp.where(stays, turned[:-1], turned[1:]).reshape(-1, x.shape[1])


def _earlier(ref, r0, rows, n):
    if r0 == 0:
        x = jnp.concatenate([jnp.zeros((SUBLANES, ref.shape[1]), F32), ref[:rows, :]], axis=0)
    else:
        x = ref[r0 - SUBLANES:r0 + rows, :]
    return _window(x, SUBLANES - n)


CONV_TC = 256
CONV_NB = D_FF // CONV_TC


def _half_specs(rows, rows_axis=False):
    if rows_axis:
        return [pl.BlockSpec((rows, D_MODEL), lambda j: (j, 0)), pl.BlockSpec((rows, D_MODEL), lambda j: (j + CONV_NB, 0))]
    return [pl.BlockSpec((rows, CONV_TC), lambda j: (0, j)), pl.BlockSpec((rows, CONV_TC), lambda j: (0, j + CONV_NB))]


def _whole(a):
    return pl.BlockSpec(a.shape, lambda j: (0,) * a.ndim)


def _up_pair(h, ug_ref, uv_ref):
    return jnp.concatenate([_dot(h, ug_ref[...], "nt"), _dot(h, uv_ref[...], "nt")], axis=1)


def _conv_taps(up_ref, r0, rows, w, b):
    uin, u1, u2 = up_ref[r0:r0 + rows, :], _earlier(up_ref, r0, rows, 1), _earlier(up_ref, r0, rows, 2)
    return uin, u1, u2, w[2:3, :] * uin + w[1:2, :] * u1 + w[0:1, :] * u2 + b


def ffn_fwd(name, h, w_up_t, w_conv, b_conv, w_down):
    S = h.shape[0]

    def body(h_ref, ug_ref, uv_ref, wg_ref, wv_ref, bg_ref, bv_ref, wd_ref, dn_ref, up_ref):
        @pl.when(pl.program_id(0) == 0)
        def _():
            dn_ref[...] = jnp.zeros_like(dn_ref)

        w = jnp.concatenate([wg_ref[...], wv_ref[...]], axis=1)
        b = jnp.concatenate([bg_ref[...], bv_ref[...]], axis=1)
        rows = FFN_FWD_ROWS
        starts = list(range(0, S, rows))

        def project(r0):
            up_ref[r0:r0 + rows, :] = _up_pair(h_ref[r0:r0 + rows, :], ug_ref, uv_ref)

        def gate(r0):
            u = _conv_taps(up_ref, r0, rows, w, b)[3]
            return (_silu(u[:, :CONV_TC]) * u[:, CONV_TC:]).astype(MXU_DTYPE)

        def project_down(r0, act):
            dn_ref[r0:r0 + rows, :] += _dot(act, wd_ref[...])

        project(starts[0])
        act_prev = None
        for i, r0 in enumerate(starts):
            if i + 1 < len(starts):
                project(starts[i + 1])
            if act_prev is not None:
                project_down(starts[i - 1], act_prev)
            act_prev = gate(r0)
        project_down(starts[-1], act_prev)

    return pl.pallas_call(
        body, name=name, grid=(CONV_NB,),
        in_specs=[_whole(h)] + _half_specs(CONV_TC, rows_axis=True) + _half_specs(3) + _half_specs(1)
        + [pl.BlockSpec((CONV_TC, w_down.shape[1]), lambda j: (j, 0))],
        out_specs=[pl.BlockSpec((S, w_down.shape[1]), lambda j: (0, 0)), pl.BlockSpec((S, 2 * CONV_TC), lambda j: (0, j))],
        out_shape=[jax.ShapeDtypeStruct((S, w_down.shape[1]), F32), jax.ShapeDtypeStruct((S, 2 * D_FF), F32)],
        compiler_params=_params())(h, w_up_t, w_up_t, w_conv, w_conv, b_conv, b_conv, w_down)


def ffn_bwd(name, h, up, w_up_t, w_conv, b_conv, d_dn, w_down):
    S, D = h.shape

    def body(h_ref, up_ref, ug_ref, uv_ref, wg_ref, wv_ref, bg_ref, bv_ref, dd_ref, wd_ref,
             dh_ref, gup_ref, gd_ref, dwg_ref, dwv_ref, dbg_ref, dbv_ref, du_ref, dup_ref, act_ref):
        @pl.when(pl.program_id(0) == 0)
        def _():
            dh_ref[...] = jnp.zeros_like(dh_ref)

        w = jnp.concatenate([wg_ref[...], wv_ref[...]], axis=1)
        b = jnp.concatenate([bg_ref[...], bv_ref[...]], axis=1)
        w_pair = jnp.concatenate([ug_ref[...], uv_ref[...]], axis=0)
        rows = FFN_BWD_ROWS
        starts = list(range(0, S, rows))
        du_ref[S:S + CONV_PAD, :] = jnp.zeros((CONV_PAD, 2 * CONV_TC), F32)

        def project(r0):
            return _dot(dd_ref[r0:r0 + rows, :], wd_ref[...], "nt")

        def through_conv(r0, da):
            uin, u1, u2, u = _conv_taps(up_ref, r0, rows, w, b)
            gate, val = u[:, :CONV_TC], u[:, CONV_TC:]
            sig = 1.0 / (1.0 + jnp.exp(-gate))
            du = jnp.concatenate([da * val * (sig * (1.0 + gate * (1.0 - sig))), da * (gate * sig)], axis=1)
            du_ref[r0:r0 + rows, :] = du
            act_ref[r0:r0 + rows, :] = (gate * sig * val).astype(MXU_DTYPE)
            dw = jnp.concatenate([_colsum(du * u2), _colsum(du * u1), _colsum(du * uin)], axis=0)
            return dw, _colsum(du)

        def back_up(r0):
            du = du_ref[r0:r0 + rows + CONV_PAD, :]
            dup = (w[2:3, :] * du[:rows] + w[1:2, :] * _window(du, 1) + w[0:1, :] * _window(du, 2)).astype(MXU_DTYPE)
            dup_ref[r0:r0 + rows, :] = dup
            dh_ref[r0:r0 + rows, :] += _dot(dup, w_pair)

        dw, db = 0.0, 0.0
        da = project(starts[0])
        for i, r0 in enumerate(starts):
            da_next = project(starts[i + 1]) if i + 1 < len(starts) else None
            dw_c, db_c = through_conv(r0, da)
            if i > 0:
                back_up(starts[i - 1])
            dw, db, da = dw + dw_c, db + db_c, da_next
        back_up(starts[-1])
        g_up, g_dn = _dot(dup_ref[...], h_ref[...], "tn"), _dot(act_ref[...], dd_ref[...], "tn")
        gup_ref[0], gup_ref[1] = g_up[:CONV_TC].astype(gup_ref.dtype), g_up[CONV_TC:].astype(gup_ref.dtype)
        gd_ref[...] = g_dn.astype(gd_ref.dtype)
        dwg_ref[...], dwv_ref[...] = dw[:, :CONV_TC], dw[:, CONV_TC:]
        dbg_ref[...], dbv_ref[...] = db[:, :CONV_TC], db[:, CONV_TC:]

    half = lambda rows: pl.BlockSpec((rows, CONV_TC), lambda j: (0, j))
    rows_blk = pl.BlockSpec((CONV_TC, D), lambda j: (j, 0))
    dh, gup, gd, dwg, dwv, dbg, dbv = pl.pallas_call(
        body, name=name, grid=(CONV_NB,),
        in_specs=[_whole(h), pl.BlockSpec((S, 2 * CONV_TC), lambda j: (0, j))] + _half_specs(CONV_TC, rows_axis=True) + _half_specs(3)
        + _half_specs(1) + [_whole(d_dn), rows_blk],
        out_specs=[pl.BlockSpec((S, D), lambda j: (0, 0)), pl.BlockSpec((2, CONV_TC, D), lambda j: (0, j, 0)), rows_blk,
                   half(3), half(3), half(1), half(1)],
        out_shape=[jax.ShapeDtypeStruct((S, D), F32), jax.ShapeDtypeStruct((2, D_FF, D), MXU_DTYPE),
                   jax.ShapeDtypeStruct((D_FF, D), MXU_DTYPE)]
        + [jax.ShapeDtypeStruct((3, D_FF), F32)] * 2 + [jax.ShapeDtypeStruct((1, D_FF), F32)] * 2,
        scratch_shapes=[pltpu.VMEM((S + CONV_PAD, 2 * CONV_TC), F32), pltpu.VMEM((S, 2 * CONV_TC), MXU_DTYPE),
                        pltpu.VMEM((S, CONV_TC), MXU_DTYPE)],
        compiler_params=_params())(h, up, w_up_t, w_up_t, w_conv, w_conv, b_conv, b_conv, d_dn, w_down)
    return dh, gup.reshape(2 * D_FF, D), gd, jnp.concatenate([dwg, dwv], axis=1), jnp.concatenate([dbg, dbv], axis=1)


def adamw(name, w, parts, m, v, tr=None):
    apart = w.ndim == 3
    R, C = w.shape[0], w.shape[-1]
    tr = tr or R
    assert R % tr == 0
    c1 = 1.0 - ADAM_B1 ** ADAM_STEP
    c2 = 1.0 - ADAM_B2 ** ADAM_STEP
    np_ = len(parts)

    def body(*refs):
        w_ref, m_ref, v_ref = refs[0], refs[1 + np_], refs[2 + np_]
        go_ref, d_ref, mo_ref, vo_ref = refs[3 + np_:]
        terms = []
        for part, ref in zip(parts, refs[1:1 + np_], strict=True):
            terms += [ref[...]] if part.ndim == 2 else [ref[p] for p in range(part.shape[0])]
        g = terms[0].astype(F32)
        for term in terms[1:]:
            g = g + term.astype(F32)
        m2 = ADAM_B1 * m_ref[...] + (1.0 - ADAM_B1) * g
        v2 = ADAM_B2 * v_ref[...] + (1.0 - ADAM_B2) * (g * g)
        go_ref[...] = g
        mo_ref[...] = m2
        vo_ref[...] = v2
        d_ref[...] = -ADAM_LR * ((m2 / c1) / (jnp.sqrt(v2 / c2) + ADAM_EPS) + ADAM_WD * w_ref[...])

    blk = pl.BlockSpec((tr, C), lambda i: (i, 0))
    own = pl.BlockSpec((tr, None, C), lambda i: (i, 0, 0)) if apart else blk
    part_specs = [blk if p.ndim == 2 else pl.BlockSpec((p.shape[0], tr, C), lambda i: (0, i, 0)) for p in parts]
    return pl.pallas_call(
        body, name=name, grid=(R // tr,),
        in_specs=[own] + part_specs + [own, own], out_specs=[own] * 4,
        out_shape=[jax.ShapeDtypeStruct(w.shape, F32)] * 4, compiler_params=_params())(w, *parts, m, v)


def _place():
    return lax.axis_index("x"), lax.axis_index("y"), lax.axis_index("c")


def all_gather(name, arrs, after=None):
    n = len(arrs)
    deps = [] if after is None else [after]

    def body(*refs):
        ins, outs = refs[:n], refs[n + len(deps):2 * n + len(deps)]
        send_sems, recv_sems, local_sems = refs[2 * n + len(deps):]
        x, y, c = _place()
        me, sibling = (x, y, c), (x, y, 1 - c)
        chips = [(1 - x, y), (x, 1 - y), (1 - x, 1 - y)]
        sends = []
        for t in range(n):
            out = outs[t]

            def slot(px, py, pc, out=out):
                return out.at[4 * px + 2 * py + pc]

            def copy(k, block, to, src=None, t=t, slot=slot):
                return pltpu.make_async_remote_copy(
                    src_ref=slot(*block) if src is None else src, dst_ref=slot(*block),
                    send_sem=send_sems.at[7 * t + k], recv_sem=recv_sems.at[7 * t + k],
                    device_id=to, device_id_type=MESH_ID)

            mine = pltpu.make_async_copy(ins[t], slot(*me), local_sems.at[t])
            mine.start()
            first = [copy(0, me, sibling, src=ins[t])]
            first += [copy(1 + j, me, (*chip, c), src=ins[t]) for j, chip in enumerate(chips)]
            for cp in first:
                cp.start()
            sends.append((mine, first, copy))
        for t in range(n):
            mine, first, copy = sends[t]
            passed = [copy(4 + j, (*chip, c), sibling) for j, chip in enumerate(chips)]
            for j, chip in enumerate(chips):
                copy(1 + j, (*chip, c), me).wait_recv()
                passed[j].start()
            copy(0, sibling, me).wait_recv()
            for j, chip in enumerate(chips):
                copy(4 + j, (*chip, 1 - c), me).wait_recv()
            for cp in first + passed:
                cp.wait_send()
            mine.wait()

    any_spec = pl.BlockSpec(memory_space=pl.ANY)
    res = pl.pallas_call(
        body, name=name, in_specs=[any_spec] * (n + len(deps)), out_specs=[any_spec] * n,
        out_shape=[jax.ShapeDtypeStruct((N_DEV,) + a.shape, a.dtype) for a in arrs],
        scratch_shapes=[pltpu.SemaphoreType.DMA((7 * n,)), pltpu.SemaphoreType.DMA((7 * n,)), pltpu.SemaphoreType.DMA((n,))],
        compiler_params=pltpu.CompilerParams(has_side_effects=True))(*arrs, *deps)
    return list(res)


def ada_modulation(name, c, w_ada):
    n_mod = w_ada.shape[1]

    def exchange(src_ref, dst_ref, send_sems, recv_sems):
        x, y, c_ = _place()
        me = 4 * x + 2 * y + c_
        copies = []
        for k in range(1, N_DEV):
            px, py, pc = x ^ (k >> 2), y ^ ((k >> 1) & 1), c_ ^ (k & 1)
            copies.append(pltpu.make_async_remote_copy(
                src_ref=src_ref, dst_ref=dst_ref.at[me], send_sem=send_sems.at[k - 1], recv_sem=recv_sems.at[k - 1],
                device_id=(px, py, pc), device_id_type=MESH_ID))
        for cp in copies:
            cp.start()
        for cp in copies:
            cp.wait_recv()
        for cp in copies:
            cp.wait_send()
        return me

    def body(c_ref, w_ref, sc_ref, mod_ref, c_all, send_c, recv_c, send_m, recv_m):
        me = exchange(c_ref, c_all, send_c, recv_c)
        c_all[me] = c_ref[...]
        sc = _silu(jnp.concatenate([c_all[p] for p in range(N_DEV)], axis=0))
        sc_ref[...] = sc.astype(sc_ref.dtype)
        mod_ref[me] = _dot(sc, w_ref[...])
        exchange(mod_ref.at[me], mod_ref, send_m, recv_m)

    vmem = pl.BlockSpec(memory_space=pltpu.VMEM)
    return pl.pallas_call(
        body, name=name, in_specs=[vmem, vmem], out_specs=[vmem, vmem],
        out_shape=[jax.ShapeDtypeStruct((N_DEV, c.shape[1]), MXU_DTYPE), jax.ShapeDtypeStruct((N_DEV, N_DEV, n_mod), F32)],
        scratch_shapes=[pltpu.VMEM((N_DEV, 1, c.shape[1]), F32)] + [pltpu.SemaphoreType.DMA((N_DEV - 1,))] * 4,
        compiler_params=pltpu.CompilerParams(has_side_effects=True, vmem_limit_bytes=VMEM_LIMIT))(c, w_ada)


HBM_SPEC = pl.BlockSpec(memory_space=pltpu.HBM)
SEM_SPEC = pl.BlockSpec(memory_space=pltpu.SEMAPHORE)
DATAFLOW = pltpu.SideEffectType.DATAFLOW_SIDE_EFFECTING


def _exchange_copies(srcs, lands, send_sems, recv_sems, gather, first=0):
    x, y, c = _place()
    me = 4 * x + 2 * y + c
    out = []
    for t, (src, land) in enumerate(zip(srcs, lands, strict=True)):
        for k in range(1, N_DEV):
            px, py, pc = x ^ (k >> 2), y ^ ((k >> 1) & 1), c ^ (k & 1)
            sem = 7 * (first + t) + k - 1
            out.append((k, pltpu.make_async_remote_copy(
                src_ref=src if gather else src.at[4 * px + 2 * py + pc],
                dst_ref=land.at[me] if gather else land.at[k - 1],
                send_sem=send_sems.at[sem], recv_sem=recv_sems.at[sem],
                device_id=(px, py, pc), device_id_type=MESH_ID)))
    return out


TREE_DIRECT = (1, 2, 4, 6)
TREE_FORWARDED = (3, 5, 7)


def exchange_start(name, arrs, gather, after=None, tree=False):
    n = len(arrs)
    lands = [lax.empty(((N_DEV,) + a.shape) if gather else ((N_DEV - 1,) + a.shape[1:]), a.dtype) for a in arrs]
    deps = [] if after is None else [after]

    def body(*refs):
        srcs, land_refs = refs[:n], refs[n:2 * n]
        send_sems, recv_sems = refs[2 * n + len(deps)], refs[2 * n + len(deps) + 1]
        token = refs[-1]
        for k, cp in _exchange_copies(srcs, land_refs, send_sems, recv_sems, gather):
            if not tree or k in TREE_DIRECT:
                cp.start()
        token[...] = jnp.zeros_like(token)

    hbm = lambda a: pltpu.HBM(a.shape, a.dtype)
    res = pl.pallas_call(
        body, name=name,
        out_shape=(pltpu.SemaphoreType.DMA((7 * n,)), pltpu.SemaphoreType.DMA((7 * n,)), *[hbm(a) for a in arrs],
                   *[hbm(l) for l in lands], jax.ShapeDtypeStruct((8, 128), F32)),
        in_specs=[HBM_SPEC] * (2 * n) + [pl.BlockSpec(memory_space=pl.ANY)] * len(deps),
        out_specs=(SEM_SPEC, SEM_SPEC, *[HBM_SPEC] * (2 * n), pl.BlockSpec(memory_space=pltpu.VMEM)),
        input_output_aliases={i: 2 + i for i in range(2 * n)},
        compiler_params=pltpu.CompilerParams(has_side_effects=DATAFLOW),
    )(*[pltpu.with_memory_space_constraint(a, pltpu.HBM) for a in arrs + lands], *deps)
    return res[0], res[1], list(res[2:2 + n]), list(res[2 + n:2 + 2 * n]), res[-1]


def exchange_forward(name, started, after, first=0, count=None):
    send_sems, recv_sems, srcs, lands, _ = started
    count = len(srcs) - first if count is None else count
    mine = lands[first:first + count]
    n = len(mine)

    def copies(land_refs, send_ref, recv_ref):
        x, y, c = _place()
        out = []
        for t, land in enumerate(land_refs):
            for k in (2, 4, 6):
                slot = land.at[4 * (x ^ (k >> 2)) + 2 * (y ^ ((k >> 1) & 1)) + c]
                came, goes = 7 * (first + t) + k - 1, 7 * (first + t) + (k ^ 1) - 1
                out.append((
                    pltpu.make_async_remote_copy(src_ref=slot, dst_ref=slot, send_sem=send_ref.at[came], recv_sem=recv_ref.at[came],
                                                 device_id=(x, y, c), device_id_type=MESH_ID),
                    pltpu.make_async_remote_copy(src_ref=slot, dst_ref=slot, send_sem=send_ref.at[goes], recv_sem=recv_ref.at[goes],
                                                 device_id=(x, y, 1 - c), device_id_type=MESH_ID)))
        return out

    def arrived(*refs):
        for came, _ in copies(refs[:n], refs[n], refs[n + 1]):
            came.wait_recv()

    def pass_on(*refs):
        for _, goes in copies(refs[:n], refs[n], refs[n + 1]):
            goes.start()
        refs[-1][...] = jnp.zeros_like(refs[-1])

    hbm = lambda a: pltpu.HBM(a.shape, a.dtype)
    here = pl.pallas_call(
        arrived, name=name + "_arrived", out_shape=tuple(hbm(a) for a in mine),
        in_specs=[HBM_SPEC] * n + [SEM_SPEC, SEM_SPEC, pl.BlockSpec(memory_space=pl.ANY)],
        out_specs=tuple([HBM_SPEC] * n), input_output_aliases={i: i for i in range(n)},
        compiler_params=pltpu.CompilerParams(has_side_effects=DATAFLOW),
    )(*mine, send_sems, recv_sems, after)
    res = pl.pallas_call(
        pass_on, name=name, out_shape=(*[hbm(a) for a in mine], jax.ShapeDtypeStruct((8, 128), F32)),
        in_specs=[HBM_SPEC] * n + [SEM_SPEC, SEM_SPEC],
        out_specs=(*[HBM_SPEC] * n, pl.BlockSpec(memory_space=pltpu.VMEM)), input_output_aliases={i: i for i in range(n)},
        compiler_params=pltpu.CompilerParams(has_side_effects=DATAFLOW),
    )(*here, send_sems, recv_sems)
    lands = lands[:first] + list(res[:n]) + lands[first + count:]
    return (send_sems, recv_sems, srcs, lands, res[-1])


def exchange_wait(name, started, gather, after, first=0, count=None, tree=False):
    send_sems, recv_sems, srcs, lands, _ = started
    count = len(srcs) - first if count is None else count
    srcs, lands = srcs[first:first + count], lands[first:first + count]
    n = len(srcs)

    def body(*refs):
        src_refs, land_refs = refs[:n], refs[n:2 * n]
        copies = _exchange_copies(src_refs, land_refs, refs[2 * n], refs[2 * n + 1], gather, first)
        for _, cp in copies:
            cp.wait_send()
        for k, cp in copies:
            if not tree or k in (1,) + TREE_FORWARDED:
                cp.wait_recv()

    hbm = lambda a: pltpu.HBM(a.shape, a.dtype)
    res = pl.pallas_call(
        body, name=name, out_shape=tuple(hbm(a) for a in srcs + lands),
        in_specs=[HBM_SPEC] * (2 * n) + [SEM_SPEC, SEM_SPEC, pl.BlockSpec(memory_space=pl.ANY)],
        out_specs=tuple([HBM_SPEC] * (2 * n)), input_output_aliases={i: i for i in range(2 * n)},
        compiler_params=pltpu.CompilerParams(has_side_effects=DATAFLOW),
    )(*srcs, *lands, send_sems, recv_sems, after)
    return list(res[:n]), list(res[n:])


def _gather_cols(stack):
    p, k, n = stack.shape
    return stack.transpose(1, 0, 2).reshape(k, p * n)


def _scatter_cols(full):
    k, n = full.shape
    return full.reshape(k, N_DEV, n // N_DEV).transpose(1, 0, 2)


def _gather_rows(stack):
    p, r, n = stack.shape
    return stack.reshape(p * r, n)


def _scatter_rows(full):
    r, n = full.shape
    return full.reshape(N_DEV, r // N_DEV, n)


_IN_NAT = Q_LORA + KV_LORA
TRANSPOSED = ("w_in", "w_q_b", "w_up")
ROWS_APART = ("w_in", "w_conv")


def to_kernel_layout(name, w):
    if name == "w_in":
        z = lambda n: jnp.zeros((n, w.shape[1]), w.dtype)
        return jnp.concatenate([w[:_IN_NAT], z(KPE_LO), w[_IN_NAT:_IN_NAT + ROPE], z(LANES - KPE_LO - ROPE), w[_IN_NAT + ROPE:]], axis=0)
    if name == "w_q_b":
        return jnp.pad(w.reshape(HEADS, NOPE + ROPE, -1), ((0, 0), (0, LANES - NOPE - ROPE), (0, 0))).reshape(HEADS * LANES, -1)
    if name == "w_o":
        mla = jnp.pad(w[:HEADS * NOPE].reshape(HEADS, NOPE, -1), ((0, 0), (LANES - NOPE, 0), (0, 0))).reshape(HEADS * LANES, -1)
        return jnp.concatenate([mla, w[HEADS * NOPE:]], axis=0)
    return w


def from_kernel_layout(name, g):
    if name == "w_in":
        return jnp.concatenate([g[:_IN_NAT], g[P_KPE + KPE_LO:P_KPE + KPE_LO + ROPE], g[P_QD:]], axis=0)
    if name == "w_q_b":
        return g.reshape(HEADS, LANES, -1)[:, :NOPE + ROPE, :].reshape(HEADS * (NOPE + ROPE), -1)
    if name == "w_o":
        mla = g[:HEADS * LANES].reshape(HEADS, LANES, -1)[:, LANES - NOPE:, :].reshape(HEADS * NOPE, -1)
        return jnp.concatenate([mla, g[HEADS * LANES:]], axis=0)
    return g


SMALL_COLS = 1024
SMALL_ROWS = 24
SMALL_AT = {"loss": (0, 0, 1), "b_ada": (1, 0, 6 * D_MODEL), "g_mix_norm": (7, 0, D_MODEL), "g_q_lat": (8, 0, Q_LORA),
            "g_kv_lat": (9, 0, KV_LORA), "g_mla_q_nope": (10, 0, NOPE), "g_mla_q_pe": (10, 128, ROPE),
            "g_mla_k_nope": (10, 256, NOPE), "g_mla_k_pe": (10, 384, ROPE), "g_dil_q": (10, 512, DIL_DIM),
            "g_dil_k": (10, 640, DIL_DIM), "g_ffn_norm": (11, 0, D_MODEL), "b_conv": (12, 0, 2 * D_FF)}
SMALL_PARAMS = tuple(n for n in SMALL_AT if n != "loss")


def _pack_small(values):
    by_row = {}
    for name, (row, off, n) in SMALL_AT.items():
        by_row.setdefault(row, []).append((off, values[name].reshape(-1).astype(F32)))
    out = []
    for row in sorted(by_row):
        pieces, at = [], 0
        for off, v in sorted(by_row[row], key=lambda t: t[0]):
            pieces += [jnp.zeros((off - at,), F32), v]
            at = off + v.shape[0]
        flat = jnp.concatenate(pieces)
        nrows = -(-flat.shape[0] // SMALL_COLS)
        out.append(jnp.pad(flat, (0, nrows * SMALL_COLS - flat.shape[0])).reshape(nrows, SMALL_COLS))
    packed = jnp.concatenate(out, axis=0)
    return jnp.pad(packed, ((0, SMALL_ROWS - packed.shape[0]), (0, 0)))


def _adam(w, g, m, v):
    c1 = 1.0 - ADAM_B1 ** ADAM_STEP
    c2 = 1.0 - ADAM_B2 ** ADAM_STEP
    m2 = ADAM_B1 * m + (1.0 - ADAM_B1) * g
    v2 = ADAM_B2 * v + (1.0 - ADAM_B2) * (g * g)
    return -ADAM_LR * ((m2 / c1) / (jnp.sqrt(v2 / c2) + ADAM_EPS) + ADAM_WD * w), m2, v2


def adamw_small(name, stack, params):
    flat = [a for n in SMALL_PARAMS for a in params[n]]

    def body(stack_ref, *refs):
        ins, outs = refs[:len(flat)], refs[len(flat):]
        g_all = stack_ref[0]
        for p in range(1, N_DEV):
            g_all = g_all + stack_ref[p]
        outs[0][...] = g_all[0:1, 0:1]
        for i, pname in enumerate(SMALL_PARAMS):
            row, off, n = SMALL_AT[pname]
            w_ref, m_ref, v_ref = ins[3 * i:3 * i + 3]
            go_ref, d_ref, mo_ref, vo_ref = outs[1 + 4 * i:5 + 4 * i]
            for c0 in range(0, n, SMALL_COLS):
                cn = min(SMALL_COLS, n - c0)
                r = row + c0 // SMALL_COLS
                g = g_all[r:r + 1, off:off + cn]
                cols = (slice(None), slice(c0, c0 + cn))
                d, m2, v2 = _adam(w_ref[cols], g, m_ref[cols], v_ref[cols])
                go_ref[cols], d_ref[cols], mo_ref[cols], vo_ref[cols] = g, d, m2, v2

    whole = lambda a: pl.BlockSpec(a.shape, lambda: (0,) * a.ndim)
    out_shape = [jax.ShapeDtypeStruct((1, 1), F32)] + [jax.ShapeDtypeStruct(a.shape, F32) for n in SMALL_PARAMS for a in params[n][:1] * 4]
    res = pl.pallas_call(body, name=name, in_specs=[whole(stack)] + [whole(a) for a in flat],
                         out_specs=[pl.BlockSpec(s.shape, lambda s=s: (0,) * len(s.shape)) for s in out_shape],
                         out_shape=out_shape, compiler_params=_params())(stack, *flat)
    return res[0], {n: res[1 + 4 * i:5 + 4 * i] for i, n in enumerate(SMALL_PARAMS)}


def _local_step(x, pos, mod, target, w, fetch, emit, halfway=lambda after: None):
    S = SEQ
    sh1, sc1, g1, sh2, sc2, g2 = [mod[:, i * D_MODEL:(i + 1) * D_MODEL] for i in range(6)]
    zeros = lambda n: jnp.zeros((1, n), F32)
    g_q = jnp.concatenate([w["g_mla_q_nope"], w["g_mla_q_pe"], zeros(LANES - NOPE - ROPE)], axis=1)
    g_k = jnp.concatenate([w["g_mla_k_nope"], zeros(LANES - NOPE)], axis=1)
    g_kpe = jnp.concatenate([zeros(KPE_LO), w["g_mla_k_pe"], zeros(LANES - KPE_LO - ROPE)], axis=1)
    g_dq = jnp.concatenate([w["g_dil_q"]] * 2, axis=1)
    g_dk = jnp.concatenate([w["g_dil_k"]] * 2, axis=1)
    b_conv = w["b_conv"]

    def inv_freq(d):
        return jnp.power(ROPE_THETA, -2.0 * jnp.arange(d // 2, dtype=F32) / d)

    n_m, n_d = ROPE // 2, DIL_DIM // 2
    freqs = jnp.concatenate([inv_freq(ROPE), inv_freq(DIL_DIM), jnp.zeros((LANES - n_m - n_d,), F32)]).reshape(1, LANES)

    def tables_fn(rows, params):
        (p,), (f,) = rows, params
        c, s = jnp.cos(p * f), jnp.sin(p * f)
        one, zero = jnp.ones_like(c), jnp.zeros_like(c)
        mla = lambda t, fill: jnp.concatenate([fill[:, :KPE_LO], t[:, :n_m], t[:, :n_m], fill[:, :LANES - KPE_LO - ROPE]], axis=1)
        dil = lambda t: jnp.concatenate([t[:, n_m:n_m + n_d]] * 4, axis=1)
        return [mla(c, one), mla(s, zero), dil(c), dil(s)], []

    cos_m, sin_m, cos_d, sin_d = rowwise("rope_tables", tables_fn, [pos], [freqs], [(LANES, F32)] * 4)
    tables = [cos_m, sin_m, cos_d, sin_d]
    H_M, H_D = ROPE // 2, DIL_DIM // 2

    def ln1_fn(rows, params):
        (xv,), (g, sc, sh) = rows, params
        y, _, _ = _rms(xv, g)
        return [y * (1.0 + sc) + sh], []

    (h,) = rowwise("ln1_fwd", ln1_fn, [x], [w["g_mix_norm"], sc1, sh1], [(D_MODEL, MXU_DTYPE)], dep=sin_d)
    w_in = fetch("w_in", h)

    def proj_fn(rows, params):
        (hv, cm, sm, cd, sd), (w_t, gq, gkv, gkp, gdq, gdk) = rows, params
        pv = _dot(hv, w_t, "nt")
        kper = _rope(_grms(pv[:, P_KPE:P_QD], gkp, KPE_GROUPS)[0], cm, sm, H_M)
        qd = [_rope(_grms(c, gdq, DIL_GROUPS)[0], cd, sd, H_D) for c in _chunks(pv[:, P_QD:P_KD])]
        kd = [_rope(_grms(c, gdk, DIL_GROUPS)[0], cd, sd, H_D) for c in _chunks(pv[:, P_KD:P_VD])]
        return [pv, _rms(pv[:, P_QLAT:P_KVLAT], gq)[0], _rms(pv[:, P_KVLAT:P_KPE], gkv)[0], kper,
                jnp.concatenate(qd, axis=1), jnp.concatenate(kd, axis=1)], []

    post_params = [w["g_q_lat"], w["g_kv_lat"], g_kpe, g_dq, g_dk]
    proj, qln, kvn, kper, qd_r, kd_r = rowwise(
        "proj_fwd", proj_fn, [h] + tables, [w_in] + post_params,
        [(P_END, F32), (Q_LORA, MXU_DTYPE), (KV_LORA, MXU_DTYPE), (LANES, MXU_DTYPE)] + [(DIL_WIDTH, F32)] * 2, tm=256)
    w_q_b, w_kv_b = fetch("w_q_b", qln), fetch("w_kv_b", kvn)

    def mla_proj_fn(rows, params):
        (qlv, kvlv, kp, cm, sm), (wq_t, wkv, gq, gk) = rows, params
        qv, kvv = _dot(qlv, wq_t, "nt"), _dot(kvlv, wkv)
        value_lanes = _lane(kp.shape) >= NOPE
        qs, ks, vs = [], [], []
        for qc, kc in zip(_chunks(qv), _chunks(kvv), strict=True):
            qs.append(_rope(_grms(qc, gq, Q_GROUPS)[0], cm, sm, H_M))
            ks.append(_grms(kc, gk, K_GROUPS)[0] + kp)
            vs.append(jnp.where(value_lanes, kc, 0.0))
        return [qv, kvv] + [jnp.concatenate(t, axis=1) for t in (qs, ks, vs)], []

    q, kv, q_mla, k_mla, v_mla = rowwise(
        "mla_proj", mla_proj_fn, [qln, kvn, kper, cos_m, sin_m], [w_q_b, w_kv_b, g_q, g_k],
        [(HEADS * LANES, F32)] * 2 + [(HEADS * LANES, MXU_DTYPE)] * 3, tm=256)
    mla_scale = (NOPE + ROPE) ** -0.5
    o_cat, lse_mla = mla_fwd("mla_fwd", q_mla, k_mla, v_mla, mla_scale)
    passed = halfway(lse_mla)

    band = [band_fwd(f"band{dil}_fwd", qd_r, kd_r, proj, dil, dep=passed) for dil in DILATIONS]
    o_cat, lse_mix = combine_fwd("dil_combine", [b[0] for b in band], [b[1] for b in band], o_cat)
    w_o = fetch("w_o", o_cat)

    def mid_fn(rows, params):
        (ov, xv), (w_out, gate1, g, sc, sh) = rows, params
        mx = _dot(ov, w_out)
        x1 = xv + gate1 * mx
        y, _, _ = _rms(x1, g)
        return [mx, x1, y * (1.0 + sc) + sh], []

    mix, x1, h2 = rowwise("mix_fwd", mid_fn, [o_cat, x], [w_o, g1, w["g_ffn_norm"], sc2, sh2],
                          [(D_MODEL, F32), (D_MODEL, F32), (D_MODEL, MXU_DTYPE)], tm=256)
    w_up, w_conv, w_down = fetch("w_up", h2), fetch("w_conv", h2), fetch("w_down", h2)
    dn, up = ffn_fwd("ffn_fwd", h2, w_up, w_conv, b_conv, w_down)

    def final_fn(rows, params):
        (x1v, dnv, tgt), (gate2,) = rows, params
        r = x1v + gate2 * dnv - tgt
        dy = r * (1.0 / D_MODEL)
        loss = jnp.sum(_colsum(r * r), axis=-1, keepdims=True) * (0.5 / D_MODEL)
        return [dy, gate2 * dy], [loss, _colsum(dy * dnv)]

    dy, d_dn, loss, dg2 = rowwise("loss_head", final_fn, [x1, dn, target], [g2], [(D_MODEL, F32), (D_MODEL, MXU_DTYPE)],
                                  [1, D_MODEL])
    dh2, g_up, g_down, g_w_conv, g_b_conv = ffn_bwd("ffn_bwd", h2, up, w_up, w_conv, b_conv, d_dn, w_down)
    emit("w_down", g_down)
    emit("w_conv", g_w_conv)
    sent = emit("w_up", g_up)

    def mid_bwd_fn(rows, params):
        (dh2v, dyv, x1v, mx), (gate1, g, sc) = rows, params
        yn, n, rstd = _rms(x1v, g)
        dx_n, dg = _rms_bwd(dh2v * (1.0 + sc), n, rstd, g)
        dx1 = dyv + dx_n
        return [dx1, gate1 * dx1], [dg, _colsum(dh2v * yn), _colsum(dh2v), _colsum(dx1 * mx)]

    dx1, dmix, dg_ffn, dsc2, dsh2, dg1 = rowwise(
        "mid_bwd", mid_bwd_fn, [dh2, dy, x1, mix], [g1, w["g_ffn_norm"], sc2], [(D_MODEL, F32), (D_MODEL, MXU_DTYPE)],
        [D_MODEL] * 4, dep=sent)

    sent = emit("w_o", matmul("mix_wgrad", o_cat, dmix, "tn", tm=512, out_dtype=MXU_DTYPE))
    do_cat = matmul("mix_dgrad", dmix, w_o, "nt", tm=512, dep=sent)
    dband = None
    for dil, b in zip(DILATIONS, band):
        dband = band_bwd(f"band{dil}_bwd", qd_r, kd_r, proj, b[1], lse_mix, o_cat, do_cat, dil, before=dband)
    dq_mla, dkv_mla, dkper = mla_bwd("mla_bwd", q_mla, k_mla, v_mla, o_cat, do_cat, lse_mla, mla_scale)

    def mla_prep_bwd_fn(rows, params):
        (dqv, dkvv, qv, kvv, cm, sm), (gq, gk) = rows, params
        nope_lanes = _lane(cm.shape) < NOPE
        dqs, dkvs, dgq, dgk = [], [], 0.0, 0.0
        for dqc, dkc, qc, kc in zip(_chunks(dqv), _chunks(dkvv), _chunks(qv), _chunks(kvv), strict=True):
            _, n, rstd = _grms(qc, gq, Q_GROUPS)
            dx, dg = _grms_bwd(_rope_bwd(dqc, cm, sm, H_M), n, rstd, gq, Q_GROUPS)
            dqs.append(dx)
            dgq = dgq + dg
            _, n, rstd = _grms(kc, gk, K_GROUPS)
            dx, dg = _grms_bwd(dkc, n, rstd, gk, K_GROUPS)
            dkvs.append(jnp.where(nope_lanes, dx, dkc))
            dgk = dgk + dg
        return [jnp.concatenate(dqs, axis=1), jnp.concatenate(dkvs, axis=1)], [dgq, dgk]

    dq, dkv, dg_q, dg_k = rowwise("mla_prep_bwd", mla_prep_bwd_fn, [dq_mla, dkv_mla, q, kv, cos_m, sin_m], [g_q, g_k],
                                  [(HEADS * LANES, MXU_DTYPE)] * 2, [LANES, LANES], tm=256)
    emit("w_q_b", matmul("q_wgrad", dq, qln, "tn", out_dtype=MXU_DTYPE))
    emit("w_kv_b", matmul("kv_wgrad", kvn, dkv, "tn", out_dtype=MXU_DTYPE))

    def pre_bwd_fn(rows, params):
        dqv, dkvv, dkp, dqd_, dkd_, dvd_, pv, cm, sm, cd, sd = rows
        wq_t, wkv, gq, gkv, gkp, gdq, gdk = params
        dql, dkvl = _dot(dqv, wq_t), _dot(dkvv, wkv, "nt")
        r_q = _norm_bwd(dql, pv[:, P_QLAT:P_KVLAT], gq)
        r_kv = _norm_bwd(dkvl, pv[:, P_KVLAT:P_KPE], gkv)
        _, n, rstd = _grms(pv[:, P_KPE:P_QD], gkp, KPE_GROUPS)
        r_kp = _grms_bwd(_rope_bwd(dkp, cm, sm, H_M), n, rstd, gkp, KPE_GROUPS)
        outs, dgs = [r_q[0], r_kv[0], r_kp[0]], []
        for dval, lo, g in ((dqd_, P_QD, gdq), (dkd_, P_KD, gdk)):
            dg_sum = 0.0
            for dc, xc in zip(_chunks(dval), _chunks(pv[:, lo:lo + DIL_WIDTH]), strict=True):
                _, n, rstd = _grms(xc, g, DIL_GROUPS)
                dx, dg = _grms_bwd(_rope_bwd(dc, cd, sd, H_D), n, rstd, g, DIL_GROUPS)
                outs.append(dx)
                dg_sum = dg_sum + dg
            dgs.append(dg_sum)
        return [jnp.concatenate(outs + [dvd_], axis=1)], [r_q[1], r_kv[1], r_kp[1]] + dgs

    dproj, dg_q_lat, dg_kv_lat, dg_kpe, dg_dq, dg_dk = rowwise(
        "proj_pre_bwd", pre_bwd_fn,
        [dq, dkv, dkper] + list(dband) + [proj] + tables, [w_q_b, w_kv_b] + post_params,
        [(P_END, MXU_DTYPE)], [Q_LORA, KV_LORA, LANES, LANES, LANES], tm=256)
    sent = emit("w_in", matmul("proj_wgrad", dproj, h, "tn", tn=512, out_dtype=MXU_DTYPE))

    def ln1_bwd_fn(rows, params):
        (dpv, dres, xv), (w_t, g, sc) = rows, params
        dhv = _dot(dpv, w_t)
        yn, n, rstd = _rms(xv, g)
        dx_n, dg = _rms_bwd(dhv * (1.0 + sc), n, rstd, g)
        return [dres + dx_n], [dg, _colsum(dhv * yn), _colsum(dhv)]

    grad_x, dg_mix, dsc1, dsh1 = rowwise("proj_dgrad", ln1_bwd_fn, [dproj, dx1, x], [w_in, w["g_mix_norm"], sc1],
                                         [(D_MODEL, F32)], [D_MODEL] * 3, tm=256, dep=sent)
    dmod = jnp.concatenate([dsh1, dsc1, dg1, dsh2, dsc2, dg2], axis=-1)
    small = {"loss": loss, "b_ada": dmod, "g_mix_norm": dg_mix, "g_q_lat": dg_q_lat, "g_kv_lat": dg_kv_lat,
             "g_mla_q_nope": dg_q[:, :NOPE], "g_mla_q_pe": dg_q[:, NOPE:NOPE + ROPE], "g_mla_k_nope": dg_k[:, :NOPE],
             "g_mla_k_pe": dg_kpe[:, KPE_LO:KPE_LO + ROPE], "g_dil_q": dg_dq[:, :DIL_DIM] + dg_dq[:, DIL_DIM:],
             "g_dil_k": dg_dk[:, :DIL_DIM] + dg_dk[:, DIL_DIM:], "g_ffn_norm": dg_ffn,
             "b_conv": g_b_conv}
    return grad_x, small


COL_SHARDED = ("w_kv_b", "w_conv")
ROW_SHARDED = ("w_o", "w_down") + TRANSPOSED
ADAM_TILE = {"w_ada": 256, "w_up": 176, "w_down": 176}
GATHER_GROUPS = (("w_in",), ("w_q_b", "w_kv_b"), ("w_o",), ("w_up", "w_conv", "w_down"))
START_STAGES = ((0, 1), (2, 3))
FORWARD_STAGES = ((0, 1), (2,), (3,))
FORWARD_WITH = {"w_o": 2}
SCATTER_GROUPS = (("w_down", "w_conv", "w_up"), ("w_o",), ("w_q_b", "w_kv_b", "w_in"))
OUT_WEIGHTS = ("w_ada", "b_ada", "g_mix_norm", "w_in", "g_q_lat", "w_q_b", "g_kv_lat", "w_kv_b", "g_mla_q_nope", "g_mla_q_pe",
               "g_mla_k_nope", "g_mla_k_pe", "g_dil_q", "g_dil_k", "w_o", "g_ffn_norm", "w_up", "w_conv", "b_conv", "w_down")


def kernel(x, c, positions, w_ada, b_ada, g_mix_norm, w_in, g_q_lat, w_q_b, g_kv_lat, w_kv_b, g_mla_q_nope, g_mla_q_pe, g_mla_k_nope, g_mla_k_pe, g_dil_q, g_dil_k, w_o, g_ffn_norm, w_up, w_conv, b_conv, w_down, loss_target, m_w_ada, m_b_ada, m_g_mix_norm, m_w_in, m_g_q_lat, m_w_q_b, m_g_kv_lat, m_w_kv_b, m_g_mla_q_nope, m_g_mla_q_pe, m_g_mla_k_nope, m_g_mla_k_pe, m_g_dil_q, m_g_dil_k, m_w_o, m_g_ffn_norm, m_w_up, m_w_conv, m_b_conv, m_w_down, v_w_ada, v_b_ada, v_g_mix_norm, v_w_in, v_g_q_lat, v_w_q_b, v_g_kv_lat, v_w_kv_b, v_g_mla_q_nope, v_g_mla_q_pe, v_g_mla_k_nope, v_g_mla_k_pe, v_g_dil_q, v_g_dil_k, v_w_o, v_g_ffn_norm, v_w_up, v_w_conv, v_b_conv, v_w_down):
    args = dict(locals())
    xi, yi, ci = _place()
    me = 4 * xi + 2 * yi + ci
    def local(prefix, n):
        a = args[prefix + n]
        if n in ROWS_APART:
            return jnp.transpose(a, (2, 0, 1) if n in TRANSPOSED else (1, 0, 2))
        return a[0].T if n in TRANSPOSED else a[0]

    def as_output(n, r):
        if n in ROWS_APART:
            return jnp.transpose(r, (1, 2, 0) if n in TRANSPOSED else (1, 0, 2))
        return (r.T if n in TRANSPOSED else r)[None]

    shard = {n: local("", n) for n in COL_SHARDED + ROW_SHARDED + ("w_ada",)}
    flat = lambda n, a: a.reshape(a.shape[0], a.shape[-1]) if n in ROWS_APART else a
    small_w = {n: args[n] for n in SMALL_PARAMS}

    sc_all, mod_all = ada_modulation("ada_mod", c, shard["w_ada"])

    payload = {n: flat(n, shard[n]) if n == "w_conv" else flat(n, shard[n]).astype(MXU_DTYPE) for n in COL_SHARDED + ROW_SHARDED}
    start_order = [[n for i in groups for n in GATHER_GROUPS[i]] for groups in START_STAGES]
    exchange_of = lambda i: [e for e, groups in enumerate(START_STAGES) if i in groups][0]
    start_stage = lambda e, after: exchange_start(f"gather_start{e}", [payload[n] for n in start_order[e]], gather=True,
                                                  after=after, tree=True)
    gathered = {0: start_stage(0, mod_all)}
    after_start = gathered[0][-1]
    full, forwarded = {}, set()

    def forward(stage, after):
        e = exchange_of(FORWARD_STAGES[stage][0])
        if stage not in forwarded:
            forwarded.add(stage)
            first = start_order[e].index(GATHER_GROUPS[FORWARD_STAGES[stage][0]][0])
            count = sum(len(GATHER_GROUPS[i]) for i in FORWARD_STAGES[stage])
            gathered[e] = exchange_forward(f"gather_forward{stage}", gathered[e], after, first, count)
            if e + 1 < len(START_STAGES) and e + 1 not in gathered:
                gathered[e + 1] = start_stage(e + 1, gathered[e][-1])
        return gathered[e][-1]

    def fetch(name, after):
        if name not in full:
            (i, grp), = [(i, grp) for i, grp in enumerate(GATHER_GROUPS) if name in grp]
            (stage,) = [s for s, groups in enumerate(FORWARD_STAGES) if i in groups]
            forward(stage, after)
            if name in FORWARD_WITH:
                forward(FORWARD_WITH[name], after)
            e = exchange_of(i)
            behind = gathered[e + 1][-1] if e + 1 in gathered else after
            srcs, lands = exchange_wait(f"gather{i}_wait", gathered[e], True, behind, start_order[e].index(grp[0]), len(grp), tree=True)
            for n, src, land in zip(grp, srcs, lands, strict=True):
                stack = lax.dynamic_update_index_in_dim(land, src, me, 0)
                full[n] = to_kernel_layout(n, _gather_cols(stack) if n in COL_SHARDED else _gather_rows(stack))
        return full[name]

    mod_row = lax.dynamic_index_in_dim(mod_all, me, axis=1, keepdims=False).reshape(1, 6 * D_MODEL)
    (mod,) = rowwise("ada_bias", lambda rows, params: ([rows[0] + rows[1]], []), [mod_row, b_ada], [], [(6 * D_MODEL, F32)],
                     dep=after_start)

    own, pending, scatters = {}, {}, {}

    def emit(name, grad):
        grad = from_kernel_layout(name, grad)
        parts = _scatter_cols(grad) if name in COL_SHARDED else _scatter_rows(grad)
        own[name] = lax.dynamic_index_in_dim(parts, me, 0, keepdims=False)
        pending[name] = parts
        for i, grp in enumerate(SCATTER_GROUPS):
            if name == grp[-1]:
                scatters[i] = exchange_start(f"scatter{i}_start", [pending[n] for n in grp], gather=False)
                return scatters[i][-1]
        return None

    pos = positions.reshape(SEQ, 1).astype(F32)
    grad_x, small = _local_step(x[0], pos, mod, loss_target[0], small_w, fetch, emit, halfway=lambda after: forward(1, after))

    res, done = {}, grad_x
    for i, grp in enumerate(SCATTER_GROUPS):
        _, lands = exchange_wait(f"scatter{i}_wait", scatters[i], False, done)
        for n, land in zip(grp, lands, strict=True):
            res[n] = adamw(f"adamw_{n}", shard[n], [own[n], land], local("m_", n), local("v_", n), ADAM_TILE.get(n))
            done = res[n][0]
            res[n] = [as_output(n, r) for r in res[n]]
    (small_all,) = all_gather("gather_small", [_pack_small(small)], after=done)
    loss, small_res = adamw_small("adamw_small", small_all, {n: (args[n], args["m_" + n], args["v_" + n]) for n in SMALL_PARAMS})
    row, _, n_mod = SMALL_AT["b_ada"]
    dmod_all = small_all[:, row:row + n_mod // SMALL_COLS, :].reshape(N_DEV, n_mod)
    dmod_mine = lax.dynamic_slice_in_dim(dmod_all, me * (6 * D_MODEL // N_DEV), 6 * D_MODEL // N_DEV, axis=1)
    g_w_ada = matmul("ada_wgrad", sc_all, dmod_mine, "tn")
    res["w_ada"] = [r[None] for r in adamw("adamw_w_ada", shard["w_ada"], [g_w_ada], m_w_ada[0], v_w_ada[0], ADAM_TILE["w_ada"])]

    def leaf(kind, n):
        return res[n][kind] if n in res else small_res[n][kind]

    return (loss.reshape(()), grad_x[None], *[leaf(k, n) for k in range(4) for n in OUT_WEIGHTS])
```

```python
import jax
import jax.numpy as jnp
from jax import lax
from jax.experimental import pallas as pl
from jax.experimental.pallas import tpu as pltpu

F32 = jnp.float32
MXU_DTYPE = jnp.bfloat16

N_DEV = 8
D_MODEL = 1024
SEQ = 2048
HEADS = 8
NOPE = 64
ROPE = 32
Q_LORA = 512
KV_LORA = 256
DIL_DIM = 64
DIL_WIDTH = HEADS * DIL_DIM
DILATIONS = (1, 4, 16)
SPAN = 128
D_FF = 2816
LANES = 128
SUBLANES = 8
ROPE_THETA = 10000.0
EPS = 1e-6
NEG_INF = -1e30
ADAM_LR, ADAM_B1, ADAM_B2, ADAM_EPS, ADAM_WD, ADAM_STEP = 0.001, 0.9, 0.999, 1e-08, 0.01, 10
VMEM_LIMIT = 56 * 1024 * 1024
MESH_ID = pl.DeviceIdType.MESH

P_QLAT, P_KVLAT, P_KPE, P_QD, P_KD, P_VD, P_END = 0, 512, 768, 896, 1408, 1920, 2432
KPE_LO = 64
MIX_IN = HEADS * LANES + DIL_WIDTH


def _params(**kw):
    return pltpu.CompilerParams(vmem_limit_bytes=VMEM_LIMIT, **kw)


def rowwise(name, fn, rows, params, out_rows, out_accs=(), tm=512, dep=None):
    deps = [] if dep is None else [dep]
    rows = [r if isinstance(r, tuple) else (r, r.shape[1], 0) for r in rows]
    R = rows[0][0].shape[0]
    tm = min(tm, R)
    steps = R // tm
    assert steps * tm == R
    in_specs = []
    for a, width, cb in rows:
        ri = a.shape[0]
        per = ri // tm
        assert per * tm == ri
        if ri == R:
            in_specs.append(pl.BlockSpec((tm, width), lambda i, cb=cb: (i, cb)))
        else:
            in_specs.append(pl.BlockSpec((tm, width), lambda i, per=per, cb=cb: (i % per, cb)))
    for p in params:
        in_specs.append(pl.BlockSpec(p.shape, lambda i: (0,) * p.ndim))
    in_specs += [pl.BlockSpec(memory_space=pl.ANY)] * len(deps)
    out_shape = [jax.ShapeDtypeStruct((R, d), dt) for d, dt in out_rows]
    out_specs = [pl.BlockSpec((tm, d), lambda i: (i, 0)) for d, _ in out_rows]
    out_shape += [jax.ShapeDtypeStruct((1, n), F32) for n in out_accs]
    out_specs += [pl.BlockSpec((1, n), lambda i: (0, 0)) for n in out_accs]
    nr, npar, no, na = len(rows), len(params), len(out_rows), len(out_accs)

    def body(*refs):
        rvals = [r[...] for r in refs[:nr]]
        pvals = [r[...] for r in refs[nr:nr + npar]]
        outs, accs = fn(rvals, pvals)
        first_out = nr + npar + len(deps)
        for ref, v in zip(refs[first_out:first_out + no], outs, strict=True):
            ref[...] = v.astype(ref.dtype)
        if na:
            acc_refs = refs[first_out + no:]
            i = pl.program_id(0)

            @pl.when(i == 0)
            def _():
                for ref, v in zip(acc_refs, accs, strict=True):
                    ref[...] = v

            @pl.when(i > 0)
            def _():
                for ref, v in zip(acc_refs, accs, strict=True):
                    ref[...] += v

    res = pl.pallas_call(body, name=name, grid=(steps,), in_specs=in_specs, out_specs=out_specs,
                         out_shape=out_shape, compiler_params=_params())(*[r[0] for r in rows], *params, *deps)
    return list(res)


_DIMS = {"nn": ((1,), (0,)), "nt": ((1,), (1,)), "tn": ((0,), (0,))}


def _dot(a, b, mode="nn"):
    return lax.dot_general(a.astype(MXU_DTYPE), b.astype(MXU_DTYPE), (_DIMS[mode], ((), ())),
                           preferred_element_type=F32)


def matmul(name, a, b, mode, tm=None, tn=None, tk=None, out_dtype=F32, dep=None):
    if mode == "tn":
        K, M = a.shape
    else:
        M, K = a.shape
    N = b.shape[0] if mode == "nt" else b.shape[1]
    tm, tn, tk = tm or M, tn or N, tk or K
    nm, nn, nk = M // tm, N // tn, K // tk
    assert nm * tm == M and nn * tn == N and nk * tk == K
    a_spec = pl.BlockSpec((tk, tm), lambda i, j, k: (k, i)) if mode == "tn" else pl.BlockSpec((tm, tk), lambda i, j, k: (i, k))
    b_spec = pl.BlockSpec((tn, tk), lambda i, j, k: (j, k)) if mode == "nt" else pl.BlockSpec((tk, tn), lambda i, j, k: (k, j))
    deps = [] if dep is None else [dep]

    def body(a_ref, b_ref, *rest):
        o_ref, scratch = rest[len(deps)], rest[len(deps) + 1:]
        p = _dot(a_ref[...], b_ref[...], mode)
        if nk == 1:
            o_ref[...] = p.astype(o_ref.dtype)
        else:
            acc = scratch[0]
            k = pl.program_id(2)

            @pl.when(k == 0)
            def _():
                acc[...] = p

            @pl.when(k > 0)
            def _():
                acc[...] += p

            @pl.when(k == nk - 1)
            def _():
                o_ref[...] = acc[...].astype(o_ref.dtype)

    return pl.pallas_call(
        body, name=name, grid=(nm, nn, nk), in_specs=[a_spec, b_spec] + [pl.BlockSpec(memory_space=pl.ANY)] * len(deps),
        out_specs=pl.BlockSpec((tm, tn), lambda i, j, k: (i, j)),
        out_shape=jax.ShapeDtypeStruct((M, N), out_dtype),
        scratch_shapes=[pltpu.VMEM((tm, tn), F32)] if nk > 1 else [],
        compiler_params=_params())(a, b, *deps)


def _rms(x, g):
    rstd = lax.rsqrt(jnp.mean(x * x, axis=-1, keepdims=True) + EPS)
    n = x * rstd
    return n * g, n, rstd


def _rms_bwd(dy, n, rstd, g):
    dg = jnp.sum(dy * n, axis=0, keepdims=True)
    dn = dy * g
    dx = rstd * (dn - n * jnp.mean(dn * n, axis=-1, keepdims=True))
    return dx, dg


def _norm_bwd(dy, x, g):
    _, n, rstd = _rms(x, g)
    return _rms_bwd(dy, n, rstd, g)


def _colsum(v):
    return jnp.sum(v, axis=0, keepdims=True)


def _silu(x):
    return x * (1.0 / (1.0 + jnp.exp(-x)))


def _lane(shape):
    return lax.broadcasted_iota(jnp.int32, shape, 1)


def _group_mean(v, groups):
    i = lax.broadcasted_iota(jnp.int32, (LANES, LANES), 0)
    j = lax.broadcasted_iota(jnp.int32, (LANES, LANES), 1)
    g = jnp.zeros((LANES, LANES), F32)
    for lo, hi in groups:
        g = jnp.where((i >= lo) & (i < hi) & (j >= lo) & (j < hi), 1.0 / (hi - lo), g)
    head = v.astype(MXU_DTYPE)
    return _dot(head, g) + _dot(v - head.astype(F32), g)


def _in_groups(shape, groups):
    lane = _lane(shape)
    m = jnp.zeros(shape, jnp.bool_)
    for lo, hi in groups:
        m = m | ((lane >= lo) & (lane < hi))
    return m


def _grms(x, g, groups):
    rstd = lax.rsqrt(_group_mean(x * x, groups) + EPS)
    n = jnp.where(_in_groups(x.shape, groups), x * rstd, 0.0)
    return n * g, n, rstd


def _grms_bwd(dy, n, rstd, g, groups):
    dn = dy * g
    return rstd * (dn - n * _group_mean(dn * n, groups)), _colsum(dy * n)


def _rot(x, half, transpose=False):
    first = (_lane(x.shape) % (2 * half)) < half
    up = pltpu.roll(x, LANES - half, axis=1)
    down = pltpu.roll(x, half, axis=1)
    return jnp.where(first, up, -down) if transpose else jnp.where(first, -up, down)


def _rope(x, cos, sin, half):
    return x * cos + _rot(x, half) * sin


def _rope_bwd(dy, cos, sin, half):
    return dy * cos + _rot(dy * sin, half, transpose=True)


def _chunks(x):
    return [x[:, i:i + LANES] for i in range(0, x.shape[1], LANES)]


Q_GROUPS = ((0, NOPE), (NOPE, NOPE + ROPE))
K_GROUPS = ((0, NOPE),)
KPE_GROUPS = ((KPE_LO, KPE_LO + ROPE),)
DIL_GROUPS = ((0, DIL_DIM), (DIL_DIM, 2 * DIL_DIM))


def _col(width, rows=SEQ):
    return pl.BlockSpec((rows, width), lambda h: (0, h))


def _causal_tail(s, tq, fill):
    diag = s[:, s.shape[1] - tq:]
    keep = lax.broadcasted_iota(jnp.int32, diag.shape, 1) <= lax.broadcasted_iota(jnp.int32, diag.shape, 0)
    diag = jnp.where(keep, diag, fill)
    return diag if s.shape[1] == tq else jnp.concatenate([s[:, :s.shape[1] - tq], diag], axis=1)


def mla_fwd(name, q, k, v, scale, tq=256):
    S = q.shape[0]

    def body(q_ref, k_ref, v_ref, o_ref, lse_ref):
        nb = S // tq
        blk = lambda i: slice(i * tq, (i + 1) * tq)

        def scores(i):
            return _dot(q_ref[blk(i), :], k_ref[:(i + 1) * tq, :], "nt")

        def softmax(i, s):
            s = _causal_tail(s * scale, tq, NEG_INF)
            m = jnp.max(s, axis=-1, keepdims=True)
            e = jnp.exp(s - m)
            l = jnp.sum(e, axis=-1, keepdims=True)
            lse_ref[0, blk(i), :] = m + jnp.log(l)
            return (e * (1.0 / l)).astype(MXU_DTYPE)

        def weighted(i, p):
            o_ref[blk(i), :] = _dot(p, v_ref[:(i + 1) * tq, :])

        s, p_prev = scores(0), None
        for i in range(nb):
            s_next = scores(i + 1) if i + 1 < nb else None
            if p_prev is not None:
                weighted(i - 1, p_prev)
            p_prev, s = softmax(i, s), s_next
        weighted(nb - 1, p_prev)

    return pl.pallas_call(
        body, name=name, grid=(HEADS,), in_specs=[_col(LANES)] * 3,
        out_specs=[_col(LANES), pl.BlockSpec((1, S, 1), lambda h: (h, 0, 0))],
        out_shape=[jax.ShapeDtypeStruct((S, MIX_IN), F32), jax.ShapeDtypeStruct((HEADS, S, 1), F32)],
        compiler_params=_params())(q, k, v)


def mla_bwd(name, q, k, v, o, do, lse, scale, tq=256):
    S = q.shape[0]

    def body(q_ref, k_ref, v_ref, o_ref, do_ref, lse_ref, dq_ref, dkv_ref, dkpe_ref, dk_acc, dv_acc):
        dk_acc[...] = jnp.zeros_like(dk_acc)
        dv_acc[...] = jnp.zeros_like(dv_acc)
        for i in range(S // tq):
            kext = (i + 1) * tq
            blk = slice(i * tq, kext)
            qi, kk, vv = q_ref[blk, :], k_ref[:kext, :], v_ref[:kext, :]
            doi = do_ref[blk, :]
            s = _causal_tail(_dot(qi, kk, "nt") * scale, tq, NEG_INF)
            p = jnp.exp(s - lse_ref[0, blk, :])
            dp = _dot(doi, vv, "nt")
            delta = jnp.sum(doi * o_ref[blk, :], axis=-1, keepdims=True)
            ds = p * (dp - delta) * scale
            dq_ref[blk, :] = _dot(ds, kk)
            dk_acc[:kext, :] += _dot(ds, qi, "tn")
            dv_acc[:kext, :] += _dot(p, doi, "tn")
        dk = dk_acc[...]
        lane = _lane(dk.shape)
        dkv_ref[...] = jnp.where(lane < NOPE, dk, 0.0) + dv_acc[...]
        dkpe = jnp.where((lane >= KPE_LO) & (lane < KPE_LO + ROPE), dk, 0.0)
        h = pl.program_id(0)

        @pl.when(h == 0)
        def _():
            dkpe_ref[...] = dkpe

        @pl.when(h > 0)
        def _():
            dkpe_ref[...] += dkpe

    return pl.pallas_call(
        body, name=name, grid=(HEADS,),
        in_specs=[_col(LANES)] * 5 + [pl.BlockSpec((1, S, 1), lambda h: (h, 0, 0))],
        out_specs=[_col(LANES), _col(LANES), pl.BlockSpec((S, LANES), lambda h: (0, 0))],
        out_shape=[jax.ShapeDtypeStruct((S, HEADS * LANES), F32), jax.ShapeDtypeStruct((S, HEADS * LANES), F32),
                   jax.ShapeDtypeStruct((S, LANES), F32)],
        scratch_shapes=[pltpu.VMEM((S, LANES), F32), pltpu.VMEM((S, LANES), F32)],
        compiler_params=_params())(q, k, v, o, do, lse)


BAND_TQ = SPAN


def _band_blocks(L, tq):
    return [(i * tq, (i + 1) * tq, max(0, i * tq - SPAN)) for i in range(L // tq)]


def _class_rows(r, dil, lo, hi):
    return pl.ds(r + dil * lo, hi - lo, stride=dil) if dil > 1 else pl.ds(lo, hi - lo)


def _stack_heads(t, lo):
    zero = jnp.zeros_like(t)
    return jnp.concatenate([jnp.where(lo, t, zero), jnp.where(lo, zero, t)], axis=0)


def _band_mask2(q0, q1, k0):
    n = q1 - q0
    shape = (2 * n, q1 - k0)
    i = lax.broadcasted_iota(jnp.int32, shape, 0)
    dist = (jnp.where(i >= n, i - n, i) + q0) - (lax.broadcasted_iota(jnp.int32, shape, 1) + k0)
    return (dist >= 0) & (dist <= SPAN)


def _pair_col(col0=0):
    return pl.BlockSpec((SEQ, LANES), lambda j: (0, col0 // LANES + j))


def band_fwd(name, q, k, v, dil, dep=None):
    S = q.shape[0]
    L = S // dil
    tq = BAND_TQ
    scale = DIL_DIM ** -0.5
    deps = [] if dep is None else [dep]

    def body(q_ref, k_ref, v_ref, *rest):
        o_ref, lse_ref = rest[len(deps):]
        items = [(r, blk) for r in range(dil) for blk in _band_blocks(L, tq)]
        lo = _lane((tq, LANES)) < DIL_DIM

        def scores(item):
            r, (q0, q1, k0) = item
            qb = q_ref[_class_rows(r, dil, q0, q1), :].astype(MXU_DTYPE)
            return _dot(_stack_heads(qb, lo), k_ref[_class_rows(r, dil, k0, q1), :], "nt")

        def softmax(item, s):
            _, (q0, q1, k0) = item
            s = jnp.where(_band_mask2(q0, q1, k0), s * scale, NEG_INF)
            mx = jnp.max(s, axis=-1, keepdims=True)
            e = jnp.exp(s - mx)
            l = jnp.sum(e, axis=-1, keepdims=True)
            return (e * (1.0 / l)).astype(MXU_DTYPE), mx + jnp.log(l)

        def weighted(item, p, lse):
            r, (q0, q1, k0) = item
            pv = _dot(p, v_ref[_class_rows(r, dil, k0, q1), :])
            o_ref[_class_rows(r, dil, q0, q1), :] = jnp.where(lo, pv[:tq], pv[tq:])
            lse_ref[_class_rows(r, dil, q0, q1), :] = jnp.where(lo, lse[:tq], lse[tq:])

        s, prev = scores(items[0]), None
        for i, item in enumerate(items):
            s_next = scores(items[i + 1]) if i + 1 < len(items) else None
            if prev is not None:
                weighted(items[i - 1], *prev)
            prev, s = softmax(item, s), s_next
        weighted(items[-1], *prev)

    return pl.pallas_call(
        body, name=name, grid=(DIL_WIDTH // LANES,),
        in_specs=[_pair_col()] * 2 + [_pair_col(P_VD)] + [pl.BlockSpec(memory_space=pl.ANY)] * len(deps), out_specs=[_pair_col()] * 2,
        out_shape=[jax.ShapeDtypeStruct((S, DIL_WIDTH), F32)] * 2, compiler_params=_params())(q, k, v, *deps)


def band_bwd(name, q, k, v, lse, lse_mix, o_cat, do_cat, dil, before=None):
    S = q.shape[0]
    L = S // dil
    tq = BAND_TQ
    scale = DIL_DIM ** -0.5
    before = list(before or [])

    def body(q_ref, k_ref, v_ref, lse_ref, mix_ref, o_ref, do_ref, *rest):
        dq_ref, dk_ref, dv_ref = rest[len(before):]
        if before:
            dq0_ref, dk0_ref, dv0_ref = rest[:3]
            dk_ref[...] = dk0_ref[...]
            dv_ref[...] = dv0_ref[...]
        else:
            dk_ref[...] = jnp.zeros_like(dk_ref)
            dv_ref[...] = jnp.zeros_like(dv_ref)
        items = [(r, blk) for r in range(dil) for blk in _band_blocks(L, tq)]
        lo = _lane((tq, LANES)) < DIL_DIM
        per_head = lambda t: jnp.concatenate([t[:, 0:1], t[:, DIL_DIM:DIL_DIM + 1]], axis=0)

        def scores(item):
            r, (q0, q1, k0) = item
            qrows, krows = _class_rows(r, dil, q0, q1), _class_rows(r, dil, k0, q1)
            lse_p, dout = lse_ref[qrows, :], do_ref[qrows, :]
            w2 = per_head(jnp.exp(lse_p - mix_ref[qrows, :]))
            dd = dout * o_ref[qrows, :]
            big_d = jnp.concatenate([jnp.sum(jnp.where(lo, dd, 0.0), axis=-1, keepdims=True),
                                     jnp.sum(jnp.where(lo, 0.0, dd), axis=-1, keepdims=True)], axis=0)
            q2 = _stack_heads(q_ref[qrows, :].astype(MXU_DTYPE), lo)
            dom = (_stack_heads(dout, lo) * w2).astype(MXU_DTYPE)
            return (_dot(q2, k_ref[krows, :], "nt"), _dot(dom, v_ref[krows, :], "nt"), per_head(lse_p), w2 * big_d, q2, dom)

        def softmax_bwd(item, s, dp, lse2, wd2, q2, dom):
            _, (q0, q1, k0) = item
            p = jnp.where(_band_mask2(q0, q1, k0), jnp.exp(s * scale - lse2), 0.0)
            return p.astype(MXU_DTYPE), (p * (dp - wd2) * scale).astype(MXU_DTYPE), q2, dom

        def grads(item, p, ds, q2, dom):
            r, (q0, q1, k0) = item
            qrows, krows = _class_rows(r, dil, q0, q1), _class_rows(r, dil, k0, q1)
            dq2 = _dot(ds, k_ref[krows, :])
            dq = jnp.where(lo, dq2[:tq], dq2[tq:])
            dq_ref[qrows, :] = dq + dq0_ref[qrows, :] if before else dq
            dk_ref[krows, :] += _dot(ds, q2, "tn")
            dv_ref[krows, :] += _dot(p, dom, "tn")

        sc, prev = scores(items[0]), None
        for i, item in enumerate(items):
            sc_next = scores(items[i + 1]) if i + 1 < len(items) else None
            if prev is not None:
                grads(items[i - 1], *prev)
            prev, sc = softmax_bwd(item, *sc), sc_next
        grads(items[-1], *prev)

    cat = _pair_col(HEADS * LANES)
    return pl.pallas_call(
        body, name=name, grid=(DIL_WIDTH // LANES,),
        in_specs=[_pair_col()] * 2 + [_pair_col(P_VD)] + [_pair_col()] * 2 + [cat] * 2 + [_pair_col()] * len(before),
        out_specs=[_pair_col()] * 3, out_shape=[jax.ShapeDtypeStruct((S, DIL_WIDTH), F32)] * 3,
        compiler_params=_params())(q, k, v, lse, lse_mix, o_cat, do_cat, *before)


def combine_fwd(name, outs, lses, o_cat, tm=512):
    S = outs[0].shape[0]

    def body(o1, o2, o3, l1, l2, l3, cat_in, cat_out, mix_ref):
        ls = [l1[...], l2[...], l3[...]]
        m = jnp.maximum(jnp.maximum(ls[0], ls[1]), ls[2])
        e = [jnp.exp(l - m) for l in ls]
        den = e[0] + e[1] + e[2]
        cat_out[...] = (e[0] / den) * o1[...] + (e[1] / den) * o2[...] + (e[2] / den) * o3[...]
        mix_ref[...] = m + jnp.log(den)

    row = pl.BlockSpec((tm, DIL_WIDTH), lambda i: (i, 0))
    return pl.pallas_call(
        body, name=name, grid=(S // tm,), in_specs=[row] * 6 + [pl.BlockSpec(memory_space=pl.ANY)],
        out_specs=[pl.BlockSpec((tm, DIL_WIDTH), lambda i: (i, HEADS * LANES // DIL_WIDTH)), row],
        out_shape=[jax.ShapeDtypeStruct(o_cat.shape, F32), jax.ShapeDtypeStruct((S, DIL_WIDTH), F32)],
        input_output_aliases={6: 0}, compiler_params=_params())(*outs, *lses, o_cat)


FFN_FWD_ROWS = 512
FFN_BWD_ROWS = 256
CONV_PAD = SUBLANES


def _window(x, k):
    groups = x.reshape(-1, SUBLANES, x.shape[1])
    turned = pltpu.roll(groups, SUBLANES - k, axis=1)
    stays = lax.broadcasted_iota(jnp.int32, (groups.shape[0] - 1,) + groups.shape[1:], 1) < SUBLANES - k
    return jnp.where(stays, turned[:-1], turned[1:]).reshape(-1, x.shape[1])


def _earlier(ref, r0, rows, n):
    if r0 == 0:
        x = jnp.concatenate([jnp.zeros((SUBLANES, ref.shape[1]), F32), ref[:rows, :]], axis=0)
    else:
        x = ref[r0 - SUBLANES:r0 + rows, :]
    return _window(x, SUBLANES - n)


CONV_TC = 256
CONV_NB = D_FF // CONV_TC


def _half_specs(rows, rows_axis=False):
    if rows_axis:
        return [pl.BlockSpec((rows, D_MODEL), lambda j: (j, 0)), pl.BlockSpec((rows, D_MODEL), lambda j: (j + CONV_NB, 0))]
    return [pl.BlockSpec((rows, CONV_TC), lambda j: (0, j)), pl.BlockSpec((rows, CONV_TC), lambda j: (0, j + CONV_NB))]


def _whole(a):
    return pl.BlockSpec(a.shape, lambda j: (0,) * a.ndim)


def _up_pair(h, ug_ref, uv_ref):
    return jnp.concatenate([_dot(h, ug_ref[...], "nt"), _dot(h, uv_ref[...], "nt")], axis=1)


def _conv_taps(up_ref, r0, rows, w, b):
    uin, u1, u2 = up_ref[r0:r0 + rows, :], _earlier(up_ref, r0, rows, 1), _earlier(up_ref, r0, rows, 2)
    return uin, u1, u2, w[2:3, :] * uin + w[1:2, :] * u1 + w[0:1, :] * u2 + b


def ffn_fwd(name, h, w_up_t, w_conv, b_conv, w_down):
    S = h.shape[0]

    def body(h_ref, ug_ref, uv_ref, wg_ref, wv_ref, bg_ref, bv_ref, wd_ref, dn_ref, up_ref):
        @pl.when(pl.program_id(0) == 0)
        def _():
            dn_ref[...] = jnp.zeros_like(dn_ref)

        w = jnp.concatenate([wg_ref[...], wv_ref[...]], axis=1)
        b = jnp.concatenate([bg_ref[...], bv_ref[...]], axis=1)
        rows = FFN_FWD_ROWS
        starts = list(range(0, S, rows))

        def project(r0):
            up_ref[r0:r0 + rows, :] = _up_pair(h_ref[r0:r0 + rows, :], ug_ref, uv_ref)

        def gate(r0):
            u = _conv_taps(up_ref, r0, rows, w, b)[3]
            return (_silu(u[:, :CONV_TC]) * u[:, CONV_TC:]).astype(MXU_DTYPE)

        def project_down(r0, act):
            dn_ref[r0:r0 + rows, :] += _dot(act, wd_ref[...])

        project(starts[0])
        act_prev = None
        for i, r0 in enumerate(starts):
            if i + 1 < len(starts):
                project(starts[i + 1])
            if act_prev is not None:
                project_down(starts[i - 1], act_prev)
            act_prev = gate(r0)
        project_down(starts[-1], act_prev)

    return pl.pallas_call(
        body, name=name, grid=(CONV_NB,),
        in_specs=[_whole(h)] + _half_specs(CONV_TC, rows_axis=True) + _half_specs(3) + _half_specs(1)
        + [pl.BlockSpec((CONV_TC, w_down.shape[1]), lambda j: (j, 0))],
        out_specs=[pl.BlockSpec((S, w_down.shape[1]), lambda j: (0, 0)), pl.BlockSpec((S, 2 * CONV_TC), lambda j: (0, j))],
        out_shape=[jax.ShapeDtypeStruct((S, w_down.shape[1]), F32), jax.ShapeDtypeStruct((S, 2 * D_FF), F32)],
        compiler_params=_params())(h, w_up_t, w_up_t, w_conv, w_conv, b_conv, b_conv, w_down)


def ffn_bwd(name, h, up, w_up_t, w_conv, b_conv, d_dn, w_down):
    S, D = h.shape

    def body(h_ref, up_ref, ug_ref, uv_ref, wg_ref, wv_ref, bg_ref, bv_ref, dd_ref, wd_ref,
             dh_ref, gup_ref, gd_ref, dwg_ref, dwv_ref, dbg_ref, dbv_ref, du_ref, dup_ref, act_ref):
        @pl.when(pl.program_id(0) == 0)
        def _():
            dh_ref[...] = jnp.zeros_like(dh_ref)

        w = jnp.concatenate([wg_ref[...], wv_ref[...]], axis=1)
        b = jnp.concatenate([bg_ref[...], bv_ref[...]], axis=1)
        w_pair = jnp.concatenate([ug_ref[...], uv_ref[...]], axis=0)
        rows = FFN_BWD_ROWS
        starts = list(range(0, S, rows))
        du_ref[S:S + CONV_PAD, :] = jnp.zeros((CONV_PAD, 2 * CONV_TC), F32)

        def project(r0):
            return _dot(dd_ref[r0:r0 + rows, :], wd_ref[...], "nt")

        def through_conv(r0, da):
            uin, u1, u2, u = _conv_taps(up_ref, r0, rows, w, b)
            gate, val = u[:, :CONV_TC], u[:, CONV_TC:]
            sig = 1.0 / (1.0 + jnp.exp(-gate))
            du = jnp.concatenate([da * val * (sig * (1.0 + gate * (1.0 - sig))), da * (gate * sig)], axis=1)
            du_ref[r0:r0 + rows, :] = du
            act_ref[r0:r0 + rows, :] = (gate * sig * val).astype(MXU_DTYPE)
            dw = jnp.concatenate([_colsum(du * u2), _colsum(du * u1), _colsum(du * uin)], axis=0)
            return dw, _colsum(du)

        def back_up(r0):
            du = du_ref[r0:r0 + rows + CONV_PAD, :]
            dup = (w[2:3, :] * du[:rows] + w[1:2, :] * _window(du, 1) + w[0:1, :] * _window(du, 2)).astype(MXU_DTYPE)
            dup_ref[r0:r0 + rows, :] = dup
            dh_ref[r0:r0 + rows, :] += _dot(dup, w_pair)

        dw, db = 0.0, 0.0
        da = project(starts[0])
        for i, r0 in enumerate(starts):
            da_next = project(starts[i + 1]) if i + 1 < len(starts) else None
            dw_c, db_c = through_conv(r0, da)
            if i > 0:
                back_up(starts[i - 1])
            dw, db, da = dw + dw_c, db + db_c, da_next
        back_up(starts[-1])
        g_up, g_dn = _dot(dup_ref[...], h_ref[...], "tn"), _dot(act_ref[...], dd_ref[...], "tn")
        gup_ref[0], gup_ref[1] = g_up[:CONV_TC].astype(gup_ref.dtype), g_up[CONV_TC:].astype(gup_ref.dtype)
        gd_ref[...] = g_dn.astype(gd_ref.dtype)
        dwg_ref[...], dwv_ref[...] = dw[:, :CONV_TC], dw[:, CONV_TC:]
        dbg_ref[...], dbv_ref[...] = db[:, :CONV_TC], db[:, CONV_TC:]

    half = lambda rows: pl.BlockSpec((rows, CONV_TC), lambda j: (0, j))
    rows_blk = pl.BlockSpec((CONV_TC, D), lambda j: (j, 0))
    dh, gup, gd, dwg, dwv, dbg, dbv = pl.pallas_call(
        body, name=name, grid=(CONV_NB,),
        in_specs=[_whole(h), pl.BlockSpec((S, 2 * CONV_TC), lambda j: (0, j))] + _half_specs(CONV_TC, rows_axis=True) + _half_specs(3)
        + _half_specs(1) + [_whole(d_dn), rows_blk],
        out_specs=[pl.BlockSpec((S, D), lambda j: (0, 0)), pl.BlockSpec((2, CONV_TC, D), lambda j: (0, j, 0)), rows_blk,
                   half(3), half(3), half(1), half(1)],
        out_shape=[jax.ShapeDtypeStruct((S, D), F32), jax.ShapeDtypeStruct((2, D_FF, D), MXU_DTYPE),
                   jax.ShapeDtypeStruct((D_FF, D), MXU_DTYPE)]
        + [jax.ShapeDtypeStruct((3, D_FF), F32)] * 2 + [jax.ShapeDtypeStruct((1, D_FF), F32)] * 2,
        scratch_shapes=[pltpu.VMEM((S + CONV_PAD, 2 * CONV_TC), F32), pltpu.VMEM((S, 2 * CONV_TC), MXU_DTYPE),
                        pltpu.VMEM((S, CONV_TC), MXU_DTYPE)],
        compiler_params=_params())(h, up, w_up_t, w_up_t, w_conv, w_conv, b_conv, b_conv, d_dn, w_down)
    return dh, gup.reshape(2 * D_FF, D), gd, jnp.concatenate([dwg, dwv], axis=1), jnp.concatenate([dbg, dbv], axis=1)


def adamw(name, w, parts, m, v, tr=None):
    apart = w.ndim == 3
    R, C = w.shape[0], w.shape[-1]
    tr = tr or R
    assert R % tr == 0
    c1 = 1.0 - ADAM_B1 ** ADAM_STEP
    c2 = 1.0 - ADAM_B2 ** ADAM_STEP
    np_ = len(parts)

    def body(*refs):
        w_ref, m_ref, v_ref = refs[0], refs[1 + np_], refs[2 + np_]
        go_ref, d_ref, mo_ref, vo_ref = refs[3 + np_:]
        terms = []
        for part, ref in zip(parts, refs[1:1 + np_], strict=True):
            terms += [ref[...]] if part.ndim == 2 else [ref[p] for p in range(part.shape[0])]
        g = terms[0].astype(F32)
        for term in terms[1:]:
            g = g + term.astype(F32)
        m2 = ADAM_B1 * m_ref[...] + (1.0 - ADAM_B1) * g
        v2 = ADAM_B2 * v_ref[...] + (1.0 - ADAM_B2) * (g * g)
        go_ref[...] = g
        mo_ref[...] = m2
        vo_ref[...] = v2
        d_ref[...] = -ADAM_LR * ((m2 / c1) / (jnp.sqrt(v2 / c2) + ADAM_EPS) + ADAM_WD * w_ref[...])

    blk = pl.BlockSpec((tr, C), lambda i: (i, 0))
    own = pl.BlockSpec((tr, None, C), lambda i: (i, 0, 0)) if apart else blk
    part_specs = [blk if p.ndim == 2 else pl.BlockSpec((p.shape[0], tr, C), lambda i: (0, i, 0)) for p in parts]
    return pl.pallas_call(
        body, name=name, grid=(R // tr,),
        in_specs=[own] + part_specs + [own, own], out_specs=[own] * 4,
        out_shape=[jax.ShapeDtypeStruct(w.shape, F32)] * 4, compiler_params=_params())(w, *parts, m, v)


def _place():
    return lax.axis_index("x"), lax.axis_index("y"), lax.axis_index("c")


def all_gather(name, arrs, after=None):
    n = len(arrs)
    deps = [] if after is None else [after]

    def body(*refs):
        ins, outs = refs[:n], refs[n + len(deps):2 * n + len(deps)]
        send_sems, recv_sems, local_sems = refs[2 * n + len(deps):]
        x, y, c = _place()
        me, sibling = (x, y, c), (x, y, 1 - c)
        chips = [(1 - x, y), (x, 1 - y), (1 - x, 1 - y)]
        sends = []
        for t in range(n):
            out = outs[t]

            def slot(px, py, pc, out=out):
                return out.at[4 * px + 2 * py + pc]

            def copy(k, block, to, src=None, t=t, slot=slot):
                return pltpu.make_async_remote_copy(
                    src_ref=slot(*block) if src is None else src, dst_ref=slot(*block),
                    send_sem=send_sems.at[7 * t + k], recv_sem=recv_sems.at[7 * t + k],
                    device_id=to, device_id_type=MESH_ID)

            mine = pltpu.make_async_copy(ins[t], slot(*me), local_sems.at[t])
            mine.start()
            first = [copy(0, me, sibling, src=ins[t])]
            first += [copy(1 + j, me, (*chip, c), src=ins[t]) for j, chip in enumerate(chips)]
            for cp in first:
                cp.start()
            sends.append((mine, first, copy))
        for t in range(n):
            mine, first, copy = sends[t]
            passed = [copy(4 + j, (*chip, c), sibling) for j, chip in enumerate(chips)]
            for j, chip in enumerate(chips):
                copy(1 + j, (*chip, c), me).wait_recv()
                passed[j].start()
            copy(0, sibling, me).wait_recv()
            for j, chip in enumerate(chips):
                copy(4 + j, (*chip, 1 - c), me).wait_recv()
            for cp in first + passed:
                cp.wait_send()
            mine.wait()

    any_spec = pl.BlockSpec(memory_space=pl.ANY)
    res = pl.pallas_call(
        body, name=name, in_specs=[any_spec] * (n + len(deps)), out_specs=[any_spec] * n,
        out_shape=[jax.ShapeDtypeStruct((N_DEV,) + a.shape, a.dtype) for a in arrs],
        scratch_shapes=[pltpu.SemaphoreType.DMA((7 * n,)), pltpu.SemaphoreType.DMA((7 * n,)), pltpu.SemaphoreType.DMA((n,))],
        compiler_params=pltpu.CompilerParams(has_side_effects=True))(*arrs, *deps)
    return list(res)


def ada_modulation(name, c, w_ada, after):
    n_mod = w_ada.shape[1]

    def exchange(src_ref, dst_ref, send_sems, recv_sems):
        x, y, c_ = _place()
        me = 4 * x + 2 * y + c_
        copies = []
        for k in range(1, N_DEV):
            px, py, pc = x ^ (k >> 2), y ^ ((k >> 1) & 1), c_ ^ (k & 1)
            copies.append(pltpu.make_async_remote_copy(
                src_ref=src_ref, dst_ref=dst_ref.at[me], send_sem=send_sems.at[k - 1], recv_sem=recv_sems.at[k - 1],
                device_id=(px, py, pc), device_id_type=MESH_ID))
        for cp in copies:
            cp.start()
        for cp in copies:
            cp.wait_recv()
        for cp in copies:
            cp.wait_send()
        return me

    def body(c_ref, w_ref, _, sc_ref, mod_ref, c_all, send_c, recv_c, send_m, recv_m):
        me = exchange(c_ref, c_all, send_c, recv_c)
        c_all[me] = c_ref[...]
        sc = _silu(jnp.concatenate([c_all[p] for p in range(N_DEV)], axis=0))
        sc_ref[...] = sc.astype(sc_ref.dtype)
        mod_ref[me] = _dot(sc, w_ref[...])
        exchange(mod_ref.at[me], mod_ref, send_m, recv_m)

    vmem = pl.BlockSpec(memory_space=pltpu.VMEM)
    return pl.pallas_call(
        body, name=name, in_specs=[vmem, vmem, pl.BlockSpec(memory_space=pl.ANY)], out_specs=[vmem, vmem],
        out_shape=[jax.ShapeDtypeStruct((N_DEV, c.shape[1]), MXU_DTYPE), jax.ShapeDtypeStruct((N_DEV, N_DEV, n_mod), F32)],
        scratch_shapes=[pltpu.VMEM((N_DEV, 1, c.shape[1]), F32)] + [pltpu.SemaphoreType.DMA((N_DEV - 1,))] * 4,
        compiler_params=pltpu.CompilerParams(has_side_effects=True, vmem_limit_bytes=VMEM_LIMIT))(c, w_ada, after)


HBM_SPEC = pl.BlockSpec(memory_space=pltpu.HBM)
SEM_SPEC = pl.BlockSpec(memory_space=pltpu.SEMAPHORE)
DATAFLOW = pltpu.SideEffectType.DATAFLOW_SIDE_EFFECTING


def _exchange_copies(srcs, lands, send_sems, recv_sems, gather, first=0):
    x, y, c = _place()
    me = 4 * x + 2 * y + c
    out = []
    for t, (src, land) in enumerate(zip(srcs, lands, strict=True)):
        for k in range(1, N_DEV):
            px, py, pc = x ^ (k >> 2), y ^ ((k >> 1) & 1), c ^ (k & 1)
            sem = 7 * (first + t) + k - 1
            out.append((k, pltpu.make_async_remote_copy(
                src_ref=src if gather else src.at[4 * px + 2 * py + pc],
                dst_ref=land.at[me] if gather else land.at[k - 1],
                send_sem=send_sems.at[sem], recv_sem=recv_sems.at[sem],
                device_id=(px, py, pc), device_id_type=MESH_ID)))
    return out


TREE_DIRECT = (1, 2, 4, 6)
TREE_FORWARDED = (3, 5, 7)


def exchange_start(name, arrs, gather, after=None, tree=False):
    n = len(arrs)
    lands = [lax.empty(((N_DEV,) + a.shape) if gather else ((N_DEV - 1,) + a.shape[1:]), a.dtype) for a in arrs]
    deps = [] if after is None else [after]

    def body(*refs):
        srcs, land_refs = refs[:n], refs[n:2 * n]
        send_sems, recv_sems = refs[2 * n + len(deps)], refs[2 * n + len(deps) + 1]
        token = refs[-1]
        for k, cp in _exchange_copies(srcs, land_refs, send_sems, recv_sems, gather):
            if not tree or k in TREE_DIRECT:
                cp.start()
        token[...] = jnp.zeros_like(token)

    hbm = lambda a: pltpu.HBM(a.shape, a.dtype)
    res = pl.pallas_call(
        body, name=name,
        out_shape=(pltpu.SemaphoreType.DMA((7 * n,)), pltpu.SemaphoreType.DMA((7 * n,)), *[hbm(a) for a in arrs],
                   *[hbm(l) for l in lands], jax.ShapeDtypeStruct((8, 128), F32)),
        in_specs=[HBM_SPEC] * (2 * n) + [pl.BlockSpec(memory_space=pl.ANY)] * len(deps),
        out_specs=(SEM_SPEC, SEM_SPEC, *[HBM_SPEC] * (2 * n), pl.BlockSpec(memory_space=pltpu.VMEM)),
        input_output_aliases={i: 2 + i for i in range(2 * n)},
        compiler_params=pltpu.CompilerParams(has_side_effects=DATAFLOW),
    )(*[pltpu.with_memory_space_constraint(a, pltpu.HBM) for a in arrs + lands], *deps)
    return res[0], res[1], list(res[2:2 + n]), list(res[2 + n:2 + 2 * n]), res[-1]


def exchange_forward(name, started, after, first=0, count=None):
    send_sems, recv_sems, srcs, lands, _ = started
    count = len(srcs) - first if count is None else count
    mine = lands[first:first + count]
    n = len(mine)

    def copies(land_refs, send_ref, recv_ref):
        x, y, c = _place()
        out = []
        for t, land in enumerate(land_refs):
            for k in (2, 4, 6):
                slot = land.at[4 * (x ^ (k >> 2)) + 2 * (y ^ ((k >> 1) & 1)) + c]
                came, goes = 7 * (first + t) + k - 1, 7 * (first + t) + (k ^ 1) - 1
                out.append((
                    pltpu.make_async_remote_copy(src_ref=slot, dst_ref=slot, send_sem=send_ref.at[came], recv_sem=recv_ref.at[came],
                                                 device_id=(x, y, c), device_id_type=MESH_ID),
                    pltpu.make_async_remote_copy(src_ref=slot, dst_ref=slot, send_sem=send_ref.at[goes], recv_sem=recv_ref.at[goes],
                                                 device_id=(x, y, 1 - c), device_id_type=MESH_ID)))
        return out

    after = list(after) if isinstance(after, (list, tuple)) else [after]

    def arrived(*refs):
        for came, _ in copies(refs[:n], refs[n], refs[n + 1]):
            came.wait_recv()

    def pass_on(*refs):
        for _, goes in copies(refs[:n], refs[n], refs[n + 1]):
            goes.start()
        refs[-1][...] = jnp.zeros_like(refs[-1])

    hbm = lambda a: pltpu.HBM(a.shape, a.dtype)
    here = pl.pallas_call(
        arrived, name=name + "_arrived", out_shape=tuple(hbm(a) for a in mine),
        in_specs=[HBM_SPEC] * n + [SEM_SPEC, SEM_SPEC] + [pl.BlockSpec(memory_space=pl.ANY)] * len(after),
        out_specs=tuple([HBM_SPEC] * n), input_output_aliases={i: i for i in range(n)},
        compiler_params=pltpu.CompilerParams(has_side_effects=DATAFLOW),
    )(*mine, send_sems, recv_sems, *after)
    res = pl.pallas_call(
        pass_on, name=name, out_shape=(*[hbm(a) for a in mine], jax.ShapeDtypeStruct((8, 128), F32)),
        in_specs=[HBM_SPEC] * n + [SEM_SPEC, SEM_SPEC],
        out_specs=(*[HBM_SPEC] * n, pl.BlockSpec(memory_space=pltpu.VMEM)), input_output_aliases={i: i for i in range(n)},
        compiler_params=pltpu.CompilerParams(has_side_effects=DATAFLOW),
    )(*here, send_sems, recv_sems)
    lands = lands[:first] + list(res[:n]) + lands[first + count:]
    return (send_sems, recv_sems, srcs, lands, res[-1])


def exchange_wait(name, started, gather, after, first=0, count=None, tree=False):
    send_sems, recv_sems, srcs, lands, _ = started
    count = len(srcs) - first if count is None else count
    srcs, lands = srcs[first:first + count], lands[first:first + count]
    n = len(srcs)

    def body(*refs):
        src_refs, land_refs = refs[:n], refs[n:2 * n]
        copies = _exchange_copies(src_refs, land_refs, refs[2 * n], refs[2 * n + 1], gather, first)
        for _, cp in copies:
            cp.wait_send()
        for k, cp in copies:
            if not tree or k in (1,) + TREE_FORWARDED:
                cp.wait_recv()

    hbm = lambda a: pltpu.HBM(a.shape, a.dtype)
    res = pl.pallas_call(
        body, name=name, out_shape=tuple(hbm(a) for a in srcs + lands),
        in_specs=[HBM_SPEC] * (2 * n) + [SEM_SPEC, SEM_SPEC, pl.BlockSpec(memory_space=pl.ANY)],
        out_specs=tuple([HBM_SPEC] * (2 * n)), input_output_aliases={i: i for i in range(2 * n)},
        compiler_params=pltpu.CompilerParams(has_side_effects=DATAFLOW),
    )(*srcs, *lands, send_sems, recv_sems, after)
    return list(res[:n]), list(res[n:])


def _gather_cols(stack):
    p, k, n = stack.shape
    return stack.transpose(1, 0, 2).reshape(k, p * n)


def _scatter_cols(full):
    k, n = full.shape
    return full.reshape(k, N_DEV, n // N_DEV).transpose(1, 0, 2)


def _gather_rows(stack):
    p, r, n = stack.shape
    return stack.reshape(p * r, n)


def _scatter_rows(full):
    r, n = full.shape
    return full.reshape(N_DEV, r // N_DEV, n)


_IN_NAT = Q_LORA + KV_LORA
TRANSPOSED = ("w_in", "w_q_b", "w_up")
ROWS_APART = ("w_in", "w_conv")


def to_kernel_layout(name, w):
    if name == "w_in":
        z = lambda n: jnp.zeros((n, w.shape[1]), w.dtype)
        return jnp.concatenate([w[:_IN_NAT], z(KPE_LO), w[_IN_NAT:_IN_NAT + ROPE], z(LANES - KPE_LO - ROPE), w[_IN_NAT + ROPE:]], axis=0)
    if name == "w_q_b":
        return jnp.pad(w.reshape(HEADS, NOPE + ROPE, -1), ((0, 0), (0, LANES - NOPE - ROPE), (0, 0))).reshape(HEADS * LANES, -1)
    if name == "w_o":
        mla = jnp.pad(w[:HEADS * NOPE].reshape(HEADS, NOPE, -1), ((0, 0), (LANES - NOPE, 0), (0, 0))).reshape(HEADS * LANES, -1)
        return jnp.concatenate([mla, w[HEADS * NOPE:]], axis=0)
    return w


def from_kernel_layout(name, g):
    if name == "w_in":
        return jnp.concatenate([g[:_IN_NAT], g[P_KPE + KPE_LO:P_KPE + KPE_LO + ROPE], g[P_QD:]], axis=0)
    if name == "w_q_b":
        return g.reshape(HEADS, LANES, -1)[:, :NOPE + ROPE, :].reshape(HEADS * (NOPE + ROPE), -1)
    if name == "w_o":
        mla = g[:HEADS * LANES].reshape(HEADS, LANES, -1)[:, LANES - NOPE:, :].reshape(HEADS * NOPE, -1)
        return jnp.concatenate([mla, g[HEADS * LANES:]], axis=0)
    return g


SMALL_COLS = 1024
SMALL_ROWS = 24
SMALL_AT = {"loss": (0, 0, 1), "b_ada": (1, 0, 6 * D_MODEL), "g_mix_norm": (7, 0, D_MODEL), "g_q_lat": (8, 0, Q_LORA),
            "g_kv_lat": (9, 0, KV_LORA), "g_mla_q_nope": (10, 0, NOPE), "g_mla_q_pe": (10, 128, ROPE),
            "g_mla_k_nope": (10, 256, NOPE), "g_mla_k_pe": (10, 384, ROPE), "g_dil_q": (10, 512, DIL_DIM),
            "g_dil_k": (10, 640, DIL_DIM), "g_ffn_norm": (11, 0, D_MODEL), "b_conv": (12, 0, 2 * D_FF)}
SMALL_PARAMS = tuple(n for n in SMALL_AT if n != "loss")


def _pack_small(values):
    by_row = {}
    for name, (row, off, n) in SMALL_AT.items():
        by_row.setdefault(row, []).append((off, values[name].reshape(-1).astype(F32)))
    out = []
    for row in sorted(by_row):
        pieces, at = [], 0
        for off, v in sorted(by_row[row], key=lambda t: t[0]):
            pieces += [jnp.zeros((off - at,), F32), v]
            at = off + v.shape[0]
        flat = jnp.concatenate(pieces)
        nrows = -(-flat.shape[0] // SMALL_COLS)
        out.append(jnp.pad(flat, (0, nrows * SMALL_COLS - flat.shape[0])).reshape(nrows, SMALL_COLS))
    packed = jnp.concatenate(out, axis=0)
    return jnp.pad(packed, ((0, SMALL_ROWS - packed.shape[0]), (0, 0)))


def _adam(w, g, m, v):
    c1 = 1.0 - ADAM_B1 ** ADAM_STEP
    c2 = 1.0 - ADAM_B2 ** ADAM_STEP
    m2 = ADAM_B1 * m + (1.0 - ADAM_B1) * g
    v2 = ADAM_B2 * v + (1.0 - ADAM_B2) * (g * g)
    return -ADAM_LR * ((m2 / c1) / (jnp.sqrt(v2 / c2) + ADAM_EPS) + ADAM_WD * w), m2, v2


def adamw_small(name, stack, params):
    flat = [a for n in SMALL_PARAMS for a in params[n]]

    def body(stack_ref, *refs):
        ins, outs = refs[:len(flat)], refs[len(flat):]
        g_all = stack_ref[0]
        for p in range(1, N_DEV):
            g_all = g_all + stack_ref[p]
        outs[0][...] = g_all[0:1, 0:1]
        for i, pname in enumerate(SMALL_PARAMS):
            row, off, n = SMALL_AT[pname]
            w_ref, m_ref, v_ref = ins[3 * i:3 * i + 3]
            go_ref, d_ref, mo_ref, vo_ref = outs[1 + 4 * i:5 + 4 * i]
            for c0 in range(0, n, SMALL_COLS):
                cn = min(SMALL_COLS, n - c0)
                r = row + c0 // SMALL_COLS
                g = g_all[r:r + 1, off:off + cn]
                cols = (slice(None), slice(c0, c0 + cn))
                d, m2, v2 = _adam(w_ref[cols], g, m_ref[cols], v_ref[cols])
                go_ref[cols], d_ref[cols], mo_ref[cols], vo_ref[cols] = g, d, m2, v2

    whole = lambda a: pl.BlockSpec(a.shape, lambda: (0,) * a.ndim)
    out_shape = [jax.ShapeDtypeStruct((1, 1), F32)] + [jax.ShapeDtypeStruct(a.shape, F32) for n in SMALL_PARAMS for a in params[n][:1] * 4]
    res = pl.pallas_call(body, name=name, in_specs=[whole(stack)] + [whole(a) for a in flat],
                         out_specs=[pl.BlockSpec(s.shape, lambda s=s: (0,) * len(s.shape)) for s in out_shape],
                         out_shape=out_shape, compiler_params=_params())(stack, *flat)
    return res[0], {n: res[1 + 4 * i:5 + 4 * i] for i, n in enumerate(SMALL_PARAMS)}


def _local_step(x, pos, mod, target, w, fetch, emit, halfway=lambda after: None):
    S = SEQ
    sh1, sc1, g1, sh2, sc2, g2 = [mod[:, i * D_MODEL:(i + 1) * D_MODEL] for i in range(6)]
    zeros = lambda n: jnp.zeros((1, n), F32)
    g_q = jnp.concatenate([w["g_mla_q_nope"], w["g_mla_q_pe"], zeros(LANES - NOPE - ROPE)], axis=1)
    g_k = jnp.concatenate([w["g_mla_k_nope"], zeros(LANES - NOPE)], axis=1)
    g_kpe = jnp.concatenate([zeros(KPE_LO), w["g_mla_k_pe"], zeros(LANES - KPE_LO - ROPE)], axis=1)
    g_dq = jnp.concatenate([w["g_dil_q"]] * 2, axis=1)
    g_dk = jnp.concatenate([w["g_dil_k"]] * 2, axis=1)
    b_conv = w["b_conv"]

    def inv_freq(d):
        return jnp.power(ROPE_THETA, -2.0 * jnp.arange(d // 2, dtype=F32) / d)

    n_m, n_d = ROPE // 2, DIL_DIM // 2
    freqs = jnp.concatenate([inv_freq(ROPE), inv_freq(DIL_DIM), jnp.zeros((LANES - n_m - n_d,), F32)]).reshape(1, LANES)

    def tables_fn(rows, params):
        (p,), (f,) = rows, params
        c, s = jnp.cos(p * f), jnp.sin(p * f)
        one, zero = jnp.ones_like(c), jnp.zeros_like(c)
        mla = lambda t, fill: jnp.concatenate([fill[:, :KPE_LO], t[:, :n_m], t[:, :n_m], fill[:, :LANES - KPE_LO - ROPE]], axis=1)
        dil = lambda t: jnp.concatenate([t[:, n_m:n_m + n_d]] * 4, axis=1)
        return [mla(c, one), mla(s, zero), dil(c), dil(s)], []

    cos_m, sin_m, cos_d, sin_d = rowwise("rope_tables", tables_fn, [pos], [freqs], [(LANES, F32)] * 4)
    tables = [cos_m, sin_m, cos_d, sin_d]
    H_M, H_D = ROPE // 2, DIL_DIM // 2

    def ln1_fn(rows, params):
        (xv,), (g, sc, sh) = rows, params
        y, _, _ = _rms(xv, g)
        return [y * (1.0 + sc) + sh], []

    (h,) = rowwise("ln1_fwd", ln1_fn, [x], [w["g_mix_norm"], sc1, sh1], [(D_MODEL, MXU_DTYPE)], dep=sin_d)
    w_in = fetch("w_in", h)

    def proj_fn(rows, params):
        (hv, cm, sm, cd, sd), (w_t, gq, gkv, gkp, gdq, gdk) = rows, params
        pv = _dot(hv, w_t, "nt")
        kper = _rope(_grms(pv[:, P_KPE:P_QD], gkp, KPE_GROUPS)[0], cm, sm, H_M)
        qd = [_rope(_grms(c, gdq, DIL_GROUPS)[0], cd, sd, H_D) for c in _chunks(pv[:, P_QD:P_KD])]
        kd = [_rope(_grms(c, gdk, DIL_GROUPS)[0], cd, sd, H_D) for c in _chunks(pv[:, P_KD:P_VD])]
        return [pv, _rms(pv[:, P_QLAT:P_KVLAT], gq)[0], _rms(pv[:, P_KVLAT:P_KPE], gkv)[0], kper,
                jnp.concatenate(qd, axis=1), jnp.concatenate(kd, axis=1)], []

    post_params = [w["g_q_lat"], w["g_kv_lat"], g_kpe, g_dq, g_dk]
    proj, qln, kvn, kper, qd_r, kd_r = rowwise(
        "proj_fwd", proj_fn, [h] + tables, [w_in] + post_params,
        [(P_END, F32), (Q_LORA, MXU_DTYPE), (KV_LORA, MXU_DTYPE), (LANES, MXU_DTYPE)] + [(DIL_WIDTH, F32)] * 2, tm=256)
    w_q_b, w_kv_b = fetch("w_q_b", qln), fetch("w_kv_b", kvn)

    def mla_proj_fn(rows, params):
        (qlv, kvlv, kp, cm, sm), (wq_t, wkv, gq, gk) = rows, params
        qv, kvv = _dot(qlv, wq_t, "nt"), _dot(kvlv, wkv)
        value_lanes = _lane(kp.shape) >= NOPE
        qs, ks, vs = [], [], []
        for qc, kc in zip(_chunks(qv), _chunks(kvv), strict=True):
            qs.append(_rope(_grms(qc, gq, Q_GROUPS)[0], cm, sm, H_M))
            ks.append(_grms(kc, gk, K_GROUPS)[0] + kp)
            vs.append(jnp.where(value_lanes, kc, 0.0))
        return [qv, kvv] + [jnp.concatenate(t, axis=1) for t in (qs, ks, vs)], []

    q, kv, q_mla, k_mla, v_mla = rowwise(
        "mla_proj", mla_proj_fn, [qln, kvn, kper, cos_m, sin_m], [w_q_b, w_kv_b, g_q, g_k],
        [(HEADS * LANES, F32)] * 2 + [(HEADS * LANES, MXU_DTYPE)] * 3, tm=256)
    mla_scale = (NOPE + ROPE) ** -0.5
    o_cat, lse_mla = mla_fwd("mla_fwd", q_mla, k_mla, v_mla, mla_scale)
    passed = halfway(lse_mla)

    band = [band_fwd(f"band{dil}_fwd", qd_r, kd_r, proj, dil, dep=passed) for dil in DILATIONS]
    o_cat, lse_mix = combine_fwd("dil_combine", [b[0] for b in band], [b[1] for b in band], o_cat)
    w_o = fetch("w_o", o_cat)

    def mid_fn(rows, params):
        (ov, xv), (w_out, gate1, g, sc, sh) = rows, params
        mx = _dot(ov, w_out)
        x1 = xv + gate1 * mx
        y, _, _ = _rms(x1, g)
        return [mx, x1, y * (1.0 + sc) + sh], []

    mix, x1, h2 = rowwise("mix_fwd", mid_fn, [o_cat, x], [w_o, g1, w["g_ffn_norm"], sc2, sh2],
                          [(D_MODEL, F32), (D_MODEL, F32), (D_MODEL, MXU_DTYPE)], tm=256)
    w_up, w_conv, w_down = fetch("w_up", h2), fetch("w_conv", h2), fetch("w_down", h2)
    dn, up = ffn_fwd("ffn_fwd", h2, w_up, w_conv, b_conv, w_down)

    def final_fn(rows, params):
        (x1v, dnv, tgt), (gate2,) = rows, params
        r = x1v + gate2 * dnv - tgt
        dy = r * (1.0 / D_MODEL)
        loss = jnp.sum(_colsum(r * r), axis=-1, keepdims=True) * (0.5 / D_MODEL)
        return [dy, gate2 * dy], [loss, _colsum(dy * dnv)]

    dy, d_dn, loss, dg2 = rowwise("loss_head", final_fn, [x1, dn, target], [g2], [(D_MODEL, F32), (D_MODEL, MXU_DTYPE)],
                                  [1, D_MODEL])
    dh2, g_up, g_down, g_w_conv, g_b_conv = ffn_bwd("ffn_bwd", h2, up, w_up, w_conv, b_conv, d_dn, w_down)
    emit("w_down", g_down)
    emit("w_conv", g_w_conv)
    sent = emit("w_up", g_up)

    def mid_bwd_fn(rows, params):
        (dh2v, dyv, x1v, mx), (gate1, g, sc) = rows, params
        yn, n, rstd = _rms(x1v, g)
        dx_n, dg = _rms_bwd(dh2v * (1.0 + sc), n, rstd, g)
        dx1 = dyv + dx_n
        return [dx1, gate1 * dx1], [dg, _colsum(dh2v * yn), _colsum(dh2v), _colsum(dx1 * mx)]

    dx1, dmix, dg_ffn, dsc2, dsh2, dg1 = rowwise(
        "mid_bwd", mid_bwd_fn, [dh2, dy, x1, mix], [g1, w["g_ffn_norm"], sc2], [(D_MODEL, F32), (D_MODEL, MXU_DTYPE)],
        [D_MODEL] * 4, dep=sent)

    sent = emit("w_o", matmul("mix_wgrad", o_cat, dmix, "tn", tm=512, out_dtype=MXU_DTYPE))
    do_cat = matmul("mix_dgrad", dmix, w_o, "nt", tm=512, dep=sent)
    dband = None
    for dil, b in zip(DILATIONS, band):
        dband = band_bwd(f"band{dil}_bwd", qd_r, kd_r, proj, b[1], lse_mix, o_cat, do_cat, dil, before=dband)
    dq_mla, dkv_mla, dkper = mla_bwd("mla_bwd", q_mla, k_mla, v_mla, o_cat, do_cat, lse_mla, mla_scale)

    def mla_prep_bwd_fn(rows, params):
        (dqv, dkvv, qv, kvv, cm, sm), (gq, gk) = rows, params
        nope_lanes = _lane(cm.shape) < NOPE
        dqs, dkvs, dgq, dgk = [], [], 0.0, 0.0
        for dqc, dkc, qc, kc in zip(_chunks(dqv), _chunks(dkvv), _chunks(qv), _chunks(kvv), strict=True):
            _, n, rstd = _grms(qc, gq, Q_GROUPS)
            dx, dg = _grms_bwd(_rope_bwd(dqc, cm, sm, H_M), n, rstd, gq, Q_GROUPS)
            dqs.append(dx)
            dgq = dgq + dg
            _, n, rstd = _grms(kc, gk, K_GROUPS)
            dx, dg = _grms_bwd(dkc, n, rstd, gk, K_GROUPS)
            dkvs.append(jnp.where(nope_lanes, dx, dkc))
            dgk = dgk + dg
        return [jnp.concatenate(dqs, axis=1), jnp.concatenate(dkvs, axis=1)], [dgq, dgk]

    dq, dkv, dg_q, dg_k = rowwise("mla_prep_bwd", mla_prep_bwd_fn, [dq_mla, dkv_mla, q, kv, cos_m, sin_m], [g_q, g_k],
                                  [(HEADS * LANES, MXU_DTYPE)] * 2, [LANES, LANES], tm=256)
    emit("w_q_b", matmul("q_wgrad", dq, qln, "tn", out_dtype=MXU_DTYPE))
    emit("w_kv_b", matmul("kv_wgrad", kvn, dkv, "tn", out_dtype=MXU_DTYPE))

    def pre_bwd_fn(rows, params):
        dqv, dkvv, dkp, dqd_, dkd_, dvd_, pv, cm, sm, cd, sd = rows
        wq_t, wkv, gq, gkv, gkp, gdq, gdk = params
        dql, dkvl = _dot(dqv, wq_t), _dot(dkvv, wkv, "nt")
        r_q = _norm_bwd(dql, pv[:, P_QLAT:P_KVLAT], gq)
        r_kv = _norm_bwd(dkvl, pv[:, P_KVLAT:P_KPE], gkv)
        _, n, rstd = _grms(pv[:, P_KPE:P_QD], gkp, KPE_GROUPS)
        r_kp = _grms_bwd(_rope_bwd(dkp, cm, sm, H_M), n, rstd, gkp, KPE_GROUPS)
        outs, dgs = [r_q[0], r_kv[0], r_kp[0]], []
        for dval, lo, g in ((dqd_, P_QD, gdq), (dkd_, P_KD, gdk)):
            dg_sum = 0.0
            for dc, xc in zip(_chunks(dval), _chunks(pv[:, lo:lo + DIL_WIDTH]), strict=True):
                _, n, rstd = _grms(xc, g, DIL_GROUPS)
                dx, dg = _grms_bwd(_rope_bwd(dc, cd, sd, H_D), n, rstd, g, DIL_GROUPS)
                outs.append(dx)
                dg_sum = dg_sum + dg
            dgs.append(dg_sum)
        return [jnp.concatenate(outs + [dvd_], axis=1)], [r_q[1], r_kv[1], r_kp[1]] + dgs

    dproj, dg_q_lat, dg_kv_lat, dg_kpe, dg_dq, dg_dk = rowwise(
        "proj_pre_bwd", pre_bwd_fn,
        [dq, dkv, dkper] + list(dband) + [proj] + tables, [w_q_b, w_kv_b] + post_params,
        [(P_END, MXU_DTYPE)], [Q_LORA, KV_LORA, LANES, LANES, LANES], tm=256)
    sent = emit("w_in", matmul("proj_wgrad", dproj, h, "tn", tn=512, out_dtype=MXU_DTYPE))

    def ln1_bwd_fn(rows, params):
        (dpv, dres, xv), (w_t, g, sc) = rows, params
        dhv = _dot(dpv, w_t)
        yn, n, rstd = _rms(xv, g)
        dx_n, dg = _rms_bwd(dhv * (1.0 + sc), n, rstd, g)
        return [dres + dx_n], [dg, _colsum(dhv * yn), _colsum(dhv)]

    grad_x, dg_mix, dsc1, dsh1 = rowwise("proj_dgrad", ln1_bwd_fn, [dproj, dx1, x], [w_in, w["g_mix_norm"], sc1],
                                         [(D_MODEL, F32)], [D_MODEL] * 3, tm=256, dep=sent)
    dmod = jnp.concatenate([dsh1, dsc1, dg1, dsh2, dsc2, dg2], axis=-1)
    small = {"loss": loss, "b_ada": dmod, "g_mix_norm": dg_mix, "g_q_lat": dg_q_lat, "g_kv_lat": dg_kv_lat,
             "g_mla_q_nope": dg_q[:, :NOPE], "g_mla_q_pe": dg_q[:, NOPE:NOPE + ROPE], "g_mla_k_nope": dg_k[:, :NOPE],
             "g_mla_k_pe": dg_kpe[:, KPE_LO:KPE_LO + ROPE], "g_dil_q": dg_dq[:, :DIL_DIM] + dg_dq[:, DIL_DIM:],
             "g_dil_k": dg_dk[:, :DIL_DIM] + dg_dk[:, DIL_DIM:], "g_ffn_norm": dg_ffn,
             "b_conv": g_b_conv}
    return grad_x, small


COL_SHARDED = ("w_kv_b", "w_conv")
ROW_SHARDED = ("w_o", "w_down") + TRANSPOSED
ADAM_TILE = {"w_ada": 256, "w_up": 176, "w_down": 176}
GATHER_GROUPS = (("w_in",), ("w_q_b", "w_kv_b"), ("w_o",), ("w_up", "w_conv", "w_down"))
START_STAGES = ((0, 1), (2, 3))
FORWARD_STAGES = ((0, 1), (2,), (3,))
FORWARD_WITH = {"w_o": 2}
SCATTER_GROUPS = (("w_down", "w_conv", "w_up"), ("w_o",), ("w_q_b", "w_kv_b", "w_in"))
OUT_WEIGHTS = ("w_ada", "b_ada", "g_mix_norm", "w_in", "g_q_lat", "w_q_b", "g_kv_lat", "w_kv_b", "g_mla_q_nope", "g_mla_q_pe",
               "g_mla_k_nope", "g_mla_k_pe", "g_dil_q", "g_dil_k", "w_o", "g_ffn_norm", "w_up", "w_conv", "b_conv", "w_down")


def kernel(x, c, positions, w_ada, b_ada, g_mix_norm, w_in, g_q_lat, w_q_b, g_kv_lat, w_kv_b, g_mla_q_nope, g_mla_q_pe, g_mla_k_nope, g_mla_k_pe, g_dil_q, g_dil_k, w_o, g_ffn_norm, w_up, w_conv, b_conv, w_down, loss_target, m_w_ada, m_b_ada, m_g_mix_norm, m_w_in, m_g_q_lat, m_w_q_b, m_g_kv_lat, m_w_kv_b, m_g_mla_q_nope, m_g_mla_q_pe, m_g_mla_k_nope, m_g_mla_k_pe, m_g_dil_q, m_g_dil_k, m_w_o, m_g_ffn_norm, m_w_up, m_w_conv, m_b_conv, m_w_down, v_w_ada, v_b_ada, v_g_mix_norm, v_w_in, v_g_q_lat, v_w_q_b, v_g_kv_lat, v_w_kv_b, v_g_mla_q_nope, v_g_mla_q_pe, v_g_mla_k_nope, v_g_mla_k_pe, v_g_dil_q, v_g_dil_k, v_w_o, v_g_ffn_norm, v_w_up, v_w_conv, v_b_conv, v_w_down):
    args = dict(locals())
    xi, yi, ci = _place()
    me = 4 * xi + 2 * yi + ci
    def local(prefix, n):
        a = args[prefix + n]
        if n in ROWS_APART:
            return jnp.transpose(a, (2, 0, 1) if n in TRANSPOSED else (1, 0, 2))
        return a[0].T if n in TRANSPOSED else a[0]

    def as_output(n, r):
        if n in ROWS_APART:
            return jnp.transpose(r, (1, 2, 0) if n in TRANSPOSED else (1, 0, 2))
        return (r.T if n in TRANSPOSED else r)[None]

    shard = {n: local("", n) for n in COL_SHARDED + ROW_SHARDED + ("w_ada",)}
    flat = lambda n, a: a.reshape(a.shape[0], a.shape[-1]) if n in ROWS_APART else a
    small_w = {n: args[n] for n in SMALL_PARAMS}

    payload = {n: flat(n, shard[n]) if n == "w_conv" else flat(n, shard[n]).astype(MXU_DTYPE) for n in COL_SHARDED + ROW_SHARDED}
    start_order = [[n for i in groups for n in GATHER_GROUPS[i]] for groups in START_STAGES]
    exchange_of = lambda i: [e for e, groups in enumerate(START_STAGES) if i in groups][0]
    start_stage = lambda e, after: exchange_start(f"gather_start{e}", [payload[n] for n in start_order[e]], gather=True,
                                                  after=after, tree=True)
    gathered = {0: start_stage(0, None)}
    after_start = gathered[0][-1]
    full, forwarded = {}, set()

    sc_all, mod_all = ada_modulation("ada_mod", c, shard["w_ada"], after_start)

    def forward(stage, after):
        e = exchange_of(FORWARD_STAGES[stage][0])
        if stage not in forwarded:
            forwarded.add(stage)
            first = start_order[e].index(GATHER_GROUPS[FORWARD_STAGES[stage][0]][0])
            count = sum(len(GATHER_GROUPS[i]) for i in FORWARD_STAGES[stage])
            starts_next = e + 1 < len(START_STAGES) and e + 1 not in gathered
            ready = [payload[n] for n in start_order[e + 1]] if starts_next else []
            gathered[e] = exchange_forward(f"gather_forward{stage}", gathered[e], [after] + ready, first, count)
            if starts_next:
                gathered[e + 1] = start_stage(e + 1, gathered[e][-1])
        return gathered[e][-1]

    def fetch(name, after):
        if name not in full:
            (i, grp), = [(i, grp) for i, grp in enumerate(GATHER_GROUPS) if name in grp]
            (stage,) = [s for s, groups in enumerate(FORWARD_STAGES) if i in groups]
            forward(stage, after)
            if name in FORWARD_WITH:
                forward(FORWARD_WITH[name], after)
            e = exchange_of(i)
            behind = gathered[e + 1][-1] if e + 1 in gathered else after
            srcs, lands = exchange_wait(f"gather{i}_wait", gathered[e], True, behind, start_order[e].index(grp[0]), len(grp), tree=True)
            for n, src, land in zip(grp, srcs, lands, strict=True):
                stack = lax.dynamic_update_index_in_dim(land, src, me, 0)
                full[n] = to_kernel_layout(n, _gather_cols(stack) if n in COL_SHARDED else _gather_rows(stack))
        return full[name]

    mod_row = lax.dynamic_index_in_dim(mod_all, me, axis=1, keepdims=False).reshape(1, 6 * D_MODEL)
    (mod,) = rowwise("ada_bias", lambda rows, params: ([rows[0] + rows[1]], []), [mod_row, b_ada], [], [(6 * D_MODEL, F32)],
                     dep=after_start)

    own, pending, scatters = {}, {}, {}

    def emit(name, grad):
        grad = from_kernel_layout(name, grad)
        parts = _scatter_cols(grad) if name in COL_SHARDED else _scatter_rows(grad)
        own[name] = lax.dynamic_index_in_dim(parts, me, 0, keepdims=False)
        pending[name] = parts
        for i, grp in enumerate(SCATTER_GROUPS):
            if name == grp[-1]:
                scatters[i] = exchange_start(f"scatter{i}_start", [pending[n] for n in grp], gather=False)
                return scatters[i][-1]
        return None

    pos = positions.reshape(SEQ, 1).astype(F32)
    grad_x, small = _local_step(x[0], pos, mod, loss_target[0], small_w, fetch, emit, halfway=lambda after: forward(1, after))

    res, done = {}, grad_x
    for i, grp in enumerate(SCATTER_GROUPS):
        _, lands = exchange_wait(f"scatter{i}_wait", scatters[i], False, done)
        for n, land in zip(grp, lands, strict=True):
            res[n] = adamw(f"adamw_{n}", shard[n], [own[n], land], local("m_", n), local("v_", n), ADAM_TILE.get(n))
            done = res[n][0]
            res[n] = [as_output(n, r) for r in res[n]]
    (small_all,) = all_gather("gather_small", [_pack_small(small)], after=done)
    loss, small_res = adamw_small("adamw_small", small_all, {n: (args[n], args["m_" + n], args["v_" + n]) for n in SMALL_PARAMS})
    row, _, n_mod = SMALL_AT["b_ada"]
    dmod_all = small_all[:, row:row + n_mod // SMALL_COLS, :].reshape(N_DEV, n_mod)
    dmod_mine = lax.dynamic_slice_in_dim(dmod_all, me * (6 * D_MODEL // N_DEV), 6 * D_MODEL // N_DEV, axis=1)
    g_w_ada = matmul("ada_wgrad", sc_all, dmod_mine, "tn")
    res["w_ada"] = [r[None] for r in adamw("adamw_w_ada", shard["w_ada"], [g_w_ada], m_w_ada[0], v_w_ada[0], ADAM_TILE["w_ada"])]

    def leaf(kind, n):
        return res[n][kind] if n in res else small_res[n][kind]

    return (loss.reshape(()), grad_x[None], *[leaf(k, n) for k in range(4) for n in OUT_WEIGHTS])
```

```python
import jax
import jax.numpy as jnp
from jax import lax
from jax.experimental import pallas as pl
from jax.experimental.pallas import tpu as pltpu

F32 = jnp.float32
MXU_DTYPE = jnp.bfloat16

N_DEV = 8
D_MODEL = 1024
SEQ = 2048
HEADS = 8
NOPE = 64
ROPE = 32
Q_LORA = 512
KV_LORA = 256
DIL_DIM = 64
DIL_WIDTH = HEADS * DIL_DIM
DILATIONS = (1, 4, 16)
SPAN = 128
D_FF = 2816
LANES = 128
SUBLANES = 8
ROPE_THETA = 10000.0
EPS = 1e-6
NEG_INF = -1e30
ADAM_LR, ADAM_B1, ADAM_B2, ADAM_EPS, ADAM_WD, ADAM_STEP = 0.001, 0.9, 0.999, 1e-08, 0.01, 10
VMEM_LIMIT = 56 * 1024 * 1024
MESH_ID = pl.DeviceIdType.MESH

P_QLAT, P_KVLAT, P_KPE, P_QD, P_KD, P_VD, P_END = 0, 512, 768, 896, 1408, 1920, 2432
KPE_LO = 64
MIX_IN = HEADS * LANES + DIL_WIDTH


def _params(**kw):
    return pltpu.CompilerParams(vmem_limit_bytes=VMEM_LIMIT, **kw)


def rowwise(name, fn, rows, params, out_rows, out_accs=(), tm=512, dep=None):
    deps = [] if dep is None else [dep]
    rows = [r if isinstance(r, tuple) else (r, r.shape[1], 0) for r in rows]
    R = rows[0][0].shape[0]
    tm = min(tm, R)
    steps = R // tm
    assert steps * tm == R
    in_specs = []
    for a, width, cb in rows:
        ri = a.shape[0]
        per = ri // tm
        assert per * tm == ri
        if ri == R:
            in_specs.append(pl.BlockSpec((tm, width), lambda i, cb=cb: (i, cb)))
        else:
            in_specs.append(pl.BlockSpec((tm, width), lambda i, per=per, cb=cb: (i % per, cb)))
    for p in params:
        in_specs.append(pl.BlockSpec(p.shape, lambda i: (0,) * p.ndim))
    in_specs += [pl.BlockSpec(memory_space=pl.ANY)] * len(deps)
    out_shape = [jax.ShapeDtypeStruct((R, d), dt) for d, dt in out_rows]
    out_specs = [pl.BlockSpec((tm, d), lambda i: (i, 0)) for d, _ in out_rows]
    out_shape += [jax.ShapeDtypeStruct((1, n), F32) for n in out_accs]
    out_specs += [pl.BlockSpec((1, n), lambda i: (0, 0)) for n in out_accs]
    nr, npar, no, na = len(rows), len(params), len(out_rows), len(out_accs)

    def body(*refs):
        rvals = [r[...] for r in refs[:nr]]
        pvals = [r[...] for r in refs[nr:nr + npar]]
        outs, accs = fn(rvals, pvals)
        first_out = nr + npar + len(deps)
        for ref, v in zip(refs[first_out:first_out + no], outs, strict=True):
            ref[...] = v.astype(ref.dtype)
        if na:
            acc_refs = refs[first_out + no:]
            i = pl.program_id(0)

            @pl.when(i == 0)
            def _():
                for ref, v in zip(acc_refs, accs, strict=True):
                    ref[...] = v

            @pl.when(i > 0)
            def _():
                for ref, v in zip(acc_refs, accs, strict=True):
                    ref[...] += v

    res = pl.pallas_call(body, name=name, grid=(steps,), in_specs=in_specs, out_specs=out_specs,
                         out_shape=out_shape, compiler_params=_params())(*[r[0] for r in rows], *params, *deps)
    return list(res)


_DIMS = {"nn": ((1,), (0,)), "nt": ((1,), (1,)), "tn": ((0,), (0,))}


def _dot(a, b, mode="nn"):
    return lax.dot_general(a.astype(MXU_DTYPE), b.astype(MXU_DTYPE), (_DIMS[mode], ((), ())),
                           preferred_element_type=F32)


def matmul(name, a, b, mode, tm=None, tn=None, tk=None, out_dtype=F32, dep=None):
    if mode == "tn":
        K, M = a.shape
    else:
        M, K = a.shape
    N = b.shape[0] if mode == "nt" else b.shape[1]
    tm, tn, tk = tm or M, tn or N, tk or K
    nm, nn, nk = M // tm, N // tn, K // tk
    assert nm * tm == M and nn * tn == N and nk * tk == K
    a_spec = pl.BlockSpec((tk, tm), lambda i, j, k: (k, i)) if mode == "tn" else pl.BlockSpec((tm, tk), lambda i, j, k: (i, k))
    b_spec = pl.BlockSpec((tn, tk), lambda i, j, k: (j, k)) if mode == "nt" else pl.BlockSpec((tk, tn), lambda i, j, k: (k, j))
    deps = [] if dep is None else [dep]

    def body(a_ref, b_ref, *rest):
        o_ref, scratch = rest[len(deps)], rest[len(deps) + 1:]
        p = _dot(a_ref[...], b_ref[...], mode)
        if nk == 1:
            o_ref[...] = p.astype(o_ref.dtype)
        else:
            acc = scratch[0]
            k = pl.program_id(2)

            @pl.when(k == 0)
            def _():
                acc[...] = p

            @pl.when(k > 0)
            def _():
                acc[...] += p

            @pl.when(k == nk - 1)
            def _():
                o_ref[...] = acc[...].astype(o_ref.dtype)

    return pl.pallas_call(
        body, name=name, grid=(nm, nn, nk), in_specs=[a_spec, b_spec] + [pl.BlockSpec(memory_space=pl.ANY)] * len(deps),
        out_specs=pl.BlockSpec((tm, tn), lambda i, j, k: (i, j)),
        out_shape=jax.ShapeDtypeStruct((M, N), out_dtype),
        scratch_shapes=[pltpu.VMEM((tm, tn), F32)] if nk > 1 else [],
        compiler_params=_params())(a, b, *deps)


def _rms(x, g):
    rstd = lax.rsqrt(jnp.mean(x * x, axis=-1, keepdims=True) + EPS)
    n = x * rstd
    return n * g, n, rstd


def _rms_bwd(dy, n, rstd, g):
    dg = jnp.sum(dy * n, axis=0, keepdims=True)
    dn = dy * g
    dx = rstd * (dn - n * jnp.mean(dn * n, axis=-1, keepdims=True))
    return dx, dg


def _norm_bwd(dy, x, g):
    _, n, rstd = _rms(x, g)
    return _rms_bwd(dy, n, rstd, g)


def _colsum(v):
    return jnp.sum(v, axis=0, keepdims=True)


def _silu(x):
    return x * (1.0 / (1.0 + jnp.exp(-x)))


def _lane(shape):
    return lax.broadcasted_iota(jnp.int32, shape, 1)


def _group_mean(v, groups):
    i = lax.broadcasted_iota(jnp.int32, (LANES, LANES), 0)
    j = lax.broadcasted_iota(jnp.int32, (LANES, LANES), 1)
    g = jnp.zeros((LANES, LANES), F32)
    for lo, hi in groups:
        g = jnp.where((i >= lo) & (i < hi) & (j >= lo) & (j < hi), 1.0 / (hi - lo), g)
    head = v.astype(MXU_DTYPE)
    return _dot(head, g) + _dot(v - head.astype(F32), g)


def _in_groups(shape, groups):
    lane = _lane(shape)
    m = jnp.zeros(shape, jnp.bool_)
    for lo, hi in groups:
        m = m | ((lane >= lo) & (lane < hi))
    return m


def _grms(x, g, groups):
    rstd = lax.rsqrt(_group_mean(x * x, groups) + EPS)
    n = jnp.where(_in_groups(x.shape, groups), x * rstd, 0.0)
    return n * g, n, rstd


def _grms_bwd(dy, n, rstd, g, groups):
    dn = dy * g
    return rstd * (dn - n * _group_mean(dn * n, groups)), _colsum(dy * n)


def _rot(x, half, transpose=False):
    first = (_lane(x.shape) % (2 * half)) < half
    up = pltpu.roll(x, LANES - half, axis=1)
    down = pltpu.roll(x, half, axis=1)
    return jnp.where(first, up, -down) if transpose else jnp.where(first, -up, down)


def _rope(x, cos, sin, half):
    return x * cos + _rot(x, half) * sin


def _rope_bwd(dy, cos, sin, half):
    return dy * cos + _rot(dy * sin, half, transpose=True)


def _chunks(x):
    return [x[:, i:i + LANES] for i in range(0, x.shape[1], LANES)]


Q_GROUPS = ((0, NOPE), (NOPE, NOPE + ROPE))
K_GROUPS = ((0, NOPE),)
KPE_GROUPS = ((KPE_LO, KPE_LO + ROPE),)
DIL_GROUPS = ((0, DIL_DIM), (DIL_DIM, 2 * DIL_DIM))


def _col(width, rows=SEQ):
    return pl.BlockSpec((rows, width), lambda h: (0, h))


def _causal_tail(s, tq, fill):
    diag = s[:, s.shape[1] - tq:]
    keep = lax.broadcasted_iota(jnp.int32, diag.shape, 1) <= lax.broadcasted_iota(jnp.int32, diag.shape, 0)
    diag = jnp.where(keep, diag, fill)
    return diag if s.shape[1] == tq else jnp.concatenate([s[:, :s.shape[1] - tq], diag], axis=1)


def mla_fwd(name, q, k, v, scale, tq=256):
    S = q.shape[0]

    def body(q_ref, k_ref, v_ref, o_ref, lse_ref):
        nb = S // tq
        blk = lambda i: slice(i * tq, (i + 1) * tq)

        def scores(i):
            return _dot(q_ref[blk(i), :], k_ref[:(i + 1) * tq, :], "nt")

        def softmax(i, s):
            s = _causal_tail(s * scale, tq, NEG_INF)
            m = jnp.max(s, axis=-1, keepdims=True)
            e = jnp.exp(s - m)
            l = jnp.sum(e, axis=-1, keepdims=True)
            lse_ref[0, blk(i), :] = m + jnp.log(l)
            return (e * (1.0 / l)).astype(MXU_DTYPE)

        def weighted(i, p):
            o_ref[blk(i), :] = _dot(p, v_ref[:(i + 1) * tq, :])

        s, p_prev = scores(0), None
        for i in range(nb):
            s_next = scores(i + 1) if i + 1 < nb else None
            if p_prev is not None:
                weighted(i - 1, p_prev)
            p_prev, s = softmax(i, s), s_next
        weighted(nb - 1, p_prev)

    return pl.pallas_call(
        body, name=name, grid=(HEADS,), in_specs=[_col(LANES)] * 3,
        out_specs=[_col(LANES), pl.BlockSpec((1, S, 1), lambda h: (h, 0, 0))],
        out_shape=[jax.ShapeDtypeStruct((S, MIX_IN), F32), jax.ShapeDtypeStruct((HEADS, S, 1), F32)],
        compiler_params=_params())(q, k, v)


def mla_bwd(name, q, k, v, o, do, lse, scale, tq=256):
    S = q.shape[0]

    def body(q_ref, k_ref, v_ref, o_ref, do_ref, lse_ref, dq_ref, dkv_ref, dkpe_ref, dk_acc, dv_acc):
        dk_acc[...] = jnp.zeros_like(dk_acc)
        dv_acc[...] = jnp.zeros_like(dv_acc)
        for i in range(S // tq):
            kext = (i + 1) * tq
            blk = slice(i * tq, kext)
            qi, kk, vv = q_ref[blk, :], k_ref[:kext, :], v_ref[:kext, :]
            doi = do_ref[blk, :]
            s = _causal_tail(_dot(qi, kk, "nt") * scale, tq, NEG_INF)
            p = jnp.exp(s - lse_ref[0, blk, :])
            dp = _dot(doi, vv, "nt")
            delta = jnp.sum(doi * o_ref[blk, :], axis=-1, keepdims=True)
            ds = p * (dp - delta) * scale
            dq_ref[blk, :] = _dot(ds, kk)
            dk_acc[:kext, :] += _dot(ds, qi, "tn")
            dv_acc[:kext, :] += _dot(p, doi, "tn")
        dk = dk_acc[...]
        lane = _lane(dk.shape)
        dkv_ref[...] = jnp.where(lane < NOPE, dk, 0.0) + dv_acc[...]
        dkpe = jnp.where((lane >= KPE_LO) & (lane < KPE_LO + ROPE), dk, 0.0)
        h = pl.program_id(0)

        @pl.when(h == 0)
        def _():
            dkpe_ref[...] = dkpe

        @pl.when(h > 0)
        def _():
            dkpe_ref[...] += dkpe

    return pl.pallas_call(
        body, name=name, grid=(HEADS,),
        in_specs=[_col(LANES)] * 5 + [pl.BlockSpec((1, S, 1), lambda h: (h, 0, 0))],
        out_specs=[_col(LANES), _col(LANES), pl.BlockSpec((S, LANES), lambda h: (0, 0))],
        out_shape=[jax.ShapeDtypeStruct((S, HEADS * LANES), F32), jax.ShapeDtypeStruct((S, HEADS * LANES), F32),
                   jax.ShapeDtypeStruct((S, LANES), F32)],
        scratch_shapes=[pltpu.VMEM((S, LANES), F32), pltpu.VMEM((S, LANES), F32)],
        compiler_params=_params())(q, k, v, o, do, lse)


BAND_TQ = SPAN


def _band_blocks(L, tq):
    return [(i * tq, (i + 1) * tq, max(0, i * tq - SPAN)) for i in range(L // tq)]


def _class_rows(r, dil, lo, hi):
    return pl.ds(r + dil * lo, hi - lo, stride=dil) if dil > 1 else pl.ds(lo, hi - lo)


def _stack_heads(t, lo):
    zero = jnp.zeros_like(t)
    return jnp.concatenate([jnp.where(lo, t, zero), jnp.where(lo, zero, t)], axis=0)


def _band_mask2(q0, q1, k0):
    n = q1 - q0
    shape = (2 * n, q1 - k0)
    i = lax.broadcasted_iota(jnp.int32, shape, 0)
    dist = (jnp.where(i >= n, i - n, i) + q0) - (lax.broadcasted_iota(jnp.int32, shape, 1) + k0)
    return (dist >= 0) & (dist <= SPAN)


def _pair_col(col0=0):
    return pl.BlockSpec((SEQ, LANES), lambda j: (0, col0 // LANES + j))


def band_fwd(name, q, k, v, dil, dep=None):
    S = q.shape[0]
    L = S // dil
    tq = BAND_TQ
    scale = DIL_DIM ** -0.5
    deps = [] if dep is None else [dep]

    def body(q_ref, k_ref, v_ref, *rest):
        o_ref, lse_ref = rest[len(deps):]
        items = [(r, blk) for r in range(dil) for blk in _band_blocks(L, tq)]
        lo = _lane((tq, LANES)) < DIL_DIM

        def scores(item):
            r, (q0, q1, k0) = item
            qb = q_ref[_class_rows(r, dil, q0, q1), :].astype(MXU_DTYPE)
            return _dot(_stack_heads(qb, lo), k_ref[_class_rows(r, dil, k0, q1), :], "nt")

        def softmax(item, s):
            _, (q0, q1, k0) = item
            s = jnp.where(_band_mask2(q0, q1, k0), s * scale, NEG_INF)
            mx = jnp.max(s, axis=-1, keepdims=True)
            e = jnp.exp(s - mx)
            l = jnp.sum(e, axis=-1, keepdims=True)
            return (e * (1.0 / l)).astype(MXU_DTYPE), mx + jnp.log(l)

        def weighted(item, p, lse):
            r, (q0, q1, k0) = item
            pv = _dot(p, v_ref[_class_rows(r, dil, k0, q1), :])
            o_ref[_class_rows(r, dil, q0, q1), :] = jnp.where(lo, pv[:tq], pv[tq:])
            lse_ref[_class_rows(r, dil, q0, q1), :] = jnp.where(lo, lse[:tq], lse[tq:])

        s, prev = scores(items[0]), None
        for i, item in enumerate(items):
            s_next = scores(items[i + 1]) if i + 1 < len(items) else None
            if prev is not None:
                weighted(items[i - 1], *prev)
            prev, s = softmax(item, s), s_next
        weighted(items[-1], *prev)

    return pl.pallas_call(
        body, name=name, grid=(DIL_WIDTH // LANES,),
        in_specs=[_pair_col()] * 2 + [_pair_col(P_VD)] + [pl.BlockSpec(memory_space=pl.ANY)] * len(deps), out_specs=[_pair_col()] * 2,
        out_shape=[jax.ShapeDtypeStruct((S, DIL_WIDTH), F32)] * 2, compiler_params=_params())(q, k, v, *deps)


def band_bwd(name, q, k, v, lse, lse_mix, o_cat, do_cat, dil, before=None):
    S = q.shape[0]
    L = S // dil
    tq = BAND_TQ
    scale = DIL_DIM ** -0.5
    before = list(before or [])

    def body(q_ref, k_ref, v_ref, lse_ref, mix_ref, o_ref, do_ref, *rest):
        dq_ref, dk_ref, dv_ref = rest[len(before):]
        if before:
            dq0_ref, dk0_ref, dv0_ref = rest[:3]
            dk_ref[...] = dk0_ref[...]
            dv_ref[...] = dv0_ref[...]
        else:
            dk_ref[...] = jnp.zeros_like(dk_ref)
            dv_ref[...] = jnp.zeros_like(dv_ref)
        items = [(r, blk) for r in range(dil) for blk in _band_blocks(L, tq)]
        lo = _lane((tq, LANES)) < DIL_DIM
        per_head = lambda t: jnp.concatenate([t[:, 0:1], t[:, DIL_DIM:DIL_DIM + 1]], axis=0)

        def scores(item):
            r, (q0, q1, k0) = item
            qrows, krows = _class_rows(r, dil, q0, q1), _class_rows(r, dil, k0, q1)
            lse_p, dout = lse_ref[qrows, :], do_ref[qrows, :]
            w2 = per_head(jnp.exp(lse_p - mix_ref[qrows, :]))
            dd = dout * o_ref[qrows, :]
            big_d = jnp.concatenate([jnp.sum(jnp.where(lo, dd, 0.0), axis=-1, keepdims=True),
                                     jnp.sum(jnp.where(lo, 0.0, dd), axis=-1, keepdims=True)], axis=0)
            q2 = _stack_heads(q_ref[qrows, :].astype(MXU_DTYPE), lo)
            dom = (_stack_heads(dout, lo) * w2).astype(MXU_DTYPE)
            return (_dot(q2, k_ref[krows, :], "nt"), _dot(dom, v_ref[krows, :], "nt"), per_head(lse_p), w2 * big_d, q2, dom)

        def softmax_bwd(item, s, dp, lse2, wd2, q2, dom):
            _, (q0, q1, k0) = item
            p = jnp.where(_band_mask2(q0, q1, k0), jnp.exp(s * scale - lse2), 0.0)
            return p.astype(MXU_DTYPE), (p * (dp - wd2) * scale).astype(MXU_DTYPE), q2, dom

        def grads(item, p, ds, q2, dom):
            r, (q0, q1, k0) = item
            qrows, krows = _class_rows(r, dil, q0, q1), _class_rows(r, dil, k0, q1)
            dq2 = _dot(ds, k_ref[krows, :])
            dq = jnp.where(lo, dq2[:tq], dq2[tq:])
            dq_ref[qrows, :] = dq + dq0_ref[qrows, :] if before else dq
            dk_ref[krows, :] += _dot(ds, q2, "tn")
            dv_ref[krows, :] += _dot(p, dom, "tn")

        sc, prev = scores(items[0]), None
        for i, item in enumerate(items):
            sc_next = scores(items[i + 1]) if i + 1 < len(items) else None
            if prev is not None:
                grads(items[i - 1], *prev)
            prev, sc = softmax_bwd(item, *sc), sc_next
        grads(items[-1], *prev)

    cat = _pair_col(HEADS * LANES)
    return pl.pallas_call(
        body, name=name, grid=(DIL_WIDTH // LANES,),
        in_specs=[_pair_col()] * 2 + [_pair_col(P_VD)] + [_pair_col()] * 2 + [cat] * 2 + [_pair_col()] * len(before),
        out_specs=[_pair_col()] * 3, out_shape=[jax.ShapeDtypeStruct((S, DIL_WIDTH), F32)] * 3,
        compiler_params=_params())(q, k, v, lse, lse_mix, o_cat, do_cat, *before)


def combine_fwd(name, outs, lses, o_cat, tm=512):
    S = outs[0].shape[0]

    def body(o1, o2, o3, l1, l2, l3, cat_in, cat_out, mix_ref):
        ls = [l1[...], l2[...], l3[...]]
        m = jnp.maximum(jnp.maximum(ls[0], ls[1]), ls[2])
        e = [jnp.exp(l - m) for l in ls]
        den = e[0] + e[1] + e[2]
        cat_out[...] = (e[0] / den) * o1[...] + (e[1] / den) * o2[...] + (e[2] / den) * o3[...]
        mix_ref[...] = m + jnp.log(den)

    row = pl.BlockSpec((tm, DIL_WIDTH), lambda i: (i, 0))
    return pl.pallas_call(
        body, name=name, grid=(S // tm,), in_specs=[row] * 6 + [pl.BlockSpec(memory_space=pl.ANY)],
        out_specs=[pl.BlockSpec((tm, DIL_WIDTH), lambda i: (i, HEADS * LANES // DIL_WIDTH)), row],
        out_shape=[jax.ShapeDtypeStruct(o_cat.shape, F32), jax.ShapeDtypeStruct((S, DIL_WIDTH), F32)],
        input_output_aliases={6: 0}, compiler_params=_params())(*outs, *lses, o_cat)


FFN_FWD_ROWS = 512
FFN_BWD_ROWS = 256
CONV_PAD = SUBLANES


def _window(x, k):
    groups = x.reshape(-1, SUBLANES, x.shape[1])
    turned = pltpu.roll(groups, SUBLANES - k, axis=1)
    stays = lax.broadcasted_iota(jnp.int32, (groups.shape[0] - 1,) + groups.shape[1:], 1) < SUBLANES - k
    return jnp.where(stays, turned[:-1], turned[1:]).reshape(-1, x.shape[1])


def _earlier(ref, r0, rows, n):
    if r0 == 0:
        x = jnp.concatenate([jnp.zeros((SUBLANES, ref.shape[1]), F32), ref[:rows, :]], axis=0)
    else:
        x = ref[r0 - SUBLANES:r0 + rows, :]
    return _window(x, SUBLANES - n)


CONV_TC = 256
CONV_NB = D_FF // CONV_TC


def _half_specs(rows, rows_axis=False):
    if rows_axis:
        return [pl.BlockSpec((rows, D_MODEL), lambda j: (j, 0)), pl.BlockSpec((rows, D_MODEL), lambda j: (j + CONV_NB, 0))]
    return [pl.BlockSpec((rows, CONV_TC), lambda j: (0, j)), pl.BlockSpec((rows, CONV_TC), lambda j: (0, j + CONV_NB))]


def _whole(a):
    return pl.BlockSpec(a.shape, lambda j: (0,) * a.ndim)


def _up_pair(h, ug_ref, uv_ref):
    return jnp.concatenate([_dot(h, ug_ref[...], "nt"), _dot(h, uv_ref[...], "nt")], axis=1)


def _conv_taps(up_ref, r0, rows, w, b):
    uin, u1, u2 = up_ref[r0:r0 + rows, :], _earlier(up_ref, r0, rows, 1), _earlier(up_ref, r0, rows, 2)
    return uin, u1, u2, w[2:3, :] * uin + w[1:2, :] * u1 + w[0:1, :] * u2 + b


def ffn_fwd(name, h, w_up_t, w_conv, b_conv, w_down):
    S = h.shape[0]

    def body(h_ref, ug_ref, uv_ref, wg_ref, wv_ref, bg_ref, bv_ref, wd_ref, dn_ref, up_ref):
        @pl.when(pl.program_id(0) == 0)
        def _():
            dn_ref[...] = jnp.zeros_like(dn_ref)

        w = jnp.concatenate([wg_ref[...], wv_ref[...]], axis=1)
        b = jnp.concatenate([bg_ref[...], bv_ref[...]], axis=1)
        rows = FFN_FWD_ROWS
        starts = list(range(0, S, rows))

        def project(r0):
            up_ref[r0:r0 + rows, :] = _up_pair(h_ref[r0:r0 + rows, :], ug_ref, uv_ref)

        def gate(r0):
            u = _conv_taps(up_ref, r0, rows, w, b)[3]
            return (_silu(u[:, :CONV_TC]) * u[:, CONV_TC:]).astype(MXU_DTYPE)

        def project_down(r0, act):
            dn_ref[r0:r0 + rows, :] += _dot(act, wd_ref[...])

        project(starts[0])
        act_prev = None
        for i, r0 in enumerate(starts):
            if i + 1 < len(starts):
                project(starts[i + 1])
            if act_prev is not None:
                project_down(starts[i - 1], act_prev)
            act_prev = gate(r0)
        project_down(starts[-1], act_prev)

    return pl.pallas_call(
        body, name=name, grid=(CONV_NB,),
        in_specs=[_whole(h)] + _half_specs(CONV_TC, rows_axis=True) + _half_specs(3) + _half_specs(1)
        + [pl.BlockSpec((CONV_TC, w_down.shape[1]), lambda j: (j, 0))],
        out_specs=[pl.BlockSpec((S, w_down.shape[1]), lambda j: (0, 0)), pl.BlockSpec((S, 2 * CONV_TC), lambda j: (0, j))],
        out_shape=[jax.ShapeDtypeStruct((S, w_down.shape[1]), F32), jax.ShapeDtypeStruct((S, 2 * D_FF), F32)],
        compiler_params=_params())(h, w_up_t, w_up_t, w_conv, w_conv, b_conv, b_conv, w_down)


def ffn_bwd(name, h, up, w_up_t, w_conv, b_conv, d_dn, w_down):
    S, D = h.shape

    def body(h_ref, up_ref, ug_ref, uv_ref, wg_ref, wv_ref, bg_ref, bv_ref, dd_ref, wd_ref,
             dh_ref, gup_ref, gd_ref, dwg_ref, dwv_ref, dbg_ref, dbv_ref, du_ref, dup_ref, act_ref):
        @pl.when(pl.program_id(0) == 0)
        def _():
            dh_ref[...] = jnp.zeros_like(dh_ref)

        w = jnp.concatenate([wg_ref[...], wv_ref[...]], axis=1)
        b = jnp.concatenate([bg_ref[...], bv_ref[...]], axis=1)
        w_pair = jnp.concatenate([ug_ref[...], uv_ref[...]], axis=0)
        rows = FFN_BWD_ROWS
        starts = list(range(0, S, rows))
        du_ref[S:S + CONV_PAD, :] = jnp.zeros((CONV_PAD, 2 * CONV_TC), F32)

        def project(r0):
            return _dot(dd_ref[r0:r0 + rows, :], wd_ref[...], "nt")

        def through_conv(r0, da):
            uin, u1, u2, u = _conv_taps(up_ref, r0, rows, w, b)
            gate, val = u[:, :CONV_TC], u[:, CONV_TC:]
            sig = 1.0 / (1.0 + jnp.exp(-gate))
            du = jnp.concatenate([da * val * (sig * (1.0 + gate * (1.0 - sig))), da * (gate * sig)], axis=1)
            du_ref[r0:r0 + rows, :] = du
            act_ref[r0:r0 + rows, :] = (gate * sig * val).astype(MXU_DTYPE)
            dw = jnp.concatenate([_colsum(du * u2), _colsum(du * u1), _colsum(du * uin)], axis=0)
            return dw, _colsum(du)

        def back_up(r0):
            du = du_ref[r0:r0 + rows + CONV_PAD, :]
            dup = (w[2:3, :] * du[:rows] + w[1:2, :] * _window(du, 1) + w[0:1, :] * _window(du, 2)).astype(MXU_DTYPE)
            dup_ref[r0:r0 + rows, :] = dup
            dh_ref[r0:r0 + rows, :] += _dot(dup, w_pair)

        dw, db = 0.0, 0.0
        da = project(starts[0])
        for i, r0 in enumerate(starts):
            da_next = project(starts[i + 1]) if i + 1 < len(starts) else None
            dw_c, db_c = through_conv(r0, da)
            if i > 0:
                back_up(starts[i - 1])
            dw, db, da = dw + dw_c, db + db_c, da_next
        back_up(starts[-1])
        g_up, g_dn = _dot(dup_ref[...], h_ref[...], "tn"), _dot(act_ref[...], dd_ref[...], "tn")
        gup_ref[0], gup_ref[1] = g_up[:CONV_TC].astype(gup_ref.dtype), g_up[CONV_TC:].astype(gup_ref.dtype)
        gd_ref[...] = g_dn.astype(gd_ref.dtype)
        dwg_ref[...], dwv_ref[...] = dw[:, :CONV_TC], dw[:, CONV_TC:]
        dbg_ref[...], dbv_ref[...] = db[:, :CONV_TC], db[:, CONV_TC:]

    half = lambda rows: pl.BlockSpec((rows, CONV_TC), lambda j: (0, j))
    rows_blk = pl.BlockSpec((CONV_TC, D), lambda j: (j, 0))
    dh, gup, gd, dwg, dwv, dbg, dbv = pl.pallas_call(
        body, name=name, grid=(CONV_NB,),
        in_specs=[_whole(h), pl.BlockSpec((S, 2 * CONV_TC), lambda j: (0, j))] + _half_specs(CONV_TC, rows_axis=True) + _half_specs(3)
        + _half_specs(1) + [_whole(d_dn), rows_blk],
        out_specs=[pl.BlockSpec((S, D), lambda j: (0, 0)), pl.BlockSpec((2, CONV_TC, D), lambda j: (0, j, 0)), rows_blk,
                   half(3), half(3), half(1), half(1)],
        out_shape=[jax.ShapeDtypeStruct((S, D), F32), jax.ShapeDtypeStruct((2, D_FF, D), MXU_DTYPE),
                   jax.ShapeDtypeStruct((D_FF, D), MXU_DTYPE)]
        + [jax.ShapeDtypeStruct((3, D_FF), F32)] * 2 + [jax.ShapeDtypeStruct((1, D_FF), F32)] * 2,
        scratch_shapes=[pltpu.VMEM((S + CONV_PAD, 2 * CONV_TC), F32), pltpu.VMEM((S, 2 * CONV_TC), MXU_DTYPE),
                        pltpu.VMEM((S, CONV_TC), MXU_DTYPE)],
        compiler_params=_params())(h, up, w_up_t, w_up_t, w_conv, w_conv, b_conv, b_conv, d_dn, w_down)
    return dh, gup.reshape(2 * D_FF, D), gd, jnp.concatenate([dwg, dwv], axis=1), jnp.concatenate([dbg, dbv], axis=1)


def adamw(name, w, parts, m, v, tr=None):
    apart = w.ndim == 3
    R, C = w.shape[0], w.shape[-1]
    tr = tr or R
    assert R % tr == 0
    c1 = 1.0 - ADAM_B1 ** ADAM_STEP
    c2 = 1.0 - ADAM_B2 ** ADAM_STEP
    np_ = len(parts)

    def body(*refs):
        w_ref, m_ref, v_ref = refs[0], refs[1 + np_], refs[2 + np_]
        go_ref, d_ref, mo_ref, vo_ref = refs[3 + np_:]
        terms = []
        for part, ref in zip(parts, refs[1:1 + np_], strict=True):
            terms += [ref[...]] if part.ndim == 2 else [ref[p] for p in range(part.shape[0])]
        g = terms[0].astype(F32)
        for term in terms[1:]:
            g = g + term.astype(F32)
        m2 = ADAM_B1 * m_ref[...] + (1.0 - ADAM_B1) * g
        v2 = ADAM_B2 * v_ref[...] + (1.0 - ADAM_B2) * (g * g)
        go_ref[...] = g
        mo_ref[...] = m2
        vo_ref[...] = v2
        d_ref[...] = -ADAM_LR * ((m2 / c1) / (jnp.sqrt(v2 / c2) + ADAM_EPS) + ADAM_WD * w_ref[...])

    blk = pl.BlockSpec((tr, C), lambda i: (i, 0))
    own = pl.BlockSpec((tr, None, C), lambda i: (i, 0, 0)) if apart else blk
    part_specs = [blk if p.ndim == 2 else pl.BlockSpec((p.shape[0], tr, C), lambda i: (0, i, 0)) for p in parts]
    return pl.pallas_call(
        body, name=name, grid=(R // tr,),
        in_specs=[own] + part_specs + [own, own], out_specs=[own] * 4,
        out_shape=[jax.ShapeDtypeStruct(w.shape, F32)] * 4, compiler_params=_params())(w, *parts, m, v)


def _place():
    return lax.axis_index("x"), lax.axis_index("y"), lax.axis_index("c")


def all_gather(name, arrs, after=None):
    n = len(arrs)
    deps = [] if after is None else [after]

    def body(*refs):
        ins, outs = refs[:n], refs[n + len(deps):2 * n + len(deps)]
        send_sems, recv_sems, local_sems = refs[2 * n + len(deps):]
        x, y, c = _place()
        me, sibling = (x, y, c), (x, y, 1 - c)
        chips = [(1 - x, y), (x, 1 - y), (1 - x, 1 - y)]
        sends = []
        for t in range(n):
            out = outs[t]

            def slot(px, py, pc, out=out):
                return out.at[4 * px + 2 * py + pc]

            def copy(k, block, to, src=None, t=t, slot=slot):
                return pltpu.make_async_remote_copy(
                    src_ref=slot(*block) if src is None else src, dst_ref=slot(*block),
                    send_sem=send_sems.at[7 * t + k], recv_sem=recv_sems.at[7 * t + k],
                    device_id=to, device_id_type=MESH_ID)

            mine = pltpu.make_async_copy(ins[t], slot(*me), local_sems.at[t])
            mine.start()
            first = [copy(0, me, sibling, src=ins[t])]
            first += [copy(1 + j, me, (*chip, c), src=ins[t]) for j, chip in enumerate(chips)]
            for cp in first:
                cp.start()
            sends.append((mine, first, copy))
        for t in range(n):
            mine, first, copy = sends[t]
            passed = [copy(4 + j, (*chip, c), sibling) for j, chip in enumerate(chips)]
            for j, chip in enumerate(chips):
                copy(1 + j, (*chip, c), me).wait_recv()
                passed[j].start()
            copy(0, sibling, me).wait_recv()
            for j, chip in enumerate(chips):
                copy(4 + j, (*chip, 1 - c), me).wait_recv()
            for cp in first + passed:
                cp.wait_send()
            mine.wait()

    any_spec = pl.BlockSpec(memory_space=pl.ANY)
    res = pl.pallas_call(
        body, name=name, in_specs=[any_spec] * (n + len(deps)), out_specs=[any_spec] * n,
        out_shape=[jax.ShapeDtypeStruct((N_DEV,) + a.shape, a.dtype) for a in arrs],
        scratch_shapes=[pltpu.SemaphoreType.DMA((7 * n,)), pltpu.SemaphoreType.DMA((7 * n,)), pltpu.SemaphoreType.DMA((n,))],
        compiler_params=pltpu.CompilerParams(has_side_effects=True))(*arrs, *deps)
    return list(res)


def ada_modulation(name, c, w_ada):
    n_mod = w_ada.shape[1]

    def exchange(src_ref, dst_ref, send_sems, recv_sems):
        x, y, c_ = _place()
        me = 4 * x + 2 * y + c_
        copies = []
        for k in range(1, N_DEV):
            px, py, pc = x ^ (k >> 2), y ^ ((k >> 1) & 1), c_ ^ (k & 1)
            copies.append(pltpu.make_async_remote_copy(
                src_ref=src_ref, dst_ref=dst_ref.at[me], send_sem=send_sems.at[k - 1], recv_sem=recv_sems.at[k - 1],
                device_id=(px, py, pc), device_id_type=MESH_ID))
        for cp in copies:
            cp.start()
        for cp in copies:
            cp.wait_recv()
        for cp in copies:
            cp.wait_send()
        return me

    def body(c_ref, w_ref, sc_ref, mod_ref, c_all, send_c, recv_c, send_m, recv_m):
        me = exchange(c_ref, c_all, send_c, recv_c)
        c_all[me] = c_ref[...]
        sc = _silu(jnp.concatenate([c_all[p] for p in range(N_DEV)], axis=0))
        sc_ref[...] = sc.astype(sc_ref.dtype)
        mod_ref[me] = _dot(sc, w_ref[...])
        exchange(mod_ref.at[me], mod_ref, send_m, recv_m)

    vmem = pl.BlockSpec(memory_space=pltpu.VMEM)
    return pl.pallas_call(
        body, name=name, in_specs=[vmem, vmem], out_specs=[vmem, vmem],
        out_shape=[jax.ShapeDtypeStruct((N_DEV, c.shape[1]), MXU_DTYPE), jax.ShapeDtypeStruct((N_DEV, N_DEV, n_mod), F32)],
        scratch_shapes=[pltpu.VMEM((N_DEV, 1, c.shape[1]), F32)] + [pltpu.SemaphoreType.DMA((N_DEV - 1,))] * 4,
        compiler_params=pltpu.CompilerParams(has_side_effects=True, vmem_limit_bytes=VMEM_LIMIT))(c, w_ada)


HBM_SPEC = pl.BlockSpec(memory_space=pltpu.HBM)
SEM_SPEC = pl.BlockSpec(memory_space=pltpu.SEMAPHORE)
DATAFLOW = pltpu.SideEffectType.DATAFLOW_SIDE_EFFECTING


def _exchange_copies(srcs, lands, send_sems, recv_sems, gather, first=0):
    x, y, c = _place()
    me = 4 * x + 2 * y + c
    out = []
    for t, (src, land) in enumerate(zip(srcs, lands, strict=True)):
        for k in range(1, N_DEV):
            px, py, pc = x ^ (k >> 2), y ^ ((k >> 1) & 1), c ^ (k & 1)
            sem = 7 * (first + t) + k - 1
            out.append((k, pltpu.make_async_remote_copy(
                src_ref=src if gather else src.at[4 * px + 2 * py + pc],
                dst_ref=land.at[me] if gather else land.at[k - 1],
                send_sem=send_sems.at[sem], recv_sem=recv_sems.at[sem],
                device_id=(px, py, pc), device_id_type=MESH_ID)))
    return out


TREE_DIRECT = (1, 2, 4, 6)
TREE_FORWARDED = (3, 5, 7)


def exchange_start(name, arrs, gather, after=None, tree=False):
    n = len(arrs)
    lands = [lax.empty(((N_DEV,) + a.shape) if gather else ((N_DEV - 1,) + a.shape[1:]), a.dtype) for a in arrs]
    deps = [] if after is None else [after]

    def body(*refs):
        srcs, land_refs = refs[:n], refs[n:2 * n]
        send_sems, recv_sems = refs[2 * n + len(deps)], refs[2 * n + len(deps) + 1]
        token = refs[-1]
        for k, cp in _exchange_copies(srcs, land_refs, send_sems, recv_sems, gather):
            if not tree or k in TREE_DIRECT:
                cp.start()
        token[...] = jnp.zeros_like(token)

    hbm = lambda a: pltpu.HBM(a.shape, a.dtype)
    res = pl.pallas_call(
        body, name=name,
        out_shape=(pltpu.SemaphoreType.DMA((7 * n,)), pltpu.SemaphoreType.DMA((7 * n,)), *[hbm(a) for a in arrs],
                   *[hbm(l) for l in lands], jax.ShapeDtypeStruct((8, 128), F32)),
        in_specs=[HBM_SPEC] * (2 * n) + [pl.BlockSpec(memory_space=pl.ANY)] * len(deps),
        out_specs=(SEM_SPEC, SEM_SPEC, *[HBM_SPEC] * (2 * n), pl.BlockSpec(memory_space=pltpu.VMEM)),
        input_output_aliases={i: 2 + i for i in range(2 * n)},
        compiler_params=pltpu.CompilerParams(has_side_effects=DATAFLOW),
    )(*[pltpu.with_memory_space_constraint(a, pltpu.HBM) for a in arrs + lands], *deps)
    return res[0], res[1], list(res[2:2 + n]), list(res[2 + n:2 + 2 * n]), res[-1]


def exchange_forward(name, started, after, first=0, count=None):
    send_sems, recv_sems, srcs, lands, _ = started
    count = len(srcs) - first if count is None else count
    mine = lands[first:first + count]
    n = len(mine)

    def copies(land_refs, send_ref, recv_ref):
        x, y, c = _place()
        out = []
        for t, land in enumerate(land_refs):
            for k in (2, 4, 6):
                slot = land.at[4 * (x ^ (k >> 2)) + 2 * (y ^ ((k >> 1) & 1)) + c]
                came, goes = 7 * (first + t) + k - 1, 7 * (first + t) + (k ^ 1) - 1
                out.append((
                    pltpu.make_async_remote_copy(src_ref=slot, dst_ref=slot, send_sem=send_ref.at[came], recv_sem=recv_ref.at[came],
                                                 device_id=(x, y, c), device_id_type=MESH_ID),
                    pltpu.make_async_remote_copy(src_ref=slot, dst_ref=slot, send_sem=send_ref.at[goes], recv_sem=recv_ref.at[goes],
                                                 device_id=(x, y, 1 - c), device_id_type=MESH_ID)))
        return out

    after = list(after) if isinstance(after, (list, tuple)) else [after]

    def arrived(*refs):
        for came, _ in copies(refs[:n], refs[n], refs[n + 1]):
            came.wait_recv()

    def pass_on(*refs):
        for _, goes in copies(refs[:n], refs[n], refs[n + 1]):
            goes.start()
        refs[-1][...] = jnp.zeros_like(refs[-1])

    hbm = lambda a: pltpu.HBM(a.shape, a.dtype)
    here = pl.pallas_call(
        arrived, name=name + "_arrived", out_shape=tuple(hbm(a) for a in mine),
        in_specs=[HBM_SPEC] * n + [SEM_SPEC, SEM_SPEC] + [pl.BlockSpec(memory_space=pl.ANY)] * len(after),
        out_specs=tuple([HBM_SPEC] * n), input_output_aliases={i: i for i in range(n)},
        compiler_params=pltpu.CompilerParams(has_side_effects=DATAFLOW),
    )(*mine, send_sems, recv_sems, *after)
    res = pl.pallas_call(
        pass_on, name=name, out_shape=(*[hbm(a) for a in mine], jax.ShapeDtypeStruct((8, 128), F32)),
        in_specs=[HBM_SPEC] * n + [SEM_SPEC, SEM_SPEC],
        out_specs=(*[HBM_SPEC] * n, pl.BlockSpec(memory_space=pltpu.VMEM)), input_output_aliases={i: i for i in range(n)},
        compiler_params=pltpu.CompilerParams(has_side_effects=DATAFLOW),
    )(*here, send_sems, recv_sems)
    lands = lands[:first] + list(res[:n]) + lands[first + count:]
    return (send_sems, recv_sems, srcs, lands, res[-1])


def exchange_wait(name, started, gather, after, first=0, count=None, tree=False):
    send_sems, recv_sems, srcs, lands, _ = started
    count = len(srcs) - first if count is None else count
    srcs, lands = srcs[first:first + count], lands[first:first + count]
    n = len(srcs)

    def body(*refs):
        src_refs, land_refs = refs[:n], refs[n:2 * n]
        copies = _exchange_copies(src_refs, land_refs, refs[2 * n], refs[2 * n + 1], gather, first)
        for _, cp in copies:
            cp.wait_send()
        for k, cp in copies:
            if not tree or k in (1,) + TREE_FORWARDED:
                cp.wait_recv()

    hbm = lambda a: pltpu.HBM(a.shape, a.dtype)
    res = pl.pallas_call(
        body, name=name, out_shape=tuple(hbm(a) for a in srcs + lands),
        in_specs=[HBM_SPEC] * (2 * n) + [SEM_SPEC, SEM_SPEC, pl.BlockSpec(memory_space=pl.ANY)],
        out_specs=tuple([HBM_SPEC] * (2 * n)), input_output_aliases={i: i for i in range(2 * n)},
        compiler_params=pltpu.CompilerParams(has_side_effects=DATAFLOW),
    )(*srcs, *lands, send_sems, recv_sems, after)
    return list(res[:n]), list(res[n:])


def _gather_cols(stack):
    p, k, n = stack.shape
    return stack.transpose(1, 0, 2).reshape(k, p * n)


def _scatter_cols(full):
    k, n = full.shape
    return full.reshape(k, N_DEV, n // N_DEV).transpose(1, 0, 2)


def _gather_rows(stack):
    p, r, n = stack.shape
    return stack.reshape(p * r, n)


def _scatter_rows(full):
    r, n = full.shape
    return full.reshape(N_DEV, r // N_DEV, n)


_IN_NAT = Q_LORA + KV_LORA
TRANSPOSED = ("w_in", "w_q_b", "w_up")
ROWS_APART = ("w_in", "w_conv")


def to_kernel_layout(name, w):
    if name == "w_in":
        z = lambda n: jnp.zeros((n, w.shape[1]), w.dtype)
        return jnp.concatenate([w[:_IN_NAT], z(KPE_LO), w[_IN_NAT:_IN_NAT + ROPE], z(LANES - KPE_LO - ROPE), w[_IN_NAT + ROPE:]], axis=0)
    if name == "w_q_b":
        return jnp.pad(w.reshape(HEADS, NOPE + ROPE, -1), ((0, 0), (0, LANES - NOPE - ROPE), (0, 0))).reshape(HEADS * LANES, -1)
    if name == "w_o":
        mla = jnp.pad(w[:HEADS * NOPE].reshape(HEADS, NOPE, -1), ((0, 0), (LANES - NOPE, 0), (0, 0))).reshape(HEADS * LANES, -1)
        return jnp.concatenate([mla, w[HEADS * NOPE:]], axis=0)
    return w


def from_kernel_layout(name, g):
    if name == "w_in":
        return jnp.concatenate([g[:_IN_NAT], g[P_KPE + KPE_LO:P_KPE + KPE_LO + ROPE], g[P_QD:]], axis=0)
    if name == "w_q_b":
        return g.reshape(HEADS, LANES, -1)[:, :NOPE + ROPE, :].reshape(HEADS * (NOPE + ROPE), -1)
    if name == "w_o":
        mla = g[:HEADS * LANES].reshape(HEADS, LANES, -1)[:, LANES - NOPE:, :].reshape(HEADS * NOPE, -1)
        return jnp.concatenate([mla, g[HEADS * LANES:]], axis=0)
    return g


SMALL_COLS = 1024
SMALL_ROWS = 24
SMALL_AT = {"loss": (0, 0, 1), "b_ada": (1, 0, 6 * D_MODEL), "g_mix_norm": (7, 0, D_MODEL), "g_q_lat": (8, 0, Q_LORA),
            "g_kv_lat": (9, 0, KV_LORA), "g_mla_q_nope": (10, 0, NOPE), "g_mla_q_pe": (10, 128, ROPE),
            "g_mla_k_nope": (10, 256, NOPE), "g_mla_k_pe": (10, 384, ROPE), "g_dil_q": (10, 512, DIL_DIM),
            "g_dil_k": (10, 640, DIL_DIM), "g_ffn_norm": (11, 0, D_MODEL), "b_conv": (12, 0, 2 * D_FF)}
SMALL_PARAMS = tuple(n for n in SMALL_AT if n != "loss")


def _pack_small(values):
    by_row = {}
    for name, (row, off, n) in SMALL_AT.items():
        by_row.setdefault(row, []).append((off, values[name].reshape(-1).astype(F32)))
    out = []
    for row in sorted(by_row):
        pieces, at = [], 0
        for off, v in sorted(by_row[row], key=lambda t: t[0]):
            pieces += [jnp.zeros((off - at,), F32), v]
            at = off + v.shape[0]
        flat = jnp.concatenate(pieces)
        nrows = -(-flat.shape[0] // SMALL_COLS)
        out.append(jnp.pad(flat, (0, nrows * SMALL_COLS - flat.shape[0])).reshape(nrows, SMALL_COLS))
    packed = jnp.concatenate(out, axis=0)
    return jnp.pad(packed, ((0, SMALL_ROWS - packed.shape[0]), (0, 0)))


def _adam(w, g, m, v):
    c1 = 1.0 - ADAM_B1 ** ADAM_STEP
    c2 = 1.0 - ADAM_B2 ** ADAM_STEP
    m2 = ADAM_B1 * m + (1.0 - ADAM_B1) * g
    v2 = ADAM_B2 * v + (1.0 - ADAM_B2) * (g * g)
    return -ADAM_LR * ((m2 / c1) / (jnp.sqrt(v2 / c2) + ADAM_EPS) + ADAM_WD * w), m2, v2


def adamw_small(name, stack, params):
    flat = [a for n in SMALL_PARAMS for a in params[n]]

    def body(stack_ref, *refs):
        ins, outs = refs[:len(flat)], refs[len(flat):]
        g_all = stack_ref[0]
        for p in range(1, N_DEV):
            g_all = g_all + stack_ref[p]
        outs[0][...] = g_all[0:1, 0:1]
        for i, pname in enumerate(SMALL_PARAMS):
            row, off, n = SMALL_AT[pname]
            w_ref, m_ref, v_ref = ins[3 * i:3 * i + 3]
            go_ref, d_ref, mo_ref, vo_ref = outs[1 + 4 * i:5 + 4 * i]
            for c0 in range(0, n, SMALL_COLS):
                cn = min(SMALL_COLS, n - c0)
                r = row + c0 // SMALL_COLS
                g = g_all[r:r + 1, off:off + cn]
                cols = (slice(None), slice(c0, c0 + cn))
                d, m2, v2 = _adam(w_ref[cols], g, m_ref[cols], v_ref[cols])
                go_ref[cols], d_ref[cols], mo_ref[cols], vo_ref[cols] = g, d, m2, v2

    whole = lambda a: pl.BlockSpec(a.shape, lambda: (0,) * a.ndim)
    out_shape = [jax.ShapeDtypeStruct((1, 1), F32)] + [jax.ShapeDtypeStruct(a.shape, F32) for n in SMALL_PARAMS for a in params[n][:1] * 4]
    res = pl.pallas_call(body, name=name, in_specs=[whole(stack)] + [whole(a) for a in flat],
                         out_specs=[pl.BlockSpec(s.shape, lambda s=s: (0,) * len(s.shape)) for s in out_shape],
                         out_shape=out_shape, compiler_params=_params())(stack, *flat)
    return res[0], {n: res[1 + 4 * i:5 + 4 * i] for i, n in enumerate(SMALL_PARAMS)}


def _local_step(x, pos, mod, target, w, fetch, emit, halfway=lambda after: None):
    S = SEQ
    sh1, sc1, g1, sh2, sc2, g2 = [mod[:, i * D_MODEL:(i + 1) * D_MODEL] for i in range(6)]
    zeros = lambda n: jnp.zeros((1, n), F32)
    g_q = jnp.concatenate([w["g_mla_q_nope"], w["g_mla_q_pe"], zeros(LANES - NOPE - ROPE)], axis=1)
    g_k = jnp.concatenate([w["g_mla_k_nope"], zeros(LANES - NOPE)], axis=1)
    g_kpe = jnp.concatenate([zeros(KPE_LO), w["g_mla_k_pe"], zeros(LANES - KPE_LO - ROPE)], axis=1)
    g_dq = jnp.concatenate([w["g_dil_q"]] * 2, axis=1)
    g_dk = jnp.concatenate([w["g_dil_k"]] * 2, axis=1)
    b_conv = w["b_conv"]

    def inv_freq(d):
        return jnp.power(ROPE_THETA, -2.0 * jnp.arange(d // 2, dtype=F32) / d)

    n_m, n_d = ROPE // 2, DIL_DIM // 2
    freqs = jnp.concatenate([inv_freq(ROPE), inv_freq(DIL_DIM), jnp.zeros((LANES - n_m - n_d,), F32)]).reshape(1, LANES)

    def tables_fn(rows, params):
        (p,), (f,) = rows, params
        c, s = jnp.cos(p * f), jnp.sin(p * f)
        one, zero = jnp.ones_like(c), jnp.zeros_like(c)
        mla = lambda t, fill: jnp.concatenate([fill[:, :KPE_LO], t[:, :n_m], t[:, :n_m], fill[:, :LANES - KPE_LO - ROPE]], axis=1)
        dil = lambda t: jnp.concatenate([t[:, n_m:n_m + n_d]] * 4, axis=1)
        return [mla(c, one), mla(s, zero), dil(c), dil(s)], []

    cos_m, sin_m, cos_d, sin_d = rowwise("rope_tables", tables_fn, [pos], [freqs], [(LANES, F32)] * 4)
    tables = [cos_m, sin_m, cos_d, sin_d]
    H_M, H_D = ROPE // 2, DIL_DIM // 2

    def ln1_fn(rows, params):
        (xv,), (g, sc, sh) = rows, params
        y, _, _ = _rms(xv, g)
        return [y * (1.0 + sc) + sh], []

    (h,) = rowwise("ln1_fwd", ln1_fn, [x], [w["g_mix_norm"], sc1, sh1], [(D_MODEL, MXU_DTYPE)], dep=sin_d)
    w_in = fetch("w_in", h)

    def proj_fn(rows, params):
        (hv, cm, sm, cd, sd), (w_t, gq, gkv, gkp, gdq, gdk) = rows, params
        pv = _dot(hv, w_t, "nt")
        kper = _rope(_grms(pv[:, P_KPE:P_QD], gkp, KPE_GROUPS)[0], cm, sm, H_M)
        qd = [_rope(_grms(c, gdq, DIL_GROUPS)[0], cd, sd, H_D) for c in _chunks(pv[:, P_QD:P_KD])]
        kd = [_rope(_grms(c, gdk, DIL_GROUPS)[0], cd, sd, H_D) for c in _chunks(pv[:, P_KD:P_VD])]
        return [pv, _rms(pv[:, P_QLAT:P_KVLAT], gq)[0], _rms(pv[:, P_KVLAT:P_KPE], gkv)[0], kper,
                jnp.concatenate(qd, axis=1), jnp.concatenate(kd, axis=1)], []

    post_params = [w["g_q_lat"], w["g_kv_lat"], g_kpe, g_dq, g_dk]
    proj, qln, kvn, kper, qd_r, kd_r = rowwise(
        "proj_fwd", proj_fn, [h] + tables, [w_in] + post_params,
        [(P_END, F32), (Q_LORA, MXU_DTYPE), (KV_LORA, MXU_DTYPE), (LANES, MXU_DTYPE)] + [(DIL_WIDTH, F32)] * 2, tm=256)
    w_q_b, w_kv_b = fetch("w_q_b", qln), fetch("w_kv_b", kvn)

    def mla_proj_fn(rows, params):
        (qlv, kvlv, kp, cm, sm), (wq_t, wkv, gq, gk) = rows, params
        qv, kvv = _dot(qlv, wq_t, "nt"), _dot(kvlv, wkv)
        value_lanes = _lane(kp.shape) >= NOPE
        qs, ks, vs = [], [], []
        for qc, kc in zip(_chunks(qv), _chunks(kvv), strict=True):
            qs.append(_rope(_grms(qc, gq, Q_GROUPS)[0], cm, sm, H_M))
            ks.append(_grms(kc, gk, K_GROUPS)[0] + kp)
            vs.append(jnp.where(value_lanes, kc, 0.0))
        return [qv, kvv] + [jnp.concatenate(t, axis=1) for t in (qs, ks, vs)], []

    q, kv, q_mla, k_mla, v_mla = rowwise(
        "mla_proj", mla_proj_fn, [qln, kvn, kper, cos_m, sin_m], [w_q_b, w_kv_b, g_q, g_k],
        [(HEADS * LANES, F32)] * 2 + [(HEADS * LANES, MXU_DTYPE)] * 3, tm=256)
    mla_scale = (NOPE + ROPE) ** -0.5
    o_cat, lse_mla = mla_fwd("mla_fwd", q_mla, k_mla, v_mla, mla_scale)
    passed = halfway(lse_mla)

    band = [band_fwd(f"band{dil}_fwd", qd_r, kd_r, proj, dil, dep=passed) for dil in DILATIONS]
    o_cat, lse_mix = combine_fwd("dil_combine", [b[0] for b in band], [b[1] for b in band], o_cat)
    w_o = fetch("w_o", o_cat)

    def mid_fn(rows, params):
        (ov, xv), (w_out, gate1, g, sc, sh) = rows, params
        mx = _dot(ov, w_out)
        x1 = xv + gate1 * mx
        y, _, _ = _rms(x1, g)
        return [mx, x1, y * (1.0 + sc) + sh], []

    mix, x1, h2 = rowwise("mix_fwd", mid_fn, [o_cat, x], [w_o, g1, w["g_ffn_norm"], sc2, sh2],
                          [(D_MODEL, F32), (D_MODEL, F32), (D_MODEL, MXU_DTYPE)], tm=256)
    w_up, w_conv, w_down = fetch("w_up", h2), fetch("w_conv", h2), fetch("w_down", h2)
    dn, up = ffn_fwd("ffn_fwd", h2, w_up, w_conv, b_conv, w_down)

    def final_fn(rows, params):
        (x1v, dnv, tgt), (gate2,) = rows, params
        r = x1v + gate2 * dnv - tgt
        dy = r * (1.0 / D_MODEL)
        loss = jnp.sum(_colsum(r * r), axis=-1, keepdims=True) * (0.5 / D_MODEL)
        return [dy, gate2 * dy], [loss, _colsum(dy * dnv)]

    dy, d_dn, loss, dg2 = rowwise("loss_head", final_fn, [x1, dn, target], [g2], [(D_MODEL, F32), (D_MODEL, MXU_DTYPE)],
                                  [1, D_MODEL])
    dh2, g_up, g_down, g_w_conv, g_b_conv = ffn_bwd("ffn_bwd", h2, up, w_up, w_conv, b_conv, d_dn, w_down)
    emit("w_down", g_down)
    emit("w_conv", g_w_conv)
    sent = emit("w_up", g_up)

    def mid_bwd_fn(rows, params):
        (dh2v, dyv, x1v, mx), (gate1, g, sc) = rows, params
        yn, n, rstd = _rms(x1v, g)
        dx_n, dg = _rms_bwd(dh2v * (1.0 + sc), n, rstd, g)
        dx1 = dyv + dx_n
        return [dx1, gate1 * dx1], [dg, _colsum(dh2v * yn), _colsum(dh2v), _colsum(dx1 * mx)]

    dx1, dmix, dg_ffn, dsc2, dsh2, dg1 = rowwise(
        "mid_bwd", mid_bwd_fn, [dh2, dy, x1, mix], [g1, w["g_ffn_norm"], sc2], [(D_MODEL, F32), (D_MODEL, MXU_DTYPE)],
        [D_MODEL] * 4, dep=sent)

    sent = emit("w_o", matmul("mix_wgrad", o_cat, dmix, "tn", tm=512, out_dtype=MXU_DTYPE))
    do_cat = matmul("mix_dgrad", dmix, w_o, "nt", tm=512, dep=sent)
    dband = None
    for dil, b in zip(DILATIONS, band):
        dband = band_bwd(f"band{dil}_bwd", qd_r, kd_r, proj, b[1], lse_mix, o_cat, do_cat, dil, before=dband)
    dq_mla, dkv_mla, dkper = mla_bwd("mla_bwd", q_mla, k_mla, v_mla, o_cat, do_cat, lse_mla, mla_scale)

    def mla_prep_bwd_fn(rows, params):
        (dqv, dkvv, qv, kvv, cm, sm), (gq, gk) = rows, params
        nope_lanes = _lane(cm.shape) < NOPE
        dqs, dkvs, dgq, dgk = [], [], 0.0, 0.0
        for dqc, dkc, qc, kc in zip(_chunks(dqv), _chunks(dkvv), _chunks(qv), _chunks(kvv), strict=True):
            _, n, rstd = _grms(qc, gq, Q_GROUPS)
            dx, dg = _grms_bwd(_rope_bwd(dqc, cm, sm, H_M), n, rstd, gq, Q_GROUPS)
            dqs.append(dx)
            dgq = dgq + dg
            _, n, rstd = _grms(kc, gk, K_GROUPS)
            dx, dg = _grms_bwd(dkc, n, rstd, gk, K_GROUPS)
            dkvs.append(jnp.where(nope_lanes, dx, dkc))
            dgk = dgk + dg
        return [jnp.concatenate(dqs, axis=1), jnp.concatenate(dkvs, axis=1)], [dgq, dgk]

    dq, dkv, dg_q, dg_k = rowwise("mla_prep_bwd", mla_prep_bwd_fn, [dq_mla, dkv_mla, q, kv, cos_m, sin_m], [g_q, g_k],
                                  [(HEADS * LANES, MXU_DTYPE)] * 2, [LANES, LANES], tm=256)
    emit("w_q_b", matmul("q_wgrad", dq, qln, "tn", out_dtype=MXU_DTYPE))
    emit("w_kv_b", matmul("kv_wgrad", kvn, dkv, "tn", out_dtype=MXU_DTYPE))

    def pre_bwd_fn(rows, params):
        dqv, dkvv, dkp, dqd_, dkd_, dvd_, pv, cm, sm, cd, sd = rows
        wq_t, wkv, gq, gkv, gkp, gdq, gdk = params
        dql, dkvl = _dot(dqv, wq_t), _dot(dkvv, wkv, "nt")
        r_q = _norm_bwd(dql, pv[:, P_QLAT:P_KVLAT], gq)
        r_kv = _norm_bwd(dkvl, pv[:, P_KVLAT:P_KPE], gkv)
        _, n, rstd = _grms(pv[:, P_KPE:P_QD], gkp, KPE_GROUPS)
        r_kp = _grms_bwd(_rope_bwd(dkp, cm, sm, H_M), n, rstd, gkp, KPE_GROUPS)
        outs, dgs = [r_q[0], r_kv[0], r_kp[0]], []
        for dval, lo, g in ((dqd_, P_QD, gdq), (dkd_, P_KD, gdk)):
            dg_sum = 0.0
            for dc, xc in zip(_chunks(dval), _chunks(pv[:, lo:lo + DIL_WIDTH]), strict=True):
                _, n, rstd = _grms(xc, g, DIL_GROUPS)
                dx, dg = _grms_bwd(_rope_bwd(dc, cd, sd, H_D), n, rstd, g, DIL_GROUPS)
                outs.append(dx)
                dg_sum = dg_sum + dg
            dgs.append(dg_sum)
        return [jnp.concatenate(outs + [dvd_], axis=1)], [r_q[1], r_kv[1], r_kp[1]] + dgs

    dproj, dg_q_lat, dg_kv_lat, dg_kpe, dg_dq, dg_dk = rowwise(
        "proj_pre_bwd", pre_bwd_fn,
        [dq, dkv, dkper] + list(dband) + [proj] + tables, [w_q_b, w_kv_b] + post_params,
        [(P_END, MXU_DTYPE)], [Q_LORA, KV_LORA, LANES, LANES, LANES], tm=256)
    sent = emit("w_in", matmul("proj_wgrad", dproj, h, "tn", tn=512, out_dtype=MXU_DTYPE))

    def ln1_bwd_fn(rows, params):
        (dpv, dres, xv), (w_t, g, sc) = rows, params
        dhv = _dot(dpv, w_t)
        yn, n, rstd = _rms(xv, g)
        dx_n, dg = _rms_bwd(dhv * (1.0 + sc), n, rstd, g)
        return [dres + dx_n], [dg, _colsum(dhv * yn), _colsum(dhv)]

    grad_x, dg_mix, dsc1, dsh1 = rowwise("proj_dgrad", ln1_bwd_fn, [dproj, dx1, x], [w_in, w["g_mix_norm"], sc1],
                                         [(D_MODEL, F32)], [D_MODEL] * 3, tm=256, dep=sent)
    dmod = jnp.concatenate([dsh1, dsc1, dg1, dsh2, dsc2, dg2], axis=-1)
    small = {"loss": loss, "b_ada": dmod, "g_mix_norm": dg_mix, "g_q_lat": dg_q_lat, "g_kv_lat": dg_kv_lat,
             "g_mla_q_nope": dg_q[:, :NOPE], "g_mla_q_pe": dg_q[:, NOPE:NOPE + ROPE], "g_mla_k_nope": dg_k[:, :NOPE],
             "g_mla_k_pe": dg_kpe[:, KPE_LO:KPE_LO + ROPE], "g_dil_q": dg_dq[:, :DIL_DIM] + dg_dq[:, DIL_DIM:],
             "g_dil_k": dg_dk[:, :DIL_DIM] + dg_dk[:, DIL_DIM:], "g_ffn_norm": dg_ffn,
             "b_conv": g_b_conv}
    return grad_x, small


COL_SHARDED = ("w_kv_b", "w_conv")
ROW_SHARDED = ("w_o", "w_down") + TRANSPOSED
ADAM_TILE = {"w_ada": 256, "w_up": 176, "w_down": 176}
GATHER_GROUPS = (("w_in",), ("w_q_b", "w_kv_b"), ("w_o",), ("w_up", "w_conv", "w_down"))
START_STAGES = ((0, 1), (2, 3))
FORWARD_STAGES = ((0, 1), (2,), (3,))
FORWARD_WITH = {"w_o": 2}
SCATTER_GROUPS = (("w_down", "w_conv", "w_up"), ("w_o",), ("w_q_b", "w_kv_b", "w_in"))
OUT_WEIGHTS = ("w_ada", "b_ada", "g_mix_norm", "w_in", "g_q_lat", "w_q_b", "g_kv_lat", "w_kv_b", "g_mla_q_nope", "g_mla_q_pe",
               "g_mla_k_nope", "g_mla_k_pe", "g_dil_q", "g_dil_k", "w_o", "g_ffn_norm", "w_up", "w_conv", "b_conv", "w_down")


def kernel(x, c, positions, w_ada, b_ada, g_mix_norm, w_in, g_q_lat, w_q_b, g_kv_lat, w_kv_b, g_mla_q_nope, g_mla_q_pe, g_mla_k_nope, g_mla_k_pe, g_dil_q, g_dil_k, w_o, g_ffn_norm, w_up, w_conv, b_conv, w_down, loss_target, m_w_ada, m_b_ada, m_g_mix_norm, m_w_in, m_g_q_lat, m_w_q_b, m_g_kv_lat, m_w_kv_b, m_g_mla_q_nope, m_g_mla_q_pe, m_g_mla_k_nope, m_g_mla_k_pe, m_g_dil_q, m_g_dil_k, m_w_o, m_g_ffn_norm, m_w_up, m_w_conv, m_b_conv, m_w_down, v_w_ada, v_b_ada, v_g_mix_norm, v_w_in, v_g_q_lat, v_w_q_b, v_g_kv_lat, v_w_kv_b, v_g_mla_q_nope, v_g_mla_q_pe, v_g_mla_k_nope, v_g_mla_k_pe, v_g_dil_q, v_g_dil_k, v_w_o, v_g_ffn_norm, v_w_up, v_w_conv, v_b_conv, v_w_down):
    args = dict(locals())
    xi, yi, ci = _place()
    me = 4 * xi + 2 * yi + ci
    def local(prefix, n):
        a = args[prefix + n]
        if n in ROWS_APART:
            return jnp.transpose(a, (2, 0, 1) if n in TRANSPOSED else (1, 0, 2))
        return a[0].T if n in TRANSPOSED else a[0]

    def as_output(n, r):
        if n in ROWS_APART:
            return jnp.transpose(r, (1, 2, 0) if n in TRANSPOSED else (1, 0, 2))
        return (r.T if n in TRANSPOSED else r)[None]

    shard = {n: local("", n) for n in COL_SHARDED + ROW_SHARDED + ("w_ada",)}
    flat = lambda n, a: a.reshape(a.shape[0], a.shape[-1]) if n in ROWS_APART else a
    small_w = {n: args[n] for n in SMALL_PARAMS}

    sc_all, mod_all = ada_modulation("ada_mod", c, shard["w_ada"])

    payload = {n: flat(n, shard[n]) if n == "w_conv" else flat(n, shard[n]).astype(MXU_DTYPE) for n in COL_SHARDED + ROW_SHARDED}
    start_order = [[n for i in groups for n in GATHER_GROUPS[i]] for groups in START_STAGES]
    exchange_of = lambda i: [e for e, groups in enumerate(START_STAGES) if i in groups][0]
    start_stage = lambda e, after: exchange_start(f"gather_start{e}", [payload[n] for n in start_order[e]], gather=True,
                                                  after=after, tree=True)
    gathered = {0: start_stage(0, mod_all)}
    after_start = gathered[0][-1]
    full, forwarded = {}, set()

    def forward(stage, after):
        e = exchange_of(FORWARD_STAGES[stage][0])
        if stage not in forwarded:
            forwarded.add(stage)
            first = start_order[e].index(GATHER_GROUPS[FORWARD_STAGES[stage][0]][0])
            count = sum(len(GATHER_GROUPS[i]) for i in FORWARD_STAGES[stage])
            starts_next = e + 1 < len(START_STAGES) and e + 1 not in gathered
            ready = [payload[n] for n in start_order[e + 1]] if starts_next else []
            gathered[e] = exchange_forward(f"gather_forward{stage}", gathered[e], [after] + ready, first, count)
            if starts_next:
                gathered[e + 1] = start_stage(e + 1, gathered[e][-1])
        return gathered[e][-1]

    def fetch(name, after):
        if name not in full:
            (i, grp), = [(i, grp) for i, grp in enumerate(GATHER_GROUPS) if name in grp]
            (stage,) = [s for s, groups in enumerate(FORWARD_STAGES) if i in groups]
            forward(stage, after)
            if name in FORWARD_WITH:
                forward(FORWARD_WITH[name], after)
            e = exchange_of(i)
            behind = gathered[e + 1][-1] if e + 1 in gathered else after
            srcs, lands = exchange_wait(f"gather{i}_wait", gathered[e], True, behind, start_order[e].index(grp[0]), len(grp), tree=True)
            for n, src, land in zip(grp, srcs, lands, strict=True):
                stack = lax.dynamic_update_index_in_dim(land, src, me, 0)
                full[n] = to_kernel_layout(n, _gather_cols(stack) if n in COL_SHARDED else _gather_rows(stack))
        return full[name]

    mod_row = lax.dynamic_index_in_dim(mod_all, me, axis=1, keepdims=False).reshape(1, 6 * D_MODEL)
    (mod,) = rowwise("ada_bias", lambda rows, params: ([rows[0] + rows[1]], []), [mod_row, b_ada], [], [(6 * D_MODEL, F32)],
                     dep=after_start)

    own, pending, scatters = {}, {}, {}

    def emit(name, grad):
        grad = from_kernel_layout(name, grad)
        parts = _scatter_cols(grad) if name in COL_SHARDED else _scatter_rows(grad)
        own[name] = lax.dynamic_index_in_dim(parts, me, 0, keepdims=False)
        pending[name] = parts
        for i, grp in enumerate(SCATTER_GROUPS):
            if name == grp[-1]:
                scatters[i] = exchange_start(f"scatter{i}_start", [pending[n] for n in grp], gather=False)
                return scatters[i][-1]
        return None

    pos = positions.reshape(SEQ, 1).astype(F32)
    grad_x, small = _local_step(x[0], pos, mod, loss_target[0], small_w, fetch, emit, halfway=lambda after: forward(1, after))

    res, done = {}, grad_x
    for i, grp in enumerate(SCATTER_GROUPS):
        _, lands = exchange_wait(f"scatter{i}_wait", scatters[i], False, done)
        for n, land in zip(grp, lands, strict=True):
            res[n] = adamw(f"adamw_{n}", shard[n], [own[n], land], local("m_", n), local("v_", n), ADAM_TILE.get(n))
            done = res[n][0]
            res[n] = [as_output(n, r) for r in res[n]]
    (small_all,) = all_gather("gather_small", [_pack_small(small)], after=done)
    loss, small_res = adamw_small("adamw_small", small_all, {n: (args[n], args["m_" + n], args["v_" + n]) for n in SMALL_PARAMS})
    row, _, n_mod = SMALL_AT["b_ada"]
    dmod_all = small_all[:, row:row + n_mod // SMALL_COLS, :].reshape(N_DEV, n_mod)
    dmod_mine = lax.dynamic_slice_in_dim(dmod_all, me * (6 * D_MODEL // N_DEV), 6 * D_MODEL // N_DEV, axis=1)
    g_w_ada = matmul("ada_wgrad", sc_all, dmod_mine, "tn")
    res["w_ada"] = [r[None] for r in adamw("adamw_w_ada", shard["w_ada"], [g_w_ada], m_w_ada[0], v_w_ada[0], ADAM_TILE["w_ada"])]

    def leaf(kind, n):
        return res[n][kind] if n in res else small_res[n][kind]

    return (loss.reshape(()), grad_x[None], *[leaf(k, n) for k in range(4) for n in OUT_WEIGHTS])
```

```python
import jax
import jax.numpy as jnp
from jax import lax
from jax.experimental import pallas as pl
from jax.experimental.pallas import tpu as pltpu

F32 = jnp.float32
MXU_DTYPE = jnp.bfloat16

N_DEV = 8
D_MODEL = 1024
SEQ = 2048
HEADS = 8
NOPE = 64
ROPE = 32
Q_LORA = 512
KV_LORA = 256
DIL_DIM = 64
DIL_WIDTH = HEADS * DIL_DIM
DILATIONS = (1, 4, 16)
SPAN = 128
D_FF = 2816
LANES = 128
SUBLANES = 8
ROPE_THETA = 10000.0
EPS = 1e-6
NEG_INF = -1e30
ADAM_LR, ADAM_B1, ADAM_B2, ADAM_EPS, ADAM_WD, ADAM_STEP = 0.001, 0.9, 0.999, 1e-08, 0.01, 10
VMEM_LIMIT = 56 * 1024 * 1024
MESH_ID = pl.DeviceIdType.MESH

P_QLAT, P_KVLAT, P_KPE, P_QD, P_KD, P_VD, P_END = 0, 512, 768, 896, 1408, 1920, 2432
KPE_LO = 64
MIX_IN = HEADS * LANES + DIL_WIDTH


def _params(**kw):
    return pltpu.CompilerParams(vmem_limit_bytes=VMEM_LIMIT, **kw)


def rowwise(name, fn, rows, params, out_rows, out_accs=(), tm=512, dep=None):
    deps = [] if dep is None else [dep]
    rows = [r if isinstance(r, tuple) else (r, r.shape[1], 0) for r in rows]
    R = rows[0][0].shape[0]
    tm = min(tm, R)
    steps = R // tm
    assert steps * tm == R
    in_specs = []
    for a, width, cb in rows:
        ri = a.shape[0]
        per = ri // tm
        assert per * tm == ri
        if ri == R:
            in_specs.append(pl.BlockSpec((tm, width), lambda i, cb=cb: (i, cb)))
        else:
            in_specs.append(pl.BlockSpec((tm, width), lambda i, per=per, cb=cb: (i % per, cb)))
    for p in params:
        in_specs.append(pl.BlockSpec(p.shape, lambda i: (0,) * p.ndim))
    in_specs += [pl.BlockSpec(memory_space=pl.ANY)] * len(deps)
    out_shape = [jax.ShapeDtypeStruct((R, d), dt) for d, dt in out_rows]
    out_specs = [pl.BlockSpec((tm, d), lambda i: (i, 0)) for d, _ in out_rows]
    out_shape += [jax.ShapeDtypeStruct((1, n), F32) for n in out_accs]
    out_specs += [pl.BlockSpec((1, n), lambda i: (0, 0)) for n in out_accs]
    nr, npar, no, na = len(rows), len(params), len(out_rows), len(out_accs)

    def body(*refs):
        rvals = [r[...] for r in refs[:nr]]
        pvals = [r[...] for r in refs[nr:nr + npar]]
        outs, accs = fn(rvals, pvals)
        first_out = nr + npar + len(deps)
        for ref, v in zip(refs[first_out:first_out + no], outs, strict=True):
            ref[...] = v.astype(ref.dtype)
        if na:
            acc_refs = refs[first_out + no:]
            i = pl.program_id(0)

            @pl.when(i == 0)
            def _():
                for ref, v in zip(acc_refs, accs, strict=True):
                    ref[...] = v

            @pl.when(i > 0)
            def _():
                for ref, v in zip(acc_refs, accs, strict=True):
                    ref[...] += v

    res = pl.pallas_call(body, name=name, grid=(steps,), in_specs=in_specs, out_specs=out_specs,
                         out_shape=out_shape, compiler_params=_params())(*[r[0] for r in rows], *params, *deps)
    return list(res)


_DIMS = {"nn": ((1,), (0,)), "nt": ((1,), (1,)), "tn": ((0,), (0,))}


def _dot(a, b, mode="nn"):
    return lax.dot_general(a.astype(MXU_DTYPE), b.astype(MXU_DTYPE), (_DIMS[mode], ((), ())),
                           preferred_element_type=F32)


def matmul(name, a, b, mode, tm=None, tn=None, tk=None, out_dtype=F32, dep=None):
    if mode == "tn":
        K, M = a.shape
    else:
        M, K = a.shape
    N = b.shape[0] if mode == "nt" else b.shape[1]
    tm, tn, tk = tm or M, tn or N, tk or K
    nm, nn, nk = M // tm, N // tn, K // tk
    assert nm * tm == M and nn * tn == N and nk * tk == K
    a_spec = pl.BlockSpec((tk, tm), lambda i, j, k: (k, i)) if mode == "tn" else pl.BlockSpec((tm, tk), lambda i, j, k: (i, k))
    b_spec = pl.BlockSpec((tn, tk), lambda i, j, k: (j, k)) if mode == "nt" else pl.BlockSpec((tk, tn), lambda i, j, k: (k, j))
    deps = [] if dep is None else [dep]

    def body(a_ref, b_ref, *rest):
        o_ref, scratch = rest[len(deps)], rest[len(deps) + 1:]
        p = _dot(a_ref[...], b_ref[...], mode)
        if nk == 1:
            o_ref[...] = p.astype(o_ref.dtype)
        else:
            acc = scratch[0]
            k = pl.program_id(2)

            @pl.when(k == 0)
            def _():
                acc[...] = p

            @pl.when(k > 0)
            def _():
                acc[...] += p

            @pl.when(k == nk - 1)
            def _():
                o_ref[...] = acc[...].astype(o_ref.dtype)

    return pl.pallas_call(
        body, name=name, grid=(nm, nn, nk), in_specs=[a_spec, b_spec] + [pl.BlockSpec(memory_space=pl.ANY)] * len(deps),
        out_specs=pl.BlockSpec((tm, tn), lambda i, j, k: (i, j)),
        out_shape=jax.ShapeDtypeStruct((M, N), out_dtype),
        scratch_shapes=[pltpu.VMEM((tm, tn), F32)] if nk > 1 else [],
        compiler_params=_params())(a, b, *deps)


def _rms(x, g):
    rstd = lax.rsqrt(jnp.mean(x * x, axis=-1, keepdims=True) + EPS)
    n = x * rstd
    return n * g, n, rstd


def _rms_bwd(dy, n, rstd, g):
    dg = jnp.sum(dy * n, axis=0, keepdims=True)
    dn = dy * g
    dx = rstd * (dn - n * jnp.mean(dn * n, axis=-1, keepdims=True))
    return dx, dg


def _norm_bwd(dy, x, g):
    _, n, rstd = _rms(x, g)
    return _rms_bwd(dy, n, rstd, g)


def _colsum(v):
    return jnp.sum(v, axis=0, keepdims=True)


def _silu(x):
    return x * (1.0 / (1.0 + jnp.exp(-x)))


def _lane(shape):
    return lax.broadcasted_iota(jnp.int32, shape, 1)


def _group_mean(v, groups):
    i = lax.broadcasted_iota(jnp.int32, (LANES, LANES), 0)
    j = lax.broadcasted_iota(jnp.int32, (LANES, LANES), 1)
    g = jnp.zeros((LANES, LANES), F32)
    for lo, hi in groups:
        g = jnp.where((i >= lo) & (i < hi) & (j >= lo) & (j < hi), 1.0 / (hi - lo), g)
    head = v.astype(MXU_DTYPE)
    return _dot(head, g) + _dot(v - head.astype(F32), g)


def _in_groups(shape, groups):
    lane = _lane(shape)
    m = jnp.zeros(shape, jnp.bool_)
    for lo, hi in groups:
        m = m | ((lane >= lo) & (lane < hi))
    return m


def _grms(x, g, groups):
    rstd = lax.rsqrt(_group_mean(x * x, groups) + EPS)
    n = jnp.where(_in_groups(x.shape, groups), x * rstd, 0.0)
    return n * g, n, rstd


def _grms_bwd(dy, n, rstd, g, groups):
    dn = dy * g
    return rstd * (dn - n * _group_mean(dn * n, groups)), _colsum(dy * n)


def _rot(x, half, transpose=False):
    first = (_lane(x.shape) % (2 * half)) < half
    up = pltpu.roll(x, LANES - half, axis=1)
    down = pltpu.roll(x, half, axis=1)
    return jnp.where(first, up, -down) if transpose else jnp.where(first, -up, down)


def _rope(x, cos, sin, half):
    return x * cos + _rot(x, half) * sin


def _rope_bwd(dy, cos, sin, half):
    return dy * cos + _rot(dy * sin, half, transpose=True)


def _chunks(x):
    return [x[:, i:i + LANES] for i in range(0, x.shape[1], LANES)]


Q_GROUPS = ((0, NOPE), (NOPE, NOPE + ROPE))
K_GROUPS = ((0, NOPE),)
KPE_GROUPS = ((KPE_LO, KPE_LO + ROPE),)
DIL_GROUPS = ((0, DIL_DIM), (DIL_DIM, 2 * DIL_DIM))


def _col(width, rows=SEQ):
    return pl.BlockSpec((rows, width), lambda h: (0, h))


def _causal_tail(s, tq, fill):
    diag = s[:, s.shape[1] - tq:]
    keep = lax.broadcasted_iota(jnp.int32, diag.shape, 1) <= lax.broadcasted_iota(jnp.int32, diag.shape, 0)
    diag = jnp.where(keep, diag, fill)
    return diag if s.shape[1] == tq else jnp.concatenate([s[:, :s.shape[1] - tq], diag], axis=1)


def mla_fwd(name, q, k, v, scale, tq=256):
    S = q.shape[0]

    def body(q_ref, k_ref, v_ref, o_ref, lse_ref):
        nb = S // tq
        blk = lambda i: slice(i * tq, (i + 1) * tq)

        def scores(i):
            return _dot(q_ref[blk(i), :], k_ref[:(i + 1) * tq, :], "nt")

        def softmax(i, s):
            s = _causal_tail(s * scale, tq, NEG_INF)
            m = jnp.max(s, axis=-1, keepdims=True)
            e = jnp.exp(s - m)
            l = jnp.sum(e, axis=-1, keepdims=True)
            lse_ref[0, blk(i), :] = m + jnp.log(l)
            return (e * (1.0 / l)).astype(MXU_DTYPE)

        def weighted(i, p):
            o_ref[blk(i), :] = _dot(p, v_ref[:(i + 1) * tq, :])

        s, p_prev = scores(0), None
        for i in range(nb):
            s_next = scores(i + 1) if i + 1 < nb else None
            if p_prev is not None:
                weighted(i - 1, p_prev)
            p_prev, s = softmax(i, s), s_next
        weighted(nb - 1, p_prev)

    return pl.pallas_call(
        body, name=name, grid=(HEADS,), in_specs=[_col(LANES)] * 3,
        out_specs=[_col(LANES), pl.BlockSpec((1, S, 1), lambda h: (h, 0, 0))],
        out_shape=[jax.ShapeDtypeStruct((S, MIX_IN), F32), jax.ShapeDtypeStruct((HEADS, S, 1), F32)],
        compiler_params=_params())(q, k, v)


def mla_bwd(name, q, k, v, o, do, lse, scale, tq=256):
    S = q.shape[0]

    def body(q_ref, k_ref, v_ref, o_ref, do_ref, lse_ref, dq_ref, dkv_ref, dkpe_ref, dk_acc, dv_acc):
        dk_acc[...] = jnp.zeros_like(dk_acc)
        dv_acc[...] = jnp.zeros_like(dv_acc)
        for i in range(S // tq):
            kext = (i + 1) * tq
            blk = slice(i * tq, kext)
            qi, kk, vv = q_ref[blk, :], k_ref[:kext, :], v_ref[:kext, :]
            doi = do_ref[blk, :]
            s = _causal_tail(_dot(qi, kk, "nt") * scale, tq, NEG_INF)
            p = jnp.exp(s - lse_ref[0, blk, :])
            dp = _dot(doi, vv, "nt")
            delta = jnp.sum(doi * o_ref[blk, :], axis=-1, keepdims=True)
            ds = p * (dp - delta) * scale
            dq_ref[blk, :] = _dot(ds, kk)
            dk_acc[:kext, :] += _dot(ds, qi, "tn")
            dv_acc[:kext, :] += _dot(p, doi, "tn")
        dk = dk_acc[...]
        lane = _lane(dk.shape)
        dkv_ref[...] = jnp.where(lane < NOPE, dk, 0.0) + dv_acc[...]
        dkpe = jnp.where((lane >= KPE_LO) & (lane < KPE_LO + ROPE), dk, 0.0)
        h = pl.program_id(0)

        @pl.when(h == 0)
        def _():
            dkpe_ref[...] = dkpe

        @pl.when(h > 0)
        def _():
            dkpe_ref[...] += dkpe

    return pl.pallas_call(
        body, name=name, grid=(HEADS,),
        in_specs=[_col(LANES)] * 5 + [pl.BlockSpec((1, S, 1), lambda h: (h, 0, 0))],
        out_specs=[_col(LANES), _col(LANES), pl.BlockSpec((S, LANES), lambda h: (0, 0))],
        out_shape=[jax.ShapeDtypeStruct((S, HEADS * LANES), F32), jax.ShapeDtypeStruct((S, HEADS * LANES), F32),
                   jax.ShapeDtypeStruct((S, LANES), F32)],
        scratch_shapes=[pltpu.VMEM((S, LANES), F32), pltpu.VMEM((S, LANES), F32)],
        compiler_params=_params())(q, k, v, o, do, lse)


BAND_TQ = SPAN


def _band_blocks(L, tq):
    return [(i * tq, (i + 1) * tq, max(0, i * tq - SPAN)) for i in range(L // tq)]


def _class_rows(r, dil, lo, hi):
    return pl.ds(r + dil * lo, hi - lo, stride=dil) if dil > 1 else pl.ds(lo, hi - lo)


def _stack_heads(t, lo):
    zero = jnp.zeros_like(t)
    return jnp.concatenate([jnp.where(lo, t, zero), jnp.where(lo, zero, t)], axis=0)


def _band_mask2(q0, q1, k0):
    n = q1 - q0
    shape = (2 * n, q1 - k0)
    i = lax.broadcasted_iota(jnp.int32, shape, 0)
    dist = (jnp.where(i >= n, i - n, i) + q0) - (lax.broadcasted_iota(jnp.int32, shape, 1) + k0)
    return (dist >= 0) & (dist <= SPAN)


def _pair_col(col0=0):
    return pl.BlockSpec((SEQ, LANES), lambda j: (0, col0 // LANES + j))


def band_fwd(name, q, k, v, dil, dep=None):
    S = q.shape[0]
    L = S // dil
    tq = BAND_TQ
    scale = DIL_DIM ** -0.5
    deps = [] if dep is None else [dep]

    def body(q_ref, k_ref, v_ref, *rest):
        o_ref, lse_ref = rest[len(deps):]
        items = [(r, blk) for r in range(dil) for blk in _band_blocks(L, tq)]
        lo = _lane((tq, LANES)) < DIL_DIM

        def scores(item):
            r, (q0, q1, k0) = item
            qb = q_ref[_class_rows(r, dil, q0, q1), :].astype(MXU_DTYPE)
            return _dot(_stack_heads(qb, lo), k_ref[_class_rows(r, dil, k0, q1), :], "nt")

        def softmax(item, s):
            _, (q0, q1, k0) = item
            s = jnp.where(_band_mask2(q0, q1, k0), s * scale, NEG_INF)
            mx = jnp.max(s, axis=-1, keepdims=True)
            e = jnp.exp(s - mx)
            l = jnp.sum(e, axis=-1, keepdims=True)
            return (e * (1.0 / l)).astype(MXU_DTYPE), mx + jnp.log(l)

        def weighted(item, p, lse):
            r, (q0, q1, k0) = item
            pv = _dot(p, v_ref[_class_rows(r, dil, k0, q1), :])
            o_ref[_class_rows(r, dil, q0, q1), :] = jnp.where(lo, pv[:tq], pv[tq:])
            lse_ref[_class_rows(r, dil, q0, q1), :] = jnp.where(lo, lse[:tq], lse[tq:])

        s, prev = scores(items[0]), None
        for i, item in enumerate(items):
            s_next = scores(items[i + 1]) if i + 1 < len(items) else None
            if prev is not None:
                weighted(items[i - 1], *prev)
            prev, s = softmax(item, s), s_next
        weighted(items[-1], *prev)

    return pl.pallas_call(
        body, name=name, grid=(DIL_WIDTH // LANES,),
        in_specs=[_pair_col()] * 2 + [_pair_col(P_VD)] + [pl.BlockSpec(memory_space=pl.ANY)] * len(deps), out_specs=[_pair_col()] * 2,
        out_shape=[jax.ShapeDtypeStruct((S, DIL_WIDTH), F32)] * 2, compiler_params=_params())(q, k, v, *deps)


def band_bwd(name, q, k, v, lse, lse_mix, o_cat, do_cat, dil, before=None):
    S = q.shape[0]
    L = S // dil
    tq = BAND_TQ
    scale = DIL_DIM ** -0.5
    before = list(before or [])

    def body(q_ref, k_ref, v_ref, lse_ref, mix_ref, o_ref, do_ref, *rest):
        dq_ref, dk_ref, dv_ref = rest[len(before):]
        if before:
            dq0_ref, dk0_ref, dv0_ref = rest[:3]
            dk_ref[...] = dk0_ref[...]
            dv_ref[...] = dv0_ref[...]
        else:
            dk_ref[...] = jnp.zeros_like(dk_ref)
            dv_ref[...] = jnp.zeros_like(dv_ref)
        items = [(r, blk) for r in range(dil) for blk in _band_blocks(L, tq)]
        lo = _lane((tq, LANES)) < DIL_DIM
        per_head = lambda t: jnp.concatenate([t[:, 0:1], t[:, DIL_DIM:DIL_DIM + 1]], axis=0)

        def scores(item):
            r, (q0, q1, k0) = item
            qrows, krows = _class_rows(r, dil, q0, q1), _class_rows(r, dil, k0, q1)
            lse_p, dout = lse_ref[qrows, :], do_ref[qrows, :]
            w2 = per_head(jnp.exp(lse_p - mix_ref[qrows, :]))
            dd = dout * o_ref[qrows, :]
            big_d = jnp.concatenate([jnp.sum(jnp.where(lo, dd, 0.0), axis=-1, keepdims=True),
                                     jnp.sum(jnp.where(lo, 0.0, dd), axis=-1, keepdims=True)], axis=0)
            q2 = _stack_heads(q_ref[qrows, :].astype(MXU_DTYPE), lo)
            dom = (_stack_heads(dout, lo) * w2).astype(MXU_DTYPE)
            return (_dot(q2, k_ref[krows, :], "nt"), _dot(dom, v_ref[krows, :], "nt"), per_head(lse_p), w2 * big_d, q2, dom)

        def softmax_bwd(item, s, dp, lse2, wd2, q2, dom):
            _, (q0, q1, k0) = item
            p = jnp.where(_band_mask2(q0, q1, k0), jnp.exp(s * scale - lse2), 0.0)
            return p.astype(MXU_DTYPE), (p * (dp - wd2) * scale).astype(MXU_DTYPE), q2, dom

        def grads(item, p, ds, q2, dom):
            r, (q0, q1, k0) = item
            qrows, krows = _class_rows(r, dil, q0, q1), _class_rows(r, dil, k0, q1)
            dq2 = _dot(ds, k_ref[krows, :])
            dq = jnp.where(lo, dq2[:tq], dq2[tq:])
            dq_ref[qrows, :] = dq + dq0_ref[qrows, :] if before else dq
            dk_ref[krows, :] += _dot(ds, q2, "tn")
            dv_ref[krows, :] += _dot(p, dom, "tn")

        sc, prev = scores(items[0]), None
        for i, item in enumerate(items):
            sc_next = scores(items[i + 1]) if i + 1 < len(items) else None
            if prev is not None:
                grads(items[i - 1], *prev)
            prev, sc = softmax_bwd(item, *sc), sc_next
        grads(items[-1], *prev)

    cat = _pair_col(HEADS * LANES)
    return pl.pallas_call(
        body, name=name, grid=(DIL_WIDTH // LANES,),
        in_specs=[_pair_col()] * 2 + [_pair_col(P_VD)] + [_pair_col()] * 2 + [cat] * 2 + [_pair_col()] * len(before),
        out_specs=[_pair_col()] * 3, out_shape=[jax.ShapeDtypeStruct((S, DIL_WIDTH), F32)] * 3,
        compiler_params=_params())(q, k, v, lse, lse_mix, o_cat, do_cat, *before)


def combine_fwd(name, outs, lses, o_cat, tm=512):
    S = outs[0].shape[0]

    def body(o1, o2, o3, l1, l2, l3, cat_in, cat_out, mix_ref):
        ls = [l1[...], l2[...], l3[...]]
        m = jnp.maximum(jnp.maximum(ls[0], ls[1]), ls[2])
        e = [jnp.exp(l - m) for l in ls]
        den = e[0] + e[1] + e[2]
        cat_out[...] = (e[0] / den) * o1[...] + (e[1] / den) * o2[...] + (e[2] / den) * o3[...]
        mix_ref[...] = m + jnp.log(den)

    row = pl.BlockSpec((tm, DIL_WIDTH), lambda i: (i, 0))
    return pl.pallas_call(
        body, name=name, grid=(S // tm,), in_specs=[row] * 6 + [pl.BlockSpec(memory_space=pl.ANY)],
        out_specs=[pl.BlockSpec((tm, DIL_WIDTH), lambda i: (i, HEADS * LANES // DIL_WIDTH)), row],
        out_shape=[jax.ShapeDtypeStruct(o_cat.shape, F32), jax.ShapeDtypeStruct((S, DIL_WIDTH), F32)],
        input_output_aliases={6: 0}, compiler_params=_params())(*outs, *lses, o_cat)


FFN_FWD_ROWS = 512
FFN_BWD_ROWS = 256
CONV_PAD = SUBLANES


def _window(x, k):
    groups = x.reshape(-1, SUBLANES, x.shape[1])
    turned = pltpu.roll(groups, SUBLANES - k, axis=1)
    stays = lax.broadcasted_iota(jnp.int32, (groups.shape[0] - 1,) + groups.shape[1:], 1) < SUBLANES - k
    return jnp.where(stays, turned[:-1], turned[1:]).reshape(-1, x.shape[1])


def _earlier(ref, r0, rows, n):
    if r0 == 0:
        x = jnp.concatenate([jnp.zeros((SUBLANES, ref.shape[1]), F32), ref[:rows, :]], axis=0)
    else:
        x = ref[r0 - SUBLANES:r0 + rows, :]
    return _window(x, SUBLANES - n)


CONV_TC = 256
CONV_NB = D_FF // CONV_TC


def _half_specs(rows, rows_axis=False):
    if rows_axis:
        return [pl.BlockSpec((rows, D_MODEL), lambda j: (j, 0)), pl.BlockSpec((rows, D_MODEL), lambda j: (j + CONV_NB, 0))]
    return [pl.BlockSpec((rows, CONV_TC), lambda j: (0, j)), pl.BlockSpec((rows, CONV_TC), lambda j: (0, j + CONV_NB))]


def _whole(a):
    return pl.BlockSpec(a.shape, lambda j: (0,) * a.ndim)


def _up_pair(h, ug_ref, uv_ref):
    return jnp.concatenate([_dot(h, ug_ref[...], "nt"), _dot(h, uv_ref[...], "nt")], axis=1)


def _conv_taps(up_ref, r0, rows, w, b):
    uin, u1, u2 = up_ref[r0:r0 + rows, :], _earlier(up_ref, r0, rows, 1), _earlier(up_ref, r0, rows, 2)
    return uin, u1, u2, w[2:3, :] * uin + w[1:2, :] * u1 + w[0:1, :] * u2 + b


def ffn_fwd(name, h, w_up_t, w_conv, b_conv, w_down):
    S = h.shape[0]

    def body(h_ref, ug_ref, uv_ref, wg_ref, wv_ref, bg_ref, bv_ref, wd_ref, dn_ref, up_ref):
        @pl.when(pl.program_id(0) == 0)
        def _():
            dn_ref[...] = jnp.zeros_like(dn_ref)

        w = jnp.concatenate([wg_ref[...], wv_ref[...]], axis=1)
        b = jnp.concatenate([bg_ref[...], bv_ref[...]], axis=1)
        rows = FFN_FWD_ROWS
        starts = list(range(0, S, rows))

        def project(r0):
            up_ref[r0:r0 + rows, :] = _up_pair(h_ref[r0:r0 + rows, :], ug_ref, uv_ref)

        def gate(r0):
            u = _conv_taps(up_ref, r0, rows, w, b)[3]
            return (_silu(u[:, :CONV_TC]) * u[:, CONV_TC:]).astype(MXU_DTYPE)

        def project_down(r0, act):
            dn_ref[r0:r0 + rows, :] += _dot(act, wd_ref[...])

        project(starts[0])
        act_prev = None
        for i, r0 in enumerate(starts):
            if i + 1 < len(starts):
                project(starts[i + 1])
            if act_prev is not None:
                project_down(starts[i - 1], act_prev)
            act_prev = gate(r0)
        project_down(starts[-1], act_prev)

    return pl.pallas_call(
        body, name=name, grid=(CONV_NB,),
        in_specs=[_whole(h)] + _half_specs(CONV_TC, rows_axis=True) + _half_specs(3) + _half_specs(1)
        + [pl.BlockSpec((CONV_TC, w_down.shape[1]), lambda j: (j, 0))],
        out_specs=[pl.BlockSpec((S, w_down.shape[1]), lambda j: (0, 0)), pl.BlockSpec((S, 2 * CONV_TC), lambda j: (0, j))],
        out_shape=[jax.ShapeDtypeStruct((S, w_down.shape[1]), F32), jax.ShapeDtypeStruct((S, 2 * D_FF), F32)],
        compiler_params=_params())(h, w_up_t, w_up_t, w_conv, w_conv, b_conv, b_conv, w_down)


def ffn_bwd(name, h, up, w_up_t, w_conv, b_conv, d_dn, w_down):
    S, D = h.shape

    def body(h_ref, up_ref, ug_ref, uv_ref, wg_ref, wv_ref, bg_ref, bv_ref, dd_ref, wd_ref,
             dh_ref, gup_ref, gd_ref, dwg_ref, dwv_ref, dbg_ref, dbv_ref, du_ref, dup_ref, act_ref):
        @pl.when(pl.program_id(0) == 0)
        def _():
            dh_ref[...] = jnp.zeros_like(dh_ref)

        w = jnp.concatenate([wg_ref[...], wv_ref[...]], axis=1)
        b = jnp.concatenate([bg_ref[...], bv_ref[...]], axis=1)
        w_pair = jnp.concatenate([ug_ref[...], uv_ref[...]], axis=0)
        rows = FFN_BWD_ROWS
        starts = list(range(0, S, rows))
        du_ref[S:S + CONV_PAD, :] = jnp.zeros((CONV_PAD, 2 * CONV_TC), F32)

        def project(r0):
            return _dot(dd_ref[r0:r0 + rows, :], wd_ref[...], "nt")

        def through_conv(r0, da):
            uin, u1, u2, u = _conv_taps(up_ref, r0, rows, w, b)
            gate, val = u[:, :CONV_TC], u[:, CONV_TC:]
            sig = 1.0 / (1.0 + jnp.exp(-gate))
            du = jnp.concatenate([da * val * (sig * (1.0 + gate * (1.0 - sig))), da * (gate * sig)], axis=1)
            du_ref[r0:r0 + rows, :] = du
            act_ref[r0:r0 + rows, :] = (gate * sig * val).astype(MXU_DTYPE)
            dw = jnp.concatenate([_colsum(du * u2), _colsum(du * u1), _colsum(du * uin)], axis=0)
            return dw, _colsum(du)

        def back_up(r0):
            du = du_ref[r0:r0 + rows + CONV_PAD, :]
            dup = (w[2:3, :] * du[:rows] + w[1:2, :] * _window(du, 1) + w[0:1, :] * _window(du, 2)).astype(MXU_DTYPE)
            dup_ref[r0:r0 + rows, :] = dup
            dh_ref[r0:r0 + rows, :] += _dot(dup, w_pair)

        dw, db = 0.0, 0.0
        da = project(starts[0])
        for i, r0 in enumerate(starts):
            da_next = project(starts[i + 1]) if i + 1 < len(starts) else None
            dw_c, db_c = through_conv(r0, da)
            if i > 0:
                back_up(starts[i - 1])
            dw, db, da = dw + dw_c, db + db_c, da_next
        back_up(starts[-1])
        g_up, g_dn = _dot(dup_ref[...], h_ref[...], "tn"), _dot(act_ref[...], dd_ref[...], "tn")
        gup_ref[0], gup_ref[1] = g_up[:CONV_TC].astype(gup_ref.dtype), g_up[CONV_TC:].astype(gup_ref.dtype)
        gd_ref[...] = g_dn.astype(gd_ref.dtype)
        dwg_ref[...], dwv_ref[...] = dw[:, :CONV_TC], dw[:, CONV_TC:]
        dbg_ref[...], dbv_ref[...] = db[:, :CONV_TC], db[:, CONV_TC:]

    half = lambda rows: pl.BlockSpec((rows, CONV_TC), lambda j: (0, j))
    rows_blk = pl.BlockSpec((CONV_TC, D), lambda j: (j, 0))
    dh, gup, gd, dwg, dwv, dbg, dbv = pl.pallas_call(
        body, name=name, grid=(CONV_NB,),
        in_specs=[_whole(h), pl.BlockSpec((S, 2 * CONV_TC), lambda j: (0, j))] + _half_specs(CONV_TC, rows_axis=True) + _half_specs(3)
        + _half_specs(1) + [_whole(d_dn), rows_blk],
        out_specs=[pl.BlockSpec((S, D), lambda j: (0, 0)), pl.BlockSpec((2, CONV_TC, D), lambda j: (0, j, 0)), rows_blk,
                   half(3), half(3), half(1), half(1)],
        out_shape=[jax.ShapeDtypeStruct((S, D), F32), jax.ShapeDtypeStruct((2, D_FF, D), MXU_DTYPE),
                   jax.ShapeDtypeStruct((D_FF, D), MXU_DTYPE)]
        + [jax.ShapeDtypeStruct((3, D_FF), F32)] * 2 + [jax.ShapeDtypeStruct((1, D_FF), F32)] * 2,
        scratch_shapes=[pltpu.VMEM((S + CONV_PAD, 2 * CONV_TC), F32), pltpu.VMEM((S, 2 * CONV_TC), MXU_DTYPE),
                        pltpu.VMEM((S, CONV_TC), MXU_DTYPE)],
        compiler_params=_params())(h, up, w_up_t, w_up_t, w_conv, w_conv, b_conv, b_conv, d_dn, w_down)
    return dh, gup.reshape(2 * D_FF, D), gd, jnp.concatenate([dwg, dwv], axis=1), jnp.concatenate([dbg, dbv], axis=1)


def adamw(name, w, parts, m, v, tr=None):
    apart = w.ndim == 3
    R, C = w.shape[0], w.shape[-1]
    tr = tr or R
    assert R % tr == 0
    c1 = 1.0 - ADAM_B1 ** ADAM_STEP
    c2 = 1.0 - ADAM_B2 ** ADAM_STEP
    np_ = len(parts)

    def body(*refs):
        w_ref, m_ref, v_ref = refs[0], refs[1 + np_], refs[2 + np_]
        go_ref, d_ref, mo_ref, vo_ref = refs[3 + np_:]
        terms = []
        for part, ref in zip(parts, refs[1:1 + np_], strict=True):
            terms += [ref[...]] if part.ndim == 2 else [ref[p] for p in range(part.shape[0])]
        g = terms[0].astype(F32)
        for term in terms[1:]:
            g = g + term.astype(F32)
        m2 = ADAM_B1 * m_ref[...] + (1.0 - ADAM_B1) * g
        v2 = ADAM_B2 * v_ref[...] + (1.0 - ADAM_B2) * (g * g)
        go_ref[...] = g
        mo_ref[...] = m2
        vo_ref[...] = v2
        d_ref[...] = -ADAM_LR * ((m2 / c1) / (jnp.sqrt(v2 / c2) + ADAM_EPS) + ADAM_WD * w_ref[...])

    blk = pl.BlockSpec((tr, C), lambda i: (i, 0))
    own = pl.BlockSpec((tr, None, C), lambda i: (i, 0, 0)) if apart else blk
    part_specs = [blk if p.ndim == 2 else pl.BlockSpec((p.shape[0], tr, C), lambda i: (0, i, 0)) for p in parts]
    return pl.pallas_call(
        body, name=name, grid=(R // tr,),
        in_specs=[own] + part_specs + [own, own], out_specs=[own] * 4,
        out_shape=[jax.ShapeDtypeStruct(w.shape, F32)] * 4, compiler_params=_params())(w, *parts, m, v)


def _place():
    return lax.axis_index("x"), lax.axis_index("y"), lax.axis_index("c")


def all_gather(name, arrs, after=None):
    n = len(arrs)
    deps = [] if after is None else [after]

    def body(*refs):
        ins, outs = refs[:n], refs[n + len(deps):2 * n + len(deps)]
        send_sems, recv_sems, local_sems = refs[2 * n + len(deps):]
        x, y, c = _place()
        me, sibling = (x, y, c), (x, y, 1 - c)
        chips = [(1 - x, y), (x, 1 - y), (1 - x, 1 - y)]
        sends = []
        for t in range(n):
            out = outs[t]

            def slot(px, py, pc, out=out):
                return out.at[4 * px + 2 * py + pc]

            def copy(k, block, to, src=None, t=t, slot=slot):
                return pltpu.make_async_remote_copy(
                    src_ref=slot(*block) if src is None else src, dst_ref=slot(*block),
                    send_sem=send_sems.at[7 * t + k], recv_sem=recv_sems.at[7 * t + k],
                    device_id=to, device_id_type=MESH_ID)

            mine = pltpu.make_async_copy(ins[t], slot(*me), local_sems.at[t])
            mine.start()
            first = [copy(0, me, sibling, src=ins[t])]
            first += [copy(1 + j, me, (*chip, c), src=ins[t]) for j, chip in enumerate(chips)]
            for cp in first:
                cp.start()
            sends.append((mine, first, copy))
        for t in range(n):
            mine, first, copy = sends[t]
            passed = [copy(4 + j, (*chip, c), sibling) for j, chip in enumerate(chips)]
            for j, chip in enumerate(chips):
                copy(1 + j, (*chip, c), me).wait_recv()
                passed[j].start()
            copy(0, sibling, me).wait_recv()
            for j, chip in enumerate(chips):
                copy(4 + j, (*chip, 1 - c), me).wait_recv()
            for cp in first + passed:
                cp.wait_send()
            mine.wait()

    any_spec = pl.BlockSpec(memory_space=pl.ANY)
    res = pl.pallas_call(
        body, name=name, in_specs=[any_spec] * (n + len(deps)), out_specs=[any_spec] * n,
        out_shape=[jax.ShapeDtypeStruct((N_DEV,) + a.shape, a.dtype) for a in arrs],
        scratch_shapes=[pltpu.SemaphoreType.DMA((7 * n,)), pltpu.SemaphoreType.DMA((7 * n,)), pltpu.SemaphoreType.DMA((n,))],
        compiler_params=pltpu.CompilerParams(has_side_effects=True))(*arrs, *deps)
    return list(res)


def ada_modulation(name, c, w_ada):
    n_mod = w_ada.shape[1]

    def exchange(src_ref, dst_ref, send_sems, recv_sems):
        x, y, c_ = _place()
        me = 4 * x + 2 * y + c_
        copies = []
        for k in range(1, N_DEV):
            px, py, pc = x ^ (k >> 2), y ^ ((k >> 1) & 1), c_ ^ (k & 1)
            copies.append(pltpu.make_async_remote_copy(
                src_ref=src_ref, dst_ref=dst_ref.at[me], send_sem=send_sems.at[k - 1], recv_sem=recv_sems.at[k - 1],
                device_id=(px, py, pc), device_id_type=MESH_ID))
        for cp in copies:
            cp.start()
        for cp in copies:
            cp.wait_recv()
        for cp in copies:
            cp.wait_send()
        return me

    def body(c_ref, w_ref, sc_ref, mod_ref, c_all, send_c, recv_c, send_m, recv_m):
        me = exchange(c_ref, c_all, send_c, recv_c)
        c_all[me] = c_ref[...]
        sc = _silu(jnp.concatenate([c_all[p] for p in range(N_DEV)], axis=0))
        sc_ref[...] = sc.astype(sc_ref.dtype)
        mod_ref[me] = _dot(sc, w_ref[...])
        exchange(mod_ref.at[me], mod_ref, send_m, recv_m)

    vmem = pl.BlockSpec(memory_space=pltpu.VMEM)
    return pl.pallas_call(
        body, name=name, in_specs=[vmem, vmem], out_specs=[vmem, vmem],
        out_shape=[jax.ShapeDtypeStruct((N_DEV, c.shape[1]), MXU_DTYPE), jax.ShapeDtypeStruct((N_DEV, N_DEV, n_mod), F32)],
        scratch_shapes=[pltpu.VMEM((N_DEV, 1, c.shape[1]), F32)] + [pltpu.SemaphoreType.DMA((N_DEV - 1,))] * 4,
        compiler_params=pltpu.CompilerParams(has_side_effects=True, vmem_limit_bytes=VMEM_LIMIT))(c, w_ada)


HBM_SPEC = pl.BlockSpec(memory_space=pltpu.HBM)
SEM_SPEC = pl.BlockSpec(memory_space=pltpu.SEMAPHORE)
DATAFLOW = pltpu.SideEffectType.DATAFLOW_SIDE_EFFECTING


def _exchange_copies(srcs, lands, send_sems, recv_sems, gather, first=0):
    x, y, c = _place()
    me = 4 * x + 2 * y + c
    out = []
    for t, (src, land) in enumerate(zip(srcs, lands, strict=True)):
        for k in range(1, N_DEV):
            px, py, pc = x ^ (k >> 2), y ^ ((k >> 1) & 1), c ^ (k & 1)
            sem = 7 * (first + t) + k - 1
            out.append((k, pltpu.make_async_remote_copy(
                src_ref=src if gather else src.at[4 * px + 2 * py + pc],
                dst_ref=land.at[me] if gather else land.at[k - 1],
                send_sem=send_sems.at[sem], recv_sem=recv_sems.at[sem],
                device_id=(px, py, pc), device_id_type=MESH_ID)))
    return out


TREE_DIRECT = (1, 2, 4, 6)
TREE_FORWARDED = (3, 5, 7)


def exchange_start(name, arrs, gather, after=None, tree=False):
    n = len(arrs)
    lands = [lax.empty(((N_DEV,) + a.shape) if gather else ((N_DEV - 1,) + a.shape[1:]), a.dtype) for a in arrs]
    deps = [] if after is None else [after]

    def body(*refs):
        srcs, land_refs = refs[:n], refs[n:2 * n]
        send_sems, recv_sems = refs[2 * n + len(deps)], refs[2 * n + len(deps) + 1]
        token = refs[-1]
        for k, cp in _exchange_copies(srcs, land_refs, send_sems, recv_sems, gather):
            if not tree or k in TREE_DIRECT:
                cp.start()
        token[...] = jnp.zeros_like(token)

    hbm = lambda a: pltpu.HBM(a.shape, a.dtype)
    res = pl.pallas_call(
        body, name=name,
        out_shape=(pltpu.SemaphoreType.DMA((7 * n,)), pltpu.SemaphoreType.DMA((7 * n,)), *[hbm(a) for a in arrs],
                   *[hbm(l) for l in lands], jax.ShapeDtypeStruct((8, 128), F32)),
        in_specs=[HBM_SPEC] * (2 * n) + [pl.BlockSpec(memory_space=pl.ANY)] * len(deps),
        out_specs=(SEM_SPEC, SEM_SPEC, *[HBM_SPEC] * (2 * n), pl.BlockSpec(memory_space=pltpu.VMEM)),
        input_output_aliases={i: 2 + i for i in range(2 * n)},
        compiler_params=pltpu.CompilerParams(has_side_effects=DATAFLOW),
    )(*[pltpu.with_memory_space_constraint(a, pltpu.HBM) for a in arrs + lands], *deps)
    return res[0], res[1], list(res[2:2 + n]), list(res[2 + n:2 + 2 * n]), res[-1]


def exchange_forward(name, started, after, first=0, count=None):
    send_sems, recv_sems, srcs, lands, _ = started
    count = len(srcs) - first if count is None else count
    mine = lands[first:first + count]
    n = len(mine)

    def copies(land_refs, send_ref, recv_ref):
        x, y, c = _place()
        out = []
        for t, land in enumerate(land_refs):
            for k in (2, 4, 6):
                slot = land.at[4 * (x ^ (k >> 2)) + 2 * (y ^ ((k >> 1) & 1)) + c]
                came, goes = 7 * (first + t) + k - 1, 7 * (first + t) + (k ^ 1) - 1
                out.append((
                    pltpu.make_async_remote_copy(src_ref=slot, dst_ref=slot, send_sem=send_ref.at[came], recv_sem=recv_ref.at[came],
                                                 device_id=(x, y, c), device_id_type=MESH_ID),
                    pltpu.make_async_remote_copy(src_ref=slot, dst_ref=slot, send_sem=send_ref.at[goes], recv_sem=recv_ref.at[goes],
                                                 device_id=(x, y, 1 - c), device_id_type=MESH_ID)))
        return out

    after = list(after) if isinstance(after, (list, tuple)) else [after]

    def arrived(*refs):
        for came, _ in copies(refs[:n], refs[n], refs[n + 1]):
            came.wait_recv()

    def pass_on(*refs):
        for _, goes in copies(refs[:n], refs[n], refs[n + 1]):
            goes.start()
        refs[-1][...] = jnp.zeros_like(refs[-1])

    hbm = lambda a: pltpu.HBM(a.shape, a.dtype)
    here = pl.pallas_call(
        arrived, name=name + "_arrived", out_shape=tuple(hbm(a) for a in mine),
        in_specs=[HBM_SPEC] * n + [SEM_SPEC, SEM_SPEC] + [pl.BlockSpec(memory_space=pl.ANY)] * len(after),
        out_specs=tuple([HBM_SPEC] * n), input_output_aliases={i: i for i in range(n)},
        compiler_params=pltpu.CompilerParams(has_side_effects=DATAFLOW),
    )(*mine, send_sems, recv_sems, *after)
    res = pl.pallas_call(
        pass_on, name=name, out_shape=(*[hbm(a) for a in mine], jax.ShapeDtypeStruct((8, 128), F32)),
        in_specs=[HBM_SPEC] * n + [SEM_SPEC, SEM_SPEC],
        out_specs=(*[HBM_SPEC] * n, pl.BlockSpec(memory_space=pltpu.VMEM)), input_output_aliases={i: i for i in range(n)},
        compiler_params=pltpu.CompilerParams(has_side_effects=DATAFLOW),
    )(*here, send_sems, recv_sems)
    lands = lands[:first] + list(res[:n]) + lands[first + count:]
    return (send_sems, recv_sems, srcs, lands, res[-1])


def exchange_wait(name, started, gather, after, first=0, count=None, tree=False):
    send_sems, recv_sems, srcs, lands, _ = started
    count = len(srcs) - first if count is None else count
    srcs, lands = srcs[first:first + count], lands[first:first + count]
    n = len(srcs)

    def body(*refs):
        src_refs, land_refs = refs[:n], refs[n:2 * n]
        copies = _exchange_copies(src_refs, land_refs, refs[2 * n], refs[2 * n + 1], gather, first)
        for _, cp in copies:
            cp.wait_send()
        for k, cp in copies:
            if not tree or k in (1,) + TREE_FORWARDED:
                cp.wait_recv()

    hbm = lambda a: pltpu.HBM(a.shape, a.dtype)
    res = pl.pallas_call(
        body, name=name, out_shape=tuple(hbm(a) for a in srcs + lands),
        in_specs=[HBM_SPEC] * (2 * n) + [SEM_SPEC, SEM_SPEC, pl.BlockSpec(memory_space=pl.ANY)],
        out_specs=tuple([HBM_SPEC] * (2 * n)), input_output_aliases={i: i for i in range(2 * n)},
        compiler_params=pltpu.CompilerParams(has_side_effects=DATAFLOW),
    )(*srcs, *lands, send_sems, recv_sems, after)
    return list(res[:n]), list(res[n:])


def _gather_cols(stack):
    p, k, n = stack.shape
    return stack.transpose(1, 0, 2).reshape(k, p * n)


def _scatter_cols(full):
    k, n = full.shape
    return full.reshape(k, N_DEV, n // N_DEV).transpose(1, 0, 2)


def _gather_rows(stack):
    p, r, n = stack.shape
    return stack.reshape(p * r, n)


def _scatter_rows(full):
    r, n = full.shape
    return full.reshape(N_DEV, r // N_DEV, n)


_IN_NAT = Q_LORA + KV_LORA
TRANSPOSED = ("w_in", "w_q_b", "w_up")
ROWS_APART = ("w_in", "w_conv")


def to_kernel_layout(name, w):
    if name == "w_in":
        z = lambda n: jnp.zeros((n, w.shape[1]), w.dtype)
        return jnp.concatenate([w[:_IN_NAT], z(KPE_LO), w[_IN_NAT:_IN_NAT + ROPE], z(LANES - KPE_LO - ROPE), w[_IN_NAT + ROPE:]], axis=0)
    if name == "w_q_b":
        return jnp.pad(w.reshape(HEADS, NOPE + ROPE, -1), ((0, 0), (0, LANES - NOPE - ROPE), (0, 0))).reshape(HEADS * LANES, -1)
    if name == "w_o":
        mla = jnp.pad(w[:HEADS * NOPE].reshape(HEADS, NOPE, -1), ((0, 0), (LANES - NOPE, 0), (0, 0))).reshape(HEADS * LANES, -1)
        return jnp.concatenate([mla, w[HEADS * NOPE:]], axis=0)
    return w


def from_kernel_layout(name, g):
    if name == "w_in":
        return jnp.concatenate([g[:_IN_NAT], g[P_KPE + KPE_LO:P_KPE + KPE_LO + ROPE], g[P_QD:]], axis=0)
    if name == "w_q_b":
        return g.reshape(HEADS, LANES, -1)[:, :NOPE + ROPE, :].reshape(HEADS * (NOPE + ROPE), -1)
    if name == "w_o":
        mla = g[:HEADS * LANES].reshape(HEADS, LANES, -1)[:, LANES - NOPE:, :].reshape(HEADS * NOPE, -1)
        return jnp.concatenate([mla, g[HEADS * LANES:]], axis=0)
    return g


SMALL_COLS = 1024
SMALL_ROWS = 24
SMALL_AT = {"loss": (0, 0, 1), "b_ada": (1, 0, 6 * D_MODEL), "g_mix_norm": (7, 0, D_MODEL), "g_q_lat": (8, 0, Q_LORA),
            "g_kv_lat": (9, 0, KV_LORA), "g_mla_q_nope": (10, 0, NOPE), "g_mla_q_pe": (10, 128, ROPE),
            "g_mla_k_nope": (10, 256, NOPE), "g_mla_k_pe": (10, 384, ROPE), "g_dil_q": (10, 512, DIL_DIM),
            "g_dil_k": (10, 640, DIL_DIM), "g_ffn_norm": (11, 0, D_MODEL), "b_conv": (12, 0, 2 * D_FF)}
SMALL_PARAMS = tuple(n for n in SMALL_AT if n != "loss")


def _pack_small(values):
    by_row = {}
    for name, (row, off, n) in SMALL_AT.items():
        by_row.setdefault(row, []).append((off, values[name].reshape(-1).astype(F32)))
    out = []
    for row in sorted(by_row):
        pieces, at = [], 0
        for off, v in sorted(by_row[row], key=lambda t: t[0]):
            pieces += [jnp.zeros((off - at,), F32), v]
            at = off + v.shape[0]
        flat = jnp.concatenate(pieces)
        nrows = -(-flat.shape[0] // SMALL_COLS)
        out.append(jnp.pad(flat, (0, nrows * SMALL_COLS - flat.shape[0])).reshape(nrows, SMALL_COLS))
    packed = jnp.concatenate(out, axis=0)
    return jnp.pad(packed, ((0, SMALL_ROWS - packed.shape[0]), (0, 0)))


def _adam(w, g, m, v):
    c1 = 1.0 - ADAM_B1 ** ADAM_STEP
    c2 = 1.0 - ADAM_B2 ** ADAM_STEP
    m2 = ADAM_B1 * m + (1.0 - ADAM_B1) * g
    v2 = ADAM_B2 * v + (1.0 - ADAM_B2) * (g * g)
    return -ADAM_LR * ((m2 / c1) / (jnp.sqrt(v2 / c2) + ADAM_EPS) + ADAM_WD * w), m2, v2


def adamw_small(name, stack, params):
    flat = [a for n in SMALL_PARAMS for a in params[n]]

    def body(stack_ref, *refs):
        ins, outs = refs[:len(flat)], refs[len(flat):]
        g_all = stack_ref[0]
        for p in range(1, N_DEV):
            g_all = g_all + stack_ref[p]
        outs[0][...] = g_all[0:1, 0:1]
        for i, pname in enumerate(SMALL_PARAMS):
            row, off, n = SMALL_AT[pname]
            w_ref, m_ref, v_ref = ins[3 * i:3 * i + 3]
            go_ref, d_ref, mo_ref, vo_ref = outs[1 + 4 * i:5 + 4 * i]
            for c0 in range(0, n, SMALL_COLS):
                cn = min(SMALL_COLS, n - c0)
                r = row + c0 // SMALL_COLS
                g = g_all[r:r + 1, off:off + cn]
                cols = (slice(None), slice(c0, c0 + cn))
                d, m2, v2 = _adam(w_ref[cols], g, m_ref[cols], v_ref[cols])
                go_ref[cols], d_ref[cols], mo_ref[cols], vo_ref[cols] = g, d, m2, v2

    whole = lambda a: pl.BlockSpec(a.shape, lambda: (0,) * a.ndim)
    out_shape = [jax.ShapeDtypeStruct((1, 1), F32)] + [jax.ShapeDtypeStruct(a.shape, F32) for n in SMALL_PARAMS for a in params[n][:1] * 4]
    res = pl.pallas_call(body, name=name, in_specs=[whole(stack)] + [whole(a) for a in flat],
                         out_specs=[pl.BlockSpec(s.shape, lambda s=s: (0,) * len(s.shape)) for s in out_shape],
                         out_shape=out_shape, compiler_params=_params())(stack, *flat)
    return res[0], {n: res[1 + 4 * i:5 + 4 * i] for i, n in enumerate(SMALL_PARAMS)}


def _local_step(x, pos, mod, target, w, fetch, emit, halfway=lambda after: None):
    S = SEQ
    sh1, sc1, g1, sh2, sc2, g2 = [mod[:, i * D_MODEL:(i + 1) * D_MODEL] for i in range(6)]
    zeros = lambda n: jnp.zeros((1, n), F32)
    g_q = jnp.concatenate([w["g_mla_q_nope"], w["g_mla_q_pe"], zeros(LANES - NOPE - ROPE)], axis=1)
    g_k = jnp.concatenate([w["g_mla_k_nope"], zeros(LANES - NOPE)], axis=1)
    g_kpe = jnp.concatenate([zeros(KPE_LO), w["g_mla_k_pe"], zeros(LANES - KPE_LO - ROPE)], axis=1)
    g_dq = jnp.concatenate([w["g_dil_q"]] * 2, axis=1)
    g_dk = jnp.concatenate([w["g_dil_k"]] * 2, axis=1)
    b_conv = w["b_conv"]

    def inv_freq(d):
        return jnp.power(ROPE_THETA, -2.0 * jnp.arange(d // 2, dtype=F32) / d)

    n_m, n_d = ROPE // 2, DIL_DIM // 2
    freqs = jnp.concatenate([inv_freq(ROPE), inv_freq(DIL_DIM), jnp.zeros((LANES - n_m - n_d,), F32)]).reshape(1, LANES)

    def tables_fn(rows, params):
        (p,), (f,) = rows, params
        c, s = jnp.cos(p * f), jnp.sin(p * f)
        one, zero = jnp.ones_like(c), jnp.zeros_like(c)
        mla = lambda t, fill: jnp.concatenate([fill[:, :KPE_LO], t[:, :n_m], t[:, :n_m], fill[:, :LANES - KPE_LO - ROPE]], axis=1)
        dil = lambda t: jnp.concatenate([t[:, n_m:n_m + n_d]] * 4, axis=1)
        return [mla(c, one), mla(s, zero), dil(c), dil(s)], []

    cos_m, sin_m, cos_d, sin_d = rowwise("rope_tables", tables_fn, [pos], [freqs], [(LANES, F32)] * 4)
    tables = [cos_m, sin_m, cos_d, sin_d]
    H_M, H_D = ROPE // 2, DIL_DIM // 2

    def ln1_fn(rows, params):
        (xv,), (g, sc, sh) = rows, params
        y, _, _ = _rms(xv, g)
        return [y * (1.0 + sc) + sh], []

    (h,) = rowwise("ln1_fwd", ln1_fn, [x], [w["g_mix_norm"], sc1, sh1], [(D_MODEL, MXU_DTYPE)], dep=sin_d)
    w_in = fetch("w_in", h)

    def proj_fn(rows, params):
        (hv, cm, sm, cd, sd), (w_t, gq, gkv, gkp, gdq, gdk) = rows, params
        pv = _dot(hv, w_t, "nt")
        kper = _rope(_grms(pv[:, P_KPE:P_QD], gkp, KPE_GROUPS)[0], cm, sm, H_M)
        qd = [_rope(_grms(c, gdq, DIL_GROUPS)[0], cd, sd, H_D) for c in _chunks(pv[:, P_QD:P_KD])]
        kd = [_rope(_grms(c, gdk, DIL_GROUPS)[0], cd, sd, H_D) for c in _chunks(pv[:, P_KD:P_VD])]
        return [pv, _rms(pv[:, P_QLAT:P_KVLAT], gq)[0], _rms(pv[:, P_KVLAT:P_KPE], gkv)[0], kper,
                jnp.concatenate(qd, axis=1), jnp.concatenate(kd, axis=1)], []

    post_params = [w["g_q_lat"], w["g_kv_lat"], g_kpe, g_dq, g_dk]
    proj, qln, kvn, kper, qd_r, kd_r = rowwise(
        "proj_fwd", proj_fn, [h] + tables, [w_in] + post_params,
        [(P_END, F32), (Q_LORA, MXU_DTYPE), (KV_LORA, MXU_DTYPE), (LANES, MXU_DTYPE)] + [(DIL_WIDTH, F32)] * 2, tm=256)
    w_q_b, w_kv_b = fetch("w_q_b", qln), fetch("w_kv_b", kvn)

    def mla_proj_fn(rows, params):
        (qlv, kvlv, kp, cm, sm), (wq_t, wkv, gq, gk) = rows, params
        qv, kvv = _dot(qlv, wq_t, "nt"), _dot(kvlv, wkv)
        value_lanes = _lane(kp.shape) >= NOPE
        qs, ks, vs = [], [], []
        for qc, kc in zip(_chunks(qv), _chunks(kvv), strict=True):
            qs.append(_rope(_grms(qc, gq, Q_GROUPS)[0], cm, sm, H_M))
            ks.append(_grms(kc, gk, K_GROUPS)[0] + kp)
            vs.append(jnp.where(value_lanes, kc, 0.0))
        return [qv, kvv] + [jnp.concatenate(t, axis=1) for t in (qs, ks, vs)], []

    q, kv, q_mla, k_mla, v_mla = rowwise(
        "mla_proj", mla_proj_fn, [qln, kvn, kper, cos_m, sin_m], [w_q_b, w_kv_b, g_q, g_k],
        [(HEADS * LANES, F32)] * 2 + [(HEADS * LANES, MXU_DTYPE)] * 3, tm=256)
    mla_scale = (NOPE + ROPE) ** -0.5
    o_cat, lse_mla = mla_fwd("mla_fwd", q_mla, k_mla, v_mla, mla_scale)
    passed = halfway(lse_mla)

    band = [band_fwd(f"band{dil}_fwd", qd_r, kd_r, proj, dil, dep=passed) for dil in DILATIONS]
    o_cat, lse_mix = combine_fwd("dil_combine", [b[0] for b in band], [b[1] for b in band], o_cat)
    w_o = fetch("w_o", o_cat)

    def mid_fn(rows, params):
        (ov, xv), (w_out, gate1, g, sc, sh) = rows, params
        mx = _dot(ov, w_out)
        x1 = xv + gate1 * mx
        y, _, _ = _rms(x1, g)
        return [mx, x1, y * (1.0 + sc) + sh], []

    mix, x1, h2 = rowwise("mix_fwd", mid_fn, [o_cat, x], [w_o, g1, w["g_ffn_norm"], sc2, sh2],
                          [(D_MODEL, F32), (D_MODEL, F32), (D_MODEL, MXU_DTYPE)], tm=256)
    w_up, w_conv, w_down = fetch("w_up", h2), fetch("w_conv", h2), fetch("w_down", h2)
    dn, up = ffn_fwd("ffn_fwd", h2, w_up, w_conv, b_conv, w_down)

    def final_fn(rows, params):
        (x1v, dnv, tgt), (gate2,) = rows, params
        r = x1v + gate2 * dnv - tgt
        dy = r * (1.0 / D_MODEL)
        loss = jnp.sum(_colsum(r * r), axis=-1, keepdims=True) * (0.5 / D_MODEL)
        return [dy, gate2 * dy], [loss, _colsum(dy * dnv)]

    dy, d_dn, loss, dg2 = rowwise("loss_head", final_fn, [x1, dn, target], [g2], [(D_MODEL, F32), (D_MODEL, MXU_DTYPE)],
                                  [1, D_MODEL])
    dh2, g_up, g_down, g_w_conv, g_b_conv = ffn_bwd("ffn_bwd", h2, up, w_up, w_conv, b_conv, d_dn, w_down)
    emit("w_down", g_down)
    emit("w_conv", g_w_conv)
    sent = emit("w_up", g_up)

    def mid_bwd_fn(rows, params):
        (dh2v, dyv, x1v, mx), (gate1, g, sc) = rows, params
        yn, n, rstd = _rms(x1v, g)
        dx_n, dg = _rms_bwd(dh2v * (1.0 + sc), n, rstd, g)
        dx1 = dyv + dx_n
        return [dx1, gate1 * dx1], [dg, _colsum(dh2v * yn), _colsum(dh2v), _colsum(dx1 * mx)]

    dx1, dmix, dg_ffn, dsc2, dsh2, dg1 = rowwise(
        "mid_bwd", mid_bwd_fn, [dh2, dy, x1, mix], [g1, w["g_ffn_norm"], sc2], [(D_MODEL, F32), (D_MODEL, MXU_DTYPE)],
        [D_MODEL] * 4, dep=sent)

    sent = emit("w_o", matmul("mix_wgrad", o_cat, dmix, "tn", tm=512, out_dtype=MXU_DTYPE))
    do_cat = matmul("mix_dgrad", dmix, w_o, "nt", tm=512, dep=sent)
    dband = None
    for dil, b in zip(DILATIONS, band):
        dband = band_bwd(f"band{dil}_bwd", qd_r, kd_r, proj, b[1], lse_mix, o_cat, do_cat, dil, before=dband)
    dq_mla, dkv_mla, dkper = mla_bwd("mla_bwd", q_mla, k_mla, v_mla, o_cat, do_cat, lse_mla, mla_scale)

    def mla_prep_bwd_fn(rows, params):
        (dqv, dkvv, qv, kvv, cm, sm), (gq, gk) = rows, params
        nope_lanes = _lane(cm.shape) < NOPE
        dqs, dkvs, dgq, dgk = [], [], 0.0, 0.0
        for dqc, dkc, qc, kc in zip(_chunks(dqv), _chunks(dkvv), _chunks(qv), _chunks(kvv), strict=True):
            _, n, rstd = _grms(qc, gq, Q_GROUPS)
            dx, dg = _grms_bwd(_rope_bwd(dqc, cm, sm, H_M), n, rstd, gq, Q_GROUPS)
            dqs.append(dx)
            dgq = dgq + dg
            _, n, rstd = _grms(kc, gk, K_GROUPS)
            dx, dg = _grms_bwd(dkc, n, rstd, gk, K_GROUPS)
            dkvs.append(jnp.where(nope_lanes, dx, dkc))
            dgk = dgk + dg
        return [jnp.concatenate(dqs, axis=1), jnp.concatenate(dkvs, axis=1)], [dgq, dgk]

    dq, dkv, dg_q, dg_k = rowwise("mla_prep_bwd", mla_prep_bwd_fn, [dq_mla, dkv_mla, q, kv, cos_m, sin_m], [g_q, g_k],
                                  [(HEADS * LANES, MXU_DTYPE)] * 2, [LANES, LANES], tm=256)
    emit("w_q_b", matmul("q_wgrad", dq, qln, "tn", out_dtype=MXU_DTYPE))
    emit("w_kv_b", matmul("kv_wgrad", kvn, dkv, "tn", out_dtype=MXU_DTYPE))

    def pre_bwd_fn(rows, params):
        dqv, dkvv, dkp, dqd_, dkd_, dvd_, pv, cm, sm, cd, sd = rows
        wq_t, wkv, gq, gkv, gkp, gdq, gdk = params
        dql, dkvl = _dot(dqv, wq_t), _dot(dkvv, wkv, "nt")
        r_q = _norm_bwd(dql, pv[:, P_QLAT:P_KVLAT], gq)
        r_kv = _norm_bwd(dkvl, pv[:, P_KVLAT:P_KPE], gkv)
        _, n, rstd = _grms(pv[:, P_KPE:P_QD], gkp, KPE_GROUPS)
        r_kp = _grms_bwd(_rope_bwd(dkp, cm, sm, H_M), n, rstd, gkp, KPE_GROUPS)
        outs, dgs = [r_q[0], r_kv[0], r_kp[0]], []
        for dval, lo, g in ((dqd_, P_QD, gdq), (dkd_, P_KD, gdk)):
            dg_sum = 0.0
            for dc, xc in zip(_chunks(dval), _chunks(pv[:, lo:lo + DIL_WIDTH]), strict=True):
                _, n, rstd = _grms(xc, g, DIL_GROUPS)
                dx, dg = _grms_bwd(_rope_bwd(dc, cd, sd, H_D), n, rstd, g, DIL_GROUPS)
                outs.append(dx)
                dg_sum = dg_sum + dg
            dgs.append(dg_sum)
        return [jnp.concatenate(outs + [dvd_], axis=1)], [r_q[1], r_kv[1], r_kp[1]] + dgs

    dproj, dg_q_lat, dg_kv_lat, dg_kpe, dg_dq, dg_dk = rowwise(
        "proj_pre_bwd", pre_bwd_fn,
        [dq, dkv, dkper] + list(dband) + [proj] + tables, [w_q_b, w_kv_b] + post_params,
        [(P_END, MXU_DTYPE)], [Q_LORA, KV_LORA, LANES, LANES, LANES], tm=256)
    sent = emit("w_in", matmul("proj_wgrad", dproj, h, "tn", tn=512, out_dtype=MXU_DTYPE))

    def ln1_bwd_fn(rows, params):
        (dpv, dres, xv), (w_t, g, sc) = rows, params
        dhv = _dot(dpv, w_t)
        yn, n, rstd = _rms(xv, g)
        dx_n, dg = _rms_bwd(dhv * (1.0 + sc), n, rstd, g)
        return [dres + dx_n], [dg, _colsum(dhv * yn), _colsum(dhv)]

    grad_x, dg_mix, dsc1, dsh1 = rowwise("proj_dgrad", ln1_bwd_fn, [dproj, dx1, x], [w_in, w["g_mix_norm"], sc1],
                                         [(D_MODEL, F32)], [D_MODEL] * 3, tm=256, dep=sent)
    dmod = jnp.concatenate([dsh1, dsc1, dg1, dsh2, dsc2, dg2], axis=-1)
    small = {"loss": loss, "b_ada": dmod, "g_mix_norm": dg_mix, "g_q_lat": dg_q_lat, "g_kv_lat": dg_kv_lat,
             "g_mla_q_nope": dg_q[:, :NOPE], "g_mla_q_pe": dg_q[:, NOPE:NOPE + ROPE], "g_mla_k_nope": dg_k[:, :NOPE],
             "g_mla_k_pe": dg_kpe[:, KPE_LO:KPE_LO + ROPE], "g_dil_q": dg_dq[:, :DIL_DIM] + dg_dq[:, DIL_DIM:],
             "g_dil_k": dg_dk[:, :DIL_DIM] + dg_dk[:, DIL_DIM:], "g_ffn_norm": dg_ffn,
             "b_conv": g_b_conv}
    return grad_x, small


COL_SHARDED = ("w_kv_b", "w_conv")
ROW_SHARDED = ("w_o", "w_down") + TRANSPOSED
ADAM_TILE = {"w_ada": 256, "w_up": 176, "w_down": 176}
GATHER_GROUPS = (("w_in",), ("w_q_b", "w_kv_b"), ("w_o",), ("w_up", "w_conv", "w_down"))
START_STAGES = ((0, 1), (2, 3))
FORWARD_STAGES = ((0, 1), (2,), (3,))
FORWARD_WITH = {"w_o": 2}
SCATTER_GROUPS = (("w_down", "w_conv", "w_up"), ("w_o",), ("w_q_b", "w_kv_b", "w_in"))
OUT_WEIGHTS = ("w_ada", "b_ada", "g_mix_norm", "w_in", "g_q_lat", "w_q_b", "g_kv_lat", "w_kv_b", "g_mla_q_nope", "g_mla_q_pe",
               "g_mla_k_nope", "g_mla_k_pe", "g_dil_q", "g_dil_k", "w_o", "g_ffn_norm", "w_up", "w_conv", "b_conv", "w_down")


def kernel(x, c, positions, w_ada, b_ada, g_mix_norm, w_in, g_q_lat, w_q_b, g_kv_lat, w_kv_b, g_mla_q_nope, g_mla_q_pe, g_mla_k_nope, g_mla_k_pe, g_dil_q, g_dil_k, w_o, g_ffn_norm, w_up, w_conv, b_conv, w_down, loss_target, m_w_ada, m_b_ada, m_g_mix_norm, m_w_in, m_g_q_lat, m_w_q_b, m_g_kv_lat, m_w_kv_b, m_g_mla_q_nope, m_g_mla_q_pe, m_g_mla_k_nope, m_g_mla_k_pe, m_g_dil_q, m_g_dil_k, m_w_o, m_g_ffn_norm, m_w_up, m_w_conv, m_b_conv, m_w_down, v_w_ada, v_b_ada, v_g_mix_norm, v_w_in, v_g_q_lat, v_w_q_b, v_g_kv_lat, v_w_kv_b, v_g_mla_q_nope, v_g_mla_q_pe, v_g_mla_k_nope, v_g_mla_k_pe, v_g_dil_q, v_g_dil_k, v_w_o, v_g_ffn_norm, v_w_up, v_w_conv, v_b_conv, v_w_down):
    args = dict(locals())
    xi, yi, ci = _place()
    me = 4 * xi + 2 * yi + ci
    def local(prefix, n):
        a = args[prefix + n]
        if n in ROWS_APART:
            return jnp.transpose(a, (2, 0, 1) if n in TRANSPOSED else (1, 0, 2))
        return a[0].T if n in TRANSPOSED else a[0]

    def as_output(n, r):
        if n in ROWS_APART:
            return jnp.transpose(r, (1, 2, 0) if n in TRANSPOSED else (1, 0, 2))
        return (r.T if n in TRANSPOSED else r)[None]

    shard = {n: local("", n) for n in COL_SHARDED + ROW_SHARDED + ("w_ada",)}
    flat = lambda n, a: a.reshape(a.shape[0], a.shape[-1]) if n in ROWS_APART else a
    small_w = {n: args[n] for n in SMALL_PARAMS}

    sc_all, mod_all = ada_modulation("ada_mod", c, shard["w_ada"])

    payload = {n: flat(n, shard[n]) if n == "w_conv" else flat(n, shard[n]).astype(MXU_DTYPE) for n in COL_SHARDED + ROW_SHARDED}
    start_order = [[n for i in groups for n in GATHER_GROUPS[i]] for groups in START_STAGES]
    exchange_of = lambda i: [e for e, groups in enumerate(START_STAGES) if i in groups][0]
    start_stage = lambda e, after: exchange_start(f"gather_start{e}", [payload[n] for n in start_order[e]], gather=True,
                                                  after=after, tree=True)
    gathered = {0: start_stage(0, mod_all)}
    after_start = gathered[0][-1]
    full, forwarded = {}, set()

    def forward(stage, after):
        e = exchange_of(FORWARD_STAGES[stage][0])
        if stage not in forwarded:
            forwarded.add(stage)
            first = start_order[e].index(GATHER_GROUPS[FORWARD_STAGES[stage][0]][0])
            count = sum(len(GATHER_GROUPS[i]) for i in FORWARD_STAGES[stage])
            starts_next = e + 1 < len(START_STAGES) and e + 1 not in gathered
            ready = [payload[n] for n in start_order[e + 1]] if starts_next else []
            gathered[e] = exchange_forward(f"gather_forward{stage}", gathered[e], [after] + ready, first, count)
            if starts_next:
                gathered[e + 1] = start_stage(e + 1, gathered[e][-1])
        return gathered[e][-1]

    def fetch(name, after):
        if name not in full:
            (i, grp), = [(i, grp) for i, grp in enumerate(GATHER_GROUPS) if name in grp]
            (stage,) = [s for s, groups in enumerate(FORWARD_STAGES) if i in groups]
            forward(stage, after)
            if name in FORWARD_WITH:
                forward(FORWARD_WITH[name], after)
            e = exchange_of(i)
            behind = gathered[e + 1][-1] if e + 1 in gathered else after
            srcs, lands = exchange_wait(f"gather{i}_wait", gathered[e], True, behind, start_order[e].index(grp[0]), len(grp), tree=True)
            for n, src, land in zip(grp, srcs, lands, strict=True):
                stack = lax.dynamic_update_index_in_dim(land, src, me, 0)
                full[n] = to_kernel_layout(n, _gather_cols(stack) if n in COL_SHARDED else _gather_rows(stack))
        return full[name]

    mod_row = lax.dynamic_index_in_dim(mod_all, me, axis=1, keepdims=False).reshape(1, 6 * D_MODEL)
    (mod,) = rowwise("ada_bias", lambda rows, params: ([rows[0] + rows[1]], []), [mod_row, b_ada], [], [(6 * D_MODEL, F32)],
                     dep=after_start)

    own, pending, scatters = {}, {}, {}

    def emit(name, grad):
        grad = from_kernel_layout(name, grad)
        parts = _scatter_cols(grad) if name in COL_SHARDED else _scatter_rows(grad)
        own[name] = lax.dynamic_index_in_dim(parts, me, 0, keepdims=False)
        pending[name] = parts
        for i, grp in enumerate(SCATTER_GROUPS):
            if name == grp[-1]:
                scatters[i] = exchange_start(f"scatter{i}_start", [pending[n] for n in grp], gather=False)
                return scatters[i][-1]
        return None

    pos = positions.reshape(SEQ, 1).astype(F32)
    grad_x, small = _local_step(x[0], pos, mod, loss_target[0], small_w, fetch, emit, halfway=lambda after: forward(1, after))

    small_sent = exchange_start("small_start", [_pack_small(small)], gather=True, after=grad_x)

    res, done = {}, small_sent[-1]
    for i, grp in enumerate(SCATTER_GROUPS):
        _, lands = exchange_wait(f"scatter{i}_wait", scatters[i], False, done)
        for n, land in zip(grp, lands, strict=True):
            res[n] = adamw(f"adamw_{n}", shard[n], [own[n], land], local("m_", n), local("v_", n), ADAM_TILE.get(n))
            done = res[n][0]
            res[n] = [as_output(n, r) for r in res[n]]
    (packed,), (landed,) = exchange_wait("small_wait", small_sent, True, done)
    small_all = lax.dynamic_update_index_in_dim(landed, packed, me, 0)
    loss, small_res = adamw_small("adamw_small", small_all, {n: (args[n], args["m_" + n], args["v_" + n]) for n in SMALL_PARAMS})
    row, _, n_mod = SMALL_AT["b_ada"]
    dmod_all = small_all[:, row:row + n_mod // SMALL_COLS, :].reshape(N_DEV, n_mod)
    dmod_mine = lax.dynamic_slice_in_dim(dmod_all, me * (6 * D_MODEL // N_DEV), 6 * D_MODEL // N_DEV, axis=1)
    g_w_ada = matmul("ada_wgrad", sc_all, dmod_mine, "tn")
    res["w_ada"] = [r[None] for r in adamw("adamw_w_ada", shard["w_ada"], [g_w_ada], m_w_ada[0], v_w_ada[0], ADAM_TILE["w_ada"])]

    def leaf(kind, n):
        return res[n][kind] if n in res else small_res[n][kind]

    return (loss.reshape(()), grad_x[None], *[leaf(k, n) for k in range(4) for n in OUT_WEIGHTS])
```

```python
import jax
import jax.numpy as jnp
from jax import lax
from jax.experimental import pallas as pl
from jax.experimental.pallas import tpu as pltpu

F32 = jnp.float32
MXU_DTYPE = jnp.bfloat16

N_DEV = 8
D_MODEL = 1024
SEQ = 2048
HEADS = 8
NOPE = 64
ROPE = 32
Q_LORA = 512
KV_LORA = 256
DIL_DIM = 64
DIL_WIDTH = HEADS * DIL_DIM
DILATIONS = (1, 4, 16)
SPAN = 128
D_FF = 2816
LANES = 128
SUBLANES = 8
ROPE_THETA = 10000.0
EPS = 1e-6
NEG_INF = -1e30
ADAM_LR, ADAM_B1, ADAM_B2, ADAM_EPS, ADAM_WD, ADAM_STEP = 0.001, 0.9, 0.999, 1e-08, 0.01, 10
VMEM_LIMIT = 56 * 1024 * 1024
MESH_ID = pl.DeviceIdType.MESH

P_QLAT, P_KVLAT, P_KPE, P_QD, P_KD, P_VD, P_END = 0, 512, 768, 896, 1408, 1920, 2432
KPE_LO = 64
MIX_IN = HEADS * LANES + DIL_WIDTH


def _params(**kw):
    return pltpu.CompilerParams(vmem_limit_bytes=VMEM_LIMIT, **kw)


def rowwise(name, fn, rows, params, out_rows, out_accs=(), tm=512, dep=None):
    deps = [] if dep is None else [dep]
    rows = [r if isinstance(r, tuple) else (r, r.shape[1], 0) for r in rows]
    R = rows[0][0].shape[0]
    tm = min(tm, R)
    steps = R // tm
    assert steps * tm == R
    in_specs = []
    for a, width, cb in rows:
        ri = a.shape[0]
        per = ri // tm
        assert per * tm == ri
        if ri == R:
            in_specs.append(pl.BlockSpec((tm, width), lambda i, cb=cb: (i, cb)))
        else:
            in_specs.append(pl.BlockSpec((tm, width), lambda i, per=per, cb=cb: (i % per, cb)))
    for p in params:
        in_specs.append(pl.BlockSpec(p.shape, lambda i: (0,) * p.ndim))
    in_specs += [pl.BlockSpec(memory_space=pl.ANY)] * len(deps)
    out_shape = [jax.ShapeDtypeStruct((R, d), dt) for d, dt in out_rows]
    out_specs = [pl.BlockSpec((tm, d), lambda i: (i, 0)) for d, _ in out_rows]
    out_shape += [jax.ShapeDtypeStruct((1, n), F32) for n in out_accs]
    out_specs += [pl.BlockSpec((1, n), lambda i: (0, 0)) for n in out_accs]
    nr, npar, no, na = len(rows), len(params), len(out_rows), len(out_accs)

    def body(*refs):
        rvals = [r[...] for r in refs[:nr]]
        pvals = [r[...] for r in refs[nr:nr + npar]]
        outs, accs = fn(rvals, pvals)
        first_out = nr + npar + len(deps)
        for ref, v in zip(refs[first_out:first_out + no], outs, strict=True):
            ref[...] = v.astype(ref.dtype)
        if na:
            acc_refs = refs[first_out + no:]
            i = pl.program_id(0)

            @pl.when(i == 0)
            def _():
                for ref, v in zip(acc_refs, accs, strict=True):
                    ref[...] = v

            @pl.when(i > 0)
            def _():
                for ref, v in zip(acc_refs, accs, strict=True):
                    ref[...] += v

    res = pl.pallas_call(body, name=name, grid=(steps,), in_specs=in_specs, out_specs=out_specs,
                         out_shape=out_shape, compiler_params=_params())(*[r[0] for r in rows], *params, *deps)
    return list(res)


_DIMS = {"nn": ((1,), (0,)), "nt": ((1,), (1,)), "tn": ((0,), (0,))}


def _dot(a, b, mode="nn"):
    return lax.dot_general(a.astype(MXU_DTYPE), b.astype(MXU_DTYPE), (_DIMS[mode], ((), ())),
                           preferred_element_type=F32)


def matmul(name, a, b, mode, tm=None, tn=None, tk=None, out_dtype=F32, dep=None):
    if mode == "tn":
        K, M = a.shape
    else:
        M, K = a.shape
    N = b.shape[0] if mode == "nt" else b.shape[1]
    tm, tn, tk = tm or M, tn or N, tk or K
    nm, nn, nk = M // tm, N // tn, K // tk
    assert nm * tm == M and nn * tn == N and nk * tk == K
    a_spec = pl.BlockSpec((tk, tm), lambda i, j, k: (k, i)) if mode == "tn" else pl.BlockSpec((tm, tk), lambda i, j, k: (i, k))
    b_spec = pl.BlockSpec((tn, tk), lambda i, j, k: (j, k)) if mode == "nt" else pl.BlockSpec((tk, tn), lambda i, j, k: (k, j))
    deps = [] if dep is None else [dep]

    def body(a_ref, b_ref, *rest):
        o_ref, scratch = rest[len(deps)], rest[len(deps) + 1:]
        p = _dot(a_ref[...], b_ref[...], mode)
        if nk == 1:
            o_ref[...] = p.astype(o_ref.dtype)
        else:
            acc = scratch[0]
            k = pl.program_id(2)

            @pl.when(k == 0)
            def _():
                acc[...] = p

            @pl.when(k > 0)
            def _():
                acc[...] += p

            @pl.when(k == nk - 1)
            def _():
                o_ref[...] = acc[...].astype(o_ref.dtype)

    return pl.pallas_call(
        body, name=name, grid=(nm, nn, nk), in_specs=[a_spec, b_spec] + [pl.BlockSpec(memory_space=pl.ANY)] * len(deps),
        out_specs=pl.BlockSpec((tm, tn), lambda i, j, k: (i, j)),
        out_shape=jax.ShapeDtypeStruct((M, N), out_dtype),
        scratch_shapes=[pltpu.VMEM((tm, tn), F32)] if nk > 1 else [],
        compiler_params=_params())(a, b, *deps)


def _rms(x, g):
    rstd = lax.rsqrt(jnp.mean(x * x, axis=-1, keepdims=True) + EPS)
    n = x * rstd
    return n * g, n, rstd


def _rms_bwd(dy, n, rstd, g):
    dg = jnp.sum(dy * n, axis=0, keepdims=True)
    dn = dy * g
    dx = rstd * (dn - n * jnp.mean(dn * n, axis=-1, keepdims=True))
    return dx, dg


def _norm_bwd(dy, x, g):
    _, n, rstd = _rms(x, g)
    return _rms_bwd(dy, n, rstd, g)


def _colsum(v):
    return jnp.sum(v, axis=0, keepdims=True)


def _silu(x):
    return x * (1.0 / (1.0 + jnp.exp(-x)))


def _lane(shape):
    return lax.broadcasted_iota(jnp.int32, shape, 1)


def _group_mean(v, groups):
    i = lax.broadcasted_iota(jnp.int32, (LANES, LANES), 0)
    j = lax.broadcasted_iota(jnp.int32, (LANES, LANES), 1)
    g = jnp.zeros((LANES, LANES), F32)
    for lo, hi in groups:
        g = jnp.where((i >= lo) & (i < hi) & (j >= lo) & (j < hi), 1.0 / (hi - lo), g)
    head = v.astype(MXU_DTYPE)
    return _dot(head, g) + _dot(v - head.astype(F32), g)


def _in_groups(shape, groups):
    lane = _lane(shape)
    m = jnp.zeros(shape, jnp.bool_)
    for lo, hi in groups:
        m = m | ((lane >= lo) & (lane < hi))
    return m


def _grms(x, g, groups):
    rstd = lax.rsqrt(_group_mean(x * x, groups) + EPS)
    n = jnp.where(_in_groups(x.shape, groups), x * rstd, 0.0)
    return n * g, n, rstd


def _grms_bwd(dy, n, rstd, g, groups):
    dn = dy * g
    return rstd * (dn - n * _group_mean(dn * n, groups)), _colsum(dy * n)


def _rot(x, half, transpose=False):
    first = (_lane(x.shape) % (2 * half)) < half
    up = pltpu.roll(x, LANES - half, axis=1)
    down = pltpu.roll(x, half, axis=1)
    return jnp.where(first, up, -down) if transpose else jnp.where(first, -up, down)


def _rope(x, cos, sin, half):
    return x * cos + _rot(x, half) * sin


def _rope_bwd(dy, cos, sin, half):
    return dy * cos + _rot(dy * sin, half, transpose=True)


def _chunks(x):
    return [x[:, i:i + LANES] for i in range(0, x.shape[1], LANES)]


Q_GROUPS = ((0, NOPE), (NOPE, NOPE + ROPE))
K_GROUPS = ((0, NOPE),)
KPE_GROUPS = ((KPE_LO, KPE_LO + ROPE),)
DIL_GROUPS = ((0, DIL_DIM), (DIL_DIM, 2 * DIL_DIM))


def _col(width, rows=SEQ):
    return pl.BlockSpec((rows, width), lambda h: (0, h))


def _causal_tail(s, tq, fill):
    diag = s[:, s.shape[1] - tq:]
    keep = lax.broadcasted_iota(jnp.int32, diag.shape, 1) <= lax.broadcasted_iota(jnp.int32, diag.shape, 0)
    diag = jnp.where(keep, diag, fill)
    return diag if s.shape[1] == tq else jnp.concatenate([s[:, :s.shape[1] - tq], diag], axis=1)


def mla_fwd(name, q, k, v, scale, tq=256):
    S = q.shape[0]

    def body(q_ref, k_ref, v_ref, o_ref, lse_ref):
        nb = S // tq
        blk = lambda i: slice(i * tq, (i + 1) * tq)

        def scores(i):
            return _dot(q_ref[blk(i), :], k_ref[:(i + 1) * tq, :], "nt")

        def softmax(i, s):
            s = _causal_tail(s * scale, tq, NEG_INF)
            m = jnp.max(s, axis=-1, keepdims=True)
            e = jnp.exp(s - m)
            l = jnp.sum(e, axis=-1, keepdims=True)
            lse_ref[0, blk(i), :] = m + jnp.log(l)
            return (e * (1.0 / l)).astype(MXU_DTYPE)

        def weighted(i, p):
            o_ref[blk(i), :] = _dot(p, v_ref[:(i + 1) * tq, :])

        s, p_prev = scores(0), None
        for i in range(nb):
            s_next = scores(i + 1) if i + 1 < nb else None
            if p_prev is not None:
                weighted(i - 1, p_prev)
            p_prev, s = softmax(i, s), s_next
        weighted(nb - 1, p_prev)

    return pl.pallas_call(
        body, name=name, grid=(HEADS,), in_specs=[_col(LANES)] * 3,
        out_specs=[_col(LANES), pl.BlockSpec((1, S, 1), lambda h: (h, 0, 0))],
        out_shape=[jax.ShapeDtypeStruct((S, MIX_IN), F32), jax.ShapeDtypeStruct((HEADS, S, 1), F32)],
        compiler_params=_params())(q, k, v)


def mla_bwd(name, q, k, v, o, do, lse, scale, tq=256):
    S = q.shape[0]

    def body(q_ref, k_ref, v_ref, o_ref, do_ref, lse_ref, dq_ref, dkv_ref, dkpe_ref, dk_acc, dv_acc):
        dk_acc[...] = jnp.zeros_like(dk_acc)
        dv_acc[...] = jnp.zeros_like(dv_acc)
        for i in range(S // tq):
            kext = (i + 1) * tq
            blk = slice(i * tq, kext)
            qi, kk, vv = q_ref[blk, :], k_ref[:kext, :], v_ref[:kext, :]
            doi = do_ref[blk, :]
            s = _causal_tail(_dot(qi, kk, "nt") * scale, tq, NEG_INF)
            p = jnp.exp(s - lse_ref[0, blk, :])
            dp = _dot(doi, vv, "nt")
            delta = jnp.sum(doi * o_ref[blk, :], axis=-1, keepdims=True)
            ds = p * (dp - delta) * scale
            dq_ref[blk, :] = _dot(ds, kk)
            dk_acc[:kext, :] += _dot(ds, qi, "tn")
            dv_acc[:kext, :] += _dot(p, doi, "tn")
        dk = dk_acc[...]
        lane = _lane(dk.shape)
        dkv_ref[...] = jnp.where(lane < NOPE, dk, 0.0) + dv_acc[...]
        dkpe = jnp.where((lane >= KPE_LO) & (lane < KPE_LO + ROPE), dk, 0.0)
        h = pl.program_id(0)

        @pl.when(h == 0)
        def _():
            dkpe_ref[...] = dkpe

        @pl.when(h > 0)
        def _():
            dkpe_ref[...] += dkpe

    return pl.pallas_call(
        body, name=name, grid=(HEADS,),
        in_specs=[_col(LANES)] * 5 + [pl.BlockSpec((1, S, 1), lambda h: (h, 0, 0))],
        out_specs=[_col(LANES), _col(LANES), pl.BlockSpec((S, LANES), lambda h: (0, 0))],
        out_shape=[jax.ShapeDtypeStruct((S, HEADS * LANES), F32), jax.ShapeDtypeStruct((S, HEADS * LANES), F32),
                   jax.ShapeDtypeStruct((S, LANES), F32)],
        scratch_shapes=[pltpu.VMEM((S, LANES), F32), pltpu.VMEM((S, LANES), F32)],
        compiler_params=_params())(q, k, v, o, do, lse)


BAND_TQ = SPAN


def _band_blocks(L, tq):
    return [(i * tq, (i + 1) * tq, max(0, i * tq - SPAN)) for i in range(L // tq)]


def _class_rows(r, dil, lo, hi):
    return pl.ds(r + dil * lo, hi - lo, stride=dil) if dil > 1 else pl.ds(lo, hi - lo)


def _stack_heads(t, lo):
    zero = jnp.zeros_like(t)
    return jnp.concatenate([jnp.where(lo, t, zero), jnp.where(lo, zero, t)], axis=0)


def _band_mask2(q0, q1, k0):
    n = q1 - q0
    shape = (2 * n, q1 - k0)
    i = lax.broadcasted_iota(jnp.int32, shape, 0)
    dist = (jnp.where(i >= n, i - n, i) + q0) - (lax.broadcasted_iota(jnp.int32, shape, 1) + k0)
    return (dist >= 0) & (dist <= SPAN)


def _pair_col(col0=0):
    return pl.BlockSpec((SEQ, LANES), lambda j: (0, col0 // LANES + j))


def band_fwd(name, q, k, v, dil, dep=None):
    S = q.shape[0]
    L = S // dil
    tq = BAND_TQ
    scale = DIL_DIM ** -0.5
    deps = [] if dep is None else [dep]

    def body(q_ref, k_ref, v_ref, *rest):
        o_ref, lse_ref = rest[len(deps):]
        items = [(r, blk) for r in range(dil) for blk in _band_blocks(L, tq)]
        lo = _lane((tq, LANES)) < DIL_DIM

        def scores(item):
            r, (q0, q1, k0) = item
            qb = q_ref[_class_rows(r, dil, q0, q1), :].astype(MXU_DTYPE)
            return _dot(_stack_heads(qb, lo), k_ref[_class_rows(r, dil, k0, q1), :], "nt")

        def softmax(item, s):
            _, (q0, q1, k0) = item
            s = jnp.where(_band_mask2(q0, q1, k0), s * scale, NEG_INF)
            mx = jnp.max(s, axis=-1, keepdims=True)
            e = jnp.exp(s - mx)
            l = jnp.sum(e, axis=-1, keepdims=True)
            return (e * (1.0 / l)).astype(MXU_DTYPE), mx + jnp.log(l)

        def weighted(item, p, lse):
            r, (q0, q1, k0) = item
            pv = _dot(p, v_ref[_class_rows(r, dil, k0, q1), :])
            o_ref[_class_rows(r, dil, q0, q1), :] = jnp.where(lo, pv[:tq], pv[tq:])
            lse_ref[_class_rows(r, dil, q0, q1), :] = jnp.where(lo, lse[:tq], lse[tq:])

        s, prev = scores(items[0]), None
        for i, item in enumerate(items):
            s_next = scores(items[i + 1]) if i + 1 < len(items) else None
            if prev is not None:
                weighted(items[i - 1], *prev)
            prev, s = softmax(item, s), s_next
        weighted(items[-1], *prev)

    return pl.pallas_call(
        body, name=name, grid=(DIL_WIDTH // LANES,),
        in_specs=[_pair_col()] * 2 + [_pair_col(P_VD)] + [pl.BlockSpec(memory_space=pl.ANY)] * len(deps), out_specs=[_pair_col()] * 2,
        out_shape=[jax.ShapeDtypeStruct((S, DIL_WIDTH), F32)] * 2, compiler_params=_params())(q, k, v, *deps)


def band_bwd(name, q, k, v, lse, lse_mix, o_cat, do_cat, dil, before=None):
    S = q.shape[0]
    L = S // dil
    tq = BAND_TQ
    scale = DIL_DIM ** -0.5
    before = list(before or [])

    def body(q_ref, k_ref, v_ref, lse_ref, mix_ref, o_ref, do_ref, *rest):
        dq_ref, dk_ref, dv_ref = rest[len(before):]
        if before:
            dq0_ref, dk0_ref, dv0_ref = rest[:3]
            dk_ref[...] = dk0_ref[...]
            dv_ref[...] = dv0_ref[...]
        else:
            dk_ref[...] = jnp.zeros_like(dk_ref)
            dv_ref[...] = jnp.zeros_like(dv_ref)
        items = [(r, blk) for r in range(dil) for blk in _band_blocks(L, tq)]
        lo = _lane((tq, LANES)) < DIL_DIM
        per_head = lambda t: jnp.concatenate([t[:, 0:1], t[:, DIL_DIM:DIL_DIM + 1]], axis=0)

        def scores(item):
            r, (q0, q1, k0) = item
            qrows, krows = _class_rows(r, dil, q0, q1), _class_rows(r, dil, k0, q1)
            lse_p, dout = lse_ref[qrows, :], do_ref[qrows, :]
            w2 = per_head(jnp.exp(lse_p - mix_ref[qrows, :]))
            dd = dout * o_ref[qrows, :]
            big_d = jnp.concatenate([jnp.sum(jnp.where(lo, dd, 0.0), axis=-1, keepdims=True),
                                     jnp.sum(jnp.where(lo, 0.0, dd), axis=-1, keepdims=True)], axis=0)
            q2 = _stack_heads(q_ref[qrows, :].astype(MXU_DTYPE), lo)
            dom = (_stack_heads(dout, lo) * w2).astype(MXU_DTYPE)
            return (_dot(q2, k_ref[krows, :], "nt"), _dot(dom, v_ref[krows, :], "nt"), per_head(lse_p), w2 * big_d, q2, dom)

        def softmax_bwd(item, s, dp, lse2, wd2, q2, dom):
            _, (q0, q1, k0) = item
            p = jnp.where(_band_mask2(q0, q1, k0), jnp.exp(s * scale - lse2), 0.0)
            return p.astype(MXU_DTYPE), (p * (dp - wd2) * scale).astype(MXU_DTYPE), q2, dom

        def grads(item, p, ds, q2, dom):
            r, (q0, q1, k0) = item
            qrows, krows = _class_rows(r, dil, q0, q1), _class_rows(r, dil, k0, q1)
            dq2 = _dot(ds, k_ref[krows, :])
            dq = jnp.where(lo, dq2[:tq], dq2[tq:])
            dq_ref[qrows, :] = dq + dq0_ref[qrows, :] if before else dq
            dk_ref[krows, :] += _dot(ds, q2, "tn")
            dv_ref[krows, :] += _dot(p, dom, "tn")

        sc, prev = scores(items[0]), None
        for i, item in enumerate(items):
            sc_next = scores(items[i + 1]) if i + 1 < len(items) else None
            if prev is not None:
                grads(items[i - 1], *prev)
            prev, sc = softmax_bwd(item, *sc), sc_next
        grads(items[-1], *prev)

    cat = _pair_col(HEADS * LANES)
    return pl.pallas_call(
        body, name=name, grid=(DIL_WIDTH // LANES,),
        in_specs=[_pair_col()] * 2 + [_pair_col(P_VD)] + [_pair_col()] * 2 + [cat] * 2 + [_pair_col()] * len(before),
        out_specs=[_pair_col()] * 3, out_shape=[jax.ShapeDtypeStruct((S, DIL_WIDTH), F32)] * 3,
        compiler_params=_params())(q, k, v, lse, lse_mix, o_cat, do_cat, *before)


def combine_fwd(name, outs, lses, o_cat, tm=512):
    S = outs[0].shape[0]

    def body(o1, o2, o3, l1, l2, l3, cat_in, cat_out, mix_ref):
        ls = [l1[...], l2[...], l3[...]]
        m = jnp.maximum(jnp.maximum(ls[0], ls[1]), ls[2])
        e = [jnp.exp(l - m) for l in ls]
        den = e[0] + e[1] + e[2]
        cat_out[...] = (e[0] / den) * o1[...] + (e[1] / den) * o2[...] + (e[2] / den) * o3[...]
        mix_ref[...] = m + jnp.log(den)

    row = pl.BlockSpec((tm, DIL_WIDTH), lambda i: (i, 0))
    return pl.pallas_call(
        body, name=name, grid=(S // tm,), in_specs=[row] * 6 + [pl.BlockSpec(memory_space=pl.ANY)],
        out_specs=[pl.BlockSpec((tm, DIL_WIDTH), lambda i: (i, HEADS * LANES // DIL_WIDTH)), row],
        out_shape=[jax.ShapeDtypeStruct(o_cat.shape, F32), jax.ShapeDtypeStruct((S, DIL_WIDTH), F32)],
        input_output_aliases={6: 0}, compiler_params=_params())(*outs, *lses, o_cat)


FFN_FWD_ROWS = 512
FFN_BWD_ROWS = 256
CONV_PAD = SUBLANES


def _window(x, k):
    groups = x.reshape(-1, SUBLANES, x.shape[1])
    turned = pltpu.roll(groups, SUBLANES - k, axis=1)
    stays = lax.broadcasted_iota(jnp.int32, (groups.shape[0] - 1,) + groups.shape[1:], 1) < SUBLANES - k
    return jnp.where(stays, turned[:-1], turned[1:]).reshape(-1, x.shape[1])


def _earlier(ref, r0, rows, n):
    if r0 == 0:
        x = jnp.concatenate([jnp.zeros((SUBLANES, ref.shape[1]), F32), ref[:rows, :]], axis=0)
    else:
        x = ref[r0 - SUBLANES:r0 + rows, :]
    return _window(x, SUBLANES - n)


CONV_TC = 256
CONV_NB = D_FF // CONV_TC


def _half_specs(rows, rows_axis=False):
    if rows_axis:
        return [pl.BlockSpec((rows, D_MODEL), lambda j: (j, 0)), pl.BlockSpec((rows, D_MODEL), lambda j: (j + CONV_NB, 0))]
    return [pl.BlockSpec((rows, CONV_TC), lambda j: (0, j)), pl.BlockSpec((rows, CONV_TC), lambda j: (0, j + CONV_NB))]


def _whole(a):
    return pl.BlockSpec(a.shape, lambda j: (0,) * a.ndim)


def _up_pair(h, ug_ref, uv_ref):
    return jnp.concatenate([_dot(h, ug_ref[...], "nt"), _dot(h, uv_ref[...], "nt")], axis=1)


def _conv_taps(up_ref, r0, rows, w, b):
    uin, u1, u2 = up_ref[r0:r0 + rows, :], _earlier(up_ref, r0, rows, 1), _earlier(up_ref, r0, rows, 2)
    return uin, u1, u2, w[2:3, :] * uin + w[1:2, :] * u1 + w[0:1, :] * u2 + b


def ffn_fwd(name, h, w_up_t, w_conv, b_conv, w_down):
    S = h.shape[0]

    def body(h_ref, ug_ref, uv_ref, wg_ref, wv_ref, bg_ref, bv_ref, wd_ref, dn_ref, up_ref):
        @pl.when(pl.program_id(0) == 0)
        def _():
            dn_ref[...] = jnp.zeros_like(dn_ref)

        w = jnp.concatenate([wg_ref[...], wv_ref[...]], axis=1)
        b = jnp.concatenate([bg_ref[...], bv_ref[...]], axis=1)
        rows = FFN_FWD_ROWS
        starts = list(range(0, S, rows))

        def project(r0):
            up_ref[r0:r0 + rows, :] = _up_pair(h_ref[r0:r0 + rows, :], ug_ref, uv_ref)

        def gate(r0):
            u = _conv_taps(up_ref, r0, rows, w, b)[3]
            return (_silu(u[:, :CONV_TC]) * u[:, CONV_TC:]).astype(MXU_DTYPE)

        def project_down(r0, act):
            dn_ref[r0:r0 + rows, :] += _dot(act, wd_ref[...])

        project(starts[0])
        act_prev = None
        for i, r0 in enumerate(starts):
            if i + 1 < len(starts):
                project(starts[i + 1])
            if act_prev is not None:
                project_down(starts[i - 1], act_prev)
            act_prev = gate(r0)
        project_down(starts[-1], act_prev)

    return pl.pallas_call(
        body, name=name, grid=(CONV_NB,),
        in_specs=[_whole(h)] + _half_specs(CONV_TC, rows_axis=True) + _half_specs(3) + _half_specs(1)
        + [pl.BlockSpec((CONV_TC, w_down.shape[1]), lambda j: (j, 0))],
        out_specs=[pl.BlockSpec((S, w_down.shape[1]), lambda j: (0, 0)), pl.BlockSpec((S, 2 * CONV_TC), lambda j: (0, j))],
        out_shape=[jax.ShapeDtypeStruct((S, w_down.shape[1]), F32), jax.ShapeDtypeStruct((S, 2 * D_FF), F32)],
        compiler_params=_params())(h, w_up_t, w_up_t, w_conv, w_conv, b_conv, b_conv, w_down)


def ffn_bwd(name, h, up, w_up_t, w_conv, b_conv, d_dn, w_down):
    S, D = h.shape

    def body(h_ref, up_ref, ug_ref, uv_ref, wg_ref, wv_ref, bg_ref, bv_ref, dd_ref, wd_ref,
             dh_ref, gup_ref, gd_ref, dwg_ref, dwv_ref, dbg_ref, dbv_ref, du_ref, dup_ref, act_ref):
        @pl.when(pl.program_id(0) == 0)
        def _():
            dh_ref[...] = jnp.zeros_like(dh_ref)

        w = jnp.concatenate([wg_ref[...], wv_ref[...]], axis=1)
        b = jnp.concatenate([bg_ref[...], bv_ref[...]], axis=1)
        w_pair = jnp.concatenate([ug_ref[...], uv_ref[...]], axis=0)
        rows = FFN_BWD_ROWS
        starts = list(range(0, S, rows))
        du_ref[S:S + CONV_PAD, :] = jnp.zeros((CONV_PAD, 2 * CONV_TC), F32)

        def project(r0):
            return _dot(dd_ref[r0:r0 + rows, :], wd_ref[...], "nt")

        def through_conv(r0, da):
            uin, u1, u2, u = _conv_taps(up_ref, r0, rows, w, b)
            gate, val = u[:, :CONV_TC], u[:, CONV_TC:]
            sig = 1.0 / (1.0 + jnp.exp(-gate))
            du = jnp.concatenate([da * val * (sig * (1.0 + gate * (1.0 - sig))), da * (gate * sig)], axis=1)
            du_ref[r0:r0 + rows, :] = du
            act_ref[r0:r0 + rows, :] = (gate * sig * val).astype(MXU_DTYPE)
            dw = jnp.concatenate([_colsum(du * u2), _colsum(du * u1), _colsum(du * uin)], axis=0)
            return dw, _colsum(du)

        def back_up(r0):
            du = du_ref[r0:r0 + rows + CONV_PAD, :]
            dup = (w[2:3, :] * du[:rows] + w[1:2, :] * _window(du, 1) + w[0:1, :] * _window(du, 2)).astype(MXU_DTYPE)
            dup_ref[r0:r0 + rows, :] = dup
            dh_ref[r0:r0 + rows, :] += _dot(dup, w_pair)

        dw, db = 0.0, 0.0
        da = project(starts[0])
        for i, r0 in enumerate(starts):
            da_next = project(starts[i + 1]) if i + 1 < len(starts) else None
            dw_c, db_c = through_conv(r0, da)
            if i > 0:
                back_up(starts[i - 1])
            dw, db, da = dw + dw_c, db + db_c, da_next
        back_up(starts[-1])
        g_up, g_dn = _dot(dup_ref[...], h_ref[...], "tn"), _dot(act_ref[...], dd_ref[...], "tn")
        gup_ref[0], gup_ref[1] = g_up[:CONV_TC].astype(gup_ref.dtype), g_up[CONV_TC:].astype(gup_ref.dtype)
        gd_ref[...] = g_dn.astype(gd_ref.dtype)
        dwg_ref[...], dwv_ref[...] = dw[:, :CONV_TC], dw[:, CONV_TC:]
        dbg_ref[...], dbv_ref[...] = db[:, :CONV_TC], db[:, CONV_TC:]

    half = lambda rows: pl.BlockSpec((rows, CONV_TC), lambda j: (0, j))
    rows_blk = pl.BlockSpec((CONV_TC, D), lambda j: (j, 0))
    dh, gup, gd, dwg, dwv, dbg, dbv = pl.pallas_call(
        body, name=name, grid=(CONV_NB,),
        in_specs=[_whole(h), pl.BlockSpec((S, 2 * CONV_TC), lambda j: (0, j))] + _half_specs(CONV_TC, rows_axis=True) + _half_specs(3)
        + _half_specs(1) + [_whole(d_dn), rows_blk],
        out_specs=[pl.BlockSpec((S, D), lambda j: (0, 0)), pl.BlockSpec((2, CONV_TC, D), lambda j: (0, j, 0)), rows_blk,
                   half(3), half(3), half(1), half(1)],
        out_shape=[jax.ShapeDtypeStruct((S, D), F32), jax.ShapeDtypeStruct((2, D_FF, D), MXU_DTYPE),
                   jax.ShapeDtypeStruct((D_FF, D), MXU_DTYPE)]
        + [jax.ShapeDtypeStruct((3, D_FF), F32)] * 2 + [jax.ShapeDtypeStruct((1, D_FF), F32)] * 2,
        scratch_shapes=[pltpu.VMEM((S + CONV_PAD, 2 * CONV_TC), F32), pltpu.VMEM((S, 2 * CONV_TC), MXU_DTYPE),
                        pltpu.VMEM((S, CONV_TC), MXU_DTYPE)],
        compiler_params=_params())(h, up, w_up_t, w_up_t, w_conv, w_conv, b_conv, b_conv, d_dn, w_down)
    return dh, gup.reshape(2 * D_FF, D), gd, jnp.concatenate([dwg, dwv], axis=1), jnp.concatenate([dbg, dbv], axis=1)


def adamw(name, w, parts, m, v, tr=None):
    apart = w.ndim == 3
    R, C = w.shape[0], w.shape[-1]
    tr = tr or R
    assert R % tr == 0
    c1 = 1.0 - ADAM_B1 ** ADAM_STEP
    c2 = 1.0 - ADAM_B2 ** ADAM_STEP
    np_ = len(parts)

    def body(*refs):
        w_ref, m_ref, v_ref = refs[0], refs[1 + np_], refs[2 + np_]
        go_ref, d_ref, mo_ref, vo_ref = refs[3 + np_:]
        terms = []
        for part, ref in zip(parts, refs[1:1 + np_], strict=True):
            terms += [ref[...]] if part.ndim == 2 else [ref[p] for p in range(part.shape[0])]
        g = terms[0].astype(F32)
        for term in terms[1:]:
            g = g + term.astype(F32)
        m2 = ADAM_B1 * m_ref[...] + (1.0 - ADAM_B1) * g
        v2 = ADAM_B2 * v_ref[...] + (1.0 - ADAM_B2) * (g * g)
        go_ref[...] = g
        mo_ref[...] = m2
        vo_ref[...] = v2
        d_ref[...] = -ADAM_LR * ((m2 / c1) / (jnp.sqrt(v2 / c2) + ADAM_EPS) + ADAM_WD * w_ref[...])

    blk = pl.BlockSpec((tr, C), lambda i: (i, 0))
    own = pl.BlockSpec((tr, None, C), lambda i: (i, 0, 0)) if apart else blk
    part_specs = [blk if p.ndim == 2 else pl.BlockSpec((p.shape[0], tr, C), lambda i: (0, i, 0)) for p in parts]
    return pl.pallas_call(
        body, name=name, grid=(R // tr,),
        in_specs=[own] + part_specs + [own, own], out_specs=[own] * 4,
        out_shape=[jax.ShapeDtypeStruct(w.shape, F32)] * 4, compiler_params=_params())(w, *parts, m, v)


def _place():
    return lax.axis_index("x"), lax.axis_index("y"), lax.axis_index("c")


def ada_modulation(name, c, w_ada, after=()):
    n_mod = w_ada.shape[1]

    def exchange(src_ref, dst_ref, send_sems, recv_sems):
        x, y, c_ = _place()
        me = 4 * x + 2 * y + c_
        copies = []
        for k in range(1, N_DEV):
            px, py, pc = x ^ (k >> 2), y ^ ((k >> 1) & 1), c_ ^ (k & 1)
            copies.append(pltpu.make_async_remote_copy(
                src_ref=src_ref, dst_ref=dst_ref.at[me], send_sem=send_sems.at[k - 1], recv_sem=recv_sems.at[k - 1],
                device_id=(px, py, pc), device_id_type=MESH_ID))
        for cp in copies:
            cp.start()
        for cp in copies:
            cp.wait_recv()
        for cp in copies:
            cp.wait_send()
        return me

    def body(c_ref, w_ref, *refs):
        sc_ref, mod_ref, c_all, send_c, recv_c, send_m, recv_m = refs[len(after):]
        me = exchange(c_ref, c_all, send_c, recv_c)
        c_all[me] = c_ref[...]
        sc = _silu(jnp.concatenate([c_all[p] for p in range(N_DEV)], axis=0))
        sc_ref[...] = sc.astype(sc_ref.dtype)
        mod_ref[me] = _dot(sc, w_ref[...])
        exchange(mod_ref.at[me], mod_ref, send_m, recv_m)

    vmem = pl.BlockSpec(memory_space=pltpu.VMEM)
    return pl.pallas_call(
        body, name=name, in_specs=[vmem, vmem] + [pl.BlockSpec(memory_space=pl.ANY)] * len(after), out_specs=[vmem, vmem],
        out_shape=[jax.ShapeDtypeStruct((N_DEV, c.shape[1]), MXU_DTYPE), jax.ShapeDtypeStruct((N_DEV, N_DEV, n_mod), F32)],
        scratch_shapes=[pltpu.VMEM((N_DEV, 1, c.shape[1]), F32)] + [pltpu.SemaphoreType.DMA((N_DEV - 1,))] * 4,
        compiler_params=pltpu.CompilerParams(has_side_effects=True, vmem_limit_bytes=VMEM_LIMIT))(c, w_ada, *after)


HBM_SPEC = pl.BlockSpec(memory_space=pltpu.HBM)
SEM_SPEC = pl.BlockSpec(memory_space=pltpu.SEMAPHORE)
DATAFLOW = pltpu.SideEffectType.DATAFLOW_SIDE_EFFECTING


def _exchange_copies(srcs, lands, send_sems, recv_sems, gather, first=0):
    x, y, c = _place()
    me = 4 * x + 2 * y + c
    out = []
    for t, (src, land) in enumerate(zip(srcs, lands, strict=True)):
        for k in range(1, N_DEV):
            px, py, pc = x ^ (k >> 2), y ^ ((k >> 1) & 1), c ^ (k & 1)
            sem = 7 * (first + t) + k - 1
            out.append((k, pltpu.make_async_remote_copy(
                src_ref=src if gather else src.at[4 * px + 2 * py + pc],
                dst_ref=land.at[me] if gather else land.at[k - 1],
                send_sem=send_sems.at[sem], recv_sem=recv_sems.at[sem],
                device_id=(px, py, pc), device_id_type=MESH_ID)))
    return out


TREE_DIRECT = (1, 2, 4, 6)
TREE_FORWARDED = (3, 5, 7)


def exchange_start(name, arrs, gather, after=None, tree=False):
    n = len(arrs)
    lands = [lax.empty(((N_DEV,) + a.shape) if gather else ((N_DEV - 1,) + a.shape[1:]), a.dtype) for a in arrs]
    deps = [] if after is None else [after]

    def body(*refs):
        srcs, land_refs = refs[:n], refs[n:2 * n]
        send_sems, recv_sems = refs[2 * n + len(deps)], refs[2 * n + len(deps) + 1]
        token = refs[-1]
        for k, cp in _exchange_copies(srcs, land_refs, send_sems, recv_sems, gather):
            if not tree or k in TREE_DIRECT:
                cp.start()
        token[...] = jnp.zeros_like(token)

    hbm = lambda a: pltpu.HBM(a.shape, a.dtype)
    res = pl.pallas_call(
        body, name=name,
        out_shape=(pltpu.SemaphoreType.DMA((7 * n,)), pltpu.SemaphoreType.DMA((7 * n,)), *[hbm(a) for a in arrs],
                   *[hbm(l) for l in lands], jax.ShapeDtypeStruct((8, 128), F32)),
        in_specs=[HBM_SPEC] * (2 * n) + [pl.BlockSpec(memory_space=pl.ANY)] * len(deps),
        out_specs=(SEM_SPEC, SEM_SPEC, *[HBM_SPEC] * (2 * n), pl.BlockSpec(memory_space=pltpu.VMEM)),
        input_output_aliases={i: 2 + i for i in range(2 * n)},
        compiler_params=pltpu.CompilerParams(has_side_effects=DATAFLOW),
    )(*[pltpu.with_memory_space_constraint(a, pltpu.HBM) for a in arrs + lands], *deps)
    return res[0], res[1], list(res[2:2 + n]), list(res[2 + n:2 + 2 * n]), res[-1]


def exchange_forward(name, started, after, first=0, count=None):
    send_sems, recv_sems, srcs, lands, _ = started
    count = len(srcs) - first if count is None else count
    mine = lands[first:first + count]
    n = len(mine)

    def copies(land_refs, send_ref, recv_ref):
        x, y, c = _place()
        out = []
        for t, land in enumerate(land_refs):
            for k in (2, 4, 6):
                slot = land.at[4 * (x ^ (k >> 2)) + 2 * (y ^ ((k >> 1) & 1)) + c]
                came, goes = 7 * (first + t) + k - 1, 7 * (first + t) + (k ^ 1) - 1
                out.append((
                    pltpu.make_async_remote_copy(src_ref=slot, dst_ref=slot, send_sem=send_ref.at[came], recv_sem=recv_ref.at[came],
                                                 device_id=(x, y, c), device_id_type=MESH_ID),
                    pltpu.make_async_remote_copy(src_ref=slot, dst_ref=slot, send_sem=send_ref.at[goes], recv_sem=recv_ref.at[goes],
                                                 device_id=(x, y, 1 - c), device_id_type=MESH_ID)))
        return out

    after = list(after) if isinstance(after, (list, tuple)) else [after]

    def arrived(*refs):
        for came, _ in copies(refs[:n], refs[n], refs[n + 1]):
            came.wait_recv()

    def pass_on(*refs):
        for _, goes in copies(refs[:n], refs[n], refs[n + 1]):
            goes.start()
        refs[-1][...] = jnp.zeros_like(refs[-1])

    hbm = lambda a: pltpu.HBM(a.shape, a.dtype)
    here = pl.pallas_call(
        arrived, name=name + "_arrived", out_shape=tuple(hbm(a) for a in mine),
        in_specs=[HBM_SPEC] * n + [SEM_SPEC, SEM_SPEC] + [pl.BlockSpec(memory_space=pl.ANY)] * len(after),
        out_specs=tuple([HBM_SPEC] * n), input_output_aliases={i: i for i in range(n)},
        compiler_params=pltpu.CompilerParams(has_side_effects=DATAFLOW),
    )(*mine, send_sems, recv_sems, *after)
    res = pl.pallas_call(
        pass_on, name=name, out_shape=(*[hbm(a) for a in mine], jax.ShapeDtypeStruct((8, 128), F32)),
        in_specs=[HBM_SPEC] * n + [SEM_SPEC, SEM_SPEC],
        out_specs=(*[HBM_SPEC] * n, pl.BlockSpec(memory_space=pltpu.VMEM)), input_output_aliases={i: i for i in range(n)},
        compiler_params=pltpu.CompilerParams(has_side_effects=DATAFLOW),
    )(*here, send_sems, recv_sems)
    lands = lands[:first] + list(res[:n]) + lands[first + count:]
    return (send_sems, recv_sems, srcs, lands, res[-1])


def exchange_wait(name, started, gather, after, first=0, count=None, tree=False):
    send_sems, recv_sems, srcs, lands, _ = started
    count = len(srcs) - first if count is None else count
    srcs, lands = srcs[first:first + count], lands[first:first + count]
    n = len(srcs)

    def body(*refs):
        src_refs, land_refs = refs[:n], refs[n:2 * n]
        copies = _exchange_copies(src_refs, land_refs, refs[2 * n], refs[2 * n + 1], gather, first)
        for _, cp in copies:
            cp.wait_send()
        for k, cp in copies:
            if not tree or k in (1,) + TREE_FORWARDED:
                cp.wait_recv()

    hbm = lambda a: pltpu.HBM(a.shape, a.dtype)
    res = pl.pallas_call(
        body, name=name, out_shape=tuple(hbm(a) for a in srcs + lands),
        in_specs=[HBM_SPEC] * (2 * n) + [SEM_SPEC, SEM_SPEC, pl.BlockSpec(memory_space=pl.ANY)],
        out_specs=tuple([HBM_SPEC] * (2 * n)), input_output_aliases={i: i for i in range(2 * n)},
        compiler_params=pltpu.CompilerParams(has_side_effects=DATAFLOW),
    )(*srcs, *lands, send_sems, recv_sems, after)
    return list(res[:n]), list(res[n:])


def _gather_cols(stack):
    p, k, n = stack.shape
    return stack.transpose(1, 0, 2).reshape(k, p * n)


def _scatter_cols(full):
    k, n = full.shape
    return full.reshape(k, N_DEV, n // N_DEV).transpose(1, 0, 2)


def _gather_rows(stack):
    p, r, n = stack.shape
    return stack.reshape(p * r, n)


def _scatter_rows(full):
    r, n = full.shape
    return full.reshape(N_DEV, r // N_DEV, n)


_IN_NAT = Q_LORA + KV_LORA
TRANSPOSED = ("w_in", "w_q_b", "w_up")
ROWS_APART = ("w_in", "w_conv")


def to_kernel_layout(name, w):
    if name == "w_in":
        z = lambda n: jnp.zeros((n, w.shape[1]), w.dtype)
        return jnp.concatenate([w[:_IN_NAT], z(KPE_LO), w[_IN_NAT:_IN_NAT + ROPE], z(LANES - KPE_LO - ROPE), w[_IN_NAT + ROPE:]], axis=0)
    if name == "w_q_b":
        return jnp.pad(w.reshape(HEADS, NOPE + ROPE, -1), ((0, 0), (0, LANES - NOPE - ROPE), (0, 0))).reshape(HEADS * LANES, -1)
    if name == "w_o":
        mla = jnp.pad(w[:HEADS * NOPE].reshape(HEADS, NOPE, -1), ((0, 0), (LANES - NOPE, 0), (0, 0))).reshape(HEADS * LANES, -1)
        return jnp.concatenate([mla, w[HEADS * NOPE:]], axis=0)
    return w


def from_kernel_layout(name, g):
    if name == "w_in":
        return jnp.concatenate([g[:_IN_NAT], g[P_KPE + KPE_LO:P_KPE + KPE_LO + ROPE], g[P_QD:]], axis=0)
    if name == "w_q_b":
        return g.reshape(HEADS, LANES, -1)[:, :NOPE + ROPE, :].reshape(HEADS * (NOPE + ROPE), -1)
    if name == "w_o":
        mla = g[:HEADS * LANES].reshape(HEADS, LANES, -1)[:, LANES - NOPE:, :].reshape(HEADS * NOPE, -1)
        return jnp.concatenate([mla, g[HEADS * LANES:]], axis=0)
    return g


SMALL_COLS = 1024
SMALL_ROWS = 24
SMALL_AT = {"loss": (0, 0, 1), "b_ada": (1, 0, 6 * D_MODEL), "g_mix_norm": (7, 0, D_MODEL), "g_q_lat": (8, 0, Q_LORA),
            "g_kv_lat": (9, 0, KV_LORA), "g_mla_q_nope": (10, 0, NOPE), "g_mla_q_pe": (10, 128, ROPE),
            "g_mla_k_nope": (10, 256, NOPE), "g_mla_k_pe": (10, 384, ROPE), "g_dil_q": (10, 512, DIL_DIM),
            "g_dil_k": (10, 640, DIL_DIM), "g_ffn_norm": (11, 0, D_MODEL), "b_conv": (12, 0, 2 * D_FF)}
SMALL_PARAMS = tuple(n for n in SMALL_AT if n != "loss")


def _pack_small(values):
    by_row = {}
    for name, (row, off, n) in SMALL_AT.items():
        by_row.setdefault(row, []).append((off, values[name].reshape(-1).astype(F32)))
    out = []
    for row in sorted(by_row):
        pieces, at = [], 0
        for off, v in sorted(by_row[row], key=lambda t: t[0]):
            pieces += [jnp.zeros((off - at,), F32), v]
            at = off + v.shape[0]
        flat = jnp.concatenate(pieces)
        nrows = -(-flat.shape[0] // SMALL_COLS)
        out.append(jnp.pad(flat, (0, nrows * SMALL_COLS - flat.shape[0])).reshape(nrows, SMALL_COLS))
    packed = jnp.concatenate(out, axis=0)
    return jnp.pad(packed, ((0, SMALL_ROWS - packed.shape[0]), (0, 0)))


def _adam(w, g, m, v):
    c1 = 1.0 - ADAM_B1 ** ADAM_STEP
    c2 = 1.0 - ADAM_B2 ** ADAM_STEP
    m2 = ADAM_B1 * m + (1.0 - ADAM_B1) * g
    v2 = ADAM_B2 * v + (1.0 - ADAM_B2) * (g * g)
    return -ADAM_LR * ((m2 / c1) / (jnp.sqrt(v2 / c2) + ADAM_EPS) + ADAM_WD * w), m2, v2


def adamw_small(name, stack, params):
    flat = [a for n in SMALL_PARAMS for a in params[n]]

    def body(stack_ref, *refs):
        ins, outs = refs[:len(flat)], refs[len(flat):]
        g_all = stack_ref[0]
        for p in range(1, N_DEV):
            g_all = g_all + stack_ref[p]
        outs[0][...] = g_all[0:1, 0:1]
        for i, pname in enumerate(SMALL_PARAMS):
            row, off, n = SMALL_AT[pname]
            w_ref, m_ref, v_ref = ins[3 * i:3 * i + 3]
            go_ref, d_ref, mo_ref, vo_ref = outs[1 + 4 * i:5 + 4 * i]
            for c0 in range(0, n, SMALL_COLS):
                cn = min(SMALL_COLS, n - c0)
                r = row + c0 // SMALL_COLS
                g = g_all[r:r + 1, off:off + cn]
                cols = (slice(None), slice(c0, c0 + cn))
                d, m2, v2 = _adam(w_ref[cols], g, m_ref[cols], v_ref[cols])
                go_ref[cols], d_ref[cols], mo_ref[cols], vo_ref[cols] = g, d, m2, v2

    whole = lambda a: pl.BlockSpec(a.shape, lambda: (0,) * a.ndim)
    out_shape = [jax.ShapeDtypeStruct((1, 1), F32)] + [jax.ShapeDtypeStruct(a.shape, F32) for n in SMALL_PARAMS for a in params[n][:1] * 4]
    res = pl.pallas_call(body, name=name, in_specs=[whole(stack)] + [whole(a) for a in flat],
                         out_specs=[pl.BlockSpec(s.shape, lambda s=s: (0,) * len(s.shape)) for s in out_shape],
                         out_shape=out_shape, compiler_params=_params())(stack, *flat)
    return res[0], {n: res[1 + 4 * i:5 + 4 * i] for i, n in enumerate(SMALL_PARAMS)}


def _local_step(x, pos, mod, target, w, fetch, emit, halfway=lambda after: None):
    S = SEQ
    sh1, sc1, g1, sh2, sc2, g2 = [mod[:, i * D_MODEL:(i + 1) * D_MODEL] for i in range(6)]
    zeros = lambda n: jnp.zeros((1, n), F32)
    g_q = jnp.concatenate([w["g_mla_q_nope"], w["g_mla_q_pe"], zeros(LANES - NOPE - ROPE)], axis=1)
    g_k = jnp.concatenate([w["g_mla_k_nope"], zeros(LANES - NOPE)], axis=1)
    g_kpe = jnp.concatenate([zeros(KPE_LO), w["g_mla_k_pe"], zeros(LANES - KPE_LO - ROPE)], axis=1)
    g_dq = jnp.concatenate([w["g_dil_q"]] * 2, axis=1)
    g_dk = jnp.concatenate([w["g_dil_k"]] * 2, axis=1)
    b_conv = w["b_conv"]

    def inv_freq(d):
        return jnp.power(ROPE_THETA, -2.0 * jnp.arange(d // 2, dtype=F32) / d)

    n_m, n_d = ROPE // 2, DIL_DIM // 2
    freqs = jnp.concatenate([inv_freq(ROPE), inv_freq(DIL_DIM), jnp.zeros((LANES - n_m - n_d,), F32)]).reshape(1, LANES)

    def tables_fn(rows, params):
        (p,), (f,) = rows, params
        c, s = jnp.cos(p * f), jnp.sin(p * f)
        one, zero = jnp.ones_like(c), jnp.zeros_like(c)
        mla = lambda t, fill: jnp.concatenate([fill[:, :KPE_LO], t[:, :n_m], t[:, :n_m], fill[:, :LANES - KPE_LO - ROPE]], axis=1)
        dil = lambda t: jnp.concatenate([t[:, n_m:n_m + n_d]] * 4, axis=1)
        return [mla(c, one), mla(s, zero), dil(c), dil(s)], []

    cos_m, sin_m, cos_d, sin_d = rowwise("rope_tables", tables_fn, [pos], [freqs], [(LANES, F32)] * 4)
    tables = [cos_m, sin_m, cos_d, sin_d]
    H_M, H_D = ROPE // 2, DIL_DIM // 2

    def ln1_fn(rows, params):
        (xv,), (g, sc, sh) = rows, params
        y, _, _ = _rms(xv, g)
        return [y * (1.0 + sc) + sh], []

    (h,) = rowwise("ln1_fwd", ln1_fn, [x], [w["g_mix_norm"], sc1, sh1], [(D_MODEL, MXU_DTYPE)], dep=sin_d)
    w_in = fetch("w_in", h)

    def proj_fn(rows, params):
        (hv, cm, sm, cd, sd), (w_t, gq, gkv, gkp, gdq, gdk) = rows, params
        pv = _dot(hv, w_t, "nt")
        kper = _rope(_grms(pv[:, P_KPE:P_QD], gkp, KPE_GROUPS)[0], cm, sm, H_M)
        qd = [_rope(_grms(c, gdq, DIL_GROUPS)[0], cd, sd, H_D) for c in _chunks(pv[:, P_QD:P_KD])]
        kd = [_rope(_grms(c, gdk, DIL_GROUPS)[0], cd, sd, H_D) for c in _chunks(pv[:, P_KD:P_VD])]
        return [pv, _rms(pv[:, P_QLAT:P_KVLAT], gq)[0], _rms(pv[:, P_KVLAT:P_KPE], gkv)[0], kper,
                jnp.concatenate(qd, axis=1), jnp.concatenate(kd, axis=1)], []

    post_params = [w["g_q_lat"], w["g_kv_lat"], g_kpe, g_dq, g_dk]
    proj, qln, kvn, kper, qd_r, kd_r = rowwise(
        "proj_fwd", proj_fn, [h] + tables, [w_in] + post_params,
        [(P_END, F32), (Q_LORA, MXU_DTYPE), (KV_LORA, MXU_DTYPE), (LANES, MXU_DTYPE)] + [(DIL_WIDTH, F32)] * 2, tm=256)
    w_q_b, w_kv_b = fetch("w_q_b", qln), fetch("w_kv_b", kvn)

    def mla_proj_fn(rows, params):
        (qlv, kvlv, kp, cm, sm), (wq_t, wkv, gq, gk) = rows, params
        qv, kvv = _dot(qlv, wq_t, "nt"), _dot(kvlv, wkv)
        value_lanes = _lane(kp.shape) >= NOPE
        qs, ks, vs = [], [], []
        for qc, kc in zip(_chunks(qv), _chunks(kvv), strict=True):
            qs.append(_rope(_grms(qc, gq, Q_GROUPS)[0], cm, sm, H_M))
            ks.append(_grms(kc, gk, K_GROUPS)[0] + kp)
            vs.append(jnp.where(value_lanes, kc, 0.0))
        return [qv, kvv] + [jnp.concatenate(t, axis=1) for t in (qs, ks, vs)], []

    q, kv, q_mla, k_mla, v_mla = rowwise(
        "mla_proj", mla_proj_fn, [qln, kvn, kper, cos_m, sin_m], [w_q_b, w_kv_b, g_q, g_k],
        [(HEADS * LANES, F32)] * 2 + [(HEADS * LANES, MXU_DTYPE)] * 3, tm=256)
    mla_scale = (NOPE + ROPE) ** -0.5
    o_cat, lse_mla = mla_fwd("mla_fwd", q_mla, k_mla, v_mla, mla_scale)
    passed = halfway(lse_mla)

    band = [band_fwd(f"band{dil}_fwd", qd_r, kd_r, proj, dil, dep=passed) for dil in DILATIONS]
    o_cat, lse_mix = combine_fwd("dil_combine", [b[0] for b in band], [b[1] for b in band], o_cat)
    w_o = fetch("w_o", o_cat)

    def mid_fn(rows, params):
        (ov, xv), (w_out, gate1, g, sc, sh) = rows, params
        mx = _dot(ov, w_out)
        x1 = xv + gate1 * mx
        y, _, _ = _rms(x1, g)
        return [mx, x1, y * (1.0 + sc) + sh], []

    mix, x1, h2 = rowwise("mix_fwd", mid_fn, [o_cat, x], [w_o, g1, w["g_ffn_norm"], sc2, sh2],
                          [(D_MODEL, F32), (D_MODEL, F32), (D_MODEL, MXU_DTYPE)], tm=256)
    w_up, w_conv, w_down = fetch("w_up", h2), fetch("w_conv", h2), fetch("w_down", h2)
    dn, up = ffn_fwd("ffn_fwd", h2, w_up, w_conv, b_conv, w_down)

    def final_fn(rows, params):
        (x1v, dnv, tgt), (gate2,) = rows, params
        r = x1v + gate2 * dnv - tgt
        dy = r * (1.0 / D_MODEL)
        loss = jnp.sum(_colsum(r * r), axis=-1, keepdims=True) * (0.5 / D_MODEL)
        return [dy, gate2 * dy], [loss, _colsum(dy * dnv)]

    dy, d_dn, loss, dg2 = rowwise("loss_head", final_fn, [x1, dn, target], [g2], [(D_MODEL, F32), (D_MODEL, MXU_DTYPE)],
                                  [1, D_MODEL])
    dh2, g_up, g_down, g_w_conv, g_b_conv = ffn_bwd("ffn_bwd", h2, up, w_up, w_conv, b_conv, d_dn, w_down)
    emit("w_down", g_down)
    emit("w_conv", g_w_conv)
    sent = emit("w_up", g_up)

    def mid_bwd_fn(rows, params):
        (dh2v, dyv, x1v, mx), (gate1, g, sc) = rows, params
        yn, n, rstd = _rms(x1v, g)
        dx_n, dg = _rms_bwd(dh2v * (1.0 + sc), n, rstd, g)
        dx1 = dyv + dx_n
        return [dx1, gate1 * dx1], [dg, _colsum(dh2v * yn), _colsum(dh2v), _colsum(dx1 * mx)]

    dx1, dmix, dg_ffn, dsc2, dsh2, dg1 = rowwise(
        "mid_bwd", mid_bwd_fn, [dh2, dy, x1, mix], [g1, w["g_ffn_norm"], sc2], [(D_MODEL, F32), (D_MODEL, MXU_DTYPE)],
        [D_MODEL] * 4, dep=sent)

    sent = emit("w_o", matmul("mix_wgrad", o_cat, dmix, "tn", tm=512, out_dtype=MXU_DTYPE))
    do_cat = matmul("mix_dgrad", dmix, w_o, "nt", tm=512, dep=sent)
    dband = None
    for dil, b in zip(DILATIONS, band):
        dband = band_bwd(f"band{dil}_bwd", qd_r, kd_r, proj, b[1], lse_mix, o_cat, do_cat, dil, before=dband)
    dq_mla, dkv_mla, dkper = mla_bwd("mla_bwd", q_mla, k_mla, v_mla, o_cat, do_cat, lse_mla, mla_scale)

    def mla_prep_bwd_fn(rows, params):
        (dqv, dkvv, qv, kvv, cm, sm), (gq, gk) = rows, params
        nope_lanes = _lane(cm.shape) < NOPE
        dqs, dkvs, dgq, dgk = [], [], 0.0, 0.0
        for dqc, dkc, qc, kc in zip(_chunks(dqv), _chunks(dkvv), _chunks(qv), _chunks(kvv), strict=True):
            _, n, rstd = _grms(qc, gq, Q_GROUPS)
            dx, dg = _grms_bwd(_rope_bwd(dqc, cm, sm, H_M), n, rstd, gq, Q_GROUPS)
            dqs.append(dx)
            dgq = dgq + dg
            _, n, rstd = _grms(kc, gk, K_GROUPS)
            dx, dg = _grms_bwd(dkc, n, rstd, gk, K_GROUPS)
            dkvs.append(jnp.where(nope_lanes, dx, dkc))
            dgk = dgk + dg
        return [jnp.concatenate(dqs, axis=1), jnp.concatenate(dkvs, axis=1)], [dgq, dgk]

    dq, dkv, dg_q, dg_k = rowwise("mla_prep_bwd", mla_prep_bwd_fn, [dq_mla, dkv_mla, q, kv, cos_m, sin_m], [g_q, g_k],
                                  [(HEADS * LANES, MXU_DTYPE)] * 2, [LANES, LANES], tm=256)
    emit("w_q_b", matmul("q_wgrad", dq, qln, "tn", out_dtype=MXU_DTYPE))
    emit("w_kv_b", matmul("kv_wgrad", kvn, dkv, "tn", out_dtype=MXU_DTYPE))

    def pre_bwd_fn(rows, params):
        dqv, dkvv, dkp, dqd_, dkd_, dvd_, pv, cm, sm, cd, sd = rows
        wq_t, wkv, gq, gkv, gkp, gdq, gdk = params
        dql, dkvl = _dot(dqv, wq_t), _dot(dkvv, wkv, "nt")
        r_q = _norm_bwd(dql, pv[:, P_QLAT:P_KVLAT], gq)
        r_kv = _norm_bwd(dkvl, pv[:, P_KVLAT:P_KPE], gkv)
        _, n, rstd = _grms(pv[:, P_KPE:P_QD], gkp, KPE_GROUPS)
        r_kp = _grms_bwd(_rope_bwd(dkp, cm, sm, H_M), n, rstd, gkp, KPE_GROUPS)
        outs, dgs = [r_q[0], r_kv[0], r_kp[0]], []
        for dval, lo, g in ((dqd_, P_QD, gdq), (dkd_, P_KD, gdk)):
            dg_sum = 0.0
            for dc, xc in zip(_chunks(dval), _chunks(pv[:, lo:lo + DIL_WIDTH]), strict=True):
                _, n, rstd = _grms(xc, g, DIL_GROUPS)
                dx, dg = _grms_bwd(_rope_bwd(dc, cd, sd, H_D), n, rstd, g, DIL_GROUPS)
                outs.append(dx)
                dg_sum = dg_sum + dg
            dgs.append(dg_sum)
        return [jnp.concatenate(outs + [dvd_], axis=1)], [r_q[1], r_kv[1], r_kp[1]] + dgs

    dproj, dg_q_lat, dg_kv_lat, dg_kpe, dg_dq, dg_dk = rowwise(
        "proj_pre_bwd", pre_bwd_fn,
        [dq, dkv, dkper] + list(dband) + [proj] + tables, [w_q_b, w_kv_b] + post_params,
        [(P_END, MXU_DTYPE)], [Q_LORA, KV_LORA, LANES, LANES, LANES], tm=256)
    sent = emit("w_in", matmul("proj_wgrad", dproj, h, "tn", tn=512, out_dtype=MXU_DTYPE))

    def ln1_bwd_fn(rows, params):
        (dpv, dres, xv), (w_t, g, sc) = rows, params
        dhv = _dot(dpv, w_t)
        yn, n, rstd = _rms(xv, g)
        dx_n, dg = _rms_bwd(dhv * (1.0 + sc), n, rstd, g)
        return [dres + dx_n], [dg, _colsum(dhv * yn), _colsum(dhv)]

    grad_x, dg_mix, dsc1, dsh1 = rowwise("proj_dgrad", ln1_bwd_fn, [dproj, dx1, x], [w_in, w["g_mix_norm"], sc1],
                                         [(D_MODEL, F32)], [D_MODEL] * 3, tm=256, dep=sent)
    dmod = jnp.concatenate([dsh1, dsc1, dg1, dsh2, dsc2, dg2], axis=-1)
    small = {"loss": loss, "b_ada": dmod, "g_mix_norm": dg_mix, "g_q_lat": dg_q_lat, "g_kv_lat": dg_kv_lat,
             "g_mla_q_nope": dg_q[:, :NOPE], "g_mla_q_pe": dg_q[:, NOPE:NOPE + ROPE], "g_mla_k_nope": dg_k[:, :NOPE],
             "g_mla_k_pe": dg_kpe[:, KPE_LO:KPE_LO + ROPE], "g_dil_q": dg_dq[:, :DIL_DIM] + dg_dq[:, DIL_DIM:],
             "g_dil_k": dg_dk[:, :DIL_DIM] + dg_dk[:, DIL_DIM:], "g_ffn_norm": dg_ffn,
             "b_conv": g_b_conv}
    return grad_x, small


COL_SHARDED = ("w_kv_b", "w_conv")
ROW_SHARDED = ("w_o", "w_down") + TRANSPOSED
ADAM_TILE = {"w_ada": 256, "w_up": 176, "w_down": 176}
GATHER_GROUPS = (("w_in",), ("w_q_b", "w_kv_b"), ("w_o",), ("w_up", "w_conv", "w_down"))
START_STAGES = ((0, 1), (2, 3))
FORWARD_STAGES = ((0, 1), (2,), (3,))
FORWARD_WITH = {"w_o": 2}
SCATTER_GROUPS = (("w_down", "w_conv", "w_up"), ("w_o",), ("w_q_b", "w_kv_b", "w_in"))
OUT_WEIGHTS = ("w_ada", "b_ada", "g_mix_norm", "w_in", "g_q_lat", "w_q_b", "g_kv_lat", "w_kv_b", "g_mla_q_nope", "g_mla_q_pe",
               "g_mla_k_nope", "g_mla_k_pe", "g_dil_q", "g_dil_k", "w_o", "g_ffn_norm", "w_up", "w_conv", "b_conv", "w_down")


def kernel(x, c, positions, w_ada, b_ada, g_mix_norm, w_in, g_q_lat, w_q_b, g_kv_lat, w_kv_b, g_mla_q_nope, g_mla_q_pe, g_mla_k_nope, g_mla_k_pe, g_dil_q, g_dil_k, w_o, g_ffn_norm, w_up, w_conv, b_conv, w_down, loss_target, m_w_ada, m_b_ada, m_g_mix_norm, m_w_in, m_g_q_lat, m_w_q_b, m_g_kv_lat, m_w_kv_b, m_g_mla_q_nope, m_g_mla_q_pe, m_g_mla_k_nope, m_g_mla_k_pe, m_g_dil_q, m_g_dil_k, m_w_o, m_g_ffn_norm, m_w_up, m_w_conv, m_b_conv, m_w_down, v_w_ada, v_b_ada, v_g_mix_norm, v_w_in, v_g_q_lat, v_w_q_b, v_g_kv_lat, v_w_kv_b, v_g_mla_q_nope, v_g_mla_q_pe, v_g_mla_k_nope, v_g_mla_k_pe, v_g_dil_q, v_g_dil_k, v_w_o, v_g_ffn_norm, v_w_up, v_w_conv, v_b_conv, v_w_down):
    args = dict(locals())
    xi, yi, ci = _place()
    me = 4 * xi + 2 * yi + ci
    def local(prefix, n):
        a = args[prefix + n]
        if n in ROWS_APART:
            return jnp.transpose(a, (2, 0, 1) if n in TRANSPOSED else (1, 0, 2))
        return a[0].T if n in TRANSPOSED else a[0]

    def as_output(n, r):
        if n in ROWS_APART:
            return jnp.transpose(r, (1, 2, 0) if n in TRANSPOSED else (1, 0, 2))
        return (r.T if n in TRANSPOSED else r)[None]

    shard = {n: local("", n) for n in COL_SHARDED + ROW_SHARDED + ("w_ada",)}
    flat = lambda n, a: a.reshape(a.shape[0], a.shape[-1]) if n in ROWS_APART else a
    small_w = {n: args[n] for n in SMALL_PARAMS}

    payload = {n: flat(n, shard[n]) if n == "w_conv" else flat(n, shard[n]).astype(MXU_DTYPE) for n in COL_SHARDED + ROW_SHARDED}
    start_order = [[n for i in groups for n in GATHER_GROUPS[i]] for groups in START_STAGES]

    sc_all, mod_all = ada_modulation("ada_mod", c, shard["w_ada"], after=[payload[n] for n in start_order[0]])

    exchange_of = lambda i: [e for e, groups in enumerate(START_STAGES) if i in groups][0]
    start_stage = lambda e, after: exchange_start(f"gather_start{e}", [payload[n] for n in start_order[e]], gather=True,
                                                  after=after, tree=True)
    gathered = {0: start_stage(0, mod_all)}
    after_start = gathered[0][-1]
    full, forwarded = {}, set()

    def forward(stage, after):
        e = exchange_of(FORWARD_STAGES[stage][0])
        if stage not in forwarded:
            forwarded.add(stage)
            first = start_order[e].index(GATHER_GROUPS[FORWARD_STAGES[stage][0]][0])
            count = sum(len(GATHER_GROUPS[i]) for i in FORWARD_STAGES[stage])
            starts_next = e + 1 < len(START_STAGES) and e + 1 not in gathered
            ready = [payload[n] for n in start_order[e + 1]] if starts_next else []
            gathered[e] = exchange_forward(f"gather_forward{stage}", gathered[e], [after] + ready, first, count)
            if starts_next:
                gathered[e + 1] = start_stage(e + 1, gathered[e][-1])
        return gathered[e][-1]

    def fetch(name, after):
        if name not in full:
            (i, grp), = [(i, grp) for i, grp in enumerate(GATHER_GROUPS) if name in grp]
            (stage,) = [s for s, groups in enumerate(FORWARD_STAGES) if i in groups]
            forward(stage, after)
            if name in FORWARD_WITH:
                forward(FORWARD_WITH[name], after)
            e = exchange_of(i)
            behind = gathered[e + 1][-1] if e + 1 in gathered else after
            srcs, lands = exchange_wait(f"gather{i}_wait", gathered[e], True, behind, start_order[e].index(grp[0]), len(grp), tree=True)
            for n, src, land in zip(grp, srcs, lands, strict=True):
                stack = lax.dynamic_update_index_in_dim(land, src, me, 0)
                full[n] = to_kernel_layout(n, _gather_cols(stack) if n in COL_SHARDED else _gather_rows(stack))
        return full[name]

    mod_row = lax.dynamic_index_in_dim(mod_all, me, axis=1, keepdims=False).reshape(1, 6 * D_MODEL)
    (mod,) = rowwise("ada_bias", lambda rows, params: ([rows[0] + rows[1]], []), [mod_row, b_ada], [], [(6 * D_MODEL, F32)],
                     dep=after_start)

    own, pending, scatters = {}, {}, {}

    def emit(name, grad):
        grad = from_kernel_layout(name, grad)
        parts = _scatter_cols(grad) if name in COL_SHARDED else _scatter_rows(grad)
        own[name] = lax.dynamic_index_in_dim(parts, me, 0, keepdims=False)
        pending[name] = parts
        for i, grp in enumerate(SCATTER_GROUPS):
            if name == grp[-1]:
                scatters[i] = exchange_start(f"scatter{i}_start", [pending[n] for n in grp], gather=False)
                return scatters[i][-1]
        return None

    pos = positions.reshape(SEQ, 1).astype(F32)
    grad_x, small = _local_step(x[0], pos, mod, loss_target[0], small_w, fetch, emit, halfway=lambda after: forward(1, after))

    small_sent = exchange_start("small_start", [_pack_small(small)], gather=True, after=grad_x)

    res, done = {}, small_sent[-1]
    for i, grp in enumerate(SCATTER_GROUPS):
        _, lands = exchange_wait(f"scatter{i}_wait", scatters[i], False, done)
        for n, land in zip(grp, lands, strict=True):
            res[n] = adamw(f"adamw_{n}", shard[n], [own[n], land], local("m_", n), local("v_", n), ADAM_TILE.get(n))
            done = res[n][0]
            res[n] = [as_output(n, r) for r in res[n]]
    (packed,), (landed,) = exchange_wait("small_wait", small_sent, True, done)
    small_all = lax.dynamic_update_index_in_dim(landed, packed, me, 0)
    loss, small_res = adamw_small("adamw_small", small_all, {n: (args[n], args["m_" + n], args["v_" + n]) for n in SMALL_PARAMS})
    row, _, n_mod = SMALL_AT["b_ada"]
    dmod_all = small_all[:, row:row + n_mod // SMALL_COLS, :].reshape(N_DEV, n_mod)
    dmod_mine = lax.dynamic_slice_in_dim(dmod_all, me * (6 * D_MODEL // N_DEV), 6 * D_MODEL // N_DEV, axis=1)
    g_w_ada = matmul("ada_wgrad", sc_all, dmod_mine, "tn")
    res["w_ada"] = [r[None] for r in adamw("adamw_w_ada", shard["w_ada"], [g_w_ada], m_w_ada[0], v_w_ada[0], ADAM_TILE["w_ada"])]

    def leaf(kind, n):
        return res[n][kind] if n in res else small_res[n][kind]

    return (loss.reshape(()), grad_x[None], *[leaf(k, n) for k in range(4) for n in OUT_WEIGHTS])
```

```python
import jax
import jax.numpy as jnp
from jax import lax
from jax.experimental import pallas as pl
from jax.experimental.pallas import tpu as pltpu

F32 = jnp.float32
MXU_DTYPE = jnp.bfloat16

N_DEV = 8
D_MODEL = 1024
SEQ = 2048
HEADS = 8
NOPE = 64
ROPE = 32
Q_LORA = 512
KV_LORA = 256
DIL_DIM = 64
DIL_WIDTH = HEADS * DIL_DIM
DILATIONS = (1, 4, 16)
SPAN = 128
D_FF = 2816
LANES = 128
SUBLANES = 8
ROPE_THETA = 10000.0
EPS = 1e-6
NEG_INF = -1e30
ADAM_LR, ADAM_B1, ADAM_B2, ADAM_EPS, ADAM_WD, ADAM_STEP = 0.001, 0.9, 0.999, 1e-08, 0.01, 10
VMEM_LIMIT = 56 * 1024 * 1024
MESH_ID = pl.DeviceIdType.MESH

P_QLAT, P_KVLAT, P_KPE, P_QD, P_KD, P_VD, P_END = 0, 512, 768, 896, 1408, 1920, 2432
KPE_LO = 64
MIX_IN = HEADS * LANES + DIL_WIDTH


def _params(**kw):
    return pltpu.CompilerParams(vmem_limit_bytes=VMEM_LIMIT, **kw)


def rowwise(name, fn, rows, params, out_rows, out_accs=(), tm=512, dep=None):
    deps = [] if dep is None else [dep]
    rows = [r if isinstance(r, tuple) else (r, r.shape[1], 0) for r in rows]
    R = rows[0][0].shape[0]
    tm = min(tm, R)
    steps = R // tm
    assert steps * tm == R
    in_specs = []
    for a, width, cb in rows:
        ri = a.shape[0]
        per = ri // tm
        assert per * tm == ri
        if ri == R:
            in_specs.append(pl.BlockSpec((tm, width), lambda i, cb=cb: (i, cb)))
        else:
            in_specs.append(pl.BlockSpec((tm, width), lambda i, per=per, cb=cb: (i % per, cb)))
    for p in params:
        in_specs.append(pl.BlockSpec(p.shape, lambda i: (0,) * p.ndim))
    in_specs += [pl.BlockSpec(memory_space=pl.ANY)] * len(deps)
    out_shape = [jax.ShapeDtypeStruct((R, d), dt) for d, dt in out_rows]
    out_specs = [pl.BlockSpec((tm, d), lambda i: (i, 0)) for d, _ in out_rows]
    out_shape += [jax.ShapeDtypeStruct((1, n), F32) for n in out_accs]
    out_specs += [pl.BlockSpec((1, n), lambda i: (0, 0)) for n in out_accs]
    nr, npar, no, na = len(rows), len(params), len(out_rows), len(out_accs)

    def body(*refs):
        rvals = [r[...] for r in refs[:nr]]
        pvals = [r[...] for r in refs[nr:nr + npar]]
        outs, accs = fn(rvals, pvals)
        first_out = nr + npar + len(deps)
        for ref, v in zip(refs[first_out:first_out + no], outs, strict=True):
            ref[...] = v.astype(ref.dtype)
        if na:
            acc_refs = refs[first_out + no:]
            i = pl.program_id(0)

            @pl.when(i == 0)
            def _():
                for ref, v in zip(acc_refs, accs, strict=True):
                    ref[...] = v

            @pl.when(i > 0)
            def _():
                for ref, v in zip(acc_refs, accs, strict=True):
                    ref[...] += v

    res = pl.pallas_call(body, name=name, grid=(steps,), in_specs=in_specs, out_specs=out_specs,
                         out_shape=out_shape, compiler_params=_params())(*[r[0] for r in rows], *params, *deps)
    return list(res)


_DIMS = {"nn": ((1,), (0,)), "nt": ((1,), (1,)), "tn": ((0,), (0,))}


def _dot(a, b, mode="nn"):
    return lax.dot_general(a.astype(MXU_DTYPE), b.astype(MXU_DTYPE), (_DIMS[mode], ((), ())),
                           preferred_element_type=F32)


def matmul(name, a, b, mode, tm=None, tn=None, tk=None, out_dtype=F32, dep=None):
    if mode == "tn":
        K, M = a.shape
    else:
        M, K = a.shape
    N = b.shape[0] if mode == "nt" else b.shape[1]
    tm, tn, tk = tm or M, tn or N, tk or K
    nm, nn, nk = M // tm, N // tn, K // tk
    assert nm * tm == M and nn * tn == N and nk * tk == K
    a_spec = pl.BlockSpec((tk, tm), lambda i, j, k: (k, i)) if mode == "tn" else pl.BlockSpec((tm, tk), lambda i, j, k: (i, k))
    b_spec = pl.BlockSpec((tn, tk), lambda i, j, k: (j, k)) if mode == "nt" else pl.BlockSpec((tk, tn), lambda i, j, k: (k, j))
    deps = [] if dep is None else [dep]

    def body(a_ref, b_ref, *rest):
        o_ref, scratch = rest[len(deps)], rest[len(deps) + 1:]
        p = _dot(a_ref[...], b_ref[...], mode)
        if nk == 1:
            o_ref[...] = p.astype(o_ref.dtype)
        else:
            acc = scratch[0]
            k = pl.program_id(2)

            @pl.when(k == 0)
            def _():
                acc[...] = p

            @pl.when(k > 0)
            def _():
                acc[...] += p

            @pl.when(k == nk - 1)
            def _():
                o_ref[...] = acc[...].astype(o_ref.dtype)

    return pl.pallas_call(
        body, name=name, grid=(nm, nn, nk), in_specs=[a_spec, b_spec] + [pl.BlockSpec(memory_space=pl.ANY)] * len(deps),
        out_specs=pl.BlockSpec((tm, tn), lambda i, j, k: (i, j)),
        out_shape=jax.ShapeDtypeStruct((M, N), out_dtype),
        scratch_shapes=[pltpu.VMEM((tm, tn), F32)] if nk > 1 else [],
        compiler_params=_params())(a, b, *deps)


def _rms(x, g):
    rstd = lax.rsqrt(jnp.mean(x * x, axis=-1, keepdims=True) + EPS)
    n = x * rstd
    return n * g, n, rstd


def _rms_bwd(dy, n, rstd, g):
    dg = jnp.sum(dy * n, axis=0, keepdims=True)
    dn = dy * g
    dx = rstd * (dn - n * jnp.mean(dn * n, axis=-1, keepdims=True))
    return dx, dg


def _norm_bwd(dy, x, g):
    _, n, rstd = _rms(x, g)
    return _rms_bwd(dy, n, rstd, g)


def _colsum(v):
    return jnp.sum(v, axis=0, keepdims=True)


def _silu(x):
    return x * (1.0 / (1.0 + jnp.exp(-x)))


def _lane(shape):
    return lax.broadcasted_iota(jnp.int32, shape, 1)


def _group_mean(v, groups):
    i = lax.broadcasted_iota(jnp.int32, (LANES, LANES), 0)
    j = lax.broadcasted_iota(jnp.int32, (LANES, LANES), 1)
    g = jnp.zeros((LANES, LANES), F32)
    for lo, hi in groups:
        g = jnp.where((i >= lo) & (i < hi) & (j >= lo) & (j < hi), 1.0 / (hi - lo), g)
    head = v.astype(MXU_DTYPE)
    return _dot(head, g) + _dot(v - head.astype(F32), g)


def _in_groups(shape, groups):
    lane = _lane(shape)
    m = jnp.zeros(shape, jnp.bool_)
    for lo, hi in groups:
        m = m | ((lane >= lo) & (lane < hi))
    return m


def _grms(x, g, groups):
    rstd = lax.rsqrt(_group_mean(x * x, groups) + EPS)
    n = jnp.where(_in_groups(x.shape, groups), x * rstd, 0.0)
    return n * g, n, rstd


def _grms_bwd(dy, n, rstd, g, groups):
    dn = dy * g
    return rstd * (dn - n * _group_mean(dn * n, groups)), _colsum(dy * n)


def _rot(x, half, transpose=False):
    first = (_lane(x.shape) % (2 * half)) < half
    up = pltpu.roll(x, LANES - half, axis=1)
    down = pltpu.roll(x, half, axis=1)
    return jnp.where(first, up, -down) if transpose else jnp.where(first, -up, down)


def _rope(x, cos, sin, half):
    return x * cos + _rot(x, half) * sin


def _rope_bwd(dy, cos, sin, half):
    return dy * cos + _rot(dy * sin, half, transpose=True)


def _chunks(x):
    return [x[:, i:i + LANES] for i in range(0, x.shape[1], LANES)]


Q_GROUPS = ((0, NOPE), (NOPE, NOPE + ROPE))
K_GROUPS = ((0, NOPE),)
KPE_GROUPS = ((KPE_LO, KPE_LO + ROPE),)
DIL_GROUPS = ((0, DIL_DIM), (DIL_DIM, 2 * DIL_DIM))


def _col(width, rows=SEQ):
    return pl.BlockSpec((rows, width), lambda h: (0, h))


def _causal_tail(s, tq, fill):
    diag = s[:, s.shape[1] - tq:]
    keep = lax.broadcasted_iota(jnp.int32, diag.shape, 1) <= lax.broadcasted_iota(jnp.int32, diag.shape, 0)
    diag = jnp.where(keep, diag, fill)
    return diag if s.shape[1] == tq else jnp.concatenate([s[:, :s.shape[1] - tq], diag], axis=1)


def mla_fwd(name, q, k, v, scale, tq=256):
    S = q.shape[0]

    def body(q_ref, k_ref, v_ref, o_ref, lse_ref):
        nb = S // tq
        blk = lambda i: slice(i * tq, (i + 1) * tq)

        def scores(i):
            return _dot(q_ref[blk(i), :], k_ref[:(i + 1) * tq, :], "nt")

        def softmax(i, s):
            s = _causal_tail(s * scale, tq, NEG_INF)
            m = jnp.max(s, axis=-1, keepdims=True)
            e = jnp.exp(s - m)
            l = jnp.sum(e, axis=-1, keepdims=True)
            lse_ref[0, blk(i), :] = m + jnp.log(l)
            return (e * (1.0 / l)).astype(MXU_DTYPE)

        def weighted(i, p):
            o_ref[blk(i), :] = _dot(p, v_ref[:(i + 1) * tq, :])

        s, p_prev = scores(0), None
        for i in range(nb):
            s_next = scores(i + 1) if i + 1 < nb else None
            if p_prev is not None:
                weighted(i - 1, p_prev)
            p_prev, s = softmax(i, s), s_next
        weighted(nb - 1, p_prev)

    return pl.pallas_call(
        body, name=name, grid=(HEADS,), in_specs=[_col(LANES)] * 3,
        out_specs=[_col(LANES), pl.BlockSpec((1, S, 1), lambda h: (h, 0, 0))],
        out_shape=[jax.ShapeDtypeStruct((S, MIX_IN), F32), jax.ShapeDtypeStruct((HEADS, S, 1), F32)],
        compiler_params=_params())(q, k, v)


def mla_bwd(name, q, k, v, o, do, lse, scale, tq=256):
    S = q.shape[0]

    def body(q_ref, k_ref, v_ref, o_ref, do_ref, lse_ref, dq_ref, dkv_ref, dkpe_ref, dk_acc, dv_acc):
        dk_acc[...] = jnp.zeros_like(dk_acc)
        dv_acc[...] = jnp.zeros_like(dv_acc)
        for i in range(S // tq):
            kext = (i + 1) * tq
            blk = slice(i * tq, kext)
            qi, kk, vv = q_ref[blk, :], k_ref[:kext, :], v_ref[:kext, :]
            doi = do_ref[blk, :]
            s = _causal_tail(_dot(qi, kk, "nt") * scale, tq, NEG_INF)
            p = jnp.exp(s - lse_ref[0, blk, :])
            dp = _dot(doi, vv, "nt")
            delta = jnp.sum(doi * o_ref[blk, :], axis=-1, keepdims=True)
            ds = p * (dp - delta) * scale
            dq_ref[blk, :] = _dot(ds, kk)
            dk_acc[:kext, :] += _dot(ds, qi, "tn")
            dv_acc[:kext, :] += _dot(p, doi, "tn")
        dk = dk_acc[...]
        lane = _lane(dk.shape)
        dkv_ref[...] = jnp.where(lane < NOPE, dk, 0.0) + dv_acc[...]
        dkpe = jnp.where((lane >= KPE_LO) & (lane < KPE_LO + ROPE), dk, 0.0)
        h = pl.program_id(0)

        @pl.when(h == 0)
        def _():
            dkpe_ref[...] = dkpe

        @pl.when(h > 0)
        def _():
            dkpe_ref[...] += dkpe

    return pl.pallas_call(
        body, name=name, grid=(HEADS,),
        in_specs=[_col(LANES)] * 5 + [pl.BlockSpec((1, S, 1), lambda h: (h, 0, 0))],
        out_specs=[_col(LANES), _col(LANES), pl.BlockSpec((S, LANES), lambda h: (0, 0))],
        out_shape=[jax.ShapeDtypeStruct((S, HEADS * LANES), F32), jax.ShapeDtypeStruct((S, HEADS * LANES), F32),
                   jax.ShapeDtypeStruct((S, LANES), F32)],
        scratch_shapes=[pltpu.VMEM((S, LANES), F32), pltpu.VMEM((S, LANES), F32)],
        compiler_params=_params())(q, k, v, o, do, lse)


BAND_TQ = SPAN


def _band_blocks(L, tq):
    return [(i * tq, (i + 1) * tq, max(0, i * tq - SPAN)) for i in range(L // tq)]


def _class_rows(r, dil, lo, hi):
    return pl.ds(r + dil * lo, hi - lo, stride=dil) if dil > 1 else pl.ds(lo, hi - lo)


def _stack_heads(t, lo):
    zero = jnp.zeros_like(t)
    return jnp.concatenate([jnp.where(lo, t, zero), jnp.where(lo, zero, t)], axis=0)


def _band_mask2(q0, q1, k0):
    n = q1 - q0
    shape = (2 * n, q1 - k0)
    i = lax.broadcasted_iota(jnp.int32, shape, 0)
    dist = (jnp.where(i >= n, i - n, i) + q0) - (lax.broadcasted_iota(jnp.int32, shape, 1) + k0)
    return (dist >= 0) & (dist <= SPAN)


def _pair_col(col0=0):
    return pl.BlockSpec((SEQ, LANES), lambda j: (0, col0 // LANES + j))


def band_fwd(name, q, k, v, dil, dep=None):
    S = q.shape[0]
    L = S // dil
    tq = BAND_TQ
    scale = DIL_DIM ** -0.5
    deps = [] if dep is None else [dep]

    def body(q_ref, k_ref, v_ref, *rest):
        o_ref, lse_ref = rest[len(deps):]
        items = [(r, blk) for r in range(dil) for blk in _band_blocks(L, tq)]
        lo = _lane((tq, LANES)) < DIL_DIM

        def scores(item):
            r, (q0, q1, k0) = item
            qb = q_ref[_class_rows(r, dil, q0, q1), :].astype(MXU_DTYPE)
            return _dot(_stack_heads(qb, lo), k_ref[_class_rows(r, dil, k0, q1), :], "nt")

        def softmax(item, s):
            _, (q0, q1, k0) = item
            s = jnp.where(_band_mask2(q0, q1, k0), s * scale, NEG_INF)
            mx = jnp.max(s, axis=-1, keepdims=True)
            e = jnp.exp(s - mx)
            l = jnp.sum(e, axis=-1, keepdims=True)
            return (e * (1.0 / l)).astype(MXU_DTYPE), mx + jnp.log(l)

        def weighted(item, p, lse):
            r, (q0, q1, k0) = item
            pv = _dot(p, v_ref[_class_rows(r, dil, k0, q1), :])
            o_ref[_class_rows(r, dil, q0, q1), :] = jnp.where(lo, pv[:tq], pv[tq:])
            lse_ref[_class_rows(r, dil, q0, q1), :] = jnp.where(lo, lse[:tq], lse[tq:])

        s, prev = scores(items[0]), None
        for i, item in enumerate(items):
            s_next = scores(items[i + 1]) if i + 1 < len(items) else None
            if prev is not None:
                weighted(items[i - 1], *prev)
            prev, s = softmax(item, s), s_next
        weighted(items[-1], *prev)

    return pl.pallas_call(
        body, name=name, grid=(DIL_WIDTH // LANES,),
        in_specs=[_pair_col()] * 2 + [_pair_col(P_VD)] + [pl.BlockSpec(memory_space=pl.ANY)] * len(deps), out_specs=[_pair_col()] * 2,
        out_shape=[jax.ShapeDtypeStruct((S, DIL_WIDTH), F32)] * 2, compiler_params=_params())(q, k, v, *deps)


def band_bwd(name, q, k, v, lse, lse_mix, o_cat, do_cat, dil, before=None):
    S = q.shape[0]
    L = S // dil
    tq = BAND_TQ
    scale = DIL_DIM ** -0.5
    before = list(before or [])

    def body(q_ref, k_ref, v_ref, lse_ref, mix_ref, o_ref, do_ref, *rest):
        dq_ref, dk_ref, dv_ref = rest[len(before):]
        if before:
            dq0_ref, dk0_ref, dv0_ref = rest[:3]
            dk_ref[...] = dk0_ref[...]
            dv_ref[...] = dv0_ref[...]
        else:
            dk_ref[...] = jnp.zeros_like(dk_ref)
            dv_ref[...] = jnp.zeros_like(dv_ref)
        items = [(r, blk) for r in range(dil) for blk in _band_blocks(L, tq)]
        lo = _lane((tq, LANES)) < DIL_DIM
        per_head = lambda t: jnp.concatenate([t[:, 0:1], t[:, DIL_DIM:DIL_DIM + 1]], axis=0)

        def scores(item):
            r, (q0, q1, k0) = item
            qrows, krows = _class_rows(r, dil, q0, q1), _class_rows(r, dil, k0, q1)
            lse_p, dout = lse_ref[qrows, :], do_ref[qrows, :]
            w2 = per_head(jnp.exp(lse_p - mix_ref[qrows, :]))
            dd = dout * o_ref[qrows, :]
            big_d = jnp.concatenate([jnp.sum(jnp.where(lo, dd, 0.0), axis=-1, keepdims=True),
                                     jnp.sum(jnp.where(lo, 0.0, dd), axis=-1, keepdims=True)], axis=0)
            q2 = _stack_heads(q_ref[qrows, :].astype(MXU_DTYPE), lo)
            dom = (_stack_heads(dout, lo) * w2).astype(MXU_DTYPE)
            return (_dot(q2, k_ref[krows, :], "nt"), _dot(dom, v_ref[krows, :], "nt"), per_head(lse_p), w2 * big_d, q2, dom)

        def softmax_bwd(item, s, dp, lse2, wd2, q2, dom):
            _, (q0, q1, k0) = item
            p = jnp.where(_band_mask2(q0, q1, k0), jnp.exp(s * scale - lse2), 0.0)
            return p.astype(MXU_DTYPE), (p * (dp - wd2) * scale).astype(MXU_DTYPE), q2, dom

        def grads(item, p, ds, q2, dom):
            r, (q0, q1, k0) = item
            qrows, krows = _class_rows(r, dil, q0, q1), _class_rows(r, dil, k0, q1)
            dq2 = _dot(ds, k_ref[krows, :])
            dq = jnp.where(lo, dq2[:tq], dq2[tq:])
            dq_ref[qrows, :] = dq + dq0_ref[qrows, :] if before else dq
            dk_ref[krows, :] += _dot(ds, q2, "tn")
            dv_ref[krows, :] += _dot(p, dom, "tn")

        sc, prev = scores(items[0]), None
        for i, item in enumerate(items):
            sc_next = scores(items[i + 1]) if i + 1 < len(items) else None
            if prev is not None:
                grads(items[i - 1], *prev)
            prev, sc = softmax_bwd(item, *sc), sc_next
        grads(items[-1], *prev)

    cat = _pair_col(HEADS * LANES)
    return pl.pallas_call(
        body, name=name, grid=(DIL_WIDTH // LANES,),
        in_specs=[_pair_col()] * 2 + [_pair_col(P_VD)] + [_pair_col()] * 2 + [cat] * 2 + [_pair_col()] * len(before),
        out_specs=[_pair_col()] * 3, out_shape=[jax.ShapeDtypeStruct((S, DIL_WIDTH), F32)] * 3,
        compiler_params=_params())(q, k, v, lse, lse_mix, o_cat, do_cat, *before)


def combine_fwd(name, outs, lses, o_cat, tm=512):
    S = outs[0].shape[0]

    def body(o1, o2, o3, l1, l2, l3, cat_in, cat_out, mix_ref):
        ls = [l1[...], l2[...], l3[...]]
        m = jnp.maximum(jnp.maximum(ls[0], ls[1]), ls[2])
        e = [jnp.exp(l - m) for l in ls]
        den = e[0] + e[1] + e[2]
        cat_out[...] = (e[0] / den) * o1[...] + (e[1] / den) * o2[...] + (e[2] / den) * o3[...]
        mix_ref[...] = m + jnp.log(den)

    row = pl.BlockSpec((tm, DIL_WIDTH), lambda i: (i, 0))
    return pl.pallas_call(
        body, name=name, grid=(S // tm,), in_specs=[row] * 6 + [pl.BlockSpec(memory_space=pl.ANY)],
        out_specs=[pl.BlockSpec((tm, DIL_WIDTH), lambda i: (i, HEADS * LANES // DIL_WIDTH)), row],
        out_shape=[jax.ShapeDtypeStruct(o_cat.shape, F32), jax.ShapeDtypeStruct((S, DIL_WIDTH), F32)],
        input_output_aliases={6: 0}, compiler_params=_params())(*outs, *lses, o_cat)


FFN_FWD_ROWS = 512
FFN_BWD_ROWS = 256
CONV_PAD = SUBLANES


def _window(x, k):
    groups = x.reshape(-1, SUBLANES, x.shape[1])
    turned = pltpu.roll(groups, SUBLANES - k, axis=1)
    stays = lax.broadcasted_iota(jnp.int32, (groups.shape[0] - 1,) + groups.shape[1:], 1) < SUBLANES - k
    return jnp.where(stays, turned[:-1], turned[1:]).reshape(-1, x.shape[1])


def _earlier(ref, r0, rows, n):
    if r0 == 0:
        x = jnp.concatenate([jnp.zeros((SUBLANES, ref.shape[1]), F32), ref[:rows, :]], axis=0)
    else:
        x = ref[r0 - SUBLANES:r0 + rows, :]
    return _window(x, SUBLANES - n)


CONV_TC = 256
CONV_NB = D_FF // CONV_TC


def _half_specs(rows, rows_axis=False):
    if rows_axis:
        return [pl.BlockSpec((rows, D_MODEL), lambda j: (j, 0)), pl.BlockSpec((rows, D_MODEL), lambda j: (j + CONV_NB, 0))]
    return [pl.BlockSpec((rows, CONV_TC), lambda j: (0, j)), pl.BlockSpec((rows, CONV_TC), lambda j: (0, j + CONV_NB))]


def _whole(a):
    return pl.BlockSpec(a.shape, lambda j: (0,) * a.ndim)


def _up_pair(h, ug_ref, uv_ref):
    return jnp.concatenate([_dot(h, ug_ref[...], "nt"), _dot(h, uv_ref[...], "nt")], axis=1)


def _conv_taps(up_ref, r0, rows, w, b):
    uin, u1, u2 = up_ref[r0:r0 + rows, :], _earlier(up_ref, r0, rows, 1), _earlier(up_ref, r0, rows, 2)
    return uin, u1, u2, w[2:3, :] * uin + w[1:2, :] * u1 + w[0:1, :] * u2 + b


def ffn_fwd(name, h, w_up_t, w_conv, b_conv, w_down):
    S = h.shape[0]

    def body(h_ref, ug_ref, uv_ref, wg_ref, wv_ref, bg_ref, bv_ref, wd_ref, dn_ref, up_ref):
        @pl.when(pl.program_id(0) == 0)
        def _():
            dn_ref[...] = jnp.zeros_like(dn_ref)

        w = jnp.concatenate([wg_ref[...], wv_ref[...]], axis=1)
        b = jnp.concatenate([bg_ref[...], bv_ref[...]], axis=1)
        rows = FFN_FWD_ROWS
        starts = list(range(0, S, rows))

        def project(r0):
            up_ref[r0:r0 + rows, :] = _up_pair(h_ref[r0:r0 + rows, :], ug_ref, uv_ref)

        def gate(r0):
            u = _conv_taps(up_ref, r0, rows, w, b)[3]
            return (_silu(u[:, :CONV_TC]) * u[:, CONV_TC:]).astype(MXU_DTYPE)

        def project_down(r0, act):
            dn_ref[r0:r0 + rows, :] += _dot(act, wd_ref[...])

        project(starts[0])
        act_prev = None
        for i, r0 in enumerate(starts):
            if i + 1 < len(starts):
                project(starts[i + 1])
            if act_prev is not None:
                project_down(starts[i - 1], act_prev)
            act_prev = gate(r0)
        project_down(starts[-1], act_prev)

    return pl.pallas_call(
        body, name=name, grid=(CONV_NB,),
        in_specs=[_whole(h)] + _half_specs(CONV_TC, rows_axis=True) + _half_specs(3) + _half_specs(1)
        + [pl.BlockSpec((CONV_TC, w_down.shape[1]), lambda j: (j, 0))],
        out_specs=[pl.BlockSpec((S, w_down.shape[1]), lambda j: (0, 0)), pl.BlockSpec((S, 2 * CONV_TC), lambda j: (0, j))],
        out_shape=[jax.ShapeDtypeStruct((S, w_down.shape[1]), F32), jax.ShapeDtypeStruct((S, 2 * D_FF), F32)],
        compiler_params=_params())(h, w_up_t, w_up_t, w_conv, w_conv, b_conv, b_conv, w_down)


def ffn_bwd(name, h, up, w_up_t, w_conv, b_conv, d_dn, w_down):
    S, D = h.shape

    def body(h_ref, up_ref, ug_ref, uv_ref, wg_ref, wv_ref, bg_ref, bv_ref, dd_ref, wd_ref,
             dh_ref, gup_ref, gd_ref, dwg_ref, dwv_ref, dbg_ref, dbv_ref, du_ref, dup_ref, act_ref):
        @pl.when(pl.program_id(0) == 0)
        def _():
            dh_ref[...] = jnp.zeros_like(dh_ref)

        w = jnp.concatenate([wg_ref[...], wv_ref[...]], axis=1)
        b = jnp.concatenate([bg_ref[...], bv_ref[...]], axis=1)
        w_pair = jnp.concatenate([ug_ref[...], uv_ref[...]], axis=0)
        rows = FFN_BWD_ROWS
        starts = list(range(0, S, rows))
        du_ref[S:S + CONV_PAD, :] = jnp.zeros((CONV_PAD, 2 * CONV_TC), F32)

        def project(r0):
            return _dot(dd_ref[r0:r0 + rows, :], wd_ref[...], "nt")

        def through_conv(r0, da):
            uin, u1, u2, u = _conv_taps(up_ref, r0, rows, w, b)
            gate, val = u[:, :CONV_TC], u[:, CONV_TC:]
            sig = 1.0 / (1.0 + jnp.exp(-gate))
            du = jnp.concatenate([da * val * (sig * (1.0 + gate * (1.0 - sig))), da * (gate * sig)], axis=1)
            du_ref[r0:r0 + rows, :] = du
            act_ref[r0:r0 + rows, :] = (gate * sig * val).astype(MXU_DTYPE)
            dw = jnp.concatenate([_colsum(du * u2), _colsum(du * u1), _colsum(du * uin)], axis=0)
            return dw, _colsum(du)

        def back_up(r0):
            du = du_ref[r0:r0 + rows + CONV_PAD, :]
            dup = (w[2:3, :] * du[:rows] + w[1:2, :] * _window(du, 1) + w[0:1, :] * _window(du, 2)).astype(MXU_DTYPE)
            dup_ref[r0:r0 + rows, :] = dup
            dh_ref[r0:r0 + rows, :] += _dot(dup, w_pair)

        dw, db = 0.0, 0.0
        da = project(starts[0])
        for i, r0 in enumerate(starts):
            da_next = project(starts[i + 1]) if i + 1 < len(starts) else None
            dw_c, db_c = through_conv(r0, da)
            if i > 0:
                back_up(starts[i - 1])
            dw, db, da = dw + dw_c, db + db_c, da_next
        back_up(starts[-1])
        g_up, g_dn = _dot(dup_ref[...], h_ref[...], "tn"), _dot(act_ref[...], dd_ref[...], "tn")
        gup_ref[0], gup_ref[1] = g_up[:CONV_TC].astype(gup_ref.dtype), g_up[CONV_TC:].astype(gup_ref.dtype)
        gd_ref[...] = g_dn.astype(gd_ref.dtype)
        dwg_ref[...], dwv_ref[...] = dw[:, :CONV_TC], dw[:, CONV_TC:]
        dbg_ref[...], dbv_ref[...] = db[:, :CONV_TC], db[:, CONV_TC:]

    half = lambda rows: pl.BlockSpec((rows, CONV_TC), lambda j: (0, j))
    rows_blk = pl.BlockSpec((CONV_TC, D), lambda j: (j, 0))
    dh, gup, gd, dwg, dwv, dbg, dbv = pl.pallas_call(
        body, name=name, grid=(CONV_NB,),
        in_specs=[_whole(h), pl.BlockSpec((S, 2 * CONV_TC), lambda j: (0, j))] + _half_specs(CONV_TC, rows_axis=True) + _half_specs(3)
        + _half_specs(1) + [_whole(d_dn), rows_blk],
        out_specs=[pl.BlockSpec((S, D), lambda j: (0, 0)), pl.BlockSpec((2, CONV_TC, D), lambda j: (0, j, 0)), rows_blk,
                   half(3), half(3), half(1), half(1)],
        out_shape=[jax.ShapeDtypeStruct((S, D), F32), jax.ShapeDtypeStruct((2, D_FF, D), MXU_DTYPE),
                   jax.ShapeDtypeStruct((D_FF, D), MXU_DTYPE)]
        + [jax.ShapeDtypeStruct((3, D_FF), F32)] * 2 + [jax.ShapeDtypeStruct((1, D_FF), F32)] * 2,
        scratch_shapes=[pltpu.VMEM((S + CONV_PAD, 2 * CONV_TC), F32), pltpu.VMEM((S, 2 * CONV_TC), MXU_DTYPE),
                        pltpu.VMEM((S, CONV_TC), MXU_DTYPE)],
        compiler_params=_params())(h, up, w_up_t, w_up_t, w_conv, w_conv, b_conv, b_conv, d_dn, w_down)
    return dh, gup.reshape(2 * D_FF, D), gd, jnp.concatenate([dwg, dwv], axis=1), jnp.concatenate([dbg, dbv], axis=1)


def adamw(name, w, parts, m, v, tr=None):
    apart = w.ndim == 3
    R, C = w.shape[0], w.shape[-1]
    tr = tr or R
    assert R % tr == 0
    c1 = 1.0 - ADAM_B1 ** ADAM_STEP
    c2 = 1.0 - ADAM_B2 ** ADAM_STEP
    np_ = len(parts)

    def body(*refs):
        w_ref, m_ref, v_ref = refs[0], refs[1 + np_], refs[2 + np_]
        go_ref, d_ref, mo_ref, vo_ref = refs[3 + np_:]
        terms = []
        for part, ref in zip(parts, refs[1:1 + np_], strict=True):
            terms += [ref[...]] if part.ndim == 2 else [ref[p] for p in range(part.shape[0])]
        g = terms[0].astype(F32)
        for term in terms[1:]:
            g = g + term.astype(F32)
        m2 = ADAM_B1 * m_ref[...] + (1.0 - ADAM_B1) * g
        v2 = ADAM_B2 * v_ref[...] + (1.0 - ADAM_B2) * (g * g)
        go_ref[...] = g
        mo_ref[...] = m2
        vo_ref[...] = v2
        d_ref[...] = -ADAM_LR * ((m2 / c1) / (jnp.sqrt(v2 / c2) + ADAM_EPS) + ADAM_WD * w_ref[...])

    blk = pl.BlockSpec((tr, C), lambda i: (i, 0))
    own = pl.BlockSpec((tr, None, C), lambda i: (i, 0, 0)) if apart else blk
    part_specs = [blk if p.ndim == 2 else pl.BlockSpec((p.shape[0], tr, C), lambda i: (0, i, 0)) for p in parts]
    return pl.pallas_call(
        body, name=name, grid=(R // tr,),
        in_specs=[own] + part_specs + [own, own], out_specs=[own] * 4,
        out_shape=[jax.ShapeDtypeStruct(w.shape, F32)] * 4, compiler_params=_params())(w, *parts, m, v)


def _place():
    return lax.axis_index("x"), lax.axis_index("y"), lax.axis_index("c")


def ada_modulation(name, c, w_ada, after=()):
    n_mod = w_ada.shape[1]

    def exchange(src_ref, dst_ref, send_sems, recv_sems):
        x, y, c_ = _place()
        me = 4 * x + 2 * y + c_
        copies = []
        for k in range(1, N_DEV):
            px, py, pc = x ^ (k >> 2), y ^ ((k >> 1) & 1), c_ ^ (k & 1)
            copies.append(pltpu.make_async_remote_copy(
                src_ref=src_ref, dst_ref=dst_ref.at[me], send_sem=send_sems.at[k - 1], recv_sem=recv_sems.at[k - 1],
                device_id=(px, py, pc), device_id_type=MESH_ID))
        for cp in copies:
            cp.start()
        for cp in copies:
            cp.wait_recv()
        for cp in copies:
            cp.wait_send()
        return me

    def body(c_ref, w_ref, *refs):
        sc_ref, mod_ref, c_all, send_c, recv_c, send_m, recv_m = refs[len(after):]
        me = exchange(c_ref, c_all, send_c, recv_c)
        c_all[me] = c_ref[...]
        sc = _silu(jnp.concatenate([c_all[p] for p in range(N_DEV)], axis=0))
        sc_ref[...] = sc.astype(sc_ref.dtype)
        mod_ref[me] = _dot(sc, w_ref[...])
        exchange(mod_ref.at[me], mod_ref, send_m, recv_m)

    vmem = pl.BlockSpec(memory_space=pltpu.VMEM)
    return pl.pallas_call(
        body, name=name, in_specs=[vmem, vmem] + [pl.BlockSpec(memory_space=pl.ANY)] * len(after), out_specs=[vmem, vmem],
        out_shape=[jax.ShapeDtypeStruct((N_DEV, c.shape[1]), MXU_DTYPE), jax.ShapeDtypeStruct((N_DEV, N_DEV, n_mod), F32)],
        scratch_shapes=[pltpu.VMEM((N_DEV, 1, c.shape[1]), F32)] + [pltpu.SemaphoreType.DMA((N_DEV - 1,))] * 4,
        compiler_params=pltpu.CompilerParams(has_side_effects=True, vmem_limit_bytes=VMEM_LIMIT))(c, w_ada, *after)


HBM_SPEC = pl.BlockSpec(memory_space=pltpu.HBM)
SEM_SPEC = pl.BlockSpec(memory_space=pltpu.SEMAPHORE)
DATAFLOW = pltpu.SideEffectType.DATAFLOW_SIDE_EFFECTING


def _exchange_copies(srcs, lands, send_sems, recv_sems, gather, first=0):
    x, y, c = _place()
    me = 4 * x + 2 * y + c
    out = []
    for t, (src, land) in enumerate(zip(srcs, lands, strict=True)):
        for k in range(1, N_DEV):
            px, py, pc = x ^ (k >> 2), y ^ ((k >> 1) & 1), c ^ (k & 1)
            sem = 7 * (first + t) + k - 1
            out.append((k, pltpu.make_async_remote_copy(
                src_ref=src if gather else src.at[4 * px + 2 * py + pc],
                dst_ref=land.at[me] if gather else land.at[k - 1],
                send_sem=send_sems.at[sem], recv_sem=recv_sems.at[sem],
                device_id=(px, py, pc), device_id_type=MESH_ID)))
    return out


def _own_copies(srcs, lands, send_sems, first=0):
    x, y, c = _place()
    total = send_sems.shape[0] // N_DEV
    return [pltpu.make_async_copy(src, land.at[4 * x + 2 * y + c], send_sems.at[7 * total + first + t])
            for t, (src, land) in enumerate(zip(srcs, lands, strict=True))]


TREE_DIRECT = (1, 2, 4, 6)
TREE_FORWARDED = (3, 5, 7)


def exchange_start(name, arrs, gather, after=None, tree=False):
    n = len(arrs)
    lands = [lax.empty(((N_DEV,) + a.shape) if gather else ((N_DEV - 1,) + a.shape[1:]), a.dtype) for a in arrs]
    deps = [] if after is None else [after]

    def body(*refs):
        srcs, land_refs = refs[:n], refs[n:2 * n]
        send_sems, recv_sems = refs[2 * n + len(deps)], refs[2 * n + len(deps) + 1]
        token = refs[-1]
        for k, cp in _exchange_copies(srcs, land_refs, send_sems, recv_sems, gather):
            if not tree or k in TREE_DIRECT:
                cp.start()
        if gather:
            for cp in _own_copies(srcs, land_refs, send_sems):
                cp.start()
        token[...] = jnp.zeros_like(token)

    hbm = lambda a: pltpu.HBM(a.shape, a.dtype)
    res = pl.pallas_call(
        body, name=name,
        out_shape=(pltpu.SemaphoreType.DMA(((N_DEV if gather else 7) * n,)), pltpu.SemaphoreType.DMA((7 * n,)), *[hbm(a) for a in arrs],
                   *[hbm(l) for l in lands], jax.ShapeDtypeStruct((8, 128), F32)),
        in_specs=[HBM_SPEC] * (2 * n) + [pl.BlockSpec(memory_space=pl.ANY)] * len(deps),
        out_specs=(SEM_SPEC, SEM_SPEC, *[HBM_SPEC] * (2 * n), pl.BlockSpec(memory_space=pltpu.VMEM)),
        input_output_aliases={i: 2 + i for i in range(2 * n)},
        compiler_params=pltpu.CompilerParams(has_side_effects=DATAFLOW),
    )(*[pltpu.with_memory_space_constraint(a, pltpu.HBM) for a in arrs + lands], *deps)
    return res[0], res[1], list(res[2:2 + n]), list(res[2 + n:2 + 2 * n]), res[-1]


def exchange_forward(name, started, after, first=0, count=None):
    send_sems, recv_sems, srcs, lands, _ = started
    count = len(srcs) - first if count is None else count
    mine = lands[first:first + count]
    n = len(mine)

    def copies(land_refs, send_ref, recv_ref):
        x, y, c = _place()
        out = []
        for t, land in enumerate(land_refs):
            for k in (2, 4, 6):
                slot = land.at[4 * (x ^ (k >> 2)) + 2 * (y ^ ((k >> 1) & 1)) + c]
                came, goes = 7 * (first + t) + k - 1, 7 * (first + t) + (k ^ 1) - 1
                out.append((
                    pltpu.make_async_remote_copy(src_ref=slot, dst_ref=slot, send_sem=send_ref.at[came], recv_sem=recv_ref.at[came],
                                                 device_id=(x, y, c), device_id_type=MESH_ID),
                    pltpu.make_async_remote_copy(src_ref=slot, dst_ref=slot, send_sem=send_ref.at[goes], recv_sem=recv_ref.at[goes],
                                                 device_id=(x, y, 1 - c), device_id_type=MESH_ID)))
        return out

    after = list(after) if isinstance(after, (list, tuple)) else [after]

    def arrived(*refs):
        for came, _ in copies(refs[:n], refs[n], refs[n + 1]):
            came.wait_recv()

    def pass_on(*refs):
        for _, goes in copies(refs[:n], refs[n], refs[n + 1]):
            goes.start()
        refs[-1][...] = jnp.zeros_like(refs[-1])

    hbm = lambda a: pltpu.HBM(a.shape, a.dtype)
    here = pl.pallas_call(
        arrived, name=name + "_arrived", out_shape=tuple(hbm(a) for a in mine),
        in_specs=[HBM_SPEC] * n + [SEM_SPEC, SEM_SPEC] + [pl.BlockSpec(memory_space=pl.ANY)] * len(after),
        out_specs=tuple([HBM_SPEC] * n), input_output_aliases={i: i for i in range(n)},
        compiler_params=pltpu.CompilerParams(has_side_effects=DATAFLOW),
    )(*mine, send_sems, recv_sems, *after)
    res = pl.pallas_call(
        pass_on, name=name, out_shape=(*[hbm(a) for a in mine], jax.ShapeDtypeStruct((8, 128), F32)),
        in_specs=[HBM_SPEC] * n + [SEM_SPEC, SEM_SPEC],
        out_specs=(*[HBM_SPEC] * n, pl.BlockSpec(memory_space=pltpu.VMEM)), input_output_aliases={i: i for i in range(n)},
        compiler_params=pltpu.CompilerParams(has_side_effects=DATAFLOW),
    )(*here, send_sems, recv_sems)
    lands = lands[:first] + list(res[:n]) + lands[first + count:]
    return (send_sems, recv_sems, srcs, lands, res[-1])


def exchange_wait(name, started, gather, after, first=0, count=None, tree=False):
    send_sems, recv_sems, srcs, lands, _ = started
    count = len(srcs) - first if count is None else count
    srcs, lands = srcs[first:first + count], lands[first:first + count]
    n = len(srcs)

    def body(*refs):
        src_refs, land_refs = refs[:n], refs[n:2 * n]
        copies = _exchange_copies(src_refs, land_refs, refs[2 * n], refs[2 * n + 1], gather, first)
        for _, cp in copies:
            cp.wait_send()
        for k, cp in copies:
            if not tree or k in (1,) + TREE_FORWARDED:
                cp.wait_recv()
        if gather:
            for cp in _own_copies(src_refs, land_refs, refs[2 * n], first):
                cp.wait()

    hbm = lambda a: pltpu.HBM(a.shape, a.dtype)
    res = pl.pallas_call(
        body, name=name, out_shape=tuple(hbm(a) for a in srcs + lands),
        in_specs=[HBM_SPEC] * (2 * n) + [SEM_SPEC, SEM_SPEC, pl.BlockSpec(memory_space=pl.ANY)],
        out_specs=tuple([HBM_SPEC] * (2 * n)), input_output_aliases={i: i for i in range(2 * n)},
        compiler_params=pltpu.CompilerParams(has_side_effects=DATAFLOW),
    )(*srcs, *lands, send_sems, recv_sems, after)
    return list(res[:n]), list(res[n:])


def _gather_cols(stack):
    p, k, n = stack.shape
    return stack.transpose(1, 0, 2).reshape(k, p * n)


def _scatter_cols(full):
    k, n = full.shape
    return full.reshape(k, N_DEV, n // N_DEV).transpose(1, 0, 2)


def _gather_rows(stack):
    p, r, n = stack.shape
    return stack.reshape(p * r, n)


def _scatter_rows(full):
    r, n = full.shape
    return full.reshape(N_DEV, r // N_DEV, n)


_IN_NAT = Q_LORA + KV_LORA
TRANSPOSED = ("w_in", "w_q_b", "w_up")
ROWS_APART = ("w_in", "w_conv")


def to_kernel_layout(name, w):
    if name == "w_in":
        z = lambda n: jnp.zeros((n, w.shape[1]), w.dtype)
        return jnp.concatenate([w[:_IN_NAT], z(KPE_LO), w[_IN_NAT:_IN_NAT + ROPE], z(LANES - KPE_LO - ROPE), w[_IN_NAT + ROPE:]], axis=0)
    if name == "w_q_b":
        return jnp.pad(w.reshape(HEADS, NOPE + ROPE, -1), ((0, 0), (0, LANES - NOPE - ROPE), (0, 0))).reshape(HEADS * LANES, -1)
    if name == "w_o":
        mla = jnp.pad(w[:HEADS * NOPE].reshape(HEADS, NOPE, -1), ((0, 0), (LANES - NOPE, 0), (0, 0))).reshape(HEADS * LANES, -1)
        return jnp.concatenate([mla, w[HEADS * NOPE:]], axis=0)
    return w


def from_kernel_layout(name, g):
    if name == "w_in":
        return jnp.concatenate([g[:_IN_NAT], g[P_KPE + KPE_LO:P_KPE + KPE_LO + ROPE], g[P_QD:]], axis=0)
    if name == "w_q_b":
        return g.reshape(HEADS, LANES, -1)[:, :NOPE + ROPE, :].reshape(HEADS * (NOPE + ROPE), -1)
    if name == "w_o":
        mla = g[:HEADS * LANES].reshape(HEADS, LANES, -1)[:, LANES - NOPE:, :].reshape(HEADS * NOPE, -1)
        return jnp.concatenate([mla, g[HEADS * LANES:]], axis=0)
    return g


SMALL_COLS = 1024
SMALL_ROWS = 24
SMALL_AT = {"loss": (0, 0, 1), "b_ada": (1, 0, 6 * D_MODEL), "g_mix_norm": (7, 0, D_MODEL), "g_q_lat": (8, 0, Q_LORA),
            "g_kv_lat": (9, 0, KV_LORA), "g_mla_q_nope": (10, 0, NOPE), "g_mla_q_pe": (10, 128, ROPE),
            "g_mla_k_nope": (10, 256, NOPE), "g_mla_k_pe": (10, 384, ROPE), "g_dil_q": (10, 512, DIL_DIM),
            "g_dil_k": (10, 640, DIL_DIM), "g_ffn_norm": (11, 0, D_MODEL), "b_conv": (12, 0, 2 * D_FF)}
SMALL_PARAMS = tuple(n for n in SMALL_AT if n != "loss")


def _pack_small(values):
    by_row = {}
    for name, (row, off, n) in SMALL_AT.items():
        by_row.setdefault(row, []).append((off, values[name].reshape(-1).astype(F32)))
    out = []
    for row in sorted(by_row):
        pieces, at = [], 0
        for off, v in sorted(by_row[row], key=lambda t: t[0]):
            pieces += [jnp.zeros((off - at,), F32), v]
            at = off + v.shape[0]
        flat = jnp.concatenate(pieces)
        nrows = -(-flat.shape[0] // SMALL_COLS)
        out.append(jnp.pad(flat, (0, nrows * SMALL_COLS - flat.shape[0])).reshape(nrows, SMALL_COLS))
    packed = jnp.concatenate(out, axis=0)
    return jnp.pad(packed, ((0, SMALL_ROWS - packed.shape[0]), (0, 0)))


def _adam(w, g, m, v):
    c1 = 1.0 - ADAM_B1 ** ADAM_STEP
    c2 = 1.0 - ADAM_B2 ** ADAM_STEP
    m2 = ADAM_B1 * m + (1.0 - ADAM_B1) * g
    v2 = ADAM_B2 * v + (1.0 - ADAM_B2) * (g * g)
    return -ADAM_LR * ((m2 / c1) / (jnp.sqrt(v2 / c2) + ADAM_EPS) + ADAM_WD * w), m2, v2


def adamw_small(name, stack, params):
    flat = [a for n in SMALL_PARAMS for a in params[n]]

    def body(stack_ref, *refs):
        ins, outs = refs[:len(flat)], refs[len(flat):]
        g_all = stack_ref[0]
        for p in range(1, N_DEV):
            g_all = g_all + stack_ref[p]
        outs[0][...] = g_all[0:1, 0:1]
        for i, pname in enumerate(SMALL_PARAMS):
            row, off, n = SMALL_AT[pname]
            w_ref, m_ref, v_ref = ins[3 * i:3 * i + 3]
            go_ref, d_ref, mo_ref, vo_ref = outs[1 + 4 * i:5 + 4 * i]
            for c0 in range(0, n, SMALL_COLS):
                cn = min(SMALL_COLS, n - c0)
                r = row + c0 // SMALL_COLS
                g = g_all[r:r + 1, off:off + cn]
                cols = (slice(None), slice(c0, c0 + cn))
                d, m2, v2 = _adam(w_ref[cols], g, m_ref[cols], v_ref[cols])
                go_ref[cols], d_ref[cols], mo_ref[cols], vo_ref[cols] = g, d, m2, v2

    whole = lambda a: pl.BlockSpec(a.shape, lambda: (0,) * a.ndim)
    out_shape = [jax.ShapeDtypeStruct((1, 1), F32)] + [jax.ShapeDtypeStruct(a.shape, F32) for n in SMALL_PARAMS for a in params[n][:1] * 4]
    res = pl.pallas_call(body, name=name, in_specs=[whole(stack)] + [whole(a) for a in flat],
                         out_specs=[pl.BlockSpec(s.shape, lambda s=s: (0,) * len(s.shape)) for s in out_shape],
                         out_shape=out_shape, compiler_params=_params())(stack, *flat)
    return res[0], {n: res[1 + 4 * i:5 + 4 * i] for i, n in enumerate(SMALL_PARAMS)}


def _local_step(x, pos, mod, target, w, fetch, emit, halfway=lambda after: None):
    S = SEQ
    sh1, sc1, g1, sh2, sc2, g2 = [mod[:, i * D_MODEL:(i + 1) * D_MODEL] for i in range(6)]
    zeros = lambda n: jnp.zeros((1, n), F32)
    g_q = jnp.concatenate([w["g_mla_q_nope"], w["g_mla_q_pe"], zeros(LANES - NOPE - ROPE)], axis=1)
    g_k = jnp.concatenate([w["g_mla_k_nope"], zeros(LANES - NOPE)], axis=1)
    g_kpe = jnp.concatenate([zeros(KPE_LO), w["g_mla_k_pe"], zeros(LANES - KPE_LO - ROPE)], axis=1)
    g_dq = jnp.concatenate([w["g_dil_q"]] * 2, axis=1)
    g_dk = jnp.concatenate([w["g_dil_k"]] * 2, axis=1)
    b_conv = w["b_conv"]

    def inv_freq(d):
        return jnp.power(ROPE_THETA, -2.0 * jnp.arange(d // 2, dtype=F32) / d)

    n_m, n_d = ROPE // 2, DIL_DIM // 2
    freqs = jnp.concatenate([inv_freq(ROPE), inv_freq(DIL_DIM), jnp.zeros((LANES - n_m - n_d,), F32)]).reshape(1, LANES)

    def tables_fn(rows, params):
        (p,), (f,) = rows, params
        c, s = jnp.cos(p * f), jnp.sin(p * f)
        one, zero = jnp.ones_like(c), jnp.zeros_like(c)
        mla = lambda t, fill: jnp.concatenate([fill[:, :KPE_LO], t[:, :n_m], t[:, :n_m], fill[:, :LANES - KPE_LO - ROPE]], axis=1)
        dil = lambda t: jnp.concatenate([t[:, n_m:n_m + n_d]] * 4, axis=1)
        return [mla(c, one), mla(s, zero), dil(c), dil(s)], []

    cos_m, sin_m, cos_d, sin_d = rowwise("rope_tables", tables_fn, [pos], [freqs], [(LANES, F32)] * 4)
    tables = [cos_m, sin_m, cos_d, sin_d]
    H_M, H_D = ROPE // 2, DIL_DIM // 2

    def ln1_fn(rows, params):
        (xv,), (g, sc, sh) = rows, params
        y, _, _ = _rms(xv, g)
        return [y * (1.0 + sc) + sh], []

    (h,) = rowwise("ln1_fwd", ln1_fn, [x], [w["g_mix_norm"], sc1, sh1], [(D_MODEL, MXU_DTYPE)], dep=sin_d)
    w_in = fetch("w_in", h)

    def proj_fn(rows, params):
        (hv, cm, sm, cd, sd), (w_t, gq, gkv, gkp, gdq, gdk) = rows, params
        pv = _dot(hv, w_t, "nt")
        kper = _rope(_grms(pv[:, P_KPE:P_QD], gkp, KPE_GROUPS)[0], cm, sm, H_M)
        qd = [_rope(_grms(c, gdq, DIL_GROUPS)[0], cd, sd, H_D) for c in _chunks(pv[:, P_QD:P_KD])]
        kd = [_rope(_grms(c, gdk, DIL_GROUPS)[0], cd, sd, H_D) for c in _chunks(pv[:, P_KD:P_VD])]
        return [pv, _rms(pv[:, P_QLAT:P_KVLAT], gq)[0], _rms(pv[:, P_KVLAT:P_KPE], gkv)[0], kper,
                jnp.concatenate(qd, axis=1), jnp.concatenate(kd, axis=1)], []

    post_params = [w["g_q_lat"], w["g_kv_lat"], g_kpe, g_dq, g_dk]
    proj, qln, kvn, kper, qd_r, kd_r = rowwise(
        "proj_fwd", proj_fn, [h] + tables, [w_in] + post_params,
        [(P_END, F32), (Q_LORA, MXU_DTYPE), (KV_LORA, MXU_DTYPE), (LANES, MXU_DTYPE)] + [(DIL_WIDTH, F32)] * 2, tm=256)
    w_q_b, w_kv_b = fetch("w_q_b", qln), fetch("w_kv_b", kvn)

    def mla_proj_fn(rows, params):
        (qlv, kvlv, kp, cm, sm), (wq_t, wkv, gq, gk) = rows, params
        qv, kvv = _dot(qlv, wq_t, "nt"), _dot(kvlv, wkv)
        value_lanes = _lane(kp.shape) >= NOPE
        qs, ks, vs = [], [], []
        for qc, kc in zip(_chunks(qv), _chunks(kvv), strict=True):
            qs.append(_rope(_grms(qc, gq, Q_GROUPS)[0], cm, sm, H_M))
            ks.append(_grms(kc, gk, K_GROUPS)[0] + kp)
            vs.append(jnp.where(value_lanes, kc, 0.0))
        return [qv, kvv] + [jnp.concatenate(t, axis=1) for t in (qs, ks, vs)], []

    q, kv, q_mla, k_mla, v_mla = rowwise(
        "mla_proj", mla_proj_fn, [qln, kvn, kper, cos_m, sin_m], [w_q_b, w_kv_b, g_q, g_k],
        [(HEADS * LANES, F32)] * 2 + [(HEADS * LANES, MXU_DTYPE)] * 3, tm=256)
    mla_scale = (NOPE + ROPE) ** -0.5
    o_cat, lse_mla = mla_fwd("mla_fwd", q_mla, k_mla, v_mla, mla_scale)
    passed = halfway(lse_mla)

    band = [band_fwd(f"band{dil}_fwd", qd_r, kd_r, proj, dil, dep=passed) for dil in DILATIONS]
    o_cat, lse_mix = combine_fwd("dil_combine", [b[0] for b in band], [b[1] for b in band], o_cat)
    w_o = fetch("w_o", o_cat)

    def mid_fn(rows, params):
        (ov, xv), (w_out, gate1, g, sc, sh) = rows, params
        mx = _dot(ov, w_out)
        x1 = xv + gate1 * mx
        y, _, _ = _rms(x1, g)
        return [mx, x1, y * (1.0 + sc) + sh], []

    mix, x1, h2 = rowwise("mix_fwd", mid_fn, [o_cat, x], [w_o, g1, w["g_ffn_norm"], sc2, sh2],
                          [(D_MODEL, F32), (D_MODEL, F32), (D_MODEL, MXU_DTYPE)], tm=256)
    w_up, w_conv, w_down = fetch("w_up", h2), fetch("w_conv", h2), fetch("w_down", h2)
    dn, up = ffn_fwd("ffn_fwd", h2, w_up, w_conv, b_conv, w_down)

    def final_fn(rows, params):
        (x1v, dnv, tgt), (gate2,) = rows, params
        r = x1v + gate2 * dnv - tgt
        dy = r * (1.0 / D_MODEL)
        loss = jnp.sum(_colsum(r * r), axis=-1, keepdims=True) * (0.5 / D_MODEL)
        return [dy, gate2 * dy], [loss, _colsum(dy * dnv)]

    dy, d_dn, loss, dg2 = rowwise("loss_head", final_fn, [x1, dn, target], [g2], [(D_MODEL, F32), (D_MODEL, MXU_DTYPE)],
                                  [1, D_MODEL])
    dh2, g_up, g_down, g_w_conv, g_b_conv = ffn_bwd("ffn_bwd", h2, up, w_up, w_conv, b_conv, d_dn, w_down)
    emit("w_down", g_down)
    emit("w_conv", g_w_conv)
    sent = emit("w_up", g_up)

    def mid_bwd_fn(rows, params):
        (dh2v, dyv, x1v, mx), (gate1, g, sc) = rows, params
        yn, n, rstd = _rms(x1v, g)
        dx_n, dg = _rms_bwd(dh2v * (1.0 + sc), n, rstd, g)
        dx1 = dyv + dx_n
        return [dx1, gate1 * dx1], [dg, _colsum(dh2v * yn), _colsum(dh2v), _colsum(dx1 * mx)]

    dx1, dmix, dg_ffn, dsc2, dsh2, dg1 = rowwise(
        "mid_bwd", mid_bwd_fn, [dh2, dy, x1, mix], [g1, w["g_ffn_norm"], sc2], [(D_MODEL, F32), (D_MODEL, MXU_DTYPE)],
        [D_MODEL] * 4, dep=sent)

    sent = emit("w_o", matmul("mix_wgrad", o_cat, dmix, "tn", tm=512, out_dtype=MXU_DTYPE))
    do_cat = matmul("mix_dgrad", dmix, w_o, "nt", tm=512, dep=sent)
    dband = None
    for dil, b in zip(DILATIONS, band):
        dband = band_bwd(f"band{dil}_bwd", qd_r, kd_r, proj, b[1], lse_mix, o_cat, do_cat, dil, before=dband)
    dq_mla, dkv_mla, dkper = mla_bwd("mla_bwd", q_mla, k_mla, v_mla, o_cat, do_cat, lse_mla, mla_scale)

    def mla_prep_bwd_fn(rows, params):
        (dqv, dkvv, qv, kvv, cm, sm), (gq, gk) = rows, params
        nope_lanes = _lane(cm.shape) < NOPE
        dqs, dkvs, dgq, dgk = [], [], 0.0, 0.0
        for dqc, dkc, qc, kc in zip(_chunks(dqv), _chunks(dkvv), _chunks(qv), _chunks(kvv), strict=True):
            _, n, rstd = _grms(qc, gq, Q_GROUPS)
            dx, dg = _grms_bwd(_rope_bwd(dqc, cm, sm, H_M), n, rstd, gq, Q_GROUPS)
            dqs.append(dx)
            dgq = dgq + dg
            _, n, rstd = _grms(kc, gk, K_GROUPS)
            dx, dg = _grms_bwd(dkc, n, rstd, gk, K_GROUPS)
            dkvs.append(jnp.where(nope_lanes, dx, dkc))
            dgk = dgk + dg
        return [jnp.concatenate(dqs, axis=1), jnp.concatenate(dkvs, axis=1)], [dgq, dgk]

    dq, dkv, dg_q, dg_k = rowwise("mla_prep_bwd", mla_prep_bwd_fn, [dq_mla, dkv_mla, q, kv, cos_m, sin_m], [g_q, g_k],
                                  [(HEADS * LANES, MXU_DTYPE)] * 2, [LANES, LANES], tm=256)
    emit("w_q_b", matmul("q_wgrad", dq, qln, "tn", out_dtype=MXU_DTYPE))
    emit("w_kv_b", matmul("kv_wgrad", kvn, dkv, "tn", out_dtype=MXU_DTYPE))

    def pre_bwd_fn(rows, params):
        dqv, dkvv, dkp, dqd_, dkd_, dvd_, pv, cm, sm, cd, sd = rows
        wq_t, wkv, gq, gkv, gkp, gdq, gdk = params
        dql, dkvl = _dot(dqv, wq_t), _dot(dkvv, wkv, "nt")
        r_q = _norm_bwd(dql, pv[:, P_QLAT:P_KVLAT], gq)
        r_kv = _norm_bwd(dkvl, pv[:, P_KVLAT:P_KPE], gkv)
        _, n, rstd = _grms(pv[:, P_KPE:P_QD], gkp, KPE_GROUPS)
        r_kp = _grms_bwd(_rope_bwd(dkp, cm, sm, H_M), n, rstd, gkp, KPE_GROUPS)
        outs, dgs = [r_q[0], r_kv[0], r_kp[0]], []
        for dval, lo, g in ((dqd_, P_QD, gdq), (dkd_, P_KD, gdk)):
            dg_sum = 0.0
            for dc, xc in zip(_chunks(dval), _chunks(pv[:, lo:lo + DIL_WIDTH]), strict=True):
                _, n, rstd = _grms(xc, g, DIL_GROUPS)
                dx, dg = _grms_bwd(_rope_bwd(dc, cd, sd, H_D), n, rstd, g, DIL_GROUPS)
                outs.append(dx)
                dg_sum = dg_sum + dg
            dgs.append(dg_sum)
        return [jnp.concatenate(outs + [dvd_], axis=1)], [r_q[1], r_kv[1], r_kp[1]] + dgs

    dproj, dg_q_lat, dg_kv_lat, dg_kpe, dg_dq, dg_dk = rowwise(
        "proj_pre_bwd", pre_bwd_fn,
        [dq, dkv, dkper] + list(dband) + [proj] + tables, [w_q_b, w_kv_b] + post_params,
        [(P_END, MXU_DTYPE)], [Q_LORA, KV_LORA, LANES, LANES, LANES], tm=256)
    sent = emit("w_in", matmul("proj_wgrad", dproj, h, "tn", tn=512, out_dtype=MXU_DTYPE))

    def ln1_bwd_fn(rows, params):
        (dpv, dres, xv), (w_t, g, sc) = rows, params
        dhv = _dot(dpv, w_t)
        yn, n, rstd = _rms(xv, g)
        dx_n, dg = _rms_bwd(dhv * (1.0 + sc), n, rstd, g)
        return [dres + dx_n], [dg, _colsum(dhv * yn), _colsum(dhv)]

    grad_x, dg_mix, dsc1, dsh1 = rowwise("proj_dgrad", ln1_bwd_fn, [dproj, dx1, x], [w_in, w["g_mix_norm"], sc1],
                                         [(D_MODEL, F32)], [D_MODEL] * 3, tm=256, dep=sent)
    dmod = jnp.concatenate([dsh1, dsc1, dg1, dsh2, dsc2, dg2], axis=-1)
    small = {"loss": loss, "b_ada": dmod, "g_mix_norm": dg_mix, "g_q_lat": dg_q_lat, "g_kv_lat": dg_kv_lat,
             "g_mla_q_nope": dg_q[:, :NOPE], "g_mla_q_pe": dg_q[:, NOPE:NOPE + ROPE], "g_mla_k_nope": dg_k[:, :NOPE],
             "g_mla_k_pe": dg_kpe[:, KPE_LO:KPE_LO + ROPE], "g_dil_q": dg_dq[:, :DIL_DIM] + dg_dq[:, DIL_DIM:],
             "g_dil_k": dg_dk[:, :DIL_DIM] + dg_dk[:, DIL_DIM:], "g_ffn_norm": dg_ffn,
             "b_conv": g_b_conv}
    return grad_x, small


COL_SHARDED = ("w_kv_b", "w_conv")
ROW_SHARDED = ("w_o", "w_down") + TRANSPOSED
ADAM_TILE = {"w_ada": 256, "w_up": 176, "w_down": 176}
GATHER_GROUPS = (("w_in",), ("w_q_b", "w_kv_b"), ("w_o",), ("w_up", "w_conv", "w_down"))
START_STAGES = ((0, 1), (2, 3))
FORWARD_STAGES = ((0, 1), (2,), (3,))
FORWARD_WITH = {"w_o": 2}
SCATTER_GROUPS = (("w_down", "w_conv", "w_up"), ("w_o",), ("w_q_b", "w_kv_b", "w_in"))
OUT_WEIGHTS = ("w_ada", "b_ada", "g_mix_norm", "w_in", "g_q_lat", "w_q_b", "g_kv_lat", "w_kv_b", "g_mla_q_nope", "g_mla_q_pe",
               "g_mla_k_nope", "g_mla_k_pe", "g_dil_q", "g_dil_k", "w_o", "g_ffn_norm", "w_up", "w_conv", "b_conv", "w_down")


def kernel(x, c, positions, w_ada, b_ada, g_mix_norm, w_in, g_q_lat, w_q_b, g_kv_lat, w_kv_b, g_mla_q_nope, g_mla_q_pe, g_mla_k_nope, g_mla_k_pe, g_dil_q, g_dil_k, w_o, g_ffn_norm, w_up, w_conv, b_conv, w_down, loss_target, m_w_ada, m_b_ada, m_g_mix_norm, m_w_in, m_g_q_lat, m_w_q_b, m_g_kv_lat, m_w_kv_b, m_g_mla_q_nope, m_g_mla_q_pe, m_g_mla_k_nope, m_g_mla_k_pe, m_g_dil_q, m_g_dil_k, m_w_o, m_g_ffn_norm, m_w_up, m_w_conv, m_b_conv, m_w_down, v_w_ada, v_b_ada, v_g_mix_norm, v_w_in, v_g_q_lat, v_w_q_b, v_g_kv_lat, v_w_kv_b, v_g_mla_q_nope, v_g_mla_q_pe, v_g_mla_k_nope, v_g_mla_k_pe, v_g_dil_q, v_g_dil_k, v_w_o, v_g_ffn_norm, v_w_up, v_w_conv, v_b_conv, v_w_down):
    args = dict(locals())
    xi, yi, ci = _place()
    me = 4 * xi + 2 * yi + ci
    def local(prefix, n):
        a = args[prefix + n]
        if n in ROWS_APART:
            return jnp.transpose(a, (2, 0, 1) if n in TRANSPOSED else (1, 0, 2))
        return a[0].T if n in TRANSPOSED else a[0]

    def as_output(n, r):
        if n in ROWS_APART:
            return jnp.transpose(r, (1, 2, 0) if n in TRANSPOSED else (1, 0, 2))
        return (r.T if n in TRANSPOSED else r)[None]

    shard = {n: local("", n) for n in COL_SHARDED + ROW_SHARDED + ("w_ada",)}
    flat = lambda n, a: a.reshape(a.shape[0], a.shape[-1]) if n in ROWS_APART else a
    small_w = {n: args[n] for n in SMALL_PARAMS}

    payload = {n: flat(n, shard[n]) if n == "w_conv" else flat(n, shard[n]).astype(MXU_DTYPE) for n in COL_SHARDED + ROW_SHARDED}
    start_order = [[n for i in groups for n in GATHER_GROUPS[i]] for groups in START_STAGES]

    sc_all, mod_all = ada_modulation("ada_mod", c, shard["w_ada"], after=[payload[n] for n in start_order[0]])

    exchange_of = lambda i: [e for e, groups in enumerate(START_STAGES) if i in groups][0]
    start_stage = lambda e, after: exchange_start(f"gather_start{e}", [payload[n] for n in start_order[e]], gather=True,
                                                  after=after, tree=True)
    gathered = {0: start_stage(0, mod_all)}
    after_start = gathered[0][-1]
    full, forwarded = {}, set()

    def forward(stage, after):
        e = exchange_of(FORWARD_STAGES[stage][0])
        if stage not in forwarded:
            forwarded.add(stage)
            first = start_order[e].index(GATHER_GROUPS[FORWARD_STAGES[stage][0]][0])
            count = sum(len(GATHER_GROUPS[i]) for i in FORWARD_STAGES[stage])
            starts_next = e + 1 < len(START_STAGES) and e + 1 not in gathered
            ready = [payload[n] for n in start_order[e + 1]] if starts_next else []
            gathered[e] = exchange_forward(f"gather_forward{stage}", gathered[e], [after] + ready, first, count)
            if starts_next:
                gathered[e + 1] = start_stage(e + 1, gathered[e][-1])
        return gathered[e][-1]

    def fetch(name, after):
        if name not in full:
            (i, grp), = [(i, grp) for i, grp in enumerate(GATHER_GROUPS) if name in grp]
            (stage,) = [s for s, groups in enumerate(FORWARD_STAGES) if i in groups]
            forward(stage, after)
            if name in FORWARD_WITH:
                forward(FORWARD_WITH[name], after)
            e = exchange_of(i)
            behind = gathered[e + 1][-1] if e + 1 in gathered else after
            srcs, lands = exchange_wait(f"gather{i}_wait", gathered[e], True, behind, start_order[e].index(grp[0]), len(grp), tree=True)
            for n, stack in zip(grp, lands, strict=True):
                full[n] = to_kernel_layout(n, _gather_cols(stack) if n in COL_SHARDED else _gather_rows(stack))
        return full[name]

    mod_row = lax.dynamic_index_in_dim(mod_all, me, axis=1, keepdims=False).reshape(1, 6 * D_MODEL)
    (mod,) = rowwise("ada_bias", lambda rows, params: ([rows[0] + rows[1]], []), [mod_row, b_ada], [], [(6 * D_MODEL, F32)],
                     dep=after_start)

    own, pending, scatters = {}, {}, {}

    def emit(name, grad):
        grad = from_kernel_layout(name, grad)
        parts = _scatter_cols(grad) if name in COL_SHARDED else _scatter_rows(grad)
        own[name] = lax.dynamic_index_in_dim(parts, me, 0, keepdims=False)
        pending[name] = parts
        for i, grp in enumerate(SCATTER_GROUPS):
            if name == grp[-1]:
                scatters[i] = exchange_start(f"scatter{i}_start", [pending[n] for n in grp], gather=False)
                return scatters[i][-1]
        return None

    pos = positions.reshape(SEQ, 1).astype(F32)
    grad_x, small = _local_step(x[0], pos, mod, loss_target[0], small_w, fetch, emit, halfway=lambda after: forward(1, after))

    small_sent = exchange_start("small_start", [_pack_small(small)], gather=True, after=grad_x)

    res, done = {}, small_sent[-1]
    for i, grp in enumerate(SCATTER_GROUPS):
        _, lands = exchange_wait(f"scatter{i}_wait", scatters[i], False, done)
        for n, land in zip(grp, lands, strict=True):
            res[n] = adamw(f"adamw_{n}", shard[n], [own[n], land], local("m_", n), local("v_", n), ADAM_TILE.get(n))
            done = res[n][0]
            res[n] = [as_output(n, r) for r in res[n]]
    _, (small_all,) = exchange_wait("small_wait", small_sent, True, done)
    loss, small_res = adamw_small("adamw_small", small_all, {n: (args[n], args["m_" + n], args["v_" + n]) for n in SMALL_PARAMS})
    row, _, n_mod = SMALL_AT["b_ada"]
    dmod_all = small_all[:, row:row + n_mod // SMALL_COLS, :].reshape(N_DEV, n_mod)
    dmod_mine = lax.dynamic_slice_in_dim(dmod_all, me * (6 * D_MODEL // N_DEV), 6 * D_MODEL // N_DEV, axis=1)
    g_w_ada = matmul("ada_wgrad", sc_all, dmod_mine, "tn")
    res["w_ada"] = [r[None] for r in adamw("adamw_w_ada", shard["w_ada"], [g_w_ada], m_w_ada[0], v_w_ada[0], ADAM_TILE["w_ada"])]

    def leaf(kind, n):
        return res[n][kind] if n in res else small_res[n][kind]

    return (loss.reshape(()), grad_x[None], *[leaf(k, n) for k in range(4) for n in OUT_WEIGHTS])
```

```python
import jax
import jax.numpy as jnp
from jax import lax
from jax.experimental import pallas as pl
from jax.experimental.pallas import tpu as pltpu

F32 = jnp.float32
MXU_DTYPE = jnp.bfloat16

N_DEV = 8
D_MODEL = 1024
SEQ = 2048
HEADS = 8
NOPE = 64
ROPE = 32
Q_LORA = 512
KV_LORA = 256
DIL_DIM = 64
DIL_WIDTH = HEADS * DIL_DIM
DILATIONS = (1, 4, 16)
SPAN = 128
D_FF = 2816
LANES = 128
SUBLANES = 8
ROPE_THETA = 10000.0
EPS = 1e-6
NEG_INF = -1e30
ADAM_LR, ADAM_B1, ADAM_B2, ADAM_EPS, ADAM_WD, ADAM_STEP = 0.001, 0.9, 0.999, 1e-08, 0.01, 10
VMEM_LIMIT = 56 * 1024 * 1024
MESH_ID = pl.DeviceIdType.MESH

P_QLAT, P_KVLAT, P_KPE, P_QD, P_KD, P_VD, P_END = 0, 512, 768, 896, 1408, 1920, 2432
KPE_LO = 64
MIX_IN = HEADS * LANES + DIL_WIDTH


def _params(**kw):
    return pltpu.CompilerParams(vmem_limit_bytes=VMEM_LIMIT, **kw)


def rowwise(name, fn, rows, params, out_rows, out_accs=(), tm=512, dep=None):
    deps = [] if dep is None else [dep]
    rows = [r if isinstance(r, tuple) else (r, r.shape[1], 0) for r in rows]
    R = rows[0][0].shape[0]
    tm = min(tm, R)
    steps = R // tm
    assert steps * tm == R
    in_specs = []
    for a, width, cb in rows:
        ri = a.shape[0]
        per = ri // tm
        assert per * tm == ri
        if ri == R:
            in_specs.append(pl.BlockSpec((tm, width), lambda i, cb=cb: (i, cb)))
        else:
            in_specs.append(pl.BlockSpec((tm, width), lambda i, per=per, cb=cb: (i % per, cb)))
    for p in params:
        in_specs.append(pl.BlockSpec(p.shape, lambda i: (0,) * p.ndim))
    in_specs += [pl.BlockSpec(memory_space=pl.ANY)] * len(deps)
    out_shape = [jax.ShapeDtypeStruct((R, d), dt) for d, dt in out_rows]
    out_specs = [pl.BlockSpec((tm, d), lambda i: (i, 0)) for d, _ in out_rows]
    out_shape += [jax.ShapeDtypeStruct((1, n), F32) for n in out_accs]
    out_specs += [pl.BlockSpec((1, n), lambda i: (0, 0)) for n in out_accs]
    nr, npar, no, na = len(rows), len(params), len(out_rows), len(out_accs)

    def body(*refs):
        rvals = [r[...] for r in refs[:nr]]
        pvals = [r[...] for r in refs[nr:nr + npar]]
        outs, accs = fn(rvals, pvals)
        first_out = nr + npar + len(deps)
        for ref, v in zip(refs[first_out:first_out + no], outs, strict=True):
            ref[...] = v.astype(ref.dtype)
        if na:
            acc_refs = refs[first_out + no:]
            i = pl.program_id(0)

            @pl.when(i == 0)
            def _():
                for ref, v in zip(acc_refs, accs, strict=True):
                    ref[...] = v

            @pl.when(i > 0)
            def _():
                for ref, v in zip(acc_refs, accs, strict=True):
                    ref[...] += v

    res = pl.pallas_call(body, name=name, grid=(steps,), in_specs=in_specs, out_specs=out_specs,
                         out_shape=out_shape, compiler_params=_params())(*[r[0] for r in rows], *params, *deps)
    return list(res)


_DIMS = {"nn": ((1,), (0,)), "nt": ((1,), (1,)), "tn": ((0,), (0,))}


def _dot(a, b, mode="nn"):
    return lax.dot_general(a.astype(MXU_DTYPE), b.astype(MXU_DTYPE), (_DIMS[mode], ((), ())),
                           preferred_element_type=F32)


def matmul(name, a, b, mode, tm=None, tn=None, tk=None, out_dtype=F32, dep=None):
    if mode == "tn":
        K, M = a.shape
    else:
        M, K = a.shape
    N = b.shape[0] if mode == "nt" else b.shape[1]
    tm, tn, tk = tm or M, tn or N, tk or K
    nm, nn, nk = M // tm, N // tn, K // tk
    assert nm * tm == M and nn * tn == N and nk * tk == K
    a_spec = pl.BlockSpec((tk, tm), lambda i, j, k: (k, i)) if mode == "tn" else pl.BlockSpec((tm, tk), lambda i, j, k: (i, k))
    b_spec = pl.BlockSpec((tn, tk), lambda i, j, k: (j, k)) if mode == "nt" else pl.BlockSpec((tk, tn), lambda i, j, k: (k, j))
    deps = [] if dep is None else [dep]

    def body(a_ref, b_ref, *rest):
        o_ref, scratch = rest[len(deps)], rest[len(deps) + 1:]
        p = _dot(a_ref[...], b_ref[...], mode)
        if nk == 1:
            o_ref[...] = p.astype(o_ref.dtype)
        else:
            acc = scratch[0]
            k = pl.program_id(2)

            @pl.when(k == 0)
            def _():
                acc[...] = p

            @pl.when(k > 0)
            def _():
                acc[...] += p

            @pl.when(k == nk - 1)
            def _():
                o_ref[...] = acc[...].astype(o_ref.dtype)

    return pl.pallas_call(
        body, name=name, grid=(nm, nn, nk), in_specs=[a_spec, b_spec] + [pl.BlockSpec(memory_space=pl.ANY)] * len(deps),
        out_specs=pl.BlockSpec((tm, tn), lambda i, j, k: (i, j)),
        out_shape=jax.ShapeDtypeStruct((M, N), out_dtype),
        scratch_shapes=[pltpu.VMEM((tm, tn), F32)] if nk > 1 else [],
        compiler_params=_params())(a, b, *deps)


def _rms(x, g):
    rstd = lax.rsqrt(jnp.mean(x * x, axis=-1, keepdims=True) + EPS)
    n = x * rstd
    return n * g, n, rstd


def _rms_bwd(dy, n, rstd, g):
    dg = jnp.sum(dy * n, axis=0, keepdims=True)
    dn = dy * g
    dx = rstd * (dn - n * jnp.mean(dn * n, axis=-1, keepdims=True))
    return dx, dg


def _norm_bwd(dy, x, g):
    _, n, rstd = _rms(x, g)
    return _rms_bwd(dy, n, rstd, g)


def _colsum(v):
    return jnp.sum(v, axis=0, keepdims=True)


def _silu(x):
    return x * (1.0 / (1.0 + jnp.exp(-x)))


def _lane(shape):
    return lax.broadcasted_iota(jnp.int32, shape, 1)


def _group_mean(v, groups):
    i = lax.broadcasted_iota(jnp.int32, (LANES, LANES), 0)
    j = lax.broadcasted_iota(jnp.int32, (LANES, LANES), 1)
    g = jnp.zeros((LANES, LANES), F32)
    for lo, hi in groups:
        g = jnp.where((i >= lo) & (i < hi) & (j >= lo) & (j < hi), 1.0 / (hi - lo), g)
    head = v.astype(MXU_DTYPE)
    return _dot(head, g) + _dot(v - head.astype(F32), g)


def _in_groups(shape, groups):
    lane = _lane(shape)
    m = jnp.zeros(shape, jnp.bool_)
    for lo, hi in groups:
        m = m | ((lane >= lo) & (lane < hi))
    return m


def _grms(x, g, groups):
    rstd = lax.rsqrt(_group_mean(x * x, groups) + EPS)
    n = jnp.where(_in_groups(x.shape, groups), x * rstd, 0.0)
    return n * g, n, rstd


def _grms_bwd(dy, n, rstd, g, groups):
    dn = dy * g
    return rstd * (dn - n * _group_mean(dn * n, groups)), _colsum(dy * n)


def _rot(x, half, transpose=False):
    first = (_lane(x.shape) % (2 * half)) < half
    up = pltpu.roll(x, LANES - half, axis=1)
    down = pltpu.roll(x, half, axis=1)
    return jnp.where(first, up, -down) if transpose else jnp.where(first, -up, down)


def _rope(x, cos, sin, half):
    return x * cos + _rot(x, half) * sin


def _rope_bwd(dy, cos, sin, half):
    return dy * cos + _rot(dy * sin, half, transpose=True)


def _chunks(x):
    return [x[:, i:i + LANES] for i in range(0, x.shape[1], LANES)]


Q_GROUPS = ((0, NOPE), (NOPE, NOPE + ROPE))
K_GROUPS = ((0, NOPE),)
KPE_GROUPS = ((KPE_LO, KPE_LO + ROPE),)
DIL_GROUPS = ((0, DIL_DIM), (DIL_DIM, 2 * DIL_DIM))


def _col(width, rows=SEQ):
    return pl.BlockSpec((rows, width), lambda h: (0, h))


def _causal_tail(s, tq, fill):
    diag = s[:, s.shape[1] - tq:]
    keep = lax.broadcasted_iota(jnp.int32, diag.shape, 1) <= lax.broadcasted_iota(jnp.int32, diag.shape, 0)
    diag = jnp.where(keep, diag, fill)
    return diag if s.shape[1] == tq else jnp.concatenate([s[:, :s.shape[1] - tq], diag], axis=1)


def mla_fwd(name, q, k, v, scale, tq=256):
    S = q.shape[0]

    def body(q_ref, k_ref, v_ref, o_ref, lse_ref):
        nb = S // tq
        blk = lambda i: slice(i * tq, (i + 1) * tq)

        def scores(i):
            return _dot(q_ref[blk(i), :], k_ref[:(i + 1) * tq, :], "nt")

        def softmax(i, s):
            s = _causal_tail(s * scale, tq, NEG_INF)
            m = jnp.max(s, axis=-1, keepdims=True)
            e = jnp.exp(s - m)
            l = jnp.sum(e, axis=-1, keepdims=True)
            lse_ref[0, blk(i), :] = m + jnp.log(l)
            return (e * (1.0 / l)).astype(MXU_DTYPE)

        def weighted(i, p):
            o_ref[blk(i), :] = _dot(p, v_ref[:(i + 1) * tq, :])

        s, p_prev = scores(0), None
        for i in range(nb):
            s_next = scores(i + 1) if i + 1 < nb else None
            if p_prev is not None:
                weighted(i - 1, p_prev)
            p_prev, s = softmax(i, s), s_next
        weighted(nb - 1, p_prev)

    return pl.pallas_call(
        body, name=name, grid=(HEADS,), in_specs=[_col(LANES)] * 3,
        out_specs=[_col(LANES), pl.BlockSpec((1, S, 1), lambda h: (h, 0, 0))],
        out_shape=[jax.ShapeDtypeStruct((S, MIX_IN), F32), jax.ShapeDtypeStruct((HEADS, S, 1), F32)],
        compiler_params=_params())(q, k, v)


def mla_bwd(name, q, k, v, o, do, lse, scale, tq=256):
    S = q.shape[0]

    def body(q_ref, k_ref, v_ref, o_ref, do_ref, lse_ref, dq_ref, dkv_ref, dkpe_ref, dk_acc, dv_acc):
        dk_acc[...] = jnp.zeros_like(dk_acc)
        dv_acc[...] = jnp.zeros_like(dv_acc)
        for i in range(S // tq):
            kext = (i + 1) * tq
            blk = slice(i * tq, kext)
            qi, kk, vv = q_ref[blk, :], k_ref[:kext, :], v_ref[:kext, :]
            doi = do_ref[blk, :]
            s = _causal_tail(_dot(qi, kk, "nt") * scale, tq, NEG_INF)
            p = jnp.exp(s - lse_ref[0, blk, :])
            dp = _dot(doi, vv, "nt")
            delta = jnp.sum(doi * o_ref[blk, :], axis=-1, keepdims=True)
            ds = p * (dp - delta) * scale
            dq_ref[blk, :] = _dot(ds, kk)
            dk_acc[:kext, :] += _dot(ds, qi, "tn")
            dv_acc[:kext, :] += _dot(p, doi, "tn")
        dk = dk_acc[...]
        lane = _lane(dk.shape)
        dkv_ref[...] = jnp.where(lane < NOPE, dk, 0.0) + dv_acc[...]
        dkpe = jnp.where((lane >= KPE_LO) & (lane < KPE_LO + ROPE), dk, 0.0)
        h = pl.program_id(0)

        @pl.when(h == 0)
        def _():
            dkpe_ref[...] = dkpe

        @pl.when(h > 0)
        def _():
            dkpe_ref[...] += dkpe

    return pl.pallas_call(
        body, name=name, grid=(HEADS,),
        in_specs=[_col(LANES)] * 5 + [pl.BlockSpec((1, S, 1), lambda h: (h, 0, 0))],
        out_specs=[_col(LANES), _col(LANES), pl.BlockSpec((S, LANES), lambda h: (0, 0))],
        out_shape=[jax.ShapeDtypeStruct((S, HEADS * LANES), F32), jax.ShapeDtypeStruct((S, HEADS * LANES), F32),
                   jax.ShapeDtypeStruct((S, LANES), F32)],
        scratch_shapes=[pltpu.VMEM((S, LANES), F32), pltpu.VMEM((S, LANES), F32)],
        compiler_params=_params())(q, k, v, o, do, lse)


BAND_TQ = SPAN


def _band_blocks(L, tq):
    return [(i * tq, (i + 1) * tq, max(0, i * tq - SPAN)) for i in range(L // tq)]


def _class_rows(r, dil, lo, hi):
    return pl.ds(r + dil * lo, hi - lo, stride=dil) if dil > 1 else pl.ds(lo, hi - lo)


def _stack_heads(t, lo):
    zero = jnp.zeros_like(t)
    return jnp.concatenate([jnp.where(lo, t, zero), jnp.where(lo, zero, t)], axis=0)


def _band_mask2(q0, q1, k0):
    n = q1 - q0
    shape = (2 * n, q1 - k0)
    i = lax.broadcasted_iota(jnp.int32, shape, 0)
    dist = (jnp.where(i >= n, i - n, i) + q0) - (lax.broadcasted_iota(jnp.int32, shape, 1) + k0)
    return (dist >= 0) & (dist <= SPAN)


def _pair_col(col0=0):
    return pl.BlockSpec((SEQ, LANES), lambda j: (0, col0 // LANES + j))


def band_fwd(name, q, k, v, dil, dep=None):
    S = q.shape[0]
    L = S // dil
    tq = BAND_TQ
    scale = DIL_DIM ** -0.5
    deps = [] if dep is None else [dep]

    def body(q_ref, k_ref, v_ref, *rest):
        o_ref, lse_ref = rest[len(deps):]
        items = [(r, blk) for r in range(dil) for blk in _band_blocks(L, tq)]
        lo = _lane((tq, LANES)) < DIL_DIM

        def scores(item):
            r, (q0, q1, k0) = item
            qb = q_ref[_class_rows(r, dil, q0, q1), :].astype(MXU_DTYPE)
            return _dot(_stack_heads(qb, lo), k_ref[_class_rows(r, dil, k0, q1), :], "nt")

        def softmax(item, s):
            _, (q0, q1, k0) = item
            s = jnp.where(_band_mask2(q0, q1, k0), s * scale, NEG_INF)
            mx = jnp.max(s, axis=-1, keepdims=True)
            e = jnp.exp(s - mx)
            l = jnp.sum(e, axis=-1, keepdims=True)
            return (e * (1.0 / l)).astype(MXU_DTYPE), mx + jnp.log(l)

        def weighted(item, p, lse):
            r, (q0, q1, k0) = item
            pv = _dot(p, v_ref[_class_rows(r, dil, k0, q1), :])
            o_ref[_class_rows(r, dil, q0, q1), :] = jnp.where(lo, pv[:tq], pv[tq:])
            lse_ref[_class_rows(r, dil, q0, q1), :] = jnp.where(lo, lse[:tq], lse[tq:])

        s, prev = scores(items[0]), None
        for i, item in enumerate(items):
            s_next = scores(items[i + 1]) if i + 1 < len(items) else None
            if prev is not None:
                weighted(items[i - 1], *prev)
            prev, s = softmax(item, s), s_next
        weighted(items[-1], *prev)

    return pl.pallas_call(
        body, name=name, grid=(DIL_WIDTH // LANES,),
        in_specs=[_pair_col()] * 2 + [_pair_col(P_VD)] + [pl.BlockSpec(memory_space=pl.ANY)] * len(deps), out_specs=[_pair_col()] * 2,
        out_shape=[jax.ShapeDtypeStruct((S, DIL_WIDTH), F32)] * 2, compiler_params=_params())(q, k, v, *deps)


def band_bwd(name, q, k, v, lse, lse_mix, o_cat, do_cat, dil, before=None):
    S = q.shape[0]
    L = S // dil
    tq = BAND_TQ
    scale = DIL_DIM ** -0.5
    before = list(before or [])

    def body(q_ref, k_ref, v_ref, lse_ref, mix_ref, o_ref, do_ref, *rest):
        dq_ref, dk_ref, dv_ref = rest[len(before):]
        if before:
            dq0_ref, dk0_ref, dv0_ref = rest[:3]
            dk_ref[...] = dk0_ref[...]
            dv_ref[...] = dv0_ref[...]
        else:
            dk_ref[...] = jnp.zeros_like(dk_ref)
            dv_ref[...] = jnp.zeros_like(dv_ref)
        items = [(r, blk) for r in range(dil) for blk in _band_blocks(L, tq)]
        lo = _lane((tq, LANES)) < DIL_DIM
        per_head = lambda t: jnp.concatenate([t[:, 0:1], t[:, DIL_DIM:DIL_DIM + 1]], axis=0)

        def scores(item):
            r, (q0, q1, k0) = item
            qrows, krows = _class_rows(r, dil, q0, q1), _class_rows(r, dil, k0, q1)
            lse_p, dout = lse_ref[qrows, :], do_ref[qrows, :]
            w2 = per_head(jnp.exp(lse_p - mix_ref[qrows, :]))
            dd = dout * o_ref[qrows, :]
            big_d = jnp.concatenate([jnp.sum(jnp.where(lo, dd, 0.0), axis=-1, keepdims=True),
                                     jnp.sum(jnp.where(lo, 0.0, dd), axis=-1, keepdims=True)], axis=0)
            q2 = _stack_heads(q_ref[qrows, :].astype(MXU_DTYPE), lo)
            dom = (_stack_heads(dout, lo) * w2).astype(MXU_DTYPE)
            return (_dot(q2, k_ref[krows, :], "nt"), _dot(dom, v_ref[krows, :], "nt"), per_head(lse_p), w2 * big_d, q2, dom)

        def softmax_bwd(item, s, dp, lse2, wd2, q2, dom):
            _, (q0, q1, k0) = item
            p = jnp.where(_band_mask2(q0, q1, k0), jnp.exp(s * scale - lse2), 0.0)
            return p.astype(MXU_DTYPE), (p * (dp - wd2) * scale).astype(MXU_DTYPE), q2, dom

        def grads(item, p, ds, q2, dom):
            r, (q0, q1, k0) = item
            qrows, krows = _class_rows(r, dil, q0, q1), _class_rows(r, dil, k0, q1)
            dq2 = _dot(ds, k_ref[krows, :])
            dq = jnp.where(lo, dq2[:tq], dq2[tq:])
            dq_ref[qrows, :] = dq + dq0_ref[qrows, :] if before else dq
            dk_ref[krows, :] += _dot(ds, q2, "tn")
            dv_ref[krows, :] += _dot(p, dom, "tn")

        sc, prev = scores(items[0]), None
        for i, item in enumerate(items):
            sc_next = scores(items[i + 1]) if i + 1 < len(items) else None
            if prev is not None:
                grads(items[i - 1], *prev)
            prev, sc = softmax_bwd(item, *sc), sc_next
        grads(items[-1], *prev)

    cat = _pair_col(HEADS * LANES)
    return pl.pallas_call(
        body, name=name, grid=(DIL_WIDTH // LANES,),
        in_specs=[_pair_col()] * 2 + [_pair_col(P_VD)] + [_pair_col()] * 2 + [cat] * 2 + [_pair_col()] * len(before),
        out_specs=[_pair_col()] * 3, out_shape=[jax.ShapeDtypeStruct((S, DIL_WIDTH), F32)] * 3,
        compiler_params=_params())(q, k, v, lse, lse_mix, o_cat, do_cat, *before)


def combine_fwd(name, outs, lses, o_cat, tm=512):
    S = outs[0].shape[0]

    def body(o1, o2, o3, l1, l2, l3, cat_in, cat_out, mix_ref):
        ls = [l1[...], l2[...], l3[...]]
        m = jnp.maximum(jnp.maximum(ls[0], ls[1]), ls[2])
        e = [jnp.exp(l - m) for l in ls]
        den = e[0] + e[1] + e[2]
        cat_out[...] = (e[0] / den) * o1[...] + (e[1] / den) * o2[...] + (e[2] / den) * o3[...]
        mix_ref[...] = m + jnp.log(den)

    row = pl.BlockSpec((tm, DIL_WIDTH), lambda i: (i, 0))
    return pl.pallas_call(
        body, name=name, grid=(S // tm,), in_specs=[row] * 6 + [pl.BlockSpec(memory_space=pl.ANY)],
        out_specs=[pl.BlockSpec((tm, DIL_WIDTH), lambda i: (i, HEADS * LANES // DIL_WIDTH)), row],
        out_shape=[jax.ShapeDtypeStruct(o_cat.shape, F32), jax.ShapeDtypeStruct((S, DIL_WIDTH), F32)],
        input_output_aliases={6: 0}, compiler_params=_params())(*outs, *lses, o_cat)


FFN_FWD_ROWS = 512
FFN_BWD_ROWS = 256
CONV_PAD = SUBLANES


def _window(x, k):
    groups = x.reshape(-1, SUBLANES, x.shape[1])
    turned = pltpu.roll(groups, SUBLANES - k, axis=1)
    stays = lax.broadcasted_iota(jnp.int32, (groups.shape[0] - 1,) + groups.shape[1:], 1) < SUBLANES - k
    return jnp.where(stays, turned[:-1], turned[1:]).reshape(-1, x.shape[1])


def _earlier(ref, r0, rows, n):
    if r0 == 0:
        x = jnp.concatenate([jnp.zeros((SUBLANES, ref.shape[1]), F32), ref[:rows, :]], axis=0)
    else:
        x = ref[r0 - SUBLANES:r0 + rows, :]
    return _window(x, SUBLANES - n)


CONV_TC = 256
CONV_NB = D_FF // CONV_TC


def _half_specs(rows, rows_axis=False):
    if rows_axis:
        return [pl.BlockSpec((rows, D_MODEL), lambda j: (j, 0)), pl.BlockSpec((rows, D_MODEL), lambda j: (j + CONV_NB, 0))]
    return [pl.BlockSpec((rows, CONV_TC), lambda j: (0, j)), pl.BlockSpec((rows, CONV_TC), lambda j: (0, j + CONV_NB))]


def _whole(a):
    return pl.BlockSpec(a.shape, lambda j: (0,) * a.ndim)


def _up_pair(h, ug_ref, uv_ref):
    return jnp.concatenate([_dot(h, ug_ref[...], "nt"), _dot(h, uv_ref[...], "nt")], axis=1)


def _conv_taps(up_ref, r0, rows, w, b):
    uin, u1, u2 = up_ref[r0:r0 + rows, :], _earlier(up_ref, r0, rows, 1), _earlier(up_ref, r0, rows, 2)
    return uin, u1, u2, w[2:3, :] * uin + w[1:2, :] * u1 + w[0:1, :] * u2 + b


def ffn_fwd(name, h, w_up_t, w_conv, b_conv, w_down):
    S = h.shape[0]

    def body(h_ref, ug_ref, uv_ref, wg_ref, wv_ref, bg_ref, bv_ref, wd_ref, dn_ref, up_ref):
        @pl.when(pl.program_id(0) == 0)
        def _():
            dn_ref[...] = jnp.zeros_like(dn_ref)

        w = jnp.concatenate([wg_ref[...], wv_ref[...]], axis=1)
        b = jnp.concatenate([bg_ref[...], bv_ref[...]], axis=1)
        rows = FFN_FWD_ROWS
        starts = list(range(0, S, rows))

        def project(r0):
            up_ref[r0:r0 + rows, :] = _up_pair(h_ref[r0:r0 + rows, :], ug_ref, uv_ref)

        def gate(r0):
            u = _conv_taps(up_ref, r0, rows, w, b)[3]
            return (_silu(u[:, :CONV_TC]) * u[:, CONV_TC:]).astype(MXU_DTYPE)

        def project_down(r0, act):
            dn_ref[r0:r0 + rows, :] += _dot(act, wd_ref[...])

        project(starts[0])
        act_prev = None
        for i, r0 in enumerate(starts):
            if i + 1 < len(starts):
                project(starts[i + 1])
            if act_prev is not None:
                project_down(starts[i - 1], act_prev)
            act_prev = gate(r0)
        project_down(starts[-1], act_prev)

    return pl.pallas_call(
        body, name=name, grid=(CONV_NB,),
        in_specs=[_whole(h)] + _half_specs(CONV_TC, rows_axis=True) + _half_specs(3) + _half_specs(1)
        + [pl.BlockSpec((CONV_TC, w_down.shape[1]), lambda j: (j, 0))],
        out_specs=[pl.BlockSpec((S, w_down.shape[1]), lambda j: (0, 0)), pl.BlockSpec((S, 2 * CONV_TC), lambda j: (0, j))],
        out_shape=[jax.ShapeDtypeStruct((S, w_down.shape[1]), F32), jax.ShapeDtypeStruct((S, 2 * D_FF), F32)],
        compiler_params=_params())(h, w_up_t, w_up_t, w_conv, w_conv, b_conv, b_conv, w_down)


def ffn_bwd(name, h, up, w_up_t, w_conv, b_conv, d_dn, w_down):
    S, D = h.shape

    def body(h_ref, up_ref, ug_ref, uv_ref, wg_ref, wv_ref, bg_ref, bv_ref, dd_ref, wd_ref,
             dh_ref, gup_ref, gd_ref, dwg_ref, dwv_ref, dbg_ref, dbv_ref, du_ref, dup_ref, act_ref):
        @pl.when(pl.program_id(0) == 0)
        def _():
            dh_ref[...] = jnp.zeros_like(dh_ref)

        w = jnp.concatenate([wg_ref[...], wv_ref[...]], axis=1)
        b = jnp.concatenate([bg_ref[...], bv_ref[...]], axis=1)
        w_pair = jnp.concatenate([ug_ref[...], uv_ref[...]], axis=0)
        rows = FFN_BWD_ROWS
        starts = list(range(0, S, rows))
        du_ref[S:S + CONV_PAD, :] = jnp.zeros((CONV_PAD, 2 * CONV_TC), F32)

        def project(r0):
            return _dot(dd_ref[r0:r0 + rows, :], wd_ref[...], "nt")

        def through_conv(r0, da):
            uin, u1, u2, u = _conv_taps(up_ref, r0, rows, w, b)
            gate, val = u[:, :CONV_TC], u[:, CONV_TC:]
            sig = 1.0 / (1.0 + jnp.exp(-gate))
            du = jnp.concatenate([da * val * (sig * (1.0 + gate * (1.0 - sig))), da * (gate * sig)], axis=1)
            du_ref[r0:r0 + rows, :] = du
            act_ref[r0:r0 + rows, :] = (gate * sig * val).astype(MXU_DTYPE)
            dw = jnp.concatenate([_colsum(du * u2), _colsum(du * u1), _colsum(du * uin)], axis=0)
            return dw, _colsum(du)

        def back_up(r0):
            du = du_ref[r0:r0 + rows + CONV_PAD, :]
            dup = (w[2:3, :] * du[:rows] + w[1:2, :] * _window(du, 1) + w[0:1, :] * _window(du, 2)).astype(MXU_DTYPE)
            dup_ref[r0:r0 + rows, :] = dup
            dh_ref[r0:r0 + rows, :] += _dot(dup, w_pair)

        dw, db = 0.0, 0.0
        da = project(starts[0])
        for i, r0 in enumerate(starts):
            da_next = project(starts[i + 1]) if i + 1 < len(starts) else None
            dw_c, db_c = through_conv(r0, da)
            if i > 0:
                back_up(starts[i - 1])
            dw, db, da = dw + dw_c, db + db_c, da_next
        back_up(starts[-1])
        g_up, g_dn = _dot(dup_ref[...], h_ref[...], "tn"), _dot(act_ref[...], dd_ref[...], "tn")
        gup_ref[0], gup_ref[1] = g_up[:CONV_TC].astype(gup_ref.dtype), g_up[CONV_TC:].astype(gup_ref.dtype)
        gd_ref[...] = g_dn.astype(gd_ref.dtype)
        dwg_ref[...], dwv_ref[...] = dw[:, :CONV_TC], dw[:, CONV_TC:]
        dbg_ref[...], dbv_ref[...] = db[:, :CONV_TC], db[:, CONV_TC:]

    half = lambda rows: pl.BlockSpec((rows, CONV_TC), lambda j: (0, j))
    rows_blk = pl.BlockSpec((CONV_TC, D), lambda j: (j, 0))
    dh, gup, gd, dwg, dwv, dbg, dbv = pl.pallas_call(
        body, name=name, grid=(CONV_NB,),
        in_specs=[_whole(h), pl.BlockSpec((S, 2 * CONV_TC), lambda j: (0, j))] + _half_specs(CONV_TC, rows_axis=True) + _half_specs(3)
        + _half_specs(1) + [_whole(d_dn), rows_blk],
        out_specs=[pl.BlockSpec((S, D), lambda j: (0, 0)), pl.BlockSpec((2, CONV_TC, D), lambda j: (0, j, 0)), rows_blk,
                   half(3), half(3), half(1), half(1)],
        out_shape=[jax.ShapeDtypeStruct((S, D), F32), jax.ShapeDtypeStruct((2, D_FF, D), MXU_DTYPE),
                   jax.ShapeDtypeStruct((D_FF, D), MXU_DTYPE)]
        + [jax.ShapeDtypeStruct((3, D_FF), F32)] * 2 + [jax.ShapeDtypeStruct((1, D_FF), F32)] * 2,
        scratch_shapes=[pltpu.VMEM((S + CONV_PAD, 2 * CONV_TC), F32), pltpu.VMEM((S, 2 * CONV_TC), MXU_DTYPE),
                        pltpu.VMEM((S, CONV_TC), MXU_DTYPE)],
        compiler_params=_params())(h, up, w_up_t, w_up_t, w_conv, w_conv, b_conv, b_conv, d_dn, w_down)
    return dh, gup.reshape(2 * D_FF, D), gd, jnp.concatenate([dwg, dwv], axis=1), jnp.concatenate([dbg, dbv], axis=1)


def adamw(name, w, parts, m, v, tr=None):
    apart = w.ndim == 3
    R, C = w.shape[0], w.shape[-1]
    tr = tr or R
    assert R % tr == 0
    c1 = 1.0 - ADAM_B1 ** ADAM_STEP
    c2 = 1.0 - ADAM_B2 ** ADAM_STEP
    np_ = len(parts)

    def body(*refs):
        w_ref, m_ref, v_ref = refs[0], refs[1 + np_], refs[2 + np_]
        go_ref, d_ref, mo_ref, vo_ref = refs[3 + np_:]
        terms = []
        for part, ref in zip(parts, refs[1:1 + np_], strict=True):
            terms += [ref[...]] if part.ndim == 2 else [ref[p] for p in range(part.shape[0])]
        g = terms[0].astype(F32)
        for term in terms[1:]:
            g = g + term.astype(F32)
        m2 = ADAM_B1 * m_ref[...] + (1.0 - ADAM_B1) * g
        v2 = ADAM_B2 * v_ref[...] + (1.0 - ADAM_B2) * (g * g)
        go_ref[...] = g
        mo_ref[...] = m2
        vo_ref[...] = v2
        d_ref[...] = -ADAM_LR * ((m2 / c1) / (jnp.sqrt(v2 / c2) + ADAM_EPS) + ADAM_WD * w_ref[...])

    blk = pl.BlockSpec((tr, C), lambda i: (i, 0))
    own = pl.BlockSpec((tr, None, C), lambda i: (i, 0, 0)) if apart else blk
    part_specs = [blk if p.ndim == 2 else pl.BlockSpec((p.shape[0], tr, C), lambda i: (0, i, 0)) for p in parts]
    return pl.pallas_call(
        body, name=name, grid=(R // tr,),
        in_specs=[own] + part_specs + [own, own], out_specs=[own] * 4,
        out_shape=[jax.ShapeDtypeStruct(w.shape, F32)] * 4, compiler_params=_params())(w, *parts, m, v)


def _place():
    return lax.axis_index("x"), lax.axis_index("y"), lax.axis_index("c")


def ada_modulation(name, c, w_ada, after=()):
    n_mod = w_ada.shape[1]

    def exchange(src_ref, dst_ref, send_sems, recv_sems):
        x, y, c_ = _place()
        me = 4 * x + 2 * y + c_
        copies = []
        for k in range(1, N_DEV):
            px, py, pc = x ^ (k >> 2), y ^ ((k >> 1) & 1), c_ ^ (k & 1)
            copies.append(pltpu.make_async_remote_copy(
                src_ref=src_ref, dst_ref=dst_ref.at[me], send_sem=send_sems.at[k - 1], recv_sem=recv_sems.at[k - 1],
                device_id=(px, py, pc), device_id_type=MESH_ID))
        for cp in copies:
            cp.start()
        for cp in copies:
            cp.wait_recv()
        for cp in copies:
            cp.wait_send()
        return me

    def body(c_ref, w_ref, *refs):
        sc_ref, mod_ref, c_all, send_c, recv_c, send_m, recv_m = refs[len(after):]
        me = exchange(c_ref, c_all, send_c, recv_c)
        c_all[me] = c_ref[...]
        sc = _silu(jnp.concatenate([c_all[p] for p in range(N_DEV)], axis=0))
        sc_ref[...] = sc.astype(sc_ref.dtype)
        mod_ref[me] = _dot(sc, w_ref[...])
        exchange(mod_ref.at[me], mod_ref, send_m, recv_m)

    vmem = pl.BlockSpec(memory_space=pltpu.VMEM)
    return pl.pallas_call(
        body, name=name, in_specs=[vmem, vmem] + [pl.BlockSpec(memory_space=pl.ANY)] * len(after), out_specs=[vmem, vmem],
        out_shape=[jax.ShapeDtypeStruct((N_DEV, c.shape[1]), MXU_DTYPE), jax.ShapeDtypeStruct((N_DEV, N_DEV, n_mod), F32)],
        scratch_shapes=[pltpu.VMEM((N_DEV, 1, c.shape[1]), F32)] + [pltpu.SemaphoreType.DMA((N_DEV - 1,))] * 4,
        compiler_params=pltpu.CompilerParams(has_side_effects=True, vmem_limit_bytes=VMEM_LIMIT))(c, w_ada, *after)


HBM_SPEC = pl.BlockSpec(memory_space=pltpu.HBM)
SEM_SPEC = pl.BlockSpec(memory_space=pltpu.SEMAPHORE)
DATAFLOW = pltpu.SideEffectType.DATAFLOW_SIDE_EFFECTING


def _exchange_copies(srcs, lands, send_sems, recv_sems, gather, first=0):
    x, y, c = _place()
    me = 4 * x + 2 * y + c
    out = []
    for t, (src, land) in enumerate(zip(srcs, lands, strict=True)):
        for k in range(1, N_DEV):
            px, py, pc = x ^ (k >> 2), y ^ ((k >> 1) & 1), c ^ (k & 1)
            sem = 7 * (first + t) + k - 1
            out.append((k, pltpu.make_async_remote_copy(
                src_ref=src if gather else src.at[4 * px + 2 * py + pc],
                dst_ref=land.at[me] if gather else land.at[k - 1],
                send_sem=send_sems.at[sem], recv_sem=recv_sems.at[sem],
                device_id=(px, py, pc), device_id_type=MESH_ID)))
    return out


def _own_copies(srcs, lands, send_sems, gather, first=0):
    x, y, c = _place()
    me = 4 * x + 2 * y + c
    total = send_sems.shape[0] // N_DEV
    return [pltpu.make_async_copy(src if gather else src.at[me], land.at[me] if gather else land.at[N_DEV - 1],
                                  send_sems.at[7 * total + first + t])
            for t, (src, land) in enumerate(zip(srcs, lands, strict=True))]


TREE_DIRECT = (1, 2, 4, 6)
TREE_FORWARDED = (3, 5, 7)


def exchange_start(name, arrs, gather, after=None, tree=False):
    n = len(arrs)
    lands = [lax.empty((N_DEV,) + (a.shape if gather else a.shape[1:]), a.dtype) for a in arrs]
    deps = [] if after is None else [after]

    def body(*refs):
        srcs, land_refs = refs[:n], refs[n:2 * n]
        send_sems, recv_sems = refs[2 * n + len(deps)], refs[2 * n + len(deps) + 1]
        token = refs[-1]
        for k, cp in _exchange_copies(srcs, land_refs, send_sems, recv_sems, gather):
            if not tree or k in TREE_DIRECT:
                cp.start()
        for cp in _own_copies(srcs, land_refs, send_sems, gather):
            cp.start()
        token[...] = jnp.zeros_like(token)

    hbm = lambda a: pltpu.HBM(a.shape, a.dtype)
    res = pl.pallas_call(
        body, name=name,
        out_shape=(pltpu.SemaphoreType.DMA((N_DEV * n,)), pltpu.SemaphoreType.DMA((7 * n,)), *[hbm(a) for a in arrs],
                   *[hbm(l) for l in lands], jax.ShapeDtypeStruct((8, 128), F32)),
        in_specs=[HBM_SPEC] * (2 * n) + [pl.BlockSpec(memory_space=pl.ANY)] * len(deps),
        out_specs=(SEM_SPEC, SEM_SPEC, *[HBM_SPEC] * (2 * n), pl.BlockSpec(memory_space=pltpu.VMEM)),
        input_output_aliases={i: 2 + i for i in range(2 * n)},
        compiler_params=pltpu.CompilerParams(has_side_effects=DATAFLOW),
    )(*[pltpu.with_memory_space_constraint(a, pltpu.HBM) for a in arrs + lands], *deps)
    return res[0], res[1], list(res[2:2 + n]), list(res[2 + n:2 + 2 * n]), res[-1]


def exchange_forward(name, started, after, first=0, count=None):
    send_sems, recv_sems, srcs, lands, _ = started
    count = len(srcs) - first if count is None else count
    mine = lands[first:first + count]
    n = len(mine)

    def copies(land_refs, send_ref, recv_ref):
        x, y, c = _place()
        out = []
        for t, land in enumerate(land_refs):
            for k in (2, 4, 6):
                slot = land.at[4 * (x ^ (k >> 2)) + 2 * (y ^ ((k >> 1) & 1)) + c]
                came, goes = 7 * (first + t) + k - 1, 7 * (first + t) + (k ^ 1) - 1
                out.append((
                    pltpu.make_async_remote_copy(src_ref=slot, dst_ref=slot, send_sem=send_ref.at[came], recv_sem=recv_ref.at[came],
                                                 device_id=(x, y, c), device_id_type=MESH_ID),
                    pltpu.make_async_remote_copy(src_ref=slot, dst_ref=slot, send_sem=send_ref.at[goes], recv_sem=recv_ref.at[goes],
                                                 device_id=(x, y, 1 - c), device_id_type=MESH_ID)))
        return out

    after = list(after) if isinstance(after, (list, tuple)) else [after]

    def arrived(*refs):
        for came, _ in copies(refs[:n], refs[n], refs[n + 1]):
            came.wait_recv()

    def pass_on(*refs):
        for _, goes in copies(refs[:n], refs[n], refs[n + 1]):
            goes.start()
        refs[-1][...] = jnp.zeros_like(refs[-1])

    hbm = lambda a: pltpu.HBM(a.shape, a.dtype)
    here = pl.pallas_call(
        arrived, name=name + "_arrived", out_shape=tuple(hbm(a) for a in mine),
        in_specs=[HBM_SPEC] * n + [SEM_SPEC, SEM_SPEC] + [pl.BlockSpec(memory_space=pl.ANY)] * len(after),
        out_specs=tuple([HBM_SPEC] * n), input_output_aliases={i: i for i in range(n)},
        compiler_params=pltpu.CompilerParams(has_side_effects=DATAFLOW),
    )(*mine, send_sems, recv_sems, *after)
    res = pl.pallas_call(
        pass_on, name=name, out_shape=(*[hbm(a) for a in mine], jax.ShapeDtypeStruct((8, 128), F32)),
        in_specs=[HBM_SPEC] * n + [SEM_SPEC, SEM_SPEC],
        out_specs=(*[HBM_SPEC] * n, pl.BlockSpec(memory_space=pltpu.VMEM)), input_output_aliases={i: i for i in range(n)},
        compiler_params=pltpu.CompilerParams(has_side_effects=DATAFLOW),
    )(*here, send_sems, recv_sems)
    lands = lands[:first] + list(res[:n]) + lands[first + count:]
    return (send_sems, recv_sems, srcs, lands, res[-1])


def exchange_wait(name, started, gather, after, first=0, count=None, tree=False):
    send_sems, recv_sems, srcs, lands, _ = started
    count = len(srcs) - first if count is None else count
    srcs, lands = srcs[first:first + count], lands[first:first + count]
    n = len(srcs)

    def body(*refs):
        src_refs, land_refs = refs[:n], refs[n:2 * n]
        copies = _exchange_copies(src_refs, land_refs, refs[2 * n], refs[2 * n + 1], gather, first)
        for _, cp in copies:
            cp.wait_send()
        for k, cp in copies:
            if not tree or k in (1,) + TREE_FORWARDED:
                cp.wait_recv()
        for cp in _own_copies(src_refs, land_refs, refs[2 * n], gather, first):
            cp.wait()

    hbm = lambda a: pltpu.HBM(a.shape, a.dtype)
    res = pl.pallas_call(
        body, name=name, out_shape=tuple(hbm(a) for a in srcs + lands),
        in_specs=[HBM_SPEC] * (2 * n) + [SEM_SPEC, SEM_SPEC, pl.BlockSpec(memory_space=pl.ANY)],
        out_specs=tuple([HBM_SPEC] * (2 * n)), input_output_aliases={i: i for i in range(2 * n)},
        compiler_params=pltpu.CompilerParams(has_side_effects=DATAFLOW),
    )(*srcs, *lands, send_sems, recv_sems, after)
    return list(res[:n]), list(res[n:])


def _gather_cols(stack):
    p, k, n = stack.shape
    return stack.transpose(1, 0, 2).reshape(k, p * n)


def _scatter_cols(full):
    k, n = full.shape
    return full.reshape(k, N_DEV, n // N_DEV).transpose(1, 0, 2)


def _gather_rows(stack):
    p, r, n = stack.shape
    return stack.reshape(p * r, n)


def _scatter_rows(full):
    r, n = full.shape
    return full.reshape(N_DEV, r // N_DEV, n)


_IN_NAT = Q_LORA + KV_LORA
TRANSPOSED = ("w_in", "w_q_b", "w_up")
ROWS_APART = ("w_in", "w_conv")


def to_kernel_layout(name, w):
    if name == "w_in":
        z = lambda n: jnp.zeros((n, w.shape[1]), w.dtype)
        return jnp.concatenate([w[:_IN_NAT], z(KPE_LO), w[_IN_NAT:_IN_NAT + ROPE], z(LANES - KPE_LO - ROPE), w[_IN_NAT + ROPE:]], axis=0)
    if name == "w_q_b":
        return jnp.pad(w.reshape(HEADS, NOPE + ROPE, -1), ((0, 0), (0, LANES - NOPE - ROPE), (0, 0))).reshape(HEADS * LANES, -1)
    if name == "w_o":
        mla = jnp.pad(w[:HEADS * NOPE].reshape(HEADS, NOPE, -1), ((0, 0), (LANES - NOPE, 0), (0, 0))).reshape(HEADS * LANES, -1)
        return jnp.concatenate([mla, w[HEADS * NOPE:]], axis=0)
    return w


def from_kernel_layout(name, g):
    if name == "w_in":
        return jnp.concatenate([g[:_IN_NAT], g[P_KPE + KPE_LO:P_KPE + KPE_LO + ROPE], g[P_QD:]], axis=0)
    if name == "w_q_b":
        return g.reshape(HEADS, LANES, -1)[:, :NOPE + ROPE, :].reshape(HEADS * (NOPE + ROPE), -1)
    if name == "w_o":
        mla = g[:HEADS * LANES].reshape(HEADS, LANES, -1)[:, LANES - NOPE:, :].reshape(HEADS * NOPE, -1)
        return jnp.concatenate([mla, g[HEADS * LANES:]], axis=0)
    return g


SMALL_COLS = 1024
SMALL_ROWS = 24
SMALL_AT = {"loss": (0, 0, 1), "b_ada": (1, 0, 6 * D_MODEL), "g_mix_norm": (7, 0, D_MODEL), "g_q_lat": (8, 0, Q_LORA),
            "g_kv_lat": (9, 0, KV_LORA), "g_mla_q_nope": (10, 0, NOPE), "g_mla_q_pe": (10, 128, ROPE),
            "g_mla_k_nope": (10, 256, NOPE), "g_mla_k_pe": (10, 384, ROPE), "g_dil_q": (10, 512, DIL_DIM),
            "g_dil_k": (10, 640, DIL_DIM), "g_ffn_norm": (11, 0, D_MODEL), "b_conv": (12, 0, 2 * D_FF)}
SMALL_PARAMS = tuple(n for n in SMALL_AT if n != "loss")


def _pack_small(values):
    by_row = {}
    for name, (row, off, n) in SMALL_AT.items():
        by_row.setdefault(row, []).append((off, values[name].reshape(-1).astype(F32)))
    out = []
    for row in sorted(by_row):
        pieces, at = [], 0
        for off, v in sorted(by_row[row], key=lambda t: t[0]):
            pieces += [jnp.zeros((off - at,), F32), v]
            at = off + v.shape[0]
        flat = jnp.concatenate(pieces)
        nrows = -(-flat.shape[0] // SMALL_COLS)
        out.append(jnp.pad(flat, (0, nrows * SMALL_COLS - flat.shape[0])).reshape(nrows, SMALL_COLS))
    packed = jnp.concatenate(out, axis=0)
    return jnp.pad(packed, ((0, SMALL_ROWS - packed.shape[0]), (0, 0)))


def _adam(w, g, m, v):
    c1 = 1.0 - ADAM_B1 ** ADAM_STEP
    c2 = 1.0 - ADAM_B2 ** ADAM_STEP
    m2 = ADAM_B1 * m + (1.0 - ADAM_B1) * g
    v2 = ADAM_B2 * v + (1.0 - ADAM_B2) * (g * g)
    return -ADAM_LR * ((m2 / c1) / (jnp.sqrt(v2 / c2) + ADAM_EPS) + ADAM_WD * w), m2, v2


def adamw_small(name, stack, params):
    flat = [a for n in SMALL_PARAMS for a in params[n]]

    def body(stack_ref, *refs):
        ins, outs = refs[:len(flat)], refs[len(flat):]
        g_all = stack_ref[0]
        for p in range(1, N_DEV):
            g_all = g_all + stack_ref[p]
        outs[0][...] = g_all[0:1, 0:1]
        for i, pname in enumerate(SMALL_PARAMS):
            row, off, n = SMALL_AT[pname]
            w_ref, m_ref, v_ref = ins[3 * i:3 * i + 3]
            go_ref, d_ref, mo_ref, vo_ref = outs[1 + 4 * i:5 + 4 * i]
            for c0 in range(0, n, SMALL_COLS):
                cn = min(SMALL_COLS, n - c0)
                r = row + c0 // SMALL_COLS
                g = g_all[r:r + 1, off:off + cn]
                cols = (slice(None), slice(c0, c0 + cn))
                d, m2, v2 = _adam(w_ref[cols], g, m_ref[cols], v_ref[cols])
                go_ref[cols], d_ref[cols], mo_ref[cols], vo_ref[cols] = g, d, m2, v2

    whole = lambda a: pl.BlockSpec(a.shape, lambda: (0,) * a.ndim)
    out_shape = [jax.ShapeDtypeStruct((1, 1), F32)] + [jax.ShapeDtypeStruct(a.shape, F32) for n in SMALL_PARAMS for a in params[n][:1] * 4]
    res = pl.pallas_call(body, name=name, in_specs=[whole(stack)] + [whole(a) for a in flat],
                         out_specs=[pl.BlockSpec(s.shape, lambda s=s: (0,) * len(s.shape)) for s in out_shape],
                         out_shape=out_shape, compiler_params=_params())(stack, *flat)
    return res[0], {n: res[1 + 4 * i:5 + 4 * i] for i, n in enumerate(SMALL_PARAMS)}


def _local_step(x, pos, mod, target, w, fetch, emit, halfway=lambda after: None):
    S = SEQ
    sh1, sc1, g1, sh2, sc2, g2 = [mod[:, i * D_MODEL:(i + 1) * D_MODEL] for i in range(6)]
    zeros = lambda n: jnp.zeros((1, n), F32)
    g_q = jnp.concatenate([w["g_mla_q_nope"], w["g_mla_q_pe"], zeros(LANES - NOPE - ROPE)], axis=1)
    g_k = jnp.concatenate([w["g_mla_k_nope"], zeros(LANES - NOPE)], axis=1)
    g_kpe = jnp.concatenate([zeros(KPE_LO), w["g_mla_k_pe"], zeros(LANES - KPE_LO - ROPE)], axis=1)
    g_dq = jnp.concatenate([w["g_dil_q"]] * 2, axis=1)
    g_dk = jnp.concatenate([w["g_dil_k"]] * 2, axis=1)
    b_conv = w["b_conv"]

    def inv_freq(d):
        return jnp.power(ROPE_THETA, -2.0 * jnp.arange(d // 2, dtype=F32) / d)

    n_m, n_d = ROPE // 2, DIL_DIM // 2
    freqs = jnp.concatenate([inv_freq(ROPE), inv_freq(DIL_DIM), jnp.zeros((LANES - n_m - n_d,), F32)]).reshape(1, LANES)

    def tables_fn(rows, params):
        (p,), (f,) = rows, params
        c, s = jnp.cos(p * f), jnp.sin(p * f)
        one, zero = jnp.ones_like(c), jnp.zeros_like(c)
        mla = lambda t, fill: jnp.concatenate([fill[:, :KPE_LO], t[:, :n_m], t[:, :n_m], fill[:, :LANES - KPE_LO - ROPE]], axis=1)
        dil = lambda t: jnp.concatenate([t[:, n_m:n_m + n_d]] * 4, axis=1)
        return [mla(c, one), mla(s, zero), dil(c), dil(s)], []

    cos_m, sin_m, cos_d, sin_d = rowwise("rope_tables", tables_fn, [pos], [freqs], [(LANES, F32)] * 4)
    tables = [cos_m, sin_m, cos_d, sin_d]
    H_M, H_D = ROPE // 2, DIL_DIM // 2

    def ln1_fn(rows, params):
        (xv,), (g, sc, sh) = rows, params
        y, _, _ = _rms(xv, g)
        return [y * (1.0 + sc) + sh], []

    (h,) = rowwise("ln1_fwd", ln1_fn, [x], [w["g_mix_norm"], sc1, sh1], [(D_MODEL, MXU_DTYPE)], dep=sin_d)
    w_in = fetch("w_in", h)

    def proj_fn(rows, params):
        (hv, cm, sm, cd, sd), (w_t, gq, gkv, gkp, gdq, gdk) = rows, params
        pv = _dot(hv, w_t, "nt")
        kper = _rope(_grms(pv[:, P_KPE:P_QD], gkp, KPE_GROUPS)[0], cm, sm, H_M)
        qd = [_rope(_grms(c, gdq, DIL_GROUPS)[0], cd, sd, H_D) for c in _chunks(pv[:, P_QD:P_KD])]
        kd = [_rope(_grms(c, gdk, DIL_GROUPS)[0], cd, sd, H_D) for c in _chunks(pv[:, P_KD:P_VD])]
        return [pv, _rms(pv[:, P_QLAT:P_KVLAT], gq)[0], _rms(pv[:, P_KVLAT:P_KPE], gkv)[0], kper,
                jnp.concatenate(qd, axis=1), jnp.concatenate(kd, axis=1)], []

    post_params = [w["g_q_lat"], w["g_kv_lat"], g_kpe, g_dq, g_dk]
    proj, qln, kvn, kper, qd_r, kd_r = rowwise(
        "proj_fwd", proj_fn, [h] + tables, [w_in] + post_params,
        [(P_END, F32), (Q_LORA, MXU_DTYPE), (KV_LORA, MXU_DTYPE), (LANES, MXU_DTYPE)] + [(DIL_WIDTH, F32)] * 2, tm=256)
    w_q_b, w_kv_b = fetch("w_q_b", qln), fetch("w_kv_b", kvn)

    def mla_proj_fn(rows, params):
        (qlv, kvlv, kp, cm, sm), (wq_t, wkv, gq, gk) = rows, params
        qv, kvv = _dot(qlv, wq_t, "nt"), _dot(kvlv, wkv)
        value_lanes = _lane(kp.shape) >= NOPE
        qs, ks, vs = [], [], []
        for qc, kc in zip(_chunks(qv), _chunks(kvv), strict=True):
            qs.append(_rope(_grms(qc, gq, Q_GROUPS)[0], cm, sm, H_M))
            ks.append(_grms(kc, gk, K_GROUPS)[0] + kp)
            vs.append(jnp.where(value_lanes, kc, 0.0))
        return [qv, kvv] + [jnp.concatenate(t, axis=1) for t in (qs, ks, vs)], []

    q, kv, q_mla, k_mla, v_mla = rowwise(
        "mla_proj", mla_proj_fn, [qln, kvn, kper, cos_m, sin_m], [w_q_b, w_kv_b, g_q, g_k],
        [(HEADS * LANES, F32)] * 2 + [(HEADS * LANES, MXU_DTYPE)] * 3, tm=256)
    mla_scale = (NOPE + ROPE) ** -0.5
    o_cat, lse_mla = mla_fwd("mla_fwd", q_mla, k_mla, v_mla, mla_scale)
    passed = halfway(lse_mla)

    band = [band_fwd(f"band{dil}_fwd", qd_r, kd_r, proj, dil, dep=passed) for dil in DILATIONS]
    o_cat, lse_mix = combine_fwd("dil_combine", [b[0] for b in band], [b[1] for b in band], o_cat)
    w_o = fetch("w_o", o_cat)

    def mid_fn(rows, params):
        (ov, xv), (w_out, gate1, g, sc, sh) = rows, params
        mx = _dot(ov, w_out)
        x1 = xv + gate1 * mx
        y, _, _ = _rms(x1, g)
        return [mx, x1, y * (1.0 + sc) + sh], []

    mix, x1, h2 = rowwise("mix_fwd", mid_fn, [o_cat, x], [w_o, g1, w["g_ffn_norm"], sc2, sh2],
                          [(D_MODEL, F32), (D_MODEL, F32), (D_MODEL, MXU_DTYPE)], tm=256)
    w_up, w_conv, w_down = fetch("w_up", h2), fetch("w_conv", h2), fetch("w_down", h2)
    dn, up = ffn_fwd("ffn_fwd", h2, w_up, w_conv, b_conv, w_down)

    def final_fn(rows, params):
        (x1v, dnv, tgt), (gate2,) = rows, params
        r = x1v + gate2 * dnv - tgt
        dy = r * (1.0 / D_MODEL)
        loss = jnp.sum(_colsum(r * r), axis=-1, keepdims=True) * (0.5 / D_MODEL)
        return [dy, gate2 * dy], [loss, _colsum(dy * dnv)]

    dy, d_dn, loss, dg2 = rowwise("loss_head", final_fn, [x1, dn, target], [g2], [(D_MODEL, F32), (D_MODEL, MXU_DTYPE)],
                                  [1, D_MODEL])
    dh2, g_up, g_down, g_w_conv, g_b_conv = ffn_bwd("ffn_bwd", h2, up, w_up, w_conv, b_conv, d_dn, w_down)
    emit("w_down", g_down)
    emit("w_conv", g_w_conv)
    sent = emit("w_up", g_up)

    def mid_bwd_fn(rows, params):
        (dh2v, dyv, x1v, mx), (gate1, g, sc) = rows, params
        yn, n, rstd = _rms(x1v, g)
        dx_n, dg = _rms_bwd(dh2v * (1.0 + sc), n, rstd, g)
        dx1 = dyv + dx_n
        return [dx1, gate1 * dx1], [dg, _colsum(dh2v * yn), _colsum(dh2v), _colsum(dx1 * mx)]

    dx1, dmix, dg_ffn, dsc2, dsh2, dg1 = rowwise(
        "mid_bwd", mid_bwd_fn, [dh2, dy, x1, mix], [g1, w["g_ffn_norm"], sc2], [(D_MODEL, F32), (D_MODEL, MXU_DTYPE)],
        [D_MODEL] * 4, dep=sent)

    sent = emit("w_o", matmul("mix_wgrad", o_cat, dmix, "tn", tm=512, out_dtype=MXU_DTYPE))
    do_cat = matmul("mix_dgrad", dmix, w_o, "nt", tm=512, dep=sent)
    dband = None
    for dil, b in zip(DILATIONS, band):
        dband = band_bwd(f"band{dil}_bwd", qd_r, kd_r, proj, b[1], lse_mix, o_cat, do_cat, dil, before=dband)
    dq_mla, dkv_mla, dkper = mla_bwd("mla_bwd", q_mla, k_mla, v_mla, o_cat, do_cat, lse_mla, mla_scale)

    def mla_prep_bwd_fn(rows, params):
        (dqv, dkvv, qv, kvv, cm, sm), (gq, gk) = rows, params
        nope_lanes = _lane(cm.shape) < NOPE
        dqs, dkvs, dgq, dgk = [], [], 0.0, 0.0
        for dqc, dkc, qc, kc in zip(_chunks(dqv), _chunks(dkvv), _chunks(qv), _chunks(kvv), strict=True):
            _, n, rstd = _grms(qc, gq, Q_GROUPS)
            dx, dg = _grms_bwd(_rope_bwd(dqc, cm, sm, H_M), n, rstd, gq, Q_GROUPS)
            dqs.append(dx)
            dgq = dgq + dg
            _, n, rstd = _grms(kc, gk, K_GROUPS)
            dx, dg = _grms_bwd(dkc, n, rstd, gk, K_GROUPS)
            dkvs.append(jnp.where(nope_lanes, dx, dkc))
            dgk = dgk + dg
        return [jnp.concatenate(dqs, axis=1), jnp.concatenate(dkvs, axis=1)], [dgq, dgk]

    dq, dkv, dg_q, dg_k = rowwise("mla_prep_bwd", mla_prep_bwd_fn, [dq_mla, dkv_mla, q, kv, cos_m, sin_m], [g_q, g_k],
                                  [(HEADS * LANES, MXU_DTYPE)] * 2, [LANES, LANES], tm=256)
    emit("w_q_b", matmul("q_wgrad", dq, qln, "tn", out_dtype=MXU_DTYPE))
    emit("w_kv_b", matmul("kv_wgrad", kvn, dkv, "tn", out_dtype=MXU_DTYPE))

    def pre_bwd_fn(rows, params):
        dqv, dkvv, dkp, dqd_, dkd_, dvd_, pv, cm, sm, cd, sd = rows
        wq_t, wkv, gq, gkv, gkp, gdq, gdk = params
        dql, dkvl = _dot(dqv, wq_t), _dot(dkvv, wkv, "nt")
        r_q = _norm_bwd(dql, pv[:, P_QLAT:P_KVLAT], gq)
        r_kv = _norm_bwd(dkvl, pv[:, P_KVLAT:P_KPE], gkv)
        _, n, rstd = _grms(pv[:, P_KPE:P_QD], gkp, KPE_GROUPS)
        r_kp = _grms_bwd(_rope_bwd(dkp, cm, sm, H_M), n, rstd, gkp, KPE_GROUPS)
        outs, dgs = [r_q[0], r_kv[0], r_kp[0]], []
        for dval, lo, g in ((dqd_, P_QD, gdq), (dkd_, P_KD, gdk)):
            dg_sum = 0.0
            for dc, xc in zip(_chunks(dval), _chunks(pv[:, lo:lo + DIL_WIDTH]), strict=True):
                _, n, rstd = _grms(xc, g, DIL_GROUPS)
                dx, dg = _grms_bwd(_rope_bwd(dc, cd, sd, H_D), n, rstd, g, DIL_GROUPS)
                outs.append(dx)
                dg_sum = dg_sum + dg
            dgs.append(dg_sum)
        return [jnp.concatenate(outs + [dvd_], axis=1)], [r_q[1], r_kv[1], r_kp[1]] + dgs

    dproj, dg_q_lat, dg_kv_lat, dg_kpe, dg_dq, dg_dk = rowwise(
        "proj_pre_bwd", pre_bwd_fn,
        [dq, dkv, dkper] + list(dband) + [proj] + tables, [w_q_b, w_kv_b] + post_params,
        [(P_END, MXU_DTYPE)], [Q_LORA, KV_LORA, LANES, LANES, LANES], tm=256)
    sent = emit("w_in", matmul("proj_wgrad", dproj, h, "tn", tn=512, out_dtype=MXU_DTYPE))

    def ln1_bwd_fn(rows, params):
        (dpv, dres, xv), (w_t, g, sc) = rows, params
        dhv = _dot(dpv, w_t)
        yn, n, rstd = _rms(xv, g)
        dx_n, dg = _rms_bwd(dhv * (1.0 + sc), n, rstd, g)
        return [dres + dx_n], [dg, _colsum(dhv * yn), _colsum(dhv)]

    grad_x, dg_mix, dsc1, dsh1 = rowwise("proj_dgrad", ln1_bwd_fn, [dproj, dx1, x], [w_in, w["g_mix_norm"], sc1],
                                         [(D_MODEL, F32)], [D_MODEL] * 3, tm=256, dep=sent)
    dmod = jnp.concatenate([dsh1, dsc1, dg1, dsh2, dsc2, dg2], axis=-1)
    small = {"loss": loss, "b_ada": dmod, "g_mix_norm": dg_mix, "g_q_lat": dg_q_lat, "g_kv_lat": dg_kv_lat,
             "g_mla_q_nope": dg_q[:, :NOPE], "g_mla_q_pe": dg_q[:, NOPE:NOPE + ROPE], "g_mla_k_nope": dg_k[:, :NOPE],
             "g_mla_k_pe": dg_kpe[:, KPE_LO:KPE_LO + ROPE], "g_dil_q": dg_dq[:, :DIL_DIM] + dg_dq[:, DIL_DIM:],
             "g_dil_k": dg_dk[:, :DIL_DIM] + dg_dk[:, DIL_DIM:], "g_ffn_norm": dg_ffn,
             "b_conv": g_b_conv}
    return grad_x, small


COL_SHARDED = ("w_kv_b", "w_conv")
ROW_SHARDED = ("w_o", "w_down") + TRANSPOSED
ADAM_TILE = {"w_ada": 256, "w_up": 176, "w_down": 176}
GATHER_GROUPS = (("w_in",), ("w_q_b", "w_kv_b"), ("w_o",), ("w_up", "w_conv", "w_down"))
START_STAGES = ((0, 1), (2, 3))
FORWARD_STAGES = ((0, 1), (2,), (3,))
FORWARD_WITH = {"w_o": 2}
SCATTER_GROUPS = (("w_down", "w_conv", "w_up"), ("w_o",), ("w_q_b", "w_kv_b", "w_in"))
OUT_WEIGHTS = ("w_ada", "b_ada", "g_mix_norm", "w_in", "g_q_lat", "w_q_b", "g_kv_lat", "w_kv_b", "g_mla_q_nope", "g_mla_q_pe",
               "g_mla_k_nope", "g_mla_k_pe", "g_dil_q", "g_dil_k", "w_o", "g_ffn_norm", "w_up", "w_conv", "b_conv", "w_down")


def kernel(x, c, positions, w_ada, b_ada, g_mix_norm, w_in, g_q_lat, w_q_b, g_kv_lat, w_kv_b, g_mla_q_nope, g_mla_q_pe, g_mla_k_nope, g_mla_k_pe, g_dil_q, g_dil_k, w_o, g_ffn_norm, w_up, w_conv, b_conv, w_down, loss_target, m_w_ada, m_b_ada, m_g_mix_norm, m_w_in, m_g_q_lat, m_w_q_b, m_g_kv_lat, m_w_kv_b, m_g_mla_q_nope, m_g_mla_q_pe, m_g_mla_k_nope, m_g_mla_k_pe, m_g_dil_q, m_g_dil_k, m_w_o, m_g_ffn_norm, m_w_up, m_w_conv, m_b_conv, m_w_down, v_w_ada, v_b_ada, v_g_mix_norm, v_w_in, v_g_q_lat, v_w_q_b, v_g_kv_lat, v_w_kv_b, v_g_mla_q_nope, v_g_mla_q_pe, v_g_mla_k_nope, v_g_mla_k_pe, v_g_dil_q, v_g_dil_k, v_w_o, v_g_ffn_norm, v_w_up, v_w_conv, v_b_conv, v_w_down):
    args = dict(locals())
    xi, yi, ci = _place()
    me = 4 * xi + 2 * yi + ci
    def local(prefix, n):
        a = args[prefix + n]
        if n in ROWS_APART:
            return jnp.transpose(a, (2, 0, 1) if n in TRANSPOSED else (1, 0, 2))
        return a[0].T if n in TRANSPOSED else a[0]

    def as_output(n, r):
        if n in ROWS_APART:
            return jnp.transpose(r, (1, 2, 0) if n in TRANSPOSED else (1, 0, 2))
        return (r.T if n in TRANSPOSED else r)[None]

    shard = {n: local("", n) for n in COL_SHARDED + ROW_SHARDED + ("w_ada",)}
    flat = lambda n, a: a.reshape(a.shape[0], a.shape[-1]) if n in ROWS_APART else a
    small_w = {n: args[n] for n in SMALL_PARAMS}

    payload = {n: flat(n, shard[n]) if n == "w_conv" else flat(n, shard[n]).astype(MXU_DTYPE) for n in COL_SHARDED + ROW_SHARDED}
    start_order = [[n for i in groups for n in GATHER_GROUPS[i]] for groups in START_STAGES]

    sc_all, mod_all = ada_modulation("ada_mod", c, shard["w_ada"], after=[payload[n] for n in start_order[0]])

    exchange_of = lambda i: [e for e, groups in enumerate(START_STAGES) if i in groups][0]
    start_stage = lambda e, after: exchange_start(f"gather_start{e}", [payload[n] for n in start_order[e]], gather=True,
                                                  after=after, tree=True)
    gathered = {0: start_stage(0, mod_all)}
    after_start = gathered[0][-1]
    full, forwarded = {}, set()

    def forward(stage, after):
        e = exchange_of(FORWARD_STAGES[stage][0])
        if stage not in forwarded:
            forwarded.add(stage)
            first = start_order[e].index(GATHER_GROUPS[FORWARD_STAGES[stage][0]][0])
            count = sum(len(GATHER_GROUPS[i]) for i in FORWARD_STAGES[stage])
            starts_next = e + 1 < len(START_STAGES) and e + 1 not in gathered
            ready = [payload[n] for n in start_order[e + 1]] if starts_next else []
            gathered[e] = exchange_forward(f"gather_forward{stage}", gathered[e], [after] + ready, first, count)
            if starts_next:
                gathered[e + 1] = start_stage(e + 1, gathered[e][-1])
        return gathered[e][-1]

    def fetch(name, after):
        if name not in full:
            (i, grp), = [(i, grp) for i, grp in enumerate(GATHER_GROUPS) if name in grp]
            (stage,) = [s for s, groups in enumerate(FORWARD_STAGES) if i in groups]
            forward(stage, after)
            if name in FORWARD_WITH:
                forward(FORWARD_WITH[name], after)
            e = exchange_of(i)
            behind = gathered[e + 1][-1] if e + 1 in gathered else after
            srcs, lands = exchange_wait(f"gather{i}_wait", gathered[e], True, behind, start_order[e].index(grp[0]), len(grp), tree=True)
            for n, stack in zip(grp, lands, strict=True):
                full[n] = to_kernel_layout(n, _gather_cols(stack) if n in COL_SHARDED else _gather_rows(stack))
        return full[name]

    mod_row = lax.dynamic_index_in_dim(mod_all, me, axis=1, keepdims=False).reshape(1, 6 * D_MODEL)
    (mod,) = rowwise("ada_bias", lambda rows, params: ([rows[0] + rows[1]], []), [mod_row, b_ada], [], [(6 * D_MODEL, F32)],
                     dep=after_start)

    pending, scatters = {}, {}

    def emit(name, grad):
        grad = from_kernel_layout(name, grad)
        pending[name] = _scatter_cols(grad) if name in COL_SHARDED else _scatter_rows(grad)
        for i, grp in enumerate(SCATTER_GROUPS):
            if name == grp[-1]:
                scatters[i] = exchange_start(f"scatter{i}_start", [pending[n] for n in grp], gather=False)
                return scatters[i][-1]
        return None

    pos = positions.reshape(SEQ, 1).astype(F32)
    grad_x, small = _local_step(x[0], pos, mod, loss_target[0], small_w, fetch, emit, halfway=lambda after: forward(1, after))

    small_sent = exchange_start("small_start", [_pack_small(small)], gather=True, after=grad_x)

    res, done = {}, small_sent[-1]
    for i, grp in enumerate(SCATTER_GROUPS):
        _, lands = exchange_wait(f"scatter{i}_wait", scatters[i], False, done)
        for n, land in zip(grp, lands, strict=True):
            res[n] = adamw(f"adamw_{n}", shard[n], [land], local("m_", n), local("v_", n), ADAM_TILE.get(n))
            done = res[n][0]
            res[n] = [as_output(n, r) for r in res[n]]
    _, (small_all,) = exchange_wait("small_wait", small_sent, True, done)
    loss, small_res = adamw_small("adamw_small", small_all, {n: (args[n], args["m_" + n], args["v_" + n]) for n in SMALL_PARAMS})
    row, _, n_mod = SMALL_AT["b_ada"]
    dmod_all = small_all[:, row:row + n_mod // SMALL_COLS, :].reshape(N_DEV, n_mod)
    dmod_mine = lax.dynamic_slice_in_dim(dmod_all, me * (6 * D_MODEL // N_DEV), 6 * D_MODEL // N_DEV, axis=1)
    g_w_ada = matmul("ada_wgrad", sc_all, dmod_mine, "tn")
    res["w_ada"] = [r[None] for r in adamw("adamw_w_ada", shard["w_ada"], [g_w_ada], m_w_ada[0], v_w_ada[0], ADAM_TILE["w_ada"])]

    def leaf(kind, n):
        return res[n][kind] if n in res else small_res[n][kind]

    return (loss.reshape(()), grad_x[None], *[leaf(k, n) for k in range(4) for n in OUT_WEIGHTS])
```

```python
import jax
import jax.numpy as jnp
from jax import lax
from jax.experimental import pallas as pl
from jax.experimental.pallas import tpu as pltpu

F32 = jnp.float32
MXU_DTYPE = jnp.bfloat16

N_DEV = 8
D_MODEL = 1024
SEQ = 2048
HEADS = 8
NOPE = 64
ROPE = 32
Q_LORA = 512
KV_LORA = 256
DIL_DIM = 64
DIL_WIDTH = HEADS * DIL_DIM
DILATIONS = (1, 4, 16)
SPAN = 128
D_FF = 2816
LANES = 128
SUBLANES = 8
ROPE_THETA = 10000.0
EPS = 1e-6
NEG_INF = -1e30
ADAM_LR, ADAM_B1, ADAM_B2, ADAM_EPS, ADAM_WD, ADAM_STEP = 0.001, 0.9, 0.999, 1e-08, 0.01, 10
VMEM_LIMIT = 56 * 1024 * 1024
MESH_ID = pl.DeviceIdType.MESH

P_QLAT, P_KVLAT, P_KPE, P_QD, P_KD, P_VD, P_END = 0, 512, 768, 896, 1408, 1920, 2432
KPE_LO = 64
MIX_IN = HEADS * LANES + DIL_WIDTH


def _params(**kw):
    return pltpu.CompilerParams(vmem_limit_bytes=VMEM_LIMIT, **kw)


def rowwise(name, fn, rows, params, out_rows, out_accs=(), tm=512, dep=None):
    deps = [] if dep is None else [dep]
    rows = [r if isinstance(r, tuple) else (r, r.shape[1], 0) for r in rows]
    R = rows[0][0].shape[0]
    tm = min(tm, R)
    steps = R // tm
    assert steps * tm == R
    in_specs = []
    for a, width, cb in rows:
        ri = a.shape[0]
        per = ri // tm
        assert per * tm == ri
        if ri == R:
            in_specs.append(pl.BlockSpec((tm, width), lambda i, cb=cb: (i, cb)))
        else:
            in_specs.append(pl.BlockSpec((tm, width), lambda i, per=per, cb=cb: (i % per, cb)))
    for p in params:
        in_specs.append(pl.BlockSpec(p.shape, lambda i: (0,) * p.ndim))
    in_specs += [pl.BlockSpec(memory_space=pl.ANY)] * len(deps)
    out_shape = [jax.ShapeDtypeStruct((R, d), dt) for d, dt in out_rows]
    out_specs = [pl.BlockSpec((tm, d), lambda i: (i, 0)) for d, _ in out_rows]
    out_shape += [jax.ShapeDtypeStruct((1, n), F32) for n in out_accs]
    out_specs += [pl.BlockSpec((1, n), lambda i: (0, 0)) for n in out_accs]
    nr, npar, no, na = len(rows), len(params), len(out_rows), len(out_accs)

    def body(*refs):
        rvals = [r[...] for r in refs[:nr]]
        pvals = [r[...] for r in refs[nr:nr + npar]]
        outs, accs = fn(rvals, pvals)
        first_out = nr + npar + len(deps)
        for ref, v in zip(refs[first_out:first_out + no], outs, strict=True):
            ref[...] = v.astype(ref.dtype)
        if na:
            acc_refs = refs[first_out + no:]
            i = pl.program_id(0)

            @pl.when(i == 0)
            def _():
                for ref, v in zip(acc_refs, accs, strict=True):
                    ref[...] = v

            @pl.when(i > 0)
            def _():
                for ref, v in zip(acc_refs, accs, strict=True):
                    ref[...] += v

    res = pl.pallas_call(body, name=name, grid=(steps,), in_specs=in_specs, out_specs=out_specs,
                         out_shape=out_shape, compiler_params=_params())(*[r[0] for r in rows], *params, *deps)
    return list(res)


_DIMS = {"nn": ((1,), (0,)), "nt": ((1,), (1,)), "tn": ((0,), (0,))}


def _dot(a, b, mode="nn"):
    return lax.dot_general(a.astype(MXU_DTYPE), b.astype(MXU_DTYPE), (_DIMS[mode], ((), ())),
                           preferred_element_type=F32)


def matmul(name, a, b, mode, tm=None, tn=None, tk=None, out_dtype=F32, dep=None):
    if mode == "tn":
        K, M = a.shape
    else:
        M, K = a.shape
    N = b.shape[0] if mode == "nt" else b.shape[1]
    tm, tn, tk = tm or M, tn or N, tk or K
    nm, nn, nk = M // tm, N // tn, K // tk
    assert nm * tm == M and nn * tn == N and nk * tk == K
    a_spec = pl.BlockSpec((tk, tm), lambda i, j, k: (k, i)) if mode == "tn" else pl.BlockSpec((tm, tk), lambda i, j, k: (i, k))
    b_spec = pl.BlockSpec((tn, tk), lambda i, j, k: (j, k)) if mode == "nt" else pl.BlockSpec((tk, tn), lambda i, j, k: (k, j))
    deps = [] if dep is None else [dep]

    def body(a_ref, b_ref, *rest):
        o_ref, scratch = rest[len(deps)], rest[len(deps) + 1:]
        p = _dot(a_ref[...], b_ref[...], mode)
        if nk == 1:
            o_ref[...] = p.astype(o_ref.dtype)
        else:
            acc = scratch[0]
            k = pl.program_id(2)

            @pl.when(k == 0)
            def _():
                acc[...] = p

            @pl.when(k > 0)
            def _():
                acc[...] += p

            @pl.when(k == nk - 1)
            def _():
                o_ref[...] = acc[...].astype(o_ref.dtype)

    return pl.pallas_call(
        body, name=name, grid=(nm, nn, nk), in_specs=[a_spec, b_spec] + [pl.BlockSpec(memory_space=pl.ANY)] * len(deps),
        out_specs=pl.BlockSpec((tm, tn), lambda i, j, k: (i, j)),
        out_shape=jax.ShapeDtypeStruct((M, N), out_dtype),
        scratch_shapes=[pltpu.VMEM((tm, tn), F32)] if nk > 1 else [],
        compiler_params=_params())(a, b, *deps)


def _rms(x, g):
    rstd = lax.rsqrt(jnp.mean(x * x, axis=-1, keepdims=True) + EPS)
    n = x * rstd
    return n * g, n, rstd


def _rms_bwd(dy, n, rstd, g):
    dg = jnp.sum(dy * n, axis=0, keepdims=True)
    dn = dy * g
    dx = rstd * (dn - n * jnp.mean(dn * n, axis=-1, keepdims=True))
    return dx, dg


def _norm_bwd(dy, x, g):
    _, n, rstd = _rms(x, g)
    return _rms_bwd(dy, n, rstd, g)


def _colsum(v):
    return jnp.sum(v, axis=0, keepdims=True)


def _silu(x):
    return x * (1.0 / (1.0 + jnp.exp(-x)))


def _lane(shape):
    return lax.broadcasted_iota(jnp.int32, shape, 1)


def _group_mean(v, groups):
    i = lax.broadcasted_iota(jnp.int32, (LANES, LANES), 0)
    j = lax.broadcasted_iota(jnp.int32, (LANES, LANES), 1)
    g = jnp.zeros((LANES, LANES), F32)
    for lo, hi in groups:
        g = jnp.where((i >= lo) & (i < hi) & (j >= lo) & (j < hi), 1.0 / (hi - lo), g)
    head = v.astype(MXU_DTYPE)
    return _dot(head, g) + _dot(v - head.astype(F32), g)


def _in_groups(shape, groups):
    lane = _lane(shape)
    m = jnp.zeros(shape, jnp.bool_)
    for lo, hi in groups:
        m = m | ((lane >= lo) & (lane < hi))
    return m


def _grms(x, g, groups):
    rstd = lax.rsqrt(_group_mean(x * x, groups) + EPS)
    n = jnp.where(_in_groups(x.shape, groups), x * rstd, 0.0)
    return n * g, n, rstd


def _grms_bwd(dy, n, rstd, g, groups):
    dn = dy * g
    return rstd * (dn - n * _group_mean(dn * n, groups)), _colsum(dy * n)


def _rot(x, half, transpose=False):
    first = (_lane(x.shape) % (2 * half)) < half
    up = pltpu.roll(x, LANES - half, axis=1)
    down = pltpu.roll(x, half, axis=1)
    return jnp.where(first, up, -down) if transpose else jnp.where(first, -up, down)


def _rope(x, cos, sin, half):
    return x * cos + _rot(x, half) * sin


def _rope_bwd(dy, cos, sin, half):
    return dy * cos + _rot(dy * sin, half, transpose=True)


def _chunks(x):
    return [x[:, i:i + LANES] for i in range(0, x.shape[1], LANES)]


Q_GROUPS = ((0, NOPE), (NOPE, NOPE + ROPE))
K_GROUPS = ((0, NOPE),)
KPE_GROUPS = ((KPE_LO, KPE_LO + ROPE),)
DIL_GROUPS = ((0, DIL_DIM), (DIL_DIM, 2 * DIL_DIM))


def _col(width, rows=SEQ):
    return pl.BlockSpec((rows, width), lambda h: (0, h))


def _causal_tail(s, tq, fill):
    diag = s[:, s.shape[1] - tq:]
    keep = lax.broadcasted_iota(jnp.int32, diag.shape, 1) <= lax.broadcasted_iota(jnp.int32, diag.shape, 0)
    diag = jnp.where(keep, diag, fill)
    return diag if s.shape[1] == tq else jnp.concatenate([s[:, :s.shape[1] - tq], diag], axis=1)


def mla_fwd(name, q, k, v, scale, tq=256):
    S = q.shape[0]

    def body(q_ref, k_ref, v_ref, o_ref, lse_ref):
        nb = S // tq
        blk = lambda i: slice(i * tq, (i + 1) * tq)

        def scores(i):
            return _dot(q_ref[blk(i), :], k_ref[:(i + 1) * tq, :], "nt")

        def softmax(i, s):
            s = _causal_tail(s * scale, tq, NEG_INF)
            m = jnp.max(s, axis=-1, keepdims=True)
            e = jnp.exp(s - m)
            l = jnp.sum(e, axis=-1, keepdims=True)
            lse_ref[0, blk(i), :] = m + jnp.log(l)
            return (e * (1.0 / l)).astype(MXU_DTYPE)

        def weighted(i, p):
            o_ref[blk(i), :] = _dot(p, v_ref[:(i + 1) * tq, :])

        s, p_prev = scores(0), None
        for i in range(nb):
            s_next = scores(i + 1) if i + 1 < nb else None
            if p_prev is not None:
                weighted(i - 1, p_prev)
            p_prev, s = softmax(i, s), s_next
        weighted(nb - 1, p_prev)

    return pl.pallas_call(
        body, name=name, grid=(HEADS,), in_specs=[_col(LANES)] * 3,
        out_specs=[_col(LANES), pl.BlockSpec((1, S, 1), lambda h: (h, 0, 0))],
        out_shape=[jax.ShapeDtypeStruct((S, MIX_IN), F32), jax.ShapeDtypeStruct((HEADS, S, 1), F32)],
        compiler_params=_params())(q, k, v)


def mla_bwd(name, q, k, v, o, do, lse, scale, tq=256):
    S = q.shape[0]

    def body(q_ref, k_ref, v_ref, o_ref, do_ref, lse_ref, dq_ref, dkv_ref, dkpe_ref, dk_acc, dv_acc):
        dk_acc[...] = jnp.zeros_like(dk_acc)
        dv_acc[...] = jnp.zeros_like(dv_acc)
        for i in range(S // tq):
            kext = (i + 1) * tq
            blk = slice(i * tq, kext)
            qi, kk, vv = q_ref[blk, :], k_ref[:kext, :], v_ref[:kext, :]
            doi = do_ref[blk, :]
            s = _causal_tail(_dot(qi, kk, "nt") * scale, tq, NEG_INF)
            p = jnp.exp(s - lse_ref[0, blk, :])
            dp = _dot(doi, vv, "nt")
            delta = jnp.sum(doi * o_ref[blk, :], axis=-1, keepdims=True)
            ds = p * (dp - delta) * scale
            dq_ref[blk, :] = _dot(ds, kk)
            dk_acc[:kext, :] += _dot(ds, qi, "tn")
            dv_acc[:kext, :] += _dot(p, doi, "tn")
        dk = dk_acc[...]
        lane = _lane(dk.shape)
        dkv_ref[...] = jnp.where(lane < NOPE, dk, 0.0) + dv_acc[...]
        dkpe = jnp.where((lane >= KPE_LO) & (lane < KPE_LO + ROPE), dk, 0.0)
        h = pl.program_id(0)

        @pl.when(h == 0)
        def _():
            dkpe_ref[...] = dkpe

        @pl.when(h > 0)
        def _():
            dkpe_ref[...] += dkpe

    return pl.pallas_call(
        body, name=name, grid=(HEADS,),
        in_specs=[_col(LANES)] * 5 + [pl.BlockSpec((1, S, 1), lambda h: (h, 0, 0))],
        out_specs=[_col(LANES), _col(LANES), pl.BlockSpec((S, LANES), lambda h: (0, 0))],
        out_shape=[jax.ShapeDtypeStruct((S, HEADS * LANES), F32), jax.ShapeDtypeStruct((S, HEADS * LANES), F32),
                   jax.ShapeDtypeStruct((S, LANES), F32)],
        scratch_shapes=[pltpu.VMEM((S, LANES), F32), pltpu.VMEM((S, LANES), F32)],
        compiler_params=_params())(q, k, v, o, do, lse)


BAND_TQ = SPAN


def _band_blocks(L, tq):
    return [(i * tq, (i + 1) * tq, max(0, i * tq - SPAN)) for i in range(L // tq)]


def _class_rows(r, dil, lo, hi):
    return pl.ds(r + dil * lo, hi - lo, stride=dil) if dil > 1 else pl.ds(lo, hi - lo)


def _stack_heads(t, lo):
    zero = jnp.zeros_like(t)
    return jnp.concatenate([jnp.where(lo, t, zero), jnp.where(lo, zero, t)], axis=0)


def _band_mask2(q0, q1, k0):
    n = q1 - q0
    shape = (2 * n, q1 - k0)
    i = lax.broadcasted_iota(jnp.int32, shape, 0)
    dist = (jnp.where(i >= n, i - n, i) + q0) - (lax.broadcasted_iota(jnp.int32, shape, 1) + k0)
    return (dist >= 0) & (dist <= SPAN)


def _pair_col(col0=0):
    return pl.BlockSpec((SEQ, LANES), lambda j: (0, col0 // LANES + j))


def band_fwd(name, q, k, v, dil, dep=None):
    S = q.shape[0]
    L = S // dil
    tq = BAND_TQ
    scale = DIL_DIM ** -0.5
    deps = [] if dep is None else [dep]

    def body(q_ref, k_ref, v_ref, *rest):
        o_ref, lse_ref = rest[len(deps):]
        items = [(r, blk) for r in range(dil) for blk in _band_blocks(L, tq)]
        lo = _lane((tq, LANES)) < DIL_DIM

        def scores(item):
            r, (q0, q1, k0) = item
            qb = q_ref[_class_rows(r, dil, q0, q1), :].astype(MXU_DTYPE)
            return _dot(_stack_heads(qb, lo), k_ref[_class_rows(r, dil, k0, q1), :], "nt")

        def softmax(item, s):
            _, (q0, q1, k0) = item
            s = jnp.where(_band_mask2(q0, q1, k0), s * scale, NEG_INF)
            mx = jnp.max(s, axis=-1, keepdims=True)
            e = jnp.exp(s - mx)
            l = jnp.sum(e, axis=-1, keepdims=True)
            return (e * (1.0 / l)).astype(MXU_DTYPE), mx + jnp.log(l)

        def weighted(item, p, lse):
            r, (q0, q1, k0) = item
            pv = _dot(p, v_ref[_class_rows(r, dil, k0, q1), :])
            o_ref[_class_rows(r, dil, q0, q1), :] = jnp.where(lo, pv[:tq], pv[tq:])
            lse_ref[_class_rows(r, dil, q0, q1), :] = jnp.where(lo, lse[:tq], lse[tq:])

        s, prev = scores(items[0]), None
        for i, item in enumerate(items):
            s_next = scores(items[i + 1]) if i + 1 < len(items) else None
            if prev is not None:
                weighted(items[i - 1], *prev)
            prev, s = softmax(item, s), s_next
        weighted(items[-1], *prev)

    return pl.pallas_call(
        body, name=name, grid=(DIL_WIDTH // LANES,),
        in_specs=[_pair_col()] * 2 + [_pair_col(P_VD)] + [pl.BlockSpec(memory_space=pl.ANY)] * len(deps), out_specs=[_pair_col()] * 2,
        out_shape=[jax.ShapeDtypeStruct((S, DIL_WIDTH), F32)] * 2, compiler_params=_params())(q, k, v, *deps)


def band_bwd(name, q, k, v, lse, lse_mix, o_cat, do_cat, dil, before=None):
    S = q.shape[0]
    L = S // dil
    tq = BAND_TQ
    scale = DIL_DIM ** -0.5
    before = list(before or [])

    def body(q_ref, k_ref, v_ref, lse_ref, mix_ref, o_ref, do_ref, *rest):
        dq_ref, dk_ref, dv_ref = rest[len(before):]
        if before:
            dq0_ref, dk0_ref, dv0_ref = rest[:3]
            dk_ref[...] = dk0_ref[...]
            dv_ref[...] = dv0_ref[...]
        else:
            dk_ref[...] = jnp.zeros_like(dk_ref)
            dv_ref[...] = jnp.zeros_like(dv_ref)
        items = [(r, blk) for r in range(dil) for blk in _band_blocks(L, tq)]
        lo = _lane((tq, LANES)) < DIL_DIM
        per_head = lambda t: jnp.concatenate([t[:, 0:1], t[:, DIL_DIM:DIL_DIM + 1]], axis=0)

        def scores(item):
            r, (q0, q1, k0) = item
            qrows, krows = _class_rows(r, dil, q0, q1), _class_rows(r, dil, k0, q1)
            lse_p, dout = lse_ref[qrows, :], do_ref[qrows, :]
            w2 = per_head(jnp.exp(lse_p - mix_ref[qrows, :]))
            dd = dout * o_ref[qrows, :]
            big_d = jnp.concatenate([jnp.sum(jnp.where(lo, dd, 0.0), axis=-1, keepdims=True),
                                     jnp.sum(jnp.where(lo, 0.0, dd), axis=-1, keepdims=True)], axis=0)
            q2 = _stack_heads(q_ref[qrows, :].astype(MXU_DTYPE), lo)
            dom = (_stack_heads(dout, lo) * w2).astype(MXU_DTYPE)
            return (_dot(q2, k_ref[krows, :], "nt"), _dot(dom, v_ref[krows, :], "nt"), per_head(lse_p), w2 * big_d, q2, dom)

        def softmax_bwd(item, s, dp, lse2, wd2, q2, dom):
            _, (q0, q1, k0) = item
            p = jnp.where(_band_mask2(q0, q1, k0), jnp.exp(s * scale - lse2), 0.0)
            return p.astype(MXU_DTYPE), (p * (dp - wd2) * scale).astype(MXU_DTYPE), q2, dom

        def grads(item, p, ds, q2, dom):
            r, (q0, q1, k0) = item
            qrows, krows = _class_rows(r, dil, q0, q1), _class_rows(r, dil, k0, q1)
            dq2 = _dot(ds, k_ref[krows, :])
            dq = jnp.where(lo, dq2[:tq], dq2[tq:])
            dq_ref[qrows, :] = dq + dq0_ref[qrows, :] if before else dq
            dk_ref[krows, :] += _dot(ds, q2, "tn")
            dv_ref[krows, :] += _dot(p, dom, "tn")

        sc, prev = scores(items[0]), None
        for i, item in enumerate(items):
            sc_next = scores(items[i + 1]) if i + 1 < len(items) else None
            if prev is not None:
                grads(items[i - 1], *prev)
            prev, sc = softmax_bwd(item, *sc), sc_next
        grads(items[-1], *prev)

    cat = _pair_col(HEADS * LANES)
    return pl.pallas_call(
        body, name=name, grid=(DIL_WIDTH // LANES,),
        in_specs=[_pair_col()] * 2 + [_pair_col(P_VD)] + [_pair_col()] * 2 + [cat] * 2 + [_pair_col()] * len(before),
        out_specs=[_pair_col()] * 3, out_shape=[jax.ShapeDtypeStruct((S, DIL_WIDTH), F32)] * 3,
        compiler_params=_params())(q, k, v, lse, lse_mix, o_cat, do_cat, *before)


def combine_fwd(name, outs, lses, o_cat, tm=512):
    S = outs[0].shape[0]

    def body(o1, o2, o3, l1, l2, l3, cat_in, cat_out, mix_ref):
        ls = [l1[...], l2[...], l3[...]]
        m = jnp.maximum(jnp.maximum(ls[0], ls[1]), ls[2])
        e = [jnp.exp(l - m) for l in ls]
        den = e[0] + e[1] + e[2]
        cat_out[...] = (e[0] / den) * o1[...] + (e[1] / den) * o2[...] + (e[2] / den) * o3[...]
        mix_ref[...] = m + jnp.log(den)

    row = pl.BlockSpec((tm, DIL_WIDTH), lambda i: (i, 0))
    return pl.pallas_call(
        body, name=name, grid=(S // tm,), in_specs=[row] * 6 + [pl.BlockSpec(memory_space=pl.ANY)],
        out_specs=[pl.BlockSpec((tm, DIL_WIDTH), lambda i: (i, HEADS * LANES // DIL_WIDTH)), row],
        out_shape=[jax.ShapeDtypeStruct(o_cat.shape, F32), jax.ShapeDtypeStruct((S, DIL_WIDTH), F32)],
        input_output_aliases={6: 0}, compiler_params=_params())(*outs, *lses, o_cat)


FFN_FWD_ROWS = 512
FFN_BWD_ROWS = 256
CONV_PAD = SUBLANES


def _window(x, k):
    groups = x.reshape(-1, SUBLANES, x.shape[1])
    turned = pltpu.roll(groups, SUBLANES - k, axis=1)
    stays = lax.broadcasted_iota(jnp.int32, (groups.shape[0] - 1,) + groups.shape[1:], 1) < SUBLANES - k
    return jnp.where(stays, turned[:-1], turned[1:]).reshape(-1, x.shape[1])


def _earlier(ref, r0, rows, n):
    if r0 == 0:
        x = jnp.concatenate([jnp.zeros((SUBLANES, ref.shape[1]), F32), ref[:rows, :]], axis=0)
    else:
        x = ref[r0 - SUBLANES:r0 + rows, :]
    return _window(x, SUBLANES - n)


CONV_TC = 256
CONV_NB = D_FF // CONV_TC


def _half_specs(rows, rows_axis=False):
    if rows_axis:
        return [pl.BlockSpec((rows, D_MODEL), lambda j: (j, 0)), pl.BlockSpec((rows, D_MODEL), lambda j: (j + CONV_NB, 0))]
    return [pl.BlockSpec((rows, CONV_TC), lambda j: (0, j)), pl.BlockSpec((rows, CONV_TC), lambda j: (0, j + CONV_NB))]


def _whole(a):
    return pl.BlockSpec(a.shape, lambda j: (0,) * a.ndim)


def _up_pair(h, ug_ref, uv_ref):
    return jnp.concatenate([_dot(h, ug_ref[...], "nt"), _dot(h, uv_ref[...], "nt")], axis=1)


def _conv_taps(up_ref, r0, rows, w, b):
    uin, u1, u2 = up_ref[r0:r0 + rows, :], _earlier(up_ref, r0, rows, 1), _earlier(up_ref, r0, rows, 2)
    return uin, u1, u2, w[2:3, :] * uin + w[1:2, :] * u1 + w[0:1, :] * u2 + b


def ffn_fwd(name, h, w_up_t, w_conv, b_conv, w_down):
    S = h.shape[0]

    def body(h_ref, ug_ref, uv_ref, wg_ref, wv_ref, bg_ref, bv_ref, wd_ref, dn_ref, up_ref):
        @pl.when(pl.program_id(0) == 0)
        def _():
            dn_ref[...] = jnp.zeros_like(dn_ref)

        w = jnp.concatenate([wg_ref[...], wv_ref[...]], axis=1)
        b = jnp.concatenate([bg_ref[...], bv_ref[...]], axis=1)
        rows = FFN_FWD_ROWS
        starts = list(range(0, S, rows))

        def project(r0):
            up_ref[r0:r0 + rows, :] = _up_pair(h_ref[r0:r0 + rows, :], ug_ref, uv_ref)

        def gate(r0):
            u = _conv_taps(up_ref, r0, rows, w, b)[3]
            return (_silu(u[:, :CONV_TC]) * u[:, CONV_TC:]).astype(MXU_DTYPE)

        def project_down(r0, act):
            dn_ref[r0:r0 + rows, :] += _dot(act, wd_ref[...])

        project(starts[0])
        act_prev = None
        for i, r0 in enumerate(starts):
            if i + 1 < len(starts):
                project(starts[i + 1])
            if act_prev is not None:
                project_down(starts[i - 1], act_prev)
            act_prev = gate(r0)
        project_down(starts[-1], act_prev)

    return pl.pallas_call(
        body, name=name, grid=(CONV_NB,),
        in_specs=[_whole(h)] + _half_specs(CONV_TC, rows_axis=True) + _half_specs(3) + _half_specs(1)
        + [pl.BlockSpec((CONV_TC, w_down.shape[1]), lambda j: (j, 0))],
        out_specs=[pl.BlockSpec((S, w_down.shape[1]), lambda j: (0, 0)), pl.BlockSpec((S, 2 * CONV_TC), lambda j: (0, j))],
        out_shape=[jax.ShapeDtypeStruct((S, w_down.shape[1]), F32), jax.ShapeDtypeStruct((S, 2 * D_FF), F32)],
        compiler_params=_params())(h, w_up_t, w_up_t, w_conv, w_conv, b_conv, b_conv, w_down)


def ffn_bwd(name, h, up, w_up_t, w_conv, b_conv, d_dn, w_down):
    S, D = h.shape

    def body(h_ref, up_ref, ug_ref, uv_ref, wg_ref, wv_ref, bg_ref, bv_ref, dd_ref, wd_ref,
             dh_ref, gup_ref, gd_ref, dwg_ref, dwv_ref, dbg_ref, dbv_ref, du_ref, dup_ref, act_ref):
        @pl.when(pl.program_id(0) == 0)
        def _():
            dh_ref[...] = jnp.zeros_like(dh_ref)

        w = jnp.concatenate([wg_ref[...], wv_ref[...]], axis=1)
        b = jnp.concatenate([bg_ref[...], bv_ref[...]], axis=1)
        w_pair = jnp.concatenate([ug_ref[...], uv_ref[...]], axis=0)
        rows = FFN_BWD_ROWS
        starts = list(range(0, S, rows))
        du_ref[S:S + CONV_PAD, :] = jnp.zeros((CONV_PAD, 2 * CONV_TC), F32)

        def project(r0):
            return _dot(dd_ref[r0:r0 + rows, :], wd_ref[...], "nt")

        def through_conv(r0, da):
            uin, u1, u2, u = _conv_taps(up_ref, r0, rows, w, b)
            gate, val = u[:, :CONV_TC], u[:, CONV_TC:]
            sig = 1.0 / (1.0 + jnp.exp(-gate))
            du = jnp.concatenate([da * val * (sig * (1.0 + gate * (1.0 - sig))), da * (gate * sig)], axis=1)
            du_ref[r0:r0 + rows, :] = du
            act_ref[r0:r0 + rows, :] = (gate * sig * val).astype(MXU_DTYPE)
            dw = jnp.concatenate([_colsum(du * u2), _colsum(du * u1), _colsum(du * uin)], axis=0)
            return dw, _colsum(du)

        def back_up(r0):
            du = du_ref[r0:r0 + rows + CONV_PAD, :]
            dup = (w[2:3, :] * du[:rows] + w[1:2, :] * _window(du, 1) + w[0:1, :] * _window(du, 2)).astype(MXU_DTYPE)
            dup_ref[r0:r0 + rows, :] = dup
            dh_ref[r0:r0 + rows, :] += _dot(dup, w_pair)

        dw, db = 0.0, 0.0
        da = project(starts[0])
        for i, r0 in enumerate(starts):
            da_next = project(starts[i + 1]) if i + 1 < len(starts) else None
            dw_c, db_c = through_conv(r0, da)
            if i > 0:
                back_up(starts[i - 1])
            dw, db, da = dw + dw_c, db + db_c, da_next
        back_up(starts[-1])
        g_up, g_dn = _dot(dup_ref[...], h_ref[...], "tn"), _dot(act_ref[...], dd_ref[...], "tn")
        gup_ref[0], gup_ref[1] = g_up[:CONV_TC].astype(gup_ref.dtype), g_up[CONV_TC:].astype(gup_ref.dtype)
        gd_ref[...] = g_dn.astype(gd_ref.dtype)
        dwg_ref[...], dwv_ref[...] = dw[:, :CONV_TC], dw[:, CONV_TC:]
        dbg_ref[...], dbv_ref[...] = db[:, :CONV_TC], db[:, CONV_TC:]

    half = lambda rows: pl.BlockSpec((rows, CONV_TC), lambda j: (0, j))
    rows_blk = pl.BlockSpec((CONV_TC, D), lambda j: (j, 0))
    dh, gup, gd, dwg, dwv, dbg, dbv = pl.pallas_call(
        body, name=name, grid=(CONV_NB,),
        in_specs=[_whole(h), pl.BlockSpec((S, 2 * CONV_TC), lambda j: (0, j))] + _half_specs(CONV_TC, rows_axis=True) + _half_specs(3)
        + _half_specs(1) + [_whole(d_dn), rows_blk],
        out_specs=[pl.BlockSpec((S, D), lambda j: (0, 0)), pl.BlockSpec((2, CONV_TC, D), lambda j: (0, j, 0)), rows_blk,
                   half(3), half(3), half(1), half(1)],
        out_shape=[jax.ShapeDtypeStruct((S, D), F32), jax.ShapeDtypeStruct((2, D_FF, D), MXU_DTYPE),
                   jax.ShapeDtypeStruct((D_FF, D), MXU_DTYPE)]
        + [jax.ShapeDtypeStruct((3, D_FF), F32)] * 2 + [jax.ShapeDtypeStruct((1, D_FF), F32)] * 2,
        scratch_shapes=[pltpu.VMEM((S + CONV_PAD, 2 * CONV_TC), F32), pltpu.VMEM((S, 2 * CONV_TC), MXU_DTYPE),
                        pltpu.VMEM((S, CONV_TC), MXU_DTYPE)],
        compiler_params=_params())(h, up, w_up_t, w_up_t, w_conv, w_conv, b_conv, b_conv, d_dn, w_down)
    return dh, gup.reshape(2 * D_FF, D), gd, jnp.concatenate([dwg, dwv], axis=1), jnp.concatenate([dbg, dbv], axis=1)


def adamw(name, w, parts, m, v, tr=None):
    apart = w.ndim == 3
    R, C = w.shape[0], w.shape[-1]
    tr = tr or R
    assert R % tr == 0
    c1 = 1.0 - ADAM_B1 ** ADAM_STEP
    c2 = 1.0 - ADAM_B2 ** ADAM_STEP
    np_ = len(parts)

    def body(*refs):
        w_ref, m_ref, v_ref = refs[0], refs[1 + np_], refs[2 + np_]
        go_ref, d_ref, mo_ref, vo_ref = refs[3 + np_:]
        terms = []
        for part, ref in zip(parts, refs[1:1 + np_], strict=True):
            terms += [ref[...]] if part.ndim == 2 else [ref[p] for p in range(part.shape[0])]
        g = terms[0].astype(F32)
        for term in terms[1:]:
            g = g + term.astype(F32)
        m2 = ADAM_B1 * m_ref[...] + (1.0 - ADAM_B1) * g
        v2 = ADAM_B2 * v_ref[...] + (1.0 - ADAM_B2) * (g * g)
        go_ref[...] = g
        mo_ref[...] = m2
        vo_ref[...] = v2
        d_ref[...] = -ADAM_LR * ((m2 / c1) / (jnp.sqrt(v2 / c2) + ADAM_EPS) + ADAM_WD * w_ref[...])

    blk = pl.BlockSpec((tr, C), lambda i: (i, 0))
    own = pl.BlockSpec((tr, None, C), lambda i: (i, 0, 0)) if apart else blk
    part_specs = [blk if p.ndim == 2 else pl.BlockSpec((p.shape[0], tr, C), lambda i: (0, i, 0)) for p in parts]
    return pl.pallas_call(
        body, name=name, grid=(R // tr,),
        in_specs=[own] + part_specs + [own, own], out_specs=[own] * 4,
        out_shape=[jax.ShapeDtypeStruct(w.shape, F32)] * 4, compiler_params=_params())(w, *parts, m, v)


def _place():
    return lax.axis_index("x"), lax.axis_index("y"), lax.axis_index("c")


def ada_modulation(name, c, w_ada, after=()):
    n_mod = w_ada.shape[1]

    def exchange(src_ref, dst_ref, send_sems, recv_sems):
        x, y, c_ = _place()
        me = 4 * x + 2 * y + c_
        copies = []
        for k in range(1, N_DEV):
            px, py, pc = x ^ (k >> 2), y ^ ((k >> 1) & 1), c_ ^ (k & 1)
            copies.append(pltpu.make_async_remote_copy(
                src_ref=src_ref, dst_ref=dst_ref.at[me], send_sem=send_sems.at[k - 1], recv_sem=recv_sems.at[k - 1],
                device_id=(px, py, pc), device_id_type=MESH_ID))
        for cp in copies:
            cp.start()
        for cp in copies:
            cp.wait_recv()
        for cp in copies:
            cp.wait_send()
        return me

    def body(c_ref, w_ref, *refs):
        sc_ref, mod_ref, c_all, send_c, recv_c, send_m, recv_m = refs[len(after):]
        me = exchange(c_ref, c_all, send_c, recv_c)
        c_all[me] = c_ref[...]
        sc = _silu(jnp.concatenate([c_all[p] for p in range(N_DEV)], axis=0))
        sc_ref[...] = sc.astype(sc_ref.dtype)
        mod_ref[me] = _dot(sc, w_ref[...])
        exchange(mod_ref.at[me], mod_ref, send_m, recv_m)

    vmem = pl.BlockSpec(memory_space=pltpu.VMEM)
    return pl.pallas_call(
        body, name=name, in_specs=[vmem, vmem] + [pl.BlockSpec(memory_space=pl.ANY)] * len(after), out_specs=[vmem, vmem],
        out_shape=[jax.ShapeDtypeStruct((N_DEV, c.shape[1]), MXU_DTYPE), jax.ShapeDtypeStruct((N_DEV, N_DEV, n_mod), F32)],
        scratch_shapes=[pltpu.VMEM((N_DEV, 1, c.shape[1]), F32)] + [pltpu.SemaphoreType.DMA((N_DEV - 1,))] * 4,
        compiler_params=pltpu.CompilerParams(has_side_effects=True, vmem_limit_bytes=VMEM_LIMIT))(c, w_ada, *after)


HBM_SPEC = pl.BlockSpec(memory_space=pltpu.HBM)
SEM_SPEC = pl.BlockSpec(memory_space=pltpu.SEMAPHORE)
DATAFLOW = pltpu.SideEffectType.DATAFLOW_SIDE_EFFECTING


def _exchange_copies(srcs, lands, send_sems, recv_sems, gather, first=0):
    x, y, c = _place()
    me = 4 * x + 2 * y + c
    out = []
    for t, (src, land) in enumerate(zip(srcs, lands, strict=True)):
        for k in range(1, N_DEV):
            px, py, pc = x ^ (k >> 2), y ^ ((k >> 1) & 1), c ^ (k & 1)
            sem = 7 * (first + t) + k - 1
            out.append((k, pltpu.make_async_remote_copy(
                src_ref=src if gather else src.at[4 * px + 2 * py + pc],
                dst_ref=land.at[me] if gather else land.at[k - 1],
                send_sem=send_sems.at[sem], recv_sem=recv_sems.at[sem],
                device_id=(px, py, pc), device_id_type=MESH_ID)))
    return out


def _own_copies(srcs, lands, send_sems, gather, first=0):
    x, y, c = _place()
    me = 4 * x + 2 * y + c
    total = send_sems.shape[0] // N_DEV
    return [pltpu.make_async_copy(src if gather else src.at[me], land.at[me] if gather else land.at[N_DEV - 1],
                                  send_sems.at[7 * total + first + t])
            for t, (src, land) in enumerate(zip(srcs, lands, strict=True))]


TREE_DIRECT = (1, 2, 4, 6)
TREE_FORWARDED = (3, 5, 7)


def exchange_start(name, arrs, gather, after=None, tree=False):
    n = len(arrs)
    lands = [lax.empty((N_DEV,) + (a.shape if gather else a.shape[1:]), a.dtype) for a in arrs]
    deps = [] if after is None else [after]

    def body(*refs):
        srcs, land_refs = refs[:n], refs[n:2 * n]
        send_sems, recv_sems = refs[2 * n + len(deps)], refs[2 * n + len(deps) + 1]
        token = refs[-1]
        for k, cp in _exchange_copies(srcs, land_refs, send_sems, recv_sems, gather):
            if not tree or k in TREE_DIRECT:
                cp.start()
        for cp in _own_copies(srcs, land_refs, send_sems, gather):
            cp.start()
        token[...] = jnp.zeros_like(token)

    hbm = lambda a: pltpu.HBM(a.shape, a.dtype)
    res = pl.pallas_call(
        body, name=name,
        out_shape=(pltpu.SemaphoreType.DMA((N_DEV * n,)), pltpu.SemaphoreType.DMA((7 * n,)), *[hbm(a) for a in arrs],
                   *[hbm(l) for l in lands], jax.ShapeDtypeStruct((8, 128), F32)),
        in_specs=[HBM_SPEC] * (2 * n) + [pl.BlockSpec(memory_space=pl.ANY)] * len(deps),
        out_specs=(SEM_SPEC, SEM_SPEC, *[HBM_SPEC] * (2 * n), pl.BlockSpec(memory_space=pltpu.VMEM)),
        input_output_aliases={i: 2 + i for i in range(2 * n)},
        compiler_params=pltpu.CompilerParams(has_side_effects=DATAFLOW),
    )(*[pltpu.with_memory_space_constraint(a, pltpu.HBM) for a in arrs + lands], *deps)
    return res[0], res[1], list(res[2:2 + n]), list(res[2 + n:2 + 2 * n]), res[-1]


def exchange_forward(name, started, after, first=0, count=None):
    send_sems, recv_sems, srcs, lands, _ = started
    count = len(srcs) - first if count is None else count
    mine = lands[first:first + count]
    n = len(mine)

    def copies(land_refs, send_ref, recv_ref):
        x, y, c = _place()
        out = []
        for t, land in enumerate(land_refs):
            for k in (2, 4, 6):
                slot = land.at[4 * (x ^ (k >> 2)) + 2 * (y ^ ((k >> 1) & 1)) + c]
                came, goes = 7 * (first + t) + k - 1, 7 * (first + t) + (k ^ 1) - 1
                out.append((
                    pltpu.make_async_remote_copy(src_ref=slot, dst_ref=slot, send_sem=send_ref.at[came], recv_sem=recv_ref.at[came],
                                                 device_id=(x, y, c), device_id_type=MESH_ID),
                    pltpu.make_async_remote_copy(src_ref=slot, dst_ref=slot, send_sem=send_ref.at[goes], recv_sem=recv_ref.at[goes],
                                                 device_id=(x, y, 1 - c), device_id_type=MESH_ID)))
        return out

    after = list(after) if isinstance(after, (list, tuple)) else [after]

    def arrived(*refs):
        for came, _ in copies(refs[:n], refs[n], refs[n + 1]):
            came.wait_recv()

    def pass_on(*refs):
        for _, goes in copies(refs[:n], refs[n], refs[n + 1]):
            goes.start()
        refs[-1][...] = jnp.zeros_like(refs[-1])

    hbm = lambda a: pltpu.HBM(a.shape, a.dtype)
    here = pl.pallas_call(
        arrived, name=name + "_arrived", out_shape=tuple(hbm(a) for a in mine),
        in_specs=[HBM_SPEC] * n + [SEM_SPEC, SEM_SPEC] + [pl.BlockSpec(memory_space=pl.ANY)] * len(after),
        out_specs=tuple([HBM_SPEC] * n), input_output_aliases={i: i for i in range(n)},
        compiler_params=pltpu.CompilerParams(has_side_effects=DATAFLOW),
    )(*mine, send_sems, recv_sems, *after)
    res = pl.pallas_call(
        pass_on, name=name, out_shape=(*[hbm(a) for a in mine], jax.ShapeDtypeStruct((8, 128), F32)),
        in_specs=[HBM_SPEC] * n + [SEM_SPEC, SEM_SPEC],
        out_specs=(*[HBM_SPEC] * n, pl.BlockSpec(memory_space=pltpu.VMEM)), input_output_aliases={i: i for i in range(n)},
        compiler_params=pltpu.CompilerParams(has_side_effects=DATAFLOW),
    )(*here, send_sems, recv_sems)
    lands = lands[:first] + list(res[:n]) + lands[first + count:]
    return (send_sems, recv_sems, srcs, lands, res[-1])


def exchange_wait(name, started, gather, after, first=0, count=None, tree=False):
    send_sems, recv_sems, srcs, lands, _ = started
    count = len(srcs) - first if count is None else count
    srcs, lands = srcs[first:first + count], lands[first:first + count]
    n = len(srcs)

    def body(*refs):
        src_refs, land_refs = refs[:n], refs[n:2 * n]
        copies = _exchange_copies(src_refs, land_refs, refs[2 * n], refs[2 * n + 1], gather, first)
        for _, cp in copies:
            cp.wait_send()
        for k, cp in copies:
            if not tree or k in (1,) + TREE_FORWARDED:
                cp.wait_recv()
        for cp in _own_copies(src_refs, land_refs, refs[2 * n], gather, first):
            cp.wait()

    hbm = lambda a: pltpu.HBM(a.shape, a.dtype)
    res = pl.pallas_call(
        body, name=name, out_shape=tuple(hbm(a) for a in srcs + lands),
        in_specs=[HBM_SPEC] * (2 * n) + [SEM_SPEC, SEM_SPEC, pl.BlockSpec(memory_space=pl.ANY)],
        out_specs=tuple([HBM_SPEC] * (2 * n)), input_output_aliases={i: i for i in range(2 * n)},
        compiler_params=pltpu.CompilerParams(has_side_effects=DATAFLOW),
    )(*srcs, *lands, send_sems, recv_sems, after)
    return list(res[:n]), list(res[n:])


def _gather_cols(stack):
    p, k, n = stack.shape
    return stack.transpose(1, 0, 2).reshape(k, p * n)


def _scatter_cols(full):
    k, n = full.shape
    return full.reshape(k, N_DEV, n // N_DEV).transpose(1, 0, 2)


def _gather_rows(stack):
    p, r, n = stack.shape
    return stack.reshape(p * r, n)


def _scatter_rows(full):
    r, n = full.shape
    return full.reshape(N_DEV, r // N_DEV, n)


_IN_NAT = Q_LORA + KV_LORA
TRANSPOSED = ("w_in", "w_q_b", "w_up")
ROWS_APART = ("w_in", "w_conv")


def to_kernel_layout(name, w):
    if name == "w_in":
        z = lambda n: jnp.zeros((n, w.shape[1]), w.dtype)
        return jnp.concatenate([w[:_IN_NAT], z(KPE_LO), w[_IN_NAT:_IN_NAT + ROPE], z(LANES - KPE_LO - ROPE), w[_IN_NAT + ROPE:]], axis=0)
    if name == "w_q_b":
        return jnp.pad(w.reshape(HEADS, NOPE + ROPE, -1), ((0, 0), (0, LANES - NOPE - ROPE), (0, 0))).reshape(HEADS * LANES, -1)
    if name == "w_o":
        mla = jnp.pad(w[:HEADS * NOPE].reshape(HEADS, NOPE, -1), ((0, 0), (LANES - NOPE, 0), (0, 0))).reshape(HEADS * LANES, -1)
        return jnp.concatenate([mla, w[HEADS * NOPE:]], axis=0)
    return w


def from_kernel_layout(name, g):
    if name == "w_in":
        return jnp.concatenate([g[:_IN_NAT], g[P_KPE + KPE_LO:P_KPE + KPE_LO + ROPE], g[P_QD:]], axis=0)
    if name == "w_q_b":
        return g.reshape(HEADS, LANES, -1)[:, :NOPE + ROPE, :].reshape(HEADS * (NOPE + ROPE), -1)
    if name == "w_o":
        mla = g[:HEADS * LANES].reshape(HEADS, LANES, -1)[:, LANES - NOPE:, :].reshape(HEADS * NOPE, -1)
        return jnp.concatenate([mla, g[HEADS * LANES:]], axis=0)
    return g


SMALL_COLS = 1024
SMALL_ROWS = 24
SMALL_AT = {"loss": (0, 0, 1), "b_ada": (1, 0, 6 * D_MODEL), "g_mix_norm": (7, 0, D_MODEL), "g_q_lat": (8, 0, Q_LORA),
            "g_kv_lat": (9, 0, KV_LORA), "g_mla_q_nope": (10, 0, NOPE), "g_mla_q_pe": (10, 128, ROPE),
            "g_mla_k_nope": (10, 256, NOPE), "g_mla_k_pe": (10, 384, ROPE), "g_dil_q": (10, 512, DIL_DIM),
            "g_dil_k": (10, 640, DIL_DIM), "g_ffn_norm": (11, 0, D_MODEL), "b_conv": (12, 0, 2 * D_FF)}
SMALL_PARAMS = tuple(n for n in SMALL_AT if n != "loss")


def _pack_small(values):
    by_row = {}
    for name, (row, off, n) in SMALL_AT.items():
        by_row.setdefault(row, []).append((off, values[name].reshape(-1).astype(F32)))
    out = []
    for row in sorted(by_row):
        pieces, at = [], 0
        for off, v in sorted(by_row[row], key=lambda t: t[0]):
            pieces += [jnp.zeros((off - at,), F32), v]
            at = off + v.shape[0]
        flat = jnp.concatenate(pieces)
        nrows = -(-flat.shape[0] // SMALL_COLS)
        out.append(jnp.pad(flat, (0, nrows * SMALL_COLS - flat.shape[0])).reshape(nrows, SMALL_COLS))
    packed = jnp.concatenate(out, axis=0)
    return jnp.pad(packed, ((0, SMALL_ROWS - packed.shape[0]), (0, 0)))


def _adam(w, g, m, v):
    c1 = 1.0 - ADAM_B1 ** ADAM_STEP
    c2 = 1.0 - ADAM_B2 ** ADAM_STEP
    m2 = ADAM_B1 * m + (1.0 - ADAM_B1) * g
    v2 = ADAM_B2 * v + (1.0 - ADAM_B2) * (g * g)
    return -ADAM_LR * ((m2 / c1) / (jnp.sqrt(v2 / c2) + ADAM_EPS) + ADAM_WD * w), m2, v2


def adamw_small(name, stack, params):
    flat = [a for n in SMALL_PARAMS for a in params[n]]

    def body(stack_ref, *refs):
        ins, outs = refs[:len(flat)], refs[len(flat):]
        g_all = stack_ref[0]
        for p in range(1, N_DEV):
            g_all = g_all + stack_ref[p]
        outs[0][...] = g_all[0:1, 0:1]
        for i, pname in enumerate(SMALL_PARAMS):
            row, off, n = SMALL_AT[pname]
            w_ref, m_ref, v_ref = ins[3 * i:3 * i + 3]
            go_ref, d_ref, mo_ref, vo_ref = outs[1 + 4 * i:5 + 4 * i]
            for c0 in range(0, n, SMALL_COLS):
                cn = min(SMALL_COLS, n - c0)
                r = row + c0 // SMALL_COLS
                g = g_all[r:r + 1, off:off + cn]
                cols = (slice(None), slice(c0, c0 + cn))
                d, m2, v2 = _adam(w_ref[cols], g, m_ref[cols], v_ref[cols])
                go_ref[cols], d_ref[cols], mo_ref[cols], vo_ref[cols] = g, d, m2, v2

    whole = lambda a: pl.BlockSpec(a.shape, lambda: (0,) * a.ndim)
    out_shape = [jax.ShapeDtypeStruct((1, 1), F32)] + [jax.ShapeDtypeStruct(a.shape, F32) for n in SMALL_PARAMS for a in params[n][:1] * 4]
    res = pl.pallas_call(body, name=name, in_specs=[whole(stack)] + [whole(a) for a in flat],
                         out_specs=[pl.BlockSpec(s.shape, lambda s=s: (0,) * len(s.shape)) for s in out_shape],
                         out_shape=out_shape, compiler_params=_params())(stack, *flat)
    return res[0], {n: res[1 + 4 * i:5 + 4 * i] for i, n in enumerate(SMALL_PARAMS)}


def _local_step(x, pos, mod, target, w, fetch, emit, halfway=lambda after: None):
    S = SEQ
    sh1, sc1, g1, sh2, sc2, g2 = [mod[:, i * D_MODEL:(i + 1) * D_MODEL] for i in range(6)]
    zeros = lambda n: jnp.zeros((1, n), F32)
    g_q = jnp.concatenate([w["g_mla_q_nope"], w["g_mla_q_pe"], zeros(LANES - NOPE - ROPE)], axis=1)
    g_k = jnp.concatenate([w["g_mla_k_nope"], zeros(LANES - NOPE)], axis=1)
    g_kpe = jnp.concatenate([zeros(KPE_LO), w["g_mla_k_pe"], zeros(LANES - KPE_LO - ROPE)], axis=1)
    g_dq = jnp.concatenate([w["g_dil_q"]] * 2, axis=1)
    g_dk = jnp.concatenate([w["g_dil_k"]] * 2, axis=1)
    b_conv = w["b_conv"]

    def inv_freq(d):
        return jnp.power(ROPE_THETA, -2.0 * jnp.arange(d // 2, dtype=F32) / d)

    n_m, n_d = ROPE // 2, DIL_DIM // 2
    freqs = jnp.concatenate([inv_freq(ROPE), inv_freq(DIL_DIM), jnp.zeros((LANES - n_m - n_d,), F32)]).reshape(1, LANES)

    def tables_fn(rows, params):
        (p,), (f,) = rows, params
        c, s = jnp.cos(p * f), jnp.sin(p * f)
        one, zero = jnp.ones_like(c), jnp.zeros_like(c)
        mla = lambda t, fill: jnp.concatenate([fill[:, :KPE_LO], t[:, :n_m], t[:, :n_m], fill[:, :LANES - KPE_LO - ROPE]], axis=1)
        dil = lambda t: jnp.concatenate([t[:, n_m:n_m + n_d]] * 4, axis=1)
        return [mla(c, one), mla(s, zero), dil(c), dil(s)], []

    cos_m, sin_m, cos_d, sin_d = rowwise("rope_tables", tables_fn, [pos], [freqs], [(LANES, F32)] * 4)
    tables = [cos_m, sin_m, cos_d, sin_d]
    H_M, H_D = ROPE // 2, DIL_DIM // 2

    def ln1_fn(rows, params):
        (xv,), (g, sc, sh) = rows, params
        y, _, _ = _rms(xv, g)
        return [y * (1.0 + sc) + sh], []

    (h,) = rowwise("ln1_fwd", ln1_fn, [x], [w["g_mix_norm"], sc1, sh1], [(D_MODEL, MXU_DTYPE)], dep=sin_d)
    w_in = fetch("w_in", h)

    def proj_fn(rows, params):
        (hv, cm, sm, cd, sd), (w_t, gq, gkv, gkp, gdq, gdk) = rows, params
        pv = _dot(hv, w_t, "nt")
        kper = _rope(_grms(pv[:, P_KPE:P_QD], gkp, KPE_GROUPS)[0], cm, sm, H_M)
        qd = [_rope(_grms(c, gdq, DIL_GROUPS)[0], cd, sd, H_D) for c in _chunks(pv[:, P_QD:P_KD])]
        kd = [_rope(_grms(c, gdk, DIL_GROUPS)[0], cd, sd, H_D) for c in _chunks(pv[:, P_KD:P_VD])]
        return [pv, _rms(pv[:, P_QLAT:P_KVLAT], gq)[0], _rms(pv[:, P_KVLAT:P_KPE], gkv)[0], kper,
                jnp.concatenate(qd, axis=1), jnp.concatenate(kd, axis=1)], []

    post_params = [w["g_q_lat"], w["g_kv_lat"], g_kpe, g_dq, g_dk]
    proj, qln, kvn, kper, qd_r, kd_r = rowwise(
        "proj_fwd", proj_fn, [h] + tables, [w_in] + post_params,
        [(P_END, F32), (Q_LORA, MXU_DTYPE), (KV_LORA, MXU_DTYPE), (LANES, MXU_DTYPE)] + [(DIL_WIDTH, F32)] * 2, tm=256)
    w_q_b, w_kv_b = fetch("w_q_b", qln), fetch("w_kv_b", kvn)

    def mla_proj_fn(rows, params):
        (qlv, kvlv, kp, cm, sm), (wq_t, wkv, gq, gk) = rows, params
        qv, kvv = _dot(qlv, wq_t, "nt"), _dot(kvlv, wkv)
        value_lanes = _lane(kp.shape) >= NOPE
        qs, ks, vs = [], [], []
        for qc, kc in zip(_chunks(qv), _chunks(kvv), strict=True):
            qs.append(_rope(_grms(qc, gq, Q_GROUPS)[0], cm, sm, H_M))
            ks.append(_grms(kc, gk, K_GROUPS)[0] + kp)
            vs.append(jnp.where(value_lanes, kc, 0.0))
        return [qv, kvv] + [jnp.concatenate(t, axis=1) for t in (qs, ks, vs)], []

    q, kv, q_mla, k_mla, v_mla = rowwise(
        "mla_proj", mla_proj_fn, [qln, kvn, kper, cos_m, sin_m], [w_q_b, w_kv_b, g_q, g_k],
        [(HEADS * LANES, F32)] * 2 + [(HEADS * LANES, MXU_DTYPE)] * 3, tm=256)
    mla_scale = (NOPE + ROPE) ** -0.5
    o_cat, lse_mla = mla_fwd("mla_fwd", q_mla, k_mla, v_mla, mla_scale)
    passed = halfway(lse_mla)

    band = [band_fwd(f"band{dil}_fwd", qd_r, kd_r, proj, dil, dep=passed) for dil in DILATIONS]
    o_cat, lse_mix = combine_fwd("dil_combine", [b[0] for b in band], [b[1] for b in band], o_cat)
    w_o = fetch("w_o", o_cat)

    def mid_fn(rows, params):
        (ov, xv), (w_out, gate1, g, sc, sh) = rows, params
        mx = _dot(ov, w_out)
        x1 = xv + gate1 * mx
        y, _, _ = _rms(x1, g)
        return [mx, x1, y * (1.0 + sc) + sh], []

    mix, x1, h2 = rowwise("mix_fwd", mid_fn, [o_cat, x], [w_o, g1, w["g_ffn_norm"], sc2, sh2],
                          [(D_MODEL, F32), (D_MODEL, F32), (D_MODEL, MXU_DTYPE)], tm=256)
    w_up, w_conv, w_down = fetch("w_up", h2), fetch("w_conv", h2), fetch("w_down", h2)
    dn, up = ffn_fwd("ffn_fwd", h2, w_up, w_conv, b_conv, w_down)

    def final_fn(rows, params):
        (x1v, dnv, tgt), (gate2,) = rows, params
        r = x1v + gate2 * dnv - tgt
        dy = r * (1.0 / D_MODEL)
        loss = jnp.sum(_colsum(r * r), axis=-1, keepdims=True) * (0.5 / D_MODEL)
        return [dy, gate2 * dy], [loss, _colsum(dy * dnv)]

    dy, d_dn, loss, dg2 = rowwise("loss_head", final_fn, [x1, dn, target], [g2], [(D_MODEL, F32), (D_MODEL, MXU_DTYPE)],
                                  [1, D_MODEL])
    dh2, g_up, g_down, g_w_conv, g_b_conv = ffn_bwd("ffn_bwd", h2, up, w_up, w_conv, b_conv, d_dn, w_down)
    emit("w_down", g_down)
    emit("w_conv", g_w_conv)
    sent = emit("w_up", g_up)

    def mid_bwd_fn(rows, params):
        (dh2v, dyv, x1v, mx), (gate1, g, sc, w_out) = rows, params
        yn, n, rstd = _rms(x1v, g)
        dx_n, dg = _rms_bwd(dh2v * (1.0 + sc), n, rstd, g)
        dx1 = dyv + dx_n
        dm = gate1 * dx1
        return [dx1, dm, _dot(dm, w_out, "nt")], [dg, _colsum(dh2v * yn), _colsum(dh2v), _colsum(dx1 * mx)]

    dx1, dmix, do_cat, dg_ffn, dsc2, dsh2, dg1 = rowwise(
        "mid_bwd", mid_bwd_fn, [dh2, dy, x1, mix], [g1, w["g_ffn_norm"], sc2, w_o],
        [(D_MODEL, F32), (D_MODEL, MXU_DTYPE), (w_o.shape[0], F32)], [D_MODEL] * 4, tm=256, dep=sent)

    sent = emit("w_o", matmul("mix_wgrad", o_cat, dmix, "tn", tm=512, out_dtype=MXU_DTYPE))
    dband = None
    for dil, b in zip(DILATIONS, band):
        dband = band_bwd(f"band{dil}_bwd", qd_r, kd_r, proj, b[1], lse_mix, o_cat, do_cat, dil, before=dband)
    dq_mla, dkv_mla, dkper = mla_bwd("mla_bwd", q_mla, k_mla, v_mla, o_cat, do_cat, lse_mla, mla_scale)

    def mla_prep_bwd_fn(rows, params):
        (dqv, dkvv, qv, kvv, cm, sm), (gq, gk) = rows, params
        nope_lanes = _lane(cm.shape) < NOPE
        dqs, dkvs, dgq, dgk = [], [], 0.0, 0.0
        for dqc, dkc, qc, kc in zip(_chunks(dqv), _chunks(dkvv), _chunks(qv), _chunks(kvv), strict=True):
            _, n, rstd = _grms(qc, gq, Q_GROUPS)
            dx, dg = _grms_bwd(_rope_bwd(dqc, cm, sm, H_M), n, rstd, gq, Q_GROUPS)
            dqs.append(dx)
            dgq = dgq + dg
            _, n, rstd = _grms(kc, gk, K_GROUPS)
            dx, dg = _grms_bwd(dkc, n, rstd, gk, K_GROUPS)
            dkvs.append(jnp.where(nope_lanes, dx, dkc))
            dgk = dgk + dg
        return [jnp.concatenate(dqs, axis=1), jnp.concatenate(dkvs, axis=1)], [dgq, dgk]

    dq, dkv, dg_q, dg_k = rowwise("mla_prep_bwd", mla_prep_bwd_fn, [dq_mla, dkv_mla, q, kv, cos_m, sin_m], [g_q, g_k],
                                  [(HEADS * LANES, MXU_DTYPE)] * 2, [LANES, LANES], tm=256, dep=sent)
    emit("w_q_b", matmul("q_wgrad", dq, qln, "tn", out_dtype=MXU_DTYPE))
    emit("w_kv_b", matmul("kv_wgrad", kvn, dkv, "tn", out_dtype=MXU_DTYPE))

    def pre_bwd_fn(rows, params):
        dqv, dkvv, dkp, dqd_, dkd_, dvd_, pv, cm, sm, cd, sd = rows
        wq_t, wkv, gq, gkv, gkp, gdq, gdk = params
        dql, dkvl = _dot(dqv, wq_t), _dot(dkvv, wkv, "nt")
        r_q = _norm_bwd(dql, pv[:, P_QLAT:P_KVLAT], gq)
        r_kv = _norm_bwd(dkvl, pv[:, P_KVLAT:P_KPE], gkv)
        _, n, rstd = _grms(pv[:, P_KPE:P_QD], gkp, KPE_GROUPS)
        r_kp = _grms_bwd(_rope_bwd(dkp, cm, sm, H_M), n, rstd, gkp, KPE_GROUPS)
        outs, dgs = [r_q[0], r_kv[0], r_kp[0]], []
        for dval, lo, g in ((dqd_, P_QD, gdq), (dkd_, P_KD, gdk)):
            dg_sum = 0.0
            for dc, xc in zip(_chunks(dval), _chunks(pv[:, lo:lo + DIL_WIDTH]), strict=True):
                _, n, rstd = _grms(xc, g, DIL_GROUPS)
                dx, dg = _grms_bwd(_rope_bwd(dc, cd, sd, H_D), n, rstd, g, DIL_GROUPS)
                outs.append(dx)
                dg_sum = dg_sum + dg
            dgs.append(dg_sum)
        return [jnp.concatenate(outs + [dvd_], axis=1)], [r_q[1], r_kv[1], r_kp[1]] + dgs

    dproj, dg_q_lat, dg_kv_lat, dg_kpe, dg_dq, dg_dk = rowwise(
        "proj_pre_bwd", pre_bwd_fn,
        [dq, dkv, dkper] + list(dband) + [proj] + tables, [w_q_b, w_kv_b] + post_params,
        [(P_END, MXU_DTYPE)], [Q_LORA, KV_LORA, LANES, LANES, LANES], tm=256)
    sent = emit("w_in", matmul("proj_wgrad", dproj, h, "tn", tn=512, out_dtype=MXU_DTYPE))

    def ln1_bwd_fn(rows, params):
        (dpv, dres, xv), (w_t, g, sc) = rows, params
        dhv = _dot(dpv, w_t)
        yn, n, rstd = _rms(xv, g)
        dx_n, dg = _rms_bwd(dhv * (1.0 + sc), n, rstd, g)
        return [dres + dx_n], [dg, _colsum(dhv * yn), _colsum(dhv)]

    grad_x, dg_mix, dsc1, dsh1 = rowwise("proj_dgrad", ln1_bwd_fn, [dproj, dx1, x], [w_in, w["g_mix_norm"], sc1],
                                         [(D_MODEL, F32)], [D_MODEL] * 3, tm=256, dep=sent)
    dmod = jnp.concatenate([dsh1, dsc1, dg1, dsh2, dsc2, dg2], axis=-1)
    small = {"loss": loss, "b_ada": dmod, "g_mix_norm": dg_mix, "g_q_lat": dg_q_lat, "g_kv_lat": dg_kv_lat,
             "g_mla_q_nope": dg_q[:, :NOPE], "g_mla_q_pe": dg_q[:, NOPE:NOPE + ROPE], "g_mla_k_nope": dg_k[:, :NOPE],
             "g_mla_k_pe": dg_kpe[:, KPE_LO:KPE_LO + ROPE], "g_dil_q": dg_dq[:, :DIL_DIM] + dg_dq[:, DIL_DIM:],
             "g_dil_k": dg_dk[:, :DIL_DIM] + dg_dk[:, DIL_DIM:], "g_ffn_norm": dg_ffn,
             "b_conv": g_b_conv}
    return grad_x, small


COL_SHARDED = ("w_kv_b", "w_conv")
ROW_SHARDED = ("w_o", "w_down") + TRANSPOSED
ADAM_TILE = {"w_ada": 256, "w_up": 176, "w_down": 176}
GATHER_GROUPS = (("w_in",), ("w_q_b", "w_kv_b"), ("w_o",), ("w_up", "w_conv", "w_down"))
START_STAGES = ((0, 1), (2, 3))
FORWARD_STAGES = ((0, 1), (2,), (3,))
FORWARD_WITH = {"w_o": 2}
SCATTER_GROUPS = (("w_down", "w_conv", "w_up"), ("w_o",), ("w_q_b", "w_kv_b", "w_in"))
OUT_WEIGHTS = ("w_ada", "b_ada", "g_mix_norm", "w_in", "g_q_lat", "w_q_b", "g_kv_lat", "w_kv_b", "g_mla_q_nope", "g_mla_q_pe",
               "g_mla_k_nope", "g_mla_k_pe", "g_dil_q", "g_dil_k", "w_o", "g_ffn_norm", "w_up", "w_conv", "b_conv", "w_down")


def kernel(x, c, positions, w_ada, b_ada, g_mix_norm, w_in, g_q_lat, w_q_b, g_kv_lat, w_kv_b, g_mla_q_nope, g_mla_q_pe, g_mla_k_nope, g_mla_k_pe, g_dil_q, g_dil_k, w_o, g_ffn_norm, w_up, w_conv, b_conv, w_down, loss_target, m_w_ada, m_b_ada, m_g_mix_norm, m_w_in, m_g_q_lat, m_w_q_b, m_g_kv_lat, m_w_kv_b, m_g_mla_q_nope, m_g_mla_q_pe, m_g_mla_k_nope, m_g_mla_k_pe, m_g_dil_q, m_g_dil_k, m_w_o, m_g_ffn_norm, m_w_up, m_w_conv, m_b_conv, m_w_down, v_w_ada, v_b_ada, v_g_mix_norm, v_w_in, v_g_q_lat, v_w_q_b, v_g_kv_lat, v_w_kv_b, v_g_mla_q_nope, v_g_mla_q_pe, v_g_mla_k_nope, v_g_mla_k_pe, v_g_dil_q, v_g_dil_k, v_w_o, v_g_ffn_norm, v_w_up, v_w_conv, v_b_conv, v_w_down):
    args = dict(locals())
    xi, yi, ci = _place()
    me = 4 * xi + 2 * yi + ci
    def local(prefix, n):
        a = args[prefix + n]
        if n in ROWS_APART:
            return jnp.transpose(a, (2, 0, 1) if n in TRANSPOSED else (1, 0, 2))
        return a[0].T if n in TRANSPOSED else a[0]

    def as_output(n, r):
        if n in ROWS_APART:
            return jnp.transpose(r, (1, 2, 0) if n in TRANSPOSED else (1, 0, 2))
        return (r.T if n in TRANSPOSED else r)[None]

    shard = {n: local("", n) for n in COL_SHARDED + ROW_SHARDED + ("w_ada",)}
    flat = lambda n, a: a.reshape(a.shape[0], a.shape[-1]) if n in ROWS_APART else a
    small_w = {n: args[n] for n in SMALL_PARAMS}

    payload = {n: flat(n, shard[n]) if n == "w_conv" else flat(n, shard[n]).astype(MXU_DTYPE) for n in COL_SHARDED + ROW_SHARDED}
    start_order = [[n for i in groups for n in GATHER_GROUPS[i]] for groups in START_STAGES]

    sc_all, mod_all = ada_modulation("ada_mod", c, shard["w_ada"], after=[payload[n] for n in start_order[0]])

    exchange_of = lambda i: [e for e, groups in enumerate(START_STAGES) if i in groups][0]
    start_stage = lambda e, after: exchange_start(f"gather_start{e}", [payload[n] for n in start_order[e]], gather=True,
                                                  after=after, tree=True)
    gathered = {0: start_stage(0, mod_all)}
    after_start = gathered[0][-1]
    full, forwarded = {}, set()

    def forward(stage, after):
        e = exchange_of(FORWARD_STAGES[stage][0])
        if stage not in forwarded:
            forwarded.add(stage)
            first = start_order[e].index(GATHER_GROUPS[FORWARD_STAGES[stage][0]][0])
            count = sum(len(GATHER_GROUPS[i]) for i in FORWARD_STAGES[stage])
            starts_next = e + 1 < len(START_STAGES) and e + 1 not in gathered
            ready = [payload[n] for n in start_order[e + 1]] if starts_next else []
            gathered[e] = exchange_forward(f"gather_forward{stage}", gathered[e], [after] + ready, first, count)
            if starts_next:
                gathered[e + 1] = start_stage(e + 1, gathered[e][-1])
        return gathered[e][-1]

    def fetch(name, after):
        if name not in full:
            (i, grp), = [(i, grp) for i, grp in enumerate(GATHER_GROUPS) if name in grp]
            (stage,) = [s for s, groups in enumerate(FORWARD_STAGES) if i in groups]
            forward(stage, after)
            if name in FORWARD_WITH:
                forward(FORWARD_WITH[name], after)
            e = exchange_of(i)
            behind = gathered[e + 1][-1] if e + 1 in gathered else after
            srcs, lands = exchange_wait(f"gather{i}_wait", gathered[e], True, behind, start_order[e].index(grp[0]), len(grp), tree=True)
            for n, stack in zip(grp, lands, strict=True):
                full[n] = to_kernel_layout(n, _gather_cols(stack) if n in COL_SHARDED else _gather_rows(stack))
        return full[name]

    mod_row = lax.dynamic_index_in_dim(mod_all, me, axis=1, keepdims=False).reshape(1, 6 * D_MODEL)
    (mod,) = rowwise("ada_bias", lambda rows, params: ([rows[0] + rows[1]], []), [mod_row, b_ada], [], [(6 * D_MODEL, F32)],
                     dep=after_start)

    pending, scatters = {}, {}

    def emit(name, grad):
        grad = from_kernel_layout(name, grad)
        pending[name] = _scatter_cols(grad) if name in COL_SHARDED else _scatter_rows(grad)
        for i, grp in enumerate(SCATTER_GROUPS):
            if name == grp[-1]:
                scatters[i] = exchange_start(f"scatter{i}_start", [pending[n] for n in grp], gather=False)
                return scatters[i][-1]
        return None

    pos = positions.reshape(SEQ, 1).astype(F32)
    grad_x, small = _local_step(x[0], pos, mod, loss_target[0], small_w, fetch, emit, halfway=lambda after: forward(1, after))

    small_sent = exchange_start("small_start", [_pack_small(small)], gather=True, after=grad_x)

    res, done = {}, small_sent[-1]
    for i, grp in enumerate(SCATTER_GROUPS):
        _, lands = exchange_wait(f"scatter{i}_wait", scatters[i], False, done)
        for n, land in zip(grp, lands, strict=True):
            res[n] = adamw(f"adamw_{n}", shard[n], [land], local("m_", n), local("v_", n), ADAM_TILE.get(n))
            done = res[n][0]
            res[n] = [as_output(n, r) for r in res[n]]
    _, (small_all,) = exchange_wait("small_wait", small_sent, True, done)
    loss, small_res = adamw_small("adamw_small", small_all, {n: (args[n], args["m_" + n], args["v_" + n]) for n in SMALL_PARAMS})
    row, _, n_mod = SMALL_AT["b_ada"]
    dmod_all = small_all[:, row:row + n_mod // SMALL_COLS, :].reshape(N_DEV, n_mod)
    dmod_mine = lax.dynamic_slice_in_dim(dmod_all, me * (6 * D_MODEL // N_DEV), 6 * D_MODEL // N_DEV, axis=1)
    g_w_ada = matmul("ada_wgrad", sc_all, dmod_mine, "tn")
    res["w_ada"] = [r[None] for r in adamw("adamw_w_ada", shard["w_ada"], [g_w_ada], m_w_ada[0], v_w_ada[0], ADAM_TILE["w_ada"])]

    def leaf(kind, n):
        return res[n][kind] if n in res else small_res[n][kind]

    return (loss.reshape(()), grad_x[None], *[leaf(k, n) for k in range(4) for n in OUT_WEIGHTS])
```

```python
import jax
import jax.numpy as jnp
from jax import lax
from jax.experimental import pallas as pl
from jax.experimental.pallas import tpu as pltpu

F32 = jnp.float32
MXU_DTYPE = jnp.bfloat16

N_DEV = 8
D_MODEL = 1024
SEQ = 2048
HEADS = 8
NOPE = 64
ROPE = 32
Q_LORA = 512
KV_LORA = 256
DIL_DIM = 64
DIL_WIDTH = HEADS * DIL_DIM
DILATIONS = (1, 4, 16)
SPAN = 128
D_FF = 2816
LANES = 128
SUBLANES = 8
ROPE_THETA = 10000.0
EPS = 1e-6
NEG_INF = -1e30
ADAM_LR, ADAM_B1, ADAM_B2, ADAM_EPS, ADAM_WD, ADAM_STEP = 0.001, 0.9, 0.999, 1e-08, 0.01, 10
VMEM_LIMIT = 56 * 1024 * 1024
MESH_ID = pl.DeviceIdType.MESH

P_QLAT, P_KVLAT, P_KPE, P_QD, P_KD, P_VD, P_END = 0, 512, 768, 896, 1408, 1920, 2432
KPE_LO = 64
MIX_IN = HEADS * LANES + DIL_WIDTH


def _params(**kw):
    return pltpu.CompilerParams(vmem_limit_bytes=VMEM_LIMIT, **kw)


def rowwise(name, fn, rows, params, out_rows, out_accs=(), tm=512, dep=None):
    deps = [] if dep is None else [dep]
    rows = [r if isinstance(r, tuple) else (r, r.shape[1], 0) for r in rows]
    R = rows[0][0].shape[0]
    tm = min(tm, R)
    steps = R // tm
    assert steps * tm == R
    in_specs = []
    for a, width, cb in rows:
        ri = a.shape[0]
        per = ri // tm
        assert per * tm == ri
        if ri == R:
            in_specs.append(pl.BlockSpec((tm, width), lambda i, cb=cb: (i, cb)))
        else:
            in_specs.append(pl.BlockSpec((tm, width), lambda i, per=per, cb=cb: (i % per, cb)))
    for p in params:
        in_specs.append(pl.BlockSpec(p.shape, lambda i: (0,) * p.ndim))
    in_specs += [pl.BlockSpec(memory_space=pl.ANY)] * len(deps)
    out_shape = [jax.ShapeDtypeStruct((R, d), dt) for d, dt in out_rows]
    out_specs = [pl.BlockSpec((tm, d), lambda i: (i, 0)) for d, _ in out_rows]
    out_shape += [jax.ShapeDtypeStruct((1, n), F32) for n in out_accs]
    out_specs += [pl.BlockSpec((1, n), lambda i: (0, 0)) for n in out_accs]
    nr, npar, no, na = len(rows), len(params), len(out_rows), len(out_accs)

    def body(*refs):
        rvals = [r[...] for r in refs[:nr]]
        pvals = [r[...] for r in refs[nr:nr + npar]]
        outs, accs = fn(rvals, pvals)
        first_out = nr + npar + len(deps)
        for ref, v in zip(refs[first_out:first_out + no], outs, strict=True):
            ref[...] = v.astype(ref.dtype)
        if na:
            acc_refs = refs[first_out + no:]
            i = pl.program_id(0)

            @pl.when(i == 0)
            def _():
                for ref, v in zip(acc_refs, accs, strict=True):
                    ref[...] = v

            @pl.when(i > 0)
            def _():
                for ref, v in zip(acc_refs, accs, strict=True):
                    ref[...] += v

    res = pl.pallas_call(body, name=name, grid=(steps,), in_specs=in_specs, out_specs=out_specs,
                         out_shape=out_shape, compiler_params=_params())(*[r[0] for r in rows], *params, *deps)
    return list(res)


_DIMS = {"nn": ((1,), (0,)), "nt": ((1,), (1,)), "tn": ((0,), (0,))}


def _dot(a, b, mode="nn"):
    return lax.dot_general(a.astype(MXU_DTYPE), b.astype(MXU_DTYPE), (_DIMS[mode], ((), ())),
                           preferred_element_type=F32)


def matmul(name, a, b, mode, tm=None, tn=None, tk=None, out_dtype=F32, dep=None):
    if mode == "tn":
        K, M = a.shape
    else:
        M, K = a.shape
    N = b.shape[0] if mode == "nt" else b.shape[1]
    tm, tn, tk = tm or M, tn or N, tk or K
    nm, nn, nk = M // tm, N // tn, K // tk
    assert nm * tm == M and nn * tn == N and nk * tk == K
    a_spec = pl.BlockSpec((tk, tm), lambda i, j, k: (k, i)) if mode == "tn" else pl.BlockSpec((tm, tk), lambda i, j, k: (i, k))
    b_spec = pl.BlockSpec((tn, tk), lambda i, j, k: (j, k)) if mode == "nt" else pl.BlockSpec((tk, tn), lambda i, j, k: (k, j))
    deps = [] if dep is None else [dep]

    def body(a_ref, b_ref, *rest):
        o_ref, scratch = rest[len(deps)], rest[len(deps) + 1:]
        p = _dot(a_ref[...], b_ref[...], mode)
        if nk == 1:
            o_ref[...] = p.astype(o_ref.dtype)
        else:
            acc = scratch[0]
            k = pl.program_id(2)

            @pl.when(k == 0)
            def _():
                acc[...] = p

            @pl.when(k > 0)
            def _():
                acc[...] += p

            @pl.when(k == nk - 1)
            def _():
                o_ref[...] = acc[...].astype(o_ref.dtype)

    return pl.pallas_call(
        body, name=name, grid=(nm, nn, nk), in_specs=[a_spec, b_spec] + [pl.BlockSpec(memory_space=pl.ANY)] * len(deps),
        out_specs=pl.BlockSpec((tm, tn), lambda i, j, k: (i, j)),
        out_shape=jax.ShapeDtypeStruct((M, N), out_dtype),
        scratch_shapes=[pltpu.VMEM((tm, tn), F32)] if nk > 1 else [],
        compiler_params=_params())(a, b, *deps)


def _rms(x, g):
    rstd = lax.rsqrt(jnp.mean(x * x, axis=-1, keepdims=True) + EPS)
    n = x * rstd
    return n * g, n, rstd


def _rms_bwd(dy, n, rstd, g):
    dg = jnp.sum(dy * n, axis=0, keepdims=True)
    dn = dy * g
    dx = rstd * (dn - n * jnp.mean(dn * n, axis=-1, keepdims=True))
    return dx, dg


def _norm_bwd(dy, x, g):
    _, n, rstd = _rms(x, g)
    return _rms_bwd(dy, n, rstd, g)


def _colsum(v):
    return jnp.sum(v, axis=0, keepdims=True)


def _silu(x):
    return x * (1.0 / (1.0 + jnp.exp(-x)))


def _lane(shape):
    return lax.broadcasted_iota(jnp.int32, shape, 1)


def _group_mean(v, groups):
    i = lax.broadcasted_iota(jnp.int32, (LANES, LANES), 0)
    j = lax.broadcasted_iota(jnp.int32, (LANES, LANES), 1)
    g = jnp.zeros((LANES, LANES), F32)
    for lo, hi in groups:
        g = jnp.where((i >= lo) & (i < hi) & (j >= lo) & (j < hi), 1.0 / (hi - lo), g)
    head = v.astype(MXU_DTYPE)
    return _dot(head, g) + _dot(v - head.astype(F32), g)


def _in_groups(shape, groups):
    lane = _lane(shape)
    m = jnp.zeros(shape, jnp.bool_)
    for lo, hi in groups:
        m = m | ((lane >= lo) & (lane < hi))
    return m


def _grms(x, g, groups):
    rstd = lax.rsqrt(_group_mean(x * x, groups) + EPS)
    n = jnp.where(_in_groups(x.shape, groups), x * rstd, 0.0)
    return n * g, n, rstd


def _grms_bwd(dy, n, rstd, g, groups):
    dn = dy * g
    return rstd * (dn - n * _group_mean(dn * n, groups)), _colsum(dy * n)


def _rot(x, half, transpose=False):
    first = (_lane(x.shape) % (2 * half)) < half
    up = pltpu.roll(x, LANES - half, axis=1)
    down = pltpu.roll(x, half, axis=1)
    return jnp.where(first, up, -down) if transpose else jnp.where(first, -up, down)


def _rope(x, cos, sin, half):
    return x * cos + _rot(x, half) * sin


def _rope_bwd(dy, cos, sin, half):
    return dy * cos + _rot(dy * sin, half, transpose=True)


def _chunks(x):
    return [x[:, i:i + LANES] for i in range(0, x.shape[1], LANES)]


Q_GROUPS = ((0, NOPE), (NOPE, NOPE + ROPE))
K_GROUPS = ((0, NOPE),)
KPE_GROUPS = ((KPE_LO, KPE_LO + ROPE),)
DIL_GROUPS = ((0, DIL_DIM), (DIL_DIM, 2 * DIL_DIM))


def _col(width, rows=SEQ):
    return pl.BlockSpec((rows, width), lambda h: (0, h))


def _causal_tail(s, tq, fill):
    diag = s[:, s.shape[1] - tq:]
    keep = lax.broadcasted_iota(jnp.int32, diag.shape, 1) <= lax.broadcasted_iota(jnp.int32, diag.shape, 0)
    diag = jnp.where(keep, diag, fill)
    return diag if s.shape[1] == tq else jnp.concatenate([s[:, :s.shape[1] - tq], diag], axis=1)


def mla_fwd(name, q, k, v, scale, tq=256):
    S = q.shape[0]

    def body(q_ref, k_ref, v_ref, o_ref, lse_ref):
        nb = S // tq
        blk = lambda i: slice(i * tq, (i + 1) * tq)

        def scores(i):
            return _dot(q_ref[blk(i), :], k_ref[:(i + 1) * tq, :], "nt")

        def softmax(i, s):
            s = _causal_tail(s * scale, tq, NEG_INF)
            m = jnp.max(s, axis=-1, keepdims=True)
            e = jnp.exp(s - m)
            l = jnp.sum(e, axis=-1, keepdims=True)
            lse_ref[0, blk(i), :] = m + jnp.log(l)
            return (e * (1.0 / l)).astype(MXU_DTYPE)

        def weighted(i, p):
            o_ref[blk(i), :] = _dot(p, v_ref[:(i + 1) * tq, :])

        s, p_prev = scores(0), None
        for i in range(nb):
            s_next = scores(i + 1) if i + 1 < nb else None
            if p_prev is not None:
                weighted(i - 1, p_prev)
            p_prev, s = softmax(i, s), s_next
        weighted(nb - 1, p_prev)

    return pl.pallas_call(
        body, name=name, grid=(HEADS,), in_specs=[_col(LANES)] * 3,
        out_specs=[_col(LANES), pl.BlockSpec((1, S, 1), lambda h: (h, 0, 0))],
        out_shape=[jax.ShapeDtypeStruct((S, MIX_IN), F32), jax.ShapeDtypeStruct((HEADS, S, 1), F32)],
        compiler_params=_params())(q, k, v)


def mla_bwd(name, q, k, v, o, do, lse, scale, tq=256):
    S = q.shape[0]

    def body(q_ref, k_ref, v_ref, o_ref, do_ref, lse_ref, dq_ref, dkv_ref, dkpe_ref, dk_acc, dv_acc):
        dk_acc[...] = jnp.zeros_like(dk_acc)
        dv_acc[...] = jnp.zeros_like(dv_acc)
        for i in range(S // tq):
            kext = (i + 1) * tq
            blk = slice(i * tq, kext)
            qi, kk, vv = q_ref[blk, :], k_ref[:kext, :], v_ref[:kext, :]
            doi = do_ref[blk, :]
            s = _causal_tail(_dot(qi, kk, "nt") * scale, tq, NEG_INF)
            p = jnp.exp(s - lse_ref[0, blk, :])
            dp = _dot(doi, vv, "nt")
            delta = jnp.sum(doi * o_ref[blk, :], axis=-1, keepdims=True)
            ds = p * (dp - delta) * scale
            dq_ref[blk, :] = _dot(ds, kk)
            dk_acc[:kext, :] += _dot(ds, qi, "tn")
            dv_acc[:kext, :] += _dot(p, doi, "tn")
        dk = dk_acc[...]
        lane = _lane(dk.shape)
        dkv_ref[...] = jnp.where(lane < NOPE, dk, 0.0) + dv_acc[...]
        dkpe = jnp.where((lane >= KPE_LO) & (lane < KPE_LO + ROPE), dk, 0.0)
        h = pl.program_id(0)

        @pl.when(h == 0)
        def _():
            dkpe_ref[...] = dkpe

        @pl.when(h > 0)
        def _():
            dkpe_ref[...] += dkpe

    return pl.pallas_call(
        body, name=name, grid=(HEADS,),
        in_specs=[_col(LANES)] * 5 + [pl.BlockSpec((1, S, 1), lambda h: (h, 0, 0))],
        out_specs=[_col(LANES), _col(LANES), pl.BlockSpec((S, LANES), lambda h: (0, 0))],
        out_shape=[jax.ShapeDtypeStruct((S, HEADS * LANES), F32), jax.ShapeDtypeStruct((S, HEADS * LANES), F32),
                   jax.ShapeDtypeStruct((S, LANES), F32)],
        scratch_shapes=[pltpu.VMEM((S, LANES), F32), pltpu.VMEM((S, LANES), F32)],
        compiler_params=_params())(q, k, v, o, do, lse)


BAND_TQ = SPAN


def _band_blocks(L, tq):
    return [(i * tq, (i + 1) * tq, max(0, i * tq - SPAN)) for i in range(L // tq)]


def _class_rows(r, dil, lo, hi):
    return pl.ds(r + dil * lo, hi - lo, stride=dil) if dil > 1 else pl.ds(lo, hi - lo)


def _stack_heads(t, lo):
    zero = jnp.zeros_like(t)
    return jnp.concatenate([jnp.where(lo, t, zero), jnp.where(lo, zero, t)], axis=0)


def _band_mask2(q0, q1, k0):
    n = q1 - q0
    shape = (2 * n, q1 - k0)
    i = lax.broadcasted_iota(jnp.int32, shape, 0)
    dist = (jnp.where(i >= n, i - n, i) + q0) - (lax.broadcasted_iota(jnp.int32, shape, 1) + k0)
    return (dist >= 0) & (dist <= SPAN)


def _pair_col(col0=0):
    return pl.BlockSpec((SEQ, LANES), lambda j: (0, col0 // LANES + j))


def band_fwd(name, q, k, v, dil, dep=None):
    S = q.shape[0]
    L = S // dil
    tq = BAND_TQ
    scale = DIL_DIM ** -0.5
    deps = [] if dep is None else [dep]

    def body(q_ref, k_ref, v_ref, *rest):
        o_ref, lse_ref = rest[len(deps):]
        items = [(r, blk) for r in range(dil) for blk in _band_blocks(L, tq)]
        lo = _lane((tq, LANES)) < DIL_DIM

        def scores(item):
            r, (q0, q1, k0) = item
            qb = q_ref[_class_rows(r, dil, q0, q1), :].astype(MXU_DTYPE)
            return _dot(_stack_heads(qb, lo), k_ref[_class_rows(r, dil, k0, q1), :], "nt")

        def softmax(item, s):
            _, (q0, q1, k0) = item
            s = jnp.where(_band_mask2(q0, q1, k0), s * scale, NEG_INF)
            mx = jnp.max(s, axis=-1, keepdims=True)
            e = jnp.exp(s - mx)
            l = jnp.sum(e, axis=-1, keepdims=True)
            return (e * (1.0 / l)).astype(MXU_DTYPE), mx + jnp.log(l)

        def weighted(item, p, lse):
            r, (q0, q1, k0) = item
            pv = _dot(p, v_ref[_class_rows(r, dil, k0, q1), :])
            o_ref[_class_rows(r, dil, q0, q1), :] = jnp.where(lo, pv[:tq], pv[tq:])
            lse_ref[_class_rows(r, dil, q0, q1), :] = jnp.where(lo, lse[:tq], lse[tq:])

        s, prev = scores(items[0]), None
        for i, item in enumerate(items):
            s_next = scores(items[i + 1]) if i + 1 < len(items) else None
            if prev is not None:
                weighted(items[i - 1], *prev)
            prev, s = softmax(item, s), s_next
        weighted(items[-1], *prev)

    return pl.pallas_call(
        body, name=name, grid=(DIL_WIDTH // LANES,),
        in_specs=[_pair_col()] * 2 + [_pair_col(P_VD)] + [pl.BlockSpec(memory_space=pl.ANY)] * len(deps), out_specs=[_pair_col()] * 2,
        out_shape=[jax.ShapeDtypeStruct((S, DIL_WIDTH), F32)] * 2, compiler_params=_params())(q, k, v, *deps)


def band_bwd(name, q, k, v, lse, lse_mix, o_cat, do_cat, dil, before=None):
    S = q.shape[0]
    L = S // dil
    tq = BAND_TQ
    scale = DIL_DIM ** -0.5
    before = list(before or [])

    def body(q_ref, k_ref, v_ref, lse_ref, mix_ref, o_ref, do_ref, *rest):
        dq_ref, dk_ref, dv_ref = rest[len(before):]
        if before:
            dq0_ref, dk0_ref, dv0_ref = rest[:3]
            dk_ref[...] = dk0_ref[...]
            dv_ref[...] = dv0_ref[...]
        else:
            dk_ref[...] = jnp.zeros_like(dk_ref)
            dv_ref[...] = jnp.zeros_like(dv_ref)
        items = [(r, blk) for r in range(dil) for blk in _band_blocks(L, tq)]
        lo = _lane((tq, LANES)) < DIL_DIM
        per_head = lambda t: jnp.concatenate([t[:, 0:1], t[:, DIL_DIM:DIL_DIM + 1]], axis=0)

        def scores(item):
            r, (q0, q1, k0) = item
            qrows, krows = _class_rows(r, dil, q0, q1), _class_rows(r, dil, k0, q1)
            lse_p, dout = lse_ref[qrows, :], do_ref[qrows, :]
            w2 = per_head(jnp.exp(lse_p - mix_ref[qrows, :]))
            dd = dout * o_ref[qrows, :]
            big_d = jnp.concatenate([jnp.sum(jnp.where(lo, dd, 0.0), axis=-1, keepdims=True),
                                     jnp.sum(jnp.where(lo, 0.0, dd), axis=-1, keepdims=True)], axis=0)
            q2 = _stack_heads(q_ref[qrows, :].astype(MXU_DTYPE), lo)
            dom = (_stack_heads(dout, lo) * w2).astype(MXU_DTYPE)
            return (_dot(q2, k_ref[krows, :], "nt"), _dot(dom, v_ref[krows, :], "nt"), per_head(lse_p), w2 * big_d, q2, dom)

        def softmax_bwd(item, s, dp, lse2, wd2, q2, dom):
            _, (q0, q1, k0) = item
            p = jnp.where(_band_mask2(q0, q1, k0), jnp.exp(s * scale - lse2), 0.0)
            return p.astype(MXU_DTYPE), (p * (dp - wd2) * scale).astype(MXU_DTYPE), q2, dom

        def grads(item, p, ds, q2, dom):
            r, (q0, q1, k0) = item
            qrows, krows = _class_rows(r, dil, q0, q1), _class_rows(r, dil, k0, q1)
            dq2 = _dot(ds, k_ref[krows, :])
            dq = jnp.where(lo, dq2[:tq], dq2[tq:])
            dq_ref[qrows, :] = dq + dq0_ref[qrows, :] if before else dq
            dk_ref[krows, :] += _dot(ds, q2, "tn")
            dv_ref[krows, :] += _dot(p, dom, "tn")

        sc, prev = scores(items[0]), None
        for i, item in enumerate(items):
            sc_next = scores(items[i + 1]) if i + 1 < len(items) else None
            if prev is not None:
                grads(items[i - 1], *prev)
            prev, sc = softmax_bwd(item, *sc), sc_next
        grads(items[-1], *prev)

    cat = _pair_col(HEADS * LANES)
    return pl.pallas_call(
        body, name=name, grid=(DIL_WIDTH // LANES,),
        in_specs=[_pair_col()] * 2 + [_pair_col(P_VD)] + [_pair_col()] * 2 + [cat] * 2 + [_pair_col()] * len(before),
        out_specs=[_pair_col()] * 3, out_shape=[jax.ShapeDtypeStruct((S, DIL_WIDTH), F32)] * 3,
        compiler_params=_params())(q, k, v, lse, lse_mix, o_cat, do_cat, *before)


def combine_fwd(name, outs, lses, o_cat, tm=512):
    S = outs[0].shape[0]

    def body(o1, o2, o3, l1, l2, l3, cat_in, cat_out, mix_ref):
        ls = [l1[...], l2[...], l3[...]]
        m = jnp.maximum(jnp.maximum(ls[0], ls[1]), ls[2])
        e = [jnp.exp(l - m) for l in ls]
        den = e[0] + e[1] + e[2]
        cat_out[...] = (e[0] / den) * o1[...] + (e[1] / den) * o2[...] + (e[2] / den) * o3[...]
        mix_ref[...] = m + jnp.log(den)

    row = pl.BlockSpec((tm, DIL_WIDTH), lambda i: (i, 0))
    return pl.pallas_call(
        body, name=name, grid=(S // tm,), in_specs=[row] * 6 + [pl.BlockSpec(memory_space=pl.ANY)],
        out_specs=[pl.BlockSpec((tm, DIL_WIDTH), lambda i: (i, HEADS * LANES // DIL_WIDTH)), row],
        out_shape=[jax.ShapeDtypeStruct(o_cat.shape, F32), jax.ShapeDtypeStruct((S, DIL_WIDTH), F32)],
        input_output_aliases={6: 0}, compiler_params=_params())(*outs, *lses, o_cat)


FFN_FWD_ROWS = 512
FFN_BWD_ROWS = 256
CONV_PAD = SUBLANES


def _window(x, k):
    groups = x.reshape(-1, SUBLANES, x.shape[1])
    turned = pltpu.roll(groups, SUBLANES - k, axis=1)
    stays = lax.broadcasted_iota(jnp.int32, (groups.shape[0] - 1,) + groups.shape[1:], 1) < SUBLANES - k
    return jnp.where(stays, turned[:-1], turned[1:]).reshape(-1, x.shape[1])


def _earlier(ref, r0, rows, n):
    if r0 == 0:
        x = jnp.concatenate([jnp.zeros((SUBLANES, ref.shape[1]), F32), ref[:rows, :]], axis=0)
    else:
        x = ref[r0 - SUBLANES:r0 + rows, :]
    return _window(x, SUBLANES - n)


CONV_TC = 256
CONV_NB = D_FF // CONV_TC


def _half_specs(rows, rows_axis=False):
    if rows_axis:
        return [pl.BlockSpec((rows, D_MODEL), lambda j: (j, 0)), pl.BlockSpec((rows, D_MODEL), lambda j: (j + CONV_NB, 0))]
    return [pl.BlockSpec((rows, CONV_TC), lambda j: (0, j)), pl.BlockSpec((rows, CONV_TC), lambda j: (0, j + CONV_NB))]


def _whole(a):
    return pl.BlockSpec(a.shape, lambda j: (0,) * a.ndim)


def _up_pair(h, ug_ref, uv_ref):
    return jnp.concatenate([_dot(h, ug_ref[...], "nt"), _dot(h, uv_ref[...], "nt")], axis=1)


def _conv_taps(up_ref, r0, rows, w, b):
    uin, u1, u2 = up_ref[r0:r0 + rows, :], _earlier(up_ref, r0, rows, 1), _earlier(up_ref, r0, rows, 2)
    return uin, u1, u2, w[2:3, :] * uin + w[1:2, :] * u1 + w[0:1, :] * u2 + b


def ffn_fwd(name, h, w_up_t, w_conv, b_conv, w_down):
    S = h.shape[0]

    def body(h_ref, ug_ref, uv_ref, wg_ref, wv_ref, bg_ref, bv_ref, wd_ref, dn_ref, up_ref):
        @pl.when(pl.program_id(0) == 0)
        def _():
            dn_ref[...] = jnp.zeros_like(dn_ref)

        w = jnp.concatenate([wg_ref[...], wv_ref[...]], axis=1)
        b = jnp.concatenate([bg_ref[...], bv_ref[...]], axis=1)
        rows = FFN_FWD_ROWS
        starts = list(range(0, S, rows))

        def project(r0):
            up_ref[r0:r0 + rows, :] = _up_pair(h_ref[r0:r0 + rows, :], ug_ref, uv_ref)

        def gate(r0):
            u = _conv_taps(up_ref, r0, rows, w, b)[3]
            return (_silu(u[:, :CONV_TC]) * u[:, CONV_TC:]).astype(MXU_DTYPE)

        def project_down(r0, act):
            dn_ref[r0:r0 + rows, :] += _dot(act, wd_ref[...])

        project(starts[0])
        act_prev = None
        for i, r0 in enumerate(starts):
            if i + 1 < len(starts):
                project(starts[i + 1])
            if act_prev is not None:
                project_down(starts[i - 1], act_prev)
            act_prev = gate(r0)
        project_down(starts[-1], act_prev)

    return pl.pallas_call(
        body, name=name, grid=(CONV_NB,),
        in_specs=[_whole(h)] + _half_specs(CONV_TC, rows_axis=True) + _half_specs(3) + _half_specs(1)
        + [pl.BlockSpec((CONV_TC, w_down.shape[1]), lambda j: (j, 0))],
        out_specs=[pl.BlockSpec((S, w_down.shape[1]), lambda j: (0, 0)), pl.BlockSpec((S, 2 * CONV_TC), lambda j: (0, j))],
        out_shape=[jax.ShapeDtypeStruct((S, w_down.shape[1]), F32), jax.ShapeDtypeStruct((S, 2 * D_FF), F32)],
        compiler_params=_params())(h, w_up_t, w_up_t, w_conv, w_conv, b_conv, b_conv, w_down)


def ffn_bwd(name, h, up, w_up_t, w_conv, b_conv, d_dn, w_down):
    S, D = h.shape

    def body(h_ref, up_ref, ug_ref, uv_ref, wg_ref, wv_ref, bg_ref, bv_ref, dd_ref, wd_ref,
             dh_ref, gup_ref, gd_ref, dwg_ref, dwv_ref, dbg_ref, dbv_ref, du_ref, dup_ref, act_ref):
        @pl.when(pl.program_id(0) == 0)
        def _():
            dh_ref[...] = jnp.zeros_like(dh_ref)

        w = jnp.concatenate([wg_ref[...], wv_ref[...]], axis=1)
        b = jnp.concatenate([bg_ref[...], bv_ref[...]], axis=1)
        w_pair = jnp.concatenate([ug_ref[...], uv_ref[...]], axis=0)
        rows = FFN_BWD_ROWS
        starts = list(range(0, S, rows))
        du_ref[S:S + CONV_PAD, :] = jnp.zeros((CONV_PAD, 2 * CONV_TC), F32)

        def project(r0):
            return _dot(dd_ref[r0:r0 + rows, :], wd_ref[...], "nt")

        def through_conv(r0, da):
            uin, u1, u2, u = _conv_taps(up_ref, r0, rows, w, b)
            gate, val = u[:, :CONV_TC], u[:, CONV_TC:]
            sig = 1.0 / (1.0 + jnp.exp(-gate))
            du = jnp.concatenate([da * val * (sig * (1.0 + gate * (1.0 - sig))), da * (gate * sig)], axis=1)
            du_ref[r0:r0 + rows, :] = du
            act_ref[r0:r0 + rows, :] = (gate * sig * val).astype(MXU_DTYPE)
            dw = jnp.concatenate([_colsum(du * u2), _colsum(du * u1), _colsum(du * uin)], axis=0)
            return dw, _colsum(du)

        def back_up(r0):
            du = du_ref[r0:r0 + rows + CONV_PAD, :]
            dup = (w[2:3, :] * du[:rows] + w[1:2, :] * _window(du, 1) + w[0:1, :] * _window(du, 2)).astype(MXU_DTYPE)
            dup_ref[r0:r0 + rows, :] = dup
            dh_ref[r0:r0 + rows, :] += _dot(dup, w_pair)

        dw, db = 0.0, 0.0
        da = project(starts[0])
        for i, r0 in enumerate(starts):
            da_next = project(starts[i + 1]) if i + 1 < len(starts) else None
            dw_c, db_c = through_conv(r0, da)
            if i > 0:
                back_up(starts[i - 1])
            dw, db, da = dw + dw_c, db + db_c, da_next
        back_up(starts[-1])
        g_up, g_dn = _dot(dup_ref[...], h_ref[...], "tn"), _dot(act_ref[...], dd_ref[...], "tn")
        gup_ref[0], gup_ref[1] = g_up[:CONV_TC].astype(gup_ref.dtype), g_up[CONV_TC:].astype(gup_ref.dtype)
        gd_ref[...] = g_dn.astype(gd_ref.dtype)
        dwg_ref[...], dwv_ref[...] = dw[:, :CONV_TC], dw[:, CONV_TC:]
        dbg_ref[...], dbv_ref[...] = db[:, :CONV_TC], db[:, CONV_TC:]

    half = lambda rows: pl.BlockSpec((rows, CONV_TC), lambda j: (0, j))
    rows_blk = pl.BlockSpec((CONV_TC, D), lambda j: (j, 0))
    dh, gup, gd, dwg, dwv, dbg, dbv = pl.pallas_call(
        body, name=name, grid=(CONV_NB,),
        in_specs=[_whole(h), pl.BlockSpec((S, 2 * CONV_TC), lambda j: (0, j))] + _half_specs(CONV_TC, rows_axis=True) + _half_specs(3)
        + _half_specs(1) + [_whole(d_dn), rows_blk],
        out_specs=[pl.BlockSpec((S, D), lambda j: (0, 0)), pl.BlockSpec((2, CONV_TC, D), lambda j: (0, j, 0)), rows_blk,
                   half(3), half(3), half(1), half(1)],
        out_shape=[jax.ShapeDtypeStruct((S, D), F32), jax.ShapeDtypeStruct((2, D_FF, D), MXU_DTYPE),
                   jax.ShapeDtypeStruct((D_FF, D), MXU_DTYPE)]
        + [jax.ShapeDtypeStruct((3, D_FF), F32)] * 2 + [jax.ShapeDtypeStruct((1, D_FF), F32)] * 2,
        scratch_shapes=[pltpu.VMEM((S + CONV_PAD, 2 * CONV_TC), F32), pltpu.VMEM((S, 2 * CONV_TC), MXU_DTYPE),
                        pltpu.VMEM((S, CONV_TC), MXU_DTYPE)],
        compiler_params=_params())(h, up, w_up_t, w_up_t, w_conv, w_conv, b_conv, b_conv, d_dn, w_down)
    return dh, gup.reshape(2 * D_FF, D), gd, jnp.concatenate([dwg, dwv], axis=1), jnp.concatenate([dbg, dbv], axis=1)


def adamw(name, w, parts, m, v, tr=None):
    apart = w.ndim == 3
    R, C = w.shape[0], w.shape[-1]
    tr = tr or R
    assert R % tr == 0
    c1 = 1.0 - ADAM_B1 ** ADAM_STEP
    c2 = 1.0 - ADAM_B2 ** ADAM_STEP
    np_ = len(parts)

    def body(*refs):
        w_ref, m_ref, v_ref = refs[0], refs[1 + np_], refs[2 + np_]
        go_ref, d_ref, mo_ref, vo_ref = refs[3 + np_:]
        terms = []
        for part, ref in zip(parts, refs[1:1 + np_], strict=True):
            terms += [ref[...]] if part.ndim == 2 else [ref[p] for p in range(part.shape[0])]
        g = terms[0].astype(F32)
        for term in terms[1:]:
            g = g + term.astype(F32)
        m2 = ADAM_B1 * m_ref[...] + (1.0 - ADAM_B1) * g
        v2 = ADAM_B2 * v_ref[...] + (1.0 - ADAM_B2) * (g * g)
        go_ref[...] = g
        mo_ref[...] = m2
        vo_ref[...] = v2
        d_ref[...] = -ADAM_LR * ((m2 / c1) / (jnp.sqrt(v2 / c2) + ADAM_EPS) + ADAM_WD * w_ref[...])

    blk = pl.BlockSpec((tr, C), lambda i: (i, 0))
    own = pl.BlockSpec((tr, None, C), lambda i: (i, 0, 0)) if apart else blk
    part_specs = [blk if p.ndim == 2 else pl.BlockSpec((p.shape[0], tr, C), lambda i: (0, i, 0)) for p in parts]
    return pl.pallas_call(
        body, name=name, grid=(R // tr,),
        in_specs=[own] + part_specs + [own, own], out_specs=[own] * 4,
        out_shape=[jax.ShapeDtypeStruct(w.shape, F32)] * 4, compiler_params=_params())(w, *parts, m, v)


def _place():
    return lax.axis_index("x"), lax.axis_index("y"), lax.axis_index("c")


def ada_modulation(name, c, w_ada, after=()):
    n_mod = w_ada.shape[1]

    def exchange(src_ref, dst_ref, send_sems, recv_sems):
        x, y, c_ = _place()
        me = 4 * x + 2 * y + c_
        copies = []
        for k in range(1, N_DEV):
            px, py, pc = x ^ (k >> 2), y ^ ((k >> 1) & 1), c_ ^ (k & 1)
            copies.append(pltpu.make_async_remote_copy(
                src_ref=src_ref, dst_ref=dst_ref.at[me], send_sem=send_sems.at[k - 1], recv_sem=recv_sems.at[k - 1],
                device_id=(px, py, pc), device_id_type=MESH_ID))
        for cp in copies:
            cp.start()
        for cp in copies:
            cp.wait_recv()
        for cp in copies:
            cp.wait_send()
        return me

    def body(c_ref, w_ref, *refs):
        sc_ref, mod_ref, c_all, send_c, recv_c, send_m, recv_m = refs[len(after):]
        me = exchange(c_ref, c_all, send_c, recv_c)
        c_all[me] = c_ref[...]
        sc = _silu(jnp.concatenate([c_all[p] for p in range(N_DEV)], axis=0))
        sc_ref[...] = sc.astype(sc_ref.dtype)
        mod_ref[me] = _dot(sc, w_ref[...])
        exchange(mod_ref.at[me], mod_ref, send_m, recv_m)

    vmem = pl.BlockSpec(memory_space=pltpu.VMEM)
    return pl.pallas_call(
        body, name=name, in_specs=[vmem, vmem] + [pl.BlockSpec(memory_space=pl.ANY)] * len(after), out_specs=[vmem, vmem],
        out_shape=[jax.ShapeDtypeStruct((N_DEV, c.shape[1]), MXU_DTYPE), jax.ShapeDtypeStruct((N_DEV, N_DEV, n_mod), F32)],
        scratch_shapes=[pltpu.VMEM((N_DEV, 1, c.shape[1]), F32)] + [pltpu.SemaphoreType.DMA((N_DEV - 1,))] * 4,
        compiler_params=pltpu.CompilerParams(has_side_effects=True, vmem_limit_bytes=VMEM_LIMIT))(c, w_ada, *after)


HBM_SPEC = pl.BlockSpec(memory_space=pltpu.HBM)
SEM_SPEC = pl.BlockSpec(memory_space=pltpu.SEMAPHORE)
DATAFLOW = pltpu.SideEffectType.DATAFLOW_SIDE_EFFECTING


def _exchange_copies(srcs, lands, send_sems, recv_sems, gather, first=0):
    x, y, c = _place()
    me = 4 * x + 2 * y + c
    out = []
    for t, (src, land) in enumerate(zip(srcs, lands, strict=True)):
        for k in range(1, N_DEV):
            px, py, pc = x ^ (k >> 2), y ^ ((k >> 1) & 1), c ^ (k & 1)
            sem = 7 * (first + t) + k - 1
            out.append((k, pltpu.make_async_remote_copy(
                src_ref=src if gather else src.at[4 * px + 2 * py + pc],
                dst_ref=land.at[me] if gather else land.at[k - 1],
                send_sem=send_sems.at[sem], recv_sem=recv_sems.at[sem],
                device_id=(px, py, pc), device_id_type=MESH_ID)))
    return out


def _own_copies(srcs, lands, send_sems, gather, first=0):
    x, y, c = _place()
    me = 4 * x + 2 * y + c
    total = send_sems.shape[0] // N_DEV
    return [pltpu.make_async_copy(src if gather else src.at[me], land.at[me] if gather else land.at[N_DEV - 1],
                                  send_sems.at[7 * total + first + t])
            for t, (src, land) in enumerate(zip(srcs, lands, strict=True))]


TREE_DIRECT = (1, 2, 4, 6)
TREE_FORWARDED = (3, 5, 7)


def exchange_start(name, arrs, gather, after=None, tree=False):
    n = len(arrs)
    lands = [lax.empty((N_DEV,) + (a.shape if gather else a.shape[1:]), a.dtype) for a in arrs]
    deps = [] if after is None else [after]

    def body(*refs):
        srcs, land_refs = refs[:n], refs[n:2 * n]
        send_sems, recv_sems = refs[2 * n + len(deps)], refs[2 * n + len(deps) + 1]
        token = refs[-1]
        for k, cp in _exchange_copies(srcs, land_refs, send_sems, recv_sems, gather):
            if not tree or k in TREE_DIRECT:
                cp.start()
        for cp in _own_copies(srcs, land_refs, send_sems, gather):
            cp.start()
        token[...] = jnp.zeros_like(token)

    hbm = lambda a: pltpu.HBM(a.shape, a.dtype)
    res = pl.pallas_call(
        body, name=name,
        out_shape=(pltpu.SemaphoreType.DMA((N_DEV * n,)), pltpu.SemaphoreType.DMA((7 * n,)), *[hbm(a) for a in arrs],
                   *[hbm(l) for l in lands], jax.ShapeDtypeStruct((8, 128), F32)),
        in_specs=[HBM_SPEC] * (2 * n) + [pl.BlockSpec(memory_space=pl.ANY)] * len(deps),
        out_specs=(SEM_SPEC, SEM_SPEC, *[HBM_SPEC] * (2 * n), pl.BlockSpec(memory_space=pltpu.VMEM)),
        input_output_aliases={i: 2 + i for i in range(2 * n)},
        compiler_params=pltpu.CompilerParams(has_side_effects=DATAFLOW),
    )(*[pltpu.with_memory_space_constraint(a, pltpu.HBM) for a in arrs + lands], *deps)
    return res[0], res[1], list(res[2:2 + n]), list(res[2 + n:2 + 2 * n]), res[-1]


def exchange_forward(name, started, after, first=0, count=None):
    send_sems, recv_sems, srcs, lands, _ = started
    count = len(srcs) - first if count is None else count
    mine = lands[first:first + count]
    n = len(mine)

    def copies(land_refs, send_ref, recv_ref):
        x, y, c = _place()
        out = []
        for t, land in enumerate(land_refs):
            for k in (2, 4, 6):
                slot = land.at[4 * (x ^ (k >> 2)) + 2 * (y ^ ((k >> 1) & 1)) + c]
                came, goes = 7 * (first + t) + k - 1, 7 * (first + t) + (k ^ 1) - 1
                out.append((
                    pltpu.make_async_remote_copy(src_ref=slot, dst_ref=slot, send_sem=send_ref.at[came], recv_sem=recv_ref.at[came],
                                                 device_id=(x, y, c), device_id_type=MESH_ID),
                    pltpu.make_async_remote_copy(src_ref=slot, dst_ref=slot, send_sem=send_ref.at[goes], recv_sem=recv_ref.at[goes],
                                                 device_id=(x, y, 1 - c), device_id_type=MESH_ID)))
        return out

    after = list(after) if isinstance(after, (list, tuple)) else [after]

    def arrived(*refs):
        for came, _ in copies(refs[:n], refs[n], refs[n + 1]):
            came.wait_recv()

    def pass_on(*refs):
        for _, goes in copies(refs[:n], refs[n], refs[n + 1]):
            goes.start()
        refs[-1][...] = jnp.zeros_like(refs[-1])

    hbm = lambda a: pltpu.HBM(a.shape, a.dtype)
    here = pl.pallas_call(
        arrived, name=name + "_arrived", out_shape=tuple(hbm(a) for a in mine),
        in_specs=[HBM_SPEC] * n + [SEM_SPEC, SEM_SPEC] + [pl.BlockSpec(memory_space=pl.ANY)] * len(after),
        out_specs=tuple([HBM_SPEC] * n), input_output_aliases={i: i for i in range(n)},
        compiler_params=pltpu.CompilerParams(has_side_effects=DATAFLOW),
    )(*mine, send_sems, recv_sems, *after)
    res = pl.pallas_call(
        pass_on, name=name, out_shape=(*[hbm(a) for a in mine], jax.ShapeDtypeStruct((8, 128), F32)),
        in_specs=[HBM_SPEC] * n + [SEM_SPEC, SEM_SPEC],
        out_specs=(*[HBM_SPEC] * n, pl.BlockSpec(memory_space=pltpu.VMEM)), input_output_aliases={i: i for i in range(n)},
        compiler_params=pltpu.CompilerParams(has_side_effects=DATAFLOW),
    )(*here, send_sems, recv_sems)
    lands = lands[:first] + list(res[:n]) + lands[first + count:]
    return (send_sems, recv_sems, srcs, lands, res[-1])


def exchange_wait(name, started, gather, after, first=0, count=None, tree=False):
    send_sems, recv_sems, srcs, lands, _ = started
    count = len(srcs) - first if count is None else count
    srcs, lands = srcs[first:first + count], lands[first:first + count]
    n = len(srcs)

    def body(*refs):
        src_refs, land_refs = refs[:n], refs[n:2 * n]
        copies = _exchange_copies(src_refs, land_refs, refs[2 * n], refs[2 * n + 1], gather, first)
        for _, cp in copies:
            cp.wait_send()
        for k, cp in copies:
            if not tree or k in (1,) + TREE_FORWARDED:
                cp.wait_recv()
        for cp in _own_copies(src_refs, land_refs, refs[2 * n], gather, first):
            cp.wait()

    hbm = lambda a: pltpu.HBM(a.shape, a.dtype)
    res = pl.pallas_call(
        body, name=name, out_shape=tuple(hbm(a) for a in srcs + lands),
        in_specs=[HBM_SPEC] * (2 * n) + [SEM_SPEC, SEM_SPEC, pl.BlockSpec(memory_space=pl.ANY)],
        out_specs=tuple([HBM_SPEC] * (2 * n)), input_output_aliases={i: i for i in range(2 * n)},
        compiler_params=pltpu.CompilerParams(has_side_effects=DATAFLOW),
    )(*srcs, *lands, send_sems, recv_sems, after)
    return list(res[:n]), list(res[n:])


def _gather_cols(stack):
    p, k, n = stack.shape
    return stack.transpose(1, 0, 2).reshape(k, p * n)


def _scatter_cols(full):
    k, n = full.shape
    return full.reshape(k, N_DEV, n // N_DEV).transpose(1, 0, 2)


def _gather_rows(stack):
    p, r, n = stack.shape
    return stack.reshape(p * r, n)


def _scatter_rows(full):
    r, n = full.shape
    return full.reshape(N_DEV, r // N_DEV, n)


_IN_NAT = Q_LORA + KV_LORA
TRANSPOSED = ("w_in", "w_q_b", "w_up")
ROWS_APART = ("w_in", "w_conv")


def to_kernel_layout(name, w):
    if name == "w_in":
        z = lambda n: jnp.zeros((n, w.shape[1]), w.dtype)
        return jnp.concatenate([w[:_IN_NAT], z(KPE_LO), w[_IN_NAT:_IN_NAT + ROPE], z(LANES - KPE_LO - ROPE), w[_IN_NAT + ROPE:]], axis=0)
    if name == "w_q_b":
        return jnp.pad(w.reshape(HEADS, NOPE + ROPE, -1), ((0, 0), (0, LANES - NOPE - ROPE), (0, 0))).reshape(HEADS * LANES, -1)
    if name == "w_o":
        mla = jnp.pad(w[:HEADS * NOPE].reshape(HEADS, NOPE, -1), ((0, 0), (LANES - NOPE, 0), (0, 0))).reshape(HEADS * LANES, -1)
        return jnp.concatenate([mla, w[HEADS * NOPE:]], axis=0)
    return w


def from_kernel_layout(name, g):
    if name == "w_in":
        return jnp.concatenate([g[:_IN_NAT], g[P_KPE + KPE_LO:P_KPE + KPE_LO + ROPE], g[P_QD:]], axis=0)
    if name == "w_q_b":
        return g.reshape(HEADS, LANES, -1)[:, :NOPE + ROPE, :].reshape(HEADS * (NOPE + ROPE), -1)
    if name == "w_o":
        mla = g[:HEADS * LANES].reshape(HEADS, LANES, -1)[:, LANES - NOPE:, :].reshape(HEADS * NOPE, -1)
        return jnp.concatenate([mla, g[HEADS * LANES:]], axis=0)
    return g


SMALL_COLS = 1024
SMALL_ROWS = 24
SMALL_AT = {"loss": (0, 0, 1), "b_ada": (1, 0, 6 * D_MODEL), "g_mix_norm": (7, 0, D_MODEL), "g_q_lat": (8, 0, Q_LORA),
            "g_kv_lat": (9, 0, KV_LORA), "g_mla_q_nope": (10, 0, NOPE), "g_mla_q_pe": (10, 128, ROPE),
            "g_mla_k_nope": (10, 256, NOPE), "g_mla_k_pe": (10, 384, ROPE), "g_dil_q": (10, 512, DIL_DIM),
            "g_dil_k": (10, 640, DIL_DIM), "g_ffn_norm": (11, 0, D_MODEL), "b_conv": (12, 0, 2 * D_FF)}
SMALL_PARAMS = tuple(n for n in SMALL_AT if n != "loss")


def _pack_small(values):
    by_row = {}
    for name, (row, off, n) in SMALL_AT.items():
        by_row.setdefault(row, []).append((off, values[name].reshape(-1).astype(F32)))
    out = []
    for row in sorted(by_row):
        pieces, at = [], 0
        for off, v in sorted(by_row[row], key=lambda t: t[0]):
            pieces += [jnp.zeros((off - at,), F32), v]
            at = off + v.shape[0]
        flat = jnp.concatenate(pieces)
        nrows = -(-flat.shape[0] // SMALL_COLS)
        out.append(jnp.pad(flat, (0, nrows * SMALL_COLS - flat.shape[0])).reshape(nrows, SMALL_COLS))
    packed = jnp.concatenate(out, axis=0)
    return jnp.pad(packed, ((0, SMALL_ROWS - packed.shape[0]), (0, 0)))


def _adam(w, g, m, v):
    c1 = 1.0 - ADAM_B1 ** ADAM_STEP
    c2 = 1.0 - ADAM_B2 ** ADAM_STEP
    m2 = ADAM_B1 * m + (1.0 - ADAM_B1) * g
    v2 = ADAM_B2 * v + (1.0 - ADAM_B2) * (g * g)
    return -ADAM_LR * ((m2 / c1) / (jnp.sqrt(v2 / c2) + ADAM_EPS) + ADAM_WD * w), m2, v2


def adamw_small(name, stack, params):
    flat = [a for n in SMALL_PARAMS for a in params[n]]

    def body(stack_ref, *refs):
        ins, outs = refs[:len(flat)], refs[len(flat):]
        g_all = stack_ref[0]
        for p in range(1, N_DEV):
            g_all = g_all + stack_ref[p]
        outs[0][...] = g_all[0:1, 0:1]
        for i, pname in enumerate(SMALL_PARAMS):
            row, off, n = SMALL_AT[pname]
            w_ref, m_ref, v_ref = ins[3 * i:3 * i + 3]
            go_ref, d_ref, mo_ref, vo_ref = outs[1 + 4 * i:5 + 4 * i]
            for c0 in range(0, n, SMALL_COLS):
                cn = min(SMALL_COLS, n - c0)
                r = row + c0 // SMALL_COLS
                g = g_all[r:r + 1, off:off + cn]
                cols = (slice(None), slice(c0, c0 + cn))
                d, m2, v2 = _adam(w_ref[cols], g, m_ref[cols], v_ref[cols])
                go_ref[cols], d_ref[cols], mo_ref[cols], vo_ref[cols] = g, d, m2, v2

    whole = lambda a: pl.BlockSpec(a.shape, lambda: (0,) * a.ndim)
    out_shape = [jax.ShapeDtypeStruct((1, 1), F32)] + [jax.ShapeDtypeStruct(a.shape, F32) for n in SMALL_PARAMS for a in params[n][:1] * 4]
    res = pl.pallas_call(body, name=name, in_specs=[whole(stack)] + [whole(a) for a in flat],
                         out_specs=[pl.BlockSpec(s.shape, lambda s=s: (0,) * len(s.shape)) for s in out_shape],
                         out_shape=out_shape, compiler_params=_params())(stack, *flat)
    return res[0], {n: res[1 + 4 * i:5 + 4 * i] for i, n in enumerate(SMALL_PARAMS)}


def _local_step(x, pos, mod, target, w, fetch, emit, halfway=lambda after: None):
    S = SEQ
    sh1, sc1, g1, sh2, sc2, g2 = [mod[:, i * D_MODEL:(i + 1) * D_MODEL] for i in range(6)]
    zeros = lambda n: jnp.zeros((1, n), F32)
    g_q = jnp.concatenate([w["g_mla_q_nope"], w["g_mla_q_pe"], zeros(LANES - NOPE - ROPE)], axis=1)
    g_k = jnp.concatenate([w["g_mla_k_nope"], zeros(LANES - NOPE)], axis=1)
    g_kpe = jnp.concatenate([zeros(KPE_LO), w["g_mla_k_pe"], zeros(LANES - KPE_LO - ROPE)], axis=1)
    g_dq = jnp.concatenate([w["g_dil_q"]] * 2, axis=1)
    g_dk = jnp.concatenate([w["g_dil_k"]] * 2, axis=1)
    b_conv = w["b_conv"]

    def inv_freq(d):
        return jnp.power(ROPE_THETA, -2.0 * jnp.arange(d // 2, dtype=F32) / d)

    n_m, n_d = ROPE // 2, DIL_DIM // 2
    freqs = jnp.concatenate([inv_freq(ROPE), inv_freq(DIL_DIM), jnp.zeros((LANES - n_m - n_d,), F32)]).reshape(1, LANES)

    def tables_fn(rows, params):
        (p,), (f,) = rows, params
        c, s = jnp.cos(p * f), jnp.sin(p * f)
        one, zero = jnp.ones_like(c), jnp.zeros_like(c)
        mla = lambda t, fill: jnp.concatenate([fill[:, :KPE_LO], t[:, :n_m], t[:, :n_m], fill[:, :LANES - KPE_LO - ROPE]], axis=1)
        dil = lambda t: jnp.concatenate([t[:, n_m:n_m + n_d]] * 4, axis=1)
        return [mla(c, one), mla(s, zero), dil(c), dil(s)], []

    cos_m, sin_m, cos_d, sin_d = rowwise("rope_tables", tables_fn, [pos], [freqs], [(LANES, F32)] * 4)
    tables = [cos_m, sin_m, cos_d, sin_d]
    H_M, H_D = ROPE // 2, DIL_DIM // 2

    def ln1_fn(rows, params):
        (xv,), (g, sc, sh) = rows, params
        y, _, _ = _rms(xv, g)
        return [y * (1.0 + sc) + sh], []

    (h,) = rowwise("ln1_fwd", ln1_fn, [x], [w["g_mix_norm"], sc1, sh1], [(D_MODEL, MXU_DTYPE)], dep=sin_d)
    w_in = fetch("w_in", h)

    def proj_fn(rows, params):
        (hv, cm, sm, cd, sd), (w_t, gq, gkv, gkp, gdq, gdk) = rows, params
        pv = _dot(hv, w_t, "nt")
        kper = _rope(_grms(pv[:, P_KPE:P_QD], gkp, KPE_GROUPS)[0], cm, sm, H_M)
        qd = [_rope(_grms(c, gdq, DIL_GROUPS)[0], cd, sd, H_D) for c in _chunks(pv[:, P_QD:P_KD])]
        kd = [_rope(_grms(c, gdk, DIL_GROUPS)[0], cd, sd, H_D) for c in _chunks(pv[:, P_KD:P_VD])]
        return [pv, _rms(pv[:, P_QLAT:P_KVLAT], gq)[0], _rms(pv[:, P_KVLAT:P_KPE], gkv)[0], kper,
                jnp.concatenate(qd, axis=1), jnp.concatenate(kd, axis=1)], []

    post_params = [w["g_q_lat"], w["g_kv_lat"], g_kpe, g_dq, g_dk]
    proj, qln, kvn, kper, qd_r, kd_r = rowwise(
        "proj_fwd", proj_fn, [h] + tables, [w_in] + post_params,
        [(P_END, F32), (Q_LORA, MXU_DTYPE), (KV_LORA, MXU_DTYPE), (LANES, MXU_DTYPE)] + [(DIL_WIDTH, F32)] * 2, tm=256)
    w_q_b, w_kv_b = fetch("w_q_b", qln), fetch("w_kv_b", kvn)

    def mla_proj_fn(rows, params):
        (qlv, kvlv, kp, cm, sm), (wq_t, wkv, gq, gk) = rows, params
        qv, kvv = _dot(qlv, wq_t, "nt"), _dot(kvlv, wkv)
        value_lanes = _lane(kp.shape) >= NOPE
        qs, ks, vs = [], [], []
        for qc, kc in zip(_chunks(qv), _chunks(kvv), strict=True):
            qs.append(_rope(_grms(qc, gq, Q_GROUPS)[0], cm, sm, H_M))
            ks.append(_grms(kc, gk, K_GROUPS)[0] + kp)
            vs.append(jnp.where(value_lanes, kc, 0.0))
        return [qv, kvv] + [jnp.concatenate(t, axis=1) for t in (qs, ks, vs)], []

    q, kv, q_mla, k_mla, v_mla = rowwise(
        "mla_proj", mla_proj_fn, [qln, kvn, kper, cos_m, sin_m], [w_q_b, w_kv_b, g_q, g_k],
        [(HEADS * LANES, F32)] * 2 + [(HEADS * LANES, MXU_DTYPE)] * 3, tm=256)
    mla_scale = (NOPE + ROPE) ** -0.5
    o_cat, lse_mla = mla_fwd("mla_fwd", q_mla, k_mla, v_mla, mla_scale)
    passed = halfway(lse_mla)

    band = [band_fwd(f"band{dil}_fwd", qd_r, kd_r, proj, dil, dep=passed) for dil in DILATIONS]
    o_cat, lse_mix = combine_fwd("dil_combine", [b[0] for b in band], [b[1] for b in band], o_cat)
    w_o = fetch("w_o", o_cat)

    def mid_fn(rows, params):
        (ov, xv), (w_out, gate1, g, sc, sh) = rows, params
        mx = _dot(ov, w_out)
        x1 = xv + gate1 * mx
        y, _, _ = _rms(x1, g)
        return [mx, x1, y * (1.0 + sc) + sh], []

    mix, x1, h2 = rowwise("mix_fwd", mid_fn, [o_cat, x], [w_o, g1, w["g_ffn_norm"], sc2, sh2],
                          [(D_MODEL, F32), (D_MODEL, F32), (D_MODEL, MXU_DTYPE)], tm=256)
    w_up, w_conv, w_down = fetch("w_up", h2), fetch("w_conv", h2), fetch("w_down", h2)
    dn, up = ffn_fwd("ffn_fwd", h2, w_up, w_conv, b_conv, w_down)

    def final_fn(rows, params):
        (x1v, dnv, tgt), (gate2,) = rows, params
        r = x1v + gate2 * dnv - tgt
        dy = r * (1.0 / D_MODEL)
        loss = jnp.sum(_colsum(r * r), axis=-1, keepdims=True) * (0.5 / D_MODEL)
        return [dy, gate2 * dy], [loss, _colsum(dy * dnv)]

    dy, d_dn, loss, dg2 = rowwise("loss_head", final_fn, [x1, dn, target], [g2], [(D_MODEL, F32), (D_MODEL, MXU_DTYPE)],
                                  [1, D_MODEL])
    dh2, g_up, g_down, g_w_conv, g_b_conv = ffn_bwd("ffn_bwd", h2, up, w_up, w_conv, b_conv, d_dn, w_down)
    emit("w_down", g_down)
    emit("w_conv", g_w_conv)
    sent = emit("w_up", g_up)

    def mid_bwd_fn(rows, params):
        (dh2v, dyv, x1v, mx), (gate1, g, sc) = rows, params
        yn, n, rstd = _rms(x1v, g)
        dx_n, dg = _rms_bwd(dh2v * (1.0 + sc), n, rstd, g)
        dx1 = dyv + dx_n
        return [dx1, gate1 * dx1], [dg, _colsum(dh2v * yn), _colsum(dh2v), _colsum(dx1 * mx)]

    dx1, dmix, dg_ffn, dsc2, dsh2, dg1 = rowwise(
        "mid_bwd", mid_bwd_fn, [dh2, dy, x1, mix], [g1, w["g_ffn_norm"], sc2], [(D_MODEL, F32), (D_MODEL, MXU_DTYPE)],
        [D_MODEL] * 4, dep=sent)

    sent = emit("w_o", matmul("mix_wgrad", o_cat, dmix, "tn", tm=512, out_dtype=MXU_DTYPE))
    do_cat = matmul("mix_dgrad", dmix, w_o, "nt", tm=512, dep=sent)
    dband = None
    for dil, b in zip(DILATIONS, band):
        dband = band_bwd(f"band{dil}_bwd", qd_r, kd_r, proj, b[1], lse_mix, o_cat, do_cat, dil, before=dband)
    dq_mla, dkv_mla, dkper = mla_bwd("mla_bwd", q_mla, k_mla, v_mla, o_cat, do_cat, lse_mla, mla_scale)

    def mla_prep_bwd_fn(rows, params):
        (dqv, dkvv, qv, kvv, cm, sm), (gq, gk) = rows, params
        nope_lanes = _lane(cm.shape) < NOPE
        dqs, dkvs, dgq, dgk = [], [], 0.0, 0.0
        for dqc, dkc, qc, kc in zip(_chunks(dqv), _chunks(dkvv), _chunks(qv), _chunks(kvv), strict=True):
            _, n, rstd = _grms(qc, gq, Q_GROUPS)
            dx, dg = _grms_bwd(_rope_bwd(dqc, cm, sm, H_M), n, rstd, gq, Q_GROUPS)
            dqs.append(dx)
            dgq = dgq + dg
            _, n, rstd = _grms(kc, gk, K_GROUPS)
            dx, dg = _grms_bwd(dkc, n, rstd, gk, K_GROUPS)
            dkvs.append(jnp.where(nope_lanes, dx, dkc))
            dgk = dgk + dg
        return [jnp.concatenate(dqs, axis=1), jnp.concatenate(dkvs, axis=1)], [dgq, dgk]

    dq, dkv, dg_q, dg_k = rowwise("mla_prep_bwd", mla_prep_bwd_fn, [dq_mla, dkv_mla, q, kv, cos_m, sin_m], [g_q, g_k],
                                  [(HEADS * LANES, MXU_DTYPE)] * 2, [LANES, LANES], tm=256)
    emit("w_q_b", matmul("q_wgrad", dq, qln, "tn", out_dtype=MXU_DTYPE))
    emit("w_kv_b", matmul("kv_wgrad", kvn, dkv, "tn", out_dtype=MXU_DTYPE))

    def pre_bwd_fn(rows, params):
        dqv, dkvv, dkp, dqd_, dkd_, dvd_, pv, cm, sm, cd, sd = rows
        wq_t, wkv, gq, gkv, gkp, gdq, gdk = params
        dql, dkvl = _dot(dqv, wq_t), _dot(dkvv, wkv, "nt")
        r_q = _norm_bwd(dql, pv[:, P_QLAT:P_KVLAT], gq)
        r_kv = _norm_bwd(dkvl, pv[:, P_KVLAT:P_KPE], gkv)
        _, n, rstd = _grms(pv[:, P_KPE:P_QD], gkp, KPE_GROUPS)
        r_kp = _grms_bwd(_rope_bwd(dkp, cm, sm, H_M), n, rstd, gkp, KPE_GROUPS)
        outs, dgs = [r_q[0], r_kv[0], r_kp[0]], []
        for dval, lo, g in ((dqd_, P_QD, gdq), (dkd_, P_KD, gdk)):
            dg_sum = 0.0
            for dc, xc in zip(_chunks(dval), _chunks(pv[:, lo:lo + DIL_WIDTH]), strict=True):
                _, n, rstd = _grms(xc, g, DIL_GROUPS)
                dx, dg = _grms_bwd(_rope_bwd(dc, cd, sd, H_D), n, rstd, g, DIL_GROUPS)
                outs.append(dx)
                dg_sum = dg_sum + dg
            dgs.append(dg_sum)
        return [jnp.concatenate(outs + [dvd_], axis=1)], [r_q[1], r_kv[1], r_kp[1]] + dgs

    dproj, dg_q_lat, dg_kv_lat, dg_kpe, dg_dq, dg_dk = rowwise(
        "proj_pre_bwd", pre_bwd_fn,
        [dq, dkv, dkper] + list(dband) + [proj] + tables, [w_q_b, w_kv_b] + post_params,
        [(P_END, MXU_DTYPE)], [Q_LORA, KV_LORA, LANES, LANES, LANES], tm=256)
    sent = emit("w_in", matmul("proj_wgrad", dproj, h, "tn", tn=512, tk=1024, out_dtype=MXU_DTYPE))

    def ln1_bwd_fn(rows, params):
        (dpv, dres, xv), (w_t, g, sc) = rows, params
        dhv = _dot(dpv, w_t)
        yn, n, rstd = _rms(xv, g)
        dx_n, dg = _rms_bwd(dhv * (1.0 + sc), n, rstd, g)
        return [dres + dx_n], [dg, _colsum(dhv * yn), _colsum(dhv)]

    grad_x, dg_mix, dsc1, dsh1 = rowwise("proj_dgrad", ln1_bwd_fn, [dproj, dx1, x], [w_in, w["g_mix_norm"], sc1],
                                         [(D_MODEL, F32)], [D_MODEL] * 3, tm=256, dep=sent)
    dmod = jnp.concatenate([dsh1, dsc1, dg1, dsh2, dsc2, dg2], axis=-1)
    small = {"loss": loss, "b_ada": dmod, "g_mix_norm": dg_mix, "g_q_lat": dg_q_lat, "g_kv_lat": dg_kv_lat,
             "g_mla_q_nope": dg_q[:, :NOPE], "g_mla_q_pe": dg_q[:, NOPE:NOPE + ROPE], "g_mla_k_nope": dg_k[:, :NOPE],
             "g_mla_k_pe": dg_kpe[:, KPE_LO:KPE_LO + ROPE], "g_dil_q": dg_dq[:, :DIL_DIM] + dg_dq[:, DIL_DIM:],
             "g_dil_k": dg_dk[:, :DIL_DIM] + dg_dk[:, DIL_DIM:], "g_ffn_norm": dg_ffn,
             "b_conv": g_b_conv}
    return grad_x, small


COL_SHARDED = ("w_kv_b", "w_conv")
ROW_SHARDED = ("w_o", "w_down") + TRANSPOSED
ADAM_TILE = {"w_ada": 256, "w_up": 176, "w_down": 176}
GATHER_GROUPS = (("w_in",), ("w_q_b", "w_kv_b"), ("w_o",), ("w_up", "w_conv", "w_down"))
START_STAGES = ((0, 1), (2, 3))
FORWARD_STAGES = ((0, 1), (2,), (3,))
FORWARD_WITH = {"w_o": 2}
SCATTER_GROUPS = (("w_down", "w_conv", "w_up"), ("w_o",), ("w_q_b", "w_kv_b", "w_in"))
OUT_WEIGHTS = ("w_ada", "b_ada", "g_mix_norm", "w_in", "g_q_lat", "w_q_b", "g_kv_lat", "w_kv_b", "g_mla_q_nope", "g_mla_q_pe",
               "g_mla_k_nope", "g_mla_k_pe", "g_dil_q", "g_dil_k", "w_o", "g_ffn_norm", "w_up", "w_conv", "b_conv", "w_down")


def kernel(x, c, positions, w_ada, b_ada, g_mix_norm, w_in, g_q_lat, w_q_b, g_kv_lat, w_kv_b, g_mla_q_nope, g_mla_q_pe, g_mla_k_nope, g_mla_k_pe, g_dil_q, g_dil_k, w_o, g_ffn_norm, w_up, w_conv, b_conv, w_down, loss_target, m_w_ada, m_b_ada, m_g_mix_norm, m_w_in, m_g_q_lat, m_w_q_b, m_g_kv_lat, m_w_kv_b, m_g_mla_q_nope, m_g_mla_q_pe, m_g_mla_k_nope, m_g_mla_k_pe, m_g_dil_q, m_g_dil_k, m_w_o, m_g_ffn_norm, m_w_up, m_w_conv, m_b_conv, m_w_down, v_w_ada, v_b_ada, v_g_mix_norm, v_w_in, v_g_q_lat, v_w_q_b, v_g_kv_lat, v_w_kv_b, v_g_mla_q_nope, v_g_mla_q_pe, v_g_mla_k_nope, v_g_mla_k_pe, v_g_dil_q, v_g_dil_k, v_w_o, v_g_ffn_norm, v_w_up, v_w_conv, v_b_conv, v_w_down):
    args = dict(locals())
    xi, yi, ci = _place()
    me = 4 * xi + 2 * yi + ci
    def local(prefix, n):
        a = args[prefix + n]
        if n in ROWS_APART:
            return jnp.transpose(a, (2, 0, 1) if n in TRANSPOSED else (1, 0, 2))
        return a[0].T if n in TRANSPOSED else a[0]

    def as_output(n, r):
        if n in ROWS_APART:
            return jnp.transpose(r, (1, 2, 0) if n in TRANSPOSED else (1, 0, 2))
        return (r.T if n in TRANSPOSED else r)[None]

    shard = {n: local("", n) for n in COL_SHARDED + ROW_SHARDED + ("w_ada",)}
    flat = lambda n, a: a.reshape(a.shape[0], a.shape[-1]) if n in ROWS_APART else a
    small_w = {n: args[n] for n in SMALL_PARAMS}

    payload = {n: flat(n, shard[n]) if n == "w_conv" else flat(n, shard[n]).astype(MXU_DTYPE) for n in COL_SHARDED + ROW_SHARDED}
    start_order = [[n for i in groups for n in GATHER_GROUPS[i]] for groups in START_STAGES]

    sc_all, mod_all = ada_modulation("ada_mod", c, shard["w_ada"], after=[payload[n] for n in start_order[0]])

    exchange_of = lambda i: [e for e, groups in enumerate(START_STAGES) if i in groups][0]
    start_stage = lambda e, after: exchange_start(f"gather_start{e}", [payload[n] for n in start_order[e]], gather=True,
                                                  after=after, tree=True)
    gathered = {0: start_stage(0, mod_all)}
    after_start = gathered[0][-1]
    full, forwarded = {}, set()

    def forward(stage, after):
        e = exchange_of(FORWARD_STAGES[stage][0])
        if stage not in forwarded:
            forwarded.add(stage)
            first = start_order[e].index(GATHER_GROUPS[FORWARD_STAGES[stage][0]][0])
            count = sum(len(GATHER_GROUPS[i]) for i in FORWARD_STAGES[stage])
            starts_next = e + 1 < len(START_STAGES) and e + 1 not in gathered
            ready = [payload[n] for n in start_order[e + 1]] if starts_next else []
            gathered[e] = exchange_forward(f"gather_forward{stage}", gathered[e], [after] + ready, first, count)
            if starts_next:
                gathered[e + 1] = start_stage(e + 1, gathered[e][-1])
        return gathered[e][-1]

    def fetch(name, after):
        if name not in full:
            (i, grp), = [(i, grp) for i, grp in enumerate(GATHER_GROUPS) if name in grp]
            (stage,) = [s for s, groups in enumerate(FORWARD_STAGES) if i in groups]
            forward(stage, after)
            if name in FORWARD_WITH:
                forward(FORWARD_WITH[name], after)
            e = exchange_of(i)
            behind = gathered[e + 1][-1] if e + 1 in gathered else after
            srcs, lands = exchange_wait(f"gather{i}_wait", gathered[e], True, behind, start_order[e].index(grp[0]), len(grp), tree=True)
            for n, stack in zip(grp, lands, strict=True):
                full[n] = to_kernel_layout(n, _gather_cols(stack) if n in COL_SHARDED else _gather_rows(stack))
        return full[name]

    mod_row = lax.dynamic_index_in_dim(mod_all, me, axis=1, keepdims=False).reshape(1, 6 * D_MODEL)
    (mod,) = rowwise("ada_bias", lambda rows, params: ([rows[0] + rows[1]], []), [mod_row, b_ada], [], [(6 * D_MODEL, F32)],
                     dep=after_start)

    pending, scatters = {}, {}

    def emit(name, grad):
        grad = from_kernel_layout(name, grad)
        pending[name] = _scatter_cols(grad) if name in COL_SHARDED else _scatter_rows(grad)
        for i, grp in enumerate(SCATTER_GROUPS):
            if name == grp[-1]:
                scatters[i] = exchange_start(f"scatter{i}_start", [pending[n] for n in grp], gather=False)
                return scatters[i][-1]
        return None

    pos = positions.reshape(SEQ, 1).astype(F32)
    grad_x, small = _local_step(x[0], pos, mod, loss_target[0], small_w, fetch, emit, halfway=lambda after: forward(1, after))

    small_sent = exchange_start("small_start", [_pack_small(small)], gather=True, after=grad_x)

    res, done = {}, small_sent[-1]
    for i, grp in enumerate(SCATTER_GROUPS):
        _, lands = exchange_wait(f"scatter{i}_wait", scatters[i], False, done)
        for n, land in zip(grp, lands, strict=True):
            res[n] = adamw(f"adamw_{n}", shard[n], [land], local("m_", n), local("v_", n), ADAM_TILE.get(n))
            done = res[n][0]
            res[n] = [as_output(n, r) for r in res[n]]
    _, (small_all,) = exchange_wait("small_wait", small_sent, True, done)
    loss, small_res = adamw_small("adamw_small", small_all, {n: (args[n], args["m_" + n], args["v_" + n]) for n in SMALL_PARAMS})
    row, _, n_mod = SMALL_AT["b_ada"]
    dmod_all = small_all[:, row:row + n_mod // SMALL_COLS, :].reshape(N_DEV, n_mod)
    dmod_mine = lax.dynamic_slice_in_dim(dmod_all, me * (6 * D_MODEL // N_DEV), 6 * D_MODEL // N_DEV, axis=1)
    g_w_ada = matmul("ada_wgrad", sc_all, dmod_mine, "tn")
    res["w_ada"] = [r[None] for r in adamw("adamw_w_ada", shard["w_ada"], [g_w_ada], m_w_ada[0], v_w_ada[0], ADAM_TILE["w_ada"])]

    def leaf(kind, n):
        return res[n][kind] if n in res else small_res[n][kind]

    return (loss.reshape(()), grad_x[None], *[leaf(k, n) for k in range(4) for n in OUT_WEIGHTS])
```

```python
import jax
import jax.numpy as jnp
from jax import lax
from jax.experimental import pallas as pl
from jax.experimental.pallas import tpu as pltpu

F32 = jnp.float32
MXU_DTYPE = jnp.bfloat16

N_DEV = 8
D_MODEL = 1024
SEQ = 2048
HEADS = 8
NOPE = 64
ROPE = 32
Q_LORA = 512
KV_LORA = 256
DIL_DIM = 64
DIL_WIDTH = HEADS * DIL_DIM
DILATIONS = (1, 4, 16)
SPAN = 128
D_FF = 2816
LANES = 128
SUBLANES = 8
ROPE_THETA = 10000.0
EPS = 1e-6
NEG_INF = -1e30
ADAM_LR, ADAM_B1, ADAM_B2, ADAM_EPS, ADAM_WD, ADAM_STEP = 0.001, 0.9, 0.999, 1e-08, 0.01, 10
VMEM_LIMIT = 56 * 1024 * 1024
MESH_ID = pl.DeviceIdType.MESH

P_QLAT, P_KVLAT, P_KPE, P_QD, P_KD, P_VD, P_END = 0, 512, 768, 896, 1408, 1920, 2432
KPE_LO = 64
MIX_IN = HEADS * LANES + DIL_WIDTH


def _params(**kw):
    return pltpu.CompilerParams(vmem_limit_bytes=VMEM_LIMIT, **kw)


def rowwise(name, fn, rows, params, out_rows, out_accs=(), tm=512, dep=None):
    deps = [] if dep is None else [dep]
    rows = [r if isinstance(r, tuple) else (r, r.shape[1], 0) for r in rows]
    R = rows[0][0].shape[0]
    tm = min(tm, R)
    steps = R // tm
    assert steps * tm == R
    in_specs = []
    for a, width, cb in rows:
        ri = a.shape[0]
        per = ri // tm
        assert per * tm == ri
        if ri == R:
            in_specs.append(pl.BlockSpec((tm, width), lambda i, cb=cb: (i, cb)))
        else:
            in_specs.append(pl.BlockSpec((tm, width), lambda i, per=per, cb=cb: (i % per, cb)))
    for p in params:
        in_specs.append(pl.BlockSpec(p.shape, lambda i: (0,) * p.ndim))
    in_specs += [pl.BlockSpec(memory_space=pl.ANY)] * len(deps)
    out_shape = [jax.ShapeDtypeStruct((R, d), dt) for d, dt in out_rows]
    out_specs = [pl.BlockSpec((tm, d), lambda i: (i, 0)) for d, _ in out_rows]
    out_shape += [jax.ShapeDtypeStruct((1, n), F32) for n in out_accs]
    out_specs += [pl.BlockSpec((1, n), lambda i: (0, 0)) for n in out_accs]
    nr, npar, no, na = len(rows), len(params), len(out_rows), len(out_accs)

    def body(*refs):
        rvals = [r[...] for r in refs[:nr]]
        pvals = [r[...] for r in refs[nr:nr + npar]]
        outs, accs = fn(rvals, pvals)
        first_out = nr + npar + len(deps)
        for ref, v in zip(refs[first_out:first_out + no], outs, strict=True):
            ref[...] = v.astype(ref.dtype)
        if na:
            acc_refs = refs[first_out + no:]
            i = pl.program_id(0)

            @pl.when(i == 0)
            def _():
                for ref, v in zip(acc_refs, accs, strict=True):
                    ref[...] = v

            @pl.when(i > 0)
            def _():
                for ref, v in zip(acc_refs, accs, strict=True):
                    ref[...] += v

    res = pl.pallas_call(body, name=name, grid=(steps,), in_specs=in_specs, out_specs=out_specs,
                         out_shape=out_shape, compiler_params=_params())(*[r[0] for r in rows], *params, *deps)
    return list(res)


_DIMS = {"nn": ((1,), (0,)), "nt": ((1,), (1,)), "tn": ((0,), (0,))}


def _dot(a, b, mode="nn"):
    return lax.dot_general(a.astype(MXU_DTYPE), b.astype(MXU_DTYPE), (_DIMS[mode], ((), ())),
                           preferred_element_type=F32)


def matmul(name, a, b, mode, tm=None, tn=None, tk=None, out_dtype=F32, dep=None):
    if mode == "tn":
        K, M = a.shape
    else:
        M, K = a.shape
    N = b.shape[0] if mode == "nt" else b.shape[1]
    tm, tn, tk = tm or M, tn or N, tk or K
    nm, nn, nk = M // tm, N // tn, K // tk
    assert nm * tm == M and nn * tn == N and nk * tk == K
    a_spec = pl.BlockSpec((tk, tm), lambda i, j, k: (k, i)) if mode == "tn" else pl.BlockSpec((tm, tk), lambda i, j, k: (i, k))
    b_spec = pl.BlockSpec((tn, tk), lambda i, j, k: (j, k)) if mode == "nt" else pl.BlockSpec((tk, tn), lambda i, j, k: (k, j))
    deps = [] if dep is None else [dep]

    def body(a_ref, b_ref, *rest):
        o_ref, scratch = rest[len(deps)], rest[len(deps) + 1:]
        p = _dot(a_ref[...], b_ref[...], mode)
        if nk == 1:
            o_ref[...] = p.astype(o_ref.dtype)
        else:
            acc = scratch[0]
            k = pl.program_id(2)

            @pl.when(k == 0)
            def _():
                acc[...] = p

            @pl.when(k > 0)
            def _():
                acc[...] += p

            @pl.when(k == nk - 1)
            def _():
                o_ref[...] = acc[...].astype(o_ref.dtype)

    return pl.pallas_call(
        body, name=name, grid=(nm, nn, nk), in_specs=[a_spec, b_spec] + [pl.BlockSpec(memory_space=pl.ANY)] * len(deps),
        out_specs=pl.BlockSpec((tm, tn), lambda i, j, k: (i, j)),
        out_shape=jax.ShapeDtypeStruct((M, N), out_dtype),
        scratch_shapes=[pltpu.VMEM((tm, tn), F32)] if nk > 1 else [],
        compiler_params=_params())(a, b, *deps)


def _rms(x, g):
    rstd = lax.rsqrt(jnp.mean(x * x, axis=-1, keepdims=True) + EPS)
    n = x * rstd
    return n * g, n, rstd


def _rms_bwd(dy, n, rstd, g):
    dg = jnp.sum(dy * n, axis=0, keepdims=True)
    dn = dy * g
    dx = rstd * (dn - n * jnp.mean(dn * n, axis=-1, keepdims=True))
    return dx, dg


def _norm_bwd(dy, x, g):
    _, n, rstd = _rms(x, g)
    return _rms_bwd(dy, n, rstd, g)


def _colsum(v):
    return jnp.sum(v, axis=0, keepdims=True)


def _silu(x):
    return x * (1.0 / (1.0 + jnp.exp(-x)))


def _lane(shape):
    return lax.broadcasted_iota(jnp.int32, shape, 1)


def _group_mean(v, groups):
    i = lax.broadcasted_iota(jnp.int32, (LANES, LANES), 0)
    j = lax.broadcasted_iota(jnp.int32, (LANES, LANES), 1)
    g = jnp.zeros((LANES, LANES), F32)
    for lo, hi in groups:
        g = jnp.where((i >= lo) & (i < hi) & (j >= lo) & (j < hi), 1.0 / (hi - lo), g)
    head = v.astype(MXU_DTYPE)
    return _dot(head, g) + _dot(v - head.astype(F32), g)


def _in_groups(shape, groups):
    lane = _lane(shape)
    m = jnp.zeros(shape, jnp.bool_)
    for lo, hi in groups:
        m = m | ((lane >= lo) & (lane < hi))
    return m


def _grms(x, g, groups):
    rstd = lax.rsqrt(_group_mean(x * x, groups) + EPS)
    n = jnp.where(_in_groups(x.shape, groups), x * rstd, 0.0)
    return n * g, n, rstd


def _grms_bwd(dy, n, rstd, g, groups):
    dn = dy * g
    return rstd * (dn - n * _group_mean(dn * n, groups)), _colsum(dy * n)


def _rot(x, half, transpose=False):
    first = (_lane(x.shape) % (2 * half)) < half
    up = pltpu.roll(x, LANES - half, axis=1)
    down = pltpu.roll(x, half, axis=1)
    return jnp.where(first, up, -down) if transpose else jnp.where(first, -up, down)


def _rope(x, cos, sin, half):
    return x * cos + _rot(x, half) * sin


def _rope_bwd(dy, cos, sin, half):
    return dy * cos + _rot(dy * sin, half, transpose=True)


def _chunks(x):
    return [x[:, i:i + LANES] for i in range(0, x.shape[1], LANES)]


Q_GROUPS = ((0, NOPE), (NOPE, NOPE + ROPE))
K_GROUPS = ((0, NOPE),)
KPE_GROUPS = ((KPE_LO, KPE_LO + ROPE),)
DIL_GROUPS = ((0, DIL_DIM), (DIL_DIM, 2 * DIL_DIM))


def _col(width, rows=SEQ):
    return pl.BlockSpec((rows, width), lambda h: (0, h))


def _causal_tail(s, tq, fill):
    diag = s[:, s.shape[1] - tq:]
    keep = lax.broadcasted_iota(jnp.int32, diag.shape, 1) <= lax.broadcasted_iota(jnp.int32, diag.shape, 0)
    diag = jnp.where(keep, diag, fill)
    return diag if s.shape[1] == tq else jnp.concatenate([s[:, :s.shape[1] - tq], diag], axis=1)


def mla_fwd(name, q, k, v, scale, tq=256):
    S = q.shape[0]

    def body(q_ref, k_ref, v_ref, o_ref, lse_ref):
        nb = S // tq
        blk = lambda i: slice(i * tq, (i + 1) * tq)

        def scores(i):
            return _dot(q_ref[blk(i), :], k_ref[:(i + 1) * tq, :], "nt")

        def softmax(i, s):
            s = _causal_tail(s * scale, tq, NEG_INF)
            m = jnp.max(s, axis=-1, keepdims=True)
            e = jnp.exp(s - m)
            l = jnp.sum(e, axis=-1, keepdims=True)
            lse_ref[0, blk(i), :] = m + jnp.log(l)
            return (e * (1.0 / l)).astype(MXU_DTYPE)

        def weighted(i, p):
            o_ref[blk(i), :] = _dot(p, v_ref[:(i + 1) * tq, :])

        s, p_prev = scores(0), None
        for i in range(nb):
            s_next = scores(i + 1) if i + 1 < nb else None
            if p_prev is not None:
                weighted(i - 1, p_prev)
            p_prev, s = softmax(i, s), s_next
        weighted(nb - 1, p_prev)

    return pl.pallas_call(
        body, name=name, grid=(HEADS,), in_specs=[_col(LANES)] * 3,
        out_specs=[_col(LANES), pl.BlockSpec((1, S, 1), lambda h: (h, 0, 0))],
        out_shape=[jax.ShapeDtypeStruct((S, MIX_IN), F32), jax.ShapeDtypeStruct((HEADS, S, 1), F32)],
        compiler_params=_params())(q, k, v)


def mla_bwd(name, q, k, v, o, do, lse, scale, tq=256):
    S = q.shape[0]

    def body(q_ref, k_ref, v_ref, o_ref, do_ref, lse_ref, dq_ref, dkv_ref, dkpe_ref, dk_acc, dv_acc):
        dk_acc[...] = jnp.zeros_like(dk_acc)
        dv_acc[...] = jnp.zeros_like(dv_acc)
        for i in range(S // tq):
            kext = (i + 1) * tq
            blk = slice(i * tq, kext)
            qi, kk, vv = q_ref[blk, :], k_ref[:kext, :], v_ref[:kext, :]
            doi = do_ref[blk, :]
            s = _causal_tail(_dot(qi, kk, "nt") * scale, tq, NEG_INF)
            p = jnp.exp(s - lse_ref[0, blk, :])
            dp = _dot(doi, vv, "nt")
            delta = jnp.sum(doi * o_ref[blk, :], axis=-1, keepdims=True)
            ds = p * (dp - delta) * scale
            dq_ref[blk, :] = _dot(ds, kk)
            dk_acc[:kext, :] += _dot(ds, qi, "tn")
            dv_acc[:kext, :] += _dot(p, doi, "tn")
        dk = dk_acc[...]
        lane = _lane(dk.shape)
        dkv_ref[...] = jnp.where(lane < NOPE, dk, 0.0) + dv_acc[...]
        dkpe = jnp.where((lane >= KPE_LO) & (lane < KPE_LO + ROPE), dk, 0.0)
        h = pl.program_id(0)

        @pl.when(h == 0)
        def _():
            dkpe_ref[...] = dkpe

        @pl.when(h > 0)
        def _():
            dkpe_ref[...] += dkpe

    return pl.pallas_call(
        body, name=name, grid=(HEADS,),
        in_specs=[_col(LANES)] * 5 + [pl.BlockSpec((1, S, 1), lambda h: (h, 0, 0))],
        out_specs=[_col(LANES), _col(LANES), pl.BlockSpec((S, LANES), lambda h: (0, 0))],
        out_shape=[jax.ShapeDtypeStruct((S, HEADS * LANES), F32), jax.ShapeDtypeStruct((S, HEADS * LANES), F32),
                   jax.ShapeDtypeStruct((S, LANES), F32)],
        scratch_shapes=[pltpu.VMEM((S, LANES), F32), pltpu.VMEM((S, LANES), F32)],
        compiler_params=_params())(q, k, v, o, do, lse)


BAND_TQ = SPAN


def _band_blocks(L, tq):
    return [(i * tq, (i + 1) * tq, max(0, i * tq - SPAN)) for i in range(L // tq)]


def _class_rows(r, dil, lo, hi):
    return pl.ds(r + dil * lo, hi - lo, stride=dil) if dil > 1 else pl.ds(lo, hi - lo)


def _stack_heads(t, lo):
    zero = jnp.zeros_like(t)
    return jnp.concatenate([jnp.where(lo, t, zero), jnp.where(lo, zero, t)], axis=0)


def _band_mask2(q0, q1, k0):
    n = q1 - q0
    shape = (2 * n, q1 - k0)
    i = lax.broadcasted_iota(jnp.int32, shape, 0)
    dist = (jnp.where(i >= n, i - n, i) + q0) - (lax.broadcasted_iota(jnp.int32, shape, 1) + k0)
    return (dist >= 0) & (dist <= SPAN)


def _pair_col(col0=0):
    return pl.BlockSpec((SEQ, LANES), lambda j: (0, col0 // LANES + j))


def band_fwd(name, q, k, v, dil, dep=None):
    S = q.shape[0]
    L = S // dil
    tq = BAND_TQ
    scale = DIL_DIM ** -0.5
    deps = [] if dep is None else [dep]

    def body(q_ref, k_ref, v_ref, *rest):
        o_ref, lse_ref = rest[len(deps):]
        items = [(r, blk) for r in range(dil) for blk in _band_blocks(L, tq)]
        lo = _lane((tq, LANES)) < DIL_DIM

        def scores(item):
            r, (q0, q1, k0) = item
            qb = q_ref[_class_rows(r, dil, q0, q1), :].astype(MXU_DTYPE)
            return _dot(_stack_heads(qb, lo), k_ref[_class_rows(r, dil, k0, q1), :], "nt")

        def softmax(item, s):
            _, (q0, q1, k0) = item
            s = jnp.where(_band_mask2(q0, q1, k0), s * scale, NEG_INF)
            mx = jnp.max(s, axis=-1, keepdims=True)
            e = jnp.exp(s - mx)
            l = jnp.sum(e, axis=-1, keepdims=True)
            return (e * (1.0 / l)).astype(MXU_DTYPE), mx + jnp.log(l)

        def weighted(item, p, lse):
            r, (q0, q1, k0) = item
            pv = _dot(p, v_ref[_class_rows(r, dil, k0, q1), :])
            o_ref[_class_rows(r, dil, q0, q1), :] = jnp.where(lo, pv[:tq], pv[tq:])
            lse_ref[_class_rows(r, dil, q0, q1), :] = jnp.where(lo, lse[:tq], lse[tq:])

        s, prev = scores(items[0]), None
        for i, item in enumerate(items):
            s_next = scores(items[i + 1]) if i + 1 < len(items) else None
            if prev is not None:
                weighted(items[i - 1], *prev)
            prev, s = softmax(item, s), s_next
        weighted(items[-1], *prev)

    return pl.pallas_call(
        body, name=name, grid=(DIL_WIDTH // LANES,),
        in_specs=[_pair_col()] * 2 + [_pair_col(P_VD)] + [pl.BlockSpec(memory_space=pl.ANY)] * len(deps), out_specs=[_pair_col()] * 2,
        out_shape=[jax.ShapeDtypeStruct((S, DIL_WIDTH), F32)] * 2, compiler_params=_params())(q, k, v, *deps)


def band_bwd(name, q, k, v, lse, lse_mix, o_cat, do_cat, dil, before=None):
    S = q.shape[0]
    L = S // dil
    tq = BAND_TQ
    scale = DIL_DIM ** -0.5
    before = list(before or [])

    def body(q_ref, k_ref, v_ref, lse_ref, mix_ref, o_ref, do_ref, *rest):
        dq_ref, dk_ref, dv_ref = rest[len(before):]
        if before:
            dq0_ref, dk0_ref, dv0_ref = rest[:3]
            dk_ref[...] = dk0_ref[...]
            dv_ref[...] = dv0_ref[...]
        else:
            dk_ref[...] = jnp.zeros_like(dk_ref)
            dv_ref[...] = jnp.zeros_like(dv_ref)
        items = [(r, blk) for r in range(dil) for blk in _band_blocks(L, tq)]
        lo = _lane((tq, LANES)) < DIL_DIM
        per_head = lambda t: jnp.concatenate([t[:, 0:1], t[:, DIL_DIM:DIL_DIM + 1]], axis=0)

        def scores(item):
            r, (q0, q1, k0) = item
            qrows, krows = _class_rows(r, dil, q0, q1), _class_rows(r, dil, k0, q1)
            lse_p, dout = lse_ref[qrows, :], do_ref[qrows, :]
            w2 = per_head(jnp.exp(lse_p - mix_ref[qrows, :]))
            dd = dout * o_ref[qrows, :]
            big_d = jnp.concatenate([jnp.sum(jnp.where(lo, dd, 0.0), axis=-1, keepdims=True),
                                     jnp.sum(jnp.where(lo, 0.0, dd), axis=-1, keepdims=True)], axis=0)
            q2 = _stack_heads(q_ref[qrows, :].astype(MXU_DTYPE), lo)
            dom = (_stack_heads(dout, lo) * w2).astype(MXU_DTYPE)
            return (_dot(q2, k_ref[krows, :], "nt"), _dot(dom, v_ref[krows, :], "nt"), per_head(lse_p), w2 * big_d, q2, dom)

        def softmax_bwd(item, s, dp, lse2, wd2, q2, dom):
            _, (q0, q1, k0) = item
            p = jnp.where(_band_mask2(q0, q1, k0), jnp.exp(s * scale - lse2), 0.0)
            return p.astype(MXU_DTYPE), (p * (dp - wd2) * scale).astype(MXU_DTYPE), q2, dom

        def grads(item, p, ds, q2, dom):
            r, (q0, q1, k0) = item
            qrows, krows = _class_rows(r, dil, q0, q1), _class_rows(r, dil, k0, q1)
            dq2 = _dot(ds, k_ref[krows, :])
            dq = jnp.where(lo, dq2[:tq], dq2[tq:])
            dq_ref[qrows, :] = dq + dq0_ref[qrows, :] if before else dq
            dk_ref[krows, :] += _dot(ds, q2, "tn")
            dv_ref[krows, :] += _dot(p, dom, "tn")

        sc, prev = scores(items[0]), None
        for i, item in enumerate(items):
            sc_next = scores(items[i + 1]) if i + 1 < len(items) else None
            if prev is not None:
                grads(items[i - 1], *prev)
            prev, sc = softmax_bwd(item, *sc), sc_next
        grads(items[-1], *prev)

    cat = _pair_col(HEADS * LANES)
    return pl.pallas_call(
        body, name=name, grid=(DIL_WIDTH // LANES,),
        in_specs=[_pair_col()] * 2 + [_pair_col(P_VD)] + [_pair_col()] * 2 + [cat] * 2 + [_pair_col()] * len(before),
        out_specs=[_pair_col()] * 3, out_shape=[jax.ShapeDtypeStruct((S, DIL_WIDTH), F32)] * 3,
        compiler_params=_params())(q, k, v, lse, lse_mix, o_cat, do_cat, *before)


def combine_fwd(name, outs, lses, o_cat, tm=512):
    S = outs[0].shape[0]

    def body(o1, o2, o3, l1, l2, l3, cat_in, cat_out, mix_ref):
        ls = [l1[...], l2[...], l3[...]]
        m = jnp.maximum(jnp.maximum(ls[0], ls[1]), ls[2])
        e = [jnp.exp(l - m) for l in ls]
        den = e[0] + e[1] + e[2]
        cat_out[...] = (e[0] / den) * o1[...] + (e[1] / den) * o2[...] + (e[2] / den) * o3[...]
        mix_ref[...] = m + jnp.log(den)

    row = pl.BlockSpec((tm, DIL_WIDTH), lambda i: (i, 0))
    return pl.pallas_call(
        body, name=name, grid=(S // tm,), in_specs=[row] * 6 + [pl.BlockSpec(memory_space=pl.ANY)],
        out_specs=[pl.BlockSpec((tm, DIL_WIDTH), lambda i: (i, HEADS * LANES // DIL_WIDTH)), row],
        out_shape=[jax.ShapeDtypeStruct(o_cat.shape, F32), jax.ShapeDtypeStruct((S, DIL_WIDTH), F32)],
        input_output_aliases={6: 0}, compiler_params=_params())(*outs, *lses, o_cat)


FFN_FWD_ROWS = 512
FFN_BWD_ROWS = 256
CONV_PAD = SUBLANES


def _window(x, k):
    groups = x.reshape(-1, SUBLANES, x.shape[1])
    turned = pltpu.roll(groups, SUBLANES - k, axis=1)
    stays = lax.broadcasted_iota(jnp.int32, (groups.shape[0] - 1,) + groups.shape[1:], 1) < SUBLANES - k
    return jnp.where(stays, turned[:-1], turned[1:]).reshape(-1, x.shape[1])


def _earlier(ref, r0, rows, n):
    if r0 == 0:
        x = jnp.concatenate([jnp.zeros((SUBLANES, ref.shape[1]), F32), ref[:rows, :]], axis=0)
    else:
        x = ref[r0 - SUBLANES:r0 + rows, :]
    return _window(x, SUBLANES - n)


CONV_TC = 256
CONV_NB = D_FF // CONV_TC


def _half_specs(rows, rows_axis=False):
    if rows_axis:
        return [pl.BlockSpec((rows, D_MODEL), lambda j: (j, 0)), pl.BlockSpec((rows, D_MODEL), lambda j: (j + CONV_NB, 0))]
    return [pl.BlockSpec((rows, CONV_TC), lambda j: (0, j)), pl.BlockSpec((rows, CONV_TC), lambda j: (0, j + CONV_NB))]


def _whole(a):
    return pl.BlockSpec(a.shape, lambda j: (0,) * a.ndim)


def _up_pair(h, ug_ref, uv_ref):
    return jnp.concatenate([_dot(h, ug_ref[...], "nt"), _dot(h, uv_ref[...], "nt")], axis=1)


def _conv_taps(up_ref, r0, rows, w, b):
    uin, u1, u2 = up_ref[r0:r0 + rows, :], _earlier(up_ref, r0, rows, 1), _earlier(up_ref, r0, rows, 2)
    return uin, u1, u2, w[2:3, :] * uin + w[1:2, :] * u1 + w[0:1, :] * u2 + b


def ffn_fwd(name, h, w_up_t, w_conv, b_conv, w_down):
    S = h.shape[0]

    def body(h_ref, ug_ref, uv_ref, wg_ref, wv_ref, bg_ref, bv_ref, wd_ref, dn_ref, up_ref):
        @pl.when(pl.program_id(0) == 0)
        def _():
            dn_ref[...] = jnp.zeros_like(dn_ref)

        w = jnp.concatenate([wg_ref[...], wv_ref[...]], axis=1)
        b = jnp.concatenate([bg_ref[...], bv_ref[...]], axis=1)
        rows = FFN_FWD_ROWS
        starts = list(range(0, S, rows))

        def project(r0):
            up_ref[r0:r0 + rows, :] = _up_pair(h_ref[r0:r0 + rows, :], ug_ref, uv_ref)

        def gate(r0):
            u = _conv_taps(up_ref, r0, rows, w, b)[3]
            return (_silu(u[:, :CONV_TC]) * u[:, CONV_TC:]).astype(MXU_DTYPE)

        def project_down(r0, act):
            dn_ref[r0:r0 + rows, :] += _dot(act, wd_ref[...])

        project(starts[0])
        act_prev = None
        for i, r0 in enumerate(starts):
            if i + 1 < len(starts):
                project(starts[i + 1])
            if act_prev is not None:
                project_down(starts[i - 1], act_prev)
            act_prev = gate(r0)
        project_down(starts[-1], act_prev)

    return pl.pallas_call(
        body, name=name, grid=(CONV_NB,),
        in_specs=[_whole(h)] + _half_specs(CONV_TC, rows_axis=True) + _half_specs(3) + _half_specs(1)
        + [pl.BlockSpec((CONV_TC, w_down.shape[1]), lambda j: (j, 0))],
        out_specs=[pl.BlockSpec((S, w_down.shape[1]), lambda j: (0, 0)), pl.BlockSpec((S, 2 * CONV_TC), lambda j: (0, j))],
        out_shape=[jax.ShapeDtypeStruct((S, w_down.shape[1]), F32), jax.ShapeDtypeStruct((S, 2 * D_FF), F32)],
        compiler_params=_params())(h, w_up_t, w_up_t, w_conv, w_conv, b_conv, b_conv, w_down)


def ffn_bwd(name, h, up, w_up_t, w_conv, b_conv, d_dn, w_down):
    S, D = h.shape

    def body(h_ref, up_ref, ug_ref, uv_ref, wg_ref, wv_ref, bg_ref, bv_ref, dd_ref, wd_ref,
             dh_ref, gup_ref, gd_ref, dwg_ref, dwv_ref, dbg_ref, dbv_ref, du_ref, dup_ref, act_ref):
        @pl.when(pl.program_id(0) == 0)
        def _():
            dh_ref[...] = jnp.zeros_like(dh_ref)

        w = jnp.concatenate([wg_ref[...], wv_ref[...]], axis=1)
        b = jnp.concatenate([bg_ref[...], bv_ref[...]], axis=1)
        w_pair = jnp.concatenate([ug_ref[...], uv_ref[...]], axis=0)
        rows = FFN_BWD_ROWS
        starts = list(range(0, S, rows))
        du_ref[S:S + CONV_PAD, :] = jnp.zeros((CONV_PAD, 2 * CONV_TC), F32)

        def project(r0):
            return _dot(dd_ref[r0:r0 + rows, :], wd_ref[...], "nt")

        def through_conv(r0, da):
            uin, u1, u2, u = _conv_taps(up_ref, r0, rows, w, b)
            gate, val = u[:, :CONV_TC], u[:, CONV_TC:]
            sig = 1.0 / (1.0 + jnp.exp(-gate))
            du = jnp.concatenate([da * val * (sig * (1.0 + gate * (1.0 - sig))), da * (gate * sig)], axis=1)
            du_ref[r0:r0 + rows, :] = du
            act_ref[r0:r0 + rows, :] = (gate * sig * val).astype(MXU_DTYPE)
            dw = jnp.concatenate([_colsum(du * u2), _colsum(du * u1), _colsum(du * uin)], axis=0)
            return dw, _colsum(du)

        def back_up(r0):
            du = du_ref[r0:r0 + rows + CONV_PAD, :]
            dup = (w[2:3, :] * du[:rows] + w[1:2, :] * _window(du, 1) + w[0:1, :] * _window(du, 2)).astype(MXU_DTYPE)
            dup_ref[r0:r0 + rows, :] = dup
            dh_ref[r0:r0 + rows, :] += _dot(dup, w_pair)

        dw, db = 0.0, 0.0
        da = project(starts[0])
        for i, r0 in enumerate(starts):
            da_next = project(starts[i + 1]) if i + 1 < len(starts) else None
            dw_c, db_c = through_conv(r0, da)
            if i > 0:
                back_up(starts[i - 1])
            dw, db, da = dw + dw_c, db + db_c, da_next
        back_up(starts[-1])
        g_up, g_dn = _dot(dup_ref[...], h_ref[...], "tn"), _dot(act_ref[...], dd_ref[...], "tn")
        gup_ref[0], gup_ref[1] = g_up[:CONV_TC].astype(gup_ref.dtype), g_up[CONV_TC:].astype(gup_ref.dtype)
        gd_ref[...] = g_dn.astype(gd_ref.dtype)
        dwg_ref[...], dwv_ref[...] = dw[:, :CONV_TC], dw[:, CONV_TC:]
        dbg_ref[...], dbv_ref[...] = db[:, :CONV_TC], db[:, CONV_TC:]

    half = lambda rows: pl.BlockSpec((rows, CONV_TC), lambda j: (0, j))
    rows_blk = pl.BlockSpec((CONV_TC, D), lambda j: (j, 0))
    dh, gup, gd, dwg, dwv, dbg, dbv = pl.pallas_call(
        body, name=name, grid=(CONV_NB,),
        in_specs=[_whole(h), pl.BlockSpec((S, 2 * CONV_TC), lambda j: (0, j))] + _half_specs(CONV_TC, rows_axis=True) + _half_specs(3)
        + _half_specs(1) + [_whole(d_dn), rows_blk],
        out_specs=[pl.BlockSpec((S, D), lambda j: (0, 0)), pl.BlockSpec((2, CONV_TC, D), lambda j: (0, j, 0)), rows_blk,
                   half(3), half(3), half(1), half(1)],
        out_shape=[jax.ShapeDtypeStruct((S, D), F32), jax.ShapeDtypeStruct((2, D_FF, D), MXU_DTYPE),
                   jax.ShapeDtypeStruct((D_FF, D), MXU_DTYPE)]
        + [jax.ShapeDtypeStruct((3, D_FF), F32)] * 2 + [jax.ShapeDtypeStruct((1, D_FF), F32)] * 2,
        scratch_shapes=[pltpu.VMEM((S + CONV_PAD, 2 * CONV_TC), F32), pltpu.VMEM((S, 2 * CONV_TC), MXU_DTYPE),
                        pltpu.VMEM((S, CONV_TC), MXU_DTYPE)],
        compiler_params=_params())(h, up, w_up_t, w_up_t, w_conv, w_conv, b_conv, b_conv, d_dn, w_down)
    return dh, gup.reshape(2 * D_FF, D), gd, jnp.concatenate([dwg, dwv], axis=1), jnp.concatenate([dbg, dbv], axis=1)


def adamw(name, w, parts, m, v, tr=None):
    apart = w.ndim == 3
    R, C = w.shape[0], w.shape[-1]
    tr = tr or R
    assert R % tr == 0
    c1 = 1.0 - ADAM_B1 ** ADAM_STEP
    c2 = 1.0 - ADAM_B2 ** ADAM_STEP
    np_ = len(parts)

    def body(*refs):
        w_ref, m_ref, v_ref = refs[0], refs[1 + np_], refs[2 + np_]
        go_ref, d_ref, mo_ref, vo_ref = refs[3 + np_:]
        terms = []
        for part, ref in zip(parts, refs[1:1 + np_], strict=True):
            terms += [ref[...]] if part.ndim == 2 else [ref[p] for p in range(part.shape[0])]
        g = terms[0].astype(F32)
        for term in terms[1:]:
            g = g + term.astype(F32)
        m2 = ADAM_B1 * m_ref[...] + (1.0 - ADAM_B1) * g
        v2 = ADAM_B2 * v_ref[...] + (1.0 - ADAM_B2) * (g * g)
        go_ref[...] = g
        mo_ref[...] = m2
        vo_ref[...] = v2
        d_ref[...] = -ADAM_LR * ((m2 / c1) / (jnp.sqrt(v2 / c2) + ADAM_EPS) + ADAM_WD * w_ref[...])

    blk = pl.BlockSpec((tr, C), lambda i: (i, 0))
    own = pl.BlockSpec((tr, None, C), lambda i: (i, 0, 0)) if apart else blk
    part_specs = [blk if p.ndim == 2 else pl.BlockSpec((p.shape[0], tr, C), lambda i: (0, i, 0)) for p in parts]
    return pl.pallas_call(
        body, name=name, grid=(R // tr,),
        in_specs=[own] + part_specs + [own, own], out_specs=[own] * 4,
        out_shape=[jax.ShapeDtypeStruct(w.shape, F32)] * 4, compiler_params=_params())(w, *parts, m, v)


def _place():
    return lax.axis_index("x"), lax.axis_index("y"), lax.axis_index("c")


def ada_modulation(name, c, w_ada, after=()):
    n_mod = w_ada.shape[1]

    def exchange(src_ref, dst_ref, send_sems, recv_sems):
        x, y, c_ = _place()
        me = 4 * x + 2 * y + c_
        copies = []
        for k in range(1, N_DEV):
            px, py, pc = x ^ (k >> 2), y ^ ((k >> 1) & 1), c_ ^ (k & 1)
            copies.append(pltpu.make_async_remote_copy(
                src_ref=src_ref, dst_ref=dst_ref.at[me], send_sem=send_sems.at[k - 1], recv_sem=recv_sems.at[k - 1],
                device_id=(px, py, pc), device_id_type=MESH_ID))
        for cp in copies:
            cp.start()
        for cp in copies:
            cp.wait_recv()
        for cp in copies:
            cp.wait_send()
        return me

    def body(c_ref, w_ref, *refs):
        sc_ref, mod_ref, c_all, send_c, recv_c, send_m, recv_m = refs[len(after):]
        me = exchange(c_ref, c_all, send_c, recv_c)
        c_all[me] = c_ref[...]
        sc = _silu(jnp.concatenate([c_all[p] for p in range(N_DEV)], axis=0))
        sc_ref[...] = sc.astype(sc_ref.dtype)
        mod_ref[me] = _dot(sc, w_ref[...])
        exchange(mod_ref.at[me], mod_ref, send_m, recv_m)

    vmem = pl.BlockSpec(memory_space=pltpu.VMEM)
    return pl.pallas_call(
        body, name=name, in_specs=[vmem, vmem] + [pl.BlockSpec(memory_space=pl.ANY)] * len(after), out_specs=[vmem, vmem],
        out_shape=[jax.ShapeDtypeStruct((N_DEV, c.shape[1]), MXU_DTYPE), jax.ShapeDtypeStruct((N_DEV, N_DEV, n_mod), F32)],
        scratch_shapes=[pltpu.VMEM((N_DEV, 1, c.shape[1]), F32)] + [pltpu.SemaphoreType.DMA((N_DEV - 1,))] * 4,
        compiler_params=pltpu.CompilerParams(has_side_effects=True, vmem_limit_bytes=VMEM_LIMIT))(c, w_ada, *after)


HBM_SPEC = pl.BlockSpec(memory_space=pltpu.HBM)
SEM_SPEC = pl.BlockSpec(memory_space=pltpu.SEMAPHORE)
DATAFLOW = pltpu.SideEffectType.DATAFLOW_SIDE_EFFECTING


def _exchange_copies(srcs, lands, send_sems, recv_sems, gather, first=0):
    x, y, c = _place()
    me = 4 * x + 2 * y + c
    out = []
    for t, (src, land) in enumerate(zip(srcs, lands, strict=True)):
        for k in range(1, N_DEV):
            px, py, pc = x ^ (k >> 2), y ^ ((k >> 1) & 1), c ^ (k & 1)
            sem = 7 * (first + t) + k - 1
            out.append((k, pltpu.make_async_remote_copy(
                src_ref=src if gather else src.at[4 * px + 2 * py + pc],
                dst_ref=land.at[me] if gather else land.at[k - 1],
                send_sem=send_sems.at[sem], recv_sem=recv_sems.at[sem],
                device_id=(px, py, pc), device_id_type=MESH_ID)))
    return out


def _own_copies(srcs, lands, send_sems, gather, first=0):
    x, y, c = _place()
    me = 4 * x + 2 * y + c
    total = send_sems.shape[0] // N_DEV
    return [pltpu.make_async_copy(src if gather else src.at[me], land.at[me] if gather else land.at[N_DEV - 1],
                                  send_sems.at[7 * total + first + t])
            for t, (src, land) in enumerate(zip(srcs, lands, strict=True))]


TREE_DIRECT = (1, 2, 4, 6)
TREE_FORWARDED = (3, 5, 7)


def exchange_start(name, arrs, gather, after=None, tree=False):
    n = len(arrs)
    lands = [lax.empty((N_DEV,) + (a.shape if gather else a.shape[1:]), a.dtype) for a in arrs]
    deps = [] if after is None else [after]

    def body(*refs):
        srcs, land_refs = refs[:n], refs[n:2 * n]
        send_sems, recv_sems = refs[2 * n + len(deps)], refs[2 * n + len(deps) + 1]
        token = refs[-1]
        for k, cp in _exchange_copies(srcs, land_refs, send_sems, recv_sems, gather):
            if not tree or k in TREE_DIRECT:
                cp.start()
        for cp in _own_copies(srcs, land_refs, send_sems, gather):
            cp.start()
        token[...] = jnp.zeros_like(token)

    hbm = lambda a: pltpu.HBM(a.shape, a.dtype)
    res = pl.pallas_call(
        body, name=name,
        out_shape=(pltpu.SemaphoreType.DMA((N_DEV * n,)), pltpu.SemaphoreType.DMA((7 * n,)), *[hbm(a) for a in arrs],
                   *[hbm(l) for l in lands], jax.ShapeDtypeStruct((8, 128), F32)),
        in_specs=[HBM_SPEC] * (2 * n) + [pl.BlockSpec(memory_space=pl.ANY)] * len(deps),
        out_specs=(SEM_SPEC, SEM_SPEC, *[HBM_SPEC] * (2 * n), pl.BlockSpec(memory_space=pltpu.VMEM)),
        input_output_aliases={i: 2 + i for i in range(2 * n)},
        compiler_params=pltpu.CompilerParams(has_side_effects=DATAFLOW),
    )(*[pltpu.with_memory_space_constraint(a, pltpu.HBM) for a in arrs + lands], *deps)
    return res[0], res[1], list(res[2:2 + n]), list(res[2 + n:2 + 2 * n]), res[-1]


def exchange_forward(name, started, after, first=0, count=None):
    send_sems, recv_sems, srcs, lands, _ = started
    count = len(srcs) - first if count is None else count
    mine = lands[first:first + count]
    n = len(mine)

    def copies(land_refs, send_ref, recv_ref):
        x, y, c = _place()
        out = []
        for t, land in enumerate(land_refs):
            for k in (2, 4, 6):
                slot = land.at[4 * (x ^ (k >> 2)) + 2 * (y ^ ((k >> 1) & 1)) + c]
                came, goes = 7 * (first + t) + k - 1, 7 * (first + t) + (k ^ 1) - 1
                out.append((
                    pltpu.make_async_remote_copy(src_ref=slot, dst_ref=slot, send_sem=send_ref.at[came], recv_sem=recv_ref.at[came],
                                                 device_id=(x, y, c), device_id_type=MESH_ID),
                    pltpu.make_async_remote_copy(src_ref=slot, dst_ref=slot, send_sem=send_ref.at[goes], recv_sem=recv_ref.at[goes],
                                                 device_id=(x, y, 1 - c), device_id_type=MESH_ID)))
        return out

    after = list(after) if isinstance(after, (list, tuple)) else [after]

    def arrived(*refs):
        for came, _ in copies(refs[:n], refs[n], refs[n + 1]):
            came.wait_recv()

    def pass_on(*refs):
        for _, goes in copies(refs[:n], refs[n], refs[n + 1]):
            goes.start()
        refs[-1][...] = jnp.zeros_like(refs[-1])

    hbm = lambda a: pltpu.HBM(a.shape, a.dtype)
    here = pl.pallas_call(
        arrived, name=name + "_arrived", out_shape=tuple(hbm(a) for a in mine),
        in_specs=[HBM_SPEC] * n + [SEM_SPEC, SEM_SPEC] + [pl.BlockSpec(memory_space=pl.ANY)] * len(after),
        out_specs=tuple([HBM_SPEC] * n), input_output_aliases={i: i for i in range(n)},
        compiler_params=pltpu.CompilerParams(has_side_effects=DATAFLOW),
    )(*mine, send_sems, recv_sems, *after)
    res = pl.pallas_call(
        pass_on, name=name, out_shape=(*[hbm(a) for a in mine], jax.ShapeDtypeStruct((8, 128), F32)),
        in_specs=[HBM_SPEC] * n + [SEM_SPEC, SEM_SPEC],
        out_specs=(*[HBM_SPEC] * n, pl.BlockSpec(memory_space=pltpu.VMEM)), input_output_aliases={i: i for i in range(n)},
        compiler_params=pltpu.CompilerParams(has_side_effects=DATAFLOW),
    )(*here, send_sems, recv_sems)
    lands = lands[:first] + list(res[:n]) + lands[first + count:]
    return (send_sems, recv_sems, srcs, lands, res[-1])


def exchange_wait(name, started, gather, after, first=0, count=None, tree=False):
    send_sems, recv_sems, srcs, lands, _ = started
    count = len(srcs) - first if count is None else count
    srcs, lands = srcs[first:first + count], lands[first:first + count]
    n = len(srcs)

    def body(*refs):
        src_refs, land_refs = refs[:n], refs[n:2 * n]
        copies = _exchange_copies(src_refs, land_refs, refs[2 * n], refs[2 * n + 1], gather, first)
        for _, cp in copies:
            cp.wait_send()
        for k, cp in copies:
            if not tree or k in (1,) + TREE_FORWARDED:
                cp.wait_recv()
        for cp in _own_copies(src_refs, land_refs, refs[2 * n], gather, first):
            cp.wait()

    hbm = lambda a: pltpu.HBM(a.shape, a.dtype)
    res = pl.pallas_call(
        body, name=name, out_shape=tuple(hbm(a) for a in srcs + lands),
        in_specs=[HBM_SPEC] * (2 * n) + [SEM_SPEC, SEM_SPEC, pl.BlockSpec(memory_space=pl.ANY)],
        out_specs=tuple([HBM_SPEC] * (2 * n)), input_output_aliases={i: i for i in range(2 * n)},
        compiler_params=pltpu.CompilerParams(has_side_effects=DATAFLOW),
    )(*srcs, *lands, send_sems, recv_sems, after)
    return list(res[:n]), list(res[n:])


def _gather_cols(stack):
    p, k, n = stack.shape
    return stack.transpose(1, 0, 2).reshape(k, p * n)


def _scatter_cols(full):
    k, n = full.shape
    return full.reshape(k, N_DEV, n // N_DEV).transpose(1, 0, 2)


def _gather_rows(stack):
    p, r, n = stack.shape
    return stack.reshape(p * r, n)


def _scatter_rows(full):
    r, n = full.shape
    return full.reshape(N_DEV, r // N_DEV, n)


_IN_NAT = Q_LORA + KV_LORA
TRANSPOSED = ("w_in", "w_q_b", "w_up")
ROWS_APART = ("w_in", "w_conv")


def to_kernel_layout(name, w):
    if name == "w_in":
        z = lambda n: jnp.zeros((n, w.shape[1]), w.dtype)
        return jnp.concatenate([w[:_IN_NAT], z(KPE_LO), w[_IN_NAT:_IN_NAT + ROPE], z(LANES - KPE_LO - ROPE), w[_IN_NAT + ROPE:]], axis=0)
    if name == "w_q_b":
        return jnp.pad(w.reshape(HEADS, NOPE + ROPE, -1), ((0, 0), (0, LANES - NOPE - ROPE), (0, 0))).reshape(HEADS * LANES, -1)
    if name == "w_o":
        mla = jnp.pad(w[:HEADS * NOPE].reshape(HEADS, NOPE, -1), ((0, 0), (LANES - NOPE, 0), (0, 0))).reshape(HEADS * LANES, -1)
        return jnp.concatenate([mla, w[HEADS * NOPE:]], axis=0)
    return w


def from_kernel_layout(name, g):
    if name == "w_in":
        return jnp.concatenate([g[:_IN_NAT], g[P_KPE + KPE_LO:P_KPE + KPE_LO + ROPE], g[P_QD:]], axis=0)
    if name == "w_q_b":
        return g.reshape(HEADS, LANES, -1)[:, :NOPE + ROPE, :].reshape(HEADS * (NOPE + ROPE), -1)
    if name == "w_o":
        mla = g[:HEADS * LANES].reshape(HEADS, LANES, -1)[:, LANES - NOPE:, :].reshape(HEADS * NOPE, -1)
        return jnp.concatenate([mla, g[HEADS * LANES:]], axis=0)
    return g


SMALL_COLS = 1024
SMALL_ROWS = 24
SMALL_AT = {"loss": (0, 0, 1), "b_ada": (1, 0, 6 * D_MODEL), "g_mix_norm": (7, 0, D_MODEL), "g_q_lat": (8, 0, Q_LORA),
            "g_kv_lat": (9, 0, KV_LORA), "g_mla_q_nope": (10, 0, NOPE), "g_mla_q_pe": (10, 128, ROPE),
            "g_mla_k_nope": (10, 256, NOPE), "g_mla_k_pe": (10, 384, ROPE), "g_dil_q": (10, 512, DIL_DIM),
            "g_dil_k": (10, 640, DIL_DIM), "g_ffn_norm": (11, 0, D_MODEL), "b_conv": (12, 0, 2 * D_FF)}
SMALL_PARAMS = tuple(n for n in SMALL_AT if n != "loss")


def _pack_small(values):
    by_row = {}
    for name, (row, off, n) in SMALL_AT.items():
        by_row.setdefault(row, []).append((off, values[name].reshape(-1).astype(F32)))
    out = []
    for row in sorted(by_row):
        pieces, at = [], 0
        for off, v in sorted(by_row[row], key=lambda t: t[0]):
            pieces += [jnp.zeros((off - at,), F32), v]
            at = off + v.shape[0]
        flat = jnp.concatenate(pieces)
        nrows = -(-flat.shape[0] // SMALL_COLS)
        out.append(jnp.pad(flat, (0, nrows * SMALL_COLS - flat.shape[0])).reshape(nrows, SMALL_COLS))
    packed = jnp.concatenate(out, axis=0)
    return jnp.pad(packed, ((0, SMALL_ROWS - packed.shape[0]), (0, 0)))


def _adam(w, g, m, v):
    c1 = 1.0 - ADAM_B1 ** ADAM_STEP
    c2 = 1.0 - ADAM_B2 ** ADAM_STEP
    m2 = ADAM_B1 * m + (1.0 - ADAM_B1) * g
    v2 = ADAM_B2 * v + (1.0 - ADAM_B2) * (g * g)
    return -ADAM_LR * ((m2 / c1) / (jnp.sqrt(v2 / c2) + ADAM_EPS) + ADAM_WD * w), m2, v2


def adamw_small(name, stack, params):
    flat = [a for n in SMALL_PARAMS for a in params[n]]

    def body(stack_ref, *refs):
        ins, outs = refs[:len(flat)], refs[len(flat):]
        g_all = stack_ref[0]
        for p in range(1, N_DEV):
            g_all = g_all + stack_ref[p]
        outs[0][...] = g_all[0:1, 0:1]
        for i, pname in enumerate(SMALL_PARAMS):
            row, off, n = SMALL_AT[pname]
            w_ref, m_ref, v_ref = ins[3 * i:3 * i + 3]
            go_ref, d_ref, mo_ref, vo_ref = outs[1 + 4 * i:5 + 4 * i]
            for c0 in range(0, n, SMALL_COLS):
                cn = min(SMALL_COLS, n - c0)
                r = row + c0 // SMALL_COLS
                g = g_all[r:r + 1, off:off + cn]
                cols = (slice(None), slice(c0, c0 + cn))
                d, m2, v2 = _adam(w_ref[cols], g, m_ref[cols], v_ref[cols])
                go_ref[cols], d_ref[cols], mo_ref[cols], vo_ref[cols] = g, d, m2, v2

    whole = lambda a: pl.BlockSpec(a.shape, lambda: (0,) * a.ndim)
    out_shape = [jax.ShapeDtypeStruct((1, 1), F32)] + [jax.ShapeDtypeStruct(a.shape, F32) for n in SMALL_PARAMS for a in params[n][:1] * 4]
    res = pl.pallas_call(body, name=name, in_specs=[whole(stack)] + [whole(a) for a in flat],
                         out_specs=[pl.BlockSpec(s.shape, lambda s=s: (0,) * len(s.shape)) for s in out_shape],
                         out_shape=out_shape, compiler_params=_params())(stack, *flat)
    return res[0], {n: res[1 + 4 * i:5 + 4 * i] for i, n in enumerate(SMALL_PARAMS)}


def _local_step(x, pos, mod, target, w, fetch, emit, halfway=lambda after: None):
    S = SEQ
    sh1, sc1, g1, sh2, sc2, g2 = [mod[:, i * D_MODEL:(i + 1) * D_MODEL] for i in range(6)]
    zeros = lambda n: jnp.zeros((1, n), F32)
    g_q = jnp.concatenate([w["g_mla_q_nope"], w["g_mla_q_pe"], zeros(LANES - NOPE - ROPE)], axis=1)
    g_k = jnp.concatenate([w["g_mla_k_nope"], zeros(LANES - NOPE)], axis=1)
    g_kpe = jnp.concatenate([zeros(KPE_LO), w["g_mla_k_pe"], zeros(LANES - KPE_LO - ROPE)], axis=1)
    g_dq = jnp.concatenate([w["g_dil_q"]] * 2, axis=1)
    g_dk = jnp.concatenate([w["g_dil_k"]] * 2, axis=1)
    b_conv = w["b_conv"]

    def inv_freq(d):
        return jnp.power(ROPE_THETA, -2.0 * jnp.arange(d // 2, dtype=F32) / d)

    n_m, n_d = ROPE // 2, DIL_DIM // 2
    freqs = jnp.concatenate([inv_freq(ROPE), inv_freq(DIL_DIM), jnp.zeros((LANES - n_m - n_d,), F32)]).reshape(1, LANES)

    def tables_fn(rows, params):
        (p,), (f,) = rows, params
        c, s = jnp.cos(p * f), jnp.sin(p * f)
        one, zero = jnp.ones_like(c), jnp.zeros_like(c)
        mla = lambda t, fill: jnp.concatenate([fill[:, :KPE_LO], t[:, :n_m], t[:, :n_m], fill[:, :LANES - KPE_LO - ROPE]], axis=1)
        dil = lambda t: jnp.concatenate([t[:, n_m:n_m + n_d]] * 4, axis=1)
        return [mla(c, one), mla(s, zero), dil(c), dil(s)], []

    cos_m, sin_m, cos_d, sin_d = rowwise("rope_tables", tables_fn, [pos], [freqs], [(LANES, F32)] * 4)
    tables = [cos_m, sin_m, cos_d, sin_d]
    H_M, H_D = ROPE // 2, DIL_DIM // 2

    def ln1_fn(rows, params):
        (xv,), (g, sc, sh) = rows, params
        y, _, _ = _rms(xv, g)
        return [y * (1.0 + sc) + sh], []

    (h,) = rowwise("ln1_fwd", ln1_fn, [x], [w["g_mix_norm"], sc1, sh1], [(D_MODEL, MXU_DTYPE)], dep=sin_d)
    w_in = fetch("w_in", h)

    def proj_fn(rows, params):
        (hv, cm, sm, cd, sd), (w_t, gq, gkv, gkp, gdq, gdk) = rows, params
        pv = _dot(hv, w_t, "nt")
        kper = _rope(_grms(pv[:, P_KPE:P_QD], gkp, KPE_GROUPS)[0], cm, sm, H_M)
        qd = [_rope(_grms(c, gdq, DIL_GROUPS)[0], cd, sd, H_D) for c in _chunks(pv[:, P_QD:P_KD])]
        kd = [_rope(_grms(c, gdk, DIL_GROUPS)[0], cd, sd, H_D) for c in _chunks(pv[:, P_KD:P_VD])]
        return [pv, _rms(pv[:, P_QLAT:P_KVLAT], gq)[0], _rms(pv[:, P_KVLAT:P_KPE], gkv)[0], kper,
                jnp.concatenate(qd, axis=1), jnp.concatenate(kd, axis=1)], []

    post_params = [w["g_q_lat"], w["g_kv_lat"], g_kpe, g_dq, g_dk]
    proj, qln, kvn, kper, qd_r, kd_r = rowwise(
        "proj_fwd", proj_fn, [h] + tables, [w_in] + post_params,
        [(P_END, F32), (Q_LORA, MXU_DTYPE), (KV_LORA, MXU_DTYPE), (LANES, MXU_DTYPE)] + [(DIL_WIDTH, F32)] * 2, tm=256)
    w_q_b, w_kv_b = fetch("w_q_b", qln), fetch("w_kv_b", kvn)

    def mla_proj_fn(rows, params):
        (qlv, kvlv, kp, cm, sm), (wq_t, wkv, gq, gk) = rows, params
        qv, kvv = _dot(qlv, wq_t, "nt"), _dot(kvlv, wkv)
        value_lanes = _lane(kp.shape) >= NOPE
        qs, ks, vs = [], [], []
        for qc, kc in zip(_chunks(qv), _chunks(kvv), strict=True):
            qs.append(_rope(_grms(qc, gq, Q_GROUPS)[0], cm, sm, H_M))
            ks.append(_grms(kc, gk, K_GROUPS)[0] + kp)
            vs.append(jnp.where(value_lanes, kc, 0.0))
        return [qv, kvv] + [jnp.concatenate(t, axis=1) for t in (qs, ks, vs)], []

    q, kv, q_mla, k_mla, v_mla = rowwise(
        "mla_proj", mla_proj_fn, [qln, kvn, kper, cos_m, sin_m], [w_q_b, w_kv_b, g_q, g_k],
        [(HEADS * LANES, F32)] * 2 + [(HEADS * LANES, MXU_DTYPE)] * 3, tm=256)
    mla_scale = (NOPE + ROPE) ** -0.5
    o_cat, lse_mla = mla_fwd("mla_fwd", q_mla, k_mla, v_mla, mla_scale)
    passed = halfway(lse_mla)

    band = [band_fwd(f"band{dil}_fwd", qd_r, kd_r, proj, dil, dep=passed) for dil in DILATIONS]
    o_cat, lse_mix = combine_fwd("dil_combine", [b[0] for b in band], [b[1] for b in band], o_cat)
    w_o = fetch("w_o", o_cat)

    def mid_fn(rows, params):
        (ov, xv), (w_out, gate1, g, sc, sh) = rows, params
        mx = _dot(ov, w_out)
        x1 = xv + gate1 * mx
        y, _, _ = _rms(x1, g)
        return [mx, x1, y * (1.0 + sc) + sh], []

    mix, x1, h2 = rowwise("mix_fwd", mid_fn, [o_cat, x], [w_o, g1, w["g_ffn_norm"], sc2, sh2],
                          [(D_MODEL, F32), (D_MODEL, F32), (D_MODEL, MXU_DTYPE)], tm=256)
    w_up, w_conv, w_down = fetch("w_up", h2), fetch("w_conv", h2), fetch("w_down", h2)
    dn, up = ffn_fwd("ffn_fwd", h2, w_up, w_conv, b_conv, w_down)

    def final_fn(rows, params):
        (x1v, dnv, tgt), (gate2,) = rows, params
        r = x1v + gate2 * dnv - tgt
        dy = r * (1.0 / D_MODEL)
        loss = jnp.sum(_colsum(r * r), axis=-1, keepdims=True) * (0.5 / D_MODEL)
        return [dy, gate2 * dy], [loss, _colsum(dy * dnv)]

    dy, d_dn, loss, dg2 = rowwise("loss_head", final_fn, [x1, dn, target], [g2], [(D_MODEL, F32), (D_MODEL, MXU_DTYPE)],
                                  [1, D_MODEL])
    dh2, g_up, g_down, g_w_conv, g_b_conv = ffn_bwd("ffn_bwd", h2, up, w_up, w_conv, b_conv, d_dn, w_down)
    emit("w_down", g_down)
    emit("w_conv", g_w_conv)
    sent = emit("w_up", g_up)

    def mid_bwd_fn(rows, params):
        (dh2v, dyv, x1v, mx), (gate1, g, sc) = rows, params
        yn, n, rstd = _rms(x1v, g)
        dx_n, dg = _rms_bwd(dh2v * (1.0 + sc), n, rstd, g)
        dx1 = dyv + dx_n
        return [dx1, gate1 * dx1], [dg, _colsum(dh2v * yn), _colsum(dh2v), _colsum(dx1 * mx)]

    dx1, dmix, dg_ffn, dsc2, dsh2, dg1 = rowwise(
        "mid_bwd", mid_bwd_fn, [dh2, dy, x1, mix], [g1, w["g_ffn_norm"], sc2], [(D_MODEL, F32), (D_MODEL, MXU_DTYPE)],
        [D_MODEL] * 4, dep=sent)

    sent = emit("w_o", matmul("mix_wgrad", o_cat, dmix, "tn", tm=512, out_dtype=MXU_DTYPE))
    do_cat = matmul("mix_dgrad", dmix, w_o, "nt", tm=512, dep=sent)
    dband = None
    for dil, b in zip(DILATIONS, band):
        dband = band_bwd(f"band{dil}_bwd", qd_r, kd_r, proj, b[1], lse_mix, o_cat, do_cat, dil, before=dband)
    dq_mla, dkv_mla, dkper = mla_bwd("mla_bwd", q_mla, k_mla, v_mla, o_cat, do_cat, lse_mla, mla_scale)

    def mla_prep_bwd_fn(rows, params):
        (dqv, dkvv, qv, kvv, cm, sm), (gq, gk) = rows, params
        nope_lanes = _lane(cm.shape) < NOPE
        dqs, dkvs, dgq, dgk = [], [], 0.0, 0.0
        for dqc, dkc, qc, kc in zip(_chunks(dqv), _chunks(dkvv), _chunks(qv), _chunks(kvv), strict=True):
            _, n, rstd = _grms(qc, gq, Q_GROUPS)
            dx, dg = _grms_bwd(_rope_bwd(dqc, cm, sm, H_M), n, rstd, gq, Q_GROUPS)
            dqs.append(dx)
            dgq = dgq + dg
            _, n, rstd = _grms(kc, gk, K_GROUPS)
            dx, dg = _grms_bwd(dkc, n, rstd, gk, K_GROUPS)
            dkvs.append(jnp.where(nope_lanes, dx, dkc))
            dgk = dgk + dg
        return [jnp.concatenate(dqs, axis=1), jnp.concatenate(dkvs, axis=1)], [dgq, dgk]

    dq, dkv, dg_q, dg_k = rowwise("mla_prep_bwd", mla_prep_bwd_fn, [dq_mla, dkv_mla, q, kv, cos_m, sin_m], [g_q, g_k],
                                  [(HEADS * LANES, MXU_DTYPE)] * 2, [LANES, LANES], tm=256)
    emit("w_q_b", matmul("q_wgrad", dq, qln, "tn", out_dtype=MXU_DTYPE))
    sent = emit("w_kv_b", matmul("kv_wgrad", kvn, dkv, "tn", out_dtype=MXU_DTYPE))

    def pre_bwd_fn(rows, params):
        dqv, dkvv, dkp, dqd_, dkd_, dvd_, pv, cm, sm, cd, sd = rows
        wq_t, wkv, gq, gkv, gkp, gdq, gdk = params
        dql, dkvl = _dot(dqv, wq_t), _dot(dkvv, wkv, "nt")
        r_q = _norm_bwd(dql, pv[:, P_QLAT:P_KVLAT], gq)
        r_kv = _norm_bwd(dkvl, pv[:, P_KVLAT:P_KPE], gkv)
        _, n, rstd = _grms(pv[:, P_KPE:P_QD], gkp, KPE_GROUPS)
        r_kp = _grms_bwd(_rope_bwd(dkp, cm, sm, H_M), n, rstd, gkp, KPE_GROUPS)
        outs, dgs = [r_q[0], r_kv[0], r_kp[0]], []
        for dval, lo, g in ((dqd_, P_QD, gdq), (dkd_, P_KD, gdk)):
            dg_sum = 0.0
            for dc, xc in zip(_chunks(dval), _chunks(pv[:, lo:lo + DIL_WIDTH]), strict=True):
                _, n, rstd = _grms(xc, g, DIL_GROUPS)
                dx, dg = _grms_bwd(_rope_bwd(dc, cd, sd, H_D), n, rstd, g, DIL_GROUPS)
                outs.append(dx)
                dg_sum = dg_sum + dg
            dgs.append(dg_sum)
        return [jnp.concatenate(outs + [dvd_], axis=1)], [r_q[1], r_kv[1], r_kp[1]] + dgs

    dproj, dg_q_lat, dg_kv_lat, dg_kpe, dg_dq, dg_dk = rowwise(
        "proj_pre_bwd", pre_bwd_fn,
        [dq, dkv, dkper] + list(dband) + [proj] + tables, [w_q_b, w_kv_b] + post_params,
        [(P_END, MXU_DTYPE)], [Q_LORA, KV_LORA, LANES, LANES, LANES], tm=256, dep=sent)
    sent = emit("w_in", matmul("proj_wgrad", dproj, h, "tn", tn=512, out_dtype=MXU_DTYPE))

    def ln1_bwd_fn(rows, params):
        (dpv, dres, xv), (w_t, g, sc) = rows, params
        dhv = _dot(dpv, w_t)
        yn, n, rstd = _rms(xv, g)
        dx_n, dg = _rms_bwd(dhv * (1.0 + sc), n, rstd, g)
        return [dres + dx_n], [dg, _colsum(dhv * yn), _colsum(dhv)]

    grad_x, dg_mix, dsc1, dsh1 = rowwise("proj_dgrad", ln1_bwd_fn, [dproj, dx1, x], [w_in, w["g_mix_norm"], sc1],
                                         [(D_MODEL, F32)], [D_MODEL] * 3, tm=256, dep=sent)
    dmod = jnp.concatenate([dsh1, dsc1, dg1, dsh2, dsc2, dg2], axis=-1)
    small = {"loss": loss, "b_ada": dmod, "g_mix_norm": dg_mix, "g_q_lat": dg_q_lat, "g_kv_lat": dg_kv_lat,
             "g_mla_q_nope": dg_q[:, :NOPE], "g_mla_q_pe": dg_q[:, NOPE:NOPE + ROPE], "g_mla_k_nope": dg_k[:, :NOPE],
             "g_mla_k_pe": dg_kpe[:, KPE_LO:KPE_LO + ROPE], "g_dil_q": dg_dq[:, :DIL_DIM] + dg_dq[:, DIL_DIM:],
             "g_dil_k": dg_dk[:, :DIL_DIM] + dg_dk[:, DIL_DIM:], "g_ffn_norm": dg_ffn,
             "b_conv": g_b_conv}
    return grad_x, small


COL_SHARDED = ("w_kv_b", "w_conv")
ROW_SHARDED = ("w_o", "w_down") + TRANSPOSED
ADAM_TILE = {"w_ada": 256, "w_up": 176, "w_down": 176}
GATHER_GROUPS = (("w_in",), ("w_q_b", "w_kv_b"), ("w_o",), ("w_up", "w_conv", "w_down"))
START_STAGES = ((0, 1), (2, 3))
FORWARD_STAGES = ((0, 1), (2,), (3,))
FORWARD_WITH = {"w_o": 2}
SCATTER_GROUPS = (("w_down", "w_conv", "w_up"), ("w_o",), ("w_q_b", "w_kv_b"), ("w_in",))
OUT_WEIGHTS = ("w_ada", "b_ada", "g_mix_norm", "w_in", "g_q_lat", "w_q_b", "g_kv_lat", "w_kv_b", "g_mla_q_nope", "g_mla_q_pe",
               "g_mla_k_nope", "g_mla_k_pe", "g_dil_q", "g_dil_k", "w_o", "g_ffn_norm", "w_up", "w_conv", "b_conv", "w_down")


def kernel(x, c, positions, w_ada, b_ada, g_mix_norm, w_in, g_q_lat, w_q_b, g_kv_lat, w_kv_b, g_mla_q_nope, g_mla_q_pe, g_mla_k_nope, g_mla_k_pe, g_dil_q, g_dil_k, w_o, g_ffn_norm, w_up, w_conv, b_conv, w_down, loss_target, m_w_ada, m_b_ada, m_g_mix_norm, m_w_in, m_g_q_lat, m_w_q_b, m_g_kv_lat, m_w_kv_b, m_g_mla_q_nope, m_g_mla_q_pe, m_g_mla_k_nope, m_g_mla_k_pe, m_g_dil_q, m_g_dil_k, m_w_o, m_g_ffn_norm, m_w_up, m_w_conv, m_b_conv, m_w_down, v_w_ada, v_b_ada, v_g_mix_norm, v_w_in, v_g_q_lat, v_w_q_b, v_g_kv_lat, v_w_kv_b, v_g_mla_q_nope, v_g_mla_q_pe, v_g_mla_k_nope, v_g_mla_k_pe, v_g_dil_q, v_g_dil_k, v_w_o, v_g_ffn_norm, v_w_up, v_w_conv, v_b_conv, v_w_down):
    args = dict(locals())
    xi, yi, ci = _place()
    me = 4 * xi + 2 * yi + ci
    def local(prefix, n):
        a = args[prefix + n]
        if n in ROWS_APART:
            return jnp.transpose(a, (2, 0, 1) if n in TRANSPOSED else (1, 0, 2))
        return a[0].T if n in TRANSPOSED else a[0]

    def as_output(n, r):
        if n in ROWS_APART:
            return jnp.transpose(r, (1, 2, 0) if n in TRANSPOSED else (1, 0, 2))
        return (r.T if n in TRANSPOSED else r)[None]

    shard = {n: local("", n) for n in COL_SHARDED + ROW_SHARDED + ("w_ada",)}
    flat = lambda n, a: a.reshape(a.shape[0], a.shape[-1]) if n in ROWS_APART else a
    small_w = {n: args[n] for n in SMALL_PARAMS}

    payload = {n: flat(n, shard[n]) if n == "w_conv" else flat(n, shard[n]).astype(MXU_DTYPE) for n in COL_SHARDED + ROW_SHARDED}
    start_order = [[n for i in groups for n in GATHER_GROUPS[i]] for groups in START_STAGES]

    sc_all, mod_all = ada_modulation("ada_mod", c, shard["w_ada"], after=[payload[n] for n in start_order[0]])

    exchange_of = lambda i: [e for e, groups in enumerate(START_STAGES) if i in groups][0]
    start_stage = lambda e, after: exchange_start(f"gather_start{e}", [payload[n] for n in start_order[e]], gather=True,
                                                  after=after, tree=True)
    gathered = {0: start_stage(0, mod_all)}
    after_start = gathered[0][-1]
    full, forwarded = {}, set()

    def forward(stage, after):
        e = exchange_of(FORWARD_STAGES[stage][0])
        if stage not in forwarded:
            forwarded.add(stage)
            first = start_order[e].index(GATHER_GROUPS[FORWARD_STAGES[stage][0]][0])
            count = sum(len(GATHER_GROUPS[i]) for i in FORWARD_STAGES[stage])
            starts_next = e + 1 < len(START_STAGES) and e + 1 not in gathered
            ready = [payload[n] for n in start_order[e + 1]] if starts_next else []
            gathered[e] = exchange_forward(f"gather_forward{stage}", gathered[e], [after] + ready, first, count)
            if starts_next:
                gathered[e + 1] = start_stage(e + 1, gathered[e][-1])
        return gathered[e][-1]

    def fetch(name, after):
        if name not in full:
            (i, grp), = [(i, grp) for i, grp in enumerate(GATHER_GROUPS) if name in grp]
            (stage,) = [s for s, groups in enumerate(FORWARD_STAGES) if i in groups]
            forward(stage, after)
            if name in FORWARD_WITH:
                forward(FORWARD_WITH[name], after)
            e = exchange_of(i)
            behind = gathered[e + 1][-1] if e + 1 in gathered else after
            srcs, lands = exchange_wait(f"gather{i}_wait", gathered[e], True, behind, start_order[e].index(grp[0]), len(grp), tree=True)
            for n, stack in zip(grp, lands, strict=True):
                full[n] = to_kernel_layout(n, _gather_cols(stack) if n in COL_SHARDED else _gather_rows(stack))
        return full[name]

    mod_row = lax.dynamic_index_in_dim(mod_all, me, axis=1, keepdims=False).reshape(1, 6 * D_MODEL)
    (mod,) = rowwise("ada_bias", lambda rows, params: ([rows[0] + rows[1]], []), [mod_row, b_ada], [], [(6 * D_MODEL, F32)],
                     dep=after_start)

    pending, scatters = {}, {}

    def emit(name, grad):
        grad = from_kernel_layout(name, grad)
        pending[name] = _scatter_cols(grad) if name in COL_SHARDED else _scatter_rows(grad)
        for i, grp in enumerate(SCATTER_GROUPS):
            if name == grp[-1]:
                scatters[i] = exchange_start(f"scatter{i}_start", [pending[n] for n in grp], gather=False)
                return scatters[i][-1]
        return None

    pos = positions.reshape(SEQ, 1).astype(F32)
    grad_x, small = _local_step(x[0], pos, mod, loss_target[0], small_w, fetch, emit, halfway=lambda after: forward(1, after))

    small_sent = exchange_start("small_start", [_pack_small(small)], gather=True, after=grad_x)

    res, done = {}, small_sent[-1]
    for i, grp in enumerate(SCATTER_GROUPS):
        _, lands = exchange_wait(f"scatter{i}_wait", scatters[i], False, done)
        for n, land in zip(grp, lands, strict=True):
            res[n] = adamw(f"adamw_{n}", shard[n], [land], local("m_", n), local("v_", n), ADAM_TILE.get(n))
            done = res[n][0]
            res[n] = [as_output(n, r) for r in res[n]]
    _, (small_all,) = exchange_wait("small_wait", small_sent, True, done)
    loss, small_res = adamw_small("adamw_small", small_all, {n: (args[n], args["m_" + n], args["v_" + n]) for n in SMALL_PARAMS})
    row, _, n_mod = SMALL_AT["b_ada"]
    dmod_all = small_all[:, row:row + n_mod // SMALL_COLS, :].reshape(N_DEV, n_mod)
    dmod_mine = lax.dynamic_slice_in_dim(dmod_all, me * (6 * D_MODEL // N_DEV), 6 * D_MODEL // N_DEV, axis=1)
    g_w_ada = matmul("ada_wgrad", sc_all, dmod_mine, "tn")
    res["w_ada"] = [r[None] for r in adamw("adamw_w_ada", shard["w_ada"], [g_w_ada], m_w_ada[0], v_w_ada[0], ADAM_TILE["w_ada"])]

    def leaf(kind, n):
        return res[n][kind] if n in res else small_res[n][kind]

    return (loss.reshape(()), grad_x[None], *[leaf(k, n) for k in range(4) for n in OUT_WEIGHTS])
```

```python
import jax
import jax.numpy as jnp
from jax import lax
from jax.experimental import pallas as pl
from jax.experimental.pallas import tpu as pltpu

F32 = jnp.float32
MXU_DTYPE = jnp.bfloat16

N_DEV = 8
D_MODEL = 1024
SEQ = 2048
HEADS = 8
NOPE = 64
ROPE = 32
Q_LORA = 512
KV_LORA = 256
DIL_DIM = 64
DIL_WIDTH = HEADS * DIL_DIM
DILATIONS = (1, 4, 16)
SPAN = 128
D_FF = 2816
LANES = 128
SUBLANES = 8
ROPE_THETA = 10000.0
EPS = 1e-6
NEG_INF = -1e30
ADAM_LR, ADAM_B1, ADAM_B2, ADAM_EPS, ADAM_WD, ADAM_STEP = 0.001, 0.9, 0.999, 1e-08, 0.01, 10
VMEM_LIMIT = 56 * 1024 * 1024
MESH_ID = pl.DeviceIdType.MESH

P_QLAT, P_KVLAT, P_KPE, P_QD, P_KD, P_VD, P_END = 0, 512, 768, 896, 1408, 1920, 2432
KPE_LO = 64
MIX_IN = HEADS * LANES + DIL_WIDTH


def _params(**kw):
    return pltpu.CompilerParams(vmem_limit_bytes=VMEM_LIMIT, **kw)


def rowwise(name, fn, rows, params, out_rows, out_accs=(), tm=512, dep=None):
    deps = [] if dep is None else [dep]
    rows = [r if isinstance(r, tuple) else (r, r.shape[1], 0) for r in rows]
    R = rows[0][0].shape[0]
    tm = min(tm, R)
    steps = R // tm
    assert steps * tm == R
    in_specs = []
    for a, width, cb in rows:
        ri = a.shape[0]
        per = ri // tm
        assert per * tm == ri
        if ri == R:
            in_specs.append(pl.BlockSpec((tm, width), lambda i, cb=cb: (i, cb)))
        else:
            in_specs.append(pl.BlockSpec((tm, width), lambda i, per=per, cb=cb: (i % per, cb)))
    for p in params:
        in_specs.append(pl.BlockSpec(p.shape, lambda i: (0,) * p.ndim))
    in_specs += [pl.BlockSpec(memory_space=pl.ANY)] * len(deps)
    out_shape = [jax.ShapeDtypeStruct((R, d), dt) for d, dt in out_rows]
    out_specs = [pl.BlockSpec((tm, d), lambda i: (i, 0)) for d, _ in out_rows]
    out_shape += [jax.ShapeDtypeStruct((1, n), F32) for n in out_accs]
    out_specs += [pl.BlockSpec((1, n), lambda i: (0, 0)) for n in out_accs]
    nr, npar, no, na = len(rows), len(params), len(out_rows), len(out_accs)

    def body(*refs):
        rvals = [r[...] for r in refs[:nr]]
        pvals = [r[...] for r in refs[nr:nr + npar]]
        outs, accs = fn(rvals, pvals)
        first_out = nr + npar + len(deps)
        for ref, v in zip(refs[first_out:first_out + no], outs, strict=True):
            ref[...] = v.astype(ref.dtype)
        if na:
            acc_refs = refs[first_out + no:]
            i = pl.program_id(0)

            @pl.when(i == 0)
            def _():
                for ref, v in zip(acc_refs, accs, strict=True):
                    ref[...] = v

            @pl.when(i > 0)
            def _():
                for ref, v in zip(acc_refs, accs, strict=True):
                    ref[...] += v

    res = pl.pallas_call(body, name=name, grid=(steps,), in_specs=in_specs, out_specs=out_specs,
                         out_shape=out_shape, compiler_params=_params())(*[r[0] for r in rows], *params, *deps)
    return list(res)


_DIMS = {"nn": ((1,), (0,)), "nt": ((1,), (1,)), "tn": ((0,), (0,))}


def _dot(a, b, mode="nn"):
    return lax.dot_general(a.astype(MXU_DTYPE), b.astype(MXU_DTYPE), (_DIMS[mode], ((), ())),
                           preferred_element_type=F32)


def matmul(name, a, b, mode, tm=None, tn=None, tk=None, out_dtype=F32, dep=None):
    if mode == "tn":
        K, M = a.shape
    else:
        M, K = a.shape
    N = b.shape[0] if mode == "nt" else b.shape[1]
    tm, tn, tk = tm or M, tn or N, tk or K
    nm, nn, nk = M // tm, N // tn, K // tk
    assert nm * tm == M and nn * tn == N and nk * tk == K
    a_spec = pl.BlockSpec((tk, tm), lambda i, j, k: (k, i)) if mode == "tn" else pl.BlockSpec((tm, tk), lambda i, j, k: (i, k))
    b_spec = pl.BlockSpec((tn, tk), lambda i, j, k: (j, k)) if mode == "nt" else pl.BlockSpec((tk, tn), lambda i, j, k: (k, j))
    deps = [] if dep is None else [dep]

    def body(a_ref, b_ref, *rest):
        o_ref, scratch = rest[len(deps)], rest[len(deps) + 1:]
        p = _dot(a_ref[...], b_ref[...], mode)
        if nk == 1:
            o_ref[...] = p.astype(o_ref.dtype)
        else:
            acc = scratch[0]
            k = pl.program_id(2)

            @pl.when(k == 0)
            def _():
                acc[...] = p

            @pl.when(k > 0)
            def _():
                acc[...] += p

            @pl.when(k == nk - 1)
            def _():
                o_ref[...] = acc[...].astype(o_ref.dtype)

    return pl.pallas_call(
        body, name=name, grid=(nm, nn, nk), in_specs=[a_spec, b_spec] + [pl.BlockSpec(memory_space=pl.ANY)] * len(deps),
        out_specs=pl.BlockSpec((tm, tn), lambda i, j, k: (i, j)),
        out_shape=jax.ShapeDtypeStruct((M, N), out_dtype),
        scratch_shapes=[pltpu.VMEM((tm, tn), F32)] if nk > 1 else [],
        compiler_params=_params())(a, b, *deps)


def _rms(x, g):
    rstd = lax.rsqrt(jnp.mean(x * x, axis=-1, keepdims=True) + EPS)
    n = x * rstd
    return n * g, n, rstd


def _rms_bwd(dy, n, rstd, g):
    dg = jnp.sum(dy * n, axis=0, keepdims=True)
    dn = dy * g
    dx = rstd * (dn - n * jnp.mean(dn * n, axis=-1, keepdims=True))
    return dx, dg


def _norm_bwd(dy, x, g):
    _, n, rstd = _rms(x, g)
    return _rms_bwd(dy, n, rstd, g)


def _colsum(v):
    return jnp.sum(v, axis=0, keepdims=True)


def _silu(x):
    return x * (1.0 / (1.0 + jnp.exp(-x)))


def _lane(shape):
    return lax.broadcasted_iota(jnp.int32, shape, 1)


def _group_mean(v, groups):
    i = lax.broadcasted_iota(jnp.int32, (LANES, LANES), 0)
    j = lax.broadcasted_iota(jnp.int32, (LANES, LANES), 1)
    g = jnp.zeros((LANES, LANES), F32)
    for lo, hi in groups:
        g = jnp.where((i >= lo) & (i < hi) & (j >= lo) & (j < hi), 1.0 / (hi - lo), g)
    head = v.astype(MXU_DTYPE)
    return _dot(head, g) + _dot(v - head.astype(F32), g)


def _in_groups(shape, groups):
    lane = _lane(shape)
    m = jnp.zeros(shape, jnp.bool_)
    for lo, hi in groups:
        m = m | ((lane >= lo) & (lane < hi))
    return m


def _grms(x, g, groups):
    rstd = lax.rsqrt(_group_mean(x * x, groups) + EPS)
    n = jnp.where(_in_groups(x.shape, groups), x * rstd, 0.0)
    return n * g, n, rstd


def _grms_bwd(dy, n, rstd, g, groups):
    dn = dy * g
    return rstd * (dn - n * _group_mean(dn * n, groups)), _colsum(dy * n)


def _rot(x, half, transpose=False):
    first = (_lane(x.shape) % (2 * half)) < half
    up = pltpu.roll(x, LANES - half, axis=1)
    down = pltpu.roll(x, half, axis=1)
    return jnp.where(first, up, -down) if transpose else jnp.where(first, -up, down)


def _rope(x, cos, sin, half):
    return x * cos + _rot(x, half) * sin


def _rope_bwd(dy, cos, sin, half):
    return dy * cos + _rot(dy * sin, half, transpose=True)


def _chunks(x):
    return [x[:, i:i + LANES] for i in range(0, x.shape[1], LANES)]


Q_GROUPS = ((0, NOPE), (NOPE, NOPE + ROPE))
K_GROUPS = ((0, NOPE),)
KPE_GROUPS = ((KPE_LO, KPE_LO + ROPE),)
DIL_GROUPS = ((0, DIL_DIM), (DIL_DIM, 2 * DIL_DIM))


def _col(width, rows=SEQ):
    return pl.BlockSpec((rows, width), lambda h: (0, h))


def _causal_tail(s, tq, fill):
    diag = s[:, s.shape[1] - tq:]
    keep = lax.broadcasted_iota(jnp.int32, diag.shape, 1) <= lax.broadcasted_iota(jnp.int32, diag.shape, 0)
    diag = jnp.where(keep, diag, fill)
    return diag if s.shape[1] == tq else jnp.concatenate([s[:, :s.shape[1] - tq], diag], axis=1)


def mla_fwd(name, q, k, v, scale, tq=256):
    S = q.shape[0]

    def body(q_ref, k_ref, v_ref, o_ref, lse_ref):
        nb = S // tq
        blk = lambda i: slice(i * tq, (i + 1) * tq)

        def scores(i):
            return _dot(q_ref[blk(i), :], k_ref[:(i + 1) * tq, :], "nt")

        def softmax(i, s):
            s = _causal_tail(s * scale, tq, NEG_INF)
            m = jnp.max(s, axis=-1, keepdims=True)
            e = jnp.exp(s - m)
            l = jnp.sum(e, axis=-1, keepdims=True)
            lse_ref[0, blk(i), :] = m + jnp.log(l)
            return (e * (1.0 / l)).astype(MXU_DTYPE)

        def weighted(i, p):
            o_ref[blk(i), :] = _dot(p, v_ref[:(i + 1) * tq, :])

        s, p_prev = scores(0), None
        for i in range(nb):
            s_next = scores(i + 1) if i + 1 < nb else None
            if p_prev is not None:
                weighted(i - 1, p_prev)
            p_prev, s = softmax(i, s), s_next
        weighted(nb - 1, p_prev)

    return pl.pallas_call(
        body, name=name, grid=(HEADS,), in_specs=[_col(LANES)] * 3,
        out_specs=[_col(LANES), pl.BlockSpec((1, S, 1), lambda h: (h, 0, 0))],
        out_shape=[jax.ShapeDtypeStruct((S, MIX_IN), F32), jax.ShapeDtypeStruct((HEADS, S, 1), F32)],
        compiler_params=_params())(q, k, v)


def mla_bwd(name, q, k, v, o, do, lse, scale, tq=256):
    S = q.shape[0]

    def body(q_ref, k_ref, v_ref, o_ref, do_ref, lse_ref, dq_ref, dkv_ref, dkpe_ref, dk_acc, dv_acc):
        dk_acc[...] = jnp.zeros_like(dk_acc)
        dv_acc[...] = jnp.zeros_like(dv_acc)
        for i in range(S // tq):
            kext = (i + 1) * tq
            blk = slice(i * tq, kext)
            qi, kk, vv = q_ref[blk, :], k_ref[:kext, :], v_ref[:kext, :]
            doi = do_ref[blk, :]
            s = _causal_tail(_dot(qi, kk, "nt") * scale, tq, NEG_INF)
            p = jnp.exp(s - lse_ref[0, blk, :])
            dp = _dot(doi, vv, "nt")
            delta = jnp.sum(doi * o_ref[blk, :], axis=-1, keepdims=True)
            ds = p * (dp - delta) * scale
            dq_ref[blk, :] = _dot(ds, kk)
            dk_acc[:kext, :] += _dot(ds, qi, "tn")
            dv_acc[:kext, :] += _dot(p, doi, "tn")
        dk = dk_acc[...]
        lane = _lane(dk.shape)
        dkv_ref[...] = jnp.where(lane < NOPE, dk, 0.0) + dv_acc[...]
        dkpe = jnp.where((lane >= KPE_LO) & (lane < KPE_LO + ROPE), dk, 0.0)
        h = pl.program_id(0)

        @pl.when(h == 0)
        def _():
            dkpe_ref[...] = dkpe

        @pl.when(h > 0)
        def _():
            dkpe_ref[...] += dkpe

    return pl.pallas_call(
        body, name=name, grid=(HEADS,),
        in_specs=[_col(LANES)] * 5 + [pl.BlockSpec((1, S, 1), lambda h: (h, 0, 0))],
        out_specs=[_col(LANES), _col(LANES), pl.BlockSpec((S, LANES), lambda h: (0, 0))],
        out_shape=[jax.ShapeDtypeStruct((S, HEADS * LANES), F32), jax.ShapeDtypeStruct((S, HEADS * LANES), F32),
                   jax.ShapeDtypeStruct((S, LANES), F32)],
        scratch_shapes=[pltpu.VMEM((S, LANES), F32), pltpu.VMEM((S, LANES), F32)],
        compiler_params=_params())(q, k, v, o, do, lse)


BAND_TQ = SPAN


def _band_blocks(L, tq):
    return [(i * tq, (i + 1) * tq, max(0, i * tq - SPAN)) for i in range(L // tq)]


def _class_rows(r, dil, lo, hi):
    return pl.ds(r + dil * lo, hi - lo, stride=dil) if dil > 1 else pl.ds(lo, hi - lo)


def _stack_heads(t, lo):
    zero = jnp.zeros_like(t)
    return jnp.concatenate([jnp.where(lo, t, zero), jnp.where(lo, zero, t)], axis=0)


def _band_mask2(q0, q1, k0):
    n = q1 - q0
    shape = (2 * n, q1 - k0)
    i = lax.broadcasted_iota(jnp.int32, shape, 0)
    dist = (jnp.where(i >= n, i - n, i) + q0) - (lax.broadcasted_iota(jnp.int32, shape, 1) + k0)
    return (dist >= 0) & (dist <= SPAN)


def _pair_col(col0=0):
    return pl.BlockSpec((SEQ, LANES), lambda j: (0, col0 // LANES + j))


def band_fwd(name, q, k, v, dil, dep=None):
    S = q.shape[0]
    L = S // dil
    tq = BAND_TQ
    scale = DIL_DIM ** -0.5
    deps = [] if dep is None else [dep]

    def body(q_ref, k_ref, v_ref, *rest):
        o_ref, lse_ref = rest[len(deps):]
        items = [(r, blk) for r in range(dil) for blk in _band_blocks(L, tq)]
        lo = _lane((tq, LANES)) < DIL_DIM

        def scores(item):
            r, (q0, q1, k0) = item
            qb = q_ref[_class_rows(r, dil, q0, q1), :].astype(MXU_DTYPE)
            return _dot(_stack_heads(qb, lo), k_ref[_class_rows(r, dil, k0, q1), :], "nt")

        def softmax(item, s):
            _, (q0, q1, k0) = item
            s = jnp.where(_band_mask2(q0, q1, k0), s * scale, NEG_INF)
            mx = jnp.max(s, axis=-1, keepdims=True)
            e = jnp.exp(s - mx)
            l = jnp.sum(e, axis=-1, keepdims=True)
            return (e * (1.0 / l)).astype(MXU_DTYPE), mx + jnp.log(l)

        def weighted(item, p, lse):
            r, (q0, q1, k0) = item
            pv = _dot(p, v_ref[_class_rows(r, dil, k0, q1), :])
            o_ref[_class_rows(r, dil, q0, q1), :] = jnp.where(lo, pv[:tq], pv[tq:])
            lse_ref[_class_rows(r, dil, q0, q1), :] = jnp.where(lo, lse[:tq], lse[tq:])

        s, prev = scores(items[0]), None
        for i, item in enumerate(items):
            s_next = scores(items[i + 1]) if i + 1 < len(items) else None
            if prev is not None:
                weighted(items[i - 1], *prev)
            prev, s = softmax(item, s), s_next
        weighted(items[-1], *prev)

    return pl.pallas_call(
        body, name=name, grid=(DIL_WIDTH // LANES,),
        in_specs=[_pair_col()] * 2 + [_pair_col(P_VD)] + [pl.BlockSpec(memory_space=pl.ANY)] * len(deps), out_specs=[_pair_col()] * 2,
        out_shape=[jax.ShapeDtypeStruct((S, DIL_WIDTH), F32)] * 2, compiler_params=_params())(q, k, v, *deps)


def band_bwd(name, q, k, v, lse, lse_mix, o_cat, do_cat, dil, before=None):
    S = q.shape[0]
    L = S // dil
    tq = BAND_TQ
    scale = DIL_DIM ** -0.5
    before = list(before or [])

    def body(q_ref, k_ref, v_ref, lse_ref, mix_ref, o_ref, do_ref, *rest):
        dq_ref, dk_ref, dv_ref = rest[len(before):]
        if before:
            dq0_ref, dk0_ref, dv0_ref = rest[:3]
            dk_ref[...] = dk0_ref[...]
            dv_ref[...] = dv0_ref[...]
        else:
            dk_ref[...] = jnp.zeros_like(dk_ref)
            dv_ref[...] = jnp.zeros_like(dv_ref)
        items = [(r, blk) for r in range(dil) for blk in _band_blocks(L, tq)]
        lo = _lane((tq, LANES)) < DIL_DIM
        per_head = lambda t: jnp.concatenate([t[:, 0:1], t[:, DIL_DIM:DIL_DIM + 1]], axis=0)

        def scores(item):
            r, (q0, q1, k0) = item
            qrows, krows = _class_rows(r, dil, q0, q1), _class_rows(r, dil, k0, q1)
            lse_p, dout = lse_ref[qrows, :], do_ref[qrows, :]
            w2 = per_head(jnp.exp(lse_p - mix_ref[qrows, :]))
            dd = dout * o_ref[qrows, :]
            big_d = jnp.concatenate([jnp.sum(jnp.where(lo, dd, 0.0), axis=-1, keepdims=True),
                                     jnp.sum(jnp.where(lo, 0.0, dd), axis=-1, keepdims=True)], axis=0)
            q2 = _stack_heads(q_ref[qrows, :].astype(MXU_DTYPE), lo)
            dom = (_stack_heads(dout, lo) * w2).astype(MXU_DTYPE)
            return (_dot(q2, k_ref[krows, :], "nt"), _dot(dom, v_ref[krows, :], "nt"), per_head(lse_p), w2 * big_d, q2, dom)

        def softmax_bwd(item, s, dp, lse2, wd2, q2, dom):
            _, (q0, q1, k0) = item
            p = jnp.where(_band_mask2(q0, q1, k0), jnp.exp(s * scale - lse2), 0.0)
            return p.astype(MXU_DTYPE), (p * (dp - wd2) * scale).astype(MXU_DTYPE), q2, dom

        def grads(item, p, ds, q2, dom):
            r, (q0, q1, k0) = item
            qrows, krows = _class_rows(r, dil, q0, q1), _class_rows(r, dil, k0, q1)
            dq2 = _dot(ds, k_ref[krows, :])
            dq = jnp.where(lo, dq2[:tq], dq2[tq:])
            dq_ref[qrows, :] = dq + dq0_ref[qrows, :] if before else dq
            dk_ref[krows, :] += _dot(ds, q2, "tn")
            dv_ref[krows, :] += _dot(p, dom, "tn")

        sc, prev = scores(items[0]), None
        for i, item in enumerate(items):
            sc_next = scores(items[i + 1]) if i + 1 < len(items) else None
            if prev is not None:
                grads(items[i - 1], *prev)
            prev, sc = softmax_bwd(item, *sc), sc_next
        grads(items[-1], *prev)

    cat = _pair_col(HEADS * LANES)
    return pl.pallas_call(
        body, name=name, grid=(DIL_WIDTH // LANES,),
        in_specs=[_pair_col()] * 2 + [_pair_col(P_VD)] + [_pair_col()] * 2 + [cat] * 2 + [_pair_col()] * len(before),
        out_specs=[_pair_col()] * 3, out_shape=[jax.ShapeDtypeStruct((S, DIL_WIDTH), F32)] * 3,
        compiler_params=_params())(q, k, v, lse, lse_mix, o_cat, do_cat, *before)


def combine_fwd(name, outs, lses, o_cat, tm=512):
    S = outs[0].shape[0]

    def body(o1, o2, o3, l1, l2, l3, cat_in, cat_out, mix_ref):
        ls = [l1[...], l2[...], l3[...]]
        m = jnp.maximum(jnp.maximum(ls[0], ls[1]), ls[2])
        e = [jnp.exp(l - m) for l in ls]
        den = e[0] + e[1] + e[2]
        cat_out[...] = (e[0] / den) * o1[...] + (e[1] / den) * o2[...] + (e[2] / den) * o3[...]
        mix_ref[...] = m + jnp.log(den)

    row = pl.BlockSpec((tm, DIL_WIDTH), lambda i: (i, 0))
    return pl.pallas_call(
        body, name=name, grid=(S // tm,), in_specs=[row] * 6 + [pl.BlockSpec(memory_space=pl.ANY)],
        out_specs=[pl.BlockSpec((tm, DIL_WIDTH), lambda i: (i, HEADS * LANES // DIL_WIDTH)), row],
        out_shape=[jax.ShapeDtypeStruct(o_cat.shape, F32), jax.ShapeDtypeStruct((S, DIL_WIDTH), F32)],
        input_output_aliases={6: 0}, compiler_params=_params())(*outs, *lses, o_cat)


FFN_FWD_ROWS = 512
FFN_BWD_ROWS = 256
CONV_PAD = SUBLANES


def _window(x, k):
    groups = x.reshape(-1, SUBLANES, x.shape[1])
    turned = pltpu.roll(groups, SUBLANES - k, axis=1)
    stays = lax.broadcasted_iota(jnp.int32, (groups.shape[0] - 1,) + groups.shape[1:], 1) < SUBLANES - k
    return jnp.where(stays, turned[:-1], turned[1:]).reshape(-1, x.shape[1])


def _earlier(ref, r0, rows, n):
    if r0 == 0:
        x = jnp.concatenate([jnp.zeros((SUBLANES, ref.shape[1]), F32), ref[:rows, :]], axis=0)
    else:
        x = ref[r0 - SUBLANES:r0 + rows, :]
    return _window(x, SUBLANES - n)


CONV_TC = 256
CONV_NB = D_FF // CONV_TC


def _half_specs(rows, rows_axis=False):
    if rows_axis:
        return [pl.BlockSpec((rows, D_MODEL), lambda j: (j, 0)), pl.BlockSpec((rows, D_MODEL), lambda j: (j + CONV_NB, 0))]
    return [pl.BlockSpec((rows, CONV_TC), lambda j: (0, j)), pl.BlockSpec((rows, CONV_TC), lambda j: (0, j + CONV_NB))]


def _whole(a):
    return pl.BlockSpec(a.shape, lambda j: (0,) * a.ndim)


def _up_pair(h, ug_ref, uv_ref):
    return jnp.concatenate([_dot(h, ug_ref[...], "nt"), _dot(h, uv_ref[...], "nt")], axis=1)


def _conv_taps(up_ref, r0, rows, w, b):
    uin, u1, u2 = up_ref[r0:r0 + rows, :], _earlier(up_ref, r0, rows, 1), _earlier(up_ref, r0, rows, 2)
    return uin, u1, u2, w[2:3, :] * uin + w[1:2, :] * u1 + w[0:1, :] * u2 + b


def ffn_fwd(name, h, w_up_t, w_conv, b_conv, w_down):
    S = h.shape[0]

    def body(h_ref, ug_ref, uv_ref, wg_ref, wv_ref, bg_ref, bv_ref, wd_ref, dn_ref, up_ref):
        @pl.when(pl.program_id(0) == 0)
        def _():
            dn_ref[...] = jnp.zeros_like(dn_ref)

        w = jnp.concatenate([wg_ref[...], wv_ref[...]], axis=1)
        b = jnp.concatenate([bg_ref[...], bv_ref[...]], axis=1)
        rows = FFN_FWD_ROWS
        starts = list(range(0, S, rows))

        def project(r0):
            up_ref[r0:r0 + rows, :] = _up_pair(h_ref[r0:r0 + rows, :], ug_ref, uv_ref)

        def gate(r0):
            u = _conv_taps(up_ref, r0, rows, w, b)[3]
            return (_silu(u[:, :CONV_TC]) * u[:, CONV_TC:]).astype(MXU_DTYPE)

        def project_down(r0, act):
            dn_ref[r0:r0 + rows, :] += _dot(act, wd_ref[...])

        project(starts[0])
        act_prev = None
        for i, r0 in enumerate(starts):
            if i + 1 < len(starts):
                project(starts[i + 1])
            if act_prev is not None:
                project_down(starts[i - 1], act_prev)
            act_prev = gate(r0)
        project_down(starts[-1], act_prev)

    return pl.pallas_call(
        body, name=name, grid=(CONV_NB,),
        in_specs=[_whole(h)] + _half_specs(CONV_TC, rows_axis=True) + _half_specs(3) + _half_specs(1)
        + [pl.BlockSpec((CONV_TC, w_down.shape[1]), lambda j: (j, 0))],
        out_specs=[pl.BlockSpec((S, w_down.shape[1]), lambda j: (0, 0)), pl.BlockSpec((S, 2 * CONV_TC), lambda j: (0, j))],
        out_shape=[jax.ShapeDtypeStruct((S, w_down.shape[1]), F32), jax.ShapeDtypeStruct((S, 2 * D_FF), F32)],
        compiler_params=_params())(h, w_up_t, w_up_t, w_conv, w_conv, b_conv, b_conv, w_down)


def ffn_bwd(name, h, up, w_up_t, w_conv, b_conv, d_dn, w_down):
    S, D = h.shape

    def body(h_ref, up_ref, ug_ref, uv_ref, wg_ref, wv_ref, bg_ref, bv_ref, dd_ref, wd_ref,
             dh_ref, gup_ref, gd_ref, dwg_ref, dwv_ref, dbg_ref, dbv_ref, du_ref, dup_ref, act_ref):
        @pl.when(pl.program_id(0) == 0)
        def _():
            dh_ref[...] = jnp.zeros_like(dh_ref)

        w = jnp.concatenate([wg_ref[...], wv_ref[...]], axis=1)
        b = jnp.concatenate([bg_ref[...], bv_ref[...]], axis=1)
        w_pair = jnp.concatenate([ug_ref[...], uv_ref[...]], axis=0)
        rows = FFN_BWD_ROWS
        starts = list(range(0, S, rows))
        du_ref[S:S + CONV_PAD, :] = jnp.zeros((CONV_PAD, 2 * CONV_TC), F32)

        def project(r0):
            return _dot(dd_ref[r0:r0 + rows, :], wd_ref[...], "nt")

        def through_conv(r0, da):
            uin, u1, u2, u = _conv_taps(up_ref, r0, rows, w, b)
            gate, val = u[:, :CONV_TC], u[:, CONV_TC:]
            sig = 1.0 / (1.0 + jnp.exp(-gate))
            du = jnp.concatenate([da * val * (sig * (1.0 + gate * (1.0 - sig))), da * (gate * sig)], axis=1)
            du_ref[r0:r0 + rows, :] = du
            act_ref[r0:r0 + rows, :] = (gate * sig * val).astype(MXU_DTYPE)
            dw = jnp.concatenate([_colsum(du * u2), _colsum(du * u1), _colsum(du * uin)], axis=0)
            return dw, _colsum(du)

        def back_up(r0):
            du = du_ref[r0:r0 + rows + CONV_PAD, :]
            dup = (w[2:3, :] * du[:rows] + w[1:2, :] * _window(du, 1) + w[0:1, :] * _window(du, 2)).astype(MXU_DTYPE)
            dup_ref[r0:r0 + rows, :] = dup
            dh_ref[r0:r0 + rows, :] += _dot(dup, w_pair)

        dw, db = 0.0, 0.0
        da = project(starts[0])
        for i, r0 in enumerate(starts):
            da_next = project(starts[i + 1]) if i + 1 < len(starts) else None
            dw_c, db_c = through_conv(r0, da)
            if i > 0:
                back_up(starts[i - 1])
            dw, db, da = dw + dw_c, db + db_c, da_next
        back_up(starts[-1])
        g_up, g_dn = _dot(dup_ref[...], h_ref[...], "tn"), _dot(act_ref[...], dd_ref[...], "tn")
        gup_ref[0], gup_ref[1] = g_up[:CONV_TC].astype(gup_ref.dtype), g_up[CONV_TC:].astype(gup_ref.dtype)
        gd_ref[...] = g_dn.astype(gd_ref.dtype)
        dwg_ref[...], dwv_ref[...] = dw[:, :CONV_TC], dw[:, CONV_TC:]
        dbg_ref[...], dbv_ref[...] = db[:, :CONV_TC], db[:, CONV_TC:]

    half = lambda rows: pl.BlockSpec((rows, CONV_TC), lambda j: (0, j))
    rows_blk = pl.BlockSpec((CONV_TC, D), lambda j: (j, 0))
    dh, gup, gd, dwg, dwv, dbg, dbv = pl.pallas_call(
        body, name=name, grid=(CONV_NB,),
        in_specs=[_whole(h), pl.BlockSpec((S, 2 * CONV_TC), lambda j: (0, j))] + _half_specs(CONV_TC, rows_axis=True) + _half_specs(3)
        + _half_specs(1) + [_whole(d_dn), rows_blk],
        out_specs=[pl.BlockSpec((S, D), lambda j: (0, 0)), pl.BlockSpec((2, CONV_TC, D), lambda j: (0, j, 0)), rows_blk,
                   half(3), half(3), half(1), half(1)],
        out_shape=[jax.ShapeDtypeStruct((S, D), F32), jax.ShapeDtypeStruct((2, D_FF, D), MXU_DTYPE),
                   jax.ShapeDtypeStruct((D_FF, D), MXU_DTYPE)]
        + [jax.ShapeDtypeStruct((3, D_FF), F32)] * 2 + [jax.ShapeDtypeStruct((1, D_FF), F32)] * 2,
        scratch_shapes=[pltpu.VMEM((S + CONV_PAD, 2 * CONV_TC), F32), pltpu.VMEM((S, 2 * CONV_TC), MXU_DTYPE),
                        pltpu.VMEM((S, CONV_TC), MXU_DTYPE)],
        compiler_params=_params())(h, up, w_up_t, w_up_t, w_conv, w_conv, b_conv, b_conv, d_dn, w_down)
    return dh, gup.reshape(2 * D_FF, D), gd, jnp.concatenate([dwg, dwv], axis=1), jnp.concatenate([dbg, dbv], axis=1)


def adamw(name, w, parts, m, v, tr=None):
    apart = w.ndim == 3
    R, C = w.shape[0], w.shape[-1]
    tr = tr or R
    assert R % tr == 0
    c1 = 1.0 - ADAM_B1 ** ADAM_STEP
    c2 = 1.0 - ADAM_B2 ** ADAM_STEP
    np_ = len(parts)

    def body(*refs):
        w_ref, m_ref, v_ref = refs[0], refs[1 + np_], refs[2 + np_]
        go_ref, d_ref, mo_ref, vo_ref = refs[3 + np_:]
        terms = []
        for part, ref in zip(parts, refs[1:1 + np_], strict=True):
            terms += [ref[...]] if part.ndim == 2 else [ref[p] for p in range(part.shape[0])]
        g = terms[0].astype(F32)
        for term in terms[1:]:
            g = g + term.astype(F32)
        m2 = ADAM_B1 * m_ref[...] + (1.0 - ADAM_B1) * g
        v2 = ADAM_B2 * v_ref[...] + (1.0 - ADAM_B2) * (g * g)
        go_ref[...] = g
        mo_ref[...] = m2
        vo_ref[...] = v2
        d_ref[...] = -ADAM_LR * ((m2 / c1) / (jnp.sqrt(v2 / c2) + ADAM_EPS) + ADAM_WD * w_ref[...])

    blk = pl.BlockSpec((tr, C), lambda i: (i, 0))
    own = pl.BlockSpec((tr, None, C), lambda i: (i, 0, 0)) if apart else blk
    part_specs = [blk if p.ndim == 2 else pl.BlockSpec((p.shape[0], tr, C), lambda i: (0, i, 0)) for p in parts]
    return pl.pallas_call(
        body, name=name, grid=(R // tr,),
        in_specs=[own] + part_specs + [own, own], out_specs=[own] * 4,
        out_shape=[jax.ShapeDtypeStruct(w.shape, F32)] * 4, compiler_params=_params())(w, *parts, m, v)


def _place():
    return lax.axis_index("x"), lax.axis_index("y"), lax.axis_index("c")


def ada_modulation(name, c, w_ada, after=()):
    n_mod = w_ada.shape[1]

    def exchange(src_ref, dst_ref, send_sems, recv_sems):
        x, y, c_ = _place()
        me = 4 * x + 2 * y + c_
        copies = []
        for k in range(1, N_DEV):
            px, py, pc = x ^ (k >> 2), y ^ ((k >> 1) & 1), c_ ^ (k & 1)
            copies.append(pltpu.make_async_remote_copy(
                src_ref=src_ref, dst_ref=dst_ref.at[me], send_sem=send_sems.at[k - 1], recv_sem=recv_sems.at[k - 1],
                device_id=(px, py, pc), device_id_type=MESH_ID))
        for cp in copies:
            cp.start()
        for cp in copies:
            cp.wait_recv()
        for cp in copies:
            cp.wait_send()
        return me

    def body(c_ref, w_ref, *refs):
        sc_ref, mod_ref, c_all, send_c, recv_c, send_m, recv_m = refs[len(after):]
        me = exchange(c_ref, c_all, send_c, recv_c)
        c_all[me] = c_ref[...]
        sc = _silu(jnp.concatenate([c_all[p] for p in range(N_DEV)], axis=0))
        sc_ref[...] = sc.astype(sc_ref.dtype)
        mod_ref[me] = _dot(sc, w_ref[...])
        exchange(mod_ref.at[me], mod_ref, send_m, recv_m)

    vmem = pl.BlockSpec(memory_space=pltpu.VMEM)
    return pl.pallas_call(
        body, name=name, in_specs=[vmem, vmem] + [pl.BlockSpec(memory_space=pl.ANY)] * len(after), out_specs=[vmem, vmem],
        out_shape=[jax.ShapeDtypeStruct((N_DEV, c.shape[1]), MXU_DTYPE), jax.ShapeDtypeStruct((N_DEV, N_DEV, n_mod), F32)],
        scratch_shapes=[pltpu.VMEM((N_DEV, 1, c.shape[1]), F32)] + [pltpu.SemaphoreType.DMA((N_DEV - 1,))] * 4,
        compiler_params=pltpu.CompilerParams(has_side_effects=True, vmem_limit_bytes=VMEM_LIMIT))(c, w_ada, *after)


HBM_SPEC = pl.BlockSpec(memory_space=pltpu.HBM)
SEM_SPEC = pl.BlockSpec(memory_space=pltpu.SEMAPHORE)
DATAFLOW = pltpu.SideEffectType.DATAFLOW_SIDE_EFFECTING


def _exchange_copies(srcs, lands, send_sems, recv_sems, gather, first=0):
    x, y, c = _place()
    me = 4 * x + 2 * y + c
    out = []
    for t, (src, land) in enumerate(zip(srcs, lands, strict=True)):
        for k in range(1, N_DEV):
            px, py, pc = x ^ (k >> 2), y ^ ((k >> 1) & 1), c ^ (k & 1)
            sem = 7 * (first + t) + k - 1
            out.append((k, pltpu.make_async_remote_copy(
                src_ref=src if gather else src.at[4 * px + 2 * py + pc],
                dst_ref=land.at[me] if gather else land.at[k - 1],
                send_sem=send_sems.at[sem], recv_sem=recv_sems.at[sem],
                device_id=(px, py, pc), device_id_type=MESH_ID)))
    return out


def _own_copies(srcs, lands, send_sems, gather, first=0):
    x, y, c = _place()
    me = 4 * x + 2 * y + c
    total = send_sems.shape[0] // N_DEV
    return [pltpu.make_async_copy(src if gather else src.at[me], land.at[me] if gather else land.at[N_DEV - 1],
                                  send_sems.at[7 * total + first + t])
            for t, (src, land) in enumerate(zip(srcs, lands, strict=True))]


TREE_DIRECT = (1, 2, 4, 6)
TREE_FORWARDED = (3, 5, 7)


def exchange_start(name, arrs, gather, after=None, tree=False):
    n = len(arrs)
    lands = [lax.empty((N_DEV,) + (a.shape if gather else a.shape[1:]), a.dtype) for a in arrs]
    deps = [] if after is None else [after]

    def body(*refs):
        srcs, land_refs = refs[:n], refs[n:2 * n]
        send_sems, recv_sems = refs[2 * n + len(deps)], refs[2 * n + len(deps) + 1]
        token = refs[-1]
        for k, cp in _exchange_copies(srcs, land_refs, send_sems, recv_sems, gather):
            if not tree or k in TREE_DIRECT:
                cp.start()
        for cp in _own_copies(srcs, land_refs, send_sems, gather):
            cp.start()
        token[...] = jnp.zeros_like(token)

    hbm = lambda a: pltpu.HBM(a.shape, a.dtype)
    res = pl.pallas_call(
        body, name=name,
        out_shape=(pltpu.SemaphoreType.DMA((N_DEV * n,)), pltpu.SemaphoreType.DMA((7 * n,)), *[hbm(a) for a in arrs],
                   *[hbm(l) for l in lands], jax.ShapeDtypeStruct((8, 128), F32)),
        in_specs=[HBM_SPEC] * (2 * n) + [pl.BlockSpec(memory_space=pl.ANY)] * len(deps),
        out_specs=(SEM_SPEC, SEM_SPEC, *[HBM_SPEC] * (2 * n), pl.BlockSpec(memory_space=pltpu.VMEM)),
        input_output_aliases={i: 2 + i for i in range(2 * n)},
        compiler_params=pltpu.CompilerParams(has_side_effects=DATAFLOW),
    )(*[pltpu.with_memory_space_constraint(a, pltpu.HBM) for a in arrs + lands], *deps)
    return res[0], res[1], list(res[2:2 + n]), list(res[2 + n:2 + 2 * n]), res[-1]


def exchange_forward(name, started, after, first=0, count=None):
    send_sems, recv_sems, srcs, lands, _ = started
    count = len(srcs) - first if count is None else count
    mine = lands[first:first + count]
    n = len(mine)

    def copies(land_refs, send_ref, recv_ref):
        x, y, c = _place()
        out = []
        for t, land in enumerate(land_refs):
            for k in (2, 4, 6):
                slot = land.at[4 * (x ^ (k >> 2)) + 2 * (y ^ ((k >> 1) & 1)) + c]
                came, goes = 7 * (first + t) + k - 1, 7 * (first + t) + (k ^ 1) - 1
                out.append((
                    pltpu.make_async_remote_copy(src_ref=slot, dst_ref=slot, send_sem=send_ref.at[came], recv_sem=recv_ref.at[came],
                                                 device_id=(x, y, c), device_id_type=MESH_ID),
                    pltpu.make_async_remote_copy(src_ref=slot, dst_ref=slot, send_sem=send_ref.at[goes], recv_sem=recv_ref.at[goes],
                                                 device_id=(x, y, 1 - c), device_id_type=MESH_ID)))
        return out

    after = list(after) if isinstance(after, (list, tuple)) else [after]

    def arrived(*refs):
        for came, _ in copies(refs[:n], refs[n], refs[n + 1]):
            came.wait_recv()

    def pass_on(*refs):
        for _, goes in copies(refs[:n], refs[n], refs[n + 1]):
            goes.start()
        refs[-1][...] = jnp.zeros_like(refs[-1])

    hbm = lambda a: pltpu.HBM(a.shape, a.dtype)
    here = pl.pallas_call(
        arrived, name=name + "_arrived", out_shape=tuple(hbm(a) for a in mine),
        in_specs=[HBM_SPEC] * n + [SEM_SPEC, SEM_SPEC] + [pl.BlockSpec(memory_space=pl.ANY)] * len(after),
        out_specs=tuple([HBM_SPEC] * n), input_output_aliases={i: i for i in range(n)},
        compiler_params=pltpu.CompilerParams(has_side_effects=DATAFLOW),
    )(*mine, send_sems, recv_sems, *after)
    res = pl.pallas_call(
        pass_on, name=name, out_shape=(*[hbm(a) for a in mine], jax.ShapeDtypeStruct((8, 128), F32)),
        in_specs=[HBM_SPEC] * n + [SEM_SPEC, SEM_SPEC],
        out_specs=(*[HBM_SPEC] * n, pl.BlockSpec(memory_space=pltpu.VMEM)), input_output_aliases={i: i for i in range(n)},
        compiler_params=pltpu.CompilerParams(has_side_effects=DATAFLOW),
    )(*here, send_sems, recv_sems)
    lands = lands[:first] + list(res[:n]) + lands[first + count:]
    return (send_sems, recv_sems, srcs, lands, res[-1])


def exchange_wait(name, started, gather, after, first=0, count=None, tree=False):
    send_sems, recv_sems, srcs, lands, _ = started
    count = len(srcs) - first if count is None else count
    srcs, lands = srcs[first:first + count], lands[first:first + count]
    n = len(srcs)

    def body(*refs):
        src_refs, land_refs = refs[:n], refs[n:2 * n]
        copies = _exchange_copies(src_refs, land_refs, refs[2 * n], refs[2 * n + 1], gather, first)
        for _, cp in copies:
            cp.wait_send()
        for k, cp in copies:
            if not tree or k in (1,) + TREE_FORWARDED:
                cp.wait_recv()
        for cp in _own_copies(src_refs, land_refs, refs[2 * n], gather, first):
            cp.wait()

    hbm = lambda a: pltpu.HBM(a.shape, a.dtype)
    res = pl.pallas_call(
        body, name=name, out_shape=tuple(hbm(a) for a in srcs + lands),
        in_specs=[HBM_SPEC] * (2 * n) + [SEM_SPEC, SEM_SPEC, pl.BlockSpec(memory_space=pl.ANY)],
        out_specs=tuple([HBM_SPEC] * (2 * n)), input_output_aliases={i: i for i in range(2 * n)},
        compiler_params=pltpu.CompilerParams(has_side_effects=DATAFLOW),
    )(*srcs, *lands, send_sems, recv_sems, after)
    return list(res[:n]), list(res[n:])


def _gather_cols(stack):
    p, k, n = stack.shape
    return stack.transpose(1, 0, 2).reshape(k, p * n)


def _scatter_cols(full):
    k, n = full.shape
    return full.reshape(k, N_DEV, n // N_DEV).transpose(1, 0, 2)


def _gather_rows(stack):
    p, r, n = stack.shape
    return stack.reshape(p * r, n)


def _scatter_rows(full):
    r, n = full.shape
    return full.reshape(N_DEV, r // N_DEV, n)


_IN_NAT = Q_LORA + KV_LORA
TRANSPOSED = ("w_in", "w_q_b", "w_up")
ROWS_APART = ("w_in", "w_conv")


def to_kernel_layout(name, w):
    if name == "w_in":
        z = lambda n: jnp.zeros((n, w.shape[1]), w.dtype)
        return jnp.concatenate([w[:_IN_NAT], z(KPE_LO), w[_IN_NAT:_IN_NAT + ROPE], z(LANES - KPE_LO - ROPE), w[_IN_NAT + ROPE:]], axis=0)
    if name == "w_q_b":
        return jnp.pad(w.reshape(HEADS, NOPE + ROPE, -1), ((0, 0), (0, LANES - NOPE - ROPE), (0, 0))).reshape(HEADS * LANES, -1)
    if name == "w_o":
        mla = jnp.pad(w[:HEADS * NOPE].reshape(HEADS, NOPE, -1), ((0, 0), (LANES - NOPE, 0), (0, 0))).reshape(HEADS * LANES, -1)
        return jnp.concatenate([mla, w[HEADS * NOPE:]], axis=0)
    return w


def from_kernel_layout(name, g):
    if name == "w_in":
        return jnp.concatenate([g[:_IN_NAT], g[P_KPE + KPE_LO:P_KPE + KPE_LO + ROPE], g[P_QD:]], axis=0)
    if name == "w_q_b":
        return g.reshape(HEADS, LANES, -1)[:, :NOPE + ROPE, :].reshape(HEADS * (NOPE + ROPE), -1)
    if name == "w_o":
        mla = g[:HEADS * LANES].reshape(HEADS, LANES, -1)[:, LANES - NOPE:, :].reshape(HEADS * NOPE, -1)
        return jnp.concatenate([mla, g[HEADS * LANES:]], axis=0)
    return g


SMALL_COLS = 1024
SMALL_ROWS = 24
SMALL_AT = {"loss": (0, 0, 1), "b_ada": (1, 0, 6 * D_MODEL), "g_mix_norm": (7, 0, D_MODEL), "g_q_lat": (8, 0, Q_LORA),
            "g_kv_lat": (9, 0, KV_LORA), "g_mla_q_nope": (10, 0, NOPE), "g_mla_q_pe": (10, 128, ROPE),
            "g_mla_k_nope": (10, 256, NOPE), "g_mla_k_pe": (10, 384, ROPE), "g_dil_q": (10, 512, DIL_DIM),
            "g_dil_k": (10, 640, DIL_DIM), "g_ffn_norm": (11, 0, D_MODEL), "b_conv": (12, 0, 2 * D_FF)}
SMALL_PARAMS = tuple(n for n in SMALL_AT if n != "loss")


def _pack_small(values):
    by_row = {}
    for name, (row, off, n) in SMALL_AT.items():
        by_row.setdefault(row, []).append((off, values[name].reshape(-1).astype(F32)))
    out = []
    for row in sorted(by_row):
        pieces, at = [], 0
        for off, v in sorted(by_row[row], key=lambda t: t[0]):
            pieces += [jnp.zeros((off - at,), F32), v]
            at = off + v.shape[0]
        flat = jnp.concatenate(pieces)
        nrows = -(-flat.shape[0] // SMALL_COLS)
        out.append(jnp.pad(flat, (0, nrows * SMALL_COLS - flat.shape[0])).reshape(nrows, SMALL_COLS))
    packed = jnp.concatenate(out, axis=0)
    return jnp.pad(packed, ((0, SMALL_ROWS - packed.shape[0]), (0, 0)))


def _adam(w, g, m, v):
    c1 = 1.0 - ADAM_B1 ** ADAM_STEP
    c2 = 1.0 - ADAM_B2 ** ADAM_STEP
    m2 = ADAM_B1 * m + (1.0 - ADAM_B1) * g
    v2 = ADAM_B2 * v + (1.0 - ADAM_B2) * (g * g)
    return -ADAM_LR * ((m2 / c1) / (jnp.sqrt(v2 / c2) + ADAM_EPS) + ADAM_WD * w), m2, v2


def adamw_small(name, stack, params):
    flat = [a for n in SMALL_PARAMS for a in params[n]]

    def body(stack_ref, *refs):
        ins, outs = refs[:len(flat)], refs[len(flat):]
        g_all = stack_ref[0]
        for p in range(1, N_DEV):
            g_all = g_all + stack_ref[p]
        outs[0][...] = g_all[0:1, 0:1]
        for i, pname in enumerate(SMALL_PARAMS):
            row, off, n = SMALL_AT[pname]
            w_ref, m_ref, v_ref = ins[3 * i:3 * i + 3]
            go_ref, d_ref, mo_ref, vo_ref = outs[1 + 4 * i:5 + 4 * i]
            for c0 in range(0, n, SMALL_COLS):
                cn = min(SMALL_COLS, n - c0)
                r = row + c0 // SMALL_COLS
                g = g_all[r:r + 1, off:off + cn]
                cols = (slice(None), slice(c0, c0 + cn))
                d, m2, v2 = _adam(w_ref[cols], g, m_ref[cols], v_ref[cols])
                go_ref[cols], d_ref[cols], mo_ref[cols], vo_ref[cols] = g, d, m2, v2

    whole = lambda a: pl.BlockSpec(a.shape, lambda: (0,) * a.ndim)
    out_shape = [jax.ShapeDtypeStruct((1, 1), F32)] + [jax.ShapeDtypeStruct(a.shape, F32) for n in SMALL_PARAMS for a in params[n][:1] * 4]
    res = pl.pallas_call(body, name=name, in_specs=[whole(stack)] + [whole(a) for a in flat],
                         out_specs=[pl.BlockSpec(s.shape, lambda s=s: (0,) * len(s.shape)) for s in out_shape],
                         out_shape=out_shape, compiler_params=_params())(stack, *flat)
    return res[0], {n: res[1 + 4 * i:5 + 4 * i] for i, n in enumerate(SMALL_PARAMS)}


def _local_step(x, pos, mod, target, w, fetch, emit, halfway=lambda after: None):
    S = SEQ
    sh1, sc1, g1, sh2, sc2, g2 = [mod[:, i * D_MODEL:(i + 1) * D_MODEL] for i in range(6)]
    zeros = lambda n: jnp.zeros((1, n), F32)
    g_q = jnp.concatenate([w["g_mla_q_nope"], w["g_mla_q_pe"], zeros(LANES - NOPE - ROPE)], axis=1)
    g_k = jnp.concatenate([w["g_mla_k_nope"], zeros(LANES - NOPE)], axis=1)
    g_kpe = jnp.concatenate([zeros(KPE_LO), w["g_mla_k_pe"], zeros(LANES - KPE_LO - ROPE)], axis=1)
    g_dq = jnp.concatenate([w["g_dil_q"]] * 2, axis=1)
    g_dk = jnp.concatenate([w["g_dil_k"]] * 2, axis=1)
    b_conv = w["b_conv"]

    def inv_freq(d):
        return jnp.power(ROPE_THETA, -2.0 * jnp.arange(d // 2, dtype=F32) / d)

    n_m, n_d = ROPE // 2, DIL_DIM // 2
    freqs = jnp.concatenate([inv_freq(ROPE), inv_freq(DIL_DIM), jnp.zeros((LANES - n_m - n_d,), F32)]).reshape(1, LANES)

    def tables_fn(rows, params):
        (p,), (f,) = rows, params
        c, s = jnp.cos(p * f), jnp.sin(p * f)
        one, zero = jnp.ones_like(c), jnp.zeros_like(c)
        mla = lambda t, fill: jnp.concatenate([fill[:, :KPE_LO], t[:, :n_m], t[:, :n_m], fill[:, :LANES - KPE_LO - ROPE]], axis=1)
        dil = lambda t: jnp.concatenate([t[:, n_m:n_m + n_d]] * 4, axis=1)
        return [mla(c, one), mla(s, zero), dil(c), dil(s)], []

    cos_m, sin_m, cos_d, sin_d = rowwise("rope_tables", tables_fn, [pos], [freqs], [(LANES, F32)] * 4)
    tables = [cos_m, sin_m, cos_d, sin_d]
    H_M, H_D = ROPE // 2, DIL_DIM // 2

    def ln1_fn(rows, params):
        (xv,), (g, sc, sh) = rows, params
        y, _, _ = _rms(xv, g)
        return [y * (1.0 + sc) + sh], []

    (h,) = rowwise("ln1_fwd", ln1_fn, [x], [w["g_mix_norm"], sc1, sh1], [(D_MODEL, MXU_DTYPE)], dep=sin_d)
    w_in = fetch("w_in", h)

    def proj_fn(rows, params):
        (hv, cm, sm, cd, sd), (w_t, gq, gkv, gkp, gdq, gdk) = rows, params
        pv = _dot(hv, w_t, "nt")
        kper = _rope(_grms(pv[:, P_KPE:P_QD], gkp, KPE_GROUPS)[0], cm, sm, H_M)
        qd = [_rope(_grms(c, gdq, DIL_GROUPS)[0], cd, sd, H_D) for c in _chunks(pv[:, P_QD:P_KD])]
        kd = [_rope(_grms(c, gdk, DIL_GROUPS)[0], cd, sd, H_D) for c in _chunks(pv[:, P_KD:P_VD])]
        return [pv, _rms(pv[:, P_QLAT:P_KVLAT], gq)[0], _rms(pv[:, P_KVLAT:P_KPE], gkv)[0], kper,
                jnp.concatenate(qd, axis=1), jnp.concatenate(kd, axis=1)], []

    post_params = [w["g_q_lat"], w["g_kv_lat"], g_kpe, g_dq, g_dk]
    proj, qln, kvn, kper, qd_r, kd_r = rowwise(
        "proj_fwd", proj_fn, [h] + tables, [w_in] + post_params,
        [(P_END, F32), (Q_LORA, MXU_DTYPE), (KV_LORA, MXU_DTYPE), (LANES, MXU_DTYPE)] + [(DIL_WIDTH, F32)] * 2, tm=256)
    w_q_b, w_kv_b = fetch("w_q_b", qln), fetch("w_kv_b", kvn)

    def mla_proj_fn(rows, params):
        (qlv, kvlv, kp, cm, sm), (wq_t, wkv, gq, gk) = rows, params
        qv, kvv = _dot(qlv, wq_t, "nt"), _dot(kvlv, wkv)
        value_lanes = _lane(kp.shape) >= NOPE
        qs, ks, vs = [], [], []
        for qc, kc in zip(_chunks(qv), _chunks(kvv), strict=True):
            qs.append(_rope(_grms(qc, gq, Q_GROUPS)[0], cm, sm, H_M))
            ks.append(_grms(kc, gk, K_GROUPS)[0] + kp)
            vs.append(jnp.where(value_lanes, kc, 0.0))
        return [qv, kvv] + [jnp.concatenate(t, axis=1) for t in (qs, ks, vs)], []

    q, kv, q_mla, k_mla, v_mla = rowwise(
        "mla_proj", mla_proj_fn, [qln, kvn, kper, cos_m, sin_m], [w_q_b, w_kv_b, g_q, g_k],
        [(HEADS * LANES, F32)] * 2 + [(HEADS * LANES, MXU_DTYPE)] * 3, tm=256)
    mla_scale = (NOPE + ROPE) ** -0.5
    o_cat, lse_mla = mla_fwd("mla_fwd", q_mla, k_mla, v_mla, mla_scale)
    passed = halfway(lse_mla)

    band = [band_fwd(f"band{dil}_fwd", qd_r, kd_r, proj, dil, dep=passed) for dil in DILATIONS]
    o_cat, lse_mix = combine_fwd("dil_combine", [b[0] for b in band], [b[1] for b in band], o_cat)
    w_o = fetch("w_o", o_cat)

    def mid_fn(rows, params):
        (ov, xv), (w_out, gate1, g, sc, sh) = rows, params
        mx = _dot(ov, w_out)
        x1 = xv + gate1 * mx
        y, _, _ = _rms(x1, g)
        return [mx, x1, y * (1.0 + sc) + sh], []

    mix, x1, h2 = rowwise("mix_fwd", mid_fn, [o_cat, x], [w_o, g1, w["g_ffn_norm"], sc2, sh2],
                          [(D_MODEL, F32), (D_MODEL, F32), (D_MODEL, MXU_DTYPE)], tm=256)
    w_up, w_conv, w_down = fetch("w_up", h2), fetch("w_conv", h2), fetch("w_down", h2)
    dn, up = ffn_fwd("ffn_fwd", h2, w_up, w_conv, b_conv, w_down)

    def final_fn(rows, params):
        (x1v, dnv, tgt), (gate2,) = rows, params
        r = x1v + gate2 * dnv - tgt
        dy = r * (1.0 / D_MODEL)
        loss = jnp.sum(_colsum(r * r), axis=-1, keepdims=True) * (0.5 / D_MODEL)
        return [dy, gate2 * dy], [loss, _colsum(dy * dnv)]

    dy, d_dn, loss, dg2 = rowwise("loss_head", final_fn, [x1, dn, target], [g2], [(D_MODEL, F32), (D_MODEL, MXU_DTYPE)],
                                  [1, D_MODEL])
    dh2, g_up, g_down, g_w_conv, g_b_conv = ffn_bwd("ffn_bwd", h2, up, w_up, w_conv, b_conv, d_dn, w_down)
    emit("w_down", g_down)
    emit("w_conv", g_w_conv)
    sent = emit("w_up", g_up)

    def mid_bwd_fn(rows, params):
        (dh2v, dyv, x1v, mx), (gate1, g, sc, w_out) = rows, params
        yn, n, rstd = _rms(x1v, g)
        dx_n, dg = _rms_bwd(dh2v * (1.0 + sc), n, rstd, g)
        dx1 = dyv + dx_n
        dm = gate1 * dx1
        return [dx1, dm, _dot(dm, w_out, "nt")], [dg, _colsum(dh2v * yn), _colsum(dh2v), _colsum(dx1 * mx)]

    dx1, dmix, do_cat, dg_ffn, dsc2, dsh2, dg1 = rowwise(
        "mid_bwd", mid_bwd_fn, [dh2, dy, x1, mix], [g1, w["g_ffn_norm"], sc2, w_o],
        [(D_MODEL, F32), (D_MODEL, MXU_DTYPE), (w_o.shape[0], F32)], [D_MODEL] * 4, tm=256, dep=sent)

    sent = emit("w_o", matmul("mix_wgrad", o_cat, dmix, "tn", tm=512, out_dtype=MXU_DTYPE))
    dband = None
    for dil, b in zip(DILATIONS, band):
        dband = band_bwd(f"band{dil}_bwd", qd_r, kd_r, proj, b[1], lse_mix, o_cat, do_cat, dil, before=dband)
    dq_mla, dkv_mla, dkper = mla_bwd("mla_bwd", q_mla, k_mla, v_mla, o_cat, do_cat, lse_mla, mla_scale)

    def mla_prep_bwd_fn(rows, params):
        (dqv, dkvv, qv, kvv, cm, sm), (gq, gk) = rows, params
        nope_lanes = _lane(cm.shape) < NOPE
        dqs, dkvs, dgq, dgk = [], [], 0.0, 0.0
        for dqc, dkc, qc, kc in zip(_chunks(dqv), _chunks(dkvv), _chunks(qv), _chunks(kvv), strict=True):
            _, n, rstd = _grms(qc, gq, Q_GROUPS)
            dx, dg = _grms_bwd(_rope_bwd(dqc, cm, sm, H_M), n, rstd, gq, Q_GROUPS)
            dqs.append(dx)
            dgq = dgq + dg
            _, n, rstd = _grms(kc, gk, K_GROUPS)
            dx, dg = _grms_bwd(dkc, n, rstd, gk, K_GROUPS)
            dkvs.append(jnp.where(nope_lanes, dx, dkc))
            dgk = dgk + dg
        return [jnp.concatenate(dqs, axis=1), jnp.concatenate(dkvs, axis=1)], [dgq, dgk]

    dq, dkv, dg_q, dg_k = rowwise("mla_prep_bwd", mla_prep_bwd_fn, [dq_mla, dkv_mla, q, kv, cos_m, sin_m], [g_q, g_k],
                                  [(HEADS * LANES, MXU_DTYPE)] * 2, [LANES, LANES], tm=256, dep=sent)
    emit("w_q_b", matmul("q_wgrad", dq, qln, "tn", out_dtype=MXU_DTYPE))
    sent = emit("w_kv_b", matmul("kv_wgrad", kvn, dkv, "tn", out_dtype=MXU_DTYPE))

    def pre_bwd_fn(rows, params):
        dqv, dkvv, dkp, dqd_, dkd_, dvd_, pv, cm, sm, cd, sd = rows
        wq_t, wkv, gq, gkv, gkp, gdq, gdk = params
        dql, dkvl = _dot(dqv, wq_t), _dot(dkvv, wkv, "nt")
        r_q = _norm_bwd(dql, pv[:, P_QLAT:P_KVLAT], gq)
        r_kv = _norm_bwd(dkvl, pv[:, P_KVLAT:P_KPE], gkv)
        _, n, rstd = _grms(pv[:, P_KPE:P_QD], gkp, KPE_GROUPS)
        r_kp = _grms_bwd(_rope_bwd(dkp, cm, sm, H_M), n, rstd, gkp, KPE_GROUPS)
        outs, dgs = [r_q[0], r_kv[0], r_kp[0]], []
        for dval, lo, g in ((dqd_, P_QD, gdq), (dkd_, P_KD, gdk)):
            dg_sum = 0.0
            for dc, xc in zip(_chunks(dval), _chunks(pv[:, lo:lo + DIL_WIDTH]), strict=True):
                _, n, rstd = _grms(xc, g, DIL_GROUPS)
                dx, dg = _grms_bwd(_rope_bwd(dc, cd, sd, H_D), n, rstd, g, DIL_GROUPS)
                outs.append(dx)
                dg_sum = dg_sum + dg
            dgs.append(dg_sum)
        return [jnp.concatenate(outs + [dvd_], axis=1)], [r_q[1], r_kv[1], r_kp[1]] + dgs

    dproj, dg_q_lat, dg_kv_lat, dg_kpe, dg_dq, dg_dk = rowwise(
        "proj_pre_bwd", pre_bwd_fn,
        [dq, dkv, dkper] + list(dband) + [proj] + tables, [w_q_b, w_kv_b] + post_params,
        [(P_END, MXU_DTYPE)], [Q_LORA, KV_LORA, LANES, LANES, LANES], tm=256, dep=sent)
    sent = emit("w_in", matmul("proj_wgrad", dproj, h, "tn", tn=512, out_dtype=MXU_DTYPE))

    def ln1_bwd_fn(rows, params):
        (dpv, dres, xv), (w_t, g, sc) = rows, params
        dhv = _dot(dpv, w_t)
        yn, n, rstd = _rms(xv, g)
        dx_n, dg = _rms_bwd(dhv * (1.0 + sc), n, rstd, g)
        return [dres + dx_n], [dg, _colsum(dhv * yn), _colsum(dhv)]

    grad_x, dg_mix, dsc1, dsh1 = rowwise("proj_dgrad", ln1_bwd_fn, [dproj, dx1, x], [w_in, w["g_mix_norm"], sc1],
                                         [(D_MODEL, F32)], [D_MODEL] * 3, tm=256, dep=sent)
    dmod = jnp.concatenate([dsh1, dsc1, dg1, dsh2, dsc2, dg2], axis=-1)
    small = {"loss": loss, "b_ada": dmod, "g_mix_norm": dg_mix, "g_q_lat": dg_q_lat, "g_kv_lat": dg_kv_lat,
             "g_mla_q_nope": dg_q[:, :NOPE], "g_mla_q_pe": dg_q[:, NOPE:NOPE + ROPE], "g_mla_k_nope": dg_k[:, :NOPE],
             "g_mla_k_pe": dg_kpe[:, KPE_LO:KPE_LO + ROPE], "g_dil_q": dg_dq[:, :DIL_DIM] + dg_dq[:, DIL_DIM:],
             "g_dil_k": dg_dk[:, :DIL_DIM] + dg_dk[:, DIL_DIM:], "g_ffn_norm": dg_ffn,
             "b_conv": g_b_conv}
    return grad_x, small


COL_SHARDED = ("w_kv_b", "w_conv")
ROW_SHARDED = ("w_o", "w_down") + TRANSPOSED
ADAM_TILE = {"w_ada": 256, "w_up": 176, "w_down": 176}
GATHER_GROUPS = (("w_in",), ("w_q_b", "w_kv_b"), ("w_o",), ("w_up", "w_conv", "w_down"))
START_STAGES = ((0, 1), (2, 3))
FORWARD_STAGES = ((0, 1), (2,), (3,))
FORWARD_WITH = {"w_o": 2}
SCATTER_GROUPS = (("w_down", "w_conv", "w_up"), ("w_o",), ("w_q_b", "w_kv_b"), ("w_in",))
OUT_WEIGHTS = ("w_ada", "b_ada", "g_mix_norm", "w_in", "g_q_lat", "w_q_b", "g_kv_lat", "w_kv_b", "g_mla_q_nope", "g_mla_q_pe",
               "g_mla_k_nope", "g_mla_k_pe", "g_dil_q", "g_dil_k", "w_o", "g_ffn_norm", "w_up", "w_conv", "b_conv", "w_down")


def kernel(x, c, positions, w_ada, b_ada, g_mix_norm, w_in, g_q_lat, w_q_b, g_kv_lat, w_kv_b, g_mla_q_nope, g_mla_q_pe, g_mla_k_nope, g_mla_k_pe, g_dil_q, g_dil_k, w_o, g_ffn_norm, w_up, w_conv, b_conv, w_down, loss_target, m_w_ada, m_b_ada, m_g_mix_norm, m_w_in, m_g_q_lat, m_w_q_b, m_g_kv_lat, m_w_kv_b, m_g_mla_q_nope, m_g_mla_q_pe, m_g_mla_k_nope, m_g_mla_k_pe, m_g_dil_q, m_g_dil_k, m_w_o, m_g_ffn_norm, m_w_up, m_w_conv, m_b_conv, m_w_down, v_w_ada, v_b_ada, v_g_mix_norm, v_w_in, v_g_q_lat, v_w_q_b, v_g_kv_lat, v_w_kv_b, v_g_mla_q_nope, v_g_mla_q_pe, v_g_mla_k_nope, v_g_mla_k_pe, v_g_dil_q, v_g_dil_k, v_w_o, v_g_ffn_norm, v_w_up, v_w_conv, v_b_conv, v_w_down):
    args = dict(locals())
    xi, yi, ci = _place()
    me = 4 * xi + 2 * yi + ci
    def local(prefix, n):
        a = args[prefix + n]
        if n in ROWS_APART:
            return jnp.transpose(a, (2, 0, 1) if n in TRANSPOSED else (1, 0, 2))
        return a[0].T if n in TRANSPOSED else a[0]

    def as_output(n, r):
        if n in ROWS_APART:
            return jnp.transpose(r, (1, 2, 0) if n in TRANSPOSED else (1, 0, 2))
        return (r.T if n in TRANSPOSED else r)[None]

    shard = {n: local("", n) for n in COL_SHARDED + ROW_SHARDED + ("w_ada",)}
    flat = lambda n, a: a.reshape(a.shape[0], a.shape[-1]) if n in ROWS_APART else a
    small_w = {n: args[n] for n in SMALL_PARAMS}

    payload = {n: flat(n, shard[n]) if n == "w_conv" else flat(n, shard[n]).astype(MXU_DTYPE) for n in COL_SHARDED + ROW_SHARDED}
    start_order = [[n for i in groups for n in GATHER_GROUPS[i]] for groups in START_STAGES]

    sc_all, mod_all = ada_modulation("ada_mod", c, shard["w_ada"], after=[payload[n] for n in start_order[0]])

    exchange_of = lambda i: [e for e, groups in enumerate(START_STAGES) if i in groups][0]
    start_stage = lambda e, after: exchange_start(f"gather_start{e}", [payload[n] for n in start_order[e]], gather=True,
                                                  after=after, tree=True)
    gathered = {0: start_stage(0, mod_all)}
    after_start = gathered[0][-1]
    full, forwarded = {}, set()

    def forward(stage, after):
        e = exchange_of(FORWARD_STAGES[stage][0])
        if stage not in forwarded:
            forwarded.add(stage)
            first = start_order[e].index(GATHER_GROUPS[FORWARD_STAGES[stage][0]][0])
            count = sum(len(GATHER_GROUPS[i]) for i in FORWARD_STAGES[stage])
            starts_next = e + 1 < len(START_STAGES) and e + 1 not in gathered
            ready = [payload[n] for n in start_order[e + 1]] if starts_next else []
            gathered[e] = exchange_forward(f"gather_forward{stage}", gathered[e], [after] + ready, first, count)
            if starts_next:
                gathered[e + 1] = start_stage(e + 1, gathered[e][-1])
        return gathered[e][-1]

    def fetch(name, after):
        if name not in full:
            (i, grp), = [(i, grp) for i, grp in enumerate(GATHER_GROUPS) if name in grp]
            (stage,) = [s for s, groups in enumerate(FORWARD_STAGES) if i in groups]
            forward(stage, after)
            if name in FORWARD_WITH:
                forward(FORWARD_WITH[name], after)
            e = exchange_of(i)
            behind = gathered[e + 1][-1] if e + 1 in gathered else after
            srcs, lands = exchange_wait(f"gather{i}_wait", gathered[e], True, behind, start_order[e].index(grp[0]), len(grp), tree=True)
            for n, stack in zip(grp, lands, strict=True):
                full[n] = to_kernel_layout(n, _gather_cols(stack) if n in COL_SHARDED else _gather_rows(stack))
        return full[name]

    mod_row = lax.dynamic_index_in_dim(mod_all, me, axis=1, keepdims=False).reshape(1, 6 * D_MODEL)
    (mod,) = rowwise("ada_bias", lambda rows, params: ([rows[0] + rows[1]], []), [mod_row, b_ada], [], [(6 * D_MODEL, F32)],
                     dep=after_start)

    pending, scatters = {}, {}

    def emit(name, grad):
        grad = from_kernel_layout(name, grad)
        pending[name] = _scatter_cols(grad) if name in COL_SHARDED else _scatter_rows(grad)
        for i, grp in enumerate(SCATTER_GROUPS):
            if name == grp[-1]:
                scatters[i] = exchange_start(f"scatter{i}_start", [pending[n] for n in grp], gather=False)
                return scatters[i][-1]
        return None

    pos = positions.reshape(SEQ, 1).astype(F32)
    grad_x, small = _local_step(x[0], pos, mod, loss_target[0], small_w, fetch, emit, halfway=lambda after: forward(1, after))

    small_sent = exchange_start("small_start", [_pack_small(small)], gather=True, after=grad_x)

    res, done = {}, small_sent[-1]
    for i, grp in enumerate(SCATTER_GROUPS):
        _, lands = exchange_wait(f"scatter{i}_wait", scatters[i], False, done)
        for n, land in zip(grp, lands, strict=True):
            res[n] = adamw(f"adamw_{n}", shard[n], [land], local("m_", n), local("v_", n), ADAM_TILE.get(n))
            done = res[n][0]
            res[n] = [as_output(n, r) for r in res[n]]
    _, (small_all,) = exchange_wait("small_wait", small_sent, True, done)
    loss, small_res = adamw_small("adamw_small", small_all, {n: (args[n], args["m_" + n], args["v_" + n]) for n in SMALL_PARAMS})
    row, _, n_mod = SMALL_AT["b_ada"]
    dmod_all = small_all[:, row:row + n_mod // SMALL_COLS, :].reshape(N_DEV, n_mod)
    dmod_mine = lax.dynamic_slice_in_dim(dmod_all, me * (6 * D_MODEL // N_DEV), 6 * D_MODEL // N_DEV, axis=1)
    g_w_ada = matmul("ada_wgrad", sc_all, dmod_mine, "tn")
    res["w_ada"] = [r[None] for r in adamw("adamw_w_ada", shard["w_ada"], [g_w_ada], m_w_ada[0], v_w_ada[0], ADAM_TILE["w_ada"])]

    def leaf(kind, n):
        return res[n][kind] if n in res else small_res[n][kind]

    return (loss.reshape(()), grad_x[None], *[leaf(k, n) for k in range(4) for n in OUT_WEIGHTS])
```

```python
import jax
import jax.numpy as jnp
from jax import lax
from jax.experimental import pallas as pl
from jax.experimental.pallas import tpu as pltpu

F32 = jnp.float32
MXU_DTYPE = jnp.bfloat16

N_DEV = 8
D_MODEL = 1024
SEQ = 2048
HEADS = 8
NOPE = 64
ROPE = 32
Q_LORA = 512
KV_LORA = 256
DIL_DIM = 64
DIL_WIDTH = HEADS * DIL_DIM
DILATIONS = (1, 4, 16)
SPAN = 128
D_FF = 2816
LANES = 128
SUBLANES = 8
ROPE_THETA = 10000.0
EPS = 1e-6
NEG_INF = -1e30
ADAM_LR, ADAM_B1, ADAM_B2, ADAM_EPS, ADAM_WD, ADAM_STEP = 0.001, 0.9, 0.999, 1e-08, 0.01, 10
VMEM_LIMIT = 56 * 1024 * 1024
MESH_ID = pl.DeviceIdType.MESH

P_QLAT, P_KVLAT, P_KPE, P_QD, P_KD, P_VD, P_END = 0, 512, 768, 896, 1408, 1920, 2432
KPE_LO = 64
MIX_IN = HEADS * LANES + DIL_WIDTH


def _params(**kw):
    return pltpu.CompilerParams(vmem_limit_bytes=VMEM_LIMIT, **kw)


def rowwise(name, fn, rows, params, out_rows, out_accs=(), tm=512, dep=None):
    deps = [] if dep is None else [dep]
    rows = [r if isinstance(r, tuple) else (r, r.shape[1], 0) for r in rows]
    R = rows[0][0].shape[0]
    tm = min(tm, R)
    steps = R // tm
    assert steps * tm == R
    in_specs = []
    for a, width, cb in rows:
        ri = a.shape[0]
        per = ri // tm
        assert per * tm == ri
        if ri == R:
            in_specs.append(pl.BlockSpec((tm, width), lambda i, cb=cb: (i, cb)))
        else:
            in_specs.append(pl.BlockSpec((tm, width), lambda i, per=per, cb=cb: (i % per, cb)))
    for p in params:
        in_specs.append(pl.BlockSpec(p.shape, lambda i: (0,) * p.ndim))
    in_specs += [pl.BlockSpec(memory_space=pl.ANY)] * len(deps)
    out_shape = [jax.ShapeDtypeStruct((R, d), dt) for d, dt in out_rows]
    out_specs = [pl.BlockSpec((tm, d), lambda i: (i, 0)) for d, _ in out_rows]
    out_shape += [jax.ShapeDtypeStruct((1, n), F32) for n in out_accs]
    out_specs += [pl.BlockSpec((1, n), lambda i: (0, 0)) for n in out_accs]
    nr, npar, no, na = len(rows), len(params), len(out_rows), len(out_accs)

    def body(*refs):
        rvals = [r[...] for r in refs[:nr]]
        pvals = [r[...] for r in refs[nr:nr + npar]]
        outs, accs = fn(rvals, pvals)
        first_out = nr + npar + len(deps)
        for ref, v in zip(refs[first_out:first_out + no], outs, strict=True):
            ref[...] = v.astype(ref.dtype)
        if na:
            acc_refs = refs[first_out + no:]
            i = pl.program_id(0)

            @pl.when(i == 0)
            def _():
                for ref, v in zip(acc_refs, accs, strict=True):
                    ref[...] = v

            @pl.when(i > 0)
            def _():
                for ref, v in zip(acc_refs, accs, strict=True):
                    ref[...] += v

    res = pl.pallas_call(body, name=name, grid=(steps,), in_specs=in_specs, out_specs=out_specs,
                         out_shape=out_shape, compiler_params=_params())(*[r[0] for r in rows], *params, *deps)
    return list(res)


_DIMS = {"nn": ((1,), (0,)), "nt": ((1,), (1,)), "tn": ((0,), (0,))}


def _dot(a, b, mode="nn"):
    return lax.dot_general(a.astype(MXU_DTYPE), b.astype(MXU_DTYPE), (_DIMS[mode], ((), ())),
                           preferred_element_type=F32)


def matmul(name, a, b, mode, tm=None, tn=None, tk=None, out_dtype=F32, dep=None):
    if mode == "tn":
        K, M = a.shape
    else:
        M, K = a.shape
    N = b.shape[0] if mode == "nt" else b.shape[1]
    tm, tn, tk = tm or M, tn or N, tk or K
    nm, nn, nk = M // tm, N // tn, K // tk
    assert nm * tm == M and nn * tn == N and nk * tk == K
    a_spec = pl.BlockSpec((tk, tm), lambda i, j, k: (k, i)) if mode == "tn" else pl.BlockSpec((tm, tk), lambda i, j, k: (i, k))
    b_spec = pl.BlockSpec((tn, tk), lambda i, j, k: (j, k)) if mode == "nt" else pl.BlockSpec((tk, tn), lambda i, j, k: (k, j))
    deps = [] if dep is None else [dep]

    def body(a_ref, b_ref, *rest):
        o_ref, scratch = rest[len(deps)], rest[len(deps) + 1:]
        p = _dot(a_ref[...], b_ref[...], mode)
        if nk == 1:
            o_ref[...] = p.astype(o_ref.dtype)
        else:
            acc = scratch[0]
            k = pl.program_id(2)

            @pl.when(k == 0)
            def _():
                acc[...] = p

            @pl.when(k > 0)
            def _():
                acc[...] += p

            @pl.when(k == nk - 1)
            def _():
                o_ref[...] = acc[...].astype(o_ref.dtype)

    return pl.pallas_call(
        body, name=name, grid=(nm, nn, nk), in_specs=[a_spec, b_spec] + [pl.BlockSpec(memory_space=pl.ANY)] * len(deps),
        out_specs=pl.BlockSpec((tm, tn), lambda i, j, k: (i, j)),
        out_shape=jax.ShapeDtypeStruct((M, N), out_dtype),
        scratch_shapes=[pltpu.VMEM((tm, tn), F32)] if nk > 1 else [],
        compiler_params=_params())(a, b, *deps)


def _rms(x, g):
    rstd = lax.rsqrt(jnp.mean(x * x, axis=-1, keepdims=True) + EPS)
    n = x * rstd
    return n * g, n, rstd


def _rms_bwd(dy, n, rstd, g):
    dg = jnp.sum(dy * n, axis=0, keepdims=True)
    dn = dy * g
    dx = rstd * (dn - n * jnp.mean(dn * n, axis=-1, keepdims=True))
    return dx, dg


def _norm_bwd(dy, x, g):
    _, n, rstd = _rms(x, g)
    return _rms_bwd(dy, n, rstd, g)


def _colsum(v):
    return jnp.sum(v, axis=0, keepdims=True)


def _silu(x):
    return x * (1.0 / (1.0 + jnp.exp(-x)))


def _lane(shape):
    return lax.broadcasted_iota(jnp.int32, shape, 1)


def _group_mean(v, groups):
    i = lax.broadcasted_iota(jnp.int32, (LANES, LANES), 0)
    j = lax.broadcasted_iota(jnp.int32, (LANES, LANES), 1)
    g = jnp.zeros((LANES, LANES), F32)
    for lo, hi in groups:
        g = jnp.where((i >= lo) & (i < hi) & (j >= lo) & (j < hi), 1.0 / (hi - lo), g)
    head = v.astype(MXU_DTYPE)
    return _dot(head, g) + _dot(v - head.astype(F32), g)


def _in_groups(shape, groups):
    lane = _lane(shape)
    m = jnp.zeros(shape, jnp.bool_)
    for lo, hi in groups:
        m = m | ((lane >= lo) & (lane < hi))
    return m


def _grms(x, g, groups):
    rstd = lax.rsqrt(_group_mean(x * x, groups) + EPS)
    n = jnp.where(_in_groups(x.shape, groups), x * rstd, 0.0)
    return n * g, n, rstd


def _grms_bwd(dy, n, rstd, g, groups):
    dn = dy * g
    return rstd * (dn - n * _group_mean(dn * n, groups)), _colsum(dy * n)


def _rot(x, half, transpose=False):
    first = (_lane(x.shape) % (2 * half)) < half
    up = pltpu.roll(x, LANES - half, axis=1)
    down = pltpu.roll(x, half, axis=1)
    return jnp.where(first, up, -down) if transpose else jnp.where(first, -up, down)


def _rope(x, cos, sin, half):
    return x * cos + _rot(x, half) * sin


def _rope_bwd(dy, cos, sin, half):
    return dy * cos + _rot(dy * sin, half, transpose=True)


def _chunks(x):
    return [x[:, i:i + LANES] for i in range(0, x.shape[1], LANES)]


Q_GROUPS = ((0, NOPE), (NOPE, NOPE + ROPE))
K_GROUPS = ((0, NOPE),)
KPE_GROUPS = ((KPE_LO, KPE_LO + ROPE),)
DIL_GROUPS = ((0, DIL_DIM), (DIL_DIM, 2 * DIL_DIM))


def _col(width, rows=SEQ):
    return pl.BlockSpec((rows, width), lambda h: (0, h))


def _causal_tail(s, tq, fill):
    diag = s[:, s.shape[1] - tq:]
    keep = lax.broadcasted_iota(jnp.int32, diag.shape, 1) <= lax.broadcasted_iota(jnp.int32, diag.shape, 0)
    diag = jnp.where(keep, diag, fill)
    return diag if s.shape[1] == tq else jnp.concatenate([s[:, :s.shape[1] - tq], diag], axis=1)


def mla_fwd(name, q, k, v, scale, tq=256):
    S = q.shape[0]

    def body(q_ref, k_ref, v_ref, o_ref, lse_ref):
        nb = S // tq
        blk = lambda i: slice(i * tq, (i + 1) * tq)

        def scores(i):
            return _dot(q_ref[blk(i), :], k_ref[:(i + 1) * tq, :], "nt")

        def softmax(i, s):
            s = _causal_tail(s * scale, tq, NEG_INF)
            m = jnp.max(s, axis=-1, keepdims=True)
            e = jnp.exp(s - m)
            l = jnp.sum(e, axis=-1, keepdims=True)
            lse_ref[0, blk(i), :] = m + jnp.log(l)
            return (e * (1.0 / l)).astype(MXU_DTYPE)

        def weighted(i, p):
            o_ref[blk(i), :] = _dot(p, v_ref[:(i + 1) * tq, :])

        s, p_prev = scores(0), None
        for i in range(nb):
            s_next = scores(i + 1) if i + 1 < nb else None
            if p_prev is not None:
                weighted(i - 1, p_prev)
            p_prev, s = softmax(i, s), s_next
        weighted(nb - 1, p_prev)

    return pl.pallas_call(
        body, name=name, grid=(HEADS,), in_specs=[_col(LANES)] * 3,
        out_specs=[_col(LANES), pl.BlockSpec((1, S, 1), lambda h: (h, 0, 0))],
        out_shape=[jax.ShapeDtypeStruct((S, MIX_IN), F32), jax.ShapeDtypeStruct((HEADS, S, 1), F32)],
        compiler_params=_params())(q, k, v)


def mla_bwd(name, q, k, v, o, do, lse, scale, tq=256):
    S = q.shape[0]

    def body(q_ref, k_ref, v_ref, o_ref, do_ref, lse_ref, dq_ref, dkv_ref, dkpe_ref, dk_acc, dv_acc):
        dk_acc[...] = jnp.zeros_like(dk_acc)
        dv_acc[...] = jnp.zeros_like(dv_acc)
        for i in range(S // tq):
            kext = (i + 1) * tq
            blk = slice(i * tq, kext)
            qi, kk, vv = q_ref[blk, :], k_ref[:kext, :], v_ref[:kext, :]
            doi = do_ref[blk, :]
            s = _causal_tail(_dot(qi, kk, "nt") * scale, tq, NEG_INF)
            p = jnp.exp(s - lse_ref[0, blk, :])
            dp = _dot(doi, vv, "nt")
            delta = jnp.sum(doi * o_ref[blk, :], axis=-1, keepdims=True)
            ds = p * (dp - delta) * scale
            dq_ref[blk, :] = _dot(ds, kk)
            dk_acc[:kext, :] += _dot(ds, qi, "tn")
            dv_acc[:kext, :] += _dot(p, doi, "tn")
        dk = dk_acc[...]
        lane = _lane(dk.shape)
        dkv_ref[...] = jnp.where(lane < NOPE, dk, 0.0) + dv_acc[...]
        dkpe = jnp.where((lane >= KPE_LO) & (lane < KPE_LO + ROPE), dk, 0.0)
        h = pl.program_id(0)

        @pl.when(h == 0)
        def _():
            dkpe_ref[...] = dkpe

        @pl.when(h > 0)
        def _():
            dkpe_ref[...] += dkpe

    return pl.pallas_call(
        body, name=name, grid=(HEADS,),
        in_specs=[_col(LANES)] * 5 + [pl.BlockSpec((1, S, 1), lambda h: (h, 0, 0))],
        out_specs=[_col(LANES), _col(LANES), pl.BlockSpec((S, LANES), lambda h: (0, 0))],
        out_shape=[jax.ShapeDtypeStruct((S, HEADS * LANES), F32), jax.ShapeDtypeStruct((S, HEADS * LANES), F32),
                   jax.ShapeDtypeStruct((S, LANES), F32)],
        scratch_shapes=[pltpu.VMEM((S, LANES), F32), pltpu.VMEM((S, LANES), F32)],
        compiler_params=_params())(q, k, v, o, do, lse)


BAND_TQ = SPAN


def _band_blocks(L, tq):
    return [(i * tq, (i + 1) * tq, max(0, i * tq - SPAN)) for i in range(L // tq)]


def _class_rows(r, dil, lo, hi):
    return pl.ds(r + dil * lo, hi - lo, stride=dil) if dil > 1 else pl.ds(lo, hi - lo)


def _stack_heads(t, lo):
    zero = jnp.zeros_like(t)
    return jnp.concatenate([jnp.where(lo, t, zero), jnp.where(lo, zero, t)], axis=0)


def _band_mask2(q0, q1, k0):
    n = q1 - q0
    shape = (2 * n, q1 - k0)
    i = lax.broadcasted_iota(jnp.int32, shape, 0)
    dist = (jnp.where(i >= n, i - n, i) + q0) - (lax.broadcasted_iota(jnp.int32, shape, 1) + k0)
    return (dist >= 0) & (dist <= SPAN)


def _pair_col(col0=0):
    return pl.BlockSpec((SEQ, LANES), lambda j: (0, col0 // LANES + j))


def band_fwd(name, q, k, v, dil, dep=None):
    S = q.shape[0]
    L = S // dil
    tq = BAND_TQ
    scale = DIL_DIM ** -0.5
    deps = [] if dep is None else [dep]

    def body(q_ref, k_ref, v_ref, *rest):
        o_ref, lse_ref = rest[len(deps):]
        items = [(r, blk) for r in range(dil) for blk in _band_blocks(L, tq)]
        lo = _lane((tq, LANES)) < DIL_DIM

        def scores(item):
            r, (q0, q1, k0) = item
            qb = q_ref[_class_rows(r, dil, q0, q1), :].astype(MXU_DTYPE)
            return _dot(_stack_heads(qb, lo), k_ref[_class_rows(r, dil, k0, q1), :], "nt")

        def softmax(item, s):
            _, (q0, q1, k0) = item
            s = jnp.where(_band_mask2(q0, q1, k0), s * scale, NEG_INF)
            mx = jnp.max(s, axis=-1, keepdims=True)
            e = jnp.exp(s - mx)
            l = jnp.sum(e, axis=-1, keepdims=True)
            return (e * (1.0 / l)).astype(MXU_DTYPE), mx + jnp.log(l)

        def weighted(item, p, lse):
            r, (q0, q1, k0) = item
            pv = _dot(p, v_ref[_class_rows(r, dil, k0, q1), :])
            o_ref[_class_rows(r, dil, q0, q1), :] = jnp.where(lo, pv[:tq], pv[tq:])
            lse_ref[_class_rows(r, dil, q0, q1), :] = jnp.where(lo, lse[:tq], lse[tq:])

        s, prev = scores(items[0]), None
        for i, item in enumerate(items):
            s_next = scores(items[i + 1]) if i + 1 < len(items) else None
            if prev is not None:
                weighted(items[i - 1], *prev)
            prev, s = softmax(item, s), s_next
        weighted(items[-1], *prev)

    return pl.pallas_call(
        body, name=name, grid=(DIL_WIDTH // LANES,),
        in_specs=[_pair_col()] * 2 + [_pair_col(P_VD)] + [pl.BlockSpec(memory_space=pl.ANY)] * len(deps), out_specs=[_pair_col()] * 2,
        out_shape=[jax.ShapeDtypeStruct((S, DIL_WIDTH), F32)] * 2, compiler_params=_params())(q, k, v, *deps)


def band_bwd(name, q, k, v, lse, lse_mix, o_cat, do_cat, dil, before=None):
    S = q.shape[0]
    L = S // dil
    tq = BAND_TQ
    scale = DIL_DIM ** -0.5
    before = list(before or [])

    def body(q_ref, k_ref, v_ref, lse_ref, mix_ref, o_ref, do_ref, *rest):
        dq_ref, dk_ref, dv_ref = rest[len(before):]
        if before:
            dq0_ref, dk0_ref, dv0_ref = rest[:3]
            dk_ref[...] = dk0_ref[...]
            dv_ref[...] = dv0_ref[...]
        else:
            dk_ref[...] = jnp.zeros_like(dk_ref)
            dv_ref[...] = jnp.zeros_like(dv_ref)
        items = [(r, blk) for r in range(dil) for blk in _band_blocks(L, tq)]
        lo = _lane((tq, LANES)) < DIL_DIM
        per_head = lambda t: jnp.concatenate([t[:, 0:1], t[:, DIL_DIM:DIL_DIM + 1]], axis=0)

        def scores(item):
            r, (q0, q1, k0) = item
            qrows, krows = _class_rows(r, dil, q0, q1), _class_rows(r, dil, k0, q1)
            lse_p, dout = lse_ref[qrows, :], do_ref[qrows, :]
            w2 = per_head(jnp.exp(lse_p - mix_ref[qrows, :]))
            dd = dout * o_ref[qrows, :]
            big_d = jnp.concatenate([jnp.sum(jnp.where(lo, dd, 0.0), axis=-1, keepdims=True),
                                     jnp.sum(jnp.where(lo, 0.0, dd), axis=-1, keepdims=True)], axis=0)
            q2 = _stack_heads(q_ref[qrows, :].astype(MXU_DTYPE), lo)
            dom = (_stack_heads(dout, lo) * w2).astype(MXU_DTYPE)
            return (_dot(q2, k_ref[krows, :], "nt"), _dot(dom, v_ref[krows, :], "nt"), per_head(lse_p), w2 * big_d, q2, dom)

        def softmax_bwd(item, s, dp, lse2, wd2, q2, dom):
            _, (q0, q1, k0) = item
            p = jnp.where(_band_mask2(q0, q1, k0), jnp.exp(s * scale - lse2), 0.0)
            return p.astype(MXU_DTYPE), (p * (dp - wd2) * scale).astype(MXU_DTYPE), q2, dom

        def grads(item, p, ds, q2, dom):
            r, (q0, q1, k0) = item
            qrows, krows = _class_rows(r, dil, q0, q1), _class_rows(r, dil, k0, q1)
            dq2 = _dot(ds, k_ref[krows, :])
            dq = jnp.where(lo, dq2[:tq], dq2[tq:])
            dq_ref[qrows, :] = dq + dq0_ref[qrows, :] if before else dq
            dk_ref[krows, :] += _dot(ds, q2, "tn")
            dv_ref[krows, :] += _dot(p, dom, "tn")

        sc, prev = scores(items[0]), None
        for i, item in enumerate(items):
            sc_next = scores(items[i + 1]) if i + 1 < len(items) else None
            if prev is not None:
                grads(items[i - 1], *prev)
            prev, sc = softmax_bwd(item, *sc), sc_next
        grads(items[-1], *prev)

    cat = _pair_col(HEADS * LANES)
    return pl.pallas_call(
        body, name=name, grid=(DIL_WIDTH // LANES,),
        in_specs=[_pair_col()] * 2 + [_pair_col(P_VD)] + [_pair_col()] * 2 + [cat] * 2 + [_pair_col()] * len(before),
        out_specs=[_pair_col()] * 3, out_shape=[jax.ShapeDtypeStruct((S, DIL_WIDTH), F32)] * 3,
        compiler_params=_params())(q, k, v, lse, lse_mix, o_cat, do_cat, *before)


def combine_fwd(name, outs, lses, o_cat, tm=512):
    S = outs[0].shape[0]

    def body(o1, o2, o3, l1, l2, l3, cat_in, cat_out, mix_ref):
        ls = [l1[...], l2[...], l3[...]]
        m = jnp.maximum(jnp.maximum(ls[0], ls[1]), ls[2])
        e = [jnp.exp(l - m) for l in ls]
        den = e[0] + e[1] + e[2]
        cat_out[...] = (e[0] / den) * o1[...] + (e[1] / den) * o2[...] + (e[2] / den) * o3[...]
        mix_ref[...] = m + jnp.log(den)

    row = pl.BlockSpec((tm, DIL_WIDTH), lambda i: (i, 0))
    return pl.pallas_call(
        body, name=name, grid=(S // tm,), in_specs=[row] * 6 + [pl.BlockSpec(memory_space=pl.ANY)],
        out_specs=[pl.BlockSpec((tm, DIL_WIDTH), lambda i: (i, HEADS * LANES // DIL_WIDTH)), row],
        out_shape=[jax.ShapeDtypeStruct(o_cat.shape, F32), jax.ShapeDtypeStruct((S, DIL_WIDTH), F32)],
        input_output_aliases={6: 0}, compiler_params=_params())(*outs, *lses, o_cat)


FFN_FWD_ROWS = 512
FFN_BWD_ROWS = 256
CONV_PAD = SUBLANES


def _window(x, k):
    groups = x.reshape(-1, SUBLANES, x.shape[1])
    turned = pltpu.roll(groups, SUBLANES - k, axis=1)
    stays = lax.broadcasted_iota(jnp.int32, (groups.shape[0] - 1,) + groups.shape[1:], 1) < SUBLANES - k
    return jnp.where(stays, turned[:-1], turned[1:]).reshape(-1, x.shape[1])


def _earlier(ref, r0, rows, n):
    if r0 == 0:
        x = jnp.concatenate([jnp.zeros((SUBLANES, ref.shape[1]), F32), ref[:rows, :]], axis=0)
    else:
        x = ref[r0 - SUBLANES:r0 + rows, :]
    return _window(x, SUBLANES - n)


CONV_TC = 256
CONV_NB = D_FF // CONV_TC


def _half_specs(rows, rows_axis=False):
    if rows_axis:
        return [pl.BlockSpec((rows, D_MODEL), lambda j: (j, 0)), pl.BlockSpec((rows, D_MODEL), lambda j: (j + CONV_NB, 0))]
    return [pl.BlockSpec((rows, CONV_TC), lambda j: (0, j)), pl.BlockSpec((rows, CONV_TC), lambda j: (0, j + CONV_NB))]


def _whole(a):
    return pl.BlockSpec(a.shape, lambda j: (0,) * a.ndim)


def _up_pair(h, ug_ref, uv_ref):
    return jnp.concatenate([_dot(h, ug_ref[...], "nt"), _dot(h, uv_ref[...], "nt")], axis=1)


def _conv_taps(up_ref, r0, rows, w, b):
    uin, u1, u2 = up_ref[r0:r0 + rows, :], _earlier(up_ref, r0, rows, 1), _earlier(up_ref, r0, rows, 2)
    return uin, u1, u2, w[2:3, :] * uin + w[1:2, :] * u1 + w[0:1, :] * u2 + b


def ffn_fwd(name, h, w_up_t, w_conv, b_conv, w_down):
    S = h.shape[0]

    def body(h_ref, ug_ref, uv_ref, wg_ref, wv_ref, bg_ref, bv_ref, wd_ref, dn_ref, up_ref):
        @pl.when(pl.program_id(0) == 0)
        def _():
            dn_ref[...] = jnp.zeros_like(dn_ref)

        w = jnp.concatenate([wg_ref[...], wv_ref[...]], axis=1)
        b = jnp.concatenate([bg_ref[...], bv_ref[...]], axis=1)
        rows = FFN_FWD_ROWS
        starts = list(range(0, S, rows))

        def project(r0):
            up_ref[r0:r0 + rows, :] = _up_pair(h_ref[r0:r0 + rows, :], ug_ref, uv_ref)

        def gate(r0):
            u = _conv_taps(up_ref, r0, rows, w, b)[3]
            return (_silu(u[:, :CONV_TC]) * u[:, CONV_TC:]).astype(MXU_DTYPE)

        def project_down(r0, act):
            dn_ref[r0:r0 + rows, :] += _dot(act, wd_ref[...])

        project(starts[0])
        act_prev = None
        for i, r0 in enumerate(starts):
            if i + 1 < len(starts):
                project(starts[i + 1])
            if act_prev is not None:
                project_down(starts[i - 1], act_prev)
            act_prev = gate(r0)
        project_down(starts[-1], act_prev)

    return pl.pallas_call(
        body, name=name, grid=(CONV_NB,),
        in_specs=[_whole(h)] + _half_specs(CONV_TC, rows_axis=True) + _half_specs(3) + _half_specs(1)
        + [pl.BlockSpec((CONV_TC, w_down.shape[1]), lambda j: (j, 0))],
        out_specs=[pl.BlockSpec((S, w_down.shape[1]), lambda j: (0, 0)), pl.BlockSpec((S, 2 * CONV_TC), lambda j: (0, j))],
        out_shape=[jax.ShapeDtypeStruct((S, w_down.shape[1]), F32), jax.ShapeDtypeStruct((S, 2 * D_FF), F32)],
        compiler_params=_params())(h, w_up_t, w_up_t, w_conv, w_conv, b_conv, b_conv, w_down)


def ffn_bwd(name, h, up, w_up_t, w_conv, b_conv, d_dn, w_down):
    S, D = h.shape

    def body(h_ref, up_ref, ug_ref, uv_ref, wg_ref, wv_ref, bg_ref, bv_ref, dd_ref, wd_ref,
             dh_ref, gup_ref, gd_ref, dwg_ref, dwv_ref, dbg_ref, dbv_ref, du_ref, dup_ref, act_ref):
        @pl.when(pl.program_id(0) == 0)
        def _():
            dh_ref[...] = jnp.zeros_like(dh_ref)

        w = jnp.concatenate([wg_ref[...], wv_ref[...]], axis=1)
        b = jnp.concatenate([bg_ref[...], bv_ref[...]], axis=1)
        w_pair = jnp.concatenate([ug_ref[...], uv_ref[...]], axis=0)
        rows = FFN_BWD_ROWS
        starts = list(range(0, S, rows))
        du_ref[S:S + CONV_PAD, :] = jnp.zeros((CONV_PAD, 2 * CONV_TC), F32)

        def project(r0):
            return _dot(dd_ref[r0:r0 + rows, :], wd_ref[...], "nt")

        def through_conv(r0, da):
            uin, u1, u2, u = _conv_taps(up_ref, r0, rows, w, b)
            gate, val = u[:, :CONV_TC], u[:, CONV_TC:]
            sig = 1.0 / (1.0 + jnp.exp(-gate))
            du = jnp.concatenate([da * val * (sig * (1.0 + gate * (1.0 - sig))), da * (gate * sig)], axis=1)
            du_ref[r0:r0 + rows, :] = du
            act_ref[r0:r0 + rows, :] = (gate * sig * val).astype(MXU_DTYPE)
            dw = jnp.concatenate([_colsum(du * u2), _colsum(du * u1), _colsum(du * uin)], axis=0)
            return dw, _colsum(du)

        def back_up(r0):
            du = du_ref[r0:r0 + rows + CONV_PAD, :]
            dup = (w[2:3, :] * du[:rows] + w[1:2, :] * _window(du, 1) + w[0:1, :] * _window(du, 2)).astype(MXU_DTYPE)
            dup_ref[r0:r0 + rows, :] = dup
            dh_ref[r0:r0 + rows, :] += _dot(dup, w_pair)

        dw, db = 0.0, 0.0
        da = project(starts[0])
        for i, r0 in enumerate(starts):
            da_next = project(starts[i + 1]) if i + 1 < len(starts) else None
            dw_c, db_c = through_conv(r0, da)
            if i > 0:
                back_up(starts[i - 1])
            dw, db, da = dw + dw_c, db + db_c, da_next
        back_up(starts[-1])
        g_up, g_dn = _dot(dup_ref[...], h_ref[...], "tn"), _dot(act_ref[...], dd_ref[...], "tn")
        gup_ref[0], gup_ref[1] = g_up[:CONV_TC].astype(gup_ref.dtype), g_up[CONV_TC:].astype(gup_ref.dtype)
        gd_ref[...] = g_dn.astype(gd_ref.dtype)
        dwg_ref[...], dwv_ref[...] = dw[:, :CONV_TC], dw[:, CONV_TC:]
        dbg_ref[...], dbv_ref[...] = db[:, :CONV_TC], db[:, CONV_TC:]

    half = lambda rows: pl.BlockSpec((rows, CONV_TC), lambda j: (0, j))
    rows_blk = pl.BlockSpec((CONV_TC, D), lambda j: (j, 0))
    dh, gup, gd, dwg, dwv, dbg, dbv = pl.pallas_call(
        body, name=name, grid=(CONV_NB,),
        in_specs=[_whole(h), pl.BlockSpec((S, 2 * CONV_TC), lambda j: (0, j))] + _half_specs(CONV_TC, rows_axis=True) + _half_specs(3)
        + _half_specs(1) + [_whole(d_dn), rows_blk],
        out_specs=[pl.BlockSpec((S, D), lambda j: (0, 0)), pl.BlockSpec((2, CONV_TC, D), lambda j: (0, j, 0)), rows_blk,
                   half(3), half(3), half(1), half(1)],
        out_shape=[jax.ShapeDtypeStruct((S, D), F32), jax.ShapeDtypeStruct((2, D_FF, D), MXU_DTYPE),
                   jax.ShapeDtypeStruct((D_FF, D), MXU_DTYPE)]
        + [jax.ShapeDtypeStruct((3, D_FF), F32)] * 2 + [jax.ShapeDtypeStruct((1, D_FF), F32)] * 2,
        scratch_shapes=[pltpu.VMEM((S + CONV_PAD, 2 * CONV_TC), F32), pltpu.VMEM((S, 2 * CONV_TC), MXU_DTYPE),
                        pltpu.VMEM((S, CONV_TC), MXU_DTYPE)],
        compiler_params=_params())(h, up, w_up_t, w_up_t, w_conv, w_conv, b_conv, b_conv, d_dn, w_down)
    return dh, gup.reshape(2 * D_FF, D), gd, jnp.concatenate([dwg, dwv], axis=1), jnp.concatenate([dbg, dbv], axis=1)


def adamw(name, w, parts, m, v, tr=None):
    apart = w.ndim == 3
    R, C = w.shape[0], w.shape[-1]
    tr = tr or R
    assert R % tr == 0
    c1 = 1.0 - ADAM_B1 ** ADAM_STEP
    c2 = 1.0 - ADAM_B2 ** ADAM_STEP
    np_ = len(parts)

    def body(*refs):
        w_ref, m_ref, v_ref = refs[0], refs[1 + np_], refs[2 + np_]
        go_ref, d_ref, mo_ref, vo_ref = refs[3 + np_:]
        terms = []
        for part, ref in zip(parts, refs[1:1 + np_], strict=True):
            terms += [ref[...]] if part.ndim == 2 else [ref[p] for p in range(part.shape[0])]
        g = terms[0].astype(F32)
        for term in terms[1:]:
            g = g + term.astype(F32)
        m2 = ADAM_B1 * m_ref[...] + (1.0 - ADAM_B1) * g
        v2 = ADAM_B2 * v_ref[...] + (1.0 - ADAM_B2) * (g * g)
        go_ref[...] = g
        mo_ref[...] = m2
        vo_ref[...] = v2
        d_ref[...] = -ADAM_LR * ((m2 / c1) / (jnp.sqrt(v2 / c2) + ADAM_EPS) + ADAM_WD * w_ref[...])

    blk = pl.BlockSpec((tr, C), lambda i: (i, 0))
    own = pl.BlockSpec((tr, None, C), lambda i: (i, 0, 0)) if apart else blk
    part_specs = [blk if p.ndim == 2 else pl.BlockSpec((p.shape[0], tr, C), lambda i: (0, i, 0)) for p in parts]
    return pl.pallas_call(
        body, name=name, grid=(R // tr,),
        in_specs=[own] + part_specs + [own, own], out_specs=[own] * 4,
        out_shape=[jax.ShapeDtypeStruct(w.shape, F32)] * 4, compiler_params=_params())(w, *parts, m, v)


def _place():
    return lax.axis_index("x"), lax.axis_index("y"), lax.axis_index("c")


def ada_modulation(name, c, w_ada, after=()):
    n_mod = w_ada.shape[1]

    def exchange(src_ref, dst_ref, send_sems, recv_sems):
        x, y, c_ = _place()
        me = 4 * x + 2 * y + c_
        copies = []
        for k in range(1, N_DEV):
            px, py, pc = x ^ (k >> 2), y ^ ((k >> 1) & 1), c_ ^ (k & 1)
            copies.append(pltpu.make_async_remote_copy(
                src_ref=src_ref, dst_ref=dst_ref.at[me], send_sem=send_sems.at[k - 1], recv_sem=recv_sems.at[k - 1],
                device_id=(px, py, pc), device_id_type=MESH_ID))
        for cp in copies:
            cp.start()
        for cp in copies:
            cp.wait_recv()
        for cp in copies:
            cp.wait_send()
        return me

    def body(c_ref, w_ref, *refs):
        sc_ref, mod_ref, c_all, send_c, recv_c, send_m, recv_m = refs[len(after):]
        me = exchange(c_ref, c_all, send_c, recv_c)
        c_all[me] = c_ref[...]
        sc = _silu(jnp.concatenate([c_all[p] for p in range(N_DEV)], axis=0))
        sc_ref[...] = sc.astype(sc_ref.dtype)
        mod_ref[me] = _dot(sc, w_ref[...])
        exchange(mod_ref.at[me], mod_ref, send_m, recv_m)

    vmem = pl.BlockSpec(memory_space=pltpu.VMEM)
    return pl.pallas_call(
        body, name=name, in_specs=[vmem, vmem] + [pl.BlockSpec(memory_space=pl.ANY)] * len(after), out_specs=[vmem, vmem],
        out_shape=[jax.ShapeDtypeStruct((N_DEV, c.shape[1]), MXU_DTYPE), jax.ShapeDtypeStruct((N_DEV, N_DEV, n_mod), F32)],
        scratch_shapes=[pltpu.VMEM((N_DEV, 1, c.shape[1]), F32)] + [pltpu.SemaphoreType.DMA((N_DEV - 1,))] * 4,
        compiler_params=pltpu.CompilerParams(has_side_effects=True, vmem_limit_bytes=VMEM_LIMIT))(c, w_ada, *after)


HBM_SPEC = pl.BlockSpec(memory_space=pltpu.HBM)
SEM_SPEC = pl.BlockSpec(memory_space=pltpu.SEMAPHORE)
DATAFLOW = pltpu.SideEffectType.DATAFLOW_SIDE_EFFECTING


def _exchange_copies(srcs, lands, send_sems, recv_sems, gather, first=0):
    x, y, c = _place()
    me = 4 * x + 2 * y + c
    out = []
    for t, (src, land) in enumerate(zip(srcs, lands, strict=True)):
        for k in range(1, N_DEV):
            px, py, pc = x ^ (k >> 2), y ^ ((k >> 1) & 1), c ^ (k & 1)
            sem = 7 * (first + t) + k - 1
            out.append((k, pltpu.make_async_remote_copy(
                src_ref=src if gather else src.at[4 * px + 2 * py + pc],
                dst_ref=land.at[me] if gather else land.at[k - 1],
                send_sem=send_sems.at[sem], recv_sem=recv_sems.at[sem],
                device_id=(px, py, pc), device_id_type=MESH_ID)))
    return out


def _own_copies(srcs, lands, send_sems, gather, first=0):
    x, y, c = _place()
    me = 4 * x + 2 * y + c
    total = send_sems.shape[0] // N_DEV
    return [pltpu.make_async_copy(src if gather else src.at[me], land.at[me] if gather else land.at[N_DEV - 1],
                                  send_sems.at[7 * total + first + t])
            for t, (src, land) in enumerate(zip(srcs, lands, strict=True))]


TREE_DIRECT = (1, 2, 4, 6)
TREE_FORWARDED = (3, 5, 7)


def exchange_start(name, arrs, gather, after=None, tree=False):
    n = len(arrs)
    lands = [lax.empty((N_DEV,) + (a.shape if gather else a.shape[1:]), a.dtype) for a in arrs]
    deps = [] if after is None else [after]

    def body(*refs):
        srcs, land_refs = refs[:n], refs[n:2 * n]
        send_sems, recv_sems = refs[2 * n + len(deps)], refs[2 * n + len(deps) + 1]
        token = refs[-1]
        for k, cp in _exchange_copies(srcs, land_refs, send_sems, recv_sems, gather):
            if not tree or k in TREE_DIRECT:
                cp.start()
        for cp in _own_copies(srcs, land_refs, send_sems, gather):
            cp.start()
        token[...] = jnp.zeros_like(token)

    hbm = lambda a: pltpu.HBM(a.shape, a.dtype)
    res = pl.pallas_call(
        body, name=name,
        out_shape=(pltpu.SemaphoreType.DMA((N_DEV * n,)), pltpu.SemaphoreType.DMA((7 * n,)), *[hbm(a) for a in arrs],
                   *[hbm(l) for l in lands], jax.ShapeDtypeStruct((8, 128), F32)),
        in_specs=[HBM_SPEC] * (2 * n) + [pl.BlockSpec(memory_space=pl.ANY)] * len(deps),
        out_specs=(SEM_SPEC, SEM_SPEC, *[HBM_SPEC] * (2 * n), pl.BlockSpec(memory_space=pltpu.VMEM)),
        input_output_aliases={i: 2 + i for i in range(2 * n)},
        compiler_params=pltpu.CompilerParams(has_side_effects=DATAFLOW),
    )(*[pltpu.with_memory_space_constraint(a, pltpu.HBM) for a in arrs + lands], *deps)
    return res[0], res[1], list(res[2:2 + n]), list(res[2 + n:2 + 2 * n]), res[-1]


def exchange_forward(name, started, after, first=0, count=None, then_start=()):
    send_sems, recv_sems, srcs, lands, _ = started
    count = len(srcs) - first if count is None else count
    mine = lands[first:first + count]
    n = len(mine)

    def copies(land_refs, send_ref, recv_ref):
        x, y, c = _place()
        out = []
        for t, land in enumerate(land_refs):
            for k in (2, 4, 6):
                slot = land.at[4 * (x ^ (k >> 2)) + 2 * (y ^ ((k >> 1) & 1)) + c]
                came, goes = 7 * (first + t) + k - 1, 7 * (first + t) + (k ^ 1) - 1
                out.append((
                    pltpu.make_async_remote_copy(src_ref=slot, dst_ref=slot, send_sem=send_ref.at[came], recv_sem=recv_ref.at[came],
                                                 device_id=(x, y, c), device_id_type=MESH_ID),
                    pltpu.make_async_remote_copy(src_ref=slot, dst_ref=slot, send_sem=send_ref.at[goes], recv_sem=recv_ref.at[goes],
                                                 device_id=(x, y, 1 - c), device_id_type=MESH_ID)))
        return out

    after = list(after) if isinstance(after, (list, tuple)) else [after]

    def arrived(*refs):
        for came, _ in copies(refs[:n], refs[n], refs[n + 1]):
            came.wait_recv()

    nxt = list(then_start)
    m = len(nxt)
    nxt_lands = [lax.empty((N_DEV,) + a.shape, a.dtype) for a in nxt]
    n_in = n + 2 + 2 * m

    def pass_on(*refs):
        for _, goes in copies(refs[:n], refs[n], refs[n + 1]):
            goes.start()
        token = refs[n_in + n]
        if m:
            srcs2, lands2, send2, recv2 = refs[n + 2:n + 2 + m], refs[n + 2 + m:n_in], refs[n_in + n + 1], refs[n_in + n + 2]
            for k, cp in _exchange_copies(srcs2, lands2, send2, recv2, True):
                if k in TREE_DIRECT:
                    cp.start()
            for cp in _own_copies(srcs2, lands2, send2, True):
                cp.start()
        token[...] = jnp.zeros_like(token)

    hbm = lambda a: pltpu.HBM(a.shape, a.dtype)
    here = pl.pallas_call(
        arrived, name=name + "_arrived", out_shape=tuple(hbm(a) for a in mine),
        in_specs=[HBM_SPEC] * n + [SEM_SPEC, SEM_SPEC] + [pl.BlockSpec(memory_space=pl.ANY)] * len(after),
        out_specs=tuple([HBM_SPEC] * n), input_output_aliases={i: i for i in range(n)},
        compiler_params=pltpu.CompilerParams(has_side_effects=DATAFLOW),
    )(*mine, send_sems, recv_sems, *after)
    more_shapes = (pltpu.SemaphoreType.DMA((N_DEV * m,)), pltpu.SemaphoreType.DMA((7 * m,)), *[hbm(a) for a in nxt + nxt_lands]) if m else ()
    more_specs = (SEM_SPEC, SEM_SPEC, *[HBM_SPEC] * (2 * m)) if m else ()
    res = pl.pallas_call(
        pass_on, name=name, out_shape=(*[hbm(a) for a in mine], jax.ShapeDtypeStruct((8, 128), F32), *more_shapes),
        in_specs=[HBM_SPEC] * n + [SEM_SPEC, SEM_SPEC] + [HBM_SPEC] * (2 * m),
        out_specs=(*[HBM_SPEC] * n, pl.BlockSpec(memory_space=pltpu.VMEM), *more_specs),
        input_output_aliases={**{i: i for i in range(n)}, **{n + 2 + j: n + 3 + j for j in range(2 * m)}},
        compiler_params=pltpu.CompilerParams(has_side_effects=DATAFLOW),
    )(*here, send_sems, recv_sems, *[pltpu.with_memory_space_constraint(a, pltpu.HBM) for a in nxt + nxt_lands])
    lands = lands[:first] + list(res[:n]) + lands[first + count:]
    passed = (send_sems, recv_sems, srcs, lands, res[n])
    if not m:
        return passed
    return passed, (res[n + 1], res[n + 2], list(res[n + 3:n + 3 + m]), list(res[n + 3 + m:n + 3 + 2 * m]), res[n])


def exchange_wait(name, started, gather, after, first=0, count=None, tree=False):
    send_sems, recv_sems, srcs, lands, _ = started
    count = len(srcs) - first if count is None else count
    srcs, lands = srcs[first:first + count], lands[first:first + count]
    n = len(srcs)

    def body(*refs):
        src_refs, land_refs = refs[:n], refs[n:2 * n]
        copies = _exchange_copies(src_refs, land_refs, refs[2 * n], refs[2 * n + 1], gather, first)
        for _, cp in copies:
            cp.wait_send()
        for k, cp in copies:
            if not tree or k in (1,) + TREE_FORWARDED:
                cp.wait_recv()
        for cp in _own_copies(src_refs, land_refs, refs[2 * n], gather, first):
            cp.wait()

    hbm = lambda a: pltpu.HBM(a.shape, a.dtype)
    res = pl.pallas_call(
        body, name=name, out_shape=tuple(hbm(a) for a in srcs + lands),
        in_specs=[HBM_SPEC] * (2 * n) + [SEM_SPEC, SEM_SPEC, pl.BlockSpec(memory_space=pl.ANY)],
        out_specs=tuple([HBM_SPEC] * (2 * n)), input_output_aliases={i: i for i in range(2 * n)},
        compiler_params=pltpu.CompilerParams(has_side_effects=DATAFLOW),
    )(*srcs, *lands, send_sems, recv_sems, after)
    return list(res[:n]), list(res[n:])


def _gather_cols(stack):
    p, k, n = stack.shape
    return stack.transpose(1, 0, 2).reshape(k, p * n)


def _scatter_cols(full):
    k, n = full.shape
    return full.reshape(k, N_DEV, n // N_DEV).transpose(1, 0, 2)


def _gather_rows(stack):
    p, r, n = stack.shape
    return stack.reshape(p * r, n)


def _scatter_rows(full):
    r, n = full.shape
    return full.reshape(N_DEV, r // N_DEV, n)


_IN_NAT = Q_LORA + KV_LORA
TRANSPOSED = ("w_in", "w_q_b", "w_up")
ROWS_APART = ("w_in", "w_conv")


def to_kernel_layout(name, w):
    if name == "w_in":
        z = lambda n: jnp.zeros((n, w.shape[1]), w.dtype)
        return jnp.concatenate([w[:_IN_NAT], z(KPE_LO), w[_IN_NAT:_IN_NAT + ROPE], z(LANES - KPE_LO - ROPE), w[_IN_NAT + ROPE:]], axis=0)
    if name == "w_q_b":
        return jnp.pad(w.reshape(HEADS, NOPE + ROPE, -1), ((0, 0), (0, LANES - NOPE - ROPE), (0, 0))).reshape(HEADS * LANES, -1)
    if name == "w_o":
        mla = jnp.pad(w[:HEADS * NOPE].reshape(HEADS, NOPE, -1), ((0, 0), (LANES - NOPE, 0), (0, 0))).reshape(HEADS * LANES, -1)
        return jnp.concatenate([mla, w[HEADS * NOPE:]], axis=0)
    return w


def from_kernel_layout(name, g):
    if name == "w_in":
        return jnp.concatenate([g[:_IN_NAT], g[P_KPE + KPE_LO:P_KPE + KPE_LO + ROPE], g[P_QD:]], axis=0)
    if name == "w_q_b":
        return g.reshape(HEADS, LANES, -1)[:, :NOPE + ROPE, :].reshape(HEADS * (NOPE + ROPE), -1)
    if name == "w_o":
        mla = g[:HEADS * LANES].reshape(HEADS, LANES, -1)[:, LANES - NOPE:, :].reshape(HEADS * NOPE, -1)
        return jnp.concatenate([mla, g[HEADS * LANES:]], axis=0)
    return g


SMALL_COLS = 1024
SMALL_ROWS = 24
SMALL_AT = {"loss": (0, 0, 1), "b_ada": (1, 0, 6 * D_MODEL), "g_mix_norm": (7, 0, D_MODEL), "g_q_lat": (8, 0, Q_LORA),
            "g_kv_lat": (9, 0, KV_LORA), "g_mla_q_nope": (10, 0, NOPE), "g_mla_q_pe": (10, 128, ROPE),
            "g_mla_k_nope": (10, 256, NOPE), "g_mla_k_pe": (10, 384, ROPE), "g_dil_q": (10, 512, DIL_DIM),
            "g_dil_k": (10, 640, DIL_DIM), "g_ffn_norm": (11, 0, D_MODEL), "b_conv": (12, 0, 2 * D_FF)}
SMALL_PARAMS = tuple(n for n in SMALL_AT if n != "loss")


def _pack_small(values):
    by_row = {}
    for name, (row, off, n) in SMALL_AT.items():
        by_row.setdefault(row, []).append((off, values[name].reshape(-1).astype(F32)))
    out = []
    for row in sorted(by_row):
        pieces, at = [], 0
        for off, v in sorted(by_row[row], key=lambda t: t[0]):
            pieces += [jnp.zeros((off - at,), F32), v]
            at = off + v.shape[0]
        flat = jnp.concatenate(pieces)
        nrows = -(-flat.shape[0] // SMALL_COLS)
        out.append(jnp.pad(flat, (0, nrows * SMALL_COLS - flat.shape[0])).reshape(nrows, SMALL_COLS))
    packed = jnp.concatenate(out, axis=0)
    return jnp.pad(packed, ((0, SMALL_ROWS - packed.shape[0]), (0, 0)))


def _adam(w, g, m, v):
    c1 = 1.0 - ADAM_B1 ** ADAM_STEP
    c2 = 1.0 - ADAM_B2 ** ADAM_STEP
    m2 = ADAM_B1 * m + (1.0 - ADAM_B1) * g
    v2 = ADAM_B2 * v + (1.0 - ADAM_B2) * (g * g)
    return -ADAM_LR * ((m2 / c1) / (jnp.sqrt(v2 / c2) + ADAM_EPS) + ADAM_WD * w), m2, v2


def adamw_small(name, stack, params):
    flat = [a for n in SMALL_PARAMS for a in params[n]]

    def body(stack_ref, *refs):
        ins, outs = refs[:len(flat)], refs[len(flat):]
        g_all = stack_ref[0]
        for p in range(1, N_DEV):
            g_all = g_all + stack_ref[p]
        outs[0][...] = g_all[0:1, 0:1]
        for i, pname in enumerate(SMALL_PARAMS):
            row, off, n = SMALL_AT[pname]
            w_ref, m_ref, v_ref = ins[3 * i:3 * i + 3]
            go_ref, d_ref, mo_ref, vo_ref = outs[1 + 4 * i:5 + 4 * i]
            for c0 in range(0, n, SMALL_COLS):
                cn = min(SMALL_COLS, n - c0)
                r = row + c0 // SMALL_COLS
                g = g_all[r:r + 1, off:off + cn]
                cols = (slice(None), slice(c0, c0 + cn))
                d, m2, v2 = _adam(w_ref[cols], g, m_ref[cols], v_ref[cols])
                go_ref[cols], d_ref[cols], mo_ref[cols], vo_ref[cols] = g, d, m2, v2

    whole = lambda a: pl.BlockSpec(a.shape, lambda: (0,) * a.ndim)
    out_shape = [jax.ShapeDtypeStruct((1, 1), F32)] + [jax.ShapeDtypeStruct(a.shape, F32) for n in SMALL_PARAMS for a in params[n][:1] * 4]
    res = pl.pallas_call(body, name=name, in_specs=[whole(stack)] + [whole(a) for a in flat],
                         out_specs=[pl.BlockSpec(s.shape, lambda s=s: (0,) * len(s.shape)) for s in out_shape],
                         out_shape=out_shape, compiler_params=_params())(stack, *flat)
    return res[0], {n: res[1 + 4 * i:5 + 4 * i] for i, n in enumerate(SMALL_PARAMS)}


def _local_step(x, pos, mod, target, w, fetch, emit, halfway=lambda after: None):
    S = SEQ
    sh1, sc1, g1, sh2, sc2, g2 = [mod[:, i * D_MODEL:(i + 1) * D_MODEL] for i in range(6)]
    zeros = lambda n: jnp.zeros((1, n), F32)
    g_q = jnp.concatenate([w["g_mla_q_nope"], w["g_mla_q_pe"], zeros(LANES - NOPE - ROPE)], axis=1)
    g_k = jnp.concatenate([w["g_mla_k_nope"], zeros(LANES - NOPE)], axis=1)
    g_kpe = jnp.concatenate([zeros(KPE_LO), w["g_mla_k_pe"], zeros(LANES - KPE_LO - ROPE)], axis=1)
    g_dq = jnp.concatenate([w["g_dil_q"]] * 2, axis=1)
    g_dk = jnp.concatenate([w["g_dil_k"]] * 2, axis=1)
    b_conv = w["b_conv"]

    def inv_freq(d):
        return jnp.power(ROPE_THETA, -2.0 * jnp.arange(d // 2, dtype=F32) / d)

    n_m, n_d = ROPE // 2, DIL_DIM // 2
    freqs = jnp.concatenate([inv_freq(ROPE), inv_freq(DIL_DIM), jnp.zeros((LANES - n_m - n_d,), F32)]).reshape(1, LANES)

    def tables_fn(rows, params):
        (p,), (f,) = rows, params
        c, s = jnp.cos(p * f), jnp.sin(p * f)
        one, zero = jnp.ones_like(c), jnp.zeros_like(c)
        mla = lambda t, fill: jnp.concatenate([fill[:, :KPE_LO], t[:, :n_m], t[:, :n_m], fill[:, :LANES - KPE_LO - ROPE]], axis=1)
        dil = lambda t: jnp.concatenate([t[:, n_m:n_m + n_d]] * 4, axis=1)
        return [mla(c, one), mla(s, zero), dil(c), dil(s)], []

    cos_m, sin_m, cos_d, sin_d = rowwise("rope_tables", tables_fn, [pos], [freqs], [(LANES, F32)] * 4)
    tables = [cos_m, sin_m, cos_d, sin_d]
    H_M, H_D = ROPE // 2, DIL_DIM // 2

    def ln1_fn(rows, params):
        (xv,), (g, sc, sh) = rows, params
        y, _, _ = _rms(xv, g)
        return [y * (1.0 + sc) + sh], []

    (h,) = rowwise("ln1_fwd", ln1_fn, [x], [w["g_mix_norm"], sc1, sh1], [(D_MODEL, MXU_DTYPE)], dep=sin_d)
    w_in = fetch("w_in", h)

    def proj_fn(rows, params):
        (hv, cm, sm, cd, sd), (w_t, gq, gkv, gkp, gdq, gdk) = rows, params
        pv = _dot(hv, w_t, "nt")
        kper = _rope(_grms(pv[:, P_KPE:P_QD], gkp, KPE_GROUPS)[0], cm, sm, H_M)
        qd = [_rope(_grms(c, gdq, DIL_GROUPS)[0], cd, sd, H_D) for c in _chunks(pv[:, P_QD:P_KD])]
        kd = [_rope(_grms(c, gdk, DIL_GROUPS)[0], cd, sd, H_D) for c in _chunks(pv[:, P_KD:P_VD])]
        return [pv, _rms(pv[:, P_QLAT:P_KVLAT], gq)[0], _rms(pv[:, P_KVLAT:P_KPE], gkv)[0], kper,
                jnp.concatenate(qd, axis=1), jnp.concatenate(kd, axis=1)], []

    post_params = [w["g_q_lat"], w["g_kv_lat"], g_kpe, g_dq, g_dk]
    proj, qln, kvn, kper, qd_r, kd_r = rowwise(
        "proj_fwd", proj_fn, [h] + tables, [w_in] + post_params,
        [(P_END, F32), (Q_LORA, MXU_DTYPE), (KV_LORA, MXU_DTYPE), (LANES, MXU_DTYPE)] + [(DIL_WIDTH, F32)] * 2, tm=256)
    w_q_b, w_kv_b = fetch("w_q_b", qln), fetch("w_kv_b", kvn)

    def mla_proj_fn(rows, params):
        (qlv, kvlv, kp, cm, sm), (wq_t, wkv, gq, gk) = rows, params
        qv, kvv = _dot(qlv, wq_t, "nt"), _dot(kvlv, wkv)
        value_lanes = _lane(kp.shape) >= NOPE
        qs, ks, vs = [], [], []
        for qc, kc in zip(_chunks(qv), _chunks(kvv), strict=True):
            qs.append(_rope(_grms(qc, gq, Q_GROUPS)[0], cm, sm, H_M))
            ks.append(_grms(kc, gk, K_GROUPS)[0] + kp)
            vs.append(jnp.where(value_lanes, kc, 0.0))
        return [qv, kvv] + [jnp.concatenate(t, axis=1) for t in (qs, ks, vs)], []

    q, kv, q_mla, k_mla, v_mla = rowwise(
        "mla_proj", mla_proj_fn, [qln, kvn, kper, cos_m, sin_m], [w_q_b, w_kv_b, g_q, g_k],
        [(HEADS * LANES, F32)] * 2 + [(HEADS * LANES, MXU_DTYPE)] * 3, tm=256)
    mla_scale = (NOPE + ROPE) ** -0.5
    o_cat, lse_mla = mla_fwd("mla_fwd", q_mla, k_mla, v_mla, mla_scale)
    passed = halfway(lse_mla)

    band = [band_fwd(f"band{dil}_fwd", qd_r, kd_r, proj, dil, dep=passed) for dil in DILATIONS]
    o_cat, lse_mix = combine_fwd("dil_combine", [b[0] for b in band], [b[1] for b in band], o_cat)
    w_o = fetch("w_o", o_cat)

    def mid_fn(rows, params):
        (ov, xv), (w_out, gate1, g, sc, sh) = rows, params
        mx = _dot(ov, w_out)
        x1 = xv + gate1 * mx
        y, _, _ = _rms(x1, g)
        return [mx, x1, y * (1.0 + sc) + sh], []

    mix, x1, h2 = rowwise("mix_fwd", mid_fn, [o_cat, x], [w_o, g1, w["g_ffn_norm"], sc2, sh2],
                          [(D_MODEL, F32), (D_MODEL, F32), (D_MODEL, MXU_DTYPE)], tm=256)
    w_up, w_conv, w_down = fetch("w_up", h2), fetch("w_conv", h2), fetch("w_down", h2)
    dn, up = ffn_fwd("ffn_fwd", h2, w_up, w_conv, b_conv, w_down)

    def final_fn(rows, params):
        (x1v, dnv, tgt), (gate2,) = rows, params
        r = x1v + gate2 * dnv - tgt
        dy = r * (1.0 / D_MODEL)
        loss = jnp.sum(_colsum(r * r), axis=-1, keepdims=True) * (0.5 / D_MODEL)
        return [dy, gate2 * dy], [loss, _colsum(dy * dnv)]

    dy, d_dn, loss, dg2 = rowwise("loss_head", final_fn, [x1, dn, target], [g2], [(D_MODEL, F32), (D_MODEL, MXU_DTYPE)],
                                  [1, D_MODEL])
    dh2, g_up, g_down, g_w_conv, g_b_conv = ffn_bwd("ffn_bwd", h2, up, w_up, w_conv, b_conv, d_dn, w_down)
    emit("w_down", g_down)
    emit("w_conv", g_w_conv)
    sent = emit("w_up", g_up)

    def mid_bwd_fn(rows, params):
        (dh2v, dyv, x1v, mx), (gate1, g, sc, w_out) = rows, params
        yn, n, rstd = _rms(x1v, g)
        dx_n, dg = _rms_bwd(dh2v * (1.0 + sc), n, rstd, g)
        dx1 = dyv + dx_n
        dm = gate1 * dx1
        return [dx1, dm, _dot(dm, w_out, "nt")], [dg, _colsum(dh2v * yn), _colsum(dh2v), _colsum(dx1 * mx)]

    dx1, dmix, do_cat, dg_ffn, dsc2, dsh2, dg1 = rowwise(
        "mid_bwd", mid_bwd_fn, [dh2, dy, x1, mix], [g1, w["g_ffn_norm"], sc2, w_o],
        [(D_MODEL, F32), (D_MODEL, MXU_DTYPE), (w_o.shape[0], F32)], [D_MODEL] * 4, tm=256, dep=sent)

    sent = emit("w_o", matmul("mix_wgrad", o_cat, dmix, "tn", tm=512, out_dtype=MXU_DTYPE))
    dband = None
    for dil, b in zip(DILATIONS, band):
        dband = band_bwd(f"band{dil}_bwd", qd_r, kd_r, proj, b[1], lse_mix, o_cat, do_cat, dil, before=dband)
    dq_mla, dkv_mla, dkper = mla_bwd("mla_bwd", q_mla, k_mla, v_mla, o_cat, do_cat, lse_mla, mla_scale)

    def mla_prep_bwd_fn(rows, params):
        (dqv, dkvv, qv, kvv, cm, sm), (gq, gk) = rows, params
        nope_lanes = _lane(cm.shape) < NOPE
        dqs, dkvs, dgq, dgk = [], [], 0.0, 0.0
        for dqc, dkc, qc, kc in zip(_chunks(dqv), _chunks(dkvv), _chunks(qv), _chunks(kvv), strict=True):
            _, n, rstd = _grms(qc, gq, Q_GROUPS)
            dx, dg = _grms_bwd(_rope_bwd(dqc, cm, sm, H_M), n, rstd, gq, Q_GROUPS)
            dqs.append(dx)
            dgq = dgq + dg
            _, n, rstd = _grms(kc, gk, K_GROUPS)
            dx, dg = _grms_bwd(dkc, n, rstd, gk, K_GROUPS)
            dkvs.append(jnp.where(nope_lanes, dx, dkc))
            dgk = dgk + dg
        return [jnp.concatenate(dqs, axis=1), jnp.concatenate(dkvs, axis=1)], [dgq, dgk]

    dq, dkv, dg_q, dg_k = rowwise("mla_prep_bwd", mla_prep_bwd_fn, [dq_mla, dkv_mla, q, kv, cos_m, sin_m], [g_q, g_k],
                                  [(HEADS * LANES, MXU_DTYPE)] * 2, [LANES, LANES], tm=256, dep=sent)
    emit("w_q_b", matmul("q_wgrad", dq, qln, "tn", out_dtype=MXU_DTYPE))
    sent = emit("w_kv_b", matmul("kv_wgrad", kvn, dkv, "tn", out_dtype=MXU_DTYPE))

    def pre_bwd_fn(rows, params):
        dqv, dkvv, dkp, dqd_, dkd_, dvd_, pv, cm, sm, cd, sd = rows
        wq_t, wkv, gq, gkv, gkp, gdq, gdk = params
        dql, dkvl = _dot(dqv, wq_t), _dot(dkvv, wkv, "nt")
        r_q = _norm_bwd(dql, pv[:, P_QLAT:P_KVLAT], gq)
        r_kv = _norm_bwd(dkvl, pv[:, P_KVLAT:P_KPE], gkv)
        _, n, rstd = _grms(pv[:, P_KPE:P_QD], gkp, KPE_GROUPS)
        r_kp = _grms_bwd(_rope_bwd(dkp, cm, sm, H_M), n, rstd, gkp, KPE_GROUPS)
        outs, dgs = [r_q[0], r_kv[0], r_kp[0]], []
        for dval, lo, g in ((dqd_, P_QD, gdq), (dkd_, P_KD, gdk)):
            dg_sum = 0.0
            for dc, xc in zip(_chunks(dval), _chunks(pv[:, lo:lo + DIL_WIDTH]), strict=True):
                _, n, rstd = _grms(xc, g, DIL_GROUPS)
                dx, dg = _grms_bwd(_rope_bwd(dc, cd, sd, H_D), n, rstd, g, DIL_GROUPS)
                outs.append(dx)
                dg_sum = dg_sum + dg
            dgs.append(dg_sum)
        return [jnp.concatenate(outs + [dvd_], axis=1)], [r_q[1], r_kv[1], r_kp[1]] + dgs

    dproj, dg_q_lat, dg_kv_lat, dg_kpe, dg_dq, dg_dk = rowwise(
        "proj_pre_bwd", pre_bwd_fn,
        [dq, dkv, dkper] + list(dband) + [proj] + tables, [w_q_b, w_kv_b] + post_params,
        [(P_END, MXU_DTYPE)], [Q_LORA, KV_LORA, LANES, LANES, LANES], tm=256, dep=sent)
    sent = emit("w_in", matmul("proj_wgrad", dproj, h, "tn", tn=512, out_dtype=MXU_DTYPE))

    def ln1_bwd_fn(rows, params):
        (dpv, dres, xv), (w_t, g, sc) = rows, params
        dhv = _dot(dpv, w_t)
        yn, n, rstd = _rms(xv, g)
        dx_n, dg = _rms_bwd(dhv * (1.0 + sc), n, rstd, g)
        return [dres + dx_n], [dg, _colsum(dhv * yn), _colsum(dhv)]

    grad_x, dg_mix, dsc1, dsh1 = rowwise("proj_dgrad", ln1_bwd_fn, [dproj, dx1, x], [w_in, w["g_mix_norm"], sc1],
                                         [(D_MODEL, F32)], [D_MODEL] * 3, tm=256, dep=sent)
    dmod = jnp.concatenate([dsh1, dsc1, dg1, dsh2, dsc2, dg2], axis=-1)
    small = {"loss": loss, "b_ada": dmod, "g_mix_norm": dg_mix, "g_q_lat": dg_q_lat, "g_kv_lat": dg_kv_lat,
             "g_mla_q_nope": dg_q[:, :NOPE], "g_mla_q_pe": dg_q[:, NOPE:NOPE + ROPE], "g_mla_k_nope": dg_k[:, :NOPE],
             "g_mla_k_pe": dg_kpe[:, KPE_LO:KPE_LO + ROPE], "g_dil_q": dg_dq[:, :DIL_DIM] + dg_dq[:, DIL_DIM:],
             "g_dil_k": dg_dk[:, :DIL_DIM] + dg_dk[:, DIL_DIM:], "g_ffn_norm": dg_ffn,
             "b_conv": g_b_conv}
    return grad_x, small


COL_SHARDED = ("w_kv_b", "w_conv")
ROW_SHARDED = ("w_o", "w_down") + TRANSPOSED
ADAM_TILE = {"w_ada": 256, "w_up": 176, "w_down": 176}
GATHER_GROUPS = (("w_in",), ("w_q_b", "w_kv_b"), ("w_o",), ("w_up", "w_conv", "w_down"))
START_STAGES = ((0, 1), (2, 3))
FORWARD_STAGES = ((0, 1), (2,), (3,))
FORWARD_WITH = {"w_o": 2}
SCATTER_GROUPS = (("w_down", "w_conv", "w_up"), ("w_o",), ("w_q_b", "w_kv_b"), ("w_in",))
OUT_WEIGHTS = ("w_ada", "b_ada", "g_mix_norm", "w_in", "g_q_lat", "w_q_b", "g_kv_lat", "w_kv_b", "g_mla_q_nope", "g_mla_q_pe",
               "g_mla_k_nope", "g_mla_k_pe", "g_dil_q", "g_dil_k", "w_o", "g_ffn_norm", "w_up", "w_conv", "b_conv", "w_down")


def kernel(x, c, positions, w_ada, b_ada, g_mix_norm, w_in, g_q_lat, w_q_b, g_kv_lat, w_kv_b, g_mla_q_nope, g_mla_q_pe, g_mla_k_nope, g_mla_k_pe, g_dil_q, g_dil_k, w_o, g_ffn_norm, w_up, w_conv, b_conv, w_down, loss_target, m_w_ada, m_b_ada, m_g_mix_norm, m_w_in, m_g_q_lat, m_w_q_b, m_g_kv_lat, m_w_kv_b, m_g_mla_q_nope, m_g_mla_q_pe, m_g_mla_k_nope, m_g_mla_k_pe, m_g_dil_q, m_g_dil_k, m_w_o, m_g_ffn_norm, m_w_up, m_w_conv, m_b_conv, m_w_down, v_w_ada, v_b_ada, v_g_mix_norm, v_w_in, v_g_q_lat, v_w_q_b, v_g_kv_lat, v_w_kv_b, v_g_mla_q_nope, v_g_mla_q_pe, v_g_mla_k_nope, v_g_mla_k_pe, v_g_dil_q, v_g_dil_k, v_w_o, v_g_ffn_norm, v_w_up, v_w_conv, v_b_conv, v_w_down):
    args = dict(locals())
    xi, yi, ci = _place()
    me = 4 * xi + 2 * yi + ci
    def local(prefix, n):
        a = args[prefix + n]
        if n in ROWS_APART:
            return jnp.transpose(a, (2, 0, 1) if n in TRANSPOSED else (1, 0, 2))
        return a[0].T if n in TRANSPOSED else a[0]

    def as_output(n, r):
        if n in ROWS_APART:
            return jnp.transpose(r, (1, 2, 0) if n in TRANSPOSED else (1, 0, 2))
        return (r.T if n in TRANSPOSED else r)[None]

    shard = {n: local("", n) for n in COL_SHARDED + ROW_SHARDED + ("w_ada",)}
    flat = lambda n, a: a.reshape(a.shape[0], a.shape[-1]) if n in ROWS_APART else a
    small_w = {n: args[n] for n in SMALL_PARAMS}

    payload = {n: flat(n, shard[n]) if n == "w_conv" else flat(n, shard[n]).astype(MXU_DTYPE) for n in COL_SHARDED + ROW_SHARDED}
    start_order = [[n for i in groups for n in GATHER_GROUPS[i]] for groups in START_STAGES]

    sc_all, mod_all = ada_modulation("ada_mod", c, shard["w_ada"], after=[payload[n] for n in start_order[0]])

    exchange_of = lambda i: [e for e, groups in enumerate(START_STAGES) if i in groups][0]
    start_stage = lambda e, after: exchange_start(f"gather_start{e}", [payload[n] for n in start_order[e]], gather=True,
                                                  after=after, tree=True)
    gathered = {0: start_stage(0, mod_all)}
    after_start = gathered[0][-1]
    full, forwarded = {}, set()

    def forward(stage, after):
        e = exchange_of(FORWARD_STAGES[stage][0])
        if stage not in forwarded:
            forwarded.add(stage)
            first = start_order[e].index(GATHER_GROUPS[FORWARD_STAGES[stage][0]][0])
            count = sum(len(GATHER_GROUPS[i]) for i in FORWARD_STAGES[stage])
            starts_next = e + 1 < len(START_STAGES) and e + 1 not in gathered
            ready = [payload[n] for n in start_order[e + 1]] if starts_next else []
            if starts_next:
                gathered[e], gathered[e + 1] = exchange_forward(f"gather_forward{stage}", gathered[e], [after] + ready, first, count,
                                                                then_start=ready)
            else:
                gathered[e] = exchange_forward(f"gather_forward{stage}", gathered[e], [after], first, count)
        return gathered[e][-1]

    def fetch(name, after):
        if name not in full:
            (i, grp), = [(i, grp) for i, grp in enumerate(GATHER_GROUPS) if name in grp]
            (stage,) = [s for s, groups in enumerate(FORWARD_STAGES) if i in groups]
            forward(stage, after)
            if name in FORWARD_WITH:
                forward(FORWARD_WITH[name], after)
            e = exchange_of(i)
            behind = gathered[e + 1][-1] if e + 1 in gathered else after
            srcs, lands = exchange_wait(f"gather{i}_wait", gathered[e], True, behind, start_order[e].index(grp[0]), len(grp), tree=True)
            for n, stack in zip(grp, lands, strict=True):
                full[n] = to_kernel_layout(n, _gather_cols(stack) if n in COL_SHARDED else _gather_rows(stack))
        return full[name]

    mod_row = lax.dynamic_index_in_dim(mod_all, me, axis=1, keepdims=False).reshape(1, 6 * D_MODEL)
    (mod,) = rowwise("ada_bias", lambda rows, params: ([rows[0] + rows[1]], []), [mod_row, b_ada], [], [(6 * D_MODEL, F32)],
                     dep=after_start)

    pending, scatters = {}, {}

    def emit(name, grad):
        grad = from_kernel_layout(name, grad)
        pending[name] = _scatter_cols(grad) if name in COL_SHARDED else _scatter_rows(grad)
        for i, grp in enumerate(SCATTER_GROUPS):
            if name == grp[-1]:
                scatters[i] = exchange_start(f"scatter{i}_start", [pending[n] for n in grp], gather=False)
                return scatters[i][-1]
        return None

    pos = positions.reshape(SEQ, 1).astype(F32)
    grad_x, small = _local_step(x[0], pos, mod, loss_target[0], small_w, fetch, emit, halfway=lambda after: forward(1, after))

    small_sent = exchange_start("small_start", [_pack_small(small)], gather=True, after=grad_x)

    res, done = {}, small_sent[-1]
    for i, grp in enumerate(SCATTER_GROUPS):
        _, lands = exchange_wait(f"scatter{i}_wait", scatters[i], False, done)
        for n, land in zip(grp, lands, strict=True):
            res[n] = adamw(f"adamw_{n}", shard[n], [land], local("m_", n), local("v_", n), ADAM_TILE.get(n))
            done = res[n][0]
            res[n] = [as_output(n, r) for r in res[n]]
    _, (small_all,) = exchange_wait("small_wait", small_sent, True, done)
    loss, small_res = adamw_small("adamw_small", small_all, {n: (args[n], args["m_" + n], args["v_" + n]) for n in SMALL_PARAMS})
    row, _, n_mod = SMALL_AT["b_ada"]
    dmod_all = small_all[:, row:row + n_mod // SMALL_COLS, :].reshape(N_DEV, n_mod)
    dmod_mine = lax.dynamic_slice_in_dim(dmod_all, me * (6 * D_MODEL // N_DEV), 6 * D_MODEL // N_DEV, axis=1)
    g_w_ada = matmul("ada_wgrad", sc_all, dmod_mine, "tn")
    res["w_ada"] = [r[None] for r in adamw("adamw_w_ada", shard["w_ada"], [g_w_ada], m_w_ada[0], v_w_ada[0], ADAM_TILE["w_ada"])]

    def leaf(kind, n):
        return res[n][kind] if n in res else small_res[n][kind]

    return (loss.reshape(()), grad_x[None], *[leaf(k, n) for k in range(4) for n in OUT_WEIGHTS])
```

```python
import jax
import jax.numpy as jnp
from jax import lax
from jax.experimental import pallas as pl
from jax.experimental.pallas import tpu as pltpu

F32 = jnp.float32
MXU_DTYPE = jnp.bfloat16

N_DEV = 8
D_MODEL = 1024
SEQ = 2048
HEADS = 8
NOPE = 64
ROPE = 32
Q_LORA = 512
KV_LORA = 256
DIL_DIM = 64
DIL_WIDTH = HEADS * DIL_DIM
DILATIONS = (1, 4, 16)
SPAN = 128
D_FF = 2816
LANES = 128
SUBLANES = 8
ROPE_THETA = 10000.0
EPS = 1e-6
NEG_INF = -1e30
ADAM_LR, ADAM_B1, ADAM_B2, ADAM_EPS, ADAM_WD, ADAM_STEP = 0.001, 0.9, 0.999, 1e-08, 0.01, 10
VMEM_LIMIT = 56 * 1024 * 1024
MESH_ID = pl.DeviceIdType.MESH

P_QLAT, P_KVLAT, P_KPE, P_QD, P_KD, P_VD, P_END = 0, 512, 768, 896, 1408, 1920, 2432
KPE_LO = 64
MIX_IN = HEADS * LANES + DIL_WIDTH


def _params(**kw):
    return pltpu.CompilerParams(vmem_limit_bytes=VMEM_LIMIT, **kw)


def rowwise(name, fn, rows, params, out_rows, out_accs=(), tm=512, dep=None):
    deps = [] if dep is None else [dep]
    rows = [r if isinstance(r, tuple) else (r, r.shape[1], 0) for r in rows]
    R = rows[0][0].shape[0]
    tm = min(tm, R)
    steps = R // tm
    assert steps * tm == R
    in_specs = []
    for a, width, cb in rows:
        ri = a.shape[0]
        per = ri // tm
        assert per * tm == ri
        if ri == R:
            in_specs.append(pl.BlockSpec((tm, width), lambda i, cb=cb: (i, cb)))
        else:
            in_specs.append(pl.BlockSpec((tm, width), lambda i, per=per, cb=cb: (i % per, cb)))
    for p in params:
        in_specs.append(pl.BlockSpec(p.shape, lambda i: (0,) * p.ndim))
    in_specs += [pl.BlockSpec(memory_space=pl.ANY)] * len(deps)
    out_shape = [jax.ShapeDtypeStruct((R, d), dt) for d, dt in out_rows]
    out_specs = [pl.BlockSpec((tm, d), lambda i: (i, 0)) for d, _ in out_rows]
    out_shape += [jax.ShapeDtypeStruct((1, n), F32) for n in out_accs]
    out_specs += [pl.BlockSpec((1, n), lambda i: (0, 0)) for n in out_accs]
    nr, npar, no, na = len(rows), len(params), len(out_rows), len(out_accs)

    def body(*refs):
        rvals = [r[...] for r in refs[:nr]]
        pvals = [r[...] for r in refs[nr:nr + npar]]
        outs, accs = fn(rvals, pvals)
        first_out = nr + npar + len(deps)
        for ref, v in zip(refs[first_out:first_out + no], outs, strict=True):
            ref[...] = v.astype(ref.dtype)
        if na:
            acc_refs = refs[first_out + no:]
            i = pl.program_id(0)

            @pl.when(i == 0)
            def _():
                for ref, v in zip(acc_refs, accs, strict=True):
                    ref[...] = v

            @pl.when(i > 0)
            def _():
                for ref, v in zip(acc_refs, accs, strict=True):
                    ref[...] += v

    res = pl.pallas_call(body, name=name, grid=(steps,), in_specs=in_specs, out_specs=out_specs,
                         out_shape=out_shape, compiler_params=_params())(*[r[0] for r in rows], *params, *deps)
    return list(res)


_DIMS = {"nn": ((1,), (0,)), "nt": ((1,), (1,)), "tn": ((0,), (0,))}


def _dot(a, b, mode="nn"):
    return lax.dot_general(a.astype(MXU_DTYPE), b.astype(MXU_DTYPE), (_DIMS[mode], ((), ())),
                           preferred_element_type=F32)


def matmul(name, a, b, mode, tm=None, tn=None, tk=None, out_dtype=F32, dep=None):
    if mode == "tn":
        K, M = a.shape
    else:
        M, K = a.shape
    N = b.shape[0] if mode == "nt" else b.shape[1]
    tm, tn, tk = tm or M, tn or N, tk or K
    nm, nn, nk = M // tm, N // tn, K // tk
    assert nm * tm == M and nn * tn == N and nk * tk == K
    a_spec = pl.BlockSpec((tk, tm), lambda i, j, k: (k, i)) if mode == "tn" else pl.BlockSpec((tm, tk), lambda i, j, k: (i, k))
    b_spec = pl.BlockSpec((tn, tk), lambda i, j, k: (j, k)) if mode == "nt" else pl.BlockSpec((tk, tn), lambda i, j, k: (k, j))
    deps = [] if dep is None else [dep]

    def body(a_ref, b_ref, *rest):
        o_ref, scratch = rest[len(deps)], rest[len(deps) + 1:]
        p = _dot(a_ref[...], b_ref[...], mode)
        if nk == 1:
            o_ref[...] = p.astype(o_ref.dtype)
        else:
            acc = scratch[0]
            k = pl.program_id(2)

            @pl.when(k == 0)
            def _():
                acc[...] = p

            @pl.when(k > 0)
            def _():
                acc[...] += p

            @pl.when(k == nk - 1)
            def _():
                o_ref[...] = acc[...].astype(o_ref.dtype)

    return pl.pallas_call(
        body, name=name, grid=(nm, nn, nk), in_specs=[a_spec, b_spec] + [pl.BlockSpec(memory_space=pl.ANY)] * len(deps),
        out_specs=pl.BlockSpec((tm, tn), lambda i, j, k: (i, j)),
        out_shape=jax.ShapeDtypeStruct((M, N), out_dtype),
        scratch_shapes=[pltpu.VMEM((tm, tn), F32)] if nk > 1 else [],
        compiler_params=_params())(a, b, *deps)


def _rms(x, g):
    rstd = lax.rsqrt(jnp.mean(x * x, axis=-1, keepdims=True) + EPS)
    n = x * rstd
    return n * g, n, rstd


def _rms_bwd(dy, n, rstd, g):
    dg = jnp.sum(dy * n, axis=0, keepdims=True)
    dn = dy * g
    dx = rstd * (dn - n * jnp.mean(dn * n, axis=-1, keepdims=True))
    return dx, dg


def _norm_bwd(dy, x, g):
    _, n, rstd = _rms(x, g)
    return _rms_bwd(dy, n, rstd, g)


def _colsum(v):
    return jnp.sum(v, axis=0, keepdims=True)


def _silu(x):
    return x * (1.0 / (1.0 + jnp.exp(-x)))


def _lane(shape):
    return lax.broadcasted_iota(jnp.int32, shape, 1)


def _group_mean(v, groups):
    i = lax.broadcasted_iota(jnp.int32, (LANES, LANES), 0)
    j = lax.broadcasted_iota(jnp.int32, (LANES, LANES), 1)
    g = jnp.zeros((LANES, LANES), F32)
    for lo, hi in groups:
        g = jnp.where((i >= lo) & (i < hi) & (j >= lo) & (j < hi), 1.0 / (hi - lo), g)
    head = v.astype(MXU_DTYPE)
    return _dot(head, g) + _dot(v - head.astype(F32), g)


def _in_groups(shape, groups):
    lane = _lane(shape)
    m = jnp.zeros(shape, jnp.bool_)
    for lo, hi in groups:
        m = m | ((lane >= lo) & (lane < hi))
    return m


def _grms(x, g, groups):
    rstd = lax.rsqrt(_group_mean(x * x, groups) + EPS)
    n = jnp.where(_in_groups(x.shape, groups), x * rstd, 0.0)
    return n * g, n, rstd


def _grms_bwd(dy, n, rstd, g, groups):
    dn = dy * g
    return rstd * (dn - n * _group_mean(dn * n, groups)), _colsum(dy * n)


def _rot(x, half, transpose=False):
    first = (_lane(x.shape) % (2 * half)) < half
    up = pltpu.roll(x, LANES - half, axis=1)
    down = pltpu.roll(x, half, axis=1)
    return jnp.where(first, up, -down) if transpose else jnp.where(first, -up, down)


def _rope(x, cos, sin, half):
    return x * cos + _rot(x, half) * sin


def _rope_bwd(dy, cos, sin, half):
    return dy * cos + _rot(dy * sin, half, transpose=True)


def _chunks(x):
    return [x[:, i:i + LANES] for i in range(0, x.shape[1], LANES)]


Q_GROUPS = ((0, NOPE), (NOPE, NOPE + ROPE))
K_GROUPS = ((0, NOPE),)
KPE_GROUPS = ((KPE_LO, KPE_LO + ROPE),)
DIL_GROUPS = ((0, DIL_DIM), (DIL_DIM, 2 * DIL_DIM))


def _col(width, rows=SEQ):
    return pl.BlockSpec((rows, width), lambda h: (0, h))


def _causal_tail(s, tq, fill):
    diag = s[:, s.shape[1] - tq:]
    keep = lax.broadcasted_iota(jnp.int32, diag.shape, 1) <= lax.broadcasted_iota(jnp.int32, diag.shape, 0)
    diag = jnp.where(keep, diag, fill)
    return diag if s.shape[1] == tq else jnp.concatenate([s[:, :s.shape[1] - tq], diag], axis=1)


def mla_fwd(name, q, k, v, scale, tq=256):
    S = q.shape[0]

    def body(q_ref, k_ref, v_ref, o_ref, lse_ref):
        nb = S // tq
        blk = lambda i: slice(i * tq, (i + 1) * tq)

        def scores(i):
            return _dot(q_ref[blk(i), :], k_ref[:(i + 1) * tq, :], "nt")

        def softmax(i, s):
            s = _causal_tail(s * scale, tq, NEG_INF)
            m = jnp.max(s, axis=-1, keepdims=True)
            e = jnp.exp(s - m)
            l = jnp.sum(e, axis=-1, keepdims=True)
            lse_ref[0, blk(i), :] = m + jnp.log(l)
            return (e * (1.0 / l)).astype(MXU_DTYPE)

        def weighted(i, p):
            o_ref[blk(i), :] = _dot(p, v_ref[:(i + 1) * tq, :])

        s, p_prev = scores(0), None
        for i in range(nb):
            s_next = scores(i + 1) if i + 1 < nb else None
            if p_prev is not None:
                weighted(i - 1, p_prev)
            p_prev, s = softmax(i, s), s_next
        weighted(nb - 1, p_prev)

    return pl.pallas_call(
        body, name=name, grid=(HEADS,), in_specs=[_col(LANES)] * 3,
        out_specs=[_col(LANES), pl.BlockSpec((1, S, 1), lambda h: (h, 0, 0))],
        out_shape=[jax.ShapeDtypeStruct((S, MIX_IN), F32), jax.ShapeDtypeStruct((HEADS, S, 1), F32)],
        compiler_params=_params())(q, k, v)


def mla_bwd(name, q, k, v, o, do, lse, scale, tq=256):
    S = q.shape[0]

    def body(q_ref, k_ref, v_ref, o_ref, do_ref, lse_ref, dq_ref, dkv_ref, dkpe_ref, dk_acc, dv_acc):
        dk_acc[...] = jnp.zeros_like(dk_acc)
        dv_acc[...] = jnp.zeros_like(dv_acc)
        for i in range(S // tq):
            kext = (i + 1) * tq
            blk = slice(i * tq, kext)
            qi, kk, vv = q_ref[blk, :], k_ref[:kext, :], v_ref[:kext, :]
            doi = do_ref[blk, :]
            s = _causal_tail(_dot(qi, kk, "nt") * scale, tq, NEG_INF)
            p = jnp.exp(s - lse_ref[0, blk, :])
            dp = _dot(doi, vv, "nt")
            delta = jnp.sum(doi * o_ref[blk, :], axis=-1, keepdims=True)
            ds = p * (dp - delta) * scale
            dq_ref[blk, :] = _dot(ds, kk)
            dk_acc[:kext, :] += _dot(ds, qi, "tn")
            dv_acc[:kext, :] += _dot(p, doi, "tn")
        dk = dk_acc[...]
        lane = _lane(dk.shape)
        dkv_ref[...] = jnp.where(lane < NOPE, dk, 0.0) + dv_acc[...]
        dkpe = jnp.where((lane >= KPE_LO) & (lane < KPE_LO + ROPE), dk, 0.0)
        h = pl.program_id(0)

        @pl.when(h == 0)
        def _():
            dkpe_ref[...] = dkpe

        @pl.when(h > 0)
        def _():
            dkpe_ref[...] += dkpe

    return pl.pallas_call(
        body, name=name, grid=(HEADS,),
        in_specs=[_col(LANES)] * 5 + [pl.BlockSpec((1, S, 1), lambda h: (h, 0, 0))],
        out_specs=[_col(LANES), _col(LANES), pl.BlockSpec((S, LANES), lambda h: (0, 0))],
        out_shape=[jax.ShapeDtypeStruct((S, HEADS * LANES), F32), jax.ShapeDtypeStruct((S, HEADS * LANES), F32),
                   jax.ShapeDtypeStruct((S, LANES), F32)],
        scratch_shapes=[pltpu.VMEM((S, LANES), F32), pltpu.VMEM((S, LANES), F32)],
        compiler_params=_params())(q, k, v, o, do, lse)


BAND_TQ = SPAN


def _band_blocks(L, tq):
    return [(i * tq, (i + 1) * tq, max(0, i * tq - SPAN)) for i in range(L // tq)]


def _class_rows(r, dil, lo, hi):
    return pl.ds(r + dil * lo, hi - lo, stride=dil) if dil > 1 else pl.ds(lo, hi - lo)


def _stack_heads(t, lo):
    zero = jnp.zeros_like(t)
    return jnp.concatenate([jnp.where(lo, t, zero), jnp.where(lo, zero, t)], axis=0)


def _band_mask2(q0, q1, k0):
    n = q1 - q0
    shape = (2 * n, q1 - k0)
    i = lax.broadcasted_iota(jnp.int32, shape, 0)
    dist = (jnp.where(i >= n, i - n, i) + q0) - (lax.broadcasted_iota(jnp.int32, shape, 1) + k0)
    return (dist >= 0) & (dist <= SPAN)


def _pair_col(col0=0):
    return pl.BlockSpec((SEQ, LANES), lambda j: (0, col0 // LANES + j))


def band_fwd(name, q, k, v, dil, dep=None):
    S = q.shape[0]
    L = S // dil
    tq = BAND_TQ
    scale = DIL_DIM ** -0.5
    deps = [] if dep is None else [dep]

    def body(q_ref, k_ref, v_ref, *rest):
        o_ref, lse_ref = rest[len(deps):]
        items = [(r, blk) for r in range(dil) for blk in _band_blocks(L, tq)]
        lo = _lane((tq, LANES)) < DIL_DIM

        def scores(item):
            r, (q0, q1, k0) = item
            qb = q_ref[_class_rows(r, dil, q0, q1), :].astype(MXU_DTYPE)
            return _dot(_stack_heads(qb, lo), k_ref[_class_rows(r, dil, k0, q1), :], "nt")

        def softmax(item, s):
            _, (q0, q1, k0) = item
            s = jnp.where(_band_mask2(q0, q1, k0), s * scale, NEG_INF)
            mx = jnp.max(s, axis=-1, keepdims=True)
            e = jnp.exp(s - mx)
            l = jnp.sum(e, axis=-1, keepdims=True)
            return (e * (1.0 / l)).astype(MXU_DTYPE), mx + jnp.log(l)

        def weighted(item, p, lse):
            r, (q0, q1, k0) = item
            pv = _dot(p, v_ref[_class_rows(r, dil, k0, q1), :])
            o_ref[_class_rows(r, dil, q0, q1), :] = jnp.where(lo, pv[:tq], pv[tq:])
            lse_ref[_class_rows(r, dil, q0, q1), :] = jnp.where(lo, lse[:tq], lse[tq:])

        s, prev = scores(items[0]), None
        for i, item in enumerate(items):
            s_next = scores(items[i + 1]) if i + 1 < len(items) else None
            if prev is not None:
                weighted(items[i - 1], *prev)
            prev, s = softmax(item, s), s_next
        weighted(items[-1], *prev)

    return pl.pallas_call(
        body, name=name, grid=(DIL_WIDTH // LANES,),
        in_specs=[_pair_col()] * 2 + [_pair_col(P_VD)] + [pl.BlockSpec(memory_space=pl.ANY)] * len(deps), out_specs=[_pair_col()] * 2,
        out_shape=[jax.ShapeDtypeStruct((S, DIL_WIDTH), F32)] * 2, compiler_params=_params())(q, k, v, *deps)


def band_bwd(name, q, k, v, lse, lse_mix, o_cat, do_cat, dil, before=None):
    S = q.shape[0]
    L = S // dil
    tq = BAND_TQ
    scale = DIL_DIM ** -0.5
    before = list(before or [])

    def body(q_ref, k_ref, v_ref, lse_ref, mix_ref, o_ref, do_ref, *rest):
        dq_ref, dk_ref, dv_ref = rest[len(before):]
        if before:
            dq0_ref, dk0_ref, dv0_ref = rest[:3]
            dk_ref[...] = dk0_ref[...]
            dv_ref[...] = dv0_ref[...]
        else:
            dk_ref[...] = jnp.zeros_like(dk_ref)
            dv_ref[...] = jnp.zeros_like(dv_ref)
        items = [(r, blk) for r in range(dil) for blk in _band_blocks(L, tq)]
        lo = _lane((tq, LANES)) < DIL_DIM
        per_head = lambda t: jnp.concatenate([t[:, 0:1], t[:, DIL_DIM:DIL_DIM + 1]], axis=0)

        def scores(item):
            r, (q0, q1, k0) = item
            qrows, krows = _class_rows(r, dil, q0, q1), _class_rows(r, dil, k0, q1)
            lse_p, dout = lse_ref[qrows, :], do_ref[qrows, :]
            w2 = per_head(jnp.exp(lse_p - mix_ref[qrows, :]))
            dd = dout * o_ref[qrows, :]
            big_d = jnp.concatenate([jnp.sum(jnp.where(lo, dd, 0.0), axis=-1, keepdims=True),
                                     jnp.sum(jnp.where(lo, 0.0, dd), axis=-1, keepdims=True)], axis=0)
            q2 = _stack_heads(q_ref[qrows, :].astype(MXU_DTYPE), lo)
            dom = (_stack_heads(dout, lo) * w2).astype(MXU_DTYPE)
            return (_dot(q2, k_ref[krows, :], "nt"), _dot(dom, v_ref[krows, :], "nt"), per_head(lse_p), w2 * big_d, q2, dom)

        def softmax_bwd(item, s, dp, lse2, wd2, q2, dom):
            _, (q0, q1, k0) = item
            p = jnp.where(_band_mask2(q0, q1, k0), jnp.exp(s * scale - lse2), 0.0)
            return p.astype(MXU_DTYPE), (p * (dp - wd2) * scale).astype(MXU_DTYPE), q2, dom

        def grads(item, p, ds, q2, dom):
            r, (q0, q1, k0) = item
            qrows, krows = _class_rows(r, dil, q0, q1), _class_rows(r, dil, k0, q1)
            dq2 = _dot(ds, k_ref[krows, :])
            dq = jnp.where(lo, dq2[:tq], dq2[tq:])
            dq_ref[qrows, :] = dq + dq0_ref[qrows, :] if before else dq
            dk_ref[krows, :] += _dot(ds, q2, "tn")
            dv_ref[krows, :] += _dot(p, dom, "tn")

        sc, prev = scores(items[0]), None
        for i, item in enumerate(items):
            sc_next = scores(items[i + 1]) if i + 1 < len(items) else None
            if prev is not None:
                grads(items[i - 1], *prev)
            prev, sc = softmax_bwd(item, *sc), sc_next
        grads(items[-1], *prev)

    cat = _pair_col(HEADS * LANES)
    return pl.pallas_call(
        body, name=name, grid=(DIL_WIDTH // LANES,),
        in_specs=[_pair_col()] * 2 + [_pair_col(P_VD)] + [_pair_col()] * 2 + [cat] * 2 + [_pair_col()] * len(before),
        out_specs=[_pair_col()] * 3, out_shape=[jax.ShapeDtypeStruct((S, DIL_WIDTH), F32)] * 3,
        compiler_params=_params())(q, k, v, lse, lse_mix, o_cat, do_cat, *before)


def combine_fwd(name, outs, lses, o_cat, tm=512):
    S = outs[0].shape[0]

    def body(o1, o2, o3, l1, l2, l3, cat_in, cat_out, mix_ref):
        ls = [l1[...], l2[...], l3[...]]
        m = jnp.maximum(jnp.maximum(ls[0], ls[1]), ls[2])
        e = [jnp.exp(l - m) for l in ls]
        den = e[0] + e[1] + e[2]
        cat_out[...] = (e[0] / den) * o1[...] + (e[1] / den) * o2[...] + (e[2] / den) * o3[...]
        mix_ref[...] = m + jnp.log(den)

    row = pl.BlockSpec((tm, DIL_WIDTH), lambda i: (i, 0))
    return pl.pallas_call(
        body, name=name, grid=(S // tm,), in_specs=[row] * 6 + [pl.BlockSpec(memory_space=pl.ANY)],
        out_specs=[pl.BlockSpec((tm, DIL_WIDTH), lambda i: (i, HEADS * LANES // DIL_WIDTH)), row],
        out_shape=[jax.ShapeDtypeStruct(o_cat.shape, F32), jax.ShapeDtypeStruct((S, DIL_WIDTH), F32)],
        input_output_aliases={6: 0}, compiler_params=_params())(*outs, *lses, o_cat)


FFN_FWD_ROWS = 512
FFN_BWD_ROWS = 256
CONV_PAD = SUBLANES


def _window(x, k):
    groups = x.reshape(-1, SUBLANES, x.shape[1])
    turned = pltpu.roll(groups, SUBLANES - k, axis=1)
    stays = lax.broadcasted_iota(jnp.int32, (groups.shape[0] - 1,) + groups.shape[1:], 1) < SUBLANES - k
    return jnp.where(stays, turned[:-1], turned[1:]).reshape(-1, x.shape[1])


def _earlier(ref, r0, rows, n):
    if r0 == 0:
        x = jnp.concatenate([jnp.zeros((SUBLANES, ref.shape[1]), F32), ref[:rows, :]], axis=0)
    else:
        x = ref[r0 - SUBLANES:r0 + rows, :]
    return _window(x, SUBLANES - n)


CONV_TC = 256
CONV_NB = D_FF // CONV_TC


def _half_specs(rows, rows_axis=False):
    if rows_axis:
        return [pl.BlockSpec((rows, D_MODEL), lambda j: (j, 0)), pl.BlockSpec((rows, D_MODEL), lambda j: (j + CONV_NB, 0))]
    return [pl.BlockSpec((rows, CONV_TC), lambda j: (0, j)), pl.BlockSpec((rows, CONV_TC), lambda j: (0, j + CONV_NB))]


def _whole(a):
    return pl.BlockSpec(a.shape, lambda j: (0,) * a.ndim)


def _up_pair(h, ug_ref, uv_ref):
    return jnp.concatenate([_dot(h, ug_ref[...], "nt"), _dot(h, uv_ref[...], "nt")], axis=1)


def _conv_taps(up_ref, r0, rows, w, b):
    uin, u1, u2 = up_ref[r0:r0 + rows, :], _earlier(up_ref, r0, rows, 1), _earlier(up_ref, r0, rows, 2)
    return uin, u1, u2, w[2:3, :] * uin + w[1:2, :] * u1 + w[0:1, :] * u2 + b


def ffn_fwd(name, h, w_up_t, w_conv, b_conv, w_down):
    S = h.shape[0]

    def body(h_ref, ug_ref, uv_ref, wg_ref, wv_ref, bg_ref, bv_ref, wd_ref, dn_ref, up_ref):
        @pl.when(pl.program_id(0) == 0)
        def _():
            dn_ref[...] = jnp.zeros_like(dn_ref)

        w = jnp.concatenate([wg_ref[...], wv_ref[...]], axis=1)
        b = jnp.concatenate([bg_ref[...], bv_ref[...]], axis=1)
        rows = FFN_FWD_ROWS
        starts = list(range(0, S, rows))

        def project(r0):
            up_ref[r0:r0 + rows, :] = _up_pair(h_ref[r0:r0 + rows, :], ug_ref, uv_ref)

        def gate(r0):
            u = _conv_taps(up_ref, r0, rows, w, b)[3]
            return (_silu(u[:, :CONV_TC]) * u[:, CONV_TC:]).astype(MXU_DTYPE)

        def project_down(r0, act):
            dn_ref[r0:r0 + rows, :] += _dot(act, wd_ref[...])

        project(starts[0])
        act_prev = None
        for i, r0 in enumerate(starts):
            if i + 1 < len(starts):
                project(starts[i + 1])
            if act_prev is not None:
                project_down(starts[i - 1], act_prev)
            act_prev = gate(r0)
        project_down(starts[-1], act_prev)

    return pl.pallas_call(
        body, name=name, grid=(CONV_NB,),
        in_specs=[_whole(h)] + _half_specs(CONV_TC, rows_axis=True) + _half_specs(3) + _half_specs(1)
        + [pl.BlockSpec((CONV_TC, w_down.shape[1]), lambda j: (j, 0))],
        out_specs=[pl.BlockSpec((S, w_down.shape[1]), lambda j: (0, 0)), pl.BlockSpec((S, 2 * CONV_TC), lambda j: (0, j))],
        out_shape=[jax.ShapeDtypeStruct((S, w_down.shape[1]), F32), jax.ShapeDtypeStruct((S, 2 * D_FF), F32)],
        compiler_params=_params())(h, w_up_t, w_up_t, w_conv, w_conv, b_conv, b_conv, w_down)


def ffn_bwd(name, h, up, w_up_t, w_conv, b_conv, d_dn, w_down):
    S, D = h.shape

    def body(h_ref, up_ref, ug_ref, uv_ref, wg_ref, wv_ref, bg_ref, bv_ref, dd_ref, wd_ref,
             dh_ref, gup_ref, gd_ref, dwg_ref, dwv_ref, dbg_ref, dbv_ref, du_ref, dup_ref, act_ref):
        @pl.when(pl.program_id(0) == 0)
        def _():
            dh_ref[...] = jnp.zeros_like(dh_ref)

        w = jnp.concatenate([wg_ref[...], wv_ref[...]], axis=1)
        b = jnp.concatenate([bg_ref[...], bv_ref[...]], axis=1)
        w_pair = jnp.concatenate([ug_ref[...], uv_ref[...]], axis=0)
        rows = FFN_BWD_ROWS
        starts = list(range(0, S, rows))
        du_ref[S:S + CONV_PAD, :] = jnp.zeros((CONV_PAD, 2 * CONV_TC), F32)

        def project(r0):
            return _dot(dd_ref[r0:r0 + rows, :], wd_ref[...], "nt")

        def through_conv(r0, da):
            uin, u1, u2, u = _conv_taps(up_ref, r0, rows, w, b)
            gate, val = u[:, :CONV_TC], u[:, CONV_TC:]
            sig = 1.0 / (1.0 + jnp.exp(-gate))
            du = jnp.concatenate([da * val * (sig * (1.0 + gate * (1.0 - sig))), da * (gate * sig)], axis=1)
            du_ref[r0:r0 + rows, :] = du
            act_ref[r0:r0 + rows, :] = (gate * sig * val).astype(MXU_DTYPE)
            dw = jnp.concatenate([_colsum(du * u2), _colsum(du * u1), _colsum(du * uin)], axis=0)
            return dw, _colsum(du)

        def back_up(r0):
            du = du_ref[r0:r0 + rows + CONV_PAD, :]
            dup = (w[2:3, :] * du[:rows] + w[1:2, :] * _window(du, 1) + w[0:1, :] * _window(du, 2)).astype(MXU_DTYPE)
            dup_ref[r0:r0 + rows, :] = dup
            dh_ref[r0:r0 + rows, :] += _dot(dup, w_pair)

        dw, db = 0.0, 0.0
        da = project(starts[0])
        for i, r0 in enumerate(starts):
            da_next = project(starts[i + 1]) if i + 1 < len(starts) else None
            dw_c, db_c = through_conv(r0, da)
            if i > 0:
                back_up(starts[i - 1])
            dw, db, da = dw + dw_c, db + db_c, da_next
        back_up(starts[-1])
        g_up, g_dn = _dot(dup_ref[...], h_ref[...], "tn"), _dot(act_ref[...], dd_ref[...], "tn")
        gup_ref[0], gup_ref[1] = g_up[:CONV_TC].astype(gup_ref.dtype), g_up[CONV_TC:].astype(gup_ref.dtype)
        gd_ref[...] = g_dn.astype(gd_ref.dtype)
        dwg_ref[...], dwv_ref[...] = dw[:, :CONV_TC], dw[:, CONV_TC:]
        dbg_ref[...], dbv_ref[...] = db[:, :CONV_TC], db[:, CONV_TC:]

    half = lambda rows: pl.BlockSpec((rows, CONV_TC), lambda j: (0, j))
    rows_blk = pl.BlockSpec((CONV_TC, D), lambda j: (j, 0))
    dh, gup, gd, dwg, dwv, dbg, dbv = pl.pallas_call(
        body, name=name, grid=(CONV_NB,),
        in_specs=[_whole(h), pl.BlockSpec((S, 2 * CONV_TC), lambda j: (0, j))] + _half_specs(CONV_TC, rows_axis=True) + _half_specs(3)
        + _half_specs(1) + [_whole(d_dn), rows_blk],
        out_specs=[pl.BlockSpec((S, D), lambda j: (0, 0)), pl.BlockSpec((2, CONV_TC, D), lambda j: (0, j, 0)), rows_blk,
                   half(3), half(3), half(1), half(1)],
        out_shape=[jax.ShapeDtypeStruct((S, D), F32), jax.ShapeDtypeStruct((2, D_FF, D), MXU_DTYPE),
                   jax.ShapeDtypeStruct((D_FF, D), MXU_DTYPE)]
        + [jax.ShapeDtypeStruct((3, D_FF), F32)] * 2 + [jax.ShapeDtypeStruct((1, D_FF), F32)] * 2,
        scratch_shapes=[pltpu.VMEM((S + CONV_PAD, 2 * CONV_TC), F32), pltpu.VMEM((S, 2 * CONV_TC), MXU_DTYPE),
                        pltpu.VMEM((S, CONV_TC), MXU_DTYPE)],
        compiler_params=_params())(h, up, w_up_t, w_up_t, w_conv, w_conv, b_conv, b_conv, d_dn, w_down)
    return dh, gup.reshape(2 * D_FF, D), gd, jnp.concatenate([dwg, dwv], axis=1), jnp.concatenate([dbg, dbv], axis=1)


def adamw(name, w, parts, m, v, tr=None):
    apart = w.ndim == 3
    R, C = w.shape[0], w.shape[-1]
    tr = tr or R
    assert R % tr == 0
    c1 = 1.0 - ADAM_B1 ** ADAM_STEP
    c2 = 1.0 - ADAM_B2 ** ADAM_STEP
    np_ = len(parts)

    def body(*refs):
        w_ref, m_ref, v_ref = refs[0], refs[1 + np_], refs[2 + np_]
        go_ref, d_ref, mo_ref, vo_ref = refs[3 + np_:]
        terms = []
        for part, ref in zip(parts, refs[1:1 + np_], strict=True):
            terms += [ref[...]] if part.ndim == 2 else [ref[p] for p in range(part.shape[0])]
        g = terms[0].astype(F32)
        for term in terms[1:]:
            g = g + term.astype(F32)
        m2 = ADAM_B1 * m_ref[...] + (1.0 - ADAM_B1) * g
        v2 = ADAM_B2 * v_ref[...] + (1.0 - ADAM_B2) * (g * g)
        go_ref[...] = g
        mo_ref[...] = m2
        vo_ref[...] = v2
        d_ref[...] = -ADAM_LR * ((m2 / c1) / (jnp.sqrt(v2 / c2) + ADAM_EPS) + ADAM_WD * w_ref[...])

    blk = pl.BlockSpec((tr, C), lambda i: (i, 0))
    own = pl.BlockSpec((tr, None, C), lambda i: (i, 0, 0)) if apart else blk
    part_specs = [blk if p.ndim == 2 else pl.BlockSpec((p.shape[0], tr, C), lambda i: (0, i, 0)) for p in parts]
    return pl.pallas_call(
        body, name=name, grid=(R // tr,),
        in_specs=[own] + part_specs + [own, own], out_specs=[own] * 4,
        out_shape=[jax.ShapeDtypeStruct(w.shape, F32)] * 4, compiler_params=_params())(w, *parts, m, v)


def _place():
    return lax.axis_index("x"), lax.axis_index("y"), lax.axis_index("c")


def ada_modulation(name, c, w_ada, after=()):
    n_mod = w_ada.shape[1]

    def exchange(src_ref, dst_ref, send_sems, recv_sems):
        x, y, c_ = _place()
        me = 4 * x + 2 * y + c_
        copies = []
        for k in range(1, N_DEV):
            px, py, pc = x ^ (k >> 2), y ^ ((k >> 1) & 1), c_ ^ (k & 1)
            copies.append(pltpu.make_async_remote_copy(
                src_ref=src_ref, dst_ref=dst_ref.at[me], send_sem=send_sems.at[k - 1], recv_sem=recv_sems.at[k - 1],
                device_id=(px, py, pc), device_id_type=MESH_ID))
        for cp in copies:
            cp.start()
        for cp in copies:
            cp.wait_recv()
        for cp in copies:
            cp.wait_send()
        return me

    def body(c_ref, w_ref, *refs):
        sc_ref, mod_ref, c_all, send_c, recv_c, send_m, recv_m = refs[len(after):]
        me = exchange(c_ref, c_all, send_c, recv_c)
        c_all[me] = c_ref[...]
        sc = _silu(jnp.concatenate([c_all[p] for p in range(N_DEV)], axis=0))
        sc_ref[...] = sc.astype(sc_ref.dtype)
        mod_ref[me] = _dot(sc, w_ref[...])
        exchange(mod_ref.at[me], mod_ref, send_m, recv_m)

    vmem = pl.BlockSpec(memory_space=pltpu.VMEM)
    return pl.pallas_call(
        body, name=name, in_specs=[vmem, vmem] + [pl.BlockSpec(memory_space=pl.ANY)] * len(after), out_specs=[vmem, vmem],
        out_shape=[jax.ShapeDtypeStruct((N_DEV, c.shape[1]), MXU_DTYPE), jax.ShapeDtypeStruct((N_DEV, N_DEV, n_mod), F32)],
        scratch_shapes=[pltpu.VMEM((N_DEV, 1, c.shape[1]), F32)] + [pltpu.SemaphoreType.DMA((N_DEV - 1,))] * 4,
        compiler_params=pltpu.CompilerParams(has_side_effects=True, vmem_limit_bytes=VMEM_LIMIT))(c, w_ada, *after)


HBM_SPEC = pl.BlockSpec(memory_space=pltpu.HBM)
SEM_SPEC = pl.BlockSpec(memory_space=pltpu.SEMAPHORE)
DATAFLOW = pltpu.SideEffectType.DATAFLOW_SIDE_EFFECTING


def _exchange_copies(srcs, lands, send_sems, recv_sems, gather, first=0):
    x, y, c = _place()
    me = 4 * x + 2 * y + c
    out = []
    for t, (src, land) in enumerate(zip(srcs, lands, strict=True)):
        for k in range(1, N_DEV):
            px, py, pc = x ^ (k >> 2), y ^ ((k >> 1) & 1), c ^ (k & 1)
            sem = 7 * (first + t) + k - 1
            out.append((k, pltpu.make_async_remote_copy(
                src_ref=src if gather else src.at[4 * px + 2 * py + pc],
                dst_ref=land.at[me] if gather else land.at[k - 1],
                send_sem=send_sems.at[sem], recv_sem=recv_sems.at[sem],
                device_id=(px, py, pc), device_id_type=MESH_ID)))
    return out


def _own_copies(srcs, lands, send_sems, gather, first=0):
    x, y, c = _place()
    me = 4 * x + 2 * y + c
    total = send_sems.shape[0] // N_DEV
    return [pltpu.make_async_copy(src if gather else src.at[me], land.at[me] if gather else land.at[N_DEV - 1],
                                  send_sems.at[7 * total + first + t])
            for t, (src, land) in enumerate(zip(srcs, lands, strict=True))]


TREE_DIRECT = (1, 2, 4, 6)
TREE_FORWARDED = (3, 5, 7)


def exchange_start(name, arrs, gather, after=None, tree=False):
    n = len(arrs)
    lands = [lax.empty((N_DEV,) + (a.shape if gather else a.shape[1:]), a.dtype) for a in arrs]
    deps = [] if after is None else [after]

    def body(*refs):
        srcs, land_refs = refs[:n], refs[n:2 * n]
        send_sems, recv_sems = refs[2 * n + len(deps)], refs[2 * n + len(deps) + 1]
        token = refs[-1]
        for k, cp in _exchange_copies(srcs, land_refs, send_sems, recv_sems, gather):
            if not tree or k in TREE_DIRECT:
                cp.start()
        for cp in _own_copies(srcs, land_refs, send_sems, gather):
            cp.start(priority=1)
        token[...] = jnp.zeros_like(token)

    hbm = lambda a: pltpu.HBM(a.shape, a.dtype)
    res = pl.pallas_call(
        body, name=name,
        out_shape=(pltpu.SemaphoreType.DMA((N_DEV * n,)), pltpu.SemaphoreType.DMA((7 * n,)), *[hbm(a) for a in arrs],
                   *[hbm(l) for l in lands], jax.ShapeDtypeStruct((8, 128), F32)),
        in_specs=[HBM_SPEC] * (2 * n) + [pl.BlockSpec(memory_space=pl.ANY)] * len(deps),
        out_specs=(SEM_SPEC, SEM_SPEC, *[HBM_SPEC] * (2 * n), pl.BlockSpec(memory_space=pltpu.VMEM)),
        input_output_aliases={i: 2 + i for i in range(2 * n)},
        compiler_params=pltpu.CompilerParams(has_side_effects=DATAFLOW),
    )(*[pltpu.with_memory_space_constraint(a, pltpu.HBM) for a in arrs + lands], *deps)
    return res[0], res[1], list(res[2:2 + n]), list(res[2 + n:2 + 2 * n]), res[-1]


def exchange_forward(name, started, after, first=0, count=None):
    send_sems, recv_sems, srcs, lands, _ = started
    count = len(srcs) - first if count is None else count
    mine = lands[first:first + count]
    n = len(mine)

    def copies(land_refs, send_ref, recv_ref):
        x, y, c = _place()
        out = []
        for t, land in enumerate(land_refs):
            for k in (2, 4, 6):
                slot = land.at[4 * (x ^ (k >> 2)) + 2 * (y ^ ((k >> 1) & 1)) + c]
                came, goes = 7 * (first + t) + k - 1, 7 * (first + t) + (k ^ 1) - 1
                out.append((
                    pltpu.make_async_remote_copy(src_ref=slot, dst_ref=slot, send_sem=send_ref.at[came], recv_sem=recv_ref.at[came],
                                                 device_id=(x, y, c), device_id_type=MESH_ID),
                    pltpu.make_async_remote_copy(src_ref=slot, dst_ref=slot, send_sem=send_ref.at[goes], recv_sem=recv_ref.at[goes],
                                                 device_id=(x, y, 1 - c), device_id_type=MESH_ID)))
        return out

    after = list(after) if isinstance(after, (list, tuple)) else [after]

    def arrived(*refs):
        for came, _ in copies(refs[:n], refs[n], refs[n + 1]):
            came.wait_recv()

    def pass_on(*refs):
        for _, goes in copies(refs[:n], refs[n], refs[n + 1]):
            goes.start()
        refs[-1][...] = jnp.zeros_like(refs[-1])

    hbm = lambda a: pltpu.HBM(a.shape, a.dtype)
    here = pl.pallas_call(
        arrived, name=name + "_arrived", out_shape=tuple(hbm(a) for a in mine),
        in_specs=[HBM_SPEC] * n + [SEM_SPEC, SEM_SPEC] + [pl.BlockSpec(memory_space=pl.ANY)] * len(after),
        out_specs=tuple([HBM_SPEC] * n), input_output_aliases={i: i for i in range(n)},
        compiler_params=pltpu.CompilerParams(has_side_effects=DATAFLOW),
    )(*mine, send_sems, recv_sems, *after)
    res = pl.pallas_call(
        pass_on, name=name, out_shape=(*[hbm(a) for a in mine], jax.ShapeDtypeStruct((8, 128), F32)),
        in_specs=[HBM_SPEC] * n + [SEM_SPEC, SEM_SPEC],
        out_specs=(*[HBM_SPEC] * n, pl.BlockSpec(memory_space=pltpu.VMEM)), input_output_aliases={i: i for i in range(n)},
        compiler_params=pltpu.CompilerParams(has_side_effects=DATAFLOW),
    )(*here, send_sems, recv_sems)
    lands = lands[:first] + list(res[:n]) + lands[first + count:]
    return (send_sems, recv_sems, srcs, lands, res[-1])


def exchange_wait(name, started, gather, after, first=0, count=None, tree=False):
    send_sems, recv_sems, srcs, lands, _ = started
    count = len(srcs) - first if count is None else count
    srcs, lands = srcs[first:first + count], lands[first:first + count]
    n = len(srcs)

    def body(*refs):
        src_refs, land_refs = refs[:n], refs[n:2 * n]
        copies = _exchange_copies(src_refs, land_refs, refs[2 * n], refs[2 * n + 1], gather, first)
        for _, cp in copies:
            cp.wait_send()
        for k, cp in copies:
            if not tree or k in (1,) + TREE_FORWARDED:
                cp.wait_recv()
        for cp in _own_copies(src_refs, land_refs, refs[2 * n], gather, first):
            cp.wait()

    hbm = lambda a: pltpu.HBM(a.shape, a.dtype)
    res = pl.pallas_call(
        body, name=name, out_shape=tuple(hbm(a) for a in srcs + lands),
        in_specs=[HBM_SPEC] * (2 * n) + [SEM_SPEC, SEM_SPEC, pl.BlockSpec(memory_space=pl.ANY)],
        out_specs=tuple([HBM_SPEC] * (2 * n)), input_output_aliases={i: i for i in range(2 * n)},
        compiler_params=pltpu.CompilerParams(has_side_effects=DATAFLOW),
    )(*srcs, *lands, send_sems, recv_sems, after)
    return list(res[:n]), list(res[n:])


def _gather_cols(stack):
    p, k, n = stack.shape
    return stack.transpose(1, 0, 2).reshape(k, p * n)


def _scatter_cols(full):
    k, n = full.shape
    return full.reshape(k, N_DEV, n // N_DEV).transpose(1, 0, 2)


def _gather_rows(stack):
    p, r, n = stack.shape
    return stack.reshape(p * r, n)


def _scatter_rows(full):
    r, n = full.shape
    return full.reshape(N_DEV, r // N_DEV, n)


_IN_NAT = Q_LORA + KV_LORA
TRANSPOSED = ("w_in", "w_q_b", "w_up")
ROWS_APART = ("w_in", "w_conv")


def to_kernel_layout(name, w):
    if name == "w_in":
        z = lambda n: jnp.zeros((n, w.shape[1]), w.dtype)
        return jnp.concatenate([w[:_IN_NAT], z(KPE_LO), w[_IN_NAT:_IN_NAT + ROPE], z(LANES - KPE_LO - ROPE), w[_IN_NAT + ROPE:]], axis=0)
    if name == "w_q_b":
        return jnp.pad(w.reshape(HEADS, NOPE + ROPE, -1), ((0, 0), (0, LANES - NOPE - ROPE), (0, 0))).reshape(HEADS * LANES, -1)
    if name == "w_o":
        mla = jnp.pad(w[:HEADS * NOPE].reshape(HEADS, NOPE, -1), ((0, 0), (LANES - NOPE, 0), (0, 0))).reshape(HEADS * LANES, -1)
        return jnp.concatenate([mla, w[HEADS * NOPE:]], axis=0)
    return w


def from_kernel_layout(name, g):
    if name == "w_in":
        return jnp.concatenate([g[:_IN_NAT], g[P_KPE + KPE_LO:P_KPE + KPE_LO + ROPE], g[P_QD:]], axis=0)
    if name == "w_q_b":
        return g.reshape(HEADS, LANES, -1)[:, :NOPE + ROPE, :].reshape(HEADS * (NOPE + ROPE), -1)
    if name == "w_o":
        mla = g[:HEADS * LANES].reshape(HEADS, LANES, -1)[:, LANES - NOPE:, :].reshape(HEADS * NOPE, -1)
        return jnp.concatenate([mla, g[HEADS * LANES:]], axis=0)
    return g


SMALL_COLS = 1024
SMALL_ROWS = 24
SMALL_AT = {"loss": (0, 0, 1), "b_ada": (1, 0, 6 * D_MODEL), "g_mix_norm": (7, 0, D_MODEL), "g_q_lat": (8, 0, Q_LORA),
            "g_kv_lat": (9, 0, KV_LORA), "g_mla_q_nope": (10, 0, NOPE), "g_mla_q_pe": (10, 128, ROPE),
            "g_mla_k_nope": (10, 256, NOPE), "g_mla_k_pe": (10, 384, ROPE), "g_dil_q": (10, 512, DIL_DIM),
            "g_dil_k": (10, 640, DIL_DIM), "g_ffn_norm": (11, 0, D_MODEL), "b_conv": (12, 0, 2 * D_FF)}
SMALL_PARAMS = tuple(n for n in SMALL_AT if n != "loss")


def _pack_small(values):
    by_row = {}
    for name, (row, off, n) in SMALL_AT.items():
        by_row.setdefault(row, []).append((off, values[name].reshape(-1).astype(F32)))
    out = []
    for row in sorted(by_row):
        pieces, at = [], 0
        for off, v in sorted(by_row[row], key=lambda t: t[0]):
            pieces += [jnp.zeros((off - at,), F32), v]
            at = off + v.shape[0]
        flat = jnp.concatenate(pieces)
        nrows = -(-flat.shape[0] // SMALL_COLS)
        out.append(jnp.pad(flat, (0, nrows * SMALL_COLS - flat.shape[0])).reshape(nrows, SMALL_COLS))
    packed = jnp.concatenate(out, axis=0)
    return jnp.pad(packed, ((0, SMALL_ROWS - packed.shape[0]), (0, 0)))


def _adam(w, g, m, v):
    c1 = 1.0 - ADAM_B1 ** ADAM_STEP
    c2 = 1.0 - ADAM_B2 ** ADAM_STEP
    m2 = ADAM_B1 * m + (1.0 - ADAM_B1) * g
    v2 = ADAM_B2 * v + (1.0 - ADAM_B2) * (g * g)
    return -ADAM_LR * ((m2 / c1) / (jnp.sqrt(v2 / c2) + ADAM_EPS) + ADAM_WD * w), m2, v2


def adamw_small(name, stack, params):
    flat = [a for n in SMALL_PARAMS for a in params[n]]

    def body(stack_ref, *refs):
        ins, outs = refs[:len(flat)], refs[len(flat):]
        g_all = stack_ref[0]
        for p in range(1, N_DEV):
            g_all = g_all + stack_ref[p]
        outs[0][...] = g_all[0:1, 0:1]
        for i, pname in enumerate(SMALL_PARAMS):
            row, off, n = SMALL_AT[pname]
            w_ref, m_ref, v_ref = ins[3 * i:3 * i + 3]
            go_ref, d_ref, mo_ref, vo_ref = outs[1 + 4 * i:5 + 4 * i]
            for c0 in range(0, n, SMALL_COLS):
                cn = min(SMALL_COLS, n - c0)
                r = row + c0 // SMALL_COLS
                g = g_all[r:r + 1, off:off + cn]
                cols = (slice(None), slice(c0, c0 + cn))
                d, m2, v2 = _adam(w_ref[cols], g, m_ref[cols], v_ref[cols])
                go_ref[cols], d_ref[cols], mo_ref[cols], vo_ref[cols] = g, d, m2, v2

    whole = lambda a: pl.BlockSpec(a.shape, lambda: (0,) * a.ndim)
    out_shape = [jax.ShapeDtypeStruct((1, 1), F32)] + [jax.ShapeDtypeStruct(a.shape, F32) for n in SMALL_PARAMS for a in params[n][:1] * 4]
    res = pl.pallas_call(body, name=name, in_specs=[whole(stack)] + [whole(a) for a in flat],
                         out_specs=[pl.BlockSpec(s.shape, lambda s=s: (0,) * len(s.shape)) for s in out_shape],
                         out_shape=out_shape, compiler_params=_params())(stack, *flat)
    return res[0], {n: res[1 + 4 * i:5 + 4 * i] for i, n in enumerate(SMALL_PARAMS)}


def _local_step(x, pos, mod, target, w, fetch, emit, halfway=lambda after: None):
    S = SEQ
    sh1, sc1, g1, sh2, sc2, g2 = [mod[:, i * D_MODEL:(i + 1) * D_MODEL] for i in range(6)]
    zeros = lambda n: jnp.zeros((1, n), F32)
    g_q = jnp.concatenate([w["g_mla_q_nope"], w["g_mla_q_pe"], zeros(LANES - NOPE - ROPE)], axis=1)
    g_k = jnp.concatenate([w["g_mla_k_nope"], zeros(LANES - NOPE)], axis=1)
    g_kpe = jnp.concatenate([zeros(KPE_LO), w["g_mla_k_pe"], zeros(LANES - KPE_LO - ROPE)], axis=1)
    g_dq = jnp.concatenate([w["g_dil_q"]] * 2, axis=1)
    g_dk = jnp.concatenate([w["g_dil_k"]] * 2, axis=1)
    b_conv = w["b_conv"]

    def inv_freq(d):
        return jnp.power(ROPE_THETA, -2.0 * jnp.arange(d // 2, dtype=F32) / d)

    n_m, n_d = ROPE // 2, DIL_DIM // 2
    freqs = jnp.concatenate([inv_freq(ROPE), inv_freq(DIL_DIM), jnp.zeros((LANES - n_m - n_d,), F32)]).reshape(1, LANES)

    def tables_fn(rows, params):
        (p,), (f,) = rows, params
        c, s = jnp.cos(p * f), jnp.sin(p * f)
        one, zero = jnp.ones_like(c), jnp.zeros_like(c)
        mla = lambda t, fill: jnp.concatenate([fill[:, :KPE_LO], t[:, :n_m], t[:, :n_m], fill[:, :LANES - KPE_LO - ROPE]], axis=1)
        dil = lambda t: jnp.concatenate([t[:, n_m:n_m + n_d]] * 4, axis=1)
        return [mla(c, one), mla(s, zero), dil(c), dil(s)], []

    cos_m, sin_m, cos_d, sin_d = rowwise("rope_tables", tables_fn, [pos], [freqs], [(LANES, F32)] * 4)
    tables = [cos_m, sin_m, cos_d, sin_d]
    H_M, H_D = ROPE // 2, DIL_DIM // 2

    def ln1_fn(rows, params):
        (xv,), (g, sc, sh) = rows, params
        y, _, _ = _rms(xv, g)
        return [y * (1.0 + sc) + sh], []

    (h,) = rowwise("ln1_fwd", ln1_fn, [x], [w["g_mix_norm"], sc1, sh1], [(D_MODEL, MXU_DTYPE)], dep=sin_d)
    w_in = fetch("w_in", h)

    def proj_fn(rows, params):
        (hv, cm, sm, cd, sd), (w_t, gq, gkv, gkp, gdq, gdk) = rows, params
        pv = _dot(hv, w_t, "nt")
        kper = _rope(_grms(pv[:, P_KPE:P_QD], gkp, KPE_GROUPS)[0], cm, sm, H_M)
        qd = [_rope(_grms(c, gdq, DIL_GROUPS)[0], cd, sd, H_D) for c in _chunks(pv[:, P_QD:P_KD])]
        kd = [_rope(_grms(c, gdk, DIL_GROUPS)[0], cd, sd, H_D) for c in _chunks(pv[:, P_KD:P_VD])]
        return [pv, _rms(pv[:, P_QLAT:P_KVLAT], gq)[0], _rms(pv[:, P_KVLAT:P_KPE], gkv)[0], kper,
                jnp.concatenate(qd, axis=1), jnp.concatenate(kd, axis=1)], []

    post_params = [w["g_q_lat"], w["g_kv_lat"], g_kpe, g_dq, g_dk]
    proj, qln, kvn, kper, qd_r, kd_r = rowwise(
        "proj_fwd", proj_fn, [h] + tables, [w_in] + post_params,
        [(P_END, F32), (Q_LORA, MXU_DTYPE), (KV_LORA, MXU_DTYPE), (LANES, MXU_DTYPE)] + [(DIL_WIDTH, F32)] * 2, tm=256)
    w_q_b, w_kv_b = fetch("w_q_b", qln), fetch("w_kv_b", kvn)

    def mla_proj_fn(rows, params):
        (qlv, kvlv, kp, cm, sm), (wq_t, wkv, gq, gk) = rows, params
        qv, kvv = _dot(qlv, wq_t, "nt"), _dot(kvlv, wkv)
        value_lanes = _lane(kp.shape) >= NOPE
        qs, ks, vs = [], [], []
        for qc, kc in zip(_chunks(qv), _chunks(kvv), strict=True):
            qs.append(_rope(_grms(qc, gq, Q_GROUPS)[0], cm, sm, H_M))
            ks.append(_grms(kc, gk, K_GROUPS)[0] + kp)
            vs.append(jnp.where(value_lanes, kc, 0.0))
        return [qv, kvv] + [jnp.concatenate(t, axis=1) for t in (qs, ks, vs)], []

    q, kv, q_mla, k_mla, v_mla = rowwise(
        "mla_proj", mla_proj_fn, [qln, kvn, kper, cos_m, sin_m], [w_q_b, w_kv_b, g_q, g_k],
        [(HEADS * LANES, F32)] * 2 + [(HEADS * LANES, MXU_DTYPE)] * 3, tm=256)
    mla_scale = (NOPE + ROPE) ** -0.5
    o_cat, lse_mla = mla_fwd("mla_fwd", q_mla, k_mla, v_mla, mla_scale)
    passed = halfway(lse_mla)

    band = [band_fwd(f"band{dil}_fwd", qd_r, kd_r, proj, dil, dep=passed) for dil in DILATIONS]
    o_cat, lse_mix = combine_fwd("dil_combine", [b[0] for b in band], [b[1] for b in band], o_cat)
    w_o = fetch("w_o", o_cat)

    def mid_fn(rows, params):
        (ov, xv), (w_out, gate1, g, sc, sh) = rows, params
        mx = _dot(ov, w_out)
        x1 = xv + gate1 * mx
        y, _, _ = _rms(x1, g)
        return [mx, x1, y * (1.0 + sc) + sh], []

    mix, x1, h2 = rowwise("mix_fwd", mid_fn, [o_cat, x], [w_o, g1, w["g_ffn_norm"], sc2, sh2],
                          [(D_MODEL, F32), (D_MODEL, F32), (D_MODEL, MXU_DTYPE)], tm=256)
    w_up, w_conv, w_down = fetch("w_up", h2), fetch("w_conv", h2), fetch("w_down", h2)
    dn, up = ffn_fwd("ffn_fwd", h2, w_up, w_conv, b_conv, w_down)

    def final_fn(rows, params):
        (x1v, dnv, tgt), (gate2,) = rows, params
        r = x1v + gate2 * dnv - tgt
        dy = r * (1.0 / D_MODEL)
        loss = jnp.sum(_colsum(r * r), axis=-1, keepdims=True) * (0.5 / D_MODEL)
        return [dy, gate2 * dy], [loss, _colsum(dy * dnv)]

    dy, d_dn, loss, dg2 = rowwise("loss_head", final_fn, [x1, dn, target], [g2], [(D_MODEL, F32), (D_MODEL, MXU_DTYPE)],
                                  [1, D_MODEL])
    dh2, g_up, g_down, g_w_conv, g_b_conv = ffn_bwd("ffn_bwd", h2, up, w_up, w_conv, b_conv, d_dn, w_down)
    emit("w_down", g_down)
    emit("w_conv", g_w_conv)
    sent = emit("w_up", g_up)

    def mid_bwd_fn(rows, params):
        (dh2v, dyv, x1v, mx), (gate1, g, sc, w_out) = rows, params
        yn, n, rstd = _rms(x1v, g)
        dx_n, dg = _rms_bwd(dh2v * (1.0 + sc), n, rstd, g)
        dx1 = dyv + dx_n
        dm = gate1 * dx1
        return [dx1, dm, _dot(dm, w_out, "nt")], [dg, _colsum(dh2v * yn), _colsum(dh2v), _colsum(dx1 * mx)]

    dx1, dmix, do_cat, dg_ffn, dsc2, dsh2, dg1 = rowwise(
        "mid_bwd", mid_bwd_fn, [dh2, dy, x1, mix], [g1, w["g_ffn_norm"], sc2, w_o],
        [(D_MODEL, F32), (D_MODEL, MXU_DTYPE), (w_o.shape[0], F32)], [D_MODEL] * 4, tm=256, dep=sent)

    sent = emit("w_o", matmul("mix_wgrad", o_cat, dmix, "tn", tm=512, out_dtype=MXU_DTYPE))
    dband = None
    for dil, b in zip(DILATIONS, band):
        dband = band_bwd(f"band{dil}_bwd", qd_r, kd_r, proj, b[1], lse_mix, o_cat, do_cat, dil, before=dband)
    dq_mla, dkv_mla, dkper = mla_bwd("mla_bwd", q_mla, k_mla, v_mla, o_cat, do_cat, lse_mla, mla_scale)

    def mla_prep_bwd_fn(rows, params):
        (dqv, dkvv, qv, kvv, cm, sm), (gq, gk) = rows, params
        nope_lanes = _lane(cm.shape) < NOPE
        dqs, dkvs, dgq, dgk = [], [], 0.0, 0.0
        for dqc, dkc, qc, kc in zip(_chunks(dqv), _chunks(dkvv), _chunks(qv), _chunks(kvv), strict=True):
            _, n, rstd = _grms(qc, gq, Q_GROUPS)
            dx, dg = _grms_bwd(_rope_bwd(dqc, cm, sm, H_M), n, rstd, gq, Q_GROUPS)
            dqs.append(dx)
            dgq = dgq + dg
            _, n, rstd = _grms(kc, gk, K_GROUPS)
            dx, dg = _grms_bwd(dkc, n, rstd, gk, K_GROUPS)
            dkvs.append(jnp.where(nope_lanes, dx, dkc))
            dgk = dgk + dg
        return [jnp.concatenate(dqs, axis=1), jnp.concatenate(dkvs, axis=1)], [dgq, dgk]

    dq, dkv, dg_q, dg_k = rowwise("mla_prep_bwd", mla_prep_bwd_fn, [dq_mla, dkv_mla, q, kv, cos_m, sin_m], [g_q, g_k],
                                  [(HEADS * LANES, MXU_DTYPE)] * 2, [LANES, LANES], tm=256, dep=sent)
    emit("w_q_b", matmul("q_wgrad", dq, qln, "tn", out_dtype=MXU_DTYPE))
    sent = emit("w_kv_b", matmul("kv_wgrad", kvn, dkv, "tn", out_dtype=MXU_DTYPE))

    def pre_bwd_fn(rows, params):
        dqv, dkvv, dkp, dqd_, dkd_, dvd_, pv, cm, sm, cd, sd = rows
        wq_t, wkv, gq, gkv, gkp, gdq, gdk = params
        dql, dkvl = _dot(dqv, wq_t), _dot(dkvv, wkv, "nt")
        r_q = _norm_bwd(dql, pv[:, P_QLAT:P_KVLAT], gq)
        r_kv = _norm_bwd(dkvl, pv[:, P_KVLAT:P_KPE], gkv)
        _, n, rstd = _grms(pv[:, P_KPE:P_QD], gkp, KPE_GROUPS)
        r_kp = _grms_bwd(_rope_bwd(dkp, cm, sm, H_M), n, rstd, gkp, KPE_GROUPS)
        outs, dgs = [r_q[0], r_kv[0], r_kp[0]], []
        for dval, lo, g in ((dqd_, P_QD, gdq), (dkd_, P_KD, gdk)):
            dg_sum = 0.0
            for dc, xc in zip(_chunks(dval), _chunks(pv[:, lo:lo + DIL_WIDTH]), strict=True):
                _, n, rstd = _grms(xc, g, DIL_GROUPS)
                dx, dg = _grms_bwd(_rope_bwd(dc, cd, sd, H_D), n, rstd, g, DIL_GROUPS)
                outs.append(dx)
                dg_sum = dg_sum + dg
            dgs.append(dg_sum)
        return [jnp.concatenate(outs + [dvd_], axis=1)], [r_q[1], r_kv[1], r_kp[1]] + dgs

    dproj, dg_q_lat, dg_kv_lat, dg_kpe, dg_dq, dg_dk = rowwise(
        "proj_pre_bwd", pre_bwd_fn,
        [dq, dkv, dkper] + list(dband) + [proj] + tables, [w_q_b, w_kv_b] + post_params,
        [(P_END, MXU_DTYPE)], [Q_LORA, KV_LORA, LANES, LANES, LANES], tm=256, dep=sent)
    sent = emit("w_in", matmul("proj_wgrad", dproj, h, "tn", tn=512, out_dtype=MXU_DTYPE))

    def ln1_bwd_fn(rows, params):
        (dpv, dres, xv), (w_t, g, sc) = rows, params
        dhv = _dot(dpv, w_t)
        yn, n, rstd = _rms(xv, g)
        dx_n, dg = _rms_bwd(dhv * (1.0 + sc), n, rstd, g)
        return [dres + dx_n], [dg, _colsum(dhv * yn), _colsum(dhv)]

    grad_x, dg_mix, dsc1, dsh1 = rowwise("proj_dgrad", ln1_bwd_fn, [dproj, dx1, x], [w_in, w["g_mix_norm"], sc1],
                                         [(D_MODEL, F32)], [D_MODEL] * 3, tm=256, dep=sent)
    dmod = jnp.concatenate([dsh1, dsc1, dg1, dsh2, dsc2, dg2], axis=-1)
    small = {"loss": loss, "b_ada": dmod, "g_mix_norm": dg_mix, "g_q_lat": dg_q_lat, "g_kv_lat": dg_kv_lat,
             "g_mla_q_nope": dg_q[:, :NOPE], "g_mla_q_pe": dg_q[:, NOPE:NOPE + ROPE], "g_mla_k_nope": dg_k[:, :NOPE],
             "g_mla_k_pe": dg_kpe[:, KPE_LO:KPE_LO + ROPE], "g_dil_q": dg_dq[:, :DIL_DIM] + dg_dq[:, DIL_DIM:],
             "g_dil_k": dg_dk[:, :DIL_DIM] + dg_dk[:, DIL_DIM:], "g_ffn_norm": dg_ffn,
             "b_conv": g_b_conv}
    return grad_x, small


COL_SHARDED = ("w_kv_b", "w_conv")
ROW_SHARDED = ("w_o", "w_down") + TRANSPOSED
ADAM_TILE = {"w_ada": 256, "w_up": 176, "w_down": 176}
GATHER_GROUPS = (("w_in",), ("w_q_b", "w_kv_b"), ("w_o",), ("w_up", "w_conv", "w_down"))
START_STAGES = ((0, 1), (2, 3))
FORWARD_STAGES = ((0, 1), (2,), (3,))
FORWARD_WITH = {"w_o": 2}
SCATTER_GROUPS = (("w_down", "w_conv", "w_up"), ("w_o",), ("w_q_b", "w_kv_b"), ("w_in",))
OUT_WEIGHTS = ("w_ada", "b_ada", "g_mix_norm", "w_in", "g_q_lat", "w_q_b", "g_kv_lat", "w_kv_b", "g_mla_q_nope", "g_mla_q_pe",
               "g_mla_k_nope", "g_mla_k_pe", "g_dil_q", "g_dil_k", "w_o", "g_ffn_norm", "w_up", "w_conv", "b_conv", "w_down")


def kernel(x, c, positions, w_ada, b_ada, g_mix_norm, w_in, g_q_lat, w_q_b, g_kv_lat, w_kv_b, g_mla_q_nope, g_mla_q_pe, g_mla_k_nope, g_mla_k_pe, g_dil_q, g_dil_k, w_o, g_ffn_norm, w_up, w_conv, b_conv, w_down, loss_target, m_w_ada, m_b_ada, m_g_mix_norm, m_w_in, m_g_q_lat, m_w_q_b, m_g_kv_lat, m_w_kv_b, m_g_mla_q_nope, m_g_mla_q_pe, m_g_mla_k_nope, m_g_mla_k_pe, m_g_dil_q, m_g_dil_k, m_w_o, m_g_ffn_norm, m_w_up, m_w_conv, m_b_conv, m_w_down, v_w_ada, v_b_ada, v_g_mix_norm, v_w_in, v_g_q_lat, v_w_q_b, v_g_kv_lat, v_w_kv_b, v_g_mla_q_nope, v_g_mla_q_pe, v_g_mla_k_nope, v_g_mla_k_pe, v_g_dil_q, v_g_dil_k, v_w_o, v_g_ffn_norm, v_w_up, v_w_conv, v_b_conv, v_w_down):
    args = dict(locals())
    xi, yi, ci = _place()
    me = 4 * xi + 2 * yi + ci
    def local(prefix, n):
        a = args[prefix + n]
        if n in ROWS_APART:
            return jnp.transpose(a, (2, 0, 1) if n in TRANSPOSED else (1, 0, 2))
        return a[0].T if n in TRANSPOSED else a[0]

    def as_output(n, r):
        if n in ROWS_APART:
            return jnp.transpose(r, (1, 2, 0) if n in TRANSPOSED else (1, 0, 2))
        return (r.T if n in TRANSPOSED else r)[None]

    shard = {n: local("", n) for n in COL_SHARDED + ROW_SHARDED + ("w_ada",)}
    flat = lambda n, a: a.reshape(a.shape[0], a.shape[-1]) if n in ROWS_APART else a
    small_w = {n: args[n] for n in SMALL_PARAMS}

    payload = {n: flat(n, shard[n]) if n == "w_conv" else flat(n, shard[n]).astype(MXU_DTYPE) for n in COL_SHARDED + ROW_SHARDED}
    start_order = [[n for i in groups for n in GATHER_GROUPS[i]] for groups in START_STAGES]

    sc_all, mod_all = ada_modulation("ada_mod", c, shard["w_ada"], after=[payload[n] for n in start_order[0]])

    exchange_of = lambda i: [e for e, groups in enumerate(START_STAGES) if i in groups][0]
    start_stage = lambda e, after: exchange_start(f"gather_start{e}", [payload[n] for n in start_order[e]], gather=True,
                                                  after=after, tree=True)
    gathered = {0: start_stage(0, mod_all)}
    after_start = gathered[0][-1]
    full, forwarded = {}, set()

    def forward(stage, after):
        e = exchange_of(FORWARD_STAGES[stage][0])
        if stage not in forwarded:
            forwarded.add(stage)
            first = start_order[e].index(GATHER_GROUPS[FORWARD_STAGES[stage][0]][0])
            count = sum(len(GATHER_GROUPS[i]) for i in FORWARD_STAGES[stage])
            starts_next = e + 1 < len(START_STAGES) and e + 1 not in gathered
            ready = [payload[n] for n in start_order[e + 1]] if starts_next else []
            gathered[e] = exchange_forward(f"gather_forward{stage}", gathered[e], [after] + ready, first, count)
            if starts_next:
                gathered[e + 1] = start_stage(e + 1, gathered[e][-1])
        return gathered[e][-1]

    def fetch(name, after):
        if name not in full:
            (i, grp), = [(i, grp) for i, grp in enumerate(GATHER_GROUPS) if name in grp]
            (stage,) = [s for s, groups in enumerate(FORWARD_STAGES) if i in groups]
            forward(stage, after)
            if name in FORWARD_WITH:
                forward(FORWARD_WITH[name], after)
            e = exchange_of(i)
            behind = gathered[e + 1][-1] if e + 1 in gathered else after
            srcs, lands = exchange_wait(f"gather{i}_wait", gathered[e], True, behind, start_order[e].index(grp[0]), len(grp), tree=True)
            for n, stack in zip(grp, lands, strict=True):
                full[n] = to_kernel_layout(n, _gather_cols(stack) if n in COL_SHARDED else _gather_rows(stack))
        return full[name]

    mod_row = lax.dynamic_index_in_dim(mod_all, me, axis=1, keepdims=False).reshape(1, 6 * D_MODEL)
    (mod,) = rowwise("ada_bias", lambda rows, params: ([rows[0] + rows[1]], []), [mod_row, b_ada], [], [(6 * D_MODEL, F32)],
                     dep=after_start)

    pending, scatters = {}, {}

    def emit(name, grad):
        grad = from_kernel_layout(name, grad)
        pending[name] = _scatter_cols(grad) if name in COL_SHARDED else _scatter_rows(grad)
        for i, grp in enumerate(SCATTER_GROUPS):
            if name == grp[-1]:
                scatters[i] = exchange_start(f"scatter{i}_start", [pending[n] for n in grp], gather=False)
                return scatters[i][-1]
        return None

    pos = positions.reshape(SEQ, 1).astype(F32)
    grad_x, small = _local_step(x[0], pos, mod, loss_target[0], small_w, fetch, emit, halfway=lambda after: forward(1, after))

    small_sent = exchange_start("small_start", [_pack_small(small)], gather=True, after=grad_x)

    res, done = {}, small_sent[-1]
    for i, grp in enumerate(SCATTER_GROUPS):
        _, lands = exchange_wait(f"scatter{i}_wait", scatters[i], False, done)
        for n, land in zip(grp, lands, strict=True):
            res[n] = adamw(f"adamw_{n}", shard[n], [land], local("m_", n), local("v_", n), ADAM_TILE.get(n))
            done = res[n][0]
            res[n] = [as_output(n, r) for r in res[n]]
    _, (small_all,) = exchange_wait("small_wait", small_sent, True, done)
    loss, small_res = adamw_small("adamw_small", small_all, {n: (args[n], args["m_" + n], args["v_" + n]) for n in SMALL_PARAMS})
    row, _, n_mod = SMALL_AT["b_ada"]
    dmod_all = small_all[:, row:row + n_mod // SMALL_COLS, :].reshape(N_DEV, n_mod)
    dmod_mine = lax.dynamic_slice_in_dim(dmod_all, me * (6 * D_MODEL // N_DEV), 6 * D_MODEL // N_DEV, axis=1)
    g_w_ada = matmul("ada_wgrad", sc_all, dmod_mine, "tn")
    res["w_ada"] = [r[None] for r in adamw("adamw_w_ada", shard["w_ada"], [g_w_ada], m_w_ada[0], v_w_ada[0], ADAM_TILE["w_ada"])]

    def leaf(kind, n):
        return res[n][kind] if n in res else small_res[n][kind]

    return (loss.reshape(()), grad_x[None], *[leaf(k, n) for k in range(4) for n in OUT_WEIGHTS])
```
